```python
import math
import jax, jax.numpy as jnp
from jax import lax
import numpy as np

D_MODEL = 1024
BATCH = 8
SEQ = 2048
DEPTH = 1

EPS = 1e-6
D_FF = 2816
FFN_RES_WEIGHT = 0.5
N_MOD = 9

SWA_HEADS = 8
SWA_KV_HEADS = 2
SWA_HEAD_DIM = 64
SWA_GROUP = SWA_HEADS // SWA_KV_HEADS
WINDOW = 128

MLA_HEADS = 4
MLA_Q_RANK = 256
MLA_KV_RANK = 128
MLA_NOPE = 128
MLA_ROPE = 64
MLA_V = 128
ROPE_THETA = 10000.0
Q_BLOCK = 128

NUM_BUCKETS = 32
MAX_DISTANCE = 128

IN_SPLITS = (SWA_HEADS * SWA_HEAD_DIM, SWA_KV_HEADS * SWA_HEAD_DIM, SWA_KV_HEADS * SWA_HEAD_DIM,
             MLA_Q_RANK, MLA_KV_RANK, MLA_ROPE)
D_IN = sum(IN_SPLITS)
MIX_OUT = SWA_HEADS * SWA_HEAD_DIM + MLA_HEADS * MLA_V

kernel_name = "hybrid_swa_sink_mla_macaron_adaln"


def rms_norm(x, g):
    xf = x.astype(jnp.float32)
    y = xf * lax.rsqrt(jnp.mean(xf * xf, axis=-1, keepdims=True) + EPS)
    return (y * g.astype(jnp.float32)).astype(x.dtype)


def modulate(h, shift, scale):
    return h * (1 + scale[:, None, :]) + shift[:, None, :]


def swiglu(h, w_gate, w_up, w_down):
    return (jax.nn.silu(h @ w_gate) * (h @ w_up)) @ w_down


def t5_bucket(dist):
    max_exact = NUM_BUCKETS // 2
    n = jnp.maximum(dist, 0)
    nf = jnp.maximum(n, 1).astype(jnp.float32)
    large = max_exact + (jnp.log(nf / max_exact) / math.log(MAX_DISTANCE / max_exact)
                         * (NUM_BUCKETS - max_exact)).astype(jnp.int32)
    large = jnp.minimum(large, NUM_BUCKETS - 1)
    return jnp.where(n < max_exact, n, large)


def rope_tables(seq, dim):
    inv = ROPE_THETA ** (-jnp.arange(0, dim, 2, dtype=jnp.float32) / dim)
    ang = jnp.arange(seq, dtype=jnp.float32)[:, None] * inv[None, :]
    return jnp.cos(ang), jnp.sin(ang)


def apply_rope(x, cos, sin):
    c = cos[None, :, None, :].astype(x.dtype)
    s = sin[None, :, None, :].astype(x.dtype)
    x1, x2 = jnp.split(x, 2, axis=-1)
    return jnp.concatenate([x1 * c - x2 * s, x2 * c + x1 * s], axis=-1)


def sliding_window_gqa(q, k, v, sinks, rel_bias):
    B, S = q.shape[0], q.shape[1]
    nb = S // WINDOW
    qb = q.reshape(B, nb, WINDOW, SWA_KV_HEADS, SWA_GROUP, SWA_HEAD_DIM)
    kb = k.reshape(B, nb, WINDOW, SWA_KV_HEADS, SWA_HEAD_DIM)
    vb = v.reshape(B, nb, WINDOW, SWA_KV_HEADS, SWA_HEAD_DIM)
    pad_k = jnp.zeros_like(kb[:, :1])
    kk = jnp.concatenate([jnp.concatenate([pad_k, kb[:, :-1]], axis=1), kb], axis=2)
    vv = jnp.concatenate([jnp.concatenate([jnp.zeros_like(vb[:, :1]), vb[:, :-1]], axis=1), vb], axis=2)

    s = jnp.einsum('bnqhgd,bnkhd->bnhgqk', qb, kk).astype(jnp.float32) * (SWA_HEAD_DIM ** -0.5)

    qi = jnp.arange(WINDOW)[:, None]
    kj = jnp.arange(2 * WINDOW)[None, :]
    dist = qi + WINDOW - kj
    band = (dist >= 0) & (dist < WINDOW)
    blk = jnp.arange(nb)[:, None]
    key_ok = (blk > 0) | (jnp.arange(2 * WINDOW)[None, :] >= WINDOW)
    valid = band[None, :, :] & key_ok[:, None, :]

    bias = rel_bias.astype(jnp.float32)[t5_bucket(dist)]
    bias = bias.transpose(2, 0, 1).reshape(SWA_KV_HEADS, SWA_GROUP, WINDOW, 2 * WINDOW)
    s = jnp.where(valid[None, :, None, None], s + bias[None, None], -jnp.inf)

    sink = jnp.broadcast_to(sinks.astype(jnp.float32).reshape(1, 1, SWA_KV_HEADS, SWA_GROUP, 1, 1),
                            s.shape[:-1] + (1,))
    p = jax.nn.softmax(jnp.concatenate([s, sink], axis=-1), axis=-1)[..., :-1]
    o = jnp.einsum('bnhgqk,bnkhd->bnqhgd', p.astype(vv.dtype), vv)
    return o.reshape(B, S, SWA_HEADS * SWA_HEAD_DIM)


def mla_attention(q_lat, kv_lat, k_rope, q_norm, kv_norm, w_uq, w_ukv):
    B, S = q_lat.shape[0], q_lat.shape[1]
    cos, sin = rope_tables(S, MLA_ROPE)
    q = (rms_norm(q_lat, q_norm) @ w_uq).reshape(B, S, MLA_HEADS, MLA_NOPE + MLA_ROPE)
    q_nope, q_rope = q[..., :MLA_NOPE], apply_rope(q[..., MLA_NOPE:], cos, sin)
    kv = (rms_norm(kv_lat, kv_norm) @ w_ukv).reshape(B, S, MLA_HEADS, MLA_NOPE + MLA_V)
    k_nope, v = kv[..., :MLA_NOPE], kv[..., MLA_NOPE:]
    k_r = apply_rope(k_rope[:, :, None, :], cos, sin)[:, :, 0, :]
    scale = (MLA_NOPE + MLA_ROPE) ** -0.5

    nb = S // Q_BLOCK
    qn = q_nope.reshape(B, nb, Q_BLOCK, MLA_HEADS, MLA_NOPE).transpose(1, 0, 2, 3, 4)
    qr = q_rope.reshape(B, nb, Q_BLOCK, MLA_HEADS, MLA_ROPE).transpose(1, 0, 2, 3, 4)
    kpos = jnp.arange(S)

    def one_block(args):
        qn_b, qr_b, i = args
        s = (jnp.einsum('bqhd,bkhd->bhqk', qn_b, k_nope)
             + jnp.einsum('bqhd,bkd->bhqk', qr_b, k_r)).astype(jnp.float32) * scale
        qpos = i * Q_BLOCK + jnp.arange(Q_BLOCK)
        s = jnp.where(kpos[None, :] <= qpos[:, None], s, -jnp.inf)
        p = jax.nn.softmax(s, axis=-1).astype(v.dtype)
        return jnp.einsum('bhqk,bkhd->bqhd', p, v)

    o = lax.map(one_block, (qn, qr, jnp.arange(nb)))
    return o.transpose(1, 0, 2, 3, 4).reshape(B, S, MLA_HEADS * MLA_V)


def token_mixing(h, w_in, q_norm, kv_norm, w_uq, w_ukv, sinks, w_o, rel_bias):
    B, S = h.shape[0], h.shape[1]
    proj = h @ w_in
    idx = np.cumsum(IN_SPLITS)[:-1].tolist()
    q_a, k_a, v_a, q_lat, kv_lat, k_rope = jnp.split(proj, idx, axis=-1)
    out_a = sliding_window_gqa(q_a.reshape(B, S, SWA_HEADS, SWA_HEAD_DIM),
                               k_a.reshape(B, S, SWA_KV_HEADS, SWA_HEAD_DIM),
                               v_a.reshape(B, S, SWA_KV_HEADS, SWA_HEAD_DIM),
                               sinks, rel_bias)
    out_b = mla_attention(q_lat, kv_lat, k_rope, q_norm, kv_norm, w_uq, w_ukv)
    return jnp.concatenate([out_a, out_b], axis=-1) @ w_o


def _fwd_setup_inputs(seed: int = 0) -> dict:
    key = jax.random.key(seed)
    ks = jax.random.split(key, 24)
    L, D = DEPTH, D_MODEL

    def w(k, shape, fan_in):
        return jax.random.normal(k, shape, jnp.float32) * fan_in ** -0.5

    def gain(k, shape):
        return 1.0 + 0.05 * jax.random.normal(k, shape, jnp.float32)

    return {
        "x": jax.random.normal(ks[0], (BATCH, SEQ, D), jnp.float32),
        "c": jax.random.normal(ks[1], (BATCH, D), jnp.float32),
        "w_mod": w(ks[2], (L, D, N_MOD * D), D) * 0.5,
        "b_mod": 0.02 * jax.random.normal(ks[3], (L, N_MOD * D), jnp.float32),
        "norm_ffn1": gain(ks[4], (L, D)),
        "ffn1_gate": w(ks[5], (L, D, D_FF), D),
        "ffn1_up": w(ks[6], (L, D, D_FF), D),
        "ffn1_down": w(ks[7], (L, D_FF, D), D_FF),
        "norm_mix": gain(ks[8], (L, D)),
        "w_in": w(ks[9], (L, D, D_IN), D),
        "q_norm": gain(ks[10], (L, MLA_Q_RANK)),
        "kv_norm": gain(ks[11], (L, MLA_KV_RANK)),
        "w_uq": w(ks[12], (L, MLA_Q_RANK, MLA_HEADS * (MLA_NOPE + MLA_ROPE)), MLA_Q_RANK),
        "w_ukv": w(ks[13], (L, MLA_KV_RANK, MLA_HEADS * (MLA_NOPE + MLA_V)), MLA_KV_RANK),
        "sinks": 0.5 * jax.random.normal(ks[14], (L, SWA_HEADS), jnp.float32),
        "w_o": w(ks[15], (L, MIX_OUT, D), MIX_OUT),
        "norm_ffn2": gain(ks[16], (L, D)),
        "ffn2_gate": w(ks[17], (L, D, D_FF), D),
        "ffn2_up": w(ks[18], (L, D, D_FF), D),
        "ffn2_down": w(ks[19], (L, D_FF, D), D_FF),
        "rel_bias": 0.5 * jax.random.normal(ks[20], (NUM_BUCKETS, SWA_HEADS), jnp.float32),
        "norm_final": gain(ks[21], (D,)),
    }


def _fwd_reference(x, c, w_mod, b_mod, norm_ffn1, ffn1_gate, ffn1_up, ffn1_down, norm_mix, w_in,
              q_norm, kv_norm, w_uq, w_ukv, sinks, w_o, norm_ffn2, ffn2_gate, ffn2_up, ffn2_down,
              rel_bias, norm_final):
    c_act = jax.nn.silu(c)
    for l in range(DEPTH):
        mod = c_act @ w_mod[l] + b_mod[l]
        sh1, sc1, g1, sh2, sc2, g2, sh3, sc3, g3 = jnp.split(mod, N_MOD, axis=-1)

        h = modulate(rms_norm(x, norm_ffn1[l]), sh1, sc1)
        x = x + FFN_RES_WEIGHT * g1[:, None, :] * swiglu(h, ffn1_gate[l], ffn1_up[l], ffn1_down[l])

        h = modulate(rms_norm(x, norm_mix[l]), sh2, sc2)
        x = x + g2[:, None, :] * token_mixing(h, w_in[l], q_norm[l], kv_norm[l], w_uq[l], w_ukv[l],
                                              sinks[l], w_o[l], rel_bias)

        h = modulate(rms_norm(x, norm_ffn2[l]), sh3, sc3)
        x = x + FFN_RES_WEIGHT * g3[:, None, :] * swiglu(h, ffn2_gate[l], ffn2_up[l], ffn2_down[l])
    return rms_norm(x, norm_final)


import jax as _jax
import jax.numpy as _jnp

TWIN_FORMAT = 'train_step'
FWD_PARAMS = ['x', 'c', 'w_mod', 'b_mod', 'norm_ffn1', 'ffn1_gate', 'ffn1_up', 'ffn1_down', 'norm_mix', 'w_in', 'q_norm', 'kv_norm', 'w_uq', 'w_ukv', 'sinks', 'w_o', 'norm_ffn2', 'ffn2_gate', 'ffn2_up', 'ffn2_down', 'rel_bias', 'norm_final']
TWIN_WEIGHTS = ['w_mod', 'b_mod', 'norm_ffn1', 'ffn1_gate', 'ffn1_up', 'ffn1_down', 'norm_mix', 'w_in', 'q_norm', 'kv_norm', 'w_uq', 'w_ukv', 'sinks', 'w_o', 'norm_ffn2', 'ffn2_gate', 'ffn2_up', 'ffn2_down', 'rel_bias', 'norm_final']
TWIN_DIFF_INPUT = 'x'
TWIN_INPUTS = ['x', 'c', 'w_mod', 'b_mod', 'norm_ffn1', 'ffn1_gate', 'ffn1_up', 'ffn1_down', 'norm_mix', 'w_in', 'q_norm', 'kv_norm', 'w_uq', 'w_ukv', 'sinks', 'w_o', 'norm_ffn2', 'ffn2_gate', 'ffn2_up', 'ffn2_down', 'rel_bias', 'norm_final', 'loss_target', 'm_w_mod', 'm_b_mod', 'm_norm_ffn1', 'm_ffn1_gate', 'm_ffn1_up', 'm_ffn1_down', 'm_norm_mix', 'm_w_in', 'm_q_norm', 'm_kv_norm', 'm_w_uq', 'm_w_ukv', 'm_sinks', 'm_w_o', 'm_norm_ffn2', 'm_ffn2_gate', 'm_ffn2_up', 'm_ffn2_down', 'm_rel_bias', 'm_norm_final', 'v_w_mod', 'v_b_mod', 'v_norm_ffn1', 'v_ffn1_gate', 'v_ffn1_up', 'v_ffn1_down', 'v_norm_mix', 'v_w_in', 'v_q_norm', 'v_kv_norm', 'v_w_uq', 'v_w_ukv', 'v_sinks', 'v_w_o', 'v_norm_ffn2', 'v_ffn2_gate', 'v_ffn2_up', 'v_ffn2_down', 'v_rel_bias', 'v_norm_final']
TWIN_OUTPUTS = ['loss', 'grad_x', 'grad_w_mod', 'grad_b_mod', 'grad_norm_ffn1', 'grad_ffn1_gate', 'grad_ffn1_up', 'grad_ffn1_down', 'grad_norm_mix', 'grad_w_in', 'grad_q_norm', 'grad_kv_norm', 'grad_w_uq', 'grad_w_ukv', 'grad_sinks', 'grad_w_o', 'grad_norm_ffn2', 'grad_ffn2_gate', 'grad_ffn2_up', 'grad_ffn2_down', 'grad_rel_bias', 'grad_norm_final', 'delta_w_mod', 'delta_b_mod', 'delta_norm_ffn1', 'delta_ffn1_gate', 'delta_ffn1_up', 'delta_ffn1_down', 'delta_norm_mix', 'delta_w_in', 'delta_q_norm', 'delta_kv_norm', 'delta_w_uq', 'delta_w_ukv', 'delta_sinks', 'delta_w_o', 'delta_norm_ffn2', 'delta_ffn2_gate', 'delta_ffn2_up', 'delta_ffn2_down', 'delta_rel_bias', 'delta_norm_final', 'new_m_w_mod', 'new_m_b_mod', 'new_m_norm_ffn1', 'new_m_ffn1_gate', 'new_m_ffn1_up', 'new_m_ffn1_down', 'new_m_norm_mix', 'new_m_w_in', 'new_m_q_norm', 'new_m_kv_norm', 'new_m_w_uq', 'new_m_w_ukv', 'new_m_sinks', 'new_m_w_o', 'new_m_norm_ffn2', 'new_m_ffn2_gate', 'new_m_ffn2_up', 'new_m_ffn2_down', 'new_m_rel_bias', 'new_m_norm_final', 'new_v_w_mod', 'new_v_b_mod', 'new_v_norm_ffn1', 'new_v_ffn1_gate', 'new_v_ffn1_up', 'new_v_ffn1_down', 'new_v_norm_mix', 'new_v_w_in', 'new_v_q_norm', 'new_v_kv_norm', 'new_v_w_uq', 'new_v_w_ukv', 'new_v_sinks', 'new_v_w_o', 'new_v_norm_ffn2', 'new_v_ffn2_gate', 'new_v_ffn2_up', 'new_v_ffn2_down', 'new_v_rel_bias', 'new_v_norm_final']
TWIN_LEAF_KINDS = {'loss': 'loss', 'grad_x': 'grad_x', 'grad_w_mod': 'grad_w', 'grad_b_mod': 'grad_w', 'grad_norm_ffn1': 'grad_w', 'grad_ffn1_gate': 'grad_w', 'grad_ffn1_up': 'grad_w', 'grad_ffn1_down': 'grad_w', 'grad_norm_mix': 'grad_w', 'grad_w_in': 'grad_w', 'grad_q_norm': 'grad_w', 'grad_kv_norm': 'grad_w', 'grad_w_uq': 'grad_w', 'grad_w_ukv': 'grad_w', 'grad_sinks': 'grad_w', 'grad_w_o': 'grad_w', 'grad_norm_ffn2': 'grad_w', 'grad_ffn2_gate': 'grad_w', 'grad_ffn2_up': 'grad_w', 'grad_ffn2_down': 'grad_w', 'grad_rel_bias': 'grad_w', 'grad_norm_final': 'grad_w', 'delta_w_mod': 'delta_w', 'delta_b_mod': 'delta_w', 'delta_norm_ffn1': 'delta_w', 'delta_ffn1_gate': 'delta_w', 'delta_ffn1_up': 'delta_w', 'delta_ffn1_down': 'delta_w', 'delta_norm_mix': 'delta_w', 'delta_w_in': 'delta_w', 'delta_q_norm': 'delta_w', 'delta_kv_norm': 'delta_w', 'delta_w_uq': 'delta_w', 'delta_w_ukv': 'delta_w', 'delta_sinks': 'delta_w', 'delta_w_o': 'delta_w', 'delta_norm_ffn2': 'delta_w', 'delta_ffn2_gate': 'delta_w', 'delta_ffn2_up': 'delta_w', 'delta_ffn2_down': 'delta_w', 'delta_rel_bias': 'delta_w', 'delta_norm_final': 'delta_w', 'new_m_w_mod': 'new_m', 'new_m_b_mod': 'new_m', 'new_m_norm_ffn1': 'new_m', 'new_m_ffn1_gate': 'new_m', 'new_m_ffn1_up': 'new_m', 'new_m_ffn1_down': 'new_m', 'new_m_norm_mix': 'new_m', 'new_m_w_in': 'new_m', 'new_m_q_norm': 'new_m', 'new_m_kv_norm': 'new_m', 'new_m_w_uq': 'new_m', 'new_m_w_ukv': 'new_m', 'new_m_sinks': 'new_m', 'new_m_w_o': 'new_m', 'new_m_norm_ffn2': 'new_m', 'new_m_ffn2_gate': 'new_m', 'new_m_ffn2_up': 'new_m', 'new_m_ffn2_down': 'new_m', 'new_m_rel_bias': 'new_m', 'new_m_norm_final': 'new_m', 'new_v_w_mod': 'new_v', 'new_v_b_mod': 'new_v', 'new_v_norm_ffn1': 'new_v', 'new_v_ffn1_gate': 'new_v', 'new_v_ffn1_up': 'new_v', 'new_v_ffn1_down': 'new_v', 'new_v_norm_mix': 'new_v', 'new_v_w_in': 'new_v', 'new_v_q_norm': 'new_v', 'new_v_kv_norm': 'new_v', 'new_v_w_uq': 'new_v', 'new_v_w_ukv': 'new_v', 'new_v_sinks': 'new_v', 'new_v_w_o': 'new_v', 'new_v_norm_ffn2': 'new_v', 'new_v_ffn2_gate': 'new_v', 'new_v_ffn2_up': 'new_v', 'new_v_ffn2_down': 'new_v', 'new_v_rel_bias': 'new_v', 'new_v_norm_final': 'new_v'}


def _forward(args):
    return _fwd_reference(*[args[k] for k in FWD_PARAMS])


def _output_shape():
    out = _jax.eval_shape(lambda: _forward(_fwd_setup_inputs(0)))
    return out.shape, out.dtype

N_MICROBATCH = 1
ADAM_LR = 0.001
ADAM_B1 = 0.9
ADAM_B2 = 0.999
ADAM_EPS = 1e-08
ADAM_WD = 0.01
ADAM_STEP = 10
PER_EXAMPLE_BATCH_AXIS = {'x': 0, 'c': 0, 'loss_target': 0}
SHARED_INPUTS = []
_WEIGHT_DTYPES = {'w_mod': _jnp.float32, 'b_mod': _jnp.float32, 'norm_ffn1': _jnp.float32, 'ffn1_gate': _jnp.float32, 'ffn1_up': _jnp.float32, 'ffn1_down': _jnp.float32, 'norm_mix': _jnp.float32, 'w_in': _jnp.float32, 'q_norm': _jnp.float32, 'kv_norm': _jnp.float32, 'w_uq': _jnp.float32, 'w_ukv': _jnp.float32, 'sinks': _jnp.float32, 'w_o': _jnp.float32, 'norm_ffn2': _jnp.float32, 'ffn2_gate': _jnp.float32, 'ffn2_up': _jnp.float32, 'ffn2_down': _jnp.float32, 'rel_bias': _jnp.float32, 'norm_final': _jnp.float32}
MOMENT_SCALE = {'w_mod': 2.118913e-02, 'b_mod': 3.598027e-02, 'norm_ffn1': 1.973400e-02, 'ffn1_gate': 8.583983e-03, 'ffn1_up': 8.302756e-03, 'ffn1_down': 1.377495e-02, 'norm_mix': 1.496399e-02, 'w_in': 1.731922e-02, 'q_norm': 1.040248e-02, 'kv_norm': 2.942687e-02, 'w_uq': 5.587434e-03, 'w_ukv': 1.101776e-02, 'sinks': 3.708554e-03, 'w_o': 1.526293e-02, 'norm_ffn2': 1.901642e-02, 'ffn2_gate': 8.675068e-03, 'ffn2_up': 8.413444e-03, 'ffn2_down': 1.395574e-02, 'rel_bias': 1.131749e-02, 'norm_final': 1.604794e+01}


def _to_microbatches(a, axis):
    t = _jnp.moveaxis(a, axis, 0)
    t = t.reshape((N_MICROBATCH, t.shape[0] // N_MICROBATCH) + t.shape[1:])
    return _jnp.moveaxis(t, 1, axis + 1)


def setup_inputs(seed: int = 0) -> dict:
    inp = _fwd_setup_inputs(seed)
    key = _jax.random.fold_in(_jax.random.key(seed), 7919)
    shape, _ = _output_shape()
    out = dict(inp)
    out["loss_target"] = _jax.random.normal(_jax.random.fold_in(key, 0), shape, _jnp.float32)
    for i, name in enumerate(TWIN_WEIGHTS):
        w = inp[name].astype(_jnp.float32)
        if MOMENT_SCALE is None:
            s = _jnp.sqrt(_jnp.mean(_jnp.square(w)) + 1e-30)
        else:
            s = MOMENT_SCALE[name]
        km, kv = _jax.random.split(_jax.random.fold_in(key, i + 1))
        out[name] = w
        out["m_" + name] = s * _jax.random.normal(km, w.shape, _jnp.float32)
        out["v_" + name] = (s * s) * _jax.random.uniform(kv, w.shape, _jnp.float32, 0.5, 1.5)
    if N_MICROBATCH > 1:
        for name, axis in PER_EXAMPLE_BATCH_AXIS.items():
            out[name] = _to_microbatches(out[name], axis)
    return {'x': out['x'], 'c': out['c'], 'w_mod': out['w_mod'], 'b_mod': out['b_mod'], 'norm_ffn1': out['norm_ffn1'], 'ffn1_gate': out['ffn1_gate'], 'ffn1_up': out['ffn1_up'], 'ffn1_down': out['ffn1_down'], 'norm_mix': out['norm_mix'], 'w_in': out['w_in'], 'q_norm': out['q_norm'], 'kv_norm': out['kv_norm'], 'w_uq': out['w_uq'], 'w_ukv': out['w_ukv'], 'sinks': out['sinks'], 'w_o': out['w_o'], 'norm_ffn2': out['norm_ffn2'], 'ffn2_gate': out['ffn2_gate'], 'ffn2_up': out['ffn2_up'], 'ffn2_down': out['ffn2_down'], 'rel_bias': out['rel_bias'], 'norm_final': out['norm_final'], 'loss_target': out['loss_target'], 'm_w_mod': out['m_w_mod'], 'm_b_mod': out['m_b_mod'], 'm_norm_ffn1': out['m_norm_ffn1'], 'm_ffn1_gate': out['m_ffn1_gate'], 'm_ffn1_up': out['m_ffn1_up'], 'm_ffn1_down': out['m_ffn1_down'], 'm_norm_mix': out['m_norm_mix'], 'm_w_in': out['m_w_in'], 'm_q_norm': out['m_q_norm'], 'm_kv_norm': out['m_kv_norm'], 'm_w_uq': out['m_w_uq'], 'm_w_ukv': out['m_w_ukv'], 'm_sinks': out['m_sinks'], 'm_w_o': out['m_w_o'], 'm_norm_ffn2': out['m_norm_ffn2'], 'm_ffn2_gate': out['m_ffn2_gate'], 'm_ffn2_up': out['m_ffn2_up'], 'm_ffn2_down': out['m_ffn2_down'], 'm_rel_bias': out['m_rel_bias'], 'm_norm_final': out['m_norm_final'], 'v_w_mod': out['v_w_mod'], 'v_b_mod': out['v_b_mod'], 'v_norm_ffn1': out['v_norm_ffn1'], 'v_ffn1_gate': out['v_ffn1_gate'], 'v_ffn1_up': out['v_ffn1_up'], 'v_ffn1_down': out['v_ffn1_down'], 'v_norm_mix': out['v_norm_mix'], 'v_w_in': out['v_w_in'], 'v_q_norm': out['v_q_norm'], 'v_kv_norm': out['v_kv_norm'], 'v_w_uq': out['v_w_uq'], 'v_w_ukv': out['v_w_ukv'], 'v_sinks': out['v_sinks'], 'v_w_o': out['v_w_o'], 'v_norm_ffn2': out['v_norm_ffn2'], 'v_ffn2_gate': out['v_ffn2_gate'], 'v_ffn2_up': out['v_ffn2_up'], 'v_ffn2_down': out['v_ffn2_down'], 'v_rel_bias': out['v_rel_bias'], 'v_norm_final': out['v_norm_final']}


def _loss(weights, diff, rest, loss_target):
    with _jax.named_scope("forward"):
        args = {**rest, TWIN_DIFF_INPUT: diff, **{k: w.astype(_WEIGHT_DTYPES[k]) for k, w in weights.items()}}
        y = _forward(args)
    with _jax.named_scope("loss_head"):
        err = _jnp.square(y.astype(_jnp.float32) - loss_target)
        return 0.5 * _jnp.sum(_jnp.mean(err, axis=-1)) if err.ndim else 0.5 * err


def _adamw(w, g, m, v):
    m = ADAM_B1 * m + (1.0 - ADAM_B1) * g
    v = ADAM_B2 * v + (1.0 - ADAM_B2) * _jnp.square(g)
    m_hat = m / (1.0 - ADAM_B1 ** ADAM_STEP)
    v_hat = v / (1.0 - ADAM_B2 ** ADAM_STEP)
    delta = -ADAM_LR * (m_hat / (_jnp.sqrt(v_hat) + ADAM_EPS) + ADAM_WD * w)
    return delta, m, v


def reference(x, c, w_mod, b_mod, norm_ffn1, ffn1_gate, ffn1_up, ffn1_down, norm_mix, w_in, q_norm, kv_norm, w_uq, w_ukv, sinks, w_o, norm_ffn2, ffn2_gate, ffn2_up, ffn2_down, rel_bias, norm_final, loss_target, m_w_mod, m_b_mod, m_norm_ffn1, m_ffn1_gate, m_ffn1_up, m_ffn1_down, m_norm_mix, m_w_in, m_q_norm, m_kv_norm, m_w_uq, m_w_ukv, m_sinks, m_w_o, m_norm_ffn2, m_ffn2_gate, m_ffn2_up, m_ffn2_down, m_rel_bias, m_norm_final, v_w_mod, v_b_mod, v_norm_ffn1, v_ffn1_gate, v_ffn1_up, v_ffn1_down, v_norm_mix, v_w_in, v_q_norm, v_kv_norm, v_w_uq, v_w_ukv, v_sinks, v_w_o, v_norm_ffn2, v_ffn2_gate, v_ffn2_up, v_ffn2_down, v_rel_bias, v_norm_final):
    given = dict(x=x, c=c, w_mod=w_mod, b_mod=b_mod, norm_ffn1=norm_ffn1, ffn1_gate=ffn1_gate, ffn1_up=ffn1_up, ffn1_down=ffn1_down, norm_mix=norm_mix, w_in=w_in, q_norm=q_norm, kv_norm=kv_norm, w_uq=w_uq, w_ukv=w_ukv, sinks=sinks, w_o=w_o, norm_ffn2=norm_ffn2, ffn2_gate=ffn2_gate, ffn2_up=ffn2_up, ffn2_down=ffn2_down, rel_bias=rel_bias, norm_final=norm_final, loss_target=loss_target, m_w_mod=m_w_mod, m_b_mod=m_b_mod, m_norm_ffn1=m_norm_ffn1, m_ffn1_gate=m_ffn1_gate, m_ffn1_up=m_ffn1_up, m_ffn1_down=m_ffn1_down, m_norm_mix=m_norm_mix, m_w_in=m_w_in, m_q_norm=m_q_norm, m_kv_norm=m_kv_norm, m_w_uq=m_w_uq, m_w_ukv=m_w_ukv, m_sinks=m_sinks, m_w_o=m_w_o, m_norm_ffn2=m_norm_ffn2, m_ffn2_gate=m_ffn2_gate, m_ffn2_up=m_ffn2_up, m_ffn2_down=m_ffn2_down, m_rel_bias=m_rel_bias, m_norm_final=m_norm_final, v_w_mod=v_w_mod, v_b_mod=v_b_mod, v_norm_ffn1=v_norm_ffn1, v_ffn1_gate=v_ffn1_gate, v_ffn1_up=v_ffn1_up, v_ffn1_down=v_ffn1_down, v_norm_mix=v_norm_mix, v_w_in=v_w_in, v_q_norm=v_q_norm, v_kv_norm=v_kv_norm, v_w_uq=v_w_uq, v_w_ukv=v_w_ukv, v_sinks=v_sinks, v_w_o=v_w_o, v_norm_ffn2=v_norm_ffn2, v_ffn2_gate=v_ffn2_gate, v_ffn2_up=v_ffn2_up, v_ffn2_down=v_ffn2_down, v_rel_bias=v_rel_bias, v_norm_final=v_norm_final)
    weights = {n: given[n] for n in TWIN_WEIGHTS}
    shared = {n: given[n] for n in SHARED_INPUTS}
    per_example = {n: given[n] for n in ['x', 'c']}
    grad_fn = _jax.value_and_grad(_loss, argnums=(0, 1))

    def one_microbatch(ex, loss_target):
        ex = dict(ex)
        diff = ex.pop(TWIN_DIFF_INPUT)
        return grad_fn(weights, diff, {**shared, **ex}, loss_target)

    if N_MICROBATCH == 1:
        loss, (grad_w, grad_x) = one_microbatch(per_example, given["loss_target"])
    else:
        def body(carry, xs):
            loss_sum, grad_sum = carry
            l_k, (gw_k, gx_k) = one_microbatch(xs[0], xs[1])
            with _jax.named_scope("update"):
                return (loss_sum + l_k, _jax.tree.map(_jnp.add, grad_sum, gw_k)), gx_k

        init = (_jnp.zeros((), _jnp.float32), _jax.tree.map(_jnp.zeros_like, weights))
        (loss, grad_w), grad_x = _jax.lax.scan(body, init, (per_example, given["loss_target"]))
    with _jax.named_scope("update"):
        delta_w, new_m, new_v = {}, {}, {}
        for n in TWIN_WEIGHTS:
            delta_w[n], new_m[n], new_v[n] = _adamw(weights[n], grad_w[n], given["m_" + n], given["v_" + n])
    return (loss, grad_x, *[grad_w[n] for n in TWIN_WEIGHTS], *[delta_w[n] for n in TWIN_WEIGHTS],
            *[new_m[n] for n in TWIN_WEIGHTS], *[new_v[n] for n in TWIN_WEIGHTS])
```

```python
import functools
import math

import jax
import jax.numpy as jnp
import numpy as np
from jax import lax
from jax.experimental import pallas as pl
from jax.experimental.pallas import tpu as pltpu

F32, BF16 = jnp.float32, jnp.bfloat16

D_MODEL = 1024
D_FF = 2816
EPS = 1e-6
N_MOD = 9
SWA_HEADS, SWA_KV_HEADS, SWA_HEAD_DIM, WINDOW = 8, 2, 64, 128
SWA_GROUP = SWA_HEADS // SWA_KV_HEADS
MLA_HEADS, MLA_Q_RANK, MLA_KV_RANK, MLA_NOPE, MLA_ROPE, MLA_V = 4, 256, 128, 128, 64, 128
MLA_QK = MLA_NOPE + MLA_ROPE
ROPE_THETA = 10000.0
NUM_BUCKETS, MAX_DISTANCE = 32, 128
D_IN = 1216
N_SWA_COLS = 768
N_SWA_BLOCKS = N_SWA_COLS // SWA_HEAD_DIM
N_MLA_COLS = 512

ADAM_LR, ADAM_B1, ADAM_B2, ADAM_EPS, ADAM_WD, ADAM_STEP = 0.001, 0.9, 0.999, 1e-08, 0.01, 10

N_CHIPS = 4
N_DEV = 8
F_SHARD = D_FF // N_CHIPS
HALF = F_SHARD // 2
N_PIECES = 7
P_GATE1, P_UP1, P_DOWN1, P_GATE2, P_UP2, P_DOWN2, P_REST = range(7)
R_IN, R_O, R_UQ, R_UKV = 304, 256, 48, 32
O_IN, O_O, O_UQ, O_UKV, O_PAD = 0, 304, 560, 608, 640

PK_MOD = 72
PK_ROWS = 112
VMEM_LIMIT = 56 << 20
ROW_TILE = 512

_DN = {
    "nn": (((1,), (0,)), ((), ())),
    "nt": (((1,), (1,)), ((), ())),
    "tn": (((0,), (0,)), ((), ())),
}


def _sds(shape, dtype):
    return jax.ShapeDtypeStruct(tuple(shape), dtype)


def _cparams(sem=None):
    kw = {"vmem_limit_bytes": VMEM_LIMIT}
    if sem is not None:
        kw["dimension_semantics"] = sem
    return pltpu.CompilerParams(**kw)


def _dot(a, b, dims):
    return lax.dot_general(a.astype(BF16), b.astype(BF16), _DN[dims], preferred_element_type=F32)


def _mm(name, dims, grid, a, a_blk, a_idx, b, b_blk, b_idx, out, out_blk, out_idx, acc_shape, alias=None):
    nk = grid[2]

    def body(*refs):
        if alias is None:
            a_ref, b_ref, o_ref, acc_ref = refs
        else:
            a_ref, b_ref, _, o_ref, acc_ref = refs
        k = pl.program_id(2)

        @pl.when(k == 0)
        def _():
            acc_ref[...] = jnp.zeros_like(acc_ref)

        acc_ref[...] += _dot(a_ref[...], b_ref[...], dims)

        @pl.when(k == nk - 1)
        def _():
            o_ref[...] = acc_ref[...].astype(o_ref.dtype)

    in_specs = [pl.BlockSpec(a_blk, a_idx), pl.BlockSpec(b_blk, b_idx)]
    args = [a, b]
    kw = {}
    if alias is not None:
        in_specs.append(pl.BlockSpec(memory_space=pl.ANY))
        args.append(alias)
        kw["input_output_aliases"] = {2: 0}
    return pl.pallas_call(
        body,
        name=name,
        grid=grid,
        in_specs=in_specs,
        out_specs=pl.BlockSpec(out_blk, out_idx),
        out_shape=out,
        scratch_shapes=[pltpu.VMEM(acc_shape, F32)],
        compiler_params=_cparams(("parallel", "parallel", "arbitrary")),
        **kw,
    )(*args)


def _rowwise(name, fn, tiled, full, out_tiled, out_red, tm):
    rows = tiled[0].shape[-2]
    grid = (rows // tm,)
    n_in = len(tiled) + len(full)
    n_t = len(out_tiled)

    def tspec(shape):
        nd = len(shape)
        return pl.BlockSpec(tuple(shape[:-2]) + (tm, shape[-1]), lambda i, nd=nd: (0,) * (nd - 2) + (i, 0))

    def fspec(shape):
        nd = len(shape)
        return pl.BlockSpec(tuple(shape), lambda i, nd=nd: (0,) * nd)

    def body(*refs):
        outs = fn(*[r[...] for r in refs[:n_in]])
        if not isinstance(outs, (tuple, list)):
            outs = (outs,)
        for r, v in zip(refs[n_in:n_in + n_t], outs[:n_t]):
            r[...] = v.astype(r.dtype)
        red_refs = refs[n_in + n_t:]
        if red_refs:
            @pl.when(pl.program_id(0) == 0)
            def _():
                for r in red_refs:
                    r[...] = jnp.zeros_like(r)

            for r, v in zip(red_refs, outs[n_t:]):
                r[...] += v.astype(r.dtype)

    res = pl.pallas_call(
        body,
        name=name,
        grid=grid,
        in_specs=[tspec(a.shape) for a in tiled] + [fspec(a.shape) for a in full],
        out_specs=[tspec(o.shape) for o in out_tiled] + [fspec(o.shape) for o in out_red],
        out_shape=list(out_tiled) + list(out_red),
        compiler_params=_cparams(("arbitrary",) if out_red else ("parallel",)),
    )(*tiled, *full)
    return res


def _sum0(v):
    return jnp.sum(v, axis=0, keepdims=True)


def _sigmoid(v):
    return 1.0 / (1.0 + jnp.exp(-v))


def _rms(x):
    r = lax.rsqrt(jnp.mean(x * x, axis=-1, keepdims=True) + EPS)
    return x * r, r


def _rms_bwd(u, xr, r):
    return r * (u - xr * jnp.mean(u * xr, axis=-1, keepdims=True))


def _ffn_fwd(tag, x, gamma, sh, sc, gate, g4, base):
    s_len = x.shape[0]

    def normmod(x_, gamma_, sc_, sh_):
        xr, _ = _rms(x_)
        return xr * gamma_ * (1.0 + sc_) + sh_

    (h,) = _rowwise(f"{tag}_normmod", normmod, [x], [gamma, sc, sh], [_sds((s_len, D_MODEL), BF16)], [], 256)

    tm = min(ROW_TILE, s_len)
    ab3 = _mm(
        f"{tag}_gate_up", "nt", (s_len // tm, 2 * N_CHIPS, 1),
        h, (tm, D_MODEL), lambda i, j, k: (i, 0),
        g4, (None, None, F_SHARD, D_MODEL), lambda i, j, k: (j % N_CHIPS, base + j // N_CHIPS, 0, 0),
        _sds((2 * N_CHIPS, s_len, F_SHARD), BF16), (None, tm, F_SHARD), lambda i, j, k: (j, i, 0),
        (tm, F_SHARD),
    )

    def swiglu(ab):
        a = ab[:N_CHIPS].astype(F32)
        b = ab[N_CHIPS:].astype(F32)
        return a * _sigmoid(a) * b

    (s3,) = _rowwise(f"{tag}_swiglu", swiglu, [ab3], [], [_sds((N_CHIPS, s_len, F_SHARD), BF16)], [], 128)

    y = _mm(
        f"{tag}_down", "nn", (s_len // tm, 1, N_CHIPS),
        s3, (None, tm, F_SHARD), lambda i, j, k: (k, i, 0),
        g4, (None, None, F_SHARD, D_MODEL), lambda i, j, k: (k, base + 2, 0, 0),
        _sds((s_len, D_MODEL), F32), (tm, D_MODEL), lambda i, j, k: (i, 0),
        (tm, D_MODEL),
    )

    (x_out,) = _rowwise(
        f"{tag}_residual", lambda x_, y_, g_: x_ + 0.5 * g_ * y_, [x, y], [gate], [_sds((s_len, D_MODEL), F32)], [], 256
    )
    return x_out, (h, ab3, s3, y)


def _normmod_bwd_call(name, dh_parts, x, dxo, gamma, sc):
    s_len = x.shape[0]
    n_parts = len(dh_parts)

    def fn(*vals):
        dh = vals[0]
        for extra in vals[1:n_parts]:
            dh = dh + extra
        x_, dxo_, gamma_, sc_ = vals[n_parts:]
        xr, r = _rms(x_)
        n = xr * gamma_
        dn = dh * (1.0 + sc_)
        dx = dxo_ + _rms_bwd(dn * gamma_, xr, r)
        return dx, _sum0(dh * n), _sum0(dh), _sum0(dn * xr)

    vec = _sds((1, D_MODEL), F32)
    return _rowwise(name, fn, list(dh_parts) + [x, dxo], [gamma, sc], [_sds((s_len, D_MODEL), F32)], [vec, vec, vec], 256)


def _ffn_bwd(tag, dxo, x, gamma, sc, gate, g4, base, saved, gb):
    h, ab3, s3, y = saved
    s_len = x.shape[0]
    vec = _sds((1, D_MODEL), F32)

    dy, dgate = _rowwise(
        f"{tag}_bwd_gate", lambda dxo_, y_, g_: (0.5 * g_ * dxo_, _sum0(0.5 * y_ * dxo_)),
        [dxo, y], [gate], [_sds((s_len, D_MODEL), BF16)], [vec], 256,
    )

    tm = min(ROW_TILE, s_len)
    ds3 = _mm(
        f"{tag}_bwd_ds", "nt", (s_len // tm, N_CHIPS, 1),
        dy, (tm, D_MODEL), lambda i, j, k: (i, 0),
        g4, (None, None, F_SHARD, D_MODEL), lambda i, j, k: (j, base + 2, 0, 0),
        _sds((N_CHIPS, s_len, F_SHARD), BF16), (None, tm, F_SHARD), lambda i, j, k: (j, i, 0),
        (tm, F_SHARD),
    )

    def swiglu_bwd(ab, ds):
        a = ab[:N_CHIPS].astype(F32)
        b = ab[N_CHIPS:].astype(F32)
        ds = ds.astype(F32)
        sig = _sigmoid(a)
        da = ds * b * sig * (1.0 + a * (1.0 - sig))
        db = ds * a * sig
        return jnp.concatenate([da, db], axis=0)

    (dab3,) = _rowwise(
        f"{tag}_bwd_swiglu", swiglu_bwd, [ab3, ds3], [], [_sds((2 * N_CHIPS, s_len, F_SHARD), BF16)], [], 128
    )

    tk = tm
    gb = _mm(
        f"{tag}_bwd_wdown", "tn", (N_CHIPS, 1, s_len // tk),
        s3, (None, tk, F_SHARD), lambda i, j, k: (i, k, 0),
        dy, (tk, D_MODEL), lambda i, j, k: (k, 0),
        _sds(gb.shape, BF16), (None, None, F_SHARD, D_MODEL), lambda i, j, k: (i, base + 2, 0, 0),
        (F_SHARD, D_MODEL), alias=gb,
    )
    gb = _mm(
        f"{tag}_bwd_wgu", "tn", (2 * N_CHIPS, 1, s_len // tk),
        dab3, (None, tk, F_SHARD), lambda i, j, k: (i, k, 0),
        h, (tk, D_MODEL), lambda i, j, k: (k, 0),
        _sds(gb.shape, BF16), (None, None, F_SHARD, D_MODEL), lambda i, j, k: (i % N_CHIPS, base + i // N_CHIPS, 0, 0),
        (F_SHARD, D_MODEL), alias=gb,
    )
    dh = _mm(
        f"{tag}_bwd_dh", "nn", (s_len // tm, 1, 2 * N_CHIPS),
        dab3, (None, tm, F_SHARD), lambda i, j, k: (k, i, 0),
        g4, (None, None, F_SHARD, D_MODEL), lambda i, j, k: (k % N_CHIPS, base + k // N_CHIPS, 0, 0),
        _sds((s_len, D_MODEL), F32), (tm, D_MODEL), lambda i, j, k: (i, 0),
        (tm, D_MODEL),
    )
    dx, dsc, dsh, dgamma = _normmod_bwd_call(f"{tag}_bwd_normmod", [dh], x, dxo, gamma, sc)
    return dx, gb, (dsh, dsc, dgate, dgamma)


def _bucket_map():
    qi = np.arange(WINDOW)[:, None]
    kj = np.arange(2 * WINDOW)[None, :]
    dist = qi + WINDOW - kj
    band = (dist >= 0) & (dist < WINDOW)
    max_exact = NUM_BUCKETS // 2
    n = np.maximum(dist, 0)
    large = []
    for dt in (np.float32, np.float64):
        nf = np.maximum(n, 1).astype(dt)
        val = np.log(nf / dt(max_exact)) / dt(math.log(MAX_DISTANCE / max_exact)) * dt(NUM_BUCKETS - max_exact)
        large.append(np.minimum(max_exact + val.astype(np.int32), NUM_BUCKETS - 1))
    assert np.array_equal(large[0], large[1])
    bucket = np.where(n < max_exact, n, large[0])
    return np.where(band, bucket, -1).astype(np.int32).reshape(1, -1)


def _bias_expand(rel_bias_t, bkt):
    n = bkt.shape[1]

    def body(rb_ref, bkt_ref, o_ref):
        onehot = (lax.broadcasted_iota(jnp.int32, (NUM_BUCKETS, n), 0) == bkt_ref[...]).astype(F32)
        o_ref[...] = lax.dot_general(
            rb_ref[...], onehot, _DN["nn"], precision=lax.Precision.HIGHEST, preferred_element_type=F32
        )

    return pl.pallas_call(
        body, name="bias_expand", out_shape=_sds((SWA_HEADS, n), F32), compiler_params=_cparams()
    )(rel_bias_t, bkt)


def _bias_reduce(dbias_flat, bkt):
    n = bkt.shape[1]

    def body(db_ref, bkt_ref, o_ref):
        onehot = (lax.broadcasted_iota(jnp.int32, (NUM_BUCKETS, n), 0) == bkt_ref[...]).astype(F32)
        o_ref[...] = lax.dot_general(
            db_ref[...], onehot, _DN["nt"], precision=lax.Precision.HIGHEST, preferred_element_type=F32
        )

    return pl.pallas_call(
        body, name="bias_reduce", out_shape=_sds((SWA_HEADS, NUM_BUCKETS), F32), compiler_params=_cparams()
    )(dbias_flat, bkt)


_SWA_SCALE = SWA_HEAD_DIM ** -0.5
_NEG = -1e30


def _swa_in_specs():
    g, w, hd = SWA_GROUP, WINDOW, SWA_HEAD_DIM
    prev = lambda n: jnp.maximum(n - 1, 0)
    return [
        pl.BlockSpec((g, w, hd), lambda j, n: (j, n, 0)),
        pl.BlockSpec((1, w, hd), lambda j, n: (SWA_HEADS + j, prev(n), 0)),
        pl.BlockSpec((1, w, hd), lambda j, n: (SWA_HEADS + j, n, 0)),
        pl.BlockSpec((1, w, hd), lambda j, n: (SWA_HEADS + SWA_KV_HEADS + j, prev(n), 0)),
        pl.BlockSpec((1, w, hd), lambda j, n: (SWA_HEADS + SWA_KV_HEADS + j, n, 0)),
        pl.BlockSpec((g, w, 2 * w), lambda j, n: (j, 0, 0)),
        pl.BlockSpec((g, w, 1), lambda j, n: (j, 0, 0)),
    ]


def _swa_scores(q_ref, kp_ref, kc_ref, bias_ref, n):
    g, w = SWA_GROUP, WINDOW
    q = q_ref[...].reshape(g * w, SWA_HEAD_DIM)
    kk = jnp.concatenate([kp_ref[0], kc_ref[0]], axis=0)
    s = (_dot(q, kk, "nt") * _SWA_SCALE).reshape(g, w, 2 * w) + bias_ref[...]
    qi = lax.broadcasted_iota(jnp.int32, (w, 2 * w), 0)
    kj = lax.broadcasted_iota(jnp.int32, (w, 2 * w), 1)
    dist = qi + w - kj
    valid = (dist >= 0) & (dist < w) & ((n > 0) | (kj >= w))
    return q, kk, jnp.where(valid[None], s, _NEG), valid[None]


def _swa_fwd(swa3, bias, sinkb):
    s_len = swa3.shape[1]
    g, w, hd = SWA_GROUP, WINDOW, SWA_HEAD_DIM

    def body(q_ref, kp_ref, kc_ref, vp_ref, vc_ref, bias_ref, sink_ref, o_ref, lse_ref):
        n = pl.program_id(1)
        _, _, s, valid = _swa_scores(q_ref, kp_ref, kc_ref, bias_ref, n)
        vv = jnp.concatenate([vp_ref[0], vc_ref[0]], axis=0)
        sink = sink_ref[...]
        m = jnp.maximum(jnp.max(s, axis=-1, keepdims=True), sink)
        p = jnp.where(valid, jnp.exp(s - m), 0.0)
        l = jnp.sum(p, axis=-1, keepdims=True) + jnp.exp(sink - m)
        o = _dot(p.reshape(g * w, 2 * w), vv, "nn").reshape(g, w, hd)
        o_ref[...] = (o / l).astype(o_ref.dtype)
        lse_ref[...] = m + jnp.log(l)

    return pl.pallas_call(
        body,
        name="swa_fwd",
        grid=(SWA_KV_HEADS, s_len // w),
        in_specs=_swa_in_specs(),
        out_specs=[
            pl.BlockSpec((g, w, hd), lambda j, n: (j, n, 0)),
            pl.BlockSpec((g, w, 1), lambda j, n: (j, n, 0)),
        ],
        out_shape=[_sds((SWA_HEADS, s_len, hd), BF16), _sds((SWA_HEADS, s_len, 1), F32)],
        compiler_params=_cparams(("parallel", "parallel")),
    )(swa3, swa3, swa3, swa3, swa3, bias, sinkb)


def _swa_bwd(swa3, bias, sinkb, o3, do3, lse):
    s_len = swa3.shape[1]
    g, w, hd = SWA_GROUP, WINDOW, SWA_HEAD_DIM

    def body(q_ref, kp_ref, kc_ref, vp_ref, vc_ref, bias_ref, sink_ref, o_ref, do_ref, lse_ref,
             dq_ref, dka_ref, dkb_ref, dva_ref, dvb_ref, dbias_ref, dsink_ref):
        n = pl.program_id(1)

        @pl.when(n == 0)
        def _():
            dbias_ref[...] = jnp.zeros_like(dbias_ref)
            dsink_ref[...] = jnp.zeros_like(dsink_ref)

        q, kk, s, valid = _swa_scores(q_ref, kp_ref, kc_ref, bias_ref, n)
        vv = jnp.concatenate([vp_ref[0], vc_ref[0]], axis=0)
        lse_v = lse_ref[...]
        p = jnp.where(valid, jnp.exp(s - lse_v), 0.0)
        do = do_ref[...].astype(F32)
        delta = jnp.sum(do * o_ref[...].astype(F32), axis=-1, keepdims=True)
        do2 = do.reshape(g * w, hd)
        dp = _dot(do2, vv, "nt").reshape(g, w, 2 * w)
        ds = p * (dp - delta)
        dbias_ref[...] += ds
        dsink_ref[...] -= jnp.exp(sink_ref[...] - lse_v) * delta
        ds2 = ds.reshape(g * w, 2 * w)
        dq_ref[...] = (_dot(ds2, kk, "nn") * _SWA_SCALE).reshape(g, w, hd).astype(dq_ref.dtype)
        dkk = _dot(ds2, q, "tn") * _SWA_SCALE
        dvv = _dot(p.reshape(g * w, 2 * w), do2, "tn")
        dkb_ref[0] = dkk[:w]
        dka_ref[0] = dkk[w:]
        dvb_ref[0] = dvv[:w]
        dva_ref[0] = dvv[w:]

    kv_spec = pl.BlockSpec((1, w, hd), lambda j, n: (j, n, 0))
    kv_sds = _sds((SWA_KV_HEADS, s_len, hd), F32)
    return pl.pallas_call(
        body,
        name="swa_bwd",
        grid=(SWA_KV_HEADS, s_len // w),
        in_specs=_swa_in_specs() + [
            pl.BlockSpec((g, w, hd), lambda j, n: (j, n, 0)),
            pl.BlockSpec((g, w, hd), lambda j, n: (j, n, 0)),
            pl.BlockSpec((g, w, 1), lambda j, n: (j, n, 0)),
        ],
        out_specs=[
            pl.BlockSpec((g, w, hd), lambda j, n: (j, n, 0)),
            kv_spec, kv_spec, kv_spec, kv_spec,
            pl.BlockSpec((g, w, 2 * w), lambda j, n: (j, 0, 0)),
            pl.BlockSpec((g, w, 1), lambda j, n: (j, 0, 0)),
        ],
        out_shape=[
            _sds((SWA_HEADS, s_len, hd), BF16), kv_sds, kv_sds, kv_sds, kv_sds,
            _sds((SWA_HEADS, w, 2 * w), F32), _sds((SWA_HEADS, w, 1), F32),
        ],
        compiler_params=_cparams(("parallel", "arbitrary")),
    )(swa3, swa3, swa3, swa3, swa3, bias, sinkb, o3, do3, lse)


_MLA_SCALE = MLA_QK ** -0.5
_MLA_TQ = 256


def _mla_mask(i, tq, s_len):
    row = i * tq + lax.broadcasted_iota(jnp.int32, (tq, s_len), 0)
    col = lax.broadcasted_iota(jnp.int32, (tq, s_len), 1)
    return col <= row


def _mla_fwd(q4, k4, v4):
    s_len = q4.shape[1]
    tq = min(_MLA_TQ, s_len)

    def body(q_ref, k_ref, v_ref, o_ref, lse_ref):
        mask = _mla_mask(pl.program_id(1), tq, s_len)
        s = jnp.where(mask, _dot(q_ref[...], k_ref[...], "nt") * _MLA_SCALE, _NEG)
        m = jnp.max(s, axis=-1, keepdims=True)
        p = jnp.exp(s - m)
        l = jnp.sum(p, axis=-1, keepdims=True)
        o_ref[...] = (_dot(p, v_ref[...], "nn") / l).astype(o_ref.dtype)
        lse_ref[...] = m + jnp.log(l)

    return pl.pallas_call(
        body,
        name="mla_fwd",
        grid=(MLA_HEADS, s_len // tq),
        in_specs=[
            pl.BlockSpec((None, tq, MLA_QK), lambda h, i: (h, i, 0)),
            pl.BlockSpec((None, s_len, MLA_QK), lambda h, i: (h, 0, 0)),
            pl.BlockSpec((None, s_len, MLA_V), lambda h, i: (h, 0, 0)),
        ],
        out_specs=[
            pl.BlockSpec((None, tq, MLA_V), lambda h, i: (h, i, 0)),
            pl.BlockSpec((None, tq, 1), lambda h, i: (h, i, 0)),
        ],
        out_shape=[_sds((MLA_HEADS, s_len, MLA_V), BF16), _sds((MLA_HEADS, s_len, 1), F32)],
        compiler_params=_cparams(("parallel", "parallel")),
    )(q4, k4, v4)


def _mla_bwd(q4, k4, v4, o4, do4, lse):
    s_len = q4.shape[1]
    tq = min(_MLA_TQ, s_len)

    def body(q_ref, k_ref, v_ref, o_ref, do_ref, lse_ref, dq_ref, dk_ref, dv_ref):
        i = pl.program_id(1)

        @pl.when(i == 0)
        def _():
            dk_ref[...] = jnp.zeros_like(dk_ref)
            dv_ref[...] = jnp.zeros_like(dv_ref)

        mask = _mla_mask(i, tq, s_len)
        q, k, v = q_ref[...], k_ref[...], v_ref[...]
        s = jnp.where(mask, _dot(q, k, "nt") * _MLA_SCALE, _NEG)
        p = jnp.where(mask, jnp.exp(s - lse_ref[...]), 0.0)
        do = do_ref[...]
        delta = jnp.sum(do.astype(F32) * o_ref[...].astype(F32), axis=-1, keepdims=True)
        dp = _dot(do, v, "nt")
        ds = (p * (dp - delta) * _MLA_SCALE).astype(BF16)
        dq_ref[...] = _dot(ds, k, "nn")
        dk_ref[...] += _dot(ds, q, "tn")
        dv_ref[...] += _dot(p, do, "tn")

    return pl.pallas_call(
        body,
        name="mla_bwd",
        grid=(MLA_HEADS, s_len // tq),
        in_specs=[
            pl.BlockSpec((None, tq, MLA_QK), lambda h, i: (h, i, 0)),
            pl.BlockSpec((None, s_len, MLA_QK), lambda h, i: (h, 0, 0)),
            pl.BlockSpec((None, s_len, MLA_V), lambda h, i: (h, 0, 0)),
            pl.BlockSpec((None, tq, MLA_V), lambda h, i: (h, i, 0)),
            pl.BlockSpec((None, tq, MLA_V), lambda h, i: (h, i, 0)),
            pl.BlockSpec((None, tq, 1), lambda h, i: (h, i, 0)),
        ],
        out_specs=[
            pl.BlockSpec((None, tq, MLA_QK), lambda h, i: (h, i, 0)),
            pl.BlockSpec((None, s_len, MLA_QK), lambda h, i: (h, 0, 0)),
            pl.BlockSpec((None, s_len, MLA_V), lambda h, i: (h, 0, 0)),
        ],
        out_shape=[
            _sds((MLA_HEADS, s_len, MLA_QK), F32),
            _sds((MLA_HEADS, s_len, MLA_QK), F32),
            _sds((MLA_HEADS, s_len, MLA_V), F32),
        ],
        compiler_params=_cparams(("parallel", "arbitrary")),
    )(q4, k4, v4, o4, do4, lse)


def _mla_split(pm):
    q_lat = pm[:, :MLA_Q_RANK]
    kv_lat = pm[:, MLA_Q_RANK:MLA_Q_RANK + MLA_KV_RANK]
    kr = pm[:, MLA_Q_RANK + MLA_KV_RANK:MLA_Q_RANK + MLA_KV_RANK + MLA_ROPE]
    kr_sw = pm[:, MLA_Q_RANK + MLA_KV_RANK + MLA_ROPE:]
    return q_lat, kv_lat, kr, kr_sw


def _mla_proj(pm, cos2, sin2, q_norm, kv_norm, wq_ext, wkv):
    s_len = pm.shape[0]

    def fn(pm_, cos_, sin_, qg, kvg, wq, wk):
        q_lat, kv_lat, kr, kr_sw = _mla_split(pm_)
        nq = (_rms(q_lat)[0] * qg).astype(BF16)
        nkv = (_rms(kv_lat)[0] * kvg).astype(BF16)
        k_rot = kr * cos_ + kr_sw * sin_
        qs, ks, vs = [], [], []
        for h in range(MLA_HEADS):
            qe = _dot(nq, wq[h], "nt")
            q_rot = qe[:, MLA_NOPE:MLA_QK] * cos_ + qe[:, MLA_QK:] * sin_
            qs.append(jnp.concatenate([qe[:, :MLA_NOPE], q_rot], axis=-1))
            kve = _dot(nkv, wk[h], "nt")
            ks.append(jnp.concatenate([kve[:, :MLA_NOPE], k_rot], axis=-1))
            vs.append(kve[:, MLA_NOPE:])
        return jnp.stack(qs), jnp.stack(ks), jnp.stack(vs)

    return _rowwise(
        "mla_proj", fn, [pm, cos2, sin2], [q_norm, kv_norm, wq_ext, wkv],
        [_sds((MLA_HEADS, s_len, MLA_QK), BF16), _sds((MLA_HEADS, s_len, MLA_QK), BF16),
         _sds((MLA_HEADS, s_len, MLA_V), BF16)], [], 256,
    )


def _mla_proj_bwd(pm, cos2, sin2, dq4, dk4, dv4, q_norm, kv_norm, wq_ext, wkv):
    s_len = pm.shape[0]

    def fn(pm_, cos_, sin_, dq, dk, dv, qg, kvg, wq, wk):
        q_lat, kv_lat, _, _ = _mla_split(pm_)
        xq, rq = _rms(q_lat)
        xkv, rkv = _rms(kv_lat)
        nq = (xq * qg).astype(BF16)
        nkv = (xkv * kvg).astype(BF16)
        dnq = jnp.zeros_like(q_lat)
        dnkv = jnp.zeros_like(kv_lat)
        dkr = jnp.zeros_like(cos_)
        dwq, dwk = [], []
        for h in range(MLA_HEADS):
            dy = dq[h][:, MLA_NOPE:]
            dqe = jnp.concatenate([dq[h][:, :MLA_NOPE], dy * cos_, dy * sin_], axis=-1).astype(BF16)
            dnq = dnq + _dot(dqe, wq[h], "nn")
            dwq.append(_dot(dqe, nq, "tn"))
            dkve = jnp.concatenate([dk[h][:, :MLA_NOPE], dv[h]], axis=-1).astype(BF16)
            dnkv = dnkv + _dot(dkve, wk[h], "nn")
            dwk.append(_dot(dkve, nkv, "tn"))
            dkr = dkr + dk[h][:, MLA_NOPE:]
        dq_lat = _rms_bwd(dnq * qg, xq, rq)
        dkv_lat = _rms_bwd(dnkv * kvg, xkv, rkv)
        dpm = jnp.concatenate([dq_lat, dkv_lat, dkr * cos_, dkr * sin_], axis=-1)
        return dpm, _sum0(dnq * xq), _sum0(dnkv * xkv), jnp.stack(dwq), jnp.stack(dwk)

    return _rowwise(
        "mla_proj_bwd", fn, [pm, cos2, sin2, dq4, dk4, dv4], [q_norm, kv_norm, wq_ext, wkv],
        [_sds((s_len, N_MLA_COLS), BF16)],
        [_sds((1, MLA_Q_RANK), F32), _sds((1, MLA_KV_RANK), F32),
         _sds(wq_ext.shape, F32), _sds(wkv.shape, F32)], 256,
    )


def _rope_tables(s_len):
    inv = ROPE_THETA ** (-jnp.arange(0, MLA_ROPE, 2, dtype=F32) / MLA_ROPE)
    ang = jnp.arange(s_len, dtype=F32)[:, None] * inv[None, :]
    cos, sin = jnp.cos(ang), jnp.sin(ang)
    return jnp.concatenate([cos, cos], axis=-1), jnp.concatenate([-sin, sin], axis=-1)


def _mix_weights(g4):
    rest = g4[:, P_REST]
    w_in_t = rest[:, O_IN:O_IN + R_IN].reshape(D_IN, D_MODEL)
    w_o = rest[:, O_O:O_O + R_O].reshape(D_MODEL, D_MODEL)
    w_uq_t = rest[:, O_UQ:O_UQ + R_UQ].reshape(MLA_HEADS, MLA_QK, MLA_Q_RANK)
    w_ukv_t = rest[:, O_UKV:O_UKV + R_UKV].reshape(MLA_HEADS, MLA_NOPE + MLA_V, MLA_KV_RANK)
    half = MLA_ROPE // 2
    w_swa = w_in_t[:N_SWA_COLS]
    w_mla = jnp.concatenate([w_in_t[N_SWA_COLS:], w_in_t[D_IN - half:], w_in_t[D_IN - MLA_ROPE:D_IN - half]], axis=0)
    wq_ext = jnp.concatenate([w_uq_t, w_uq_t[:, MLA_QK - half:], w_uq_t[:, MLA_NOPE:MLA_QK - half]], axis=1)
    return w_swa, w_mla, w_o, wq_ext, w_ukv_t


def _mix_fwd(x, gamma, sh, sc, gate, weights, q_norm, kv_norm, bias, sinkb, cos2, sin2):
    w_swa, w_mla, w_o, wq_ext, wkv = weights
    s_len = x.shape[0]
    tm = min(ROW_TILE, s_len)

    def normmod(x_, gamma_, sc_, sh_):
        return _rms(x_)[0] * gamma_ * (1.0 + sc_) + sh_

    (h,) = _rowwise("mix_normmod", normmod, [x], [gamma, sc, sh], [_sds((s_len, D_MODEL), BF16)], [], 256)
    swa3 = _mm(
        "mix_proj_swa", "nt", (s_len // tm, N_SWA_BLOCKS, 1),
        h, (tm, D_MODEL), lambda i, j, k: (i, 0),
        w_swa, (SWA_HEAD_DIM, D_MODEL), lambda i, j, k: (j, 0),
        _sds((N_SWA_BLOCKS, s_len, SWA_HEAD_DIM), BF16), (None, tm, SWA_HEAD_DIM), lambda i, j, k: (j, i, 0),
        (tm, SWA_HEAD_DIM),
    )
    pm = _mm(
        "mix_proj_mla", "nt", (s_len // tm, 1, 1),
        h, (tm, D_MODEL), lambda i, j, k: (i, 0),
        w_mla, (N_MLA_COLS, D_MODEL), lambda i, j, k: (0, 0),
        _sds((s_len, N_MLA_COLS), F32), (tm, N_MLA_COLS), lambda i, j, k: (i, 0),
        (tm, N_MLA_COLS),
    )
    q4, k4, v4 = _mla_proj(pm, cos2, sin2, q_norm, kv_norm, wq_ext, wkv)
    oa3, lse_a = _swa_fwd(swa3, bias, sinkb)
    ob4, lse_b = _mla_fwd(q4, k4, v4)
    n_a = SWA_HEADS * SWA_HEAD_DIM
    za = _mm(
        "mix_out_swa", "nn", (s_len // tm, 1, SWA_HEADS),
        oa3, (None, tm, SWA_HEAD_DIM), lambda i, j, k: (k, i, 0),
        w_o, (SWA_HEAD_DIM, D_MODEL), lambda i, j, k: (k, 0),
        _sds((s_len, D_MODEL), F32), (tm, D_MODEL), lambda i, j, k: (i, 0),
        (tm, D_MODEL),
    )
    zb = _mm(
        "mix_out_mla", "nn", (s_len // tm, 1, MLA_HEADS),
        ob4, (None, tm, MLA_V), lambda i, j, k: (k, i, 0),
        w_o, (MLA_V, D_MODEL), lambda i, j, k: (n_a // MLA_V + k, 0),
        _sds((s_len, D_MODEL), F32), (tm, D_MODEL), lambda i, j, k: (i, 0),
        (tm, D_MODEL),
    )
    (x_out,) = _rowwise(
        "mix_residual", lambda x_, za_, zb_, g_: x_ + g_ * (za_ + zb_), [x, za, zb], [gate],
        [_sds((s_len, D_MODEL), F32)], [], 256,
    )
    return x_out, (h, swa3, pm, q4, k4, v4, oa3, lse_a, ob4, lse_b, za, zb)


def _mix_bwd(dxo, x, gamma, sc, gate, weights, q_norm, kv_norm, bias, sinkb, cos2, sin2, saved):
    w_swa, w_mla, w_o, wq_ext, wkv = weights
    h, swa3, pm, q4, k4, v4, oa3, lse_a, ob4, lse_b, za, zb = saved
    s_len = x.shape[0]
    tm = tk = min(ROW_TILE, s_len)
    vec = _sds((1, D_MODEL), F32)
    n_a = SWA_HEADS * SWA_HEAD_DIM

    dz, dgate = _rowwise(
        "mix_bwd_gate", lambda dxo_, za_, zb_, g_: (g_ * dxo_, _sum0((za_ + zb_) * dxo_)),
        [dxo, za, zb], [gate], [_sds((s_len, D_MODEL), BF16)], [vec], 256,
    )
    dwo_a = _mm(
        "mix_bwd_wo_swa", "tn", (SWA_HEADS, 1, s_len // tk),
        oa3, (None, tk, SWA_HEAD_DIM), lambda i, j, k: (i, k, 0),
        dz, (tk, D_MODEL), lambda i, j, k: (k, 0),
        _sds((n_a, D_MODEL), F32), (SWA_HEAD_DIM, D_MODEL), lambda i, j, k: (i, 0),
        (SWA_HEAD_DIM, D_MODEL),
    )
    dwo_b = _mm(
        "mix_bwd_wo_mla", "tn", (MLA_HEADS, 1, s_len // tk),
        ob4, (None, tk, MLA_V), lambda i, j, k: (i, k, 0),
        dz, (tk, D_MODEL), lambda i, j, k: (k, 0),
        _sds((MLA_HEADS * MLA_V, D_MODEL), F32), (MLA_V, D_MODEL), lambda i, j, k: (i, 0),
        (MLA_V, D_MODEL),
    )
    doa3 = _mm(
        "mix_bwd_do_swa", "nt", (s_len // tm, SWA_HEADS, 1),
        dz, (tm, D_MODEL), lambda i, j, k: (i, 0),
        w_o, (SWA_HEAD_DIM, D_MODEL), lambda i, j, k: (j, 0),
        _sds((SWA_HEADS, s_len, SWA_HEAD_DIM), BF16), (None, tm, SWA_HEAD_DIM), lambda i, j, k: (j, i, 0),
        (tm, SWA_HEAD_DIM),
    )
    dob4 = _mm(
        "mix_bwd_do_mla", "nt", (s_len // tm, MLA_HEADS, 1),
        dz, (tm, D_MODEL), lambda i, j, k: (i, 0),
        w_o, (MLA_V, D_MODEL), lambda i, j, k: (n_a // MLA_V + j, 0),
        _sds((MLA_HEADS, s_len, MLA_V), BF16), (None, tm, MLA_V), lambda i, j, k: (j, i, 0),
        (tm, MLA_V),
    )
    dq3, dka, dkb, dva, dvb, dbias, dsink = _swa_bwd(swa3, bias, sinkb, oa3, doa3, lse_a)
    dq4, dk4, dv4 = _mla_bwd(q4, k4, v4, ob4, dob4, lse_b)
    dpm, dq_norm, dkv_norm, dwq_ext, dwkv = _mla_proj_bwd(pm, cos2, sin2, dq4, dk4, dv4, q_norm, kv_norm, wq_ext, wkv)

    def fold(da, db):
        return da + jnp.concatenate([db[:, WINDOW:], jnp.zeros_like(db[:, :WINDOW])], axis=1)

    dswa3 = jnp.concatenate([dq3, fold(dka, dkb).astype(BF16), fold(dva, dvb).astype(BF16)], axis=0)

    dw_swa = _mm(
        "mix_bwd_win_swa", "tn", (N_SWA_BLOCKS, 1, s_len // tk),
        dswa3, (None, tk, SWA_HEAD_DIM), lambda i, j, k: (i, k, 0),
        h, (tk, D_MODEL), lambda i, j, k: (k, 0),
        _sds((N_SWA_COLS, D_MODEL), F32), (SWA_HEAD_DIM, D_MODEL), lambda i, j, k: (i, 0),
        (SWA_HEAD_DIM, D_MODEL),
    )
    dw_mla = _mm(
        "mix_bwd_win_mla", "tn", (1, 1, s_len // tk),
        dpm, (tk, N_MLA_COLS), lambda i, j, k: (k, 0),
        h, (tk, D_MODEL), lambda i, j, k: (k, 0),
        _sds((N_MLA_COLS, D_MODEL), F32), (N_MLA_COLS, D_MODEL), lambda i, j, k: (0, 0),
        (N_MLA_COLS, D_MODEL),
    )
    dh_a = _mm(
        "mix_bwd_dh_swa", "nn", (s_len // tm, 1, N_SWA_BLOCKS),
        dswa3, (None, tm, SWA_HEAD_DIM), lambda i, j, k: (k, i, 0),
        w_swa, (SWA_HEAD_DIM, D_MODEL), lambda i, j, k: (k, 0),
        _sds((s_len, D_MODEL), F32), (tm, D_MODEL), lambda i, j, k: (i, 0),
        (tm, D_MODEL),
    )
    dh_b = _mm(
        "mix_bwd_dh_mla", "nn", (s_len // tm, 1, 1),
        dpm, (tm, N_MLA_COLS), lambda i, j, k: (i, 0),
        w_mla, (N_MLA_COLS, D_MODEL), lambda i, j, k: (0, 0),
        _sds((s_len, D_MODEL), F32), (tm, D_MODEL), lambda i, j, k: (i, 0),
        (tm, D_MODEL),
    )
    dx, dsc, dsh, dgamma = _normmod_bwd_call("mix_bwd_normmod", [dh_a, dh_b], x, dxo, gamma, sc)

    half = MLA_ROPE // 2
    n_mla_in = D_IN - N_SWA_COLS
    dw_mla_base = dw_mla[:n_mla_in]
    dw_mla_base = dw_mla_base.at[n_mla_in - half:].add(dw_mla[n_mla_in:n_mla_in + half])
    dw_mla_base = dw_mla_base.at[n_mla_in - MLA_ROPE:n_mla_in - half].add(dw_mla[n_mla_in + half:])
    dw_in_t = jnp.concatenate([dw_swa, dw_mla_base], axis=0)
    dw_uq_t = dwq_ext[:, :MLA_QK]
    dw_uq_t = dw_uq_t.at[:, MLA_QK - half:].add(dwq_ext[:, MLA_QK:MLA_QK + half])
    dw_uq_t = dw_uq_t.at[:, MLA_NOPE:MLA_QK - half].add(dwq_ext[:, MLA_QK + half:])
    dw_o = jnp.concatenate([dwo_a, dwo_b], axis=0)
    rest = jnp.concatenate(
        [
            dw_in_t.reshape(N_CHIPS, R_IN, D_MODEL),
            dw_o.reshape(N_CHIPS, R_O, D_MODEL),
            dw_uq_t.reshape(N_CHIPS, R_UQ, D_MODEL),
            dwkv.reshape(N_CHIPS, R_UKV, D_MODEL),
            jnp.zeros((N_CHIPS, F_SHARD - O_PAD, D_MODEL), F32),
        ],
        axis=1,
    ).astype(BF16)
    return dx, rest, (dsh, dsc, dgate, dgamma), dq_norm, dkv_norm, dbias, dsink


def _device_step(x, target, mod, g4, norms, q_norm, kv_norm, sinks, rel_bias):
    s_len = x.shape[0]
    sh1, sc1, g1, sh2, sc2, g2, sh3, sc3, g3 = [mod[:, i * D_MODEL:(i + 1) * D_MODEL] for i in range(N_MOD)]
    n1, n2, n3, nf = norms
    bkt = jnp.asarray(_bucket_map())
    bias = _bias_expand(rel_bias.T, bkt).reshape(SWA_HEADS, WINDOW, 2 * WINDOW)
    sinkb = jnp.broadcast_to(sinks.reshape(SWA_HEADS, 1, 1), (SWA_HEADS, WINDOW, 1))
    cos2, sin2 = _rope_tables(s_len)
    mixw = _mix_weights(g4)

    x1, saved1 = _ffn_fwd("ffn1", x, n1, sh1, sc1, g1, g4, P_GATE1)
    x2, saved2 = _mix_fwd(x1, n2, sh2, sc2, g2, mixw, q_norm, kv_norm, bias, sinkb, cos2, sin2)
    x3, saved3 = _ffn_fwd("ffn2", x2, n3, sh3, sc3, g3, g4, P_GATE2)

    def head(x_, t_, g_):
        xr, r = _rms(x_)
        err = xr * g_ - t_
        dy = err * (1.0 / D_MODEL)
        return _rms_bwd(dy * g_, xr, r), _sum0(err * err), _sum0(dy * xr)

    vec = _sds((1, D_MODEL), F32)
    dx3, sq, dnf = _rowwise("loss_head", head, [x3, target], [nf], [_sds((s_len, D_MODEL), F32)], [vec, vec], 256)
    loss_part = (0.5 / D_MODEL) * jnp.sum(sq)

    gb = jnp.zeros((N_CHIPS, N_PIECES, F_SHARD, D_MODEL), BF16)
    dx2, gb, (dsh3, dsc3, dg3, dn3) = _ffn_bwd("ffn2", dx3, x2, n3, sc3, g3, g4, P_GATE2, saved3, gb)
    dx1, rest, (dsh2, dsc2, dg2, dn2), dq_norm, dkv_norm, dbias, dsink = _mix_bwd(
        dx2, x1, n2, sc2, g2, mixw, q_norm, kv_norm, bias, sinkb, cos2, sin2, saved2
    )
    dx0, gb, (dsh1, dsc1, dg1, dn1) = _ffn_bwd("ffn1", dx1, x, n1, sc1, g1, g4, P_GATE1, saved1, gb)
    gb = gb.at[:, P_REST].set(rest)

    dmod = jnp.concatenate([dsh1, dsc1, dg1, dsh2, dsc2, dg2, dsh3, dsc3, dg3], axis=-1)
    drel = _bias_reduce(dbias.reshape(SWA_HEADS, -1), bkt).T
    dsinks = jnp.sum(dsink, axis=(1, 2)).reshape(1, SWA_HEADS)
    small = (dn1, dn2, dn3, dnf, dq_norm, dkv_norm, dsinks, drel)
    return loss_part, dx0, gb, dmod, small


_MESH = pl.DeviceIdType.MESH


def _place():
    return lax.axis_index("x"), lax.axis_index("y"), lax.axis_index("c")


def _other_chips(x, y):
    return [(1 - x, y), (x, 1 - y), (1 - x, 1 - y)]


def _allgather_small(name, blk):
    m_per, n = blk.shape

    def body(x_ref, out_ref, send_sems, recv_sems, local_sem):
        x, y, c = _place()
        me, sibling = (x, y, c), (x, y, 1 - c)
        chips = _other_chips(x, y)

        def rows(px, py, pc):
            return out_ref.at[pl.ds((4 * px + 2 * py + pc) * m_per, m_per), :]

        def copy(k, block, to, src=None):
            return pltpu.make_async_remote_copy(
                src_ref=rows(*block) if src is None else src, dst_ref=rows(*block),
                send_sem=send_sems.at[k], recv_sem=recv_sems.at[k], device_id=to, device_id_type=_MESH,
            )

        mine = pltpu.make_async_copy(x_ref, rows(*me), local_sem)
        mine.start()
        first = [copy(0, me, sibling, src=x_ref)]
        first += [copy(1 + j, me, (*chip, c), src=x_ref) for j, chip in enumerate(chips)]
        for cp in first:
            cp.start()
        passed = [copy(4 + j, (*chip, c), sibling) for j, chip in enumerate(chips)]
        for j, chip in enumerate(chips):
            copy(1 + j, (*chip, c), me).wait_recv()
            passed[j].start()
        copy(0, sibling, me).wait_recv()
        for j, chip in enumerate(chips):
            copy(4 + j, (*chip, 1 - c), me).wait_recv()
        for cp in first + passed:
            cp.wait_send()
        mine.wait()

    return pl.pallas_call(
        body,
        name=name,
        out_shape=_sds((N_DEV * m_per, n), blk.dtype),
        in_specs=[pl.BlockSpec(memory_space=pltpu.VMEM)],
        out_specs=pl.BlockSpec(memory_space=pltpu.VMEM),
        scratch_shapes=[pltpu.SemaphoreType.DMA((7,)), pltpu.SemaphoreType.DMA((7,)), pltpu.SemaphoreType.DMA],
    )(blk)


def _allgather_weights(own):
    def body(own_ref, g_ref, send_sems, recv_sems, local_sem):
        x, y, c = _place()
        sibling = (x, y, 1 - c)
        chips = _other_chips(x, y)

        def slot(px, py, pc):
            return g_ref.at[2 * px + py, :, pl.ds(pc * HALF, HALF), :]

        def copy(k, src, dst, to):
            return pltpu.make_async_remote_copy(
                src_ref=src, dst_ref=dst, send_sem=send_sems.at[k], recv_sem=recv_sems.at[k],
                device_id=to, device_id_type=_MESH,
            )

        mine = pltpu.make_async_copy(own_ref, g_ref.at[2 * x + y], local_sem)
        mine.start()
        my_half = own_ref.at[:, pl.ds(c * HALF, HALF), :]
        first = [copy(j, my_half, slot(x, y, c), (*chip, c)) for j, chip in enumerate(chips)]
        for cp in first:
            cp.start()
        passed = [copy(3 + j, slot(*chip, c), slot(*chip, c), sibling) for j, chip in enumerate(chips)]
        for j, chip in enumerate(chips):
            copy(j, my_half, slot(*chip, c), (*chip, c)).wait_recv()
            passed[j].start()
        for j, chip in enumerate(chips):
            copy(3 + j, my_half, slot(*chip, 1 - c), sibling).wait_recv()
        for cp in first + passed:
            cp.wait_send()
        mine.wait()

    return pl.pallas_call(
        body,
        name="allgather_weights",
        out_shape=_sds((N_CHIPS,) + own.shape, own.dtype),
        in_specs=[pl.BlockSpec(memory_space=pl.ANY)],
        out_specs=pl.BlockSpec(memory_space=pl.ANY),
        scratch_shapes=[pltpu.SemaphoreType.DMA((6,)), pltpu.SemaphoreType.DMA((6,)), pltpu.SemaphoreType.DMA],
    )(own)


def _pair_exchange(gb):
    def body(gb_ref, land_ref, send_sem, recv_sem):
        x, y, c = _place()
        theirs = gb_ref.at[:, :, pl.ds((1 - c) * HALF, HALF), :]
        cp = pltpu.make_async_remote_copy(
            src_ref=theirs, dst_ref=land_ref, send_sem=send_sem, recv_sem=recv_sem,
            device_id=(x, y, 1 - c), device_id_type=_MESH,
        )
        cp.start()
        cp.wait()

    return pl.pallas_call(
        body,
        name="grads_pair_exchange",
        out_shape=_sds((N_CHIPS, N_PIECES, HALF, D_MODEL), gb.dtype),
        in_specs=[pl.BlockSpec(memory_space=pl.ANY)],
        out_specs=pl.BlockSpec(memory_space=pl.ANY),
        scratch_shapes=[pltpu.SemaphoreType.DMA, pltpu.SemaphoreType.DMA],
    )(gb)


def _pair_sum(gb, land):
    def body(gb_ref, land_ref, o_ref):
        c = lax.axis_index("c")
        mine = gb_ref[pl.ds(pl.multiple_of(c * HALF, 16), HALF), :]
        o_ref[...] = (mine.astype(F32) + land_ref[...].astype(F32)).astype(o_ref.dtype)

    return pl.pallas_call(
        body,
        name="grads_pair_sum",
        grid=(N_CHIPS, N_PIECES),
        in_specs=[
            pl.BlockSpec((None, None, F_SHARD, D_MODEL), lambda j, p: (j, p, 0, 0)),
            pl.BlockSpec((None, None, HALF, D_MODEL), lambda j, p: (j, p, 0, 0)),
        ],
        out_specs=pl.BlockSpec((None, None, HALF, D_MODEL), lambda j, p: (j, p, 0, 0)),
        out_shape=_sds(land.shape, BF16),
        compiler_params=_cparams(("parallel", "parallel")),
    )(gb, land)


def _chip_exchange(part):
    def body(p_ref, land_ref, send_sems, recv_sems, local_sem):
        x, y, c = _place()
        me = 2 * x + y
        chips = _other_chips(x, y)
        mine = pltpu.make_async_copy(p_ref.at[me], land_ref.at[me], local_sem)
        mine.start()

        def copy(k, chip):
            return pltpu.make_async_remote_copy(
                src_ref=p_ref.at[2 * chip[0] + chip[1]], dst_ref=land_ref.at[me],
                send_sem=send_sems.at[k], recv_sem=recv_sems.at[k], device_id=(*chip, c), device_id_type=_MESH,
            )

        sends = [copy(k, chip) for k, chip in enumerate(chips)]
        for cp in sends:
            cp.start()
        for k, chip in enumerate(chips):
            pltpu.make_async_remote_copy(
                src_ref=p_ref.at[me], dst_ref=land_ref.at[2 * chip[0] + chip[1]],
                send_sem=send_sems.at[k], recv_sem=recv_sems.at[k], device_id=(*chip, c), device_id_type=_MESH,
            ).wait_recv()
        for cp in sends:
            cp.wait_send()
        mine.wait()

    return pl.pallas_call(
        body,
        name="grads_chip_exchange",
        out_shape=_sds(part.shape, part.dtype),
        in_specs=[pl.BlockSpec(memory_space=pl.ANY)],
        out_specs=pl.BlockSpec(memory_space=pl.ANY),
        scratch_shapes=[pltpu.SemaphoreType.DMA((3,)), pltpu.SemaphoreType.DMA((3,)), pltpu.SemaphoreType.DMA],
    )(part)


def _chip_sum(land):
    def body(l_ref, o_ref):
        acc = l_ref[0].astype(F32)
        for j in range(1, N_CHIPS):
            acc = acc + l_ref[j].astype(F32)
        o_ref[...] = acc

    return pl.pallas_call(
        body,
        name="grads_chip_sum",
        grid=(N_PIECES,),
        in_specs=[pl.BlockSpec((N_CHIPS, None, HALF, D_MODEL), lambda p: (0, p, 0, 0))],
        out_specs=pl.BlockSpec((None, HALF, D_MODEL), lambda p: (p, 0, 0)),
        out_shape=_sds((N_PIECES, HALF, D_MODEL), F32),
        compiler_params=_cparams(("parallel",)),
    )(land)


def _pair_share(r_half):
    def body(h_ref, r_ref, send_sem, recv_sem, local_sem):
        x, y, c = _place()
        mine = r_ref.at[:, pl.ds(c * HALF, HALF), :]
        theirs = r_ref.at[:, pl.ds((1 - c) * HALF, HALF), :]
        local = pltpu.make_async_copy(h_ref, mine, local_sem)
        local.start()
        cp = pltpu.make_async_remote_copy(
            src_ref=h_ref, dst_ref=mine, send_sem=send_sem, recv_sem=recv_sem,
            device_id=(x, y, 1 - c), device_id_type=_MESH,
        )
        cp.start()
        pltpu.make_async_remote_copy(
            src_ref=h_ref, dst_ref=theirs, send_sem=send_sem, recv_sem=recv_sem,
            device_id=(x, y, 1 - c), device_id_type=_MESH,
        ).wait_recv()
        cp.wait_send()
        local.wait()

    return pl.pallas_call(
        body,
        name="grads_pair_share",
        out_shape=_sds((N_PIECES, F_SHARD, D_MODEL), F32),
        in_specs=[pl.BlockSpec(memory_space=pl.ANY)],
        out_specs=pl.BlockSpec(memory_space=pl.ANY),
        scratch_shapes=[pltpu.SemaphoreType.DMA, pltpu.SemaphoreType.DMA, pltpu.SemaphoreType.DMA],
    )(r_half)


def _allreduce_small(blk):
    gathered = _allgather_small("allgather_small_grads", blk).reshape(N_DEV, PK_ROWS, 128)

    def body(g_ref, o_ref):
        acc = g_ref[0]
        for k in range(1, N_DEV):
            acc = acc + g_ref[k]
        o_ref[...] = acc

    total = pl.pallas_call(body, name="small_grads_sum", out_shape=_sds((PK_ROWS, 128), F32))(gathered)
    return gathered, total


def _adamw(name, w, g, m, v):
    rows, cols = w.shape
    tm = rows
    while tm * cols * 4 > (1 << 20) and tm % 16 == 0:
        tm //= 2

    def fn(w_, g_, m_, v_):
        m_new = ADAM_B1 * m_ + (1.0 - ADAM_B1) * g_
        v_new = ADAM_B2 * v_ + (1.0 - ADAM_B2) * (g_ * g_)
        m_hat = m_new / (1.0 - ADAM_B1 ** ADAM_STEP)
        v_hat = v_new / (1.0 - ADAM_B2 ** ADAM_STEP)
        delta = -ADAM_LR * (m_hat / (jnp.sqrt(v_hat) + ADAM_EPS) + ADAM_WD * w_)
        return delta, m_new, v_new

    out = _sds(w.shape, F32)
    return _rowwise(name, fn, [w, g, m, v], [], [out, out, out], [], tm)


def _pack_small(b_mod_like, n1, n2, n3, nf, qn, kvn, sinks, rel):
    parts = [
        b_mod_like.reshape(PK_MOD, 128),
        n1.reshape(8, 128), n2.reshape(8, 128), n3.reshape(8, 128), nf.reshape(8, 128),
        qn.reshape(2, 128), kvn.reshape(1, 128),
        jnp.pad(sinks.reshape(1, SWA_HEADS), ((0, 0), (0, 128 - SWA_HEADS))),
        rel.reshape(2, 128),
        jnp.zeros((2, 128), F32),
    ]
    return jnp.concatenate(parts, axis=0)


def _unpack_small(p):
    o = PK_MOD
    return (
        p[:o].reshape(1, N_MOD * D_MODEL),
        p[o:o + 8].reshape(1, D_MODEL), p[o + 8:o + 16].reshape(1, D_MODEL),
        p[o + 16:o + 24].reshape(1, D_MODEL), p[o + 24:o + 32].reshape(D_MODEL),
        p[o + 32:o + 34].reshape(1, MLA_Q_RANK), p[o + 34:o + 35].reshape(1, MLA_KV_RANK),
        p[o + 35:o + 36, :SWA_HEADS].reshape(1, SWA_HEADS),
        p[o + 36:o + 38].reshape(NUM_BUCKETS, SWA_HEADS),
    )


def kernel(x, c, w_mod, b_mod, norm_ffn1, ffn1_gate, ffn1_up, ffn1_down, norm_mix, w_in, q_norm, kv_norm, w_uq, w_ukv, sinks, w_o, norm_ffn2, ffn2_gate, ffn2_up, ffn2_down, rel_bias, norm_final, loss_target, m_w_mod, m_b_mod, m_norm_ffn1, m_ffn1_gate, m_ffn1_up, m_ffn1_down, m_norm_mix, m_w_in, m_q_norm, m_kv_norm, m_w_uq, m_w_ukv, m_sinks, m_w_o, m_norm_ffn2, m_ffn2_gate, m_ffn2_up, m_ffn2_down, m_rel_bias, m_norm_final, v_w_mod, v_b_mod, v_norm_ffn1, v_ffn1_gate, v_ffn1_up, v_ffn1_down, v_norm_mix, v_w_in, v_q_norm, v_kv_norm, v_w_uq, v_w_ukv, v_sinks, v_w_o, v_norm_ffn2, v_ffn2_gate, v_ffn2_up, v_ffn2_down, v_rel_bias, v_norm_final):
    ax, ay, ac = _place()
    chip = 2 * ax + ay
    dev = 2 * chip + ac
    n_mod_shard = N_MOD * D_MODEL // N_CHIPS

    def t16(w):
        return w[0].T.astype(BF16)

    rest = jnp.concatenate(
        [
            t16(w_in),
            w_o[0].astype(BF16),
            t16(w_uq).reshape(R_UQ, D_MODEL),
            t16(w_ukv).reshape(R_UKV, D_MODEL),
            jnp.zeros((F_SHARD - O_PAD, D_MODEL), BF16),
        ],
        axis=0,
    )
    own = jnp.stack(
        [t16(ffn1_gate), t16(ffn1_up), ffn1_down[0].astype(BF16), t16(ffn2_gate), t16(ffn2_up), ffn2_down[0].astype(BF16), rest]
    )
    g4 = _allgather_weights(own)

    (c_act,) = _rowwise(
        "silu_c", lambda v: v * _sigmoid(v), [c.reshape(8, 128)], [], [_sds((8, 128), F32)], [], 8
    )
    c_all = _allgather_small("allgather_c", c_act).reshape(N_DEV, D_MODEL)
    mod_cols = _mm(
        "mod_fwd", "nn", (1, n_mod_shard // 768, 1),
        c_all, (N_DEV, D_MODEL), lambda i, j, k: (0, 0),
        w_mod[0], (D_MODEL, 768), lambda i, j, k: (0, j),
        _sds((N_DEV, n_mod_shard), F32), (N_DEV, 768), lambda i, j, k: (0, j),
        (N_DEV, 768),
    )
    mod_all = _allgather_small("allgather_mod", mod_cols).reshape(N_CHIPS, 2, N_DEV, n_mod_shard)[:, 0]
    mod_all = mod_all.transpose(1, 0, 2).reshape(N_DEV, N_MOD * D_MODEL)
    mod = lax.dynamic_slice_in_dim(mod_all, dev, 1, axis=0) + b_mod

    norms = (norm_ffn1, norm_mix, norm_ffn2, norm_final.reshape(1, D_MODEL))
    loss_part, grad_x, gb, dmod, small = _device_step(
        x[0], loss_target[0], mod, g4, norms, q_norm, kv_norm, sinks, rel_bias
    )
    loss = lax.psum(loss_part, ("x", "y", "c"))

    gathered, total = _allreduce_small(_pack_small(dmod, *small))
    (g_b_mod, g_n1, g_n2, g_n3, g_nf, g_qn, g_kvn, g_sinks, g_rel) = _unpack_small(total)
    dmod_all = gathered[:, :PK_MOD].reshape(N_DEV, N_MOD * D_MODEL)
    dmod_cols = lax.dynamic_slice_in_dim(dmod_all, chip * n_mod_shard, n_mod_shard, axis=1)
    g_w_mod = _mm(
        "mod_bwd", "tn", (1, n_mod_shard // 768, 1),
        c_all, (N_DEV, D_MODEL), lambda i, j, k: (0, 0),
        dmod_cols, (N_DEV, 768), lambda i, j, k: (0, j),
        _sds((D_MODEL, n_mod_shard), F32), (D_MODEL, 768), lambda i, j, k: (0, j),
        (D_MODEL, 768),
    )

    part = _pair_sum(gb, _pair_exchange(gb))
    red = _pair_share(_chip_sum(_chip_exchange(part)))
    r_rest = red[P_REST]
    g_big = {
        "ffn1_gate": red[P_GATE1].T, "ffn1_up": red[P_UP1].T, "ffn1_down": red[P_DOWN1],
        "ffn2_gate": red[P_GATE2].T, "ffn2_up": red[P_UP2].T, "ffn2_down": red[P_DOWN2],
        "w_in": r_rest[O_IN:O_IN + R_IN].T,
        "w_o": r_rest[O_O:O_O + R_O],
        "w_uq": r_rest[O_UQ:O_UQ + R_UQ].reshape(MLA_QK, MLA_Q_RANK).T,
        "w_ukv": r_rest[O_UKV:O_UKV + R_UKV].reshape(MLA_NOPE + MLA_V, MLA_KV_RANK).T,
        "w_mod": g_w_mod,
    }
    w_big = {
        "ffn1_gate": (ffn1_gate, m_ffn1_gate, v_ffn1_gate), "ffn1_up": (ffn1_up, m_ffn1_up, v_ffn1_up),
        "ffn1_down": (ffn1_down, m_ffn1_down, v_ffn1_down), "ffn2_gate": (ffn2_gate, m_ffn2_gate, v_ffn2_gate),
        "ffn2_up": (ffn2_up, m_ffn2_up, v_ffn2_up), "ffn2_down": (ffn2_down, m_ffn2_down, v_ffn2_down),
        "w_in": (w_in, m_w_in, v_w_in), "w_o": (w_o, m_w_o, v_w_o), "w_uq": (w_uq, m_w_uq, v_w_uq),
        "w_ukv": (w_ukv, m_w_ukv, v_w_ukv), "w_mod": (w_mod, m_w_mod, v_w_mod),
    }
    grads, deltas, new_m, new_v = {}, {}, {}, {}
    for nm, (w, m, v) in w_big.items():
        d_, m_, v_ = _adamw(f"adamw_{nm}", w[0], g_big[nm], m[0], v[0])
        grads[nm], deltas[nm], new_m[nm], new_v[nm] = g_big[nm][None], d_[None], m_[None], v_[None]

    small_names = ["b_mod", "norm_ffn1", "norm_mix", "norm_ffn2", "norm_final", "q_norm", "kv_norm", "sinks", "rel_bias"]
    w_small = (b_mod, norm_ffn1, norm_mix, norm_ffn2, norm_final, q_norm, kv_norm, sinks, rel_bias)
    m_small = (m_b_mod, m_norm_ffn1, m_norm_mix, m_norm_ffn2, m_norm_final, m_q_norm, m_kv_norm, m_sinks, m_rel_bias)
    v_small = (v_b_mod, v_norm_ffn1, v_norm_mix, v_norm_ffn2, v_norm_final, v_q_norm, v_kv_norm, v_sinks, v_rel_bias)
    d_p, m_p, v_p = _adamw("adamw_small", _pack_small(*w_small), total, _pack_small(*m_small), _pack_small(*v_small))
    for nm, g_, d_, m_, v_ in zip(
        small_names,
        (g_b_mod, g_n1, g_n2, g_n3, g_nf, g_qn, g_kvn, g_sinks, g_rel),
        _unpack_small(d_p), _unpack_small(m_p), _unpack_small(v_p),
    ):
        grads[nm], deltas[nm], new_m[nm], new_v[nm] = g_, d_, m_, v_

    order = ["w_mod", "b_mod", "norm_ffn1", "ffn1_gate", "ffn1_up", "ffn1_down", "norm_mix", "w_in", "q_norm", "kv_norm",
             "w_uq", "w_ukv", "sinks", "w_o", "norm_ffn2", "ffn2_gate", "ffn2_up", "ffn2_down", "rel_bias", "norm_final"]
    return (loss, grad_x[None], *[grads[n] for n in order], *[deltas[n] for n in order],
            *[new_m[n] for n in order], *[new_v[n] for n in order])
```

```python
import functools
import math

import jax
import jax.numpy as jnp
import numpy as np
from jax import lax
from jax.experimental import pallas as pl
from jax.experimental.pallas import tpu as pltpu

F32, BF16 = jnp.float32, jnp.bfloat16

D_MODEL = 1024
D_FF = 2816
EPS = 1e-6
N_MOD = 9
SWA_HEADS, SWA_KV_HEADS, SWA_HEAD_DIM, WINDOW = 8, 2, 64, 128
SWA_GROUP = SWA_HEADS // SWA_KV_HEADS
MLA_HEADS, MLA_Q_RANK, MLA_KV_RANK, MLA_NOPE, MLA_ROPE, MLA_V = 4, 256, 128, 128, 64, 128
MLA_QK = MLA_NOPE + MLA_ROPE
ROPE_THETA = 10000.0
NUM_BUCKETS, MAX_DISTANCE = 32, 128
D_IN = 1216
N_SWA_COLS = 768
N_SWA_BLOCKS = N_SWA_COLS // SWA_HEAD_DIM
N_MLA_COLS = 512

ADAM_LR, ADAM_B1, ADAM_B2, ADAM_EPS, ADAM_WD, ADAM_STEP = 0.001, 0.9, 0.999, 1e-08, 0.01, 10

N_CHIPS = 4
N_DEV = 8
F_SHARD = D_FF // N_CHIPS
HALF = F_SHARD // 2
R_IN, R_O, R_UQ, R_UKV = 304, 256, 48, 32
O_IN, O_O, O_UQ, O_UKV, O_PAD = 0, 304, 560, 608, 640

PK_MOD = 72
PK_ROWS = 112
VMEM_LIMIT = 56 << 20
ROW_TILE = 512

_DN = {
    "nn": (((1,), (0,)), ((), ())),
    "nt": (((1,), (1,)), ((), ())),
    "tn": (((0,), (0,)), ((), ())),
}


def _sds(shape, dtype):
    return jax.ShapeDtypeStruct(tuple(shape), dtype)


def _cparams(sem=None):
    kw = {"vmem_limit_bytes": VMEM_LIMIT}
    if sem is not None:
        kw["dimension_semantics"] = sem
    return pltpu.CompilerParams(**kw)


def _dot(a, b, dims):
    return lax.dot_general(a.astype(BF16), b.astype(BF16), _DN[dims], preferred_element_type=F32)


def _mm(name, dims, grid, a, a_blk, a_idx, b, b_blk, b_idx, out, out_blk, out_idx, acc_shape, alias=None):
    nk = grid[2]

    def body(*refs):
        if alias is None:
            a_ref, b_ref, o_ref, acc_ref = refs
        else:
            a_ref, b_ref, _, o_ref, acc_ref = refs
        k = pl.program_id(2)

        @pl.when(k == 0)
        def _():
            acc_ref[...] = jnp.zeros_like(acc_ref)

        acc_ref[...] += _dot(a_ref[...], b_ref[...], dims)

        @pl.when(k == nk - 1)
        def _():
            o_ref[...] = acc_ref[...].astype(o_ref.dtype)

    in_specs = [pl.BlockSpec(a_blk, a_idx), pl.BlockSpec(b_blk, b_idx)]
    args = [a, b]
    kw = {}
    if alias is not None:
        in_specs.append(pl.BlockSpec(memory_space=pl.ANY))
        args.append(alias)
        kw["input_output_aliases"] = {2: 0}
    return pl.pallas_call(
        body,
        name=name,
        grid=grid,
        in_specs=in_specs,
        out_specs=pl.BlockSpec(out_blk, out_idx),
        out_shape=out,
        scratch_shapes=[pltpu.VMEM(acc_shape, F32)],
        compiler_params=_cparams(("parallel", "parallel", "arbitrary")),
        **kw,
    )(*args)


def _rowwise(name, fn, tiled, full, out_tiled, out_red, tm):
    rows = tiled[0].shape[-2]
    grid = (rows // tm,)
    n_in = len(tiled) + len(full)
    n_t = len(out_tiled)

    def tspec(shape):
        nd = len(shape)
        return pl.BlockSpec(tuple(shape[:-2]) + (tm, shape[-1]), lambda i, nd=nd: (0,) * (nd - 2) + (i, 0))

    def fspec(shape):
        nd = len(shape)
        return pl.BlockSpec(tuple(shape), lambda i, nd=nd: (0,) * nd)

    def body(*refs):
        outs = fn(*[r[...] for r in refs[:n_in]])
        if not isinstance(outs, (tuple, list)):
            outs = (outs,)
        for r, v in zip(refs[n_in:n_in + n_t], outs[:n_t]):
            r[...] = v.astype(r.dtype)
        red_refs = refs[n_in + n_t:]
        if red_refs:
            @pl.when(pl.program_id(0) == 0)
            def _():
                for r in red_refs:
                    r[...] = jnp.zeros_like(r)

            for r, v in zip(red_refs, outs[n_t:]):
                r[...] += v.astype(r.dtype)

    res = pl.pallas_call(
        body,
        name=name,
        grid=grid,
        in_specs=[tspec(a.shape) for a in tiled] + [fspec(a.shape) for a in full],
        out_specs=[tspec(o.shape) for o in out_tiled] + [fspec(o.shape) for o in out_red],
        out_shape=list(out_tiled) + list(out_red),
        compiler_params=_cparams(("arbitrary",) if out_red else ("parallel",)),
    )(*tiled, *full)
    return res


def _sum0(v):
    return jnp.sum(v, axis=0, keepdims=True)


def _sigmoid(v):
    return 1.0 / (1.0 + jnp.exp(-v))


def _rms(x):
    r = lax.rsqrt(jnp.mean(x * x, axis=-1, keepdims=True) + EPS)
    return x * r, r


def _rms_bwd(u, xr, r):
    return r * (u - xr * jnp.mean(u * xr, axis=-1, keepdims=True))


def _ffn_fwd(tag, x, gamma, sh, sc, gate, g4, base):
    s_len = x.shape[0]

    def normmod(x_, gamma_, sc_, sh_):
        xr, _ = _rms(x_)
        return xr * gamma_ * (1.0 + sc_) + sh_

    (h,) = _rowwise(f"{tag}_normmod", normmod, [x], [gamma, sc, sh], [_sds((s_len, D_MODEL), BF16)], [], 256)

    tm = min(ROW_TILE, s_len)
    ab3 = _mm(
        f"{tag}_gate_up", "nt", (s_len // tm, 2 * N_CHIPS, 1),
        h, (tm, D_MODEL), lambda i, j, k: (i, 0),
        g4, (None, None, F_SHARD, D_MODEL), lambda i, j, k: (j % N_CHIPS, base + j // N_CHIPS, 0, 0),
        _sds((2 * N_CHIPS, s_len, F_SHARD), BF16), (None, tm, F_SHARD), lambda i, j, k: (j, i, 0),
        (tm, F_SHARD),
    )

    def swiglu(ab):
        a = ab[:N_CHIPS].astype(F32)
        b = ab[N_CHIPS:].astype(F32)
        return a * _sigmoid(a) * b

    (s3,) = _rowwise(f"{tag}_swiglu", swiglu, [ab3], [], [_sds((N_CHIPS, s_len, F_SHARD), BF16)], [], 128)

    y = _mm(
        f"{tag}_down", "nn", (s_len // tm, 1, N_CHIPS),
        s3, (None, tm, F_SHARD), lambda i, j, k: (k, i, 0),
        g4, (None, None, F_SHARD, D_MODEL), lambda i, j, k: (k, base + 2, 0, 0),
        _sds((s_len, D_MODEL), F32), (tm, D_MODEL), lambda i, j, k: (i, 0),
        (tm, D_MODEL),
    )

    (x_out,) = _rowwise(
        f"{tag}_residual", lambda x_, y_, g_: x_ + 0.5 * g_ * y_, [x, y], [gate], [_sds((s_len, D_MODEL), F32)], [], 256
    )
    return x_out, (h, ab3, s3, y)


def _normmod_bwd_call(name, dh_parts, x, dxo, gamma, sc):
    s_len = x.shape[0]
    n_parts = len(dh_parts)

    def fn(*vals):
        dh = vals[0]
        for extra in vals[1:n_parts]:
            dh = dh + extra
        x_, dxo_, gamma_, sc_ = vals[n_parts:]
        xr, r = _rms(x_)
        n = xr * gamma_
        dn = dh * (1.0 + sc_)
        dx = dxo_ + _rms_bwd(dn * gamma_, xr, r)
        return dx, _sum0(dh * n), _sum0(dh), _sum0(dn * xr)

    vec = _sds((1, D_MODEL), F32)
    return _rowwise(name, fn, list(dh_parts) + [x, dxo], [gamma, sc], [_sds((s_len, D_MODEL), F32)], [vec, vec, vec], 256)


def _ffn_bwd(tag, dxo, x, gamma, sc, gate, g4, base, saved):
    h, ab3, s3, y = saved
    s_len = x.shape[0]
    vec = _sds((1, D_MODEL), F32)

    dy, dgate = _rowwise(
        f"{tag}_bwd_gate", lambda dxo_, y_, g_: (0.5 * g_ * dxo_, _sum0(0.5 * y_ * dxo_)),
        [dxo, y], [gate], [_sds((s_len, D_MODEL), BF16)], [vec], 256,
    )

    tm = min(ROW_TILE, s_len)
    ds3 = _mm(
        f"{tag}_bwd_ds", "nt", (s_len // tm, N_CHIPS, 1),
        dy, (tm, D_MODEL), lambda i, j, k: (i, 0),
        g4, (None, None, F_SHARD, D_MODEL), lambda i, j, k: (j, base + 2, 0, 0),
        _sds((N_CHIPS, s_len, F_SHARD), BF16), (None, tm, F_SHARD), lambda i, j, k: (j, i, 0),
        (tm, F_SHARD),
    )

    def swiglu_bwd(ab, ds):
        a = ab[:N_CHIPS].astype(F32)
        b = ab[N_CHIPS:].astype(F32)
        ds = ds.astype(F32)
        sig = _sigmoid(a)
        da = ds * b * sig * (1.0 + a * (1.0 - sig))
        db = ds * a * sig
        return jnp.concatenate([da, db], axis=0)

    (dab3,) = _rowwise(
        f"{tag}_bwd_swiglu", swiglu_bwd, [ab3, ds3], [], [_sds((2 * N_CHIPS, s_len, F_SHARD), BF16)], [], 128
    )

    tk = tm
    gb = _mm(
        f"{tag}_bwd_wdown", "tn", (N_CHIPS, 1, s_len // tk),
        s3, (None, tk, F_SHARD), lambda i, j, k: (i, k, 0),
        dy, (tk, D_MODEL), lambda i, j, k: (k, 0),
        _sds((N_CHIPS, 3, F_SHARD, D_MODEL), BF16), (None, None, F_SHARD, D_MODEL), lambda i, j, k: (i, 2, 0, 0),
        (F_SHARD, D_MODEL),
    )
    gb = _mm(
        f"{tag}_bwd_wgu", "tn", (2 * N_CHIPS, 1, s_len // tk),
        dab3, (None, tk, F_SHARD), lambda i, j, k: (i, k, 0),
        h, (tk, D_MODEL), lambda i, j, k: (k, 0),
        _sds(gb.shape, BF16), (None, None, F_SHARD, D_MODEL), lambda i, j, k: (i % N_CHIPS, i // N_CHIPS, 0, 0),
        (F_SHARD, D_MODEL), alias=gb,
    )
    dh = _mm(
        f"{tag}_bwd_dh", "nn", (s_len // tm, 1, 2 * N_CHIPS),
        dab3, (None, tm, F_SHARD), lambda i, j, k: (k, i, 0),
        g4, (None, None, F_SHARD, D_MODEL), lambda i, j, k: (k % N_CHIPS, base + k // N_CHIPS, 0, 0),
        _sds((s_len, D_MODEL), F32), (tm, D_MODEL), lambda i, j, k: (i, 0),
        (tm, D_MODEL),
    )
    dx, dsc, dsh, dgamma = _normmod_bwd_call(f"{tag}_bwd_normmod", [dh], x, dxo, gamma, sc)
    return dx, gb, (dsh, dsc, dgate, dgamma)


def _bucket_map():
    qi = np.arange(WINDOW)[:, None]
    kj = np.arange(2 * WINDOW)[None, :]
    dist = qi + WINDOW - kj
    band = (dist >= 0) & (dist < WINDOW)
    max_exact = NUM_BUCKETS // 2
    n = np.maximum(dist, 0)
    large = []
    for dt in (np.float32, np.float64):
        nf = np.maximum(n, 1).astype(dt)
        val = np.log(nf / dt(max_exact)) / dt(math.log(MAX_DISTANCE / max_exact)) * dt(NUM_BUCKETS - max_exact)
        large.append(np.minimum(max_exact + val.astype(np.int32), NUM_BUCKETS - 1))
    assert np.array_equal(large[0], large[1])
    bucket = np.where(n < max_exact, n, large[0])
    return np.where(band, bucket, -1).astype(np.int32).reshape(1, -1)


def _bias_expand(rel_bias_t, bkt):
    n = bkt.shape[1]

    def body(rb_ref, bkt_ref, o_ref):
        onehot = (lax.broadcasted_iota(jnp.int32, (NUM_BUCKETS, n), 0) == bkt_ref[...]).astype(F32)
        o_ref[...] = lax.dot_general(
            rb_ref[...], onehot, _DN["nn"], precision=lax.Precision.HIGHEST, preferred_element_type=F32
        )

    return pl.pallas_call(
        body, name="bias_expand", out_shape=_sds((SWA_HEADS, n), F32), compiler_params=_cparams()
    )(rel_bias_t, bkt)


def _bias_reduce(dbias_flat, bkt):
    n = bkt.shape[1]

    def body(db_ref, bkt_ref, o_ref):
        onehot = (lax.broadcasted_iota(jnp.int32, (NUM_BUCKETS, n), 0) == bkt_ref[...]).astype(F32)
        o_ref[...] = lax.dot_general(
            db_ref[...], onehot, _DN["nt"], precision=lax.Precision.HIGHEST, preferred_element_type=F32
        )

    return pl.pallas_call(
        body, name="bias_reduce", out_shape=_sds((SWA_HEADS, NUM_BUCKETS), F32), compiler_params=_cparams()
    )(dbias_flat, bkt)


_SWA_SCALE = SWA_HEAD_DIM ** -0.5
_NEG = -1e30


def _swa_in_specs():
    g, w, hd = SWA_GROUP, WINDOW, SWA_HEAD_DIM
    prev = lambda n: jnp.maximum(n - 1, 0)
    return [
        pl.BlockSpec((g, w, hd), lambda j, n: (j, n, 0)),
        pl.BlockSpec((1, w, hd), lambda j, n: (SWA_HEADS + j, prev(n), 0)),
        pl.BlockSpec((1, w, hd), lambda j, n: (SWA_HEADS + j, n, 0)),
        pl.BlockSpec((1, w, hd), lambda j, n: (SWA_HEADS + SWA_KV_HEADS + j, prev(n), 0)),
        pl.BlockSpec((1, w, hd), lambda j, n: (SWA_HEADS + SWA_KV_HEADS + j, n, 0)),
        pl.BlockSpec((g, w, 2 * w), lambda j, n: (j, 0, 0)),
        pl.BlockSpec((g, w, 1), lambda j, n: (j, 0, 0)),
    ]


def _swa_scores(q_ref, kp_ref, kc_ref, bias_ref, n):
    g, w = SWA_GROUP, WINDOW
    q = q_ref[...].reshape(g * w, SWA_HEAD_DIM)
    kk = jnp.concatenate([kp_ref[0], kc_ref[0]], axis=0)
    s = (_dot(q, kk, "nt") * _SWA_SCALE).reshape(g, w, 2 * w) + bias_ref[...]
    qi = lax.broadcasted_iota(jnp.int32, (w, 2 * w), 0)
    kj = lax.broadcasted_iota(jnp.int32, (w, 2 * w), 1)
    dist = qi + w - kj
    valid = (dist >= 0) & (dist < w) & ((n > 0) | (kj >= w))
    return q, kk, jnp.where(valid[None], s, _NEG), valid[None]


def _swa_fwd(swa3, bias, sinkb):
    s_len = swa3.shape[1]
    g, w, hd = SWA_GROUP, WINDOW, SWA_HEAD_DIM

    def body(q_ref, kp_ref, kc_ref, vp_ref, vc_ref, bias_ref, sink_ref, o_ref, lse_ref):
        n = pl.program_id(1)
        _, _, s, valid = _swa_scores(q_ref, kp_ref, kc_ref, bias_ref, n)
        vv = jnp.concatenate([vp_ref[0], vc_ref[0]], axis=0)
        sink = sink_ref[...]
        m = jnp.maximum(jnp.max(s, axis=-1, keepdims=True), sink)
        p = jnp.where(valid, jnp.exp(s - m), 0.0)
        l = jnp.sum(p, axis=-1, keepdims=True) + jnp.exp(sink - m)
        o = _dot(p.reshape(g * w, 2 * w), vv, "nn").reshape(g, w, hd)
        o_ref[...] = (o / l).astype(o_ref.dtype)
        lse_ref[...] = m + jnp.log(l)

    return pl.pallas_call(
        body,
        name="swa_fwd",
        grid=(SWA_KV_HEADS, s_len // w),
        in_specs=_swa_in_specs(),
        out_specs=[
            pl.BlockSpec((g, w, hd), lambda j, n: (j, n, 0)),
            pl.BlockSpec((g, w, 1), lambda j, n: (j, n, 0)),
        ],
        out_shape=[_sds((SWA_HEADS, s_len, hd), BF16), _sds((SWA_HEADS, s_len, 1), F32)],
        compiler_params=_cparams(("parallel", "parallel")),
    )(swa3, swa3, swa3, swa3, swa3, bias, sinkb)


def _swa_bwd(swa3, bias, sinkb, o3, do3, lse):
    s_len = swa3.shape[1]
    g, w, hd = SWA_GROUP, WINDOW, SWA_HEAD_DIM

    def body(q_ref, kp_ref, kc_ref, vp_ref, vc_ref, bias_ref, sink_ref, o_ref, do_ref, lse_ref,
             dq_ref, dka_ref, dkb_ref, dva_ref, dvb_ref, dbias_ref, dsink_ref):
        n = pl.program_id(1)

        @pl.when(n == 0)
        def _():
            dbias_ref[...] = jnp.zeros_like(dbias_ref)
            dsink_ref[...] = jnp.zeros_like(dsink_ref)

        q, kk, s, valid = _swa_scores(q_ref, kp_ref, kc_ref, bias_ref, n)
        vv = jnp.concatenate([vp_ref[0], vc_ref[0]], axis=0)
        lse_v = lse_ref[...]
        p = jnp.where(valid, jnp.exp(s - lse_v), 0.0)
        do = do_ref[...].astype(F32)
        delta = jnp.sum(do * o_ref[...].astype(F32), axis=-1, keepdims=True)
        do2 = do.reshape(g * w, hd)
        dp = _dot(do2, vv, "nt").reshape(g, w, 2 * w)
        ds = p * (dp - delta)
        dbias_ref[...] += ds
        dsink_ref[...] -= jnp.exp(sink_ref[...] - lse_v) * delta
        ds2 = ds.reshape(g * w, 2 * w)
        dq_ref[...] = (_dot(ds2, kk, "nn") * _SWA_SCALE).reshape(g, w, hd).astype(dq_ref.dtype)
        dkk = _dot(ds2, q, "tn") * _SWA_SCALE
        dvv = _dot(p.reshape(g * w, 2 * w), do2, "tn")
        dkb_ref[0] = dkk[:w]
        dka_ref[0] = dkk[w:]
        dvb_ref[0] = dvv[:w]
        dva_ref[0] = dvv[w:]

    kv_spec = pl.BlockSpec((1, w, hd), lambda j, n: (j, n, 0))
    kv_sds = _sds((SWA_KV_HEADS, s_len, hd), F32)
    return pl.pallas_call(
        body,
        name="swa_bwd",
        grid=(SWA_KV_HEADS, s_len // w),
        in_specs=_swa_in_specs() + [
            pl.BlockSpec((g, w, hd), lambda j, n: (j, n, 0)),
            pl.BlockSpec((g, w, hd), lambda j, n: (j, n, 0)),
            pl.BlockSpec((g, w, 1), lambda j, n: (j, n, 0)),
        ],
        out_specs=[
            pl.BlockSpec((g, w, hd), lambda j, n: (j, n, 0)),
            kv_spec, kv_spec, kv_spec, kv_spec,
            pl.BlockSpec((g, w, 2 * w), lambda j, n: (j, 0, 0)),
            pl.BlockSpec((g, w, 1), lambda j, n: (j, 0, 0)),
        ],
        out_shape=[
            _sds((SWA_HEADS, s_len, hd), BF16), kv_sds, kv_sds, kv_sds, kv_sds,
            _sds((SWA_HEADS, w, 2 * w), F32), _sds((SWA_HEADS, w, 1), F32),
        ],
        compiler_params=_cparams(("parallel", "arbitrary")),
    )(swa3, swa3, swa3, swa3, swa3, bias, sinkb, o3, do3, lse)


_MLA_SCALE = MLA_QK ** -0.5
_MLA_TQ = 256


def _mla_mask(i, tq, s_len):
    row = i * tq + lax.broadcasted_iota(jnp.int32, (tq, s_len), 0)
    col = lax.broadcasted_iota(jnp.int32, (tq, s_len), 1)
    return col <= row


def _mla_fwd(q4, k4, v4):
    s_len = q4.shape[1]
    tq = min(_MLA_TQ, s_len)

    def body(q_ref, k_ref, v_ref, o_ref, lse_ref):
        mask = _mla_mask(pl.program_id(1), tq, s_len)
        s = jnp.where(mask, _dot(q_ref[...], k_ref[...], "nt") * _MLA_SCALE, _NEG)
        m = jnp.max(s, axis=-1, keepdims=True)
        p = jnp.exp(s - m)
        l = jnp.sum(p, axis=-1, keepdims=True)
        o_ref[...] = (_dot(p, v_ref[...], "nn") / l).astype(o_ref.dtype)
        lse_ref[...] = m + jnp.log(l)

    return pl.pallas_call(
        body,
        name="mla_fwd",
        grid=(MLA_HEADS, s_len // tq),
        in_specs=[
            pl.BlockSpec((None, tq, MLA_QK), lambda h, i: (h, i, 0)),
            pl.BlockSpec((None, s_len, MLA_QK), lambda h, i: (h, 0, 0)),
            pl.BlockSpec((None, s_len, MLA_V), lambda h, i: (h, 0, 0)),
        ],
        out_specs=[
            pl.BlockSpec((None, tq, MLA_V), lambda h, i: (h, i, 0)),
            pl.BlockSpec((None, tq, 1), lambda h, i: (h, i, 0)),
        ],
        out_shape=[_sds((MLA_HEADS, s_len, MLA_V), BF16), _sds((MLA_HEADS, s_len, 1), F32)],
        compiler_params=_cparams(("parallel", "parallel")),
    )(q4, k4, v4)


def _mla_bwd(q4, k4, v4, o4, do4, lse):
    s_len = q4.shape[1]
    tq = min(_MLA_TQ, s_len)

    def body(q_ref, k_ref, v_ref, o_ref, do_ref, lse_ref, dq_ref, dk_ref, dv_ref):
        i = pl.program_id(1)

        @pl.when(i == 0)
        def _():
            dk_ref[...] = jnp.zeros_like(dk_ref)
            dv_ref[...] = jnp.zeros_like(dv_ref)

        mask = _mla_mask(i, tq, s_len)
        q, k, v = q_ref[...], k_ref[...], v_ref[...]
        s = jnp.where(mask, _dot(q, k, "nt") * _MLA_SCALE, _NEG)
        p = jnp.where(mask, jnp.exp(s - lse_ref[...]), 0.0)
        do = do_ref[...]
        delta = jnp.sum(do.astype(F32) * o_ref[...].astype(F32), axis=-1, keepdims=True)
        dp = _dot(do, v, "nt")
        ds = (p * (dp - delta) * _MLA_SCALE).astype(BF16)
        dq_ref[...] = _dot(ds, k, "nn")
        dk_ref[...] += _dot(ds, q, "tn")
        dv_ref[...] += _dot(p, do, "tn")

    return pl.pallas_call(
        body,
        name="mla_bwd",
        grid=(MLA_HEADS, s_len // tq),
        in_specs=[
            pl.BlockSpec((None, tq, MLA_QK), lambda h, i: (h, i, 0)),
            pl.BlockSpec((None, s_len, MLA_QK), lambda h, i: (h, 0, 0)),
            pl.BlockSpec((None, s_len, MLA_V), lambda h, i: (h, 0, 0)),
            pl.BlockSpec((None, tq, MLA_V), lambda h, i: (h, i, 0)),
            pl.BlockSpec((None, tq, MLA_V), lambda h, i: (h, i, 0)),
            pl.BlockSpec((None, tq, 1), lambda h, i: (h, i, 0)),
        ],
        out_specs=[
            pl.BlockSpec((None, tq, MLA_QK), lambda h, i: (h, i, 0)),
            pl.BlockSpec((None, s_len, MLA_QK), lambda h, i: (h, 0, 0)),
            pl.BlockSpec((None, s_len, MLA_V), lambda h, i: (h, 0, 0)),
        ],
        out_shape=[
            _sds((MLA_HEADS, s_len, MLA_QK), F32),
            _sds((MLA_HEADS, s_len, MLA_QK), F32),
            _sds((MLA_HEADS, s_len, MLA_V), F32),
        ],
        compiler_params=_cparams(("parallel", "arbitrary")),
    )(q4, k4, v4, o4, do4, lse)


def _mla_split(pm):
    q_lat = pm[:, :MLA_Q_RANK]
    kv_lat = pm[:, MLA_Q_RANK:MLA_Q_RANK + MLA_KV_RANK]
    kr = pm[:, MLA_Q_RANK + MLA_KV_RANK:MLA_Q_RANK + MLA_KV_RANK + MLA_ROPE]
    kr_sw = pm[:, MLA_Q_RANK + MLA_KV_RANK + MLA_ROPE:]
    return q_lat, kv_lat, kr, kr_sw


def _mla_proj(pm, cos2, sin2, q_norm, kv_norm, wq_ext, wkv):
    s_len = pm.shape[0]

    def fn(pm_, cos_, sin_, qg, kvg, wq, wk):
        q_lat, kv_lat, kr, kr_sw = _mla_split(pm_)
        nq = (_rms(q_lat)[0] * qg).astype(BF16)
        nkv = (_rms(kv_lat)[0] * kvg).astype(BF16)
        k_rot = kr * cos_ + kr_sw * sin_
        qs, ks, vs = [], [], []
        for h in range(MLA_HEADS):
            qe = _dot(nq, wq[h], "nt")
            q_rot = qe[:, MLA_NOPE:MLA_QK] * cos_ + qe[:, MLA_QK:] * sin_
            qs.append(jnp.concatenate([qe[:, :MLA_NOPE], q_rot], axis=-1))
            kve = _dot(nkv, wk[h], "nt")
            ks.append(jnp.concatenate([kve[:, :MLA_NOPE], k_rot], axis=-1))
            vs.append(kve[:, MLA_NOPE:])
        return jnp.stack(qs), jnp.stack(ks), jnp.stack(vs)

    return _rowwise(
        "mla_proj", fn, [pm, cos2, sin2], [q_norm, kv_norm, wq_ext, wkv],
        [_sds((MLA_HEADS, s_len, MLA_QK), BF16), _sds((MLA_HEADS, s_len, MLA_QK), BF16),
         _sds((MLA_HEADS, s_len, MLA_V), BF16)], [], 256,
    )


def _mla_proj_bwd(pm, cos2, sin2, dq4, dk4, dv4, q_norm, kv_norm, wq_ext, wkv):
    s_len = pm.shape[0]

    def fn(pm_, cos_, sin_, dq, dk, dv, qg, kvg, wq, wk):
        q_lat, kv_lat, _, _ = _mla_split(pm_)
        xq, rq = _rms(q_lat)
        xkv, rkv = _rms(kv_lat)
        nq = (xq * qg).astype(BF16)
        nkv = (xkv * kvg).astype(BF16)
        dnq = jnp.zeros_like(q_lat)
        dnkv = jnp.zeros_like(kv_lat)
        dkr = jnp.zeros_like(cos_)
        dwq, dwk = [], []
        for h in range(MLA_HEADS):
            dy = dq[h][:, MLA_NOPE:]
            dqe = jnp.concatenate([dq[h][:, :MLA_NOPE], dy * cos_, dy * sin_], axis=-1).astype(BF16)
            dnq = dnq + _dot(dqe, wq[h], "nn")
            dwq.append(_dot(dqe, nq, "tn"))
            dkve = jnp.concatenate([dk[h][:, :MLA_NOPE], dv[h]], axis=-1).astype(BF16)
            dnkv = dnkv + _dot(dkve, wk[h], "nn")
            dwk.append(_dot(dkve, nkv, "tn"))
            dkr = dkr + dk[h][:, MLA_NOPE:]
        dq_lat = _rms_bwd(dnq * qg, xq, rq)
        dkv_lat = _rms_bwd(dnkv * kvg, xkv, rkv)
        dpm = jnp.concatenate([dq_lat, dkv_lat, dkr * cos_, dkr * sin_], axis=-1)
        return dpm, _sum0(dnq * xq), _sum0(dnkv * xkv), jnp.stack(dwq), jnp.stack(dwk)

    return _rowwise(
        "mla_proj_bwd", fn, [pm, cos2, sin2, dq4, dk4, dv4], [q_norm, kv_norm, wq_ext, wkv],
        [_sds((s_len, N_MLA_COLS), BF16)],
        [_sds((1, MLA_Q_RANK), F32), _sds((1, MLA_KV_RANK), F32),
         _sds(wq_ext.shape, F32), _sds(wkv.shape, F32)], 256,
    )


def _rope_tables(s_len):
    inv = ROPE_THETA ** (-jnp.arange(0, MLA_ROPE, 2, dtype=F32) / MLA_ROPE)
    ang = jnp.arange(s_len, dtype=F32)[:, None] * inv[None, :]
    cos, sin = jnp.cos(ang), jnp.sin(ang)
    return jnp.concatenate([cos, cos], axis=-1), jnp.concatenate([-sin, sin], axis=-1)


def _mix_weights(g4b):
    rest = g4b[:, 3]
    w_in_t = rest[:, O_IN:O_IN + R_IN].reshape(D_IN, D_MODEL)
    w_o = rest[:, O_O:O_O + R_O].reshape(D_MODEL, D_MODEL)
    w_uq_t = rest[:, O_UQ:O_UQ + R_UQ].reshape(MLA_HEADS, MLA_QK, MLA_Q_RANK)
    w_ukv_t = rest[:, O_UKV:O_UKV + R_UKV].reshape(MLA_HEADS, MLA_NOPE + MLA_V, MLA_KV_RANK)
    half = MLA_ROPE // 2
    w_swa = w_in_t[:N_SWA_COLS]
    w_mla = jnp.concatenate([w_in_t[N_SWA_COLS:], w_in_t[D_IN - half:], w_in_t[D_IN - MLA_ROPE:D_IN - half]], axis=0)
    wq_ext = jnp.concatenate([w_uq_t, w_uq_t[:, MLA_QK - half:], w_uq_t[:, MLA_NOPE:MLA_QK - half]], axis=1)
    return w_swa, w_mla, w_o, wq_ext, w_ukv_t


def _mix_fwd(x, gamma, sh, sc, gate, weights, q_norm, kv_norm, bias, sinkb, cos2, sin2):
    w_swa, w_mla, w_o, wq_ext, wkv = weights
    s_len = x.shape[0]
    tm = min(ROW_TILE, s_len)

    def normmod(x_, gamma_, sc_, sh_):
        return _rms(x_)[0] * gamma_ * (1.0 + sc_) + sh_

    (h,) = _rowwise("mix_normmod", normmod, [x], [gamma, sc, sh], [_sds((s_len, D_MODEL), BF16)], [], 256)
    swa3 = _mm(
        "mix_proj_swa", "nt", (s_len // tm, N_SWA_BLOCKS, 1),
        h, (tm, D_MODEL), lambda i, j, k: (i, 0),
        w_swa, (SWA_HEAD_DIM, D_MODEL), lambda i, j, k: (j, 0),
        _sds((N_SWA_BLOCKS, s_len, SWA_HEAD_DIM), BF16), (None, tm, SWA_HEAD_DIM), lambda i, j, k: (j, i, 0),
        (tm, SWA_HEAD_DIM),
    )
    pm = _mm(
        "mix_proj_mla", "nt", (s_len // tm, 1, 1),
        h, (tm, D_MODEL), lambda i, j, k: (i, 0),
        w_mla, (N_MLA_COLS, D_MODEL), lambda i, j, k: (0, 0),
        _sds((s_len, N_MLA_COLS), F32), (tm, N_MLA_COLS), lambda i, j, k: (i, 0),
        (tm, N_MLA_COLS),
    )
    q4, k4, v4 = _mla_proj(pm, cos2, sin2, q_norm, kv_norm, wq_ext, wkv)
    oa3, lse_a = _swa_fwd(swa3, bias, sinkb)
    ob4, lse_b = _mla_fwd(q4, k4, v4)
    n_a = SWA_HEADS * SWA_HEAD_DIM
    za = _mm(
        "mix_out_swa", "nn", (s_len // tm, 1, SWA_HEADS),
        oa3, (None, tm, SWA_HEAD_DIM), lambda i, j, k: (k, i, 0),
        w_o, (SWA_HEAD_DIM, D_MODEL), lambda i, j, k: (k, 0),
        _sds((s_len, D_MODEL), F32), (tm, D_MODEL), lambda i, j, k: (i, 0),
        (tm, D_MODEL),
    )
    zb = _mm(
        "mix_out_mla", "nn", (s_len // tm, 1, MLA_HEADS),
        ob4, (None, tm, MLA_V), lambda i, j, k: (k, i, 0),
        w_o, (MLA_V, D_MODEL), lambda i, j, k: (n_a // MLA_V + k, 0),
        _sds((s_len, D_MODEL), F32), (tm, D_MODEL), lambda i, j, k: (i, 0),
        (tm, D_MODEL),
    )
    (x_out,) = _rowwise(
        "mix_residual", lambda x_, za_, zb_, g_: x_ + g_ * (za_ + zb_), [x, za, zb], [gate],
        [_sds((s_len, D_MODEL), F32)], [], 256,
    )
    return x_out, (h, swa3, pm, q4, k4, v4, oa3, lse_a, ob4, lse_b, za, zb)


def _mix_bwd(dxo, x, gamma, sc, gate, weights, q_norm, kv_norm, bias, sinkb, cos2, sin2, saved):
    w_swa, w_mla, w_o, wq_ext, wkv = weights
    h, swa3, pm, q4, k4, v4, oa3, lse_a, ob4, lse_b, za, zb = saved
    s_len = x.shape[0]
    tm = tk = min(ROW_TILE, s_len)
    vec = _sds((1, D_MODEL), F32)
    n_a = SWA_HEADS * SWA_HEAD_DIM

    dz, dgate = _rowwise(
        "mix_bwd_gate", lambda dxo_, za_, zb_, g_: (g_ * dxo_, _sum0((za_ + zb_) * dxo_)),
        [dxo, za, zb], [gate], [_sds((s_len, D_MODEL), BF16)], [vec], 256,
    )
    dwo_a = _mm(
        "mix_bwd_wo_swa", "tn", (SWA_HEADS, 1, s_len // tk),
        oa3, (None, tk, SWA_HEAD_DIM), lambda i, j, k: (i, k, 0),
        dz, (tk, D_MODEL), lambda i, j, k: (k, 0),
        _sds((n_a, D_MODEL), F32), (SWA_HEAD_DIM, D_MODEL), lambda i, j, k: (i, 0),
        (SWA_HEAD_DIM, D_MODEL),
    )
    dwo_b = _mm(
        "mix_bwd_wo_mla", "tn", (MLA_HEADS, 1, s_len // tk),
        ob4, (None, tk, MLA_V), lambda i, j, k: (i, k, 0),
        dz, (tk, D_MODEL), lambda i, j, k: (k, 0),
        _sds((MLA_HEADS * MLA_V, D_MODEL), F32), (MLA_V, D_MODEL), lambda i, j, k: (i, 0),
        (MLA_V, D_MODEL),
    )
    doa3 = _mm(
        "mix_bwd_do_swa", "nt", (s_len // tm, SWA_HEADS, 1),
        dz, (tm, D_MODEL), lambda i, j, k: (i, 0),
        w_o, (SWA_HEAD_DIM, D_MODEL), lambda i, j, k: (j, 0),
        _sds((SWA_HEADS, s_len, SWA_HEAD_DIM), BF16), (None, tm, SWA_HEAD_DIM), lambda i, j, k: (j, i, 0),
        (tm, SWA_HEAD_DIM),
    )
    dob4 = _mm(
        "mix_bwd_do_mla", "nt", (s_len // tm, MLA_HEADS, 1),
        dz, (tm, D_MODEL), lambda i, j, k: (i, 0),
        w_o, (MLA_V, D_MODEL), lambda i, j, k: (n_a // MLA_V + j, 0),
        _sds((MLA_HEADS, s_len, MLA_V), BF16), (None, tm, MLA_V), lambda i, j, k: (j, i, 0),
        (tm, MLA_V),
    )
    dq3, dka, dkb, dva, dvb, dbias, dsink = _swa_bwd(swa3, bias, sinkb, oa3, doa3, lse_a)
    dq4, dk4, dv4 = _mla_bwd(q4, k4, v4, ob4, dob4, lse_b)
    dpm, dq_norm, dkv_norm, dwq_ext, dwkv = _mla_proj_bwd(pm, cos2, sin2, dq4, dk4, dv4, q_norm, kv_norm, wq_ext, wkv)

    def fold(da, db):
        return da + jnp.concatenate([db[:, WINDOW:], jnp.zeros_like(db[:, :WINDOW])], axis=1)

    dswa3 = jnp.concatenate([dq3, fold(dka, dkb).astype(BF16), fold(dva, dvb).astype(BF16)], axis=0)

    dw_swa = _mm(
        "mix_bwd_win_swa", "tn", (N_SWA_BLOCKS, 1, s_len // tk),
        dswa3, (None, tk, SWA_HEAD_DIM), lambda i, j, k: (i, k, 0),
        h, (tk, D_MODEL), lambda i, j, k: (k, 0),
        _sds((N_SWA_COLS, D_MODEL), F32), (SWA_HEAD_DIM, D_MODEL), lambda i, j, k: (i, 0),
        (SWA_HEAD_DIM, D_MODEL),
    )
    dw_mla = _mm(
        "mix_bwd_win_mla", "tn", (1, 1, s_len // tk),
        dpm, (tk, N_MLA_COLS), lambda i, j, k: (k, 0),
        h, (tk, D_MODEL), lambda i, j, k: (k, 0),
        _sds((N_MLA_COLS, D_MODEL), F32), (N_MLA_COLS, D_MODEL), lambda i, j, k: (0, 0),
        (N_MLA_COLS, D_MODEL),
    )
    dh_a = _mm(
        "mix_bwd_dh_swa", "nn", (s_len // tm, 1, N_SWA_BLOCKS),
        dswa3, (None, tm, SWA_HEAD_DIM), lambda i, j, k: (k, i, 0),
        w_swa, (SWA_HEAD_DIM, D_MODEL), lambda i, j, k: (k, 0),
        _sds((s_len, D_MODEL), F32), (tm, D_MODEL), lambda i, j, k: (i, 0),
        (tm, D_MODEL),
    )
    dh_b = _mm(
        "mix_bwd_dh_mla", "nn", (s_len // tm, 1, 1),
        dpm, (tm, N_MLA_COLS), lambda i, j, k: (i, 0),
        w_mla, (N_MLA_COLS, D_MODEL), lambda i, j, k: (0, 0),
        _sds((s_len, D_MODEL), F32), (tm, D_MODEL), lambda i, j, k: (i, 0),
        (tm, D_MODEL),
    )
    dx, dsc, dsh, dgamma = _normmod_bwd_call("mix_bwd_normmod", [dh_a, dh_b], x, dxo, gamma, sc)

    half = MLA_ROPE // 2
    n_mla_in = D_IN - N_SWA_COLS
    dw_mla_base = dw_mla[:n_mla_in]
    dw_mla_base = dw_mla_base.at[n_mla_in - half:].add(dw_mla[n_mla_in:n_mla_in + half])
    dw_mla_base = dw_mla_base.at[n_mla_in - MLA_ROPE:n_mla_in - half].add(dw_mla[n_mla_in + half:])
    dw_in_t = jnp.concatenate([dw_swa, dw_mla_base], axis=0)
    dw_uq_t = dwq_ext[:, :MLA_QK]
    dw_uq_t = dw_uq_t.at[:, MLA_QK - half:].add(dwq_ext[:, MLA_QK:MLA_QK + half])
    dw_uq_t = dw_uq_t.at[:, MLA_NOPE:MLA_QK - half].add(dwq_ext[:, MLA_QK + half:])
    dw_o = jnp.concatenate([dwo_a, dwo_b], axis=0)
    rest = jnp.concatenate(
        [
            dw_in_t.reshape(N_CHIPS, R_IN, D_MODEL),
            dw_o.reshape(N_CHIPS, R_O, D_MODEL),
            dw_uq_t.reshape(N_CHIPS, R_UQ, D_MODEL),
            dwkv.reshape(N_CHIPS, R_UKV, D_MODEL),
            jnp.zeros((N_CHIPS, F_SHARD - O_PAD, D_MODEL), F32),
        ],
        axis=1,
    ).astype(BF16)
    return dx, rest, (dsh, dsc, dgate, dgamma), dq_norm, dkv_norm, dbias, dsink


def _device_step(x, target, mod, g4a, g4b, norms, q_norm, kv_norm, sinks, rel_bias):
    s_len = x.shape[0]
    sh1, sc1, g1, sh2, sc2, g2, sh3, sc3, g3 = [mod[:, i * D_MODEL:(i + 1) * D_MODEL] for i in range(N_MOD)]
    n1, n2, n3, nf = norms
    bkt = jnp.asarray(_bucket_map())
    bias = _bias_expand(rel_bias.T, bkt).reshape(SWA_HEADS, WINDOW, 2 * WINDOW)
    sinkb = jnp.broadcast_to(sinks.reshape(SWA_HEADS, 1, 1), (SWA_HEADS, WINDOW, 1))
    cos2, sin2 = _rope_tables(s_len)
    mixw = _mix_weights(g4b)

    x1, saved1 = _ffn_fwd("ffn1", x, n1, sh1, sc1, g1, g4a, 0)
    x2, saved2 = _mix_fwd(x1, n2, sh2, sc2, g2, mixw, q_norm, kv_norm, bias, sinkb, cos2, sin2)
    x3, saved3 = _ffn_fwd("ffn2", x2, n3, sh3, sc3, g3, g4b, 0)

    def head(x_, t_, g_):
        xr, r = _rms(x_)
        err = xr * g_ - t_
        dy = err * (1.0 / D_MODEL)
        return _rms_bwd(dy * g_, xr, r), _sum0(err * err), _sum0(dy * xr)

    vec = _sds((1, D_MODEL), F32)
    dx3, sq, dnf = _rowwise("loss_head", head, [x3, target], [nf], [_sds((s_len, D_MODEL), F32)], [vec, vec], 256)
    loss_part = (0.5 / D_MODEL) * jnp.sum(sq)

    dx2, gb2, (dsh3, dsc3, dg3, dn3) = _ffn_bwd("ffn2", dx3, x2, n3, sc3, g3, g4b, 0, saved3)
    dx1, rest, (dsh2, dsc2, dg2, dn2), dq_norm, dkv_norm, dbias, dsink = _mix_bwd(
        dx2, x1, n2, sc2, g2, mixw, q_norm, kv_norm, bias, sinkb, cos2, sin2, saved2
    )
    dx0, gb1, (dsh1, dsc1, dg1, dn1) = _ffn_bwd("ffn1", dx1, x, n1, sc1, g1, g4a, 0, saved1)

    dmod = jnp.concatenate([dsh1, dsc1, dg1, dsh2, dsc2, dg2, dsh3, dsc3, dg3], axis=-1)
    drel = _bias_reduce(dbias.reshape(SWA_HEADS, -1), bkt).T
    dsinks = jnp.sum(dsink, axis=(1, 2)).reshape(1, SWA_HEADS)
    small = (dn1, dn2, dn3, dnf, dq_norm, dkv_norm, dsinks, drel)
    return loss_part, dx0, (gb1, gb2, rest[:, None]), dmod, small


_MESH = pl.DeviceIdType.MESH


def _place():
    return lax.axis_index("x"), lax.axis_index("y"), lax.axis_index("c")


def _other_chips(x, y):
    return [(1 - x, y), (x, 1 - y), (1 - x, 1 - y)]


def _allgather_small(name, blk):
    m_per, n = blk.shape

    def body(x_ref, out_ref, send_sems, recv_sems, local_sem):
        x, y, c = _place()
        me, sibling = (x, y, c), (x, y, 1 - c)
        chips = _other_chips(x, y)

        def rows(px, py, pc):
            return out_ref.at[pl.ds((4 * px + 2 * py + pc) * m_per, m_per), :]

        def copy(k, block, to, src=None):
            return pltpu.make_async_remote_copy(
                src_ref=rows(*block) if src is None else src, dst_ref=rows(*block),
                send_sem=send_sems.at[k], recv_sem=recv_sems.at[k], device_id=to, device_id_type=_MESH,
            )

        mine = pltpu.make_async_copy(x_ref, rows(*me), local_sem)
        mine.start()
        first = [copy(0, me, sibling, src=x_ref)]
        first += [copy(1 + j, me, (*chip, c), src=x_ref) for j, chip in enumerate(chips)]
        for cp in first:
            cp.start()
        passed = [copy(4 + j, (*chip, c), sibling) for j, chip in enumerate(chips)]
        for j, chip in enumerate(chips):
            copy(1 + j, (*chip, c), me).wait_recv()
            passed[j].start()
        copy(0, sibling, me).wait_recv()
        for j, chip in enumerate(chips):
            copy(4 + j, (*chip, 1 - c), me).wait_recv()
        for cp in first + passed:
            cp.wait_send()
        mine.wait()

    return pl.pallas_call(
        body,
        name=name,
        out_shape=_sds((N_DEV * m_per, n), blk.dtype),
        in_specs=[pl.BlockSpec(memory_space=pltpu.VMEM)],
        out_specs=pl.BlockSpec(memory_space=pltpu.VMEM),
        scratch_shapes=[pltpu.SemaphoreType.DMA((7,)), pltpu.SemaphoreType.DMA((7,)), pltpu.SemaphoreType.DMA],
    )(blk)


def _allgather_weights(name, own):
    def body(own_ref, g_ref, send_sems, recv_sems, local_sem):
        x, y, c = _place()
        sibling = (x, y, 1 - c)
        chips = _other_chips(x, y)

        def slot(px, py, pc):
            return g_ref.at[2 * px + py, :, pl.ds(pc * HALF, HALF), :]

        def copy(k, src, dst, to):
            return pltpu.make_async_remote_copy(
                src_ref=src, dst_ref=dst, send_sem=send_sems.at[k], recv_sem=recv_sems.at[k],
                device_id=to, device_id_type=_MESH,
            )

        mine = pltpu.make_async_copy(own_ref, g_ref.at[2 * x + y], local_sem)
        mine.start()
        my_half = own_ref.at[:, pl.ds(c * HALF, HALF), :]
        first = [copy(j, my_half, slot(x, y, c), (*chip, c)) for j, chip in enumerate(chips)]
        for cp in first:
            cp.start()
        passed = [copy(3 + j, slot(*chip, c), slot(*chip, c), sibling) for j, chip in enumerate(chips)]
        for j, chip in enumerate(chips):
            copy(j, my_half, slot(*chip, c), (*chip, c)).wait_recv()
            passed[j].start()
        for j, chip in enumerate(chips):
            copy(3 + j, my_half, slot(*chip, 1 - c), sibling).wait_recv()
        for cp in first + passed:
            cp.wait_send()
        mine.wait()

    return pl.pallas_call(
        body,
        name=name,
        out_shape=_sds((N_CHIPS,) + own.shape, own.dtype),
        in_specs=[pl.BlockSpec(memory_space=pl.ANY)],
        out_specs=pl.BlockSpec(memory_space=pl.ANY),
        scratch_shapes=[pltpu.SemaphoreType.DMA((6,)), pltpu.SemaphoreType.DMA((6,)), pltpu.SemaphoreType.DMA],
    )(own)


def _pair_exchange(name, gb):
    def body(gb_ref, land_ref, send_sem, recv_sem):
        x, y, c = _place()
        theirs = gb_ref.at[:, :, pl.ds((1 - c) * HALF, HALF), :]
        cp = pltpu.make_async_remote_copy(
            src_ref=theirs, dst_ref=land_ref, send_sem=send_sem, recv_sem=recv_sem,
            device_id=(x, y, 1 - c), device_id_type=_MESH,
        )
        cp.start()
        cp.wait()

    return pl.pallas_call(
        body,
        name=name,
        out_shape=_sds((N_CHIPS, gb.shape[1], HALF, D_MODEL), gb.dtype),
        in_specs=[pl.BlockSpec(memory_space=pl.ANY)],
        out_specs=pl.BlockSpec(memory_space=pl.ANY),
        scratch_shapes=[pltpu.SemaphoreType.DMA, pltpu.SemaphoreType.DMA],
    )(gb)


def _pair_sum(name, gb, land):
    def body(gb_ref, land_ref, o_ref):
        c = lax.axis_index("c")
        mine = gb_ref[pl.ds(pl.multiple_of(c * HALF, 16), HALF), :]
        o_ref[...] = (mine.astype(F32) + land_ref[...].astype(F32)).astype(o_ref.dtype)

    return pl.pallas_call(
        body,
        name=name,
        grid=(N_CHIPS, gb.shape[1]),
        in_specs=[
            pl.BlockSpec((None, None, F_SHARD, D_MODEL), lambda j, p: (j, p, 0, 0)),
            pl.BlockSpec((None, None, HALF, D_MODEL), lambda j, p: (j, p, 0, 0)),
        ],
        out_specs=pl.BlockSpec((None, None, HALF, D_MODEL), lambda j, p: (j, p, 0, 0)),
        out_shape=_sds(land.shape, BF16),
        compiler_params=_cparams(("parallel", "parallel")),
    )(gb, land)


def _chip_exchange(name, part):
    def body(p_ref, land_ref, send_sems, recv_sems, local_sem):
        x, y, c = _place()
        me = 2 * x + y
        chips = _other_chips(x, y)
        mine = pltpu.make_async_copy(p_ref.at[me], land_ref.at[me], local_sem)
        mine.start()

        def copy(k, chip):
            return pltpu.make_async_remote_copy(
                src_ref=p_ref.at[2 * chip[0] + chip[1]], dst_ref=land_ref.at[me],
                send_sem=send_sems.at[k], recv_sem=recv_sems.at[k], device_id=(*chip, c), device_id_type=_MESH,
            )

        sends = [copy(k, chip) for k, chip in enumerate(chips)]
        for cp in sends:
            cp.start()
        for k, chip in enumerate(chips):
            pltpu.make_async_remote_copy(
                src_ref=p_ref.at[me], dst_ref=land_ref.at[2 * chip[0] + chip[1]],
                send_sem=send_sems.at[k], recv_sem=recv_sems.at[k], device_id=(*chip, c), device_id_type=_MESH,
            ).wait_recv()
        for cp in sends:
            cp.wait_send()
        mine.wait()

    return pl.pallas_call(
        body,
        name=name,
        out_shape=_sds(part.shape, part.dtype),
        in_specs=[pl.BlockSpec(memory_space=pl.ANY)],
        out_specs=pl.BlockSpec(memory_space=pl.ANY),
        scratch_shapes=[pltpu.SemaphoreType.DMA((3,)), pltpu.SemaphoreType.DMA((3,)), pltpu.SemaphoreType.DMA],
    )(part)


def _chip_sum_share(name, land):
    n = land.shape[1]

    def body(l_ref, r_ref, buf, send_sems, recv_sems, local_sems):
        p = pl.program_id(0)
        x, y, c = _place()
        acc = l_ref[0].astype(F32)
        for j in range(1, N_CHIPS):
            acc = acc + l_ref[j].astype(F32)
        buf[p] = acc

        def copies(q):
            mine = r_ref.at[q, pl.ds(c * HALF, HALF), :]
            theirs = r_ref.at[q, pl.ds((1 - c) * HALF, HALF), :]
            local = pltpu.make_async_copy(buf.at[q], mine, local_sems.at[q])
            out = pltpu.make_async_remote_copy(
                src_ref=buf.at[q], dst_ref=mine, send_sem=send_sems.at[q], recv_sem=recv_sems.at[q],
                device_id=(x, y, 1 - c), device_id_type=_MESH,
            )
            arrive = pltpu.make_async_remote_copy(
                src_ref=buf.at[q], dst_ref=theirs, send_sem=send_sems.at[q], recv_sem=recv_sems.at[q],
                device_id=(x, y, 1 - c), device_id_type=_MESH,
            )
            return local, out, arrive

        local, out, _ = copies(p)
        local.start()
        out.start()

        @pl.when(p == n - 1)
        def _():
            for q in range(n):
                local, out, arrive = copies(q)
                arrive.wait_recv()
                out.wait_send()
                local.wait()

    return pl.pallas_call(
        body,
        name=name,
        grid=(n,),
        in_specs=[pl.BlockSpec((N_CHIPS, None, HALF, D_MODEL), lambda p: (0, p, 0, 0))],
        out_specs=pl.BlockSpec(memory_space=pl.ANY),
        out_shape=_sds((n, F_SHARD, D_MODEL), F32),
        scratch_shapes=[
            pltpu.VMEM((n, HALF, D_MODEL), F32),
            pltpu.SemaphoreType.DMA((n,)), pltpu.SemaphoreType.DMA((n,)), pltpu.SemaphoreType.DMA((n,)),
        ],
        compiler_params=_cparams(("arbitrary",)),
    )(land)


def _allreduce_small(blk):
    gathered = _allgather_small("allgather_small_grads", blk).reshape(N_DEV, PK_ROWS, 128)

    def body(g_ref, o_ref):
        acc = g_ref[0]
        for k in range(1, N_DEV):
            acc = acc + g_ref[k]
        o_ref[...] = acc

    total = pl.pallas_call(body, name="small_grads_sum", out_shape=_sds((PK_ROWS, 128), F32))(gathered)
    return gathered, total


def _adamw(name, w, g, m, v):
    rows, cols = w.shape
    tm = rows
    while tm * cols * 4 > (1 << 20) and tm % 16 == 0:
        tm //= 2

    def fn(w_, g_, m_, v_):
        m_new = ADAM_B1 * m_ + (1.0 - ADAM_B1) * g_
        v_new = ADAM_B2 * v_ + (1.0 - ADAM_B2) * (g_ * g_)
        m_hat = m_new / (1.0 - ADAM_B1 ** ADAM_STEP)
        v_hat = v_new / (1.0 - ADAM_B2 ** ADAM_STEP)
        delta = -ADAM_LR * (m_hat / (jnp.sqrt(v_hat) + ADAM_EPS) + ADAM_WD * w_)
        return delta, m_new, v_new

    out = _sds(w.shape, F32)
    return _rowwise(name, fn, [w, g, m, v], [], [out, out, out], [], tm)


def _pack_small(b_mod_like, n1, n2, n3, nf, qn, kvn, sinks, rel):
    parts = [
        b_mod_like.reshape(PK_MOD, 128),
        n1.reshape(8, 128), n2.reshape(8, 128), n3.reshape(8, 128), nf.reshape(8, 128),
        qn.reshape(2, 128), kvn.reshape(1, 128),
        jnp.pad(sinks.reshape(1, SWA_HEADS), ((0, 0), (0, 128 - SWA_HEADS))),
        rel.reshape(2, 128),
        jnp.zeros((2, 128), F32),
    ]
    return jnp.concatenate(parts, axis=0)


def _unpack_small(p):
    o = PK_MOD
    return (
        p[:o].reshape(1, N_MOD * D_MODEL),
        p[o:o + 8].reshape(1, D_MODEL), p[o + 8:o + 16].reshape(1, D_MODEL),
        p[o + 16:o + 24].reshape(1, D_MODEL), p[o + 24:o + 32].reshape(D_MODEL),
        p[o + 32:o + 34].reshape(1, MLA_Q_RANK), p[o + 34:o + 35].reshape(1, MLA_KV_RANK),
        p[o + 35:o + 36, :SWA_HEADS].reshape(1, SWA_HEADS),
        p[o + 36:o + 38].reshape(NUM_BUCKETS, SWA_HEADS),
    )


def kernel(x, c, w_mod, b_mod, norm_ffn1, ffn1_gate, ffn1_up, ffn1_down, norm_mix, w_in, q_norm, kv_norm, w_uq, w_ukv, sinks, w_o, norm_ffn2, ffn2_gate, ffn2_up, ffn2_down, rel_bias, norm_final, loss_target, m_w_mod, m_b_mod, m_norm_ffn1, m_ffn1_gate, m_ffn1_up, m_ffn1_down, m_norm_mix, m_w_in, m_q_norm, m_kv_norm, m_w_uq, m_w_ukv, m_sinks, m_w_o, m_norm_ffn2, m_ffn2_gate, m_ffn2_up, m_ffn2_down, m_rel_bias, m_norm_final, v_w_mod, v_b_mod, v_norm_ffn1, v_ffn1_gate, v_ffn1_up, v_ffn1_down, v_norm_mix, v_w_in, v_q_norm, v_kv_norm, v_w_uq, v_w_ukv, v_sinks, v_w_o, v_norm_ffn2, v_ffn2_gate, v_ffn2_up, v_ffn2_down, v_rel_bias, v_norm_final):
    ax, ay, ac = _place()
    chip = 2 * ax + ay
    dev = 2 * chip + ac
    n_mod_shard = N_MOD * D_MODEL // N_CHIPS

    def t16(w):
        return w[0].T.astype(BF16)

    rest = jnp.concatenate(
        [
            t16(w_in),
            w_o[0].astype(BF16),
            t16(w_uq).reshape(R_UQ, D_MODEL),
            t16(w_ukv).reshape(R_UKV, D_MODEL),
            jnp.zeros((F_SHARD - O_PAD, D_MODEL), BF16),
        ],
        axis=0,
    )
    own_a = jnp.stack([t16(ffn1_gate), t16(ffn1_up), ffn1_down[0].astype(BF16)])
    own_b = jnp.stack([t16(ffn2_gate), t16(ffn2_up), ffn2_down[0].astype(BF16), rest])
    g4a = _allgather_weights("allgather_weights_ffn1", own_a)
    g4b = _allgather_weights("allgather_weights_rest", own_b)

    (c_act,) = _rowwise(
        "silu_c", lambda v: v * _sigmoid(v), [c.reshape(8, 128)], [], [_sds((8, 128), F32)], [], 8
    )
    c_all = _allgather_small("allgather_c", c_act).reshape(N_DEV, D_MODEL)
    mod_cols = _mm(
        "mod_fwd", "nn", (1, n_mod_shard // 768, 1),
        c_all, (N_DEV, D_MODEL), lambda i, j, k: (0, 0),
        w_mod[0], (D_MODEL, 768), lambda i, j, k: (0, j),
        _sds((N_DEV, n_mod_shard), F32), (N_DEV, 768), lambda i, j, k: (0, j),
        (N_DEV, 768),
    )
    mod_all = _allgather_small("allgather_mod", mod_cols).reshape(N_CHIPS, 2, N_DEV, n_mod_shard)[:, 0]
    mod_all = mod_all.transpose(1, 0, 2).reshape(N_DEV, N_MOD * D_MODEL)
    mod = lax.dynamic_slice_in_dim(mod_all, dev, 1, axis=0) + b_mod

    norms = (norm_ffn1, norm_mix, norm_ffn2, norm_final.reshape(1, D_MODEL))
    loss_part, grad_x, (gb1, gb2, gb_rest), dmod, small = _device_step(
        x[0], loss_target[0], mod, g4a, g4b, norms, q_norm, kv_norm, sinks, rel_bias
    )
    loss = lax.psum(loss_part, ("x", "y", "c"))

    gathered, total = _allreduce_small(_pack_small(dmod, *small))
    (g_b_mod, g_n1, g_n2, g_n3, g_nf, g_qn, g_kvn, g_sinks, g_rel) = _unpack_small(total)
    dmod_all = gathered[:, :PK_MOD].reshape(N_DEV, N_MOD * D_MODEL)
    dmod_cols = lax.dynamic_slice_in_dim(dmod_all, chip * n_mod_shard, n_mod_shard, axis=1)
    g_w_mod = _mm(
        "mod_bwd", "tn", (1, n_mod_shard // 768, 1),
        c_all, (N_DEV, D_MODEL), lambda i, j, k: (0, 0),
        dmod_cols, (N_DEV, 768), lambda i, j, k: (0, j),
        _sds((D_MODEL, n_mod_shard), F32), (D_MODEL, 768), lambda i, j, k: (0, j),
        (D_MODEL, 768),
    )

    def reduce_group(tag, gb):
        part = _pair_sum(f"grads_pair_sum_{tag}", gb, _pair_exchange(f"grads_pair_exchange_{tag}", gb))
        return _chip_sum_share(f"grads_chip_sum_share_{tag}", _chip_exchange(f"grads_chip_exchange_{tag}", part))

    red2 = reduce_group("ffn2", gb2)
    r_rest = reduce_group("rest", gb_rest)[0]
    red1 = reduce_group("ffn1", gb1)
    g_big = {
        "ffn1_gate": red1[0].T, "ffn1_up": red1[1].T, "ffn1_down": red1[2],
        "ffn2_gate": red2[0].T, "ffn2_up": red2[1].T, "ffn2_down": red2[2],
        "w_in": r_rest[O_IN:O_IN + R_IN].T,
        "w_o": r_rest[O_O:O_O + R_O],
        "w_uq": r_rest[O_UQ:O_UQ + R_UQ].reshape(MLA_QK, MLA_Q_RANK).T,
        "w_ukv": r_rest[O_UKV:O_UKV + R_UKV].reshape(MLA_NOPE + MLA_V, MLA_KV_RANK).T,
        "w_mod": g_w_mod,
    }
    w_big = {
        "ffn1_gate": (ffn1_gate, m_ffn1_gate, v_ffn1_gate), "ffn1_up": (ffn1_up, m_ffn1_up, v_ffn1_up),
        "ffn1_down": (ffn1_down, m_ffn1_down, v_ffn1_down), "ffn2_gate": (ffn2_gate, m_ffn2_gate, v_ffn2_gate),
        "ffn2_up": (ffn2_up, m_ffn2_up, v_ffn2_up), "ffn2_down": (ffn2_down, m_ffn2_down, v_ffn2_down),
        "w_in": (w_in, m_w_in, v_w_in), "w_o": (w_o, m_w_o, v_w_o), "w_uq": (w_uq, m_w_uq, v_w_uq),
        "w_ukv": (w_ukv, m_w_ukv, v_w_ukv), "w_mod": (w_mod, m_w_mod, v_w_mod),
    }
    grads, deltas, new_m, new_v = {}, {}, {}, {}
    for nm, (w, m, v) in w_big.items():
        d_, m_, v_ = _adamw(f"adamw_{nm}", w[0], g_big[nm], m[0], v[0])
        grads[nm], deltas[nm], new_m[nm], new_v[nm] = g_big[nm][None], d_[None], m_[None], v_[None]

    small_names = ["b_mod", "norm_ffn1", "norm_mix", "norm_ffn2", "norm_final", "q_norm", "kv_norm", "sinks", "rel_bias"]
    w_small = (b_mod, norm_ffn1, norm_mix, norm_ffn2, norm_final, q_norm, kv_norm, sinks, rel_bias)
    m_small = (m_b_mod, m_norm_ffn1, m_norm_mix, m_norm_ffn2, m_norm_final, m_q_norm, m_kv_norm, m_sinks, m_rel_bias)
    v_small = (v_b_mod, v_norm_ffn1, v_norm_mix, v_norm_ffn2, v_norm_final, v_q_norm, v_kv_norm, v_sinks, v_rel_bias)
    d_p, m_p, v_p = _adamw("adamw_small", _pack_small(*w_small), total, _pack_small(*m_small), _pack_small(*v_small))
    for nm, g_, d_, m_, v_ in zip(
        small_names,
        (g_b_mod, g_n1, g_n2, g_n3, g_nf, g_qn, g_kvn, g_sinks, g_rel),
        _unpack_small(d_p), _unpack_small(m_p), _unpack_small(v_p),
    ):
        grads[nm], deltas[nm], new_m[nm], new_v[nm] = g_, d_, m_, v_

    order = ["w_mod", "b_mod", "norm_ffn1", "ffn1_gate", "ffn1_up", "ffn1_down", "norm_mix", "w_in", "q_norm", "kv_norm",
             "w_uq", "w_ukv", "sinks", "w_o", "norm_ffn2", "ffn2_gate", "ffn2_up", "ffn2_down", "rel_bias", "norm_final"]
    return (loss, grad_x[None], *[grads[n] for n in order], *[deltas[n] for n in order],
            *[new_m[n] for n in order], *[new_v[n] for n in order])
```

```python
import functools
import math

import jax
import jax.numpy as jnp
import numpy as np
from jax import lax
from jax.experimental import pallas as pl
from jax.experimental.pallas import tpu as pltpu

F32, BF16 = jnp.float32, jnp.bfloat16

D_MODEL = 1024
D_FF = 2816
EPS = 1e-6
N_MOD = 9
SWA_HEADS, SWA_KV_HEADS, SWA_HEAD_DIM, WINDOW = 8, 2, 64, 128
SWA_GROUP = SWA_HEADS // SWA_KV_HEADS
MLA_HEADS, MLA_Q_RANK, MLA_KV_RANK, MLA_NOPE, MLA_ROPE, MLA_V = 4, 256, 128, 128, 64, 128
MLA_QK = MLA_NOPE + MLA_ROPE
ROPE_THETA = 10000.0
NUM_BUCKETS, MAX_DISTANCE = 32, 128
D_IN = 1216
N_SWA_COLS = 768
N_SWA_BLOCKS = N_SWA_COLS // SWA_HEAD_DIM
N_MLA_COLS = 512

ADAM_LR, ADAM_B1, ADAM_B2, ADAM_EPS, ADAM_WD, ADAM_STEP = 0.001, 0.9, 0.999, 1e-08, 0.01, 10

N_CHIPS = 4
N_DEV = 8
F_SHARD = D_FF // N_CHIPS
HALF = F_SHARD // 2
R_IN, R_O, R_UQ, R_UKV = 304, 256, 48, 32
O_IN, O_O, O_UQ, O_UKV, O_PAD = 0, 304, 560, 608, 640

PK_MOD = 72
PK_ROWS = 112
VMEM_LIMIT = 56 << 20
ROW_TILE = 512

_DN = {
    "nn": (((1,), (0,)), ((), ())),
    "nt": (((1,), (1,)), ((), ())),
    "tn": (((0,), (0,)), ((), ())),
}


def _sds(shape, dtype):
    return jax.ShapeDtypeStruct(tuple(shape), dtype)


def _cparams(sem=None):
    kw = {"vmem_limit_bytes": VMEM_LIMIT}
    if sem is not None:
        kw["dimension_semantics"] = sem
    return pltpu.CompilerParams(**kw)


def _dot(a, b, dims):
    return lax.dot_general(a.astype(BF16), b.astype(BF16), _DN[dims], preferred_element_type=F32)


def _mm(name, dims, grid, a, a_blk, a_idx, b, b_blk, b_idx, out, out_blk, out_idx, acc_shape, alias=None, deps=()):
    nk = grid[2]

    def body(*refs):
        a_ref, b_ref = refs[:2]
        o_ref, acc_ref = refs[-2:]
        k = pl.program_id(2)

        @pl.when(k == 0)
        def _():
            acc_ref[...] = jnp.zeros_like(acc_ref)

        acc_ref[...] += _dot(a_ref[...], b_ref[...], dims)

        @pl.when(k == nk - 1)
        def _():
            o_ref[...] = acc_ref[...].astype(o_ref.dtype)

    in_specs = [pl.BlockSpec(a_blk, a_idx), pl.BlockSpec(b_blk, b_idx)]
    args = [a, b]
    kw = {}
    if alias is not None:
        in_specs.append(pl.BlockSpec(memory_space=pl.ANY))
        args.append(alias)
        kw["input_output_aliases"] = {2: 0}
    in_specs += [pl.BlockSpec(memory_space=pl.ANY)] * len(deps)
    args += list(deps)
    return pl.pallas_call(
        body,
        name=name,
        grid=grid,
        in_specs=in_specs,
        out_specs=pl.BlockSpec(out_blk, out_idx),
        out_shape=out,
        scratch_shapes=[pltpu.VMEM(acc_shape, F32)],
        compiler_params=_cparams(("parallel", "parallel", "arbitrary")),
        **kw,
    )(*args)


def _rowwise(name, fn, tiled, full, out_tiled, out_red, tm, deps=()):
    rows = tiled[0].shape[-2]
    grid = (rows // tm,)
    n_in = len(tiled) + len(full)
    n_t = len(out_tiled)
    n_dep = len(deps)

    def tspec(shape):
        nd = len(shape)
        return pl.BlockSpec(tuple(shape[:-2]) + (tm, shape[-1]), lambda i, nd=nd: (0,) * (nd - 2) + (i, 0))

    def fspec(shape):
        nd = len(shape)
        return pl.BlockSpec(tuple(shape), lambda i, nd=nd: (0,) * nd)

    def body(*refs):
        outs = fn(*[r[...] for r in refs[:n_in]])
        if not isinstance(outs, (tuple, list)):
            outs = (outs,)
        out_refs = refs[n_in + n_dep:]
        for r, v in zip(out_refs[:n_t], outs[:n_t]):
            r[...] = v.astype(r.dtype)
        red_refs = out_refs[n_t:]
        if red_refs:
            @pl.when(pl.program_id(0) == 0)
            def _():
                for r in red_refs:
                    r[...] = jnp.zeros_like(r)

            for r, v in zip(red_refs, outs[n_t:]):
                r[...] += v.astype(r.dtype)

    res = pl.pallas_call(
        body,
        name=name,
        grid=grid,
        in_specs=[tspec(a.shape) for a in tiled] + [fspec(a.shape) for a in full]
        + [pl.BlockSpec(memory_space=pl.ANY)] * n_dep,
        out_specs=[tspec(o.shape) for o in out_tiled] + [fspec(o.shape) for o in out_red],
        out_shape=list(out_tiled) + list(out_red),
        compiler_params=_cparams(("arbitrary",) if out_red else ("parallel",)),
    )(*tiled, *full, *deps)
    return res


def _sum0(v):
    return jnp.sum(v, axis=0, keepdims=True)


def _sigmoid(v):
    return 1.0 / (1.0 + jnp.exp(-v))


def _rms(x):
    r = lax.rsqrt(jnp.mean(x * x, axis=-1, keepdims=True) + EPS)
    return x * r, r


def _rms_bwd(u, xr, r):
    return r * (u - xr * jnp.mean(u * xr, axis=-1, keepdims=True))


def _ffn_fwd(tag, x, gamma, sh, sc, gate, g4, base, deps=()):
    s_len = x.shape[0]

    def normmod(x_, gamma_, sc_, sh_):
        xr, _ = _rms(x_)
        return xr * gamma_ * (1.0 + sc_) + sh_

    (h,) = _rowwise(f"{tag}_normmod", normmod, [x], [gamma, sc, sh], [_sds((s_len, D_MODEL), BF16)], [], 256, deps=deps)

    tm = min(ROW_TILE, s_len)
    ab3 = _mm(
        f"{tag}_gate_up", "nt", (s_len // tm, 2 * N_CHIPS, 1),
        h, (tm, D_MODEL), lambda i, j, k: (i, 0),
        g4, (None, None, F_SHARD, D_MODEL), lambda i, j, k: (j % N_CHIPS, base + j // N_CHIPS, 0, 0),
        _sds((2 * N_CHIPS, s_len, F_SHARD), BF16), (None, tm, F_SHARD), lambda i, j, k: (j, i, 0),
        (tm, F_SHARD),
    )

    def swiglu(ab):
        a = ab[:N_CHIPS].astype(F32)
        b = ab[N_CHIPS:].astype(F32)
        return a * _sigmoid(a) * b

    (s3,) = _rowwise(f"{tag}_swiglu", swiglu, [ab3], [], [_sds((N_CHIPS, s_len, F_SHARD), BF16)], [], 128)

    y = _mm(
        f"{tag}_down", "nn", (s_len // tm, 1, N_CHIPS),
        s3, (None, tm, F_SHARD), lambda i, j, k: (k, i, 0),
        g4, (None, None, F_SHARD, D_MODEL), lambda i, j, k: (k, base + 2, 0, 0),
        _sds((s_len, D_MODEL), F32), (tm, D_MODEL), lambda i, j, k: (i, 0),
        (tm, D_MODEL),
    )

    (x_out,) = _rowwise(
        f"{tag}_residual", lambda x_, y_, g_: x_ + 0.5 * g_ * y_, [x, y], [gate], [_sds((s_len, D_MODEL), F32)], [], 256
    )
    return x_out, (h, ab3, s3, y)


def _normmod_bwd_call(name, dh_parts, x, dxo, gamma, sc):
    s_len = x.shape[0]
    n_parts = len(dh_parts)

    def fn(*vals):
        dh = vals[0]
        for extra in vals[1:n_parts]:
            dh = dh + extra
        x_, dxo_, gamma_, sc_ = vals[n_parts:]
        xr, r = _rms(x_)
        n = xr * gamma_
        dn = dh * (1.0 + sc_)
        dx = dxo_ + _rms_bwd(dn * gamma_, xr, r)
        return dx, _sum0(dh * n), _sum0(dh), _sum0(dn * xr)

    vec = _sds((1, D_MODEL), F32)
    return _rowwise(name, fn, list(dh_parts) + [x, dxo], [gamma, sc], [_sds((s_len, D_MODEL), F32)], [vec, vec, vec], 256)


def _ffn_bwd(tag, dxo, x, gamma, sc, gate, g4, base, saved):
    h, ab3, s3, y = saved
    s_len = x.shape[0]
    vec = _sds((1, D_MODEL), F32)

    dy, dgate = _rowwise(
        f"{tag}_bwd_gate", lambda dxo_, y_, g_: (0.5 * g_ * dxo_, _sum0(0.5 * y_ * dxo_)),
        [dxo, y], [gate], [_sds((s_len, D_MODEL), BF16)], [vec], 256,
    )

    tm = min(ROW_TILE, s_len)
    ds3 = _mm(
        f"{tag}_bwd_ds", "nt", (s_len // tm, N_CHIPS, 1),
        dy, (tm, D_MODEL), lambda i, j, k: (i, 0),
        g4, (None, None, F_SHARD, D_MODEL), lambda i, j, k: (j, base + 2, 0, 0),
        _sds((N_CHIPS, s_len, F_SHARD), BF16), (None, tm, F_SHARD), lambda i, j, k: (j, i, 0),
        (tm, F_SHARD),
    )

    def swiglu_bwd(ab, ds):
        a = ab[:N_CHIPS].astype(F32)
        b = ab[N_CHIPS:].astype(F32)
        ds = ds.astype(F32)
        sig = _sigmoid(a)
        da = ds * b * sig * (1.0 + a * (1.0 - sig))
        db = ds * a * sig
        return jnp.concatenate([da, db], axis=0)

    (dab3,) = _rowwise(
        f"{tag}_bwd_swiglu", swiglu_bwd, [ab3, ds3], [], [_sds((2 * N_CHIPS, s_len, F_SHARD), BF16)], [], 128
    )

    tk = tm
    gb = _mm(
        f"{tag}_bwd_wdown", "tn", (N_CHIPS, 1, s_len // tk),
        s3, (None, tk, F_SHARD), lambda i, j, k: (i, k, 0),
        dy, (tk, D_MODEL), lambda i, j, k: (k, 0),
        _sds((N_CHIPS, 3, F_SHARD, D_MODEL), BF16), (None, None, F_SHARD, D_MODEL), lambda i, j, k: (i, 2, 0, 0),
        (F_SHARD, D_MODEL),
    )
    gb = _mm(
        f"{tag}_bwd_wgu", "tn", (2 * N_CHIPS, 1, s_len // tk),
        dab3, (None, tk, F_SHARD), lambda i, j, k: (i, k, 0),
        h, (tk, D_MODEL), lambda i, j, k: (k, 0),
        _sds(gb.shape, BF16), (None, None, F_SHARD, D_MODEL), lambda i, j, k: (i % N_CHIPS, i // N_CHIPS, 0, 0),
        (F_SHARD, D_MODEL), alias=gb,
    )
    dh = _mm(
        f"{tag}_bwd_dh", "nn", (s_len // tm, 1, 2 * N_CHIPS),
        dab3, (None, tm, F_SHARD), lambda i, j, k: (k, i, 0),
        g4, (None, None, F_SHARD, D_MODEL), lambda i, j, k: (k % N_CHIPS, base + k // N_CHIPS, 0, 0),
        _sds((s_len, D_MODEL), F32), (tm, D_MODEL), lambda i, j, k: (i, 0),
        (tm, D_MODEL),
    )
    dx, dsc, dsh, dgamma = _normmod_bwd_call(f"{tag}_bwd_normmod", [dh], x, dxo, gamma, sc)
    return dx, gb, (dsh, dsc, dgate, dgamma)


def _bucket_map():
    qi = np.arange(WINDOW)[:, None]
    kj = np.arange(2 * WINDOW)[None, :]
    dist = qi + WINDOW - kj
    band = (dist >= 0) & (dist < WINDOW)
    max_exact = NUM_BUCKETS // 2
    n = np.maximum(dist, 0)
    large = []
    for dt in (np.float32, np.float64):
        nf = np.maximum(n, 1).astype(dt)
        val = np.log(nf / dt(max_exact)) / dt(math.log(MAX_DISTANCE / max_exact)) * dt(NUM_BUCKETS - max_exact)
        large.append(np.minimum(max_exact + val.astype(np.int32), NUM_BUCKETS - 1))
    assert np.array_equal(large[0], large[1])
    bucket = np.where(n < max_exact, n, large[0])
    return np.where(band, bucket, -1).astype(np.int32).reshape(1, -1)


def _bias_expand(rel_bias_t, bkt):
    n = bkt.shape[1]

    def body(rb_ref, bkt_ref, o_ref):
        onehot = (lax.broadcasted_iota(jnp.int32, (NUM_BUCKETS, n), 0) == bkt_ref[...]).astype(F32)
        o_ref[...] = lax.dot_general(
            rb_ref[...], onehot, _DN["nn"], precision=lax.Precision.HIGHEST, preferred_element_type=F32
        )

    return pl.pallas_call(
        body, name="bias_expand", out_shape=_sds((SWA_HEADS, n), F32), compiler_params=_cparams()
    )(rel_bias_t, bkt)


def _bias_reduce(dbias_flat, bkt):
    n = bkt.shape[1]

    def body(db_ref, bkt_ref, o_ref):
        onehot = (lax.broadcasted_iota(jnp.int32, (NUM_BUCKETS, n), 0) == bkt_ref[...]).astype(F32)
        o_ref[...] = lax.dot_general(
            db_ref[...], onehot, _DN["nt"], precision=lax.Precision.HIGHEST, preferred_element_type=F32
        )

    return pl.pallas_call(
        body, name="bias_reduce", out_shape=_sds((SWA_HEADS, NUM_BUCKETS), F32), compiler_params=_cparams()
    )(dbias_flat, bkt)


_SWA_SCALE = SWA_HEAD_DIM ** -0.5
_NEG = -1e30


def _swa_in_specs():
    g, w, hd = SWA_GROUP, WINDOW, SWA_HEAD_DIM
    prev = lambda n: jnp.maximum(n - 1, 0)
    return [
        pl.BlockSpec((g, w, hd), lambda j, n: (j, n, 0)),
        pl.BlockSpec((1, w, hd), lambda j, n: (SWA_HEADS + j, prev(n), 0)),
        pl.BlockSpec((1, w, hd), lambda j, n: (SWA_HEADS + j, n, 0)),
        pl.BlockSpec((1, w, hd), lambda j, n: (SWA_HEADS + SWA_KV_HEADS + j, prev(n), 0)),
        pl.BlockSpec((1, w, hd), lambda j, n: (SWA_HEADS + SWA_KV_HEADS + j, n, 0)),
        pl.BlockSpec((g, w, 2 * w), lambda j, n: (j, 0, 0)),
        pl.BlockSpec((g, w, 1), lambda j, n: (j, 0, 0)),
    ]


def _swa_scores(q_ref, kp_ref, kc_ref, bias_ref, n):
    g, w = SWA_GROUP, WINDOW
    q = q_ref[...].reshape(g * w, SWA_HEAD_DIM)
    kk = jnp.concatenate([kp_ref[0], kc_ref[0]], axis=0)
    s = (_dot(q, kk, "nt") * _SWA_SCALE).reshape(g, w, 2 * w) + bias_ref[...]
    qi = lax.broadcasted_iota(jnp.int32, (w, 2 * w), 0)
    kj = lax.broadcasted_iota(jnp.int32, (w, 2 * w), 1)
    dist = qi + w - kj
    valid = (dist >= 0) & (dist < w) & ((n > 0) | (kj >= w))
    return q, kk, jnp.where(valid[None], s, _NEG), valid[None]


def _swa_fwd(swa3, bias, sinkb):
    s_len = swa3.shape[1]
    g, w, hd = SWA_GROUP, WINDOW, SWA_HEAD_DIM

    def body(q_ref, kp_ref, kc_ref, vp_ref, vc_ref, bias_ref, sink_ref, o_ref, lse_ref):
        n = pl.program_id(1)
        _, _, s, valid = _swa_scores(q_ref, kp_ref, kc_ref, bias_ref, n)
        vv = jnp.concatenate([vp_ref[0], vc_ref[0]], axis=0)
        sink = sink_ref[...]
        m = jnp.maximum(jnp.max(s, axis=-1, keepdims=True), sink)
        p = jnp.where(valid, jnp.exp(s - m), 0.0)
        l = jnp.sum(p, axis=-1, keepdims=True) + jnp.exp(sink - m)
        o = _dot(p.reshape(g * w, 2 * w), vv, "nn").reshape(g, w, hd)
        o_ref[...] = (o / l).astype(o_ref.dtype)
        lse_ref[...] = m + jnp.log(l)

    return pl.pallas_call(
        body,
        name="swa_fwd",
        grid=(SWA_KV_HEADS, s_len // w),
        in_specs=_swa_in_specs(),
        out_specs=[
            pl.BlockSpec((g, w, hd), lambda j, n: (j, n, 0)),
            pl.BlockSpec((g, w, 1), lambda j, n: (j, n, 0)),
        ],
        out_shape=[_sds((SWA_HEADS, s_len, hd), BF16), _sds((SWA_HEADS, s_len, 1), F32)],
        compiler_params=_cparams(("parallel", "parallel")),
    )(swa3, swa3, swa3, swa3, swa3, bias, sinkb)


def _swa_bwd(swa3, bias, sinkb, o3, do3, lse):
    s_len = swa3.shape[1]
    g, w, hd = SWA_GROUP, WINDOW, SWA_HEAD_DIM

    def body(q_ref, kp_ref, kc_ref, vp_ref, vc_ref, bias_ref, sink_ref, o_ref, do_ref, lse_ref,
             dq_ref, dka_ref, dkb_ref, dva_ref, dvb_ref, dbias_ref, dsink_ref):
        n = pl.program_id(1)

        @pl.when(n == 0)
        def _():
            dbias_ref[...] = jnp.zeros_like(dbias_ref)
            dsink_ref[...] = jnp.zeros_like(dsink_ref)

        q, kk, s, valid = _swa_scores(q_ref, kp_ref, kc_ref, bias_ref, n)
        vv = jnp.concatenate([vp_ref[0], vc_ref[0]], axis=0)
        lse_v = lse_ref[...]
        p = jnp.where(valid, jnp.exp(s - lse_v), 0.0)
        do = do_ref[...].astype(F32)
        delta = jnp.sum(do * o_ref[...].astype(F32), axis=-1, keepdims=True)
        do2 = do.reshape(g * w, hd)
        dp = _dot(do2, vv, "nt").reshape(g, w, 2 * w)
        ds = p * (dp - delta)
        dbias_ref[...] += ds
        dsink_ref[...] -= jnp.exp(sink_ref[...] - lse_v) * delta
        ds2 = ds.reshape(g * w, 2 * w)
        dq_ref[...] = (_dot(ds2, kk, "nn") * _SWA_SCALE).reshape(g, w, hd).astype(dq_ref.dtype)
        dkk = _dot(ds2, q, "tn") * _SWA_SCALE
        dvv = _dot(p.reshape(g * w, 2 * w), do2, "tn")
        dkb_ref[0] = dkk[:w]
        dka_ref[0] = dkk[w:]
        dvb_ref[0] = dvv[:w]
        dva_ref[0] = dvv[w:]

    kv_spec = pl.BlockSpec((1, w, hd), lambda j, n: (j, n, 0))
    kv_sds = _sds((SWA_KV_HEADS, s_len, hd), F32)
    return pl.pallas_call(
        body,
        name="swa_bwd",
        grid=(SWA_KV_HEADS, s_len // w),
        in_specs=_swa_in_specs() + [
            pl.BlockSpec((g, w, hd), lambda j, n: (j, n, 0)),
            pl.BlockSpec((g, w, hd), lambda j, n: (j, n, 0)),
            pl.BlockSpec((g, w, 1), lambda j, n: (j, n, 0)),
        ],
        out_specs=[
            pl.BlockSpec((g, w, hd), lambda j, n: (j, n, 0)),
            kv_spec, kv_spec, kv_spec, kv_spec,
            pl.BlockSpec((g, w, 2 * w), lambda j, n: (j, 0, 0)),
            pl.BlockSpec((g, w, 1), lambda j, n: (j, 0, 0)),
        ],
        out_shape=[
            _sds((SWA_HEADS, s_len, hd), BF16), kv_sds, kv_sds, kv_sds, kv_sds,
            _sds((SWA_HEADS, w, 2 * w), F32), _sds((SWA_HEADS, w, 1), F32),
        ],
        compiler_params=_cparams(("parallel", "arbitrary")),
    )(swa3, swa3, swa3, swa3, swa3, bias, sinkb, o3, do3, lse)


_MLA_SCALE = MLA_QK ** -0.5
_MLA_TQ = 256


def _mla_mask(i, tq, s_len):
    row = i * tq + lax.broadcasted_iota(jnp.int32, (tq, s_len), 0)
    col = lax.broadcasted_iota(jnp.int32, (tq, s_len), 1)
    return col <= row


def _mla_fwd(q4, k4, v4):
    s_len = q4.shape[1]
    tq = min(_MLA_TQ, s_len)

    def body(q_ref, k_ref, v_ref, o_ref, lse_ref):
        mask = _mla_mask(pl.program_id(1), tq, s_len)
        s = jnp.where(mask, _dot(q_ref[...], k_ref[...], "nt") * _MLA_SCALE, _NEG)
        m = jnp.max(s, axis=-1, keepdims=True)
        p = jnp.exp(s - m)
        l = jnp.sum(p, axis=-1, keepdims=True)
        o_ref[...] = (_dot(p, v_ref[...], "nn") / l).astype(o_ref.dtype)
        lse_ref[...] = m + jnp.log(l)

    return pl.pallas_call(
        body,
        name="mla_fwd",
        grid=(MLA_HEADS, s_len // tq),
        in_specs=[
            pl.BlockSpec((None, tq, MLA_QK), lambda h, i: (h, i, 0)),
            pl.BlockSpec((None, s_len, MLA_QK), lambda h, i: (h, 0, 0)),
            pl.BlockSpec((None, s_len, MLA_V), lambda h, i: (h, 0, 0)),
        ],
        out_specs=[
            pl.BlockSpec((None, tq, MLA_V), lambda h, i: (h, i, 0)),
            pl.BlockSpec((None, tq, 1), lambda h, i: (h, i, 0)),
        ],
        out_shape=[_sds((MLA_HEADS, s_len, MLA_V), BF16), _sds((MLA_HEADS, s_len, 1), F32)],
        compiler_params=_cparams(("parallel", "parallel")),
    )(q4, k4, v4)


def _mla_bwd(q4, k4, v4, o4, do4, lse):
    s_len = q4.shape[1]
    tq = min(_MLA_TQ, s_len)

    def body(q_ref, k_ref, v_ref, o_ref, do_ref, lse_ref, dq_ref, dk_ref, dv_ref):
        i = pl.program_id(1)

        @pl.when(i == 0)
        def _():
            dk_ref[...] = jnp.zeros_like(dk_ref)
            dv_ref[...] = jnp.zeros_like(dv_ref)

        mask = _mla_mask(i, tq, s_len)
        q, k, v = q_ref[...], k_ref[...], v_ref[...]
        s = jnp.where(mask, _dot(q, k, "nt") * _MLA_SCALE, _NEG)
        p = jnp.where(mask, jnp.exp(s - lse_ref[...]), 0.0)
        do = do_ref[...]
        delta = jnp.sum(do.astype(F32) * o_ref[...].astype(F32), axis=-1, keepdims=True)
        dp = _dot(do, v, "nt")
        ds = (p * (dp - delta) * _MLA_SCALE).astype(BF16)
        dq_ref[...] = _dot(ds, k, "nn")
        dk_ref[...] += _dot(ds, q, "tn")
        dv_ref[...] += _dot(p, do, "tn")

    return pl.pallas_call(
        body,
        name="mla_bwd",
        grid=(MLA_HEADS, s_len // tq),
        in_specs=[
            pl.BlockSpec((None, tq, MLA_QK), lambda h, i: (h, i, 0)),
            pl.BlockSpec((None, s_len, MLA_QK), lambda h, i: (h, 0, 0)),
            pl.BlockSpec((None, s_len, MLA_V), lambda h, i: (h, 0, 0)),
            pl.BlockSpec((None, tq, MLA_V), lambda h, i: (h, i, 0)),
            pl.BlockSpec((None, tq, MLA_V), lambda h, i: (h, i, 0)),
            pl.BlockSpec((None, tq, 1), lambda h, i: (h, i, 0)),
        ],
        out_specs=[
            pl.BlockSpec((None, tq, MLA_QK), lambda h, i: (h, i, 0)),
            pl.BlockSpec((None, s_len, MLA_QK), lambda h, i: (h, 0, 0)),
            pl.BlockSpec((None, s_len, MLA_V), lambda h, i: (h, 0, 0)),
        ],
        out_shape=[
            _sds((MLA_HEADS, s_len, MLA_QK), F32),
            _sds((MLA_HEADS, s_len, MLA_QK), F32),
            _sds((MLA_HEADS, s_len, MLA_V), F32),
        ],
        compiler_params=_cparams(("parallel", "arbitrary")),
    )(q4, k4, v4, o4, do4, lse)


def _mla_split(pm):
    q_lat = pm[:, :MLA_Q_RANK]
    kv_lat = pm[:, MLA_Q_RANK:MLA_Q_RANK + MLA_KV_RANK]
    kr = pm[:, MLA_Q_RANK + MLA_KV_RANK:MLA_Q_RANK + MLA_KV_RANK + MLA_ROPE]
    kr_sw = pm[:, MLA_Q_RANK + MLA_KV_RANK + MLA_ROPE:]
    return q_lat, kv_lat, kr, kr_sw


def _mla_proj(pm, cos2, sin2, q_norm, kv_norm, wq_ext, wkv):
    s_len = pm.shape[0]

    def fn(pm_, cos_, sin_, qg, kvg, wq, wk):
        q_lat, kv_lat, kr, kr_sw = _mla_split(pm_)
        nq = (_rms(q_lat)[0] * qg).astype(BF16)
        nkv = (_rms(kv_lat)[0] * kvg).astype(BF16)
        k_rot = kr * cos_ + kr_sw * sin_
        qs, ks, vs = [], [], []
        for h in range(MLA_HEADS):
            qe = _dot(nq, wq[h], "nt")
            q_rot = qe[:, MLA_NOPE:MLA_QK] * cos_ + qe[:, MLA_QK:] * sin_
            qs.append(jnp.concatenate([qe[:, :MLA_NOPE], q_rot], axis=-1))
            kve = _dot(nkv, wk[h], "nt")
            ks.append(jnp.concatenate([kve[:, :MLA_NOPE], k_rot], axis=-1))
            vs.append(kve[:, MLA_NOPE:])
        return jnp.stack(qs), jnp.stack(ks), jnp.stack(vs)

    return _rowwise(
        "mla_proj", fn, [pm, cos2, sin2], [q_norm, kv_norm, wq_ext, wkv],
        [_sds((MLA_HEADS, s_len, MLA_QK), BF16), _sds((MLA_HEADS, s_len, MLA_QK), BF16),
         _sds((MLA_HEADS, s_len, MLA_V), BF16)], [], 256,
    )


def _mla_proj_bwd(pm, cos2, sin2, dq4, dk4, dv4, q_norm, kv_norm, wq_ext, wkv):
    s_len = pm.shape[0]

    def fn(pm_, cos_, sin_, dq, dk, dv, qg, kvg, wq, wk):
        q_lat, kv_lat, _, _ = _mla_split(pm_)
        xq, rq = _rms(q_lat)
        xkv, rkv = _rms(kv_lat)
        nq = (xq * qg).astype(BF16)
        nkv = (xkv * kvg).astype(BF16)
        dnq = jnp.zeros_like(q_lat)
        dnkv = jnp.zeros_like(kv_lat)
        dkr = jnp.zeros_like(cos_)
        dwq, dwk = [], []
        for h in range(MLA_HEADS):
            dy = dq[h][:, MLA_NOPE:]
            dqe = jnp.concatenate([dq[h][:, :MLA_NOPE], dy * cos_, dy * sin_], axis=-1).astype(BF16)
            dnq = dnq + _dot(dqe, wq[h], "nn")
            dwq.append(_dot(dqe, nq, "tn"))
            dkve = jnp.concatenate([dk[h][:, :MLA_NOPE], dv[h]], axis=-1).astype(BF16)
            dnkv = dnkv + _dot(dkve, wk[h], "nn")
            dwk.append(_dot(dkve, nkv, "tn"))
            dkr = dkr + dk[h][:, MLA_NOPE:]
        dq_lat = _rms_bwd(dnq * qg, xq, rq)
        dkv_lat = _rms_bwd(dnkv * kvg, xkv, rkv)
        dpm = jnp.concatenate([dq_lat, dkv_lat, dkr * cos_, dkr * sin_], axis=-1)
        return dpm, _sum0(dnq * xq), _sum0(dnkv * xkv), jnp.stack(dwq), jnp.stack(dwk)

    return _rowwise(
        "mla_proj_bwd", fn, [pm, cos2, sin2, dq4, dk4, dv4], [q_norm, kv_norm, wq_ext, wkv],
        [_sds((s_len, N_MLA_COLS), BF16)],
        [_sds((1, MLA_Q_RANK), F32), _sds((1, MLA_KV_RANK), F32),
         _sds(wq_ext.shape, F32), _sds(wkv.shape, F32)], 256,
    )


def _rope_tables(s_len):
    inv = ROPE_THETA ** (-jnp.arange(0, MLA_ROPE, 2, dtype=F32) / MLA_ROPE)
    ang = jnp.arange(s_len, dtype=F32)[:, None] * inv[None, :]
    cos, sin = jnp.cos(ang), jnp.sin(ang)
    return jnp.concatenate([cos, cos], axis=-1), jnp.concatenate([-sin, sin], axis=-1)


def _mix_weights(g4b):
    rest = g4b[:, 3]
    w_in_t = rest[:, O_IN:O_IN + R_IN].reshape(D_IN, D_MODEL)
    w_o = rest[:, O_O:O_O + R_O].reshape(D_MODEL, D_MODEL)
    w_uq_t = rest[:, O_UQ:O_UQ + R_UQ].reshape(MLA_HEADS, MLA_QK, MLA_Q_RANK)
    w_ukv_t = rest[:, O_UKV:O_UKV + R_UKV].reshape(MLA_HEADS, MLA_NOPE + MLA_V, MLA_KV_RANK)
    half = MLA_ROPE // 2
    w_swa = w_in_t[:N_SWA_COLS]
    w_mla = jnp.concatenate([w_in_t[N_SWA_COLS:], w_in_t[D_IN - half:], w_in_t[D_IN - MLA_ROPE:D_IN - half]], axis=0)
    wq_ext = jnp.concatenate([w_uq_t, w_uq_t[:, MLA_QK - half:], w_uq_t[:, MLA_NOPE:MLA_QK - half]], axis=1)
    return w_swa, w_mla, w_o, wq_ext, w_ukv_t


def _mix_fwd(x, gamma, sh, sc, gate, rest_weights, q_norm, kv_norm, bias, sinkb, cos2, sin2):
    s_len = x.shape[0]
    tm = min(ROW_TILE, s_len)

    def normmod(x_, gamma_, sc_, sh_):
        return _rms(x_)[0] * gamma_ * (1.0 + sc_) + sh_

    deps = rest_weights.mid(x)
    (h,) = _rowwise("mix_normmod", normmod, [x], [gamma, sc, sh], [_sds((s_len, D_MODEL), BF16)], [], 256, deps=deps)
    g4b = rest_weights.get(h)
    weights = _mix_weights(g4b)
    w_swa, w_mla, w_o, wq_ext, wkv = weights
    swa3 = _mm(
        "mix_proj_swa", "nt", (s_len // tm, N_SWA_BLOCKS, 1),
        h, (tm, D_MODEL), lambda i, j, k: (i, 0),
        w_swa, (SWA_HEAD_DIM, D_MODEL), lambda i, j, k: (j, 0),
        _sds((N_SWA_BLOCKS, s_len, SWA_HEAD_DIM), BF16), (None, tm, SWA_HEAD_DIM), lambda i, j, k: (j, i, 0),
        (tm, SWA_HEAD_DIM),
    )
    pm = _mm(
        "mix_proj_mla", "nt", (s_len // tm, 1, 1),
        h, (tm, D_MODEL), lambda i, j, k: (i, 0),
        w_mla, (N_MLA_COLS, D_MODEL), lambda i, j, k: (0, 0),
        _sds((s_len, N_MLA_COLS), F32), (tm, N_MLA_COLS), lambda i, j, k: (i, 0),
        (tm, N_MLA_COLS),
    )
    q4, k4, v4 = _mla_proj(pm, cos2, sin2, q_norm, kv_norm, wq_ext, wkv)
    oa3, lse_a = _swa_fwd(swa3, bias, sinkb)
    ob4, lse_b = _mla_fwd(q4, k4, v4)
    n_a = SWA_HEADS * SWA_HEAD_DIM
    za = _mm(
        "mix_out_swa", "nn", (s_len // tm, 1, SWA_HEADS),
        oa3, (None, tm, SWA_HEAD_DIM), lambda i, j, k: (k, i, 0),
        w_o, (SWA_HEAD_DIM, D_MODEL), lambda i, j, k: (k, 0),
        _sds((s_len, D_MODEL), F32), (tm, D_MODEL), lambda i, j, k: (i, 0),
        (tm, D_MODEL),
    )
    zb = _mm(
        "mix_out_mla", "nn", (s_len // tm, 1, MLA_HEADS),
        ob4, (None, tm, MLA_V), lambda i, j, k: (k, i, 0),
        w_o, (MLA_V, D_MODEL), lambda i, j, k: (n_a // MLA_V + k, 0),
        _sds((s_len, D_MODEL), F32), (tm, D_MODEL), lambda i, j, k: (i, 0),
        (tm, D_MODEL),
    )
    (x_out,) = _rowwise(
        "mix_residual", lambda x_, za_, zb_, g_: x_ + g_ * (za_ + zb_), [x, za, zb], [gate],
        [_sds((s_len, D_MODEL), F32)], [], 256,
    )
    return x_out, g4b, (weights, h, swa3, pm, q4, k4, v4, oa3, lse_a, ob4, lse_b, za, zb)


def _mix_bwd(dxo, x, gamma, sc, gate, q_norm, kv_norm, bias, sinkb, cos2, sin2, saved):
    weights, h, swa3, pm, q4, k4, v4, oa3, lse_a, ob4, lse_b, za, zb = saved
    w_swa, w_mla, w_o, wq_ext, wkv = weights
    s_len = x.shape[0]
    tm = tk = min(ROW_TILE, s_len)
    vec = _sds((1, D_MODEL), F32)
    n_a = SWA_HEADS * SWA_HEAD_DIM

    dz, dgate = _rowwise(
        "mix_bwd_gate", lambda dxo_, za_, zb_, g_: (g_ * dxo_, _sum0((za_ + zb_) * dxo_)),
        [dxo, za, zb], [gate], [_sds((s_len, D_MODEL), BF16)], [vec], 256,
    )
    dwo_a = _mm(
        "mix_bwd_wo_swa", "tn", (SWA_HEADS, 1, s_len // tk),
        oa3, (None, tk, SWA_HEAD_DIM), lambda i, j, k: (i, k, 0),
        dz, (tk, D_MODEL), lambda i, j, k: (k, 0),
        _sds((n_a, D_MODEL), F32), (SWA_HEAD_DIM, D_MODEL), lambda i, j, k: (i, 0),
        (SWA_HEAD_DIM, D_MODEL),
    )
    dwo_b = _mm(
        "mix_bwd_wo_mla", "tn", (MLA_HEADS, 1, s_len // tk),
        ob4, (None, tk, MLA_V), lambda i, j, k: (i, k, 0),
        dz, (tk, D_MODEL), lambda i, j, k: (k, 0),
        _sds((MLA_HEADS * MLA_V, D_MODEL), F32), (MLA_V, D_MODEL), lambda i, j, k: (i, 0),
        (MLA_V, D_MODEL),
    )
    doa3 = _mm(
        "mix_bwd_do_swa", "nt", (s_len // tm, SWA_HEADS, 1),
        dz, (tm, D_MODEL), lambda i, j, k: (i, 0),
        w_o, (SWA_HEAD_DIM, D_MODEL), lambda i, j, k: (j, 0),
        _sds((SWA_HEADS, s_len, SWA_HEAD_DIM), BF16), (None, tm, SWA_HEAD_DIM), lambda i, j, k: (j, i, 0),
        (tm, SWA_HEAD_DIM),
    )
    dob4 = _mm(
        "mix_bwd_do_mla", "nt", (s_len // tm, MLA_HEADS, 1),
        dz, (tm, D_MODEL), lambda i, j, k: (i, 0),
        w_o, (MLA_V, D_MODEL), lambda i, j, k: (n_a // MLA_V + j, 0),
        _sds((MLA_HEADS, s_len, MLA_V), BF16), (None, tm, MLA_V), lambda i, j, k: (j, i, 0),
        (tm, MLA_V),
    )
    dq3, dka, dkb, dva, dvb, dbias, dsink = _swa_bwd(swa3, bias, sinkb, oa3, doa3, lse_a)
    dq4, dk4, dv4 = _mla_bwd(q4, k4, v4, ob4, dob4, lse_b)
    dpm, dq_norm, dkv_norm, dwq_ext, dwkv = _mla_proj_bwd(pm, cos2, sin2, dq4, dk4, dv4, q_norm, kv_norm, wq_ext, wkv)

    def fold(da, db):
        return da + jnp.concatenate([db[:, WINDOW:], jnp.zeros_like(db[:, :WINDOW])], axis=1)

    dswa3 = jnp.concatenate([dq3, fold(dka, dkb).astype(BF16), fold(dva, dvb).astype(BF16)], axis=0)

    dw_swa = _mm(
        "mix_bwd_win_swa", "tn", (N_SWA_BLOCKS, 1, s_len // tk),
        dswa3, (None, tk, SWA_HEAD_DIM), lambda i, j, k: (i, k, 0),
        h, (tk, D_MODEL), lambda i, j, k: (k, 0),
        _sds((N_SWA_COLS, D_MODEL), F32), (SWA_HEAD_DIM, D_MODEL), lambda i, j, k: (i, 0),
        (SWA_HEAD_DIM, D_MODEL),
    )
    dw_mla = _mm(
        "mix_bwd_win_mla", "tn", (1, 1, s_len // tk),
        dpm, (tk, N_MLA_COLS), lambda i, j, k: (k, 0),
        h, (tk, D_MODEL), lambda i, j, k: (k, 0),
        _sds((N_MLA_COLS, D_MODEL), F32), (N_MLA_COLS, D_MODEL), lambda i, j, k: (0, 0),
        (N_MLA_COLS, D_MODEL),
    )
    dh_a = _mm(
        "mix_bwd_dh_swa", "nn", (s_len // tm, 1, N_SWA_BLOCKS),
        dswa3, (None, tm, SWA_HEAD_DIM), lambda i, j, k: (k, i, 0),
        w_swa, (SWA_HEAD_DIM, D_MODEL), lambda i, j, k: (k, 0),
        _sds((s_len, D_MODEL), F32), (tm, D_MODEL), lambda i, j, k: (i, 0),
        (tm, D_MODEL),
    )
    dh_b = _mm(
        "mix_bwd_dh_mla", "nn", (s_len // tm, 1, 1),
        dpm, (tm, N_MLA_COLS), lambda i, j, k: (i, 0),
        w_mla, (N_MLA_COLS, D_MODEL), lambda i, j, k: (0, 0),
        _sds((s_len, D_MODEL), F32), (tm, D_MODEL), lambda i, j, k: (i, 0),
        (tm, D_MODEL),
    )
    dx, dsc, dsh, dgamma = _normmod_bwd_call("mix_bwd_normmod", [dh_a, dh_b], x, dxo, gamma, sc)

    half = MLA_ROPE // 2
    n_mla_in = D_IN - N_SWA_COLS
    dw_mla_base = dw_mla[:n_mla_in]
    dw_mla_base = dw_mla_base.at[n_mla_in - half:].add(dw_mla[n_mla_in:n_mla_in + half])
    dw_mla_base = dw_mla_base.at[n_mla_in - MLA_ROPE:n_mla_in - half].add(dw_mla[n_mla_in + half:])
    dw_in_t = jnp.concatenate([dw_swa, dw_mla_base], axis=0)
    dw_uq_t = dwq_ext[:, :MLA_QK]
    dw_uq_t = dw_uq_t.at[:, MLA_QK - half:].add(dwq_ext[:, MLA_QK:MLA_QK + half])
    dw_uq_t = dw_uq_t.at[:, MLA_NOPE:MLA_QK - half].add(dwq_ext[:, MLA_QK + half:])
    dw_o = jnp.concatenate([dwo_a, dwo_b], axis=0)
    rest = jnp.concatenate(
        [
            dw_in_t.reshape(N_CHIPS, R_IN, D_MODEL),
            dw_o.reshape(N_CHIPS, R_O, D_MODEL),
            dw_uq_t.reshape(N_CHIPS, R_UQ, D_MODEL),
            dwkv.reshape(N_CHIPS, R_UKV, D_MODEL),
            jnp.zeros((N_CHIPS, F_SHARD - O_PAD, D_MODEL), F32),
        ],
        axis=1,
    ).astype(BF16)
    return dx, rest, (dsh, dsc, dgate, dgamma), dq_norm, dkv_norm, dbias, dsink


def _device_step(x, target, mod, g4a, rest_weights, norms, q_norm, kv_norm, sinks, rel_bias, first_deps=()):
    s_len = x.shape[0]
    sh1, sc1, g1, sh2, sc2, g2, sh3, sc3, g3 = [mod[:, i * D_MODEL:(i + 1) * D_MODEL] for i in range(N_MOD)]
    n1, n2, n3, nf = norms
    bkt = jnp.asarray(_bucket_map())
    bias = _bias_expand(rel_bias.T, bkt).reshape(SWA_HEADS, WINDOW, 2 * WINDOW)
    sinkb = jnp.broadcast_to(sinks.reshape(SWA_HEADS, 1, 1), (SWA_HEADS, WINDOW, 1))
    cos2, sin2 = _rope_tables(s_len)

    x1, saved1 = _ffn_fwd("ffn1", x, n1, sh1, sc1, g1, g4a, 0, deps=first_deps)
    x2, g4b, saved2 = _mix_fwd(x1, n2, sh2, sc2, g2, rest_weights, q_norm, kv_norm, bias, sinkb, cos2, sin2)
    x3, saved3 = _ffn_fwd("ffn2", x2, n3, sh3, sc3, g3, g4b, 0)

    def head(x_, t_, g_):
        xr, r = _rms(x_)
        err = xr * g_ - t_
        dy = err * (1.0 / D_MODEL)
        return _rms_bwd(dy * g_, xr, r), _sum0(err * err), _sum0(dy * xr)

    vec = _sds((1, D_MODEL), F32)
    dx3, sq, dnf = _rowwise("loss_head", head, [x3, target], [nf], [_sds((s_len, D_MODEL), F32)], [vec, vec], 256)
    loss_part = (0.5 / D_MODEL) * jnp.sum(sq)

    dx2, gb2, (dsh3, dsc3, dg3, dn3) = _ffn_bwd("ffn2", dx3, x2, n3, sc3, g3, g4b, 0, saved3)
    dx1, rest, (dsh2, dsc2, dg2, dn2), dq_norm, dkv_norm, dbias, dsink = _mix_bwd(
        dx2, x1, n2, sc2, g2, q_norm, kv_norm, bias, sinkb, cos2, sin2, saved2
    )
    dx0, gb1, (dsh1, dsc1, dg1, dn1) = _ffn_bwd("ffn1", dx1, x, n1, sc1, g1, g4a, 0, saved1)

    dmod = jnp.concatenate([dsh1, dsc1, dg1, dsh2, dsc2, dg2, dsh3, dsc3, dg3], axis=-1)
    drel = _bias_reduce(dbias.reshape(SWA_HEADS, -1), bkt).T
    dsinks = jnp.sum(dsink, axis=(1, 2)).reshape(1, SWA_HEADS)
    small = (dn1, dn2, dn3, dnf, dq_norm, dkv_norm, dsinks, drel)
    return loss_part, dx0, (gb1, gb2, rest[:, None]), dmod, small


_MESH = pl.DeviceIdType.MESH


def _place():
    return lax.axis_index("x"), lax.axis_index("y"), lax.axis_index("c")


def _other_chips(x, y):
    return [(1 - x, y), (x, 1 - y), (1 - x, 1 - y)]


def _allgather_small(name, blk):
    m_per, n = blk.shape

    def body(x_ref, out_ref, send_sems, recv_sems, local_sem):
        x, y, c = _place()
        me, sibling = (x, y, c), (x, y, 1 - c)
        chips = _other_chips(x, y)

        def rows(px, py, pc):
            return out_ref.at[pl.ds((4 * px + 2 * py + pc) * m_per, m_per), :]

        def copy(k, block, to, src=None):
            return pltpu.make_async_remote_copy(
                src_ref=rows(*block) if src is None else src, dst_ref=rows(*block),
                send_sem=send_sems.at[k], recv_sem=recv_sems.at[k], device_id=to, device_id_type=_MESH,
            )

        mine = pltpu.make_async_copy(x_ref, rows(*me), local_sem)
        mine.start()
        first = [copy(0, me, sibling, src=x_ref)]
        first += [copy(1 + j, me, (*chip, c), src=x_ref) for j, chip in enumerate(chips)]
        for cp in first:
            cp.start()
        passed = [copy(4 + j, (*chip, c), sibling) for j, chip in enumerate(chips)]
        for j, chip in enumerate(chips):
            copy(1 + j, (*chip, c), me).wait_recv()
            passed[j].start()
        copy(0, sibling, me).wait_recv()
        for j, chip in enumerate(chips):
            copy(4 + j, (*chip, 1 - c), me).wait_recv()
        for cp in first + passed:
            cp.wait_send()
        mine.wait()

    return pl.pallas_call(
        body,
        name=name,
        out_shape=_sds((N_DEV * m_per, n), blk.dtype),
        in_specs=[pl.BlockSpec(memory_space=pltpu.VMEM)],
        out_specs=pl.BlockSpec(memory_space=pltpu.VMEM),
        scratch_shapes=[pltpu.SemaphoreType.DMA((7,)), pltpu.SemaphoreType.DMA((7,)), pltpu.SemaphoreType.DMA],
    )(blk)


def _allgather_weights(name, own):
    n = own.shape[0]

    def body(own_ref, g_ref, token, send_sems, recv_sems, local_sem):
        x, y, c = _place()
        sibling = (x, y, 1 - c)
        chips = _other_chips(x, y)
        mine = pltpu.make_async_copy(own_ref, g_ref.at[2 * x + y], local_sem)
        mine.start()
        for j, chip in enumerate(chips):
            for cp in _gather_sends(n, own_ref, g_ref, send_sems.at[j], recv_sems.at[j], x, y, c, chip):
                cp.start()
        for j, chip in enumerate(chips):
            _gather_all(own_ref, g_ref, send_sems.at[j], recv_sems.at[j], chip, c, c).wait_recv()
            for cp in _gather_forwards(n, g_ref, send_sems.at[3 + j], recv_sems.at[3 + j], chip, c, sibling):
                cp.start()
        for j, chip in enumerate(chips):
            _gather_all(own_ref, g_ref, send_sems.at[3 + j], recv_sems.at[3 + j], chip, 1 - c, c).wait_recv()
        for j, chip in enumerate(chips):
            _gather_all(own_ref, g_ref, send_sems.at[j], recv_sems.at[j], chip, c, c).wait_send()
            _gather_all(own_ref, g_ref, send_sems.at[3 + j], recv_sems.at[3 + j], chip, c, c).wait_send()
        mine.wait()
        token[...] = jnp.zeros_like(token)

    return pl.pallas_call(
        body,
        name=name,
        out_shape=[_sds((N_CHIPS,) + own.shape, own.dtype), _sds((8, 128), F32)],
        in_specs=[pl.BlockSpec(memory_space=pl.ANY)],
        out_specs=[pl.BlockSpec(memory_space=pl.ANY), pl.BlockSpec(memory_space=pltpu.VMEM)],
        scratch_shapes=[pltpu.SemaphoreType.DMA((6,)), pltpu.SemaphoreType.DMA((6,)), pltpu.SemaphoreType.DMA],
    )(own)


def _gather_sends(n, own_ref, g_ref, send_sem, recv_sem, x, y, c, chip):
    return [
        pltpu.make_async_remote_copy(
            src_ref=own_ref.at[p, pl.ds(c * HALF, HALF), :], dst_ref=g_ref.at[2 * x + y, p, pl.ds(c * HALF, HALF), :],
            send_sem=send_sem, recv_sem=recv_sem, device_id=(*chip, c), device_id_type=_MESH,
        )
        for p in range(n)
    ]


def _gather_forwards(n, g_ref, send_sem, recv_sem, chip, c, sibling):
    return [
        pltpu.make_async_remote_copy(
            src_ref=g_ref.at[2 * chip[0] + chip[1], p, pl.ds(c * HALF, HALF), :],
            dst_ref=g_ref.at[2 * chip[0] + chip[1], p, pl.ds(c * HALF, HALF), :],
            send_sem=send_sem, recv_sem=recv_sem, device_id=sibling, device_id_type=_MESH,
        )
        for p in range(n)
    ]


def _gather_all(own_ref, g_ref, send_sem, recv_sem, chip, half, c):
    return pltpu.make_async_remote_copy(
        src_ref=own_ref.at[:, pl.ds(c * HALF, HALF), :],
        dst_ref=g_ref.at[2 * chip[0] + chip[1], :, pl.ds(half * HALF, HALF), :],
        send_sem=send_sem, recv_sem=recv_sem, device_id=(*chip, c), device_id_type=_MESH,
    )


_HBM_SPEC = pl.BlockSpec(memory_space=pltpu.HBM)
_SEM_SPEC = pl.BlockSpec(memory_space=pltpu.SEMAPHORE)
_ANY_SPEC = pl.BlockSpec(memory_space=pl.ANY)
_VMEM_SPEC = pl.BlockSpec(memory_space=pltpu.VMEM)
_SPLIT_PARAMS = pltpu.CompilerParams(has_side_effects=pltpu.SideEffectType.DATAFLOW_SIDE_EFFECTING)
_TOKEN = jax.ShapeDtypeStruct((8, 128), F32)


def _in_hbm(a):
    return pltpu.with_memory_space_constraint(a, pltpu.HBM)


class _GatherBehind:
    def __init__(self, tag, own, after):
        self.tag, self.n = tag, own.shape[0]
        n = self.n
        x, y, _ = _place()
        g_init = lax.dynamic_update_slice(
            lax.empty((N_CHIPS,) + own.shape, own.dtype), own[None], (2 * x + y, 0, 0, 0)
        )

        def body(own_ref, g_ref, *rest):
            send_sems, recv_sems, _, _, token = rest[len(after):]
            x, y, c = _place()
            for j, chip in enumerate(_other_chips(x, y)):
                for cp in _gather_sends(n, own_ref, g_ref, send_sems.at[j], recv_sems.at[j], x, y, c, chip):
                    cp.start()
            token[...] = jnp.zeros_like(token)

        self.send1, self.recv1, self.own, self.g, self.token = pl.pallas_call(
            body,
            name=f"{tag}_start",
            out_shape=(
                pltpu.SemaphoreType.DMA((3,)), pltpu.SemaphoreType.DMA((3,)),
                pltpu.HBM(own.shape, own.dtype), pltpu.HBM(g_init.shape, g_init.dtype), _TOKEN,
            ),
            in_specs=(_HBM_SPEC, _HBM_SPEC) + (_ANY_SPEC,) * len(after),
            out_specs=(_SEM_SPEC, _SEM_SPEC, _HBM_SPEC, _HBM_SPEC, _VMEM_SPEC),
            input_output_aliases={0: 2, 1: 3},
            compiler_params=_SPLIT_PARAMS,
        )(_in_hbm(own), _in_hbm(g_init), *after)

    def mid(self, after):
        n = self.n

        def body(own_ref, g_ref, send1, recv1, after_ref, send2, recv2, g_out, token):
            x, y, c = _place()
            sibling = (x, y, 1 - c)
            chips = _other_chips(x, y)
            for j, chip in enumerate(chips):
                _gather_all(own_ref, g_ref, send1.at[j], recv1.at[j], chip, c, c).wait_recv()
                for cp in _gather_forwards(n, g_ref, send2.at[j], recv2.at[j], chip, c, sibling):
                    cp.start()
            for j, chip in enumerate(chips):
                _gather_all(own_ref, g_ref, send1.at[j], recv1.at[j], chip, c, c).wait_send()
            token[...] = jnp.zeros_like(token)

        self.send2, self.recv2, self.g, token = pl.pallas_call(
            body,
            name=f"{self.tag}_mid",
            out_shape=(
                pltpu.SemaphoreType.DMA((3,)), pltpu.SemaphoreType.DMA((3,)),
                pltpu.HBM(self.g.shape, self.g.dtype), _TOKEN,
            ),
            in_specs=(_HBM_SPEC, _HBM_SPEC, _SEM_SPEC, _SEM_SPEC, _ANY_SPEC),
            out_specs=(_SEM_SPEC, _SEM_SPEC, _HBM_SPEC, _VMEM_SPEC),
            input_output_aliases={1: 2},
            compiler_params=_SPLIT_PARAMS,
        )(self.own, self.g, self.send1, self.recv1, after)
        return [token]

    def get(self, after):
        def body(g_ref, send2, recv2, after_ref, g_out):
            x, y, c = _place()
            for j, chip in enumerate(_other_chips(x, y)):
                slot_in = g_ref.at[2 * chip[0] + chip[1], :, pl.ds((1 - c) * HALF, HALF), :]
                slot_out = g_ref.at[2 * chip[0] + chip[1], :, pl.ds(c * HALF, HALF), :]
                cp = pltpu.make_async_remote_copy(
                    src_ref=slot_out, dst_ref=slot_in, send_sem=send2.at[j], recv_sem=recv2.at[j],
                    device_id=(x, y, 1 - c), device_id_type=_MESH,
                )
                cp.wait_send()
                cp.wait_recv()

        return pl.pallas_call(
            body,
            name=f"{self.tag}_wait",
            out_shape=pltpu.HBM(self.g.shape, self.g.dtype),
            in_specs=(_HBM_SPEC, _SEM_SPEC, _SEM_SPEC, _ANY_SPEC),
            out_specs=_HBM_SPEC,
            input_output_aliases={0: 0},
            compiler_params=_SPLIT_PARAMS,
        )(self.g, self.send2, self.recv2, after)


def _pair_exchange(name, gb):
    def body(gb_ref, land_ref, send_sem, recv_sem):
        x, y, c = _place()
        theirs = gb_ref.at[:, :, pl.ds((1 - c) * HALF, HALF), :]
        cp = pltpu.make_async_remote_copy(
            src_ref=theirs, dst_ref=land_ref, send_sem=send_sem, recv_sem=recv_sem,
            device_id=(x, y, 1 - c), device_id_type=_MESH,
        )
        cp.start()
        cp.wait()

    return pl.pallas_call(
        body,
        name=name,
        out_shape=_sds((N_CHIPS, gb.shape[1], HALF, D_MODEL), gb.dtype),
        in_specs=[pl.BlockSpec(memory_space=pl.ANY)],
        out_specs=pl.BlockSpec(memory_space=pl.ANY),
        scratch_shapes=[pltpu.SemaphoreType.DMA, pltpu.SemaphoreType.DMA],
    )(gb)


def _pair_sum(name, gb, land):
    def body(gb_ref, land_ref, o_ref):
        c = lax.axis_index("c")
        mine = gb_ref[pl.ds(pl.multiple_of(c * HALF, 16), HALF), :]
        o_ref[...] = (mine.astype(F32) + land_ref[...].astype(F32)).astype(o_ref.dtype)

    return pl.pallas_call(
        body,
        name=name,
        grid=(N_CHIPS, gb.shape[1]),
        in_specs=[
            pl.BlockSpec((None, None, F_SHARD, D_MODEL), lambda j, p: (j, p, 0, 0)),
            pl.BlockSpec((None, None, HALF, D_MODEL), lambda j, p: (j, p, 0, 0)),
        ],
        out_specs=pl.BlockSpec((None, None, HALF, D_MODEL), lambda j, p: (j, p, 0, 0)),
        out_shape=_sds(land.shape, BF16),
        compiler_params=_cparams(("parallel", "parallel")),
    )(gb, land)


def _chip_exchange(name, part):
    def body(p_ref, land_ref, send_sems, recv_sems, local_sem):
        x, y, c = _place()
        me = 2 * x + y
        chips = _other_chips(x, y)
        mine = pltpu.make_async_copy(p_ref.at[me], land_ref.at[me], local_sem)
        mine.start()

        def copy(k, chip):
            return pltpu.make_async_remote_copy(
                src_ref=p_ref.at[2 * chip[0] + chip[1]], dst_ref=land_ref.at[me],
                send_sem=send_sems.at[k], recv_sem=recv_sems.at[k], device_id=(*chip, c), device_id_type=_MESH,
            )

        sends = [copy(k, chip) for k, chip in enumerate(chips)]
        for cp in sends:
            cp.start()
        for k, chip in enumerate(chips):
            pltpu.make_async_remote_copy(
                src_ref=p_ref.at[me], dst_ref=land_ref.at[2 * chip[0] + chip[1]],
                send_sem=send_sems.at[k], recv_sem=recv_sems.at[k], device_id=(*chip, c), device_id_type=_MESH,
            ).wait_recv()
        for cp in sends:
            cp.wait_send()
        mine.wait()

    return pl.pallas_call(
        body,
        name=name,
        out_shape=_sds(part.shape, part.dtype),
        in_specs=[pl.BlockSpec(memory_space=pl.ANY)],
        out_specs=pl.BlockSpec(memory_space=pl.ANY),
        scratch_shapes=[pltpu.SemaphoreType.DMA((3,)), pltpu.SemaphoreType.DMA((3,)), pltpu.SemaphoreType.DMA],
    )(part)


def _chip_sum_share(name, land):
    n = land.shape[1]

    def body(l_ref, r_ref, buf, send_sems, recv_sems, local_sems):
        p = pl.program_id(0)
        x, y, c = _place()
        acc = l_ref[0].astype(F32)
        for j in range(1, N_CHIPS):
            acc = acc + l_ref[j].astype(F32)
        buf[p] = acc

        def copies(q):
            mine = r_ref.at[q, pl.ds(c * HALF, HALF), :]
            theirs = r_ref.at[q, pl.ds((1 - c) * HALF, HALF), :]
            local = pltpu.make_async_copy(buf.at[q], mine, local_sems.at[q])
            out = pltpu.make_async_remote_copy(
                src_ref=buf.at[q], dst_ref=mine, send_sem=send_sems.at[q], recv_sem=recv_sems.at[q],
                device_id=(x, y, 1 - c), device_id_type=_MESH,
            )
            arrive = pltpu.make_async_remote_copy(
                src_ref=buf.at[q], dst_ref=theirs, send_sem=send_sems.at[q], recv_sem=recv_sems.at[q],
                device_id=(x, y, 1 - c), device_id_type=_MESH,
            )
            return local, out, arrive

        local, out, _ = copies(p)
        local.start()
        out.start()

        @pl.when(p == n - 1)
        def _():
            for q in range(n):
                local, out, arrive = copies(q)
                arrive.wait_recv()
                out.wait_send()
                local.wait()

    return pl.pallas_call(
        body,
        name=name,
        grid=(n,),
        in_specs=[pl.BlockSpec((N_CHIPS, None, HALF, D_MODEL), lambda p: (0, p, 0, 0))],
        out_specs=pl.BlockSpec(memory_space=pl.ANY),
        out_shape=_sds((n, F_SHARD, D_MODEL), F32),
        scratch_shapes=[
            pltpu.VMEM((n, HALF, D_MODEL), F32),
            pltpu.SemaphoreType.DMA((n,)), pltpu.SemaphoreType.DMA((n,)), pltpu.SemaphoreType.DMA((n,)),
        ],
        compiler_params=_cparams(("arbitrary",)),
    )(land)


def _allreduce_small(blk):
    gathered = _allgather_small("allgather_small_grads", blk).reshape(N_DEV, PK_ROWS, 128)

    def body(g_ref, o_ref):
        acc = g_ref[0]
        for k in range(1, N_DEV):
            acc = acc + g_ref[k]
        o_ref[...] = acc

    total = pl.pallas_call(body, name="small_grads_sum", out_shape=_sds((PK_ROWS, 128), F32))(gathered)
    return gathered, total


def _adamw(name, w, g, m, v):
    rows, cols = w.shape
    tm = rows
    while tm * cols * 4 > (1 << 20) and tm % 16 == 0:
        tm //= 2

    def fn(w_, g_, m_, v_):
        m_new = ADAM_B1 * m_ + (1.0 - ADAM_B1) * g_
        v_new = ADAM_B2 * v_ + (1.0 - ADAM_B2) * (g_ * g_)
        m_hat = m_new / (1.0 - ADAM_B1 ** ADAM_STEP)
        v_hat = v_new / (1.0 - ADAM_B2 ** ADAM_STEP)
        delta = -ADAM_LR * (m_hat / (jnp.sqrt(v_hat) + ADAM_EPS) + ADAM_WD * w_)
        return delta, m_new, v_new

    out = _sds(w.shape, F32)
    return _rowwise(name, fn, [w, g, m, v], [], [out, out, out], [], tm)


def _pack_small(b_mod_like, n1, n2, n3, nf, qn, kvn, sinks, rel):
    parts = [
        b_mod_like.reshape(PK_MOD, 128),
        n1.reshape(8, 128), n2.reshape(8, 128), n3.reshape(8, 128), nf.reshape(8, 128),
        qn.reshape(2, 128), kvn.reshape(1, 128),
        jnp.pad(sinks.reshape(1, SWA_HEADS), ((0, 0), (0, 128 - SWA_HEADS))),
        rel.reshape(2, 128),
        jnp.zeros((2, 128), F32),
    ]
    return jnp.concatenate(parts, axis=0)


def _unpack_small(p):
    o = PK_MOD
    return (
        p[:o].reshape(1, N_MOD * D_MODEL),
        p[o:o + 8].reshape(1, D_MODEL), p[o + 8:o + 16].reshape(1, D_MODEL),
        p[o + 16:o + 24].reshape(1, D_MODEL), p[o + 24:o + 32].reshape(D_MODEL),
        p[o + 32:o + 34].reshape(1, MLA_Q_RANK), p[o + 34:o + 35].reshape(1, MLA_KV_RANK),
        p[o + 35:o + 36, :SWA_HEADS].reshape(1, SWA_HEADS),
        p[o + 36:o + 38].reshape(NUM_BUCKETS, SWA_HEADS),
    )


def kernel(x, c, w_mod, b_mod, norm_ffn1, ffn1_gate, ffn1_up, ffn1_down, norm_mix, w_in, q_norm, kv_norm, w_uq, w_ukv, sinks, w_o, norm_ffn2, ffn2_gate, ffn2_up, ffn2_down, rel_bias, norm_final, loss_target, m_w_mod, m_b_mod, m_norm_ffn1, m_ffn1_gate, m_ffn1_up, m_ffn1_down, m_norm_mix, m_w_in, m_q_norm, m_kv_norm, m_w_uq, m_w_ukv, m_sinks, m_w_o, m_norm_ffn2, m_ffn2_gate, m_ffn2_up, m_ffn2_down, m_rel_bias, m_norm_final, v_w_mod, v_b_mod, v_norm_ffn1, v_ffn1_gate, v_ffn1_up, v_ffn1_down, v_norm_mix, v_w_in, v_q_norm, v_kv_norm, v_w_uq, v_w_ukv, v_sinks, v_w_o, v_norm_ffn2, v_ffn2_gate, v_ffn2_up, v_ffn2_down, v_rel_bias, v_norm_final):
    ax, ay, ac = _place()
    chip = 2 * ax + ay
    dev = 2 * chip + ac
    n_mod_shard = N_MOD * D_MODEL // N_CHIPS

    def t16(w):
        return w[0].T.astype(BF16)

    rest = jnp.concatenate(
        [
            t16(w_in),
            w_o[0].astype(BF16),
            t16(w_uq).reshape(R_UQ, D_MODEL),
            t16(w_ukv).reshape(R_UKV, D_MODEL),
            jnp.zeros((F_SHARD - O_PAD, D_MODEL), BF16),
        ],
        axis=0,
    )
    own_a = jnp.stack([t16(ffn1_gate), t16(ffn1_up), ffn1_down[0].astype(BF16)])
    own_b = jnp.stack([t16(ffn2_gate), t16(ffn2_up), ffn2_down[0].astype(BF16), rest])
    g4a, gathered_a = _allgather_weights("allgather_weights_ffn1", own_a)

    (c_act,) = _rowwise(
        "silu_c", lambda v: v * _sigmoid(v), [c.reshape(8, 128)], [], [_sds((8, 128), F32)], [], 8
    )
    c_all = _allgather_small("allgather_c", c_act).reshape(N_DEV, D_MODEL)
    mod_cols = _mm(
        "mod_fwd", "nn", (1, n_mod_shard // 768, 1),
        c_all, (N_DEV, D_MODEL), lambda i, j, k: (0, 0),
        w_mod[0], (D_MODEL, 768), lambda i, j, k: (0, j),
        _sds((N_DEV, n_mod_shard), F32), (N_DEV, 768), lambda i, j, k: (0, j),
        (N_DEV, 768),
    )
    mod_all = _allgather_small("allgather_mod", mod_cols).reshape(N_CHIPS, 2, N_DEV, n_mod_shard)[:, 0]
    mod_all = mod_all.transpose(1, 0, 2).reshape(N_DEV, N_MOD * D_MODEL)
    mod = lax.dynamic_slice_in_dim(mod_all, dev, 1, axis=0) + b_mod

    norms = (norm_ffn1, norm_mix, norm_ffn2, norm_final.reshape(1, D_MODEL))
    rest_weights = _GatherBehind("gather_rest", own_b, after=[gathered_a, mod_all])
    loss_part, grad_x, (gb1, gb2, gb_rest), dmod, small = _device_step(
        x[0], loss_target[0], mod, g4a, rest_weights, norms, q_norm, kv_norm, sinks, rel_bias,
        first_deps=[rest_weights.token],
    )
    loss = lax.psum(loss_part, ("x", "y", "c"))

    gathered, total = _allreduce_small(_pack_small(dmod, *small))
    (g_b_mod, g_n1, g_n2, g_n3, g_nf, g_qn, g_kvn, g_sinks, g_rel) = _unpack_small(total)
    dmod_all = gathered[:, :PK_MOD].reshape(N_DEV, N_MOD * D_MODEL)
    dmod_cols = lax.dynamic_slice_in_dim(dmod_all, chip * n_mod_shard, n_mod_shard, axis=1)
    g_w_mod = _mm(
        "mod_bwd", "tn", (1, n_mod_shard // 768, 1),
        c_all, (N_DEV, D_MODEL), lambda i, j, k: (0, 0),
        dmod_cols, (N_DEV, 768), lambda i, j, k: (0, j),
        _sds((D_MODEL, n_mod_shard), F32), (D_MODEL, 768), lambda i, j, k: (0, j),
        (D_MODEL, 768),
    )

    def reduce_group(tag, gb):
        part = _pair_sum(f"grads_pair_sum_{tag}", gb, _pair_exchange(f"grads_pair_exchange_{tag}", gb))
        return _chip_sum_share(f"grads_chip_sum_share_{tag}", _chip_exchange(f"grads_chip_exchange_{tag}", part))

    red2 = reduce_group("ffn2", gb2)
    r_rest = reduce_group("rest", gb_rest)[0]
    red1 = reduce_group("ffn1", gb1)
    g_big = {
        "ffn1_gate": red1[0].T, "ffn1_up": red1[1].T, "ffn1_down": red1[2],
        "ffn2_gate": red2[0].T, "ffn2_up": red2[1].T, "ffn2_down": red2[2],
        "w_in": r_rest[O_IN:O_IN + R_IN].T,
        "w_o": r_rest[O_O:O_O + R_O],
        "w_uq": r_rest[O_UQ:O_UQ + R_UQ].reshape(MLA_QK, MLA_Q_RANK).T,
        "w_ukv": r_rest[O_UKV:O_UKV + R_UKV].reshape(MLA_NOPE + MLA_V, MLA_KV_RANK).T,
        "w_mod": g_w_mod,
    }
    w_big = {
        "ffn1_gate": (ffn1_gate, m_ffn1_gate, v_ffn1_gate), "ffn1_up": (ffn1_up, m_ffn1_up, v_ffn1_up),
        "ffn1_down": (ffn1_down, m_ffn1_down, v_ffn1_down), "ffn2_gate": (ffn2_gate, m_ffn2_gate, v_ffn2_gate),
        "ffn2_up": (ffn2_up, m_ffn2_up, v_ffn2_up), "ffn2_down": (ffn2_down, m_ffn2_down, v_ffn2_down),
        "w_in": (w_in, m_w_in, v_w_in), "w_o": (w_o, m_w_o, v_w_o), "w_uq": (w_uq, m_w_uq, v_w_uq),
        "w_ukv": (w_ukv, m_w_ukv, v_w_ukv), "w_mod": (w_mod, m_w_mod, v_w_mod),
    }
    grads, deltas, new_m, new_v = {}, {}, {}, {}
    for nm, (w, m, v) in w_big.items():
        d_, m_, v_ = _adamw(f"adamw_{nm}", w[0], g_big[nm], m[0], v[0])
        grads[nm], deltas[nm], new_m[nm], new_v[nm] = g_big[nm][None], d_[None], m_[None], v_[None]

    small_names = ["b_mod", "norm_ffn1", "norm_mix", "norm_ffn2", "norm_final", "q_norm", "kv_norm", "sinks", "rel_bias"]
    w_small = (b_mod, norm_ffn1, norm_mix, norm_ffn2, norm_final, q_norm, kv_norm, sinks, rel_bias)
    m_small = (m_b_mod, m_norm_ffn1, m_norm_mix, m_norm_ffn2, m_norm_final, m_q_norm, m_kv_norm, m_sinks, m_rel_bias)
    v_small = (v_b_mod, v_norm_ffn1, v_norm_mix, v_norm_ffn2, v_norm_final, v_q_norm, v_kv_norm, v_sinks, v_rel_bias)
    d_p, m_p, v_p = _adamw("adamw_small", _pack_small(*w_small), total, _pack_small(*m_small), _pack_small(*v_small))
    for nm, g_, d_, m_, v_ in zip(
        small_names,
        (g_b_mod, g_n1, g_n2, g_n3, g_nf, g_qn, g_kvn, g_sinks, g_rel),
        _unpack_small(d_p), _unpack_small(m_p), _unpack_small(v_p),
    ):
        grads[nm], deltas[nm], new_m[nm], new_v[nm] = g_, d_, m_, v_

    order = ["w_mod", "b_mod", "norm_ffn1", "ffn1_gate", "ffn1_up", "ffn1_down", "norm_mix", "w_in", "q_norm", "kv_norm",
             "w_uq", "w_ukv", "sinks", "w_o", "norm_ffn2", "ffn2_gate", "ffn2_up", "ffn2_down", "rel_bias", "norm_final"]
    return (loss, grad_x[None], *[grads[n] for n in order], *[deltas[n] for n in order],
            *[new_m[n] for n in order], *[new_v[n] for n in order])
```

```python
import functools
import math

import jax
import jax.numpy as jnp
import numpy as np
from jax import lax
from jax.experimental import pallas as pl
from jax.experimental.pallas import tpu as pltpu

F32, BF16 = jnp.float32, jnp.bfloat16

D_MODEL = 1024
D_FF = 2816
EPS = 1e-6
N_MOD = 9
SWA_HEADS, SWA_KV_HEADS, SWA_HEAD_DIM, WINDOW = 8, 2, 64, 128
SWA_GROUP = SWA_HEADS // SWA_KV_HEADS
MLA_HEADS, MLA_Q_RANK, MLA_KV_RANK, MLA_NOPE, MLA_ROPE, MLA_V = 4, 256, 128, 128, 64, 128
MLA_QK = MLA_NOPE + MLA_ROPE
ROPE_THETA = 10000.0
NUM_BUCKETS, MAX_DISTANCE = 32, 128
D_IN = 1216
N_SWA_COLS = 768
N_SWA_BLOCKS = N_SWA_COLS // SWA_HEAD_DIM
N_MLA_COLS = 512

ADAM_LR, ADAM_B1, ADAM_B2, ADAM_EPS, ADAM_WD, ADAM_STEP = 0.001, 0.9, 0.999, 1e-08, 0.01, 10

N_CHIPS = 4
N_DEV = 8
F_SHARD = D_FF // N_CHIPS
HALF = F_SHARD // 2
R_IN, R_O, R_UQ, R_UKV = 304, 256, 48, 32
O_IN, O_O, O_UQ, O_UKV, O_PAD = 0, 304, 560, 608, 640

PK_MOD = 72
PK_ROWS = 112
VMEM_LIMIT = 56 << 20
ROW_TILE = 512

_DN = {
    "nn": (((1,), (0,)), ((), ())),
    "nt": (((1,), (1,)), ((), ())),
    "tn": (((0,), (0,)), ((), ())),
}


def _sds(shape, dtype):
    return jax.ShapeDtypeStruct(tuple(shape), dtype)


def _cparams(sem=None):
    kw = {"vmem_limit_bytes": VMEM_LIMIT}
    if sem is not None:
        kw["dimension_semantics"] = sem
    return pltpu.CompilerParams(**kw)


def _dot(a, b, dims):
    return lax.dot_general(a.astype(BF16), b.astype(BF16), _DN[dims], preferred_element_type=F32)


def _mm(name, dims, grid, a, a_blk, a_idx, b, b_blk, b_idx, out, out_blk, out_idx, acc_shape, alias=None, deps=()):
    nk = grid[2]

    def body(*refs):
        a_ref, b_ref = refs[:2]
        o_ref, acc_ref = refs[-2:]
        k = pl.program_id(2)

        @pl.when(k == 0)
        def _():
            acc_ref[...] = jnp.zeros_like(acc_ref)

        acc_ref[...] += _dot(a_ref[...], b_ref[...], dims)

        @pl.when(k == nk - 1)
        def _():
            o_ref[...] = acc_ref[...].astype(o_ref.dtype)

    in_specs = [pl.BlockSpec(a_blk, a_idx), pl.BlockSpec(b_blk, b_idx)]
    args = [a, b]
    kw = {}
    if alias is not None:
        in_specs.append(pl.BlockSpec(memory_space=pl.ANY))
        args.append(alias)
        kw["input_output_aliases"] = {2: 0}
    in_specs += [pl.BlockSpec(memory_space=pl.ANY)] * len(deps)
    args += list(deps)
    return pl.pallas_call(
        body,
        name=name,
        grid=grid,
        in_specs=in_specs,
        out_specs=pl.BlockSpec(out_blk, out_idx),
        out_shape=out,
        scratch_shapes=[pltpu.VMEM(acc_shape, F32)],
        compiler_params=_cparams(("parallel", "parallel", "arbitrary")),
        **kw,
    )(*args)


def _rowwise(name, fn, tiled, full, out_tiled, out_red, tm, deps=()):
    rows = tiled[0].shape[-2]
    grid = (rows // tm,)
    n_in = len(tiled) + len(full)
    n_t = len(out_tiled)
    n_dep = len(deps)

    def tspec(shape):
        nd = len(shape)
        return pl.BlockSpec(tuple(shape[:-2]) + (tm, shape[-1]), lambda i, nd=nd: (0,) * (nd - 2) + (i, 0))

    def fspec(shape):
        nd = len(shape)
        return pl.BlockSpec(tuple(shape), lambda i, nd=nd: (0,) * nd)

    def body(*refs):
        outs = fn(*[r[...] for r in refs[:n_in]])
        if not isinstance(outs, (tuple, list)):
            outs = (outs,)
        out_refs = refs[n_in + n_dep:]
        for r, v in zip(out_refs[:n_t], outs[:n_t]):
            r[...] = v.astype(r.dtype)
        red_refs = out_refs[n_t:]
        if red_refs:
            @pl.when(pl.program_id(0) == 0)
            def _():
                for r in red_refs:
                    r[...] = jnp.zeros_like(r)

            for r, v in zip(red_refs, outs[n_t:]):
                r[...] += v.astype(r.dtype)

    res = pl.pallas_call(
        body,
        name=name,
        grid=grid,
        in_specs=[tspec(a.shape) for a in tiled] + [fspec(a.shape) for a in full]
        + [pl.BlockSpec(memory_space=pl.ANY)] * n_dep,
        out_specs=[tspec(o.shape) for o in out_tiled] + [fspec(o.shape) for o in out_red],
        out_shape=list(out_tiled) + list(out_red),
        compiler_params=_cparams(("arbitrary",) if out_red else ("parallel",)),
    )(*tiled, *full, *deps)
    return res


def _sum0(v):
    return jnp.sum(v, axis=0, keepdims=True)


def _sigmoid(v):
    return 1.0 / (1.0 + jnp.exp(-v))


def _rms(x):
    r = lax.rsqrt(jnp.mean(x * x, axis=-1, keepdims=True) + EPS)
    return x * r, r


def _rms_bwd(u, xr, r):
    return r * (u - xr * jnp.mean(u * xr, axis=-1, keepdims=True))


def _ffn_fwd(tag, x, gamma, sh, sc, gate, g4, base, deps=()):
    s_len = x.shape[0]

    def normmod(x_, gamma_, sc_, sh_):
        xr, _ = _rms(x_)
        return xr * gamma_ * (1.0 + sc_) + sh_

    (h,) = _rowwise(f"{tag}_normmod", normmod, [x], [gamma, sc, sh], [_sds((s_len, D_MODEL), BF16)], [], 256, deps=deps)

    tm = min(ROW_TILE, s_len)
    ab3 = _mm(
        f"{tag}_gate_up", "nt", (s_len // tm, 2 * N_CHIPS, 1),
        h, (tm, D_MODEL), lambda i, j, k: (i, 0),
        g4, (None, None, F_SHARD, D_MODEL), lambda i, j, k: (j % N_CHIPS, base + j // N_CHIPS, 0, 0),
        _sds((2 * N_CHIPS, s_len, F_SHARD), BF16), (None, tm, F_SHARD), lambda i, j, k: (j, i, 0),
        (tm, F_SHARD),
    )

    def swiglu(ab):
        a = ab[:N_CHIPS].astype(F32)
        b = ab[N_CHIPS:].astype(F32)
        return a * _sigmoid(a) * b

    (s3,) = _rowwise(f"{tag}_swiglu", swiglu, [ab3], [], [_sds((N_CHIPS, s_len, F_SHARD), BF16)], [], 128)

    y = _mm(
        f"{tag}_down", "nn", (s_len // tm, 1, N_CHIPS),
        s3, (None, tm, F_SHARD), lambda i, j, k: (k, i, 0),
        g4, (None, None, F_SHARD, D_MODEL), lambda i, j, k: (k, base + 2, 0, 0),
        _sds((s_len, D_MODEL), F32), (tm, D_MODEL), lambda i, j, k: (i, 0),
        (tm, D_MODEL),
    )

    (x_out,) = _rowwise(
        f"{tag}_residual", lambda x_, y_, g_: x_ + 0.5 * g_ * y_, [x, y], [gate], [_sds((s_len, D_MODEL), F32)], [], 256
    )
    return x_out, (h, ab3, s3, y)


def _normmod_bwd_call(name, dh_parts, x, dxo, gamma, sc, deps=()):
    s_len = x.shape[0]
    n_parts = len(dh_parts)

    def fn(*vals):
        dh = vals[0]
        for extra in vals[1:n_parts]:
            dh = dh + extra
        x_, dxo_, gamma_, sc_ = vals[n_parts:]
        xr, r = _rms(x_)
        n = xr * gamma_
        dn = dh * (1.0 + sc_)
        dx = dxo_ + _rms_bwd(dn * gamma_, xr, r)
        return dx, _sum0(dh * n), _sum0(dh), _sum0(dn * xr)

    vec = _sds((1, D_MODEL), F32)
    return _rowwise(
        name, fn, list(dh_parts) + [x, dxo], [gamma, sc], [_sds((s_len, D_MODEL), F32)], [vec, vec, vec], 256, deps=deps
    )


class _BwdHooks:
    def __init__(self, after_gate=None, after_swiglu=None, after_wgu=None, after_dh=None):
        def nothing(_):
            return []

        self.after_gate = after_gate or nothing
        self.after_swiglu = after_swiglu or nothing
        self.after_wgu = after_wgu or nothing
        self.after_dh = after_dh or nothing


def _ffn_bwd(tag, dxo, x, gamma, sc, gate, g4, base, saved, hooks, deps=()):
    h, ab3, s3, y = saved
    s_len = x.shape[0]
    vec = _sds((1, D_MODEL), F32)

    dy, dgate = _rowwise(
        f"{tag}_bwd_gate", lambda dxo_, y_, g_: (0.5 * g_ * dxo_, _sum0(0.5 * y_ * dxo_)),
        [dxo, y], [gate], [_sds((s_len, D_MODEL), BF16)], [vec], 256, deps=deps,
    )

    tm = min(ROW_TILE, s_len)
    ds3 = _mm(
        f"{tag}_bwd_ds", "nt", (s_len // tm, N_CHIPS, 1),
        dy, (tm, D_MODEL), lambda i, j, k: (i, 0),
        g4, (None, None, F_SHARD, D_MODEL), lambda i, j, k: (j, base + 2, 0, 0),
        _sds((N_CHIPS, s_len, F_SHARD), BF16), (None, tm, F_SHARD), lambda i, j, k: (j, i, 0),
        (tm, F_SHARD),
    )

    def swiglu_bwd(ab, ds):
        a = ab[:N_CHIPS].astype(F32)
        b = ab[N_CHIPS:].astype(F32)
        ds = ds.astype(F32)
        sig = _sigmoid(a)
        da = ds * b * sig * (1.0 + a * (1.0 - sig))
        db = ds * a * sig
        return jnp.concatenate([da, db], axis=0)

    (dab3,) = _rowwise(
        f"{tag}_bwd_swiglu", swiglu_bwd, [ab3, ds3], [], [_sds((2 * N_CHIPS, s_len, F_SHARD), BF16)], [], 128
    )

    tk = tm
    gb = _mm(
        f"{tag}_bwd_wdown", "tn", (N_CHIPS, 1, s_len // tk),
        s3, (None, tk, F_SHARD), lambda i, j, k: (i, k, 0),
        dy, (tk, D_MODEL), lambda i, j, k: (k, 0),
        _sds((N_CHIPS, 3, F_SHARD, D_MODEL), BF16), (None, None, F_SHARD, D_MODEL), lambda i, j, k: (i, 2, 0, 0),
        (F_SHARD, D_MODEL), deps=hooks.after_swiglu(dab3),
    )
    gb = _mm(
        f"{tag}_bwd_wgu", "tn", (2 * N_CHIPS, 1, s_len // tk),
        dab3, (None, tk, F_SHARD), lambda i, j, k: (i, k, 0),
        h, (tk, D_MODEL), lambda i, j, k: (k, 0),
        _sds(gb.shape, BF16), (None, None, F_SHARD, D_MODEL), lambda i, j, k: (i % N_CHIPS, i // N_CHIPS, 0, 0),
        (F_SHARD, D_MODEL), alias=gb,
    )
    dh = _mm(
        f"{tag}_bwd_dh", "nn", (s_len // tm, 1, 2 * N_CHIPS),
        dab3, (None, tm, F_SHARD), lambda i, j, k: (k, i, 0),
        g4, (None, None, F_SHARD, D_MODEL), lambda i, j, k: (k % N_CHIPS, base + k // N_CHIPS, 0, 0),
        _sds((s_len, D_MODEL), F32), (tm, D_MODEL), lambda i, j, k: (i, 0),
        (tm, D_MODEL), deps=hooks.after_wgu(gb),
    )
    dx, dsc, dsh, dgamma = _normmod_bwd_call(
        f"{tag}_bwd_normmod", [dh], x, dxo, gamma, sc, deps=hooks.after_dh(dh)
    )
    return dx, gb, (dsh, dsc, dgate, dgamma)


def _bucket_map():
    qi = np.arange(WINDOW)[:, None]
    kj = np.arange(2 * WINDOW)[None, :]
    dist = qi + WINDOW - kj
    band = (dist >= 0) & (dist < WINDOW)
    max_exact = NUM_BUCKETS // 2
    n = np.maximum(dist, 0)
    large = []
    for dt in (np.float32, np.float64):
        nf = np.maximum(n, 1).astype(dt)
        val = np.log(nf / dt(max_exact)) / dt(math.log(MAX_DISTANCE / max_exact)) * dt(NUM_BUCKETS - max_exact)
        large.append(np.minimum(max_exact + val.astype(np.int32), NUM_BUCKETS - 1))
    assert np.array_equal(large[0], large[1])
    bucket = np.where(n < max_exact, n, large[0])
    return np.where(band, bucket, -1).astype(np.int32).reshape(1, -1)


def _bias_expand(rel_bias_t, bkt):
    n = bkt.shape[1]

    def body(rb_ref, bkt_ref, o_ref):
        onehot = (lax.broadcasted_iota(jnp.int32, (NUM_BUCKETS, n), 0) == bkt_ref[...]).astype(F32)
        o_ref[...] = lax.dot_general(
            rb_ref[...], onehot, _DN["nn"], precision=lax.Precision.HIGHEST, preferred_element_type=F32
        )

    return pl.pallas_call(
        body, name="bias_expand", out_shape=_sds((SWA_HEADS, n), F32), compiler_params=_cparams()
    )(rel_bias_t, bkt)


def _bias_reduce(dbias_flat, bkt):
    n = bkt.shape[1]

    def body(db_ref, bkt_ref, o_ref):
        onehot = (lax.broadcasted_iota(jnp.int32, (NUM_BUCKETS, n), 0) == bkt_ref[...]).astype(F32)
        o_ref[...] = lax.dot_general(
            db_ref[...], onehot, _DN["nt"], precision=lax.Precision.HIGHEST, preferred_element_type=F32
        )

    return pl.pallas_call(
        body, name="bias_reduce", out_shape=_sds((SWA_HEADS, NUM_BUCKETS), F32), compiler_params=_cparams()
    )(dbias_flat, bkt)


_SWA_SCALE = SWA_HEAD_DIM ** -0.5
_NEG = -1e30


def _swa_in_specs():
    g, w, hd = SWA_GROUP, WINDOW, SWA_HEAD_DIM
    prev = lambda n: jnp.maximum(n - 1, 0)
    return [
        pl.BlockSpec((g, w, hd), lambda j, n: (j, n, 0)),
        pl.BlockSpec((1, w, hd), lambda j, n: (SWA_HEADS + j, prev(n), 0)),
        pl.BlockSpec((1, w, hd), lambda j, n: (SWA_HEADS + j, n, 0)),
        pl.BlockSpec((1, w, hd), lambda j, n: (SWA_HEADS + SWA_KV_HEADS + j, prev(n), 0)),
        pl.BlockSpec((1, w, hd), lambda j, n: (SWA_HEADS + SWA_KV_HEADS + j, n, 0)),
        pl.BlockSpec((g, w, 2 * w), lambda j, n: (j, 0, 0)),
        pl.BlockSpec((g, w, 1), lambda j, n: (j, 0, 0)),
    ]


def _swa_scores(q_ref, kp_ref, kc_ref, bias_ref, n):
    g, w = SWA_GROUP, WINDOW
    q = q_ref[...].reshape(g * w, SWA_HEAD_DIM)
    kk = jnp.concatenate([kp_ref[0], kc_ref[0]], axis=0)
    s = (_dot(q, kk, "nt") * _SWA_SCALE).reshape(g, w, 2 * w) + bias_ref[...]
    qi = lax.broadcasted_iota(jnp.int32, (w, 2 * w), 0)
    kj = lax.broadcasted_iota(jnp.int32, (w, 2 * w), 1)
    dist = qi + w - kj
    valid = (dist >= 0) & (dist < w) & ((n > 0) | (kj >= w))
    return q, kk, jnp.where(valid[None], s, _NEG), valid[None]


def _swa_fwd(swa3, bias, sinkb):
    s_len = swa3.shape[1]
    g, w, hd = SWA_GROUP, WINDOW, SWA_HEAD_DIM

    def body(q_ref, kp_ref, kc_ref, vp_ref, vc_ref, bias_ref, sink_ref, o_ref, lse_ref):
        n = pl.program_id(1)
        _, _, s, valid = _swa_scores(q_ref, kp_ref, kc_ref, bias_ref, n)
        vv = jnp.concatenate([vp_ref[0], vc_ref[0]], axis=0)
        sink = sink_ref[...]
        m = jnp.maximum(jnp.max(s, axis=-1, keepdims=True), sink)
        p = jnp.where(valid, jnp.exp(s - m), 0.0)
        l = jnp.sum(p, axis=-1, keepdims=True) + jnp.exp(sink - m)
        o = _dot(p.reshape(g * w, 2 * w), vv, "nn").reshape(g, w, hd)
        o_ref[...] = (o / l).astype(o_ref.dtype)
        lse_ref[...] = m + jnp.log(l)

    return pl.pallas_call(
        body,
        name="swa_fwd",
        grid=(SWA_KV_HEADS, s_len // w),
        in_specs=_swa_in_specs(),
        out_specs=[
            pl.BlockSpec((g, w, hd), lambda j, n: (j, n, 0)),
            pl.BlockSpec((g, w, 1), lambda j, n: (j, n, 0)),
        ],
        out_shape=[_sds((SWA_HEADS, s_len, hd), BF16), _sds((SWA_HEADS, s_len, 1), F32)],
        compiler_params=_cparams(("parallel", "parallel")),
    )(swa3, swa3, swa3, swa3, swa3, bias, sinkb)


def _swa_bwd(swa3, bias, sinkb, o3, do3, lse):
    s_len = swa3.shape[1]
    g, w, hd = SWA_GROUP, WINDOW, SWA_HEAD_DIM

    def body(q_ref, kp_ref, kc_ref, vp_ref, vc_ref, bias_ref, sink_ref, o_ref, do_ref, lse_ref,
             dq_ref, dka_ref, dkb_ref, dva_ref, dvb_ref, dbias_ref, dsink_ref):
        n = pl.program_id(1)

        @pl.when(n == 0)
        def _():
            dbias_ref[...] = jnp.zeros_like(dbias_ref)
            dsink_ref[...] = jnp.zeros_like(dsink_ref)

        q, kk, s, valid = _swa_scores(q_ref, kp_ref, kc_ref, bias_ref, n)
        vv = jnp.concatenate([vp_ref[0], vc_ref[0]], axis=0)
        lse_v = lse_ref[...]
        p = jnp.where(valid, jnp.exp(s - lse_v), 0.0)
        do = do_ref[...].astype(F32)
        delta = jnp.sum(do * o_ref[...].astype(F32), axis=-1, keepdims=True)
        do2 = do.reshape(g * w, hd)
        dp = _dot(do2, vv, "nt").reshape(g, w, 2 * w)
        ds = p * (dp - delta)
        dbias_ref[...] += ds
        dsink_ref[...] -= jnp.exp(sink_ref[...] - lse_v) * delta
        ds2 = ds.reshape(g * w, 2 * w)
        dq_ref[...] = (_dot(ds2, kk, "nn") * _SWA_SCALE).reshape(g, w, hd).astype(dq_ref.dtype)
        dkk = _dot(ds2, q, "tn") * _SWA_SCALE
        dvv = _dot(p.reshape(g * w, 2 * w), do2, "tn")
        dkb_ref[0] = dkk[:w]
        dka_ref[0] = dkk[w:]
        dvb_ref[0] = dvv[:w]
        dva_ref[0] = dvv[w:]

    kv_spec = pl.BlockSpec((1, w, hd), lambda j, n: (j, n, 0))
    kv_sds = _sds((SWA_KV_HEADS, s_len, hd), F32)
    return pl.pallas_call(
        body,
        name="swa_bwd",
        grid=(SWA_KV_HEADS, s_len // w),
        in_specs=_swa_in_specs() + [
            pl.BlockSpec((g, w, hd), lambda j, n: (j, n, 0)),
            pl.BlockSpec((g, w, hd), lambda j, n: (j, n, 0)),
            pl.BlockSpec((g, w, 1), lambda j, n: (j, n, 0)),
        ],
        out_specs=[
            pl.BlockSpec((g, w, hd), lambda j, n: (j, n, 0)),
            kv_spec, kv_spec, kv_spec, kv_spec,
            pl.BlockSpec((g, w, 2 * w), lambda j, n: (j, 0, 0)),
            pl.BlockSpec((g, w, 1), lambda j, n: (j, 0, 0)),
        ],
        out_shape=[
            _sds((SWA_HEADS, s_len, hd), BF16), kv_sds, kv_sds, kv_sds, kv_sds,
            _sds((SWA_HEADS, w, 2 * w), F32), _sds((SWA_HEADS, w, 1), F32),
        ],
        compiler_params=_cparams(("parallel", "arbitrary")),
    )(swa3, swa3, swa3, swa3, swa3, bias, sinkb, o3, do3, lse)


_MLA_SCALE = MLA_QK ** -0.5
_MLA_TQ = 256


def _mla_mask(i, tq, s_len):
    row = i * tq + lax.broadcasted_iota(jnp.int32, (tq, s_len), 0)
    col = lax.broadcasted_iota(jnp.int32, (tq, s_len), 1)
    return col <= row


def _mla_fwd(q4, k4, v4):
    s_len = q4.shape[1]
    tq = min(_MLA_TQ, s_len)

    def body(q_ref, k_ref, v_ref, o_ref, lse_ref):
        mask = _mla_mask(pl.program_id(1), tq, s_len)
        s = jnp.where(mask, _dot(q_ref[...], k_ref[...], "nt") * _MLA_SCALE, _NEG)
        m = jnp.max(s, axis=-1, keepdims=True)
        p = jnp.exp(s - m)
        l = jnp.sum(p, axis=-1, keepdims=True)
        o_ref[...] = (_dot(p, v_ref[...], "nn") / l).astype(o_ref.dtype)
        lse_ref[...] = m + jnp.log(l)

    return pl.pallas_call(
        body,
        name="mla_fwd",
        grid=(MLA_HEADS, s_len // tq),
        in_specs=[
            pl.BlockSpec((None, tq, MLA_QK), lambda h, i: (h, i, 0)),
            pl.BlockSpec((None, s_len, MLA_QK), lambda h, i: (h, 0, 0)),
            pl.BlockSpec((None, s_len, MLA_V), lambda h, i: (h, 0, 0)),
        ],
        out_specs=[
            pl.BlockSpec((None, tq, MLA_V), lambda h, i: (h, i, 0)),
            pl.BlockSpec((None, tq, 1), lambda h, i: (h, i, 0)),
        ],
        out_shape=[_sds((MLA_HEADS, s_len, MLA_V), BF16), _sds((MLA_HEADS, s_len, 1), F32)],
        compiler_params=_cparams(("parallel", "parallel")),
    )(q4, k4, v4)


def _mla_bwd(q4, k4, v4, o4, do4, lse):
    s_len = q4.shape[1]
    tq = min(_MLA_TQ, s_len)

    def body(q_ref, k_ref, v_ref, o_ref, do_ref, lse_ref, dq_ref, dk_ref, dv_ref):
        i = pl.program_id(1)

        @pl.when(i == 0)
        def _():
            dk_ref[...] = jnp.zeros_like(dk_ref)
            dv_ref[...] = jnp.zeros_like(dv_ref)

        mask = _mla_mask(i, tq, s_len)
        q, k, v = q_ref[...], k_ref[...], v_ref[...]
        s = jnp.where(mask, _dot(q, k, "nt") * _MLA_SCALE, _NEG)
        p = jnp.where(mask, jnp.exp(s - lse_ref[...]), 0.0)
        do = do_ref[...]
        delta = jnp.sum(do.astype(F32) * o_ref[...].astype(F32), axis=-1, keepdims=True)
        dp = _dot(do, v, "nt")
        ds = (p * (dp - delta) * _MLA_SCALE).astype(BF16)
        dq_ref[...] = _dot(ds, k, "nn")
        dk_ref[...] += _dot(ds, q, "tn")
        dv_ref[...] += _dot(p, do, "tn")

    return pl.pallas_call(
        body,
        name="mla_bwd",
        grid=(MLA_HEADS, s_len // tq),
        in_specs=[
            pl.BlockSpec((None, tq, MLA_QK), lambda h, i: (h, i, 0)),
            pl.BlockSpec((None, s_len, MLA_QK), lambda h, i: (h, 0, 0)),
            pl.BlockSpec((None, s_len, MLA_V), lambda h, i: (h, 0, 0)),
            pl.BlockSpec((None, tq, MLA_V), lambda h, i: (h, i, 0)),
            pl.BlockSpec((None, tq, MLA_V), lambda h, i: (h, i, 0)),
            pl.BlockSpec((None, tq, 1), lambda h, i: (h, i, 0)),
        ],
        out_specs=[
            pl.BlockSpec((None, tq, MLA_QK), lambda h, i: (h, i, 0)),
            pl.BlockSpec((None, s_len, MLA_QK), lambda h, i: (h, 0, 0)),
            pl.BlockSpec((None, s_len, MLA_V), lambda h, i: (h, 0, 0)),
        ],
        out_shape=[
            _sds((MLA_HEADS, s_len, MLA_QK), F32),
            _sds((MLA_HEADS, s_len, MLA_QK), F32),
            _sds((MLA_HEADS, s_len, MLA_V), F32),
        ],
        compiler_params=_cparams(("parallel", "arbitrary")),
    )(q4, k4, v4, o4, do4, lse)


def _mla_split(pm):
    q_lat = pm[:, :MLA_Q_RANK]
    kv_lat = pm[:, MLA_Q_RANK:MLA_Q_RANK + MLA_KV_RANK]
    kr = pm[:, MLA_Q_RANK + MLA_KV_RANK:MLA_Q_RANK + MLA_KV_RANK + MLA_ROPE]
    kr_sw = pm[:, MLA_Q_RANK + MLA_KV_RANK + MLA_ROPE:]
    return q_lat, kv_lat, kr, kr_sw


def _mla_proj(pm, cos2, sin2, q_norm, kv_norm, wq_ext, wkv):
    s_len = pm.shape[0]

    def fn(pm_, cos_, sin_, qg, kvg, wq, wk):
        q_lat, kv_lat, kr, kr_sw = _mla_split(pm_)
        nq = (_rms(q_lat)[0] * qg).astype(BF16)
        nkv = (_rms(kv_lat)[0] * kvg).astype(BF16)
        k_rot = kr * cos_ + kr_sw * sin_
        qs, ks, vs = [], [], []
        for h in range(MLA_HEADS):
            qe = _dot(nq, wq[h], "nt")
            q_rot = qe[:, MLA_NOPE:MLA_QK] * cos_ + qe[:, MLA_QK:] * sin_
            qs.append(jnp.concatenate([qe[:, :MLA_NOPE], q_rot], axis=-1))
            kve = _dot(nkv, wk[h], "nt")
            ks.append(jnp.concatenate([kve[:, :MLA_NOPE], k_rot], axis=-1))
            vs.append(kve[:, MLA_NOPE:])
        return jnp.stack(qs), jnp.stack(ks), jnp.stack(vs)

    return _rowwise(
        "mla_proj", fn, [pm, cos2, sin2], [q_norm, kv_norm, wq_ext, wkv],
        [_sds((MLA_HEADS, s_len, MLA_QK), BF16), _sds((MLA_HEADS, s_len, MLA_QK), BF16),
         _sds((MLA_HEADS, s_len, MLA_V), BF16)], [], 256,
    )


def _mla_proj_bwd(pm, cos2, sin2, dq4, dk4, dv4, q_norm, kv_norm, wq_ext, wkv):
    s_len = pm.shape[0]

    def fn(pm_, cos_, sin_, dq, dk, dv, qg, kvg, wq, wk):
        q_lat, kv_lat, _, _ = _mla_split(pm_)
        xq, rq = _rms(q_lat)
        xkv, rkv = _rms(kv_lat)
        nq = (xq * qg).astype(BF16)
        nkv = (xkv * kvg).astype(BF16)
        dnq = jnp.zeros_like(q_lat)
        dnkv = jnp.zeros_like(kv_lat)
        dkr = jnp.zeros_like(cos_)
        dwq, dwk = [], []
        for h in range(MLA_HEADS):
            dy = dq[h][:, MLA_NOPE:]
            dqe = jnp.concatenate([dq[h][:, :MLA_NOPE], dy * cos_, dy * sin_], axis=-1).astype(BF16)
            dnq = dnq + _dot(dqe, wq[h], "nn")
            dwq.append(_dot(dqe, nq, "tn"))
            dkve = jnp.concatenate([dk[h][:, :MLA_NOPE], dv[h]], axis=-1).astype(BF16)
            dnkv = dnkv + _dot(dkve, wk[h], "nn")
            dwk.append(_dot(dkve, nkv, "tn"))
            dkr = dkr + dk[h][:, MLA_NOPE:]
        dq_lat = _rms_bwd(dnq * qg, xq, rq)
        dkv_lat = _rms_bwd(dnkv * kvg, xkv, rkv)
        dpm = jnp.concatenate([dq_lat, dkv_lat, dkr * cos_, dkr * sin_], axis=-1)
        return dpm, _sum0(dnq * xq), _sum0(dnkv * xkv), jnp.stack(dwq), jnp.stack(dwk)

    return _rowwise(
        "mla_proj_bwd", fn, [pm, cos2, sin2, dq4, dk4, dv4], [q_norm, kv_norm, wq_ext, wkv],
        [_sds((s_len, N_MLA_COLS), BF16)],
        [_sds((1, MLA_Q_RANK), F32), _sds((1, MLA_KV_RANK), F32),
         _sds(wq_ext.shape, F32), _sds(wkv.shape, F32)], 256,
    )


def _rope_tables(s_len):
    inv = ROPE_THETA ** (-jnp.arange(0, MLA_ROPE, 2, dtype=F32) / MLA_ROPE)
    ang = jnp.arange(s_len, dtype=F32)[:, None] * inv[None, :]
    cos, sin = jnp.cos(ang), jnp.sin(ang)
    return jnp.concatenate([cos, cos], axis=-1), jnp.concatenate([-sin, sin], axis=-1)


def _mix_weights(g4b):
    rest = g4b[:, 3]
    w_in_t = rest[:, O_IN:O_IN + R_IN].reshape(D_IN, D_MODEL)
    w_o = rest[:, O_O:O_O + R_O].reshape(D_MODEL, D_MODEL)
    w_uq_t = rest[:, O_UQ:O_UQ + R_UQ].reshape(MLA_HEADS, MLA_QK, MLA_Q_RANK)
    w_ukv_t = rest[:, O_UKV:O_UKV + R_UKV].reshape(MLA_HEADS, MLA_NOPE + MLA_V, MLA_KV_RANK)
    half = MLA_ROPE // 2
    w_swa = w_in_t[:N_SWA_COLS]
    w_mla = jnp.concatenate([w_in_t[N_SWA_COLS:], w_in_t[D_IN - half:], w_in_t[D_IN - MLA_ROPE:D_IN - half]], axis=0)
    wq_ext = jnp.concatenate([w_uq_t, w_uq_t[:, MLA_QK - half:], w_uq_t[:, MLA_NOPE:MLA_QK - half]], axis=1)
    return w_swa, w_mla, w_o, wq_ext, w_ukv_t


def _mix_fwd(x, gamma, sh, sc, gate, rest_weights, q_norm, kv_norm, bias, sinkb, cos2, sin2):
    s_len = x.shape[0]
    tm = min(ROW_TILE, s_len)

    def normmod(x_, gamma_, sc_, sh_):
        return _rms(x_)[0] * gamma_ * (1.0 + sc_) + sh_

    deps = rest_weights.mid(x)
    (h,) = _rowwise("mix_normmod", normmod, [x], [gamma, sc, sh], [_sds((s_len, D_MODEL), BF16)], [], 256, deps=deps)
    g4b = rest_weights.get(h)
    weights = _mix_weights(g4b)
    w_swa, w_mla, w_o, wq_ext, wkv = weights
    swa3 = _mm(
        "mix_proj_swa", "nt", (s_len // tm, N_SWA_BLOCKS, 1),
        h, (tm, D_MODEL), lambda i, j, k: (i, 0),
        w_swa, (SWA_HEAD_DIM, D_MODEL), lambda i, j, k: (j, 0),
        _sds((N_SWA_BLOCKS, s_len, SWA_HEAD_DIM), BF16), (None, tm, SWA_HEAD_DIM), lambda i, j, k: (j, i, 0),
        (tm, SWA_HEAD_DIM),
    )
    pm = _mm(
        "mix_proj_mla", "nt", (s_len // tm, 1, 1),
        h, (tm, D_MODEL), lambda i, j, k: (i, 0),
        w_mla, (N_MLA_COLS, D_MODEL), lambda i, j, k: (0, 0),
        _sds((s_len, N_MLA_COLS), F32), (tm, N_MLA_COLS), lambda i, j, k: (i, 0),
        (tm, N_MLA_COLS),
    )
    q4, k4, v4 = _mla_proj(pm, cos2, sin2, q_norm, kv_norm, wq_ext, wkv)
    oa3, lse_a = _swa_fwd(swa3, bias, sinkb)
    ob4, lse_b = _mla_fwd(q4, k4, v4)
    n_a = SWA_HEADS * SWA_HEAD_DIM
    za = _mm(
        "mix_out_swa", "nn", (s_len // tm, 1, SWA_HEADS),
        oa3, (None, tm, SWA_HEAD_DIM), lambda i, j, k: (k, i, 0),
        w_o, (SWA_HEAD_DIM, D_MODEL), lambda i, j, k: (k, 0),
        _sds((s_len, D_MODEL), F32), (tm, D_MODEL), lambda i, j, k: (i, 0),
        (tm, D_MODEL),
    )
    zb = _mm(
        "mix_out_mla", "nn", (s_len // tm, 1, MLA_HEADS),
        ob4, (None, tm, MLA_V), lambda i, j, k: (k, i, 0),
        w_o, (MLA_V, D_MODEL), lambda i, j, k: (n_a // MLA_V + k, 0),
        _sds((s_len, D_MODEL), F32), (tm, D_MODEL), lambda i, j, k: (i, 0),
        (tm, D_MODEL),
    )
    (x_out,) = _rowwise(
        "mix_residual", lambda x_, za_, zb_, g_: x_ + g_ * (za_ + zb_), [x, za, zb], [gate],
        [_sds((s_len, D_MODEL), F32)], [], 256,
    )
    return x_out, g4b, (weights, h, swa3, pm, q4, k4, v4, oa3, lse_a, ob4, lse_b, za, zb)


def _mix_bwd(dxo, x, gamma, sc, gate, q_norm, kv_norm, bias, sinkb, cos2, sin2, saved, hooks):
    weights, h, swa3, pm, q4, k4, v4, oa3, lse_a, ob4, lse_b, za, zb = saved
    w_swa, w_mla, w_o, wq_ext, wkv = weights
    s_len = x.shape[0]
    tm = tk = min(ROW_TILE, s_len)
    vec = _sds((1, D_MODEL), F32)
    n_a = SWA_HEADS * SWA_HEAD_DIM

    dz, dgate = _rowwise(
        "mix_bwd_gate", lambda dxo_, za_, zb_, g_: (g_ * dxo_, _sum0((za_ + zb_) * dxo_)),
        [dxo, za, zb], [gate], [_sds((s_len, D_MODEL), BF16)], [vec], 256,
    )
    dwo_a = _mm(
        "mix_bwd_wo_swa", "tn", (SWA_HEADS, 1, s_len // tk),
        oa3, (None, tk, SWA_HEAD_DIM), lambda i, j, k: (i, k, 0),
        dz, (tk, D_MODEL), lambda i, j, k: (k, 0),
        _sds((n_a, D_MODEL), F32), (SWA_HEAD_DIM, D_MODEL), lambda i, j, k: (i, 0),
        (SWA_HEAD_DIM, D_MODEL), deps=hooks.after_gate(dz),
    )
    dwo_b = _mm(
        "mix_bwd_wo_mla", "tn", (MLA_HEADS, 1, s_len // tk),
        ob4, (None, tk, MLA_V), lambda i, j, k: (i, k, 0),
        dz, (tk, D_MODEL), lambda i, j, k: (k, 0),
        _sds((MLA_HEADS * MLA_V, D_MODEL), F32), (MLA_V, D_MODEL), lambda i, j, k: (i, 0),
        (MLA_V, D_MODEL),
    )
    doa3 = _mm(
        "mix_bwd_do_swa", "nt", (s_len // tm, SWA_HEADS, 1),
        dz, (tm, D_MODEL), lambda i, j, k: (i, 0),
        w_o, (SWA_HEAD_DIM, D_MODEL), lambda i, j, k: (j, 0),
        _sds((SWA_HEADS, s_len, SWA_HEAD_DIM), BF16), (None, tm, SWA_HEAD_DIM), lambda i, j, k: (j, i, 0),
        (tm, SWA_HEAD_DIM),
    )
    dob4 = _mm(
        "mix_bwd_do_mla", "nt", (s_len // tm, MLA_HEADS, 1),
        dz, (tm, D_MODEL), lambda i, j, k: (i, 0),
        w_o, (MLA_V, D_MODEL), lambda i, j, k: (n_a // MLA_V + j, 0),
        _sds((MLA_HEADS, s_len, MLA_V), BF16), (None, tm, MLA_V), lambda i, j, k: (j, i, 0),
        (tm, MLA_V),
    )
    dq3, dka, dkb, dva, dvb, dbias, dsink = _swa_bwd(swa3, bias, sinkb, oa3, doa3, lse_a)
    dq4, dk4, dv4 = _mla_bwd(q4, k4, v4, ob4, dob4, lse_b)
    dpm, dq_norm, dkv_norm, dwq_ext, dwkv = _mla_proj_bwd(pm, cos2, sin2, dq4, dk4, dv4, q_norm, kv_norm, wq_ext, wkv)

    def fold(da, db):
        return da + jnp.concatenate([db[:, WINDOW:], jnp.zeros_like(db[:, :WINDOW])], axis=1)

    dswa3 = jnp.concatenate([dq3, fold(dka, dkb).astype(BF16), fold(dva, dvb).astype(BF16)], axis=0)

    dw_swa = _mm(
        "mix_bwd_win_swa", "tn", (N_SWA_BLOCKS, 1, s_len // tk),
        dswa3, (None, tk, SWA_HEAD_DIM), lambda i, j, k: (i, k, 0),
        h, (tk, D_MODEL), lambda i, j, k: (k, 0),
        _sds((N_SWA_COLS, D_MODEL), F32), (SWA_HEAD_DIM, D_MODEL), lambda i, j, k: (i, 0),
        (SWA_HEAD_DIM, D_MODEL),
    )
    dw_mla = _mm(
        "mix_bwd_win_mla", "tn", (1, 1, s_len // tk),
        dpm, (tk, N_MLA_COLS), lambda i, j, k: (k, 0),
        h, (tk, D_MODEL), lambda i, j, k: (k, 0),
        _sds((N_MLA_COLS, D_MODEL), F32), (N_MLA_COLS, D_MODEL), lambda i, j, k: (0, 0),
        (N_MLA_COLS, D_MODEL),
    )
    dh_a = _mm(
        "mix_bwd_dh_swa", "nn", (s_len // tm, 1, N_SWA_BLOCKS),
        dswa3, (None, tm, SWA_HEAD_DIM), lambda i, j, k: (k, i, 0),
        w_swa, (SWA_HEAD_DIM, D_MODEL), lambda i, j, k: (k, 0),
        _sds((s_len, D_MODEL), F32), (tm, D_MODEL), lambda i, j, k: (i, 0),
        (tm, D_MODEL),
    )
    dh_b = _mm(
        "mix_bwd_dh_mla", "nn", (s_len // tm, 1, 1),
        dpm, (tm, N_MLA_COLS), lambda i, j, k: (i, 0),
        w_mla, (N_MLA_COLS, D_MODEL), lambda i, j, k: (0, 0),
        _sds((s_len, D_MODEL), F32), (tm, D_MODEL), lambda i, j, k: (i, 0),
        (tm, D_MODEL),
    )
    dx, dsc, dsh, dgamma = _normmod_bwd_call("mix_bwd_normmod", [dh_a, dh_b], x, dxo, gamma, sc)

    half = MLA_ROPE // 2
    n_mla_in = D_IN - N_SWA_COLS
    dw_mla_base = dw_mla[:n_mla_in]
    dw_mla_base = dw_mla_base.at[n_mla_in - half:].add(dw_mla[n_mla_in:n_mla_in + half])
    dw_mla_base = dw_mla_base.at[n_mla_in - MLA_ROPE:n_mla_in - half].add(dw_mla[n_mla_in + half:])
    dw_in_t = jnp.concatenate([dw_swa, dw_mla_base], axis=0)
    dw_uq_t = dwq_ext[:, :MLA_QK]
    dw_uq_t = dw_uq_t.at[:, MLA_QK - half:].add(dwq_ext[:, MLA_QK:MLA_QK + half])
    dw_uq_t = dw_uq_t.at[:, MLA_NOPE:MLA_QK - half].add(dwq_ext[:, MLA_QK + half:])
    dw_o = jnp.concatenate([dwo_a, dwo_b], axis=0)
    rest = jnp.concatenate(
        [
            dw_in_t.reshape(N_CHIPS, R_IN, D_MODEL),
            dw_o.reshape(N_CHIPS, R_O, D_MODEL),
            dw_uq_t.reshape(N_CHIPS, R_UQ, D_MODEL),
            dwkv.reshape(N_CHIPS, R_UKV, D_MODEL),
            jnp.zeros((N_CHIPS, F_SHARD - O_PAD, D_MODEL), F32),
        ],
        axis=1,
    ).astype(BF16)
    return dx, rest, (dsh, dsc, dgate, dgamma), dq_norm, dkv_norm, dbias, dsink


def _device_step(x, target, mod, g4a, rest_weights, norms, q_norm, kv_norm, sinks, rel_bias, first_deps=(),
                 hooks2=None, hooks_mix=None, mix_done=None, hooks1=None):
    s_len = x.shape[0]
    sh1, sc1, g1, sh2, sc2, g2, sh3, sc3, g3 = [mod[:, i * D_MODEL:(i + 1) * D_MODEL] for i in range(N_MOD)]
    n1, n2, n3, nf = norms
    bkt = jnp.asarray(_bucket_map())
    bias = _bias_expand(rel_bias.T, bkt).reshape(SWA_HEADS, WINDOW, 2 * WINDOW)
    sinkb = jnp.broadcast_to(sinks.reshape(SWA_HEADS, 1, 1), (SWA_HEADS, WINDOW, 1))
    cos2, sin2 = _rope_tables(s_len)

    x1, saved1 = _ffn_fwd("ffn1", x, n1, sh1, sc1, g1, g4a, 0, deps=first_deps)
    x2, g4b, saved2 = _mix_fwd(x1, n2, sh2, sc2, g2, rest_weights, q_norm, kv_norm, bias, sinkb, cos2, sin2)
    x3, saved3 = _ffn_fwd("ffn2", x2, n3, sh3, sc3, g3, g4b, 0)

    def head(x_, t_, g_):
        xr, r = _rms(x_)
        err = xr * g_ - t_
        dy = err * (1.0 / D_MODEL)
        return _rms_bwd(dy * g_, xr, r), _sum0(err * err), _sum0(dy * xr)

    vec = _sds((1, D_MODEL), F32)
    dx3, sq, dnf = _rowwise("loss_head", head, [x3, target], [nf], [_sds((s_len, D_MODEL), F32)], [vec, vec], 256)
    loss_part = (0.5 / D_MODEL) * jnp.sum(sq)

    dx2, gb2, (dsh3, dsc3, dg3, dn3) = _ffn_bwd("ffn2", dx3, x2, n3, sc3, g3, g4b, 0, saved3, hooks2 or _BwdHooks())
    dx1, rest, (dsh2, dsc2, dg2, dn2), dq_norm, dkv_norm, dbias, dsink = _mix_bwd(
        dx2, x1, n2, sc2, g2, q_norm, kv_norm, bias, sinkb, cos2, sin2, saved2, hooks_mix or _BwdHooks()
    )
    rest = rest[:, None]
    deps1 = mix_done(dx1, rest) if mix_done is not None else []
    dx0, gb1, (dsh1, dsc1, dg1, dn1) = _ffn_bwd(
        "ffn1", dx1, x, n1, sc1, g1, g4a, 0, saved1, hooks1 or _BwdHooks(), deps=deps1
    )

    dmod = jnp.concatenate([dsh1, dsc1, dg1, dsh2, dsc2, dg2, dsh3, dsc3, dg3], axis=-1)
    drel = _bias_reduce(dbias.reshape(SWA_HEADS, -1), bkt).T
    dsinks = jnp.sum(dsink, axis=(1, 2)).reshape(1, SWA_HEADS)
    small = (dn1, dn2, dn3, dnf, dq_norm, dkv_norm, dsinks, drel)
    return loss_part, dx0, (gb1, gb2, rest), dmod, small


_MESH = pl.DeviceIdType.MESH


def _place():
    return lax.axis_index("x"), lax.axis_index("y"), lax.axis_index("c")


def _other_chips(x, y):
    return [(1 - x, y), (x, 1 - y), (1 - x, 1 - y)]


def _allgather_small(name, blk):
    m_per, n = blk.shape

    def body(x_ref, out_ref, send_sems, recv_sems, local_sem):
        x, y, c = _place()
        me, sibling = (x, y, c), (x, y, 1 - c)
        chips = _other_chips(x, y)

        def rows(px, py, pc):
            return out_ref.at[pl.ds((4 * px + 2 * py + pc) * m_per, m_per), :]

        def copy(k, block, to, src=None):
            return pltpu.make_async_remote_copy(
                src_ref=rows(*block) if src is None else src, dst_ref=rows(*block),
                send_sem=send_sems.at[k], recv_sem=recv_sems.at[k], device_id=to, device_id_type=_MESH,
            )

        mine = pltpu.make_async_copy(x_ref, rows(*me), local_sem)
        mine.start()
        first = [copy(0, me, sibling, src=x_ref)]
        first += [copy(1 + j, me, (*chip, c), src=x_ref) for j, chip in enumerate(chips)]
        for cp in first:
            cp.start()
        passed = [copy(4 + j, (*chip, c), sibling) for j, chip in enumerate(chips)]
        for j, chip in enumerate(chips):
            copy(1 + j, (*chip, c), me).wait_recv()
            passed[j].start()
        copy(0, sibling, me).wait_recv()
        for j, chip in enumerate(chips):
            copy(4 + j, (*chip, 1 - c), me).wait_recv()
        for cp in first + passed:
            cp.wait_send()
        mine.wait()

    return pl.pallas_call(
        body,
        name=name,
        out_shape=_sds((N_DEV * m_per, n), blk.dtype),
        in_specs=[pl.BlockSpec(memory_space=pltpu.VMEM)],
        out_specs=pl.BlockSpec(memory_space=pltpu.VMEM),
        scratch_shapes=[pltpu.SemaphoreType.DMA((7,)), pltpu.SemaphoreType.DMA((7,)), pltpu.SemaphoreType.DMA],
    )(blk)


def _allgather_weights(name, own):
    n = own.shape[0]

    def body(own_ref, g_ref, token, send_sems, recv_sems, local_sem):
        x, y, c = _place()
        sibling = (x, y, 1 - c)
        chips = _other_chips(x, y)
        mine = pltpu.make_async_copy(own_ref, g_ref.at[2 * x + y], local_sem)
        mine.start()
        for j, chip in enumerate(chips):
            for cp in _gather_sends(n, own_ref, g_ref, send_sems.at[j], recv_sems.at[j], x, y, c, chip):
                cp.start()
        for j, chip in enumerate(chips):
            _gather_all(own_ref, g_ref, send_sems.at[j], recv_sems.at[j], chip, c, c).wait_recv()
            for cp in _gather_forwards(n, g_ref, send_sems.at[3 + j], recv_sems.at[3 + j], chip, c, sibling):
                cp.start()
        for j, chip in enumerate(chips):
            _gather_all(own_ref, g_ref, send_sems.at[3 + j], recv_sems.at[3 + j], chip, 1 - c, c).wait_recv()
        for j, chip in enumerate(chips):
            _gather_all(own_ref, g_ref, send_sems.at[j], recv_sems.at[j], chip, c, c).wait_send()
            _gather_all(own_ref, g_ref, send_sems.at[3 + j], recv_sems.at[3 + j], chip, c, c).wait_send()
        mine.wait()
        token[...] = jnp.zeros_like(token)

    return pl.pallas_call(
        body,
        name=name,
        out_shape=[_sds((N_CHIPS,) + own.shape, own.dtype), _sds((8, 128), F32)],
        in_specs=[pl.BlockSpec(memory_space=pl.ANY)],
        out_specs=[pl.BlockSpec(memory_space=pl.ANY), pl.BlockSpec(memory_space=pltpu.VMEM)],
        scratch_shapes=[pltpu.SemaphoreType.DMA((6,)), pltpu.SemaphoreType.DMA((6,)), pltpu.SemaphoreType.DMA],
    )(own)


def _gather_sends(n, own_ref, g_ref, send_sem, recv_sem, x, y, c, chip):
    return [
        pltpu.make_async_remote_copy(
            src_ref=own_ref.at[p, pl.ds(c * HALF, HALF), :], dst_ref=g_ref.at[2 * x + y, p, pl.ds(c * HALF, HALF), :],
            send_sem=send_sem, recv_sem=recv_sem, device_id=(*chip, c), device_id_type=_MESH,
        )
        for p in range(n)
    ]


def _gather_forwards(n, g_ref, send_sem, recv_sem, chip, c, sibling):
    return [
        pltpu.make_async_remote_copy(
            src_ref=g_ref.at[2 * chip[0] + chip[1], p, pl.ds(c * HALF, HALF), :],
            dst_ref=g_ref.at[2 * chip[0] + chip[1], p, pl.ds(c * HALF, HALF), :],
            send_sem=send_sem, recv_sem=recv_sem, device_id=sibling, device_id_type=_MESH,
        )
        for p in range(n)
    ]


def _gather_all(own_ref, g_ref, send_sem, recv_sem, chip, half, c):
    return pltpu.make_async_remote_copy(
        src_ref=own_ref.at[:, pl.ds(c * HALF, HALF), :],
        dst_ref=g_ref.at[2 * chip[0] + chip[1], :, pl.ds(half * HALF, HALF), :],
        send_sem=send_sem, recv_sem=recv_sem, device_id=(*chip, c), device_id_type=_MESH,
    )


_HBM_SPEC = pl.BlockSpec(memory_space=pltpu.HBM)
_SEM_SPEC = pl.BlockSpec(memory_space=pltpu.SEMAPHORE)
_ANY_SPEC = pl.BlockSpec(memory_space=pl.ANY)
_VMEM_SPEC = pl.BlockSpec(memory_space=pltpu.VMEM)
_SPLIT_PARAMS = pltpu.CompilerParams(has_side_effects=pltpu.SideEffectType.DATAFLOW_SIDE_EFFECTING)
_TOKEN = jax.ShapeDtypeStruct((8, 128), F32)


def _in_hbm(a):
    return pltpu.with_memory_space_constraint(a, pltpu.HBM)


class _GatherBehind:
    def __init__(self, tag, own, after):
        self.tag, self.n = tag, own.shape[0]
        n = self.n
        x, y, _ = _place()
        g_init = lax.dynamic_update_slice(
            lax.empty((N_CHIPS,) + own.shape, own.dtype), own[None], (2 * x + y, 0, 0, 0)
        )

        def body(own_ref, g_ref, *rest):
            send_sems, recv_sems, _, _, token = rest[len(after):]
            x, y, c = _place()
            for j, chip in enumerate(_other_chips(x, y)):
                for cp in _gather_sends(n, own_ref, g_ref, send_sems.at[j], recv_sems.at[j], x, y, c, chip):
                    cp.start()
            token[...] = jnp.zeros_like(token)

        self.send1, self.recv1, self.own, self.g, self.token = pl.pallas_call(
            body,
            name=f"{tag}_start",
            out_shape=(
                pltpu.SemaphoreType.DMA((3,)), pltpu.SemaphoreType.DMA((3,)),
                pltpu.HBM(own.shape, own.dtype), pltpu.HBM(g_init.shape, g_init.dtype), _TOKEN,
            ),
            in_specs=(_HBM_SPEC, _HBM_SPEC) + (_ANY_SPEC,) * len(after),
            out_specs=(_SEM_SPEC, _SEM_SPEC, _HBM_SPEC, _HBM_SPEC, _VMEM_SPEC),
            input_output_aliases={0: 2, 1: 3},
            compiler_params=_SPLIT_PARAMS,
        )(_in_hbm(own), _in_hbm(g_init), *after)

    def mid(self, after):
        n = self.n

        def body(own_ref, g_ref, send1, recv1, after_ref, send2, recv2, g_out, token):
            x, y, c = _place()
            sibling = (x, y, 1 - c)
            chips = _other_chips(x, y)
            for j, chip in enumerate(chips):
                _gather_all(own_ref, g_ref, send1.at[j], recv1.at[j], chip, c, c).wait_recv()
                for cp in _gather_forwards(n, g_ref, send2.at[j], recv2.at[j], chip, c, sibling):
                    cp.start()
            for j, chip in enumerate(chips):
                _gather_all(own_ref, g_ref, send1.at[j], recv1.at[j], chip, c, c).wait_send()
            token[...] = jnp.zeros_like(token)

        self.send2, self.recv2, self.g, token = pl.pallas_call(
            body,
            name=f"{self.tag}_mid",
            out_shape=(
                pltpu.SemaphoreType.DMA((3,)), pltpu.SemaphoreType.DMA((3,)),
                pltpu.HBM(self.g.shape, self.g.dtype), _TOKEN,
            ),
            in_specs=(_HBM_SPEC, _HBM_SPEC, _SEM_SPEC, _SEM_SPEC, _ANY_SPEC),
            out_specs=(_SEM_SPEC, _SEM_SPEC, _HBM_SPEC, _VMEM_SPEC),
            input_output_aliases={1: 2},
            compiler_params=_SPLIT_PARAMS,
        )(self.own, self.g, self.send1, self.recv1, after)
        return [token]

    def get(self, after):
        def body(g_ref, send2, recv2, after_ref, g_out):
            x, y, c = _place()
            for j, chip in enumerate(_other_chips(x, y)):
                slot_in = g_ref.at[2 * chip[0] + chip[1], :, pl.ds((1 - c) * HALF, HALF), :]
                slot_out = g_ref.at[2 * chip[0] + chip[1], :, pl.ds(c * HALF, HALF), :]
                cp = pltpu.make_async_remote_copy(
                    src_ref=slot_out, dst_ref=slot_in, send_sem=send2.at[j], recv_sem=recv2.at[j],
                    device_id=(x, y, 1 - c), device_id_type=_MESH,
                )
                cp.wait_send()
                cp.wait_recv()

        return pl.pallas_call(
            body,
            name=f"{self.tag}_wait",
            out_shape=pltpu.HBM(self.g.shape, self.g.dtype),
            in_specs=(_HBM_SPEC, _SEM_SPEC, _SEM_SPEC, _ANY_SPEC),
            out_specs=_HBM_SPEC,
            input_output_aliases={0: 0},
            compiler_params=_SPLIT_PARAMS,
        )(self.g, self.send2, self.recv2, after)


def _pair_exchange(name, gb):
    def body(gb_ref, land_ref, send_sem, recv_sem):
        x, y, c = _place()
        theirs = gb_ref.at[:, :, pl.ds((1 - c) * HALF, HALF), :]
        cp = pltpu.make_async_remote_copy(
            src_ref=theirs, dst_ref=land_ref, send_sem=send_sem, recv_sem=recv_sem,
            device_id=(x, y, 1 - c), device_id_type=_MESH,
        )
        cp.start()
        cp.wait()

    return pl.pallas_call(
        body,
        name=name,
        out_shape=_sds((N_CHIPS, gb.shape[1], HALF, D_MODEL), gb.dtype),
        in_specs=[pl.BlockSpec(memory_space=pl.ANY)],
        out_specs=pl.BlockSpec(memory_space=pl.ANY),
        scratch_shapes=[pltpu.SemaphoreType.DMA, pltpu.SemaphoreType.DMA],
    )(gb)


def _pair_sum(name, gb, land):
    def body(gb_ref, land_ref, o_ref):
        c = lax.axis_index("c")
        mine = gb_ref[pl.ds(pl.multiple_of(c * HALF, 16), HALF), :]
        o_ref[...] = (mine.astype(F32) + land_ref[...].astype(F32)).astype(o_ref.dtype)

    return pl.pallas_call(
        body,
        name=name,
        grid=(N_CHIPS, gb.shape[1]),
        in_specs=[
            pl.BlockSpec((None, None, F_SHARD, D_MODEL), lambda j, p: (j, p, 0, 0)),
            pl.BlockSpec((None, None, HALF, D_MODEL), lambda j, p: (j, p, 0, 0)),
        ],
        out_specs=pl.BlockSpec((None, None, HALF, D_MODEL), lambda j, p: (j, p, 0, 0)),
        out_shape=_sds(land.shape, BF16),
        compiler_params=_cparams(("parallel", "parallel")),
    )(gb, land)


def _chip_exchange(name, part):
    def body(p_ref, land_ref, send_sems, recv_sems, local_sem):
        x, y, c = _place()
        me = 2 * x + y
        chips = _other_chips(x, y)
        mine = pltpu.make_async_copy(p_ref.at[me], land_ref.at[me], local_sem)
        mine.start()

        def copy(k, chip):
            return pltpu.make_async_remote_copy(
                src_ref=p_ref.at[2 * chip[0] + chip[1]], dst_ref=land_ref.at[me],
                send_sem=send_sems.at[k], recv_sem=recv_sems.at[k], device_id=(*chip, c), device_id_type=_MESH,
            )

        sends = [copy(k, chip) for k, chip in enumerate(chips)]
        for cp in sends:
            cp.start()
        for k, chip in enumerate(chips):
            pltpu.make_async_remote_copy(
                src_ref=p_ref.at[me], dst_ref=land_ref.at[2 * chip[0] + chip[1]],
                send_sem=send_sems.at[k], recv_sem=recv_sems.at[k], device_id=(*chip, c), device_id_type=_MESH,
            ).wait_recv()
        for cp in sends:
            cp.wait_send()
        mine.wait()

    return pl.pallas_call(
        body,
        name=name,
        out_shape=_sds(part.shape, part.dtype),
        in_specs=[pl.BlockSpec(memory_space=pl.ANY)],
        out_specs=pl.BlockSpec(memory_space=pl.ANY),
        scratch_shapes=[pltpu.SemaphoreType.DMA((3,)), pltpu.SemaphoreType.DMA((3,)), pltpu.SemaphoreType.DMA],
    )(part)


def _chip_sum_share(name, land):
    n = land.shape[1]

    def body(l_ref, r_ref, buf, send_sems, recv_sems, local_sems):
        p = pl.program_id(0)
        x, y, c = _place()
        acc = l_ref[0].astype(F32)
        for j in range(1, N_CHIPS):
            acc = acc + l_ref[j].astype(F32)
        buf[p] = acc

        def copies(q):
            mine = r_ref.at[q, pl.ds(c * HALF, HALF), :]
            theirs = r_ref.at[q, pl.ds((1 - c) * HALF, HALF), :]
            local = pltpu.make_async_copy(buf.at[q], mine, local_sems.at[q])
            out = pltpu.make_async_remote_copy(
                src_ref=buf.at[q], dst_ref=mine, send_sem=send_sems.at[q], recv_sem=recv_sems.at[q],
                device_id=(x, y, 1 - c), device_id_type=_MESH,
            )
            arrive = pltpu.make_async_remote_copy(
                src_ref=buf.at[q], dst_ref=theirs, send_sem=send_sems.at[q], recv_sem=recv_sems.at[q],
                device_id=(x, y, 1 - c), device_id_type=_MESH,
            )
            return local, out, arrive

        local, out, _ = copies(p)
        local.start()
        out.start()

        @pl.when(p == n - 1)
        def _():
            for q in range(n):
                local, out, arrive = copies(q)
                arrive.wait_recv()
                out.wait_send()
                local.wait()

    return pl.pallas_call(
        body,
        name=name,
        grid=(n,),
        in_specs=[pl.BlockSpec((N_CHIPS, None, HALF, D_MODEL), lambda p: (0, p, 0, 0))],
        out_specs=pl.BlockSpec(memory_space=pl.ANY),
        out_shape=_sds((n, F_SHARD, D_MODEL), F32),
        scratch_shapes=[
            pltpu.VMEM((n, HALF, D_MODEL), F32),
            pltpu.SemaphoreType.DMA((n,)), pltpu.SemaphoreType.DMA((n,)), pltpu.SemaphoreType.DMA((n,)),
        ],
        compiler_params=_cparams(("arbitrary",)),
    )(land)


class _ReduceBehind:
    def __init__(self, tag, gb, after=()):
        self.tag = tag
        n = gb.shape[1]
        land = lax.empty((N_CHIPS, n, HALF, D_MODEL), gb.dtype)

        def body(gb_ref, land_ref, *rest):
            send_sem, recv_sem, _, _, token = rest[len(after):]
            self._pair_copy(gb_ref, land_ref, send_sem, recv_sem).start()
            token[...] = jnp.zeros_like(token)

        self.send, self.recv, self.gb, self.land, self.token = pl.pallas_call(
            body,
            name=f"grads_pair_start_{tag}",
            out_shape=(
                pltpu.SemaphoreType.DMA((1,)), pltpu.SemaphoreType.DMA((1,)),
                pltpu.HBM(gb.shape, gb.dtype), pltpu.HBM(land.shape, land.dtype), _TOKEN,
            ),
            in_specs=(_HBM_SPEC, _HBM_SPEC) + (_ANY_SPEC,) * len(after),
            out_specs=(_SEM_SPEC, _SEM_SPEC, _HBM_SPEC, _HBM_SPEC, _VMEM_SPEC),
            input_output_aliases={0: 2, 1: 3},
            compiler_params=_SPLIT_PARAMS,
        )(_in_hbm(gb), _in_hbm(land), *after)

    @staticmethod
    def _pair_copy(gb_ref, land_ref, send_sem, recv_sem):
        x, y, c = _place()
        return pltpu.make_async_remote_copy(
            src_ref=gb_ref.at[:, :, pl.ds((1 - c) * HALF, HALF), :], dst_ref=land_ref,
            send_sem=send_sem.at[0], recv_sem=recv_sem.at[0], device_id=(x, y, 1 - c), device_id_type=_MESH,
        )

    @staticmethod
    def _chip_copies(p_ref, land_ref, send_sems, recv_sems):
        x, y, c = _place()
        me = 2 * x + y
        sends, arrivals = [], []
        for k, chip in enumerate(_other_chips(x, y)):
            them = 2 * chip[0] + chip[1]
            sends.append(pltpu.make_async_remote_copy(
                src_ref=p_ref.at[them], dst_ref=land_ref.at[me], send_sem=send_sems.at[k], recv_sem=recv_sems.at[k],
                device_id=(*chip, c), device_id_type=_MESH,
            ))
            arrivals.append(pltpu.make_async_remote_copy(
                src_ref=p_ref.at[me], dst_ref=land_ref.at[them], send_sem=send_sems.at[k], recv_sem=recv_sems.at[k],
                device_id=(*chip, c), device_id_type=_MESH,
            ))
        return sends, arrivals

    def mid(self, after):
        tag = self.tag

        def wait_body(gb_ref, land_ref, send_sem, recv_sem, after_ref, gb_out, land_out):
            cp = self._pair_copy(gb_ref, land_ref, send_sem, recv_sem)
            cp.wait_send()
            cp.wait_recv()

        gb, land = pl.pallas_call(
            wait_body,
            name=f"grads_pair_wait_{tag}",
            out_shape=(pltpu.HBM(self.gb.shape, self.gb.dtype), pltpu.HBM(self.land.shape, self.land.dtype)),
            in_specs=(_HBM_SPEC, _HBM_SPEC, _SEM_SPEC, _SEM_SPEC, _ANY_SPEC),
            out_specs=(_HBM_SPEC, _HBM_SPEC),
            input_output_aliases={0: 0, 1: 1},
            compiler_params=_SPLIT_PARAMS,
        )(self.gb, self.land, self.send, self.recv, after)
        part = _pair_sum(f"grads_pair_sum_{tag}", gb, land)

        x, y, _ = _place()
        start = (2 * x + y, 0, 0, 0)
        own = lax.dynamic_slice(part, start, (1,) + part.shape[1:])
        land2 = lax.dynamic_update_slice(lax.empty(part.shape, part.dtype), own, start)

        def start_body(p_ref, land_ref, send_sems, recv_sems, p_out, land_out, token):
            for cp in self._chip_copies(p_ref, land_ref, send_sems, recv_sems)[0]:
                cp.start()
            token[...] = jnp.zeros_like(token)

        self.send2, self.recv2, self.part, self.land2, token = pl.pallas_call(
            start_body,
            name=f"grads_chip_start_{tag}",
            out_shape=(
                pltpu.SemaphoreType.DMA((3,)), pltpu.SemaphoreType.DMA((3,)),
                pltpu.HBM(part.shape, part.dtype), pltpu.HBM(land2.shape, land2.dtype), _TOKEN,
            ),
            in_specs=(_HBM_SPEC, _HBM_SPEC),
            out_specs=(_SEM_SPEC, _SEM_SPEC, _HBM_SPEC, _HBM_SPEC, _VMEM_SPEC),
            input_output_aliases={0: 2, 1: 3},
            compiler_params=_SPLIT_PARAMS,
        )(_in_hbm(part), _in_hbm(land2))
        return [token]

    def get(self, after):
        n_after = len(after)

        def wait_body(p_ref, land_ref, send_sems, recv_sems, *rest):
            sends, arrivals = self._chip_copies(p_ref, land_ref, send_sems, recv_sems)
            for cp in arrivals:
                cp.wait_recv()
            for cp in sends:
                cp.wait_send()

        _, land2 = pl.pallas_call(
            wait_body,
            name=f"grads_chip_wait_{self.tag}",
            out_shape=(pltpu.HBM(self.part.shape, self.part.dtype), pltpu.HBM(self.land2.shape, self.land2.dtype)),
            in_specs=(_HBM_SPEC, _HBM_SPEC, _SEM_SPEC, _SEM_SPEC) + (_ANY_SPEC,) * n_after,
            out_specs=(_HBM_SPEC, _HBM_SPEC),
            input_output_aliases={0: 0, 1: 1},
            compiler_params=_SPLIT_PARAMS,
        )(self.part, self.land2, self.send2, self.recv2, *after)
        return _chip_sum_share(f"grads_chip_sum_share_{self.tag}", land2)


def _allreduce_small(blk):
    gathered = _allgather_small("allgather_small_grads", blk).reshape(N_DEV, PK_ROWS, 128)

    def body(g_ref, o_ref):
        acc = g_ref[0]
        for k in range(1, N_DEV):
            acc = acc + g_ref[k]
        o_ref[...] = acc

    total = pl.pallas_call(body, name="small_grads_sum", out_shape=_sds((PK_ROWS, 128), F32))(gathered)
    return gathered, total


def _adamw(name, w, g, m, v):
    rows, cols = w.shape
    tm = rows
    while tm * cols * 4 > (1 << 20) and tm % 16 == 0:
        tm //= 2

    def fn(w_, g_, m_, v_):
        m_new = ADAM_B1 * m_ + (1.0 - ADAM_B1) * g_
        v_new = ADAM_B2 * v_ + (1.0 - ADAM_B2) * (g_ * g_)
        m_hat = m_new / (1.0 - ADAM_B1 ** ADAM_STEP)
        v_hat = v_new / (1.0 - ADAM_B2 ** ADAM_STEP)
        delta = -ADAM_LR * (m_hat / (jnp.sqrt(v_hat) + ADAM_EPS) + ADAM_WD * w_)
        return delta, m_new, v_new

    out = _sds(w.shape, F32)
    return _rowwise(name, fn, [w, g, m, v], [], [out, out, out], [], tm)


def _pack_small(b_mod_like, n1, n2, n3, nf, qn, kvn, sinks, rel):
    parts = [
        b_mod_like.reshape(PK_MOD, 128),
        n1.reshape(8, 128), n2.reshape(8, 128), n3.reshape(8, 128), nf.reshape(8, 128),
        qn.reshape(2, 128), kvn.reshape(1, 128),
        jnp.pad(sinks.reshape(1, SWA_HEADS), ((0, 0), (0, 128 - SWA_HEADS))),
        rel.reshape(2, 128),
        jnp.zeros((2, 128), F32),
    ]
    return jnp.concatenate(parts, axis=0)


def _unpack_small(p):
    o = PK_MOD
    return (
        p[:o].reshape(1, N_MOD * D_MODEL),
        p[o:o + 8].reshape(1, D_MODEL), p[o + 8:o + 16].reshape(1, D_MODEL),
        p[o + 16:o + 24].reshape(1, D_MODEL), p[o + 24:o + 32].reshape(D_MODEL),
        p[o + 32:o + 34].reshape(1, MLA_Q_RANK), p[o + 34:o + 35].reshape(1, MLA_KV_RANK),
        p[o + 35:o + 36, :SWA_HEADS].reshape(1, SWA_HEADS),
        p[o + 36:o + 38].reshape(NUM_BUCKETS, SWA_HEADS),
    )


def kernel(x, c, w_mod, b_mod, norm_ffn1, ffn1_gate, ffn1_up, ffn1_down, norm_mix, w_in, q_norm, kv_norm, w_uq, w_ukv, sinks, w_o, norm_ffn2, ffn2_gate, ffn2_up, ffn2_down, rel_bias, norm_final, loss_target, m_w_mod, m_b_mod, m_norm_ffn1, m_ffn1_gate, m_ffn1_up, m_ffn1_down, m_norm_mix, m_w_in, m_q_norm, m_kv_norm, m_w_uq, m_w_ukv, m_sinks, m_w_o, m_norm_ffn2, m_ffn2_gate, m_ffn2_up, m_ffn2_down, m_rel_bias, m_norm_final, v_w_mod, v_b_mod, v_norm_ffn1, v_ffn1_gate, v_ffn1_up, v_ffn1_down, v_norm_mix, v_w_in, v_q_norm, v_kv_norm, v_w_uq, v_w_ukv, v_sinks, v_w_o, v_norm_ffn2, v_ffn2_gate, v_ffn2_up, v_ffn2_down, v_rel_bias, v_norm_final):
    ax, ay, ac = _place()
    chip = 2 * ax + ay
    dev = 2 * chip + ac
    n_mod_shard = N_MOD * D_MODEL // N_CHIPS

    def t16(w):
        return w[0].T.astype(BF16)

    rest = jnp.concatenate(
        [
            t16(w_in),
            w_o[0].astype(BF16),
            t16(w_uq).reshape(R_UQ, D_MODEL),
            t16(w_ukv).reshape(R_UKV, D_MODEL),
            jnp.zeros((F_SHARD - O_PAD, D_MODEL), BF16),
        ],
        axis=0,
    )
    own_a = jnp.stack([t16(ffn1_gate), t16(ffn1_up), ffn1_down[0].astype(BF16)])
    own_b = jnp.stack([t16(ffn2_gate), t16(ffn2_up), ffn2_down[0].astype(BF16), rest])
    g4a, gathered_a = _allgather_weights("allgather_weights_ffn1", own_a)

    (c_act,) = _rowwise(
        "silu_c", lambda v: v * _sigmoid(v), [c.reshape(8, 128)], [], [_sds((8, 128), F32)], [], 8
    )
    c_all = _allgather_small("allgather_c", c_act).reshape(N_DEV, D_MODEL)
    mod_cols = _mm(
        "mod_fwd", "nn", (1, n_mod_shard // 768, 1),
        c_all, (N_DEV, D_MODEL), lambda i, j, k: (0, 0),
        w_mod[0], (D_MODEL, 768), lambda i, j, k: (0, j),
        _sds((N_DEV, n_mod_shard), F32), (N_DEV, 768), lambda i, j, k: (0, j),
        (N_DEV, 768),
    )
    mod_all = _allgather_small("allgather_mod", mod_cols).reshape(N_CHIPS, 2, N_DEV, n_mod_shard)[:, 0]
    mod_all = mod_all.transpose(1, 0, 2).reshape(N_DEV, N_MOD * D_MODEL)
    mod = lax.dynamic_slice_in_dim(mod_all, dev, 1, axis=0) + b_mod

    norms = (norm_ffn1, norm_mix, norm_ffn2, norm_final.reshape(1, D_MODEL))
    rest_weights = _GatherBehind("gather_rest", own_b, after=[gathered_a, mod_all])

    reducing, red = {}, {}

    def ffn2_grads_done(gb2):
        reducing["ffn2"] = _ReduceBehind("ffn2", gb2)
        return [reducing["ffn2"].token]

    def mix_done(dx1, gb_rest):
        red["ffn2"] = reducing["ffn2"].get([dx1])
        reducing["rest"] = _ReduceBehind("rest", gb_rest, after=[red["ffn2"]])
        return [reducing["rest"].token]

    def ffn1_grads_done(gb1):
        red["rest"] = reducing["rest"].get([gb1])
        reducing["ffn1"] = _ReduceBehind("ffn1", gb1, after=[red["rest"]])
        return [reducing["ffn1"].token]

    loss_part, grad_x, _, dmod, small = _device_step(
        x[0], loss_target[0], mod, g4a, rest_weights, norms, q_norm, kv_norm, sinks, rel_bias,
        first_deps=[rest_weights.token],
        hooks2=_BwdHooks(after_wgu=ffn2_grads_done),
        hooks_mix=_BwdHooks(after_gate=lambda dz: reducing["ffn2"].mid(dz)),
        mix_done=mix_done,
        hooks1=_BwdHooks(
            after_swiglu=lambda dab3: reducing["rest"].mid(dab3),
            after_wgu=ffn1_grads_done,
            after_dh=lambda dh: reducing["ffn1"].mid(dh),
        ),
    )
    loss = lax.psum(loss_part, ("x", "y", "c"))

    gathered, total = _allreduce_small(_pack_small(dmod, *small))
    (g_b_mod, g_n1, g_n2, g_n3, g_nf, g_qn, g_kvn, g_sinks, g_rel) = _unpack_small(total)
    dmod_all = gathered[:, :PK_MOD].reshape(N_DEV, N_MOD * D_MODEL)
    dmod_cols = lax.dynamic_slice_in_dim(dmod_all, chip * n_mod_shard, n_mod_shard, axis=1)
    g_w_mod = _mm(
        "mod_bwd", "tn", (1, n_mod_shard // 768, 1),
        c_all, (N_DEV, D_MODEL), lambda i, j, k: (0, 0),
        dmod_cols, (N_DEV, 768), lambda i, j, k: (0, j),
        _sds((D_MODEL, n_mod_shard), F32), (D_MODEL, 768), lambda i, j, k: (0, j),
        (D_MODEL, 768),
    )

    red2, r_rest = red["ffn2"], red["rest"][0]
    g_big = {
        "ffn2_gate": red2[0].T, "ffn2_up": red2[1].T, "ffn2_down": red2[2],
        "w_in": r_rest[O_IN:O_IN + R_IN].T,
        "w_o": r_rest[O_O:O_O + R_O],
        "w_uq": r_rest[O_UQ:O_UQ + R_UQ].reshape(MLA_QK, MLA_Q_RANK).T,
        "w_ukv": r_rest[O_UKV:O_UKV + R_UKV].reshape(MLA_NOPE + MLA_V, MLA_KV_RANK).T,
        "w_mod": g_w_mod,
    }
    w_big = {
        "ffn2_gate": (ffn2_gate, m_ffn2_gate, v_ffn2_gate),
        "ffn2_up": (ffn2_up, m_ffn2_up, v_ffn2_up), "ffn2_down": (ffn2_down, m_ffn2_down, v_ffn2_down),
        "w_in": (w_in, m_w_in, v_w_in), "w_o": (w_o, m_w_o, v_w_o), "w_uq": (w_uq, m_w_uq, v_w_uq),
        "w_ukv": (w_ukv, m_w_ukv, v_w_ukv), "w_mod": (w_mod, m_w_mod, v_w_mod),
    }
    grads, deltas, new_m, new_v = {}, {}, {}, {}

    def update(names):
        for nm in names:
            w, m, v = w_big[nm]
            d_, m_, v_ = _adamw(f"adamw_{nm}", w[0], g_big[nm], m[0], v[0])
            grads[nm], deltas[nm], new_m[nm], new_v[nm] = g_big[nm][None], d_[None], m_[None], v_[None]

    update(list(w_big))
    red1 = reducing["ffn1"].get([deltas[nm] for nm in w_big])
    g_big.update({"ffn1_gate": red1[0].T, "ffn1_up": red1[1].T, "ffn1_down": red1[2]})
    w_big.update({
        "ffn1_gate": (ffn1_gate, m_ffn1_gate, v_ffn1_gate), "ffn1_up": (ffn1_up, m_ffn1_up, v_ffn1_up),
        "ffn1_down": (ffn1_down, m_ffn1_down, v_ffn1_down),
    })
    update(["ffn1_gate", "ffn1_up", "ffn1_down"])

    small_names = ["b_mod", "norm_ffn1", "norm_mix", "norm_ffn2", "norm_final", "q_norm", "kv_norm", "sinks", "rel_bias"]
    w_small = (b_mod, norm_ffn1, norm_mix, norm_ffn2, norm_final, q_norm, kv_norm, sinks, rel_bias)
    m_small = (m_b_mod, m_norm_ffn1, m_norm_mix, m_norm_ffn2, m_norm_final, m_q_norm, m_kv_norm, m_sinks, m_rel_bias)
    v_small = (v_b_mod, v_norm_ffn1, v_norm_mix, v_norm_ffn2, v_norm_final, v_q_norm, v_kv_norm, v_sinks, v_rel_bias)
    d_p, m_p, v_p = _adamw("adamw_small", _pack_small(*w_small), total, _pack_small(*m_small), _pack_small(*v_small))
    for nm, g_, d_, m_, v_ in zip(
        small_names,
        (g_b_mod, g_n1, g_n2, g_n3, g_nf, g_qn, g_kvn, g_sinks, g_rel),
        _unpack_small(d_p), _unpack_small(m_p), _unpack_small(v_p),
    ):
        grads[nm], deltas[nm], new_m[nm], new_v[nm] = g_, d_, m_, v_

    order = ["w_mod", "b_mod", "norm_ffn1", "ffn1_gate", "ffn1_up", "ffn1_down", "norm_mix", "w_in", "q_norm", "kv_norm",
             "w_uq", "w_ukv", "sinks", "w_o", "norm_ffn2", "ffn2_gate", "ffn2_up", "ffn2_down", "rel_bias", "norm_final"]
    return (loss, grad_x[None], *[grads[n] for n in order], *[deltas[n] for n in order],
            *[new_m[n] for n in order], *[new_v[n] for n in order])
```

```python
import functools
import math

import jax
import jax.numpy as jnp
import numpy as np
from jax import lax
from jax.experimental import pallas as pl
from jax.experimental.pallas import tpu as pltpu

F32, BF16 = jnp.float32, jnp.bfloat16

D_MODEL = 1024
D_FF = 2816
EPS = 1e-6
N_MOD = 9
SWA_HEADS, SWA_KV_HEADS, SWA_HEAD_DIM, WINDOW = 8, 2, 64, 128
SWA_GROUP = SWA_HEADS // SWA_KV_HEADS
MLA_HEADS, MLA_Q_RANK, MLA_KV_RANK, MLA_NOPE, MLA_ROPE, MLA_V = 4, 256, 128, 128, 64, 128
MLA_QK = MLA_NOPE + MLA_ROPE
ROPE_THETA = 10000.0
NUM_BUCKETS, MAX_DISTANCE = 32, 128
D_IN = 1216
N_SWA_COLS = 768
N_SWA_BLOCKS = N_SWA_COLS // SWA_HEAD_DIM
N_MLA_COLS = 512

ADAM_LR, ADAM_B1, ADAM_B2, ADAM_EPS, ADAM_WD, ADAM_STEP = 0.001, 0.9, 0.999, 1e-08, 0.01, 10

N_CHIPS = 4
N_DEV = 8
F_SHARD = D_FF // N_CHIPS
HALF = F_SHARD // 2
R_IN, R_O, R_UQ, R_UKV = 304, 256, 48, 32
O_IN, O_O, O_UQ, O_UKV, O_PAD = 0, 304, 560, 608, 640

PK_MOD = 72
PK_ROWS = 112
VMEM_LIMIT = 56 << 20
ROW_TILE = 512

_DN = {
    "nn": (((1,), (0,)), ((), ())),
    "nt": (((1,), (1,)), ((), ())),
    "tn": (((0,), (0,)), ((), ())),
}


def _sds(shape, dtype):
    return jax.ShapeDtypeStruct(tuple(shape), dtype)


def _cparams(sem=None):
    kw = {"vmem_limit_bytes": VMEM_LIMIT}
    if sem is not None:
        kw["dimension_semantics"] = sem
    return pltpu.CompilerParams(**kw)


def _dot(a, b, dims):
    return lax.dot_general(a.astype(BF16), b.astype(BF16), _DN[dims], preferred_element_type=F32)


def _mm(name, dims, grid, a, a_blk, a_idx, b, b_blk, b_idx, out, out_blk, out_idx, acc_shape, alias=None, deps=()):
    nk = grid[2]

    def body(*refs):
        a_ref, b_ref = refs[:2]
        o_ref, acc_ref = refs[-2:]
        k = pl.program_id(2)

        @pl.when(k == 0)
        def _():
            acc_ref[...] = jnp.zeros_like(acc_ref)

        acc_ref[...] += _dot(a_ref[...], b_ref[...], dims)

        @pl.when(k == nk - 1)
        def _():
            o_ref[...] = acc_ref[...].astype(o_ref.dtype)

    in_specs = [pl.BlockSpec(a_blk, a_idx), pl.BlockSpec(b_blk, b_idx)]
    args = [a, b]
    kw = {}
    if alias is not None:
        in_specs.append(pl.BlockSpec(memory_space=pl.ANY))
        args.append(alias)
        kw["input_output_aliases"] = {2: 0}
    in_specs += [pl.BlockSpec(memory_space=pl.ANY)] * len(deps)
    args += list(deps)
    return pl.pallas_call(
        body,
        name=name,
        grid=grid,
        in_specs=in_specs,
        out_specs=pl.BlockSpec(out_blk, out_idx),
        out_shape=out,
        scratch_shapes=[pltpu.VMEM(acc_shape, F32)],
        compiler_params=_cparams(("parallel", "parallel", "arbitrary")),
        **kw,
    )(*args)


def _rowwise(name, fn, tiled, full, out_tiled, out_red, tm, deps=()):
    rows = tiled[0].shape[-2]
    grid = (rows // tm,)
    n_in = len(tiled) + len(full)
    n_t = len(out_tiled)
    n_dep = len(deps)

    def tspec(shape):
        nd = len(shape)
        return pl.BlockSpec(tuple(shape[:-2]) + (tm, shape[-1]), lambda i, nd=nd: (0,) * (nd - 2) + (i, 0))

    def fspec(shape):
        nd = len(shape)
        return pl.BlockSpec(tuple(shape), lambda i, nd=nd: (0,) * nd)

    def body(*refs):
        outs = fn(*[r[...] for r in refs[:n_in]])
        if not isinstance(outs, (tuple, list)):
            outs = (outs,)
        out_refs = refs[n_in + n_dep:]
        for r, v in zip(out_refs[:n_t], outs[:n_t]):
            r[...] = v.astype(r.dtype)
        red_refs = out_refs[n_t:]
        if red_refs:
            @pl.when(pl.program_id(0) == 0)
            def _():
                for r in red_refs:
                    r[...] = jnp.zeros_like(r)

            for r, v in zip(red_refs, outs[n_t:]):
                r[...] += v.astype(r.dtype)

    res = pl.pallas_call(
        body,
        name=name,
        grid=grid,
        in_specs=[tspec(a.shape) for a in tiled] + [fspec(a.shape) for a in full]
        + [pl.BlockSpec(memory_space=pl.ANY)] * n_dep,
        out_specs=[tspec(o.shape) for o in out_tiled] + [fspec(o.shape) for o in out_red],
        out_shape=list(out_tiled) + list(out_red),
        compiler_params=_cparams(("arbitrary",) if out_red else ("parallel",)),
    )(*tiled, *full, *deps)
    return res


def _sum0(v):
    return jnp.sum(v, axis=0, keepdims=True)


def _sigmoid(v):
    return 1.0 / (1.0 + jnp.exp(-v))


def _rms(x):
    r = lax.rsqrt(jnp.mean(x * x, axis=-1, keepdims=True) + EPS)
    return x * r, r


def _rms_bwd(u, xr, r):
    return r * (u - xr * jnp.mean(u * xr, axis=-1, keepdims=True))


class _Ready:
    def __init__(self, g):
        self.g = g

    def mid(self, after):
        return []

    def get(self, after):
        return self.g


def _ffn_fwd(tag, x, gamma, sh, sc, gate, weights, base, mid_after):
    s_len = x.shape[0]
    deps = weights.mid(mid_after)

    def normmod(x_, gamma_, sc_, sh_):
        xr, _ = _rms(x_)
        return xr * gamma_ * (1.0 + sc_) + sh_

    (h,) = _rowwise(f"{tag}_normmod", normmod, [x], [gamma, sc, sh], [_sds((s_len, D_MODEL), BF16)], [], 256, deps=deps)
    g4 = weights.get(h)

    tm = min(ROW_TILE, s_len)
    ab3 = _mm(
        f"{tag}_gate_up", "nt", (s_len // tm, 2 * N_CHIPS, 1),
        h, (tm, D_MODEL), lambda i, j, k: (i, 0),
        g4, (None, None, F_SHARD, D_MODEL), lambda i, j, k: (j % N_CHIPS, base + j // N_CHIPS, 0, 0),
        _sds((2 * N_CHIPS, s_len, F_SHARD), BF16), (None, tm, F_SHARD), lambda i, j, k: (j, i, 0),
        (tm, F_SHARD),
    )

    def swiglu(ab):
        a = ab[:N_CHIPS].astype(F32)
        b = ab[N_CHIPS:].astype(F32)
        return a * _sigmoid(a) * b

    (s3,) = _rowwise(f"{tag}_swiglu", swiglu, [ab3], [], [_sds((N_CHIPS, s_len, F_SHARD), BF16)], [], 128)

    y = _mm(
        f"{tag}_down", "nn", (s_len // tm, 1, N_CHIPS),
        s3, (None, tm, F_SHARD), lambda i, j, k: (k, i, 0),
        g4, (None, None, F_SHARD, D_MODEL), lambda i, j, k: (k, base + 2, 0, 0),
        _sds((s_len, D_MODEL), F32), (tm, D_MODEL), lambda i, j, k: (i, 0),
        (tm, D_MODEL),
    )

    (x_out,) = _rowwise(
        f"{tag}_residual", lambda x_, y_, g_: x_ + 0.5 * g_ * y_, [x, y], [gate], [_sds((s_len, D_MODEL), F32)], [], 256
    )
    return x_out, (g4, h, ab3, s3, y)


def _normmod_bwd_call(name, dh_parts, x, dxo, gamma, sc, deps=()):
    s_len = x.shape[0]
    n_parts = len(dh_parts)

    def fn(*vals):
        dh = vals[0]
        for extra in vals[1:n_parts]:
            dh = dh + extra
        x_, dxo_, gamma_, sc_ = vals[n_parts:]
        xr, r = _rms(x_)
        n = xr * gamma_
        dn = dh * (1.0 + sc_)
        dx = dxo_ + _rms_bwd(dn * gamma_, xr, r)
        return dx, _sum0(dh * n), _sum0(dh), _sum0(dn * xr)

    vec = _sds((1, D_MODEL), F32)
    return _rowwise(
        name, fn, list(dh_parts) + [x, dxo], [gamma, sc], [_sds((s_len, D_MODEL), F32)], [vec, vec, vec], 256, deps=deps
    )


class _BwdHooks:
    def __init__(self, after_gate=None, after_swiglu=None, after_wgu=None, after_dh=None):
        def nothing(_):
            return []

        self.after_gate = after_gate or nothing
        self.after_swiglu = after_swiglu or nothing
        self.after_wgu = after_wgu or nothing
        self.after_dh = after_dh or nothing


def _ffn_bwd(tag, dxo, x, gamma, sc, gate, base, saved, hooks, deps=()):
    g4, h, ab3, s3, y = saved
    s_len = x.shape[0]
    vec = _sds((1, D_MODEL), F32)

    dy, dgate = _rowwise(
        f"{tag}_bwd_gate", lambda dxo_, y_, g_: (0.5 * g_ * dxo_, _sum0(0.5 * y_ * dxo_)),
        [dxo, y], [gate], [_sds((s_len, D_MODEL), BF16)], [vec], 256, deps=deps,
    )

    tm = min(ROW_TILE, s_len)
    ds3 = _mm(
        f"{tag}_bwd_ds", "nt", (s_len // tm, N_CHIPS, 1),
        dy, (tm, D_MODEL), lambda i, j, k: (i, 0),
        g4, (None, None, F_SHARD, D_MODEL), lambda i, j, k: (j, base + 2, 0, 0),
        _sds((N_CHIPS, s_len, F_SHARD), BF16), (None, tm, F_SHARD), lambda i, j, k: (j, i, 0),
        (tm, F_SHARD),
    )

    def swiglu_bwd(ab, ds):
        a = ab[:N_CHIPS].astype(F32)
        b = ab[N_CHIPS:].astype(F32)
        ds = ds.astype(F32)
        sig = _sigmoid(a)
        da = ds * b * sig * (1.0 + a * (1.0 - sig))
        db = ds * a * sig
        return jnp.concatenate([da, db], axis=0)

    (dab3,) = _rowwise(
        f"{tag}_bwd_swiglu", swiglu_bwd, [ab3, ds3], [], [_sds((2 * N_CHIPS, s_len, F_SHARD), BF16)], [], 128
    )

    tk = tm
    gb = _mm(
        f"{tag}_bwd_wdown", "tn", (N_CHIPS, 1, s_len // tk),
        s3, (None, tk, F_SHARD), lambda i, j, k: (i, k, 0),
        dy, (tk, D_MODEL), lambda i, j, k: (k, 0),
        _sds((N_CHIPS, 3, F_SHARD, D_MODEL), BF16), (None, None, F_SHARD, D_MODEL), lambda i, j, k: (i, 2, 0, 0),
        (F_SHARD, D_MODEL), deps=hooks.after_swiglu(dab3),
    )
    gb = _mm(
        f"{tag}_bwd_wgu", "tn", (2 * N_CHIPS, 1, s_len // tk),
        dab3, (None, tk, F_SHARD), lambda i, j, k: (i, k, 0),
        h, (tk, D_MODEL), lambda i, j, k: (k, 0),
        _sds(gb.shape, BF16), (None, None, F_SHARD, D_MODEL), lambda i, j, k: (i % N_CHIPS, i // N_CHIPS, 0, 0),
        (F_SHARD, D_MODEL), alias=gb,
    )
    dh = _mm(
        f"{tag}_bwd_dh", "nn", (s_len // tm, 1, 2 * N_CHIPS),
        dab3, (None, tm, F_SHARD), lambda i, j, k: (k, i, 0),
        g4, (None, None, F_SHARD, D_MODEL), lambda i, j, k: (k % N_CHIPS, base + k // N_CHIPS, 0, 0),
        _sds((s_len, D_MODEL), F32), (tm, D_MODEL), lambda i, j, k: (i, 0),
        (tm, D_MODEL), deps=hooks.after_wgu(gb),
    )
    dx, dsc, dsh, dgamma = _normmod_bwd_call(
        f"{tag}_bwd_normmod", [dh], x, dxo, gamma, sc, deps=hooks.after_dh(dh)
    )
    return dx, gb, (dsh, dsc, dgate, dgamma)


def _bucket_map():
    qi = np.arange(WINDOW)[:, None]
    kj = np.arange(2 * WINDOW)[None, :]
    dist = qi + WINDOW - kj
    band = (dist >= 0) & (dist < WINDOW)
    max_exact = NUM_BUCKETS // 2
    n = np.maximum(dist, 0)
    large = []
    for dt in (np.float32, np.float64):
        nf = np.maximum(n, 1).astype(dt)
        val = np.log(nf / dt(max_exact)) / dt(math.log(MAX_DISTANCE / max_exact)) * dt(NUM_BUCKETS - max_exact)
        large.append(np.minimum(max_exact + val.astype(np.int32), NUM_BUCKETS - 1))
    assert np.array_equal(large[0], large[1])
    bucket = np.where(n < max_exact, n, large[0])
    return np.where(band, bucket, -1).astype(np.int32).reshape(1, -1)


def _bias_expand(rel_bias_t, bkt):
    n = bkt.shape[1]

    def body(rb_ref, bkt_ref, o_ref):
        onehot = (lax.broadcasted_iota(jnp.int32, (NUM_BUCKETS, n), 0) == bkt_ref[...]).astype(F32)
        o_ref[...] = lax.dot_general(
            rb_ref[...], onehot, _DN["nn"], precision=lax.Precision.HIGHEST, preferred_element_type=F32
        )

    return pl.pallas_call(
        body, name="bias_expand", out_shape=_sds((SWA_HEADS, n), F32), compiler_params=_cparams()
    )(rel_bias_t, bkt)


def _bias_reduce(dbias_flat, bkt):
    n = bkt.shape[1]

    def body(db_ref, bkt_ref, o_ref):
        onehot = (lax.broadcasted_iota(jnp.int32, (NUM_BUCKETS, n), 0) == bkt_ref[...]).astype(F32)
        o_ref[...] = lax.dot_general(
            db_ref[...], onehot, _DN["nt"], precision=lax.Precision.HIGHEST, preferred_element_type=F32
        )

    return pl.pallas_call(
        body, name="bias_reduce", out_shape=_sds((SWA_HEADS, NUM_BUCKETS), F32), compiler_params=_cparams()
    )(dbias_flat, bkt)


_SWA_SCALE = SWA_HEAD_DIM ** -0.5
_NEG = -1e30


def _swa_in_specs():
    g, w, hd = SWA_GROUP, WINDOW, SWA_HEAD_DIM
    prev = lambda n: jnp.maximum(n - 1, 0)
    return [
        pl.BlockSpec((g, w, hd), lambda j, n: (j, n, 0)),
        pl.BlockSpec((1, w, hd), lambda j, n: (SWA_HEADS + j, prev(n), 0)),
        pl.BlockSpec((1, w, hd), lambda j, n: (SWA_HEADS + j, n, 0)),
        pl.BlockSpec((1, w, hd), lambda j, n: (SWA_HEADS + SWA_KV_HEADS + j, prev(n), 0)),
        pl.BlockSpec((1, w, hd), lambda j, n: (SWA_HEADS + SWA_KV_HEADS + j, n, 0)),
        pl.BlockSpec((g, w, 2 * w), lambda j, n: (j, 0, 0)),
        pl.BlockSpec((g, w, 1), lambda j, n: (j, 0, 0)),
    ]


def _swa_scores(q_ref, kp_ref, kc_ref, bias_ref, n):
    g, w = SWA_GROUP, WINDOW
    q = q_ref[...].reshape(g * w, SWA_HEAD_DIM)
    kk = jnp.concatenate([kp_ref[0], kc_ref[0]], axis=0)
    s = (_dot(q, kk, "nt") * _SWA_SCALE).reshape(g, w, 2 * w) + bias_ref[...]
    qi = lax.broadcasted_iota(jnp.int32, (w, 2 * w), 0)
    kj = lax.broadcasted_iota(jnp.int32, (w, 2 * w), 1)
    dist = qi + w - kj
    valid = (dist >= 0) & (dist < w) & ((n > 0) | (kj >= w))
    return q, kk, jnp.where(valid[None], s, _NEG), valid[None]


def _swa_fwd(swa3, bias, sinkb):
    s_len = swa3.shape[1]
    g, w, hd = SWA_GROUP, WINDOW, SWA_HEAD_DIM

    def body(q_ref, kp_ref, kc_ref, vp_ref, vc_ref, bias_ref, sink_ref, o_ref, lse_ref):
        n = pl.program_id(1)
        _, _, s, valid = _swa_scores(q_ref, kp_ref, kc_ref, bias_ref, n)
        vv = jnp.concatenate([vp_ref[0], vc_ref[0]], axis=0)
        sink = sink_ref[...]
        m = jnp.maximum(jnp.max(s, axis=-1, keepdims=True), sink)
        p = jnp.where(valid, jnp.exp(s - m), 0.0)
        l = jnp.sum(p, axis=-1, keepdims=True) + jnp.exp(sink - m)
        o = _dot(p.reshape(g * w, 2 * w), vv, "nn").reshape(g, w, hd)
        o_ref[...] = (o / l).astype(o_ref.dtype)
        lse_ref[...] = m + jnp.log(l)

    return pl.pallas_call(
        body,
        name="swa_fwd",
        grid=(SWA_KV_HEADS, s_len // w),
        in_specs=_swa_in_specs(),
        out_specs=[
            pl.BlockSpec((g, w, hd), lambda j, n: (j, n, 0)),
            pl.BlockSpec((g, w, 1), lambda j, n: (j, n, 0)),
        ],
        out_shape=[_sds((SWA_HEADS, s_len, hd), BF16), _sds((SWA_HEADS, s_len, 1), F32)],
        compiler_params=_cparams(("parallel", "parallel")),
    )(swa3, swa3, swa3, swa3, swa3, bias, sinkb)


def _swa_bwd(swa3, bias, sinkb, o3, do3, lse):
    s_len = swa3.shape[1]
    g, w, hd = SWA_GROUP, WINDOW, SWA_HEAD_DIM

    def body(q_ref, kp_ref, kc_ref, vp_ref, vc_ref, bias_ref, sink_ref, o_ref, do_ref, lse_ref,
             dq_ref, dka_ref, dkb_ref, dva_ref, dvb_ref, dbias_ref, dsink_ref):
        n = pl.program_id(1)

        @pl.when(n == 0)
        def _():
            dbias_ref[...] = jnp.zeros_like(dbias_ref)
            dsink_ref[...] = jnp.zeros_like(dsink_ref)

        q, kk, s, valid = _swa_scores(q_ref, kp_ref, kc_ref, bias_ref, n)
        vv = jnp.concatenate([vp_ref[0], vc_ref[0]], axis=0)
        lse_v = lse_ref[...]
        p = jnp.where(valid, jnp.exp(s - lse_v), 0.0)
        do = do_ref[...].astype(F32)
        delta = jnp.sum(do * o_ref[...].astype(F32), axis=-1, keepdims=True)
        do2 = do.reshape(g * w, hd)
        dp = _dot(do2, vv, "nt").reshape(g, w, 2 * w)
        ds = p * (dp - delta)
        dbias_ref[...] += ds
        dsink_ref[...] -= jnp.exp(sink_ref[...] - lse_v) * delta
        ds2 = ds.reshape(g * w, 2 * w)
        dq_ref[...] = (_dot(ds2, kk, "nn") * _SWA_SCALE).reshape(g, w, hd).astype(dq_ref.dtype)
        dkk = _dot(ds2, q, "tn") * _SWA_SCALE
        dvv = _dot(p.reshape(g * w, 2 * w), do2, "tn")
        dkb_ref[0] = dkk[:w]
        dka_ref[0] = dkk[w:]
        dvb_ref[0] = dvv[:w]
        dva_ref[0] = dvv[w:]

    kv_spec = pl.BlockSpec((1, w, hd), lambda j, n: (j, n, 0))
    kv_sds = _sds((SWA_KV_HEADS, s_len, hd), F32)
    return pl.pallas_call(
        body,
        name="swa_bwd",
        grid=(SWA_KV_HEADS, s_len // w),
        in_specs=_swa_in_specs() + [
            pl.BlockSpec((g, w, hd), lambda j, n: (j, n, 0)),
            pl.BlockSpec((g, w, hd), lambda j, n: (j, n, 0)),
            pl.BlockSpec((g, w, 1), lambda j, n: (j, n, 0)),
        ],
        out_specs=[
            pl.BlockSpec((g, w, hd), lambda j, n: (j, n, 0)),
            kv_spec, kv_spec, kv_spec, kv_spec,
            pl.BlockSpec((g, w, 2 * w), lambda j, n: (j, 0, 0)),
            pl.BlockSpec((g, w, 1), lambda j, n: (j, 0, 0)),
        ],
        out_shape=[
            _sds((SWA_HEADS, s_len, hd), BF16), kv_sds, kv_sds, kv_sds, kv_sds,
            _sds((SWA_HEADS, w, 2 * w), F32), _sds((SWA_HEADS, w, 1), F32),
        ],
        compiler_params=_cparams(("parallel", "arbitrary")),
    )(swa3, swa3, swa3, swa3, swa3, bias, sinkb, o3, do3, lse)


_MLA_SCALE = MLA_QK ** -0.5
_MLA_TQ = 256


def _mla_mask(i, tq, s_len):
    row = i * tq + lax.broadcasted_iota(jnp.int32, (tq, s_len), 0)
    col = lax.broadcasted_iota(jnp.int32, (tq, s_len), 1)
    return col <= row


def _mla_fwd(q4, k4, v4):
    s_len = q4.shape[1]
    tq = min(_MLA_TQ, s_len)

    def body(q_ref, k_ref, v_ref, o_ref, lse_ref):
        mask = _mla_mask(pl.program_id(1), tq, s_len)
        s = jnp.where(mask, _dot(q_ref[...], k_ref[...], "nt") * _MLA_SCALE, _NEG)
        m = jnp.max(s, axis=-1, keepdims=True)
        p = jnp.exp(s - m)
        l = jnp.sum(p, axis=-1, keepdims=True)
        o_ref[...] = (_dot(p, v_ref[...], "nn") / l).astype(o_ref.dtype)
        lse_ref[...] = m + jnp.log(l)

    return pl.pallas_call(
        body,
        name="mla_fwd",
        grid=(MLA_HEADS, s_len // tq),
        in_specs=[
            pl.BlockSpec((None, tq, MLA_QK), lambda h, i: (h, i, 0)),
            pl.BlockSpec((None, s_len, MLA_QK), lambda h, i: (h, 0, 0)),
            pl.BlockSpec((None, s_len, MLA_V), lambda h, i: (h, 0, 0)),
        ],
        out_specs=[
            pl.BlockSpec((None, tq, MLA_V), lambda h, i: (h, i, 0)),
            pl.BlockSpec((None, tq, 1), lambda h, i: (h, i, 0)),
        ],
        out_shape=[_sds((MLA_HEADS, s_len, MLA_V), BF16), _sds((MLA_HEADS, s_len, 1), F32)],
        compiler_params=_cparams(("parallel", "parallel")),
    )(q4, k4, v4)


def _mla_bwd(q4, k4, v4, o4, do4, lse):
    s_len = q4.shape[1]
    tq = min(_MLA_TQ, s_len)

    def body(q_ref, k_ref, v_ref, o_ref, do_ref, lse_ref, dq_ref, dk_ref, dv_ref):
        i = pl.program_id(1)

        @pl.when(i == 0)
        def _():
            dk_ref[...] = jnp.zeros_like(dk_ref)
            dv_ref[...] = jnp.zeros_like(dv_ref)

        mask = _mla_mask(i, tq, s_len)
        q, k, v = q_ref[...], k_ref[...], v_ref[...]
        s = jnp.where(mask, _dot(q, k, "nt") * _MLA_SCALE, _NEG)
        p = jnp.where(mask, jnp.exp(s - lse_ref[...]), 0.0)
        do = do_ref[...]
        delta = jnp.sum(do.astype(F32) * o_ref[...].astype(F32), axis=-1, keepdims=True)
        dp = _dot(do, v, "nt")
        ds = (p * (dp - delta) * _MLA_SCALE).astype(BF16)
        dq_ref[...] = _dot(ds, k, "nn")
        dk_ref[...] += _dot(ds, q, "tn")
        dv_ref[...] += _dot(p, do, "tn")

    return pl.pallas_call(
        body,
        name="mla_bwd",
        grid=(MLA_HEADS, s_len // tq),
        in_specs=[
            pl.BlockSpec((None, tq, MLA_QK), lambda h, i: (h, i, 0)),
            pl.BlockSpec((None, s_len, MLA_QK), lambda h, i: (h, 0, 0)),
            pl.BlockSpec((None, s_len, MLA_V), lambda h, i: (h, 0, 0)),
            pl.BlockSpec((None, tq, MLA_V), lambda h, i: (h, i, 0)),
            pl.BlockSpec((None, tq, MLA_V), lambda h, i: (h, i, 0)),
            pl.BlockSpec((None, tq, 1), lambda h, i: (h, i, 0)),
        ],
        out_specs=[
            pl.BlockSpec((None, tq, MLA_QK), lambda h, i: (h, i, 0)),
            pl.BlockSpec((None, s_len, MLA_QK), lambda h, i: (h, 0, 0)),
            pl.BlockSpec((None, s_len, MLA_V), lambda h, i: (h, 0, 0)),
        ],
        out_shape=[
            _sds((MLA_HEADS, s_len, MLA_QK), F32),
            _sds((MLA_HEADS, s_len, MLA_QK), F32),
            _sds((MLA_HEADS, s_len, MLA_V), F32),
        ],
        compiler_params=_cparams(("parallel", "arbitrary")),
    )(q4, k4, v4, o4, do4, lse)


def _mla_split(pm):
    q_lat = pm[:, :MLA_Q_RANK]
    kv_lat = pm[:, MLA_Q_RANK:MLA_Q_RANK + MLA_KV_RANK]
    kr = pm[:, MLA_Q_RANK + MLA_KV_RANK:MLA_Q_RANK + MLA_KV_RANK + MLA_ROPE]
    kr_sw = pm[:, MLA_Q_RANK + MLA_KV_RANK + MLA_ROPE:]
    return q_lat, kv_lat, kr, kr_sw


def _mla_proj(pm, cos2, sin2, q_norm, kv_norm, wq_ext, wkv):
    s_len = pm.shape[0]

    def fn(pm_, cos_, sin_, qg, kvg, wq, wk):
        q_lat, kv_lat, kr, kr_sw = _mla_split(pm_)
        nq = (_rms(q_lat)[0] * qg).astype(BF16)
        nkv = (_rms(kv_lat)[0] * kvg).astype(BF16)
        k_rot = kr * cos_ + kr_sw * sin_
        qs, ks, vs = [], [], []
        for h in range(MLA_HEADS):
            qe = _dot(nq, wq[h], "nt")
            q_rot = qe[:, MLA_NOPE:MLA_QK] * cos_ + qe[:, MLA_QK:] * sin_
            qs.append(jnp.concatenate([qe[:, :MLA_NOPE], q_rot], axis=-1))
            kve = _dot(nkv, wk[h], "nt")
            ks.append(jnp.concatenate([kve[:, :MLA_NOPE], k_rot], axis=-1))
            vs.append(kve[:, MLA_NOPE:])
        return jnp.stack(qs), jnp.stack(ks), jnp.stack(vs)

    return _rowwise(
        "mla_proj", fn, [pm, cos2, sin2], [q_norm, kv_norm, wq_ext, wkv],
        [_sds((MLA_HEADS, s_len, MLA_QK), BF16), _sds((MLA_HEADS, s_len, MLA_QK), BF16),
         _sds((MLA_HEADS, s_len, MLA_V), BF16)], [], 256,
    )


def _mla_proj_bwd(pm, cos2, sin2, dq4, dk4, dv4, q_norm, kv_norm, wq_ext, wkv):
    s_len = pm.shape[0]

    def fn(pm_, cos_, sin_, dq, dk, dv, qg, kvg, wq, wk):
        q_lat, kv_lat, _, _ = _mla_split(pm_)
        xq, rq = _rms(q_lat)
        xkv, rkv = _rms(kv_lat)
        nq = (xq * qg).astype(BF16)
        nkv = (xkv * kvg).astype(BF16)
        dnq = jnp.zeros_like(q_lat)
        dnkv = jnp.zeros_like(kv_lat)
        dkr = jnp.zeros_like(cos_)
        dwq, dwk = [], []
        for h in range(MLA_HEADS):
            dy = dq[h][:, MLA_NOPE:]
            dqe = jnp.concatenate([dq[h][:, :MLA_NOPE], dy * cos_, dy * sin_], axis=-1).astype(BF16)
            dnq = dnq + _dot(dqe, wq[h], "nn")
            dwq.append(_dot(dqe, nq, "tn"))
            dkve = jnp.concatenate([dk[h][:, :MLA_NOPE], dv[h]], axis=-1).astype(BF16)
            dnkv = dnkv + _dot(dkve, wk[h], "nn")
            dwk.append(_dot(dkve, nkv, "tn"))
            dkr = dkr + dk[h][:, MLA_NOPE:]
        dq_lat = _rms_bwd(dnq * qg, xq, rq)
        dkv_lat = _rms_bwd(dnkv * kvg, xkv, rkv)
        dpm = jnp.concatenate([dq_lat, dkv_lat, dkr * cos_, dkr * sin_], axis=-1)
        return dpm, _sum0(dnq * xq), _sum0(dnkv * xkv), jnp.stack(dwq), jnp.stack(dwk)

    return _rowwise(
        "mla_proj_bwd", fn, [pm, cos2, sin2, dq4, dk4, dv4], [q_norm, kv_norm, wq_ext, wkv],
        [_sds((s_len, N_MLA_COLS), BF16)],
        [_sds((1, MLA_Q_RANK), F32), _sds((1, MLA_KV_RANK), F32),
         _sds(wq_ext.shape, F32), _sds(wkv.shape, F32)], 256,
    )


def _rope_tables(s_len):
    inv = ROPE_THETA ** (-jnp.arange(0, MLA_ROPE, 2, dtype=F32) / MLA_ROPE)
    ang = jnp.arange(s_len, dtype=F32)[:, None] * inv[None, :]
    cos, sin = jnp.cos(ang), jnp.sin(ang)
    return jnp.concatenate([cos, cos], axis=-1), jnp.concatenate([-sin, sin], axis=-1)


def _mix_weights(g4b):
    rest = g4b[:, 3]
    w_in_t = rest[:, O_IN:O_IN + R_IN].reshape(D_IN, D_MODEL)
    w_o = rest[:, O_O:O_O + R_O].reshape(D_MODEL, D_MODEL)
    w_uq_t = rest[:, O_UQ:O_UQ + R_UQ].reshape(MLA_HEADS, MLA_QK, MLA_Q_RANK)
    w_ukv_t = rest[:, O_UKV:O_UKV + R_UKV].reshape(MLA_HEADS, MLA_NOPE + MLA_V, MLA_KV_RANK)
    half = MLA_ROPE // 2
    w_swa = w_in_t[:N_SWA_COLS]
    w_mla = jnp.concatenate([w_in_t[N_SWA_COLS:], w_in_t[D_IN - half:], w_in_t[D_IN - MLA_ROPE:D_IN - half]], axis=0)
    wq_ext = jnp.concatenate([w_uq_t, w_uq_t[:, MLA_QK - half:], w_uq_t[:, MLA_NOPE:MLA_QK - half]], axis=1)
    return w_swa, w_mla, w_o, wq_ext, w_ukv_t


def _mix_fwd(x, gamma, sh, sc, gate, rest_weights, q_norm, kv_norm, bias, sinkb, cos2, sin2):
    s_len = x.shape[0]
    tm = min(ROW_TILE, s_len)

    def normmod(x_, gamma_, sc_, sh_):
        return _rms(x_)[0] * gamma_ * (1.0 + sc_) + sh_

    deps = rest_weights.mid(x)
    (h,) = _rowwise("mix_normmod", normmod, [x], [gamma, sc, sh], [_sds((s_len, D_MODEL), BF16)], [], 256, deps=deps)
    g4b = rest_weights.get(h)
    weights = _mix_weights(g4b)
    w_swa, w_mla, w_o, wq_ext, wkv = weights
    swa3 = _mm(
        "mix_proj_swa", "nt", (s_len // tm, N_SWA_BLOCKS, 1),
        h, (tm, D_MODEL), lambda i, j, k: (i, 0),
        w_swa, (SWA_HEAD_DIM, D_MODEL), lambda i, j, k: (j, 0),
        _sds((N_SWA_BLOCKS, s_len, SWA_HEAD_DIM), BF16), (None, tm, SWA_HEAD_DIM), lambda i, j, k: (j, i, 0),
        (tm, SWA_HEAD_DIM),
    )
    pm = _mm(
        "mix_proj_mla", "nt", (s_len // tm, 1, 1),
        h, (tm, D_MODEL), lambda i, j, k: (i, 0),
        w_mla, (N_MLA_COLS, D_MODEL), lambda i, j, k: (0, 0),
        _sds((s_len, N_MLA_COLS), F32), (tm, N_MLA_COLS), lambda i, j, k: (i, 0),
        (tm, N_MLA_COLS),
    )
    q4, k4, v4 = _mla_proj(pm, cos2, sin2, q_norm, kv_norm, wq_ext, wkv)
    oa3, lse_a = _swa_fwd(swa3, bias, sinkb)
    ob4, lse_b = _mla_fwd(q4, k4, v4)
    n_a = SWA_HEADS * SWA_HEAD_DIM
    za = _mm(
        "mix_out_swa", "nn", (s_len // tm, 1, SWA_HEADS),
        oa3, (None, tm, SWA_HEAD_DIM), lambda i, j, k: (k, i, 0),
        w_o, (SWA_HEAD_DIM, D_MODEL), lambda i, j, k: (k, 0),
        _sds((s_len, D_MODEL), F32), (tm, D_MODEL), lambda i, j, k: (i, 0),
        (tm, D_MODEL),
    )
    zb = _mm(
        "mix_out_mla", "nn", (s_len // tm, 1, MLA_HEADS),
        ob4, (None, tm, MLA_V), lambda i, j, k: (k, i, 0),
        w_o, (MLA_V, D_MODEL), lambda i, j, k: (n_a // MLA_V + k, 0),
        _sds((s_len, D_MODEL), F32), (tm, D_MODEL), lambda i, j, k: (i, 0),
        (tm, D_MODEL),
    )
    (x_out,) = _rowwise(
        "mix_residual", lambda x_, za_, zb_, g_: x_ + g_ * (za_ + zb_), [x, za, zb], [gate],
        [_sds((s_len, D_MODEL), F32)], [], 256,
    )
    return x_out, g4b, (weights, h, swa3, pm, q4, k4, v4, oa3, lse_a, ob4, lse_b, za, zb)


def _mix_bwd(dxo, x, gamma, sc, gate, q_norm, kv_norm, bias, sinkb, cos2, sin2, saved, hooks):
    weights, h, swa3, pm, q4, k4, v4, oa3, lse_a, ob4, lse_b, za, zb = saved
    w_swa, w_mla, w_o, wq_ext, wkv = weights
    s_len = x.shape[0]
    tm = tk = min(ROW_TILE, s_len)
    vec = _sds((1, D_MODEL), F32)
    n_a = SWA_HEADS * SWA_HEAD_DIM

    dz, dgate = _rowwise(
        "mix_bwd_gate", lambda dxo_, za_, zb_, g_: (g_ * dxo_, _sum0((za_ + zb_) * dxo_)),
        [dxo, za, zb], [gate], [_sds((s_len, D_MODEL), BF16)], [vec], 256,
    )
    dwo_a = _mm(
        "mix_bwd_wo_swa", "tn", (SWA_HEADS, 1, s_len // tk),
        oa3, (None, tk, SWA_HEAD_DIM), lambda i, j, k: (i, k, 0),
        dz, (tk, D_MODEL), lambda i, j, k: (k, 0),
        _sds((n_a, D_MODEL), F32), (SWA_HEAD_DIM, D_MODEL), lambda i, j, k: (i, 0),
        (SWA_HEAD_DIM, D_MODEL),
    )
    dwo_b = _mm(
        "mix_bwd_wo_mla", "tn", (MLA_HEADS, 1, s_len // tk),
        ob4, (None, tk, MLA_V), lambda i, j, k: (i, k, 0),
        dz, (tk, D_MODEL), lambda i, j, k: (k, 0),
        _sds((MLA_HEADS * MLA_V, D_MODEL), F32), (MLA_V, D_MODEL), lambda i, j, k: (i, 0),
        (MLA_V, D_MODEL),
    )
    doa3 = _mm(
        "mix_bwd_do_swa", "nt", (s_len // tm, SWA_HEADS, 1),
        dz, (tm, D_MODEL), lambda i, j, k: (i, 0),
        w_o, (SWA_HEAD_DIM, D_MODEL), lambda i, j, k: (j, 0),
        _sds((SWA_HEADS, s_len, SWA_HEAD_DIM), BF16), (None, tm, SWA_HEAD_DIM), lambda i, j, k: (j, i, 0),
        (tm, SWA_HEAD_DIM), deps=hooks.after_gate(dz),
    )
    dob4 = _mm(
        "mix_bwd_do_mla", "nt", (s_len // tm, MLA_HEADS, 1),
        dz, (tm, D_MODEL), lambda i, j, k: (i, 0),
        w_o, (MLA_V, D_MODEL), lambda i, j, k: (n_a // MLA_V + j, 0),
        _sds((MLA_HEADS, s_len, MLA_V), BF16), (None, tm, MLA_V), lambda i, j, k: (j, i, 0),
        (tm, MLA_V),
    )
    dq3, dka, dkb, dva, dvb, dbias, dsink = _swa_bwd(swa3, bias, sinkb, oa3, doa3, lse_a)
    dq4, dk4, dv4 = _mla_bwd(q4, k4, v4, ob4, dob4, lse_b)
    dpm, dq_norm, dkv_norm, dwq_ext, dwkv = _mla_proj_bwd(pm, cos2, sin2, dq4, dk4, dv4, q_norm, kv_norm, wq_ext, wkv)

    def fold(da, db):
        return da + jnp.concatenate([db[:, WINDOW:], jnp.zeros_like(db[:, :WINDOW])], axis=1)

    dswa3 = jnp.concatenate([dq3, fold(dka, dkb).astype(BF16), fold(dva, dvb).astype(BF16)], axis=0)

    dw_swa = _mm(
        "mix_bwd_win_swa", "tn", (N_SWA_BLOCKS, 1, s_len // tk),
        dswa3, (None, tk, SWA_HEAD_DIM), lambda i, j, k: (i, k, 0),
        h, (tk, D_MODEL), lambda i, j, k: (k, 0),
        _sds((N_SWA_COLS, D_MODEL), F32), (SWA_HEAD_DIM, D_MODEL), lambda i, j, k: (i, 0),
        (SWA_HEAD_DIM, D_MODEL),
    )
    dw_mla = _mm(
        "mix_bwd_win_mla", "tn", (1, 1, s_len // tk),
        dpm, (tk, N_MLA_COLS), lambda i, j, k: (k, 0),
        h, (tk, D_MODEL), lambda i, j, k: (k, 0),
        _sds((N_MLA_COLS, D_MODEL), F32), (N_MLA_COLS, D_MODEL), lambda i, j, k: (0, 0),
        (N_MLA_COLS, D_MODEL),
    )
    dh_a = _mm(
        "mix_bwd_dh_swa", "nn", (s_len // tm, 1, N_SWA_BLOCKS),
        dswa3, (None, tm, SWA_HEAD_DIM), lambda i, j, k: (k, i, 0),
        w_swa, (SWA_HEAD_DIM, D_MODEL), lambda i, j, k: (k, 0),
        _sds((s_len, D_MODEL), F32), (tm, D_MODEL), lambda i, j, k: (i, 0),
        (tm, D_MODEL),
    )
    dh_b = _mm(
        "mix_bwd_dh_mla", "nn", (s_len // tm, 1, 1),
        dpm, (tm, N_MLA_COLS), lambda i, j, k: (i, 0),
        w_mla, (N_MLA_COLS, D_MODEL), lambda i, j, k: (0, 0),
        _sds((s_len, D_MODEL), F32), (tm, D_MODEL), lambda i, j, k: (i, 0),
        (tm, D_MODEL),
    )
    dx, dsc, dsh, dgamma = _normmod_bwd_call("mix_bwd_normmod", [dh_a, dh_b], x, dxo, gamma, sc)

    half = MLA_ROPE // 2
    n_mla_in = D_IN - N_SWA_COLS
    dw_mla_base = dw_mla[:n_mla_in]
    dw_mla_base = dw_mla_base.at[n_mla_in - half:].add(dw_mla[n_mla_in:n_mla_in + half])
    dw_mla_base = dw_mla_base.at[n_mla_in - MLA_ROPE:n_mla_in - half].add(dw_mla[n_mla_in + half:])
    dw_in_t = jnp.concatenate([dw_swa, dw_mla_base], axis=0)
    dw_uq_t = dwq_ext[:, :MLA_QK]
    dw_uq_t = dw_uq_t.at[:, MLA_QK - half:].add(dwq_ext[:, MLA_QK:MLA_QK + half])
    dw_uq_t = dw_uq_t.at[:, MLA_NOPE:MLA_QK - half].add(dwq_ext[:, MLA_QK + half:])
    dw_o = jnp.concatenate([dwo_a, dwo_b], axis=0)
    rest = jnp.concatenate(
        [
            dw_in_t.reshape(N_CHIPS, R_IN, D_MODEL),
            dw_o.reshape(N_CHIPS, R_O, D_MODEL),
            dw_uq_t.reshape(N_CHIPS, R_UQ, D_MODEL),
            dwkv.reshape(N_CHIPS, R_UKV, D_MODEL),
            jnp.zeros((N_CHIPS, F_SHARD - O_PAD, D_MODEL), F32),
        ],
        axis=1,
    ).astype(BF16)
    return dx, rest, (dsh, dsc, dgate, dgamma), dq_norm, dkv_norm, dbias, dsink


def _device_step(x, target, mod, ffn1_weights, rest_weights, norms, q_norm, kv_norm, sinks, rel_bias,
                 hooks2=None, hooks_mix=None, mix_done=None, hooks1=None):
    s_len = x.shape[0]
    sh1, sc1, g1, sh2, sc2, g2, sh3, sc3, g3 = [mod[:, i * D_MODEL:(i + 1) * D_MODEL] for i in range(N_MOD)]
    n1, n2, n3, nf = norms
    bkt = jnp.asarray(_bucket_map())
    bias = _bias_expand(rel_bias.T, bkt).reshape(SWA_HEADS, WINDOW, 2 * WINDOW)
    sinkb = jnp.broadcast_to(sinks.reshape(SWA_HEADS, 1, 1), (SWA_HEADS, WINDOW, 1))
    cos2, sin2 = _rope_tables(s_len)

    x1, saved1 = _ffn_fwd("ffn1", x, n1, sh1, sc1, g1, ffn1_weights, 0, sh1)
    x2, g4b, saved2 = _mix_fwd(x1, n2, sh2, sc2, g2, rest_weights, q_norm, kv_norm, bias, sinkb, cos2, sin2)
    x3, saved3 = _ffn_fwd("ffn2", x2, n3, sh3, sc3, g3, _Ready(g4b), 0, x2)

    def head(x_, t_, g_):
        xr, r = _rms(x_)
        err = xr * g_ - t_
        dy = err * (1.0 / D_MODEL)
        return _rms_bwd(dy * g_, xr, r), _sum0(err * err), _sum0(dy * xr)

    vec = _sds((1, D_MODEL), F32)
    dx3, sq, dnf = _rowwise("loss_head", head, [x3, target], [nf], [_sds((s_len, D_MODEL), F32)], [vec, vec], 256)
    loss_part = (0.5 / D_MODEL) * jnp.sum(sq)

    dx2, gb2, (dsh3, dsc3, dg3, dn3) = _ffn_bwd("ffn2", dx3, x2, n3, sc3, g3, 0, saved3, hooks2 or _BwdHooks())
    dx1, rest, (dsh2, dsc2, dg2, dn2), dq_norm, dkv_norm, dbias, dsink = _mix_bwd(
        dx2, x1, n2, sc2, g2, q_norm, kv_norm, bias, sinkb, cos2, sin2, saved2, hooks_mix or _BwdHooks()
    )
    rest = rest[:, None]
    deps1 = mix_done(dx1, rest) if mix_done is not None else []
    dx0, gb1, (dsh1, dsc1, dg1, dn1) = _ffn_bwd(
        "ffn1", dx1, x, n1, sc1, g1, 0, saved1, hooks1 or _BwdHooks(), deps=deps1
    )

    dmod = jnp.concatenate([dsh1, dsc1, dg1, dsh2, dsc2, dg2, dsh3, dsc3, dg3], axis=-1)
    drel = _bias_reduce(dbias.reshape(SWA_HEADS, -1), bkt).T
    dsinks = jnp.sum(dsink, axis=(1, 2)).reshape(1, SWA_HEADS)
    small = (dn1, dn2, dn3, dnf, dq_norm, dkv_norm, dsinks, drel)
    return loss_part, dx0, (gb1, gb2, rest), dmod, small


_MESH = pl.DeviceIdType.MESH


def _place():
    return lax.axis_index("x"), lax.axis_index("y"), lax.axis_index("c")


def _other_chips(x, y):
    return [(1 - x, y), (x, 1 - y), (1 - x, 1 - y)]


def _allgather_small(name, blk):
    m_per, n = blk.shape

    def body(x_ref, out_ref, send_sems, recv_sems, local_sem):
        x, y, c = _place()
        me, sibling = (x, y, c), (x, y, 1 - c)
        chips = _other_chips(x, y)

        def rows(px, py, pc):
            return out_ref.at[pl.ds((4 * px + 2 * py + pc) * m_per, m_per), :]

        def copy(k, block, to, src=None):
            return pltpu.make_async_remote_copy(
                src_ref=rows(*block) if src is None else src, dst_ref=rows(*block),
                send_sem=send_sems.at[k], recv_sem=recv_sems.at[k], device_id=to, device_id_type=_MESH,
            )

        mine = pltpu.make_async_copy(x_ref, rows(*me), local_sem)
        mine.start()
        first = [copy(0, me, sibling, src=x_ref)]
        first += [copy(1 + j, me, (*chip, c), src=x_ref) for j, chip in enumerate(chips)]
        for cp in first:
            cp.start()
        passed = [copy(4 + j, (*chip, c), sibling) for j, chip in enumerate(chips)]
        for j, chip in enumerate(chips):
            copy(1 + j, (*chip, c), me).wait_recv()
            passed[j].start()
        copy(0, sibling, me).wait_recv()
        for j, chip in enumerate(chips):
            copy(4 + j, (*chip, 1 - c), me).wait_recv()
        for cp in first + passed:
            cp.wait_send()
        mine.wait()

    return pl.pallas_call(
        body,
        name=name,
        out_shape=_sds((N_DEV * m_per, n), blk.dtype),
        in_specs=[pl.BlockSpec(memory_space=pltpu.VMEM)],
        out_specs=pl.BlockSpec(memory_space=pltpu.VMEM),
        scratch_shapes=[pltpu.SemaphoreType.DMA((7,)), pltpu.SemaphoreType.DMA((7,)), pltpu.SemaphoreType.DMA],
    )(blk)


def _allgather_weights(name, own):
    n = own.shape[0]

    def body(own_ref, g_ref, token, send_sems, recv_sems, local_sem):
        x, y, c = _place()
        sibling = (x, y, 1 - c)
        chips = _other_chips(x, y)
        mine = pltpu.make_async_copy(own_ref, g_ref.at[2 * x + y], local_sem)
        mine.start()
        for j, chip in enumerate(chips):
            for cp in _gather_sends(n, own_ref, g_ref, send_sems.at[j], recv_sems.at[j], x, y, c, chip):
                cp.start()
        for j, chip in enumerate(chips):
            _gather_all(own_ref, g_ref, send_sems.at[j], recv_sems.at[j], chip, c, c).wait_recv()
            for cp in _gather_forwards(n, g_ref, send_sems.at[3 + j], recv_sems.at[3 + j], chip, c, sibling):
                cp.start()
        for j, chip in enumerate(chips):
            _gather_all(own_ref, g_ref, send_sems.at[3 + j], recv_sems.at[3 + j], chip, 1 - c, c).wait_recv()
        for j, chip in enumerate(chips):
            _gather_all(own_ref, g_ref, send_sems.at[j], recv_sems.at[j], chip, c, c).wait_send()
            _gather_all(own_ref, g_ref, send_sems.at[3 + j], recv_sems.at[3 + j], chip, c, c).wait_send()
        mine.wait()
        token[...] = jnp.zeros_like(token)

    return pl.pallas_call(
        body,
        name=name,
        out_shape=[_sds((N_CHIPS,) + own.shape, own.dtype), _sds((8, 128), F32)],
        in_specs=[pl.BlockSpec(memory_space=pl.ANY)],
        out_specs=[pl.BlockSpec(memory_space=pl.ANY), pl.BlockSpec(memory_space=pltpu.VMEM)],
        scratch_shapes=[pltpu.SemaphoreType.DMA((6,)), pltpu.SemaphoreType.DMA((6,)), pltpu.SemaphoreType.DMA],
    )(own)


def _gather_sends(n, own_ref, g_ref, send_sem, recv_sem, x, y, c, chip):
    return [
        pltpu.make_async_remote_copy(
            src_ref=own_ref.at[p, pl.ds(c * HALF, HALF), :], dst_ref=g_ref.at[2 * x + y, p, pl.ds(c * HALF, HALF), :],
            send_sem=send_sem, recv_sem=recv_sem, device_id=(*chip, c), device_id_type=_MESH,
        )
        for p in range(n)
    ]


def _gather_forwards(n, g_ref, send_sem, recv_sem, chip, c, sibling):
    return [
        pltpu.make_async_remote_copy(
            src_ref=g_ref.at[2 * chip[0] + chip[1], p, pl.ds(c * HALF, HALF), :],
            dst_ref=g_ref.at[2 * chip[0] + chip[1], p, pl.ds(c * HALF, HALF), :],
            send_sem=send_sem, recv_sem=recv_sem, device_id=sibling, device_id_type=_MESH,
        )
        for p in range(n)
    ]


def _gather_all(own_ref, g_ref, send_sem, recv_sem, chip, half, c):
    return pltpu.make_async_remote_copy(
        src_ref=own_ref.at[:, pl.ds(c * HALF, HALF), :],
        dst_ref=g_ref.at[2 * chip[0] + chip[1], :, pl.ds(half * HALF, HALF), :],
        send_sem=send_sem, recv_sem=recv_sem, device_id=(*chip, c), device_id_type=_MESH,
    )


_HBM_SPEC = pl.BlockSpec(memory_space=pltpu.HBM)
_SEM_SPEC = pl.BlockSpec(memory_space=pltpu.SEMAPHORE)
_ANY_SPEC = pl.BlockSpec(memory_space=pl.ANY)
_VMEM_SPEC = pl.BlockSpec(memory_space=pltpu.VMEM)
_SPLIT_PARAMS = pltpu.CompilerParams(has_side_effects=pltpu.SideEffectType.DATAFLOW_SIDE_EFFECTING)
_TOKEN = jax.ShapeDtypeStruct((8, 128), F32)


def _in_hbm(a):
    return pltpu.with_memory_space_constraint(a, pltpu.HBM)


class _GatherBehind:
    def __init__(self, tag, own, after):
        self.tag, self.n = tag, own.shape[0]
        n = self.n
        x, y, _ = _place()
        g_init = lax.dynamic_update_slice(
            lax.empty((N_CHIPS,) + own.shape, own.dtype), own[None], (2 * x + y, 0, 0, 0)
        )

        def body(own_ref, g_ref, *rest):
            send_sems, recv_sems, _, _, token = rest[len(after):]
            x, y, c = _place()
            for j, chip in enumerate(_other_chips(x, y)):
                for cp in _gather_sends(n, own_ref, g_ref, send_sems.at[j], recv_sems.at[j], x, y, c, chip):
                    cp.start()
            token[...] = jnp.zeros_like(token)

        self.send1, self.recv1, self.own, self.g, self.token = pl.pallas_call(
            body,
            name=f"{tag}_start",
            out_shape=(
                pltpu.SemaphoreType.DMA((3,)), pltpu.SemaphoreType.DMA((3,)),
                pltpu.HBM(own.shape, own.dtype), pltpu.HBM(g_init.shape, g_init.dtype), _TOKEN,
            ),
            in_specs=(_HBM_SPEC, _HBM_SPEC) + (_ANY_SPEC,) * len(after),
            out_specs=(_SEM_SPEC, _SEM_SPEC, _HBM_SPEC, _HBM_SPEC, _VMEM_SPEC),
            input_output_aliases={0: 2, 1: 3},
            compiler_params=_SPLIT_PARAMS,
        )(_in_hbm(own), _in_hbm(g_init), *after)

    def mid(self, after):
        n = self.n

        def body(own_ref, g_ref, send1, recv1, after_ref, send2, recv2, g_out, token):
            x, y, c = _place()
            sibling = (x, y, 1 - c)
            chips = _other_chips(x, y)
            for j, chip in enumerate(chips):
                _gather_all(own_ref, g_ref, send1.at[j], recv1.at[j], chip, c, c).wait_recv()
                for cp in _gather_forwards(n, g_ref, send2.at[j], recv2.at[j], chip, c, sibling):
                    cp.start()
            for j, chip in enumerate(chips):
                _gather_all(own_ref, g_ref, send1.at[j], recv1.at[j], chip, c, c).wait_send()
            token[...] = jnp.zeros_like(token)

        self.send2, self.recv2, self.g, token = pl.pallas_call(
            body,
            name=f"{self.tag}_mid",
            out_shape=(
                pltpu.SemaphoreType.DMA((3,)), pltpu.SemaphoreType.DMA((3,)),
                pltpu.HBM(self.g.shape, self.g.dtype), _TOKEN,
            ),
            in_specs=(_HBM_SPEC, _HBM_SPEC, _SEM_SPEC, _SEM_SPEC, _ANY_SPEC),
            out_specs=(_SEM_SPEC, _SEM_SPEC, _HBM_SPEC, _VMEM_SPEC),
            input_output_aliases={1: 2},
            compiler_params=_SPLIT_PARAMS,
        )(self.own, self.g, self.send1, self.recv1, after)
        return [token]

    def get(self, after):
        def body(g_ref, send2, recv2, after_ref, g_out):
            x, y, c = _place()
            for j, chip in enumerate(_other_chips(x, y)):
                slot_in = g_ref.at[2 * chip[0] + chip[1], :, pl.ds((1 - c) * HALF, HALF), :]
                slot_out = g_ref.at[2 * chip[0] + chip[1], :, pl.ds(c * HALF, HALF), :]
                cp = pltpu.make_async_remote_copy(
                    src_ref=slot_out, dst_ref=slot_in, send_sem=send2.at[j], recv_sem=recv2.at[j],
                    device_id=(x, y, 1 - c), device_id_type=_MESH,
                )
                cp.wait_send()
                cp.wait_recv()

        return pl.pallas_call(
            body,
            name=f"{self.tag}_wait",
            out_shape=pltpu.HBM(self.g.shape, self.g.dtype),
            in_specs=(_HBM_SPEC, _SEM_SPEC, _SEM_SPEC, _ANY_SPEC),
            out_specs=_HBM_SPEC,
            input_output_aliases={0: 0},
            compiler_params=_SPLIT_PARAMS,
        )(self.g, self.send2, self.recv2, after)


def _pair_exchange(name, gb):
    def body(gb_ref, land_ref, send_sem, recv_sem):
        x, y, c = _place()
        theirs = gb_ref.at[:, :, pl.ds((1 - c) * HALF, HALF), :]
        cp = pltpu.make_async_remote_copy(
            src_ref=theirs, dst_ref=land_ref, send_sem=send_sem, recv_sem=recv_sem,
            device_id=(x, y, 1 - c), device_id_type=_MESH,
        )
        cp.start()
        cp.wait()

    return pl.pallas_call(
        body,
        name=name,
        out_shape=_sds((N_CHIPS, gb.shape[1], HALF, D_MODEL), gb.dtype),
        in_specs=[pl.BlockSpec(memory_space=pl.ANY)],
        out_specs=pl.BlockSpec(memory_space=pl.ANY),
        scratch_shapes=[pltpu.SemaphoreType.DMA, pltpu.SemaphoreType.DMA],
    )(gb)


def _pair_sum(name, gb, land):
    def body(gb_ref, land_ref, o_ref):
        c = lax.axis_index("c")
        mine = gb_ref[pl.ds(pl.multiple_of(c * HALF, 16), HALF), :]
        o_ref[...] = (mine.astype(F32) + land_ref[...].astype(F32)).astype(o_ref.dtype)

    return pl.pallas_call(
        body,
        name=name,
        grid=(N_CHIPS, gb.shape[1]),
        in_specs=[
            pl.BlockSpec((None, None, F_SHARD, D_MODEL), lambda j, p: (j, p, 0, 0)),
            pl.BlockSpec((None, None, HALF, D_MODEL), lambda j, p: (j, p, 0, 0)),
        ],
        out_specs=pl.BlockSpec((None, None, HALF, D_MODEL), lambda j, p: (j, p, 0, 0)),
        out_shape=_sds(land.shape, BF16),
        compiler_params=_cparams(("parallel", "parallel")),
    )(gb, land)


def _chip_exchange(name, part):
    def body(p_ref, land_ref, send_sems, recv_sems, local_sem):
        x, y, c = _place()
        me = 2 * x + y
        chips = _other_chips(x, y)
        mine = pltpu.make_async_copy(p_ref.at[me], land_ref.at[me], local_sem)
        mine.start()

        def copy(k, chip):
            return pltpu.make_async_remote_copy(
                src_ref=p_ref.at[2 * chip[0] + chip[1]], dst_ref=land_ref.at[me],
                send_sem=send_sems.at[k], recv_sem=recv_sems.at[k], device_id=(*chip, c), device_id_type=_MESH,
            )

        sends = [copy(k, chip) for k, chip in enumerate(chips)]
        for cp in sends:
            cp.start()
        for k, chip in enumerate(chips):
            pltpu.make_async_remote_copy(
                src_ref=p_ref.at[me], dst_ref=land_ref.at[2 * chip[0] + chip[1]],
                send_sem=send_sems.at[k], recv_sem=recv_sems.at[k], device_id=(*chip, c), device_id_type=_MESH,
            ).wait_recv()
        for cp in sends:
            cp.wait_send()
        mine.wait()

    return pl.pallas_call(
        body,
        name=name,
        out_shape=_sds(part.shape, part.dtype),
        in_specs=[pl.BlockSpec(memory_space=pl.ANY)],
        out_specs=pl.BlockSpec(memory_space=pl.ANY),
        scratch_shapes=[pltpu.SemaphoreType.DMA((3,)), pltpu.SemaphoreType.DMA((3,)), pltpu.SemaphoreType.DMA],
    )(part)


def _chip_sum_share(name, land):
    n = land.shape[1]

    def body(l_ref, r_ref, buf, send_sems, recv_sems, local_sems):
        p = pl.program_id(0)
        x, y, c = _place()
        acc = l_ref[0].astype(F32)
        for j in range(1, N_CHIPS):
            acc = acc + l_ref[j].astype(F32)
        buf[p] = acc

        def copies(q):
            mine = r_ref.at[q, pl.ds(c * HALF, HALF), :]
            theirs = r_ref.at[q, pl.ds((1 - c) * HALF, HALF), :]
            local = pltpu.make_async_copy(buf.at[q], mine, local_sems.at[q])
            out = pltpu.make_async_remote_copy(
                src_ref=buf.at[q], dst_ref=mine, send_sem=send_sems.at[q], recv_sem=recv_sems.at[q],
                device_id=(x, y, 1 - c), device_id_type=_MESH,
            )
            arrive = pltpu.make_async_remote_copy(
                src_ref=buf.at[q], dst_ref=theirs, send_sem=send_sems.at[q], recv_sem=recv_sems.at[q],
                device_id=(x, y, 1 - c), device_id_type=_MESH,
            )
            return local, out, arrive

        local, out, _ = copies(p)
        local.start()
        out.start()

        @pl.when(p == n - 1)
        def _():
            for q in range(n):
                local, out, arrive = copies(q)
                arrive.wait_recv()
                out.wait_send()
                local.wait()

    return pl.pallas_call(
        body,
        name=name,
        grid=(n,),
        in_specs=[pl.BlockSpec((N_CHIPS, None, HALF, D_MODEL), lambda p: (0, p, 0, 0))],
        out_specs=pl.BlockSpec(memory_space=pl.ANY),
        out_shape=_sds((n, F_SHARD, D_MODEL), F32),
        scratch_shapes=[
            pltpu.VMEM((n, HALF, D_MODEL), F32),
            pltpu.SemaphoreType.DMA((n,)), pltpu.SemaphoreType.DMA((n,)), pltpu.SemaphoreType.DMA((n,)),
        ],
        compiler_params=_cparams(("arbitrary",)),
    )(land)


class _ReduceBehind:
    def __init__(self, tag, gb, after=()):
        self.tag = tag
        n = gb.shape[1]
        land = lax.empty((N_CHIPS, n, HALF, D_MODEL), gb.dtype)

        def body(gb_ref, land_ref, *rest):
            send_sem, recv_sem, _, _, token = rest[len(after):]
            self._pair_copy(gb_ref, land_ref, send_sem, recv_sem).start()
            token[...] = jnp.zeros_like(token)

        self.send, self.recv, self.gb, self.land, self.token = pl.pallas_call(
            body,
            name=f"grads_pair_start_{tag}",
            out_shape=(
                pltpu.SemaphoreType.DMA((1,)), pltpu.SemaphoreType.DMA((1,)),
                pltpu.HBM(gb.shape, gb.dtype), pltpu.HBM(land.shape, land.dtype), _TOKEN,
            ),
            in_specs=(_HBM_SPEC, _HBM_SPEC) + (_ANY_SPEC,) * len(after),
            out_specs=(_SEM_SPEC, _SEM_SPEC, _HBM_SPEC, _HBM_SPEC, _VMEM_SPEC),
            input_output_aliases={0: 2, 1: 3},
            compiler_params=_SPLIT_PARAMS,
        )(_in_hbm(gb), _in_hbm(land), *after)

    @staticmethod
    def _pair_copy(gb_ref, land_ref, send_sem, recv_sem):
        x, y, c = _place()
        return pltpu.make_async_remote_copy(
            src_ref=gb_ref.at[:, :, pl.ds((1 - c) * HALF, HALF), :], dst_ref=land_ref,
            send_sem=send_sem.at[0], recv_sem=recv_sem.at[0], device_id=(x, y, 1 - c), device_id_type=_MESH,
        )

    @staticmethod
    def _chip_copies(p_ref, land_ref, send_sems, recv_sems):
        x, y, c = _place()
        me = 2 * x + y
        sends, arrivals = [], []
        for k, chip in enumerate(_other_chips(x, y)):
            them = 2 * chip[0] + chip[1]
            sends.append(pltpu.make_async_remote_copy(
                src_ref=p_ref.at[them], dst_ref=land_ref.at[me], send_sem=send_sems.at[k], recv_sem=recv_sems.at[k],
                device_id=(*chip, c), device_id_type=_MESH,
            ))
            arrivals.append(pltpu.make_async_remote_copy(
                src_ref=p_ref.at[me], dst_ref=land_ref.at[them], send_sem=send_sems.at[k], recv_sem=recv_sems.at[k],
                device_id=(*chip, c), device_id_type=_MESH,
            ))
        return sends, arrivals

    def mid(self, after):
        tag = self.tag

        def wait_body(gb_ref, land_ref, send_sem, recv_sem, after_ref, gb_out, land_out):
            cp = self._pair_copy(gb_ref, land_ref, send_sem, recv_sem)
            cp.wait_send()
            cp.wait_recv()

        gb, land = pl.pallas_call(
            wait_body,
            name=f"grads_pair_wait_{tag}",
            out_shape=(pltpu.HBM(self.gb.shape, self.gb.dtype), pltpu.HBM(self.land.shape, self.land.dtype)),
            in_specs=(_HBM_SPEC, _HBM_SPEC, _SEM_SPEC, _SEM_SPEC, _ANY_SPEC),
            out_specs=(_HBM_SPEC, _HBM_SPEC),
            input_output_aliases={0: 0, 1: 1},
            compiler_params=_SPLIT_PARAMS,
        )(self.gb, self.land, self.send, self.recv, after)
        part = _pair_sum(f"grads_pair_sum_{tag}", gb, land)

        x, y, _ = _place()
        start = (2 * x + y, 0, 0, 0)
        own = lax.dynamic_slice(part, start, (1,) + part.shape[1:])
        land2 = lax.dynamic_update_slice(lax.empty(part.shape, part.dtype), own, start)

        def start_body(p_ref, land_ref, send_sems, recv_sems, p_out, land_out, token):
            for cp in self._chip_copies(p_ref, land_ref, send_sems, recv_sems)[0]:
                cp.start()
            token[...] = jnp.zeros_like(token)

        self.send2, self.recv2, self.part, self.land2, token = pl.pallas_call(
            start_body,
            name=f"grads_chip_start_{tag}",
            out_shape=(
                pltpu.SemaphoreType.DMA((3,)), pltpu.SemaphoreType.DMA((3,)),
                pltpu.HBM(part.shape, part.dtype), pltpu.HBM(land2.shape, land2.dtype), _TOKEN,
            ),
            in_specs=(_HBM_SPEC, _HBM_SPEC),
            out_specs=(_SEM_SPEC, _SEM_SPEC, _HBM_SPEC, _HBM_SPEC, _VMEM_SPEC),
            input_output_aliases={0: 2, 1: 3},
            compiler_params=_SPLIT_PARAMS,
        )(_in_hbm(part), _in_hbm(land2))
        return [token]

    def get(self, after):
        n_after = len(after)

        def wait_body(p_ref, land_ref, send_sems, recv_sems, *rest):
            sends, arrivals = self._chip_copies(p_ref, land_ref, send_sems, recv_sems)
            for cp in arrivals:
                cp.wait_recv()
            for cp in sends:
                cp.wait_send()

        _, land2 = pl.pallas_call(
            wait_body,
            name=f"grads_chip_wait_{self.tag}",
            out_shape=(pltpu.HBM(self.part.shape, self.part.dtype), pltpu.HBM(self.land2.shape, self.land2.dtype)),
            in_specs=(_HBM_SPEC, _HBM_SPEC, _SEM_SPEC, _SEM_SPEC) + (_ANY_SPEC,) * n_after,
            out_specs=(_HBM_SPEC, _HBM_SPEC),
            input_output_aliases={0: 0, 1: 1},
            compiler_params=_SPLIT_PARAMS,
        )(self.part, self.land2, self.send2, self.recv2, *after)
        return _chip_sum_share(f"grads_chip_sum_share_{self.tag}", land2)


def _allreduce_small(blk):
    gathered = _allgather_small("allgather_small_grads", blk).reshape(N_DEV, PK_ROWS, 128)

    def body(g_ref, o_ref):
        acc = g_ref[0]
        for k in range(1, N_DEV):
            acc = acc + g_ref[k]
        o_ref[...] = acc

    total = pl.pallas_call(body, name="small_grads_sum", out_shape=_sds((PK_ROWS, 128), F32))(gathered)
    return gathered, total


def _adamw(name, w, g, m, v):
    rows, cols = w.shape
    tm = rows
    while tm * cols * 4 > (1 << 20) and tm % 16 == 0:
        tm //= 2

    def fn(w_, g_, m_, v_):
        m_new = ADAM_B1 * m_ + (1.0 - ADAM_B1) * g_
        v_new = ADAM_B2 * v_ + (1.0 - ADAM_B2) * (g_ * g_)
        m_hat = m_new / (1.0 - ADAM_B1 ** ADAM_STEP)
        v_hat = v_new / (1.0 - ADAM_B2 ** ADAM_STEP)
        delta = -ADAM_LR * (m_hat / (jnp.sqrt(v_hat) + ADAM_EPS) + ADAM_WD * w_)
        return delta, m_new, v_new

    out = _sds(w.shape, F32)
    return _rowwise(name, fn, [w, g, m, v], [], [out, out, out], [], tm)


def _pack_small(b_mod_like, n1, n2, n3, nf, qn, kvn, sinks, rel):
    parts = [
        b_mod_like.reshape(PK_MOD, 128),
        n1.reshape(8, 128), n2.reshape(8, 128), n3.reshape(8, 128), nf.reshape(8, 128),
        qn.reshape(2, 128), kvn.reshape(1, 128),
        jnp.pad(sinks.reshape(1, SWA_HEADS), ((0, 0), (0, 128 - SWA_HEADS))),
        rel.reshape(2, 128),
        jnp.zeros((2, 128), F32),
    ]
    return jnp.concatenate(parts, axis=0)


def _unpack_small(p):
    o = PK_MOD
    return (
        p[:o].reshape(1, N_MOD * D_MODEL),
        p[o:o + 8].reshape(1, D_MODEL), p[o + 8:o + 16].reshape(1, D_MODEL),
        p[o + 16:o + 24].reshape(1, D_MODEL), p[o + 24:o + 32].reshape(D_MODEL),
        p[o + 32:o + 34].reshape(1, MLA_Q_RANK), p[o + 34:o + 35].reshape(1, MLA_KV_RANK),
        p[o + 35:o + 36, :SWA_HEADS].reshape(1, SWA_HEADS),
        p[o + 36:o + 38].reshape(NUM_BUCKETS, SWA_HEADS),
    )


def kernel(x, c, w_mod, b_mod, norm_ffn1, ffn1_gate, ffn1_up, ffn1_down, norm_mix, w_in, q_norm, kv_norm, w_uq, w_ukv, sinks, w_o, norm_ffn2, ffn2_gate, ffn2_up, ffn2_down, rel_bias, norm_final, loss_target, m_w_mod, m_b_mod, m_norm_ffn1, m_ffn1_gate, m_ffn1_up, m_ffn1_down, m_norm_mix, m_w_in, m_q_norm, m_kv_norm, m_w_uq, m_w_ukv, m_sinks, m_w_o, m_norm_ffn2, m_ffn2_gate, m_ffn2_up, m_ffn2_down, m_rel_bias, m_norm_final, v_w_mod, v_b_mod, v_norm_ffn1, v_ffn1_gate, v_ffn1_up, v_ffn1_down, v_norm_mix, v_w_in, v_q_norm, v_kv_norm, v_w_uq, v_w_ukv, v_sinks, v_w_o, v_norm_ffn2, v_ffn2_gate, v_ffn2_up, v_ffn2_down, v_rel_bias, v_norm_final):
    ax, ay, ac = _place()
    chip = 2 * ax + ay
    dev = 2 * chip + ac
    n_mod_shard = N_MOD * D_MODEL // N_CHIPS

    def t16(w):
        return w[0].T.astype(BF16)

    rest = jnp.concatenate(
        [
            t16(w_in),
            w_o[0].astype(BF16),
            t16(w_uq).reshape(R_UQ, D_MODEL),
            t16(w_ukv).reshape(R_UKV, D_MODEL),
            jnp.zeros((F_SHARD - O_PAD, D_MODEL), BF16),
        ],
        axis=0,
    )
    own_a = jnp.stack([t16(ffn1_gate), t16(ffn1_up), ffn1_down[0].astype(BF16)])
    own_b = jnp.stack([t16(ffn2_gate), t16(ffn2_up), ffn2_down[0].astype(BF16), rest])
    ffn1_weights = _GatherBehind("gather_ffn1", own_a, after=())
    rest_weights = _GatherBehind("gather_rest", own_b, after=[ffn1_weights.token])

    (c_act,) = _rowwise(
        "silu_c", lambda v: v * _sigmoid(v), [c.reshape(8, 128)], [], [_sds((8, 128), F32)], [], 8,
        deps=[rest_weights.token],
    )
    c_all = _allgather_small("allgather_c", c_act).reshape(N_DEV, D_MODEL)
    mod_cols = _mm(
        "mod_fwd", "nn", (1, n_mod_shard // 768, 1),
        c_all, (N_DEV, D_MODEL), lambda i, j, k: (0, 0),
        w_mod[0], (D_MODEL, 768), lambda i, j, k: (0, j),
        _sds((N_DEV, n_mod_shard), F32), (N_DEV, 768), lambda i, j, k: (0, j),
        (N_DEV, 768),
    )
    mod_all = _allgather_small("allgather_mod", mod_cols).reshape(N_CHIPS, 2, N_DEV, n_mod_shard)[:, 0]
    mod_all = mod_all.transpose(1, 0, 2).reshape(N_DEV, N_MOD * D_MODEL)
    mod = lax.dynamic_slice_in_dim(mod_all, dev, 1, axis=0) + b_mod

    norms = (norm_ffn1, norm_mix, norm_ffn2, norm_final.reshape(1, D_MODEL))

    reducing, red = {}, {}

    def ffn2_grads_done(gb2):
        reducing["ffn2"] = _ReduceBehind("ffn2", gb2)
        return [reducing["ffn2"].token]

    def mix_done(dx1, gb_rest):
        red["ffn2"] = reducing["ffn2"].get([dx1])
        reducing["rest"] = _ReduceBehind("rest", gb_rest, after=[red["ffn2"]])
        return [reducing["rest"].token]

    def ffn1_grads_done(gb1):
        red["rest"] = reducing["rest"].get([gb1])
        reducing["ffn1"] = _ReduceBehind("ffn1", gb1, after=[red["rest"]])
        return [reducing["ffn1"].token]

    loss_part, grad_x, _, dmod, small = _device_step(
        x[0], loss_target[0], mod, ffn1_weights, rest_weights, norms, q_norm, kv_norm, sinks, rel_bias,
        hooks2=_BwdHooks(after_wgu=ffn2_grads_done),
        hooks_mix=_BwdHooks(after_gate=lambda dz: reducing["ffn2"].mid(dz)),
        mix_done=mix_done,
        hooks1=_BwdHooks(
            after_swiglu=lambda dab3: reducing["rest"].mid(dab3),
            after_wgu=ffn1_grads_done,
        ),
    )
    loss = lax.psum(loss_part, ("x", "y", "c"))

    gathered, total = _allreduce_small(_pack_small(dmod, *small))
    reducing["ffn1"].mid(total)
    (g_b_mod, g_n1, g_n2, g_n3, g_nf, g_qn, g_kvn, g_sinks, g_rel) = _unpack_small(total)
    dmod_all = gathered[:, :PK_MOD].reshape(N_DEV, N_MOD * D_MODEL)
    dmod_cols = lax.dynamic_slice_in_dim(dmod_all, chip * n_mod_shard, n_mod_shard, axis=1)
    g_w_mod = _mm(
        "mod_bwd", "tn", (1, n_mod_shard // 768, 1),
        c_all, (N_DEV, D_MODEL), lambda i, j, k: (0, 0),
        dmod_cols, (N_DEV, 768), lambda i, j, k: (0, j),
        _sds((D_MODEL, n_mod_shard), F32), (D_MODEL, 768), lambda i, j, k: (0, j),
        (D_MODEL, 768),
    )

    red2, r_rest = red["ffn2"], red["rest"][0]
    transposed = {"ffn1_gate", "ffn1_up", "ffn2_gate", "ffn2_up", "w_in", "w_uq", "w_ukv"}
    g_big = {
        "ffn2_gate": red2[0], "ffn2_up": red2[1], "ffn2_down": red2[2],
        "w_in": r_rest[O_IN:O_IN + R_IN],
        "w_o": r_rest[O_O:O_O + R_O],
        "w_uq": r_rest[O_UQ:O_UQ + R_UQ].reshape(MLA_QK, MLA_Q_RANK),
        "w_ukv": r_rest[O_UKV:O_UKV + R_UKV].reshape(MLA_NOPE + MLA_V, MLA_KV_RANK),
        "w_mod": g_w_mod,
    }
    w_big = {
        "ffn2_gate": (ffn2_gate, m_ffn2_gate, v_ffn2_gate),
        "ffn2_up": (ffn2_up, m_ffn2_up, v_ffn2_up), "ffn2_down": (ffn2_down, m_ffn2_down, v_ffn2_down),
        "w_in": (w_in, m_w_in, v_w_in), "w_o": (w_o, m_w_o, v_w_o), "w_uq": (w_uq, m_w_uq, v_w_uq),
        "w_ukv": (w_ukv, m_w_ukv, v_w_ukv), "w_mod": (w_mod, m_w_mod, v_w_mod),
    }
    grads, deltas, new_m, new_v = {}, {}, {}, {}

    def update(names):
        for nm in names:
            w, m, v = w_big[nm]
            if nm in transposed:
                outs = _adamw(f"adamw_{nm}", w[0].T, g_big[nm], m[0].T, v[0].T)
                g_, d_, m_, v_ = [a.T for a in (g_big[nm],) + tuple(outs)]
            else:
                g_, (d_, m_, v_) = g_big[nm], _adamw(f"adamw_{nm}", w[0], g_big[nm], m[0], v[0])
            grads[nm], deltas[nm], new_m[nm], new_v[nm] = g_[None], d_[None], m_[None], v_[None]

    update(list(w_big))
    red1 = reducing["ffn1"].get([deltas[nm] for nm in w_big])
    g_big.update({"ffn1_gate": red1[0], "ffn1_up": red1[1], "ffn1_down": red1[2]})
    w_big.update({
        "ffn1_gate": (ffn1_gate, m_ffn1_gate, v_ffn1_gate), "ffn1_up": (ffn1_up, m_ffn1_up, v_ffn1_up),
        "ffn1_down": (ffn1_down, m_ffn1_down, v_ffn1_down),
    })
    update(["ffn1_gate", "ffn1_up", "ffn1_down"])

    small_names = ["b_mod", "norm_ffn1", "norm_mix", "norm_ffn2", "norm_final", "q_norm", "kv_norm", "sinks", "rel_bias"]
    w_small = (b_mod, norm_ffn1, norm_mix, norm_ffn2, norm_final, q_norm, kv_norm, sinks, rel_bias)
    m_small = (m_b_mod, m_norm_ffn1, m_norm_mix, m_norm_ffn2, m_norm_final, m_q_norm, m_kv_norm, m_sinks, m_rel_bias)
    v_small = (v_b_mod, v_norm_ffn1, v_norm_mix, v_norm_ffn2, v_norm_final, v_q_norm, v_kv_norm, v_sinks, v_rel_bias)
    d_p, m_p, v_p = _adamw("adamw_small", _pack_small(*w_small), total, _pack_small(*m_small), _pack_small(*v_small))
    for nm, g_, d_, m_, v_ in zip(
        small_names,
        (g_b_mod, g_n1, g_n2, g_n3, g_nf, g_qn, g_kvn, g_sinks, g_rel),
        _unpack_small(d_p), _unpack_small(m_p), _unpack_small(v_p),
    ):
        grads[nm], deltas[nm], new_m[nm], new_v[nm] = g_, d_, m_, v_

    order = ["w_mod", "b_mod", "norm_ffn1", "ffn1_gate", "ffn1_up", "ffn1_down", "norm_mix", "w_in", "q_norm", "kv_norm",
             "w_uq", "w_ukv", "sinks", "w_o", "norm_ffn2", "ffn2_gate", "ffn2_up", "ffn2_down", "rel_bias", "norm_final"]
    return (loss, grad_x[None], *[grads[n] for n in order], *[deltas[n] for n in order],
            *[new_m[n] for n in order], *[new_v[n] for n in order])
```

```python
import functools
import math

import jax
import jax.numpy as jnp
import numpy as np
from jax import lax
from jax.experimental import pallas as pl
from jax.experimental.pallas import tpu as pltpu

F32, BF16 = jnp.float32, jnp.bfloat16

D_MODEL = 1024
D_FF = 2816
EPS = 1e-6
N_MOD = 9
SWA_HEADS, SWA_KV_HEADS, SWA_HEAD_DIM, WINDOW = 8, 2, 64, 128
SWA_GROUP = SWA_HEADS // SWA_KV_HEADS
MLA_HEADS, MLA_Q_RANK, MLA_KV_RANK, MLA_NOPE, MLA_ROPE, MLA_V = 4, 256, 128, 128, 64, 128
MLA_QK = MLA_NOPE + MLA_ROPE
ROPE_THETA = 10000.0
NUM_BUCKETS, MAX_DISTANCE = 32, 128
D_IN = 1216
N_SWA_COLS = 768
N_SWA_BLOCKS = N_SWA_COLS // SWA_HEAD_DIM
N_MLA_COLS = 512

ADAM_LR, ADAM_B1, ADAM_B2, ADAM_EPS, ADAM_WD, ADAM_STEP = 0.001, 0.9, 0.999, 1e-08, 0.01, 10

N_CHIPS = 4
N_DEV = 8
F_SHARD = D_FF // N_CHIPS
HALF = F_SHARD // 2
R_IN, R_O, R_UQ, R_UKV = 304, 256, 48, 32
O_IN, O_O, O_UQ, O_UKV, O_PAD = 0, 304, 560, 608, 640

PK_MOD = 72
PK_ROWS = 112
VMEM_LIMIT = 56 << 20
ROW_TILE = 1024

_DN = {
    "nn": (((1,), (0,)), ((), ())),
    "nt": (((1,), (1,)), ((), ())),
    "tn": (((0,), (0,)), ((), ())),
}


def _sds(shape, dtype):
    return jax.ShapeDtypeStruct(tuple(shape), dtype)


def _cparams(sem=None):
    kw = {"vmem_limit_bytes": VMEM_LIMIT}
    if sem is not None:
        kw["dimension_semantics"] = sem
    return pltpu.CompilerParams(**kw)


def _dot(a, b, dims):
    return lax.dot_general(a.astype(BF16), b.astype(BF16), _DN[dims], preferred_element_type=F32)


def _mm(name, dims, grid, a, a_blk, a_idx, b, b_blk, b_idx, out, out_blk, out_idx, acc_shape, alias=None, deps=()):
    nk = grid[2]

    def body(*refs):
        a_ref, b_ref = refs[:2]
        o_ref, acc_ref = refs[-2:]
        part = _dot(a_ref[...], b_ref[...], dims)
        if nk == 1:
            o_ref[...] = part.astype(o_ref.dtype)
            return
        k = pl.program_id(2)

        @pl.when(k == 0)
        def _():
            acc_ref[...] = part

        @pl.when((k > 0) & (k < nk - 1))
        def _():
            acc_ref[...] += part

        @pl.when(k == nk - 1)
        def _():
            o_ref[...] = (acc_ref[...] + part).astype(o_ref.dtype)

    in_specs = [pl.BlockSpec(a_blk, a_idx), pl.BlockSpec(b_blk, b_idx)]
    args = [a, b]
    kw = {}
    if alias is not None:
        in_specs.append(pl.BlockSpec(memory_space=pl.ANY))
        args.append(alias)
        kw["input_output_aliases"] = {2: 0}
    in_specs += [pl.BlockSpec(memory_space=pl.ANY)] * len(deps)
    args += list(deps)
    return pl.pallas_call(
        body,
        name=name,
        grid=grid,
        in_specs=in_specs,
        out_specs=pl.BlockSpec(out_blk, out_idx),
        out_shape=out,
        scratch_shapes=[pltpu.VMEM(acc_shape if nk > 1 else (8, 128), F32)],
        compiler_params=_cparams(("parallel", "parallel", "arbitrary")),
        **kw,
    )(*args)


def _rowwise(name, fn, tiled, full, out_tiled, out_red, tm, deps=()):
    rows = tiled[0].shape[-2]
    grid = (rows // tm,)
    n_in = len(tiled) + len(full)
    n_t = len(out_tiled)
    n_dep = len(deps)

    def tspec(shape):
        nd = len(shape)
        return pl.BlockSpec(tuple(shape[:-2]) + (tm, shape[-1]), lambda i, nd=nd: (0,) * (nd - 2) + (i, 0))

    def fspec(shape):
        nd = len(shape)
        return pl.BlockSpec(tuple(shape), lambda i, nd=nd: (0,) * nd)

    def body(*refs):
        outs = fn(*[r[...] for r in refs[:n_in]])
        if not isinstance(outs, (tuple, list)):
            outs = (outs,)
        out_refs = refs[n_in + n_dep:]
        for r, v in zip(out_refs[:n_t], outs[:n_t]):
            r[...] = v.astype(r.dtype)
        red_refs = out_refs[n_t:]
        if red_refs:
            @pl.when(pl.program_id(0) == 0)
            def _():
                for r in red_refs:
                    r[...] = jnp.zeros_like(r)

            for r, v in zip(red_refs, outs[n_t:]):
                r[...] += v.astype(r.dtype)

    res = pl.pallas_call(
        body,
        name=name,
        grid=grid,
        in_specs=[tspec(a.shape) for a in tiled] + [fspec(a.shape) for a in full]
        + [pl.BlockSpec(memory_space=pl.ANY)] * n_dep,
        out_specs=[tspec(o.shape) for o in out_tiled] + [fspec(o.shape) for o in out_red],
        out_shape=list(out_tiled) + list(out_red),
        compiler_params=_cparams(("arbitrary",) if out_red else ("parallel",)),
    )(*tiled, *full, *deps)
    return res


def _sum0(v):
    return jnp.sum(v, axis=0, keepdims=True)


def _sigmoid(v):
    return 1.0 / (1.0 + jnp.exp(-v))


def _rms(x):
    r = lax.rsqrt(jnp.mean(x * x, axis=-1, keepdims=True) + EPS)
    return x * r, r


def _rms_bwd(u, xr, r):
    return r * (u - xr * jnp.mean(u * xr, axis=-1, keepdims=True))


class _Ready:
    def __init__(self, g):
        self.g = g

    def mid(self, after):
        return []

    def get(self, after):
        return self.g


def _ffn_fwd(tag, x, gamma, sh, sc, gate, gu_src, base, down_src, down_idx, mid_after):
    s_len = x.shape[0]
    deps = gu_src.mid(mid_after)

    def normmod(x_, gamma_, sc_, sh_):
        xr, _ = _rms(x_)
        return xr * gamma_ * (1.0 + sc_) + sh_

    (h,) = _rowwise(f"{tag}_normmod", normmod, [x], [gamma, sc, sh], [_sds((s_len, D_MODEL), BF16)], [], 256, deps=deps)
    g4 = gu_src.get(h)

    tm = min(ROW_TILE, s_len)
    ab3 = _mm(
        f"{tag}_gate_up", "nt", (s_len // tm, 2 * N_CHIPS, 1),
        h, (tm, D_MODEL), lambda i, j, k: (i, 0),
        g4, (None, None, F_SHARD, D_MODEL), lambda i, j, k: (j % N_CHIPS, base + j // N_CHIPS, 0, 0),
        _sds((2 * N_CHIPS, s_len, F_SHARD), BF16), (None, tm, F_SHARD), lambda i, j, k: (j, i, 0),
        (tm, F_SHARD),
    )

    def swiglu(ab):
        a = ab[:N_CHIPS].astype(F32)
        b = ab[N_CHIPS:].astype(F32)
        return a * _sigmoid(a) * b

    (s3,) = _rowwise(
        f"{tag}_swiglu", swiglu, [ab3], [], [_sds((N_CHIPS, s_len, F_SHARD), BF16)], [], 128, deps=down_src.mid(ab3)
    )
    g_down = down_src.get(s3)

    y = _mm(
        f"{tag}_down", "nn", (s_len // tm, 1, N_CHIPS),
        s3, (None, tm, F_SHARD), lambda i, j, k: (k, i, 0),
        g_down, (None, None, F_SHARD, D_MODEL), lambda i, j, k: (k, down_idx, 0, 0),
        _sds((s_len, D_MODEL), F32), (tm, D_MODEL), lambda i, j, k: (i, 0),
        (tm, D_MODEL),
    )

    (x_out,) = _rowwise(
        f"{tag}_residual", lambda x_, y_, g_: x_ + 0.5 * g_ * y_, [x, y], [gate], [_sds((s_len, D_MODEL), F32)], [], 256
    )
    return x_out, (g4, base, g_down, down_idx, h, ab3, s3, y)


def _normmod_bwd_call(name, dh_parts, x, dxo, gamma, sc, deps=()):
    s_len = x.shape[0]
    n_parts = len(dh_parts)

    def fn(*vals):
        dh = vals[0]
        for extra in vals[1:n_parts]:
            dh = dh + extra
        x_, dxo_, gamma_, sc_ = vals[n_parts:]
        xr, r = _rms(x_)
        n = xr * gamma_
        dn = dh * (1.0 + sc_)
        dx = dxo_ + _rms_bwd(dn * gamma_, xr, r)
        return dx, _sum0(dh * n), _sum0(dh), _sum0(dn * xr)

    vec = _sds((1, D_MODEL), F32)
    return _rowwise(
        name, fn, list(dh_parts) + [x, dxo], [gamma, sc], [_sds((s_len, D_MODEL), F32)], [vec, vec, vec], 256, deps=deps
    )


class _BwdHooks:
    def __init__(self, after_gate=None, after_swiglu=None, after_wgu=None, after_dh=None):
        def nothing(_):
            return []

        self.after_gate = after_gate or nothing
        self.after_swiglu = after_swiglu or nothing
        self.after_wgu = after_wgu or nothing
        self.after_dh = after_dh or nothing


def _ffn_bwd(tag, dxo, x, gamma, sc, gate, saved, hooks, deps=()):
    g4, base, g_down, down_idx, h, ab3, s3, y = saved
    s_len = x.shape[0]
    vec = _sds((1, D_MODEL), F32)

    dy, dgate = _rowwise(
        f"{tag}_bwd_gate", lambda dxo_, y_, g_: (0.5 * g_ * dxo_, _sum0(0.5 * y_ * dxo_)),
        [dxo, y], [gate], [_sds((s_len, D_MODEL), BF16)], [vec], 256, deps=deps,
    )

    tm = min(ROW_TILE, s_len)
    ds3 = _mm(
        f"{tag}_bwd_ds", "nt", (s_len // tm, N_CHIPS, 1),
        dy, (tm, D_MODEL), lambda i, j, k: (i, 0),
        g_down, (None, None, F_SHARD, D_MODEL), lambda i, j, k: (j, down_idx, 0, 0),
        _sds((N_CHIPS, s_len, F_SHARD), BF16), (None, tm, F_SHARD), lambda i, j, k: (j, i, 0),
        (tm, F_SHARD),
    )

    def swiglu_bwd(ab, ds):
        a = ab[:N_CHIPS].astype(F32)
        b = ab[N_CHIPS:].astype(F32)
        ds = ds.astype(F32)
        sig = _sigmoid(a)
        da = ds * b * sig * (1.0 + a * (1.0 - sig))
        db = ds * a * sig
        return jnp.concatenate([da, db], axis=0)

    (dab3,) = _rowwise(
        f"{tag}_bwd_swiglu", swiglu_bwd, [ab3, ds3], [], [_sds((2 * N_CHIPS, s_len, F_SHARD), BF16)], [], 128
    )

    tk = tm
    gb = _mm(
        f"{tag}_bwd_wdown", "tn", (N_CHIPS, 1, s_len // tk),
        s3, (None, tk, F_SHARD), lambda i, j, k: (i, k, 0),
        dy, (tk, D_MODEL), lambda i, j, k: (k, 0),
        _sds((N_CHIPS, 3, F_SHARD, D_MODEL), BF16), (None, None, F_SHARD, D_MODEL), lambda i, j, k: (i, 2, 0, 0),
        (F_SHARD, D_MODEL), deps=hooks.after_swiglu(dab3),
    )
    gb = _mm(
        f"{tag}_bwd_wgu", "tn", (2 * N_CHIPS, 1, s_len // tk),
        dab3, (None, tk, F_SHARD), lambda i, j, k: (i, k, 0),
        h, (tk, D_MODEL), lambda i, j, k: (k, 0),
        _sds(gb.shape, BF16), (None, None, F_SHARD, D_MODEL), lambda i, j, k: (i % N_CHIPS, i // N_CHIPS, 0, 0),
        (F_SHARD, D_MODEL), alias=gb,
    )
    dh = _mm(
        f"{tag}_bwd_dh", "nn", (s_len // tm, 1, 2 * N_CHIPS),
        dab3, (None, tm, F_SHARD), lambda i, j, k: (k, i, 0),
        g4, (None, None, F_SHARD, D_MODEL), lambda i, j, k: (k % N_CHIPS, base + k // N_CHIPS, 0, 0),
        _sds((s_len, D_MODEL), F32), (tm, D_MODEL), lambda i, j, k: (i, 0),
        (tm, D_MODEL), deps=hooks.after_wgu(gb),
    )
    dx, dsc, dsh, dgamma = _normmod_bwd_call(
        f"{tag}_bwd_normmod", [dh], x, dxo, gamma, sc, deps=hooks.after_dh(dh)
    )
    return dx, gb, (dsh, dsc, dgate, dgamma)


def _bucket_map():
    qi = np.arange(WINDOW)[:, None]
    kj = np.arange(2 * WINDOW)[None, :]
    dist = qi + WINDOW - kj
    band = (dist >= 0) & (dist < WINDOW)
    max_exact = NUM_BUCKETS // 2
    n = np.maximum(dist, 0)
    large = []
    for dt in (np.float32, np.float64):
        nf = np.maximum(n, 1).astype(dt)
        val = np.log(nf / dt(max_exact)) / dt(math.log(MAX_DISTANCE / max_exact)) * dt(NUM_BUCKETS - max_exact)
        large.append(np.minimum(max_exact + val.astype(np.int32), NUM_BUCKETS - 1))
    assert np.array_equal(large[0], large[1])
    bucket = np.where(n < max_exact, n, large[0])
    return np.where(band, bucket, -1).astype(np.int32).reshape(1, -1)


def _bias_expand(rel_bias_t, bkt):
    n = bkt.shape[1]

    def body(rb_ref, bkt_ref, o_ref):
        onehot = (lax.broadcasted_iota(jnp.int32, (NUM_BUCKETS, n), 0) == bkt_ref[...]).astype(F32)
        o_ref[...] = lax.dot_general(
            rb_ref[...], onehot, _DN["nn"], precision=lax.Precision.HIGHEST, preferred_element_type=F32
        )

    return pl.pallas_call(
        body, name="bias_expand", out_shape=_sds((SWA_HEADS, n), F32), compiler_params=_cparams()
    )(rel_bias_t, bkt)


def _bias_reduce(dbias_flat, bkt):
    n = bkt.shape[1]

    def body(db_ref, bkt_ref, o_ref):
        onehot = (lax.broadcasted_iota(jnp.int32, (NUM_BUCKETS, n), 0) == bkt_ref[...]).astype(F32)
        o_ref[...] = lax.dot_general(
            db_ref[...], onehot, _DN["nt"], precision=lax.Precision.HIGHEST, preferred_element_type=F32
        )

    return pl.pallas_call(
        body, name="bias_reduce", out_shape=_sds((SWA_HEADS, NUM_BUCKETS), F32), compiler_params=_cparams()
    )(dbias_flat, bkt)


_SWA_SCALE = SWA_HEAD_DIM ** -0.5
_NEG = -1e30


def _swa_in_specs():
    g, w, hd = SWA_GROUP, WINDOW, SWA_HEAD_DIM
    prev = lambda n: jnp.maximum(n - 1, 0)
    return [
        pl.BlockSpec((g, w, hd), lambda j, n: (j, n, 0)),
        pl.BlockSpec((1, w, hd), lambda j, n: (SWA_HEADS + j, prev(n), 0)),
        pl.BlockSpec((1, w, hd), lambda j, n: (SWA_HEADS + j, n, 0)),
        pl.BlockSpec((1, w, hd), lambda j, n: (SWA_HEADS + SWA_KV_HEADS + j, prev(n), 0)),
        pl.BlockSpec((1, w, hd), lambda j, n: (SWA_HEADS + SWA_KV_HEADS + j, n, 0)),
        pl.BlockSpec((g, w, 2 * w), lambda j, n: (j, 0, 0)),
        pl.BlockSpec((g, w, 1), lambda j, n: (j, 0, 0)),
    ]


def _swa_scores(q_ref, kp_ref, kc_ref, bias_ref, n):
    g, w = SWA_GROUP, WINDOW
    q = q_ref[...].reshape(g * w, SWA_HEAD_DIM)
    kk = jnp.concatenate([kp_ref[0], kc_ref[0]], axis=0)
    s = (_dot(q, kk, "nt") * _SWA_SCALE).reshape(g, w, 2 * w) + bias_ref[...]
    qi = lax.broadcasted_iota(jnp.int32, (w, 2 * w), 0)
    kj = lax.broadcasted_iota(jnp.int32, (w, 2 * w), 1)
    dist = qi + w - kj
    valid = (dist >= 0) & (dist < w) & ((n > 0) | (kj >= w))
    return q, kk, jnp.where(valid[None], s, _NEG), valid[None]


def _swa_fwd(swa3, bias, sinkb):
    s_len = swa3.shape[1]
    g, w, hd = SWA_GROUP, WINDOW, SWA_HEAD_DIM

    def body(q_ref, kp_ref, kc_ref, vp_ref, vc_ref, bias_ref, sink_ref, o_ref, lse_ref):
        n = pl.program_id(1)
        _, _, s, valid = _swa_scores(q_ref, kp_ref, kc_ref, bias_ref, n)
        vv = jnp.concatenate([vp_ref[0], vc_ref[0]], axis=0)
        sink = sink_ref[...]
        m = jnp.maximum(jnp.max(s, axis=-1, keepdims=True), sink)
        p = jnp.where(valid, jnp.exp(s - m), 0.0)
        l = jnp.sum(p, axis=-1, keepdims=True) + jnp.exp(sink - m)
        o = _dot(p.reshape(g * w, 2 * w), vv, "nn").reshape(g, w, hd)
        o_ref[...] = (o / l).astype(o_ref.dtype)
        lse_ref[...] = m + jnp.log(l)

    return pl.pallas_call(
        body,
        name="swa_fwd",
        grid=(SWA_KV_HEADS, s_len // w),
        in_specs=_swa_in_specs(),
        out_specs=[
            pl.BlockSpec((g, w, hd), lambda j, n: (j, n, 0)),
            pl.BlockSpec((g, w, 1), lambda j, n: (j, n, 0)),
        ],
        out_shape=[_sds((SWA_HEADS, s_len, hd), BF16), _sds((SWA_HEADS, s_len, 1), F32)],
        compiler_params=_cparams(("parallel", "parallel")),
    )(swa3, swa3, swa3, swa3, swa3, bias, sinkb)


def _swa_bwd(swa3, bias, sinkb, o3, do3, lse):
    s_len = swa3.shape[1]
    g, w, hd = SWA_GROUP, WINDOW, SWA_HEAD_DIM

    def body(q_ref, kp_ref, kc_ref, vp_ref, vc_ref, bias_ref, sink_ref, o_ref, do_ref, lse_ref,
             dq_ref, dka_ref, dkb_ref, dva_ref, dvb_ref, dbias_ref, dsink_ref):
        n = pl.program_id(1)

        @pl.when(n == 0)
        def _():
            dbias_ref[...] = jnp.zeros_like(dbias_ref)
            dsink_ref[...] = jnp.zeros_like(dsink_ref)

        q, kk, s, valid = _swa_scores(q_ref, kp_ref, kc_ref, bias_ref, n)
        vv = jnp.concatenate([vp_ref[0], vc_ref[0]], axis=0)
        lse_v = lse_ref[...]
        p = jnp.where(valid, jnp.exp(s - lse_v), 0.0)
        do = do_ref[...].astype(F32)
        delta = jnp.sum(do * o_ref[...].astype(F32), axis=-1, keepdims=True)
        do2 = do.reshape(g * w, hd)
        dp = _dot(do2, vv, "nt").reshape(g, w, 2 * w)
        ds = p * (dp - delta)
        dbias_ref[...] += ds
        dsink_ref[...] -= jnp.exp(sink_ref[...] - lse_v) * delta
        ds2 = ds.reshape(g * w, 2 * w)
        dq_ref[...] = (_dot(ds2, kk, "nn") * _SWA_SCALE).reshape(g, w, hd).astype(dq_ref.dtype)
        dkk = _dot(ds2, q, "tn") * _SWA_SCALE
        dvv = _dot(p.reshape(g * w, 2 * w), do2, "tn")
        dkb_ref[0] = dkk[:w]
        dka_ref[0] = dkk[w:]
        dvb_ref[0] = dvv[:w]
        dva_ref[0] = dvv[w:]

    kv_spec = pl.BlockSpec((1, w, hd), lambda j, n: (j, n, 0))
    kv_sds = _sds((SWA_KV_HEADS, s_len, hd), F32)
    return pl.pallas_call(
        body,
        name="swa_bwd",
        grid=(SWA_KV_HEADS, s_len // w),
        in_specs=_swa_in_specs() + [
            pl.BlockSpec((g, w, hd), lambda j, n: (j, n, 0)),
            pl.BlockSpec((g, w, hd), lambda j, n: (j, n, 0)),
            pl.BlockSpec((g, w, 1), lambda j, n: (j, n, 0)),
        ],
        out_specs=[
            pl.BlockSpec((g, w, hd), lambda j, n: (j, n, 0)),
            kv_spec, kv_spec, kv_spec, kv_spec,
            pl.BlockSpec((g, w, 2 * w), lambda j, n: (j, 0, 0)),
            pl.BlockSpec((g, w, 1), lambda j, n: (j, 0, 0)),
        ],
        out_shape=[
            _sds((SWA_HEADS, s_len, hd), BF16), kv_sds, kv_sds, kv_sds, kv_sds,
            _sds((SWA_HEADS, w, 2 * w), F32), _sds((SWA_HEADS, w, 1), F32),
        ],
        compiler_params=_cparams(("parallel", "arbitrary")),
    )(swa3, swa3, swa3, swa3, swa3, bias, sinkb, o3, do3, lse)


_MLA_SCALE = MLA_QK ** -0.5
_MLA_TQ = 256


def _mla_mask(i, tq, s_len):
    row = i * tq + lax.broadcasted_iota(jnp.int32, (tq, s_len), 0)
    col = lax.broadcasted_iota(jnp.int32, (tq, s_len), 1)
    return col <= row


def _mla_fwd(q4, k4, v4):
    s_len = q4.shape[1]
    tq = min(_MLA_TQ, s_len)

    def body(q_ref, k_ref, v_ref, o_ref, lse_ref):
        mask = _mla_mask(pl.program_id(1), tq, s_len)
        s = jnp.where(mask, _dot(q_ref[...], k_ref[...], "nt") * _MLA_SCALE, _NEG)
        m = jnp.max(s, axis=-1, keepdims=True)
        p = jnp.exp(s - m)
        l = jnp.sum(p, axis=-1, keepdims=True)
        o_ref[...] = (_dot(p, v_ref[...], "nn") / l).astype(o_ref.dtype)
        lse_ref[...] = m + jnp.log(l)

    return pl.pallas_call(
        body,
        name="mla_fwd",
        grid=(MLA_HEADS, s_len // tq),
        in_specs=[
            pl.BlockSpec((None, tq, MLA_QK), lambda h, i: (h, i, 0)),
            pl.BlockSpec((None, s_len, MLA_QK), lambda h, i: (h, 0, 0)),
            pl.BlockSpec((None, s_len, MLA_V), lambda h, i: (h, 0, 0)),
        ],
        out_specs=[
            pl.BlockSpec((None, tq, MLA_V), lambda h, i: (h, i, 0)),
            pl.BlockSpec((None, tq, 1), lambda h, i: (h, i, 0)),
        ],
        out_shape=[_sds((MLA_HEADS, s_len, MLA_V), BF16), _sds((MLA_HEADS, s_len, 1), F32)],
        compiler_params=_cparams(("parallel", "parallel")),
    )(q4, k4, v4)


def _mla_bwd(q4, k4, v4, o4, do4, lse):
    s_len = q4.shape[1]
    tq = min(_MLA_TQ, s_len)

    def body(q_ref, k_ref, v_ref, o_ref, do_ref, lse_ref, dq_ref, dk_ref, dv_ref):
        i = pl.program_id(1)

        @pl.when(i == 0)
        def _():
            dk_ref[...] = jnp.zeros_like(dk_ref)
            dv_ref[...] = jnp.zeros_like(dv_ref)

        mask = _mla_mask(i, tq, s_len)
        q, k, v = q_ref[...], k_ref[...], v_ref[...]
        s = jnp.where(mask, _dot(q, k, "nt") * _MLA_SCALE, _NEG)
        p = jnp.where(mask, jnp.exp(s - lse_ref[...]), 0.0)
        do = do_ref[...]
        delta = jnp.sum(do.astype(F32) * o_ref[...].astype(F32), axis=-1, keepdims=True)
        dp = _dot(do, v, "nt")
        ds = (p * (dp - delta) * _MLA_SCALE).astype(BF16)
        dq_ref[...] = _dot(ds, k, "nn")
        dk_ref[...] += _dot(ds, q, "tn")
        dv_ref[...] += _dot(p, do, "tn")

    return pl.pallas_call(
        body,
        name="mla_bwd",
        grid=(MLA_HEADS, s_len // tq),
        in_specs=[
            pl.BlockSpec((None, tq, MLA_QK), lambda h, i: (h, i, 0)),
            pl.BlockSpec((None, s_len, MLA_QK), lambda h, i: (h, 0, 0)),
            pl.BlockSpec((None, s_len, MLA_V), lambda h, i: (h, 0, 0)),
            pl.BlockSpec((None, tq, MLA_V), lambda h, i: (h, i, 0)),
            pl.BlockSpec((None, tq, MLA_V), lambda h, i: (h, i, 0)),
            pl.BlockSpec((None, tq, 1), lambda h, i: (h, i, 0)),
        ],
        out_specs=[
            pl.BlockSpec((None, tq, MLA_QK), lambda h, i: (h, i, 0)),
            pl.BlockSpec((None, s_len, MLA_QK), lambda h, i: (h, 0, 0)),
            pl.BlockSpec((None, s_len, MLA_V), lambda h, i: (h, 0, 0)),
        ],
        out_shape=[
            _sds((MLA_HEADS, s_len, MLA_QK), F32),
            _sds((MLA_HEADS, s_len, MLA_QK), F32),
            _sds((MLA_HEADS, s_len, MLA_V), F32),
        ],
        compiler_params=_cparams(("parallel", "arbitrary")),
    )(q4, k4, v4, o4, do4, lse)


def _mla_split(pm):
    q_lat = pm[:, :MLA_Q_RANK]
    kv_lat = pm[:, MLA_Q_RANK:MLA_Q_RANK + MLA_KV_RANK]
    kr = pm[:, MLA_Q_RANK + MLA_KV_RANK:MLA_Q_RANK + MLA_KV_RANK + MLA_ROPE]
    kr_sw = pm[:, MLA_Q_RANK + MLA_KV_RANK + MLA_ROPE:]
    return q_lat, kv_lat, kr, kr_sw


def _mla_proj(pm, cos2, sin2, q_norm, kv_norm, wq_ext, wkv):
    s_len = pm.shape[0]

    def fn(pm_, cos_, sin_, qg, kvg, wq, wk):
        q_lat, kv_lat, kr, kr_sw = _mla_split(pm_)
        nq = (_rms(q_lat)[0] * qg).astype(BF16)
        nkv = (_rms(kv_lat)[0] * kvg).astype(BF16)
        k_rot = kr * cos_ + kr_sw * sin_
        qs, ks, vs = [], [], []
        for h in range(MLA_HEADS):
            qe = _dot(nq, wq[h], "nt")
            q_rot = qe[:, MLA_NOPE:MLA_QK] * cos_ + qe[:, MLA_QK:] * sin_
            qs.append(jnp.concatenate([qe[:, :MLA_NOPE], q_rot], axis=-1))
            kve = _dot(nkv, wk[h], "nt")
            ks.append(jnp.concatenate([kve[:, :MLA_NOPE], k_rot], axis=-1))
            vs.append(kve[:, MLA_NOPE:])
        return jnp.stack(qs), jnp.stack(ks), jnp.stack(vs)

    return _rowwise(
        "mla_proj", fn, [pm, cos2, sin2], [q_norm, kv_norm, wq_ext, wkv],
        [_sds((MLA_HEADS, s_len, MLA_QK), BF16), _sds((MLA_HEADS, s_len, MLA_QK), BF16),
         _sds((MLA_HEADS, s_len, MLA_V), BF16)], [], 256,
    )


def _mla_proj_bwd(pm, cos2, sin2, dq4, dk4, dv4, q_norm, kv_norm, wq_ext, wkv):
    s_len = pm.shape[0]

    def fn(pm_, cos_, sin_, dq, dk, dv, qg, kvg, wq, wk):
        q_lat, kv_lat, _, _ = _mla_split(pm_)
        xq, rq = _rms(q_lat)
        xkv, rkv = _rms(kv_lat)
        nq = (xq * qg).astype(BF16)
        nkv = (xkv * kvg).astype(BF16)
        dnq = jnp.zeros_like(q_lat)
        dnkv = jnp.zeros_like(kv_lat)
        dkr = jnp.zeros_like(cos_)
        dwq, dwk = [], []
        for h in range(MLA_HEADS):
            dy = dq[h][:, MLA_NOPE:]
            dqe = jnp.concatenate([dq[h][:, :MLA_NOPE], dy * cos_, dy * sin_], axis=-1).astype(BF16)
            dnq = dnq + _dot(dqe, wq[h], "nn")
            dwq.append(_dot(dqe, nq, "tn"))
            dkve = jnp.concatenate([dk[h][:, :MLA_NOPE], dv[h]], axis=-1).astype(BF16)
            dnkv = dnkv + _dot(dkve, wk[h], "nn")
            dwk.append(_dot(dkve, nkv, "tn"))
            dkr = dkr + dk[h][:, MLA_NOPE:]
        dq_lat = _rms_bwd(dnq * qg, xq, rq)
        dkv_lat = _rms_bwd(dnkv * kvg, xkv, rkv)
        dpm = jnp.concatenate([dq_lat, dkv_lat, dkr * cos_, dkr * sin_], axis=-1)
        return dpm, _sum0(dnq * xq), _sum0(dnkv * xkv), jnp.stack(dwq), jnp.stack(dwk)

    return _rowwise(
        "mla_proj_bwd", fn, [pm, cos2, sin2, dq4, dk4, dv4], [q_norm, kv_norm, wq_ext, wkv],
        [_sds((s_len, N_MLA_COLS), BF16)],
        [_sds((1, MLA_Q_RANK), F32), _sds((1, MLA_KV_RANK), F32),
         _sds(wq_ext.shape, F32), _sds(wkv.shape, F32)], 256,
    )


def _rope_tables(s_len):
    inv = ROPE_THETA ** (-jnp.arange(0, MLA_ROPE, 2, dtype=F32) / MLA_ROPE)
    ang = jnp.arange(s_len, dtype=F32)[:, None] * inv[None, :]
    cos, sin = jnp.cos(ang), jnp.sin(ang)
    return jnp.concatenate([cos, cos], axis=-1), jnp.concatenate([-sin, sin], axis=-1)


def _mix_weights(g4b):
    rest = g4b[:, 3]
    w_in_t = rest[:, O_IN:O_IN + R_IN].reshape(D_IN, D_MODEL)
    w_o = rest[:, O_O:O_O + R_O].reshape(D_MODEL, D_MODEL)
    w_uq_t = rest[:, O_UQ:O_UQ + R_UQ].reshape(MLA_HEADS, MLA_QK, MLA_Q_RANK)
    w_ukv_t = rest[:, O_UKV:O_UKV + R_UKV].reshape(MLA_HEADS, MLA_NOPE + MLA_V, MLA_KV_RANK)
    half = MLA_ROPE // 2
    w_swa = w_in_t[:N_SWA_COLS]
    w_mla = jnp.concatenate([w_in_t[N_SWA_COLS:], w_in_t[D_IN - half:], w_in_t[D_IN - MLA_ROPE:D_IN - half]], axis=0)
    wq_ext = jnp.concatenate([w_uq_t, w_uq_t[:, MLA_QK - half:], w_uq_t[:, MLA_NOPE:MLA_QK - half]], axis=1)
    return w_swa, w_mla, w_o, wq_ext, w_ukv_t


def _mix_fwd(x, gamma, sh, sc, gate, rest_weights, q_norm, kv_norm, bias, sinkb, cos2, sin2):
    s_len = x.shape[0]
    tm = min(ROW_TILE, s_len)

    def normmod(x_, gamma_, sc_, sh_):
        return _rms(x_)[0] * gamma_ * (1.0 + sc_) + sh_

    deps = rest_weights.mid(x)
    (h,) = _rowwise("mix_normmod", normmod, [x], [gamma, sc, sh], [_sds((s_len, D_MODEL), BF16)], [], 256, deps=deps)
    g4b = rest_weights.get(h)
    weights = _mix_weights(g4b)
    w_swa, w_mla, w_o, wq_ext, wkv = weights
    swa3 = _mm(
        "mix_proj_swa", "nt", (s_len // tm, N_SWA_BLOCKS, 1),
        h, (tm, D_MODEL), lambda i, j, k: (i, 0),
        w_swa, (SWA_HEAD_DIM, D_MODEL), lambda i, j, k: (j, 0),
        _sds((N_SWA_BLOCKS, s_len, SWA_HEAD_DIM), BF16), (None, tm, SWA_HEAD_DIM), lambda i, j, k: (j, i, 0),
        (tm, SWA_HEAD_DIM),
    )
    pm = _mm(
        "mix_proj_mla", "nt", (s_len // tm, 1, 1),
        h, (tm, D_MODEL), lambda i, j, k: (i, 0),
        w_mla, (N_MLA_COLS, D_MODEL), lambda i, j, k: (0, 0),
        _sds((s_len, N_MLA_COLS), F32), (tm, N_MLA_COLS), lambda i, j, k: (i, 0),
        (tm, N_MLA_COLS),
    )
    q4, k4, v4 = _mla_proj(pm, cos2, sin2, q_norm, kv_norm, wq_ext, wkv)
    oa3, lse_a = _swa_fwd(swa3, bias, sinkb)
    ob4, lse_b = _mla_fwd(q4, k4, v4)
    n_a = SWA_HEADS * SWA_HEAD_DIM
    za = _mm(
        "mix_out_swa", "nn", (s_len // tm, 1, SWA_HEADS),
        oa3, (None, tm, SWA_HEAD_DIM), lambda i, j, k: (k, i, 0),
        w_o, (SWA_HEAD_DIM, D_MODEL), lambda i, j, k: (k, 0),
        _sds((s_len, D_MODEL), F32), (tm, D_MODEL), lambda i, j, k: (i, 0),
        (tm, D_MODEL),
    )
    zb = _mm(
        "mix_out_mla", "nn", (s_len // tm, 1, MLA_HEADS),
        ob4, (None, tm, MLA_V), lambda i, j, k: (k, i, 0),
        w_o, (MLA_V, D_MODEL), lambda i, j, k: (n_a // MLA_V + k, 0),
        _sds((s_len, D_MODEL), F32), (tm, D_MODEL), lambda i, j, k: (i, 0),
        (tm, D_MODEL),
    )
    (x_out,) = _rowwise(
        "mix_residual", lambda x_, za_, zb_, g_: x_ + g_ * (za_ + zb_), [x, za, zb], [gate],
        [_sds((s_len, D_MODEL), F32)], [], 256,
    )
    return x_out, g4b, (weights, h, swa3, pm, q4, k4, v4, oa3, lse_a, ob4, lse_b, za, zb)


def _mix_bwd(dxo, x, gamma, sc, gate, q_norm, kv_norm, bias, sinkb, cos2, sin2, saved, hooks):
    weights, h, swa3, pm, q4, k4, v4, oa3, lse_a, ob4, lse_b, za, zb = saved
    w_swa, w_mla, w_o, wq_ext, wkv = weights
    s_len = x.shape[0]
    tm = tk = min(ROW_TILE, s_len)
    vec = _sds((1, D_MODEL), F32)
    n_a = SWA_HEADS * SWA_HEAD_DIM

    dz, dgate = _rowwise(
        "mix_bwd_gate", lambda dxo_, za_, zb_, g_: (g_ * dxo_, _sum0((za_ + zb_) * dxo_)),
        [dxo, za, zb], [gate], [_sds((s_len, D_MODEL), BF16)], [vec], 256,
    )
    dwo_a = _mm(
        "mix_bwd_wo_swa", "tn", (SWA_HEADS, 1, s_len // tk),
        oa3, (None, tk, SWA_HEAD_DIM), lambda i, j, k: (i, k, 0),
        dz, (tk, D_MODEL), lambda i, j, k: (k, 0),
        _sds((n_a, D_MODEL), F32), (SWA_HEAD_DIM, D_MODEL), lambda i, j, k: (i, 0),
        (SWA_HEAD_DIM, D_MODEL),
    )
    dwo_b = _mm(
        "mix_bwd_wo_mla", "tn", (MLA_HEADS, 1, s_len // tk),
        ob4, (None, tk, MLA_V), lambda i, j, k: (i, k, 0),
        dz, (tk, D_MODEL), lambda i, j, k: (k, 0),
        _sds((MLA_HEADS * MLA_V, D_MODEL), F32), (MLA_V, D_MODEL), lambda i, j, k: (i, 0),
        (MLA_V, D_MODEL),
    )
    doa3 = _mm(
        "mix_bwd_do_swa", "nt", (s_len // tm, SWA_HEADS, 1),
        dz, (tm, D_MODEL), lambda i, j, k: (i, 0),
        w_o, (SWA_HEAD_DIM, D_MODEL), lambda i, j, k: (j, 0),
        _sds((SWA_HEADS, s_len, SWA_HEAD_DIM), BF16), (None, tm, SWA_HEAD_DIM), lambda i, j, k: (j, i, 0),
        (tm, SWA_HEAD_DIM), deps=hooks.after_gate(dz),
    )
    dob4 = _mm(
        "mix_bwd_do_mla", "nt", (s_len // tm, MLA_HEADS, 1),
        dz, (tm, D_MODEL), lambda i, j, k: (i, 0),
        w_o, (MLA_V, D_MODEL), lambda i, j, k: (n_a // MLA_V + j, 0),
        _sds((MLA_HEADS, s_len, MLA_V), BF16), (None, tm, MLA_V), lambda i, j, k: (j, i, 0),
        (tm, MLA_V),
    )
    dq3, dka, dkb, dva, dvb, dbias, dsink = _swa_bwd(swa3, bias, sinkb, oa3, doa3, lse_a)
    dq4, dk4, dv4 = _mla_bwd(q4, k4, v4, ob4, dob4, lse_b)
    dpm, dq_norm, dkv_norm, dwq_ext, dwkv = _mla_proj_bwd(pm, cos2, sin2, dq4, dk4, dv4, q_norm, kv_norm, wq_ext, wkv)

    def fold(da, db):
        return da + jnp.concatenate([db[:, WINDOW:], jnp.zeros_like(db[:, :WINDOW])], axis=1)

    dswa3 = jnp.concatenate([dq3, fold(dka, dkb).astype(BF16), fold(dva, dvb).astype(BF16)], axis=0)

    dw_swa = _mm(
        "mix_bwd_win_swa", "tn", (N_SWA_BLOCKS, 1, s_len // tk),
        dswa3, (None, tk, SWA_HEAD_DIM), lambda i, j, k: (i, k, 0),
        h, (tk, D_MODEL), lambda i, j, k: (k, 0),
        _sds((N_SWA_COLS, D_MODEL), F32), (SWA_HEAD_DIM, D_MODEL), lambda i, j, k: (i, 0),
        (SWA_HEAD_DIM, D_MODEL),
    )
    dw_mla = _mm(
        "mix_bwd_win_mla", "tn", (1, 1, s_len // tk),
        dpm, (tk, N_MLA_COLS), lambda i, j, k: (k, 0),
        h, (tk, D_MODEL), lambda i, j, k: (k, 0),
        _sds((N_MLA_COLS, D_MODEL), F32), (N_MLA_COLS, D_MODEL), lambda i, j, k: (0, 0),
        (N_MLA_COLS, D_MODEL),
    )
    dh_a = _mm(
        "mix_bwd_dh_swa", "nn", (s_len // tm, 1, N_SWA_BLOCKS),
        dswa3, (None, tm, SWA_HEAD_DIM), lambda i, j, k: (k, i, 0),
        w_swa, (SWA_HEAD_DIM, D_MODEL), lambda i, j, k: (k, 0),
        _sds((s_len, D_MODEL), F32), (tm, D_MODEL), lambda i, j, k: (i, 0),
        (tm, D_MODEL),
    )
    dh_b = _mm(
        "mix_bwd_dh_mla", "nn", (s_len // tm, 1, 1),
        dpm, (tm, N_MLA_COLS), lambda i, j, k: (i, 0),
        w_mla, (N_MLA_COLS, D_MODEL), lambda i, j, k: (0, 0),
        _sds((s_len, D_MODEL), F32), (tm, D_MODEL), lambda i, j, k: (i, 0),
        (tm, D_MODEL),
    )
    dx, dsc, dsh, dgamma = _normmod_bwd_call("mix_bwd_normmod", [dh_a, dh_b], x, dxo, gamma, sc)

    half = MLA_ROPE // 2
    n_mla_in = D_IN - N_SWA_COLS
    dw_mla_base = dw_mla[:n_mla_in]
    dw_mla_base = dw_mla_base.at[n_mla_in - half:].add(dw_mla[n_mla_in:n_mla_in + half])
    dw_mla_base = dw_mla_base.at[n_mla_in - MLA_ROPE:n_mla_in - half].add(dw_mla[n_mla_in + half:])
    dw_in_t = jnp.concatenate([dw_swa, dw_mla_base], axis=0)
    dw_uq_t = dwq_ext[:, :MLA_QK]
    dw_uq_t = dw_uq_t.at[:, MLA_QK - half:].add(dwq_ext[:, MLA_QK:MLA_QK + half])
    dw_uq_t = dw_uq_t.at[:, MLA_NOPE:MLA_QK - half].add(dwq_ext[:, MLA_QK + half:])
    dw_o = jnp.concatenate([dwo_a, dwo_b], axis=0)
    rest = jnp.concatenate(
        [
            dw_in_t.reshape(N_CHIPS, R_IN, D_MODEL),
            dw_o.reshape(N_CHIPS, R_O, D_MODEL),
            dw_uq_t.reshape(N_CHIPS, R_UQ, D_MODEL),
            dwkv.reshape(N_CHIPS, R_UKV, D_MODEL),
            jnp.zeros((N_CHIPS, F_SHARD - O_PAD, D_MODEL), F32),
        ],
        axis=1,
    ).astype(BF16)
    return dx, rest, (dsh, dsc, dgate, dgamma), dq_norm, dkv_norm, dbias, dsink


def _device_step(x, target, mod, gu1_weights, down1_weights, rest_weights, norms, q_norm, kv_norm, sinks, rel_bias,
                 hooks2=None, hooks_mix=None, mix_done=None, hooks1=None):
    s_len = x.shape[0]
    sh1, sc1, g1, sh2, sc2, g2, sh3, sc3, g3 = [mod[:, i * D_MODEL:(i + 1) * D_MODEL] for i in range(N_MOD)]
    n1, n2, n3, nf = norms
    bkt = jnp.asarray(_bucket_map())
    bias = _bias_expand(rel_bias.T, bkt).reshape(SWA_HEADS, WINDOW, 2 * WINDOW)
    sinkb = jnp.broadcast_to(sinks.reshape(SWA_HEADS, 1, 1), (SWA_HEADS, WINDOW, 1))
    cos2, sin2 = _rope_tables(s_len)

    x1, saved1 = _ffn_fwd("ffn1", x, n1, sh1, sc1, g1, gu1_weights, 0, down1_weights, 0, sh1)
    x2, g4b, saved2 = _mix_fwd(x1, n2, sh2, sc2, g2, rest_weights, q_norm, kv_norm, bias, sinkb, cos2, sin2)
    x3, saved3 = _ffn_fwd("ffn2", x2, n3, sh3, sc3, g3, _Ready(g4b), 0, _Ready(g4b), 2, x2)

    def head(x_, t_, g_):
        xr, r = _rms(x_)
        err = xr * g_ - t_
        dy = err * (1.0 / D_MODEL)
        return _rms_bwd(dy * g_, xr, r), _sum0(err * err), _sum0(dy * xr)

    vec = _sds((1, D_MODEL), F32)
    dx3, sq, dnf = _rowwise("loss_head", head, [x3, target], [nf], [_sds((s_len, D_MODEL), F32)], [vec, vec], 256)
    loss_part = (0.5 / D_MODEL) * jnp.sum(sq)

    dx2, gb2, (dsh3, dsc3, dg3, dn3) = _ffn_bwd("ffn2", dx3, x2, n3, sc3, g3, saved3, hooks2 or _BwdHooks())
    dx1, rest, (dsh2, dsc2, dg2, dn2), dq_norm, dkv_norm, dbias, dsink = _mix_bwd(
        dx2, x1, n2, sc2, g2, q_norm, kv_norm, bias, sinkb, cos2, sin2, saved2, hooks_mix or _BwdHooks()
    )
    rest = rest[:, None]
    deps1 = mix_done(dx1, rest) if mix_done is not None else []
    dx0, gb1, (dsh1, dsc1, dg1, dn1) = _ffn_bwd(
        "ffn1", dx1, x, n1, sc1, g1, saved1, hooks1 or _BwdHooks(), deps=deps1
    )

    dmod = jnp.concatenate([dsh1, dsc1, dg1, dsh2, dsc2, dg2, dsh3, dsc3, dg3], axis=-1)
    drel = _bias_reduce(dbias.reshape(SWA_HEADS, -1), bkt).T
    dsinks = jnp.sum(dsink, axis=(1, 2)).reshape(1, SWA_HEADS)
    small = (dn1, dn2, dn3, dnf, dq_norm, dkv_norm, dsinks, drel)
    return loss_part, dx0, (gb1, gb2, rest), dmod, small


_MESH = pl.DeviceIdType.MESH


def _place():
    return lax.axis_index("x"), lax.axis_index("y"), lax.axis_index("c")


def _other_chips(x, y):
    return [(1 - x, y), (x, 1 - y), (1 - x, 1 - y)]


def _allgather_small(name, blk):
    m_per, n = blk.shape

    def body(x_ref, out_ref, send_sems, recv_sems, local_sem):
        x, y, c = _place()
        me, sibling = (x, y, c), (x, y, 1 - c)
        chips = _other_chips(x, y)

        def rows(px, py, pc):
            return out_ref.at[pl.ds((4 * px + 2 * py + pc) * m_per, m_per), :]

        def copy(k, block, to, src=None):
            return pltpu.make_async_remote_copy(
                src_ref=rows(*block) if src is None else src, dst_ref=rows(*block),
                send_sem=send_sems.at[k], recv_sem=recv_sems.at[k], device_id=to, device_id_type=_MESH,
            )

        mine = pltpu.make_async_copy(x_ref, rows(*me), local_sem)
        mine.start()
        first = [copy(0, me, sibling, src=x_ref)]
        first += [copy(1 + j, me, (*chip, c), src=x_ref) for j, chip in enumerate(chips)]
        for cp in first:
            cp.start()
        passed = [copy(4 + j, (*chip, c), sibling) for j, chip in enumerate(chips)]
        for j, chip in enumerate(chips):
            copy(1 + j, (*chip, c), me).wait_recv()
            passed[j].start()
        copy(0, sibling, me).wait_recv()
        for j, chip in enumerate(chips):
            copy(4 + j, (*chip, 1 - c), me).wait_recv()
        for cp in first + passed:
            cp.wait_send()
        mine.wait()

    return pl.pallas_call(
        body,
        name=name,
        out_shape=_sds((N_DEV * m_per, n), blk.dtype),
        in_specs=[pl.BlockSpec(memory_space=pltpu.VMEM)],
        out_specs=pl.BlockSpec(memory_space=pltpu.VMEM),
        scratch_shapes=[pltpu.SemaphoreType.DMA((7,)), pltpu.SemaphoreType.DMA((7,)), pltpu.SemaphoreType.DMA],
    )(blk)


def _allgather_weights(name, own):
    n = own.shape[0]

    def body(own_ref, g_ref, token, send_sems, recv_sems, local_sem):
        x, y, c = _place()
        sibling = (x, y, 1 - c)
        chips = _other_chips(x, y)
        mine = pltpu.make_async_copy(own_ref, g_ref.at[2 * x + y], local_sem)
        mine.start()
        for j, chip in enumerate(chips):
            for cp in _gather_sends(n, own_ref, g_ref, send_sems.at[j], recv_sems.at[j], x, y, c, chip):
                cp.start()
        for j, chip in enumerate(chips):
            _gather_all(own_ref, g_ref, send_sems.at[j], recv_sems.at[j], chip, c, c).wait_recv()
            for cp in _gather_forwards(n, g_ref, send_sems.at[3 + j], recv_sems.at[3 + j], chip, c, sibling):
                cp.start()
        for j, chip in enumerate(chips):
            _gather_all(own_ref, g_ref, send_sems.at[3 + j], recv_sems.at[3 + j], chip, 1 - c, c).wait_recv()
        for j, chip in enumerate(chips):
            _gather_all(own_ref, g_ref, send_sems.at[j], recv_sems.at[j], chip, c, c).wait_send()
            _gather_all(own_ref, g_ref, send_sems.at[3 + j], recv_sems.at[3 + j], chip, c, c).wait_send()
        mine.wait()
        token[...] = jnp.zeros_like(token)

    return pl.pallas_call(
        body,
        name=name,
        out_shape=[_sds((N_CHIPS,) + own.shape, own.dtype), _sds((8, 128), F32)],
        in_specs=[pl.BlockSpec(memory_space=pl.ANY)],
        out_specs=[pl.BlockSpec(memory_space=pl.ANY), pl.BlockSpec(memory_space=pltpu.VMEM)],
        scratch_shapes=[pltpu.SemaphoreType.DMA((6,)), pltpu.SemaphoreType.DMA((6,)), pltpu.SemaphoreType.DMA],
    )(own)


def _gather_sends(n, own_ref, g_ref, send_sem, recv_sem, x, y, c, chip):
    return [
        pltpu.make_async_remote_copy(
            src_ref=own_ref.at[p, pl.ds(c * HALF, HALF), :], dst_ref=g_ref.at[2 * x + y, p, pl.ds(c * HALF, HALF), :],
            send_sem=send_sem, recv_sem=recv_sem, device_id=(*chip, c), device_id_type=_MESH,
        )
        for p in range(n)
    ]


def _gather_forwards(n, g_ref, send_sem, recv_sem, chip, c, sibling):
    return [
        pltpu.make_async_remote_copy(
            src_ref=g_ref.at[2 * chip[0] + chip[1], p, pl.ds(c * HALF, HALF), :],
            dst_ref=g_ref.at[2 * chip[0] + chip[1], p, pl.ds(c * HALF, HALF), :],
            send_sem=send_sem, recv_sem=recv_sem, device_id=sibling, device_id_type=_MESH,
        )
        for p in range(n)
    ]


def _gather_all(own_ref, g_ref, send_sem, recv_sem, chip, half, c):
    return pltpu.make_async_remote_copy(
        src_ref=own_ref.at[:, pl.ds(c * HALF, HALF), :],
        dst_ref=g_ref.at[2 * chip[0] + chip[1], :, pl.ds(half * HALF, HALF), :],
        send_sem=send_sem, recv_sem=recv_sem, device_id=(*chip, c), device_id_type=_MESH,
    )


_HBM_SPEC = pl.BlockSpec(memory_space=pltpu.HBM)
_SEM_SPEC = pl.BlockSpec(memory_space=pltpu.SEMAPHORE)
_ANY_SPEC = pl.BlockSpec(memory_space=pl.ANY)
_VMEM_SPEC = pl.BlockSpec(memory_space=pltpu.VMEM)
_SPLIT_PARAMS = pltpu.CompilerParams(has_side_effects=pltpu.SideEffectType.DATAFLOW_SIDE_EFFECTING)
_TOKEN = jax.ShapeDtypeStruct((8, 128), F32)


def _in_hbm(a):
    return pltpu.with_memory_space_constraint(a, pltpu.HBM)


class _GatherBehind:
    def __init__(self, tag, own, after):
        self.tag, self.n = tag, own.shape[0]
        n = self.n
        x, y, _ = _place()
        g_init = lax.dynamic_update_slice(
            lax.empty((N_CHIPS,) + own.shape, own.dtype), own[None], (2 * x + y, 0, 0, 0)
        )

        def body(own_ref, g_ref, *rest):
            send_sems, recv_sems, _, _, token = rest[len(after):]
            x, y, c = _place()
            for j, chip in enumerate(_other_chips(x, y)):
                for cp in _gather_sends(n, own_ref, g_ref, send_sems.at[j], recv_sems.at[j], x, y, c, chip):
                    cp.start()
            token[...] = jnp.zeros_like(token)

        self.send1, self.recv1, self.own, self.g, self.token = pl.pallas_call(
            body,
            name=f"{tag}_start",
            out_shape=(
                pltpu.SemaphoreType.DMA((3,)), pltpu.SemaphoreType.DMA((3,)),
                pltpu.HBM(own.shape, own.dtype), pltpu.HBM(g_init.shape, g_init.dtype), _TOKEN,
            ),
            in_specs=(_HBM_SPEC, _HBM_SPEC) + (_ANY_SPEC,) * len(after),
            out_specs=(_SEM_SPEC, _SEM_SPEC, _HBM_SPEC, _HBM_SPEC, _VMEM_SPEC),
            input_output_aliases={0: 2, 1: 3},
            compiler_params=_SPLIT_PARAMS,
        )(_in_hbm(own), _in_hbm(g_init), *after)

    def mid(self, after):
        n = self.n

        def body(own_ref, g_ref, send1, recv1, after_ref, send2, recv2, g_out, token):
            x, y, c = _place()
            sibling = (x, y, 1 - c)
            chips = _other_chips(x, y)
            for j, chip in enumerate(chips):
                _gather_all(own_ref, g_ref, send1.at[j], recv1.at[j], chip, c, c).wait_recv()
                for cp in _gather_forwards(n, g_ref, send2.at[j], recv2.at[j], chip, c, sibling):
                    cp.start()
            for j, chip in enumerate(chips):
                _gather_all(own_ref, g_ref, send1.at[j], recv1.at[j], chip, c, c).wait_send()
            token[...] = jnp.zeros_like(token)

        self.send2, self.recv2, self.g, token = pl.pallas_call(
            body,
            name=f"{self.tag}_mid",
            out_shape=(
                pltpu.SemaphoreType.DMA((3,)), pltpu.SemaphoreType.DMA((3,)),
                pltpu.HBM(self.g.shape, self.g.dtype), _TOKEN,
            ),
            in_specs=(_HBM_SPEC, _HBM_SPEC, _SEM_SPEC, _SEM_SPEC, _ANY_SPEC),
            out_specs=(_SEM_SPEC, _SEM_SPEC, _HBM_SPEC, _VMEM_SPEC),
            input_output_aliases={1: 2},
            compiler_params=_SPLIT_PARAMS,
        )(self.own, self.g, self.send1, self.recv1, after)
        return [token]

    def get(self, after):
        def body(g_ref, send2, recv2, after_ref, g_out):
            x, y, c = _place()
            for j, chip in enumerate(_other_chips(x, y)):
                slot_in = g_ref.at[2 * chip[0] + chip[1], :, pl.ds((1 - c) * HALF, HALF), :]
                slot_out = g_ref.at[2 * chip[0] + chip[1], :, pl.ds(c * HALF, HALF), :]
                cp = pltpu.make_async_remote_copy(
                    src_ref=slot_out, dst_ref=slot_in, send_sem=send2.at[j], recv_sem=recv2.at[j],
                    device_id=(x, y, 1 - c), device_id_type=_MESH,
                )
                cp.wait_send()
                cp.wait_recv()

        return pl.pallas_call(
            body,
            name=f"{self.tag}_wait",
            out_shape=pltpu.HBM(self.g.shape, self.g.dtype),
            in_specs=(_HBM_SPEC, _SEM_SPEC, _SEM_SPEC, _ANY_SPEC),
            out_specs=_HBM_SPEC,
            input_output_aliases={0: 0},
            compiler_params=_SPLIT_PARAMS,
        )(self.g, self.send2, self.recv2, after)


def _pair_exchange(name, gb):
    def body(gb_ref, land_ref, send_sem, recv_sem):
        x, y, c = _place()
        theirs = gb_ref.at[:, :, pl.ds((1 - c) * HALF, HALF), :]
        cp = pltpu.make_async_remote_copy(
            src_ref=theirs, dst_ref=land_ref, send_sem=send_sem, recv_sem=recv_sem,
            device_id=(x, y, 1 - c), device_id_type=_MESH,
        )
        cp.start()
        cp.wait()

    return pl.pallas_call(
        body,
        name=name,
        out_shape=_sds((N_CHIPS, gb.shape[1], HALF, D_MODEL), gb.dtype),
        in_specs=[pl.BlockSpec(memory_space=pl.ANY)],
        out_specs=pl.BlockSpec(memory_space=pl.ANY),
        scratch_shapes=[pltpu.SemaphoreType.DMA, pltpu.SemaphoreType.DMA],
    )(gb)


def _pair_sum(name, gb, land):
    def body(gb_ref, land_ref, o_ref):
        c = lax.axis_index("c")
        mine = gb_ref[pl.ds(pl.multiple_of(c * HALF, 16), HALF), :]
        o_ref[...] = (mine.astype(F32) + land_ref[...].astype(F32)).astype(o_ref.dtype)

    return pl.pallas_call(
        body,
        name=name,
        grid=(N_CHIPS, gb.shape[1]),
        in_specs=[
            pl.BlockSpec((None, None, F_SHARD, D_MODEL), lambda j, p: (j, p, 0, 0)),
            pl.BlockSpec((None, None, HALF, D_MODEL), lambda j, p: (j, p, 0, 0)),
        ],
        out_specs=pl.BlockSpec((None, None, HALF, D_MODEL), lambda j, p: (j, p, 0, 0)),
        out_shape=_sds(land.shape, BF16),
        compiler_params=_cparams(("parallel", "parallel")),
    )(gb, land)


def _chip_exchange(name, part):
    def body(p_ref, land_ref, send_sems, recv_sems, local_sem):
        x, y, c = _place()
        me = 2 * x + y
        chips = _other_chips(x, y)
        mine = pltpu.make_async_copy(p_ref.at[me], land_ref.at[me], local_sem)
        mine.start()

        def copy(k, chip):
            return pltpu.make_async_remote_copy(
                src_ref=p_ref.at[2 * chip[0] + chip[1]], dst_ref=land_ref.at[me],
                send_sem=send_sems.at[k], recv_sem=recv_sems.at[k], device_id=(*chip, c), device_id_type=_MESH,
            )

        sends = [copy(k, chip) for k, chip in enumerate(chips)]
        for cp in sends:
            cp.start()
        for k, chip in enumerate(chips):
            pltpu.make_async_remote_copy(
                src_ref=p_ref.at[me], dst_ref=land_ref.at[2 * chip[0] + chip[1]],
                send_sem=send_sems.at[k], recv_sem=recv_sems.at[k], device_id=(*chip, c), device_id_type=_MESH,
            ).wait_recv()
        for cp in sends:
            cp.wait_send()
        mine.wait()

    return pl.pallas_call(
        body,
        name=name,
        out_shape=_sds(part.shape, part.dtype),
        in_specs=[pl.BlockSpec(memory_space=pl.ANY)],
        out_specs=pl.BlockSpec(memory_space=pl.ANY),
        scratch_shapes=[pltpu.SemaphoreType.DMA((3,)), pltpu.SemaphoreType.DMA((3,)), pltpu.SemaphoreType.DMA],
    )(part)


def _chip_sum_share(name, land):
    n = land.shape[1]

    def body(l_ref, r_ref, buf, send_sems, recv_sems, local_sems):
        p = pl.program_id(0)
        x, y, c = _place()
        acc = l_ref[0].astype(F32)
        for j in range(1, N_CHIPS):
            acc = acc + l_ref[j].astype(F32)
        buf[p] = acc

        def copies(q):
            mine = r_ref.at[q, pl.ds(c * HALF, HALF), :]
            theirs = r_ref.at[q, pl.ds((1 - c) * HALF, HALF), :]
            local = pltpu.make_async_copy(buf.at[q], mine, local_sems.at[q])
            out = pltpu.make_async_remote_copy(
                src_ref=buf.at[q], dst_ref=mine, send_sem=send_sems.at[q], recv_sem=recv_sems.at[q],
                device_id=(x, y, 1 - c), device_id_type=_MESH,
            )
            arrive = pltpu.make_async_remote_copy(
                src_ref=buf.at[q], dst_ref=theirs, send_sem=send_sems.at[q], recv_sem=recv_sems.at[q],
                device_id=(x, y, 1 - c), device_id_type=_MESH,
            )
            return local, out, arrive

        local, out, _ = copies(p)
        local.start()
        out.start()

        @pl.when(p == n - 1)
        def _():
            for q in range(n):
                local, out, arrive = copies(q)
                arrive.wait_recv()
                out.wait_send()
                local.wait()

    return pl.pallas_call(
        body,
        name=name,
        grid=(n,),
        in_specs=[pl.BlockSpec((N_CHIPS, None, HALF, D_MODEL), lambda p: (0, p, 0, 0))],
        out_specs=pl.BlockSpec(memory_space=pl.ANY),
        out_shape=_sds((n, F_SHARD, D_MODEL), F32),
        scratch_shapes=[
            pltpu.VMEM((n, HALF, D_MODEL), F32),
            pltpu.SemaphoreType.DMA((n,)), pltpu.SemaphoreType.DMA((n,)), pltpu.SemaphoreType.DMA((n,)),
        ],
        compiler_params=_cparams(("arbitrary",)),
    )(land)


class _ReduceBehind:
    def __init__(self, tag, gb, after=()):
        self.tag = tag
        n = gb.shape[1]
        land = lax.empty((N_CHIPS, n, HALF, D_MODEL), gb.dtype)

        def body(gb_ref, land_ref, *rest):
            send_sem, recv_sem, _, _, token = rest[len(after):]
            self._pair_copy(gb_ref, land_ref, send_sem, recv_sem).start()
            token[...] = jnp.zeros_like(token)

        self.send, self.recv, self.gb, self.land, self.token = pl.pallas_call(
            body,
            name=f"grads_pair_start_{tag}",
            out_shape=(
                pltpu.SemaphoreType.DMA((1,)), pltpu.SemaphoreType.DMA((1,)),
                pltpu.HBM(gb.shape, gb.dtype), pltpu.HBM(land.shape, land.dtype), _TOKEN,
            ),
            in_specs=(_HBM_SPEC, _HBM_SPEC) + (_ANY_SPEC,) * len(after),
            out_specs=(_SEM_SPEC, _SEM_SPEC, _HBM_SPEC, _HBM_SPEC, _VMEM_SPEC),
            input_output_aliases={0: 2, 1: 3},
            compiler_params=_SPLIT_PARAMS,
        )(_in_hbm(gb), _in_hbm(land), *after)

    @staticmethod
    def _pair_copy(gb_ref, land_ref, send_sem, recv_sem):
        x, y, c = _place()
        return pltpu.make_async_remote_copy(
            src_ref=gb_ref.at[:, :, pl.ds((1 - c) * HALF, HALF), :], dst_ref=land_ref,
            send_sem=send_sem.at[0], recv_sem=recv_sem.at[0], device_id=(x, y, 1 - c), device_id_type=_MESH,
        )

    @staticmethod
    def _chip_copies(p_ref, land_ref, send_sems, recv_sems):
        x, y, c = _place()
        me = 2 * x + y
        sends, arrivals = [], []
        for k, chip in enumerate(_other_chips(x, y)):
            them = 2 * chip[0] + chip[1]
            sends.append(pltpu.make_async_remote_copy(
                src_ref=p_ref.at[them], dst_ref=land_ref.at[me], send_sem=send_sems.at[k], recv_sem=recv_sems.at[k],
                device_id=(*chip, c), device_id_type=_MESH,
            ))
            arrivals.append(pltpu.make_async_remote_copy(
                src_ref=p_ref.at[me], dst_ref=land_ref.at[them], send_sem=send_sems.at[k], recv_sem=recv_sems.at[k],
                device_id=(*chip, c), device_id_type=_MESH,
            ))
        return sends, arrivals

    def mid(self, after):
        tag = self.tag

        def wait_body(gb_ref, land_ref, send_sem, recv_sem, after_ref, gb_out, land_out):
            cp = self._pair_copy(gb_ref, land_ref, send_sem, recv_sem)
            cp.wait_send()
            cp.wait_recv()

        gb, land = pl.pallas_call(
            wait_body,
            name=f"grads_pair_wait_{tag}",
            out_shape=(pltpu.HBM(self.gb.shape, self.gb.dtype), pltpu.HBM(self.land.shape, self.land.dtype)),
            in_specs=(_HBM_SPEC, _HBM_SPEC, _SEM_SPEC, _SEM_SPEC, _ANY_SPEC),
            out_specs=(_HBM_SPEC, _HBM_SPEC),
            input_output_aliases={0: 0, 1: 1},
            compiler_params=_SPLIT_PARAMS,
        )(self.gb, self.land, self.send, self.recv, after)
        part = _pair_sum(f"grads_pair_sum_{tag}", gb, land)

        x, y, _ = _place()
        start = (2 * x + y, 0, 0, 0)
        own = lax.dynamic_slice(part, start, (1,) + part.shape[1:])
        land2 = lax.dynamic_update_slice(lax.empty(part.shape, part.dtype), own, start)

        def start_body(p_ref, land_ref, send_sems, recv_sems, p_out, land_out, token):
            for cp in self._chip_copies(p_ref, land_ref, send_sems, recv_sems)[0]:
                cp.start()
            token[...] = jnp.zeros_like(token)

        self.send2, self.recv2, self.part, self.land2, token = pl.pallas_call(
            start_body,
            name=f"grads_chip_start_{tag}",
            out_shape=(
                pltpu.SemaphoreType.DMA((3,)), pltpu.SemaphoreType.DMA((3,)),
                pltpu.HBM(part.shape, part.dtype), pltpu.HBM(land2.shape, land2.dtype), _TOKEN,
            ),
            in_specs=(_HBM_SPEC, _HBM_SPEC),
            out_specs=(_SEM_SPEC, _SEM_SPEC, _HBM_SPEC, _HBM_SPEC, _VMEM_SPEC),
            input_output_aliases={0: 2, 1: 3},
            compiler_params=_SPLIT_PARAMS,
        )(_in_hbm(part), _in_hbm(land2))
        return [token]

    def get(self, after):
        n_after = len(after)

        def wait_body(p_ref, land_ref, send_sems, recv_sems, *rest):
            sends, arrivals = self._chip_copies(p_ref, land_ref, send_sems, recv_sems)
            for cp in arrivals:
                cp.wait_recv()
            for cp in sends:
                cp.wait_send()

        _, land2 = pl.pallas_call(
            wait_body,
            name=f"grads_chip_wait_{self.tag}",
            out_shape=(pltpu.HBM(self.part.shape, self.part.dtype), pltpu.HBM(self.land2.shape, self.land2.dtype)),
            in_specs=(_HBM_SPEC, _HBM_SPEC, _SEM_SPEC, _SEM_SPEC) + (_ANY_SPEC,) * n_after,
            out_specs=(_HBM_SPEC, _HBM_SPEC),
            input_output_aliases={0: 0, 1: 1},
            compiler_params=_SPLIT_PARAMS,
        )(self.part, self.land2, self.send2, self.recv2, *after)
        return _chip_sum_share(f"grads_chip_sum_share_{self.tag}", land2)


def _allreduce_small(blk):
    gathered = _allgather_small("allgather_small_grads", blk).reshape(N_DEV, PK_ROWS, 128)

    def body(g_ref, o_ref):
        acc = g_ref[0]
        for k in range(1, N_DEV):
            acc = acc + g_ref[k]
        o_ref[...] = acc

    total = pl.pallas_call(body, name="small_grads_sum", out_shape=_sds((PK_ROWS, 128), F32))(gathered)
    return gathered, total


def _adamw(name, w, g, m, v):
    rows, cols = w.shape
    tm = rows
    while tm * cols * 4 > (1 << 20) and tm % 16 == 0:
        tm //= 2

    def fn(w_, g_, m_, v_):
        m_new = ADAM_B1 * m_ + (1.0 - ADAM_B1) * g_
        v_new = ADAM_B2 * v_ + (1.0 - ADAM_B2) * (g_ * g_)
        m_hat = m_new / (1.0 - ADAM_B1 ** ADAM_STEP)
        v_hat = v_new / (1.0 - ADAM_B2 ** ADAM_STEP)
        delta = -ADAM_LR * (m_hat / (jnp.sqrt(v_hat) + ADAM_EPS) + ADAM_WD * w_)
        return delta, m_new, v_new

    out = _sds(w.shape, F32)
    return _rowwise(name, fn, [w, g, m, v], [], [out, out, out], [], tm)


def _pack_small(b_mod_like, n1, n2, n3, nf, qn, kvn, sinks, rel):
    parts = [
        b_mod_like.reshape(PK_MOD, 128),
        n1.reshape(8, 128), n2.reshape(8, 128), n3.reshape(8, 128), nf.reshape(8, 128),
        qn.reshape(2, 128), kvn.reshape(1, 128),
        jnp.pad(sinks.reshape(1, SWA_HEADS), ((0, 0), (0, 128 - SWA_HEADS))),
        rel.reshape(2, 128),
        jnp.zeros((2, 128), F32),
    ]
    return jnp.concatenate(parts, axis=0)


def _unpack_small(p):
    o = PK_MOD
    return (
        p[:o].reshape(1, N_MOD * D_MODEL),
        p[o:o + 8].reshape(1, D_MODEL), p[o + 8:o + 16].reshape(1, D_MODEL),
        p[o + 16:o + 24].reshape(1, D_MODEL), p[o + 24:o + 32].reshape(D_MODEL),
        p[o + 32:o + 34].reshape(1, MLA_Q_RANK), p[o + 34:o + 35].reshape(1, MLA_KV_RANK),
        p[o + 35:o + 36, :SWA_HEADS].reshape(1, SWA_HEADS),
        p[o + 36:o + 38].reshape(NUM_BUCKETS, SWA_HEADS),
    )


def kernel(x, c, w_mod, b_mod, norm_ffn1, ffn1_gate, ffn1_up, ffn1_down, norm_mix, w_in, q_norm, kv_norm, w_uq, w_ukv, sinks, w_o, norm_ffn2, ffn2_gate, ffn2_up, ffn2_down, rel_bias, norm_final, loss_target, m_w_mod, m_b_mod, m_norm_ffn1, m_ffn1_gate, m_ffn1_up, m_ffn1_down, m_norm_mix, m_w_in, m_q_norm, m_kv_norm, m_w_uq, m_w_ukv, m_sinks, m_w_o, m_norm_ffn2, m_ffn2_gate, m_ffn2_up, m_ffn2_down, m_rel_bias, m_norm_final, v_w_mod, v_b_mod, v_norm_ffn1, v_ffn1_gate, v_ffn1_up, v_ffn1_down, v_norm_mix, v_w_in, v_q_norm, v_kv_norm, v_w_uq, v_w_ukv, v_sinks, v_w_o, v_norm_ffn2, v_ffn2_gate, v_ffn2_up, v_ffn2_down, v_rel_bias, v_norm_final):
    ax, ay, ac = _place()
    chip = 2 * ax + ay
    dev = 2 * chip + ac
    n_mod_shard = N_MOD * D_MODEL // N_CHIPS

    def t16(w):
        return w[0].T.astype(BF16)

    rest = jnp.concatenate(
        [
            t16(w_in),
            w_o[0].astype(BF16),
            t16(w_uq).reshape(R_UQ, D_MODEL),
            t16(w_ukv).reshape(R_UKV, D_MODEL),
            jnp.zeros((F_SHARD - O_PAD, D_MODEL), BF16),
        ],
        axis=0,
    )
    own_gu1 = jnp.stack([t16(ffn1_gate), t16(ffn1_up)])
    own_down1 = ffn1_down[0].astype(BF16)[None]
    own_b = jnp.stack([t16(ffn2_gate), t16(ffn2_up), ffn2_down[0].astype(BF16), rest])

    (c_act,) = _rowwise(
        "silu_c", lambda v: v * _sigmoid(v), [c.reshape(8, 128)], [], [_sds((8, 128), F32)], [], 8
    )
    c_all = _allgather_small("allgather_c", c_act).reshape(N_DEV, D_MODEL)
    mod_cols = _mm(
        "mod_fwd", "nn", (1, n_mod_shard // 768, 1),
        c_all, (N_DEV, D_MODEL), lambda i, j, k: (0, 0),
        w_mod[0], (D_MODEL, 768), lambda i, j, k: (0, j),
        _sds((N_DEV, n_mod_shard), F32), (N_DEV, 768), lambda i, j, k: (0, j),
        (N_DEV, 768),
    )
    mod_all = _allgather_small("allgather_mod", mod_cols).reshape(N_CHIPS, 2, N_DEV, n_mod_shard)[:, 0]
    mod_all = mod_all.transpose(1, 0, 2).reshape(N_DEV, N_MOD * D_MODEL)
    mod = lax.dynamic_slice_in_dim(mod_all, dev, 1, axis=0) + b_mod

    norms = (norm_ffn1, norm_mix, norm_ffn2, norm_final.reshape(1, D_MODEL))

    gu1_weights = _GatherBehind("gather_gu1", own_gu1, after=[mod_all])
    down1_weights = _GatherBehind("gather_down1", own_down1, after=[gu1_weights.token])
    rest_weights = _GatherBehind("gather_rest", own_b, after=[down1_weights.token])

    reducing, red = {}, {}

    def ffn2_grads_done(gb2):
        reducing["ffn2"] = _ReduceBehind("ffn2", gb2)
        return [reducing["ffn2"].token]

    def mix_done(dx1, gb_rest):
        red["ffn2"] = reducing["ffn2"].get([dx1])
        reducing["rest"] = _ReduceBehind("rest", gb_rest, after=[red["ffn2"]])
        return [reducing["rest"].token]

    def ffn1_grads_done(gb1):
        red["rest"] = reducing["rest"].get([gb1])
        reducing["ffn1"] = _ReduceBehind("ffn1", gb1, after=[red["rest"]])
        return [reducing["ffn1"].token]

    loss_part, grad_x, _, dmod, small = _device_step(
        x[0], loss_target[0], mod, gu1_weights, down1_weights, rest_weights, norms, q_norm, kv_norm, sinks, rel_bias,
        hooks2=_BwdHooks(after_wgu=ffn2_grads_done),
        hooks_mix=_BwdHooks(after_gate=lambda dz: reducing["ffn2"].mid(dz)),
        mix_done=mix_done,
        hooks1=_BwdHooks(
            after_swiglu=lambda dab3: reducing["rest"].mid(dab3),
            after_wgu=ffn1_grads_done,
        ),
    )
    loss = lax.psum(loss_part, ("x", "y", "c"))

    gathered, total = _allreduce_small(_pack_small(dmod, *small))
    reducing["ffn1"].mid(total)
    (g_b_mod, g_n1, g_n2, g_n3, g_nf, g_qn, g_kvn, g_sinks, g_rel) = _unpack_small(total)
    dmod_all = gathered[:, :PK_MOD].reshape(N_DEV, N_MOD * D_MODEL)
    dmod_cols = lax.dynamic_slice_in_dim(dmod_all, chip * n_mod_shard, n_mod_shard, axis=1)
    g_w_mod = _mm(
        "mod_bwd", "tn", (1, n_mod_shard // 768, 1),
        c_all, (N_DEV, D_MODEL), lambda i, j, k: (0, 0),
        dmod_cols, (N_DEV, 768), lambda i, j, k: (0, j),
        _sds((D_MODEL, n_mod_shard), F32), (D_MODEL, 768), lambda i, j, k: (0, j),
        (D_MODEL, 768),
    )

    red2, r_rest = red["ffn2"], red["rest"][0]
    transposed = {"ffn1_gate", "ffn1_up", "ffn2_gate", "ffn2_up", "w_in", "w_uq", "w_ukv"}
    g_big = {
        "ffn2_gate": red2[0], "ffn2_up": red2[1], "ffn2_down": red2[2],
        "w_in": r_rest[O_IN:O_IN + R_IN],
        "w_o": r_rest[O_O:O_O + R_O],
        "w_uq": r_rest[O_UQ:O_UQ + R_UQ].reshape(MLA_QK, MLA_Q_RANK),
        "w_ukv": r_rest[O_UKV:O_UKV + R_UKV].reshape(MLA_NOPE + MLA_V, MLA_KV_RANK),
        "w_mod": g_w_mod,
    }
    w_big = {
        "ffn2_gate": (ffn2_gate, m_ffn2_gate, v_ffn2_gate),
        "ffn2_up": (ffn2_up, m_ffn2_up, v_ffn2_up), "ffn2_down": (ffn2_down, m_ffn2_down, v_ffn2_down),
        "w_in": (w_in, m_w_in, v_w_in), "w_o": (w_o, m_w_o, v_w_o), "w_uq": (w_uq, m_w_uq, v_w_uq),
        "w_ukv": (w_ukv, m_w_ukv, v_w_ukv), "w_mod": (w_mod, m_w_mod, v_w_mod),
    }
    grads, deltas, new_m, new_v = {}, {}, {}, {}

    def update(names):
        for nm in names:
            w, m, v = w_big[nm]
            if nm in transposed:
                outs = _adamw(f"adamw_{nm}", w[0].T, g_big[nm], m[0].T, v[0].T)
                g_, d_, m_, v_ = [a.T for a in (g_big[nm],) + tuple(outs)]
            else:
                g_, (d_, m_, v_) = g_big[nm], _adamw(f"adamw_{nm}", w[0], g_big[nm], m[0], v[0])
            grads[nm], deltas[nm], new_m[nm], new_v[nm] = g_[None], d_[None], m_[None], v_[None]

    update(list(w_big))
    red1 = reducing["ffn1"].get([deltas[nm] for nm in w_big])
    g_big.update({"ffn1_gate": red1[0], "ffn1_up": red1[1], "ffn1_down": red1[2]})
    w_big.update({
        "ffn1_gate": (ffn1_gate, m_ffn1_gate, v_ffn1_gate), "ffn1_up": (ffn1_up, m_ffn1_up, v_ffn1_up),
        "ffn1_down": (ffn1_down, m_ffn1_down, v_ffn1_down),
    })
    update(["ffn1_gate", "ffn1_up", "ffn1_down"])

    small_names = ["b_mod", "norm_ffn1", "norm_mix", "norm_ffn2", "norm_final", "q_norm", "kv_norm", "sinks", "rel_bias"]
    w_small = (b_mod, norm_ffn1, norm_mix, norm_ffn2, norm_final, q_norm, kv_norm, sinks, rel_bias)
    m_small = (m_b_mod, m_norm_ffn1, m_norm_mix, m_norm_ffn2, m_norm_final, m_q_norm, m_kv_norm, m_sinks, m_rel_bias)
    v_small = (v_b_mod, v_norm_ffn1, v_norm_mix, v_norm_ffn2, v_norm_final, v_q_norm, v_kv_norm, v_sinks, v_rel_bias)
    d_p, m_p, v_p = _adamw("adamw_small", _pack_small(*w_small), total, _pack_small(*m_small), _pack_small(*v_small))
    for nm, g_, d_, m_, v_ in zip(
        small_names,
        (g_b_mod, g_n1, g_n2, g_n3, g_nf, g_qn, g_kvn, g_sinks, g_rel),
        _unpack_small(d_p), _unpack_small(m_p), _unpack_small(v_p),
    ):
        grads[nm], deltas[nm], new_m[nm], new_v[nm] = g_, d_, m_, v_

    order = ["w_mod", "b_mod", "norm_ffn1", "ffn1_gate", "ffn1_up", "ffn1_down", "norm_mix", "w_in", "q_norm", "kv_norm",
             "w_uq", "w_ukv", "sinks", "w_o", "norm_ffn2", "ffn2_gate", "ffn2_up", "ffn2_down", "rel_bias", "norm_final"]
    return (loss, grad_x[None], *[grads[n] for n in order], *[deltas[n] for n in order],
            *[new_m[n] for n in order], *[new_v[n] for n in order])
```

```python
import functools
import math

import jax
import jax.numpy as jnp
import numpy as np
from jax import lax
from jax.experimental import pallas as pl
from jax.experimental.pallas import tpu as pltpu

F32, BF16 = jnp.float32, jnp.bfloat16

D_MODEL = 1024
D_FF = 2816
EPS = 1e-6
N_MOD = 9
SWA_HEADS, SWA_KV_HEADS, SWA_HEAD_DIM, WINDOW = 8, 2, 64, 128
SWA_GROUP = SWA_HEADS // SWA_KV_HEADS
MLA_HEADS, MLA_Q_RANK, MLA_KV_RANK, MLA_NOPE, MLA_ROPE, MLA_V = 4, 256, 128, 128, 64, 128
MLA_QK = MLA_NOPE + MLA_ROPE
ROPE_THETA = 10000.0
NUM_BUCKETS, MAX_DISTANCE = 32, 128
D_IN = 1216
N_SWA_COLS = 768
N_SWA_BLOCKS = N_SWA_COLS // SWA_HEAD_DIM
N_MLA_COLS = 512

ADAM_LR, ADAM_B1, ADAM_B2, ADAM_EPS, ADAM_WD, ADAM_STEP = 0.001, 0.9, 0.999, 1e-08, 0.01, 10

N_CHIPS = 4
N_DEV = 8
F_SHARD = D_FF // N_CHIPS
HALF = F_SHARD // 2
R_IN, R_O, R_UQ, R_UKV = 304, 256, 48, 32
O_IN, O_O, O_UQ, O_UKV, O_PAD = 0, 304, 560, 608, 640

PK_MOD = 72
PK_ROWS = 112
VMEM_LIMIT = 56 << 20
ROW_TILE = 1024

_DN = {
    "nn": (((1,), (0,)), ((), ())),
    "nt": (((1,), (1,)), ((), ())),
    "tn": (((0,), (0,)), ((), ())),
}


def _sds(shape, dtype):
    return jax.ShapeDtypeStruct(tuple(shape), dtype)


def _cparams(sem=None):
    kw = {"vmem_limit_bytes": VMEM_LIMIT}
    if sem is not None:
        kw["dimension_semantics"] = sem
    return pltpu.CompilerParams(**kw)


def _dot(a, b, dims):
    return lax.dot_general(a.astype(BF16), b.astype(BF16), _DN[dims], preferred_element_type=F32)


def _mm(name, dims, grid, a, a_blk, a_idx, b, b_blk, b_idx, out, out_blk, out_idx, acc_shape, alias=None, deps=()):
    nk = grid[2]

    def body(*refs):
        a_ref, b_ref = refs[:2]
        o_ref, acc_ref = refs[-2:]
        part = _dot(a_ref[...], b_ref[...], dims)
        if nk == 1:
            o_ref[...] = part.astype(o_ref.dtype)
            return
        k = pl.program_id(2)

        @pl.when(k == 0)
        def _():
            acc_ref[...] = part

        @pl.when((k > 0) & (k < nk - 1))
        def _():
            acc_ref[...] += part

        @pl.when(k == nk - 1)
        def _():
            o_ref[...] = (acc_ref[...] + part).astype(o_ref.dtype)

    in_specs = [pl.BlockSpec(a_blk, a_idx), pl.BlockSpec(b_blk, b_idx)]
    args = [a, b]
    kw = {}
    if alias is not None:
        in_specs.append(pl.BlockSpec(memory_space=pl.ANY))
        args.append(alias)
        kw["input_output_aliases"] = {2: 0}
    in_specs += [pl.BlockSpec(memory_space=pl.ANY)] * len(deps)
    args += list(deps)
    return pl.pallas_call(
        body,
        name=name,
        grid=grid,
        in_specs=in_specs,
        out_specs=pl.BlockSpec(out_blk, out_idx),
        out_shape=out,
        scratch_shapes=[pltpu.VMEM(acc_shape if nk > 1 else (8, 128), F32)],
        compiler_params=_cparams(("parallel", "parallel", "arbitrary")),
        **kw,
    )(*args)


def _rowwise(name, fn, tiled, full, out_tiled, out_red, tm, deps=()):
    rows = tiled[0].shape[-2]
    grid = (rows // tm,)
    n_in = len(tiled) + len(full)
    n_t = len(out_tiled)
    n_dep = len(deps)

    def tspec(shape):
        nd = len(shape)
        return pl.BlockSpec(tuple(shape[:-2]) + (tm, shape[-1]), lambda i, nd=nd: (0,) * (nd - 2) + (i, 0))

    def fspec(shape):
        nd = len(shape)
        return pl.BlockSpec(tuple(shape), lambda i, nd=nd: (0,) * nd)

    def body(*refs):
        outs = fn(*[r[...] for r in refs[:n_in]])
        if not isinstance(outs, (tuple, list)):
            outs = (outs,)
        out_refs = refs[n_in + n_dep:]
        for r, v in zip(out_refs[:n_t], outs[:n_t]):
            r[...] = v.astype(r.dtype)
        red_refs = out_refs[n_t:]
        if red_refs:
            @pl.when(pl.program_id(0) == 0)
            def _():
                for r in red_refs:
                    r[...] = jnp.zeros_like(r)

            for r, v in zip(red_refs, outs[n_t:]):
                r[...] += v.astype(r.dtype)

    res = pl.pallas_call(
        body,
        name=name,
        grid=grid,
        in_specs=[tspec(a.shape) for a in tiled] + [fspec(a.shape) for a in full]
        + [pl.BlockSpec(memory_space=pl.ANY)] * n_dep,
        out_specs=[tspec(o.shape) for o in out_tiled] + [fspec(o.shape) for o in out_red],
        out_shape=list(out_tiled) + list(out_red),
        compiler_params=_cparams(("arbitrary",) if out_red else ("parallel",)),
    )(*tiled, *full, *deps)
    return res


def _sum0(v):
    return jnp.sum(v, axis=0, keepdims=True)


def _sigmoid(v):
    return 1.0 / (1.0 + jnp.exp(-v))


def _rms(x):
    r = lax.rsqrt(jnp.mean(x * x, axis=-1, keepdims=True) + EPS)
    return x * r, r


def _rms_bwd(u, xr, r):
    return r * (u - xr * jnp.mean(u * xr, axis=-1, keepdims=True))


class _Ready:
    def __init__(self, g):
        self.g = g

    def mid(self, after):
        return []

    def get(self, after):
        return self.g


def _ffn_fwd(tag, x, gamma, sh, sc, gate, gu_src, base, down_src, down_idx, mid_after):
    s_len = x.shape[0]
    deps = gu_src.mid(mid_after)

    def normmod(x_, gamma_, sc_, sh_):
        xr, _ = _rms(x_)
        return xr * gamma_ * (1.0 + sc_) + sh_

    (h,) = _rowwise(f"{tag}_normmod", normmod, [x], [gamma, sc, sh], [_sds((s_len, D_MODEL), BF16)], [], 256, deps=deps)
    g4 = gu_src.get(h)

    tm = min(ROW_TILE, s_len)
    ab3 = _mm(
        f"{tag}_gate_up", "nt", (s_len // tm, 2 * N_CHIPS, 1),
        h, (tm, D_MODEL), lambda i, j, k: (i, 0),
        g4, (None, None, F_SHARD, D_MODEL), lambda i, j, k: (j % N_CHIPS, base + j // N_CHIPS, 0, 0),
        _sds((2 * N_CHIPS, s_len, F_SHARD), BF16), (None, tm, F_SHARD), lambda i, j, k: (j, i, 0),
        (tm, F_SHARD),
    )

    def swiglu(ab):
        a = ab[:N_CHIPS].astype(F32)
        b = ab[N_CHIPS:].astype(F32)
        return a * _sigmoid(a) * b

    (s3,) = _rowwise(
        f"{tag}_swiglu", swiglu, [ab3], [], [_sds((N_CHIPS, s_len, F_SHARD), BF16)], [], 128,
        deps=down_src.mid(ab3) if down_src is not None else (),
    )
    g_down = down_src.get(s3) if down_src is not None else g4

    y = _mm(
        f"{tag}_down", "nn", (s_len // tm, 1, N_CHIPS),
        s3, (None, tm, F_SHARD), lambda i, j, k: (k, i, 0),
        g_down, (None, None, F_SHARD, D_MODEL), lambda i, j, k: (k, down_idx, 0, 0),
        _sds((s_len, D_MODEL), F32), (tm, D_MODEL), lambda i, j, k: (i, 0),
        (tm, D_MODEL),
    )

    (x_out,) = _rowwise(
        f"{tag}_residual", lambda x_, y_, g_: x_ + 0.5 * g_ * y_, [x, y], [gate], [_sds((s_len, D_MODEL), F32)], [], 256
    )
    return x_out, (g4, base, g_down, down_idx, h, ab3, s3, y)


def _normmod_bwd_call(name, dh_parts, x, dxo, gamma, sc, deps=()):
    s_len = x.shape[0]
    n_parts = len(dh_parts)

    def fn(*vals):
        dh = vals[0]
        for extra in vals[1:n_parts]:
            dh = dh + extra
        x_, dxo_, gamma_, sc_ = vals[n_parts:]
        xr, r = _rms(x_)
        n = xr * gamma_
        dn = dh * (1.0 + sc_)
        dx = dxo_ + _rms_bwd(dn * gamma_, xr, r)
        return dx, _sum0(dh * n), _sum0(dh), _sum0(dn * xr)

    vec = _sds((1, D_MODEL), F32)
    return _rowwise(
        name, fn, list(dh_parts) + [x, dxo], [gamma, sc], [_sds((s_len, D_MODEL), F32)], [vec, vec, vec], 256, deps=deps
    )


class _BwdHooks:
    def __init__(self, after_gate=None, after_swiglu=None, after_wdown=None, after_wgu=None, after_dh=None):
        def nothing(_):
            return []

        self.after_gate = after_gate or nothing
        self.after_swiglu = after_swiglu or nothing
        self.after_wdown = after_wdown or nothing
        self.after_wgu = after_wgu or nothing
        self.after_dh = after_dh or nothing


def _ffn_bwd(tag, dxo, x, gamma, sc, gate, saved, hooks, deps=()):
    g4, base, g_down, down_idx, h, ab3, s3, y = saved
    s_len = x.shape[0]
    vec = _sds((1, D_MODEL), F32)

    dy, dgate = _rowwise(
        f"{tag}_bwd_gate", lambda dxo_, y_, g_: (0.5 * g_ * dxo_, _sum0(0.5 * y_ * dxo_)),
        [dxo, y], [gate], [_sds((s_len, D_MODEL), BF16)], [vec], 256, deps=deps,
    )

    tm = min(ROW_TILE, s_len)
    ds3 = _mm(
        f"{tag}_bwd_ds", "nt", (s_len // tm, N_CHIPS, 1),
        dy, (tm, D_MODEL), lambda i, j, k: (i, 0),
        g_down, (None, None, F_SHARD, D_MODEL), lambda i, j, k: (j, down_idx, 0, 0),
        _sds((N_CHIPS, s_len, F_SHARD), BF16), (None, tm, F_SHARD), lambda i, j, k: (j, i, 0),
        (tm, F_SHARD),
    )

    def swiglu_bwd(ab, ds):
        a = ab[:N_CHIPS].astype(F32)
        b = ab[N_CHIPS:].astype(F32)
        ds = ds.astype(F32)
        sig = _sigmoid(a)
        da = ds * b * sig * (1.0 + a * (1.0 - sig))
        db = ds * a * sig
        return jnp.concatenate([da, db], axis=0)

    (dab3,) = _rowwise(
        f"{tag}_bwd_swiglu", swiglu_bwd, [ab3, ds3], [], [_sds((2 * N_CHIPS, s_len, F_SHARD), BF16)], [], 128
    )

    tk = tm
    gb_down = _mm(
        f"{tag}_bwd_wdown", "tn", (N_CHIPS, 1, s_len // tk),
        s3, (None, tk, F_SHARD), lambda i, j, k: (i, k, 0),
        dy, (tk, D_MODEL), lambda i, j, k: (k, 0),
        _sds((N_CHIPS, 1, F_SHARD, D_MODEL), BF16), (None, None, F_SHARD, D_MODEL), lambda i, j, k: (i, 0, 0, 0),
        (F_SHARD, D_MODEL), deps=hooks.after_swiglu(dab3),
    )
    gb_gu = _mm(
        f"{tag}_bwd_wgu", "tn", (2 * N_CHIPS, 1, s_len // tk),
        dab3, (None, tk, F_SHARD), lambda i, j, k: (i, k, 0),
        h, (tk, D_MODEL), lambda i, j, k: (k, 0),
        _sds((N_CHIPS, 2, F_SHARD, D_MODEL), BF16), (None, None, F_SHARD, D_MODEL),
        lambda i, j, k: (i % N_CHIPS, i // N_CHIPS, 0, 0),
        (F_SHARD, D_MODEL), deps=hooks.after_wdown(gb_down),
    )
    dh = _mm(
        f"{tag}_bwd_dh", "nn", (s_len // tm, 1, 2 * N_CHIPS),
        dab3, (None, tm, F_SHARD), lambda i, j, k: (k, i, 0),
        g4, (None, None, F_SHARD, D_MODEL), lambda i, j, k: (k % N_CHIPS, base + k // N_CHIPS, 0, 0),
        _sds((s_len, D_MODEL), F32), (tm, D_MODEL), lambda i, j, k: (i, 0),
        (tm, D_MODEL), deps=hooks.after_wgu(gb_gu),
    )
    dx, dsc, dsh, dgamma = _normmod_bwd_call(
        f"{tag}_bwd_normmod", [dh], x, dxo, gamma, sc, deps=hooks.after_dh(dh)
    )
    return dx, (gb_down, gb_gu), (dsh, dsc, dgate, dgamma)


def _bucket_map():
    qi = np.arange(WINDOW)[:, None]
    kj = np.arange(2 * WINDOW)[None, :]
    dist = qi + WINDOW - kj
    band = (dist >= 0) & (dist < WINDOW)
    max_exact = NUM_BUCKETS // 2
    n = np.maximum(dist, 0)
    large = []
    for dt in (np.float32, np.float64):
        nf = np.maximum(n, 1).astype(dt)
        val = np.log(nf / dt(max_exact)) / dt(math.log(MAX_DISTANCE / max_exact)) * dt(NUM_BUCKETS - max_exact)
        large.append(np.minimum(max_exact + val.astype(np.int32), NUM_BUCKETS - 1))
    assert np.array_equal(large[0], large[1])
    bucket = np.where(n < max_exact, n, large[0])
    return np.where(band, bucket, -1).astype(np.int32).reshape(1, -1)


def _bias_expand(rel_bias_t, bkt):
    n = bkt.shape[1]

    def body(rb_ref, bkt_ref, o_ref):
        onehot = (lax.broadcasted_iota(jnp.int32, (NUM_BUCKETS, n), 0) == bkt_ref[...]).astype(F32)
        o_ref[...] = lax.dot_general(
            rb_ref[...], onehot, _DN["nn"], precision=lax.Precision.HIGHEST, preferred_element_type=F32
        )

    return pl.pallas_call(
        body, name="bias_expand", out_shape=_sds((SWA_HEADS, n), F32), compiler_params=_cparams()
    )(rel_bias_t, bkt)


def _bias_reduce(dbias_flat, bkt):
    n = bkt.shape[1]

    def body(db_ref, bkt_ref, o_ref):
        onehot = (lax.broadcasted_iota(jnp.int32, (NUM_BUCKETS, n), 0) == bkt_ref[...]).astype(F32)
        o_ref[...] = lax.dot_general(
            db_ref[...], onehot, _DN["nt"], precision=lax.Precision.HIGHEST, preferred_element_type=F32
        )

    return pl.pallas_call(
        body, name="bias_reduce", out_shape=_sds((SWA_HEADS, NUM_BUCKETS), F32), compiler_params=_cparams()
    )(dbias_flat, bkt)


_SWA_SCALE = SWA_HEAD_DIM ** -0.5
_NEG = -1e30


def _swa_in_specs():
    g, w, hd = SWA_GROUP, WINDOW, SWA_HEAD_DIM
    prev = lambda n: jnp.maximum(n - 1, 0)
    return [
        pl.BlockSpec((g, w, hd), lambda j, n: (j, n, 0)),
        pl.BlockSpec((1, w, hd), lambda j, n: (SWA_HEADS + j, prev(n), 0)),
        pl.BlockSpec((1, w, hd), lambda j, n: (SWA_HEADS + j, n, 0)),
        pl.BlockSpec((1, w, hd), lambda j, n: (SWA_HEADS + SWA_KV_HEADS + j, prev(n), 0)),
        pl.BlockSpec((1, w, hd), lambda j, n: (SWA_HEADS + SWA_KV_HEADS + j, n, 0)),
        pl.BlockSpec((g, w, 2 * w), lambda j, n: (j, 0, 0)),
        pl.BlockSpec((g, w, 1), lambda j, n: (j, 0, 0)),
    ]


def _swa_scores(q_ref, kp_ref, kc_ref, bias_ref, n):
    g, w = SWA_GROUP, WINDOW
    q = q_ref[...].reshape(g * w, SWA_HEAD_DIM)
    kk = jnp.concatenate([kp_ref[0], kc_ref[0]], axis=0)
    s = (_dot(q, kk, "nt") * _SWA_SCALE).reshape(g, w, 2 * w) + bias_ref[...]
    qi = lax.broadcasted_iota(jnp.int32, (w, 2 * w), 0)
    kj = lax.broadcasted_iota(jnp.int32, (w, 2 * w), 1)
    dist = qi + w - kj
    valid = (dist >= 0) & (dist < w) & ((n > 0) | (kj >= w))
    return q, kk, jnp.where(valid[None], s, _NEG), valid[None]


def _swa_fwd(swa3, bias, sinkb):
    s_len = swa3.shape[1]
    g, w, hd = SWA_GROUP, WINDOW, SWA_HEAD_DIM

    def body(q_ref, kp_ref, kc_ref, vp_ref, vc_ref, bias_ref, sink_ref, o_ref, lse_ref):
        n = pl.program_id(1)
        _, _, s, valid = _swa_scores(q_ref, kp_ref, kc_ref, bias_ref, n)
        vv = jnp.concatenate([vp_ref[0], vc_ref[0]], axis=0)
        sink = sink_ref[...]
        m = jnp.maximum(jnp.max(s, axis=-1, keepdims=True), sink)
        p = jnp.where(valid, jnp.exp(s - m), 0.0)
        l = jnp.sum(p, axis=-1, keepdims=True) + jnp.exp(sink - m)
        o = _dot(p.reshape(g * w, 2 * w), vv, "nn").reshape(g, w, hd)
        o_ref[...] = (o / l).astype(o_ref.dtype)
        lse_ref[...] = m + jnp.log(l)

    return pl.pallas_call(
        body,
        name="swa_fwd",
        grid=(SWA_KV_HEADS, s_len // w),
        in_specs=_swa_in_specs(),
        out_specs=[
            pl.BlockSpec((g, w, hd), lambda j, n: (j, n, 0)),
            pl.BlockSpec((g, w, 1), lambda j, n: (j, n, 0)),
        ],
        out_shape=[_sds((SWA_HEADS, s_len, hd), BF16), _sds((SWA_HEADS, s_len, 1), F32)],
        compiler_params=_cparams(("parallel", "parallel")),
    )(swa3, swa3, swa3, swa3, swa3, bias, sinkb)


def _swa_bwd(swa3, bias, sinkb, o3, do3, lse):
    s_len = swa3.shape[1]
    g, w, hd = SWA_GROUP, WINDOW, SWA_HEAD_DIM

    def body(q_ref, kp_ref, kc_ref, vp_ref, vc_ref, bias_ref, sink_ref, o_ref, do_ref, lse_ref,
             dq_ref, dka_ref, dkb_ref, dva_ref, dvb_ref, dbias_ref, dsink_ref):
        n = pl.program_id(1)

        @pl.when(n == 0)
        def _():
            dbias_ref[...] = jnp.zeros_like(dbias_ref)
            dsink_ref[...] = jnp.zeros_like(dsink_ref)

        q, kk, s, valid = _swa_scores(q_ref, kp_ref, kc_ref, bias_ref, n)
        vv = jnp.concatenate([vp_ref[0], vc_ref[0]], axis=0)
        lse_v = lse_ref[...]
        p = jnp.where(valid, jnp.exp(s - lse_v), 0.0)
        do = do_ref[...].astype(F32)
        delta = jnp.sum(do * o_ref[...].astype(F32), axis=-1, keepdims=True)
        do2 = do.reshape(g * w, hd)
        dp = _dot(do2, vv, "nt").reshape(g, w, 2 * w)
        ds = p * (dp - delta)
        dbias_ref[...] += ds
        dsink_ref[...] -= jnp.exp(sink_ref[...] - lse_v) * delta
        ds2 = ds.reshape(g * w, 2 * w)
        dq_ref[...] = (_dot(ds2, kk, "nn") * _SWA_SCALE).reshape(g, w, hd).astype(dq_ref.dtype)
        dkk = _dot(ds2, q, "tn") * _SWA_SCALE
        dvv = _dot(p.reshape(g * w, 2 * w), do2, "tn")
        dkb_ref[0] = dkk[:w]
        dka_ref[0] = dkk[w:]
        dvb_ref[0] = dvv[:w]
        dva_ref[0] = dvv[w:]

    kv_spec = pl.BlockSpec((1, w, hd), lambda j, n: (j, n, 0))
    kv_sds = _sds((SWA_KV_HEADS, s_len, hd), F32)
    return pl.pallas_call(
        body,
        name="swa_bwd",
        grid=(SWA_KV_HEADS, s_len // w),
        in_specs=_swa_in_specs() + [
            pl.BlockSpec((g, w, hd), lambda j, n: (j, n, 0)),
            pl.BlockSpec((g, w, hd), lambda j, n: (j, n, 0)),
            pl.BlockSpec((g, w, 1), lambda j, n: (j, n, 0)),
        ],
        out_specs=[
            pl.BlockSpec((g, w, hd), lambda j, n: (j, n, 0)),
            kv_spec, kv_spec, kv_spec, kv_spec,
            pl.BlockSpec((g, w, 2 * w), lambda j, n: (j, 0, 0)),
            pl.BlockSpec((g, w, 1), lambda j, n: (j, 0, 0)),
        ],
        out_shape=[
            _sds((SWA_HEADS, s_len, hd), BF16), kv_sds, kv_sds, kv_sds, kv_sds,
            _sds((SWA_HEADS, w, 2 * w), F32), _sds((SWA_HEADS, w, 1), F32),
        ],
        compiler_params=_cparams(("parallel", "arbitrary")),
    )(swa3, swa3, swa3, swa3, swa3, bias, sinkb, o3, do3, lse)


_MLA_SCALE = MLA_QK ** -0.5
_MLA_TQ = 256


def _mla_mask(i, tq, s_len):
    row = i * tq + lax.broadcasted_iota(jnp.int32, (tq, s_len), 0)
    col = lax.broadcasted_iota(jnp.int32, (tq, s_len), 1)
    return col <= row


def _mla_fwd(q4, k4, v4):
    s_len = q4.shape[1]
    tq = min(_MLA_TQ, s_len)

    def body(q_ref, k_ref, v_ref, o_ref, lse_ref):
        mask = _mla_mask(pl.program_id(1), tq, s_len)
        s = jnp.where(mask, _dot(q_ref[...], k_ref[...], "nt") * _MLA_SCALE, _NEG)
        m = jnp.max(s, axis=-1, keepdims=True)
        p = jnp.exp(s - m)
        l = jnp.sum(p, axis=-1, keepdims=True)
        o_ref[...] = (_dot(p, v_ref[...], "nn") / l).astype(o_ref.dtype)
        lse_ref[...] = m + jnp.log(l)

    return pl.pallas_call(
        body,
        name="mla_fwd",
        grid=(MLA_HEADS, s_len // tq),
        in_specs=[
            pl.BlockSpec((None, tq, MLA_QK), lambda h, i: (h, i, 0)),
            pl.BlockSpec((None, s_len, MLA_QK), lambda h, i: (h, 0, 0)),
            pl.BlockSpec((None, s_len, MLA_V), lambda h, i: (h, 0, 0)),
        ],
        out_specs=[
            pl.BlockSpec((None, tq, MLA_V), lambda h, i: (h, i, 0)),
            pl.BlockSpec((None, tq, 1), lambda h, i: (h, i, 0)),
        ],
        out_shape=[_sds((MLA_HEADS, s_len, MLA_V), BF16), _sds((MLA_HEADS, s_len, 1), F32)],
        compiler_params=_cparams(("parallel", "parallel")),
    )(q4, k4, v4)


def _mla_bwd(q4, k4, v4, o4, do4, lse):
    s_len = q4.shape[1]
    tq = min(_MLA_TQ, s_len)

    def body(q_ref, k_ref, v_ref, o_ref, do_ref, lse_ref, dq_ref, dk_ref, dv_ref):
        i = pl.program_id(1)

        @pl.when(i == 0)
        def _():
            dk_ref[...] = jnp.zeros_like(dk_ref)
            dv_ref[...] = jnp.zeros_like(dv_ref)

        mask = _mla_mask(i, tq, s_len)
        q, k, v = q_ref[...], k_ref[...], v_ref[...]
        s = jnp.where(mask, _dot(q, k, "nt") * _MLA_SCALE, _NEG)
        p = jnp.where(mask, jnp.exp(s - lse_ref[...]), 0.0)
        do = do_ref[...]
        delta = jnp.sum(do.astype(F32) * o_ref[...].astype(F32), axis=-1, keepdims=True)
        dp = _dot(do, v, "nt")
        ds = (p * (dp - delta) * _MLA_SCALE).astype(BF16)
        dq_ref[...] = _dot(ds, k, "nn")
        dk_ref[...] += _dot(ds, q, "tn")
        dv_ref[...] += _dot(p, do, "tn")

    return pl.pallas_call(
        body,
        name="mla_bwd",
        grid=(MLA_HEADS, s_len // tq),
        in_specs=[
            pl.BlockSpec((None, tq, MLA_QK), lambda h, i: (h, i, 0)),
            pl.BlockSpec((None, s_len, MLA_QK), lambda h, i: (h, 0, 0)),
            pl.BlockSpec((None, s_len, MLA_V), lambda h, i: (h, 0, 0)),
            pl.BlockSpec((None, tq, MLA_V), lambda h, i: (h, i, 0)),
            pl.BlockSpec((None, tq, MLA_V), lambda h, i: (h, i, 0)),
            pl.BlockSpec((None, tq, 1), lambda h, i: (h, i, 0)),
        ],
        out_specs=[
            pl.BlockSpec((None, tq, MLA_QK), lambda h, i: (h, i, 0)),
            pl.BlockSpec((None, s_len, MLA_QK), lambda h, i: (h, 0, 0)),
            pl.BlockSpec((None, s_len, MLA_V), lambda h, i: (h, 0, 0)),
        ],
        out_shape=[
            _sds((MLA_HEADS, s_len, MLA_QK), F32),
            _sds((MLA_HEADS, s_len, MLA_QK), F32),
            _sds((MLA_HEADS, s_len, MLA_V), F32),
        ],
        compiler_params=_cparams(("parallel", "arbitrary")),
    )(q4, k4, v4, o4, do4, lse)


def _mla_split(pm):
    q_lat = pm[:, :MLA_Q_RANK]
    kv_lat = pm[:, MLA_Q_RANK:MLA_Q_RANK + MLA_KV_RANK]
    kr = pm[:, MLA_Q_RANK + MLA_KV_RANK:MLA_Q_RANK + MLA_KV_RANK + MLA_ROPE]
    kr_sw = pm[:, MLA_Q_RANK + MLA_KV_RANK + MLA_ROPE:]
    return q_lat, kv_lat, kr, kr_sw


def _mla_proj(pm, cos2, sin2, q_norm, kv_norm, wq_ext, wkv):
    s_len = pm.shape[0]

    def fn(pm_, cos_, sin_, qg, kvg, wq, wk):
        q_lat, kv_lat, kr, kr_sw = _mla_split(pm_)
        nq = (_rms(q_lat)[0] * qg).astype(BF16)
        nkv = (_rms(kv_lat)[0] * kvg).astype(BF16)
        k_rot = kr * cos_ + kr_sw * sin_
        qs, ks, vs = [], [], []
        for h in range(MLA_HEADS):
            qe = _dot(nq, wq[h], "nt")
            q_rot = qe[:, MLA_NOPE:MLA_QK] * cos_ + qe[:, MLA_QK:] * sin_
            qs.append(jnp.concatenate([qe[:, :MLA_NOPE], q_rot], axis=-1))
            kve = _dot(nkv, wk[h], "nt")
            ks.append(jnp.concatenate([kve[:, :MLA_NOPE], k_rot], axis=-1))
            vs.append(kve[:, MLA_NOPE:])
        return jnp.stack(qs), jnp.stack(ks), jnp.stack(vs)

    return _rowwise(
        "mla_proj", fn, [pm, cos2, sin2], [q_norm, kv_norm, wq_ext, wkv],
        [_sds((MLA_HEADS, s_len, MLA_QK), BF16), _sds((MLA_HEADS, s_len, MLA_QK), BF16),
         _sds((MLA_HEADS, s_len, MLA_V), BF16)], [], 256,
    )


def _mla_proj_bwd(pm, cos2, sin2, dq4, dk4, dv4, q_norm, kv_norm, wq_ext, wkv):
    s_len = pm.shape[0]

    def fn(pm_, cos_, sin_, dq, dk, dv, qg, kvg, wq, wk):
        q_lat, kv_lat, _, _ = _mla_split(pm_)
        xq, rq = _rms(q_lat)
        xkv, rkv = _rms(kv_lat)
        nq = (xq * qg).astype(BF16)
        nkv = (xkv * kvg).astype(BF16)
        dnq = jnp.zeros_like(q_lat)
        dnkv = jnp.zeros_like(kv_lat)
        dkr = jnp.zeros_like(cos_)
        dwq, dwk = [], []
        for h in range(MLA_HEADS):
            dy = dq[h][:, MLA_NOPE:]
            dqe = jnp.concatenate([dq[h][:, :MLA_NOPE], dy * cos_, dy * sin_], axis=-1).astype(BF16)
            dnq = dnq + _dot(dqe, wq[h], "nn")
            dwq.append(_dot(dqe, nq, "tn"))
            dkve = jnp.concatenate([dk[h][:, :MLA_NOPE], dv[h]], axis=-1).astype(BF16)
            dnkv = dnkv + _dot(dkve, wk[h], "nn")
            dwk.append(_dot(dkve, nkv, "tn"))
            dkr = dkr + dk[h][:, MLA_NOPE:]
        dq_lat = _rms_bwd(dnq * qg, xq, rq)
        dkv_lat = _rms_bwd(dnkv * kvg, xkv, rkv)
        dpm = jnp.concatenate([dq_lat, dkv_lat, dkr * cos_, dkr * sin_], axis=-1)
        return dpm, _sum0(dnq * xq), _sum0(dnkv * xkv), jnp.stack(dwq), jnp.stack(dwk)

    return _rowwise(
        "mla_proj_bwd", fn, [pm, cos2, sin2, dq4, dk4, dv4], [q_norm, kv_norm, wq_ext, wkv],
        [_sds((s_len, N_MLA_COLS), BF16)],
        [_sds((1, MLA_Q_RANK), F32), _sds((1, MLA_KV_RANK), F32),
         _sds(wq_ext.shape, F32), _sds(wkv.shape, F32)], 256,
    )


def _rope_tables(s_len):
    inv = ROPE_THETA ** (-jnp.arange(0, MLA_ROPE, 2, dtype=F32) / MLA_ROPE)
    ang = jnp.arange(s_len, dtype=F32)[:, None] * inv[None, :]
    cos, sin = jnp.cos(ang), jnp.sin(ang)
    return jnp.concatenate([cos, cos], axis=-1), jnp.concatenate([-sin, sin], axis=-1)


def _mix_weights(g_mix):
    rest = g_mix[:, 0]
    w_in_t = rest[:, O_IN:O_IN + R_IN].reshape(D_IN, D_MODEL)
    w_o = rest[:, O_O:O_O + R_O].reshape(D_MODEL, D_MODEL)
    w_uq_t = rest[:, O_UQ:O_UQ + R_UQ].reshape(MLA_HEADS, MLA_QK, MLA_Q_RANK)
    w_ukv_t = rest[:, O_UKV:O_UKV + R_UKV].reshape(MLA_HEADS, MLA_NOPE + MLA_V, MLA_KV_RANK)
    half = MLA_ROPE // 2
    w_swa = w_in_t[:N_SWA_COLS]
    w_mla = jnp.concatenate([w_in_t[N_SWA_COLS:], w_in_t[D_IN - half:], w_in_t[D_IN - MLA_ROPE:D_IN - half]], axis=0)
    wq_ext = jnp.concatenate([w_uq_t, w_uq_t[:, MLA_QK - half:], w_uq_t[:, MLA_NOPE:MLA_QK - half]], axis=1)
    return w_swa, w_mla, w_o, wq_ext, w_ukv_t


def _mix_fwd(x, gamma, sh, sc, gate, mix_src, q_norm, kv_norm, bias, sinkb, cos2, sin2):
    s_len = x.shape[0]
    tm = min(ROW_TILE, s_len)

    def normmod(x_, gamma_, sc_, sh_):
        return _rms(x_)[0] * gamma_ * (1.0 + sc_) + sh_

    deps = mix_src.mid(x)
    (h,) = _rowwise("mix_normmod", normmod, [x], [gamma, sc, sh], [_sds((s_len, D_MODEL), BF16)], [], 256, deps=deps)
    weights = _mix_weights(mix_src.get(h))
    w_swa, w_mla, w_o, wq_ext, wkv = weights
    swa3 = _mm(
        "mix_proj_swa", "nt", (s_len // tm, N_SWA_BLOCKS, 1),
        h, (tm, D_MODEL), lambda i, j, k: (i, 0),
        w_swa, (SWA_HEAD_DIM, D_MODEL), lambda i, j, k: (j, 0),
        _sds((N_SWA_BLOCKS, s_len, SWA_HEAD_DIM), BF16), (None, tm, SWA_HEAD_DIM), lambda i, j, k: (j, i, 0),
        (tm, SWA_HEAD_DIM),
    )
    pm = _mm(
        "mix_proj_mla", "nt", (s_len // tm, 1, 1),
        h, (tm, D_MODEL), lambda i, j, k: (i, 0),
        w_mla, (N_MLA_COLS, D_MODEL), lambda i, j, k: (0, 0),
        _sds((s_len, N_MLA_COLS), F32), (tm, N_MLA_COLS), lambda i, j, k: (i, 0),
        (tm, N_MLA_COLS),
    )
    q4, k4, v4 = _mla_proj(pm, cos2, sin2, q_norm, kv_norm, wq_ext, wkv)
    oa3, lse_a = _swa_fwd(swa3, bias, sinkb)
    ob4, lse_b = _mla_fwd(q4, k4, v4)
    n_a = SWA_HEADS * SWA_HEAD_DIM
    za = _mm(
        "mix_out_swa", "nn", (s_len // tm, 1, SWA_HEADS),
        oa3, (None, tm, SWA_HEAD_DIM), lambda i, j, k: (k, i, 0),
        w_o, (SWA_HEAD_DIM, D_MODEL), lambda i, j, k: (k, 0),
        _sds((s_len, D_MODEL), F32), (tm, D_MODEL), lambda i, j, k: (i, 0),
        (tm, D_MODEL),
    )
    zb = _mm(
        "mix_out_mla", "nn", (s_len // tm, 1, MLA_HEADS),
        ob4, (None, tm, MLA_V), lambda i, j, k: (k, i, 0),
        w_o, (MLA_V, D_MODEL), lambda i, j, k: (n_a // MLA_V + k, 0),
        _sds((s_len, D_MODEL), F32), (tm, D_MODEL), lambda i, j, k: (i, 0),
        (tm, D_MODEL),
    )
    (x_out,) = _rowwise(
        "mix_residual", lambda x_, za_, zb_, g_: x_ + g_ * (za_ + zb_), [x, za, zb], [gate],
        [_sds((s_len, D_MODEL), F32)], [], 256,
    )
    return x_out, (weights, h, swa3, pm, q4, k4, v4, oa3, lse_a, ob4, lse_b, za, zb)


def _mix_bwd(dxo, x, gamma, sc, gate, q_norm, kv_norm, bias, sinkb, cos2, sin2, saved, hooks):
    weights, h, swa3, pm, q4, k4, v4, oa3, lse_a, ob4, lse_b, za, zb = saved
    w_swa, w_mla, w_o, wq_ext, wkv = weights
    s_len = x.shape[0]
    tm = tk = min(ROW_TILE, s_len)
    vec = _sds((1, D_MODEL), F32)
    n_a = SWA_HEADS * SWA_HEAD_DIM

    dz, dgate = _rowwise(
        "mix_bwd_gate", lambda dxo_, za_, zb_, g_: (g_ * dxo_, _sum0((za_ + zb_) * dxo_)),
        [dxo, za, zb], [gate], [_sds((s_len, D_MODEL), BF16)], [vec], 256,
    )
    dwo_a = _mm(
        "mix_bwd_wo_swa", "tn", (SWA_HEADS, 1, s_len // tk),
        oa3, (None, tk, SWA_HEAD_DIM), lambda i, j, k: (i, k, 0),
        dz, (tk, D_MODEL), lambda i, j, k: (k, 0),
        _sds((n_a, D_MODEL), F32), (SWA_HEAD_DIM, D_MODEL), lambda i, j, k: (i, 0),
        (SWA_HEAD_DIM, D_MODEL),
    )
    dwo_b = _mm(
        "mix_bwd_wo_mla", "tn", (MLA_HEADS, 1, s_len // tk),
        ob4, (None, tk, MLA_V), lambda i, j, k: (i, k, 0),
        dz, (tk, D_MODEL), lambda i, j, k: (k, 0),
        _sds((MLA_HEADS * MLA_V, D_MODEL), F32), (MLA_V, D_MODEL), lambda i, j, k: (i, 0),
        (MLA_V, D_MODEL),
    )
    doa3 = _mm(
        "mix_bwd_do_swa", "nt", (s_len // tm, SWA_HEADS, 1),
        dz, (tm, D_MODEL), lambda i, j, k: (i, 0),
        w_o, (SWA_HEAD_DIM, D_MODEL), lambda i, j, k: (j, 0),
        _sds((SWA_HEADS, s_len, SWA_HEAD_DIM), BF16), (None, tm, SWA_HEAD_DIM), lambda i, j, k: (j, i, 0),
        (tm, SWA_HEAD_DIM), deps=hooks.after_gate(dz),
    )
    dob4 = _mm(
        "mix_bwd_do_mla", "nt", (s_len // tm, MLA_HEADS, 1),
        dz, (tm, D_MODEL), lambda i, j, k: (i, 0),
        w_o, (MLA_V, D_MODEL), lambda i, j, k: (n_a // MLA_V + j, 0),
        _sds((MLA_HEADS, s_len, MLA_V), BF16), (None, tm, MLA_V), lambda i, j, k: (j, i, 0),
        (tm, MLA_V),
    )
    dq3, dka, dkb, dva, dvb, dbias, dsink = _swa_bwd(swa3, bias, sinkb, oa3, doa3, lse_a)
    dq4, dk4, dv4 = _mla_bwd(q4, k4, v4, ob4, dob4, lse_b)
    dpm, dq_norm, dkv_norm, dwq_ext, dwkv = _mla_proj_bwd(pm, cos2, sin2, dq4, dk4, dv4, q_norm, kv_norm, wq_ext, wkv)

    def fold(da, db):
        return da + jnp.concatenate([db[:, WINDOW:], jnp.zeros_like(db[:, :WINDOW])], axis=1)

    dswa3 = jnp.concatenate([dq3, fold(dka, dkb).astype(BF16), fold(dva, dvb).astype(BF16)], axis=0)

    dw_swa = _mm(
        "mix_bwd_win_swa", "tn", (N_SWA_BLOCKS, 1, s_len // tk),
        dswa3, (None, tk, SWA_HEAD_DIM), lambda i, j, k: (i, k, 0),
        h, (tk, D_MODEL), lambda i, j, k: (k, 0),
        _sds((N_SWA_COLS, D_MODEL), F32), (SWA_HEAD_DIM, D_MODEL), lambda i, j, k: (i, 0),
        (SWA_HEAD_DIM, D_MODEL),
    )
    dw_mla = _mm(
        "mix_bwd_win_mla", "tn", (1, 1, s_len // tk),
        dpm, (tk, N_MLA_COLS), lambda i, j, k: (k, 0),
        h, (tk, D_MODEL), lambda i, j, k: (k, 0),
        _sds((N_MLA_COLS, D_MODEL), F32), (N_MLA_COLS, D_MODEL), lambda i, j, k: (0, 0),
        (N_MLA_COLS, D_MODEL),
    )
    dh_a = _mm(
        "mix_bwd_dh_swa", "nn", (s_len // tm, 1, N_SWA_BLOCKS),
        dswa3, (None, tm, SWA_HEAD_DIM), lambda i, j, k: (k, i, 0),
        w_swa, (SWA_HEAD_DIM, D_MODEL), lambda i, j, k: (k, 0),
        _sds((s_len, D_MODEL), F32), (tm, D_MODEL), lambda i, j, k: (i, 0),
        (tm, D_MODEL),
    )
    dh_b = _mm(
        "mix_bwd_dh_mla", "nn", (s_len // tm, 1, 1),
        dpm, (tm, N_MLA_COLS), lambda i, j, k: (i, 0),
        w_mla, (N_MLA_COLS, D_MODEL), lambda i, j, k: (0, 0),
        _sds((s_len, D_MODEL), F32), (tm, D_MODEL), lambda i, j, k: (i, 0),
        (tm, D_MODEL),
    )
    dx, dsc, dsh, dgamma = _normmod_bwd_call("mix_bwd_normmod", [dh_a, dh_b], x, dxo, gamma, sc)

    half = MLA_ROPE // 2
    n_mla_in = D_IN - N_SWA_COLS
    dw_mla_base = dw_mla[:n_mla_in]
    dw_mla_base = dw_mla_base.at[n_mla_in - half:].add(dw_mla[n_mla_in:n_mla_in + half])
    dw_mla_base = dw_mla_base.at[n_mla_in - MLA_ROPE:n_mla_in - half].add(dw_mla[n_mla_in + half:])
    dw_in_t = jnp.concatenate([dw_swa, dw_mla_base], axis=0)
    dw_uq_t = dwq_ext[:, :MLA_QK]
    dw_uq_t = dw_uq_t.at[:, MLA_QK - half:].add(dwq_ext[:, MLA_QK:MLA_QK + half])
    dw_uq_t = dw_uq_t.at[:, MLA_NOPE:MLA_QK - half].add(dwq_ext[:, MLA_QK + half:])
    dw_o = jnp.concatenate([dwo_a, dwo_b], axis=0)
    rest = jnp.concatenate(
        [
            dw_in_t.reshape(N_CHIPS, R_IN, D_MODEL),
            dw_o.reshape(N_CHIPS, R_O, D_MODEL),
            dw_uq_t.reshape(N_CHIPS, R_UQ, D_MODEL),
            dwkv.reshape(N_CHIPS, R_UKV, D_MODEL),
            jnp.zeros((N_CHIPS, F_SHARD - O_PAD, D_MODEL), F32),
        ],
        axis=1,
    ).astype(BF16)
    return dx, rest, (dsh, dsc, dgate, dgamma), dq_norm, dkv_norm, dbias, dsink


def _device_step(x, target, mod, gu1_src, down1_src, mix_src, ffn2_src, norms, q_norm, kv_norm, sinks, rel_bias,
                 hooks2=None, hooks_mix=None, mix_done=None, hooks1=None):
    s_len = x.shape[0]
    sh1, sc1, g1, sh2, sc2, g2, sh3, sc3, g3 = [mod[:, i * D_MODEL:(i + 1) * D_MODEL] for i in range(N_MOD)]
    n1, n2, n3, nf = norms
    bkt = jnp.asarray(_bucket_map())
    bias = _bias_expand(rel_bias.T, bkt).reshape(SWA_HEADS, WINDOW, 2 * WINDOW)
    sinkb = jnp.broadcast_to(sinks.reshape(SWA_HEADS, 1, 1), (SWA_HEADS, WINDOW, 1))
    cos2, sin2 = _rope_tables(s_len)

    x1, saved1 = _ffn_fwd("ffn1", x, n1, sh1, sc1, g1, gu1_src, 0, down1_src, 0, sh1)
    x2, saved2 = _mix_fwd(x1, n2, sh2, sc2, g2, mix_src, q_norm, kv_norm, bias, sinkb, cos2, sin2)
    x3, saved3 = _ffn_fwd("ffn2", x2, n3, sh3, sc3, g3, ffn2_src, 0, None, 2, x2)

    def head(x_, t_, g_):
        xr, r = _rms(x_)
        err = xr * g_ - t_
        dy = err * (1.0 / D_MODEL)
        return _rms_bwd(dy * g_, xr, r), _sum0(err * err), _sum0(dy * xr)

    vec = _sds((1, D_MODEL), F32)
    dx3, sq, dnf = _rowwise("loss_head", head, [x3, target], [nf], [_sds((s_len, D_MODEL), F32)], [vec, vec], 256)
    loss_part = (0.5 / D_MODEL) * jnp.sum(sq)

    dx2, gb2, (dsh3, dsc3, dg3, dn3) = _ffn_bwd("ffn2", dx3, x2, n3, sc3, g3, saved3, hooks2 or _BwdHooks())
    dx1, rest, (dsh2, dsc2, dg2, dn2), dq_norm, dkv_norm, dbias, dsink = _mix_bwd(
        dx2, x1, n2, sc2, g2, q_norm, kv_norm, bias, sinkb, cos2, sin2, saved2, hooks_mix or _BwdHooks()
    )
    rest = rest[:, None]
    deps1 = mix_done(dx1, rest) if mix_done is not None else []
    dx0, gb1, (dsh1, dsc1, dg1, dn1) = _ffn_bwd(
        "ffn1", dx1, x, n1, sc1, g1, saved1, hooks1 or _BwdHooks(), deps=deps1
    )

    dmod = jnp.concatenate([dsh1, dsc1, dg1, dsh2, dsc2, dg2, dsh3, dsc3, dg3], axis=-1)
    drel = _bias_reduce(dbias.reshape(SWA_HEADS, -1), bkt).T
    dsinks = jnp.sum(dsink, axis=(1, 2)).reshape(1, SWA_HEADS)
    small = (dn1, dn2, dn3, dnf, dq_norm, dkv_norm, dsinks, drel)
    return loss_part, dx0, (gb1, gb2, rest), dmod, small


_MESH = pl.DeviceIdType.MESH


def _place():
    return lax.axis_index("x"), lax.axis_index("y"), lax.axis_index("c")


def _other_chips(x, y):
    return [(1 - x, y), (x, 1 - y), (1 - x, 1 - y)]


def _allgather_small(name, blk):
    m_per, n = blk.shape

    def body(x_ref, out_ref, send_sems, recv_sems, local_sem):
        x, y, c = _place()
        me, sibling = (x, y, c), (x, y, 1 - c)
        chips = _other_chips(x, y)

        def rows(px, py, pc):
            return out_ref.at[pl.ds((4 * px + 2 * py + pc) * m_per, m_per), :]

        def copy(k, block, to, src=None):
            return pltpu.make_async_remote_copy(
                src_ref=rows(*block) if src is None else src, dst_ref=rows(*block),
                send_sem=send_sems.at[k], recv_sem=recv_sems.at[k], device_id=to, device_id_type=_MESH,
            )

        mine = pltpu.make_async_copy(x_ref, rows(*me), local_sem)
        mine.start()
        first = [copy(0, me, sibling, src=x_ref)]
        first += [copy(1 + j, me, (*chip, c), src=x_ref) for j, chip in enumerate(chips)]
        for cp in first:
            cp.start()
        passed = [copy(4 + j, (*chip, c), sibling) for j, chip in enumerate(chips)]
        for j, chip in enumerate(chips):
            copy(1 + j, (*chip, c), me).wait_recv()
            passed[j].start()
        copy(0, sibling, me).wait_recv()
        for j, chip in enumerate(chips):
            copy(4 + j, (*chip, 1 - c), me).wait_recv()
        for cp in first + passed:
            cp.wait_send()
        mine.wait()

    return pl.pallas_call(
        body,
        name=name,
        out_shape=_sds((N_DEV * m_per, n), blk.dtype),
        in_specs=[pl.BlockSpec(memory_space=pltpu.VMEM)],
        out_specs=pl.BlockSpec(memory_space=pltpu.VMEM),
        scratch_shapes=[pltpu.SemaphoreType.DMA((7,)), pltpu.SemaphoreType.DMA((7,)), pltpu.SemaphoreType.DMA],
    )(blk)


def _allgather_weights(name, own):
    n = own.shape[0]

    def body(own_ref, g_ref, token, send_sems, recv_sems, local_sem):
        x, y, c = _place()
        sibling = (x, y, 1 - c)
        chips = _other_chips(x, y)
        mine = pltpu.make_async_copy(own_ref, g_ref.at[2 * x + y], local_sem)
        mine.start()
        for j, chip in enumerate(chips):
            for cp in _gather_sends(n, own_ref, g_ref, send_sems.at[j], recv_sems.at[j], x, y, c, chip):
                cp.start()
        for j, chip in enumerate(chips):
            _gather_all(own_ref, g_ref, send_sems.at[j], recv_sems.at[j], chip, c, c).wait_recv()
            for cp in _gather_forwards(n, g_ref, send_sems.at[3 + j], recv_sems.at[3 + j], chip, c, sibling):
                cp.start()
        for j, chip in enumerate(chips):
            _gather_all(own_ref, g_ref, send_sems.at[3 + j], recv_sems.at[3 + j], chip, 1 - c, c).wait_recv()
        for j, chip in enumerate(chips):
            _gather_all(own_ref, g_ref, send_sems.at[j], recv_sems.at[j], chip, c, c).wait_send()
            _gather_all(own_ref, g_ref, send_sems.at[3 + j], recv_sems.at[3 + j], chip, c, c).wait_send()
        mine.wait()
        token[...] = jnp.zeros_like(token)

    return pl.pallas_call(
        body,
        name=name,
        out_shape=[_sds((N_CHIPS,) + own.shape, own.dtype), _sds((8, 128), F32)],
        in_specs=[pl.BlockSpec(memory_space=pl.ANY)],
        out_specs=[pl.BlockSpec(memory_space=pl.ANY), pl.BlockSpec(memory_space=pltpu.VMEM)],
        scratch_shapes=[pltpu.SemaphoreType.DMA((6,)), pltpu.SemaphoreType.DMA((6,)), pltpu.SemaphoreType.DMA],
    )(own)


def _gather_sends(n, own_ref, g_ref, send_sem, recv_sem, x, y, c, chip):
    return [
        pltpu.make_async_remote_copy(
            src_ref=own_ref.at[p, pl.ds(c * HALF, HALF), :], dst_ref=g_ref.at[2 * x + y, p, pl.ds(c * HALF, HALF), :],
            send_sem=send_sem, recv_sem=recv_sem, device_id=(*chip, c), device_id_type=_MESH,
        )
        for p in range(n)
    ]


def _gather_forwards(n, g_ref, send_sem, recv_sem, chip, c, sibling):
    return [
        pltpu.make_async_remote_copy(
            src_ref=g_ref.at[2 * chip[0] + chip[1], p, pl.ds(c * HALF, HALF), :],
            dst_ref=g_ref.at[2 * chip[0] + chip[1], p, pl.ds(c * HALF, HALF), :],
            send_sem=send_sem, recv_sem=recv_sem, device_id=sibling, device_id_type=_MESH,
        )
        for p in range(n)
    ]


def _gather_all(own_ref, g_ref, send_sem, recv_sem, chip, half, c):
    return pltpu.make_async_remote_copy(
        src_ref=own_ref.at[:, pl.ds(c * HALF, HALF), :],
        dst_ref=g_ref.at[2 * chip[0] + chip[1], :, pl.ds(half * HALF, HALF), :],
        send_sem=send_sem, recv_sem=recv_sem, device_id=(*chip, c), device_id_type=_MESH,
    )


_HBM_SPEC = pl.BlockSpec(memory_space=pltpu.HBM)
_SEM_SPEC = pl.BlockSpec(memory_space=pltpu.SEMAPHORE)
_ANY_SPEC = pl.BlockSpec(memory_space=pl.ANY)
_VMEM_SPEC = pl.BlockSpec(memory_space=pltpu.VMEM)
_SPLIT_PARAMS = pltpu.CompilerParams(has_side_effects=pltpu.SideEffectType.DATAFLOW_SIDE_EFFECTING)
_TOKEN = jax.ShapeDtypeStruct((8, 128), F32)


def _in_hbm(a):
    return pltpu.with_memory_space_constraint(a, pltpu.HBM)


class _GatherBehind:
    def __init__(self, tag, own, then=None):
        self.tag, self.n, self.own, self.then = tag, own.shape[0], own, then

    def start(self, after):
        tag, own, n = self.tag, self.own, self.n
        x, y, _ = _place()
        g_init = lax.dynamic_update_slice(
            lax.empty((N_CHIPS,) + own.shape, own.dtype), own[None], (2 * x + y, 0, 0, 0)
        )

        def body(own_ref, g_ref, *rest):
            send_sems, recv_sems, _, _, token = rest[len(after):]
            x, y, c = _place()
            for j, chip in enumerate(_other_chips(x, y)):
                for cp in _gather_sends(n, own_ref, g_ref, send_sems.at[j], recv_sems.at[j], x, y, c, chip):
                    cp.start()
            token[...] = jnp.zeros_like(token)

        self.send1, self.recv1, self.own, self.g, self.token = pl.pallas_call(
            body,
            name=f"{tag}_start",
            out_shape=(
                pltpu.SemaphoreType.DMA((3,)), pltpu.SemaphoreType.DMA((3,)),
                pltpu.HBM(own.shape, own.dtype), pltpu.HBM(g_init.shape, g_init.dtype), _TOKEN,
            ),
            in_specs=(_HBM_SPEC, _HBM_SPEC) + (_ANY_SPEC,) * len(after),
            out_specs=(_SEM_SPEC, _SEM_SPEC, _HBM_SPEC, _HBM_SPEC, _VMEM_SPEC),
            input_output_aliases={0: 2, 1: 3},
            compiler_params=_SPLIT_PARAMS,
        )(_in_hbm(own), _in_hbm(g_init), *after)

    def mid(self, after):
        n = self.n

        def body(own_ref, g_ref, send1, recv1, after_ref, send2, recv2, g_out, token):
            x, y, c = _place()
            sibling = (x, y, 1 - c)
            chips = _other_chips(x, y)
            for j, chip in enumerate(chips):
                _gather_all(own_ref, g_ref, send1.at[j], recv1.at[j], chip, c, c).wait_recv()
                for cp in _gather_forwards(n, g_ref, send2.at[j], recv2.at[j], chip, c, sibling):
                    cp.start()
            for j, chip in enumerate(chips):
                _gather_all(own_ref, g_ref, send1.at[j], recv1.at[j], chip, c, c).wait_send()
            token[...] = jnp.zeros_like(token)

        self.send2, self.recv2, self.g, token = pl.pallas_call(
            body,
            name=f"{self.tag}_mid",
            out_shape=(
                pltpu.SemaphoreType.DMA((3,)), pltpu.SemaphoreType.DMA((3,)),
                pltpu.HBM(self.g.shape, self.g.dtype), _TOKEN,
            ),
            in_specs=(_HBM_SPEC, _HBM_SPEC, _SEM_SPEC, _SEM_SPEC, _ANY_SPEC),
            out_specs=(_SEM_SPEC, _SEM_SPEC, _HBM_SPEC, _VMEM_SPEC),
            input_output_aliases={1: 2},
            compiler_params=_SPLIT_PARAMS,
        )(self.own, self.g, self.send1, self.recv1, after)
        if self.then is None:
            return [token]
        self.then.start([token])
        return [token, self.then.token]

    def get(self, after):
        def body(g_ref, send2, recv2, after_ref, g_out):
            x, y, c = _place()
            for j, chip in enumerate(_other_chips(x, y)):
                slot_in = g_ref.at[2 * chip[0] + chip[1], :, pl.ds((1 - c) * HALF, HALF), :]
                slot_out = g_ref.at[2 * chip[0] + chip[1], :, pl.ds(c * HALF, HALF), :]
                cp = pltpu.make_async_remote_copy(
                    src_ref=slot_out, dst_ref=slot_in, send_sem=send2.at[j], recv_sem=recv2.at[j],
                    device_id=(x, y, 1 - c), device_id_type=_MESH,
                )
                cp.wait_send()
                cp.wait_recv()

        return pl.pallas_call(
            body,
            name=f"{self.tag}_wait",
            out_shape=pltpu.HBM(self.g.shape, self.g.dtype),
            in_specs=(_HBM_SPEC, _SEM_SPEC, _SEM_SPEC, _ANY_SPEC),
            out_specs=_HBM_SPEC,
            input_output_aliases={0: 0},
            compiler_params=_SPLIT_PARAMS,
        )(self.g, self.send2, self.recv2, after)


def _pair_exchange(name, gb):
    def body(gb_ref, land_ref, send_sem, recv_sem):
        x, y, c = _place()
        theirs = gb_ref.at[:, :, pl.ds((1 - c) * HALF, HALF), :]
        cp = pltpu.make_async_remote_copy(
            src_ref=theirs, dst_ref=land_ref, send_sem=send_sem, recv_sem=recv_sem,
            device_id=(x, y, 1 - c), device_id_type=_MESH,
        )
        cp.start()
        cp.wait()

    return pl.pallas_call(
        body,
        name=name,
        out_shape=_sds((N_CHIPS, gb.shape[1], HALF, D_MODEL), gb.dtype),
        in_specs=[pl.BlockSpec(memory_space=pl.ANY)],
        out_specs=pl.BlockSpec(memory_space=pl.ANY),
        scratch_shapes=[pltpu.SemaphoreType.DMA, pltpu.SemaphoreType.DMA],
    )(gb)


def _pair_sum(name, gb, land):
    def body(gb_ref, land_ref, o_ref):
        c = lax.axis_index("c")
        mine = gb_ref[pl.ds(pl.multiple_of(c * HALF, 16), HALF), :]
        o_ref[...] = (mine.astype(F32) + land_ref[...].astype(F32)).astype(o_ref.dtype)

    return pl.pallas_call(
        body,
        name=name,
        grid=(N_CHIPS, gb.shape[1]),
        in_specs=[
            pl.BlockSpec((None, None, F_SHARD, D_MODEL), lambda j, p: (j, p, 0, 0)),
            pl.BlockSpec((None, None, HALF, D_MODEL), lambda j, p: (j, p, 0, 0)),
        ],
        out_specs=pl.BlockSpec((None, None, HALF, D_MODEL), lambda j, p: (j, p, 0, 0)),
        out_shape=_sds(land.shape, BF16),
        compiler_params=_cparams(("parallel", "parallel")),
    )(gb, land)


def _chip_exchange(name, part):
    def body(p_ref, land_ref, send_sems, recv_sems, local_sem):
        x, y, c = _place()
        me = 2 * x + y
        chips = _other_chips(x, y)
        mine = pltpu.make_async_copy(p_ref.at[me], land_ref.at[me], local_sem)
        mine.start()

        def copy(k, chip):
            return pltpu.make_async_remote_copy(
                src_ref=p_ref.at[2 * chip[0] + chip[1]], dst_ref=land_ref.at[me],
                send_sem=send_sems.at[k], recv_sem=recv_sems.at[k], device_id=(*chip, c), device_id_type=_MESH,
            )

        sends = [copy(k, chip) for k, chip in enumerate(chips)]
        for cp in sends:
            cp.start()
        for k, chip in enumerate(chips):
            pltpu.make_async_remote_copy(
                src_ref=p_ref.at[me], dst_ref=land_ref.at[2 * chip[0] + chip[1]],
                send_sem=send_sems.at[k], recv_sem=recv_sems.at[k], device_id=(*chip, c), device_id_type=_MESH,
            ).wait_recv()
        for cp in sends:
            cp.wait_send()
        mine.wait()

    return pl.pallas_call(
        body,
        name=name,
        out_shape=_sds(part.shape, part.dtype),
        in_specs=[pl.BlockSpec(memory_space=pl.ANY)],
        out_specs=pl.BlockSpec(memory_space=pl.ANY),
        scratch_shapes=[pltpu.SemaphoreType.DMA((3,)), pltpu.SemaphoreType.DMA((3,)), pltpu.SemaphoreType.DMA],
    )(part)


def _chip_sum_share(name, land):
    n = land.shape[1]

    def body(l_ref, r_ref, buf, send_sems, recv_sems, local_sems):
        p = pl.program_id(0)
        x, y, c = _place()
        acc = l_ref[0].astype(F32)
        for j in range(1, N_CHIPS):
            acc = acc + l_ref[j].astype(F32)
        buf[p] = acc

        def copies(q):
            mine = r_ref.at[q, pl.ds(c * HALF, HALF), :]
            theirs = r_ref.at[q, pl.ds((1 - c) * HALF, HALF), :]
            local = pltpu.make_async_copy(buf.at[q], mine, local_sems.at[q])
            out = pltpu.make_async_remote_copy(
                src_ref=buf.at[q], dst_ref=mine, send_sem=send_sems.at[q], recv_sem=recv_sems.at[q],
                device_id=(x, y, 1 - c), device_id_type=_MESH,
            )
            arrive = pltpu.make_async_remote_copy(
                src_ref=buf.at[q], dst_ref=theirs, send_sem=send_sems.at[q], recv_sem=recv_sems.at[q],
                device_id=(x, y, 1 - c), device_id_type=_MESH,
            )
            return local, out, arrive

        local, out, _ = copies(p)
        local.start()
        out.start()

        @pl.when(p == n - 1)
        def _():
            for q in range(n):
                local, out, arrive = copies(q)
                arrive.wait_recv()
                out.wait_send()
                local.wait()

    return pl.pallas_call(
        body,
        name=name,
        grid=(n,),
        in_specs=[pl.BlockSpec((N_CHIPS, None, HALF, D_MODEL), lambda p: (0, p, 0, 0))],
        out_specs=pl.BlockSpec(memory_space=pl.ANY),
        out_shape=_sds((n, F_SHARD, D_MODEL), F32),
        scratch_shapes=[
            pltpu.VMEM((n, HALF, D_MODEL), F32),
            pltpu.SemaphoreType.DMA((n,)), pltpu.SemaphoreType.DMA((n,)), pltpu.SemaphoreType.DMA((n,)),
        ],
        compiler_params=_cparams(("arbitrary",)),
    )(land)


class _ReduceBehind:
    def __init__(self, tag, gb, after=()):
        self.tag = tag
        n = gb.shape[1]
        land = lax.empty((N_CHIPS, n, HALF, D_MODEL), gb.dtype)

        def body(gb_ref, land_ref, *rest):
            send_sem, recv_sem, _, _, token = rest[len(after):]
            self._pair_copy(gb_ref, land_ref, send_sem, recv_sem).start()
            token[...] = jnp.zeros_like(token)

        self.send, self.recv, self.gb, self.land, self.token = pl.pallas_call(
            body,
            name=f"grads_pair_start_{tag}",
            out_shape=(
                pltpu.SemaphoreType.DMA((1,)), pltpu.SemaphoreType.DMA((1,)),
                pltpu.HBM(gb.shape, gb.dtype), pltpu.HBM(land.shape, land.dtype), _TOKEN,
            ),
            in_specs=(_HBM_SPEC, _HBM_SPEC) + (_ANY_SPEC,) * len(after),
            out_specs=(_SEM_SPEC, _SEM_SPEC, _HBM_SPEC, _HBM_SPEC, _VMEM_SPEC),
            input_output_aliases={0: 2, 1: 3},
            compiler_params=_SPLIT_PARAMS,
        )(_in_hbm(gb), _in_hbm(land), *after)

    @staticmethod
    def _pair_copy(gb_ref, land_ref, send_sem, recv_sem):
        x, y, c = _place()
        return pltpu.make_async_remote_copy(
            src_ref=gb_ref.at[:, :, pl.ds((1 - c) * HALF, HALF), :], dst_ref=land_ref,
            send_sem=send_sem.at[0], recv_sem=recv_sem.at[0], device_id=(x, y, 1 - c), device_id_type=_MESH,
        )

    @staticmethod
    def _chip_copies(p_ref, land_ref, send_sems, recv_sems):
        x, y, c = _place()
        me = 2 * x + y
        sends, arrivals = [], []
        for k, chip in enumerate(_other_chips(x, y)):
            them = 2 * chip[0] + chip[1]
            sends.append(pltpu.make_async_remote_copy(
                src_ref=p_ref.at[them], dst_ref=land_ref.at[me], send_sem=send_sems.at[k], recv_sem=recv_sems.at[k],
                device_id=(*chip, c), device_id_type=_MESH,
            ))
            arrivals.append(pltpu.make_async_remote_copy(
                src_ref=p_ref.at[me], dst_ref=land_ref.at[them], send_sem=send_sems.at[k], recv_sem=recv_sems.at[k],
                device_id=(*chip, c), device_id_type=_MESH,
            ))
        return sends, arrivals

    def mid(self, after):
        tag = self.tag

        def wait_body(gb_ref, land_ref, send_sem, recv_sem, after_ref, gb_out, land_out):
            cp = self._pair_copy(gb_ref, land_ref, send_sem, recv_sem)
            cp.wait_send()
            cp.wait_recv()

        gb, land = pl.pallas_call(
            wait_body,
            name=f"grads_pair_wait_{tag}",
            out_shape=(pltpu.HBM(self.gb.shape, self.gb.dtype), pltpu.HBM(self.land.shape, self.land.dtype)),
            in_specs=(_HBM_SPEC, _HBM_SPEC, _SEM_SPEC, _SEM_SPEC, _ANY_SPEC),
            out_specs=(_HBM_SPEC, _HBM_SPEC),
            input_output_aliases={0: 0, 1: 1},
            compiler_params=_SPLIT_PARAMS,
        )(self.gb, self.land, self.send, self.recv, after)
        part = _pair_sum(f"grads_pair_sum_{tag}", gb, land)

        x, y, _ = _place()
        start = (2 * x + y, 0, 0, 0)
        own = lax.dynamic_slice(part, start, (1,) + part.shape[1:])
        land2 = lax.dynamic_update_slice(lax.empty(part.shape, part.dtype), own, start)

        def start_body(p_ref, land_ref, send_sems, recv_sems, p_out, land_out, token):
            for cp in self._chip_copies(p_ref, land_ref, send_sems, recv_sems)[0]:
                cp.start()
            token[...] = jnp.zeros_like(token)

        self.send2, self.recv2, self.part, self.land2, token = pl.pallas_call(
            start_body,
            name=f"grads_chip_start_{tag}",
            out_shape=(
                pltpu.SemaphoreType.DMA((3,)), pltpu.SemaphoreType.DMA((3,)),
                pltpu.HBM(part.shape, part.dtype), pltpu.HBM(land2.shape, land2.dtype), _TOKEN,
            ),
            in_specs=(_HBM_SPEC, _HBM_SPEC),
            out_specs=(_SEM_SPEC, _SEM_SPEC, _HBM_SPEC, _HBM_SPEC, _VMEM_SPEC),
            input_output_aliases={0: 2, 1: 3},
            compiler_params=_SPLIT_PARAMS,
        )(_in_hbm(part), _in_hbm(land2))
        return [token]

    def get(self, after):
        n_after = len(after)

        def wait_body(p_ref, land_ref, send_sems, recv_sems, *rest):
            sends, arrivals = self._chip_copies(p_ref, land_ref, send_sems, recv_sems)
            for cp in arrivals:
                cp.wait_recv()
            for cp in sends:
                cp.wait_send()

        _, land2 = pl.pallas_call(
            wait_body,
            name=f"grads_chip_wait_{self.tag}",
            out_shape=(pltpu.HBM(self.part.shape, self.part.dtype), pltpu.HBM(self.land2.shape, self.land2.dtype)),
            in_specs=(_HBM_SPEC, _HBM_SPEC, _SEM_SPEC, _SEM_SPEC) + (_ANY_SPEC,) * n_after,
            out_specs=(_HBM_SPEC, _HBM_SPEC),
            input_output_aliases={0: 0, 1: 1},
            compiler_params=_SPLIT_PARAMS,
        )(self.part, self.land2, self.send2, self.recv2, *after)
        return _chip_sum_share(f"grads_chip_sum_share_{self.tag}", land2)


def _allreduce_small(blk):
    gathered = _allgather_small("allgather_small_grads", blk).reshape(N_DEV, PK_ROWS, 128)

    def body(g_ref, o_ref):
        acc = g_ref[0]
        for k in range(1, N_DEV):
            acc = acc + g_ref[k]
        o_ref[...] = acc

    total = pl.pallas_call(body, name="small_grads_sum", out_shape=_sds((PK_ROWS, 128), F32))(gathered)
    return gathered, total


def _adamw(name, w, g, m, v):
    rows, cols = w.shape
    tm = rows
    while tm * cols * 4 > (1 << 20) and tm % 16 == 0:
        tm //= 2

    def fn(w_, g_, m_, v_):
        m_new = ADAM_B1 * m_ + (1.0 - ADAM_B1) * g_
        v_new = ADAM_B2 * v_ + (1.0 - ADAM_B2) * (g_ * g_)
        m_hat = m_new / (1.0 - ADAM_B1 ** ADAM_STEP)
        v_hat = v_new / (1.0 - ADAM_B2 ** ADAM_STEP)
        delta = -ADAM_LR * (m_hat / (jnp.sqrt(v_hat) + ADAM_EPS) + ADAM_WD * w_)
        return delta, m_new, v_new

    out = _sds(w.shape, F32)
    return _rowwise(name, fn, [w, g, m, v], [], [out, out, out], [], tm)


def _pack_small(b_mod_like, n1, n2, n3, nf, qn, kvn, sinks, rel):
    parts = [
        b_mod_like.reshape(PK_MOD, 128),
        n1.reshape(8, 128), n2.reshape(8, 128), n3.reshape(8, 128), nf.reshape(8, 128),
        qn.reshape(2, 128), kvn.reshape(1, 128),
        jnp.pad(sinks.reshape(1, SWA_HEADS), ((0, 0), (0, 128 - SWA_HEADS))),
        rel.reshape(2, 128),
        jnp.zeros((2, 128), F32),
    ]
    return jnp.concatenate(parts, axis=0)


def _unpack_small(p):
    o = PK_MOD
    return (
        p[:o].reshape(1, N_MOD * D_MODEL),
        p[o:o + 8].reshape(1, D_MODEL), p[o + 8:o + 16].reshape(1, D_MODEL),
        p[o + 16:o + 24].reshape(1, D_MODEL), p[o + 24:o + 32].reshape(D_MODEL),
        p[o + 32:o + 34].reshape(1, MLA_Q_RANK), p[o + 34:o + 35].reshape(1, MLA_KV_RANK),
        p[o + 35:o + 36, :SWA_HEADS].reshape(1, SWA_HEADS),
        p[o + 36:o + 38].reshape(NUM_BUCKETS, SWA_HEADS),
    )


def kernel(x, c, w_mod, b_mod, norm_ffn1, ffn1_gate, ffn1_up, ffn1_down, norm_mix, w_in, q_norm, kv_norm, w_uq, w_ukv, sinks, w_o, norm_ffn2, ffn2_gate, ffn2_up, ffn2_down, rel_bias, norm_final, loss_target, m_w_mod, m_b_mod, m_norm_ffn1, m_ffn1_gate, m_ffn1_up, m_ffn1_down, m_norm_mix, m_w_in, m_q_norm, m_kv_norm, m_w_uq, m_w_ukv, m_sinks, m_w_o, m_norm_ffn2, m_ffn2_gate, m_ffn2_up, m_ffn2_down, m_rel_bias, m_norm_final, v_w_mod, v_b_mod, v_norm_ffn1, v_ffn1_gate, v_ffn1_up, v_ffn1_down, v_norm_mix, v_w_in, v_q_norm, v_kv_norm, v_w_uq, v_w_ukv, v_sinks, v_w_o, v_norm_ffn2, v_ffn2_gate, v_ffn2_up, v_ffn2_down, v_rel_bias, v_norm_final):
    ax, ay, ac = _place()
    chip = 2 * ax + ay
    dev = 2 * chip + ac
    n_mod_shard = N_MOD * D_MODEL // N_CHIPS

    def t16(w):
        return w[0].T.astype(BF16)

    rest = jnp.concatenate(
        [
            t16(w_in),
            w_o[0].astype(BF16),
            t16(w_uq).reshape(R_UQ, D_MODEL),
            t16(w_ukv).reshape(R_UKV, D_MODEL),
            jnp.zeros((F_SHARD - O_PAD, D_MODEL), BF16),
        ],
        axis=0,
    )
    own_gu1 = jnp.stack([t16(ffn1_gate), t16(ffn1_up)])
    own_down1 = ffn1_down[0].astype(BF16)[None]
    own_mix = rest[None]
    own_ffn2 = jnp.stack([t16(ffn2_gate), t16(ffn2_up), ffn2_down[0].astype(BF16)])

    (c_act,) = _rowwise(
        "silu_c", lambda v: v * _sigmoid(v), [c.reshape(8, 128)], [], [_sds((8, 128), F32)], [], 8
    )
    c_all = _allgather_small("allgather_c", c_act).reshape(N_DEV, D_MODEL)
    mod_cols = _mm(
        "mod_fwd", "nn", (1, n_mod_shard // 768, 1),
        c_all, (N_DEV, D_MODEL), lambda i, j, k: (0, 0),
        w_mod[0], (D_MODEL, 768), lambda i, j, k: (0, j),
        _sds((N_DEV, n_mod_shard), F32), (N_DEV, 768), lambda i, j, k: (0, j),
        (N_DEV, 768),
    )
    mod_all = _allgather_small("allgather_mod", mod_cols).reshape(N_CHIPS, 2, N_DEV, n_mod_shard)[:, 0]
    mod_all = mod_all.transpose(1, 0, 2).reshape(N_DEV, N_MOD * D_MODEL)
    mod = lax.dynamic_slice_in_dim(mod_all, dev, 1, axis=0) + b_mod

    norms = (norm_ffn1, norm_mix, norm_ffn2, norm_final.reshape(1, D_MODEL))

    ffn2_src = _GatherBehind("gather_ffn2", own_ffn2)
    mix_src = _GatherBehind("gather_mix", own_mix, then=ffn2_src)
    down1_src = _GatherBehind("gather_down1", own_down1, then=mix_src)
    gu1_src = _GatherBehind("gather_gu1", own_gu1, then=down1_src)
    gu1_src.start([mod_all])

    reducing, red = {}, {}

    def begin(tag, gb, after):
        reducing[tag] = _ReduceBehind(tag, gb, after=after)
        return [reducing[tag].token]

    def ffn2_gu_done(gb):
        return begin("ffn2_gu", gb, reducing["ffn2_down"].mid(gb))

    def mix_done(dx1, gb_rest):
        red["ffn2_down"] = reducing["ffn2_down"].get([dx1])
        red["ffn2_gu"] = reducing["ffn2_gu"].get([red["ffn2_down"]])
        return begin("rest", gb_rest, [red["ffn2_gu"]])

    def ffn1_gu_done(gb):
        red["rest"] = reducing["rest"].get([gb])
        begin("ffn1_gu", gb, reducing["ffn1_down"].mid(red["rest"]))
        return reducing["ffn1_gu"].mid(reducing["ffn1_gu"].token)

    loss_part, grad_x, _, dmod, small = _device_step(
        x[0], loss_target[0], mod, gu1_src, down1_src, mix_src, ffn2_src, norms, q_norm, kv_norm, sinks, rel_bias,
        hooks2=_BwdHooks(after_wdown=lambda gb: begin("ffn2_down", gb, ()), after_wgu=ffn2_gu_done),
        hooks_mix=_BwdHooks(after_gate=lambda dz: reducing["ffn2_gu"].mid(dz)),
        mix_done=mix_done,
        hooks1=_BwdHooks(
            after_swiglu=lambda dab3: reducing["rest"].mid(dab3),
            after_wdown=lambda gb: begin("ffn1_down", gb, ()),
            after_wgu=ffn1_gu_done,
        ),
    )
    loss = lax.psum(loss_part, ("x", "y", "c"))

    gathered, total = _allreduce_small(_pack_small(dmod, *small))
    (g_b_mod, g_n1, g_n2, g_n3, g_nf, g_qn, g_kvn, g_sinks, g_rel) = _unpack_small(total)
    dmod_all = gathered[:, :PK_MOD].reshape(N_DEV, N_MOD * D_MODEL)
    dmod_cols = lax.dynamic_slice_in_dim(dmod_all, chip * n_mod_shard, n_mod_shard, axis=1)
    g_w_mod = _mm(
        "mod_bwd", "tn", (1, n_mod_shard // 768, 1),
        c_all, (N_DEV, D_MODEL), lambda i, j, k: (0, 0),
        dmod_cols, (N_DEV, 768), lambda i, j, k: (0, j),
        _sds((D_MODEL, n_mod_shard), F32), (D_MODEL, 768), lambda i, j, k: (0, j),
        (D_MODEL, 768),
    )

    r_rest = red["rest"][0]
    transposed = {"ffn1_gate", "ffn1_up", "ffn2_gate", "ffn2_up", "w_in", "w_uq", "w_ukv"}
    g_big = {
        "ffn2_gate": red["ffn2_gu"][0], "ffn2_up": red["ffn2_gu"][1], "ffn2_down": red["ffn2_down"][0],
        "w_in": r_rest[O_IN:O_IN + R_IN],
        "w_o": r_rest[O_O:O_O + R_O],
        "w_uq": r_rest[O_UQ:O_UQ + R_UQ].reshape(MLA_QK, MLA_Q_RANK),
        "w_ukv": r_rest[O_UKV:O_UKV + R_UKV].reshape(MLA_NOPE + MLA_V, MLA_KV_RANK),
        "w_mod": g_w_mod,
    }
    w_big = {
        "ffn2_gate": (ffn2_gate, m_ffn2_gate, v_ffn2_gate),
        "ffn2_up": (ffn2_up, m_ffn2_up, v_ffn2_up), "ffn2_down": (ffn2_down, m_ffn2_down, v_ffn2_down),
        "w_in": (w_in, m_w_in, v_w_in), "w_o": (w_o, m_w_o, v_w_o), "w_uq": (w_uq, m_w_uq, v_w_uq),
        "w_ukv": (w_ukv, m_w_ukv, v_w_ukv), "w_mod": (w_mod, m_w_mod, v_w_mod),
    }
    grads, deltas, new_m, new_v = {}, {}, {}, {}

    def update(names):
        for nm in names:
            w, m, v = w_big[nm]
            if nm in transposed:
                outs = _adamw(f"adamw_{nm}", w[0].T, g_big[nm], m[0].T, v[0].T)
                g_, d_, m_, v_ = [a.T for a in (g_big[nm],) + tuple(outs)]
            else:
                g_, (d_, m_, v_) = g_big[nm], _adamw(f"adamw_{nm}", w[0], g_big[nm], m[0], v[0])
            grads[nm], deltas[nm], new_m[nm], new_v[nm] = g_[None], d_[None], m_[None], v_[None]

    update(list(w_big))
    red1_down = reducing["ffn1_down"].get([deltas[nm] for nm in w_big])
    red1_gu = reducing["ffn1_gu"].get([red1_down])
    g_big.update({"ffn1_gate": red1_gu[0], "ffn1_up": red1_gu[1], "ffn1_down": red1_down[0]})
    w_big.update({
        "ffn1_gate": (ffn1_gate, m_ffn1_gate, v_ffn1_gate), "ffn1_up": (ffn1_up, m_ffn1_up, v_ffn1_up),
        "ffn1_down": (ffn1_down, m_ffn1_down, v_ffn1_down),
    })
    update(["ffn1_gate", "ffn1_up", "ffn1_down"])

    small_names = ["b_mod", "norm_ffn1", "norm_mix", "norm_ffn2", "norm_final", "q_norm", "kv_norm", "sinks", "rel_bias"]
    w_small = (b_mod, norm_ffn1, norm_mix, norm_ffn2, norm_final, q_norm, kv_norm, sinks, rel_bias)
    m_small = (m_b_mod, m_norm_ffn1, m_norm_mix, m_norm_ffn2, m_norm_final, m_q_norm, m_kv_norm, m_sinks, m_rel_bias)
    v_small = (v_b_mod, v_norm_ffn1, v_norm_mix, v_norm_ffn2, v_norm_final, v_q_norm, v_kv_norm, v_sinks, v_rel_bias)
    d_p, m_p, v_p = _adamw("adamw_small", _pack_small(*w_small), total, _pack_small(*m_small), _pack_small(*v_small))
    for nm, g_, d_, m_, v_ in zip(
        small_names,
        (g_b_mod, g_n1, g_n2, g_n3, g_nf, g_qn, g_kvn, g_sinks, g_rel),
        _unpack_small(d_p), _unpack_small(m_p), _unpack_small(v_p),
    ):
        grads[nm], deltas[nm], new_m[nm], new_v[nm] = g_, d_, m_, v_

    order = ["w_mod", "b_mod", "norm_ffn1", "ffn1_gate", "ffn1_up", "ffn1_down", "norm_mix", "w_in", "q_norm", "kv_norm",
             "w_uq", "w_ukv", "sinks", "w_o", "norm_ffn2", "ffn2_gate", "ffn2_up", "ffn2_down", "rel_bias", "norm_final"]
    return (loss, grad_x[None], *[grads[n] for n in order], *[deltas[n] for n in order],
            *[new_m[n] for n in order], *[new_v[n] for n in order])
```

```python
import functools
import math

import jax
import jax.numpy as jnp
import numpy as np
from jax import lax
from jax.experimental import pallas as pl
from jax.experimental.pallas import tpu as pltpu

F32, BF16 = jnp.float32, jnp.bfloat16

D_MODEL = 1024
D_FF = 2816
EPS = 1e-6
N_MOD = 9
SWA_HEADS, SWA_KV_HEADS, SWA_HEAD_DIM, WINDOW = 8, 2, 64, 128
SWA_GROUP = SWA_HEADS // SWA_KV_HEADS
MLA_HEADS, MLA_Q_RANK, MLA_KV_RANK, MLA_NOPE, MLA_ROPE, MLA_V = 4, 256, 128, 128, 64, 128
MLA_QK = MLA_NOPE + MLA_ROPE
ROPE_THETA = 10000.0
NUM_BUCKETS, MAX_DISTANCE = 32, 128
D_IN = 1216
N_SWA_COLS = 768
N_MLA_COLS = 512

ADAM_LR, ADAM_B1, ADAM_B2, ADAM_EPS, ADAM_WD, ADAM_STEP = 0.001, 0.9, 0.999, 1e-08, 0.01, 10

N_CHIPS = 4
N_DEV = 8
F_SHARD = D_FF // N_CHIPS
HALF = F_SHARD // 2
R_IN, R_O, R_UQ, R_UKV = 304, 256, 48, 32
O_IN, O_O, O_UQ, O_UKV, O_PAD = 0, 304, 560, 608, 640

PK_MOD = 72
PK_ROWS = 112
VMEM_LIMIT = 56 << 20
ROW_TILE = 1024

_DN = {
    "nn": (((1,), (0,)), ((), ())),
    "nt": (((1,), (1,)), ((), ())),
    "tn": (((0,), (0,)), ((), ())),
}


def _sds(shape, dtype):
    return jax.ShapeDtypeStruct(tuple(shape), dtype)


def _cparams(sem=None):
    kw = {"vmem_limit_bytes": VMEM_LIMIT}
    if sem is not None:
        kw["dimension_semantics"] = sem
    return pltpu.CompilerParams(**kw)


def _dot(a, b, dims):
    return lax.dot_general(a.astype(BF16), b.astype(BF16), _DN[dims], preferred_element_type=F32)


def _mm(name, dims, grid, a, a_blk, a_idx, b, b_blk, b_idx, out, out_blk, out_idx, acc_shape, alias=None, deps=()):
    nk = grid[2]

    def body(*refs):
        a_ref, b_ref = refs[:2]
        o_ref, acc_ref = refs[-2:]
        part = _dot(a_ref[...], b_ref[...], dims)
        if nk == 1:
            o_ref[...] = part.astype(o_ref.dtype)
            return
        k = pl.program_id(2)

        @pl.when(k == 0)
        def _():
            acc_ref[...] = part

        @pl.when((k > 0) & (k < nk - 1))
        def _():
            acc_ref[...] += part

        @pl.when(k == nk - 1)
        def _():
            o_ref[...] = (acc_ref[...] + part).astype(o_ref.dtype)

    in_specs = [pl.BlockSpec(a_blk, a_idx), pl.BlockSpec(b_blk, b_idx)]
    args = [a, b]
    kw = {}
    if alias is not None:
        in_specs.append(pl.BlockSpec(memory_space=pl.ANY))
        args.append(alias)
        kw["input_output_aliases"] = {2: 0}
    in_specs += [pl.BlockSpec(memory_space=pl.ANY)] * len(deps)
    args += list(deps)
    return pl.pallas_call(
        body,
        name=name,
        grid=grid,
        in_specs=in_specs,
        out_specs=pl.BlockSpec(out_blk, out_idx),
        out_shape=out,
        scratch_shapes=[pltpu.VMEM(acc_shape if nk > 1 else (8, 128), F32)],
        compiler_params=_cparams(("parallel", "parallel", "arbitrary")),
        **kw,
    )(*args)


def _rowwise(name, fn, tiled, full, out_tiled, out_red, tm, deps=()):
    rows = tiled[0].shape[-2]
    grid = (rows // tm,)
    n_in = len(tiled) + len(full)
    n_t = len(out_tiled)
    n_dep = len(deps)

    def tspec(shape):
        nd = len(shape)
        return pl.BlockSpec(tuple(shape[:-2]) + (tm, shape[-1]), lambda i, nd=nd: (0,) * (nd - 2) + (i, 0))

    def fspec(shape):
        nd = len(shape)
        return pl.BlockSpec(tuple(shape), lambda i, nd=nd: (0,) * nd)

    def body(*refs):
        outs = fn(*[r[...] for r in refs[:n_in]])
        if not isinstance(outs, (tuple, list)):
            outs = (outs,)
        out_refs = refs[n_in + n_dep:]
        for r, v in zip(out_refs[:n_t], outs[:n_t]):
            r[...] = v.astype(r.dtype)
        red_refs = out_refs[n_t:]
        if red_refs:
            @pl.when(pl.program_id(0) == 0)
            def _():
                for r in red_refs:
                    r[...] = jnp.zeros_like(r)

            for r, v in zip(red_refs, outs[n_t:]):
                r[...] += v.astype(r.dtype)

    res = pl.pallas_call(
        body,
        name=name,
        grid=grid,
        in_specs=[tspec(a.shape) for a in tiled] + [fspec(a.shape) for a in full]
        + [pl.BlockSpec(memory_space=pl.ANY)] * n_dep,
        out_specs=[tspec(o.shape) for o in out_tiled] + [fspec(o.shape) for o in out_red],
        out_shape=list(out_tiled) + list(out_red),
        compiler_params=_cparams(("arbitrary",) if out_red else ("parallel",)),
    )(*tiled, *full, *deps)
    return res


def _sum0(v):
    return jnp.sum(v, axis=0, keepdims=True)


def _sigmoid(v):
    return 1.0 / (1.0 + jnp.exp(-v))


def _rms(x):
    r = lax.rsqrt(jnp.mean(x * x, axis=-1, keepdims=True) + EPS)
    return x * r, r


def _rms_bwd(u, xr, r):
    return r * (u - xr * jnp.mean(u * xr, axis=-1, keepdims=True))


class _Ready:
    def __init__(self, g):
        self.g = g

    def mid(self, after):
        return []

    def get(self, after):
        return self.g


def _ffn_fwd(tag, x, gamma, sh, sc, gate, gu_src, base, down_src, down_idx, mid_after):
    s_len = x.shape[0]
    deps = gu_src.mid(mid_after)

    def normmod(x_, gamma_, sc_, sh_):
        xr, _ = _rms(x_)
        return xr * gamma_ * (1.0 + sc_) + sh_

    (h,) = _rowwise(f"{tag}_normmod", normmod, [x], [gamma, sc, sh], [_sds((s_len, D_MODEL), BF16)], [], 256, deps=deps)
    g4 = gu_src.get(h)

    tm = min(ROW_TILE, s_len)
    ab3 = _mm(
        f"{tag}_gate_up", "nt", (s_len // tm, 2 * N_CHIPS, 1),
        h, (tm, D_MODEL), lambda i, j, k: (i, 0),
        g4, (None, None, F_SHARD, D_MODEL), lambda i, j, k: (j % N_CHIPS, base + j // N_CHIPS, 0, 0),
        _sds((2 * N_CHIPS, s_len, F_SHARD), BF16), (None, tm, F_SHARD), lambda i, j, k: (j, i, 0),
        (tm, F_SHARD),
    )

    def swiglu(ab):
        a = ab[:N_CHIPS].astype(F32)
        b = ab[N_CHIPS:].astype(F32)
        return a * _sigmoid(a) * b

    (s3,) = _rowwise(
        f"{tag}_swiglu", swiglu, [ab3], [], [_sds((N_CHIPS, s_len, F_SHARD), BF16)], [], 128,
        deps=down_src.mid(ab3) if down_src is not None else (),
    )
    g_down = down_src.get(s3) if down_src is not None else g4

    y = _mm(
        f"{tag}_down", "nn", (s_len // tm, 1, N_CHIPS),
        s3, (None, tm, F_SHARD), lambda i, j, k: (k, i, 0),
        g_down, (None, None, F_SHARD, D_MODEL), lambda i, j, k: (k, down_idx, 0, 0),
        _sds((s_len, D_MODEL), F32), (tm, D_MODEL), lambda i, j, k: (i, 0),
        (tm, D_MODEL),
    )

    (x_out,) = _rowwise(
        f"{tag}_residual", lambda x_, y_, g_: x_ + 0.5 * g_ * y_, [x, y], [gate], [_sds((s_len, D_MODEL), F32)], [], 256
    )
    return x_out, (g4, base, g_down, down_idx, h, ab3, s3, y)


def _normmod_bwd_call(name, dh_parts, x, dxo, gamma, sc, deps=()):
    s_len = x.shape[0]
    n_parts = len(dh_parts)

    def fn(*vals):
        dh = vals[0]
        for extra in vals[1:n_parts]:
            dh = dh + extra
        x_, dxo_, gamma_, sc_ = vals[n_parts:]
        xr, r = _rms(x_)
        n = xr * gamma_
        dn = dh * (1.0 + sc_)
        dx = dxo_ + _rms_bwd(dn * gamma_, xr, r)
        return dx, _sum0(dh * n), _sum0(dh), _sum0(dn * xr)

    vec = _sds((1, D_MODEL), F32)
    return _rowwise(
        name, fn, list(dh_parts) + [x, dxo], [gamma, sc], [_sds((s_len, D_MODEL), F32)], [vec, vec, vec], 256, deps=deps
    )


class _BwdHooks:
    def __init__(self, after_gate=None, after_swiglu=None, after_wdown=None, after_wgu=None, after_dh=None):
        def nothing(_):
            return []

        self.after_gate = after_gate or nothing
        self.after_swiglu = after_swiglu or nothing
        self.after_wdown = after_wdown or nothing
        self.after_wgu = after_wgu or nothing
        self.after_dh = after_dh or nothing


def _ffn_bwd(tag, dxo, x, gamma, sc, gate, saved, hooks, deps=()):
    g4, base, g_down, down_idx, h, ab3, s3, y = saved
    s_len = x.shape[0]
    vec = _sds((1, D_MODEL), F32)

    dy, dgate = _rowwise(
        f"{tag}_bwd_gate", lambda dxo_, y_, g_: (0.5 * g_ * dxo_, _sum0(0.5 * y_ * dxo_)),
        [dxo, y], [gate], [_sds((s_len, D_MODEL), BF16)], [vec], 256, deps=deps,
    )

    tm = min(ROW_TILE, s_len)
    ds3 = _mm(
        f"{tag}_bwd_ds", "nt", (s_len // tm, N_CHIPS, 1),
        dy, (tm, D_MODEL), lambda i, j, k: (i, 0),
        g_down, (None, None, F_SHARD, D_MODEL), lambda i, j, k: (j, down_idx, 0, 0),
        _sds((N_CHIPS, s_len, F_SHARD), BF16), (None, tm, F_SHARD), lambda i, j, k: (j, i, 0),
        (tm, F_SHARD),
    )

    def swiglu_bwd(ab, ds):
        a = ab[:N_CHIPS].astype(F32)
        b = ab[N_CHIPS:].astype(F32)
        ds = ds.astype(F32)
        sig = _sigmoid(a)
        da = ds * b * sig * (1.0 + a * (1.0 - sig))
        db = ds * a * sig
        return jnp.concatenate([da, db], axis=0)

    (dab3,) = _rowwise(
        f"{tag}_bwd_swiglu", swiglu_bwd, [ab3, ds3], [], [_sds((2 * N_CHIPS, s_len, F_SHARD), BF16)], [], 128
    )

    tk = tm
    gb_down = _mm(
        f"{tag}_bwd_wdown", "tn", (N_CHIPS, 1, s_len // tk),
        s3, (None, tk, F_SHARD), lambda i, j, k: (i, k, 0),
        dy, (tk, D_MODEL), lambda i, j, k: (k, 0),
        _sds((N_CHIPS, 1, F_SHARD, D_MODEL), BF16), (None, None, F_SHARD, D_MODEL), lambda i, j, k: (i, 0, 0, 0),
        (F_SHARD, D_MODEL), deps=hooks.after_swiglu(dab3),
    )
    gb_gu = _mm(
        f"{tag}_bwd_wgu", "tn", (2 * N_CHIPS, 1, s_len // tk),
        dab3, (None, tk, F_SHARD), lambda i, j, k: (i, k, 0),
        h, (tk, D_MODEL), lambda i, j, k: (k, 0),
        _sds((N_CHIPS, 2, F_SHARD, D_MODEL), BF16), (None, None, F_SHARD, D_MODEL),
        lambda i, j, k: (i % N_CHIPS, i // N_CHIPS, 0, 0),
        (F_SHARD, D_MODEL), deps=hooks.after_wdown(gb_down),
    )
    dh = _mm(
        f"{tag}_bwd_dh", "nn", (s_len // tm, 1, 2 * N_CHIPS),
        dab3, (None, tm, F_SHARD), lambda i, j, k: (k, i, 0),
        g4, (None, None, F_SHARD, D_MODEL), lambda i, j, k: (k % N_CHIPS, base + k // N_CHIPS, 0, 0),
        _sds((s_len, D_MODEL), F32), (tm, D_MODEL), lambda i, j, k: (i, 0),
        (tm, D_MODEL), deps=hooks.after_wgu(gb_gu),
    )
    dx, dsc, dsh, dgamma = _normmod_bwd_call(
        f"{tag}_bwd_normmod", [dh], x, dxo, gamma, sc, deps=hooks.after_dh(dh)
    )
    return dx, (gb_down, gb_gu), (dsh, dsc, dgate, dgamma)


def _bucket_map():
    qi = np.arange(WINDOW)[:, None]
    kj = np.arange(2 * WINDOW)[None, :]
    dist = qi + WINDOW - kj
    band = (dist >= 0) & (dist < WINDOW)
    max_exact = NUM_BUCKETS // 2
    n = np.maximum(dist, 0)
    large = []
    for dt in (np.float32, np.float64):
        nf = np.maximum(n, 1).astype(dt)
        val = np.log(nf / dt(max_exact)) / dt(math.log(MAX_DISTANCE / max_exact)) * dt(NUM_BUCKETS - max_exact)
        large.append(np.minimum(max_exact + val.astype(np.int32), NUM_BUCKETS - 1))
    assert np.array_equal(large[0], large[1])
    bucket = np.where(n < max_exact, n, large[0])
    return np.where(band, bucket, -1).astype(np.int32).reshape(1, -1)


def _bias_expand(rel_bias_t, bkt):
    n = bkt.shape[1]

    def body(rb_ref, bkt_ref, o_ref):
        onehot = (lax.broadcasted_iota(jnp.int32, (NUM_BUCKETS, n), 0) == bkt_ref[...]).astype(F32)
        o_ref[...] = lax.dot_general(
            rb_ref[...], onehot, _DN["nn"], precision=lax.Precision.HIGHEST, preferred_element_type=F32
        )

    return pl.pallas_call(
        body, name="bias_expand", out_shape=_sds((SWA_HEADS, n), F32), compiler_params=_cparams()
    )(rel_bias_t, bkt)


def _bias_reduce(dbias_flat, bkt):
    n = bkt.shape[1]

    def body(db_ref, bkt_ref, o_ref):
        onehot = (lax.broadcasted_iota(jnp.int32, (NUM_BUCKETS, n), 0) == bkt_ref[...]).astype(F32)
        o_ref[...] = lax.dot_general(
            db_ref[...], onehot, _DN["nt"], precision=lax.Precision.HIGHEST, preferred_element_type=F32
        )

    return pl.pallas_call(
        body, name="bias_reduce", out_shape=_sds((SWA_HEADS, NUM_BUCKETS), F32), compiler_params=_cparams()
    )(dbias_flat, bkt)


_SWA_SCALE = SWA_HEAD_DIM ** -0.5
_NEG = -1e30


def _swa_in_specs():
    w = WINDOW
    n_q = SWA_HEADS * SWA_HEAD_DIM
    n_kv = 2 * SWA_KV_HEADS * SWA_HEAD_DIM
    prev = lambda n: jnp.maximum(n - 1, 0)
    return [
        pl.BlockSpec((w, n_q), lambda n: (n, 0)),
        pl.BlockSpec((w, n_kv), lambda n: (prev(n), n_q // n_kv)),
        pl.BlockSpec((w, n_kv), lambda n: (n, n_q // n_kv)),
        pl.BlockSpec((SWA_HEADS, w, 2 * w), lambda n: (0, 0, 0)),
        pl.BlockSpec((SWA_HEADS, w, 1), lambda n: (0, 0, 0)),
    ]


def _head_cols(v, first, count):
    hd = SWA_HEAD_DIM
    return jnp.stack([v[:, (first + i) * hd:(first + i + 1) * hd] for i in range(count)])


def _swa_heads(q_ref, kvp_ref, kvc_ref):
    g, w, hd = SWA_GROUP, WINDOW, SWA_HEAD_DIM
    q = q_ref[...].astype(F32)
    kv = jnp.concatenate([kvp_ref[...], kvc_ref[...]], axis=0).astype(F32)
    heads = []
    for j in range(SWA_KV_HEADS):
        qj = _head_cols(q, g * j, g).reshape(g * w, hd)
        heads.append((qj, _head_cols(kv, j, 1)[0], _head_cols(kv, SWA_KV_HEADS + j, 1)[0]))
    return heads


def _swa_scores(q, kk, bias, n):
    g, w = SWA_GROUP, WINDOW
    s = (_dot(q, kk, "nt") * _SWA_SCALE).reshape(g, w, 2 * w) + bias
    qi = lax.broadcasted_iota(jnp.int32, (w, 2 * w), 0)
    kj = lax.broadcasted_iota(jnp.int32, (w, 2 * w), 1)
    dist = qi + w - kj
    valid = ((dist >= 0) & (dist < w) & ((n > 0) | (kj >= w)))[None]
    return jnp.where(valid, s, _NEG), valid


def _swa_fwd(swa2, bias, sinkb):
    s_len = swa2.shape[0]
    g, w, hd = SWA_GROUP, WINDOW, SWA_HEAD_DIM

    def body(q_ref, kvp_ref, kvc_ref, bias_ref, sink_ref, o_ref, lse_ref):
        n = pl.program_id(0)
        outs = []
        for j, (q, kk, vv) in enumerate(_swa_heads(q_ref, kvp_ref, kvc_ref)):
            hs = slice(g * j, g * (j + 1))
            s, valid = _swa_scores(q, kk, bias_ref[hs], n)
            sink = sink_ref[hs]
            m = jnp.maximum(jnp.max(s, axis=-1, keepdims=True), sink)
            p = jnp.where(valid, jnp.exp(s - m), 0.0)
            l = jnp.sum(p, axis=-1, keepdims=True) + jnp.exp(sink - m)
            o = _dot(p.reshape(g * w, 2 * w), vv, "nn").reshape(g, w, hd) / l
            outs += [o[i] for i in range(g)]
            lse_ref[hs] = m + jnp.log(l)
        o_ref[...] = jnp.concatenate(outs, axis=-1).astype(o_ref.dtype)

    n_q = SWA_HEADS * hd
    return pl.pallas_call(
        body,
        name="swa_fwd",
        grid=(s_len // w,),
        in_specs=_swa_in_specs(),
        out_specs=[
            pl.BlockSpec((w, n_q), lambda n: (n, 0)),
            pl.BlockSpec((SWA_HEADS, w, 1), lambda n: (0, n, 0)),
        ],
        out_shape=[_sds((s_len, n_q), BF16), _sds((SWA_HEADS, s_len, 1), F32)],
        compiler_params=_cparams(("parallel",)),
    )(swa2, swa2, swa2, bias, sinkb)


def _swa_bwd(swa2, bias, sinkb, cat, dcat, lse):
    s_len = swa2.shape[0]
    g, w, hd = SWA_GROUP, WINDOW, SWA_HEAD_DIM

    def body(q_ref, kvp_ref, kvc_ref, bias_ref, sink_ref, o_ref, do_ref, lse_ref,
             dq_ref, dkva_ref, dkvb_ref, dbias_ref, dsink_ref):
        n = pl.program_id(0)

        @pl.when(n == 0)
        def _():
            dbias_ref[...] = jnp.zeros_like(dbias_ref)
            dsink_ref[...] = jnp.zeros_like(dsink_ref)

        o_all = o_ref[...].astype(F32)
        do_all = do_ref[...].astype(F32)
        dqs, dks, dvs = [], [], []
        for j, (q, kk, vv) in enumerate(_swa_heads(q_ref, kvp_ref, kvc_ref)):
            hs = slice(g * j, g * (j + 1))
            s, valid = _swa_scores(q, kk, bias_ref[hs], n)
            lse_v = lse_ref[hs]
            p = jnp.where(valid, jnp.exp(s - lse_v), 0.0)
            do = _head_cols(do_all, g * j, g)
            delta = jnp.sum(do * _head_cols(o_all, g * j, g), axis=-1, keepdims=True)
            do2 = do.reshape(g * w, hd)
            dp = _dot(do2, vv, "nt").reshape(g, w, 2 * w)
            ds = p * (dp - delta)
            dbias_ref[hs] += ds
            dsink_ref[hs] -= jnp.exp(sink_ref[hs] - lse_v) * delta
            ds2 = ds.reshape(g * w, 2 * w)
            dq = (_dot(ds2, kk, "nn") * _SWA_SCALE).reshape(g, w, hd)
            dqs += [dq[i] for i in range(g)]
            dks.append(_dot(ds2, q, "tn") * _SWA_SCALE)
            dvs.append(_dot(p.reshape(g * w, 2 * w), do2, "tn"))
        dq_ref[...] = jnp.concatenate(dqs, axis=-1).astype(dq_ref.dtype)
        dkv = jnp.concatenate(dks + dvs, axis=-1)
        dkvb_ref[...] = dkv[:w]
        dkva_ref[...] = dkv[w:]

    n_q = SWA_HEADS * hd
    n_kv = 2 * SWA_KV_HEADS * hd
    q_spec = pl.BlockSpec((w, n_q), lambda n: (n, 0))
    kv_spec = pl.BlockSpec((w, n_kv), lambda n: (n, 0))
    return pl.pallas_call(
        body,
        name="swa_bwd",
        grid=(s_len // w,),
        in_specs=_swa_in_specs() + [q_spec, q_spec, pl.BlockSpec((SWA_HEADS, w, 1), lambda n: (0, n, 0))],
        out_specs=[
            q_spec, kv_spec, kv_spec,
            pl.BlockSpec((SWA_HEADS, w, 2 * w), lambda n: (0, 0, 0)),
            pl.BlockSpec((SWA_HEADS, w, 1), lambda n: (0, 0, 0)),
        ],
        out_shape=[
            _sds((s_len, n_q), BF16), _sds((s_len, n_kv), F32), _sds((s_len, n_kv), F32),
            _sds((SWA_HEADS, w, 2 * w), F32), _sds((SWA_HEADS, w, 1), F32),
        ],
        compiler_params=_cparams(("arbitrary",)),
    )(swa2, swa2, swa2, bias, sinkb, cat, dcat, lse)


_MLA_SCALE = MLA_QK ** -0.5
_MLA_TQ = 256


def _mla_mask(i, tq, s_len):
    row = i * tq + lax.broadcasted_iota(jnp.int32, (tq, s_len), 0)
    col = lax.broadcasted_iota(jnp.int32, (tq, s_len), 1)
    return col <= row


def _mla_fwd(q4, k4, v4):
    s_len = q4.shape[1]
    tq = min(_MLA_TQ, s_len)

    def body(q_ref, k_ref, v_ref, o_ref, lse_ref):
        mask = _mla_mask(pl.program_id(1), tq, s_len)
        s = jnp.where(mask, _dot(q_ref[...], k_ref[...], "nt") * _MLA_SCALE, _NEG)
        m = jnp.max(s, axis=-1, keepdims=True)
        p = jnp.exp(s - m)
        l = jnp.sum(p, axis=-1, keepdims=True)
        o_ref[...] = (_dot(p, v_ref[...], "nn") / l).astype(o_ref.dtype)
        lse_ref[...] = m + jnp.log(l)

    return pl.pallas_call(
        body,
        name="mla_fwd",
        grid=(MLA_HEADS, s_len // tq),
        in_specs=[
            pl.BlockSpec((None, tq, MLA_QK), lambda h, i: (h, i, 0)),
            pl.BlockSpec((None, s_len, MLA_QK), lambda h, i: (h, 0, 0)),
            pl.BlockSpec((None, s_len, MLA_V), lambda h, i: (h, 0, 0)),
        ],
        out_specs=[
            pl.BlockSpec((tq, MLA_V), lambda h, i: (i, h)),
            pl.BlockSpec((None, tq, 1), lambda h, i: (h, i, 0)),
        ],
        out_shape=[_sds((s_len, MLA_HEADS * MLA_V), BF16), _sds((MLA_HEADS, s_len, 1), F32)],
        compiler_params=_cparams(("parallel", "parallel")),
    )(q4, k4, v4)


def _mla_bwd(q4, k4, v4, cat, dcat, lse):
    s_len = q4.shape[1]
    tq = min(_MLA_TQ, s_len)
    first = SWA_HEADS * SWA_HEAD_DIM // MLA_V

    def body(q_ref, k_ref, v_ref, o_ref, do_ref, lse_ref, dq_ref, dk_ref, dv_ref):
        i = pl.program_id(1)

        @pl.when(i == 0)
        def _():
            dk_ref[...] = jnp.zeros_like(dk_ref)
            dv_ref[...] = jnp.zeros_like(dv_ref)

        mask = _mla_mask(i, tq, s_len)
        q, k, v = q_ref[...], k_ref[...], v_ref[...]
        s = jnp.where(mask, _dot(q, k, "nt") * _MLA_SCALE, _NEG)
        p = jnp.where(mask, jnp.exp(s - lse_ref[...]), 0.0)
        do = do_ref[...]
        delta = jnp.sum(do.astype(F32) * o_ref[...].astype(F32), axis=-1, keepdims=True)
        dp = _dot(do, v, "nt")
        ds = (p * (dp - delta) * _MLA_SCALE).astype(BF16)
        dq_ref[...] = _dot(ds, k, "nn")
        dk_ref[...] += _dot(ds, q, "tn")
        dv_ref[...] += _dot(p, do, "tn")

    return pl.pallas_call(
        body,
        name="mla_bwd",
        grid=(MLA_HEADS, s_len // tq),
        in_specs=[
            pl.BlockSpec((None, tq, MLA_QK), lambda h, i: (h, i, 0)),
            pl.BlockSpec((None, s_len, MLA_QK), lambda h, i: (h, 0, 0)),
            pl.BlockSpec((None, s_len, MLA_V), lambda h, i: (h, 0, 0)),
            pl.BlockSpec((tq, MLA_V), lambda h, i: (i, first + h)),
            pl.BlockSpec((tq, MLA_V), lambda h, i: (i, first + h)),
            pl.BlockSpec((None, tq, 1), lambda h, i: (h, i, 0)),
        ],
        out_specs=[
            pl.BlockSpec((None, tq, MLA_QK), lambda h, i: (h, i, 0)),
            pl.BlockSpec((None, s_len, MLA_QK), lambda h, i: (h, 0, 0)),
            pl.BlockSpec((None, s_len, MLA_V), lambda h, i: (h, 0, 0)),
        ],
        out_shape=[
            _sds((MLA_HEADS, s_len, MLA_QK), F32),
            _sds((MLA_HEADS, s_len, MLA_QK), F32),
            _sds((MLA_HEADS, s_len, MLA_V), F32),
        ],
        compiler_params=_cparams(("parallel", "arbitrary")),
    )(q4, k4, v4, cat, dcat, lse)


def _mla_split(pm):
    q_lat = pm[:, :MLA_Q_RANK]
    kv_lat = pm[:, MLA_Q_RANK:MLA_Q_RANK + MLA_KV_RANK]
    kr = pm[:, MLA_Q_RANK + MLA_KV_RANK:MLA_Q_RANK + MLA_KV_RANK + MLA_ROPE]
    kr_sw = pm[:, MLA_Q_RANK + MLA_KV_RANK + MLA_ROPE:]
    return q_lat, kv_lat, kr, kr_sw


def _mla_proj(pm, cos2, sin2, q_norm, kv_norm, wq_ext, wkv):
    s_len = pm.shape[0]

    def fn(pm_, cos_, sin_, qg, kvg, wq, wk):
        q_lat, kv_lat, kr, kr_sw = _mla_split(pm_)
        nq = (_rms(q_lat)[0] * qg).astype(BF16)
        nkv = (_rms(kv_lat)[0] * kvg).astype(BF16)
        k_rot = kr * cos_ + kr_sw * sin_
        qs, ks, vs = [], [], []
        for h in range(MLA_HEADS):
            qe = _dot(nq, wq[h], "nt")
            q_rot = qe[:, MLA_NOPE:MLA_QK] * cos_ + qe[:, MLA_QK:] * sin_
            qs.append(jnp.concatenate([qe[:, :MLA_NOPE], q_rot], axis=-1))
            kve = _dot(nkv, wk[h], "nt")
            ks.append(jnp.concatenate([kve[:, :MLA_NOPE], k_rot], axis=-1))
            vs.append(kve[:, MLA_NOPE:])
        return jnp.stack(qs), jnp.stack(ks), jnp.stack(vs)

    return _rowwise(
        "mla_proj", fn, [pm, cos2, sin2], [q_norm, kv_norm, wq_ext, wkv],
        [_sds((MLA_HEADS, s_len, MLA_QK), BF16), _sds((MLA_HEADS, s_len, MLA_QK), BF16),
         _sds((MLA_HEADS, s_len, MLA_V), BF16)], [], 256,
    )


def _mla_proj_bwd(pm, cos2, sin2, dq4, dk4, dv4, q_norm, kv_norm, wq_ext, wkv):
    s_len = pm.shape[0]

    def fn(pm_, cos_, sin_, dq, dk, dv, qg, kvg, wq, wk):
        q_lat, kv_lat, _, _ = _mla_split(pm_)
        xq, rq = _rms(q_lat)
        xkv, rkv = _rms(kv_lat)
        nq = (xq * qg).astype(BF16)
        nkv = (xkv * kvg).astype(BF16)
        dnq = jnp.zeros_like(q_lat)
        dnkv = jnp.zeros_like(kv_lat)
        dkr = jnp.zeros_like(cos_)
        dwq, dwk = [], []
        for h in range(MLA_HEADS):
            dy = dq[h][:, MLA_NOPE:]
            dqe = jnp.concatenate([dq[h][:, :MLA_NOPE], dy * cos_, dy * sin_], axis=-1).astype(BF16)
            dnq = dnq + _dot(dqe, wq[h], "nn")
            dwq.append(_dot(dqe, nq, "tn"))
            dkve = jnp.concatenate([dk[h][:, :MLA_NOPE], dv[h]], axis=-1).astype(BF16)
            dnkv = dnkv + _dot(dkve, wk[h], "nn")
            dwk.append(_dot(dkve, nkv, "tn"))
            dkr = dkr + dk[h][:, MLA_NOPE:]
        dq_lat = _rms_bwd(dnq * qg, xq, rq)
        dkv_lat = _rms_bwd(dnkv * kvg, xkv, rkv)
        dpm = jnp.concatenate([dq_lat, dkv_lat, dkr * cos_, dkr * sin_], axis=-1)
        return dpm, _sum0(dnq * xq), _sum0(dnkv * xkv), jnp.stack(dwq), jnp.stack(dwk)

    return _rowwise(
        "mla_proj_bwd", fn, [pm, cos2, sin2, dq4, dk4, dv4], [q_norm, kv_norm, wq_ext, wkv],
        [_sds((s_len, N_MLA_COLS), BF16)],
        [_sds((1, MLA_Q_RANK), F32), _sds((1, MLA_KV_RANK), F32),
         _sds(wq_ext.shape, F32), _sds(wkv.shape, F32)], 256,
    )


def _rope_tables(s_len):
    inv = ROPE_THETA ** (-jnp.arange(0, MLA_ROPE, 2, dtype=F32) / MLA_ROPE)
    ang = jnp.arange(s_len, dtype=F32)[:, None] * inv[None, :]
    cos, sin = jnp.cos(ang), jnp.sin(ang)
    return jnp.concatenate([cos, cos], axis=-1), jnp.concatenate([-sin, sin], axis=-1)


def _mix_weights(g_mix):
    rest = g_mix[:, 0]
    w_in_t = rest[:, O_IN:O_IN + R_IN].reshape(D_IN, D_MODEL)
    w_o = rest[:, O_O:O_O + R_O].reshape(D_MODEL, D_MODEL)
    w_uq_t = rest[:, O_UQ:O_UQ + R_UQ].reshape(MLA_HEADS, MLA_QK, MLA_Q_RANK)
    w_ukv_t = rest[:, O_UKV:O_UKV + R_UKV].reshape(MLA_HEADS, MLA_NOPE + MLA_V, MLA_KV_RANK)
    half = MLA_ROPE // 2
    w_swa = w_in_t[:N_SWA_COLS]
    w_mla = jnp.concatenate([w_in_t[N_SWA_COLS:], w_in_t[D_IN - half:], w_in_t[D_IN - MLA_ROPE:D_IN - half]], axis=0)
    wq_ext = jnp.concatenate([w_uq_t, w_uq_t[:, MLA_QK - half:], w_uq_t[:, MLA_NOPE:MLA_QK - half]], axis=1)
    return w_swa, w_mla, w_o, wq_ext, w_ukv_t


def _mix_fwd(x, gamma, sh, sc, gate, mix_src, q_norm, kv_norm, bias, sinkb, cos2, sin2):
    s_len = x.shape[0]
    tm = min(ROW_TILE, s_len)

    def normmod(x_, gamma_, sc_, sh_):
        return _rms(x_)[0] * gamma_ * (1.0 + sc_) + sh_

    deps = mix_src.mid(x)
    (h,) = _rowwise("mix_normmod", normmod, [x], [gamma, sc, sh], [_sds((s_len, D_MODEL), BF16)], [], 256, deps=deps)
    weights = _mix_weights(mix_src.get(h))
    w_swa, w_mla, w_o, wq_ext, wkv = weights
    swa2 = _mm(
        "mix_proj_swa", "nt", (s_len // tm, 1, 1),
        h, (tm, D_MODEL), lambda i, j, k: (i, 0),
        w_swa, (N_SWA_COLS, D_MODEL), lambda i, j, k: (0, 0),
        _sds((s_len, N_SWA_COLS), BF16), (tm, N_SWA_COLS), lambda i, j, k: (i, 0),
        (tm, N_SWA_COLS),
    )
    pm = _mm(
        "mix_proj_mla", "nt", (s_len // tm, 1, 1),
        h, (tm, D_MODEL), lambda i, j, k: (i, 0),
        w_mla, (N_MLA_COLS, D_MODEL), lambda i, j, k: (0, 0),
        _sds((s_len, N_MLA_COLS), F32), (tm, N_MLA_COLS), lambda i, j, k: (i, 0),
        (tm, N_MLA_COLS),
    )
    q4, k4, v4 = _mla_proj(pm, cos2, sin2, q_norm, kv_norm, wq_ext, wkv)
    oa, lse_a = _swa_fwd(swa2, bias, sinkb)
    ob, lse_b = _mla_fwd(q4, k4, v4)
    cat = jnp.concatenate([oa, ob], axis=1)
    z = _mm(
        "mix_out", "nn", (s_len // tm, 1, 1),
        cat, (tm, D_MODEL), lambda i, j, k: (i, 0),
        w_o, (D_MODEL, D_MODEL), lambda i, j, k: (0, 0),
        _sds((s_len, D_MODEL), F32), (tm, D_MODEL), lambda i, j, k: (i, 0),
        (tm, D_MODEL),
    )
    (x_out,) = _rowwise(
        "mix_residual", lambda x_, z_, g_: x_ + g_ * z_, [x, z], [gate], [_sds((s_len, D_MODEL), F32)], [], 256
    )
    return x_out, (weights, h, swa2, pm, q4, k4, v4, cat, lse_a, lse_b, z)


def _mix_bwd(dxo, x, gamma, sc, gate, q_norm, kv_norm, bias, sinkb, cos2, sin2, saved, hooks):
    weights, h, swa2, pm, q4, k4, v4, cat, lse_a, lse_b, z = saved
    w_swa, w_mla, w_o, wq_ext, wkv = weights
    s_len = x.shape[0]
    tm = tk = min(ROW_TILE, s_len)
    vec = _sds((1, D_MODEL), F32)
    n_proj = N_SWA_COLS + N_MLA_COLS
    half_d = D_MODEL // 2

    dz, dgate = _rowwise(
        "mix_bwd_gate", lambda dxo_, z_, g_: (g_ * dxo_, _sum0(z_ * dxo_)),
        [dxo, z], [gate], [_sds((s_len, D_MODEL), BF16)], [vec], 256,
    )
    dw_o = _mm(
        "mix_bwd_wo", "tn", (2, 1, s_len // tk),
        cat, (tk, half_d), lambda i, j, k: (k, i),
        dz, (tk, D_MODEL), lambda i, j, k: (k, 0),
        _sds((D_MODEL, D_MODEL), F32), (half_d, D_MODEL), lambda i, j, k: (i, 0),
        (half_d, D_MODEL),
    )
    dcat = _mm(
        "mix_bwd_dcat", "nt", (s_len // tm, 1, 1),
        dz, (tm, D_MODEL), lambda i, j, k: (i, 0),
        w_o, (D_MODEL, D_MODEL), lambda i, j, k: (0, 0),
        _sds((s_len, D_MODEL), BF16), (tm, D_MODEL), lambda i, j, k: (i, 0),
        (tm, D_MODEL), deps=hooks.after_gate(dz),
    )
    dq_a, dkva, dkvb, dbias, dsink = _swa_bwd(swa2, bias, sinkb, cat, dcat, lse_a)
    dq4, dk4, dv4 = _mla_bwd(q4, k4, v4, cat, dcat, lse_b)
    dpm, dq_norm, dkv_norm, dwq_ext, dwkv = _mla_proj_bwd(pm, cos2, sin2, dq4, dk4, dv4, q_norm, kv_norm, wq_ext, wkv)

    dkv = dkva + jnp.concatenate([dkvb[WINDOW:], jnp.zeros_like(dkvb[:WINDOW])], axis=0)
    dproj = jnp.concatenate([dq_a, dkv.astype(BF16), dpm], axis=1)
    w_proj = jnp.concatenate([w_swa, w_mla], axis=0)

    dw_proj = _mm(
        "mix_bwd_win", "tn", (n_proj // 256, 1, s_len // tk),
        dproj, (tk, 256), lambda i, j, k: (k, i),
        h, (tk, D_MODEL), lambda i, j, k: (k, 0),
        _sds((n_proj, D_MODEL), F32), (256, D_MODEL), lambda i, j, k: (i, 0),
        (256, D_MODEL),
    )
    dw_swa, dw_mla = dw_proj[:N_SWA_COLS], dw_proj[N_SWA_COLS:]
    dh = _mm(
        "mix_bwd_dh", "nn", (s_len // tm, 1, 1),
        dproj, (tm, n_proj), lambda i, j, k: (i, 0),
        w_proj, (n_proj, D_MODEL), lambda i, j, k: (0, 0),
        _sds((s_len, D_MODEL), F32), (tm, D_MODEL), lambda i, j, k: (i, 0),
        (tm, D_MODEL),
    )
    dx, dsc, dsh, dgamma = _normmod_bwd_call("mix_bwd_normmod", [dh], x, dxo, gamma, sc)

    half = MLA_ROPE // 2
    n_mla_in = D_IN - N_SWA_COLS
    dw_mla_base = dw_mla[:n_mla_in]
    dw_mla_base = dw_mla_base.at[n_mla_in - half:].add(dw_mla[n_mla_in:n_mla_in + half])
    dw_mla_base = dw_mla_base.at[n_mla_in - MLA_ROPE:n_mla_in - half].add(dw_mla[n_mla_in + half:])
    dw_in_t = jnp.concatenate([dw_swa, dw_mla_base], axis=0)
    dw_uq_t = dwq_ext[:, :MLA_QK]
    dw_uq_t = dw_uq_t.at[:, MLA_QK - half:].add(dwq_ext[:, MLA_QK:MLA_QK + half])
    dw_uq_t = dw_uq_t.at[:, MLA_NOPE:MLA_QK - half].add(dwq_ext[:, MLA_QK + half:])
    rest = jnp.concatenate(
        [
            dw_in_t.reshape(N_CHIPS, R_IN, D_MODEL),
            dw_o.reshape(N_CHIPS, R_O, D_MODEL),
            dw_uq_t.reshape(N_CHIPS, R_UQ, D_MODEL),
            dwkv.reshape(N_CHIPS, R_UKV, D_MODEL),
            jnp.zeros((N_CHIPS, F_SHARD - O_PAD, D_MODEL), F32),
        ],
        axis=1,
    ).astype(BF16)
    return dx, rest, (dsh, dsc, dgate, dgamma), dq_norm, dkv_norm, dbias, dsink


def _device_step(x, target, mod, gu1_src, down1_src, mix_src, ffn2_src, norms, q_norm, kv_norm, sinks, rel_bias,
                 hooks2=None, hooks_mix=None, mix_done=None, hooks1=None):
    s_len = x.shape[0]
    sh1, sc1, g1, sh2, sc2, g2, sh3, sc3, g3 = [mod[:, i * D_MODEL:(i + 1) * D_MODEL] for i in range(N_MOD)]
    n1, n2, n3, nf = norms
    bkt = jnp.asarray(_bucket_map())
    bias = _bias_expand(rel_bias.T, bkt).reshape(SWA_HEADS, WINDOW, 2 * WINDOW)
    sinkb = jnp.broadcast_to(sinks.reshape(SWA_HEADS, 1, 1), (SWA_HEADS, WINDOW, 1))
    cos2, sin2 = _rope_tables(s_len)

    x1, saved1 = _ffn_fwd("ffn1", x, n1, sh1, sc1, g1, gu1_src, 0, down1_src, 0, sh1)
    x2, saved2 = _mix_fwd(x1, n2, sh2, sc2, g2, mix_src, q_norm, kv_norm, bias, sinkb, cos2, sin2)
    x3, saved3 = _ffn_fwd("ffn2", x2, n3, sh3, sc3, g3, ffn2_src, 0, None, 2, x2)

    def head(x_, t_, g_):
        xr, r = _rms(x_)
        err = xr * g_ - t_
        dy = err * (1.0 / D_MODEL)
        return _rms_bwd(dy * g_, xr, r), _sum0(err * err), _sum0(dy * xr)

    vec = _sds((1, D_MODEL), F32)
    dx3, sq, dnf = _rowwise("loss_head", head, [x3, target], [nf], [_sds((s_len, D_MODEL), F32)], [vec, vec], 256)
    loss_part = (0.5 / D_MODEL) * jnp.sum(sq)

    dx2, gb2, (dsh3, dsc3, dg3, dn3) = _ffn_bwd("ffn2", dx3, x2, n3, sc3, g3, saved3, hooks2 or _BwdHooks())
    dx1, rest, (dsh2, dsc2, dg2, dn2), dq_norm, dkv_norm, dbias, dsink = _mix_bwd(
        dx2, x1, n2, sc2, g2, q_norm, kv_norm, bias, sinkb, cos2, sin2, saved2, hooks_mix or _BwdHooks()
    )
    rest = rest[:, None]
    deps1 = mix_done(dx1, rest) if mix_done is not None else []
    dx0, gb1, (dsh1, dsc1, dg1, dn1) = _ffn_bwd(
        "ffn1", dx1, x, n1, sc1, g1, saved1, hooks1 or _BwdHooks(), deps=deps1
    )

    dmod = jnp.concatenate([dsh1, dsc1, dg1, dsh2, dsc2, dg2, dsh3, dsc3, dg3], axis=-1)
    drel = _bias_reduce(dbias.reshape(SWA_HEADS, -1), bkt).T
    dsinks = jnp.sum(dsink, axis=(1, 2)).reshape(1, SWA_HEADS)
    small = (dn1, dn2, dn3, dnf, dq_norm, dkv_norm, dsinks, drel)
    return loss_part, dx0, (gb1, gb2, rest), dmod, small


_MESH = pl.DeviceIdType.MESH


def _place():
    return lax.axis_index("x"), lax.axis_index("y"), lax.axis_index("c")


def _other_chips(x, y):
    return [(1 - x, y), (x, 1 - y), (1 - x, 1 - y)]


def _allgather_small(name, blk):
    m_per, n = blk.shape

    def body(x_ref, out_ref, send_sems, recv_sems, local_sem):
        x, y, c = _place()
        me, sibling = (x, y, c), (x, y, 1 - c)
        chips = _other_chips(x, y)

        def rows(px, py, pc):
            return out_ref.at[pl.ds((4 * px + 2 * py + pc) * m_per, m_per), :]

        def copy(k, block, to, src=None):
            return pltpu.make_async_remote_copy(
                src_ref=rows(*block) if src is None else src, dst_ref=rows(*block),
                send_sem=send_sems.at[k], recv_sem=recv_sems.at[k], device_id=to, device_id_type=_MESH,
            )

        mine = pltpu.make_async_copy(x_ref, rows(*me), local_sem)
        mine.start()
        first = [copy(0, me, sibling, src=x_ref)]
        first += [copy(1 + j, me, (*chip, c), src=x_ref) for j, chip in enumerate(chips)]
        for cp in first:
            cp.start()
        passed = [copy(4 + j, (*chip, c), sibling) for j, chip in enumerate(chips)]
        for j, chip in enumerate(chips):
            copy(1 + j, (*chip, c), me).wait_recv()
            passed[j].start()
        copy(0, sibling, me).wait_recv()
        for j, chip in enumerate(chips):
            copy(4 + j, (*chip, 1 - c), me).wait_recv()
        for cp in first + passed:
            cp.wait_send()
        mine.wait()

    return pl.pallas_call(
        body,
        name=name,
        out_shape=_sds((N_DEV * m_per, n), blk.dtype),
        in_specs=[pl.BlockSpec(memory_space=pltpu.VMEM)],
        out_specs=pl.BlockSpec(memory_space=pltpu.VMEM),
        scratch_shapes=[pltpu.SemaphoreType.DMA((7,)), pltpu.SemaphoreType.DMA((7,)), pltpu.SemaphoreType.DMA],
    )(blk)


def _allgather_weights(name, own):
    n = own.shape[0]

    def body(own_ref, g_ref, token, send_sems, recv_sems, local_sem):
        x, y, c = _place()
        sibling = (x, y, 1 - c)
        chips = _other_chips(x, y)
        mine = pltpu.make_async_copy(own_ref, g_ref.at[2 * x + y], local_sem)
        mine.start()
        for j, chip in enumerate(chips):
            for cp in _gather_sends(n, own_ref, g_ref, send_sems.at[j], recv_sems.at[j], x, y, c, chip):
                cp.start()
        for j, chip in enumerate(chips):
            _gather_all(own_ref, g_ref, send_sems.at[j], recv_sems.at[j], chip, c, c).wait_recv()
            for cp in _gather_forwards(n, g_ref, send_sems.at[3 + j], recv_sems.at[3 + j], chip, c, sibling):
                cp.start()
        for j, chip in enumerate(chips):
            _gather_all(own_ref, g_ref, send_sems.at[3 + j], recv_sems.at[3 + j], chip, 1 - c, c).wait_recv()
        for j, chip in enumerate(chips):
            _gather_all(own_ref, g_ref, send_sems.at[j], recv_sems.at[j], chip, c, c).wait_send()
            _gather_all(own_ref, g_ref, send_sems.at[3 + j], recv_sems.at[3 + j], chip, c, c).wait_send()
        mine.wait()
        token[...] = jnp.zeros_like(token)

    return pl.pallas_call(
        body,
        name=name,
        out_shape=[_sds((N_CHIPS,) + own.shape, own.dtype), _sds((8, 128), F32)],
        in_specs=[pl.BlockSpec(memory_space=pl.ANY)],
        out_specs=[pl.BlockSpec(memory_space=pl.ANY), pl.BlockSpec(memory_space=pltpu.VMEM)],
        scratch_shapes=[pltpu.SemaphoreType.DMA((6,)), pltpu.SemaphoreType.DMA((6,)), pltpu.SemaphoreType.DMA],
    )(own)


def _gather_sends(n, own_ref, g_ref, send_sem, recv_sem, x, y, c, chip):
    return [
        pltpu.make_async_remote_copy(
            src_ref=own_ref.at[p, pl.ds(c * HALF, HALF), :], dst_ref=g_ref.at[2 * x + y, p, pl.ds(c * HALF, HALF), :],
            send_sem=send_sem, recv_sem=recv_sem, device_id=(*chip, c), device_id_type=_MESH,
        )
        for p in range(n)
    ]


def _gather_forwards(n, g_ref, send_sem, recv_sem, chip, c, sibling):
    return [
        pltpu.make_async_remote_copy(
            src_ref=g_ref.at[2 * chip[0] + chip[1], p, pl.ds(c * HALF, HALF), :],
            dst_ref=g_ref.at[2 * chip[0] + chip[1], p, pl.ds(c * HALF, HALF), :],
            send_sem=send_sem, recv_sem=recv_sem, device_id=sibling, device_id_type=_MESH,
        )
        for p in range(n)
    ]


def _gather_all(own_ref, g_ref, send_sem, recv_sem, chip, half, c):
    return pltpu.make_async_remote_copy(
        src_ref=own_ref.at[:, pl.ds(c * HALF, HALF), :],
        dst_ref=g_ref.at[2 * chip[0] + chip[1], :, pl.ds(half * HALF, HALF), :],
        send_sem=send_sem, recv_sem=recv_sem, device_id=(*chip, c), device_id_type=_MESH,
    )


_HBM_SPEC = pl.BlockSpec(memory_space=pltpu.HBM)
_SEM_SPEC = pl.BlockSpec(memory_space=pltpu.SEMAPHORE)
_ANY_SPEC = pl.BlockSpec(memory_space=pl.ANY)
_VMEM_SPEC = pl.BlockSpec(memory_space=pltpu.VMEM)
_SPLIT_PARAMS = pltpu.CompilerParams(has_side_effects=pltpu.SideEffectType.DATAFLOW_SIDE_EFFECTING)
_TOKEN = jax.ShapeDtypeStruct((8, 128), F32)


def _in_hbm(a):
    return pltpu.with_memory_space_constraint(a, pltpu.HBM)


class _GatherBehind:
    def __init__(self, tag, own, then=None):
        self.tag, self.n, self.own, self.then = tag, own.shape[0], own, then

    def start(self, after):
        tag, own, n = self.tag, self.own, self.n
        x, y, _ = _place()
        g_init = lax.dynamic_update_slice(
            lax.empty((N_CHIPS,) + own.shape, own.dtype), own[None], (2 * x + y, 0, 0, 0)
        )

        def body(own_ref, g_ref, *rest):
            send_sems, recv_sems, _, _, token = rest[len(after):]
            x, y, c = _place()
            for j, chip in enumerate(_other_chips(x, y)):
                for cp in _gather_sends(n, own_ref, g_ref, send_sems.at[j], recv_sems.at[j], x, y, c, chip):
                    cp.start()
            token[...] = jnp.zeros_like(token)

        self.send1, self.recv1, self.own, self.g, self.token = pl.pallas_call(
            body,
            name=f"{tag}_start",
            out_shape=(
                pltpu.SemaphoreType.DMA((3,)), pltpu.SemaphoreType.DMA((3,)),
                pltpu.HBM(own.shape, own.dtype), pltpu.HBM(g_init.shape, g_init.dtype), _TOKEN,
            ),
            in_specs=(_HBM_SPEC, _HBM_SPEC) + (_ANY_SPEC,) * len(after),
            out_specs=(_SEM_SPEC, _SEM_SPEC, _HBM_SPEC, _HBM_SPEC, _VMEM_SPEC),
            input_output_aliases={0: 2, 1: 3},
            compiler_params=_SPLIT_PARAMS,
        )(_in_hbm(own), _in_hbm(g_init), *after)

    def mid(self, after):
        n = self.n

        def body(own_ref, g_ref, send1, recv1, after_ref, send2, recv2, g_out, token):
            x, y, c = _place()
            sibling = (x, y, 1 - c)
            chips = _other_chips(x, y)
            for j, chip in enumerate(chips):
                _gather_all(own_ref, g_ref, send1.at[j], recv1.at[j], chip, c, c).wait_recv()
                for cp in _gather_forwards(n, g_ref, send2.at[j], recv2.at[j], chip, c, sibling):
                    cp.start()
            for j, chip in enumerate(chips):
                _gather_all(own_ref, g_ref, send1.at[j], recv1.at[j], chip, c, c).wait_send()
            token[...] = jnp.zeros_like(token)

        self.send2, self.recv2, self.g, token = pl.pallas_call(
            body,
            name=f"{self.tag}_mid",
            out_shape=(
                pltpu.SemaphoreType.DMA((3,)), pltpu.SemaphoreType.DMA((3,)),
                pltpu.HBM(self.g.shape, self.g.dtype), _TOKEN,
            ),
            in_specs=(_HBM_SPEC, _HBM_SPEC, _SEM_SPEC, _SEM_SPEC, _ANY_SPEC),
            out_specs=(_SEM_SPEC, _SEM_SPEC, _HBM_SPEC, _VMEM_SPEC),
            input_output_aliases={1: 2},
            compiler_params=_SPLIT_PARAMS,
        )(self.own, self.g, self.send1, self.recv1, after)
        if self.then is None:
            return [token]
        self.then.start([token])
        return [token, self.then.token]

    def get(self, after):
        def body(g_ref, send2, recv2, after_ref, g_out):
            x, y, c = _place()
            for j, chip in enumerate(_other_chips(x, y)):
                slot_in = g_ref.at[2 * chip[0] + chip[1], :, pl.ds((1 - c) * HALF, HALF), :]
                slot_out = g_ref.at[2 * chip[0] + chip[1], :, pl.ds(c * HALF, HALF), :]
                cp = pltpu.make_async_remote_copy(
                    src_ref=slot_out, dst_ref=slot_in, send_sem=send2.at[j], recv_sem=recv2.at[j],
                    device_id=(x, y, 1 - c), device_id_type=_MESH,
                )
                cp.wait_send()
                cp.wait_recv()

        return pl.pallas_call(
            body,
            name=f"{self.tag}_wait",
            out_shape=pltpu.HBM(self.g.shape, self.g.dtype),
            in_specs=(_HBM_SPEC, _SEM_SPEC, _SEM_SPEC, _ANY_SPEC),
            out_specs=_HBM_SPEC,
            input_output_aliases={0: 0},
            compiler_params=_SPLIT_PARAMS,
        )(self.g, self.send2, self.recv2, after)


def _pair_exchange(name, gb):
    def body(gb_ref, land_ref, send_sem, recv_sem):
        x, y, c = _place()
        theirs = gb_ref.at[:, :, pl.ds((1 - c) * HALF, HALF), :]
        cp = pltpu.make_async_remote_copy(
            src_ref=theirs, dst_ref=land_ref, send_sem=send_sem, recv_sem=recv_sem,
            device_id=(x, y, 1 - c), device_id_type=_MESH,
        )
        cp.start()
        cp.wait()

    return pl.pallas_call(
        body,
        name=name,
        out_shape=_sds((N_CHIPS, gb.shape[1], HALF, D_MODEL), gb.dtype),
        in_specs=[pl.BlockSpec(memory_space=pl.ANY)],
        out_specs=pl.BlockSpec(memory_space=pl.ANY),
        scratch_shapes=[pltpu.SemaphoreType.DMA, pltpu.SemaphoreType.DMA],
    )(gb)


def _pair_sum(name, gb, land):
    def body(gb_ref, land_ref, o_ref):
        c = lax.axis_index("c")
        mine = gb_ref[pl.ds(pl.multiple_of(c * HALF, 16), HALF), :]
        o_ref[...] = (mine.astype(F32) + land_ref[...].astype(F32)).astype(o_ref.dtype)

    return pl.pallas_call(
        body,
        name=name,
        grid=(N_CHIPS, gb.shape[1]),
        in_specs=[
            pl.BlockSpec((None, None, F_SHARD, D_MODEL), lambda j, p: (j, p, 0, 0)),
            pl.BlockSpec((None, None, HALF, D_MODEL), lambda j, p: (j, p, 0, 0)),
        ],
        out_specs=pl.BlockSpec((None, None, HALF, D_MODEL), lambda j, p: (j, p, 0, 0)),
        out_shape=_sds(land.shape, BF16),
        compiler_params=_cparams(("parallel", "parallel")),
    )(gb, land)


def _chip_exchange(name, part):
    def body(p_ref, land_ref, send_sems, recv_sems, local_sem):
        x, y, c = _place()
        me = 2 * x + y
        chips = _other_chips(x, y)
        mine = pltpu.make_async_copy(p_ref.at[me], land_ref.at[me], local_sem)
        mine.start()

        def copy(k, chip):
            return pltpu.make_async_remote_copy(
                src_ref=p_ref.at[2 * chip[0] + chip[1]], dst_ref=land_ref.at[me],
                send_sem=send_sems.at[k], recv_sem=recv_sems.at[k], device_id=(*chip, c), device_id_type=_MESH,
            )

        sends = [copy(k, chip) for k, chip in enumerate(chips)]
        for cp in sends:
            cp.start()
        for k, chip in enumerate(chips):
            pltpu.make_async_remote_copy(
                src_ref=p_ref.at[me], dst_ref=land_ref.at[2 * chip[0] + chip[1]],
                send_sem=send_sems.at[k], recv_sem=recv_sems.at[k], device_id=(*chip, c), device_id_type=_MESH,
            ).wait_recv()
        for cp in sends:
            cp.wait_send()
        mine.wait()

    return pl.pallas_call(
        body,
        name=name,
        out_shape=_sds(part.shape, part.dtype),
        in_specs=[pl.BlockSpec(memory_space=pl.ANY)],
        out_specs=pl.BlockSpec(memory_space=pl.ANY),
        scratch_shapes=[pltpu.SemaphoreType.DMA((3,)), pltpu.SemaphoreType.DMA((3,)), pltpu.SemaphoreType.DMA],
    )(part)


def _chip_sum_share(name, land):
    n = land.shape[1]

    def body(l_ref, r_ref, buf, send_sems, recv_sems, local_sems):
        p = pl.program_id(0)
        x, y, c = _place()
        acc = l_ref[0].astype(F32)
        for j in range(1, N_CHIPS):
            acc = acc + l_ref[j].astype(F32)
        buf[p] = acc

        def copies(q):
            mine = r_ref.at[q, pl.ds(c * HALF, HALF), :]
            theirs = r_ref.at[q, pl.ds((1 - c) * HALF, HALF), :]
            local = pltpu.make_async_copy(buf.at[q], mine, local_sems.at[q])
            out = pltpu.make_async_remote_copy(
                src_ref=buf.at[q], dst_ref=mine, send_sem=send_sems.at[q], recv_sem=recv_sems.at[q],
                device_id=(x, y, 1 - c), device_id_type=_MESH,
            )
            arrive = pltpu.make_async_remote_copy(
                src_ref=buf.at[q], dst_ref=theirs, send_sem=send_sems.at[q], recv_sem=recv_sems.at[q],
                device_id=(x, y, 1 - c), device_id_type=_MESH,
            )
            return local, out, arrive

        local, out, _ = copies(p)
        local.start()
        out.start()

        @pl.when(p == n - 1)
        def _():
            for q in range(n):
                local, out, arrive = copies(q)
                arrive.wait_recv()
                out.wait_send()
                local.wait()

    return pl.pallas_call(
        body,
        name=name,
        grid=(n,),
        in_specs=[pl.BlockSpec((N_CHIPS, None, HALF, D_MODEL), lambda p: (0, p, 0, 0))],
        out_specs=pl.BlockSpec(memory_space=pl.ANY),
        out_shape=_sds((n, F_SHARD, D_MODEL), F32),
        scratch_shapes=[
            pltpu.VMEM((n, HALF, D_MODEL), F32),
            pltpu.SemaphoreType.DMA((n,)), pltpu.SemaphoreType.DMA((n,)), pltpu.SemaphoreType.DMA((n,)),
        ],
        compiler_params=_cparams(("arbitrary",)),
    )(land)


class _ReduceBehind:
    def __init__(self, tag, gb, after=()):
        self.tag = tag
        n = gb.shape[1]
        land = lax.empty((N_CHIPS, n, HALF, D_MODEL), gb.dtype)

        def body(gb_ref, land_ref, *rest):
            send_sem, recv_sem, _, _, token = rest[len(after):]
            self._pair_copy(gb_ref, land_ref, send_sem, recv_sem).start()
            token[...] = jnp.zeros_like(token)

        self.send, self.recv, self.gb, self.land, self.token = pl.pallas_call(
            body,
            name=f"grads_pair_start_{tag}",
            out_shape=(
                pltpu.SemaphoreType.DMA((1,)), pltpu.SemaphoreType.DMA((1,)),
                pltpu.HBM(gb.shape, gb.dtype), pltpu.HBM(land.shape, land.dtype), _TOKEN,
            ),
            in_specs=(_HBM_SPEC, _HBM_SPEC) + (_ANY_SPEC,) * len(after),
            out_specs=(_SEM_SPEC, _SEM_SPEC, _HBM_SPEC, _HBM_SPEC, _VMEM_SPEC),
            input_output_aliases={0: 2, 1: 3},
            compiler_params=_SPLIT_PARAMS,
        )(_in_hbm(gb), _in_hbm(land), *after)

    @staticmethod
    def _pair_copy(gb_ref, land_ref, send_sem, recv_sem):
        x, y, c = _place()
        return pltpu.make_async_remote_copy(
            src_ref=gb_ref.at[:, :, pl.ds((1 - c) * HALF, HALF), :], dst_ref=land_ref,
            send_sem=send_sem.at[0], recv_sem=recv_sem.at[0], device_id=(x, y, 1 - c), device_id_type=_MESH,
        )

    @staticmethod
    def _chip_copies(p_ref, land_ref, send_sems, recv_sems):
        x, y, c = _place()
        me = 2 * x + y
        sends, arrivals = [], []
        for k, chip in enumerate(_other_chips(x, y)):
            them = 2 * chip[0] + chip[1]
            sends.append(pltpu.make_async_remote_copy(
                src_ref=p_ref.at[them], dst_ref=land_ref.at[me], send_sem=send_sems.at[k], recv_sem=recv_sems.at[k],
                device_id=(*chip, c), device_id_type=_MESH,
            ))
            arrivals.append(pltpu.make_async_remote_copy(
                src_ref=p_ref.at[me], dst_ref=land_ref.at[them], send_sem=send_sems.at[k], recv_sem=recv_sems.at[k],
                device_id=(*chip, c), device_id_type=_MESH,
            ))
        return sends, arrivals

    def mid(self, after):
        tag = self.tag

        def wait_body(gb_ref, land_ref, send_sem, recv_sem, after_ref, gb_out, land_out):
            cp = self._pair_copy(gb_ref, land_ref, send_sem, recv_sem)
            cp.wait_send()
            cp.wait_recv()

        gb, land = pl.pallas_call(
            wait_body,
            name=f"grads_pair_wait_{tag}",
            out_shape=(pltpu.HBM(self.gb.shape, self.gb.dtype), pltpu.HBM(self.land.shape, self.land.dtype)),
            in_specs=(_HBM_SPEC, _HBM_SPEC, _SEM_SPEC, _SEM_SPEC, _ANY_SPEC),
            out_specs=(_HBM_SPEC, _HBM_SPEC),
            input_output_aliases={0: 0, 1: 1},
            compiler_params=_SPLIT_PARAMS,
        )(self.gb, self.land, self.send, self.recv, after)
        part = _pair_sum(f"grads_pair_sum_{tag}", gb, land)

        x, y, _ = _place()
        start = (2 * x + y, 0, 0, 0)
        own = lax.dynamic_slice(part, start, (1,) + part.shape[1:])
        land2 = lax.dynamic_update_slice(lax.empty(part.shape, part.dtype), own, start)

        def start_body(p_ref, land_ref, send_sems, recv_sems, p_out, land_out, token):
            for cp in self._chip_copies(p_ref, land_ref, send_sems, recv_sems)[0]:
                cp.start()
            token[...] = jnp.zeros_like(token)

        self.send2, self.recv2, self.part, self.land2, token = pl.pallas_call(
            start_body,
            name=f"grads_chip_start_{tag}",
            out_shape=(
                pltpu.SemaphoreType.DMA((3,)), pltpu.SemaphoreType.DMA((3,)),
                pltpu.HBM(part.shape, part.dtype), pltpu.HBM(land2.shape, land2.dtype), _TOKEN,
            ),
            in_specs=(_HBM_SPEC, _HBM_SPEC),
            out_specs=(_SEM_SPEC, _SEM_SPEC, _HBM_SPEC, _HBM_SPEC, _VMEM_SPEC),
            input_output_aliases={0: 2, 1: 3},
            compiler_params=_SPLIT_PARAMS,
        )(_in_hbm(part), _in_hbm(land2))
        return [token]

    def get(self, after):
        n_after = len(after)

        def wait_body(p_ref, land_ref, send_sems, recv_sems, *rest):
            sends, arrivals = self._chip_copies(p_ref, land_ref, send_sems, recv_sems)
            for cp in arrivals:
                cp.wait_recv()
            for cp in sends:
                cp.wait_send()

        _, land2 = pl.pallas_call(
            wait_body,
            name=f"grads_chip_wait_{self.tag}",
            out_shape=(pltpu.HBM(self.part.shape, self.part.dtype), pltpu.HBM(self.land2.shape, self.land2.dtype)),
            in_specs=(_HBM_SPEC, _HBM_SPEC, _SEM_SPEC, _SEM_SPEC) + (_ANY_SPEC,) * n_after,
            out_specs=(_HBM_SPEC, _HBM_SPEC),
            input_output_aliases={0: 0, 1: 1},
            compiler_params=_SPLIT_PARAMS,
        )(self.part, self.land2, self.send2, self.recv2, *after)
        return _chip_sum_share(f"grads_chip_sum_share_{self.tag}", land2)


def _allreduce_small(blk):
    gathered = _allgather_small("allgather_small_grads", blk).reshape(N_DEV, PK_ROWS, 128)

    def body(g_ref, o_ref):
        acc = g_ref[0]
        for k in range(1, N_DEV):
            acc = acc + g_ref[k]
        o_ref[...] = acc

    total = pl.pallas_call(body, name="small_grads_sum", out_shape=_sds((PK_ROWS, 128), F32))(gathered)
    return gathered, total


def _adamw(name, w, g, m, v):
    rows, cols = w.shape
    tm = rows
    while tm * cols * 4 > (1 << 20) and tm % 16 == 0:
        tm //= 2

    def fn(w_, g_, m_, v_):
        m_new = ADAM_B1 * m_ + (1.0 - ADAM_B1) * g_
        v_new = ADAM_B2 * v_ + (1.0 - ADAM_B2) * (g_ * g_)
        m_hat = m_new / (1.0 - ADAM_B1 ** ADAM_STEP)
        v_hat = v_new / (1.0 - ADAM_B2 ** ADAM_STEP)
        delta = -ADAM_LR * (m_hat / (jnp.sqrt(v_hat) + ADAM_EPS) + ADAM_WD * w_)
        return delta, m_new, v_new

    out = _sds(w.shape, F32)
    return _rowwise(name, fn, [w, g, m, v], [], [out, out, out], [], tm)


def _pack_small(b_mod_like, n1, n2, n3, nf, qn, kvn, sinks, rel):
    parts = [
        b_mod_like.reshape(PK_MOD, 128),
        n1.reshape(8, 128), n2.reshape(8, 128), n3.reshape(8, 128), nf.reshape(8, 128),
        qn.reshape(2, 128), kvn.reshape(1, 128),
        jnp.pad(sinks.reshape(1, SWA_HEADS), ((0, 0), (0, 128 - SWA_HEADS))),
        rel.reshape(2, 128),
        jnp.zeros((2, 128), F32),
    ]
    return jnp.concatenate(parts, axis=0)


def _unpack_small(p):
    o = PK_MOD
    return (
        p[:o].reshape(1, N_MOD * D_MODEL),
        p[o:o + 8].reshape(1, D_MODEL), p[o + 8:o + 16].reshape(1, D_MODEL),
        p[o + 16:o + 24].reshape(1, D_MODEL), p[o + 24:o + 32].reshape(D_MODEL),
        p[o + 32:o + 34].reshape(1, MLA_Q_RANK), p[o + 34:o + 35].reshape(1, MLA_KV_RANK),
        p[o + 35:o + 36, :SWA_HEADS].reshape(1, SWA_HEADS),
        p[o + 36:o + 38].reshape(NUM_BUCKETS, SWA_HEADS),
    )


def kernel(x, c, w_mod, b_mod, norm_ffn1, ffn1_gate, ffn1_up, ffn1_down, norm_mix, w_in, q_norm, kv_norm, w_uq, w_ukv, sinks, w_o, norm_ffn2, ffn2_gate, ffn2_up, ffn2_down, rel_bias, norm_final, loss_target, m_w_mod, m_b_mod, m_norm_ffn1, m_ffn1_gate, m_ffn1_up, m_ffn1_down, m_norm_mix, m_w_in, m_q_norm, m_kv_norm, m_w_uq, m_w_ukv, m_sinks, m_w_o, m_norm_ffn2, m_ffn2_gate, m_ffn2_up, m_ffn2_down, m_rel_bias, m_norm_final, v_w_mod, v_b_mod, v_norm_ffn1, v_ffn1_gate, v_ffn1_up, v_ffn1_down, v_norm_mix, v_w_in, v_q_norm, v_kv_norm, v_w_uq, v_w_ukv, v_sinks, v_w_o, v_norm_ffn2, v_ffn2_gate, v_ffn2_up, v_ffn2_down, v_rel_bias, v_norm_final):
    ax, ay, ac = _place()
    chip = 2 * ax + ay
    dev = 2 * chip + ac
    n_mod_shard = N_MOD * D_MODEL // N_CHIPS

    def t16(w):
        return w[0].T.astype(BF16)

    rest = jnp.concatenate(
        [
            t16(w_in),
            w_o[0].astype(BF16),
            t16(w_uq).reshape(R_UQ, D_MODEL),
            t16(w_ukv).reshape(R_UKV, D_MODEL),
            jnp.zeros((F_SHARD - O_PAD, D_MODEL), BF16),
        ],
        axis=0,
    )
    own_gu1 = jnp.stack([t16(ffn1_gate), t16(ffn1_up)])
    own_down1 = ffn1_down[0].astype(BF16)[None]
    own_mix = rest[None]
    own_ffn2 = jnp.stack([t16(ffn2_gate), t16(ffn2_up), ffn2_down[0].astype(BF16)])

    (c_act,) = _rowwise(
        "silu_c", lambda v: v * _sigmoid(v), [c.reshape(8, 128)], [], [_sds((8, 128), F32)], [], 8
    )
    c_all = _allgather_small("allgather_c", c_act).reshape(N_DEV, D_MODEL)
    mod_cols = _mm(
        "mod_fwd", "nn", (1, n_mod_shard // 768, 1),
        c_all, (N_DEV, D_MODEL), lambda i, j, k: (0, 0),
        w_mod[0], (D_MODEL, 768), lambda i, j, k: (0, j),
        _sds((N_DEV, n_mod_shard), F32), (N_DEV, 768), lambda i, j, k: (0, j),
        (N_DEV, 768),
    )
    mod_all = _allgather_small("allgather_mod", mod_cols).reshape(N_CHIPS, 2, N_DEV, n_mod_shard)[:, 0]
    mod_all = mod_all.transpose(1, 0, 2).reshape(N_DEV, N_MOD * D_MODEL)
    mod = lax.dynamic_slice_in_dim(mod_all, dev, 1, axis=0) + b_mod

    norms = (norm_ffn1, norm_mix, norm_ffn2, norm_final.reshape(1, D_MODEL))

    ffn2_src = _GatherBehind("gather_ffn2", own_ffn2)
    mix_src = _GatherBehind("gather_mix", own_mix, then=ffn2_src)
    down1_src = _GatherBehind("gather_down1", own_down1, then=mix_src)
    gu1_src = _GatherBehind("gather_gu1", own_gu1, then=down1_src)
    gu1_src.start([mod_all])

    reducing, red = {}, {}

    def begin(tag, gb, after):
        reducing[tag] = _ReduceBehind(tag, gb, after=after)
        return [reducing[tag].token]

    def ffn2_gu_done(gb):
        return begin("ffn2_gu", gb, reducing["ffn2_down"].mid(gb))

    def mix_done(dx1, gb_rest):
        red["ffn2_down"] = reducing["ffn2_down"].get([dx1])
        red["ffn2_gu"] = reducing["ffn2_gu"].get([red["ffn2_down"]])
        return begin("rest", gb_rest, [red["ffn2_gu"]])

    def ffn1_gu_done(gb):
        red["rest"] = reducing["rest"].get([gb])
        begin("ffn1_gu", gb, reducing["ffn1_down"].mid(red["rest"]))
        return reducing["ffn1_gu"].mid(reducing["ffn1_gu"].token)

    loss_part, grad_x, _, dmod, small = _device_step(
        x[0], loss_target[0], mod, gu1_src, down1_src, mix_src, ffn2_src, norms, q_norm, kv_norm, sinks, rel_bias,
        hooks2=_BwdHooks(after_wdown=lambda gb: begin("ffn2_down", gb, ()), after_wgu=ffn2_gu_done),
        hooks_mix=_BwdHooks(after_gate=lambda dz: reducing["ffn2_gu"].mid(dz)),
        mix_done=mix_done,
        hooks1=_BwdHooks(
            after_swiglu=lambda dab3: reducing["rest"].mid(dab3),
            after_wdown=lambda gb: begin("ffn1_down", gb, ()),
            after_wgu=ffn1_gu_done,
        ),
    )
    loss = lax.psum(loss_part, ("x", "y", "c"))

    gathered, total = _allreduce_small(_pack_small(dmod, *small))
    (g_b_mod, g_n1, g_n2, g_n3, g_nf, g_qn, g_kvn, g_sinks, g_rel) = _unpack_small(total)
    dmod_all = gathered[:, :PK_MOD].reshape(N_DEV, N_MOD * D_MODEL)
    dmod_cols = lax.dynamic_slice_in_dim(dmod_all, chip * n_mod_shard, n_mod_shard, axis=1)
    g_w_mod = _mm(
        "mod_bwd", "tn", (1, n_mod_shard // 768, 1),
        c_all, (N_DEV, D_MODEL), lambda i, j, k: (0, 0),
        dmod_cols, (N_DEV, 768), lambda i, j, k: (0, j),
        _sds((D_MODEL, n_mod_shard), F32), (D_MODEL, 768), lambda i, j, k: (0, j),
        (D_MODEL, 768),
    )

    r_rest = red["rest"][0]
    transposed = {"ffn1_gate", "ffn1_up", "ffn2_gate", "ffn2_up", "w_in", "w_uq", "w_ukv"}
    g_big = {
        "ffn2_gate": red["ffn2_gu"][0], "ffn2_up": red["ffn2_gu"][1], "ffn2_down": red["ffn2_down"][0],
        "w_in": r_rest[O_IN:O_IN + R_IN],
        "w_o": r_rest[O_O:O_O + R_O],
        "w_uq": r_rest[O_UQ:O_UQ + R_UQ].reshape(MLA_QK, MLA_Q_RANK),
        "w_ukv": r_rest[O_UKV:O_UKV + R_UKV].reshape(MLA_NOPE + MLA_V, MLA_KV_RANK),
        "w_mod": g_w_mod,
    }
    w_big = {
        "ffn2_gate": (ffn2_gate, m_ffn2_gate, v_ffn2_gate),
        "ffn2_up": (ffn2_up, m_ffn2_up, v_ffn2_up), "ffn2_down": (ffn2_down, m_ffn2_down, v_ffn2_down),
        "w_in": (w_in, m_w_in, v_w_in), "w_o": (w_o, m_w_o, v_w_o), "w_uq": (w_uq, m_w_uq, v_w_uq),
        "w_ukv": (w_ukv, m_w_ukv, v_w_ukv), "w_mod": (w_mod, m_w_mod, v_w_mod),
    }
    grads, deltas, new_m, new_v = {}, {}, {}, {}

    def update(names):
        for nm in names:
            w, m, v = w_big[nm]
            if nm in transposed:
                outs = _adamw(f"adamw_{nm}", w[0].T, g_big[nm], m[0].T, v[0].T)
                g_, d_, m_, v_ = [a.T for a in (g_big[nm],) + tuple(outs)]
            else:
                g_, (d_, m_, v_) = g_big[nm], _adamw(f"adamw_{nm}", w[0], g_big[nm], m[0], v[0])
            grads[nm], deltas[nm], new_m[nm], new_v[nm] = g_[None], d_[None], m_[None], v_[None]

    update(list(w_big))
    red1_down = reducing["ffn1_down"].get([deltas[nm] for nm in w_big])
    red1_gu = reducing["ffn1_gu"].get([red1_down])
    g_big.update({"ffn1_gate": red1_gu[0], "ffn1_up": red1_gu[1], "ffn1_down": red1_down[0]})
    w_big.update({
        "ffn1_gate": (ffn1_gate, m_ffn1_gate, v_ffn1_gate), "ffn1_up": (ffn1_up, m_ffn1_up, v_ffn1_up),
        "ffn1_down": (ffn1_down, m_ffn1_down, v_ffn1_down),
    })
    update(["ffn1_gate", "ffn1_up", "ffn1_down"])

    small_names = ["b_mod", "norm_ffn1", "norm_mix", "norm_ffn2", "norm_final", "q_norm", "kv_norm", "sinks", "rel_bias"]
    w_small = (b_mod, norm_ffn1, norm_mix, norm_ffn2, norm_final, q_norm, kv_norm, sinks, rel_bias)
    m_small = (m_b_mod, m_norm_ffn1, m_norm_mix, m_norm_ffn2, m_norm_final, m_q_norm, m_kv_norm, m_sinks, m_rel_bias)
    v_small = (v_b_mod, v_norm_ffn1, v_norm_mix, v_norm_ffn2, v_norm_final, v_q_norm, v_kv_norm, v_sinks, v_rel_bias)
    d_p, m_p, v_p = _adamw("adamw_small", _pack_small(*w_small), total, _pack_small(*m_small), _pack_small(*v_small))
    for nm, g_, d_, m_, v_ in zip(
        small_names,
        (g_b_mod, g_n1, g_n2, g_n3, g_nf, g_qn, g_kvn, g_sinks, g_rel),
        _unpack_small(d_p), _unpack_small(m_p), _unpack_small(v_p),
    ):
        grads[nm], deltas[nm], new_m[nm], new_v[nm] = g_, d_, m_, v_

    order = ["w_mod", "b_mod", "norm_ffn1", "ffn1_gate", "ffn1_up", "ffn1_down", "norm_mix", "w_in", "q_norm", "kv_norm",
             "w_uq", "w_ukv", "sinks", "w_o", "norm_ffn2", "ffn2_gate", "ffn2_up", "ffn2_down", "rel_bias", "norm_final"]
    return (loss, grad_x[None], *[grads[n] for n in order], *[deltas[n] for n in order],
            *[new_m[n] for n in order], *[new_v[n] for n in order])
```

```python
import functools
import math

import jax
import jax.numpy as jnp
import numpy as np
from jax import lax
from jax.experimental import pallas as pl
from jax.experimental.pallas import tpu as pltpu

F32, BF16 = jnp.float32, jnp.bfloat16

D_MODEL = 1024
D_FF = 2816
EPS = 1e-6
N_MOD = 9
SWA_HEADS, SWA_KV_HEADS, SWA_HEAD_DIM, WINDOW = 8, 2, 64, 128
SWA_GROUP = SWA_HEADS // SWA_KV_HEADS
MLA_HEADS, MLA_Q_RANK, MLA_KV_RANK, MLA_NOPE, MLA_ROPE, MLA_V = 4, 256, 128, 128, 64, 128
MLA_QK = MLA_NOPE + MLA_ROPE
ROPE_THETA = 10000.0
NUM_BUCKETS, MAX_DISTANCE = 32, 128
D_IN = 1216
N_SWA_COLS = 768
N_MLA_COLS = 512

ADAM_LR, ADAM_B1, ADAM_B2, ADAM_EPS, ADAM_WD, ADAM_STEP = 0.001, 0.9, 0.999, 1e-08, 0.01, 10

N_CHIPS = 4
N_DEV = 8
F_SHARD = D_FF // N_CHIPS
HALF = F_SHARD // 2
R_IN, R_O, R_UQ, R_UKV = 304, 256, 48, 32
O_IN, O_O, O_UQ, O_UKV, O_PAD = 0, 304, 560, 608, 640

PK_MOD = 72
PK_ROWS = 112
VMEM_LIMIT = 56 << 20
ROW_TILE = 2048

_DN = {
    "nn": (((1,), (0,)), ((), ())),
    "nt": (((1,), (1,)), ((), ())),
    "tn": (((0,), (0,)), ((), ())),
}


def _sds(shape, dtype):
    return jax.ShapeDtypeStruct(tuple(shape), dtype)


def _cparams(sem=None):
    kw = {"vmem_limit_bytes": VMEM_LIMIT}
    if sem is not None:
        kw["dimension_semantics"] = sem
    return pltpu.CompilerParams(**kw)


def _dot(a, b, dims):
    return lax.dot_general(a.astype(BF16), b.astype(BF16), _DN[dims], preferred_element_type=F32)


def _mm(name, dims, grid, a, a_blk, a_idx, b, b_blk, b_idx, out, out_blk, out_idx, acc_shape, alias=None, deps=()):
    nk = grid[2]

    def body(*refs):
        a_ref, b_ref = refs[:2]
        o_ref, acc_ref = refs[-2:]
        part = _dot(a_ref[...], b_ref[...], dims)
        if nk == 1:
            o_ref[...] = part.astype(o_ref.dtype)
            return
        k = pl.program_id(2)

        @pl.when(k == 0)
        def _():
            acc_ref[...] = part

        @pl.when((k > 0) & (k < nk - 1))
        def _():
            acc_ref[...] += part

        @pl.when(k == nk - 1)
        def _():
            o_ref[...] = (acc_ref[...] + part).astype(o_ref.dtype)

    in_specs = [pl.BlockSpec(a_blk, a_idx), pl.BlockSpec(b_blk, b_idx)]
    args = [a, b]
    kw = {}
    if alias is not None:
        in_specs.append(pl.BlockSpec(memory_space=pl.ANY))
        args.append(alias)
        kw["input_output_aliases"] = {2: 0}
    in_specs += [pl.BlockSpec(memory_space=pl.ANY)] * len(deps)
    args += list(deps)
    return pl.pallas_call(
        body,
        name=name,
        grid=grid,
        in_specs=in_specs,
        out_specs=pl.BlockSpec(out_blk, out_idx),
        out_shape=out,
        scratch_shapes=[pltpu.VMEM(acc_shape if nk > 1 else (8, 128), F32)],
        compiler_params=_cparams(("parallel", "parallel", "arbitrary")),
        **kw,
    )(*args)


def _rowwise(name, fn, tiled, full, out_tiled, out_red, tm, deps=()):
    rows = tiled[0].shape[-2]
    grid = (rows // tm,)
    n_in = len(tiled) + len(full)
    n_t = len(out_tiled)
    n_dep = len(deps)

    def tspec(shape):
        nd = len(shape)
        return pl.BlockSpec(tuple(shape[:-2]) + (tm, shape[-1]), lambda i, nd=nd: (0,) * (nd - 2) + (i, 0))

    def fspec(shape):
        nd = len(shape)
        return pl.BlockSpec(tuple(shape), lambda i, nd=nd: (0,) * nd)

    def body(*refs):
        outs = fn(*[r[...] for r in refs[:n_in]])
        if not isinstance(outs, (tuple, list)):
            outs = (outs,)
        out_refs = refs[n_in + n_dep:]
        for r, v in zip(out_refs[:n_t], outs[:n_t]):
            r[...] = v.astype(r.dtype)
        red_refs = out_refs[n_t:]
        if red_refs:
            @pl.when(pl.program_id(0) == 0)
            def _():
                for r in red_refs:
                    r[...] = jnp.zeros_like(r)

            for r, v in zip(red_refs, outs[n_t:]):
                r[...] += v.astype(r.dtype)

    res = pl.pallas_call(
        body,
        name=name,
        grid=grid,
        in_specs=[tspec(a.shape) for a in tiled] + [fspec(a.shape) for a in full]
        + [pl.BlockSpec(memory_space=pl.ANY)] * n_dep,
        out_specs=[tspec(o.shape) for o in out_tiled] + [fspec(o.shape) for o in out_red],
        out_shape=list(out_tiled) + list(out_red),
        compiler_params=_cparams(("arbitrary",) if out_red else ("parallel",)),
    )(*tiled, *full, *deps)
    return res


def _sum0(v):
    return jnp.sum(v, axis=0, keepdims=True)


def _sigmoid(v):
    return 1.0 / (1.0 + jnp.exp(-v))


def _rms(x):
    r = lax.rsqrt(jnp.mean(x * x, axis=-1, keepdims=True) + EPS)
    return x * r, r


def _rms_bwd(u, xr, r):
    return r * (u - xr * jnp.mean(u * xr, axis=-1, keepdims=True))


class _Ready:
    def __init__(self, g):
        self.g = g

    def mid(self, after):
        return []

    def get(self, after):
        return self.g


def _ffn_fwd(tag, x, gamma, sh, sc, gate, gu_src, base, down_src, down_idx, mid_after):
    s_len = x.shape[0]
    deps = gu_src.mid(mid_after)

    def normmod(x_, gamma_, sc_, sh_):
        xr, _ = _rms(x_)
        return xr * gamma_ * (1.0 + sc_) + sh_

    (h,) = _rowwise(f"{tag}_normmod", normmod, [x], [gamma, sc, sh], [_sds((s_len, D_MODEL), BF16)], [], 256, deps=deps)
    g4 = gu_src.get(h)

    tm = min(ROW_TILE, s_len)
    ab3 = _mm(
        f"{tag}_gate_up", "nt", (s_len // tm, 2 * N_CHIPS, 1),
        h, (tm, D_MODEL), lambda i, j, k: (i, 0),
        g4, (None, None, F_SHARD, D_MODEL), lambda i, j, k: (j % N_CHIPS, base + j // N_CHIPS, 0, 0),
        _sds((2 * N_CHIPS, s_len, F_SHARD), BF16), (None, tm, F_SHARD), lambda i, j, k: (j, i, 0),
        (tm, F_SHARD),
    )

    def swiglu(ab):
        a = ab[:N_CHIPS].astype(F32)
        b = ab[N_CHIPS:].astype(F32)
        return a * _sigmoid(a) * b

    (s3,) = _rowwise(
        f"{tag}_swiglu", swiglu, [ab3], [], [_sds((N_CHIPS, s_len, F_SHARD), BF16)], [], 128,
        deps=down_src.mid(ab3) if down_src is not None else (),
    )
    g_down = down_src.get(s3) if down_src is not None else g4

    y = _mm(
        f"{tag}_down", "nn", (s_len // tm, 1, N_CHIPS),
        s3, (None, tm, F_SHARD), lambda i, j, k: (k, i, 0),
        g_down, (None, None, F_SHARD, D_MODEL), lambda i, j, k: (k, down_idx, 0, 0),
        _sds((s_len, D_MODEL), F32), (tm, D_MODEL), lambda i, j, k: (i, 0),
        (tm, D_MODEL),
    )

    (x_out,) = _rowwise(
        f"{tag}_residual", lambda x_, y_, g_: x_ + 0.5 * g_ * y_, [x, y], [gate], [_sds((s_len, D_MODEL), F32)], [], 256
    )
    return x_out, (g4, base, g_down, down_idx, h, ab3, s3, y)


def _normmod_bwd_call(name, dh_parts, x, dxo, gamma, sc, deps=()):
    s_len = x.shape[0]
    n_parts = len(dh_parts)

    def fn(*vals):
        dh = vals[0]
        for extra in vals[1:n_parts]:
            dh = dh + extra
        x_, dxo_, gamma_, sc_ = vals[n_parts:]
        xr, r = _rms(x_)
        n = xr * gamma_
        dn = dh * (1.0 + sc_)
        dx = dxo_ + _rms_bwd(dn * gamma_, xr, r)
        return dx, _sum0(dh * n), _sum0(dh), _sum0(dn * xr)

    vec = _sds((1, D_MODEL), F32)
    return _rowwise(
        name, fn, list(dh_parts) + [x, dxo], [gamma, sc], [_sds((s_len, D_MODEL), F32)], [vec, vec, vec], 256, deps=deps
    )


class _BwdHooks:
    def __init__(self, after_gate=None, after_swiglu=None, after_wdown=None, after_wgu=None, after_dh=None):
        def nothing(_):
            return []

        self.after_gate = after_gate or nothing
        self.after_swiglu = after_swiglu or nothing
        self.after_wdown = after_wdown or nothing
        self.after_wgu = after_wgu or nothing
        self.after_dh = after_dh or nothing


def _ffn_bwd(tag, dxo, x, gamma, sc, gate, saved, hooks, deps=()):
    g4, base, g_down, down_idx, h, ab3, s3, y = saved
    s_len = x.shape[0]
    vec = _sds((1, D_MODEL), F32)

    dy, dgate = _rowwise(
        f"{tag}_bwd_gate", lambda dxo_, y_, g_: (0.5 * g_ * dxo_, _sum0(0.5 * y_ * dxo_)),
        [dxo, y], [gate], [_sds((s_len, D_MODEL), BF16)], [vec], 256, deps=deps,
    )

    tm = min(ROW_TILE, s_len)
    ds3 = _mm(
        f"{tag}_bwd_ds", "nt", (s_len // tm, N_CHIPS, 1),
        dy, (tm, D_MODEL), lambda i, j, k: (i, 0),
        g_down, (None, None, F_SHARD, D_MODEL), lambda i, j, k: (j, down_idx, 0, 0),
        _sds((N_CHIPS, s_len, F_SHARD), BF16), (None, tm, F_SHARD), lambda i, j, k: (j, i, 0),
        (tm, F_SHARD),
    )

    def swiglu_bwd(ab, ds):
        a = ab[:N_CHIPS].astype(F32)
        b = ab[N_CHIPS:].astype(F32)
        ds = ds.astype(F32)
        sig = _sigmoid(a)
        da = ds * b * sig * (1.0 + a * (1.0 - sig))
        db = ds * a * sig
        return jnp.concatenate([da, db], axis=0)

    (dab3,) = _rowwise(
        f"{tag}_bwd_swiglu", swiglu_bwd, [ab3, ds3], [], [_sds((2 * N_CHIPS, s_len, F_SHARD), BF16)], [], 128
    )

    tk = tm
    gb_down = _mm(
        f"{tag}_bwd_wdown", "tn", (N_CHIPS, 1, s_len // tk),
        s3, (None, tk, F_SHARD), lambda i, j, k: (i, k, 0),
        dy, (tk, D_MODEL), lambda i, j, k: (k, 0),
        _sds((N_CHIPS, 1, F_SHARD, D_MODEL), BF16), (None, None, F_SHARD, D_MODEL), lambda i, j, k: (i, 0, 0, 0),
        (F_SHARD, D_MODEL), deps=hooks.after_swiglu(dab3),
    )
    gb_gu = _mm(
        f"{tag}_bwd_wgu", "tn", (2 * N_CHIPS, 1, s_len // tk),
        dab3, (None, tk, F_SHARD), lambda i, j, k: (i, k, 0),
        h, (tk, D_MODEL), lambda i, j, k: (k, 0),
        _sds((N_CHIPS, 2, F_SHARD, D_MODEL), BF16), (None, None, F_SHARD, D_MODEL),
        lambda i, j, k: (i % N_CHIPS, i // N_CHIPS, 0, 0),
        (F_SHARD, D_MODEL), deps=hooks.after_wdown(gb_down),
    )
    dh = _mm(
        f"{tag}_bwd_dh", "nn", (s_len // tm, 1, 2 * N_CHIPS),
        dab3, (None, tm, F_SHARD), lambda i, j, k: (k, i, 0),
        g4, (None, None, F_SHARD, D_MODEL), lambda i, j, k: (k % N_CHIPS, base + k // N_CHIPS, 0, 0),
        _sds((s_len, D_MODEL), F32), (tm, D_MODEL), lambda i, j, k: (i, 0),
        (tm, D_MODEL), deps=hooks.after_wgu(gb_gu),
    )
    dx, dsc, dsh, dgamma = _normmod_bwd_call(
        f"{tag}_bwd_normmod", [dh], x, dxo, gamma, sc, deps=hooks.after_dh(dh)
    )
    return dx, (gb_down, gb_gu), (dsh, dsc, dgate, dgamma)


def _bucket_map():
    qi = np.arange(WINDOW)[:, None]
    kj = np.arange(2 * WINDOW)[None, :]
    dist = qi + WINDOW - kj
    band = (dist >= 0) & (dist < WINDOW)
    max_exact = NUM_BUCKETS // 2
    n = np.maximum(dist, 0)
    large = []
    for dt in (np.float32, np.float64):
        nf = np.maximum(n, 1).astype(dt)
        val = np.log(nf / dt(max_exact)) / dt(math.log(MAX_DISTANCE / max_exact)) * dt(NUM_BUCKETS - max_exact)
        large.append(np.minimum(max_exact + val.astype(np.int32), NUM_BUCKETS - 1))
    assert np.array_equal(large[0], large[1])
    bucket = np.where(n < max_exact, n, large[0])
    return np.where(band, bucket, -1).astype(np.int32).reshape(1, -1)


def _bias_expand(rel_bias_t, bkt):
    n = bkt.shape[1]

    def body(rb_ref, bkt_ref, o_ref):
        onehot = (lax.broadcasted_iota(jnp.int32, (NUM_BUCKETS, n), 0) == bkt_ref[...]).astype(F32)
        o_ref[...] = lax.dot_general(
            rb_ref[...], onehot, _DN["nn"], precision=lax.Precision.HIGHEST, preferred_element_type=F32
        )

    return pl.pallas_call(
        body, name="bias_expand", out_shape=_sds((SWA_HEADS, n), F32), compiler_params=_cparams()
    )(rel_bias_t, bkt)


def _bias_reduce(dbias_flat, bkt):
    n = bkt.shape[1]

    def body(db_ref, bkt_ref, o_ref):
        onehot = (lax.broadcasted_iota(jnp.int32, (NUM_BUCKETS, n), 0) == bkt_ref[...]).astype(F32)
        o_ref[...] = lax.dot_general(
            db_ref[...], onehot, _DN["nt"], precision=lax.Precision.HIGHEST, preferred_element_type=F32
        )

    return pl.pallas_call(
        body, name="bias_reduce", out_shape=_sds((SWA_HEADS, NUM_BUCKETS), F32), compiler_params=_cparams()
    )(dbias_flat, bkt)


_SWA_SCALE = SWA_HEAD_DIM ** -0.5
_NEG = -1e30


def _swa_in_specs():
    w = WINDOW
    n_q = SWA_HEADS * SWA_HEAD_DIM
    n_kv = 2 * SWA_KV_HEADS * SWA_HEAD_DIM
    prev = lambda n: jnp.maximum(n - 1, 0)
    return [
        pl.BlockSpec((w, n_q), lambda n: (n, 0)),
        pl.BlockSpec((w, n_kv), lambda n: (prev(n), n_q // n_kv)),
        pl.BlockSpec((w, n_kv), lambda n: (n, n_q // n_kv)),
        pl.BlockSpec((SWA_HEADS, w, 2 * w), lambda n: (0, 0, 0)),
        pl.BlockSpec((SWA_HEADS, w, 1), lambda n: (0, 0, 0)),
    ]


def _head_cols(v, first, count):
    hd = SWA_HEAD_DIM
    return jnp.stack([v[:, (first + i) * hd:(first + i + 1) * hd] for i in range(count)])


def _swa_heads(q_ref, kvp_ref, kvc_ref):
    g, w, hd = SWA_GROUP, WINDOW, SWA_HEAD_DIM
    q = q_ref[...].astype(F32)
    kv = jnp.concatenate([kvp_ref[...], kvc_ref[...]], axis=0).astype(F32)
    heads = []
    for j in range(SWA_KV_HEADS):
        qj = _head_cols(q, g * j, g).reshape(g * w, hd)
        heads.append((qj, _head_cols(kv, j, 1)[0], _head_cols(kv, SWA_KV_HEADS + j, 1)[0]))
    return heads


def _swa_scores(q, kk, bias, n):
    g, w = SWA_GROUP, WINDOW
    s = (_dot(q, kk, "nt") * _SWA_SCALE).reshape(g, w, 2 * w) + bias
    qi = lax.broadcasted_iota(jnp.int32, (w, 2 * w), 0)
    kj = lax.broadcasted_iota(jnp.int32, (w, 2 * w), 1)
    dist = qi + w - kj
    valid = ((dist >= 0) & (dist < w) & ((n > 0) | (kj >= w)))[None]
    return jnp.where(valid, s, _NEG), valid


def _swa_fwd(swa2, bias, sinkb):
    s_len = swa2.shape[0]
    g, w, hd = SWA_GROUP, WINDOW, SWA_HEAD_DIM

    def body(q_ref, kvp_ref, kvc_ref, bias_ref, sink_ref, o_ref, lse_ref):
        n = pl.program_id(0)
        outs = []
        for j, (q, kk, vv) in enumerate(_swa_heads(q_ref, kvp_ref, kvc_ref)):
            hs = slice(g * j, g * (j + 1))
            s, valid = _swa_scores(q, kk, bias_ref[hs], n)
            sink = sink_ref[hs]
            m = jnp.maximum(jnp.max(s, axis=-1, keepdims=True), sink)
            p = jnp.where(valid, jnp.exp(s - m), 0.0)
            l = jnp.sum(p, axis=-1, keepdims=True) + jnp.exp(sink - m)
            o = _dot(p.reshape(g * w, 2 * w), vv, "nn").reshape(g, w, hd) / l
            outs += [o[i] for i in range(g)]
            lse_ref[hs] = m + jnp.log(l)
        o_ref[...] = jnp.concatenate(outs, axis=-1).astype(o_ref.dtype)

    n_q = SWA_HEADS * hd
    return pl.pallas_call(
        body,
        name="swa_fwd",
        grid=(s_len // w,),
        in_specs=_swa_in_specs(),
        out_specs=[
            pl.BlockSpec((w, n_q), lambda n: (n, 0)),
            pl.BlockSpec((SWA_HEADS, w, 1), lambda n: (0, n, 0)),
        ],
        out_shape=[_sds((s_len, n_q), BF16), _sds((SWA_HEADS, s_len, 1), F32)],
        compiler_params=_cparams(("parallel",)),
    )(swa2, swa2, swa2, bias, sinkb)


def _swa_bwd(swa2, bias, sinkb, cat, dcat, lse):
    s_len = swa2.shape[0]
    g, w, hd = SWA_GROUP, WINDOW, SWA_HEAD_DIM

    def body(q_ref, kvp_ref, kvc_ref, bias_ref, sink_ref, o_ref, do_ref, lse_ref,
             dq_ref, dkva_ref, dkvb_ref, dbias_ref, dsink_ref):
        n = pl.program_id(0)

        @pl.when(n == 0)
        def _():
            dbias_ref[...] = jnp.zeros_like(dbias_ref)
            dsink_ref[...] = jnp.zeros_like(dsink_ref)

        o_all = o_ref[...].astype(F32)
        do_all = do_ref[...].astype(F32)
        dqs, dks, dvs = [], [], []
        for j, (q, kk, vv) in enumerate(_swa_heads(q_ref, kvp_ref, kvc_ref)):
            hs = slice(g * j, g * (j + 1))
            s, valid = _swa_scores(q, kk, bias_ref[hs], n)
            lse_v = lse_ref[hs]
            p = jnp.where(valid, jnp.exp(s - lse_v), 0.0)
            do = _head_cols(do_all, g * j, g)
            delta = jnp.sum(do * _head_cols(o_all, g * j, g), axis=-1, keepdims=True)
            do2 = do.reshape(g * w, hd)
            dp = _dot(do2, vv, "nt").reshape(g, w, 2 * w)
            ds = p * (dp - delta)
            dbias_ref[hs] += ds
            dsink_ref[hs] -= jnp.exp(sink_ref[hs] - lse_v) * delta
            ds2 = ds.reshape(g * w, 2 * w)
            dq = (_dot(ds2, kk, "nn") * _SWA_SCALE).reshape(g, w, hd)
            dqs += [dq[i] for i in range(g)]
            dks.append(_dot(ds2, q, "tn") * _SWA_SCALE)
            dvs.append(_dot(p.reshape(g * w, 2 * w), do2, "tn"))
        dq_ref[...] = jnp.concatenate(dqs, axis=-1).astype(dq_ref.dtype)
        dkv = jnp.concatenate(dks + dvs, axis=-1)
        dkvb_ref[...] = dkv[:w]
        dkva_ref[...] = dkv[w:]

    n_q = SWA_HEADS * hd
    n_kv = 2 * SWA_KV_HEADS * hd
    q_spec = pl.BlockSpec((w, n_q), lambda n: (n, 0))
    kv_spec = pl.BlockSpec((w, n_kv), lambda n: (n, 0))
    return pl.pallas_call(
        body,
        name="swa_bwd",
        grid=(s_len // w,),
        in_specs=_swa_in_specs() + [q_spec, q_spec, pl.BlockSpec((SWA_HEADS, w, 1), lambda n: (0, n, 0))],
        out_specs=[
            q_spec, kv_spec, kv_spec,
            pl.BlockSpec((SWA_HEADS, w, 2 * w), lambda n: (0, 0, 0)),
            pl.BlockSpec((SWA_HEADS, w, 1), lambda n: (0, 0, 0)),
        ],
        out_shape=[
            _sds((s_len, n_q), BF16), _sds((s_len, n_kv), F32), _sds((s_len, n_kv), F32),
            _sds((SWA_HEADS, w, 2 * w), F32), _sds((SWA_HEADS, w, 1), F32),
        ],
        compiler_params=_cparams(("arbitrary",)),
    )(swa2, swa2, swa2, bias, sinkb, cat, dcat, lse)


_MLA_SCALE = MLA_QK ** -0.5
_MLA_TQ = 256


def _mla_diag_mask(tq):
    return lax.broadcasted_iota(jnp.int32, (tq, tq), 1) <= lax.broadcasted_iota(jnp.int32, (tq, tq), 0)


def _key_rows(ref, c, tq):
    return ref[pl.ds(pl.multiple_of(c * tq, tq), tq), :]


def _mla_fwd(q4, k4, v4):
    s_len = q4.shape[1]
    tq = min(_MLA_TQ, s_len)

    def body(q_ref, k_ref, v_ref, o_ref, lse_ref):
        i = pl.program_id(1)
        q = q_ref[...]

        def block(c, carry, mask):
            m, l, acc = carry
            s = _dot(q, _key_rows(k_ref, c, tq), "nt") * _MLA_SCALE
            if mask is not None:
                s = jnp.where(mask, s, _NEG)
            m_new = jnp.maximum(m, jnp.max(s, axis=-1, keepdims=True))
            alpha = jnp.exp(m - m_new)
            p = jnp.exp(s - m_new)
            l = alpha * l + jnp.sum(p, axis=-1, keepdims=True)
            return m_new, l, alpha * acc + _dot(p, _key_rows(v_ref, c, tq), "nn")

        start = (jnp.full((tq, 1), _NEG, F32), jnp.zeros((tq, 1), F32), jnp.zeros((tq, MLA_V), F32))
        carry = lax.fori_loop(0, i, lambda c, carry: block(c, carry, None), start)
        m, l, acc = block(i, carry, _mla_diag_mask(tq))
        o_ref[...] = (acc / l).astype(o_ref.dtype)
        lse_ref[...] = m + jnp.log(l)

    return pl.pallas_call(
        body,
        name="mla_fwd",
        grid=(MLA_HEADS, s_len // tq),
        in_specs=[
            pl.BlockSpec((None, tq, MLA_QK), lambda h, i: (h, i, 0)),
            pl.BlockSpec((None, s_len, MLA_QK), lambda h, i: (h, 0, 0)),
            pl.BlockSpec((None, s_len, MLA_V), lambda h, i: (h, 0, 0)),
        ],
        out_specs=[
            pl.BlockSpec((tq, MLA_V), lambda h, i: (i, h)),
            pl.BlockSpec((None, tq, 1), lambda h, i: (h, i, 0)),
        ],
        out_shape=[_sds((s_len, MLA_HEADS * MLA_V), BF16), _sds((MLA_HEADS, s_len, 1), F32)],
        compiler_params=_cparams(("parallel", "parallel")),
    )(q4, k4, v4)


def _mla_bwd(q4, k4, v4, cat, dcat, lse):
    s_len = q4.shape[1]
    tq = min(_MLA_TQ, s_len)
    first = SWA_HEADS * SWA_HEAD_DIM // MLA_V

    def body(q_ref, k_ref, v_ref, o_ref, do_ref, lse_ref, dq_ref, dk_ref, dv_ref):
        i = pl.program_id(1)

        @pl.when(i == 0)
        def _():
            dk_ref[...] = jnp.zeros_like(dk_ref)
            dv_ref[...] = jnp.zeros_like(dv_ref)

        q, do, lse_v = q_ref[...], do_ref[...], lse_ref[...]
        delta = jnp.sum(do.astype(F32) * o_ref[...].astype(F32), axis=-1, keepdims=True)

        def block(c, dq, mask):
            rows = pl.ds(pl.multiple_of(c * tq, tq), tq)
            k, v = k_ref[rows, :], v_ref[rows, :]
            p = jnp.exp(_dot(q, k, "nt") * _MLA_SCALE - lse_v)
            if mask is not None:
                p = jnp.where(mask, p, 0.0)
            ds = (p * (_dot(do, v, "nt") - delta) * _MLA_SCALE).astype(BF16)
            dk_ref[rows, :] += _dot(ds, q, "tn")
            dv_ref[rows, :] += _dot(p, do, "tn")
            return dq + _dot(ds, k, "nn")

        dq = lax.fori_loop(0, i, lambda c, dq: block(c, dq, None), jnp.zeros((tq, MLA_QK), F32))
        dq_ref[...] = block(i, dq, _mla_diag_mask(tq))

    return pl.pallas_call(
        body,
        name="mla_bwd",
        grid=(MLA_HEADS, s_len // tq),
        in_specs=[
            pl.BlockSpec((None, tq, MLA_QK), lambda h, i: (h, i, 0)),
            pl.BlockSpec((None, s_len, MLA_QK), lambda h, i: (h, 0, 0)),
            pl.BlockSpec((None, s_len, MLA_V), lambda h, i: (h, 0, 0)),
            pl.BlockSpec((tq, MLA_V), lambda h, i: (i, first + h)),
            pl.BlockSpec((tq, MLA_V), lambda h, i: (i, first + h)),
            pl.BlockSpec((None, tq, 1), lambda h, i: (h, i, 0)),
        ],
        out_specs=[
            pl.BlockSpec((None, tq, MLA_QK), lambda h, i: (h, i, 0)),
            pl.BlockSpec((None, s_len, MLA_QK), lambda h, i: (h, 0, 0)),
            pl.BlockSpec((None, s_len, MLA_V), lambda h, i: (h, 0, 0)),
        ],
        out_shape=[
            _sds((MLA_HEADS, s_len, MLA_QK), F32),
            _sds((MLA_HEADS, s_len, MLA_QK), F32),
            _sds((MLA_HEADS, s_len, MLA_V), F32),
        ],
        compiler_params=_cparams(("parallel", "arbitrary")),
    )(q4, k4, v4, cat, dcat, lse)


def _mla_split(pm):
    q_lat = pm[:, :MLA_Q_RANK]
    kv_lat = pm[:, MLA_Q_RANK:MLA_Q_RANK + MLA_KV_RANK]
    kr = pm[:, MLA_Q_RANK + MLA_KV_RANK:MLA_Q_RANK + MLA_KV_RANK + MLA_ROPE]
    kr_sw = pm[:, MLA_Q_RANK + MLA_KV_RANK + MLA_ROPE:]
    return q_lat, kv_lat, kr, kr_sw


def _mla_proj(pm, cos2, sin2, q_norm, kv_norm, wq_ext, wkv):
    s_len = pm.shape[0]

    def fn(pm_, cos_, sin_, qg, kvg, wq, wk):
        q_lat, kv_lat, kr, kr_sw = _mla_split(pm_)
        nq = (_rms(q_lat)[0] * qg).astype(BF16)
        nkv = (_rms(kv_lat)[0] * kvg).astype(BF16)
        k_rot = kr * cos_ + kr_sw * sin_
        qs, ks, vs = [], [], []
        for h in range(MLA_HEADS):
            qe = _dot(nq, wq[h], "nt")
            q_rot = qe[:, MLA_NOPE:MLA_QK] * cos_ + qe[:, MLA_QK:] * sin_
            qs.append(jnp.concatenate([qe[:, :MLA_NOPE], q_rot], axis=-1))
            kve = _dot(nkv, wk[h], "nt")
            ks.append(jnp.concatenate([kve[:, :MLA_NOPE], k_rot], axis=-1))
            vs.append(kve[:, MLA_NOPE:])
        return jnp.stack(qs), jnp.stack(ks), jnp.stack(vs)

    return _rowwise(
        "mla_proj", fn, [pm, cos2, sin2], [q_norm, kv_norm, wq_ext, wkv],
        [_sds((MLA_HEADS, s_len, MLA_QK), BF16), _sds((MLA_HEADS, s_len, MLA_QK), BF16),
         _sds((MLA_HEADS, s_len, MLA_V), BF16)], [], 256,
    )


def _mla_proj_bwd(pm, cos2, sin2, dq4, dk4, dv4, q_norm, kv_norm, wq_ext, wkv):
    s_len = pm.shape[0]

    def fn(pm_, cos_, sin_, dq, dk, dv, qg, kvg, wq, wk):
        q_lat, kv_lat, _, _ = _mla_split(pm_)
        xq, rq = _rms(q_lat)
        xkv, rkv = _rms(kv_lat)
        nq = (xq * qg).astype(BF16)
        nkv = (xkv * kvg).astype(BF16)
        dnq = jnp.zeros_like(q_lat)
        dnkv = jnp.zeros_like(kv_lat)
        dkr = jnp.zeros_like(cos_)
        dwq, dwk = [], []
        for h in range(MLA_HEADS):
            dy = dq[h][:, MLA_NOPE:]
            dqe = jnp.concatenate([dq[h][:, :MLA_NOPE], dy * cos_, dy * sin_], axis=-1).astype(BF16)
            dnq = dnq + _dot(dqe, wq[h], "nn")
            dwq.append(_dot(dqe, nq, "tn"))
            dkve = jnp.concatenate([dk[h][:, :MLA_NOPE], dv[h]], axis=-1).astype(BF16)
            dnkv = dnkv + _dot(dkve, wk[h], "nn")
            dwk.append(_dot(dkve, nkv, "tn"))
            dkr = dkr + dk[h][:, MLA_NOPE:]
        dq_lat = _rms_bwd(dnq * qg, xq, rq)
        dkv_lat = _rms_bwd(dnkv * kvg, xkv, rkv)
        dpm = jnp.concatenate([dq_lat, dkv_lat, dkr * cos_, dkr * sin_], axis=-1)
        return dpm, _sum0(dnq * xq), _sum0(dnkv * xkv), jnp.stack(dwq), jnp.stack(dwk)

    return _rowwise(
        "mla_proj_bwd", fn, [pm, cos2, sin2, dq4, dk4, dv4], [q_norm, kv_norm, wq_ext, wkv],
        [_sds((s_len, N_MLA_COLS), BF16)],
        [_sds((1, MLA_Q_RANK), F32), _sds((1, MLA_KV_RANK), F32),
         _sds(wq_ext.shape, F32), _sds(wkv.shape, F32)], 256,
    )


def _rope_tables(s_len):
    inv = ROPE_THETA ** (-jnp.arange(0, MLA_ROPE, 2, dtype=F32) / MLA_ROPE)
    ang = jnp.arange(s_len, dtype=F32)[:, None] * inv[None, :]
    cos, sin = jnp.cos(ang), jnp.sin(ang)
    return jnp.concatenate([cos, cos], axis=-1), jnp.concatenate([-sin, sin], axis=-1)


def _mix_weights(g_mix):
    rest = g_mix[:, 0]
    w_in_t = rest[:, O_IN:O_IN + R_IN].reshape(D_IN, D_MODEL)
    w_o = rest[:, O_O:O_O + R_O].reshape(D_MODEL, D_MODEL)
    w_uq_t = rest[:, O_UQ:O_UQ + R_UQ].reshape(MLA_HEADS, MLA_QK, MLA_Q_RANK)
    w_ukv_t = rest[:, O_UKV:O_UKV + R_UKV].reshape(MLA_HEADS, MLA_NOPE + MLA_V, MLA_KV_RANK)
    half = MLA_ROPE // 2
    w_swa = w_in_t[:N_SWA_COLS]
    w_mla = jnp.concatenate([w_in_t[N_SWA_COLS:], w_in_t[D_IN - half:], w_in_t[D_IN - MLA_ROPE:D_IN - half]], axis=0)
    wq_ext = jnp.concatenate([w_uq_t, w_uq_t[:, MLA_QK - half:], w_uq_t[:, MLA_NOPE:MLA_QK - half]], axis=1)
    return w_swa, w_mla, w_o, wq_ext, w_ukv_t


def _mix_fwd(x, gamma, sh, sc, gate, mix_src, q_norm, kv_norm, bias, sinkb, cos2, sin2):
    s_len = x.shape[0]
    tm = min(ROW_TILE, s_len)

    def normmod(x_, gamma_, sc_, sh_):
        return _rms(x_)[0] * gamma_ * (1.0 + sc_) + sh_

    deps = mix_src.mid(x)
    (h,) = _rowwise("mix_normmod", normmod, [x], [gamma, sc, sh], [_sds((s_len, D_MODEL), BF16)], [], 256, deps=deps)
    weights = _mix_weights(mix_src.get(h))
    w_swa, w_mla, w_o, wq_ext, wkv = weights
    swa2 = _mm(
        "mix_proj_swa", "nt", (s_len // tm, 1, 1),
        h, (tm, D_MODEL), lambda i, j, k: (i, 0),
        w_swa, (N_SWA_COLS, D_MODEL), lambda i, j, k: (0, 0),
        _sds((s_len, N_SWA_COLS), BF16), (tm, N_SWA_COLS), lambda i, j, k: (i, 0),
        (tm, N_SWA_COLS),
    )
    pm = _mm(
        "mix_proj_mla", "nt", (s_len // tm, 1, 1),
        h, (tm, D_MODEL), lambda i, j, k: (i, 0),
        w_mla, (N_MLA_COLS, D_MODEL), lambda i, j, k: (0, 0),
        _sds((s_len, N_MLA_COLS), F32), (tm, N_MLA_COLS), lambda i, j, k: (i, 0),
        (tm, N_MLA_COLS),
    )
    q4, k4, v4 = _mla_proj(pm, cos2, sin2, q_norm, kv_norm, wq_ext, wkv)
    oa, lse_a = _swa_fwd(swa2, bias, sinkb)
    ob, lse_b = _mla_fwd(q4, k4, v4)
    cat = jnp.concatenate([oa, ob], axis=1)
    z = _mm(
        "mix_out", "nn", (s_len // tm, 1, 1),
        cat, (tm, D_MODEL), lambda i, j, k: (i, 0),
        w_o, (D_MODEL, D_MODEL), lambda i, j, k: (0, 0),
        _sds((s_len, D_MODEL), F32), (tm, D_MODEL), lambda i, j, k: (i, 0),
        (tm, D_MODEL),
    )
    (x_out,) = _rowwise(
        "mix_residual", lambda x_, z_, g_: x_ + g_ * z_, [x, z], [gate], [_sds((s_len, D_MODEL), F32)], [], 256
    )
    return x_out, (weights, h, swa2, pm, q4, k4, v4, cat, lse_a, lse_b, z)


def _mix_bwd(dxo, x, gamma, sc, gate, q_norm, kv_norm, bias, sinkb, cos2, sin2, saved, hooks):
    weights, h, swa2, pm, q4, k4, v4, cat, lse_a, lse_b, z = saved
    w_swa, w_mla, w_o, wq_ext, wkv = weights
    s_len = x.shape[0]
    tm = tk = min(ROW_TILE, s_len)
    vec = _sds((1, D_MODEL), F32)
    n_proj = N_SWA_COLS + N_MLA_COLS
    half_d = D_MODEL // 2

    dz, dgate = _rowwise(
        "mix_bwd_gate", lambda dxo_, z_, g_: (g_ * dxo_, _sum0(z_ * dxo_)),
        [dxo, z], [gate], [_sds((s_len, D_MODEL), BF16)], [vec], 256,
    )
    dw_o = _mm(
        "mix_bwd_wo", "tn", (2, 1, s_len // tk),
        cat, (tk, half_d), lambda i, j, k: (k, i),
        dz, (tk, D_MODEL), lambda i, j, k: (k, 0),
        _sds((D_MODEL, D_MODEL), F32), (half_d, D_MODEL), lambda i, j, k: (i, 0),
        (half_d, D_MODEL),
    )
    dcat = _mm(
        "mix_bwd_dcat", "nt", (s_len // tm, 1, 1),
        dz, (tm, D_MODEL), lambda i, j, k: (i, 0),
        w_o, (D_MODEL, D_MODEL), lambda i, j, k: (0, 0),
        _sds((s_len, D_MODEL), BF16), (tm, D_MODEL), lambda i, j, k: (i, 0),
        (tm, D_MODEL), deps=hooks.after_gate(dz),
    )
    dq_a, dkva, dkvb, dbias, dsink = _swa_bwd(swa2, bias, sinkb, cat, dcat, lse_a)
    dq4, dk4, dv4 = _mla_bwd(q4, k4, v4, cat, dcat, lse_b)
    dpm, dq_norm, dkv_norm, dwq_ext, dwkv = _mla_proj_bwd(pm, cos2, sin2, dq4, dk4, dv4, q_norm, kv_norm, wq_ext, wkv)

    dkv = dkva + jnp.concatenate([dkvb[WINDOW:], jnp.zeros_like(dkvb[:WINDOW])], axis=0)
    dproj = jnp.concatenate([dq_a, dkv.astype(BF16), dpm], axis=1)
    w_proj = jnp.concatenate([w_swa, w_mla], axis=0)

    dw_proj = _mm(
        "mix_bwd_win", "tn", (n_proj // 256, 1, s_len // tk),
        dproj, (tk, 256), lambda i, j, k: (k, i),
        h, (tk, D_MODEL), lambda i, j, k: (k, 0),
        _sds((n_proj, D_MODEL), F32), (256, D_MODEL), lambda i, j, k: (i, 0),
        (256, D_MODEL),
    )
    dw_swa, dw_mla = dw_proj[:N_SWA_COLS], dw_proj[N_SWA_COLS:]
    dh = _mm(
        "mix_bwd_dh", "nn", (s_len // tm, 1, 1),
        dproj, (tm, n_proj), lambda i, j, k: (i, 0),
        w_proj, (n_proj, D_MODEL), lambda i, j, k: (0, 0),
        _sds((s_len, D_MODEL), F32), (tm, D_MODEL), lambda i, j, k: (i, 0),
        (tm, D_MODEL),
    )
    dx, dsc, dsh, dgamma = _normmod_bwd_call("mix_bwd_normmod", [dh], x, dxo, gamma, sc)

    half = MLA_ROPE // 2
    n_mla_in = D_IN - N_SWA_COLS
    dw_mla_base = dw_mla[:n_mla_in]
    dw_mla_base = dw_mla_base.at[n_mla_in - half:].add(dw_mla[n_mla_in:n_mla_in + half])
    dw_mla_base = dw_mla_base.at[n_mla_in - MLA_ROPE:n_mla_in - half].add(dw_mla[n_mla_in + half:])
    dw_in_t = jnp.concatenate([dw_swa, dw_mla_base], axis=0)
    dw_uq_t = dwq_ext[:, :MLA_QK]
    dw_uq_t = dw_uq_t.at[:, MLA_QK - half:].add(dwq_ext[:, MLA_QK:MLA_QK + half])
    dw_uq_t = dw_uq_t.at[:, MLA_NOPE:MLA_QK - half].add(dwq_ext[:, MLA_QK + half:])
    rest = jnp.concatenate(
        [
            dw_in_t.reshape(N_CHIPS, R_IN, D_MODEL),
            dw_o.reshape(N_CHIPS, R_O, D_MODEL),
            dw_uq_t.reshape(N_CHIPS, R_UQ, D_MODEL),
            dwkv.reshape(N_CHIPS, R_UKV, D_MODEL),
            jnp.zeros((N_CHIPS, F_SHARD - O_PAD, D_MODEL), F32),
        ],
        axis=1,
    ).astype(BF16)
    return dx, rest, (dsh, dsc, dgate, dgamma), dq_norm, dkv_norm, dbias, dsink


def _device_step(x, target, mod, gu1_src, down1_src, mix_src, ffn2_src, norms, q_norm, kv_norm, sinks, rel_bias,
                 hooks2=None, hooks_mix=None, mix_done=None, hooks1=None):
    s_len = x.shape[0]
    sh1, sc1, g1, sh2, sc2, g2, sh3, sc3, g3 = [mod[:, i * D_MODEL:(i + 1) * D_MODEL] for i in range(N_MOD)]
    n1, n2, n3, nf = norms
    bkt = jnp.asarray(_bucket_map())
    bias = _bias_expand(rel_bias.T, bkt).reshape(SWA_HEADS, WINDOW, 2 * WINDOW)
    sinkb = jnp.broadcast_to(sinks.reshape(SWA_HEADS, 1, 1), (SWA_HEADS, WINDOW, 1))
    cos2, sin2 = _rope_tables(s_len)

    x1, saved1 = _ffn_fwd("ffn1", x, n1, sh1, sc1, g1, gu1_src, 0, down1_src, 0, sh1)
    x2, saved2 = _mix_fwd(x1, n2, sh2, sc2, g2, mix_src, q_norm, kv_norm, bias, sinkb, cos2, sin2)
    x3, saved3 = _ffn_fwd("ffn2", x2, n3, sh3, sc3, g3, ffn2_src, 0, None, 2, x2)

    def head(x_, t_, g_):
        xr, r = _rms(x_)
        err = xr * g_ - t_
        dy = err * (1.0 / D_MODEL)
        return _rms_bwd(dy * g_, xr, r), _sum0(err * err), _sum0(dy * xr)

    vec = _sds((1, D_MODEL), F32)
    dx3, sq, dnf = _rowwise("loss_head", head, [x3, target], [nf], [_sds((s_len, D_MODEL), F32)], [vec, vec], 256)
    loss_part = (0.5 / D_MODEL) * jnp.sum(sq)

    dx2, gb2, (dsh3, dsc3, dg3, dn3) = _ffn_bwd("ffn2", dx3, x2, n3, sc3, g3, saved3, hooks2 or _BwdHooks())
    dx1, rest, (dsh2, dsc2, dg2, dn2), dq_norm, dkv_norm, dbias, dsink = _mix_bwd(
        dx2, x1, n2, sc2, g2, q_norm, kv_norm, bias, sinkb, cos2, sin2, saved2, hooks_mix or _BwdHooks()
    )
    rest = rest[:, None]
    deps1 = mix_done(dx1, rest) if mix_done is not None else []
    dx0, gb1, (dsh1, dsc1, dg1, dn1) = _ffn_bwd(
        "ffn1", dx1, x, n1, sc1, g1, saved1, hooks1 or _BwdHooks(), deps=deps1
    )

    dmod = jnp.concatenate([dsh1, dsc1, dg1, dsh2, dsc2, dg2, dsh3, dsc3, dg3], axis=-1)
    drel = _bias_reduce(dbias.reshape(SWA_HEADS, -1), bkt).T
    dsinks = jnp.sum(dsink, axis=(1, 2)).reshape(1, SWA_HEADS)
    small = (dn1, dn2, dn3, dnf, dq_norm, dkv_norm, dsinks, drel)
    return loss_part, dx0, (gb1, gb2, rest), dmod, small


_MESH = pl.DeviceIdType.MESH


def _place():
    return lax.axis_index("x"), lax.axis_index("y"), lax.axis_index("c")


def _other_chips(x, y):
    return [(1 - x, y), (x, 1 - y), (1 - x, 1 - y)]


def _allgather_small(name, blk):
    m_per, n = blk.shape

    def body(x_ref, out_ref, send_sems, recv_sems, local_sem):
        x, y, c = _place()
        me, sibling = (x, y, c), (x, y, 1 - c)
        chips = _other_chips(x, y)

        def rows(px, py, pc):
            return out_ref.at[pl.ds((4 * px + 2 * py + pc) * m_per, m_per), :]

        def copy(k, block, to, src=None):
            return pltpu.make_async_remote_copy(
                src_ref=rows(*block) if src is None else src, dst_ref=rows(*block),
                send_sem=send_sems.at[k], recv_sem=recv_sems.at[k], device_id=to, device_id_type=_MESH,
            )

        mine = pltpu.make_async_copy(x_ref, rows(*me), local_sem)
        mine.start()
        first = [copy(0, me, sibling, src=x_ref)]
        first += [copy(1 + j, me, (*chip, c), src=x_ref) for j, chip in enumerate(chips)]
        for cp in first:
            cp.start()
        passed = [copy(4 + j, (*chip, c), sibling) for j, chip in enumerate(chips)]
        for j, chip in enumerate(chips):
            copy(1 + j, (*chip, c), me).wait_recv()
            passed[j].start()
        copy(0, sibling, me).wait_recv()
        for j, chip in enumerate(chips):
            copy(4 + j, (*chip, 1 - c), me).wait_recv()
        for cp in first + passed:
            cp.wait_send()
        mine.wait()

    return pl.pallas_call(
        body,
        name=name,
        out_shape=_sds((N_DEV * m_per, n), blk.dtype),
        in_specs=[pl.BlockSpec(memory_space=pltpu.VMEM)],
        out_specs=pl.BlockSpec(memory_space=pltpu.VMEM),
        scratch_shapes=[pltpu.SemaphoreType.DMA((7,)), pltpu.SemaphoreType.DMA((7,)), pltpu.SemaphoreType.DMA],
    )(blk)


def _allgather_weights(name, own):
    n = own.shape[0]

    def body(own_ref, g_ref, token, send_sems, recv_sems, local_sem):
        x, y, c = _place()
        sibling = (x, y, 1 - c)
        chips = _other_chips(x, y)
        mine = pltpu.make_async_copy(own_ref, g_ref.at[2 * x + y], local_sem)
        mine.start()
        for j, chip in enumerate(chips):
            for cp in _gather_sends(n, own_ref, g_ref, send_sems.at[j], recv_sems.at[j], x, y, c, chip):
                cp.start()
        for j, chip in enumerate(chips):
            _gather_all(own_ref, g_ref, send_sems.at[j], recv_sems.at[j], chip, c, c).wait_recv()
            for cp in _gather_forwards(n, g_ref, send_sems.at[3 + j], recv_sems.at[3 + j], chip, c, sibling):
                cp.start()
        for j, chip in enumerate(chips):
            _gather_all(own_ref, g_ref, send_sems.at[3 + j], recv_sems.at[3 + j], chip, 1 - c, c).wait_recv()
        for j, chip in enumerate(chips):
            _gather_all(own_ref, g_ref, send_sems.at[j], recv_sems.at[j], chip, c, c).wait_send()
            _gather_all(own_ref, g_ref, send_sems.at[3 + j], recv_sems.at[3 + j], chip, c, c).wait_send()
        mine.wait()
        token[...] = jnp.zeros_like(token)

    return pl.pallas_call(
        body,
        name=name,
        out_shape=[_sds((N_CHIPS,) + own.shape, own.dtype), _sds((8, 128), F32)],
        in_specs=[pl.BlockSpec(memory_space=pl.ANY)],
        out_specs=[pl.BlockSpec(memory_space=pl.ANY), pl.BlockSpec(memory_space=pltpu.VMEM)],
        scratch_shapes=[pltpu.SemaphoreType.DMA((6,)), pltpu.SemaphoreType.DMA((6,)), pltpu.SemaphoreType.DMA],
    )(own)


def _gather_sends(n, own_ref, g_ref, send_sem, recv_sem, x, y, c, chip):
    return [
        pltpu.make_async_remote_copy(
            src_ref=own_ref.at[p, pl.ds(c * HALF, HALF), :], dst_ref=g_ref.at[2 * x + y, p, pl.ds(c * HALF, HALF), :],
            send_sem=send_sem, recv_sem=recv_sem, device_id=(*chip, c), device_id_type=_MESH,
        )
        for p in range(n)
    ]


def _gather_forwards(n, g_ref, send_sem, recv_sem, chip, c, sibling):
    return [
        pltpu.make_async_remote_copy(
            src_ref=g_ref.at[2 * chip[0] + chip[1], p, pl.ds(c * HALF, HALF), :],
            dst_ref=g_ref.at[2 * chip[0] + chip[1], p, pl.ds(c * HALF, HALF), :],
            send_sem=send_sem, recv_sem=recv_sem, device_id=sibling, device_id_type=_MESH,
        )
        for p in range(n)
    ]


def _gather_all(own_ref, g_ref, send_sem, recv_sem, chip, half, c):
    return pltpu.make_async_remote_copy(
        src_ref=own_ref.at[:, pl.ds(c * HALF, HALF), :],
        dst_ref=g_ref.at[2 * chip[0] + chip[1], :, pl.ds(half * HALF, HALF), :],
        send_sem=send_sem, recv_sem=recv_sem, device_id=(*chip, c), device_id_type=_MESH,
    )


_HBM_SPEC = pl.BlockSpec(memory_space=pltpu.HBM)
_SEM_SPEC = pl.BlockSpec(memory_space=pltpu.SEMAPHORE)
_ANY_SPEC = pl.BlockSpec(memory_space=pl.ANY)
_VMEM_SPEC = pl.BlockSpec(memory_space=pltpu.VMEM)
_SPLIT_PARAMS = pltpu.CompilerParams(has_side_effects=pltpu.SideEffectType.DATAFLOW_SIDE_EFFECTING)
_TOKEN = jax.ShapeDtypeStruct((8, 128), F32)


def _in_hbm(a):
    return pltpu.with_memory_space_constraint(a, pltpu.HBM)


class _GatherBehind:
    def __init__(self, tag, own, then=None):
        self.tag, self.n, self.own, self.then = tag, own.shape[0], own, then

    def start(self, after):
        tag, own, n = self.tag, self.own, self.n
        x, y, _ = _place()
        g_init = lax.dynamic_update_slice(
            lax.empty((N_CHIPS,) + own.shape, own.dtype), own[None], (2 * x + y, 0, 0, 0)
        )

        def body(own_ref, g_ref, *rest):
            send_sems, recv_sems, _, _, token = rest[len(after):]
            x, y, c = _place()
            for j, chip in enumerate(_other_chips(x, y)):
                for cp in _gather_sends(n, own_ref, g_ref, send_sems.at[j], recv_sems.at[j], x, y, c, chip):
                    cp.start()
            token[...] = jnp.zeros_like(token)

        self.send1, self.recv1, self.own, self.g, self.token = pl.pallas_call(
            body,
            name=f"{tag}_start",
            out_shape=(
                pltpu.SemaphoreType.DMA((3,)), pltpu.SemaphoreType.DMA((3,)),
                pltpu.HBM(own.shape, own.dtype), pltpu.HBM(g_init.shape, g_init.dtype), _TOKEN,
            ),
            in_specs=(_HBM_SPEC, _HBM_SPEC) + (_ANY_SPEC,) * len(after),
            out_specs=(_SEM_SPEC, _SEM_SPEC, _HBM_SPEC, _HBM_SPEC, _VMEM_SPEC),
            input_output_aliases={0: 2, 1: 3},
            compiler_params=_SPLIT_PARAMS,
        )(_in_hbm(own), _in_hbm(g_init), *after)

    def mid(self, after):
        n = self.n

        def body(own_ref, g_ref, send1, recv1, after_ref, send2, recv2, g_out, token):
            x, y, c = _place()
            sibling = (x, y, 1 - c)
            chips = _other_chips(x, y)
            for j, chip in enumerate(chips):
                _gather_all(own_ref, g_ref, send1.at[j], recv1.at[j], chip, c, c).wait_recv()
                for cp in _gather_forwards(n, g_ref, send2.at[j], recv2.at[j], chip, c, sibling):
                    cp.start()
            for j, chip in enumerate(chips):
                _gather_all(own_ref, g_ref, send1.at[j], recv1.at[j], chip, c, c).wait_send()
            token[...] = jnp.zeros_like(token)

        self.send2, self.recv2, self.g, token = pl.pallas_call(
            body,
            name=f"{self.tag}_mid",
            out_shape=(
                pltpu.SemaphoreType.DMA((3,)), pltpu.SemaphoreType.DMA((3,)),
                pltpu.HBM(self.g.shape, self.g.dtype), _TOKEN,
            ),
            in_specs=(_HBM_SPEC, _HBM_SPEC, _SEM_SPEC, _SEM_SPEC, _ANY_SPEC),
            out_specs=(_SEM_SPEC, _SEM_SPEC, _HBM_SPEC, _VMEM_SPEC),
            input_output_aliases={1: 2},
            compiler_params=_SPLIT_PARAMS,
        )(self.own, self.g, self.send1, self.recv1, after)
        if self.then is None:
            return [token]
        self.then.start([token])
        return [token, self.then.token]

    def get(self, after):
        def body(g_ref, send2, recv2, after_ref, g_out):
            x, y, c = _place()
            for j, chip in enumerate(_other_chips(x, y)):
                slot_in = g_ref.at[2 * chip[0] + chip[1], :, pl.ds((1 - c) * HALF, HALF), :]
                slot_out = g_ref.at[2 * chip[0] + chip[1], :, pl.ds(c * HALF, HALF), :]
                cp = pltpu.make_async_remote_copy(
                    src_ref=slot_out, dst_ref=slot_in, send_sem=send2.at[j], recv_sem=recv2.at[j],
                    device_id=(x, y, 1 - c), device_id_type=_MESH,
                )
                cp.wait_send()
                cp.wait_recv()

        return pl.pallas_call(
            body,
            name=f"{self.tag}_wait",
            out_shape=pltpu.HBM(self.g.shape, self.g.dtype),
            in_specs=(_HBM_SPEC, _SEM_SPEC, _SEM_SPEC, _ANY_SPEC),
            out_specs=_HBM_SPEC,
            input_output_aliases={0: 0},
            compiler_params=_SPLIT_PARAMS,
        )(self.g, self.send2, self.recv2, after)


def _pair_exchange(name, gb):
    def body(gb_ref, land_ref, send_sem, recv_sem):
        x, y, c = _place()
        theirs = gb_ref.at[:, :, pl.ds((1 - c) * HALF, HALF), :]
        cp = pltpu.make_async_remote_copy(
            src_ref=theirs, dst_ref=land_ref, send_sem=send_sem, recv_sem=recv_sem,
            device_id=(x, y, 1 - c), device_id_type=_MESH,
        )
        cp.start()
        cp.wait()

    return pl.pallas_call(
        body,
        name=name,
        out_shape=_sds((N_CHIPS, gb.shape[1], HALF, D_MODEL), gb.dtype),
        in_specs=[pl.BlockSpec(memory_space=pl.ANY)],
        out_specs=pl.BlockSpec(memory_space=pl.ANY),
        scratch_shapes=[pltpu.SemaphoreType.DMA, pltpu.SemaphoreType.DMA],
    )(gb)


def _pair_sum(name, gb, land):
    def body(gb_ref, land_ref, o_ref):
        c = lax.axis_index("c")
        mine = gb_ref[pl.ds(pl.multiple_of(c * HALF, 16), HALF), :]
        o_ref[...] = (mine.astype(F32) + land_ref[...].astype(F32)).astype(o_ref.dtype)

    return pl.pallas_call(
        body,
        name=name,
        grid=(N_CHIPS, gb.shape[1]),
        in_specs=[
            pl.BlockSpec((None, None, F_SHARD, D_MODEL), lambda j, p: (j, p, 0, 0)),
            pl.BlockSpec((None, None, HALF, D_MODEL), lambda j, p: (j, p, 0, 0)),
        ],
        out_specs=pl.BlockSpec((None, None, HALF, D_MODEL), lambda j, p: (j, p, 0, 0)),
        out_shape=_sds(land.shape, BF16),
        compiler_params=_cparams(("parallel", "parallel")),
    )(gb, land)


def _chip_exchange(name, part):
    def body(p_ref, land_ref, send_sems, recv_sems, local_sem):
        x, y, c = _place()
        me = 2 * x + y
        chips = _other_chips(x, y)
        mine = pltpu.make_async_copy(p_ref.at[me], land_ref.at[me], local_sem)
        mine.start()

        def copy(k, chip):
            return pltpu.make_async_remote_copy(
                src_ref=p_ref.at[2 * chip[0] + chip[1]], dst_ref=land_ref.at[me],
                send_sem=send_sems.at[k], recv_sem=recv_sems.at[k], device_id=(*chip, c), device_id_type=_MESH,
            )

        sends = [copy(k, chip) for k, chip in enumerate(chips)]
        for cp in sends:
            cp.start()
        for k, chip in enumerate(chips):
            pltpu.make_async_remote_copy(
                src_ref=p_ref.at[me], dst_ref=land_ref.at[2 * chip[0] + chip[1]],
                send_sem=send_sems.at[k], recv_sem=recv_sems.at[k], device_id=(*chip, c), device_id_type=_MESH,
            ).wait_recv()
        for cp in sends:
            cp.wait_send()
        mine.wait()

    return pl.pallas_call(
        body,
        name=name,
        out_shape=_sds(part.shape, part.dtype),
        in_specs=[pl.BlockSpec(memory_space=pl.ANY)],
        out_specs=pl.BlockSpec(memory_space=pl.ANY),
        scratch_shapes=[pltpu.SemaphoreType.DMA((3,)), pltpu.SemaphoreType.DMA((3,)), pltpu.SemaphoreType.DMA],
    )(part)


def _chip_sum_share(name, land):
    n = land.shape[1]

    def body(l_ref, r_ref, buf, send_sems, recv_sems, local_sems):
        p = pl.program_id(0)
        x, y, c = _place()
        acc = l_ref[0].astype(F32)
        for j in range(1, N_CHIPS):
            acc = acc + l_ref[j].astype(F32)
        buf[p] = acc

        def copies(q):
            mine = r_ref.at[q, pl.ds(c * HALF, HALF), :]
            theirs = r_ref.at[q, pl.ds((1 - c) * HALF, HALF), :]
            local = pltpu.make_async_copy(buf.at[q], mine, local_sems.at[q])
            out = pltpu.make_async_remote_copy(
                src_ref=buf.at[q], dst_ref=mine, send_sem=send_sems.at[q], recv_sem=recv_sems.at[q],
                device_id=(x, y, 1 - c), device_id_type=_MESH,
            )
            arrive = pltpu.make_async_remote_copy(
                src_ref=buf.at[q], dst_ref=theirs, send_sem=send_sems.at[q], recv_sem=recv_sems.at[q],
                device_id=(x, y, 1 - c), device_id_type=_MESH,
            )
            return local, out, arrive

        local, out, _ = copies(p)
        local.start()
        out.start()

        @pl.when(p == n - 1)
        def _():
            for q in range(n):
                local, out, arrive = copies(q)
                arrive.wait_recv()
                out.wait_send()
                local.wait()

    return pl.pallas_call(
        body,
        name=name,
        grid=(n,),
        in_specs=[pl.BlockSpec((N_CHIPS, None, HALF, D_MODEL), lambda p: (0, p, 0, 0))],
        out_specs=pl.BlockSpec(memory_space=pl.ANY),
        out_shape=_sds((n, F_SHARD, D_MODEL), F32),
        scratch_shapes=[
            pltpu.VMEM((n, HALF, D_MODEL), F32),
            pltpu.SemaphoreType.DMA((n,)), pltpu.SemaphoreType.DMA((n,)), pltpu.SemaphoreType.DMA((n,)),
        ],
        compiler_params=_cparams(("arbitrary",)),
    )(land)


class _ReduceBehind:
    def __init__(self, tag, gb, after=()):
        self.tag = tag
        n = gb.shape[1]
        land = lax.empty((N_CHIPS, n, HALF, D_MODEL), gb.dtype)

        def body(gb_ref, land_ref, *rest):
            send_sem, recv_sem, _, _, token = rest[len(after):]
            self._pair_copy(gb_ref, land_ref, send_sem, recv_sem).start()
            token[...] = jnp.zeros_like(token)

        self.send, self.recv, self.gb, self.land, self.token = pl.pallas_call(
            body,
            name=f"grads_pair_start_{tag}",
            out_shape=(
                pltpu.SemaphoreType.DMA((1,)), pltpu.SemaphoreType.DMA((1,)),
                pltpu.HBM(gb.shape, gb.dtype), pltpu.HBM(land.shape, land.dtype), _TOKEN,
            ),
            in_specs=(_HBM_SPEC, _HBM_SPEC) + (_ANY_SPEC,) * len(after),
            out_specs=(_SEM_SPEC, _SEM_SPEC, _HBM_SPEC, _HBM_SPEC, _VMEM_SPEC),
            input_output_aliases={0: 2, 1: 3},
            compiler_params=_SPLIT_PARAMS,
        )(_in_hbm(gb), _in_hbm(land), *after)

    @staticmethod
    def _pair_copy(gb_ref, land_ref, send_sem, recv_sem):
        x, y, c = _place()
        return pltpu.make_async_remote_copy(
            src_ref=gb_ref.at[:, :, pl.ds((1 - c) * HALF, HALF), :], dst_ref=land_ref,
            send_sem=send_sem.at[0], recv_sem=recv_sem.at[0], device_id=(x, y, 1 - c), device_id_type=_MESH,
        )

    @staticmethod
    def _chip_copies(p_ref, land_ref, send_sems, recv_sems):
        x, y, c = _place()
        me = 2 * x + y
        sends, arrivals = [], []
        for k, chip in enumerate(_other_chips(x, y)):
            them = 2 * chip[0] + chip[1]
            sends.append(pltpu.make_async_remote_copy(
                src_ref=p_ref.at[them], dst_ref=land_ref.at[me], send_sem=send_sems.at[k], recv_sem=recv_sems.at[k],
                device_id=(*chip, c), device_id_type=_MESH,
            ))
            arrivals.append(pltpu.make_async_remote_copy(
                src_ref=p_ref.at[me], dst_ref=land_ref.at[them], send_sem=send_sems.at[k], recv_sem=recv_sems.at[k],
                device_id=(*chip, c), device_id_type=_MESH,
            ))
        return sends, arrivals

    def mid(self, after):
        tag = self.tag

        def wait_body(gb_ref, land_ref, send_sem, recv_sem, after_ref, gb_out, land_out):
            cp = self._pair_copy(gb_ref, land_ref, send_sem, recv_sem)
            cp.wait_send()
            cp.wait_recv()

        gb, land = pl.pallas_call(
            wait_body,
            name=f"grads_pair_wait_{tag}",
            out_shape=(pltpu.HBM(self.gb.shape, self.gb.dtype), pltpu.HBM(self.land.shape, self.land.dtype)),
            in_specs=(_HBM_SPEC, _HBM_SPEC, _SEM_SPEC, _SEM_SPEC, _ANY_SPEC),
            out_specs=(_HBM_SPEC, _HBM_SPEC),
            input_output_aliases={0: 0, 1: 1},
            compiler_params=_SPLIT_PARAMS,
        )(self.gb, self.land, self.send, self.recv, after)
        part = _pair_sum(f"grads_pair_sum_{tag}", gb, land)

        x, y, _ = _place()
        start = (2 * x + y, 0, 0, 0)
        own = lax.dynamic_slice(part, start, (1,) + part.shape[1:])
        land2 = lax.dynamic_update_slice(lax.empty(part.shape, part.dtype), own, start)

        def start_body(p_ref, land_ref, send_sems, recv_sems, p_out, land_out, token):
            for cp in self._chip_copies(p_ref, land_ref, send_sems, recv_sems)[0]:
                cp.start()
            token[...] = jnp.zeros_like(token)

        self.send2, self.recv2, self.part, self.land2, token = pl.pallas_call(
            start_body,
            name=f"grads_chip_start_{tag}",
            out_shape=(
                pltpu.SemaphoreType.DMA((3,)), pltpu.SemaphoreType.DMA((3,)),
                pltpu.HBM(part.shape, part.dtype), pltpu.HBM(land2.shape, land2.dtype), _TOKEN,
            ),
            in_specs=(_HBM_SPEC, _HBM_SPEC),
            out_specs=(_SEM_SPEC, _SEM_SPEC, _HBM_SPEC, _HBM_SPEC, _VMEM_SPEC),
            input_output_aliases={0: 2, 1: 3},
            compiler_params=_SPLIT_PARAMS,
        )(_in_hbm(part), _in_hbm(land2))
        return [token]

    def get(self, after):
        n_after = len(after)

        def wait_body(p_ref, land_ref, send_sems, recv_sems, *rest):
            sends, arrivals = self._chip_copies(p_ref, land_ref, send_sems, recv_sems)
            for cp in arrivals:
                cp.wait_recv()
            for cp in sends:
                cp.wait_send()

        _, land2 = pl.pallas_call(
            wait_body,
            name=f"grads_chip_wait_{self.tag}",
            out_shape=(pltpu.HBM(self.part.shape, self.part.dtype), pltpu.HBM(self.land2.shape, self.land2.dtype)),
            in_specs=(_HBM_SPEC, _HBM_SPEC, _SEM_SPEC, _SEM_SPEC) + (_ANY_SPEC,) * n_after,
            out_specs=(_HBM_SPEC, _HBM_SPEC),
            input_output_aliases={0: 0, 1: 1},
            compiler_params=_SPLIT_PARAMS,
        )(self.part, self.land2, self.send2, self.recv2, *after)
        return _chip_sum_share(f"grads_chip_sum_share_{self.tag}", land2)


def _allreduce_small(blk):
    gathered = _allgather_small("allgather_small_grads", blk).reshape(N_DEV, PK_ROWS, 128)

    def body(g_ref, o_ref):
        acc = g_ref[0]
        for k in range(1, N_DEV):
            acc = acc + g_ref[k]
        o_ref[...] = acc

    total = pl.pallas_call(body, name="small_grads_sum", out_shape=_sds((PK_ROWS, 128), F32))(gathered)
    return gathered, total


def _adamw(name, w, g, m, v):
    rows, cols = w.shape
    tm = rows
    while tm * cols * 4 > (1 << 20) and tm % 16 == 0:
        tm //= 2

    def fn(w_, g_, m_, v_):
        m_new = ADAM_B1 * m_ + (1.0 - ADAM_B1) * g_
        v_new = ADAM_B2 * v_ + (1.0 - ADAM_B2) * (g_ * g_)
        m_hat = m_new / (1.0 - ADAM_B1 ** ADAM_STEP)
        v_hat = v_new / (1.0 - ADAM_B2 ** ADAM_STEP)
        delta = -ADAM_LR * (m_hat / (jnp.sqrt(v_hat) + ADAM_EPS) + ADAM_WD * w_)
        return delta, m_new, v_new

    out = _sds(w.shape, F32)
    return _rowwise(name, fn, [w, g, m, v], [], [out, out, out], [], tm)


def _pack_small(b_mod_like, n1, n2, n3, nf, qn, kvn, sinks, rel):
    parts = [
        b_mod_like.reshape(PK_MOD, 128),
        n1.reshape(8, 128), n2.reshape(8, 128), n3.reshape(8, 128), nf.reshape(8, 128),
        qn.reshape(2, 128), kvn.reshape(1, 128),
        jnp.pad(sinks.reshape(1, SWA_HEADS), ((0, 0), (0, 128 - SWA_HEADS))),
        rel.reshape(2, 128),
        jnp.zeros((2, 128), F32),
    ]
    return jnp.concatenate(parts, axis=0)


def _unpack_small(p):
    o = PK_MOD
    return (
        p[:o].reshape(1, N_MOD * D_MODEL),
        p[o:o + 8].reshape(1, D_MODEL), p[o + 8:o + 16].reshape(1, D_MODEL),
        p[o + 16:o + 24].reshape(1, D_MODEL), p[o + 24:o + 32].reshape(D_MODEL),
        p[o + 32:o + 34].reshape(1, MLA_Q_RANK), p[o + 34:o + 35].reshape(1, MLA_KV_RANK),
        p[o + 35:o + 36, :SWA_HEADS].reshape(1, SWA_HEADS),
        p[o + 36:o + 38].reshape(NUM_BUCKETS, SWA_HEADS),
    )


def kernel(x, c, w_mod, b_mod, norm_ffn1, ffn1_gate, ffn1_up, ffn1_down, norm_mix, w_in, q_norm, kv_norm, w_uq, w_ukv, sinks, w_o, norm_ffn2, ffn2_gate, ffn2_up, ffn2_down, rel_bias, norm_final, loss_target, m_w_mod, m_b_mod, m_norm_ffn1, m_ffn1_gate, m_ffn1_up, m_ffn1_down, m_norm_mix, m_w_in, m_q_norm, m_kv_norm, m_w_uq, m_w_ukv, m_sinks, m_w_o, m_norm_ffn2, m_ffn2_gate, m_ffn2_up, m_ffn2_down, m_rel_bias, m_norm_final, v_w_mod, v_b_mod, v_norm_ffn1, v_ffn1_gate, v_ffn1_up, v_ffn1_down, v_norm_mix, v_w_in, v_q_norm, v_kv_norm, v_w_uq, v_w_ukv, v_sinks, v_w_o, v_norm_ffn2, v_ffn2_gate, v_ffn2_up, v_ffn2_down, v_rel_bias, v_norm_final):
    ax, ay, ac = _place()
    chip = 2 * ax + ay
    dev = 2 * chip + ac
    n_mod_shard = N_MOD * D_MODEL // N_CHIPS

    def t16(w):
        return w[0].T.astype(BF16)

    rest = jnp.concatenate(
        [
            t16(w_in),
            w_o[0].astype(BF16),
            t16(w_uq).reshape(R_UQ, D_MODEL),
            t16(w_ukv).reshape(R_UKV, D_MODEL),
            jnp.zeros((F_SHARD - O_PAD, D_MODEL), BF16),
        ],
        axis=0,
    )
    own_gu1 = jnp.stack([t16(ffn1_gate), t16(ffn1_up)])
    own_down1 = ffn1_down[0].astype(BF16)[None]
    own_mix = rest[None]
    own_ffn2 = jnp.stack([t16(ffn2_gate), t16(ffn2_up), ffn2_down[0].astype(BF16)])

    (c_act,) = _rowwise(
        "silu_c", lambda v: v * _sigmoid(v), [c.reshape(8, 128)], [], [_sds((8, 128), F32)], [], 8
    )
    c_all = _allgather_small("allgather_c", c_act).reshape(N_DEV, D_MODEL)
    mod_cols = _mm(
        "mod_fwd", "nn", (1, n_mod_shard // 768, 1),
        c_all, (N_DEV, D_MODEL), lambda i, j, k: (0, 0),
        w_mod[0], (D_MODEL, 768), lambda i, j, k: (0, j),
        _sds((N_DEV, n_mod_shard), F32), (N_DEV, 768), lambda i, j, k: (0, j),
        (N_DEV, 768),
    )
    mod_all = _allgather_small("allgather_mod", mod_cols).reshape(N_CHIPS, 2, N_DEV, n_mod_shard)[:, 0]
    mod_all = mod_all.transpose(1, 0, 2).reshape(N_DEV, N_MOD * D_MODEL)
    mod = lax.dynamic_slice_in_dim(mod_all, dev, 1, axis=0) + b_mod

    norms = (norm_ffn1, norm_mix, norm_ffn2, norm_final.reshape(1, D_MODEL))

    ffn2_src = _GatherBehind("gather_ffn2", own_ffn2)
    mix_src = _GatherBehind("gather_mix", own_mix, then=ffn2_src)
    down1_src = _GatherBehind("gather_down1", own_down1, then=mix_src)
    gu1_src = _GatherBehind("gather_gu1", own_gu1, then=down1_src)
    gu1_src.start([mod_all])

    reducing, red = {}, {}

    def begin(tag, gb, after):
        reducing[tag] = _ReduceBehind(tag, gb, after=after)
        return [reducing[tag].token]

    def ffn2_gu_done(gb):
        return begin("ffn2_gu", gb, reducing["ffn2_down"].mid(gb))

    def mix_done(dx1, gb_rest):
        red["ffn2_down"] = reducing["ffn2_down"].get([dx1])
        red["ffn2_gu"] = reducing["ffn2_gu"].get([red["ffn2_down"]])
        return begin("rest", gb_rest, [red["ffn2_gu"]])

    def ffn1_gu_done(gb):
        red["rest"] = reducing["rest"].get([gb])
        begin("ffn1_gu", gb, reducing["ffn1_down"].mid(red["rest"]))
        return reducing["ffn1_gu"].mid(reducing["ffn1_gu"].token)

    loss_part, grad_x, _, dmod, small = _device_step(
        x[0], loss_target[0], mod, gu1_src, down1_src, mix_src, ffn2_src, norms, q_norm, kv_norm, sinks, rel_bias,
        hooks2=_BwdHooks(after_wdown=lambda gb: begin("ffn2_down", gb, ()), after_wgu=ffn2_gu_done),
        hooks_mix=_BwdHooks(after_gate=lambda dz: reducing["ffn2_gu"].mid(dz)),
        mix_done=mix_done,
        hooks1=_BwdHooks(
            after_swiglu=lambda dab3: reducing["rest"].mid(dab3),
            after_wdown=lambda gb: begin("ffn1_down", gb, ()),
            after_wgu=ffn1_gu_done,
        ),
    )
    loss = lax.psum(loss_part, ("x", "y", "c"))

    gathered, total = _allreduce_small(_pack_small(dmod, *small))
    (g_b_mod, g_n1, g_n2, g_n3, g_nf, g_qn, g_kvn, g_sinks, g_rel) = _unpack_small(total)
    dmod_all = gathered[:, :PK_MOD].reshape(N_DEV, N_MOD * D_MODEL)
    dmod_cols = lax.dynamic_slice_in_dim(dmod_all, chip * n_mod_shard, n_mod_shard, axis=1)
    g_w_mod = _mm(
        "mod_bwd", "tn", (1, n_mod_shard // 768, 1),
        c_all, (N_DEV, D_MODEL), lambda i, j, k: (0, 0),
        dmod_cols, (N_DEV, 768), lambda i, j, k: (0, j),
        _sds((D_MODEL, n_mod_shard), F32), (D_MODEL, 768), lambda i, j, k: (0, j),
        (D_MODEL, 768),
    )

    r_rest = red["rest"][0]
    transposed = {"ffn1_gate", "ffn1_up", "ffn2_gate", "ffn2_up", "w_in", "w_uq", "w_ukv"}
    g_big = {
        "ffn2_gate": red["ffn2_gu"][0], "ffn2_up": red["ffn2_gu"][1], "ffn2_down": red["ffn2_down"][0],
        "w_in": r_rest[O_IN:O_IN + R_IN],
        "w_o": r_rest[O_O:O_O + R_O],
        "w_uq": r_rest[O_UQ:O_UQ + R_UQ].reshape(MLA_QK, MLA_Q_RANK),
        "w_ukv": r_rest[O_UKV:O_UKV + R_UKV].reshape(MLA_NOPE + MLA_V, MLA_KV_RANK),
        "w_mod": g_w_mod,
    }
    w_big = {
        "ffn2_gate": (ffn2_gate, m_ffn2_gate, v_ffn2_gate),
        "ffn2_up": (ffn2_up, m_ffn2_up, v_ffn2_up), "ffn2_down": (ffn2_down, m_ffn2_down, v_ffn2_down),
        "w_in": (w_in, m_w_in, v_w_in), "w_o": (w_o, m_w_o, v_w_o), "w_uq": (w_uq, m_w_uq, v_w_uq),
        "w_ukv": (w_ukv, m_w_ukv, v_w_ukv), "w_mod": (w_mod, m_w_mod, v_w_mod),
    }
    grads, deltas, new_m, new_v = {}, {}, {}, {}

    def update(names):
        for nm in names:
            w, m, v = w_big[nm]
            if nm in transposed:
                outs = _adamw(f"adamw_{nm}", w[0].T, g_big[nm], m[0].T, v[0].T)
                g_, d_, m_, v_ = [a.T for a in (g_big[nm],) + tuple(outs)]
            else:
                g_, (d_, m_, v_) = g_big[nm], _adamw(f"adamw_{nm}", w[0], g_big[nm], m[0], v[0])
            grads[nm], deltas[nm], new_m[nm], new_v[nm] = g_[None], d_[None], m_[None], v_[None]

    update(list(w_big))
    red1_down = reducing["ffn1_down"].get([deltas[nm] for nm in w_big])
    red1_gu = reducing["ffn1_gu"].get([red1_down])
    g_big.update({"ffn1_gate": red1_gu[0], "ffn1_up": red1_gu[1], "ffn1_down": red1_down[0]})
    w_big.update({
        "ffn1_gate": (ffn1_gate, m_ffn1_gate, v_ffn1_gate), "ffn1_up": (ffn1_up, m_ffn1_up, v_ffn1_up),
        "ffn1_down": (ffn1_down, m_ffn1_down, v_ffn1_down),
    })
    update(["ffn1_gate", "ffn1_up", "ffn1_down"])

    small_names = ["b_mod", "norm_ffn1", "norm_mix", "norm_ffn2", "norm_final", "q_norm", "kv_norm", "sinks", "rel_bias"]
    w_small = (b_mod, norm_ffn1, norm_mix, norm_ffn2, norm_final, q_norm, kv_norm, sinks, rel_bias)
    m_small = (m_b_mod, m_norm_ffn1, m_norm_mix, m_norm_ffn2, m_norm_final, m_q_norm, m_kv_norm, m_sinks, m_rel_bias)
    v_small = (v_b_mod, v_norm_ffn1, v_norm_mix, v_norm_ffn2, v_norm_final, v_q_norm, v_kv_norm, v_sinks, v_rel_bias)
    d_p, m_p, v_p = _adamw("adamw_small", _pack_small(*w_small), total, _pack_small(*m_small), _pack_small(*v_small))
    for nm, g_, d_, m_, v_ in zip(
        small_names,
        (g_b_mod, g_n1, g_n2, g_n3, g_nf, g_qn, g_kvn, g_sinks, g_rel),
        _unpack_small(d_p), _unpack_small(m_p), _unpack_small(v_p),
    ):
        grads[nm], deltas[nm], new_m[nm], new_v[nm] = g_, d_, m_, v_

    order = ["w_mod", "b_mod", "norm_ffn1", "ffn1_gate", "ffn1_up", "ffn1_down", "norm_mix", "w_in", "q_norm", "kv_norm",
             "w_uq", "w_ukv", "sinks", "w_o", "norm_ffn2", "ffn2_gate", "ffn2_up", "ffn2_down", "rel_bias", "norm_final"]
    return (loss, grad_x[None], *[grads[n] for n in order], *[deltas[n] for n in order],
            *[new_m[n] for n in order], *[new_v[n] for n in order])
```

```python
import functools
import math

import jax
import jax.numpy as jnp
import numpy as np
from jax import lax
from jax.experimental import pallas as pl
from jax.experimental.pallas import tpu as pltpu

F32, BF16 = jnp.float32, jnp.bfloat16

D_MODEL = 1024
D_FF = 2816
EPS = 1e-6
N_MOD = 9
SWA_HEADS, SWA_KV_HEADS, SWA_HEAD_DIM, WINDOW = 8, 2, 64, 128
SWA_GROUP = SWA_HEADS // SWA_KV_HEADS
MLA_HEADS, MLA_Q_RANK, MLA_KV_RANK, MLA_NOPE, MLA_ROPE, MLA_V = 4, 256, 128, 128, 64, 128
MLA_QK = MLA_NOPE + MLA_ROPE
ROPE_THETA = 10000.0
NUM_BUCKETS, MAX_DISTANCE = 32, 128
D_IN = 1216
N_SWA_COLS = 768
N_MLA_COLS = 512

ADAM_LR, ADAM_B1, ADAM_B2, ADAM_EPS, ADAM_WD, ADAM_STEP = 0.001, 0.9, 0.999, 1e-08, 0.01, 10

N_CHIPS = 4
N_DEV = 8
F_SHARD = D_FF // N_CHIPS
HALF = F_SHARD // 2
R_IN, R_O, R_UQ, R_UKV = 304, 256, 48, 32
O_IN, O_O, O_UQ, O_UKV, O_PAD = 0, 304, 560, 608, 640

PK_MOD = 72
PK_ROWS = 112
VMEM_LIMIT = 56 << 20
ROW_TILE = 2048

_DN = {
    "nn": (((1,), (0,)), ((), ())),
    "nt": (((1,), (1,)), ((), ())),
    "tn": (((0,), (0,)), ((), ())),
}


def _sds(shape, dtype):
    return jax.ShapeDtypeStruct(tuple(shape), dtype)


def _cparams(sem=None):
    kw = {"vmem_limit_bytes": VMEM_LIMIT}
    if sem is not None:
        kw["dimension_semantics"] = sem
    return pltpu.CompilerParams(**kw)


def _dot(a, b, dims):
    return lax.dot_general(a.astype(BF16), b.astype(BF16), _DN[dims], preferred_element_type=F32)


def _mm(name, dims, grid, a, a_blk, a_idx, b, b_blk, b_idx, out, out_blk, out_idx, acc_shape, alias=None, deps=()):
    nk = grid[2]

    def body(*refs):
        a_ref, b_ref = refs[:2]
        o_ref, acc_ref = refs[-2:]
        part = _dot(a_ref[...], b_ref[...], dims)
        if nk == 1:
            o_ref[...] = part.astype(o_ref.dtype)
            return
        k = pl.program_id(2)

        @pl.when(k == 0)
        def _():
            acc_ref[...] = part

        @pl.when((k > 0) & (k < nk - 1))
        def _():
            acc_ref[...] += part

        @pl.when(k == nk - 1)
        def _():
            o_ref[...] = (acc_ref[...] + part).astype(o_ref.dtype)

    in_specs = [pl.BlockSpec(a_blk, a_idx), pl.BlockSpec(b_blk, b_idx)]
    args = [a, b]
    kw = {}
    if alias is not None:
        in_specs.append(pl.BlockSpec(memory_space=pl.ANY))
        args.append(alias)
        kw["input_output_aliases"] = {2: 0}
    in_specs += [pl.BlockSpec(memory_space=pl.ANY)] * len(deps)
    args += list(deps)
    return pl.pallas_call(
        body,
        name=name,
        grid=grid,
        in_specs=in_specs,
        out_specs=pl.BlockSpec(out_blk, out_idx),
        out_shape=out,
        scratch_shapes=[pltpu.VMEM(acc_shape if nk > 1 else (8, 128), F32)],
        compiler_params=_cparams(("parallel", "parallel", "arbitrary")),
        **kw,
    )(*args)


def _rowwise(name, fn, tiled, full, out_tiled, out_red, tm, deps=()):
    rows = tiled[0].shape[-2]
    grid = (rows // tm,)
    n_in = len(tiled) + len(full)
    n_t = len(out_tiled)
    n_dep = len(deps)

    def tspec(shape):
        nd = len(shape)
        return pl.BlockSpec(tuple(shape[:-2]) + (tm, shape[-1]), lambda i, nd=nd: (0,) * (nd - 2) + (i, 0))

    def fspec(shape):
        nd = len(shape)
        return pl.BlockSpec(tuple(shape), lambda i, nd=nd: (0,) * nd)

    def body(*refs):
        outs = fn(*[r[...] for r in refs[:n_in]])
        if not isinstance(outs, (tuple, list)):
            outs = (outs,)
        out_refs = refs[n_in + n_dep:]
        for r, v in zip(out_refs[:n_t], outs[:n_t]):
            r[...] = v.astype(r.dtype)
        red_refs = out_refs[n_t:]
        if red_refs:
            @pl.when(pl.program_id(0) == 0)
            def _():
                for r in red_refs:
                    r[...] = jnp.zeros_like(r)

            for r, v in zip(red_refs, outs[n_t:]):
                r[...] += v.astype(r.dtype)

    res = pl.pallas_call(
        body,
        name=name,
        grid=grid,
        in_specs=[tspec(a.shape) for a in tiled] + [fspec(a.shape) for a in full]
        + [pl.BlockSpec(memory_space=pl.ANY)] * n_dep,
        out_specs=[tspec(o.shape) for o in out_tiled] + [fspec(o.shape) for o in out_red],
        out_shape=list(out_tiled) + list(out_red),
        compiler_params=_cparams(("arbitrary",) if out_red else ("parallel",)),
    )(*tiled, *full, *deps)
    return res


def _sum0(v):
    return jnp.sum(v, axis=0, keepdims=True)


def _sigmoid(v):
    return 1.0 / (1.0 + jnp.exp(-v))


def _rms(x):
    r = lax.rsqrt(jnp.mean(x * x, axis=-1, keepdims=True) + EPS)
    return x * r, r


def _rms_bwd(u, xr, r):
    return r * (u - xr * jnp.mean(u * xr, axis=-1, keepdims=True))


class _Ready:
    def __init__(self, g):
        self.g = g

    def mid(self, after):
        return []

    def get(self, after):
        return self.g


def _ffn_fwd(tag, x, gamma, sh, sc, gate, gu_src, base, down_src, down_idx, mid_after):
    s_len = x.shape[0]
    deps = gu_src.mid(mid_after)

    def normmod(x_, gamma_, sc_, sh_):
        xr, _ = _rms(x_)
        return xr * gamma_ * (1.0 + sc_) + sh_

    (h,) = _rowwise(f"{tag}_normmod", normmod, [x], [gamma, sc, sh], [_sds((s_len, D_MODEL), BF16)], [], 256, deps=deps)
    g4 = gu_src.get(h)

    tm = min(ROW_TILE, s_len)
    ab3 = _mm(
        f"{tag}_gate_up", "nt", (s_len // tm, 2 * N_CHIPS, 1),
        h, (tm, D_MODEL), lambda i, j, k: (i, 0),
        g4, (None, None, F_SHARD, D_MODEL), lambda i, j, k: (j % N_CHIPS, base + j // N_CHIPS, 0, 0),
        _sds((2 * N_CHIPS, s_len, F_SHARD), BF16), (None, tm, F_SHARD), lambda i, j, k: (j, i, 0),
        (tm, F_SHARD),
    )

    def swiglu(ab):
        a = ab[:N_CHIPS].astype(F32)
        b = ab[N_CHIPS:].astype(F32)
        return a * _sigmoid(a) * b

    (s3,) = _rowwise(
        f"{tag}_swiglu", swiglu, [ab3], [], [_sds((N_CHIPS, s_len, F_SHARD), BF16)], [], 128,
        deps=down_src.mid(ab3) if down_src is not None else (),
    )
    g_down = down_src.get(s3) if down_src is not None else g4

    y = _mm(
        f"{tag}_down", "nn", (s_len // tm, 1, N_CHIPS),
        s3, (None, tm, F_SHARD), lambda i, j, k: (k, i, 0),
        g_down, (None, None, F_SHARD, D_MODEL), lambda i, j, k: (k, down_idx, 0, 0),
        _sds((s_len, D_MODEL), F32), (tm, D_MODEL), lambda i, j, k: (i, 0),
        (tm, D_MODEL),
    )

    (x_out,) = _rowwise(
        f"{tag}_residual", lambda x_, y_, g_: x_ + 0.5 * g_ * y_, [x, y], [gate], [_sds((s_len, D_MODEL), F32)], [], 256
    )
    return x_out, (g4, base, g_down, down_idx, h, ab3, s3, y)


def _normmod_bwd_call(name, dh_parts, x, dxo, gamma, sc, deps=()):
    s_len = x.shape[0]
    n_parts = len(dh_parts)

    def fn(*vals):
        dh = vals[0]
        for extra in vals[1:n_parts]:
            dh = dh + extra
        x_, dxo_, gamma_, sc_ = vals[n_parts:]
        xr, r = _rms(x_)
        n = xr * gamma_
        dn = dh * (1.0 + sc_)
        dx = dxo_ + _rms_bwd(dn * gamma_, xr, r)
        return dx, _sum0(dh * n), _sum0(dh), _sum0(dn * xr)

    vec = _sds((1, D_MODEL), F32)
    return _rowwise(
        name, fn, list(dh_parts) + [x, dxo], [gamma, sc], [_sds((s_len, D_MODEL), F32)], [vec, vec, vec], 256, deps=deps
    )


class _BwdHooks:
    def __init__(self, after_gate=None, after_swiglu=None, after_wdown=None, after_wgu=None, after_dh=None):
        def nothing(_):
            return []

        self.after_gate = after_gate or nothing
        self.after_swiglu = after_swiglu or nothing
        self.after_wdown = after_wdown or nothing
        self.after_wgu = after_wgu or nothing
        self.after_dh = after_dh or nothing


def _ffn_bwd(tag, dxo, x, gamma, sc, gate, saved, hooks, deps=()):
    g4, base, g_down, down_idx, h, ab3, s3, y = saved
    s_len = x.shape[0]
    vec = _sds((1, D_MODEL), F32)

    dy, dgate = _rowwise(
        f"{tag}_bwd_gate", lambda dxo_, y_, g_: (0.5 * g_ * dxo_, _sum0(0.5 * y_ * dxo_)),
        [dxo, y], [gate], [_sds((s_len, D_MODEL), BF16)], [vec], 256, deps=deps,
    )

    tm = min(ROW_TILE, s_len)
    ds3 = _mm(
        f"{tag}_bwd_ds", "nt", (s_len // tm, N_CHIPS, 1),
        dy, (tm, D_MODEL), lambda i, j, k: (i, 0),
        g_down, (None, None, F_SHARD, D_MODEL), lambda i, j, k: (j, down_idx, 0, 0),
        _sds((N_CHIPS, s_len, F_SHARD), BF16), (None, tm, F_SHARD), lambda i, j, k: (j, i, 0),
        (tm, F_SHARD),
    )

    def swiglu_bwd(ab, ds):
        a = ab[:N_CHIPS].astype(F32)
        b = ab[N_CHIPS:].astype(F32)
        ds = ds.astype(F32)
        sig = _sigmoid(a)
        da = ds * b * sig * (1.0 + a * (1.0 - sig))
        db = ds * a * sig
        return jnp.concatenate([da, db], axis=0)

    (dab3,) = _rowwise(
        f"{tag}_bwd_swiglu", swiglu_bwd, [ab3, ds3], [], [_sds((2 * N_CHIPS, s_len, F_SHARD), BF16)], [], 128
    )

    tk = tm
    gb_down = _mm(
        f"{tag}_bwd_wdown", "tn", (N_CHIPS, 1, s_len // tk),
        s3, (None, tk, F_SHARD), lambda i, j, k: (i, k, 0),
        dy, (tk, D_MODEL), lambda i, j, k: (k, 0),
        _sds((N_CHIPS, 1, F_SHARD, D_MODEL), BF16), (None, None, F_SHARD, D_MODEL), lambda i, j, k: (i, 0, 0, 0),
        (F_SHARD, D_MODEL), deps=hooks.after_swiglu(dab3),
    )
    gb_gu = _mm(
        f"{tag}_bwd_wgu", "tn", (2 * N_CHIPS, 1, s_len // tk),
        dab3, (None, tk, F_SHARD), lambda i, j, k: (i, k, 0),
        h, (tk, D_MODEL), lambda i, j, k: (k, 0),
        _sds((N_CHIPS, 2, F_SHARD, D_MODEL), BF16), (None, None, F_SHARD, D_MODEL),
        lambda i, j, k: (i % N_CHIPS, i // N_CHIPS, 0, 0),
        (F_SHARD, D_MODEL), deps=hooks.after_wdown(gb_down),
    )
    dh = _mm(
        f"{tag}_bwd_dh", "nn", (s_len // tm, 1, 2 * N_CHIPS),
        dab3, (None, tm, F_SHARD), lambda i, j, k: (k, i, 0),
        g4, (None, None, F_SHARD, D_MODEL), lambda i, j, k: (k % N_CHIPS, base + k // N_CHIPS, 0, 0),
        _sds((s_len, D_MODEL), F32), (tm, D_MODEL), lambda i, j, k: (i, 0),
        (tm, D_MODEL), deps=hooks.after_wgu(gb_gu),
    )
    dx, dsc, dsh, dgamma = _normmod_bwd_call(
        f"{tag}_bwd_normmod", [dh], x, dxo, gamma, sc, deps=hooks.after_dh(dh)
    )
    return dx, (gb_down, gb_gu), (dsh, dsc, dgate, dgamma)


def _bucket_map():
    qi = np.arange(WINDOW)[:, None]
    kj = np.arange(2 * WINDOW)[None, :]
    dist = qi + WINDOW - kj
    band = (dist >= 0) & (dist < WINDOW)
    max_exact = NUM_BUCKETS // 2
    n = np.maximum(dist, 0)
    large = []
    for dt in (np.float32, np.float64):
        nf = np.maximum(n, 1).astype(dt)
        val = np.log(nf / dt(max_exact)) / dt(math.log(MAX_DISTANCE / max_exact)) * dt(NUM_BUCKETS - max_exact)
        large.append(np.minimum(max_exact + val.astype(np.int32), NUM_BUCKETS - 1))
    assert np.array_equal(large[0], large[1])
    bucket = np.where(n < max_exact, n, large[0])
    return np.where(band, bucket, -1).astype(np.int32).reshape(1, -1)


def _bias_expand(rel_bias_t, bkt):
    n = bkt.shape[1]

    def body(rb_ref, bkt_ref, o_ref):
        onehot = (lax.broadcasted_iota(jnp.int32, (NUM_BUCKETS, n), 0) == bkt_ref[...]).astype(F32)
        o_ref[...] = lax.dot_general(
            rb_ref[...], onehot, _DN["nn"], precision=lax.Precision.HIGHEST, preferred_element_type=F32
        )

    return pl.pallas_call(
        body, name="bias_expand", out_shape=_sds((SWA_HEADS, n), F32), compiler_params=_cparams()
    )(rel_bias_t, bkt)


def _bias_reduce(dbias_flat, bkt):
    n = bkt.shape[1]

    def body(db_ref, bkt_ref, o_ref):
        onehot = (lax.broadcasted_iota(jnp.int32, (NUM_BUCKETS, n), 0) == bkt_ref[...]).astype(F32)
        o_ref[...] = lax.dot_general(
            db_ref[...], onehot, _DN["nt"], precision=lax.Precision.HIGHEST, preferred_element_type=F32
        )

    return pl.pallas_call(
        body, name="bias_reduce", out_shape=_sds((SWA_HEADS, NUM_BUCKETS), F32), compiler_params=_cparams()
    )(dbias_flat, bkt)


_SWA_SCALE = SWA_HEAD_DIM ** -0.5
_NEG = -1e30


def _swa_in_specs():
    w = WINDOW
    n_q = SWA_HEADS * SWA_HEAD_DIM
    n_kv = 2 * SWA_KV_HEADS * SWA_HEAD_DIM
    prev = lambda n: jnp.maximum(n - 1, 0)
    return [
        pl.BlockSpec((w, n_q), lambda n: (n, 0)),
        pl.BlockSpec((w, n_kv), lambda n: (prev(n), n_q // n_kv)),
        pl.BlockSpec((w, n_kv), lambda n: (n, n_q // n_kv)),
        pl.BlockSpec((SWA_HEADS, w, 2 * w), lambda n: (0, 0, 0)),
        pl.BlockSpec((SWA_HEADS, w, 1), lambda n: (0, 0, 0)),
    ]


def _head_cols(v, first, count):
    hd = SWA_HEAD_DIM
    return jnp.stack([v[:, (first + i) * hd:(first + i + 1) * hd] for i in range(count)])


def _swa_heads(q_ref, kvp_ref, kvc_ref):
    g, w, hd = SWA_GROUP, WINDOW, SWA_HEAD_DIM
    q = q_ref[...].astype(F32)
    kv = jnp.concatenate([kvp_ref[...], kvc_ref[...]], axis=0).astype(F32)
    heads = []
    for j in range(SWA_KV_HEADS):
        qj = _head_cols(q, g * j, g).reshape(g * w, hd)
        heads.append((qj, _head_cols(kv, j, 1)[0], _head_cols(kv, SWA_KV_HEADS + j, 1)[0]))
    return heads


def _swa_scores(q, kk, bias, n):
    g, w = SWA_GROUP, WINDOW
    s = (_dot(q, kk, "nt") * _SWA_SCALE).reshape(g, w, 2 * w) + bias
    qi = lax.broadcasted_iota(jnp.int32, (w, 2 * w), 0)
    kj = lax.broadcasted_iota(jnp.int32, (w, 2 * w), 1)
    dist = qi + w - kj
    valid = ((dist >= 0) & (dist < w) & ((n > 0) | (kj >= w)))[None]
    return jnp.where(valid, s, _NEG), valid


def _swa_fwd(swa2, bias, sinkb):
    s_len = swa2.shape[0]
    g, w, hd = SWA_GROUP, WINDOW, SWA_HEAD_DIM

    def body(q_ref, kvp_ref, kvc_ref, bias_ref, sink_ref, o_ref, lse_ref):
        n = pl.program_id(0)
        outs = []
        for j, (q, kk, vv) in enumerate(_swa_heads(q_ref, kvp_ref, kvc_ref)):
            hs = slice(g * j, g * (j + 1))
            s, valid = _swa_scores(q, kk, bias_ref[hs], n)
            sink = sink_ref[hs]
            m = jnp.maximum(jnp.max(s, axis=-1, keepdims=True), sink)
            p = jnp.where(valid, jnp.exp(s - m), 0.0)
            l = jnp.sum(p, axis=-1, keepdims=True) + jnp.exp(sink - m)
            o = _dot(p.reshape(g * w, 2 * w), vv, "nn").reshape(g, w, hd) / l
            outs += [o[i] for i in range(g)]
            lse_ref[hs] = m + jnp.log(l)
        o_ref[...] = jnp.concatenate(outs, axis=-1).astype(o_ref.dtype)

    n_q = SWA_HEADS * hd
    return pl.pallas_call(
        body,
        name="swa_fwd",
        grid=(s_len // w,),
        in_specs=_swa_in_specs(),
        out_specs=[
            pl.BlockSpec((w, n_q), lambda n: (n, 0)),
            pl.BlockSpec((SWA_HEADS, w, 1), lambda n: (0, n, 0)),
        ],
        out_shape=[_sds((s_len, n_q), BF16), _sds((SWA_HEADS, s_len, 1), F32)],
        compiler_params=_cparams(("parallel",)),
    )(swa2, swa2, swa2, bias, sinkb)


def _swa_bwd(swa2, bias, sinkb, cat, dcat, lse):
    s_len = swa2.shape[0]
    g, w, hd = SWA_GROUP, WINDOW, SWA_HEAD_DIM

    def body(q_ref, kvp_ref, kvc_ref, bias_ref, sink_ref, o_ref, do_ref, lse_ref,
             dq_ref, dkva_ref, dkvb_ref, dbias_ref, dsink_ref):
        n = pl.program_id(0)

        @pl.when(n == 0)
        def _():
            dbias_ref[...] = jnp.zeros_like(dbias_ref)
            dsink_ref[...] = jnp.zeros_like(dsink_ref)

        o_all = o_ref[...].astype(F32)
        do_all = do_ref[...].astype(F32)
        dqs, dks, dvs = [], [], []
        for j, (q, kk, vv) in enumerate(_swa_heads(q_ref, kvp_ref, kvc_ref)):
            hs = slice(g * j, g * (j + 1))
            s, valid = _swa_scores(q, kk, bias_ref[hs], n)
            lse_v = lse_ref[hs]
            p = jnp.where(valid, jnp.exp(s - lse_v), 0.0)
            do = _head_cols(do_all, g * j, g)
            delta = jnp.sum(do * _head_cols(o_all, g * j, g), axis=-1, keepdims=True)
            do2 = do.reshape(g * w, hd)
            dp = _dot(do2, vv, "nt").reshape(g, w, 2 * w)
            ds = p * (dp - delta)
            dbias_ref[hs] += ds
            dsink_ref[hs] -= jnp.exp(sink_ref[hs] - lse_v) * delta
            ds2 = ds.reshape(g * w, 2 * w)
            dq = (_dot(ds2, kk, "nn") * _SWA_SCALE).reshape(g, w, hd)
            dqs += [dq[i] for i in range(g)]
            dks.append(_dot(ds2, q, "tn") * _SWA_SCALE)
            dvs.append(_dot(p.reshape(g * w, 2 * w), do2, "tn"))
        dq_ref[...] = jnp.concatenate(dqs, axis=-1).astype(dq_ref.dtype)
        dkv = jnp.concatenate(dks + dvs, axis=-1)
        dkvb_ref[...] = dkv[:w]
        dkva_ref[...] = dkv[w:]

    n_q = SWA_HEADS * hd
    n_kv = 2 * SWA_KV_HEADS * hd
    q_spec = pl.BlockSpec((w, n_q), lambda n: (n, 0))
    kv_spec = pl.BlockSpec((w, n_kv), lambda n: (n, 0))
    return pl.pallas_call(
        body,
        name="swa_bwd",
        grid=(s_len // w,),
        in_specs=_swa_in_specs() + [q_spec, q_spec, pl.BlockSpec((SWA_HEADS, w, 1), lambda n: (0, n, 0))],
        out_specs=[
            q_spec, kv_spec, kv_spec,
            pl.BlockSpec((SWA_HEADS, w, 2 * w), lambda n: (0, 0, 0)),
            pl.BlockSpec((SWA_HEADS, w, 1), lambda n: (0, 0, 0)),
        ],
        out_shape=[
            _sds((s_len, n_q), BF16), _sds((s_len, n_kv), F32), _sds((s_len, n_kv), F32),
            _sds((SWA_HEADS, w, 2 * w), F32), _sds((SWA_HEADS, w, 1), F32),
        ],
        compiler_params=_cparams(("arbitrary",)),
    )(swa2, swa2, swa2, bias, sinkb, cat, dcat, lse)


_MLA_SCALE = MLA_QK ** -0.5
_MLA_TQ = 256


def _mla_mask(i, tq, n_keys):
    row = i * tq + lax.broadcasted_iota(jnp.int32, (tq, n_keys), 0)
    col = lax.broadcasted_iota(jnp.int32, (tq, n_keys), 1)
    return col <= row


def _per_query_block(i, n_blocks, fn):
    for ii in range(n_blocks):
        pl.when(i == ii)(functools.partial(fn, ii))


def _mla_fwd(q4, k4, v4):
    s_len = q4.shape[1]
    tq = min(_MLA_TQ, s_len)

    def body(q_ref, k_ref, v_ref, o_ref, lse_ref):
        def block(ii):
            n_keys = (ii + 1) * tq
            mask = _mla_mask(ii, tq, n_keys)
            s = jnp.where(mask, _dot(q_ref[...], k_ref[:n_keys, :], "nt") * _MLA_SCALE, _NEG)
            m = jnp.max(s, axis=-1, keepdims=True)
            p = jnp.exp(s - m)
            l = jnp.sum(p, axis=-1, keepdims=True)
            o_ref[...] = (_dot(p, v_ref[:n_keys, :], "nn") / l).astype(o_ref.dtype)
            lse_ref[...] = m + jnp.log(l)

        _per_query_block(pl.program_id(1), s_len // tq, block)

    return pl.pallas_call(
        body,
        name="mla_fwd",
        grid=(MLA_HEADS, s_len // tq),
        in_specs=[
            pl.BlockSpec((None, tq, MLA_QK), lambda h, i: (h, i, 0)),
            pl.BlockSpec((None, s_len, MLA_QK), lambda h, i: (h, 0, 0)),
            pl.BlockSpec((None, s_len, MLA_V), lambda h, i: (h, 0, 0)),
        ],
        out_specs=[
            pl.BlockSpec((tq, MLA_V), lambda h, i: (i, h)),
            pl.BlockSpec((None, tq, 1), lambda h, i: (h, i, 0)),
        ],
        out_shape=[_sds((s_len, MLA_HEADS * MLA_V), BF16), _sds((MLA_HEADS, s_len, 1), F32)],
        compiler_params=_cparams(("parallel", "parallel")),
    )(q4, k4, v4)


def _mla_bwd(q4, k4, v4, cat, dcat, lse):
    s_len = q4.shape[1]
    tq = min(_MLA_TQ, s_len)
    first = SWA_HEADS * SWA_HEAD_DIM // MLA_V

    def body(q_ref, k_ref, v_ref, o_ref, do_ref, lse_ref, dq_ref, dk_ref, dv_ref):
        i = pl.program_id(1)

        @pl.when(i == 0)
        def _():
            dk_ref[...] = jnp.zeros_like(dk_ref)
            dv_ref[...] = jnp.zeros_like(dv_ref)

        def block(ii):
            n_keys = (ii + 1) * tq
            mask = _mla_mask(ii, tq, n_keys)
            q, k, v, do = q_ref[...], k_ref[:n_keys, :], v_ref[:n_keys, :], do_ref[...]
            s = jnp.where(mask, _dot(q, k, "nt") * _MLA_SCALE, _NEG)
            p = jnp.where(mask, jnp.exp(s - lse_ref[...]), 0.0)
            delta = jnp.sum(do.astype(F32) * o_ref[...].astype(F32), axis=-1, keepdims=True)
            ds = (p * (_dot(do, v, "nt") - delta) * _MLA_SCALE).astype(BF16)
            dq_ref[...] = _dot(ds, k, "nn")
            dk_ref[:n_keys, :] += _dot(ds, q, "tn")
            dv_ref[:n_keys, :] += _dot(p, do, "tn")

        _per_query_block(i, s_len // tq, block)

    return pl.pallas_call(
        body,
        name="mla_bwd",
        grid=(MLA_HEADS, s_len // tq),
        in_specs=[
            pl.BlockSpec((None, tq, MLA_QK), lambda h, i: (h, i, 0)),
            pl.BlockSpec((None, s_len, MLA_QK), lambda h, i: (h, 0, 0)),
            pl.BlockSpec((None, s_len, MLA_V), lambda h, i: (h, 0, 0)),
            pl.BlockSpec((tq, MLA_V), lambda h, i: (i, first + h)),
            pl.BlockSpec((tq, MLA_V), lambda h, i: (i, first + h)),
            pl.BlockSpec((None, tq, 1), lambda h, i: (h, i, 0)),
        ],
        out_specs=[
            pl.BlockSpec((None, tq, MLA_QK), lambda h, i: (h, i, 0)),
            pl.BlockSpec((None, s_len, MLA_QK), lambda h, i: (h, 0, 0)),
            pl.BlockSpec((None, s_len, MLA_V), lambda h, i: (h, 0, 0)),
        ],
        out_shape=[
            _sds((MLA_HEADS, s_len, MLA_QK), F32),
            _sds((MLA_HEADS, s_len, MLA_QK), F32),
            _sds((MLA_HEADS, s_len, MLA_V), F32),
        ],
        compiler_params=_cparams(("parallel", "arbitrary")),
    )(q4, k4, v4, cat, dcat, lse)


def _mla_split(pm):
    q_lat = pm[:, :MLA_Q_RANK]
    kv_lat = pm[:, MLA_Q_RANK:MLA_Q_RANK + MLA_KV_RANK]
    kr = pm[:, MLA_Q_RANK + MLA_KV_RANK:MLA_Q_RANK + MLA_KV_RANK + MLA_ROPE]
    kr_sw = pm[:, MLA_Q_RANK + MLA_KV_RANK + MLA_ROPE:]
    return q_lat, kv_lat, kr, kr_sw


def _mla_proj(pm, cos2, sin2, q_norm, kv_norm, wq_ext, wkv):
    s_len = pm.shape[0]

    def fn(pm_, cos_, sin_, qg, kvg, wq, wk):
        q_lat, kv_lat, kr, kr_sw = _mla_split(pm_)
        nq = (_rms(q_lat)[0] * qg).astype(BF16)
        nkv = (_rms(kv_lat)[0] * kvg).astype(BF16)
        k_rot = kr * cos_ + kr_sw * sin_
        qs, ks, vs = [], [], []
        for h in range(MLA_HEADS):
            qe = _dot(nq, wq[h], "nt")
            q_rot = qe[:, MLA_NOPE:MLA_QK] * cos_ + qe[:, MLA_QK:] * sin_
            qs.append(jnp.concatenate([qe[:, :MLA_NOPE], q_rot], axis=-1))
            kve = _dot(nkv, wk[h], "nt")
            ks.append(jnp.concatenate([kve[:, :MLA_NOPE], k_rot], axis=-1))
            vs.append(kve[:, MLA_NOPE:])
        return jnp.stack(qs), jnp.stack(ks), jnp.stack(vs)

    return _rowwise(
        "mla_proj", fn, [pm, cos2, sin2], [q_norm, kv_norm, wq_ext, wkv],
        [_sds((MLA_HEADS, s_len, MLA_QK), BF16), _sds((MLA_HEADS, s_len, MLA_QK), BF16),
         _sds((MLA_HEADS, s_len, MLA_V), BF16)], [], 256,
    )


def _mla_proj_bwd(pm, cos2, sin2, dq4, dk4, dv4, q_norm, kv_norm, wq_ext, wkv):
    s_len = pm.shape[0]

    def fn(pm_, cos_, sin_, dq, dk, dv, qg, kvg, wq, wk):
        q_lat, kv_lat, _, _ = _mla_split(pm_)
        xq, rq = _rms(q_lat)
        xkv, rkv = _rms(kv_lat)
        nq = (xq * qg).astype(BF16)
        nkv = (xkv * kvg).astype(BF16)
        dnq = jnp.zeros_like(q_lat)
        dnkv = jnp.zeros_like(kv_lat)
        dkr = jnp.zeros_like(cos_)
        dwq, dwk = [], []
        for h in range(MLA_HEADS):
            dy = dq[h][:, MLA_NOPE:]
            dqe = jnp.concatenate([dq[h][:, :MLA_NOPE], dy * cos_, dy * sin_], axis=-1).astype(BF16)
            dnq = dnq + _dot(dqe, wq[h], "nn")
            dwq.append(_dot(dqe, nq, "tn"))
            dkve = jnp.concatenate([dk[h][:, :MLA_NOPE], dv[h]], axis=-1).astype(BF16)
            dnkv = dnkv + _dot(dkve, wk[h], "nn")
            dwk.append(_dot(dkve, nkv, "tn"))
            dkr = dkr + dk[h][:, MLA_NOPE:]
        dq_lat = _rms_bwd(dnq * qg, xq, rq)
        dkv_lat = _rms_bwd(dnkv * kvg, xkv, rkv)
        dpm = jnp.concatenate([dq_lat, dkv_lat, dkr * cos_, dkr * sin_], axis=-1)
        return dpm, _sum0(dnq * xq), _sum0(dnkv * xkv), jnp.stack(dwq), jnp.stack(dwk)

    return _rowwise(
        "mla_proj_bwd", fn, [pm, cos2, sin2, dq4, dk4, dv4], [q_norm, kv_norm, wq_ext, wkv],
        [_sds((s_len, N_MLA_COLS), BF16)],
        [_sds((1, MLA_Q_RANK), F32), _sds((1, MLA_KV_RANK), F32),
         _sds(wq_ext.shape, F32), _sds(wkv.shape, F32)], 256,
    )


def _rope_tables(s_len):
    inv = ROPE_THETA ** (-jnp.arange(0, MLA_ROPE, 2, dtype=F32) / MLA_ROPE)
    ang = jnp.arange(s_len, dtype=F32)[:, None] * inv[None, :]
    cos, sin = jnp.cos(ang), jnp.sin(ang)
    return jnp.concatenate([cos, cos], axis=-1), jnp.concatenate([-sin, sin], axis=-1)


def _mix_weights(g_mix):
    rest = g_mix[:, 0]
    w_in_t = rest[:, O_IN:O_IN + R_IN].reshape(D_IN, D_MODEL)
    w_o = rest[:, O_O:O_O + R_O].reshape(D_MODEL, D_MODEL)
    w_uq_t = rest[:, O_UQ:O_UQ + R_UQ].reshape(MLA_HEADS, MLA_QK, MLA_Q_RANK)
    w_ukv_t = rest[:, O_UKV:O_UKV + R_UKV].reshape(MLA_HEADS, MLA_NOPE + MLA_V, MLA_KV_RANK)
    half = MLA_ROPE // 2
    w_swa = w_in_t[:N_SWA_COLS]
    w_mla = jnp.concatenate([w_in_t[N_SWA_COLS:], w_in_t[D_IN - half:], w_in_t[D_IN - MLA_ROPE:D_IN - half]], axis=0)
    wq_ext = jnp.concatenate([w_uq_t, w_uq_t[:, MLA_QK - half:], w_uq_t[:, MLA_NOPE:MLA_QK - half]], axis=1)
    return w_swa, w_mla, w_o, wq_ext, w_ukv_t


def _mix_fwd(x, gamma, sh, sc, gate, mix_src, q_norm, kv_norm, bias, sinkb, cos2, sin2):
    s_len = x.shape[0]
    tm = min(ROW_TILE, s_len)

    def normmod(x_, gamma_, sc_, sh_):
        return _rms(x_)[0] * gamma_ * (1.0 + sc_) + sh_

    deps = mix_src.mid(x)
    (h,) = _rowwise("mix_normmod", normmod, [x], [gamma, sc, sh], [_sds((s_len, D_MODEL), BF16)], [], 256, deps=deps)
    weights = _mix_weights(mix_src.get(h))
    w_swa, w_mla, w_o, wq_ext, wkv = weights
    swa2 = _mm(
        "mix_proj_swa", "nt", (s_len // tm, 1, 1),
        h, (tm, D_MODEL), lambda i, j, k: (i, 0),
        w_swa, (N_SWA_COLS, D_MODEL), lambda i, j, k: (0, 0),
        _sds((s_len, N_SWA_COLS), BF16), (tm, N_SWA_COLS), lambda i, j, k: (i, 0),
        (tm, N_SWA_COLS),
    )
    pm = _mm(
        "mix_proj_mla", "nt", (s_len // tm, 1, 1),
        h, (tm, D_MODEL), lambda i, j, k: (i, 0),
        w_mla, (N_MLA_COLS, D_MODEL), lambda i, j, k: (0, 0),
        _sds((s_len, N_MLA_COLS), F32), (tm, N_MLA_COLS), lambda i, j, k: (i, 0),
        (tm, N_MLA_COLS),
    )
    q4, k4, v4 = _mla_proj(pm, cos2, sin2, q_norm, kv_norm, wq_ext, wkv)
    oa, lse_a = _swa_fwd(swa2, bias, sinkb)
    ob, lse_b = _mla_fwd(q4, k4, v4)
    cat = jnp.concatenate([oa, ob], axis=1)
    z = _mm(
        "mix_out", "nn", (s_len // tm, 1, 1),
        cat, (tm, D_MODEL), lambda i, j, k: (i, 0),
        w_o, (D_MODEL, D_MODEL), lambda i, j, k: (0, 0),
        _sds((s_len, D_MODEL), F32), (tm, D_MODEL), lambda i, j, k: (i, 0),
        (tm, D_MODEL),
    )
    (x_out,) = _rowwise(
        "mix_residual", lambda x_, z_, g_: x_ + g_ * z_, [x, z], [gate], [_sds((s_len, D_MODEL), F32)], [], 256
    )
    return x_out, (weights, h, swa2, pm, q4, k4, v4, cat, lse_a, lse_b, z)


def _mix_bwd(dxo, x, gamma, sc, gate, q_norm, kv_norm, bias, sinkb, cos2, sin2, saved, hooks):
    weights, h, swa2, pm, q4, k4, v4, cat, lse_a, lse_b, z = saved
    w_swa, w_mla, w_o, wq_ext, wkv = weights
    s_len = x.shape[0]
    tm = tk = min(ROW_TILE, s_len)
    vec = _sds((1, D_MODEL), F32)
    n_proj = N_SWA_COLS + N_MLA_COLS
    half_d = D_MODEL // 2

    dz, dgate = _rowwise(
        "mix_bwd_gate", lambda dxo_, z_, g_: (g_ * dxo_, _sum0(z_ * dxo_)),
        [dxo, z], [gate], [_sds((s_len, D_MODEL), BF16)], [vec], 256,
    )
    dw_o = _mm(
        "mix_bwd_wo", "tn", (2, 1, s_len // tk),
        cat, (tk, half_d), lambda i, j, k: (k, i),
        dz, (tk, D_MODEL), lambda i, j, k: (k, 0),
        _sds((D_MODEL, D_MODEL), F32), (half_d, D_MODEL), lambda i, j, k: (i, 0),
        (half_d, D_MODEL),
    )
    dcat = _mm(
        "mix_bwd_dcat", "nt", (s_len // tm, 1, 1),
        dz, (tm, D_MODEL), lambda i, j, k: (i, 0),
        w_o, (D_MODEL, D_MODEL), lambda i, j, k: (0, 0),
        _sds((s_len, D_MODEL), BF16), (tm, D_MODEL), lambda i, j, k: (i, 0),
        (tm, D_MODEL), deps=hooks.after_gate(dz),
    )
    dq_a, dkva, dkvb, dbias, dsink = _swa_bwd(swa2, bias, sinkb, cat, dcat, lse_a)
    dq4, dk4, dv4 = _mla_bwd(q4, k4, v4, cat, dcat, lse_b)
    dpm, dq_norm, dkv_norm, dwq_ext, dwkv = _mla_proj_bwd(pm, cos2, sin2, dq4, dk4, dv4, q_norm, kv_norm, wq_ext, wkv)

    dkv = dkva + jnp.concatenate([dkvb[WINDOW:], jnp.zeros_like(dkvb[:WINDOW])], axis=0)
    dproj = jnp.concatenate([dq_a, dkv.astype(BF16), dpm], axis=1)
    w_proj = jnp.concatenate([w_swa, w_mla], axis=0)

    dw_proj = _mm(
        "mix_bwd_win", "tn", (n_proj // 256, 1, s_len // tk),
        dproj, (tk, 256), lambda i, j, k: (k, i),
        h, (tk, D_MODEL), lambda i, j, k: (k, 0),
        _sds((n_proj, D_MODEL), F32), (256, D_MODEL), lambda i, j, k: (i, 0),
        (256, D_MODEL),
    )
    dw_swa, dw_mla = dw_proj[:N_SWA_COLS], dw_proj[N_SWA_COLS:]
    dh = _mm(
        "mix_bwd_dh", "nn", (s_len // tm, 1, 1),
        dproj, (tm, n_proj), lambda i, j, k: (i, 0),
        w_proj, (n_proj, D_MODEL), lambda i, j, k: (0, 0),
        _sds((s_len, D_MODEL), F32), (tm, D_MODEL), lambda i, j, k: (i, 0),
        (tm, D_MODEL),
    )
    dx, dsc, dsh, dgamma = _normmod_bwd_call("mix_bwd_normmod", [dh], x, dxo, gamma, sc)

    half = MLA_ROPE // 2
    n_mla_in = D_IN - N_SWA_COLS
    dw_mla_base = dw_mla[:n_mla_in]
    dw_mla_base = dw_mla_base.at[n_mla_in - half:].add(dw_mla[n_mla_in:n_mla_in + half])
    dw_mla_base = dw_mla_base.at[n_mla_in - MLA_ROPE:n_mla_in - half].add(dw_mla[n_mla_in + half:])
    dw_in_t = jnp.concatenate([dw_swa, dw_mla_base], axis=0)
    dw_uq_t = dwq_ext[:, :MLA_QK]
    dw_uq_t = dw_uq_t.at[:, MLA_QK - half:].add(dwq_ext[:, MLA_QK:MLA_QK + half])
    dw_uq_t = dw_uq_t.at[:, MLA_NOPE:MLA_QK - half].add(dwq_ext[:, MLA_QK + half:])
    rest = jnp.concatenate(
        [
            dw_in_t.reshape(N_CHIPS, R_IN, D_MODEL),
            dw_o.reshape(N_CHIPS, R_O, D_MODEL),
            dw_uq_t.reshape(N_CHIPS, R_UQ, D_MODEL),
            dwkv.reshape(N_CHIPS, R_UKV, D_MODEL),
            jnp.zeros((N_CHIPS, F_SHARD - O_PAD, D_MODEL), F32),
        ],
        axis=1,
    ).astype(BF16)
    return dx, rest, (dsh, dsc, dgate, dgamma), dq_norm, dkv_norm, dbias, dsink


def _device_step(x, target, mod, gu1_src, down1_src, mix_src, ffn2_src, norms, q_norm, kv_norm, sinks, rel_bias,
                 hooks2=None, hooks_mix=None, mix_done=None, hooks1=None):
    s_len = x.shape[0]
    sh1, sc1, g1, sh2, sc2, g2, sh3, sc3, g3 = [mod[:, i * D_MODEL:(i + 1) * D_MODEL] for i in range(N_MOD)]
    n1, n2, n3, nf = norms
    bkt = jnp.asarray(_bucket_map())
    bias = _bias_expand(rel_bias.T, bkt).reshape(SWA_HEADS, WINDOW, 2 * WINDOW)
    sinkb = jnp.broadcast_to(sinks.reshape(SWA_HEADS, 1, 1), (SWA_HEADS, WINDOW, 1))
    cos2, sin2 = _rope_tables(s_len)

    x1, saved1 = _ffn_fwd("ffn1", x, n1, sh1, sc1, g1, gu1_src, 0, down1_src, 0, sh1)
    x2, saved2 = _mix_fwd(x1, n2, sh2, sc2, g2, mix_src, q_norm, kv_norm, bias, sinkb, cos2, sin2)
    x3, saved3 = _ffn_fwd("ffn2", x2, n3, sh3, sc3, g3, ffn2_src, 0, None, 2, x2)

    def head(x_, t_, g_):
        xr, r = _rms(x_)
        err = xr * g_ - t_
        dy = err * (1.0 / D_MODEL)
        return _rms_bwd(dy * g_, xr, r), _sum0(err * err), _sum0(dy * xr)

    vec = _sds((1, D_MODEL), F32)
    dx3, sq, dnf = _rowwise("loss_head", head, [x3, target], [nf], [_sds((s_len, D_MODEL), F32)], [vec, vec], 256)
    loss_part = (0.5 / D_MODEL) * jnp.sum(sq)

    dx2, gb2, (dsh3, dsc3, dg3, dn3) = _ffn_bwd("ffn2", dx3, x2, n3, sc3, g3, saved3, hooks2 or _BwdHooks())
    dx1, rest, (dsh2, dsc2, dg2, dn2), dq_norm, dkv_norm, dbias, dsink = _mix_bwd(
        dx2, x1, n2, sc2, g2, q_norm, kv_norm, bias, sinkb, cos2, sin2, saved2, hooks_mix or _BwdHooks()
    )
    rest = rest[:, None]
    deps1 = mix_done(dx1, rest) if mix_done is not None else []
    dx0, gb1, (dsh1, dsc1, dg1, dn1) = _ffn_bwd(
        "ffn1", dx1, x, n1, sc1, g1, saved1, hooks1 or _BwdHooks(), deps=deps1
    )

    dmod = jnp.concatenate([dsh1, dsc1, dg1, dsh2, dsc2, dg2, dsh3, dsc3, dg3], axis=-1)
    drel = _bias_reduce(dbias.reshape(SWA_HEADS, -1), bkt).T
    dsinks = jnp.sum(dsink, axis=(1, 2)).reshape(1, SWA_HEADS)
    small = (dn1, dn2, dn3, dnf, dq_norm, dkv_norm, dsinks, drel)
    return loss_part, dx0, (gb1, gb2, rest), dmod, small


_MESH = pl.DeviceIdType.MESH


def _place():
    return lax.axis_index("x"), lax.axis_index("y"), lax.axis_index("c")


def _other_chips(x, y):
    return [(1 - x, y), (x, 1 - y), (1 - x, 1 - y)]


def _allgather_small(name, blk):
    m_per, n = blk.shape

    def body(x_ref, out_ref, send_sems, recv_sems, local_sem):
        x, y, c = _place()
        me, sibling = (x, y, c), (x, y, 1 - c)
        chips = _other_chips(x, y)

        def rows(px, py, pc):
            return out_ref.at[pl.ds((4 * px + 2 * py + pc) * m_per, m_per), :]

        def copy(k, block, to, src=None):
            return pltpu.make_async_remote_copy(
                src_ref=rows(*block) if src is None else src, dst_ref=rows(*block),
                send_sem=send_sems.at[k], recv_sem=recv_sems.at[k], device_id=to, device_id_type=_MESH,
            )

        mine = pltpu.make_async_copy(x_ref, rows(*me), local_sem)
        mine.start()
        first = [copy(0, me, sibling, src=x_ref)]
        first += [copy(1 + j, me, (*chip, c), src=x_ref) for j, chip in enumerate(chips)]
        for cp in first:
            cp.start()
        passed = [copy(4 + j, (*chip, c), sibling) for j, chip in enumerate(chips)]
        for j, chip in enumerate(chips):
            copy(1 + j, (*chip, c), me).wait_recv()
            passed[j].start()
        copy(0, sibling, me).wait_recv()
        for j, chip in enumerate(chips):
            copy(4 + j, (*chip, 1 - c), me).wait_recv()
        for cp in first + passed:
            cp.wait_send()
        mine.wait()

    return pl.pallas_call(
        body,
        name=name,
        out_shape=_sds((N_DEV * m_per, n), blk.dtype),
        in_specs=[pl.BlockSpec(memory_space=pltpu.VMEM)],
        out_specs=pl.BlockSpec(memory_space=pltpu.VMEM),
        scratch_shapes=[pltpu.SemaphoreType.DMA((7,)), pltpu.SemaphoreType.DMA((7,)), pltpu.SemaphoreType.DMA],
    )(blk)


def _allgather_weights(name, own):
    n = own.shape[0]

    def body(own_ref, g_ref, token, send_sems, recv_sems, local_sem):
        x, y, c = _place()
        sibling = (x, y, 1 - c)
        chips = _other_chips(x, y)
        mine = pltpu.make_async_copy(own_ref, g_ref.at[2 * x + y], local_sem)
        mine.start()
        for j, chip in enumerate(chips):
            for cp in _gather_sends(n, own_ref, g_ref, send_sems.at[j], recv_sems.at[j], x, y, c, chip):
                cp.start()
        for j, chip in enumerate(chips):
            _gather_all(own_ref, g_ref, send_sems.at[j], recv_sems.at[j], chip, c, c).wait_recv()
            for cp in _gather_forwards(n, g_ref, send_sems.at[3 + j], recv_sems.at[3 + j], chip, c, sibling):
                cp.start()
        for j, chip in enumerate(chips):
            _gather_all(own_ref, g_ref, send_sems.at[3 + j], recv_sems.at[3 + j], chip, 1 - c, c).wait_recv()
        for j, chip in enumerate(chips):
            _gather_all(own_ref, g_ref, send_sems.at[j], recv_sems.at[j], chip, c, c).wait_send()
            _gather_all(own_ref, g_ref, send_sems.at[3 + j], recv_sems.at[3 + j], chip, c, c).wait_send()
        mine.wait()
        token[...] = jnp.zeros_like(token)

    return pl.pallas_call(
        body,
        name=name,
        out_shape=[_sds((N_CHIPS,) + own.shape, own.dtype), _sds((8, 128), F32)],
        in_specs=[pl.BlockSpec(memory_space=pl.ANY)],
        out_specs=[pl.BlockSpec(memory_space=pl.ANY), pl.BlockSpec(memory_space=pltpu.VMEM)],
        scratch_shapes=[pltpu.SemaphoreType.DMA((6,)), pltpu.SemaphoreType.DMA((6,)), pltpu.SemaphoreType.DMA],
    )(own)


def _gather_sends(n, own_ref, g_ref, send_sem, recv_sem, x, y, c, chip):
    return [
        pltpu.make_async_remote_copy(
            src_ref=own_ref.at[p, pl.ds(c * HALF, HALF), :], dst_ref=g_ref.at[2 * x + y, p, pl.ds(c * HALF, HALF), :],
            send_sem=send_sem, recv_sem=recv_sem, device_id=(*chip, c), device_id_type=_MESH,
        )
        for p in range(n)
    ]


def _gather_forwards(n, g_ref, send_sem, recv_sem, chip, c, sibling):
    return [
        pltpu.make_async_remote_copy(
            src_ref=g_ref.at[2 * chip[0] + chip[1], p, pl.ds(c * HALF, HALF), :],
            dst_ref=g_ref.at[2 * chip[0] + chip[1], p, pl.ds(c * HALF, HALF), :],
            send_sem=send_sem, recv_sem=recv_sem, device_id=sibling, device_id_type=_MESH,
        )
        for p in range(n)
    ]


def _gather_all(own_ref, g_ref, send_sem, recv_sem, chip, half, c):
    return pltpu.make_async_remote_copy(
        src_ref=own_ref.at[:, pl.ds(c * HALF, HALF), :],
        dst_ref=g_ref.at[2 * chip[0] + chip[1], :, pl.ds(half * HALF, HALF), :],
        send_sem=send_sem, recv_sem=recv_sem, device_id=(*chip, c), device_id_type=_MESH,
    )


_HBM_SPEC = pl.BlockSpec(memory_space=pltpu.HBM)
_SEM_SPEC = pl.BlockSpec(memory_space=pltpu.SEMAPHORE)
_ANY_SPEC = pl.BlockSpec(memory_space=pl.ANY)
_VMEM_SPEC = pl.BlockSpec(memory_space=pltpu.VMEM)
_SPLIT_PARAMS = pltpu.CompilerParams(has_side_effects=pltpu.SideEffectType.DATAFLOW_SIDE_EFFECTING)
_TOKEN = jax.ShapeDtypeStruct((8, 128), F32)


def _in_hbm(a):
    return pltpu.with_memory_space_constraint(a, pltpu.HBM)


class _GatherBehind:
    def __init__(self, tag, own, then=None):
        self.tag, self.n, self.own, self.then = tag, own.shape[0], own, then

    def start(self, after):
        tag, own, n = self.tag, self.own, self.n
        x, y, _ = _place()
        g_init = lax.dynamic_update_slice(
            lax.empty((N_CHIPS,) + own.shape, own.dtype), own[None], (2 * x + y, 0, 0, 0)
        )

        def body(own_ref, g_ref, *rest):
            send_sems, recv_sems, _, _, token = rest[len(after):]
            x, y, c = _place()
            for j, chip in enumerate(_other_chips(x, y)):
                for cp in _gather_sends(n, own_ref, g_ref, send_sems.at[j], recv_sems.at[j], x, y, c, chip):
                    cp.start()
            token[...] = jnp.zeros_like(token)

        self.send1, self.recv1, self.own, self.g, self.token = pl.pallas_call(
            body,
            name=f"{tag}_start",
            out_shape=(
                pltpu.SemaphoreType.DMA((3,)), pltpu.SemaphoreType.DMA((3,)),
                pltpu.HBM(own.shape, own.dtype), pltpu.HBM(g_init.shape, g_init.dtype), _TOKEN,
            ),
            in_specs=(_HBM_SPEC, _HBM_SPEC) + (_ANY_SPEC,) * len(after),
            out_specs=(_SEM_SPEC, _SEM_SPEC, _HBM_SPEC, _HBM_SPEC, _VMEM_SPEC),
            input_output_aliases={0: 2, 1: 3},
            compiler_params=_SPLIT_PARAMS,
        )(_in_hbm(own), _in_hbm(g_init), *after)

    def mid(self, after):
        n = self.n

        def body(own_ref, g_ref, send1, recv1, after_ref, send2, recv2, g_out, token):
            x, y, c = _place()
            sibling = (x, y, 1 - c)
            chips = _other_chips(x, y)
            for j, chip in enumerate(chips):
                _gather_all(own_ref, g_ref, send1.at[j], recv1.at[j], chip, c, c).wait_recv()
                for cp in _gather_forwards(n, g_ref, send2.at[j], recv2.at[j], chip, c, sibling):
                    cp.start()
            for j, chip in enumerate(chips):
                _gather_all(own_ref, g_ref, send1.at[j], recv1.at[j], chip, c, c).wait_send()
            token[...] = jnp.zeros_like(token)

        self.send2, self.recv2, self.g, token = pl.pallas_call(
            body,
            name=f"{self.tag}_mid",
            out_shape=(
                pltpu.SemaphoreType.DMA((3,)), pltpu.SemaphoreType.DMA((3,)),
                pltpu.HBM(self.g.shape, self.g.dtype), _TOKEN,
            ),
            in_specs=(_HBM_SPEC, _HBM_SPEC, _SEM_SPEC, _SEM_SPEC, _ANY_SPEC),
            out_specs=(_SEM_SPEC, _SEM_SPEC, _HBM_SPEC, _VMEM_SPEC),
            input_output_aliases={1: 2},
            compiler_params=_SPLIT_PARAMS,
        )(self.own, self.g, self.send1, self.recv1, after)
        if self.then is None:
            return [token]
        self.then.start([token])
        return [token, self.then.token]

    def get(self, after):
        def body(g_ref, send2, recv2, after_ref, g_out):
            x, y, c = _place()
            for j, chip in enumerate(_other_chips(x, y)):
                slot_in = g_ref.at[2 * chip[0] + chip[1], :, pl.ds((1 - c) * HALF, HALF), :]
                slot_out = g_ref.at[2 * chip[0] + chip[1], :, pl.ds(c * HALF, HALF), :]
                cp = pltpu.make_async_remote_copy(
                    src_ref=slot_out, dst_ref=slot_in, send_sem=send2.at[j], recv_sem=recv2.at[j],
                    device_id=(x, y, 1 - c), device_id_type=_MESH,
                )
                cp.wait_send()
                cp.wait_recv()

        return pl.pallas_call(
            body,
            name=f"{self.tag}_wait",
            out_shape=pltpu.HBM(self.g.shape, self.g.dtype),
            in_specs=(_HBM_SPEC, _SEM_SPEC, _SEM_SPEC, _ANY_SPEC),
            out_specs=_HBM_SPEC,
            input_output_aliases={0: 0},
            compiler_params=_SPLIT_PARAMS,
        )(self.g, self.send2, self.recv2, after)


def _pair_exchange(name, gb):
    def body(gb_ref, land_ref, send_sem, recv_sem):
        x, y, c = _place()
        theirs = gb_ref.at[:, :, pl.ds((1 - c) * HALF, HALF), :]
        cp = pltpu.make_async_remote_copy(
            src_ref=theirs, dst_ref=land_ref, send_sem=send_sem, recv_sem=recv_sem,
            device_id=(x, y, 1 - c), device_id_type=_MESH,
        )
        cp.start()
        cp.wait()

    return pl.pallas_call(
        body,
        name=name,
        out_shape=_sds((N_CHIPS, gb.shape[1], HALF, D_MODEL), gb.dtype),
        in_specs=[pl.BlockSpec(memory_space=pl.ANY)],
        out_specs=pl.BlockSpec(memory_space=pl.ANY),
        scratch_shapes=[pltpu.SemaphoreType.DMA, pltpu.SemaphoreType.DMA],
    )(gb)


def _pair_sum(name, gb, land):
    def body(gb_ref, land_ref, o_ref):
        c = lax.axis_index("c")
        mine = gb_ref[pl.ds(pl.multiple_of(c * HALF, 16), HALF), :]
        o_ref[...] = (mine.astype(F32) + land_ref[...].astype(F32)).astype(o_ref.dtype)

    return pl.pallas_call(
        body,
        name=name,
        grid=(N_CHIPS, gb.shape[1]),
        in_specs=[
            pl.BlockSpec((None, None, F_SHARD, D_MODEL), lambda j, p: (j, p, 0, 0)),
            pl.BlockSpec((None, None, HALF, D_MODEL), lambda j, p: (j, p, 0, 0)),
        ],
        out_specs=pl.BlockSpec((None, None, HALF, D_MODEL), lambda j, p: (j, p, 0, 0)),
        out_shape=_sds(land.shape, BF16),
        compiler_params=_cparams(("parallel", "parallel")),
    )(gb, land)


def _chip_exchange(name, part):
    def body(p_ref, land_ref, send_sems, recv_sems, local_sem):
        x, y, c = _place()
        me = 2 * x + y
        chips = _other_chips(x, y)
        mine = pltpu.make_async_copy(p_ref.at[me], land_ref.at[me], local_sem)
        mine.start()

        def copy(k, chip):
            return pltpu.make_async_remote_copy(
                src_ref=p_ref.at[2 * chip[0] + chip[1]], dst_ref=land_ref.at[me],
                send_sem=send_sems.at[k], recv_sem=recv_sems.at[k], device_id=(*chip, c), device_id_type=_MESH,
            )

        sends = [copy(k, chip) for k, chip in enumerate(chips)]
        for cp in sends:
            cp.start()
        for k, chip in enumerate(chips):
            pltpu.make_async_remote_copy(
                src_ref=p_ref.at[me], dst_ref=land_ref.at[2 * chip[0] + chip[1]],
                send_sem=send_sems.at[k], recv_sem=recv_sems.at[k], device_id=(*chip, c), device_id_type=_MESH,
            ).wait_recv()
        for cp in sends:
            cp.wait_send()
        mine.wait()

    return pl.pallas_call(
        body,
        name=name,
        out_shape=_sds(part.shape, part.dtype),
        in_specs=[pl.BlockSpec(memory_space=pl.ANY)],
        out_specs=pl.BlockSpec(memory_space=pl.ANY),
        scratch_shapes=[pltpu.SemaphoreType.DMA((3,)), pltpu.SemaphoreType.DMA((3,)), pltpu.SemaphoreType.DMA],
    )(part)


def _chip_sum_share(name, land):
    n = land.shape[1]

    def body(l_ref, r_ref, buf, send_sems, recv_sems, local_sems):
        p = pl.program_id(0)
        x, y, c = _place()
        acc = l_ref[0].astype(F32)
        for j in range(1, N_CHIPS):
            acc = acc + l_ref[j].astype(F32)
        buf[p] = acc

        def copies(q):
            mine = r_ref.at[q, pl.ds(c * HALF, HALF), :]
            theirs = r_ref.at[q, pl.ds((1 - c) * HALF, HALF), :]
            local = pltpu.make_async_copy(buf.at[q], mine, local_sems.at[q])
            out = pltpu.make_async_remote_copy(
                src_ref=buf.at[q], dst_ref=mine, send_sem=send_sems.at[q], recv_sem=recv_sems.at[q],
                device_id=(x, y, 1 - c), device_id_type=_MESH,
            )
            arrive = pltpu.make_async_remote_copy(
                src_ref=buf.at[q], dst_ref=theirs, send_sem=send_sems.at[q], recv_sem=recv_sems.at[q],
                device_id=(x, y, 1 - c), device_id_type=_MESH,
            )
            return local, out, arrive

        local, out, _ = copies(p)
        local.start()
        out.start()

        @pl.when(p == n - 1)
        def _():
            for q in range(n):
                local, out, arrive = copies(q)
                arrive.wait_recv()
                out.wait_send()
                local.wait()

    return pl.pallas_call(
        body,
        name=name,
        grid=(n,),
        in_specs=[pl.BlockSpec((N_CHIPS, None, HALF, D_MODEL), lambda p: (0, p, 0, 0))],
        out_specs=pl.BlockSpec(memory_space=pl.ANY),
        out_shape=_sds((n, F_SHARD, D_MODEL), F32),
        scratch_shapes=[
            pltpu.VMEM((n, HALF, D_MODEL), F32),
            pltpu.SemaphoreType.DMA((n,)), pltpu.SemaphoreType.DMA((n,)), pltpu.SemaphoreType.DMA((n,)),
        ],
        compiler_params=_cparams(("arbitrary",)),
    )(land)


class _ReduceBehind:
    def __init__(self, tag, gb, after=()):
        self.tag = tag
        n = gb.shape[1]
        land = lax.empty((N_CHIPS, n, HALF, D_MODEL), gb.dtype)

        def body(gb_ref, land_ref, *rest):
            send_sem, recv_sem, _, _, token = rest[len(after):]
            self._pair_copy(gb_ref, land_ref, send_sem, recv_sem).start()
            token[...] = jnp.zeros_like(token)

        self.send, self.recv, self.gb, self.land, self.token = pl.pallas_call(
            body,
            name=f"grads_pair_start_{tag}",
            out_shape=(
                pltpu.SemaphoreType.DMA((1,)), pltpu.SemaphoreType.DMA((1,)),
                pltpu.HBM(gb.shape, gb.dtype), pltpu.HBM(land.shape, land.dtype), _TOKEN,
            ),
            in_specs=(_HBM_SPEC, _HBM_SPEC) + (_ANY_SPEC,) * len(after),
            out_specs=(_SEM_SPEC, _SEM_SPEC, _HBM_SPEC, _HBM_SPEC, _VMEM_SPEC),
            input_output_aliases={0: 2, 1: 3},
            compiler_params=_SPLIT_PARAMS,
        )(_in_hbm(gb), _in_hbm(land), *after)

    @staticmethod
    def _pair_copy(gb_ref, land_ref, send_sem, recv_sem):
        x, y, c = _place()
        return pltpu.make_async_remote_copy(
            src_ref=gb_ref.at[:, :, pl.ds((1 - c) * HALF, HALF), :], dst_ref=land_ref,
            send_sem=send_sem.at[0], recv_sem=recv_sem.at[0], device_id=(x, y, 1 - c), device_id_type=_MESH,
        )

    @staticmethod
    def _chip_copies(p_ref, land_ref, send_sems, recv_sems):
        x, y, c = _place()
        me = 2 * x + y
        sends, arrivals = [], []
        for k, chip in enumerate(_other_chips(x, y)):
            them = 2 * chip[0] + chip[1]
            sends.append(pltpu.make_async_remote_copy(
                src_ref=p_ref.at[them], dst_ref=land_ref.at[me], send_sem=send_sems.at[k], recv_sem=recv_sems.at[k],
                device_id=(*chip, c), device_id_type=_MESH,
            ))
            arrivals.append(pltpu.make_async_remote_copy(
                src_ref=p_ref.at[me], dst_ref=land_ref.at[them], send_sem=send_sems.at[k], recv_sem=recv_sems.at[k],
                device_id=(*chip, c), device_id_type=_MESH,
            ))
        return sends, arrivals

    def mid(self, after):
        tag = self.tag

        def wait_body(gb_ref, land_ref, send_sem, recv_sem, after_ref, gb_out, land_out):
            cp = self._pair_copy(gb_ref, land_ref, send_sem, recv_sem)
            cp.wait_send()
            cp.wait_recv()

        gb, land = pl.pallas_call(
            wait_body,
            name=f"grads_pair_wait_{tag}",
            out_shape=(pltpu.HBM(self.gb.shape, self.gb.dtype), pltpu.HBM(self.land.shape, self.land.dtype)),
            in_specs=(_HBM_SPEC, _HBM_SPEC, _SEM_SPEC, _SEM_SPEC, _ANY_SPEC),
            out_specs=(_HBM_SPEC, _HBM_SPEC),
            input_output_aliases={0: 0, 1: 1},
            compiler_params=_SPLIT_PARAMS,
        )(self.gb, self.land, self.send, self.recv, after)
        part = _pair_sum(f"grads_pair_sum_{tag}", gb, land)

        x, y, _ = _place()
        start = (2 * x + y, 0, 0, 0)
        own = lax.dynamic_slice(part, start, (1,) + part.shape[1:])
        land2 = lax.dynamic_update_slice(lax.empty(part.shape, part.dtype), own, start)

        def start_body(p_ref, land_ref, send_sems, recv_sems, p_out, land_out, token):
            for cp in self._chip_copies(p_ref, land_ref, send_sems, recv_sems)[0]:
                cp.start()
            token[...] = jnp.zeros_like(token)

        self.send2, self.recv2, self.part, self.land2, token = pl.pallas_call(
            start_body,
            name=f"grads_chip_start_{tag}",
            out_shape=(
                pltpu.SemaphoreType.DMA((3,)), pltpu.SemaphoreType.DMA((3,)),
                pltpu.HBM(part.shape, part.dtype), pltpu.HBM(land2.shape, land2.dtype), _TOKEN,
            ),
            in_specs=(_HBM_SPEC, _HBM_SPEC),
            out_specs=(_SEM_SPEC, _SEM_SPEC, _HBM_SPEC, _HBM_SPEC, _VMEM_SPEC),
            input_output_aliases={0: 2, 1: 3},
            compiler_params=_SPLIT_PARAMS,
        )(_in_hbm(part), _in_hbm(land2))
        return [token]

    def get(self, after):
        n_after = len(after)

        def wait_body(p_ref, land_ref, send_sems, recv_sems, *rest):
            sends, arrivals = self._chip_copies(p_ref, land_ref, send_sems, recv_sems)
            for cp in arrivals:
                cp.wait_recv()
            for cp in sends:
                cp.wait_send()

        _, land2 = pl.pallas_call(
            wait_body,
            name=f"grads_chip_wait_{self.tag}",
            out_shape=(pltpu.HBM(self.part.shape, self.part.dtype), pltpu.HBM(self.land2.shape, self.land2.dtype)),
            in_specs=(_HBM_SPEC, _HBM_SPEC, _SEM_SPEC, _SEM_SPEC) + (_ANY_SPEC,) * n_after,
            out_specs=(_HBM_SPEC, _HBM_SPEC),
            input_output_aliases={0: 0, 1: 1},
            compiler_params=_SPLIT_PARAMS,
        )(self.part, self.land2, self.send2, self.recv2, *after)
        return _chip_sum_share(f"grads_chip_sum_share_{self.tag}", land2)


def _allreduce_small(blk):
    gathered = _allgather_small("allgather_small_grads", blk).reshape(N_DEV, PK_ROWS, 128)

    def body(g_ref, o_ref):
        acc = g_ref[0]
        for k in range(1, N_DEV):
            acc = acc + g_ref[k]
        o_ref[...] = acc

    total = pl.pallas_call(body, name="small_grads_sum", out_shape=_sds((PK_ROWS, 128), F32))(gathered)
    return gathered, total


def _adamw(name, w, g, m, v):
    rows, cols = w.shape
    tm = rows
    while tm * cols * 4 > (1 << 20) and tm % 16 == 0:
        tm //= 2

    def fn(w_, g_, m_, v_):
        m_new = ADAM_B1 * m_ + (1.0 - ADAM_B1) * g_
        v_new = ADAM_B2 * v_ + (1.0 - ADAM_B2) * (g_ * g_)
        m_hat = m_new / (1.0 - ADAM_B1 ** ADAM_STEP)
        v_hat = v_new / (1.0 - ADAM_B2 ** ADAM_STEP)
        delta = -ADAM_LR * (m_hat / (jnp.sqrt(v_hat) + ADAM_EPS) + ADAM_WD * w_)
        return delta, m_new, v_new

    out = _sds(w.shape, F32)
    return _rowwise(name, fn, [w, g, m, v], [], [out, out, out], [], tm)


def _pack_small(b_mod_like, n1, n2, n3, nf, qn, kvn, sinks, rel):
    parts = [
        b_mod_like.reshape(PK_MOD, 128),
        n1.reshape(8, 128), n2.reshape(8, 128), n3.reshape(8, 128), nf.reshape(8, 128),
        qn.reshape(2, 128), kvn.reshape(1, 128),
        jnp.pad(sinks.reshape(1, SWA_HEADS), ((0, 0), (0, 128 - SWA_HEADS))),
        rel.reshape(2, 128),
        jnp.zeros((2, 128), F32),
    ]
    return jnp.concatenate(parts, axis=0)


def _unpack_small(p):
    o = PK_MOD
    return (
        p[:o].reshape(1, N_MOD * D_MODEL),
        p[o:o + 8].reshape(1, D_MODEL), p[o + 8:o + 16].reshape(1, D_MODEL),
        p[o + 16:o + 24].reshape(1, D_MODEL), p[o + 24:o + 32].reshape(D_MODEL),
        p[o + 32:o + 34].reshape(1, MLA_Q_RANK), p[o + 34:o + 35].reshape(1, MLA_KV_RANK),
        p[o + 35:o + 36, :SWA_HEADS].reshape(1, SWA_HEADS),
        p[o + 36:o + 38].reshape(NUM_BUCKETS, SWA_HEADS),
    )


def kernel(x, c, w_mod, b_mod, norm_ffn1, ffn1_gate, ffn1_up, ffn1_down, norm_mix, w_in, q_norm, kv_norm, w_uq, w_ukv, sinks, w_o, norm_ffn2, ffn2_gate, ffn2_up, ffn2_down, rel_bias, norm_final, loss_target, m_w_mod, m_b_mod, m_norm_ffn1, m_ffn1_gate, m_ffn1_up, m_ffn1_down, m_norm_mix, m_w_in, m_q_norm, m_kv_norm, m_w_uq, m_w_ukv, m_sinks, m_w_o, m_norm_ffn2, m_ffn2_gate, m_ffn2_up, m_ffn2_down, m_rel_bias, m_norm_final, v_w_mod, v_b_mod, v_norm_ffn1, v_ffn1_gate, v_ffn1_up, v_ffn1_down, v_norm_mix, v_w_in, v_q_norm, v_kv_norm, v_w_uq, v_w_ukv, v_sinks, v_w_o, v_norm_ffn2, v_ffn2_gate, v_ffn2_up, v_ffn2_down, v_rel_bias, v_norm_final):
    ax, ay, ac = _place()
    chip = 2 * ax + ay
    dev = 2 * chip + ac
    n_mod_shard = N_MOD * D_MODEL // N_CHIPS

    def t16(w):
        return w[0].T.astype(BF16)

    rest = jnp.concatenate(
        [
            t16(w_in),
            w_o[0].astype(BF16),
            t16(w_uq).reshape(R_UQ, D_MODEL),
            t16(w_ukv).reshape(R_UKV, D_MODEL),
            jnp.zeros((F_SHARD - O_PAD, D_MODEL), BF16),
        ],
        axis=0,
    )
    own_gu1 = jnp.stack([t16(ffn1_gate), t16(ffn1_up)])
    own_down1 = ffn1_down[0].astype(BF16)[None]
    own_mix = rest[None]
    own_ffn2 = jnp.stack([t16(ffn2_gate), t16(ffn2_up), ffn2_down[0].astype(BF16)])

    (c_act,) = _rowwise(
        "silu_c", lambda v: v * _sigmoid(v), [c.reshape(8, 128)], [], [_sds((8, 128), F32)], [], 8
    )
    c_all = _allgather_small("allgather_c", c_act).reshape(N_DEV, D_MODEL)
    mod_cols = _mm(
        "mod_fwd", "nn", (1, n_mod_shard // 768, 1),
        c_all, (N_DEV, D_MODEL), lambda i, j, k: (0, 0),
        w_mod[0], (D_MODEL, 768), lambda i, j, k: (0, j),
        _sds((N_DEV, n_mod_shard), F32), (N_DEV, 768), lambda i, j, k: (0, j),
        (N_DEV, 768),
    )
    mod_all = _allgather_small("allgather_mod", mod_cols).reshape(N_CHIPS, 2, N_DEV, n_mod_shard)[:, 0]
    mod_all = mod_all.transpose(1, 0, 2).reshape(N_DEV, N_MOD * D_MODEL)
    mod = lax.dynamic_slice_in_dim(mod_all, dev, 1, axis=0) + b_mod

    norms = (norm_ffn1, norm_mix, norm_ffn2, norm_final.reshape(1, D_MODEL))

    ffn2_src = _GatherBehind("gather_ffn2", own_ffn2)
    mix_src = _GatherBehind("gather_mix", own_mix, then=ffn2_src)
    down1_src = _GatherBehind("gather_down1", own_down1, then=mix_src)
    gu1_src = _GatherBehind("gather_gu1", own_gu1, then=down1_src)
    gu1_src.start([mod_all])

    reducing, red = {}, {}

    def begin(tag, gb, after):
        reducing[tag] = _ReduceBehind(tag, gb, after=after)
        return [reducing[tag].token]

    def ffn2_gu_done(gb):
        return begin("ffn2_gu", gb, reducing["ffn2_down"].mid(gb))

    def mix_done(dx1, gb_rest):
        red["ffn2_down"] = reducing["ffn2_down"].get([dx1])
        red["ffn2_gu"] = reducing["ffn2_gu"].get([red["ffn2_down"]])
        return begin("rest", gb_rest, [red["ffn2_gu"]])

    def ffn1_gu_done(gb):
        red["rest"] = reducing["rest"].get([gb])
        begin("ffn1_gu", gb, reducing["ffn1_down"].mid(red["rest"]))
        return reducing["ffn1_gu"].mid(reducing["ffn1_gu"].token)

    loss_part, grad_x, _, dmod, small = _device_step(
        x[0], loss_target[0], mod, gu1_src, down1_src, mix_src, ffn2_src, norms, q_norm, kv_norm, sinks, rel_bias,
        hooks2=_BwdHooks(after_wdown=lambda gb: begin("ffn2_down", gb, ()), after_wgu=ffn2_gu_done),
        hooks_mix=_BwdHooks(after_gate=lambda dz: reducing["ffn2_gu"].mid(dz)),
        mix_done=mix_done,
        hooks1=_BwdHooks(
            after_swiglu=lambda dab3: reducing["rest"].mid(dab3),
            after_wdown=lambda gb: begin("ffn1_down", gb, ()),
            after_wgu=ffn1_gu_done,
        ),
    )
    loss = lax.psum(loss_part, ("x", "y", "c"))

    gathered, total = _allreduce_small(_pack_small(dmod, *small))
    (g_b_mod, g_n1, g_n2, g_n3, g_nf, g_qn, g_kvn, g_sinks, g_rel) = _unpack_small(total)
    dmod_all = gathered[:, :PK_MOD].reshape(N_DEV, N_MOD * D_MODEL)
    dmod_cols = lax.dynamic_slice_in_dim(dmod_all, chip * n_mod_shard, n_mod_shard, axis=1)
    g_w_mod = _mm(
        "mod_bwd", "tn", (1, n_mod_shard // 768, 1),
        c_all, (N_DEV, D_MODEL), lambda i, j, k: (0, 0),
        dmod_cols, (N_DEV, 768), lambda i, j, k: (0, j),
        _sds((D_MODEL, n_mod_shard), F32), (D_MODEL, 768), lambda i, j, k: (0, j),
        (D_MODEL, 768),
    )

    r_rest = red["rest"][0]
    transposed = {"ffn1_gate", "ffn1_up", "ffn2_gate", "ffn2_up", "w_in", "w_uq", "w_ukv"}
    g_big = {
        "ffn2_gate": red["ffn2_gu"][0], "ffn2_up": red["ffn2_gu"][1], "ffn2_down": red["ffn2_down"][0],
        "w_in": r_rest[O_IN:O_IN + R_IN],
        "w_o": r_rest[O_O:O_O + R_O],
        "w_uq": r_rest[O_UQ:O_UQ + R_UQ].reshape(MLA_QK, MLA_Q_RANK),
        "w_ukv": r_rest[O_UKV:O_UKV + R_UKV].reshape(MLA_NOPE + MLA_V, MLA_KV_RANK),
        "w_mod": g_w_mod,
    }
    w_big = {
        "ffn2_gate": (ffn2_gate, m_ffn2_gate, v_ffn2_gate),
        "ffn2_up": (ffn2_up, m_ffn2_up, v_ffn2_up), "ffn2_down": (ffn2_down, m_ffn2_down, v_ffn2_down),
        "w_in": (w_in, m_w_in, v_w_in), "w_o": (w_o, m_w_o, v_w_o), "w_uq": (w_uq, m_w_uq, v_w_uq),
        "w_ukv": (w_ukv, m_w_ukv, v_w_ukv), "w_mod": (w_mod, m_w_mod, v_w_mod),
    }
    grads, deltas, new_m, new_v = {}, {}, {}, {}

    def update(names):
        for nm in names:
            w, m, v = w_big[nm]
            if nm in transposed:
                outs = _adamw(f"adamw_{nm}", w[0].T, g_big[nm], m[0].T, v[0].T)
                g_, d_, m_, v_ = [a.T for a in (g_big[nm],) + tuple(outs)]
            else:
                g_, (d_, m_, v_) = g_big[nm], _adamw(f"adamw_{nm}", w[0], g_big[nm], m[0], v[0])
            grads[nm], deltas[nm], new_m[nm], new_v[nm] = g_[None], d_[None], m_[None], v_[None]

    update(list(w_big))
    red1_down = reducing["ffn1_down"].get([deltas[nm] for nm in w_big])
    red1_gu = reducing["ffn1_gu"].get([red1_down])
    g_big.update({"ffn1_gate": red1_gu[0], "ffn1_up": red1_gu[1], "ffn1_down": red1_down[0]})
    w_big.update({
        "ffn1_gate": (ffn1_gate, m_ffn1_gate, v_ffn1_gate), "ffn1_up": (ffn1_up, m_ffn1_up, v_ffn1_up),
        "ffn1_down": (ffn1_down, m_ffn1_down, v_ffn1_down),
    })
    update(["ffn1_gate", "ffn1_up", "ffn1_down"])

    small_names = ["b_mod", "norm_ffn1", "norm_mix", "norm_ffn2", "norm_final", "q_norm", "kv_norm", "sinks", "rel_bias"]
    w_small = (b_mod, norm_ffn1, norm_mix, norm_ffn2, norm_final, q_norm, kv_norm, sinks, rel_bias)
    m_small = (m_b_mod, m_norm_ffn1, m_norm_mix, m_norm_ffn2, m_norm_final, m_q_norm, m_kv_norm, m_sinks, m_rel_bias)
    v_small = (v_b_mod, v_norm_ffn1, v_norm_mix, v_norm_ffn2, v_norm_final, v_q_norm, v_kv_norm, v_sinks, v_rel_bias)
    d_p, m_p, v_p = _adamw("adamw_small", _pack_small(*w_small), total, _pack_small(*m_small), _pack_small(*v_small))
    for nm, g_, d_, m_, v_ in zip(
        small_names,
        (g_b_mod, g_n1, g_n2, g_n3, g_nf, g_qn, g_kvn, g_sinks, g_rel),
        _unpack_small(d_p), _unpack_small(m_p), _unpack_small(v_p),
    ):
        grads[nm], deltas[nm], new_m[nm], new_v[nm] = g_, d_, m_, v_

    order = ["w_mod", "b_mod", "norm_ffn1", "ffn1_gate", "ffn1_up", "ffn1_down", "norm_mix", "w_in", "q_norm", "kv_norm",
             "w_uq", "w_ukv", "sinks", "w_o", "norm_ffn2", "ffn2_gate", "ffn2_up", "ffn2_down", "rel_bias", "norm_final"]
    return (loss, grad_x[None], *[grads[n] for n in order], *[deltas[n] for n in order],
            *[new_m[n] for n in order], *[new_v[n] for n in order])
```

```python
import functools
import math

import jax
import jax.numpy as jnp
import numpy as np
from jax import lax
from jax.experimental import pallas as pl
from jax.experimental.pallas import tpu as pltpu

F32, BF16 = jnp.float32, jnp.bfloat16

D_MODEL = 1024
D_FF = 2816
EPS = 1e-6
N_MOD = 9
SWA_HEADS, SWA_KV_HEADS, SWA_HEAD_DIM, WINDOW = 8, 2, 64, 128
SWA_GROUP = SWA_HEADS // SWA_KV_HEADS
MLA_HEADS, MLA_Q_RANK, MLA_KV_RANK, MLA_NOPE, MLA_ROPE, MLA_V = 4, 256, 128, 128, 64, 128
MLA_QK = MLA_NOPE + MLA_ROPE
ROPE_THETA = 10000.0
NUM_BUCKETS, MAX_DISTANCE = 32, 128
D_IN = 1216
N_SWA_COLS = 768
N_MLA_COLS = 512

ADAM_LR, ADAM_B1, ADAM_B2, ADAM_EPS, ADAM_WD, ADAM_STEP = 0.001, 0.9, 0.999, 1e-08, 0.01, 10

N_CHIPS = 4
N_DEV = 8
F_SHARD = D_FF // N_CHIPS
HALF = F_SHARD // 2
R_IN, R_O, R_UQ, R_UKV = 304, 256, 48, 32
O_IN, O_O, O_UQ, O_UKV, O_PAD = 0, 304, 560, 608, 640

PK_MOD = 72
PK_ROWS = 112
VMEM_LIMIT = 56 << 20
ROW_TILE = 2048

_DN = {
    "nn": (((1,), (0,)), ((), ())),
    "nt": (((1,), (1,)), ((), ())),
    "tn": (((0,), (0,)), ((), ())),
}


def _sds(shape, dtype):
    return jax.ShapeDtypeStruct(tuple(shape), dtype)


def _cparams(sem=None):
    kw = {"vmem_limit_bytes": VMEM_LIMIT}
    if sem is not None:
        kw["dimension_semantics"] = sem
    return pltpu.CompilerParams(**kw)


def _dot(a, b, dims):
    return lax.dot_general(a.astype(BF16), b.astype(BF16), _DN[dims], preferred_element_type=F32)


def _mm(name, dims, grid, a, a_blk, a_idx, b, b_blk, b_idx, out, out_blk, out_idx, acc_shape, alias=None, deps=()):
    nk = grid[2]

    def body(*refs):
        a_ref, b_ref = refs[:2]
        o_ref, acc_ref = refs[-2:]
        part = _dot(a_ref[...], b_ref[...], dims)
        if nk == 1:
            o_ref[...] = part.astype(o_ref.dtype)
            return
        k = pl.program_id(2)

        @pl.when(k == 0)
        def _():
            acc_ref[...] = part

        @pl.when((k > 0) & (k < nk - 1))
        def _():
            acc_ref[...] += part

        @pl.when(k == nk - 1)
        def _():
            o_ref[...] = (acc_ref[...] + part).astype(o_ref.dtype)

    in_specs = [pl.BlockSpec(a_blk, a_idx), pl.BlockSpec(b_blk, b_idx)]
    args = [a, b]
    kw = {}
    if alias is not None:
        in_specs.append(pl.BlockSpec(memory_space=pl.ANY))
        args.append(alias)
        kw["input_output_aliases"] = {2: 0}
    in_specs += [pl.BlockSpec(memory_space=pl.ANY)] * len(deps)
    args += list(deps)
    return pl.pallas_call(
        body,
        name=name,
        grid=grid,
        in_specs=in_specs,
        out_specs=pl.BlockSpec(out_blk, out_idx),
        out_shape=out,
        scratch_shapes=[pltpu.VMEM(acc_shape if nk > 1 else (8, 128), F32)],
        compiler_params=_cparams(("parallel", "parallel", "arbitrary")),
        **kw,
    )(*args)


def _rowwise(name, fn, tiled, full, out_tiled, out_red, tm, deps=()):
    rows = tiled[0].shape[-2]
    grid = (rows // tm,)
    n_in = len(tiled) + len(full)
    n_t = len(out_tiled)
    n_dep = len(deps)

    def tspec(shape):
        nd = len(shape)
        return pl.BlockSpec(tuple(shape[:-2]) + (tm, shape[-1]), lambda i, nd=nd: (0,) * (nd - 2) + (i, 0))

    def fspec(shape):
        nd = len(shape)
        return pl.BlockSpec(tuple(shape), lambda i, nd=nd: (0,) * nd)

    def body(*refs):
        outs = fn(*[r[...] for r in refs[:n_in]])
        if not isinstance(outs, (tuple, list)):
            outs = (outs,)
        out_refs = refs[n_in + n_dep:]
        for r, v in zip(out_refs[:n_t], outs[:n_t]):
            r[...] = v.astype(r.dtype)
        red_refs = out_refs[n_t:]
        if red_refs:
            @pl.when(pl.program_id(0) == 0)
            def _():
                for r in red_refs:
                    r[...] = jnp.zeros_like(r)

            for r, v in zip(red_refs, outs[n_t:]):
                r[...] += v.astype(r.dtype)

    res = pl.pallas_call(
        body,
        name=name,
        grid=grid,
        in_specs=[tspec(a.shape) for a in tiled] + [fspec(a.shape) for a in full]
        + [pl.BlockSpec(memory_space=pl.ANY)] * n_dep,
        out_specs=[tspec(o.shape) for o in out_tiled] + [fspec(o.shape) for o in out_red],
        out_shape=list(out_tiled) + list(out_red),
        compiler_params=_cparams(("arbitrary",) if out_red else ("parallel",)),
    )(*tiled, *full, *deps)
    return res


def _sum0(v):
    return jnp.sum(v, axis=0, keepdims=True)


def _sigmoid(v):
    return 1.0 / (1.0 + jnp.exp(-v))


def _rms(x):
    r = lax.rsqrt(jnp.mean(x * x, axis=-1, keepdims=True) + EPS)
    return x * r, r


def _rms_bwd(u, xr, r):
    return r * (u - xr * jnp.mean(u * xr, axis=-1, keepdims=True))


class _Ready:
    def __init__(self, g):
        self.g = g

    def mid(self, after):
        return []

    def get(self, after):
        return self.g


def _ffn_fwd(tag, x, gamma, sh, sc, gate, gu_src, base, down_src, down_idx, mid_after):
    s_len = x.shape[0]
    deps = gu_src.mid(mid_after)

    def normmod(x_, gamma_, sc_, sh_):
        xr, _ = _rms(x_)
        return xr * gamma_ * (1.0 + sc_) + sh_

    (h,) = _rowwise(f"{tag}_normmod", normmod, [x], [gamma, sc, sh], [_sds((s_len, D_MODEL), BF16)], [], 256, deps=deps)
    g4 = gu_src.get(h)

    tm = min(ROW_TILE, s_len)
    def gate_up(h_ref, wg_ref, wu_ref, ab_ref, s_ref):
        hv = h_ref[...]
        a = _dot(hv, wg_ref[...], "nt")
        b = _dot(hv, wu_ref[...], "nt")
        ab_ref[0] = a.astype(ab_ref.dtype)
        ab_ref[1] = b.astype(ab_ref.dtype)
        s_ref[...] = (a * _sigmoid(a) * b).astype(s_ref.dtype)

    ab4, s3 = pl.pallas_call(
        gate_up,
        name=f"{tag}_gate_up",
        grid=(s_len // tm, N_CHIPS),
        in_specs=[
            pl.BlockSpec((tm, D_MODEL), lambda i, c: (i, 0)),
            pl.BlockSpec((None, None, F_SHARD, D_MODEL), lambda i, c: (c, base, 0, 0)),
            pl.BlockSpec((None, None, F_SHARD, D_MODEL), lambda i, c: (c, base + 1, 0, 0)),
        ],
        out_specs=[
            pl.BlockSpec((None, 2, tm, F_SHARD), lambda i, c: (c, 0, i, 0)),
            pl.BlockSpec((None, tm, F_SHARD), lambda i, c: (c, i, 0)),
        ],
        out_shape=[_sds((N_CHIPS, 2, s_len, F_SHARD), BF16), _sds((N_CHIPS, s_len, F_SHARD), BF16)],
        compiler_params=_cparams(("parallel", "parallel")),
    )(h, g4, g4)
    if down_src is None:
        g_down, ddeps = g4, ()
    else:
        ddeps = down_src.mid(ab4)
        g_down = down_src.get(s3)

    y = _mm(
        f"{tag}_down", "nn", (s_len // tm, 1, N_CHIPS),
        s3, (None, tm, F_SHARD), lambda i, j, k: (k, i, 0),
        g_down, (None, None, F_SHARD, D_MODEL), lambda i, j, k: (k, down_idx, 0, 0),
        _sds((s_len, D_MODEL), F32), (tm, D_MODEL), lambda i, j, k: (i, 0),
        (tm, D_MODEL), deps=ddeps,
    )

    (x_out,) = _rowwise(
        f"{tag}_residual", lambda x_, y_, g_: x_ + 0.5 * g_ * y_, [x, y], [gate], [_sds((s_len, D_MODEL), F32)], [], 256
    )
    return x_out, (g4, base, g_down, down_idx, h, ab4, s3, y)


def _normmod_bwd_call(name, dh_parts, x, dxo, gamma, sc, deps=()):
    s_len = x.shape[0]
    n_parts = len(dh_parts)

    def fn(*vals):
        dh = vals[0]
        for extra in vals[1:n_parts]:
            dh = dh + extra
        x_, dxo_, gamma_, sc_ = vals[n_parts:]
        xr, r = _rms(x_)
        n = xr * gamma_
        dn = dh * (1.0 + sc_)
        dx = dxo_ + _rms_bwd(dn * gamma_, xr, r)
        return dx, _sum0(dh * n), _sum0(dh), _sum0(dn * xr)

    vec = _sds((1, D_MODEL), F32)
    return _rowwise(
        name, fn, list(dh_parts) + [x, dxo], [gamma, sc], [_sds((s_len, D_MODEL), F32)], [vec, vec, vec], 256, deps=deps
    )


class _BwdHooks:
    def __init__(self, after_gate=None, after_swiglu=None, after_wdown=None, after_wgu=None, after_dh=None):
        def nothing(_):
            return []

        self.after_gate = after_gate or nothing
        self.after_swiglu = after_swiglu or nothing
        self.after_wdown = after_wdown or nothing
        self.after_wgu = after_wgu or nothing
        self.after_dh = after_dh or nothing


def _ffn_bwd(tag, dxo, x, gamma, sc, gate, saved, hooks, deps=()):
    g4, base, g_down, down_idx, h, ab4, s3, y = saved
    s_len = x.shape[0]
    vec = _sds((1, D_MODEL), F32)

    dy, dgate = _rowwise(
        f"{tag}_bwd_gate", lambda dxo_, y_, g_: (0.5 * g_ * dxo_, _sum0(0.5 * y_ * dxo_)),
        [dxo, y], [gate], [_sds((s_len, D_MODEL), BF16)], [vec], 256, deps=deps,
    )

    tm = min(ROW_TILE, s_len)

    def d_gate_up(dy_ref, wd_ref, ab_ref, o_ref):
        ds = _dot(dy_ref[...], wd_ref[...], "nt")
        a = ab_ref[0].astype(F32)
        b = ab_ref[1].astype(F32)
        sig = _sigmoid(a)
        o_ref[0] = (ds * b * sig * (1.0 + a * (1.0 - sig))).astype(o_ref.dtype)
        o_ref[1] = (ds * a * sig).astype(o_ref.dtype)

    ab_spec = pl.BlockSpec((None, 2, tm, F_SHARD), lambda i, c: (c, 0, i, 0))
    dab4 = pl.pallas_call(
        d_gate_up,
        name=f"{tag}_bwd_dgate_up",
        grid=(s_len // tm, N_CHIPS),
        in_specs=[
            pl.BlockSpec((tm, D_MODEL), lambda i, c: (i, 0)),
            pl.BlockSpec((None, None, F_SHARD, D_MODEL), lambda i, c: (c, down_idx, 0, 0)),
            ab_spec,
        ],
        out_specs=ab_spec,
        out_shape=_sds(ab4.shape, BF16),
        compiler_params=_cparams(("parallel", "parallel")),
    )(dy, g_down, ab4)

    tk = tm
    gb_down = _mm(
        f"{tag}_bwd_wdown", "tn", (N_CHIPS, 1, s_len // tk),
        s3, (None, tk, F_SHARD), lambda i, j, k: (i, k, 0),
        dy, (tk, D_MODEL), lambda i, j, k: (k, 0),
        _sds((N_CHIPS, 1, F_SHARD, D_MODEL), BF16), (None, None, F_SHARD, D_MODEL), lambda i, j, k: (i, 0, 0, 0),
        (F_SHARD, D_MODEL), deps=hooks.after_swiglu(dab4),
    )
    gb_gu = _mm(
        f"{tag}_bwd_wgu", "tn", (2 * N_CHIPS, 1, s_len // tk),
        dab4, (None, None, tk, F_SHARD), lambda i, j, k: (i % N_CHIPS, i // N_CHIPS, k, 0),
        h, (tk, D_MODEL), lambda i, j, k: (k, 0),
        _sds((N_CHIPS, 2, F_SHARD, D_MODEL), BF16), (None, None, F_SHARD, D_MODEL),
        lambda i, j, k: (i % N_CHIPS, i // N_CHIPS, 0, 0),
        (F_SHARD, D_MODEL), deps=hooks.after_wdown(gb_down),
    )
    dh = _mm(
        f"{tag}_bwd_dh", "nn", (s_len // tm, 1, 2 * N_CHIPS),
        dab4, (None, None, tm, F_SHARD), lambda i, j, k: (k % N_CHIPS, k // N_CHIPS, i, 0),
        g4, (None, None, F_SHARD, D_MODEL), lambda i, j, k: (k % N_CHIPS, base + k // N_CHIPS, 0, 0),
        _sds((s_len, D_MODEL), F32), (tm, D_MODEL), lambda i, j, k: (i, 0),
        (tm, D_MODEL), deps=hooks.after_wgu(gb_gu),
    )
    dx, dsc, dsh, dgamma = _normmod_bwd_call(
        f"{tag}_bwd_normmod", [dh], x, dxo, gamma, sc, deps=hooks.after_dh(dh)
    )
    return dx, (gb_down, gb_gu), (dsh, dsc, dgate, dgamma)


def _bucket_map():
    qi = np.arange(WINDOW)[:, None]
    kj = np.arange(2 * WINDOW)[None, :]
    dist = qi + WINDOW - kj
    band = (dist >= 0) & (dist < WINDOW)
    max_exact = NUM_BUCKETS // 2
    n = np.maximum(dist, 0)
    large = []
    for dt in (np.float32, np.float64):
        nf = np.maximum(n, 1).astype(dt)
        val = np.log(nf / dt(max_exact)) / dt(math.log(MAX_DISTANCE / max_exact)) * dt(NUM_BUCKETS - max_exact)
        large.append(np.minimum(max_exact + val.astype(np.int32), NUM_BUCKETS - 1))
    assert np.array_equal(large[0], large[1])
    bucket = np.where(n < max_exact, n, large[0])
    return np.where(band, bucket, -1).astype(np.int32).reshape(1, -1)


def _bias_expand(rel_bias_t, bkt):
    n = bkt.shape[1]

    def body(rb_ref, bkt_ref, o_ref):
        onehot = (lax.broadcasted_iota(jnp.int32, (NUM_BUCKETS, n), 0) == bkt_ref[...]).astype(F32)
        o_ref[...] = lax.dot_general(
            rb_ref[...], onehot, _DN["nn"], precision=lax.Precision.HIGHEST, preferred_element_type=F32
        )

    return pl.pallas_call(
        body, name="bias_expand", out_shape=_sds((SWA_HEADS, n), F32), compiler_params=_cparams()
    )(rel_bias_t, bkt)


def _bias_reduce(dbias_flat, bkt):
    n = bkt.shape[1]

    def body(db_ref, bkt_ref, o_ref):
        onehot = (lax.broadcasted_iota(jnp.int32, (NUM_BUCKETS, n), 0) == bkt_ref[...]).astype(F32)
        o_ref[...] = lax.dot_general(
            db_ref[...], onehot, _DN["nt"], precision=lax.Precision.HIGHEST, preferred_element_type=F32
        )

    return pl.pallas_call(
        body, name="bias_reduce", out_shape=_sds((SWA_HEADS, NUM_BUCKETS), F32), compiler_params=_cparams()
    )(dbias_flat, bkt)


_SWA_SCALE = SWA_HEAD_DIM ** -0.5
_NEG = -1e30


def _swa_in_specs():
    w = WINDOW
    n_q = SWA_HEADS * SWA_HEAD_DIM
    n_kv = 2 * SWA_KV_HEADS * SWA_HEAD_DIM
    prev = lambda n: jnp.maximum(n - 1, 0)
    return [
        pl.BlockSpec((w, n_q), lambda n: (n, 0)),
        pl.BlockSpec((w, n_kv), lambda n: (prev(n), n_q // n_kv)),
        pl.BlockSpec((w, n_kv), lambda n: (n, n_q // n_kv)),
        pl.BlockSpec((SWA_HEADS, w, 2 * w), lambda n: (0, 0, 0)),
        pl.BlockSpec((SWA_HEADS, w, 1), lambda n: (0, 0, 0)),
    ]


def _head_cols(v, first, count):
    hd = SWA_HEAD_DIM
    return jnp.stack([v[:, (first + i) * hd:(first + i + 1) * hd] for i in range(count)])


def _swa_heads(q_ref, kvp_ref, kvc_ref):
    g, w, hd = SWA_GROUP, WINDOW, SWA_HEAD_DIM
    q = q_ref[...].astype(F32)
    kv = jnp.concatenate([kvp_ref[...], kvc_ref[...]], axis=0).astype(F32)
    heads = []
    for j in range(SWA_KV_HEADS):
        qj = _head_cols(q, g * j, g).reshape(g * w, hd)
        heads.append((qj, _head_cols(kv, j, 1)[0], _head_cols(kv, SWA_KV_HEADS + j, 1)[0]))
    return heads


def _swa_scores(q, kk, bias, n):
    g, w = SWA_GROUP, WINDOW
    s = (_dot(q, kk, "nt") * _SWA_SCALE).reshape(g, w, 2 * w) + bias
    qi = lax.broadcasted_iota(jnp.int32, (w, 2 * w), 0)
    kj = lax.broadcasted_iota(jnp.int32, (w, 2 * w), 1)
    dist = qi + w - kj
    valid = ((dist >= 0) & (dist < w) & ((n > 0) | (kj >= w)))[None]
    return jnp.where(valid, s, _NEG), valid


def _swa_fwd(swa2, bias, sinkb):
    s_len = swa2.shape[0]
    g, w, hd = SWA_GROUP, WINDOW, SWA_HEAD_DIM

    def body(q_ref, kvp_ref, kvc_ref, bias_ref, sink_ref, o_ref, lse_ref):
        n = pl.program_id(0)
        outs = []
        for j, (q, kk, vv) in enumerate(_swa_heads(q_ref, kvp_ref, kvc_ref)):
            hs = slice(g * j, g * (j + 1))
            s, valid = _swa_scores(q, kk, bias_ref[hs], n)
            sink = sink_ref[hs]
            m = jnp.maximum(jnp.max(s, axis=-1, keepdims=True), sink)
            p = jnp.where(valid, jnp.exp(s - m), 0.0)
            l = jnp.sum(p, axis=-1, keepdims=True) + jnp.exp(sink - m)
            o = _dot(p.reshape(g * w, 2 * w), vv, "nn").reshape(g, w, hd) / l
            outs += [o[i] for i in range(g)]
            lse_ref[hs] = m + jnp.log(l)
        o_ref[...] = jnp.concatenate(outs, axis=-1).astype(o_ref.dtype)

    n_q = SWA_HEADS * hd
    return pl.pallas_call(
        body,
        name="swa_fwd",
        grid=(s_len // w,),
        in_specs=_swa_in_specs(),
        out_specs=[
            pl.BlockSpec((w, n_q), lambda n: (n, 0)),
            pl.BlockSpec((SWA_HEADS, w, 1), lambda n: (0, n, 0)),
        ],
        out_shape=[_sds((s_len, n_q), BF16), _sds((SWA_HEADS, s_len, 1), F32)],
        compiler_params=_cparams(("parallel",)),
    )(swa2, swa2, swa2, bias, sinkb)


def _swa_bwd(swa2, bias, sinkb, cat, dcat, lse):
    s_len = swa2.shape[0]
    g, w, hd = SWA_GROUP, WINDOW, SWA_HEAD_DIM

    def body(q_ref, kvp_ref, kvc_ref, bias_ref, sink_ref, o_ref, do_ref, lse_ref,
             dq_ref, dkva_ref, dkvb_ref, dbias_ref, dsink_ref):
        n = pl.program_id(0)

        @pl.when(n == 0)
        def _():
            dbias_ref[...] = jnp.zeros_like(dbias_ref)
            dsink_ref[...] = jnp.zeros_like(dsink_ref)

        o_all = o_ref[...].astype(F32)
        do_all = do_ref[...].astype(F32)
        dqs, dks, dvs = [], [], []
        for j, (q, kk, vv) in enumerate(_swa_heads(q_ref, kvp_ref, kvc_ref)):
            hs = slice(g * j, g * (j + 1))
            s, valid = _swa_scores(q, kk, bias_ref[hs], n)
            lse_v = lse_ref[hs]
            p = jnp.where(valid, jnp.exp(s - lse_v), 0.0)
            do = _head_cols(do_all, g * j, g)
            delta = jnp.sum(do * _head_cols(o_all, g * j, g), axis=-1, keepdims=True)
            do2 = do.reshape(g * w, hd)
            dp = _dot(do2, vv, "nt").reshape(g, w, 2 * w)
            ds = p * (dp - delta)
            dbias_ref[hs] += ds
            dsink_ref[hs] -= jnp.exp(sink_ref[hs] - lse_v) * delta
            ds2 = ds.reshape(g * w, 2 * w)
            dq = (_dot(ds2, kk, "nn") * _SWA_SCALE).reshape(g, w, hd)
            dqs += [dq[i] for i in range(g)]
            dks.append(_dot(ds2, q, "tn") * _SWA_SCALE)
            dvs.append(_dot(p.reshape(g * w, 2 * w), do2, "tn"))
        dq_ref[...] = jnp.concatenate(dqs, axis=-1).astype(dq_ref.dtype)
        dkv = jnp.concatenate(dks + dvs, axis=-1)
        dkvb_ref[...] = dkv[:w]
        dkva_ref[...] = dkv[w:]

    n_q = SWA_HEADS * hd
    n_kv = 2 * SWA_KV_HEADS * hd
    q_spec = pl.BlockSpec((w, n_q), lambda n: (n, 0))
    kv_spec = pl.BlockSpec((w, n_kv), lambda n: (n, 0))
    return pl.pallas_call(
        body,
        name="swa_bwd",
        grid=(s_len // w,),
        in_specs=_swa_in_specs() + [q_spec, q_spec, pl.BlockSpec((SWA_HEADS, w, 1), lambda n: (0, n, 0))],
        out_specs=[
            q_spec, kv_spec, kv_spec,
            pl.BlockSpec((SWA_HEADS, w, 2 * w), lambda n: (0, 0, 0)),
            pl.BlockSpec((SWA_HEADS, w, 1), lambda n: (0, 0, 0)),
        ],
        out_shape=[
            _sds((s_len, n_q), BF16), _sds((s_len, n_kv), F32), _sds((s_len, n_kv), F32),
            _sds((SWA_HEADS, w, 2 * w), F32), _sds((SWA_HEADS, w, 1), F32),
        ],
        compiler_params=_cparams(("arbitrary",)),
    )(swa2, swa2, swa2, bias, sinkb, cat, dcat, lse)


_MLA_SCALE = MLA_QK ** -0.5
_MLA_TQ = 256


def _mla_mask(i, tq, n_keys):
    row = i * tq + lax.broadcasted_iota(jnp.int32, (tq, n_keys), 0)
    col = lax.broadcasted_iota(jnp.int32, (tq, n_keys), 1)
    return col <= row


def _per_query_block(i, n_blocks, fn):
    for ii in range(n_blocks):
        pl.when(i == ii)(functools.partial(fn, ii))


def _mla_fwd(q4, k4, v4):
    s_len = q4.shape[1]
    tq = min(_MLA_TQ, s_len)

    def body(q_ref, k_ref, v_ref, o_ref, lse_ref):
        def block(ii):
            n_keys = (ii + 1) * tq
            mask = _mla_mask(ii, tq, n_keys)
            s = jnp.where(mask, _dot(q_ref[...], k_ref[:n_keys, :], "nt") * _MLA_SCALE, _NEG)
            m = jnp.max(s, axis=-1, keepdims=True)
            p = jnp.exp(s - m)
            l = jnp.sum(p, axis=-1, keepdims=True)
            o_ref[...] = (_dot(p, v_ref[:n_keys, :], "nn") / l).astype(o_ref.dtype)
            lse_ref[...] = m + jnp.log(l)

        _per_query_block(pl.program_id(1), s_len // tq, block)

    return pl.pallas_call(
        body,
        name="mla_fwd",
        grid=(MLA_HEADS, s_len // tq),
        in_specs=[
            pl.BlockSpec((None, tq, MLA_QK), lambda h, i: (h, i, 0)),
            pl.BlockSpec((None, s_len, MLA_QK), lambda h, i: (h, 0, 0)),
            pl.BlockSpec((None, s_len, MLA_V), lambda h, i: (h, 0, 0)),
        ],
        out_specs=[
            pl.BlockSpec((tq, MLA_V), lambda h, i: (i, h)),
            pl.BlockSpec((None, tq, 1), lambda h, i: (h, i, 0)),
        ],
        out_shape=[_sds((s_len, MLA_HEADS * MLA_V), BF16), _sds((MLA_HEADS, s_len, 1), F32)],
        compiler_params=_cparams(("parallel", "parallel")),
    )(q4, k4, v4)


def _mla_bwd(q4, k4, v4, cat, dcat, lse):
    s_len = q4.shape[1]
    tq = min(_MLA_TQ, s_len)
    first = SWA_HEADS * SWA_HEAD_DIM // MLA_V

    def body(q_ref, k_ref, v_ref, o_ref, do_ref, lse_ref, dq_ref, dk_ref, dv_ref):
        i = pl.program_id(1)

        @pl.when(i == 0)
        def _():
            dk_ref[...] = jnp.zeros_like(dk_ref)
            dv_ref[...] = jnp.zeros_like(dv_ref)

        def block(ii):
            n_keys = (ii + 1) * tq
            mask = _mla_mask(ii, tq, n_keys)
            q, k, v, do = q_ref[...], k_ref[:n_keys, :], v_ref[:n_keys, :], do_ref[...]
            s = jnp.where(mask, _dot(q, k, "nt") * _MLA_SCALE, _NEG)
            p = jnp.where(mask, jnp.exp(s - lse_ref[...]), 0.0)
            delta = jnp.sum(do.astype(F32) * o_ref[...].astype(F32), axis=-1, keepdims=True)
            ds = (p * (_dot(do, v, "nt") - delta) * _MLA_SCALE).astype(BF16)
            dq_ref[...] = _dot(ds, k, "nn")
            dk_ref[:n_keys, :] += _dot(ds, q, "tn")
            dv_ref[:n_keys, :] += _dot(p, do, "tn")

        _per_query_block(i, s_len // tq, block)

    return pl.pallas_call(
        body,
        name="mla_bwd",
        grid=(MLA_HEADS, s_len // tq),
        in_specs=[
            pl.BlockSpec((None, tq, MLA_QK), lambda h, i: (h, i, 0)),
            pl.BlockSpec((None, s_len, MLA_QK), lambda h, i: (h, 0, 0)),
            pl.BlockSpec((None, s_len, MLA_V), lambda h, i: (h, 0, 0)),
            pl.BlockSpec((tq, MLA_V), lambda h, i: (i, first + h)),
            pl.BlockSpec((tq, MLA_V), lambda h, i: (i, first + h)),
            pl.BlockSpec((None, tq, 1), lambda h, i: (h, i, 0)),
        ],
        out_specs=[
            pl.BlockSpec((None, tq, MLA_QK), lambda h, i: (h, i, 0)),
            pl.BlockSpec((None, s_len, MLA_QK), lambda h, i: (h, 0, 0)),
            pl.BlockSpec((None, s_len, MLA_V), lambda h, i: (h, 0, 0)),
        ],
        out_shape=[
            _sds((MLA_HEADS, s_len, MLA_QK), F32),
            _sds((MLA_HEADS, s_len, MLA_QK), F32),
            _sds((MLA_HEADS, s_len, MLA_V), F32),
        ],
        compiler_params=_cparams(("parallel", "arbitrary")),
    )(q4, k4, v4, cat, dcat, lse)


def _mla_split(pm):
    q_lat = pm[:, :MLA_Q_RANK]
    kv_lat = pm[:, MLA_Q_RANK:MLA_Q_RANK + MLA_KV_RANK]
    kr = pm[:, MLA_Q_RANK + MLA_KV_RANK:MLA_Q_RANK + MLA_KV_RANK + MLA_ROPE]
    kr_sw = pm[:, MLA_Q_RANK + MLA_KV_RANK + MLA_ROPE:]
    return q_lat, kv_lat, kr, kr_sw


def _mla_proj(pm, cos2, sin2, q_norm, kv_norm, wq_ext, wkv):
    s_len = pm.shape[0]

    def fn(pm_, cos_, sin_, qg, kvg, wq, wk):
        q_lat, kv_lat, kr, kr_sw = _mla_split(pm_)
        nq = (_rms(q_lat)[0] * qg).astype(BF16)
        nkv = (_rms(kv_lat)[0] * kvg).astype(BF16)
        k_rot = kr * cos_ + kr_sw * sin_
        qs, ks, vs = [], [], []
        for h in range(MLA_HEADS):
            qe = _dot(nq, wq[h], "nt")
            q_rot = qe[:, MLA_NOPE:MLA_QK] * cos_ + qe[:, MLA_QK:] * sin_
            qs.append(jnp.concatenate([qe[:, :MLA_NOPE], q_rot], axis=-1))
            kve = _dot(nkv, wk[h], "nt")
            ks.append(jnp.concatenate([kve[:, :MLA_NOPE], k_rot], axis=-1))
            vs.append(kve[:, MLA_NOPE:])
        return jnp.stack(qs), jnp.stack(ks), jnp.stack(vs)

    return _rowwise(
        "mla_proj", fn, [pm, cos2, sin2], [q_norm, kv_norm, wq_ext, wkv],
        [_sds((MLA_HEADS, s_len, MLA_QK), BF16), _sds((MLA_HEADS, s_len, MLA_QK), BF16),
         _sds((MLA_HEADS, s_len, MLA_V), BF16)], [], 256,
    )


def _mla_proj_bwd(pm, cos2, sin2, dq4, dk4, dv4, q_norm, kv_norm, wq_ext, wkv):
    s_len = pm.shape[0]

    def fn(pm_, cos_, sin_, dq, dk, dv, qg, kvg, wq, wk):
        q_lat, kv_lat, _, _ = _mla_split(pm_)
        xq, rq = _rms(q_lat)
        xkv, rkv = _rms(kv_lat)
        nq = (xq * qg).astype(BF16)
        nkv = (xkv * kvg).astype(BF16)
        dnq = jnp.zeros_like(q_lat)
        dnkv = jnp.zeros_like(kv_lat)
        dkr = jnp.zeros_like(cos_)
        dwq, dwk = [], []
        for h in range(MLA_HEADS):
            dy = dq[h][:, MLA_NOPE:]
            dqe = jnp.concatenate([dq[h][:, :MLA_NOPE], dy * cos_, dy * sin_], axis=-1).astype(BF16)
            dnq = dnq + _dot(dqe, wq[h], "nn")
            dwq.append(_dot(dqe, nq, "tn"))
            dkve = jnp.concatenate([dk[h][:, :MLA_NOPE], dv[h]], axis=-1).astype(BF16)
            dnkv = dnkv + _dot(dkve, wk[h], "nn")
            dwk.append(_dot(dkve, nkv, "tn"))
            dkr = dkr + dk[h][:, MLA_NOPE:]
        dq_lat = _rms_bwd(dnq * qg, xq, rq)
        dkv_lat = _rms_bwd(dnkv * kvg, xkv, rkv)
        dpm = jnp.concatenate([dq_lat, dkv_lat, dkr * cos_, dkr * sin_], axis=-1)
        return dpm, _sum0(dnq * xq), _sum0(dnkv * xkv), jnp.stack(dwq), jnp.stack(dwk)

    return _rowwise(
        "mla_proj_bwd", fn, [pm, cos2, sin2, dq4, dk4, dv4], [q_norm, kv_norm, wq_ext, wkv],
        [_sds((s_len, N_MLA_COLS), BF16)],
        [_sds((1, MLA_Q_RANK), F32), _sds((1, MLA_KV_RANK), F32),
         _sds(wq_ext.shape, F32), _sds(wkv.shape, F32)], 256,
    )


def _rope_tables(s_len):
    inv = ROPE_THETA ** (-jnp.arange(0, MLA_ROPE, 2, dtype=F32) / MLA_ROPE)
    ang = jnp.arange(s_len, dtype=F32)[:, None] * inv[None, :]
    cos, sin = jnp.cos(ang), jnp.sin(ang)
    return jnp.concatenate([cos, cos], axis=-1), jnp.concatenate([-sin, sin], axis=-1)


def _mix_weights(g_mix):
    rest = g_mix[:, 0]
    w_in_t = rest[:, O_IN:O_IN + R_IN].reshape(D_IN, D_MODEL)
    w_o = rest[:, O_O:O_O + R_O].reshape(D_MODEL, D_MODEL)
    w_uq_t = rest[:, O_UQ:O_UQ + R_UQ].reshape(MLA_HEADS, MLA_QK, MLA_Q_RANK)
    w_ukv_t = rest[:, O_UKV:O_UKV + R_UKV].reshape(MLA_HEADS, MLA_NOPE + MLA_V, MLA_KV_RANK)
    half = MLA_ROPE // 2
    w_swa = w_in_t[:N_SWA_COLS]
    w_mla = jnp.concatenate([w_in_t[N_SWA_COLS:], w_in_t[D_IN - half:], w_in_t[D_IN - MLA_ROPE:D_IN - half]], axis=0)
    wq_ext = jnp.concatenate([w_uq_t, w_uq_t[:, MLA_QK - half:], w_uq_t[:, MLA_NOPE:MLA_QK - half]], axis=1)
    return w_swa, w_mla, w_o, wq_ext, w_ukv_t


def _mix_fwd(x, gamma, sh, sc, gate, mix_src, q_norm, kv_norm, bias, sinkb, cos2, sin2):
    s_len = x.shape[0]
    tm = min(ROW_TILE, s_len)

    def normmod(x_, gamma_, sc_, sh_):
        return _rms(x_)[0] * gamma_ * (1.0 + sc_) + sh_

    deps = mix_src.mid(x)
    (h,) = _rowwise("mix_normmod", normmod, [x], [gamma, sc, sh], [_sds((s_len, D_MODEL), BF16)], [], 256, deps=deps)
    weights = _mix_weights(mix_src.get(h))
    w_swa, w_mla, w_o, wq_ext, wkv = weights
    swa2 = _mm(
        "mix_proj_swa", "nt", (s_len // tm, 1, 1),
        h, (tm, D_MODEL), lambda i, j, k: (i, 0),
        w_swa, (N_SWA_COLS, D_MODEL), lambda i, j, k: (0, 0),
        _sds((s_len, N_SWA_COLS), BF16), (tm, N_SWA_COLS), lambda i, j, k: (i, 0),
        (tm, N_SWA_COLS),
    )
    pm = _mm(
        "mix_proj_mla", "nt", (s_len // tm, 1, 1),
        h, (tm, D_MODEL), lambda i, j, k: (i, 0),
        w_mla, (N_MLA_COLS, D_MODEL), lambda i, j, k: (0, 0),
        _sds((s_len, N_MLA_COLS), F32), (tm, N_MLA_COLS), lambda i, j, k: (i, 0),
        (tm, N_MLA_COLS),
    )
    q4, k4, v4 = _mla_proj(pm, cos2, sin2, q_norm, kv_norm, wq_ext, wkv)
    oa, lse_a = _swa_fwd(swa2, bias, sinkb)
    ob, lse_b = _mla_fwd(q4, k4, v4)
    cat = jnp.concatenate([oa, ob], axis=1)
    z = _mm(
        "mix_out", "nn", (s_len // tm, 1, 1),
        cat, (tm, D_MODEL), lambda i, j, k: (i, 0),
        w_o, (D_MODEL, D_MODEL), lambda i, j, k: (0, 0),
        _sds((s_len, D_MODEL), F32), (tm, D_MODEL), lambda i, j, k: (i, 0),
        (tm, D_MODEL),
    )
    (x_out,) = _rowwise(
        "mix_residual", lambda x_, z_, g_: x_ + g_ * z_, [x, z], [gate], [_sds((s_len, D_MODEL), F32)], [], 256
    )
    return x_out, (weights, h, swa2, pm, q4, k4, v4, cat, lse_a, lse_b, z)


def _mix_bwd(dxo, x, gamma, sc, gate, q_norm, kv_norm, bias, sinkb, cos2, sin2, saved, hooks):
    weights, h, swa2, pm, q4, k4, v4, cat, lse_a, lse_b, z = saved
    w_swa, w_mla, w_o, wq_ext, wkv = weights
    s_len = x.shape[0]
    tm = tk = min(ROW_TILE, s_len)
    vec = _sds((1, D_MODEL), F32)
    n_proj = N_SWA_COLS + N_MLA_COLS
    half_d = D_MODEL // 2

    dz, dgate = _rowwise(
        "mix_bwd_gate", lambda dxo_, z_, g_: (g_ * dxo_, _sum0(z_ * dxo_)),
        [dxo, z], [gate], [_sds((s_len, D_MODEL), BF16)], [vec], 256,
    )
    dw_o = _mm(
        "mix_bwd_wo", "tn", (2, 1, s_len // tk),
        cat, (tk, half_d), lambda i, j, k: (k, i),
        dz, (tk, D_MODEL), lambda i, j, k: (k, 0),
        _sds((D_MODEL, D_MODEL), F32), (half_d, D_MODEL), lambda i, j, k: (i, 0),
        (half_d, D_MODEL),
    )
    dcat = _mm(
        "mix_bwd_dcat", "nt", (s_len // tm, 1, 1),
        dz, (tm, D_MODEL), lambda i, j, k: (i, 0),
        w_o, (D_MODEL, D_MODEL), lambda i, j, k: (0, 0),
        _sds((s_len, D_MODEL), BF16), (tm, D_MODEL), lambda i, j, k: (i, 0),
        (tm, D_MODEL), deps=hooks.after_gate(dz),
    )
    dq_a, dkva, dkvb, dbias, dsink = _swa_bwd(swa2, bias, sinkb, cat, dcat, lse_a)
    dq4, dk4, dv4 = _mla_bwd(q4, k4, v4, cat, dcat, lse_b)
    dpm, dq_norm, dkv_norm, dwq_ext, dwkv = _mla_proj_bwd(pm, cos2, sin2, dq4, dk4, dv4, q_norm, kv_norm, wq_ext, wkv)

    dkv = dkva + jnp.concatenate([dkvb[WINDOW:], jnp.zeros_like(dkvb[:WINDOW])], axis=0)
    dproj = jnp.concatenate([dq_a, dkv.astype(BF16), dpm], axis=1)
    w_proj = jnp.concatenate([w_swa, w_mla], axis=0)

    dw_proj = _mm(
        "mix_bwd_win", "tn", (n_proj // 256, 1, s_len // tk),
        dproj, (tk, 256), lambda i, j, k: (k, i),
        h, (tk, D_MODEL), lambda i, j, k: (k, 0),
        _sds((n_proj, D_MODEL), F32), (256, D_MODEL), lambda i, j, k: (i, 0),
        (256, D_MODEL),
    )
    dw_swa, dw_mla = dw_proj[:N_SWA_COLS], dw_proj[N_SWA_COLS:]
    dh = _mm(
        "mix_bwd_dh", "nn", (s_len // tm, 1, 1),
        dproj, (tm, n_proj), lambda i, j, k: (i, 0),
        w_proj, (n_proj, D_MODEL), lambda i, j, k: (0, 0),
        _sds((s_len, D_MODEL), F32), (tm, D_MODEL), lambda i, j, k: (i, 0),
        (tm, D_MODEL),
    )
    dx, dsc, dsh, dgamma = _normmod_bwd_call("mix_bwd_normmod", [dh], x, dxo, gamma, sc)

    half = MLA_ROPE // 2
    n_mla_in = D_IN - N_SWA_COLS
    dw_mla_base = dw_mla[:n_mla_in]
    dw_mla_base = dw_mla_base.at[n_mla_in - half:].add(dw_mla[n_mla_in:n_mla_in + half])
    dw_mla_base = dw_mla_base.at[n_mla_in - MLA_ROPE:n_mla_in - half].add(dw_mla[n_mla_in + half:])
    dw_in_t = jnp.concatenate([dw_swa, dw_mla_base], axis=0)
    dw_uq_t = dwq_ext[:, :MLA_QK]
    dw_uq_t = dw_uq_t.at[:, MLA_QK - half:].add(dwq_ext[:, MLA_QK:MLA_QK + half])
    dw_uq_t = dw_uq_t.at[:, MLA_NOPE:MLA_QK - half].add(dwq_ext[:, MLA_QK + half:])
    rest = jnp.concatenate(
        [
            dw_in_t.reshape(N_CHIPS, R_IN, D_MODEL),
            dw_o.reshape(N_CHIPS, R_O, D_MODEL),
            dw_uq_t.reshape(N_CHIPS, R_UQ, D_MODEL),
            dwkv.reshape(N_CHIPS, R_UKV, D_MODEL),
            jnp.zeros((N_CHIPS, F_SHARD - O_PAD, D_MODEL), F32),
        ],
        axis=1,
    ).astype(BF16)
    return dx, rest, (dsh, dsc, dgate, dgamma), dq_norm, dkv_norm, dbias, dsink


def _device_step(x, target, mod, gu1_src, down1_src, mix_src, ffn2_src, norms, q_norm, kv_norm, sinks, rel_bias,
                 hooks2=None, hooks_mix=None, mix_done=None, hooks1=None):
    s_len = x.shape[0]
    sh1, sc1, g1, sh2, sc2, g2, sh3, sc3, g3 = [mod[:, i * D_MODEL:(i + 1) * D_MODEL] for i in range(N_MOD)]
    n1, n2, n3, nf = norms
    bkt = jnp.asarray(_bucket_map())
    bias = _bias_expand(rel_bias.T, bkt).reshape(SWA_HEADS, WINDOW, 2 * WINDOW)
    sinkb = jnp.broadcast_to(sinks.reshape(SWA_HEADS, 1, 1), (SWA_HEADS, WINDOW, 1))
    cos2, sin2 = _rope_tables(s_len)

    x1, saved1 = _ffn_fwd("ffn1", x, n1, sh1, sc1, g1, gu1_src, 0, down1_src, 0, sh1)
    x2, saved2 = _mix_fwd(x1, n2, sh2, sc2, g2, mix_src, q_norm, kv_norm, bias, sinkb, cos2, sin2)
    x3, saved3 = _ffn_fwd("ffn2", x2, n3, sh3, sc3, g3, ffn2_src, 0, None, 2, x2)

    def head(x_, t_, g_):
        xr, r = _rms(x_)
        err = xr * g_ - t_
        dy = err * (1.0 / D_MODEL)
        return _rms_bwd(dy * g_, xr, r), _sum0(err * err), _sum0(dy * xr)

    vec = _sds((1, D_MODEL), F32)
    dx3, sq, dnf = _rowwise("loss_head", head, [x3, target], [nf], [_sds((s_len, D_MODEL), F32)], [vec, vec], 256)
    loss_part = (0.5 / D_MODEL) * jnp.sum(sq)

    dx2, gb2, (dsh3, dsc3, dg3, dn3) = _ffn_bwd("ffn2", dx3, x2, n3, sc3, g3, saved3, hooks2 or _BwdHooks())
    dx1, rest, (dsh2, dsc2, dg2, dn2), dq_norm, dkv_norm, dbias, dsink = _mix_bwd(
        dx2, x1, n2, sc2, g2, q_norm, kv_norm, bias, sinkb, cos2, sin2, saved2, hooks_mix or _BwdHooks()
    )
    rest = rest[:, None]
    deps1 = mix_done(dx1, rest) if mix_done is not None else []
    dx0, gb1, (dsh1, dsc1, dg1, dn1) = _ffn_bwd(
        "ffn1", dx1, x, n1, sc1, g1, saved1, hooks1 or _BwdHooks(), deps=deps1
    )

    dmod = jnp.concatenate([dsh1, dsc1, dg1, dsh2, dsc2, dg2, dsh3, dsc3, dg3], axis=-1)
    drel = _bias_reduce(dbias.reshape(SWA_HEADS, -1), bkt).T
    dsinks = jnp.sum(dsink, axis=(1, 2)).reshape(1, SWA_HEADS)
    small = (dn1, dn2, dn3, dnf, dq_norm, dkv_norm, dsinks, drel)
    return loss_part, dx0, (gb1, gb2, rest), dmod, small


_MESH = pl.DeviceIdType.MESH


def _place():
    return lax.axis_index("x"), lax.axis_index("y"), lax.axis_index("c")


def _other_chips(x, y):
    return [(1 - x, y), (x, 1 - y), (1 - x, 1 - y)]


def _allgather_small(name, blk):
    m_per, n = blk.shape

    def body(x_ref, out_ref, send_sems, recv_sems, local_sem):
        x, y, c = _place()
        me, sibling = (x, y, c), (x, y, 1 - c)
        chips = _other_chips(x, y)

        def rows(px, py, pc):
            return out_ref.at[pl.ds((4 * px + 2 * py + pc) * m_per, m_per), :]

        def copy(k, block, to, src=None):
            return pltpu.make_async_remote_copy(
                src_ref=rows(*block) if src is None else src, dst_ref=rows(*block),
                send_sem=send_sems.at[k], recv_sem=recv_sems.at[k], device_id=to, device_id_type=_MESH,
            )

        mine = pltpu.make_async_copy(x_ref, rows(*me), local_sem)
        mine.start()
        first = [copy(0, me, sibling, src=x_ref)]
        first += [copy(1 + j, me, (*chip, c), src=x_ref) for j, chip in enumerate(chips)]
        for cp in first:
            cp.start()
        passed = [copy(4 + j, (*chip, c), sibling) for j, chip in enumerate(chips)]
        for j, chip in enumerate(chips):
            copy(1 + j, (*chip, c), me).wait_recv()
            passed[j].start()
        copy(0, sibling, me).wait_recv()
        for j, chip in enumerate(chips):
            copy(4 + j, (*chip, 1 - c), me).wait_recv()
        for cp in first + passed:
            cp.wait_send()
        mine.wait()

    return pl.pallas_call(
        body,
        name=name,
        out_shape=_sds((N_DEV * m_per, n), blk.dtype),
        in_specs=[pl.BlockSpec(memory_space=pltpu.VMEM)],
        out_specs=pl.BlockSpec(memory_space=pltpu.VMEM),
        scratch_shapes=[pltpu.SemaphoreType.DMA((7,)), pltpu.SemaphoreType.DMA((7,)), pltpu.SemaphoreType.DMA],
    )(blk)


def _allgather_weights(name, own):
    n = own.shape[0]

    def body(own_ref, g_ref, token, send_sems, recv_sems, local_sem):
        x, y, c = _place()
        sibling = (x, y, 1 - c)
        chips = _other_chips(x, y)
        mine = pltpu.make_async_copy(own_ref, g_ref.at[2 * x + y], local_sem)
        mine.start()
        for j, chip in enumerate(chips):
            for cp in _gather_sends(n, own_ref, g_ref, send_sems.at[j], recv_sems.at[j], x, y, c, chip):
                cp.start()
        for j, chip in enumerate(chips):
            _gather_all(own_ref, g_ref, send_sems.at[j], recv_sems.at[j], chip, c, c).wait_recv()
            for cp in _gather_forwards(n, g_ref, send_sems.at[3 + j], recv_sems.at[3 + j], chip, c, sibling):
                cp.start()
        for j, chip in enumerate(chips):
            _gather_all(own_ref, g_ref, send_sems.at[3 + j], recv_sems.at[3 + j], chip, 1 - c, c).wait_recv()
        for j, chip in enumerate(chips):
            _gather_all(own_ref, g_ref, send_sems.at[j], recv_sems.at[j], chip, c, c).wait_send()
            _gather_all(own_ref, g_ref, send_sems.at[3 + j], recv_sems.at[3 + j], chip, c, c).wait_send()
        mine.wait()
        token[...] = jnp.zeros_like(token)

    return pl.pallas_call(
        body,
        name=name,
        out_shape=[_sds((N_CHIPS,) + own.shape, own.dtype), _sds((8, 128), F32)],
        in_specs=[pl.BlockSpec(memory_space=pl.ANY)],
        out_specs=[pl.BlockSpec(memory_space=pl.ANY), pl.BlockSpec(memory_space=pltpu.VMEM)],
        scratch_shapes=[pltpu.SemaphoreType.DMA((6,)), pltpu.SemaphoreType.DMA((6,)), pltpu.SemaphoreType.DMA],
    )(own)


def _gather_sends(n, own_ref, g_ref, send_sem, recv_sem, x, y, c, chip):
    return [
        pltpu.make_async_remote_copy(
            src_ref=own_ref.at[p, pl.ds(c * HALF, HALF), :], dst_ref=g_ref.at[2 * x + y, p, pl.ds(c * HALF, HALF), :],
            send_sem=send_sem, recv_sem=recv_sem, device_id=(*chip, c), device_id_type=_MESH,
        )
        for p in range(n)
    ]


def _gather_forwards(n, g_ref, send_sem, recv_sem, chip, c, sibling):
    return [
        pltpu.make_async_remote_copy(
            src_ref=g_ref.at[2 * chip[0] + chip[1], p, pl.ds(c * HALF, HALF), :],
            dst_ref=g_ref.at[2 * chip[0] + chip[1], p, pl.ds(c * HALF, HALF), :],
            send_sem=send_sem, recv_sem=recv_sem, device_id=sibling, device_id_type=_MESH,
        )
        for p in range(n)
    ]


def _gather_all(own_ref, g_ref, send_sem, recv_sem, chip, half, c):
    return pltpu.make_async_remote_copy(
        src_ref=own_ref.at[:, pl.ds(c * HALF, HALF), :],
        dst_ref=g_ref.at[2 * chip[0] + chip[1], :, pl.ds(half * HALF, HALF), :],
        send_sem=send_sem, recv_sem=recv_sem, device_id=(*chip, c), device_id_type=_MESH,
    )


_HBM_SPEC = pl.BlockSpec(memory_space=pltpu.HBM)
_SEM_SPEC = pl.BlockSpec(memory_space=pltpu.SEMAPHORE)
_ANY_SPEC = pl.BlockSpec(memory_space=pl.ANY)
_VMEM_SPEC = pl.BlockSpec(memory_space=pltpu.VMEM)
_SPLIT_PARAMS = pltpu.CompilerParams(has_side_effects=pltpu.SideEffectType.DATAFLOW_SIDE_EFFECTING)
_TOKEN = jax.ShapeDtypeStruct((8, 128), F32)


def _in_hbm(a):
    return pltpu.with_memory_space_constraint(a, pltpu.HBM)


class _GatherBehind:
    def __init__(self, tag, own, then=None):
        self.tag, self.n, self.own, self.then = tag, own.shape[0], own, then

    def start(self, after):
        tag, own, n = self.tag, self.own, self.n
        x, y, _ = _place()
        g_init = lax.dynamic_update_slice(
            lax.empty((N_CHIPS,) + own.shape, own.dtype), own[None], (2 * x + y, 0, 0, 0)
        )

        def body(own_ref, g_ref, *rest):
            send_sems, recv_sems, _, _, token = rest[len(after):]
            x, y, c = _place()
            for j, chip in enumerate(_other_chips(x, y)):
                for cp in _gather_sends(n, own_ref, g_ref, send_sems.at[j], recv_sems.at[j], x, y, c, chip):
                    cp.start()
            token[...] = jnp.zeros_like(token)

        self.send1, self.recv1, self.own, self.g, self.token = pl.pallas_call(
            body,
            name=f"{tag}_start",
            out_shape=(
                pltpu.SemaphoreType.DMA((3,)), pltpu.SemaphoreType.DMA((3,)),
                pltpu.HBM(own.shape, own.dtype), pltpu.HBM(g_init.shape, g_init.dtype), _TOKEN,
            ),
            in_specs=(_HBM_SPEC, _HBM_SPEC) + (_ANY_SPEC,) * len(after),
            out_specs=(_SEM_SPEC, _SEM_SPEC, _HBM_SPEC, _HBM_SPEC, _VMEM_SPEC),
            input_output_aliases={0: 2, 1: 3},
            compiler_params=_SPLIT_PARAMS,
        )(_in_hbm(own), _in_hbm(g_init), *after)

    def mid(self, after):
        n = self.n

        def body(own_ref, g_ref, send1, recv1, after_ref, send2, recv2, g_out, token):
            x, y, c = _place()
            sibling = (x, y, 1 - c)
            chips = _other_chips(x, y)
            for j, chip in enumerate(chips):
                _gather_all(own_ref, g_ref, send1.at[j], recv1.at[j], chip, c, c).wait_recv()
                for cp in _gather_forwards(n, g_ref, send2.at[j], recv2.at[j], chip, c, sibling):
                    cp.start()
            for j, chip in enumerate(chips):
                _gather_all(own_ref, g_ref, send1.at[j], recv1.at[j], chip, c, c).wait_send()
            token[...] = jnp.zeros_like(token)

        self.send2, self.recv2, self.g, token = pl.pallas_call(
            body,
            name=f"{self.tag}_mid",
            out_shape=(
                pltpu.SemaphoreType.DMA((3,)), pltpu.SemaphoreType.DMA((3,)),
                pltpu.HBM(self.g.shape, self.g.dtype), _TOKEN,
            ),
            in_specs=(_HBM_SPEC, _HBM_SPEC, _SEM_SPEC, _SEM_SPEC, _ANY_SPEC),
            out_specs=(_SEM_SPEC, _SEM_SPEC, _HBM_SPEC, _VMEM_SPEC),
            input_output_aliases={1: 2},
            compiler_params=_SPLIT_PARAMS,
        )(self.own, self.g, self.send1, self.recv1, after)
        if self.then is None:
            return [token]
        self.then.start([token])
        return [token, self.then.token]

    def get(self, after):
        def body(g_ref, send2, recv2, after_ref, g_out):
            x, y, c = _place()
            for j, chip in enumerate(_other_chips(x, y)):
                slot_in = g_ref.at[2 * chip[0] + chip[1], :, pl.ds((1 - c) * HALF, HALF), :]
                slot_out = g_ref.at[2 * chip[0] + chip[1], :, pl.ds(c * HALF, HALF), :]
                cp = pltpu.make_async_remote_copy(
                    src_ref=slot_out, dst_ref=slot_in, send_sem=send2.at[j], recv_sem=recv2.at[j],
                    device_id=(x, y, 1 - c), device_id_type=_MESH,
                )
                cp.wait_send()
                cp.wait_recv()

        return pl.pallas_call(
            body,
            name=f"{self.tag}_wait",
            out_shape=pltpu.HBM(self.g.shape, self.g.dtype),
            in_specs=(_HBM_SPEC, _SEM_SPEC, _SEM_SPEC, _ANY_SPEC),
            out_specs=_HBM_SPEC,
            input_output_aliases={0: 0},
            compiler_params=_SPLIT_PARAMS,
        )(self.g, self.send2, self.recv2, after)


def _pair_exchange(name, gb):
    def body(gb_ref, land_ref, send_sem, recv_sem):
        x, y, c = _place()
        theirs = gb_ref.at[:, :, pl.ds((1 - c) * HALF, HALF), :]
        cp = pltpu.make_async_remote_copy(
            src_ref=theirs, dst_ref=land_ref, send_sem=send_sem, recv_sem=recv_sem,
            device_id=(x, y, 1 - c), device_id_type=_MESH,
        )
        cp.start()
        cp.wait()

    return pl.pallas_call(
        body,
        name=name,
        out_shape=_sds((N_CHIPS, gb.shape[1], HALF, D_MODEL), gb.dtype),
        in_specs=[pl.BlockSpec(memory_space=pl.ANY)],
        out_specs=pl.BlockSpec(memory_space=pl.ANY),
        scratch_shapes=[pltpu.SemaphoreType.DMA, pltpu.SemaphoreType.DMA],
    )(gb)


def _pair_sum(name, gb, land):
    def body(gb_ref, land_ref, o_ref):
        c = lax.axis_index("c")
        mine = gb_ref[pl.ds(pl.multiple_of(c * HALF, 16), HALF), :]
        o_ref[...] = (mine.astype(F32) + land_ref[...].astype(F32)).astype(o_ref.dtype)

    return pl.pallas_call(
        body,
        name=name,
        grid=(N_CHIPS, gb.shape[1]),
        in_specs=[
            pl.BlockSpec((None, None, F_SHARD, D_MODEL), lambda j, p: (j, p, 0, 0)),
            pl.BlockSpec((None, None, HALF, D_MODEL), lambda j, p: (j, p, 0, 0)),
        ],
        out_specs=pl.BlockSpec((None, None, HALF, D_MODEL), lambda j, p: (j, p, 0, 0)),
        out_shape=_sds(land.shape, BF16),
        compiler_params=_cparams(("parallel", "parallel")),
    )(gb, land)


def _chip_exchange(name, part):
    def body(p_ref, land_ref, send_sems, recv_sems, local_sem):
        x, y, c = _place()
        me = 2 * x + y
        chips = _other_chips(x, y)
        mine = pltpu.make_async_copy(p_ref.at[me], land_ref.at[me], local_sem)
        mine.start()

        def copy(k, chip):
            return pltpu.make_async_remote_copy(
                src_ref=p_ref.at[2 * chip[0] + chip[1]], dst_ref=land_ref.at[me],
                send_sem=send_sems.at[k], recv_sem=recv_sems.at[k], device_id=(*chip, c), device_id_type=_MESH,
            )

        sends = [copy(k, chip) for k, chip in enumerate(chips)]
        for cp in sends:
            cp.start()
        for k, chip in enumerate(chips):
            pltpu.make_async_remote_copy(
                src_ref=p_ref.at[me], dst_ref=land_ref.at[2 * chip[0] + chip[1]],
                send_sem=send_sems.at[k], recv_sem=recv_sems.at[k], device_id=(*chip, c), device_id_type=_MESH,
            ).wait_recv()
        for cp in sends:
            cp.wait_send()
        mine.wait()

    return pl.pallas_call(
        body,
        name=name,
        out_shape=_sds(part.shape, part.dtype),
        in_specs=[pl.BlockSpec(memory_space=pl.ANY)],
        out_specs=pl.BlockSpec(memory_space=pl.ANY),
        scratch_shapes=[pltpu.SemaphoreType.DMA((3,)), pltpu.SemaphoreType.DMA((3,)), pltpu.SemaphoreType.DMA],
    )(part)


def _chip_sum_share(name, land):
    n = land.shape[1]

    def body(l_ref, r_ref, buf, send_sems, recv_sems, local_sems):
        p = pl.program_id(0)
        x, y, c = _place()
        acc = l_ref[0].astype(F32)
        for j in range(1, N_CHIPS):
            acc = acc + l_ref[j].astype(F32)
        buf[p] = acc

        def copies(q):
            mine = r_ref.at[q, pl.ds(c * HALF, HALF), :]
            theirs = r_ref.at[q, pl.ds((1 - c) * HALF, HALF), :]
            local = pltpu.make_async_copy(buf.at[q], mine, local_sems.at[q])
            out = pltpu.make_async_remote_copy(
                src_ref=buf.at[q], dst_ref=mine, send_sem=send_sems.at[q], recv_sem=recv_sems.at[q],
                device_id=(x, y, 1 - c), device_id_type=_MESH,
            )
            arrive = pltpu.make_async_remote_copy(
                src_ref=buf.at[q], dst_ref=theirs, send_sem=send_sems.at[q], recv_sem=recv_sems.at[q],
                device_id=(x, y, 1 - c), device_id_type=_MESH,
            )
            return local, out, arrive

        local, out, _ = copies(p)
        local.start()
        out.start()

        @pl.when(p == n - 1)
        def _():
            for q in range(n):
                local, out, arrive = copies(q)
                arrive.wait_recv()
                out.wait_send()
                local.wait()

    return pl.pallas_call(
        body,
        name=name,
        grid=(n,),
        in_specs=[pl.BlockSpec((N_CHIPS, None, HALF, D_MODEL), lambda p: (0, p, 0, 0))],
        out_specs=pl.BlockSpec(memory_space=pl.ANY),
        out_shape=_sds((n, F_SHARD, D_MODEL), F32),
        scratch_shapes=[
            pltpu.VMEM((n, HALF, D_MODEL), F32),
            pltpu.SemaphoreType.DMA((n,)), pltpu.SemaphoreType.DMA((n,)), pltpu.SemaphoreType.DMA((n,)),
        ],
        compiler_params=_cparams(("arbitrary",)),
    )(land)


class _ReduceBehind:
    def __init__(self, tag, gb, after=()):
        self.tag = tag
        n = gb.shape[1]
        land = lax.empty((N_CHIPS, n, HALF, D_MODEL), gb.dtype)

        def body(gb_ref, land_ref, *rest):
            send_sem, recv_sem, _, _, token = rest[len(after):]
            self._pair_copy(gb_ref, land_ref, send_sem, recv_sem).start()
            token[...] = jnp.zeros_like(token)

        self.send, self.recv, self.gb, self.land, self.token = pl.pallas_call(
            body,
            name=f"grads_pair_start_{tag}",
            out_shape=(
                pltpu.SemaphoreType.DMA((1,)), pltpu.SemaphoreType.DMA((1,)),
                pltpu.HBM(gb.shape, gb.dtype), pltpu.HBM(land.shape, land.dtype), _TOKEN,
            ),
            in_specs=(_HBM_SPEC, _HBM_SPEC) + (_ANY_SPEC,) * len(after),
            out_specs=(_SEM_SPEC, _SEM_SPEC, _HBM_SPEC, _HBM_SPEC, _VMEM_SPEC),
            input_output_aliases={0: 2, 1: 3},
            compiler_params=_SPLIT_PARAMS,
        )(_in_hbm(gb), _in_hbm(land), *after)

    @staticmethod
    def _pair_copy(gb_ref, land_ref, send_sem, recv_sem):
        x, y, c = _place()
        return pltpu.make_async_remote_copy(
            src_ref=gb_ref.at[:, :, pl.ds((1 - c) * HALF, HALF), :], dst_ref=land_ref,
            send_sem=send_sem.at[0], recv_sem=recv_sem.at[0], device_id=(x, y, 1 - c), device_id_type=_MESH,
        )

    @staticmethod
    def _chip_copies(p_ref, land_ref, send_sems, recv_sems):
        x, y, c = _place()
        me = 2 * x + y
        sends, arrivals = [], []
        for k, chip in enumerate(_other_chips(x, y)):
            them = 2 * chip[0] + chip[1]
            sends.append(pltpu.make_async_remote_copy(
                src_ref=p_ref.at[them], dst_ref=land_ref.at[me], send_sem=send_sems.at[k], recv_sem=recv_sems.at[k],
                device_id=(*chip, c), device_id_type=_MESH,
            ))
            arrivals.append(pltpu.make_async_remote_copy(
                src_ref=p_ref.at[me], dst_ref=land_ref.at[them], send_sem=send_sems.at[k], recv_sem=recv_sems.at[k],
                device_id=(*chip, c), device_id_type=_MESH,
            ))
        return sends, arrivals

    def mid(self, after):
        tag = self.tag

        def wait_body(gb_ref, land_ref, send_sem, recv_sem, after_ref, gb_out, land_out):
            cp = self._pair_copy(gb_ref, land_ref, send_sem, recv_sem)
            cp.wait_send()
            cp.wait_recv()

        gb, land = pl.pallas_call(
            wait_body,
            name=f"grads_pair_wait_{tag}",
            out_shape=(pltpu.HBM(self.gb.shape, self.gb.dtype), pltpu.HBM(self.land.shape, self.land.dtype)),
            in_specs=(_HBM_SPEC, _HBM_SPEC, _SEM_SPEC, _SEM_SPEC, _ANY_SPEC),
            out_specs=(_HBM_SPEC, _HBM_SPEC),
            input_output_aliases={0: 0, 1: 1},
            compiler_params=_SPLIT_PARAMS,
        )(self.gb, self.land, self.send, self.recv, after)
        part = _pair_sum(f"grads_pair_sum_{tag}", gb, land)

        x, y, _ = _place()
        start = (2 * x + y, 0, 0, 0)
        own = lax.dynamic_slice(part, start, (1,) + part.shape[1:])
        land2 = lax.dynamic_update_slice(lax.empty(part.shape, part.dtype), own, start)

        def start_body(p_ref, land_ref, send_sems, recv_sems, p_out, land_out, token):
            for cp in self._chip_copies(p_ref, land_ref, send_sems, recv_sems)[0]:
                cp.start()
            token[...] = jnp.zeros_like(token)

        self.send2, self.recv2, self.part, self.land2, token = pl.pallas_call(
            start_body,
            name=f"grads_chip_start_{tag}",
            out_shape=(
                pltpu.SemaphoreType.DMA((3,)), pltpu.SemaphoreType.DMA((3,)),
                pltpu.HBM(part.shape, part.dtype), pltpu.HBM(land2.shape, land2.dtype), _TOKEN,
            ),
            in_specs=(_HBM_SPEC, _HBM_SPEC),
            out_specs=(_SEM_SPEC, _SEM_SPEC, _HBM_SPEC, _HBM_SPEC, _VMEM_SPEC),
            input_output_aliases={0: 2, 1: 3},
            compiler_params=_SPLIT_PARAMS,
        )(_in_hbm(part), _in_hbm(land2))
        return [token]

    def get(self, after):
        n_after = len(after)

        def wait_body(p_ref, land_ref, send_sems, recv_sems, *rest):
            sends, arrivals = self._chip_copies(p_ref, land_ref, send_sems, recv_sems)
            for cp in arrivals:
                cp.wait_recv()
            for cp in sends:
                cp.wait_send()

        _, land2 = pl.pallas_call(
            wait_body,
            name=f"grads_chip_wait_{self.tag}",
            out_shape=(pltpu.HBM(self.part.shape, self.part.dtype), pltpu.HBM(self.land2.shape, self.land2.dtype)),
            in_specs=(_HBM_SPEC, _HBM_SPEC, _SEM_SPEC, _SEM_SPEC) + (_ANY_SPEC,) * n_after,
            out_specs=(_HBM_SPEC, _HBM_SPEC),
            input_output_aliases={0: 0, 1: 1},
            compiler_params=_SPLIT_PARAMS,
        )(self.part, self.land2, self.send2, self.recv2, *after)
        return _chip_sum_share(f"grads_chip_sum_share_{self.tag}", land2)


def _allreduce_small(blk):
    gathered = _allgather_small("allgather_small_grads", blk).reshape(N_DEV, PK_ROWS, 128)

    def body(g_ref, o_ref):
        acc = g_ref[0]
        for k in range(1, N_DEV):
            acc = acc + g_ref[k]
        o_ref[...] = acc

    total = pl.pallas_call(body, name="small_grads_sum", out_shape=_sds((PK_ROWS, 128), F32))(gathered)
    return gathered, total


def _adamw(name, w, g, m, v):
    rows, cols = w.shape
    tm = rows
    while tm * cols * 4 > (1 << 20) and tm % 16 == 0:
        tm //= 2

    def fn(w_, g_, m_, v_):
        m_new = ADAM_B1 * m_ + (1.0 - ADAM_B1) * g_
        v_new = ADAM_B2 * v_ + (1.0 - ADAM_B2) * (g_ * g_)
        m_hat = m_new / (1.0 - ADAM_B1 ** ADAM_STEP)
        v_hat = v_new / (1.0 - ADAM_B2 ** ADAM_STEP)
        delta = -ADAM_LR * (m_hat / (jnp.sqrt(v_hat) + ADAM_EPS) + ADAM_WD * w_)
        return delta, m_new, v_new

    out = _sds(w.shape, F32)
    return _rowwise(name, fn, [w, g, m, v], [], [out, out, out], [], tm)


def _pack_small(b_mod_like, n1, n2, n3, nf, qn, kvn, sinks, rel):
    parts = [
        b_mod_like.reshape(PK_MOD, 128),
        n1.reshape(8, 128), n2.reshape(8, 128), n3.reshape(8, 128), nf.reshape(8, 128),
        qn.reshape(2, 128), kvn.reshape(1, 128),
        jnp.pad(sinks.reshape(1, SWA_HEADS), ((0, 0), (0, 128 - SWA_HEADS))),
        rel.reshape(2, 128),
        jnp.zeros((2, 128), F32),
    ]
    return jnp.concatenate(parts, axis=0)


def _unpack_small(p):
    o = PK_MOD
    return (
        p[:o].reshape(1, N_MOD * D_MODEL),
        p[o:o + 8].reshape(1, D_MODEL), p[o + 8:o + 16].reshape(1, D_MODEL),
        p[o + 16:o + 24].reshape(1, D_MODEL), p[o + 24:o + 32].reshape(D_MODEL),
        p[o + 32:o + 34].reshape(1, MLA_Q_RANK), p[o + 34:o + 35].reshape(1, MLA_KV_RANK),
        p[o + 35:o + 36, :SWA_HEADS].reshape(1, SWA_HEADS),
        p[o + 36:o + 38].reshape(NUM_BUCKETS, SWA_HEADS),
    )


def kernel(x, c, w_mod, b_mod, norm_ffn1, ffn1_gate, ffn1_up, ffn1_down, norm_mix, w_in, q_norm, kv_norm, w_uq, w_ukv, sinks, w_o, norm_ffn2, ffn2_gate, ffn2_up, ffn2_down, rel_bias, norm_final, loss_target, m_w_mod, m_b_mod, m_norm_ffn1, m_ffn1_gate, m_ffn1_up, m_ffn1_down, m_norm_mix, m_w_in, m_q_norm, m_kv_norm, m_w_uq, m_w_ukv, m_sinks, m_w_o, m_norm_ffn2, m_ffn2_gate, m_ffn2_up, m_ffn2_down, m_rel_bias, m_norm_final, v_w_mod, v_b_mod, v_norm_ffn1, v_ffn1_gate, v_ffn1_up, v_ffn1_down, v_norm_mix, v_w_in, v_q_norm, v_kv_norm, v_w_uq, v_w_ukv, v_sinks, v_w_o, v_norm_ffn2, v_ffn2_gate, v_ffn2_up, v_ffn2_down, v_rel_bias, v_norm_final):
    ax, ay, ac = _place()
    chip = 2 * ax + ay
    dev = 2 * chip + ac
    n_mod_shard = N_MOD * D_MODEL // N_CHIPS

    def t16(w):
        return w[0].T.astype(BF16)

    rest = jnp.concatenate(
        [
            t16(w_in),
            w_o[0].astype(BF16),
            t16(w_uq).reshape(R_UQ, D_MODEL),
            t16(w_ukv).reshape(R_UKV, D_MODEL),
            jnp.zeros((F_SHARD - O_PAD, D_MODEL), BF16),
        ],
        axis=0,
    )
    own_gu1 = jnp.stack([t16(ffn1_gate), t16(ffn1_up)])
    own_down1 = ffn1_down[0].astype(BF16)[None]
    own_mix = rest[None]
    own_ffn2 = jnp.stack([t16(ffn2_gate), t16(ffn2_up), ffn2_down[0].astype(BF16)])

    (c_act,) = _rowwise(
        "silu_c", lambda v: v * _sigmoid(v), [c.reshape(8, 128)], [], [_sds((8, 128), F32)], [], 8
    )
    c_all = _allgather_small("allgather_c", c_act).reshape(N_DEV, D_MODEL)
    mod_cols = _mm(
        "mod_fwd", "nn", (1, n_mod_shard // 768, 1),
        c_all, (N_DEV, D_MODEL), lambda i, j, k: (0, 0),
        w_mod[0], (D_MODEL, 768), lambda i, j, k: (0, j),
        _sds((N_DEV, n_mod_shard), F32), (N_DEV, 768), lambda i, j, k: (0, j),
        (N_DEV, 768),
    )
    mod_all = _allgather_small("allgather_mod", mod_cols).reshape(N_CHIPS, 2, N_DEV, n_mod_shard)[:, 0]
    mod_all = mod_all.transpose(1, 0, 2).reshape(N_DEV, N_MOD * D_MODEL)
    mod = lax.dynamic_slice_in_dim(mod_all, dev, 1, axis=0) + b_mod

    norms = (norm_ffn1, norm_mix, norm_ffn2, norm_final.reshape(1, D_MODEL))

    ffn2_src = _GatherBehind("gather_ffn2", own_ffn2)
    mix_src = _GatherBehind("gather_mix", own_mix, then=ffn2_src)
    down1_src = _GatherBehind("gather_down1", own_down1, then=mix_src)
    gu1_src = _GatherBehind("gather_gu1", own_gu1, then=down1_src)
    gu1_src.start([mod_all])

    reducing, red = {}, {}

    def begin(tag, gb, after):
        reducing[tag] = _ReduceBehind(tag, gb, after=after)
        return [reducing[tag].token]

    def ffn2_gu_done(gb):
        return begin("ffn2_gu", gb, reducing["ffn2_down"].mid(gb))

    def mix_done(dx1, gb_rest):
        red["ffn2_down"] = reducing["ffn2_down"].get([dx1])
        red["ffn2_gu"] = reducing["ffn2_gu"].get([red["ffn2_down"]])
        return begin("rest", gb_rest, [red["ffn2_gu"]])

    def ffn1_gu_done(gb):
        red["rest"] = reducing["rest"].get([gb])
        begin("ffn1_gu", gb, reducing["ffn1_down"].mid(red["rest"]))
        return reducing["ffn1_gu"].mid(reducing["ffn1_gu"].token)

    loss_part, grad_x, _, dmod, small = _device_step(
        x[0], loss_target[0], mod, gu1_src, down1_src, mix_src, ffn2_src, norms, q_norm, kv_norm, sinks, rel_bias,
        hooks2=_BwdHooks(after_wdown=lambda gb: begin("ffn2_down", gb, ()), after_wgu=ffn2_gu_done),
        hooks_mix=_BwdHooks(after_gate=lambda dz: reducing["ffn2_gu"].mid(dz)),
        mix_done=mix_done,
        hooks1=_BwdHooks(
            after_swiglu=lambda dab3: reducing["rest"].mid(dab3),
            after_wdown=lambda gb: begin("ffn1_down", gb, ()),
            after_wgu=ffn1_gu_done,
        ),
    )
    loss = lax.psum(loss_part, ("x", "y", "c"))

    gathered, total = _allreduce_small(_pack_small(dmod, *small))
    (g_b_mod, g_n1, g_n2, g_n3, g_nf, g_qn, g_kvn, g_sinks, g_rel) = _unpack_small(total)
    dmod_all = gathered[:, :PK_MOD].reshape(N_DEV, N_MOD * D_MODEL)
    dmod_cols = lax.dynamic_slice_in_dim(dmod_all, chip * n_mod_shard, n_mod_shard, axis=1)
    g_w_mod = _mm(
        "mod_bwd", "tn", (1, n_mod_shard // 768, 1),
        c_all, (N_DEV, D_MODEL), lambda i, j, k: (0, 0),
        dmod_cols, (N_DEV, 768), lambda i, j, k: (0, j),
        _sds((D_MODEL, n_mod_shard), F32), (D_MODEL, 768), lambda i, j, k: (0, j),
        (D_MODEL, 768),
    )

    r_rest = red["rest"][0]
    transposed = {"ffn1_gate", "ffn1_up", "ffn2_gate", "ffn2_up", "w_in", "w_uq", "w_ukv"}
    g_big = {
        "ffn2_gate": red["ffn2_gu"][0], "ffn2_up": red["ffn2_gu"][1], "ffn2_down": red["ffn2_down"][0],
        "w_in": r_rest[O_IN:O_IN + R_IN],
        "w_o": r_rest[O_O:O_O + R_O],
        "w_uq": r_rest[O_UQ:O_UQ + R_UQ].reshape(MLA_QK, MLA_Q_RANK),
        "w_ukv": r_rest[O_UKV:O_UKV + R_UKV].reshape(MLA_NOPE + MLA_V, MLA_KV_RANK),
        "w_mod": g_w_mod,
    }
    w_big = {
        "ffn2_gate": (ffn2_gate, m_ffn2_gate, v_ffn2_gate),
        "ffn2_up": (ffn2_up, m_ffn2_up, v_ffn2_up), "ffn2_down": (ffn2_down, m_ffn2_down, v_ffn2_down),
        "w_in": (w_in, m_w_in, v_w_in), "w_o": (w_o, m_w_o, v_w_o), "w_uq": (w_uq, m_w_uq, v_w_uq),
        "w_ukv": (w_ukv, m_w_ukv, v_w_ukv), "w_mod": (w_mod, m_w_mod, v_w_mod),
    }
    grads, deltas, new_m, new_v = {}, {}, {}, {}

    def update(names):
        for nm in names:
            w, m, v = w_big[nm]
            if nm in transposed:
                outs = _adamw(f"adamw_{nm}", w[0].T, g_big[nm], m[0].T, v[0].T)
                g_, d_, m_, v_ = [a.T for a in (g_big[nm],) + tuple(outs)]
            else:
                g_, (d_, m_, v_) = g_big[nm], _adamw(f"adamw_{nm}", w[0], g_big[nm], m[0], v[0])
            grads[nm], deltas[nm], new_m[nm], new_v[nm] = g_[None], d_[None], m_[None], v_[None]

    update(list(w_big))
    red1_down = reducing["ffn1_down"].get([deltas[nm] for nm in w_big])
    red1_gu = reducing["ffn1_gu"].get([red1_down])
    g_big.update({"ffn1_gate": red1_gu[0], "ffn1_up": red1_gu[1], "ffn1_down": red1_down[0]})
    w_big.update({
        "ffn1_gate": (ffn1_gate, m_ffn1_gate, v_ffn1_gate), "ffn1_up": (ffn1_up, m_ffn1_up, v_ffn1_up),
        "ffn1_down": (ffn1_down, m_ffn1_down, v_ffn1_down),
    })
    update(["ffn1_gate", "ffn1_up", "ffn1_down"])

    small_names = ["b_mod", "norm_ffn1", "norm_mix", "norm_ffn2", "norm_final", "q_norm", "kv_norm", "sinks", "rel_bias"]
    w_small = (b_mod, norm_ffn1, norm_mix, norm_ffn2, norm_final, q_norm, kv_norm, sinks, rel_bias)
    m_small = (m_b_mod, m_norm_ffn1, m_norm_mix, m_norm_ffn2, m_norm_final, m_q_norm, m_kv_norm, m_sinks, m_rel_bias)
    v_small = (v_b_mod, v_norm_ffn1, v_norm_mix, v_norm_ffn2, v_norm_final, v_q_norm, v_kv_norm, v_sinks, v_rel_bias)
    d_p, m_p, v_p = _adamw("adamw_small", _pack_small(*w_small), total, _pack_small(*m_small), _pack_small(*v_small))
    for nm, g_, d_, m_, v_ in zip(
        small_names,
        (g_b_mod, g_n1, g_n2, g_n3, g_nf, g_qn, g_kvn, g_sinks, g_rel),
        _unpack_small(d_p), _unpack_small(m_p), _unpack_small(v_p),
    ):
        grads[nm], deltas[nm], new_m[nm], new_v[nm] = g_, d_, m_, v_

    order = ["w_mod", "b_mod", "norm_ffn1", "ffn1_gate", "ffn1_up", "ffn1_down", "norm_mix", "w_in", "q_norm", "kv_norm",
             "w_uq", "w_ukv", "sinks", "w_o", "norm_ffn2", "ffn2_gate", "ffn2_up", "ffn2_down", "rel_bias", "norm_final"]
    return (loss, grad_x[None], *[grads[n] for n in order], *[deltas[n] for n in order],
            *[new_m[n] for n in order], *[new_v[n] for n in order])
```

```python
import functools
import math

import jax
import jax.numpy as jnp
import numpy as np
from jax import lax
from jax.experimental import pallas as pl
from jax.experimental.pallas import tpu as pltpu

F32, BF16 = jnp.float32, jnp.bfloat16

D_MODEL = 1024
D_FF = 2816
EPS = 1e-6
N_MOD = 9
SWA_HEADS, SWA_KV_HEADS, SWA_HEAD_DIM, WINDOW = 8, 2, 64, 128
SWA_GROUP = SWA_HEADS // SWA_KV_HEADS
MLA_HEADS, MLA_Q_RANK, MLA_KV_RANK, MLA_NOPE, MLA_ROPE, MLA_V = 4, 256, 128, 128, 64, 128
MLA_QK = MLA_NOPE + MLA_ROPE
ROPE_THETA = 10000.0
NUM_BUCKETS, MAX_DISTANCE = 32, 128
D_IN = 1216
N_SWA_COLS = 768
N_MLA_COLS = 512

ADAM_LR, ADAM_B1, ADAM_B2, ADAM_EPS, ADAM_WD, ADAM_STEP = 0.001, 0.9, 0.999, 1e-08, 0.01, 10

N_CHIPS = 4
N_DEV = 8
F_SHARD = D_FF // N_CHIPS
HALF = F_SHARD // 2
R_IN, R_O, R_UQ, R_UKV = 304, 256, 48, 32
O_IN, O_O, O_UQ, O_UKV, O_PAD = 0, 304, 560, 608, 640

PK_MOD = 72
PK_ROWS = 112
VMEM_LIMIT = 56 << 20
ROW_TILE = 2048

_DN = {
    "nn": (((1,), (0,)), ((), ())),
    "nt": (((1,), (1,)), ((), ())),
    "tn": (((0,), (0,)), ((), ())),
}


def _sds(shape, dtype):
    return jax.ShapeDtypeStruct(tuple(shape), dtype)


def _cparams(sem=None):
    kw = {"vmem_limit_bytes": VMEM_LIMIT}
    if sem is not None:
        kw["dimension_semantics"] = sem
    return pltpu.CompilerParams(**kw)


def _dot(a, b, dims):
    return lax.dot_general(a.astype(BF16), b.astype(BF16), _DN[dims], preferred_element_type=F32)


def _mm(name, dims, grid, a, a_blk, a_idx, b, b_blk, b_idx, out, out_blk, out_idx, acc_shape, alias=None, deps=()):
    nk = grid[2]

    def body(*refs):
        a_ref, b_ref = refs[:2]
        o_ref, acc_ref = refs[-2:]
        part = _dot(a_ref[...], b_ref[...], dims)
        if nk == 1:
            o_ref[...] = part.astype(o_ref.dtype)
            return
        k = pl.program_id(2)

        @pl.when(k == 0)
        def _():
            acc_ref[...] = part

        @pl.when((k > 0) & (k < nk - 1))
        def _():
            acc_ref[...] += part

        @pl.when(k == nk - 1)
        def _():
            o_ref[...] = (acc_ref[...] + part).astype(o_ref.dtype)

    in_specs = [pl.BlockSpec(a_blk, a_idx), pl.BlockSpec(b_blk, b_idx)]
    args = [a, b]
    kw = {}
    if alias is not None:
        in_specs.append(pl.BlockSpec(memory_space=pl.ANY))
        args.append(alias)
        kw["input_output_aliases"] = {2: 0}
    in_specs += [pl.BlockSpec(memory_space=pl.ANY)] * len(deps)
    args += list(deps)
    return pl.pallas_call(
        body,
        name=name,
        grid=grid,
        in_specs=in_specs,
        out_specs=pl.BlockSpec(out_blk, out_idx),
        out_shape=out,
        scratch_shapes=[pltpu.VMEM(acc_shape if nk > 1 else (8, 128), F32)],
        compiler_params=_cparams(("parallel", "parallel", "arbitrary")),
        **kw,
    )(*args)


def _rowwise(name, fn, tiled, full, out_tiled, out_red, tm, deps=()):
    rows = tiled[0].shape[-2]
    grid = (rows // tm,)
    n_in = len(tiled) + len(full)
    n_t = len(out_tiled)
    n_dep = len(deps)

    def tspec(shape):
        nd = len(shape)
        return pl.BlockSpec(tuple(shape[:-2]) + (tm, shape[-1]), lambda i, nd=nd: (0,) * (nd - 2) + (i, 0))

    def fspec(shape):
        nd = len(shape)
        return pl.BlockSpec(tuple(shape), lambda i, nd=nd: (0,) * nd)

    def body(*refs):
        outs = fn(*[r[...] for r in refs[:n_in]])
        if not isinstance(outs, (tuple, list)):
            outs = (outs,)
        out_refs = refs[n_in + n_dep:]
        for r, v in zip(out_refs[:n_t], outs[:n_t]):
            r[...] = v.astype(r.dtype)
        red_refs = out_refs[n_t:]
        if red_refs:
            @pl.when(pl.program_id(0) == 0)
            def _():
                for r in red_refs:
                    r[...] = jnp.zeros_like(r)

            for r, v in zip(red_refs, outs[n_t:]):
                r[...] += v.astype(r.dtype)

    res = pl.pallas_call(
        body,
        name=name,
        grid=grid,
        in_specs=[tspec(a.shape) for a in tiled] + [fspec(a.shape) for a in full]
        + [pl.BlockSpec(memory_space=pl.ANY)] * n_dep,
        out_specs=[tspec(o.shape) for o in out_tiled] + [fspec(o.shape) for o in out_red],
        out_shape=list(out_tiled) + list(out_red),
        compiler_params=_cparams(("arbitrary",) if out_red else ("parallel",)),
    )(*tiled, *full, *deps)
    return res


def _sum0(v):
    return jnp.sum(v, axis=0, keepdims=True)


def _sigmoid(v):
    return 1.0 / (1.0 + jnp.exp(-v))


def _rms(x):
    r = lax.rsqrt(jnp.mean(x * x, axis=-1, keepdims=True) + EPS)
    return x * r, r


def _rms_bwd(u, xr, r):
    return r * (u - xr * jnp.mean(u * xr, axis=-1, keepdims=True))


class _Ready:
    def __init__(self, g):
        self.g = g

    def mid(self, after):
        return []

    def get(self, after):
        return self.g


def _ffn_fwd(tag, x, gamma, sh, sc, gate, gu_src, base, down_src, down_idx, mid_after):
    s_len = x.shape[0]
    deps = gu_src.mid(mid_after)

    def normmod(x_, gamma_, sc_, sh_):
        xr, _ = _rms(x_)
        return xr * gamma_ * (1.0 + sc_) + sh_

    (h,) = _rowwise(f"{tag}_normmod", normmod, [x], [gamma, sc, sh], [_sds((s_len, D_MODEL), BF16)], [], 256, deps=deps)
    g4 = gu_src.get(h)

    tm = min(ROW_TILE, s_len)
    def gate_up(h_ref, wg_ref, wu_ref, ab_ref, s_ref):
        hv = h_ref[...]
        a = _dot(hv, wg_ref[...], "nt")
        b = _dot(hv, wu_ref[...], "nt")
        ab_ref[0] = a.astype(ab_ref.dtype)
        ab_ref[1] = b.astype(ab_ref.dtype)
        s_ref[...] = (a * _sigmoid(a) * b).astype(s_ref.dtype)

    ab4, s3 = pl.pallas_call(
        gate_up,
        name=f"{tag}_gate_up",
        grid=(s_len // tm, N_CHIPS),
        in_specs=[
            pl.BlockSpec((tm, D_MODEL), lambda i, c: (i, 0)),
            pl.BlockSpec((None, None, F_SHARD, D_MODEL), lambda i, c: (c, base, 0, 0)),
            pl.BlockSpec((None, None, F_SHARD, D_MODEL), lambda i, c: (c, base + 1, 0, 0)),
        ],
        out_specs=[
            pl.BlockSpec((None, 2, tm, F_SHARD), lambda i, c: (c, 0, i, 0)),
            pl.BlockSpec((None, tm, F_SHARD), lambda i, c: (c, i, 0)),
        ],
        out_shape=[_sds((N_CHIPS, 2, s_len, F_SHARD), BF16), _sds((N_CHIPS, s_len, F_SHARD), BF16)],
        compiler_params=_cparams(("parallel", "parallel")),
    )(h, g4, g4)
    if down_src is None:
        g_down, ddeps = g4, ()
    else:
        ddeps = down_src.mid(ab4)
        g_down = down_src.get(s3)

    y = _mm(
        f"{tag}_down", "nn", (s_len // tm, 1, N_CHIPS),
        s3, (None, tm, F_SHARD), lambda i, j, k: (k, i, 0),
        g_down, (None, None, F_SHARD, D_MODEL), lambda i, j, k: (k, down_idx, 0, 0),
        _sds((s_len, D_MODEL), F32), (tm, D_MODEL), lambda i, j, k: (i, 0),
        (tm, D_MODEL), deps=ddeps,
    )

    (x_out,) = _rowwise(
        f"{tag}_residual", lambda x_, y_, g_: x_ + 0.5 * g_ * y_, [x, y], [gate], [_sds((s_len, D_MODEL), F32)], [], 256
    )
    return x_out, (g4, base, g_down, down_idx, h, ab4, s3, y)


def _normmod_bwd_call(name, dh_parts, x, dxo, gamma, sc, deps=()):
    s_len = x.shape[0]
    n_parts = len(dh_parts)

    def fn(*vals):
        dh = vals[0]
        for extra in vals[1:n_parts]:
            dh = dh + extra
        x_, dxo_, gamma_, sc_ = vals[n_parts:]
        xr, r = _rms(x_)
        n = xr * gamma_
        dn = dh * (1.0 + sc_)
        dx = dxo_ + _rms_bwd(dn * gamma_, xr, r)
        return dx, _sum0(dh * n), _sum0(dh), _sum0(dn * xr)

    vec = _sds((1, D_MODEL), F32)
    return _rowwise(
        name, fn, list(dh_parts) + [x, dxo], [gamma, sc], [_sds((s_len, D_MODEL), F32)], [vec, vec, vec], 256, deps=deps
    )


class _BwdHooks:
    def __init__(self, after_gate=None, after_swiglu=None, after_wdown=None, after_wgu=None, after_dh=None):
        def nothing(_):
            return []

        self.after_gate = after_gate or nothing
        self.after_swiglu = after_swiglu or nothing
        self.after_wdown = after_wdown or nothing
        self.after_wgu = after_wgu or nothing
        self.after_dh = after_dh or nothing


def _ffn_bwd(tag, dxo, x, gamma, sc, gate, saved, hooks, deps=()):
    g4, base, g_down, down_idx, h, ab4, s3, y = saved
    s_len = x.shape[0]
    vec = _sds((1, D_MODEL), F32)

    dy, dgate = _rowwise(
        f"{tag}_bwd_gate", lambda dxo_, y_, g_: (0.5 * g_ * dxo_, _sum0(0.5 * y_ * dxo_)),
        [dxo, y], [gate], [_sds((s_len, D_MODEL), BF16)], [vec], 256, deps=deps,
    )

    tm = min(ROW_TILE, s_len)

    def d_gate_up(dy_ref, wd_ref, ab_ref, o_ref):
        ds = _dot(dy_ref[...], wd_ref[...], "nt")
        a = ab_ref[0].astype(F32)
        b = ab_ref[1].astype(F32)
        sig = _sigmoid(a)
        o_ref[0] = (ds * b * sig * (1.0 + a * (1.0 - sig))).astype(o_ref.dtype)
        o_ref[1] = (ds * a * sig).astype(o_ref.dtype)

    ab_spec = pl.BlockSpec((None, 2, tm, F_SHARD), lambda i, c: (c, 0, i, 0))
    dab4 = pl.pallas_call(
        d_gate_up,
        name=f"{tag}_bwd_dgate_up",
        grid=(s_len // tm, N_CHIPS),
        in_specs=[
            pl.BlockSpec((tm, D_MODEL), lambda i, c: (i, 0)),
            pl.BlockSpec((None, None, F_SHARD, D_MODEL), lambda i, c: (c, down_idx, 0, 0)),
            ab_spec,
        ],
        out_specs=ab_spec,
        out_shape=_sds(ab4.shape, BF16),
        compiler_params=_cparams(("parallel", "parallel")),
    )(dy, g_down, ab4)

    tk = tm
    gb_down = _mm(
        f"{tag}_bwd_wdown", "tn", (N_CHIPS, 1, s_len // tk),
        s3, (None, tk, F_SHARD), lambda i, j, k: (i, k, 0),
        dy, (tk, D_MODEL), lambda i, j, k: (k, 0),
        _sds((N_CHIPS, 1, F_SHARD, D_MODEL), BF16), (None, None, F_SHARD, D_MODEL), lambda i, j, k: (i, 0, 0, 0),
        (F_SHARD, D_MODEL), deps=hooks.after_swiglu(dab4),
    )
    gb_gu = _mm(
        f"{tag}_bwd_wgu", "tn", (2 * N_CHIPS, 1, s_len // tk),
        dab4, (None, None, tk, F_SHARD), lambda i, j, k: (i % N_CHIPS, i // N_CHIPS, k, 0),
        h, (tk, D_MODEL), lambda i, j, k: (k, 0),
        _sds((N_CHIPS, 2, F_SHARD, D_MODEL), BF16), (None, None, F_SHARD, D_MODEL),
        lambda i, j, k: (i % N_CHIPS, i // N_CHIPS, 0, 0),
        (F_SHARD, D_MODEL), deps=hooks.after_wdown(gb_down),
    )
    dh = _mm(
        f"{tag}_bwd_dh", "nn", (s_len // tm, 1, 2 * N_CHIPS),
        dab4, (None, None, tm, F_SHARD), lambda i, j, k: (k % N_CHIPS, k // N_CHIPS, i, 0),
        g4, (None, None, F_SHARD, D_MODEL), lambda i, j, k: (k % N_CHIPS, base + k // N_CHIPS, 0, 0),
        _sds((s_len, D_MODEL), F32), (tm, D_MODEL), lambda i, j, k: (i, 0),
        (tm, D_MODEL), deps=hooks.after_wgu(gb_gu),
    )
    dx, dsc, dsh, dgamma = _normmod_bwd_call(
        f"{tag}_bwd_normmod", [dh], x, dxo, gamma, sc, deps=hooks.after_dh(dh)
    )
    return dx, (gb_down, gb_gu), (dsh, dsc, dgate, dgamma)


def _bucket_map():
    qi = np.arange(WINDOW)[:, None]
    kj = np.arange(2 * WINDOW)[None, :]
    dist = qi + WINDOW - kj
    band = (dist >= 0) & (dist < WINDOW)
    max_exact = NUM_BUCKETS // 2
    n = np.maximum(dist, 0)
    large = []
    for dt in (np.float32, np.float64):
        nf = np.maximum(n, 1).astype(dt)
        val = np.log(nf / dt(max_exact)) / dt(math.log(MAX_DISTANCE / max_exact)) * dt(NUM_BUCKETS - max_exact)
        large.append(np.minimum(max_exact + val.astype(np.int32), NUM_BUCKETS - 1))
    assert np.array_equal(large[0], large[1])
    bucket = np.where(n < max_exact, n, large[0])
    return np.where(band, bucket, -1).astype(np.int32).reshape(1, -1)


def _bias_expand(rel_bias_t, bkt):
    n = bkt.shape[1]

    def body(rb_ref, bkt_ref, o_ref):
        onehot = (lax.broadcasted_iota(jnp.int32, (NUM_BUCKETS, n), 0) == bkt_ref[...]).astype(F32)
        o_ref[...] = lax.dot_general(
            rb_ref[...], onehot, _DN["nn"], precision=lax.Precision.HIGHEST, preferred_element_type=F32
        )

    return pl.pallas_call(
        body, name="bias_expand", out_shape=_sds((SWA_HEADS, n), F32), compiler_params=_cparams()
    )(rel_bias_t, bkt)


def _bias_reduce(dbias_flat, bkt):
    n = bkt.shape[1]

    def body(db_ref, bkt_ref, o_ref):
        onehot = (lax.broadcasted_iota(jnp.int32, (NUM_BUCKETS, n), 0) == bkt_ref[...]).astype(F32)
        o_ref[...] = lax.dot_general(
            db_ref[...], onehot, _DN["nt"], precision=lax.Precision.HIGHEST, preferred_element_type=F32
        )

    return pl.pallas_call(
        body, name="bias_reduce", out_shape=_sds((SWA_HEADS, NUM_BUCKETS), F32), compiler_params=_cparams()
    )(dbias_flat, bkt)


_SWA_SCALE = SWA_HEAD_DIM ** -0.5
_NEG = -1e30


def _swa_in_specs():
    w = WINDOW
    n_q = SWA_HEADS * SWA_HEAD_DIM
    n_kv = 2 * SWA_KV_HEADS * SWA_HEAD_DIM
    prev = lambda n: jnp.maximum(n - 1, 0)
    return [
        pl.BlockSpec((w, n_q), lambda n: (n, 0)),
        pl.BlockSpec((w, n_kv), lambda n: (prev(n), n_q // n_kv)),
        pl.BlockSpec((w, n_kv), lambda n: (n, n_q // n_kv)),
        pl.BlockSpec((SWA_HEADS, w, 2 * w), lambda n: (0, 0, 0)),
        pl.BlockSpec((SWA_HEADS, w, 1), lambda n: (0, 0, 0)),
    ]


def _head_cols(v, first, count):
    hd = SWA_HEAD_DIM
    return jnp.stack([v[:, (first + i) * hd:(first + i + 1) * hd] for i in range(count)])


def _swa_heads(q_ref, kvp_ref, kvc_ref):
    g, w, hd = SWA_GROUP, WINDOW, SWA_HEAD_DIM
    q = q_ref[...].astype(F32)
    kv = jnp.concatenate([kvp_ref[...], kvc_ref[...]], axis=0).astype(F32)
    heads = []
    for j in range(SWA_KV_HEADS):
        qj = _head_cols(q, g * j, g).reshape(g * w, hd)
        heads.append((qj, _head_cols(kv, j, 1)[0], _head_cols(kv, SWA_KV_HEADS + j, 1)[0]))
    return heads


def _swa_scores(q, kk, bias, n):
    g, w = SWA_GROUP, WINDOW
    s = (_dot(q, kk, "nt") * _SWA_SCALE).reshape(g, w, 2 * w) + bias
    qi = lax.broadcasted_iota(jnp.int32, (w, 2 * w), 0)
    kj = lax.broadcasted_iota(jnp.int32, (w, 2 * w), 1)
    dist = qi + w - kj
    valid = ((dist >= 0) & (dist < w) & ((n > 0) | (kj >= w)))[None]
    return jnp.where(valid, s, _NEG), valid


def _swa_fwd(swa2, bias, sinkb):
    s_len = swa2.shape[0]
    g, w, hd = SWA_GROUP, WINDOW, SWA_HEAD_DIM

    def body(q_ref, kvp_ref, kvc_ref, bias_ref, sink_ref, o_ref, lse_ref):
        n = pl.program_id(0)
        outs = []
        for j, (q, kk, vv) in enumerate(_swa_heads(q_ref, kvp_ref, kvc_ref)):
            hs = slice(g * j, g * (j + 1))
            s, valid = _swa_scores(q, kk, bias_ref[hs], n)
            sink = sink_ref[hs]
            m = jnp.maximum(jnp.max(s, axis=-1, keepdims=True), sink)
            p = jnp.where(valid, jnp.exp(s - m), 0.0)
            l = jnp.sum(p, axis=-1, keepdims=True) + jnp.exp(sink - m)
            o = _dot(p.reshape(g * w, 2 * w), vv, "nn").reshape(g, w, hd) / l
            outs += [o[i] for i in range(g)]
            lse_ref[hs] = m + jnp.log(l)
        o_ref[...] = jnp.concatenate(outs, axis=-1).astype(o_ref.dtype)

    n_q = SWA_HEADS * hd
    return pl.pallas_call(
        body,
        name="swa_fwd",
        grid=(s_len // w,),
        in_specs=_swa_in_specs(),
        out_specs=[
            pl.BlockSpec((w, n_q), lambda n: (n, 0)),
            pl.BlockSpec((SWA_HEADS, w, 1), lambda n: (0, n, 0)),
        ],
        out_shape=[_sds((s_len, n_q), BF16), _sds((SWA_HEADS, s_len, 1), F32)],
        compiler_params=_cparams(("parallel",)),
    )(swa2, swa2, swa2, bias, sinkb)


def _swa_bwd(swa2, bias, sinkb, cat, dcat, lse):
    s_len = swa2.shape[0]
    g, w, hd = SWA_GROUP, WINDOW, SWA_HEAD_DIM

    def body(q_ref, kvp_ref, kvc_ref, bias_ref, sink_ref, o_ref, do_ref, lse_ref,
             dq_ref, dkva_ref, dkvb_ref, dbias_ref, dsink_ref):
        n = pl.program_id(0)

        @pl.when(n == 0)
        def _():
            dbias_ref[...] = jnp.zeros_like(dbias_ref)
            dsink_ref[...] = jnp.zeros_like(dsink_ref)

        o_all = o_ref[...].astype(F32)
        do_all = do_ref[...].astype(F32)
        dqs, dks, dvs = [], [], []
        for j, (q, kk, vv) in enumerate(_swa_heads(q_ref, kvp_ref, kvc_ref)):
            hs = slice(g * j, g * (j + 1))
            s, valid = _swa_scores(q, kk, bias_ref[hs], n)
            lse_v = lse_ref[hs]
            p = jnp.where(valid, jnp.exp(s - lse_v), 0.0)
            do = _head_cols(do_all, g * j, g)
            delta = jnp.sum(do * _head_cols(o_all, g * j, g), axis=-1, keepdims=True)
            do2 = do.reshape(g * w, hd)
            dp = _dot(do2, vv, "nt").reshape(g, w, 2 * w)
            ds = p * (dp - delta)
            dbias_ref[hs] += ds
            dsink_ref[hs] -= jnp.exp(sink_ref[hs] - lse_v) * delta
            ds2 = ds.reshape(g * w, 2 * w)
            dq = (_dot(ds2, kk, "nn") * _SWA_SCALE).reshape(g, w, hd)
            dqs += [dq[i] for i in range(g)]
            dks.append(_dot(ds2, q, "tn") * _SWA_SCALE)
            dvs.append(_dot(p.reshape(g * w, 2 * w), do2, "tn"))
        dq_ref[...] = jnp.concatenate(dqs, axis=-1).astype(dq_ref.dtype)
        dkv = jnp.concatenate(dks + dvs, axis=-1)
        dkvb_ref[...] = dkv[:w]
        dkva_ref[...] = dkv[w:]

    n_q = SWA_HEADS * hd
    n_kv = 2 * SWA_KV_HEADS * hd
    q_spec = pl.BlockSpec((w, n_q), lambda n: (n, 0))
    kv_spec = pl.BlockSpec((w, n_kv), lambda n: (n, 0))
    return pl.pallas_call(
        body,
        name="swa_bwd",
        grid=(s_len // w,),
        in_specs=_swa_in_specs() + [q_spec, q_spec, pl.BlockSpec((SWA_HEADS, w, 1), lambda n: (0, n, 0))],
        out_specs=[
            q_spec, kv_spec, kv_spec,
            pl.BlockSpec((SWA_HEADS, w, 2 * w), lambda n: (0, 0, 0)),
            pl.BlockSpec((SWA_HEADS, w, 1), lambda n: (0, 0, 0)),
        ],
        out_shape=[
            _sds((s_len, n_q), BF16), _sds((s_len, n_kv), F32), _sds((s_len, n_kv), F32),
            _sds((SWA_HEADS, w, 2 * w), F32), _sds((SWA_HEADS, w, 1), F32),
        ],
        compiler_params=_cparams(("arbitrary",)),
    )(swa2, swa2, swa2, bias, sinkb, cat, dcat, lse)


_MLA_SCALE = MLA_QK ** -0.5
_MLA_TQ = 256


def _mla_mask(i, tq, n_keys):
    row = i * tq + lax.broadcasted_iota(jnp.int32, (tq, n_keys), 0)
    col = lax.broadcasted_iota(jnp.int32, (tq, n_keys), 1)
    return col <= row


def _per_query_block(i, n_blocks, fn):
    for ii in range(n_blocks):
        pl.when(i == ii)(functools.partial(fn, ii))


def _mla_fwd(q4, k4, v4):
    s_len = q4.shape[1]
    tq = min(_MLA_TQ, s_len)

    def body(q_ref, k_ref, v_ref, o_ref, lse_ref):
        def block(ii):
            n_keys = (ii + 1) * tq
            mask = _mla_mask(ii, tq, n_keys)
            s = jnp.where(mask, _dot(q_ref[...], k_ref[:n_keys, :], "nt") * _MLA_SCALE, _NEG)
            m = jnp.max(s, axis=-1, keepdims=True)
            p = jnp.exp(s - m)
            l = jnp.sum(p, axis=-1, keepdims=True)
            o_ref[...] = (_dot(p, v_ref[:n_keys, :], "nn") / l).astype(o_ref.dtype)
            lse_ref[...] = m + jnp.log(l)

        _per_query_block(pl.program_id(1), s_len // tq, block)

    return pl.pallas_call(
        body,
        name="mla_fwd",
        grid=(MLA_HEADS, s_len // tq),
        in_specs=[
            pl.BlockSpec((None, tq, MLA_QK), lambda h, i: (h, i, 0)),
            pl.BlockSpec((None, s_len, MLA_QK), lambda h, i: (h, 0, 0)),
            pl.BlockSpec((None, s_len, MLA_V), lambda h, i: (h, 0, 0)),
        ],
        out_specs=[
            pl.BlockSpec((tq, MLA_V), lambda h, i: (i, h)),
            pl.BlockSpec((None, tq, 1), lambda h, i: (h, i, 0)),
        ],
        out_shape=[_sds((s_len, MLA_HEADS * MLA_V), BF16), _sds((MLA_HEADS, s_len, 1), F32)],
        compiler_params=_cparams(("parallel", "parallel")),
    )(q4, k4, v4)


def _mla_bwd(q4, k4, v4, cat, dcat, lse):
    s_len = q4.shape[1]
    tq = min(_MLA_TQ, s_len)
    first = SWA_HEADS * SWA_HEAD_DIM // MLA_V

    def body(q_ref, k_ref, v_ref, o_ref, do_ref, lse_ref, dq_ref, dk_ref, dv_ref):
        i = pl.program_id(1)

        @pl.when(i == 0)
        def _():
            dk_ref[...] = jnp.zeros_like(dk_ref)
            dv_ref[...] = jnp.zeros_like(dv_ref)

        def block(ii):
            n_keys = (ii + 1) * tq
            mask = _mla_mask(ii, tq, n_keys)
            q, k, v, do = q_ref[...], k_ref[:n_keys, :], v_ref[:n_keys, :], do_ref[...]
            s = jnp.where(mask, _dot(q, k, "nt") * _MLA_SCALE, _NEG)
            p = jnp.where(mask, jnp.exp(s - lse_ref[...]), 0.0)
            delta = jnp.sum(do.astype(F32) * o_ref[...].astype(F32), axis=-1, keepdims=True)
            ds = (p * (_dot(do, v, "nt") - delta) * _MLA_SCALE).astype(BF16)
            dq_ref[...] = _dot(ds, k, "nn")
            dk_ref[:n_keys, :] += _dot(ds, q, "tn")
            dv_ref[:n_keys, :] += _dot(p, do, "tn")

        _per_query_block(i, s_len // tq, block)

    return pl.pallas_call(
        body,
        name="mla_bwd",
        grid=(MLA_HEADS, s_len // tq),
        in_specs=[
            pl.BlockSpec((None, tq, MLA_QK), lambda h, i: (h, i, 0)),
            pl.BlockSpec((None, s_len, MLA_QK), lambda h, i: (h, 0, 0)),
            pl.BlockSpec((None, s_len, MLA_V), lambda h, i: (h, 0, 0)),
            pl.BlockSpec((tq, MLA_V), lambda h, i: (i, first + h)),
            pl.BlockSpec((tq, MLA_V), lambda h, i: (i, first + h)),
            pl.BlockSpec((None, tq, 1), lambda h, i: (h, i, 0)),
        ],
        out_specs=[
            pl.BlockSpec((None, tq, MLA_QK), lambda h, i: (h, i, 0)),
            pl.BlockSpec((None, s_len, MLA_QK), lambda h, i: (h, 0, 0)),
            pl.BlockSpec((None, s_len, MLA_V), lambda h, i: (h, 0, 0)),
        ],
        out_shape=[
            _sds((MLA_HEADS, s_len, MLA_QK), F32),
            _sds((MLA_HEADS, s_len, MLA_QK), F32),
            _sds((MLA_HEADS, s_len, MLA_V), F32),
        ],
        compiler_params=_cparams(("parallel", "arbitrary")),
    )(q4, k4, v4, cat, dcat, lse)


def _mla_split(pm):
    q_lat = pm[:, :MLA_Q_RANK]
    kv_lat = pm[:, MLA_Q_RANK:MLA_Q_RANK + MLA_KV_RANK]
    kr = pm[:, MLA_Q_RANK + MLA_KV_RANK:MLA_Q_RANK + MLA_KV_RANK + MLA_ROPE]
    kr_sw = pm[:, MLA_Q_RANK + MLA_KV_RANK + MLA_ROPE:]
    return q_lat, kv_lat, kr, kr_sw


def _mla_proj(pm, cos2, sin2, q_norm, kv_norm, wq_ext, wkv):
    s_len = pm.shape[0]

    def fn(pm_, cos_, sin_, qg, kvg, wq, wk):
        q_lat, kv_lat, kr, kr_sw = _mla_split(pm_)
        nq = (_rms(q_lat)[0] * qg).astype(BF16)
        nkv = (_rms(kv_lat)[0] * kvg).astype(BF16)
        k_rot = kr * cos_ + kr_sw * sin_
        qs, ks, vs = [], [], []
        for h in range(MLA_HEADS):
            qe = _dot(nq, wq[h], "nt")
            q_rot = qe[:, MLA_NOPE:MLA_QK] * cos_ + qe[:, MLA_QK:] * sin_
            qs.append(jnp.concatenate([qe[:, :MLA_NOPE], q_rot], axis=-1))
            kve = _dot(nkv, wk[h], "nt")
            ks.append(jnp.concatenate([kve[:, :MLA_NOPE], k_rot], axis=-1))
            vs.append(kve[:, MLA_NOPE:])
        return jnp.stack(qs), jnp.stack(ks), jnp.stack(vs)

    return _rowwise(
        "mla_proj", fn, [pm, cos2, sin2], [q_norm, kv_norm, wq_ext, wkv],
        [_sds((MLA_HEADS, s_len, MLA_QK), BF16), _sds((MLA_HEADS, s_len, MLA_QK), BF16),
         _sds((MLA_HEADS, s_len, MLA_V), BF16)], [], 256,
    )


def _mla_proj_bwd(pm, cos2, sin2, dq4, dk4, dv4, q_norm, kv_norm, wq_ext, wkv):
    s_len = pm.shape[0]

    def fn(pm_, cos_, sin_, dq, dk, dv, qg, kvg, wq, wk):
        q_lat, kv_lat, _, _ = _mla_split(pm_)
        xq, rq = _rms(q_lat)
        xkv, rkv = _rms(kv_lat)
        nq = (xq * qg).astype(BF16)
        nkv = (xkv * kvg).astype(BF16)
        dnq = jnp.zeros_like(q_lat)
        dnkv = jnp.zeros_like(kv_lat)
        dkr = jnp.zeros_like(cos_)
        dwq, dwk = [], []
        for h in range(MLA_HEADS):
            dy = dq[h][:, MLA_NOPE:]
            dqe = jnp.concatenate([dq[h][:, :MLA_NOPE], dy * cos_, dy * sin_], axis=-1).astype(BF16)
            dnq = dnq + _dot(dqe, wq[h], "nn")
            dwq.append(_dot(dqe, nq, "tn"))
            dkve = jnp.concatenate([dk[h][:, :MLA_NOPE], dv[h]], axis=-1).astype(BF16)
            dnkv = dnkv + _dot(dkve, wk[h], "nn")
            dwk.append(_dot(dkve, nkv, "tn"))
            dkr = dkr + dk[h][:, MLA_NOPE:]
        dq_lat = _rms_bwd(dnq * qg, xq, rq)
        dkv_lat = _rms_bwd(dnkv * kvg, xkv, rkv)
        dpm = jnp.concatenate([dq_lat, dkv_lat, dkr * cos_, dkr * sin_], axis=-1)
        return dpm, _sum0(dnq * xq), _sum0(dnkv * xkv), jnp.stack(dwq), jnp.stack(dwk)

    return _rowwise(
        "mla_proj_bwd", fn, [pm, cos2, sin2, dq4, dk4, dv4], [q_norm, kv_norm, wq_ext, wkv],
        [_sds((s_len, N_MLA_COLS), BF16)],
        [_sds((1, MLA_Q_RANK), F32), _sds((1, MLA_KV_RANK), F32),
         _sds(wq_ext.shape, F32), _sds(wkv.shape, F32)], 256,
    )


def _rope_tables(s_len):
    inv = ROPE_THETA ** (-jnp.arange(0, MLA_ROPE, 2, dtype=F32) / MLA_ROPE)
    ang = jnp.arange(s_len, dtype=F32)[:, None] * inv[None, :]
    cos, sin = jnp.cos(ang), jnp.sin(ang)
    return jnp.concatenate([cos, cos], axis=-1), jnp.concatenate([-sin, sin], axis=-1)


def _mix_weights(g_mix):
    rest = g_mix[:, 0]
    w_in_t = rest[:, O_IN:O_IN + R_IN].reshape(D_IN, D_MODEL)
    w_o = rest[:, O_O:O_O + R_O].reshape(D_MODEL, D_MODEL)
    w_uq_t = rest[:, O_UQ:O_UQ + R_UQ].reshape(MLA_HEADS, MLA_QK, MLA_Q_RANK)
    w_ukv_t = rest[:, O_UKV:O_UKV + R_UKV].reshape(MLA_HEADS, MLA_NOPE + MLA_V, MLA_KV_RANK)
    half = MLA_ROPE // 2
    w_swa = w_in_t[:N_SWA_COLS]
    w_mla = jnp.concatenate([w_in_t[N_SWA_COLS:], w_in_t[D_IN - half:], w_in_t[D_IN - MLA_ROPE:D_IN - half]], axis=0)
    wq_ext = jnp.concatenate([w_uq_t, w_uq_t[:, MLA_QK - half:], w_uq_t[:, MLA_NOPE:MLA_QK - half]], axis=1)
    return w_swa, w_mla, w_o, wq_ext, w_ukv_t


def _mix_fwd(x, gamma, sh, sc, gate, mix_src, q_norm, kv_norm, bias, sinkb, cos2, sin2):
    s_len = x.shape[0]
    tm = min(ROW_TILE, s_len)

    def normmod(x_, gamma_, sc_, sh_):
        return _rms(x_)[0] * gamma_ * (1.0 + sc_) + sh_

    deps = mix_src.mid(x)
    (h,) = _rowwise("mix_normmod", normmod, [x], [gamma, sc, sh], [_sds((s_len, D_MODEL), BF16)], [], 256, deps=deps)
    weights = _mix_weights(mix_src.get(h))
    w_swa, w_mla, w_o, wq_ext, wkv = weights
    swa2 = _mm(
        "mix_proj_swa", "nt", (s_len // tm, 1, 1),
        h, (tm, D_MODEL), lambda i, j, k: (i, 0),
        w_swa, (N_SWA_COLS, D_MODEL), lambda i, j, k: (0, 0),
        _sds((s_len, N_SWA_COLS), BF16), (tm, N_SWA_COLS), lambda i, j, k: (i, 0),
        (tm, N_SWA_COLS),
    )
    pm = _mm(
        "mix_proj_mla", "nt", (s_len // tm, 1, 1),
        h, (tm, D_MODEL), lambda i, j, k: (i, 0),
        w_mla, (N_MLA_COLS, D_MODEL), lambda i, j, k: (0, 0),
        _sds((s_len, N_MLA_COLS), F32), (tm, N_MLA_COLS), lambda i, j, k: (i, 0),
        (tm, N_MLA_COLS),
    )
    q4, k4, v4 = _mla_proj(pm, cos2, sin2, q_norm, kv_norm, wq_ext, wkv)
    oa, lse_a = _swa_fwd(swa2, bias, sinkb)
    ob, lse_b = _mla_fwd(q4, k4, v4)
    cat = jnp.concatenate([oa, ob], axis=1)
    z = _mm(
        "mix_out", "nn", (s_len // tm, 1, 1),
        cat, (tm, D_MODEL), lambda i, j, k: (i, 0),
        w_o, (D_MODEL, D_MODEL), lambda i, j, k: (0, 0),
        _sds((s_len, D_MODEL), F32), (tm, D_MODEL), lambda i, j, k: (i, 0),
        (tm, D_MODEL),
    )
    (x_out,) = _rowwise(
        "mix_residual", lambda x_, z_, g_: x_ + g_ * z_, [x, z], [gate], [_sds((s_len, D_MODEL), F32)], [], 256
    )
    return x_out, (weights, h, swa2, pm, q4, k4, v4, cat, lse_a, lse_b, z)


def _mix_bwd(dxo, x, gamma, sc, gate, q_norm, kv_norm, bias, sinkb, cos2, sin2, saved, hooks):
    weights, h, swa2, pm, q4, k4, v4, cat, lse_a, lse_b, z = saved
    w_swa, w_mla, w_o, wq_ext, wkv = weights
    s_len = x.shape[0]
    tm = tk = min(ROW_TILE, s_len)
    vec = _sds((1, D_MODEL), F32)
    n_proj = N_SWA_COLS + N_MLA_COLS
    half_d = D_MODEL // 2

    dz, dgate = _rowwise(
        "mix_bwd_gate", lambda dxo_, z_, g_: (g_ * dxo_, _sum0(z_ * dxo_)),
        [dxo, z], [gate], [_sds((s_len, D_MODEL), BF16)], [vec], 256,
    )
    dw_o = _mm(
        "mix_bwd_wo", "tn", (2, 1, s_len // tk),
        cat, (tk, half_d), lambda i, j, k: (k, i),
        dz, (tk, D_MODEL), lambda i, j, k: (k, 0),
        _sds((D_MODEL, D_MODEL), F32), (half_d, D_MODEL), lambda i, j, k: (i, 0),
        (half_d, D_MODEL),
    )
    dcat = _mm(
        "mix_bwd_dcat", "nt", (s_len // tm, 1, 1),
        dz, (tm, D_MODEL), lambda i, j, k: (i, 0),
        w_o, (D_MODEL, D_MODEL), lambda i, j, k: (0, 0),
        _sds((s_len, D_MODEL), BF16), (tm, D_MODEL), lambda i, j, k: (i, 0),
        (tm, D_MODEL), deps=hooks.after_gate(dz),
    )
    dq_a, dkva, dkvb, dbias, dsink = _swa_bwd(swa2, bias, sinkb, cat, dcat, lse_a)
    dq4, dk4, dv4 = _mla_bwd(q4, k4, v4, cat, dcat, lse_b)
    dpm, dq_norm, dkv_norm, dwq_ext, dwkv = _mla_proj_bwd(pm, cos2, sin2, dq4, dk4, dv4, q_norm, kv_norm, wq_ext, wkv)

    dkv = dkva + jnp.concatenate([dkvb[WINDOW:], jnp.zeros_like(dkvb[:WINDOW])], axis=0)
    dproj = jnp.concatenate([dq_a, dkv.astype(BF16), dpm], axis=1)
    w_proj = jnp.concatenate([w_swa, w_mla], axis=0)

    dw_proj = _mm(
        "mix_bwd_win", "tn", (n_proj // 256, 1, s_len // tk),
        dproj, (tk, 256), lambda i, j, k: (k, i),
        h, (tk, D_MODEL), lambda i, j, k: (k, 0),
        _sds((n_proj, D_MODEL), F32), (256, D_MODEL), lambda i, j, k: (i, 0),
        (256, D_MODEL),
    )
    dw_swa, dw_mla = dw_proj[:N_SWA_COLS], dw_proj[N_SWA_COLS:]
    dh = _mm(
        "mix_bwd_dh", "nn", (s_len // tm, 1, 1),
        dproj, (tm, n_proj), lambda i, j, k: (i, 0),
        w_proj, (n_proj, D_MODEL), lambda i, j, k: (0, 0),
        _sds((s_len, D_MODEL), F32), (tm, D_MODEL), lambda i, j, k: (i, 0),
        (tm, D_MODEL),
    )
    dx, dsc, dsh, dgamma = _normmod_bwd_call("mix_bwd_normmod", [dh], x, dxo, gamma, sc)

    half = MLA_ROPE // 2
    n_mla_in = D_IN - N_SWA_COLS
    dw_mla_base = dw_mla[:n_mla_in]
    dw_mla_base = dw_mla_base.at[n_mla_in - half:].add(dw_mla[n_mla_in:n_mla_in + half])
    dw_mla_base = dw_mla_base.at[n_mla_in - MLA_ROPE:n_mla_in - half].add(dw_mla[n_mla_in + half:])
    dw_in_t = jnp.concatenate([dw_swa, dw_mla_base], axis=0)
    dw_uq_t = dwq_ext[:, :MLA_QK]
    dw_uq_t = dw_uq_t.at[:, MLA_QK - half:].add(dwq_ext[:, MLA_QK:MLA_QK + half])
    dw_uq_t = dw_uq_t.at[:, MLA_NOPE:MLA_QK - half].add(dwq_ext[:, MLA_QK + half:])
    rest = jnp.concatenate(
        [
            dw_in_t.reshape(N_CHIPS, R_IN, D_MODEL),
            dw_o.reshape(N_CHIPS, R_O, D_MODEL),
            dw_uq_t.reshape(N_CHIPS, R_UQ, D_MODEL),
            dwkv.reshape(N_CHIPS, R_UKV, D_MODEL),
            jnp.zeros((N_CHIPS, F_SHARD - O_PAD, D_MODEL), F32),
        ],
        axis=1,
    ).astype(BF16)
    return dx, rest, (dsh, dsc, dgate, dgamma), dq_norm, dkv_norm, dbias, dsink


def _device_step(x, target, mod, gu1_src, down1_src, mix_src, ffn2_src, norms, q_norm, kv_norm, sinks, rel_bias,
                 hooks2=None, hooks_mix=None, mix_done=None, hooks1=None):
    s_len = x.shape[0]
    sh1, sc1, g1, sh2, sc2, g2, sh3, sc3, g3 = [mod[:, i * D_MODEL:(i + 1) * D_MODEL] for i in range(N_MOD)]
    n1, n2, n3, nf = norms
    bkt = jnp.asarray(_bucket_map())
    bias = _bias_expand(rel_bias.T, bkt).reshape(SWA_HEADS, WINDOW, 2 * WINDOW)
    sinkb = jnp.broadcast_to(sinks.reshape(SWA_HEADS, 1, 1), (SWA_HEADS, WINDOW, 1))
    cos2, sin2 = _rope_tables(s_len)

    x1, saved1 = _ffn_fwd("ffn1", x, n1, sh1, sc1, g1, gu1_src, 0, down1_src, 0, sh1)
    x2, saved2 = _mix_fwd(x1, n2, sh2, sc2, g2, mix_src, q_norm, kv_norm, bias, sinkb, cos2, sin2)
    x3, saved3 = _ffn_fwd("ffn2", x2, n3, sh3, sc3, g3, ffn2_src, 0, None, 2, x2)

    def head(x_, t_, g_):
        xr, r = _rms(x_)
        err = xr * g_ - t_
        dy = err * (1.0 / D_MODEL)
        return _rms_bwd(dy * g_, xr, r), _sum0(err * err), _sum0(dy * xr)

    vec = _sds((1, D_MODEL), F32)
    dx3, sq, dnf = _rowwise("loss_head", head, [x3, target], [nf], [_sds((s_len, D_MODEL), F32)], [vec, vec], 256)
    loss_part = (0.5 / D_MODEL) * jnp.sum(sq)

    dx2, gb2, (dsh3, dsc3, dg3, dn3) = _ffn_bwd("ffn2", dx3, x2, n3, sc3, g3, saved3, hooks2 or _BwdHooks())
    dx1, rest, (dsh2, dsc2, dg2, dn2), dq_norm, dkv_norm, dbias, dsink = _mix_bwd(
        dx2, x1, n2, sc2, g2, q_norm, kv_norm, bias, sinkb, cos2, sin2, saved2, hooks_mix or _BwdHooks()
    )
    rest = rest[:, None]
    deps1 = mix_done(dx1, rest) if mix_done is not None else []
    dx0, gb1, (dsh1, dsc1, dg1, dn1) = _ffn_bwd(
        "ffn1", dx1, x, n1, sc1, g1, saved1, hooks1 or _BwdHooks(), deps=deps1
    )

    dmod = jnp.concatenate([dsh1, dsc1, dg1, dsh2, dsc2, dg2, dsh3, dsc3, dg3], axis=-1)
    drel = _bias_reduce(dbias.reshape(SWA_HEADS, -1), bkt).T
    dsinks = jnp.sum(dsink, axis=(1, 2)).reshape(1, SWA_HEADS)
    small = (dn1, dn2, dn3, dnf, dq_norm, dkv_norm, dsinks, drel)
    return loss_part, dx0, (gb1, gb2, rest), dmod, small


_MESH = pl.DeviceIdType.MESH


def _place():
    return lax.axis_index("x"), lax.axis_index("y"), lax.axis_index("c")


def _other_chips(x, y):
    return [(1 - x, y), (x, 1 - y), (1 - x, 1 - y)]


def _allgather_small(name, blk):
    m_per, n = blk.shape

    def body(x_ref, out_ref, send_sems, recv_sems, local_sem):
        x, y, c = _place()
        me, sibling = (x, y, c), (x, y, 1 - c)
        chips = _other_chips(x, y)

        def rows(px, py, pc):
            return out_ref.at[pl.ds((4 * px + 2 * py + pc) * m_per, m_per), :]

        def copy(k, block, to, src=None):
            return pltpu.make_async_remote_copy(
                src_ref=rows(*block) if src is None else src, dst_ref=rows(*block),
                send_sem=send_sems.at[k], recv_sem=recv_sems.at[k], device_id=to, device_id_type=_MESH,
            )

        mine = pltpu.make_async_copy(x_ref, rows(*me), local_sem)
        mine.start()
        first = [copy(0, me, sibling, src=x_ref)]
        first += [copy(1 + j, me, (*chip, c), src=x_ref) for j, chip in enumerate(chips)]
        for cp in first:
            cp.start()
        passed = [copy(4 + j, (*chip, c), sibling) for j, chip in enumerate(chips)]
        for j, chip in enumerate(chips):
            copy(1 + j, (*chip, c), me).wait_recv()
            passed[j].start()
        copy(0, sibling, me).wait_recv()
        for j, chip in enumerate(chips):
            copy(4 + j, (*chip, 1 - c), me).wait_recv()
        for cp in first + passed:
            cp.wait_send()
        mine.wait()

    return pl.pallas_call(
        body,
        name=name,
        out_shape=_sds((N_DEV * m_per, n), blk.dtype),
        in_specs=[pl.BlockSpec(memory_space=pltpu.VMEM)],
        out_specs=pl.BlockSpec(memory_space=pltpu.VMEM),
        scratch_shapes=[pltpu.SemaphoreType.DMA((7,)), pltpu.SemaphoreType.DMA((7,)), pltpu.SemaphoreType.DMA],
    )(blk)


def _allgather_weights(name, own):
    n = own.shape[0]

    def body(own_ref, g_ref, token, send_sems, recv_sems, local_sem):
        x, y, c = _place()
        sibling = (x, y, 1 - c)
        chips = _other_chips(x, y)
        mine = pltpu.make_async_copy(own_ref, g_ref.at[2 * x + y], local_sem)
        mine.start()
        for j, chip in enumerate(chips):
            for cp in _gather_sends(n, own_ref, g_ref, send_sems.at[j], recv_sems.at[j], x, y, c, chip):
                cp.start()
        for j, chip in enumerate(chips):
            _gather_all(own_ref, g_ref, send_sems.at[j], recv_sems.at[j], chip, c, c).wait_recv()
            for cp in _gather_forwards(n, g_ref, send_sems.at[3 + j], recv_sems.at[3 + j], chip, c, sibling):
                cp.start()
        for j, chip in enumerate(chips):
            _gather_all(own_ref, g_ref, send_sems.at[3 + j], recv_sems.at[3 + j], chip, 1 - c, c).wait_recv()
        for j, chip in enumerate(chips):
            _gather_all(own_ref, g_ref, send_sems.at[j], recv_sems.at[j], chip, c, c).wait_send()
            _gather_all(own_ref, g_ref, send_sems.at[3 + j], recv_sems.at[3 + j], chip, c, c).wait_send()
        mine.wait()
        token[...] = jnp.zeros_like(token)

    return pl.pallas_call(
        body,
        name=name,
        out_shape=[_sds((N_CHIPS,) + own.shape, own.dtype), _sds((8, 128), F32)],
        in_specs=[pl.BlockSpec(memory_space=pl.ANY)],
        out_specs=[pl.BlockSpec(memory_space=pl.ANY), pl.BlockSpec(memory_space=pltpu.VMEM)],
        scratch_shapes=[pltpu.SemaphoreType.DMA((6,)), pltpu.SemaphoreType.DMA((6,)), pltpu.SemaphoreType.DMA],
    )(own)


def _gather_sends(n, own_ref, g_ref, send_sem, recv_sem, x, y, c, chip):
    return [
        pltpu.make_async_remote_copy(
            src_ref=own_ref.at[p, pl.ds(c * HALF, HALF), :], dst_ref=g_ref.at[2 * x + y, p, pl.ds(c * HALF, HALF), :],
            send_sem=send_sem, recv_sem=recv_sem, device_id=(*chip, c), device_id_type=_MESH,
        )
        for p in range(n)
    ]


def _gather_forwards(n, g_ref, send_sem, recv_sem, chip, c, sibling):
    return [
        pltpu.make_async_remote_copy(
            src_ref=g_ref.at[2 * chip[0] + chip[1], p, pl.ds(c * HALF, HALF), :],
            dst_ref=g_ref.at[2 * chip[0] + chip[1], p, pl.ds(c * HALF, HALF), :],
            send_sem=send_sem, recv_sem=recv_sem, device_id=sibling, device_id_type=_MESH,
        )
        for p in range(n)
    ]


def _gather_all(own_ref, g_ref, send_sem, recv_sem, chip, half, c):
    return pltpu.make_async_remote_copy(
        src_ref=own_ref.at[:, pl.ds(c * HALF, HALF), :],
        dst_ref=g_ref.at[2 * chip[0] + chip[1], :, pl.ds(half * HALF, HALF), :],
        send_sem=send_sem, recv_sem=recv_sem, device_id=(*chip, c), device_id_type=_MESH,
    )


_HBM_SPEC = pl.BlockSpec(memory_space=pltpu.HBM)
_SEM_SPEC = pl.BlockSpec(memory_space=pltpu.SEMAPHORE)
_ANY_SPEC = pl.BlockSpec(memory_space=pl.ANY)
_VMEM_SPEC = pl.BlockSpec(memory_space=pltpu.VMEM)
_SPLIT_PARAMS = pltpu.CompilerParams(has_side_effects=pltpu.SideEffectType.DATAFLOW_SIDE_EFFECTING)
_TOKEN = jax.ShapeDtypeStruct((8, 128), F32)


def _in_hbm(a):
    return pltpu.with_memory_space_constraint(a, pltpu.HBM)


class _GatherBehind:
    def __init__(self, tag, own, then=None):
        self.tag, self.n, self.own, self.then = tag, own.shape[0], own, then

    def start(self, after):
        tag, own, n = self.tag, self.own, self.n
        x, y, _ = _place()
        g_init = lax.dynamic_update_slice(
            lax.empty((N_CHIPS,) + own.shape, own.dtype), own[None], (2 * x + y, 0, 0, 0)
        )

        def body(own_ref, g_ref, *rest):
            send_sems, recv_sems, _, _, token = rest[len(after):]
            x, y, c = _place()
            for j, chip in enumerate(_other_chips(x, y)):
                for cp in _gather_sends(n, own_ref, g_ref, send_sems.at[j], recv_sems.at[j], x, y, c, chip):
                    cp.start()
            token[...] = jnp.zeros_like(token)

        self.send1, self.recv1, self.own, self.g, self.token = pl.pallas_call(
            body,
            name=f"{tag}_start",
            out_shape=(
                pltpu.SemaphoreType.DMA((3,)), pltpu.SemaphoreType.DMA((3,)),
                pltpu.HBM(own.shape, own.dtype), pltpu.HBM(g_init.shape, g_init.dtype), _TOKEN,
            ),
            in_specs=(_HBM_SPEC, _HBM_SPEC) + (_ANY_SPEC,) * len(after),
            out_specs=(_SEM_SPEC, _SEM_SPEC, _HBM_SPEC, _HBM_SPEC, _VMEM_SPEC),
            input_output_aliases={0: 2, 1: 3},
            compiler_params=_SPLIT_PARAMS,
        )(_in_hbm(own), _in_hbm(g_init), *after)

    def mid(self, after):
        n = self.n

        def body(own_ref, g_ref, send1, recv1, after_ref, send2, recv2, g_out, token):
            x, y, c = _place()
            sibling = (x, y, 1 - c)
            chips = _other_chips(x, y)
            for j, chip in enumerate(chips):
                _gather_all(own_ref, g_ref, send1.at[j], recv1.at[j], chip, c, c).wait_recv()
                for cp in _gather_forwards(n, g_ref, send2.at[j], recv2.at[j], chip, c, sibling):
                    cp.start()
            for j, chip in enumerate(chips):
                _gather_all(own_ref, g_ref, send1.at[j], recv1.at[j], chip, c, c).wait_send()
            token[...] = jnp.zeros_like(token)

        self.send2, self.recv2, self.g, token = pl.pallas_call(
            body,
            name=f"{self.tag}_mid",
            out_shape=(
                pltpu.SemaphoreType.DMA((3,)), pltpu.SemaphoreType.DMA((3,)),
                pltpu.HBM(self.g.shape, self.g.dtype), _TOKEN,
            ),
            in_specs=(_HBM_SPEC, _HBM_SPEC, _SEM_SPEC, _SEM_SPEC, _ANY_SPEC),
            out_specs=(_SEM_SPEC, _SEM_SPEC, _HBM_SPEC, _VMEM_SPEC),
            input_output_aliases={1: 2},
            compiler_params=_SPLIT_PARAMS,
        )(self.own, self.g, self.send1, self.recv1, after)
        if self.then is None:
            return [token]
        self.then.start([token])
        return [token, self.then.token]

    def get(self, after):
        def body(g_ref, send2, recv2, after_ref, g_out):
            x, y, c = _place()
            for j, chip in enumerate(_other_chips(x, y)):
                slot_in = g_ref.at[2 * chip[0] + chip[1], :, pl.ds((1 - c) * HALF, HALF), :]
                slot_out = g_ref.at[2 * chip[0] + chip[1], :, pl.ds(c * HALF, HALF), :]
                cp = pltpu.make_async_remote_copy(
                    src_ref=slot_out, dst_ref=slot_in, send_sem=send2.at[j], recv_sem=recv2.at[j],
                    device_id=(x, y, 1 - c), device_id_type=_MESH,
                )
                cp.wait_send()
                cp.wait_recv()

        return pl.pallas_call(
            body,
            name=f"{self.tag}_wait",
            out_shape=pltpu.HBM(self.g.shape, self.g.dtype),
            in_specs=(_HBM_SPEC, _SEM_SPEC, _SEM_SPEC, _ANY_SPEC),
            out_specs=_HBM_SPEC,
            input_output_aliases={0: 0},
            compiler_params=_SPLIT_PARAMS,
        )(self.g, self.send2, self.recv2, after)


def _pair_exchange(name, gb):
    def body(gb_ref, land_ref, send_sem, recv_sem):
        x, y, c = _place()
        theirs = gb_ref.at[:, :, pl.ds((1 - c) * HALF, HALF), :]
        cp = pltpu.make_async_remote_copy(
            src_ref=theirs, dst_ref=land_ref, send_sem=send_sem, recv_sem=recv_sem,
            device_id=(x, y, 1 - c), device_id_type=_MESH,
        )
        cp.start()
        cp.wait()

    return pl.pallas_call(
        body,
        name=name,
        out_shape=_sds((N_CHIPS, gb.shape[1], HALF, D_MODEL), gb.dtype),
        in_specs=[pl.BlockSpec(memory_space=pl.ANY)],
        out_specs=pl.BlockSpec(memory_space=pl.ANY),
        scratch_shapes=[pltpu.SemaphoreType.DMA, pltpu.SemaphoreType.DMA],
    )(gb)


def _pair_sum(name, gb, land):
    def body(gb_ref, land_ref, o_ref):
        c = lax.axis_index("c")
        mine = gb_ref[pl.ds(pl.multiple_of(c * HALF, 16), HALF), :]
        o_ref[...] = (mine.astype(F32) + land_ref[...].astype(F32)).astype(o_ref.dtype)

    return pl.pallas_call(
        body,
        name=name,
        grid=(N_CHIPS, gb.shape[1]),
        in_specs=[
            pl.BlockSpec((None, None, F_SHARD, D_MODEL), lambda j, p: (j, p, 0, 0)),
            pl.BlockSpec((None, None, HALF, D_MODEL), lambda j, p: (j, p, 0, 0)),
        ],
        out_specs=pl.BlockSpec((None, None, HALF, D_MODEL), lambda j, p: (j, p, 0, 0)),
        out_shape=_sds(land.shape, BF16),
        compiler_params=_cparams(("parallel", "parallel")),
    )(gb, land)


def _chip_exchange(name, part):
    def body(p_ref, land_ref, send_sems, recv_sems, local_sem):
        x, y, c = _place()
        me = 2 * x + y
        chips = _other_chips(x, y)
        mine = pltpu.make_async_copy(p_ref.at[me], land_ref.at[me], local_sem)
        mine.start()

        def copy(k, chip):
            return pltpu.make_async_remote_copy(
                src_ref=p_ref.at[2 * chip[0] + chip[1]], dst_ref=land_ref.at[me],
                send_sem=send_sems.at[k], recv_sem=recv_sems.at[k], device_id=(*chip, c), device_id_type=_MESH,
            )

        sends = [copy(k, chip) for k, chip in enumerate(chips)]
        for cp in sends:
            cp.start()
        for k, chip in enumerate(chips):
            pltpu.make_async_remote_copy(
                src_ref=p_ref.at[me], dst_ref=land_ref.at[2 * chip[0] + chip[1]],
                send_sem=send_sems.at[k], recv_sem=recv_sems.at[k], device_id=(*chip, c), device_id_type=_MESH,
            ).wait_recv()
        for cp in sends:
            cp.wait_send()
        mine.wait()

    return pl.pallas_call(
        body,
        name=name,
        out_shape=_sds(part.shape, part.dtype),
        in_specs=[pl.BlockSpec(memory_space=pl.ANY)],
        out_specs=pl.BlockSpec(memory_space=pl.ANY),
        scratch_shapes=[pltpu.SemaphoreType.DMA((3,)), pltpu.SemaphoreType.DMA((3,)), pltpu.SemaphoreType.DMA],
    )(part)


def _chip_sum_share(name, land):
    n = land.shape[1]

    def body(l_ref, r_ref, buf, send_sems, recv_sems, local_sems):
        p = pl.program_id(0)
        x, y, c = _place()
        acc = l_ref[0].astype(F32)
        for j in range(1, N_CHIPS):
            acc = acc + l_ref[j].astype(F32)
        buf[p] = acc

        def copies(q):
            mine = r_ref.at[q, pl.ds(c * HALF, HALF), :]
            theirs = r_ref.at[q, pl.ds((1 - c) * HALF, HALF), :]
            local = pltpu.make_async_copy(buf.at[q], mine, local_sems.at[q])
            out = pltpu.make_async_remote_copy(
                src_ref=buf.at[q], dst_ref=mine, send_sem=send_sems.at[q], recv_sem=recv_sems.at[q],
                device_id=(x, y, 1 - c), device_id_type=_MESH,
            )
            arrive = pltpu.make_async_remote_copy(
                src_ref=buf.at[q], dst_ref=theirs, send_sem=send_sems.at[q], recv_sem=recv_sems.at[q],
                device_id=(x, y, 1 - c), device_id_type=_MESH,
            )
            return local, out, arrive

        local, out, _ = copies(p)
        local.start()
        out.start()

        @pl.when(p == n - 1)
        def _():
            for q in range(n):
                local, out, arrive = copies(q)
                arrive.wait_recv()
                out.wait_send()
                local.wait()

    return pl.pallas_call(
        body,
        name=name,
        grid=(n,),
        in_specs=[pl.BlockSpec((N_CHIPS, None, HALF, D_MODEL), lambda p: (0, p, 0, 0))],
        out_specs=pl.BlockSpec(memory_space=pl.ANY),
        out_shape=_sds((n, F_SHARD, D_MODEL), F32),
        scratch_shapes=[
            pltpu.VMEM((n, HALF, D_MODEL), F32),
            pltpu.SemaphoreType.DMA((n,)), pltpu.SemaphoreType.DMA((n,)), pltpu.SemaphoreType.DMA((n,)),
        ],
        compiler_params=_cparams(("arbitrary",)),
    )(land)


class _ReduceBehind:
    def __init__(self, tag, gb, after=()):
        self.tag = tag
        n = gb.shape[1]
        land = lax.empty((N_CHIPS, n, HALF, D_MODEL), gb.dtype)

        def body(gb_ref, land_ref, *rest):
            send_sem, recv_sem, _, _, token = rest[len(after):]
            self._pair_copy(gb_ref, land_ref, send_sem, recv_sem).start()
            token[...] = jnp.zeros_like(token)

        self.send, self.recv, self.gb, self.land, self.token = pl.pallas_call(
            body,
            name=f"grads_pair_start_{tag}",
            out_shape=(
                pltpu.SemaphoreType.DMA((1,)), pltpu.SemaphoreType.DMA((1,)),
                pltpu.HBM(gb.shape, gb.dtype), pltpu.HBM(land.shape, land.dtype), _TOKEN,
            ),
            in_specs=(_HBM_SPEC, _HBM_SPEC) + (_ANY_SPEC,) * len(after),
            out_specs=(_SEM_SPEC, _SEM_SPEC, _HBM_SPEC, _HBM_SPEC, _VMEM_SPEC),
            input_output_aliases={0: 2, 1: 3},
            compiler_params=_SPLIT_PARAMS,
        )(_in_hbm(gb), _in_hbm(land), *after)

    @staticmethod
    def _pair_copy(gb_ref, land_ref, send_sem, recv_sem):
        x, y, c = _place()
        return pltpu.make_async_remote_copy(
            src_ref=gb_ref.at[:, :, pl.ds((1 - c) * HALF, HALF), :], dst_ref=land_ref,
            send_sem=send_sem.at[0], recv_sem=recv_sem.at[0], device_id=(x, y, 1 - c), device_id_type=_MESH,
        )

    @staticmethod
    def _chip_copies(p_ref, land_ref, send_sems, recv_sems):
        x, y, c = _place()
        me = 2 * x + y
        sends, arrivals = [], []
        for k, chip in enumerate(_other_chips(x, y)):
            them = 2 * chip[0] + chip[1]
            sends.append(pltpu.make_async_remote_copy(
                src_ref=p_ref.at[them], dst_ref=land_ref.at[me], send_sem=send_sems.at[k], recv_sem=recv_sems.at[k],
                device_id=(*chip, c), device_id_type=_MESH,
            ))
            arrivals.append(pltpu.make_async_remote_copy(
                src_ref=p_ref.at[me], dst_ref=land_ref.at[them], send_sem=send_sems.at[k], recv_sem=recv_sems.at[k],
                device_id=(*chip, c), device_id_type=_MESH,
            ))
        return sends, arrivals

    def mid(self, after):
        tag = self.tag

        def wait_body(gb_ref, land_ref, send_sem, recv_sem, after_ref, gb_out, land_out):
            cp = self._pair_copy(gb_ref, land_ref, send_sem, recv_sem)
            cp.wait_send()
            cp.wait_recv()

        gb, land = pl.pallas_call(
            wait_body,
            name=f"grads_pair_wait_{tag}",
            out_shape=(pltpu.HBM(self.gb.shape, self.gb.dtype), pltpu.HBM(self.land.shape, self.land.dtype)),
            in_specs=(_HBM_SPEC, _HBM_SPEC, _SEM_SPEC, _SEM_SPEC, _ANY_SPEC),
            out_specs=(_HBM_SPEC, _HBM_SPEC),
            input_output_aliases={0: 0, 1: 1},
            compiler_params=_SPLIT_PARAMS,
        )(self.gb, self.land, self.send, self.recv, after)
        part = _pair_sum(f"grads_pair_sum_{tag}", gb, land)

        x, y, _ = _place()
        start = (2 * x + y, 0, 0, 0)
        own = lax.dynamic_slice(part, start, (1,) + part.shape[1:])
        land2 = lax.dynamic_update_slice(lax.empty(part.shape, part.dtype), own, start)

        def start_body(p_ref, land_ref, send_sems, recv_sems, p_out, land_out, token):
            for cp in self._chip_copies(p_ref, land_ref, send_sems, recv_sems)[0]:
                cp.start()
            token[...] = jnp.zeros_like(token)

        self.send2, self.recv2, self.part, self.land2, token = pl.pallas_call(
            start_body,
            name=f"grads_chip_start_{tag}",
            out_shape=(
                pltpu.SemaphoreType.DMA((3,)), pltpu.SemaphoreType.DMA((3,)),
                pltpu.HBM(part.shape, part.dtype), pltpu.HBM(land2.shape, land2.dtype), _TOKEN,
            ),
            in_specs=(_HBM_SPEC, _HBM_SPEC),
            out_specs=(_SEM_SPEC, _SEM_SPEC, _HBM_SPEC, _HBM_SPEC, _VMEM_SPEC),
            input_output_aliases={0: 2, 1: 3},
            compiler_params=_SPLIT_PARAMS,
        )(_in_hbm(part), _in_hbm(land2))
        return [token]

    def get(self, after):
        n_after = len(after)

        def wait_body(p_ref, land_ref, send_sems, recv_sems, *rest):
            sends, arrivals = self._chip_copies(p_ref, land_ref, send_sems, recv_sems)
            for cp in arrivals:
                cp.wait_recv()
            for cp in sends:
                cp.wait_send()

        _, land2 = pl.pallas_call(
            wait_body,
            name=f"grads_chip_wait_{self.tag}",
            out_shape=(pltpu.HBM(self.part.shape, self.part.dtype), pltpu.HBM(self.land2.shape, self.land2.dtype)),
            in_specs=(_HBM_SPEC, _HBM_SPEC, _SEM_SPEC, _SEM_SPEC) + (_ANY_SPEC,) * n_after,
            out_specs=(_HBM_SPEC, _HBM_SPEC),
            input_output_aliases={0: 0, 1: 1},
            compiler_params=_SPLIT_PARAMS,
        )(self.part, self.land2, self.send2, self.recv2, *after)
        return _chip_sum_share(f"grads_chip_sum_share_{self.tag}", land2)


def _allreduce_small(blk):
    gathered = _allgather_small("allgather_small_grads", blk).reshape(N_DEV, PK_ROWS, 128)

    def body(g_ref, o_ref):
        acc = g_ref[0]
        for k in range(1, N_DEV):
            acc = acc + g_ref[k]
        o_ref[...] = acc

    total = pl.pallas_call(body, name="small_grads_sum", out_shape=_sds((PK_ROWS, 128), F32))(gathered)
    return gathered, total


def _adamw_math(w_, g_, m_, v_):
    m_new = ADAM_B1 * m_ + (1.0 - ADAM_B1) * g_
    v_new = ADAM_B2 * v_ + (1.0 - ADAM_B2) * (g_ * g_)
    m_hat = m_new / (1.0 - ADAM_B1 ** ADAM_STEP)
    v_hat = v_new / (1.0 - ADAM_B2 ** ADAM_STEP)
    delta = -ADAM_LR * (m_hat / (jnp.sqrt(v_hat) + ADAM_EPS) + ADAM_WD * w_)
    return delta, m_new, v_new


def _adamw(name, w, g, m, v):
    rows, cols = w.shape
    tm = rows
    while tm * cols * 4 > (1 << 20) and tm % 16 == 0:
        tm //= 2
    out = _sds(w.shape, F32)
    return _rowwise(name, _adamw_math, [w, g, m, v], [], [out, out, out], [], tm)


def _adamw_small(ws, gs, ms, vs):
    n = len(ws)
    shapes = [w.shape for w in ws]
    as2d = lambda a: a.reshape(1, -1) if a.ndim == 1 else a

    def body(*refs):
        ins, outs = refs[:4 * n], refs[4 * n:]
        for k in range(n):
            res = _adamw_math(*[ins[j * n + k][...] for j in range(4)])
            for j in range(3):
                outs[j * n + k][...] = res[j]

    args = [as2d(a) for group in (ws, gs, ms, vs) for a in group]
    out = pl.pallas_call(
        body, name="adamw_small", out_shape=[_sds(as2d(w).shape, F32) for w in ws] * 3, compiler_params=_cparams()
    )(*args)
    back = [o.reshape(shapes[i % n]) for i, o in enumerate(out)]
    return back[:n], back[n:2 * n], back[2 * n:]


def _pack_small(b_mod_like, n1, n2, n3, nf, qn, kvn, sinks, rel):
    parts = [
        b_mod_like.reshape(PK_MOD, 128),
        n1.reshape(8, 128), n2.reshape(8, 128), n3.reshape(8, 128), nf.reshape(8, 128),
        qn.reshape(2, 128), kvn.reshape(1, 128),
        jnp.pad(sinks.reshape(1, SWA_HEADS), ((0, 0), (0, 128 - SWA_HEADS))),
        rel.reshape(2, 128),
        jnp.zeros((2, 128), F32),
    ]
    return jnp.concatenate(parts, axis=0)


def _unpack_small(p):
    o = PK_MOD
    return (
        p[:o].reshape(1, N_MOD * D_MODEL),
        p[o:o + 8].reshape(1, D_MODEL), p[o + 8:o + 16].reshape(1, D_MODEL),
        p[o + 16:o + 24].reshape(1, D_MODEL), p[o + 24:o + 32].reshape(D_MODEL),
        p[o + 32:o + 34].reshape(1, MLA_Q_RANK), p[o + 34:o + 35].reshape(1, MLA_KV_RANK),
        p[o + 35:o + 36, :SWA_HEADS].reshape(1, SWA_HEADS),
        p[o + 36:o + 38].reshape(NUM_BUCKETS, SWA_HEADS),
    )


def kernel(x, c, w_mod, b_mod, norm_ffn1, ffn1_gate, ffn1_up, ffn1_down, norm_mix, w_in, q_norm, kv_norm, w_uq, w_ukv, sinks, w_o, norm_ffn2, ffn2_gate, ffn2_up, ffn2_down, rel_bias, norm_final, loss_target, m_w_mod, m_b_mod, m_norm_ffn1, m_ffn1_gate, m_ffn1_up, m_ffn1_down, m_norm_mix, m_w_in, m_q_norm, m_kv_norm, m_w_uq, m_w_ukv, m_sinks, m_w_o, m_norm_ffn2, m_ffn2_gate, m_ffn2_up, m_ffn2_down, m_rel_bias, m_norm_final, v_w_mod, v_b_mod, v_norm_ffn1, v_ffn1_gate, v_ffn1_up, v_ffn1_down, v_norm_mix, v_w_in, v_q_norm, v_kv_norm, v_w_uq, v_w_ukv, v_sinks, v_w_o, v_norm_ffn2, v_ffn2_gate, v_ffn2_up, v_ffn2_down, v_rel_bias, v_norm_final):
    ax, ay, ac = _place()
    chip = 2 * ax + ay
    dev = 2 * chip + ac
    n_mod_shard = N_MOD * D_MODEL // N_CHIPS

    def t16(w):
        return w[0].T.astype(BF16)

    rest = jnp.concatenate(
        [
            t16(w_in),
            w_o[0].astype(BF16),
            t16(w_uq).reshape(R_UQ, D_MODEL),
            t16(w_ukv).reshape(R_UKV, D_MODEL),
            jnp.zeros((F_SHARD - O_PAD, D_MODEL), BF16),
        ],
        axis=0,
    )
    own_gu1 = jnp.stack([t16(ffn1_gate), t16(ffn1_up)])
    own_down1 = ffn1_down[0].astype(BF16)[None]
    own_mix = rest[None]
    own_ffn2 = jnp.stack([t16(ffn2_gate), t16(ffn2_up), ffn2_down[0].astype(BF16)])

    (c_act,) = _rowwise(
        "silu_c", lambda v: v * _sigmoid(v), [c.reshape(8, 128)], [], [_sds((8, 128), F32)], [], 8
    )
    c_all = _allgather_small("allgather_c", c_act).reshape(N_DEV, D_MODEL)
    mod_cols = _mm(
        "mod_fwd", "nn", (1, n_mod_shard // 768, 1),
        c_all, (N_DEV, D_MODEL), lambda i, j, k: (0, 0),
        w_mod[0], (D_MODEL, 768), lambda i, j, k: (0, j),
        _sds((N_DEV, n_mod_shard), F32), (N_DEV, 768), lambda i, j, k: (0, j),
        (N_DEV, 768),
    )
    mod_all = _allgather_small("allgather_mod", mod_cols).reshape(N_CHIPS, 2, N_DEV, n_mod_shard)[:, 0]
    mod_all = mod_all.transpose(1, 0, 2).reshape(N_DEV, N_MOD * D_MODEL)
    mod = lax.dynamic_slice_in_dim(mod_all, dev, 1, axis=0) + b_mod

    norms = (norm_ffn1, norm_mix, norm_ffn2, norm_final.reshape(1, D_MODEL))

    ffn2_src = _GatherBehind("gather_ffn2", own_ffn2)
    mix_src = _GatherBehind("gather_mix", own_mix, then=ffn2_src)
    down1_src = _GatherBehind("gather_down1", own_down1, then=mix_src)
    gu1_src = _GatherBehind("gather_gu1", own_gu1, then=down1_src)
    gu1_src.start([mod_all])

    reducing, red = {}, {}

    def begin(tag, gb, after):
        reducing[tag] = _ReduceBehind(tag, gb, after=after)
        return [reducing[tag].token]

    def ffn2_gu_done(gb):
        return begin("ffn2_gu", gb, reducing["ffn2_down"].mid(gb))

    def mix_done(dx1, gb_rest):
        red["ffn2_down"] = reducing["ffn2_down"].get([dx1])
        red["ffn2_gu"] = reducing["ffn2_gu"].get([red["ffn2_down"]])
        return begin("rest", gb_rest, [red["ffn2_gu"]])

    def ffn1_gu_done(gb):
        red["rest"] = reducing["rest"].get([gb])
        begin("ffn1_gu", gb, reducing["ffn1_down"].mid(red["rest"]))
        return reducing["ffn1_gu"].mid(reducing["ffn1_gu"].token)

    loss_part, grad_x, _, dmod, small = _device_step(
        x[0], loss_target[0], mod, gu1_src, down1_src, mix_src, ffn2_src, norms, q_norm, kv_norm, sinks, rel_bias,
        hooks2=_BwdHooks(after_wdown=lambda gb: begin("ffn2_down", gb, ()), after_wgu=ffn2_gu_done),
        hooks_mix=_BwdHooks(after_gate=lambda dz: reducing["ffn2_gu"].mid(dz)),
        mix_done=mix_done,
        hooks1=_BwdHooks(
            after_swiglu=lambda dab3: reducing["rest"].mid(dab3),
            after_wdown=lambda gb: begin("ffn1_down", gb, ()),
            after_wgu=ffn1_gu_done,
        ),
    )
    loss = lax.psum(loss_part, ("x", "y", "c"))

    gathered, total = _allreduce_small(_pack_small(dmod, *small))
    (g_b_mod, g_n1, g_n2, g_n3, g_nf, g_qn, g_kvn, g_sinks, g_rel) = _unpack_small(total)
    dmod_all = gathered[:, :PK_MOD].reshape(N_DEV, N_MOD * D_MODEL)
    dmod_cols = lax.dynamic_slice_in_dim(dmod_all, chip * n_mod_shard, n_mod_shard, axis=1)
    g_w_mod = _mm(
        "mod_bwd", "tn", (1, n_mod_shard // 768, 1),
        c_all, (N_DEV, D_MODEL), lambda i, j, k: (0, 0),
        dmod_cols, (N_DEV, 768), lambda i, j, k: (0, j),
        _sds((D_MODEL, n_mod_shard), F32), (D_MODEL, 768), lambda i, j, k: (0, j),
        (D_MODEL, 768),
    )

    r_rest = red["rest"][0]
    transposed = {"ffn1_gate", "ffn1_up", "ffn2_gate", "ffn2_up", "w_in", "w_uq", "w_ukv"}
    g_big = {
        "ffn2_gate": red["ffn2_gu"][0], "ffn2_up": red["ffn2_gu"][1], "ffn2_down": red["ffn2_down"][0],
        "w_in": r_rest[O_IN:O_IN + R_IN],
        "w_o": r_rest[O_O:O_O + R_O],
        "w_uq": r_rest[O_UQ:O_UQ + R_UQ].reshape(MLA_QK, MLA_Q_RANK),
        "w_ukv": r_rest[O_UKV:O_UKV + R_UKV].reshape(MLA_NOPE + MLA_V, MLA_KV_RANK),
        "w_mod": g_w_mod,
    }
    w_big = {
        "ffn2_gate": (ffn2_gate, m_ffn2_gate, v_ffn2_gate),
        "ffn2_up": (ffn2_up, m_ffn2_up, v_ffn2_up), "ffn2_down": (ffn2_down, m_ffn2_down, v_ffn2_down),
        "w_in": (w_in, m_w_in, v_w_in), "w_o": (w_o, m_w_o, v_w_o), "w_uq": (w_uq, m_w_uq, v_w_uq),
        "w_ukv": (w_ukv, m_w_ukv, v_w_ukv), "w_mod": (w_mod, m_w_mod, v_w_mod),
    }
    grads, deltas, new_m, new_v = {}, {}, {}, {}

    def update(names):
        for nm in names:
            w, m, v = w_big[nm]
            if nm in transposed:
                outs = _adamw(f"adamw_{nm}", w[0].T, g_big[nm], m[0].T, v[0].T)
                g_, d_, m_, v_ = [a.T for a in (g_big[nm],) + tuple(outs)]
            else:
                g_, (d_, m_, v_) = g_big[nm], _adamw(f"adamw_{nm}", w[0], g_big[nm], m[0], v[0])
            grads[nm], deltas[nm], new_m[nm], new_v[nm] = g_[None], d_[None], m_[None], v_[None]

    update(list(w_big))
    red1_down = reducing["ffn1_down"].get([deltas[nm] for nm in w_big])
    red1_gu = reducing["ffn1_gu"].get([red1_down])
    g_big.update({"ffn1_gate": red1_gu[0], "ffn1_up": red1_gu[1], "ffn1_down": red1_down[0]})
    w_big.update({
        "ffn1_gate": (ffn1_gate, m_ffn1_gate, v_ffn1_gate), "ffn1_up": (ffn1_up, m_ffn1_up, v_ffn1_up),
        "ffn1_down": (ffn1_down, m_ffn1_down, v_ffn1_down),
    })
    update(["ffn1_gate", "ffn1_up", "ffn1_down"])

    small_names = ["b_mod", "norm_ffn1", "norm_mix", "norm_ffn2", "norm_final", "q_norm", "kv_norm", "sinks", "rel_bias"]
    w_small = (b_mod, norm_ffn1, norm_mix, norm_ffn2, norm_final, q_norm, kv_norm, sinks, rel_bias)
    m_small = (m_b_mod, m_norm_ffn1, m_norm_mix, m_norm_ffn2, m_norm_final, m_q_norm, m_kv_norm, m_sinks, m_rel_bias)
    v_small = (v_b_mod, v_norm_ffn1, v_norm_mix, v_norm_ffn2, v_norm_final, v_q_norm, v_kv_norm, v_sinks, v_rel_bias)
    g_small = (g_b_mod, g_n1, g_n2, g_n3, g_nf, g_qn, g_kvn, g_sinks, g_rel)
    d_s, m_s, v_s = _adamw_small(w_small, g_small, m_small, v_small)
    for nm, g_, d_, m_, v_ in zip(small_names, g_small, d_s, m_s, v_s):
        grads[nm], deltas[nm], new_m[nm], new_v[nm] = g_, d_, m_, v_

    order = ["w_mod", "b_mod", "norm_ffn1", "ffn1_gate", "ffn1_up", "ffn1_down", "norm_mix", "w_in", "q_norm", "kv_norm",
             "w_uq", "w_ukv", "sinks", "w_o", "norm_ffn2", "ffn2_gate", "ffn2_up", "ffn2_down", "rel_bias", "norm_final"]
    return (loss, grad_x[None], *[grads[n] for n in order], *[deltas[n] for n in order],
            *[new_m[n] for n in order], *[new_v[n] for n in order])
```

```python
import functools
import math

import jax
import jax.numpy as jnp
import numpy as np
from jax import lax
from jax.experimental import pallas as pl
from jax.experimental.pallas import tpu as pltpu

F32, BF16 = jnp.float32, jnp.bfloat16

D_MODEL = 1024
D_FF = 2816
EPS = 1e-6
N_MOD = 9
SWA_HEADS, SWA_KV_HEADS, SWA_HEAD_DIM, WINDOW = 8, 2, 64, 128
SWA_GROUP = SWA_HEADS // SWA_KV_HEADS
MLA_HEADS, MLA_Q_RANK, MLA_KV_RANK, MLA_NOPE, MLA_ROPE, MLA_V = 4, 256, 128, 128, 64, 128
MLA_QK = MLA_NOPE + MLA_ROPE
ROPE_THETA = 10000.0
NUM_BUCKETS, MAX_DISTANCE = 32, 128
D_IN = 1216
N_SWA_COLS = 768
N_MLA_COLS = 512

ADAM_LR, ADAM_B1, ADAM_B2, ADAM_EPS, ADAM_WD, ADAM_STEP = 0.001, 0.9, 0.999, 1e-08, 0.01, 10

N_CHIPS = 4
N_DEV = 8
F_SHARD = D_FF // N_CHIPS
HALF = F_SHARD // 2
R_IN, R_O, R_UQ, R_UKV = 304, 256, 48, 32
O_IN, O_O, O_UQ, O_UKV, O_PAD = 0, 304, 560, 608, 640

PK_MOD = 72
PK_ROWS = 112
VMEM_LIMIT = 56 << 20
ROW_TILE = 2048

_DN = {
    "nn": (((1,), (0,)), ((), ())),
    "nt": (((1,), (1,)), ((), ())),
    "tn": (((0,), (0,)), ((), ())),
}


def _sds(shape, dtype):
    return jax.ShapeDtypeStruct(tuple(shape), dtype)


def _cparams(sem=None):
    kw = {"vmem_limit_bytes": VMEM_LIMIT}
    if sem is not None:
        kw["dimension_semantics"] = sem
    return pltpu.CompilerParams(**kw)


def _dot(a, b, dims):
    return lax.dot_general(a.astype(BF16), b.astype(BF16), _DN[dims], preferred_element_type=F32)


def _mm(name, dims, grid, a, a_blk, a_idx, b, b_blk, b_idx, out, out_blk, out_idx, acc_shape, alias=None, deps=()):
    nk = grid[2]

    def body(*refs):
        a_ref, b_ref = refs[:2]
        o_ref, acc_ref = refs[-2:]
        part = _dot(a_ref[...], b_ref[...], dims)
        if nk == 1:
            o_ref[...] = part.astype(o_ref.dtype)
            return
        k = pl.program_id(2)

        @pl.when(k == 0)
        def _():
            acc_ref[...] = part

        @pl.when((k > 0) & (k < nk - 1))
        def _():
            acc_ref[...] += part

        @pl.when(k == nk - 1)
        def _():
            o_ref[...] = (acc_ref[...] + part).astype(o_ref.dtype)

    in_specs = [pl.BlockSpec(a_blk, a_idx), pl.BlockSpec(b_blk, b_idx)]
    args = [a, b]
    kw = {}
    if alias is not None:
        in_specs.append(pl.BlockSpec(memory_space=pl.ANY))
        args.append(alias)
        kw["input_output_aliases"] = {2: 0}
    in_specs += [pl.BlockSpec(memory_space=pl.ANY)] * len(deps)
    args += list(deps)
    return pl.pallas_call(
        body,
        name=name,
        grid=grid,
        in_specs=in_specs,
        out_specs=pl.BlockSpec(out_blk, out_idx),
        out_shape=out,
        scratch_shapes=[pltpu.VMEM(acc_shape if nk > 1 else (8, 128), F32)],
        compiler_params=_cparams(("parallel", "parallel", "arbitrary")),
        **kw,
    )(*args)


def _rowwise(name, fn, tiled, full, out_tiled, out_red, tm, deps=()):
    rows = tiled[0].shape[-2]
    grid = (rows // tm,)
    n_in = len(tiled) + len(full)
    n_t = len(out_tiled)
    n_dep = len(deps)

    def tspec(shape):
        nd = len(shape)
        return pl.BlockSpec(tuple(shape[:-2]) + (tm, shape[-1]), lambda i, nd=nd: (0,) * (nd - 2) + (i, 0))

    def fspec(shape):
        nd = len(shape)
        return pl.BlockSpec(tuple(shape), lambda i, nd=nd: (0,) * nd)

    def body(*refs):
        outs = fn(*[r[...] for r in refs[:n_in]])
        if not isinstance(outs, (tuple, list)):
            outs = (outs,)
        out_refs = refs[n_in + n_dep:]
        for r, v in zip(out_refs[:n_t], outs[:n_t]):
            r[...] = v.astype(r.dtype)
        red_refs = out_refs[n_t:]
        if red_refs:
            @pl.when(pl.program_id(0) == 0)
            def _():
                for r in red_refs:
                    r[...] = jnp.zeros_like(r)

            for r, v in zip(red_refs, outs[n_t:]):
                r[...] += v.astype(r.dtype)

    res = pl.pallas_call(
        body,
        name=name,
        grid=grid,
        in_specs=[tspec(a.shape) for a in tiled] + [fspec(a.shape) for a in full]
        + [pl.BlockSpec(memory_space=pl.ANY)] * n_dep,
        out_specs=[tspec(o.shape) for o in out_tiled] + [fspec(o.shape) for o in out_red],
        out_shape=list(out_tiled) + list(out_red),
        compiler_params=_cparams(("arbitrary",) if out_red else ("parallel",)),
    )(*tiled, *full, *deps)
    return res


def _sum0(v):
    return jnp.sum(v, axis=0, keepdims=True)


def _sigmoid(v):
    return 1.0 / (1.0 + jnp.exp(-v))


def _rms(x):
    r = lax.rsqrt(jnp.mean(x * x, axis=-1, keepdims=True) + EPS)
    return x * r, r


def _rms_bwd(u, xr, r):
    return r * (u - xr * jnp.mean(u * xr, axis=-1, keepdims=True))


class _Ready:
    def __init__(self, g):
        self.g = g

    def mid(self, after):
        return []

    def get(self, after):
        return self.g


def _normmod(x_, gamma_, sc_, sh_):
    return _rms(x_)[0] * gamma_ * (1.0 + sc_) + sh_


def _loss_head(x_, t_, g_):
    xr, r = _rms(x_)
    err = xr * g_ - t_
    dy = err * (1.0 / D_MODEL)
    return _rms_bwd(dy * g_, xr, r), _sum0(err * err), _sum0(dy * xr)


def _ffn_fwd(tag, x, gamma, sh, sc, gate, gu_src, base, down_src, down_idx, mid_after, h_pre=None,
             next_norm=None, head=None):
    s_len = x.shape[0]
    if h_pre is None:
        (h,) = _rowwise(
            f"{tag}_normmod", _normmod, [x], [gamma, sc, sh], [_sds((s_len, D_MODEL), BF16)], [], 256,
            deps=gu_src.mid(mid_after),
        )
        gdeps = []
    else:
        h, gdeps = h_pre, gu_src.mid(mid_after)
    g4 = gu_src.get(h)

    tm = min(ROW_TILE, s_len)
    def gate_up(h_ref, wg_ref, wu_ref, *rest):
        ab_ref, s_ref = rest[-2:]
        hv = h_ref[...]
        a = _dot(hv, wg_ref[...], "nt")
        b = _dot(hv, wu_ref[...], "nt")
        ab_ref[0] = a.astype(ab_ref.dtype)
        ab_ref[1] = b.astype(ab_ref.dtype)
        s_ref[...] = (a * _sigmoid(a) * b).astype(s_ref.dtype)

    ab4, s3 = pl.pallas_call(
        gate_up,
        name=f"{tag}_gate_up",
        grid=(s_len // tm, N_CHIPS),
        in_specs=[
            pl.BlockSpec((tm, D_MODEL), lambda i, c: (i, 0)),
            pl.BlockSpec((None, None, F_SHARD, D_MODEL), lambda i, c: (c, base, 0, 0)),
            pl.BlockSpec((None, None, F_SHARD, D_MODEL), lambda i, c: (c, base + 1, 0, 0)),
        ] + [pl.BlockSpec(memory_space=pl.ANY)] * len(gdeps),
        out_specs=[
            pl.BlockSpec((None, 2, tm, F_SHARD), lambda i, c: (c, 0, i, 0)),
            pl.BlockSpec((None, tm, F_SHARD), lambda i, c: (c, i, 0)),
        ],
        out_shape=[_sds((N_CHIPS, 2, s_len, F_SHARD), BF16), _sds((N_CHIPS, s_len, F_SHARD), BF16)],
        compiler_params=_cparams(("parallel", "parallel")),
    )(h, g4, g4, *gdeps)
    if down_src is None:
        g_down, ddeps = g4, ()
    else:
        ddeps = down_src.mid(ab4)
        g_down = down_src.get(s3)

    y = _mm(
        f"{tag}_down", "nn", (s_len // tm, 1, N_CHIPS),
        s3, (None, tm, F_SHARD), lambda i, j, k: (k, i, 0),
        g_down, (None, None, F_SHARD, D_MODEL), lambda i, j, k: (k, down_idx, 0, 0),
        _sds((s_len, D_MODEL), F32), (tm, D_MODEL), lambda i, j, k: (i, 0),
        (tm, D_MODEL), deps=ddeps,
    )

    saved = (g4, base, g_down, down_idx, h, ab4, s3, y)
    act = _sds((s_len, D_MODEL), F32)
    if head is not None:
        target, norm_final = head
        vec = _sds((1, D_MODEL), F32)
        out = _rowwise(
            f"{tag}_residual_head", lambda x_, y_, t_, g_, nf_: _loss_head(x_ + 0.5 * g_ * y_, t_, nf_),
            [x, y, target], [gate, norm_final], [act], [vec, vec], 256,
        )
    elif next_norm is not None:
        def residual_norm(x_, y_, g_, gamma_, sc_, sh_):
            x_out = x_ + 0.5 * g_ * y_
            return x_out, _normmod(x_out, gamma_, sc_, sh_)

        out = _rowwise(
            f"{tag}_residual_norm", residual_norm, [x, y], [gate] + list(next_norm),
            [act, _sds((s_len, D_MODEL), BF16)], [], 256,
        )
    else:
        out = _rowwise(f"{tag}_residual", lambda x_, y_, g_: x_ + 0.5 * g_ * y_, [x, y], [gate], [act], [], 256)
    return out, saved


def _normmod_bwd_call(name, dh_parts, x, dxo, gamma, sc, deps=()):
    s_len = x.shape[0]
    n_parts = len(dh_parts)

    def fn(*vals):
        dh = vals[0]
        for extra in vals[1:n_parts]:
            dh = dh + extra
        x_, dxo_, gamma_, sc_ = vals[n_parts:]
        xr, r = _rms(x_)
        n = xr * gamma_
        dn = dh * (1.0 + sc_)
        dx = dxo_ + _rms_bwd(dn * gamma_, xr, r)
        return dx, _sum0(dh * n), _sum0(dh), _sum0(dn * xr)

    vec = _sds((1, D_MODEL), F32)
    return _rowwise(
        name, fn, list(dh_parts) + [x, dxo], [gamma, sc], [_sds((s_len, D_MODEL), F32)], [vec, vec, vec], 256, deps=deps
    )


class _BwdHooks:
    def __init__(self, after_gate=None, after_swiglu=None, after_wdown=None, after_wgu=None, after_dh=None):
        def nothing(_):
            return []

        self.after_gate = after_gate or nothing
        self.after_swiglu = after_swiglu or nothing
        self.after_wdown = after_wdown or nothing
        self.after_wgu = after_wgu or nothing
        self.after_dh = after_dh or nothing


def _ffn_bwd(tag, dxo, x, gamma, sc, gate, saved, hooks, deps=()):
    g4, base, g_down, down_idx, h, ab4, s3, y = saved
    s_len = x.shape[0]
    vec = _sds((1, D_MODEL), F32)

    dy, dgate = _rowwise(
        f"{tag}_bwd_gate", lambda dxo_, y_, g_: (0.5 * g_ * dxo_, _sum0(0.5 * y_ * dxo_)),
        [dxo, y], [gate], [_sds((s_len, D_MODEL), BF16)], [vec], 256, deps=deps,
    )

    tm = min(ROW_TILE, s_len)

    def d_gate_up(dy_ref, wd_ref, ab_ref, o_ref):
        ds = _dot(dy_ref[...], wd_ref[...], "nt")
        a = ab_ref[0].astype(F32)
        b = ab_ref[1].astype(F32)
        sig = _sigmoid(a)
        o_ref[0] = (ds * b * sig * (1.0 + a * (1.0 - sig))).astype(o_ref.dtype)
        o_ref[1] = (ds * a * sig).astype(o_ref.dtype)

    ab_spec = pl.BlockSpec((None, 2, tm, F_SHARD), lambda i, c: (c, 0, i, 0))
    dab4 = pl.pallas_call(
        d_gate_up,
        name=f"{tag}_bwd_dgate_up",
        grid=(s_len // tm, N_CHIPS),
        in_specs=[
            pl.BlockSpec((tm, D_MODEL), lambda i, c: (i, 0)),
            pl.BlockSpec((None, None, F_SHARD, D_MODEL), lambda i, c: (c, down_idx, 0, 0)),
            ab_spec,
        ],
        out_specs=ab_spec,
        out_shape=_sds(ab4.shape, BF16),
        compiler_params=_cparams(("parallel", "parallel")),
    )(dy, g_down, ab4)

    tk = tm
    gb_down = _mm(
        f"{tag}_bwd_wdown", "tn", (N_CHIPS, 1, s_len // tk),
        s3, (None, tk, F_SHARD), lambda i, j, k: (i, k, 0),
        dy, (tk, D_MODEL), lambda i, j, k: (k, 0),
        _sds((N_CHIPS, 1, F_SHARD, D_MODEL), BF16), (None, None, F_SHARD, D_MODEL), lambda i, j, k: (i, 0, 0, 0),
        (F_SHARD, D_MODEL), deps=hooks.after_swiglu(dab4),
    )
    gb_gu = _mm(
        f"{tag}_bwd_wgu", "tn", (2 * N_CHIPS, 1, s_len // tk),
        dab4, (None, None, tk, F_SHARD), lambda i, j, k: (i % N_CHIPS, i // N_CHIPS, k, 0),
        h, (tk, D_MODEL), lambda i, j, k: (k, 0),
        _sds((N_CHIPS, 2, F_SHARD, D_MODEL), BF16), (None, None, F_SHARD, D_MODEL),
        lambda i, j, k: (i % N_CHIPS, i // N_CHIPS, 0, 0),
        (F_SHARD, D_MODEL), deps=hooks.after_wdown(gb_down),
    )
    dh = _mm(
        f"{tag}_bwd_dh", "nn", (s_len // tm, 1, 2 * N_CHIPS),
        dab4, (None, None, tm, F_SHARD), lambda i, j, k: (k % N_CHIPS, k // N_CHIPS, i, 0),
        g4, (None, None, F_SHARD, D_MODEL), lambda i, j, k: (k % N_CHIPS, base + k // N_CHIPS, 0, 0),
        _sds((s_len, D_MODEL), F32), (tm, D_MODEL), lambda i, j, k: (i, 0),
        (tm, D_MODEL), deps=hooks.after_wgu(gb_gu),
    )
    dx, dsc, dsh, dgamma = _normmod_bwd_call(
        f"{tag}_bwd_normmod", [dh], x, dxo, gamma, sc, deps=hooks.after_dh(dh)
    )
    return dx, (gb_down, gb_gu), (dsh, dsc, dgate, dgamma)


def _bucket_map():
    qi = np.arange(WINDOW)[:, None]
    kj = np.arange(2 * WINDOW)[None, :]
    dist = qi + WINDOW - kj
    band = (dist >= 0) & (dist < WINDOW)
    max_exact = NUM_BUCKETS // 2
    n = np.maximum(dist, 0)
    large = []
    for dt in (np.float32, np.float64):
        nf = np.maximum(n, 1).astype(dt)
        val = np.log(nf / dt(max_exact)) / dt(math.log(MAX_DISTANCE / max_exact)) * dt(NUM_BUCKETS - max_exact)
        large.append(np.minimum(max_exact + val.astype(np.int32), NUM_BUCKETS - 1))
    assert np.array_equal(large[0], large[1])
    bucket = np.where(n < max_exact, n, large[0])
    return np.where(band, bucket, -1).astype(np.int32).reshape(1, -1)


def _bias_expand(rel_bias_t, bkt):
    n = bkt.shape[1]

    def body(rb_ref, bkt_ref, o_ref):
        onehot = (lax.broadcasted_iota(jnp.int32, (NUM_BUCKETS, n), 0) == bkt_ref[...]).astype(F32)
        o_ref[...] = lax.dot_general(
            rb_ref[...], onehot, _DN["nn"], precision=lax.Precision.HIGHEST, preferred_element_type=F32
        )

    return pl.pallas_call(
        body, name="bias_expand", out_shape=_sds((SWA_HEADS, n), F32), compiler_params=_cparams()
    )(rel_bias_t, bkt)


def _bias_reduce(dbias_flat, bkt):
    n = bkt.shape[1]

    def body(db_ref, bkt_ref, o_ref):
        onehot = (lax.broadcasted_iota(jnp.int32, (NUM_BUCKETS, n), 0) == bkt_ref[...]).astype(F32)
        o_ref[...] = lax.dot_general(
            db_ref[...], onehot, _DN["nt"], precision=lax.Precision.HIGHEST, preferred_element_type=F32
        )

    return pl.pallas_call(
        body, name="bias_reduce", out_shape=_sds((SWA_HEADS, NUM_BUCKETS), F32), compiler_params=_cparams()
    )(dbias_flat, bkt)


_SWA_SCALE = SWA_HEAD_DIM ** -0.5
_NEG = -1e30


def _swa_in_specs():
    w = WINDOW
    n_q = SWA_HEADS * SWA_HEAD_DIM
    n_kv = 2 * SWA_KV_HEADS * SWA_HEAD_DIM
    prev = lambda n: jnp.maximum(n - 1, 0)
    return [
        pl.BlockSpec((w, n_q), lambda n: (n, 0)),
        pl.BlockSpec((w, n_kv), lambda n: (prev(n), n_q // n_kv)),
        pl.BlockSpec((w, n_kv), lambda n: (n, n_q // n_kv)),
        pl.BlockSpec((SWA_HEADS, w, 2 * w), lambda n: (0, 0, 0)),
        pl.BlockSpec((SWA_HEADS, w, 1), lambda n: (0, 0, 0)),
    ]


def _head_cols(v, first, count):
    hd = SWA_HEAD_DIM
    return jnp.stack([v[:, (first + i) * hd:(first + i + 1) * hd] for i in range(count)])


def _swa_heads(q_ref, kvp_ref, kvc_ref):
    g, w, hd = SWA_GROUP, WINDOW, SWA_HEAD_DIM
    q = q_ref[...].astype(F32)
    kv = jnp.concatenate([kvp_ref[...], kvc_ref[...]], axis=0).astype(F32)
    heads = []
    for j in range(SWA_KV_HEADS):
        qj = _head_cols(q, g * j, g).reshape(g * w, hd)
        heads.append((qj, _head_cols(kv, j, 1)[0], _head_cols(kv, SWA_KV_HEADS + j, 1)[0]))
    return heads


def _swa_scores(q, kk, bias, n):
    g, w = SWA_GROUP, WINDOW
    s = (_dot(q, kk, "nt") * _SWA_SCALE).reshape(g, w, 2 * w) + bias
    qi = lax.broadcasted_iota(jnp.int32, (w, 2 * w), 0)
    kj = lax.broadcasted_iota(jnp.int32, (w, 2 * w), 1)
    dist = qi + w - kj
    valid = ((dist >= 0) & (dist < w) & ((n > 0) | (kj >= w)))[None]
    return jnp.where(valid, s, _NEG), valid


def _swa_fwd(swa2, bias, sinkb):
    s_len = swa2.shape[0]
    g, w, hd = SWA_GROUP, WINDOW, SWA_HEAD_DIM

    def body(q_ref, kvp_ref, kvc_ref, bias_ref, sink_ref, o_ref, lse_ref):
        n = pl.program_id(0)
        outs = []
        for j, (q, kk, vv) in enumerate(_swa_heads(q_ref, kvp_ref, kvc_ref)):
            hs = slice(g * j, g * (j + 1))
            s, valid = _swa_scores(q, kk, bias_ref[hs], n)
            sink = sink_ref[hs]
            m = jnp.maximum(jnp.max(s, axis=-1, keepdims=True), sink)
            p = jnp.where(valid, jnp.exp(s - m), 0.0)
            l = jnp.sum(p, axis=-1, keepdims=True) + jnp.exp(sink - m)
            o = _dot(p.reshape(g * w, 2 * w), vv, "nn").reshape(g, w, hd) / l
            outs += [o[i] for i in range(g)]
            lse_ref[hs] = m + jnp.log(l)
        o_ref[...] = jnp.concatenate(outs, axis=-1).astype(o_ref.dtype)

    n_q = SWA_HEADS * hd
    return pl.pallas_call(
        body,
        name="swa_fwd",
        grid=(s_len // w,),
        in_specs=_swa_in_specs(),
        out_specs=[
            pl.BlockSpec((w, n_q), lambda n: (n, 0)),
            pl.BlockSpec((SWA_HEADS, w, 1), lambda n: (0, n, 0)),
        ],
        out_shape=[_sds((s_len, n_q), BF16), _sds((SWA_HEADS, s_len, 1), F32)],
        compiler_params=_cparams(("parallel",)),
    )(swa2, swa2, swa2, bias, sinkb)


def _swa_bwd(swa2, bias, sinkb, cat, dcat, lse):
    s_len = swa2.shape[0]
    g, w, hd = SWA_GROUP, WINDOW, SWA_HEAD_DIM

    def body(q_ref, kvp_ref, kvc_ref, bias_ref, sink_ref, o_ref, do_ref, lse_ref,
             dq_ref, dkva_ref, dkvb_ref, dbias_ref, dsink_ref):
        n = pl.program_id(0)

        @pl.when(n == 0)
        def _():
            dbias_ref[...] = jnp.zeros_like(dbias_ref)
            dsink_ref[...] = jnp.zeros_like(dsink_ref)

        o_all = o_ref[...].astype(F32)
        do_all = do_ref[...].astype(F32)
        dqs, dks, dvs = [], [], []
        for j, (q, kk, vv) in enumerate(_swa_heads(q_ref, kvp_ref, kvc_ref)):
            hs = slice(g * j, g * (j + 1))
            s, valid = _swa_scores(q, kk, bias_ref[hs], n)
            lse_v = lse_ref[hs]
            p = jnp.where(valid, jnp.exp(s - lse_v), 0.0)
            do = _head_cols(do_all, g * j, g)
            delta = jnp.sum(do * _head_cols(o_all, g * j, g), axis=-1, keepdims=True)
            do2 = do.reshape(g * w, hd)
            dp = _dot(do2, vv, "nt").reshape(g, w, 2 * w)
            ds = p * (dp - delta)
            dbias_ref[hs] += ds
            dsink_ref[hs] -= jnp.exp(sink_ref[hs] - lse_v) * delta
            ds2 = ds.reshape(g * w, 2 * w)
            dq = (_dot(ds2, kk, "nn") * _SWA_SCALE).reshape(g, w, hd)
            dqs += [dq[i] for i in range(g)]
            dks.append(_dot(ds2, q, "tn") * _SWA_SCALE)
            dvs.append(_dot(p.reshape(g * w, 2 * w), do2, "tn"))
        dq_ref[...] = jnp.concatenate(dqs, axis=-1).astype(dq_ref.dtype)
        dkv = jnp.concatenate(dks + dvs, axis=-1)
        dkvb_ref[...] = dkv[:w]
        dkva_ref[...] = dkv[w:]

    n_q = SWA_HEADS * hd
    n_kv = 2 * SWA_KV_HEADS * hd
    q_spec = pl.BlockSpec((w, n_q), lambda n: (n, 0))
    kv_spec = pl.BlockSpec((w, n_kv), lambda n: (n, 0))
    return pl.pallas_call(
        body,
        name="swa_bwd",
        grid=(s_len // w,),
        in_specs=_swa_in_specs() + [q_spec, q_spec, pl.BlockSpec((SWA_HEADS, w, 1), lambda n: (0, n, 0))],
        out_specs=[
            q_spec, kv_spec, kv_spec,
            pl.BlockSpec((SWA_HEADS, w, 2 * w), lambda n: (0, 0, 0)),
            pl.BlockSpec((SWA_HEADS, w, 1), lambda n: (0, 0, 0)),
        ],
        out_shape=[
            _sds((s_len, n_q), BF16), _sds((s_len, n_kv), F32), _sds((s_len, n_kv), F32),
            _sds((SWA_HEADS, w, 2 * w), F32), _sds((SWA_HEADS, w, 1), F32),
        ],
        compiler_params=_cparams(("arbitrary",)),
    )(swa2, swa2, swa2, bias, sinkb, cat, dcat, lse)


_MLA_SCALE = MLA_QK ** -0.5
_MLA_TQ = 256


def _mla_mask(i, tq, n_keys):
    row = i * tq + lax.broadcasted_iota(jnp.int32, (tq, n_keys), 0)
    col = lax.broadcasted_iota(jnp.int32, (tq, n_keys), 1)
    return col <= row


def _per_query_block(i, n_blocks, fn):
    for ii in range(n_blocks):
        pl.when(i == ii)(functools.partial(fn, ii))


def _mla_fwd(q4, k4, v4):
    s_len = q4.shape[1]
    tq = min(_MLA_TQ, s_len)

    def body(q_ref, k_ref, v_ref, o_ref, lse_ref):
        def block(ii):
            n_keys = (ii + 1) * tq
            mask = _mla_mask(ii, tq, n_keys)
            s = jnp.where(mask, _dot(q_ref[...], k_ref[:n_keys, :], "nt") * _MLA_SCALE, _NEG)
            m = jnp.max(s, axis=-1, keepdims=True)
            p = jnp.exp(s - m)
            l = jnp.sum(p, axis=-1, keepdims=True)
            o_ref[...] = (_dot(p, v_ref[:n_keys, :], "nn") / l).astype(o_ref.dtype)
            lse_ref[...] = m + jnp.log(l)

        _per_query_block(pl.program_id(1), s_len // tq, block)

    return pl.pallas_call(
        body,
        name="mla_fwd",
        grid=(MLA_HEADS, s_len // tq),
        in_specs=[
            pl.BlockSpec((None, tq, MLA_QK), lambda h, i: (h, i, 0)),
            pl.BlockSpec((None, s_len, MLA_QK), lambda h, i: (h, 0, 0)),
            pl.BlockSpec((None, s_len, MLA_V), lambda h, i: (h, 0, 0)),
        ],
        out_specs=[
            pl.BlockSpec((tq, MLA_V), lambda h, i: (i, h)),
            pl.BlockSpec((None, tq, 1), lambda h, i: (h, i, 0)),
        ],
        out_shape=[_sds((s_len, MLA_HEADS * MLA_V), BF16), _sds((MLA_HEADS, s_len, 1), F32)],
        compiler_params=_cparams(("parallel", "parallel")),
    )(q4, k4, v4)


def _mla_bwd(q4, k4, v4, cat, dcat, lse):
    s_len = q4.shape[1]
    tq = min(_MLA_TQ, s_len)
    first = SWA_HEADS * SWA_HEAD_DIM // MLA_V

    def body(q_ref, k_ref, v_ref, o_ref, do_ref, lse_ref, dq_ref, dk_ref, dv_ref):
        i = pl.program_id(1)

        @pl.when(i == 0)
        def _():
            dk_ref[...] = jnp.zeros_like(dk_ref)
            dv_ref[...] = jnp.zeros_like(dv_ref)

        def block(ii):
            n_keys = (ii + 1) * tq
            mask = _mla_mask(ii, tq, n_keys)
            q, k, v, do = q_ref[...], k_ref[:n_keys, :], v_ref[:n_keys, :], do_ref[...]
            s = jnp.where(mask, _dot(q, k, "nt") * _MLA_SCALE, _NEG)
            p = jnp.where(mask, jnp.exp(s - lse_ref[...]), 0.0)
            delta = jnp.sum(do.astype(F32) * o_ref[...].astype(F32), axis=-1, keepdims=True)
            ds = (p * (_dot(do, v, "nt") - delta) * _MLA_SCALE).astype(BF16)
            dq_ref[...] = _dot(ds, k, "nn")
            dk_ref[:n_keys, :] += _dot(ds, q, "tn")
            dv_ref[:n_keys, :] += _dot(p, do, "tn")

        _per_query_block(i, s_len // tq, block)

    return pl.pallas_call(
        body,
        name="mla_bwd",
        grid=(MLA_HEADS, s_len // tq),
        in_specs=[
            pl.BlockSpec((None, tq, MLA_QK), lambda h, i: (h, i, 0)),
            pl.BlockSpec((None, s_len, MLA_QK), lambda h, i: (h, 0, 0)),
            pl.BlockSpec((None, s_len, MLA_V), lambda h, i: (h, 0, 0)),
            pl.BlockSpec((tq, MLA_V), lambda h, i: (i, first + h)),
            pl.BlockSpec((tq, MLA_V), lambda h, i: (i, first + h)),
            pl.BlockSpec((None, tq, 1), lambda h, i: (h, i, 0)),
        ],
        out_specs=[
            pl.BlockSpec((None, tq, MLA_QK), lambda h, i: (h, i, 0)),
            pl.BlockSpec((None, s_len, MLA_QK), lambda h, i: (h, 0, 0)),
            pl.BlockSpec((None, s_len, MLA_V), lambda h, i: (h, 0, 0)),
        ],
        out_shape=[
            _sds((MLA_HEADS, s_len, MLA_QK), F32),
            _sds((MLA_HEADS, s_len, MLA_QK), F32),
            _sds((MLA_HEADS, s_len, MLA_V), F32),
        ],
        compiler_params=_cparams(("parallel", "arbitrary")),
    )(q4, k4, v4, cat, dcat, lse)


def _mla_split(pm):
    q_lat = pm[:, :MLA_Q_RANK]
    kv_lat = pm[:, MLA_Q_RANK:MLA_Q_RANK + MLA_KV_RANK]
    kr = pm[:, MLA_Q_RANK + MLA_KV_RANK:MLA_Q_RANK + MLA_KV_RANK + MLA_ROPE]
    kr_sw = pm[:, MLA_Q_RANK + MLA_KV_RANK + MLA_ROPE:]
    return q_lat, kv_lat, kr, kr_sw


def _mla_proj(pm, cos2, sin2, q_norm, kv_norm, wq_ext, wkv):
    s_len = pm.shape[0]

    def fn(pm_, cos_, sin_, qg, kvg, wq, wk):
        q_lat, kv_lat, kr, kr_sw = _mla_split(pm_)
        nq = (_rms(q_lat)[0] * qg).astype(BF16)
        nkv = (_rms(kv_lat)[0] * kvg).astype(BF16)
        k_rot = kr * cos_ + kr_sw * sin_
        qs, ks, vs = [], [], []
        for h in range(MLA_HEADS):
            qe = _dot(nq, wq[h], "nt")
            q_rot = qe[:, MLA_NOPE:MLA_QK] * cos_ + qe[:, MLA_QK:] * sin_
            qs.append(jnp.concatenate([qe[:, :MLA_NOPE], q_rot], axis=-1))
            kve = _dot(nkv, wk[h], "nt")
            ks.append(jnp.concatenate([kve[:, :MLA_NOPE], k_rot], axis=-1))
            vs.append(kve[:, MLA_NOPE:])
        return jnp.stack(qs), jnp.stack(ks), jnp.stack(vs)

    return _rowwise(
        "mla_proj", fn, [pm, cos2, sin2], [q_norm, kv_norm, wq_ext, wkv],
        [_sds((MLA_HEADS, s_len, MLA_QK), BF16), _sds((MLA_HEADS, s_len, MLA_QK), BF16),
         _sds((MLA_HEADS, s_len, MLA_V), BF16)], [], 256,
    )


def _mla_proj_bwd(pm, cos2, sin2, dq4, dk4, dv4, q_norm, kv_norm, wq_ext, wkv):
    s_len = pm.shape[0]

    def fn(pm_, cos_, sin_, dq, dk, dv, qg, kvg, wq, wk):
        q_lat, kv_lat, _, _ = _mla_split(pm_)
        xq, rq = _rms(q_lat)
        xkv, rkv = _rms(kv_lat)
        nq = (xq * qg).astype(BF16)
        nkv = (xkv * kvg).astype(BF16)
        dnq = jnp.zeros_like(q_lat)
        dnkv = jnp.zeros_like(kv_lat)
        dkr = jnp.zeros_like(cos_)
        dwq, dwk = [], []
        for h in range(MLA_HEADS):
            dy = dq[h][:, MLA_NOPE:]
            dqe = jnp.concatenate([dq[h][:, :MLA_NOPE], dy * cos_, dy * sin_], axis=-1).astype(BF16)
            dnq = dnq + _dot(dqe, wq[h], "nn")
            dwq.append(_dot(dqe, nq, "tn"))
            dkve = jnp.concatenate([dk[h][:, :MLA_NOPE], dv[h]], axis=-1).astype(BF16)
            dnkv = dnkv + _dot(dkve, wk[h], "nn")
            dwk.append(_dot(dkve, nkv, "tn"))
            dkr = dkr + dk[h][:, MLA_NOPE:]
        dq_lat = _rms_bwd(dnq * qg, xq, rq)
        dkv_lat = _rms_bwd(dnkv * kvg, xkv, rkv)
        dpm = jnp.concatenate([dq_lat, dkv_lat, dkr * cos_, dkr * sin_], axis=-1)
        return dpm, _sum0(dnq * xq), _sum0(dnkv * xkv), jnp.stack(dwq), jnp.stack(dwk)

    return _rowwise(
        "mla_proj_bwd", fn, [pm, cos2, sin2, dq4, dk4, dv4], [q_norm, kv_norm, wq_ext, wkv],
        [_sds((s_len, N_MLA_COLS), BF16)],
        [_sds((1, MLA_Q_RANK), F32), _sds((1, MLA_KV_RANK), F32),
         _sds(wq_ext.shape, F32), _sds(wkv.shape, F32)], 256,
    )


def _rope_tables(s_len):
    inv = ROPE_THETA ** (-jnp.arange(0, MLA_ROPE, 2, dtype=F32) / MLA_ROPE)
    ang = jnp.arange(s_len, dtype=F32)[:, None] * inv[None, :]
    cos, sin = jnp.cos(ang), jnp.sin(ang)
    return jnp.concatenate([cos, cos], axis=-1), jnp.concatenate([-sin, sin], axis=-1)


def _mix_weights(g_mix):
    rest = g_mix[:, 0]
    w_in_t = rest[:, O_IN:O_IN + R_IN].reshape(D_IN, D_MODEL)
    w_o = rest[:, O_O:O_O + R_O].reshape(D_MODEL, D_MODEL)
    w_uq_t = rest[:, O_UQ:O_UQ + R_UQ].reshape(MLA_HEADS, MLA_QK, MLA_Q_RANK)
    w_ukv_t = rest[:, O_UKV:O_UKV + R_UKV].reshape(MLA_HEADS, MLA_NOPE + MLA_V, MLA_KV_RANK)
    half = MLA_ROPE // 2
    w_swa = w_in_t[:N_SWA_COLS]
    w_mla = jnp.concatenate([w_in_t[N_SWA_COLS:], w_in_t[D_IN - half:], w_in_t[D_IN - MLA_ROPE:D_IN - half]], axis=0)
    wq_ext = jnp.concatenate([w_uq_t, w_uq_t[:, MLA_QK - half:], w_uq_t[:, MLA_NOPE:MLA_QK - half]], axis=1)
    return w_swa, w_mla, w_o, wq_ext, w_ukv_t


def _mix_fwd(x, h, gate, mix_src, q_norm, kv_norm, bias, sinkb, cos2, sin2, next_norm):
    s_len = x.shape[0]
    tm = min(ROW_TILE, s_len)
    deps = mix_src.mid(x)
    weights = _mix_weights(mix_src.get(h))
    w_swa, w_mla, w_o, wq_ext, wkv = weights
    swa2 = _mm(
        "mix_proj_swa", "nt", (s_len // tm, 1, 1),
        h, (tm, D_MODEL), lambda i, j, k: (i, 0),
        w_swa, (N_SWA_COLS, D_MODEL), lambda i, j, k: (0, 0),
        _sds((s_len, N_SWA_COLS), BF16), (tm, N_SWA_COLS), lambda i, j, k: (i, 0),
        (tm, N_SWA_COLS), deps=deps,
    )
    pm = _mm(
        "mix_proj_mla", "nt", (s_len // tm, 1, 1),
        h, (tm, D_MODEL), lambda i, j, k: (i, 0),
        w_mla, (N_MLA_COLS, D_MODEL), lambda i, j, k: (0, 0),
        _sds((s_len, N_MLA_COLS), F32), (tm, N_MLA_COLS), lambda i, j, k: (i, 0),
        (tm, N_MLA_COLS),
    )
    q4, k4, v4 = _mla_proj(pm, cos2, sin2, q_norm, kv_norm, wq_ext, wkv)
    oa, lse_a = _swa_fwd(swa2, bias, sinkb)
    ob, lse_b = _mla_fwd(q4, k4, v4)
    cat = jnp.concatenate([oa, ob], axis=1)
    z = _mm(
        "mix_out", "nn", (s_len // tm, 1, 1),
        cat, (tm, D_MODEL), lambda i, j, k: (i, 0),
        w_o, (D_MODEL, D_MODEL), lambda i, j, k: (0, 0),
        _sds((s_len, D_MODEL), F32), (tm, D_MODEL), lambda i, j, k: (i, 0),
        (tm, D_MODEL),
    )
    def residual_norm(x_, z_, g_, gamma_, sc_, sh_):
        x_out = x_ + g_ * z_
        return x_out, _normmod(x_out, gamma_, sc_, sh_)

    out = _rowwise(
        "mix_residual_norm", residual_norm, [x, z], [gate] + list(next_norm),
        [_sds((s_len, D_MODEL), F32), _sds((s_len, D_MODEL), BF16)], [], 256,
    )
    return out, (weights, h, swa2, pm, q4, k4, v4, cat, lse_a, lse_b, z)


def _mix_bwd(dxo, x, gamma, sc, gate, q_norm, kv_norm, bias, sinkb, cos2, sin2, saved, hooks):
    weights, h, swa2, pm, q4, k4, v4, cat, lse_a, lse_b, z = saved
    w_swa, w_mla, w_o, wq_ext, wkv = weights
    s_len = x.shape[0]
    tm = tk = min(ROW_TILE, s_len)
    vec = _sds((1, D_MODEL), F32)
    n_proj = N_SWA_COLS + N_MLA_COLS
    half_d = D_MODEL // 2

    dz, dgate = _rowwise(
        "mix_bwd_gate", lambda dxo_, z_, g_: (g_ * dxo_, _sum0(z_ * dxo_)),
        [dxo, z], [gate], [_sds((s_len, D_MODEL), BF16)], [vec], 256,
    )
    dw_o = _mm(
        "mix_bwd_wo", "tn", (2, 1, s_len // tk),
        cat, (tk, half_d), lambda i, j, k: (k, i),
        dz, (tk, D_MODEL), lambda i, j, k: (k, 0),
        _sds((D_MODEL, D_MODEL), F32), (half_d, D_MODEL), lambda i, j, k: (i, 0),
        (half_d, D_MODEL),
    )
    dcat = _mm(
        "mix_bwd_dcat", "nt", (s_len // tm, 1, 1),
        dz, (tm, D_MODEL), lambda i, j, k: (i, 0),
        w_o, (D_MODEL, D_MODEL), lambda i, j, k: (0, 0),
        _sds((s_len, D_MODEL), BF16), (tm, D_MODEL), lambda i, j, k: (i, 0),
        (tm, D_MODEL), deps=hooks.after_gate(dz),
    )
    dq_a, dkva, dkvb, dbias, dsink = _swa_bwd(swa2, bias, sinkb, cat, dcat, lse_a)
    dq4, dk4, dv4 = _mla_bwd(q4, k4, v4, cat, dcat, lse_b)
    dpm, dq_norm, dkv_norm, dwq_ext, dwkv = _mla_proj_bwd(pm, cos2, sin2, dq4, dk4, dv4, q_norm, kv_norm, wq_ext, wkv)

    dkv = dkva + jnp.concatenate([dkvb[WINDOW:], jnp.zeros_like(dkvb[:WINDOW])], axis=0)
    dproj = jnp.concatenate([dq_a, dkv.astype(BF16), dpm], axis=1)
    w_proj = jnp.concatenate([w_swa, w_mla], axis=0)

    dw_proj = _mm(
        "mix_bwd_win", "tn", (n_proj // 256, 1, s_len // tk),
        dproj, (tk, 256), lambda i, j, k: (k, i),
        h, (tk, D_MODEL), lambda i, j, k: (k, 0),
        _sds((n_proj, D_MODEL), F32), (256, D_MODEL), lambda i, j, k: (i, 0),
        (256, D_MODEL),
    )
    dw_swa, dw_mla = dw_proj[:N_SWA_COLS], dw_proj[N_SWA_COLS:]
    dh = _mm(
        "mix_bwd_dh", "nn", (s_len // tm, 1, 1),
        dproj, (tm, n_proj), lambda i, j, k: (i, 0),
        w_proj, (n_proj, D_MODEL), lambda i, j, k: (0, 0),
        _sds((s_len, D_MODEL), F32), (tm, D_MODEL), lambda i, j, k: (i, 0),
        (tm, D_MODEL),
    )
    dx, dsc, dsh, dgamma = _normmod_bwd_call("mix_bwd_normmod", [dh], x, dxo, gamma, sc)

    half = MLA_ROPE // 2
    n_mla_in = D_IN - N_SWA_COLS
    dw_mla_base = dw_mla[:n_mla_in]
    dw_mla_base = dw_mla_base.at[n_mla_in - half:].add(dw_mla[n_mla_in:n_mla_in + half])
    dw_mla_base = dw_mla_base.at[n_mla_in - MLA_ROPE:n_mla_in - half].add(dw_mla[n_mla_in + half:])
    dw_in_t = jnp.concatenate([dw_swa, dw_mla_base], axis=0)
    dw_uq_t = dwq_ext[:, :MLA_QK]
    dw_uq_t = dw_uq_t.at[:, MLA_QK - half:].add(dwq_ext[:, MLA_QK:MLA_QK + half])
    dw_uq_t = dw_uq_t.at[:, MLA_NOPE:MLA_QK - half].add(dwq_ext[:, MLA_QK + half:])
    rest = jnp.concatenate(
        [
            dw_in_t.reshape(N_CHIPS, R_IN, D_MODEL),
            dw_o.reshape(N_CHIPS, R_O, D_MODEL),
            dw_uq_t.reshape(N_CHIPS, R_UQ, D_MODEL),
            dwkv.reshape(N_CHIPS, R_UKV, D_MODEL),
            jnp.zeros((N_CHIPS, F_SHARD - O_PAD, D_MODEL), F32),
        ],
        axis=1,
    ).astype(BF16)
    return dx, rest, (dsh, dsc, dgate, dgamma), dq_norm, dkv_norm, dbias, dsink


def _device_step(x, target, mod, gu1_src, down1_src, mix_src, ffn2_src, norms, q_norm, kv_norm, sinks, rel_bias,
                 hooks2=None, hooks_mix=None, mix_done=None, hooks1=None):
    s_len = x.shape[0]
    sh1, sc1, g1, sh2, sc2, g2, sh3, sc3, g3 = [mod[:, i * D_MODEL:(i + 1) * D_MODEL] for i in range(N_MOD)]
    n1, n2, n3, nf = norms
    bkt = jnp.asarray(_bucket_map())
    bias = _bias_expand(rel_bias.T, bkt).reshape(SWA_HEADS, WINDOW, 2 * WINDOW)
    sinkb = jnp.broadcast_to(sinks.reshape(SWA_HEADS, 1, 1), (SWA_HEADS, WINDOW, 1))
    cos2, sin2 = _rope_tables(s_len)

    (x1, h2), saved1 = _ffn_fwd(
        "ffn1", x, n1, sh1, sc1, g1, gu1_src, 0, down1_src, 0, sh1, next_norm=(n2, sc2, sh2)
    )
    (x2, h3), saved2 = _mix_fwd(
        x1, h2, g2, mix_src, q_norm, kv_norm, bias, sinkb, cos2, sin2, next_norm=(n3, sc3, sh3)
    )
    (dx3, sq, dnf), saved3 = _ffn_fwd(
        "ffn2", x2, n3, sh3, sc3, g3, ffn2_src, 0, None, 2, x2, h_pre=h3, head=(target, nf)
    )
    loss_part = (0.5 / D_MODEL) * jnp.sum(sq)

    dx2, gb2, (dsh3, dsc3, dg3, dn3) = _ffn_bwd("ffn2", dx3, x2, n3, sc3, g3, saved3, hooks2 or _BwdHooks())
    dx1, rest, (dsh2, dsc2, dg2, dn2), dq_norm, dkv_norm, dbias, dsink = _mix_bwd(
        dx2, x1, n2, sc2, g2, q_norm, kv_norm, bias, sinkb, cos2, sin2, saved2, hooks_mix or _BwdHooks()
    )
    rest = rest[:, None]
    deps1 = mix_done(dx1, rest) if mix_done is not None else []
    dx0, gb1, (dsh1, dsc1, dg1, dn1) = _ffn_bwd(
        "ffn1", dx1, x, n1, sc1, g1, saved1, hooks1 or _BwdHooks(), deps=deps1
    )

    dmod = jnp.concatenate([dsh1, dsc1, dg1, dsh2, dsc2, dg2, dsh3, dsc3, dg3], axis=-1)
    drel = _bias_reduce(dbias.reshape(SWA_HEADS, -1), bkt).T
    dsinks = jnp.sum(dsink, axis=(1, 2)).reshape(1, SWA_HEADS)
    small = (dn1, dn2, dn3, dnf, dq_norm, dkv_norm, dsinks, drel)
    return loss_part, dx0, (gb1, gb2, rest), dmod, small


_MESH = pl.DeviceIdType.MESH


def _place():
    return lax.axis_index("x"), lax.axis_index("y"), lax.axis_index("c")


def _other_chips(x, y):
    return [(1 - x, y), (x, 1 - y), (1 - x, 1 - y)]


def _allgather_small(name, blk):
    m_per, n = blk.shape

    def body(x_ref, out_ref, send_sems, recv_sems, local_sem):
        x, y, c = _place()
        me, sibling = (x, y, c), (x, y, 1 - c)
        chips = _other_chips(x, y)

        def rows(px, py, pc):
            return out_ref.at[pl.ds((4 * px + 2 * py + pc) * m_per, m_per), :]

        def copy(k, block, to, src=None):
            return pltpu.make_async_remote_copy(
                src_ref=rows(*block) if src is None else src, dst_ref=rows(*block),
                send_sem=send_sems.at[k], recv_sem=recv_sems.at[k], device_id=to, device_id_type=_MESH,
            )

        mine = pltpu.make_async_copy(x_ref, rows(*me), local_sem)
        mine.start()
        first = [copy(0, me, sibling, src=x_ref)]
        first += [copy(1 + j, me, (*chip, c), src=x_ref) for j, chip in enumerate(chips)]
        for cp in first:
            cp.start()
        passed = [copy(4 + j, (*chip, c), sibling) for j, chip in enumerate(chips)]
        for j, chip in enumerate(chips):
            copy(1 + j, (*chip, c), me).wait_recv()
            passed[j].start()
        copy(0, sibling, me).wait_recv()
        for j, chip in enumerate(chips):
            copy(4 + j, (*chip, 1 - c), me).wait_recv()
        for cp in first + passed:
            cp.wait_send()
        mine.wait()

    return pl.pallas_call(
        body,
        name=name,
        out_shape=_sds((N_DEV * m_per, n), blk.dtype),
        in_specs=[pl.BlockSpec(memory_space=pltpu.VMEM)],
        out_specs=pl.BlockSpec(memory_space=pltpu.VMEM),
        scratch_shapes=[pltpu.SemaphoreType.DMA((7,)), pltpu.SemaphoreType.DMA((7,)), pltpu.SemaphoreType.DMA],
    )(blk)


def _allgather_weights(name, own):
    n = own.shape[0]

    def body(own_ref, g_ref, token, send_sems, recv_sems, local_sem):
        x, y, c = _place()
        sibling = (x, y, 1 - c)
        chips = _other_chips(x, y)
        mine = pltpu.make_async_copy(own_ref, g_ref.at[2 * x + y], local_sem)
        mine.start()
        for j, chip in enumerate(chips):
            for cp in _gather_sends(n, own_ref, g_ref, send_sems.at[j], recv_sems.at[j], x, y, c, chip):
                cp.start()
        for j, chip in enumerate(chips):
            _gather_all(own_ref, g_ref, send_sems.at[j], recv_sems.at[j], chip, c, c).wait_recv()
            for cp in _gather_forwards(n, g_ref, send_sems.at[3 + j], recv_sems.at[3 + j], chip, c, sibling):
                cp.start()
        for j, chip in enumerate(chips):
            _gather_all(own_ref, g_ref, send_sems.at[3 + j], recv_sems.at[3 + j], chip, 1 - c, c).wait_recv()
        for j, chip in enumerate(chips):
            _gather_all(own_ref, g_ref, send_sems.at[j], recv_sems.at[j], chip, c, c).wait_send()
            _gather_all(own_ref, g_ref, send_sems.at[3 + j], recv_sems.at[3 + j], chip, c, c).wait_send()
        mine.wait()
        token[...] = jnp.zeros_like(token)

    return pl.pallas_call(
        body,
        name=name,
        out_shape=[_sds((N_CHIPS,) + own.shape, own.dtype), _sds((8, 128), F32)],
        in_specs=[pl.BlockSpec(memory_space=pl.ANY)],
        out_specs=[pl.BlockSpec(memory_space=pl.ANY), pl.BlockSpec(memory_space=pltpu.VMEM)],
        scratch_shapes=[pltpu.SemaphoreType.DMA((6,)), pltpu.SemaphoreType.DMA((6,)), pltpu.SemaphoreType.DMA],
    )(own)


def _gather_sends(n, own_ref, g_ref, send_sem, recv_sem, x, y, c, chip):
    return [
        pltpu.make_async_remote_copy(
            src_ref=own_ref.at[p, pl.ds(c * HALF, HALF), :], dst_ref=g_ref.at[2 * x + y, p, pl.ds(c * HALF, HALF), :],
            send_sem=send_sem, recv_sem=recv_sem, device_id=(*chip, c), device_id_type=_MESH,
        )
        for p in range(n)
    ]


def _gather_forwards(n, g_ref, send_sem, recv_sem, chip, c, sibling):
    return [
        pltpu.make_async_remote_copy(
            src_ref=g_ref.at[2 * chip[0] + chip[1], p, pl.ds(c * HALF, HALF), :],
            dst_ref=g_ref.at[2 * chip[0] + chip[1], p, pl.ds(c * HALF, HALF), :],
            send_sem=send_sem, recv_sem=recv_sem, device_id=sibling, device_id_type=_MESH,
        )
        for p in range(n)
    ]


def _gather_all(own_ref, g_ref, send_sem, recv_sem, chip, half, c):
    return pltpu.make_async_remote_copy(
        src_ref=own_ref.at[:, pl.ds(c * HALF, HALF), :],
        dst_ref=g_ref.at[2 * chip[0] + chip[1], :, pl.ds(half * HALF, HALF), :],
        send_sem=send_sem, recv_sem=recv_sem, device_id=(*chip, c), device_id_type=_MESH,
    )


_HBM_SPEC = pl.BlockSpec(memory_space=pltpu.HBM)
_SEM_SPEC = pl.BlockSpec(memory_space=pltpu.SEMAPHORE)
_ANY_SPEC = pl.BlockSpec(memory_space=pl.ANY)
_VMEM_SPEC = pl.BlockSpec(memory_space=pltpu.VMEM)
_SPLIT_PARAMS = pltpu.CompilerParams(has_side_effects=pltpu.SideEffectType.DATAFLOW_SIDE_EFFECTING)
_TOKEN = jax.ShapeDtypeStruct((8, 128), F32)


def _in_hbm(a):
    return pltpu.with_memory_space_constraint(a, pltpu.HBM)


class _GatherBehind:
    def __init__(self, tag, own, then=None):
        self.tag, self.n, self.own, self.then = tag, own.shape[0], own, then

    def start(self, after):
        tag, own, n = self.tag, self.own, self.n
        x, y, _ = _place()
        g_init = lax.dynamic_update_slice(
            lax.empty((N_CHIPS,) + own.shape, own.dtype), own[None], (2 * x + y, 0, 0, 0)
        )

        def body(own_ref, g_ref, *rest):
            send_sems, recv_sems, _, _, token = rest[len(after):]
            x, y, c = _place()
            for j, chip in enumerate(_other_chips(x, y)):
                for cp in _gather_sends(n, own_ref, g_ref, send_sems.at[j], recv_sems.at[j], x, y, c, chip):
                    cp.start()
            token[...] = jnp.zeros_like(token)

        self.send1, self.recv1, self.own, self.g, self.token = pl.pallas_call(
            body,
            name=f"{tag}_start",
            out_shape=(
                pltpu.SemaphoreType.DMA((3,)), pltpu.SemaphoreType.DMA((3,)),
                pltpu.HBM(own.shape, own.dtype), pltpu.HBM(g_init.shape, g_init.dtype), _TOKEN,
            ),
            in_specs=(_HBM_SPEC, _HBM_SPEC) + (_ANY_SPEC,) * len(after),
            out_specs=(_SEM_SPEC, _SEM_SPEC, _HBM_SPEC, _HBM_SPEC, _VMEM_SPEC),
            input_output_aliases={0: 2, 1: 3},
            compiler_params=_SPLIT_PARAMS,
        )(_in_hbm(own), _in_hbm(g_init), *after)

    def mid(self, after):
        n = self.n

        def body(own_ref, g_ref, send1, recv1, after_ref, send2, recv2, g_out, token):
            x, y, c = _place()
            sibling = (x, y, 1 - c)
            chips = _other_chips(x, y)
            for j, chip in enumerate(chips):
                _gather_all(own_ref, g_ref, send1.at[j], recv1.at[j], chip, c, c).wait_recv()
                for cp in _gather_forwards(n, g_ref, send2.at[j], recv2.at[j], chip, c, sibling):
                    cp.start()
            for j, chip in enumerate(chips):
                _gather_all(own_ref, g_ref, send1.at[j], recv1.at[j], chip, c, c).wait_send()
            token[...] = jnp.zeros_like(token)

        self.send2, self.recv2, self.g, token = pl.pallas_call(
            body,
            name=f"{self.tag}_mid",
            out_shape=(
                pltpu.SemaphoreType.DMA((3,)), pltpu.SemaphoreType.DMA((3,)),
                pltpu.HBM(self.g.shape, self.g.dtype), _TOKEN,
            ),
            in_specs=(_HBM_SPEC, _HBM_SPEC, _SEM_SPEC, _SEM_SPEC, _ANY_SPEC),
            out_specs=(_SEM_SPEC, _SEM_SPEC, _HBM_SPEC, _VMEM_SPEC),
            input_output_aliases={1: 2},
            compiler_params=_SPLIT_PARAMS,
        )(self.own, self.g, self.send1, self.recv1, after)
        if self.then is None:
            return [token]
        self.then.start([token])
        return [token, self.then.token]

    def get(self, after):
        def body(g_ref, send2, recv2, after_ref, g_out):
            x, y, c = _place()
            for j, chip in enumerate(_other_chips(x, y)):
                slot_in = g_ref.at[2 * chip[0] + chip[1], :, pl.ds((1 - c) * HALF, HALF), :]
                slot_out = g_ref.at[2 * chip[0] + chip[1], :, pl.ds(c * HALF, HALF), :]
                cp = pltpu.make_async_remote_copy(
                    src_ref=slot_out, dst_ref=slot_in, send_sem=send2.at[j], recv_sem=recv2.at[j],
                    device_id=(x, y, 1 - c), device_id_type=_MESH,
                )
                cp.wait_send()
                cp.wait_recv()

        return pl.pallas_call(
            body,
            name=f"{self.tag}_wait",
            out_shape=pltpu.HBM(self.g.shape, self.g.dtype),
            in_specs=(_HBM_SPEC, _SEM_SPEC, _SEM_SPEC, _ANY_SPEC),
            out_specs=_HBM_SPEC,
            input_output_aliases={0: 0},
            compiler_params=_SPLIT_PARAMS,
        )(self.g, self.send2, self.recv2, after)


def _pair_exchange(name, gb):
    def body(gb_ref, land_ref, send_sem, recv_sem):
        x, y, c = _place()
        theirs = gb_ref.at[:, :, pl.ds((1 - c) * HALF, HALF), :]
        cp = pltpu.make_async_remote_copy(
            src_ref=theirs, dst_ref=land_ref, send_sem=send_sem, recv_sem=recv_sem,
            device_id=(x, y, 1 - c), device_id_type=_MESH,
        )
        cp.start()
        cp.wait()

    return pl.pallas_call(
        body,
        name=name,
        out_shape=_sds((N_CHIPS, gb.shape[1], HALF, D_MODEL), gb.dtype),
        in_specs=[pl.BlockSpec(memory_space=pl.ANY)],
        out_specs=pl.BlockSpec(memory_space=pl.ANY),
        scratch_shapes=[pltpu.SemaphoreType.DMA, pltpu.SemaphoreType.DMA],
    )(gb)


def _pair_sum(name, gb, land):
    def body(gb_ref, land_ref, o_ref):
        c = lax.axis_index("c")
        mine = gb_ref[pl.ds(pl.multiple_of(c * HALF, 16), HALF), :]
        o_ref[...] = (mine.astype(F32) + land_ref[...].astype(F32)).astype(o_ref.dtype)

    return pl.pallas_call(
        body,
        name=name,
        grid=(N_CHIPS, gb.shape[1]),
        in_specs=[
            pl.BlockSpec((None, None, F_SHARD, D_MODEL), lambda j, p: (j, p, 0, 0)),
            pl.BlockSpec((None, None, HALF, D_MODEL), lambda j, p: (j, p, 0, 0)),
        ],
        out_specs=pl.BlockSpec((None, None, HALF, D_MODEL), lambda j, p: (j, p, 0, 0)),
        out_shape=_sds(land.shape, BF16),
        compiler_params=_cparams(("parallel", "parallel")),
    )(gb, land)


def _chip_exchange(name, part):
    def body(p_ref, land_ref, send_sems, recv_sems, local_sem):
        x, y, c = _place()
        me = 2 * x + y
        chips = _other_chips(x, y)
        mine = pltpu.make_async_copy(p_ref.at[me], land_ref.at[me], local_sem)
        mine.start()

        def copy(k, chip):
            return pltpu.make_async_remote_copy(
                src_ref=p_ref.at[2 * chip[0] + chip[1]], dst_ref=land_ref.at[me],
                send_sem=send_sems.at[k], recv_sem=recv_sems.at[k], device_id=(*chip, c), device_id_type=_MESH,
            )

        sends = [copy(k, chip) for k, chip in enumerate(chips)]
        for cp in sends:
            cp.start()
        for k, chip in enumerate(chips):
            pltpu.make_async_remote_copy(
                src_ref=p_ref.at[me], dst_ref=land_ref.at[2 * chip[0] + chip[1]],
                send_sem=send_sems.at[k], recv_sem=recv_sems.at[k], device_id=(*chip, c), device_id_type=_MESH,
            ).wait_recv()
        for cp in sends:
            cp.wait_send()
        mine.wait()

    return pl.pallas_call(
        body,
        name=name,
        out_shape=_sds(part.shape, part.dtype),
        in_specs=[pl.BlockSpec(memory_space=pl.ANY)],
        out_specs=pl.BlockSpec(memory_space=pl.ANY),
        scratch_shapes=[pltpu.SemaphoreType.DMA((3,)), pltpu.SemaphoreType.DMA((3,)), pltpu.SemaphoreType.DMA],
    )(part)


def _chip_sum_share(name, land):
    n = land.shape[1]

    def body(l_ref, r_ref, buf, send_sems, recv_sems, local_sems):
        p = pl.program_id(0)
        x, y, c = _place()
        acc = l_ref[0].astype(F32)
        for j in range(1, N_CHIPS):
            acc = acc + l_ref[j].astype(F32)
        buf[p] = acc

        def copies(q):
            mine = r_ref.at[q, pl.ds(c * HALF, HALF), :]
            theirs = r_ref.at[q, pl.ds((1 - c) * HALF, HALF), :]
            local = pltpu.make_async_copy(buf.at[q], mine, local_sems.at[q])
            out = pltpu.make_async_remote_copy(
                src_ref=buf.at[q], dst_ref=mine, send_sem=send_sems.at[q], recv_sem=recv_sems.at[q],
                device_id=(x, y, 1 - c), device_id_type=_MESH,
            )
            arrive = pltpu.make_async_remote_copy(
                src_ref=buf.at[q], dst_ref=theirs, send_sem=send_sems.at[q], recv_sem=recv_sems.at[q],
                device_id=(x, y, 1 - c), device_id_type=_MESH,
            )
            return local, out, arrive

        local, out, _ = copies(p)
        local.start()
        out.start()

        @pl.when(p == n - 1)
        def _():
            for q in range(n):
                local, out, arrive = copies(q)
                arrive.wait_recv()
                out.wait_send()
                local.wait()

    return pl.pallas_call(
        body,
        name=name,
        grid=(n,),
        in_specs=[pl.BlockSpec((N_CHIPS, None, HALF, D_MODEL), lambda p: (0, p, 0, 0))],
        out_specs=pl.BlockSpec(memory_space=pl.ANY),
        out_shape=_sds((n, F_SHARD, D_MODEL), F32),
        scratch_shapes=[
            pltpu.VMEM((n, HALF, D_MODEL), F32),
            pltpu.SemaphoreType.DMA((n,)), pltpu.SemaphoreType.DMA((n,)), pltpu.SemaphoreType.DMA((n,)),
        ],
        compiler_params=_cparams(("arbitrary",)),
    )(land)


class _ReduceBehind:
    def __init__(self, tag, gb, after=()):
        self.tag = tag
        n = gb.shape[1]
        land = lax.empty((N_CHIPS, n, HALF, D_MODEL), gb.dtype)

        def body(gb_ref, land_ref, *rest):
            send_sem, recv_sem, _, _, token = rest[len(after):]
            self._pair_copy(gb_ref, land_ref, send_sem, recv_sem).start()
            token[...] = jnp.zeros_like(token)

        self.send, self.recv, self.gb, self.land, self.token = pl.pallas_call(
            body,
            name=f"grads_pair_start_{tag}",
            out_shape=(
                pltpu.SemaphoreType.DMA((1,)), pltpu.SemaphoreType.DMA((1,)),
                pltpu.HBM(gb.shape, gb.dtype), pltpu.HBM(land.shape, land.dtype), _TOKEN,
            ),
            in_specs=(_HBM_SPEC, _HBM_SPEC) + (_ANY_SPEC,) * len(after),
            out_specs=(_SEM_SPEC, _SEM_SPEC, _HBM_SPEC, _HBM_SPEC, _VMEM_SPEC),
            input_output_aliases={0: 2, 1: 3},
            compiler_params=_SPLIT_PARAMS,
        )(_in_hbm(gb), _in_hbm(land), *after)

    @staticmethod
    def _pair_copy(gb_ref, land_ref, send_sem, recv_sem):
        x, y, c = _place()
        return pltpu.make_async_remote_copy(
            src_ref=gb_ref.at[:, :, pl.ds((1 - c) * HALF, HALF), :], dst_ref=land_ref,
            send_sem=send_sem.at[0], recv_sem=recv_sem.at[0], device_id=(x, y, 1 - c), device_id_type=_MESH,
        )

    @staticmethod
    def _chip_copies(p_ref, land_ref, send_sems, recv_sems):
        x, y, c = _place()
        me = 2 * x + y
        sends, arrivals = [], []
        for k, chip in enumerate(_other_chips(x, y)):
            them = 2 * chip[0] + chip[1]
            sends.append(pltpu.make_async_remote_copy(
                src_ref=p_ref.at[them], dst_ref=land_ref.at[me], send_sem=send_sems.at[k], recv_sem=recv_sems.at[k],
                device_id=(*chip, c), device_id_type=_MESH,
            ))
            arrivals.append(pltpu.make_async_remote_copy(
                src_ref=p_ref.at[me], dst_ref=land_ref.at[them], send_sem=send_sems.at[k], recv_sem=recv_sems.at[k],
                device_id=(*chip, c), device_id_type=_MESH,
            ))
        return sends, arrivals

    def mid(self, after):
        tag = self.tag

        def wait_body(gb_ref, land_ref, send_sem, recv_sem, after_ref, gb_out, land_out):
            cp = self._pair_copy(gb_ref, land_ref, send_sem, recv_sem)
            cp.wait_send()
            cp.wait_recv()

        gb, land = pl.pallas_call(
            wait_body,
            name=f"grads_pair_wait_{tag}",
            out_shape=(pltpu.HBM(self.gb.shape, self.gb.dtype), pltpu.HBM(self.land.shape, self.land.dtype)),
            in_specs=(_HBM_SPEC, _HBM_SPEC, _SEM_SPEC, _SEM_SPEC, _ANY_SPEC),
            out_specs=(_HBM_SPEC, _HBM_SPEC),
            input_output_aliases={0: 0, 1: 1},
            compiler_params=_SPLIT_PARAMS,
        )(self.gb, self.land, self.send, self.recv, after)
        part = _pair_sum(f"grads_pair_sum_{tag}", gb, land)

        x, y, _ = _place()
        start = (2 * x + y, 0, 0, 0)
        own = lax.dynamic_slice(part, start, (1,) + part.shape[1:])
        land2 = lax.dynamic_update_slice(lax.empty(part.shape, part.dtype), own, start)

        def start_body(p_ref, land_ref, send_sems, recv_sems, p_out, land_out, token):
            for cp in self._chip_copies(p_ref, land_ref, send_sems, recv_sems)[0]:
                cp.start()
            token[...] = jnp.zeros_like(token)

        self.send2, self.recv2, self.part, self.land2, token = pl.pallas_call(
            start_body,
            name=f"grads_chip_start_{tag}",
            out_shape=(
                pltpu.SemaphoreType.DMA((3,)), pltpu.SemaphoreType.DMA((3,)),
                pltpu.HBM(part.shape, part.dtype), pltpu.HBM(land2.shape, land2.dtype), _TOKEN,
            ),
            in_specs=(_HBM_SPEC, _HBM_SPEC),
            out_specs=(_SEM_SPEC, _SEM_SPEC, _HBM_SPEC, _HBM_SPEC, _VMEM_SPEC),
            input_output_aliases={0: 2, 1: 3},
            compiler_params=_SPLIT_PARAMS,
        )(_in_hbm(part), _in_hbm(land2))
        return [token]

    def get(self, after):
        n_after = len(after)

        def wait_body(p_ref, land_ref, send_sems, recv_sems, *rest):
            sends, arrivals = self._chip_copies(p_ref, land_ref, send_sems, recv_sems)
            for cp in arrivals:
                cp.wait_recv()
            for cp in sends:
                cp.wait_send()

        _, land2 = pl.pallas_call(
            wait_body,
            name=f"grads_chip_wait_{self.tag}",
            out_shape=(pltpu.HBM(self.part.shape, self.part.dtype), pltpu.HBM(self.land2.shape, self.land2.dtype)),
            in_specs=(_HBM_SPEC, _HBM_SPEC, _SEM_SPEC, _SEM_SPEC) + (_ANY_SPEC,) * n_after,
            out_specs=(_HBM_SPEC, _HBM_SPEC),
            input_output_aliases={0: 0, 1: 1},
            compiler_params=_SPLIT_PARAMS,
        )(self.part, self.land2, self.send2, self.recv2, *after)
        return _chip_sum_share(f"grads_chip_sum_share_{self.tag}", land2)


def _allreduce_small(blk):
    gathered = _allgather_small("allgather_small_grads", blk).reshape(N_DEV, PK_ROWS, 128)

    def body(g_ref, o_ref):
        acc = g_ref[0]
        for k in range(1, N_DEV):
            acc = acc + g_ref[k]
        o_ref[...] = acc

    total = pl.pallas_call(body, name="small_grads_sum", out_shape=_sds((PK_ROWS, 128), F32))(gathered)
    return gathered, total


def _adamw_math(w_, g_, m_, v_):
    m_new = ADAM_B1 * m_ + (1.0 - ADAM_B1) * g_
    v_new = ADAM_B2 * v_ + (1.0 - ADAM_B2) * (g_ * g_)
    m_hat = m_new / (1.0 - ADAM_B1 ** ADAM_STEP)
    v_hat = v_new / (1.0 - ADAM_B2 ** ADAM_STEP)
    delta = -ADAM_LR * (m_hat / (jnp.sqrt(v_hat) + ADAM_EPS) + ADAM_WD * w_)
    return delta, m_new, v_new


def _adamw(name, w, g, m, v):
    rows, cols = w.shape
    tm = rows
    while tm * cols * 4 > (1 << 20) and tm % 16 == 0:
        tm //= 2
    out = _sds(w.shape, F32)
    return _rowwise(name, _adamw_math, [w, g, m, v], [], [out, out, out], [], tm)


def _adamw_small(ws, gs, ms, vs):
    n = len(ws)
    shapes = [w.shape for w in ws]
    as2d = lambda a: a.reshape(1, -1) if a.ndim == 1 else a

    def body(*refs):
        ins, outs = refs[:4 * n], refs[4 * n:]
        for k in range(n):
            res = _adamw_math(*[ins[j * n + k][...] for j in range(4)])
            for j in range(3):
                outs[j * n + k][...] = res[j]

    args = [as2d(a) for group in (ws, gs, ms, vs) for a in group]
    out = pl.pallas_call(
        body, name="adamw_small", out_shape=[_sds(as2d(w).shape, F32) for w in ws] * 3, compiler_params=_cparams()
    )(*args)
    back = [o.reshape(shapes[i % n]) for i, o in enumerate(out)]
    return back[:n], back[n:2 * n], back[2 * n:]


def _pack_small(b_mod_like, n1, n2, n3, nf, qn, kvn, sinks, rel):
    parts = [
        b_mod_like.reshape(PK_MOD, 128),
        n1.reshape(8, 128), n2.reshape(8, 128), n3.reshape(8, 128), nf.reshape(8, 128),
        qn.reshape(2, 128), kvn.reshape(1, 128),
        jnp.pad(sinks.reshape(1, SWA_HEADS), ((0, 0), (0, 128 - SWA_HEADS))),
        rel.reshape(2, 128),
        jnp.zeros((2, 128), F32),
    ]
    return jnp.concatenate(parts, axis=0)


def _unpack_small(p):
    o = PK_MOD
    return (
        p[:o].reshape(1, N_MOD * D_MODEL),
        p[o:o + 8].reshape(1, D_MODEL), p[o + 8:o + 16].reshape(1, D_MODEL),
        p[o + 16:o + 24].reshape(1, D_MODEL), p[o + 24:o + 32].reshape(D_MODEL),
        p[o + 32:o + 34].reshape(1, MLA_Q_RANK), p[o + 34:o + 35].reshape(1, MLA_KV_RANK),
        p[o + 35:o + 36, :SWA_HEADS].reshape(1, SWA_HEADS),
        p[o + 36:o + 38].reshape(NUM_BUCKETS, SWA_HEADS),
    )


def kernel(x, c, w_mod, b_mod, norm_ffn1, ffn1_gate, ffn1_up, ffn1_down, norm_mix, w_in, q_norm, kv_norm, w_uq, w_ukv, sinks, w_o, norm_ffn2, ffn2_gate, ffn2_up, ffn2_down, rel_bias, norm_final, loss_target, m_w_mod, m_b_mod, m_norm_ffn1, m_ffn1_gate, m_ffn1_up, m_ffn1_down, m_norm_mix, m_w_in, m_q_norm, m_kv_norm, m_w_uq, m_w_ukv, m_sinks, m_w_o, m_norm_ffn2, m_ffn2_gate, m_ffn2_up, m_ffn2_down, m_rel_bias, m_norm_final, v_w_mod, v_b_mod, v_norm_ffn1, v_ffn1_gate, v_ffn1_up, v_ffn1_down, v_norm_mix, v_w_in, v_q_norm, v_kv_norm, v_w_uq, v_w_ukv, v_sinks, v_w_o, v_norm_ffn2, v_ffn2_gate, v_ffn2_up, v_ffn2_down, v_rel_bias, v_norm_final):
    ax, ay, ac = _place()
    chip = 2 * ax + ay
    dev = 2 * chip + ac
    n_mod_shard = N_MOD * D_MODEL // N_CHIPS

    def t16(w):
        return w[0].T.astype(BF16)

    rest = jnp.concatenate(
        [
            t16(w_in),
            w_o[0].astype(BF16),
            t16(w_uq).reshape(R_UQ, D_MODEL),
            t16(w_ukv).reshape(R_UKV, D_MODEL),
            jnp.zeros((F_SHARD - O_PAD, D_MODEL), BF16),
        ],
        axis=0,
    )
    own_gu1 = jnp.stack([t16(ffn1_gate), t16(ffn1_up)])
    own_down1 = ffn1_down[0].astype(BF16)[None]
    own_mix = rest[None]
    own_ffn2 = jnp.stack([t16(ffn2_gate), t16(ffn2_up), ffn2_down[0].astype(BF16)])

    (c_act,) = _rowwise(
        "silu_c", lambda v: v * _sigmoid(v), [c.reshape(8, 128)], [], [_sds((8, 128), F32)], [], 8
    )
    c_all = _allgather_small("allgather_c", c_act).reshape(N_DEV, D_MODEL)
    mod_cols = _mm(
        "mod_fwd", "nn", (1, n_mod_shard // 768, 1),
        c_all, (N_DEV, D_MODEL), lambda i, j, k: (0, 0),
        w_mod[0], (D_MODEL, 768), lambda i, j, k: (0, j),
        _sds((N_DEV, n_mod_shard), F32), (N_DEV, 768), lambda i, j, k: (0, j),
        (N_DEV, 768),
    )
    mod_all = _allgather_small("allgather_mod", mod_cols).reshape(N_CHIPS, 2, N_DEV, n_mod_shard)[:, 0]
    mod_all = mod_all.transpose(1, 0, 2).reshape(N_DEV, N_MOD * D_MODEL)
    mod = lax.dynamic_slice_in_dim(mod_all, dev, 1, axis=0) + b_mod

    norms = (norm_ffn1, norm_mix, norm_ffn2, norm_final.reshape(1, D_MODEL))

    ffn2_src = _GatherBehind("gather_ffn2", own_ffn2)
    mix_src = _GatherBehind("gather_mix", own_mix, then=ffn2_src)
    down1_src = _GatherBehind("gather_down1", own_down1, then=mix_src)
    gu1_src = _GatherBehind("gather_gu1", own_gu1, then=down1_src)
    gu1_src.start([mod_all])

    reducing, red = {}, {}

    def begin(tag, gb, after):
        reducing[tag] = _ReduceBehind(tag, gb, after=after)
        return [reducing[tag].token]

    def ffn2_gu_done(gb):
        return begin("ffn2_gu", gb, reducing["ffn2_down"].mid(gb))

    def mix_done(dx1, gb_rest):
        red["ffn2_down"] = reducing["ffn2_down"].get([dx1])
        red["ffn2_gu"] = reducing["ffn2_gu"].get([red["ffn2_down"]])
        return begin("rest", gb_rest, [red["ffn2_gu"]])

    def ffn1_gu_done(gb):
        red["rest"] = reducing["rest"].get([gb])
        begin("ffn1_gu", gb, reducing["ffn1_down"].mid(red["rest"]))
        return reducing["ffn1_gu"].mid(reducing["ffn1_gu"].token)

    loss_part, grad_x, _, dmod, small = _device_step(
        x[0], loss_target[0], mod, gu1_src, down1_src, mix_src, ffn2_src, norms, q_norm, kv_norm, sinks, rel_bias,
        hooks2=_BwdHooks(after_wdown=lambda gb: begin("ffn2_down", gb, ()), after_wgu=ffn2_gu_done),
        hooks_mix=_BwdHooks(after_gate=lambda dz: reducing["ffn2_gu"].mid(dz)),
        mix_done=mix_done,
        hooks1=_BwdHooks(
            after_swiglu=lambda dab3: reducing["rest"].mid(dab3),
            after_wdown=lambda gb: begin("ffn1_down", gb, ()),
            after_wgu=ffn1_gu_done,
        ),
    )
    loss = lax.psum(loss_part, ("x", "y", "c"))

    gathered, total = _allreduce_small(_pack_small(dmod, *small))
    (g_b_mod, g_n1, g_n2, g_n3, g_nf, g_qn, g_kvn, g_sinks, g_rel) = _unpack_small(total)
    dmod_all = gathered[:, :PK_MOD].reshape(N_DEV, N_MOD * D_MODEL)
    dmod_cols = lax.dynamic_slice_in_dim(dmod_all, chip * n_mod_shard, n_mod_shard, axis=1)
    g_w_mod = _mm(
        "mod_bwd", "tn", (1, n_mod_shard // 768, 1),
        c_all, (N_DEV, D_MODEL), lambda i, j, k: (0, 0),
        dmod_cols, (N_DEV, 768), lambda i, j, k: (0, j),
        _sds((D_MODEL, n_mod_shard), F32), (D_MODEL, 768), lambda i, j, k: (0, j),
        (D_MODEL, 768),
    )

    r_rest = red["rest"][0]
    transposed = {"ffn1_gate", "ffn1_up", "ffn2_gate", "ffn2_up", "w_in", "w_uq", "w_ukv"}
    g_big = {
        "ffn2_gate": red["ffn2_gu"][0], "ffn2_up": red["ffn2_gu"][1], "ffn2_down": red["ffn2_down"][0],
        "w_in": r_rest[O_IN:O_IN + R_IN],
        "w_o": r_rest[O_O:O_O + R_O],
        "w_uq": r_rest[O_UQ:O_UQ + R_UQ].reshape(MLA_QK, MLA_Q_RANK),
        "w_ukv": r_rest[O_UKV:O_UKV + R_UKV].reshape(MLA_NOPE + MLA_V, MLA_KV_RANK),
        "w_mod": g_w_mod,
    }
    w_big = {
        "ffn2_gate": (ffn2_gate, m_ffn2_gate, v_ffn2_gate),
        "ffn2_up": (ffn2_up, m_ffn2_up, v_ffn2_up), "ffn2_down": (ffn2_down, m_ffn2_down, v_ffn2_down),
        "w_in": (w_in, m_w_in, v_w_in), "w_o": (w_o, m_w_o, v_w_o), "w_uq": (w_uq, m_w_uq, v_w_uq),
        "w_ukv": (w_ukv, m_w_ukv, v_w_ukv), "w_mod": (w_mod, m_w_mod, v_w_mod),
    }
    grads, deltas, new_m, new_v = {}, {}, {}, {}

    def update(names):
        for nm in names:
            w, m, v = w_big[nm]
            if nm in transposed:
                outs = _adamw(f"adamw_{nm}", w[0].T, g_big[nm], m[0].T, v[0].T)
                g_, d_, m_, v_ = [a.T for a in (g_big[nm],) + tuple(outs)]
            else:
                g_, (d_, m_, v_) = g_big[nm], _adamw(f"adamw_{nm}", w[0], g_big[nm], m[0], v[0])
            grads[nm], deltas[nm], new_m[nm], new_v[nm] = g_[None], d_[None], m_[None], v_[None]

    update(list(w_big))
    red1_down = reducing["ffn1_down"].get([deltas[nm] for nm in w_big])
    red1_gu = reducing["ffn1_gu"].get([red1_down])
    g_big.update({"ffn1_gate": red1_gu[0], "ffn1_up": red1_gu[1], "ffn1_down": red1_down[0]})
    w_big.update({
        "ffn1_gate": (ffn1_gate, m_ffn1_gate, v_ffn1_gate), "ffn1_up": (ffn1_up, m_ffn1_up, v_ffn1_up),
        "ffn1_down": (ffn1_down, m_ffn1_down, v_ffn1_down),
    })
    update(["ffn1_gate", "ffn1_up", "ffn1_down"])

    small_names = ["b_mod", "norm_ffn1", "norm_mix", "norm_ffn2", "norm_final", "q_norm", "kv_norm", "sinks", "rel_bias"]
    w_small = (b_mod, norm_ffn1, norm_mix, norm_ffn2, norm_final, q_norm, kv_norm, sinks, rel_bias)
    m_small = (m_b_mod, m_norm_ffn1, m_norm_mix, m_norm_ffn2, m_norm_final, m_q_norm, m_kv_norm, m_sinks, m_rel_bias)
    v_small = (v_b_mod, v_norm_ffn1, v_norm_mix, v_norm_ffn2, v_norm_final, v_q_norm, v_kv_norm, v_sinks, v_rel_bias)
    g_small = (g_b_mod, g_n1, g_n2, g_n3, g_nf, g_qn, g_kvn, g_sinks, g_rel)
    d_s, m_s, v_s = _adamw_small(w_small, g_small, m_small, v_small)
    for nm, g_, d_, m_, v_ in zip(small_names, g_small, d_s, m_s, v_s):
        grads[nm], deltas[nm], new_m[nm], new_v[nm] = g_, d_, m_, v_

    order = ["w_mod", "b_mod", "norm_ffn1", "ffn1_gate", "ffn1_up", "ffn1_down", "norm_mix", "w_in", "q_norm", "kv_norm",
             "w_uq", "w_ukv", "sinks", "w_o", "norm_ffn2", "ffn2_gate", "ffn2_up", "ffn2_down", "rel_bias", "norm_final"]
    return (loss, grad_x[None], *[grads[n] for n in order], *[deltas[n] for n in order],
            *[new_m[n] for n in order], *[new_v[n] for n in order])
```

```python
import functools
import math

import jax
import jax.numpy as jnp
import numpy as np
from jax import lax
from jax.experimental import pallas as pl
from jax.experimental.pallas import tpu as pltpu

F32, BF16 = jnp.float32, jnp.bfloat16

D_MODEL = 1024
D_FF = 2816
EPS = 1e-6
N_MOD = 9
SWA_HEADS, SWA_KV_HEADS, SWA_HEAD_DIM, WINDOW = 8, 2, 64, 128
SWA_GROUP = SWA_HEADS // SWA_KV_HEADS
MLA_HEADS, MLA_Q_RANK, MLA_KV_RANK, MLA_NOPE, MLA_ROPE, MLA_V = 4, 256, 128, 128, 64, 128
MLA_QK = MLA_NOPE + MLA_ROPE
ROPE_THETA = 10000.0
NUM_BUCKETS, MAX_DISTANCE = 32, 128
D_IN = 1216
N_SWA_COLS = 768
N_MLA_COLS = 512

ADAM_LR, ADAM_B1, ADAM_B2, ADAM_EPS, ADAM_WD, ADAM_STEP = 0.001, 0.9, 0.999, 1e-08, 0.01, 10

N_CHIPS = 4
N_DEV = 8
F_SHARD = D_FF // N_CHIPS
HALF = F_SHARD // 2
R_IN, R_O, R_UQ, R_UKV = 304, 256, 48, 32
O_IN, O_O, O_UQ, O_UKV, O_PAD = 0, 304, 560, 608, 640

PK_MOD = 72
PK_ROWS = 112
VMEM_LIMIT = 56 << 20
ROW_TILE = 2048

_DN = {
    "nn": (((1,), (0,)), ((), ())),
    "nt": (((1,), (1,)), ((), ())),
    "tn": (((0,), (0,)), ((), ())),
}


def _sds(shape, dtype):
    return jax.ShapeDtypeStruct(tuple(shape), dtype)


def _cparams(sem=None):
    kw = {"vmem_limit_bytes": VMEM_LIMIT}
    if sem is not None:
        kw["dimension_semantics"] = sem
    return pltpu.CompilerParams(**kw)


def _dot(a, b, dims):
    return lax.dot_general(a.astype(BF16), b.astype(BF16), _DN[dims], preferred_element_type=F32)


def _mm(name, dims, grid, a, a_blk, a_idx, b, b_blk, b_idx, out, out_blk, out_idx, acc_shape, alias=None, deps=()):
    nk = grid[2]

    def body(*refs):
        a_ref, b_ref = refs[:2]
        o_ref, acc_ref = refs[-2:]
        part = _dot(a_ref[...], b_ref[...], dims)
        if nk == 1:
            o_ref[...] = part.astype(o_ref.dtype)
            return
        k = pl.program_id(2)

        @pl.when(k == 0)
        def _():
            acc_ref[...] = part

        @pl.when((k > 0) & (k < nk - 1))
        def _():
            acc_ref[...] += part

        @pl.when(k == nk - 1)
        def _():
            o_ref[...] = (acc_ref[...] + part).astype(o_ref.dtype)

    in_specs = [pl.BlockSpec(a_blk, a_idx), pl.BlockSpec(b_blk, b_idx)]
    args = [a, b]
    kw = {}
    if alias is not None:
        in_specs.append(pl.BlockSpec(memory_space=pl.ANY))
        args.append(alias)
        kw["input_output_aliases"] = {2: 0}
    in_specs += [pl.BlockSpec(memory_space=pl.ANY)] * len(deps)
    args += list(deps)
    return pl.pallas_call(
        body,
        name=name,
        grid=grid,
        in_specs=in_specs,
        out_specs=pl.BlockSpec(out_blk, out_idx),
        out_shape=out,
        scratch_shapes=[pltpu.VMEM(acc_shape if nk > 1 else (8, 128), F32)],
        compiler_params=_cparams(("parallel", "parallel", "arbitrary")),
        **kw,
    )(*args)


def _rowwise(name, fn, tiled, full, out_tiled, out_red, tm, deps=()):
    rows = tiled[0].shape[-2]
    grid = (rows // tm,)
    n_in = len(tiled) + len(full)
    n_t = len(out_tiled)
    n_dep = len(deps)

    def tspec(shape):
        nd = len(shape)
        return pl.BlockSpec(tuple(shape[:-2]) + (tm, shape[-1]), lambda i, nd=nd: (0,) * (nd - 2) + (i, 0))

    def fspec(shape):
        nd = len(shape)
        return pl.BlockSpec(tuple(shape), lambda i, nd=nd: (0,) * nd)

    def body(*refs):
        outs = fn(*[r[...] for r in refs[:n_in]])
        if not isinstance(outs, (tuple, list)):
            outs = (outs,)
        out_refs = refs[n_in + n_dep:]
        for r, v in zip(out_refs[:n_t], outs[:n_t]):
            r[...] = v.astype(r.dtype)
        red_refs = out_refs[n_t:]
        if red_refs:
            @pl.when(pl.program_id(0) == 0)
            def _():
                for r in red_refs:
                    r[...] = jnp.zeros_like(r)

            for r, v in zip(red_refs, outs[n_t:]):
                r[...] += v.astype(r.dtype)

    res = pl.pallas_call(
        body,
        name=name,
        grid=grid,
        in_specs=[tspec(a.shape) for a in tiled] + [fspec(a.shape) for a in full]
        + [pl.BlockSpec(memory_space=pl.ANY)] * n_dep,
        out_specs=[tspec(o.shape) for o in out_tiled] + [fspec(o.shape) for o in out_red],
        out_shape=list(out_tiled) + list(out_red),
        compiler_params=_cparams(("arbitrary",) if out_red else ("parallel",)),
    )(*tiled, *full, *deps)
    return res


def _sum0(v):
    return jnp.sum(v, axis=0, keepdims=True)


def _sigmoid(v):
    return 1.0 / (1.0 + jnp.exp(-v))


def _rms(x):
    r = lax.rsqrt(jnp.mean(x * x, axis=-1, keepdims=True) + EPS)
    return x * r, r


def _rms_bwd(u, xr, r):
    return r * (u - xr * jnp.mean(u * xr, axis=-1, keepdims=True))


class _Ready:
    def __init__(self, g):
        self.g = g

    def mid(self, after):
        return []

    def get(self, after):
        return self.g


def _normmod(x_, gamma_, sc_, sh_):
    return _rms(x_)[0] * gamma_ * (1.0 + sc_) + sh_


def _row_blocks(tm, size=256):
    size = min(size, tm)
    return [slice(r, r + size) for r in range(0, tm, size)]


def _loss_head(x_, t_, g_):
    xr, r = _rms(x_)
    err = xr * g_ - t_
    dy = err * (1.0 / D_MODEL)
    return _rms_bwd(dy * g_, xr, r), _sum0(err * err), _sum0(dy * xr)


def _ffn_fwd(tag, x, gamma, sh, sc, gate, gu_src, base, down_src, down_idx, mid_after, h_pre=None,
             next_norm=None, head=None):
    s_len = x.shape[0]
    if h_pre is None:
        (h,) = _rowwise(
            f"{tag}_normmod", _normmod, [x], [gamma, sc, sh], [_sds((s_len, D_MODEL), BF16)], [], 256,
            deps=gu_src.mid(mid_after),
        )
        gdeps = []
    else:
        h, gdeps = h_pre, gu_src.mid(mid_after)
    g4 = gu_src.get(h)

    tm = min(ROW_TILE, s_len)
    def gate_up(h_ref, wg_ref, wu_ref, *rest):
        ab_ref, s_ref = rest[-2:]
        wg, wu = wg_ref[...], wu_ref[...]
        for rows in _row_blocks(tm):
            hv = h_ref[rows, :]
            a = _dot(hv, wg, "nt")
            b = _dot(hv, wu, "nt")
            ab_ref[0, rows, :] = a.astype(ab_ref.dtype)
            ab_ref[1, rows, :] = b.astype(ab_ref.dtype)
            s_ref[rows, :] = (a * _sigmoid(a) * b).astype(s_ref.dtype)

    ab4, s3 = pl.pallas_call(
        gate_up,
        name=f"{tag}_gate_up",
        grid=(s_len // tm, N_CHIPS),
        in_specs=[
            pl.BlockSpec((tm, D_MODEL), lambda i, c: (i, 0)),
            pl.BlockSpec((None, None, F_SHARD, D_MODEL), lambda i, c: (c, base, 0, 0)),
            pl.BlockSpec((None, None, F_SHARD, D_MODEL), lambda i, c: (c, base + 1, 0, 0)),
        ] + [pl.BlockSpec(memory_space=pl.ANY)] * len(gdeps),
        out_specs=[
            pl.BlockSpec((None, 2, tm, F_SHARD), lambda i, c: (c, 0, i, 0)),
            pl.BlockSpec((None, tm, F_SHARD), lambda i, c: (c, i, 0)),
        ],
        out_shape=[_sds((N_CHIPS, 2, s_len, F_SHARD), BF16), _sds((N_CHIPS, s_len, F_SHARD), BF16)],
        compiler_params=_cparams(("parallel", "parallel")),
    )(h, g4, g4, *gdeps)
    if down_src is None:
        g_down, ddeps = g4, ()
    else:
        ddeps = down_src.mid(ab4)
        g_down = down_src.get(s3)

    y = _mm(
        f"{tag}_down", "nn", (s_len // tm, 1, N_CHIPS),
        s3, (None, tm, F_SHARD), lambda i, j, k: (k, i, 0),
        g_down, (None, None, F_SHARD, D_MODEL), lambda i, j, k: (k, down_idx, 0, 0),
        _sds((s_len, D_MODEL), F32), (tm, D_MODEL), lambda i, j, k: (i, 0),
        (tm, D_MODEL), deps=ddeps,
    )

    saved = (g4, base, g_down, down_idx, h, ab4, s3, y)
    act = _sds((s_len, D_MODEL), F32)
    if head is not None:
        target, norm_final = head
        vec = _sds((1, D_MODEL), F32)
        out = _rowwise(
            f"{tag}_residual_head", lambda x_, y_, t_, g_, nf_: _loss_head(x_ + 0.5 * g_ * y_, t_, nf_),
            [x, y, target], [gate, norm_final], [act], [vec, vec], 256,
        )
    elif next_norm is not None:
        def residual_norm(x_, y_, g_, gamma_, sc_, sh_):
            x_out = x_ + 0.5 * g_ * y_
            return x_out, _normmod(x_out, gamma_, sc_, sh_)

        out = _rowwise(
            f"{tag}_residual_norm", residual_norm, [x, y], [gate] + list(next_norm),
            [act, _sds((s_len, D_MODEL), BF16)], [], 256,
        )
    else:
        out = _rowwise(f"{tag}_residual", lambda x_, y_, g_: x_ + 0.5 * g_ * y_, [x, y], [gate], [act], [], 256)
    return out, saved


def _normmod_bwd_call(name, dh_parts, x, dxo, gamma, sc, deps=()):
    s_len = x.shape[0]
    n_parts = len(dh_parts)

    def fn(*vals):
        dh = vals[0]
        for extra in vals[1:n_parts]:
            dh = dh + extra
        x_, dxo_, gamma_, sc_ = vals[n_parts:]
        xr, r = _rms(x_)
        n = xr * gamma_
        dn = dh * (1.0 + sc_)
        dx = dxo_ + _rms_bwd(dn * gamma_, xr, r)
        return dx, _sum0(dh * n), _sum0(dh), _sum0(dn * xr)

    vec = _sds((1, D_MODEL), F32)
    return _rowwise(
        name, fn, list(dh_parts) + [x, dxo], [gamma, sc], [_sds((s_len, D_MODEL), F32)], [vec, vec, vec], 256, deps=deps
    )


class _BwdHooks:
    def __init__(self, after_gate=None, after_swiglu=None, after_wdown=None, after_wgu=None, after_dh=None):
        def nothing(_):
            return []

        self.after_gate = after_gate or nothing
        self.after_swiglu = after_swiglu or nothing
        self.after_wdown = after_wdown or nothing
        self.after_wgu = after_wgu or nothing
        self.after_dh = after_dh or nothing


def _ffn_bwd(tag, dxo, x, gamma, sc, gate, saved, hooks, deps=()):
    g4, base, g_down, down_idx, h, ab4, s3, y = saved
    s_len = x.shape[0]
    vec = _sds((1, D_MODEL), F32)

    dy, dgate = _rowwise(
        f"{tag}_bwd_gate", lambda dxo_, y_, g_: (0.5 * g_ * dxo_, _sum0(0.5 * y_ * dxo_)),
        [dxo, y], [gate], [_sds((s_len, D_MODEL), BF16)], [vec], 256, deps=deps,
    )

    tm = min(ROW_TILE, s_len)

    def d_gate_up(dy_ref, wd_ref, ab_ref, o_ref):
        wd = wd_ref[...]
        for rows in _row_blocks(tm):
            ds = _dot(dy_ref[rows, :], wd, "nt")
            a = ab_ref[0, rows, :].astype(F32)
            b = ab_ref[1, rows, :].astype(F32)
            sig = _sigmoid(a)
            o_ref[0, rows, :] = (ds * b * sig * (1.0 + a * (1.0 - sig))).astype(o_ref.dtype)
            o_ref[1, rows, :] = (ds * a * sig).astype(o_ref.dtype)

    ab_spec = pl.BlockSpec((None, 2, tm, F_SHARD), lambda i, c: (c, 0, i, 0))
    dab4 = pl.pallas_call(
        d_gate_up,
        name=f"{tag}_bwd_dgate_up",
        grid=(s_len // tm, N_CHIPS),
        in_specs=[
            pl.BlockSpec((tm, D_MODEL), lambda i, c: (i, 0)),
            pl.BlockSpec((None, None, F_SHARD, D_MODEL), lambda i, c: (c, down_idx, 0, 0)),
            ab_spec,
        ],
        out_specs=ab_spec,
        out_shape=_sds(ab4.shape, BF16),
        compiler_params=_cparams(("parallel", "parallel")),
    )(dy, g_down, ab4)

    tk = tm
    gb_down = _mm(
        f"{tag}_bwd_wdown", "tn", (N_CHIPS, 1, s_len // tk),
        s3, (None, tk, F_SHARD), lambda i, j, k: (i, k, 0),
        dy, (tk, D_MODEL), lambda i, j, k: (k, 0),
        _sds((N_CHIPS, 1, F_SHARD, D_MODEL), BF16), (None, None, F_SHARD, D_MODEL), lambda i, j, k: (i, 0, 0, 0),
        (F_SHARD, D_MODEL), deps=hooks.after_swiglu(dab4),
    )
    gb_gu = _mm(
        f"{tag}_bwd_wgu", "tn", (2 * N_CHIPS, 1, s_len // tk),
        dab4, (None, None, tk, F_SHARD), lambda i, j, k: (i % N_CHIPS, i // N_CHIPS, k, 0),
        h, (tk, D_MODEL), lambda i, j, k: (k, 0),
        _sds((N_CHIPS, 2, F_SHARD, D_MODEL), BF16), (None, None, F_SHARD, D_MODEL),
        lambda i, j, k: (i % N_CHIPS, i // N_CHIPS, 0, 0),
        (F_SHARD, D_MODEL), deps=hooks.after_wdown(gb_down),
    )
    assert base % 2 == 0
    dh_deps = hooks.after_wgu(gb_gu)

    def d_h(dab_ref, w_ref, *rest):
        o_ref, acc_ref = rest[-2:]
        c = pl.program_id(1)
        part = _dot(dab_ref[0], w_ref[0], "nn") + _dot(dab_ref[1], w_ref[1], "nn")

        @pl.when(c == 0)
        def _():
            acc_ref[...] = part

        @pl.when((c > 0) & (c < N_CHIPS - 1))
        def _():
            acc_ref[...] += part

        @pl.when(c == N_CHIPS - 1)
        def _():
            o_ref[...] = acc_ref[...] + part

    dh = pl.pallas_call(
        d_h,
        name=f"{tag}_bwd_dh",
        grid=(s_len // tm, N_CHIPS),
        in_specs=[
            pl.BlockSpec((None, 2, tm, F_SHARD), lambda i, c: (c, 0, i, 0)),
            pl.BlockSpec((None, 2, F_SHARD, D_MODEL), lambda i, c: (c, base // 2, 0, 0)),
        ] + [pl.BlockSpec(memory_space=pl.ANY)] * len(dh_deps),
        out_specs=pl.BlockSpec((tm, D_MODEL), lambda i, c: (i, 0)),
        out_shape=_sds((s_len, D_MODEL), F32),
        scratch_shapes=[pltpu.VMEM((tm, D_MODEL), F32)],
        compiler_params=_cparams(("parallel", "arbitrary")),
    )(dab4, g4, *dh_deps)
    dx, dsc, dsh, dgamma = _normmod_bwd_call(
        f"{tag}_bwd_normmod", [dh], x, dxo, gamma, sc, deps=hooks.after_dh(dh)
    )
    return dx, (gb_down, gb_gu), (dsh, dsc, dgate, dgamma)


def _bucket_map():
    qi = np.arange(WINDOW)[:, None]
    kj = np.arange(2 * WINDOW)[None, :]
    dist = qi + WINDOW - kj
    band = (dist >= 0) & (dist < WINDOW)
    max_exact = NUM_BUCKETS // 2
    n = np.maximum(dist, 0)
    large = []
    for dt in (np.float32, np.float64):
        nf = np.maximum(n, 1).astype(dt)
        val = np.log(nf / dt(max_exact)) / dt(math.log(MAX_DISTANCE / max_exact)) * dt(NUM_BUCKETS - max_exact)
        large.append(np.minimum(max_exact + val.astype(np.int32), NUM_BUCKETS - 1))
    assert np.array_equal(large[0], large[1])
    bucket = np.where(n < max_exact, n, large[0])
    return np.where(band, bucket, -1).astype(np.int32).reshape(1, -1)


def _bias_expand(rel_bias_t, bkt):
    n = bkt.shape[1]

    def body(rb_ref, bkt_ref, o_ref):
        onehot = (lax.broadcasted_iota(jnp.int32, (NUM_BUCKETS, n), 0) == bkt_ref[...]).astype(F32)
        o_ref[...] = lax.dot_general(
            rb_ref[...], onehot, _DN["nn"], precision=lax.Precision.HIGHEST, preferred_element_type=F32
        )

    return pl.pallas_call(
        body, name="bias_expand", out_shape=_sds((SWA_HEADS, n), F32), compiler_params=_cparams()
    )(rel_bias_t, bkt)


def _bias_reduce(dbias_flat, bkt):
    n = bkt.shape[1]

    def body(db_ref, bkt_ref, o_ref):
        onehot = (lax.broadcasted_iota(jnp.int32, (NUM_BUCKETS, n), 0) == bkt_ref[...]).astype(F32)
        o_ref[...] = lax.dot_general(
            db_ref[...], onehot, _DN["nt"], precision=lax.Precision.HIGHEST, preferred_element_type=F32
        )

    return pl.pallas_call(
        body, name="bias_reduce", out_shape=_sds((SWA_HEADS, NUM_BUCKETS), F32), compiler_params=_cparams()
    )(dbias_flat, bkt)


_SWA_SCALE = SWA_HEAD_DIM ** -0.5
_NEG = -1e30


def _swa_in_specs():
    w = WINDOW
    n_q = SWA_HEADS * SWA_HEAD_DIM
    n_kv = 2 * SWA_KV_HEADS * SWA_HEAD_DIM
    prev = lambda n: jnp.maximum(n - 1, 0)
    return [
        pl.BlockSpec((w, n_q), lambda n: (n, 0)),
        pl.BlockSpec((w, n_kv), lambda n: (prev(n), n_q // n_kv)),
        pl.BlockSpec((w, n_kv), lambda n: (n, n_q // n_kv)),
        pl.BlockSpec((SWA_HEADS, w, 2 * w), lambda n: (0, 0, 0)),
        pl.BlockSpec((SWA_HEADS, w, 1), lambda n: (0, 0, 0)),
    ]


def _head_cols(v, first, count):
    hd = SWA_HEAD_DIM
    return jnp.stack([v[:, (first + i) * hd:(first + i + 1) * hd] for i in range(count)])


def _swa_heads(q_ref, kvp_ref, kvc_ref):
    g, w, hd = SWA_GROUP, WINDOW, SWA_HEAD_DIM
    q = q_ref[...].astype(F32)
    kv = jnp.concatenate([kvp_ref[...], kvc_ref[...]], axis=0).astype(F32)
    heads = []
    for j in range(SWA_KV_HEADS):
        qj = _head_cols(q, g * j, g).reshape(g * w, hd)
        heads.append((qj, _head_cols(kv, j, 1)[0], _head_cols(kv, SWA_KV_HEADS + j, 1)[0]))
    return heads


def _swa_scores(q, kk, bias, n):
    g, w = SWA_GROUP, WINDOW
    s = (_dot(q, kk, "nt") * _SWA_SCALE).reshape(g, w, 2 * w) + bias
    qi = lax.broadcasted_iota(jnp.int32, (w, 2 * w), 0)
    kj = lax.broadcasted_iota(jnp.int32, (w, 2 * w), 1)
    dist = qi + w - kj
    valid = ((dist >= 0) & (dist < w) & ((n > 0) | (kj >= w)))[None]
    return jnp.where(valid, s, _NEG), valid


def _swa_fwd(swa2, bias, sinkb):
    s_len = swa2.shape[0]
    g, w, hd = SWA_GROUP, WINDOW, SWA_HEAD_DIM

    def body(q_ref, kvp_ref, kvc_ref, bias_ref, sink_ref, o_ref, lse_ref):
        n = pl.program_id(0)
        outs = []
        for j, (q, kk, vv) in enumerate(_swa_heads(q_ref, kvp_ref, kvc_ref)):
            hs = slice(g * j, g * (j + 1))
            s, valid = _swa_scores(q, kk, bias_ref[hs], n)
            sink = sink_ref[hs]
            m = jnp.maximum(jnp.max(s, axis=-1, keepdims=True), sink)
            p = jnp.where(valid, jnp.exp(s - m), 0.0)
            l = jnp.sum(p, axis=-1, keepdims=True) + jnp.exp(sink - m)
            o = _dot(p.reshape(g * w, 2 * w), vv, "nn").reshape(g, w, hd) / l
            outs += [o[i] for i in range(g)]
            lse_ref[hs] = m + jnp.log(l)
        o_ref[...] = jnp.concatenate(outs, axis=-1).astype(o_ref.dtype)

    n_q = SWA_HEADS * hd
    return pl.pallas_call(
        body,
        name="swa_fwd",
        grid=(s_len // w,),
        in_specs=_swa_in_specs(),
        out_specs=[
            pl.BlockSpec((w, n_q), lambda n: (n, 0)),
            pl.BlockSpec((SWA_HEADS, w, 1), lambda n: (0, n, 0)),
        ],
        out_shape=[_sds((s_len, n_q), BF16), _sds((SWA_HEADS, s_len, 1), F32)],
        compiler_params=_cparams(("parallel",)),
    )(swa2, swa2, swa2, bias, sinkb)


def _swa_bwd(swa2, bias, sinkb, cat, dcat, lse):
    s_len = swa2.shape[0]
    g, w, hd = SWA_GROUP, WINDOW, SWA_HEAD_DIM

    def body(q_ref, kvp_ref, kvc_ref, bias_ref, sink_ref, o_ref, do_ref, lse_ref,
             dq_ref, dkva_ref, dkvb_ref, dbias_ref, dsink_ref):
        n = pl.program_id(0)

        @pl.when(n == 0)
        def _():
            dbias_ref[...] = jnp.zeros_like(dbias_ref)
            dsink_ref[...] = jnp.zeros_like(dsink_ref)

        o_all = o_ref[...].astype(F32)
        do_all = do_ref[...].astype(F32)
        dqs, dks, dvs = [], [], []
        for j, (q, kk, vv) in enumerate(_swa_heads(q_ref, kvp_ref, kvc_ref)):
            hs = slice(g * j, g * (j + 1))
            s, valid = _swa_scores(q, kk, bias_ref[hs], n)
            lse_v = lse_ref[hs]
            p = jnp.where(valid, jnp.exp(s - lse_v), 0.0)
            do = _head_cols(do_all, g * j, g)
            delta = jnp.sum(do * _head_cols(o_all, g * j, g), axis=-1, keepdims=True)
            do2 = do.reshape(g * w, hd)
            dp = _dot(do2, vv, "nt").reshape(g, w, 2 * w)
            ds = p * (dp - delta)
            dbias_ref[hs] += ds
            dsink_ref[hs] -= jnp.exp(sink_ref[hs] - lse_v) * delta
            ds2 = ds.reshape(g * w, 2 * w)
            dq = (_dot(ds2, kk, "nn") * _SWA_SCALE).reshape(g, w, hd)
            dqs += [dq[i] for i in range(g)]
            dks.append(_dot(ds2, q, "tn") * _SWA_SCALE)
            dvs.append(_dot(p.reshape(g * w, 2 * w), do2, "tn"))
        dq_ref[...] = jnp.concatenate(dqs, axis=-1).astype(dq_ref.dtype)
        dkv = jnp.concatenate(dks + dvs, axis=-1)
        dkvb_ref[...] = dkv[:w]
        dkva_ref[...] = dkv[w:]

    n_q = SWA_HEADS * hd
    n_kv = 2 * SWA_KV_HEADS * hd
    q_spec = pl.BlockSpec((w, n_q), lambda n: (n, 0))
    kv_spec = pl.BlockSpec((w, n_kv), lambda n: (n, 0))
    return pl.pallas_call(
        body,
        name="swa_bwd",
        grid=(s_len // w,),
        in_specs=_swa_in_specs() + [q_spec, q_spec, pl.BlockSpec((SWA_HEADS, w, 1), lambda n: (0, n, 0))],
        out_specs=[
            q_spec, kv_spec, kv_spec,
            pl.BlockSpec((SWA_HEADS, w, 2 * w), lambda n: (0, 0, 0)),
            pl.BlockSpec((SWA_HEADS, w, 1), lambda n: (0, 0, 0)),
        ],
        out_shape=[
            _sds((s_len, n_q), BF16), _sds((s_len, n_kv), F32), _sds((s_len, n_kv), F32),
            _sds((SWA_HEADS, w, 2 * w), F32), _sds((SWA_HEADS, w, 1), F32),
        ],
        compiler_params=_cparams(("arbitrary",)),
    )(swa2, swa2, swa2, bias, sinkb, cat, dcat, lse)


_MLA_SCALE = MLA_QK ** -0.5
_MLA_TQ = 256


def _mla_mask(i, tq, n_keys):
    row = i * tq + lax.broadcasted_iota(jnp.int32, (tq, n_keys), 0)
    col = lax.broadcasted_iota(jnp.int32, (tq, n_keys), 1)
    return col <= row


def _per_query_block(i, n_blocks, fn):
    for ii in range(n_blocks):
        pl.when(i == ii)(functools.partial(fn, ii))


def _mla_fwd(q4, k4, v4):
    s_len = q4.shape[1]
    tq = min(_MLA_TQ, s_len)

    def body(q_ref, k_ref, v_ref, o_ref, lse_ref):
        def block(ii):
            n_keys = (ii + 1) * tq
            mask = _mla_mask(ii, tq, n_keys)
            s = jnp.where(mask, _dot(q_ref[...], k_ref[:n_keys, :], "nt") * _MLA_SCALE, _NEG)
            m = jnp.max(s, axis=-1, keepdims=True)
            p = jnp.exp(s - m)
            l = jnp.sum(p, axis=-1, keepdims=True)
            o_ref[...] = (_dot(p, v_ref[:n_keys, :], "nn") / l).astype(o_ref.dtype)
            lse_ref[...] = m + jnp.log(l)

        _per_query_block(pl.program_id(1), s_len // tq, block)

    return pl.pallas_call(
        body,
        name="mla_fwd",
        grid=(MLA_HEADS, s_len // tq),
        in_specs=[
            pl.BlockSpec((None, tq, MLA_QK), lambda h, i: (h, i, 0)),
            pl.BlockSpec((None, s_len, MLA_QK), lambda h, i: (h, 0, 0)),
            pl.BlockSpec((None, s_len, MLA_V), lambda h, i: (h, 0, 0)),
        ],
        out_specs=[
            pl.BlockSpec((tq, MLA_V), lambda h, i: (i, h)),
            pl.BlockSpec((None, tq, 1), lambda h, i: (h, i, 0)),
        ],
        out_shape=[_sds((s_len, MLA_HEADS * MLA_V), BF16), _sds((MLA_HEADS, s_len, 1), F32)],
        compiler_params=_cparams(("parallel", "parallel")),
    )(q4, k4, v4)


def _mla_bwd(q4, k4, v4, cat, dcat, lse):
    s_len = q4.shape[1]
    tq = min(_MLA_TQ, s_len)
    first = SWA_HEADS * SWA_HEAD_DIM // MLA_V

    def body(q_ref, k_ref, v_ref, o_ref, do_ref, lse_ref, dq_ref, dk_ref, dv_ref):
        i = pl.program_id(1)

        @pl.when(i == 0)
        def _():
            dk_ref[...] = jnp.zeros_like(dk_ref)
            dv_ref[...] = jnp.zeros_like(dv_ref)

        def block(ii):
            n_keys = (ii + 1) * tq
            mask = _mla_mask(ii, tq, n_keys)
            q, k, v, do = q_ref[...], k_ref[:n_keys, :], v_ref[:n_keys, :], do_ref[...]
            s = jnp.where(mask, _dot(q, k, "nt") * _MLA_SCALE, _NEG)
            p = jnp.where(mask, jnp.exp(s - lse_ref[...]), 0.0)
            delta = jnp.sum(do.astype(F32) * o_ref[...].astype(F32), axis=-1, keepdims=True)
            ds = (p * (_dot(do, v, "nt") - delta) * _MLA_SCALE).astype(BF16)
            dq_ref[...] = _dot(ds, k, "nn")
            dk_ref[:n_keys, :] += _dot(ds, q, "tn")
            dv_ref[:n_keys, :] += _dot(p, do, "tn")

        _per_query_block(i, s_len // tq, block)

    return pl.pallas_call(
        body,
        name="mla_bwd",
        grid=(MLA_HEADS, s_len // tq),
        in_specs=[
            pl.BlockSpec((None, tq, MLA_QK), lambda h, i: (h, i, 0)),
            pl.BlockSpec((None, s_len, MLA_QK), lambda h, i: (h, 0, 0)),
            pl.BlockSpec((None, s_len, MLA_V), lambda h, i: (h, 0, 0)),
            pl.BlockSpec((tq, MLA_V), lambda h, i: (i, first + h)),
            pl.BlockSpec((tq, MLA_V), lambda h, i: (i, first + h)),
            pl.BlockSpec((None, tq, 1), lambda h, i: (h, i, 0)),
        ],
        out_specs=[
            pl.BlockSpec((None, tq, MLA_QK), lambda h, i: (h, i, 0)),
            pl.BlockSpec((None, s_len, MLA_QK), lambda h, i: (h, 0, 0)),
            pl.BlockSpec((None, s_len, MLA_V), lambda h, i: (h, 0, 0)),
        ],
        out_shape=[
            _sds((MLA_HEADS, s_len, MLA_QK), F32),
            _sds((MLA_HEADS, s_len, MLA_QK), F32),
            _sds((MLA_HEADS, s_len, MLA_V), F32),
        ],
        compiler_params=_cparams(("parallel", "arbitrary")),
    )(q4, k4, v4, cat, dcat, lse)


def _mla_split(pm):
    q_lat = pm[:, :MLA_Q_RANK]
    kv_lat = pm[:, MLA_Q_RANK:MLA_Q_RANK + MLA_KV_RANK]
    kr = pm[:, MLA_Q_RANK + MLA_KV_RANK:MLA_Q_RANK + MLA_KV_RANK + MLA_ROPE]
    kr_sw = pm[:, MLA_Q_RANK + MLA_KV_RANK + MLA_ROPE:]
    return q_lat, kv_lat, kr, kr_sw


def _mla_proj(pm, cos2, sin2, q_norm, kv_norm, wq_ext, wkv):
    s_len = pm.shape[0]

    def fn(pm_, cos_, sin_, qg, kvg, wq, wk):
        q_lat, kv_lat, kr, kr_sw = _mla_split(pm_)
        nq = (_rms(q_lat)[0] * qg).astype(BF16)
        nkv = (_rms(kv_lat)[0] * kvg).astype(BF16)
        k_rot = kr * cos_ + kr_sw * sin_
        qs, ks, vs = [], [], []
        for h in range(MLA_HEADS):
            qe = _dot(nq, wq[h], "nt")
            q_rot = qe[:, MLA_NOPE:MLA_QK] * cos_ + qe[:, MLA_QK:] * sin_
            qs.append(jnp.concatenate([qe[:, :MLA_NOPE], q_rot], axis=-1))
            kve = _dot(nkv, wk[h], "nt")
            ks.append(jnp.concatenate([kve[:, :MLA_NOPE], k_rot], axis=-1))
            vs.append(kve[:, MLA_NOPE:])
        return jnp.stack(qs), jnp.stack(ks), jnp.stack(vs)

    return _rowwise(
        "mla_proj", fn, [pm, cos2, sin2], [q_norm, kv_norm, wq_ext, wkv],
        [_sds((MLA_HEADS, s_len, MLA_QK), BF16), _sds((MLA_HEADS, s_len, MLA_QK), BF16),
         _sds((MLA_HEADS, s_len, MLA_V), BF16)], [], 256,
    )


def _mla_proj_bwd(pm, cos2, sin2, dq4, dk4, dv4, q_norm, kv_norm, wq_ext, wkv):
    s_len = pm.shape[0]

    def fn(pm_, cos_, sin_, dq, dk, dv, qg, kvg, wq, wk):
        q_lat, kv_lat, _, _ = _mla_split(pm_)
        xq, rq = _rms(q_lat)
        xkv, rkv = _rms(kv_lat)
        nq = (xq * qg).astype(BF16)
        nkv = (xkv * kvg).astype(BF16)
        dnq = jnp.zeros_like(q_lat)
        dnkv = jnp.zeros_like(kv_lat)
        dkr = jnp.zeros_like(cos_)
        dwq, dwk = [], []
        for h in range(MLA_HEADS):
            dy = dq[h][:, MLA_NOPE:]
            dqe = jnp.concatenate([dq[h][:, :MLA_NOPE], dy * cos_, dy * sin_], axis=-1).astype(BF16)
            dnq = dnq + _dot(dqe, wq[h], "nn")
            dwq.append(_dot(dqe, nq, "tn"))
            dkve = jnp.concatenate([dk[h][:, :MLA_NOPE], dv[h]], axis=-1).astype(BF16)
            dnkv = dnkv + _dot(dkve, wk[h], "nn")
            dwk.append(_dot(dkve, nkv, "tn"))
            dkr = dkr + dk[h][:, MLA_NOPE:]
        dq_lat = _rms_bwd(dnq * qg, xq, rq)
        dkv_lat = _rms_bwd(dnkv * kvg, xkv, rkv)
        dpm = jnp.concatenate([dq_lat, dkv_lat, dkr * cos_, dkr * sin_], axis=-1)
        return dpm, _sum0(dnq * xq), _sum0(dnkv * xkv), jnp.stack(dwq), jnp.stack(dwk)

    return _rowwise(
        "mla_proj_bwd", fn, [pm, cos2, sin2, dq4, dk4, dv4], [q_norm, kv_norm, wq_ext, wkv],
        [_sds((s_len, N_MLA_COLS), BF16)],
        [_sds((1, MLA_Q_RANK), F32), _sds((1, MLA_KV_RANK), F32),
         _sds(wq_ext.shape, F32), _sds(wkv.shape, F32)], 256,
    )


def _rope_tables(s_len):
    inv = ROPE_THETA ** (-jnp.arange(0, MLA_ROPE, 2, dtype=F32) / MLA_ROPE)
    ang = jnp.arange(s_len, dtype=F32)[:, None] * inv[None, :]
    cos, sin = jnp.cos(ang), jnp.sin(ang)
    return jnp.concatenate([cos, cos], axis=-1), jnp.concatenate([-sin, sin], axis=-1)


def _mix_weights(g_mix):
    rest = g_mix[:, 0]
    w_in_t = rest[:, O_IN:O_IN + R_IN].reshape(D_IN, D_MODEL)
    w_o = rest[:, O_O:O_O + R_O].reshape(D_MODEL, D_MODEL)
    w_uq_t = rest[:, O_UQ:O_UQ + R_UQ].reshape(MLA_HEADS, MLA_QK, MLA_Q_RANK)
    w_ukv_t = rest[:, O_UKV:O_UKV + R_UKV].reshape(MLA_HEADS, MLA_NOPE + MLA_V, MLA_KV_RANK)
    half = MLA_ROPE // 2
    w_swa = w_in_t[:N_SWA_COLS]
    w_mla = jnp.concatenate([w_in_t[N_SWA_COLS:], w_in_t[D_IN - half:], w_in_t[D_IN - MLA_ROPE:D_IN - half]], axis=0)
    wq_ext = jnp.concatenate([w_uq_t, w_uq_t[:, MLA_QK - half:], w_uq_t[:, MLA_NOPE:MLA_QK - half]], axis=1)
    return w_swa, w_mla, w_o, wq_ext, w_ukv_t


def _mix_fwd(x, h, gate, mix_src, q_norm, kv_norm, bias, sinkb, cos2, sin2, next_norm):
    s_len = x.shape[0]
    tm = min(ROW_TILE, s_len)
    deps = mix_src.mid(x)
    weights = _mix_weights(mix_src.get(h))
    w_swa, w_mla, w_o, wq_ext, wkv = weights
    swa2 = _mm(
        "mix_proj_swa", "nt", (s_len // tm, 1, 1),
        h, (tm, D_MODEL), lambda i, j, k: (i, 0),
        w_swa, (N_SWA_COLS, D_MODEL), lambda i, j, k: (0, 0),
        _sds((s_len, N_SWA_COLS), BF16), (tm, N_SWA_COLS), lambda i, j, k: (i, 0),
        (tm, N_SWA_COLS), deps=deps,
    )
    pm = _mm(
        "mix_proj_mla", "nt", (s_len // tm, 1, 1),
        h, (tm, D_MODEL), lambda i, j, k: (i, 0),
        w_mla, (N_MLA_COLS, D_MODEL), lambda i, j, k: (0, 0),
        _sds((s_len, N_MLA_COLS), F32), (tm, N_MLA_COLS), lambda i, j, k: (i, 0),
        (tm, N_MLA_COLS),
    )
    q4, k4, v4 = _mla_proj(pm, cos2, sin2, q_norm, kv_norm, wq_ext, wkv)
    oa, lse_a = _swa_fwd(swa2, bias, sinkb)
    ob, lse_b = _mla_fwd(q4, k4, v4)
    cat = jnp.concatenate([oa, ob], axis=1)
    z = _mm(
        "mix_out", "nn", (s_len // tm, 1, 1),
        cat, (tm, D_MODEL), lambda i, j, k: (i, 0),
        w_o, (D_MODEL, D_MODEL), lambda i, j, k: (0, 0),
        _sds((s_len, D_MODEL), F32), (tm, D_MODEL), lambda i, j, k: (i, 0),
        (tm, D_MODEL),
    )
    def residual_norm(x_, z_, g_, gamma_, sc_, sh_):
        x_out = x_ + g_ * z_
        return x_out, _normmod(x_out, gamma_, sc_, sh_)

    out = _rowwise(
        "mix_residual_norm", residual_norm, [x, z], [gate] + list(next_norm),
        [_sds((s_len, D_MODEL), F32), _sds((s_len, D_MODEL), BF16)], [], 256,
    )
    return out, (weights, h, swa2, pm, q4, k4, v4, cat, lse_a, lse_b, z)


def _mix_bwd(dxo, x, gamma, sc, gate, q_norm, kv_norm, bias, sinkb, cos2, sin2, saved, hooks):
    weights, h, swa2, pm, q4, k4, v4, cat, lse_a, lse_b, z = saved
    w_swa, w_mla, w_o, wq_ext, wkv = weights
    s_len = x.shape[0]
    tm = tk = min(ROW_TILE, s_len)
    vec = _sds((1, D_MODEL), F32)
    n_proj = N_SWA_COLS + N_MLA_COLS
    half_d = D_MODEL // 2

    dz, dgate = _rowwise(
        "mix_bwd_gate", lambda dxo_, z_, g_: (g_ * dxo_, _sum0(z_ * dxo_)),
        [dxo, z], [gate], [_sds((s_len, D_MODEL), BF16)], [vec], 256,
    )
    dw_o = _mm(
        "mix_bwd_wo", "tn", (2, 1, s_len // tk),
        cat, (tk, half_d), lambda i, j, k: (k, i),
        dz, (tk, D_MODEL), lambda i, j, k: (k, 0),
        _sds((D_MODEL, D_MODEL), F32), (half_d, D_MODEL), lambda i, j, k: (i, 0),
        (half_d, D_MODEL),
    )
    dcat = _mm(
        "mix_bwd_dcat", "nt", (s_len // tm, 1, 1),
        dz, (tm, D_MODEL), lambda i, j, k: (i, 0),
        w_o, (D_MODEL, D_MODEL), lambda i, j, k: (0, 0),
        _sds((s_len, D_MODEL), BF16), (tm, D_MODEL), lambda i, j, k: (i, 0),
        (tm, D_MODEL), deps=hooks.after_gate(dz),
    )
    dq_a, dkva, dkvb, dbias, dsink = _swa_bwd(swa2, bias, sinkb, cat, dcat, lse_a)
    dq4, dk4, dv4 = _mla_bwd(q4, k4, v4, cat, dcat, lse_b)
    dpm, dq_norm, dkv_norm, dwq_ext, dwkv = _mla_proj_bwd(pm, cos2, sin2, dq4, dk4, dv4, q_norm, kv_norm, wq_ext, wkv)

    dkv = dkva + jnp.concatenate([dkvb[WINDOW:], jnp.zeros_like(dkvb[:WINDOW])], axis=0)
    dproj = jnp.concatenate([dq_a, dkv.astype(BF16), dpm], axis=1)
    w_proj = jnp.concatenate([w_swa, w_mla], axis=0)

    dw_proj = _mm(
        "mix_bwd_win", "tn", (n_proj // 256, 1, s_len // tk),
        dproj, (tk, 256), lambda i, j, k: (k, i),
        h, (tk, D_MODEL), lambda i, j, k: (k, 0),
        _sds((n_proj, D_MODEL), F32), (256, D_MODEL), lambda i, j, k: (i, 0),
        (256, D_MODEL),
    )
    dw_swa, dw_mla = dw_proj[:N_SWA_COLS], dw_proj[N_SWA_COLS:]
    dh = _mm(
        "mix_bwd_dh", "nn", (s_len // tm, 1, 1),
        dproj, (tm, n_proj), lambda i, j, k: (i, 0),
        w_proj, (n_proj, D_MODEL), lambda i, j, k: (0, 0),
        _sds((s_len, D_MODEL), F32), (tm, D_MODEL), lambda i, j, k: (i, 0),
        (tm, D_MODEL),
    )
    dx, dsc, dsh, dgamma = _normmod_bwd_call("mix_bwd_normmod", [dh], x, dxo, gamma, sc)

    half = MLA_ROPE // 2
    n_mla_in = D_IN - N_SWA_COLS
    dw_mla_base = dw_mla[:n_mla_in]
    dw_mla_base = dw_mla_base.at[n_mla_in - half:].add(dw_mla[n_mla_in:n_mla_in + half])
    dw_mla_base = dw_mla_base.at[n_mla_in - MLA_ROPE:n_mla_in - half].add(dw_mla[n_mla_in + half:])
    dw_in_t = jnp.concatenate([dw_swa, dw_mla_base], axis=0)
    dw_uq_t = dwq_ext[:, :MLA_QK]
    dw_uq_t = dw_uq_t.at[:, MLA_QK - half:].add(dwq_ext[:, MLA_QK:MLA_QK + half])
    dw_uq_t = dw_uq_t.at[:, MLA_NOPE:MLA_QK - half].add(dwq_ext[:, MLA_QK + half:])
    rest = jnp.concatenate(
        [
            dw_in_t.reshape(N_CHIPS, R_IN, D_MODEL),
            dw_o.reshape(N_CHIPS, R_O, D_MODEL),
            dw_uq_t.reshape(N_CHIPS, R_UQ, D_MODEL),
            dwkv.reshape(N_CHIPS, R_UKV, D_MODEL),
            jnp.zeros((N_CHIPS, F_SHARD - O_PAD, D_MODEL), F32),
        ],
        axis=1,
    ).astype(BF16)
    return dx, rest, (dsh, dsc, dgate, dgamma), dq_norm, dkv_norm, dbias, dsink


def _device_step(x, target, mod, gu1_src, down1_src, mix_src, ffn2_src, norms, q_norm, kv_norm, sinks, rel_bias,
                 hooks2=None, hooks_mix=None, mix_done=None, hooks1=None):
    s_len = x.shape[0]
    sh1, sc1, g1, sh2, sc2, g2, sh3, sc3, g3 = [mod[:, i * D_MODEL:(i + 1) * D_MODEL] for i in range(N_MOD)]
    n1, n2, n3, nf = norms
    bkt = jnp.asarray(_bucket_map())
    bias = _bias_expand(rel_bias.T, bkt).reshape(SWA_HEADS, WINDOW, 2 * WINDOW)
    sinkb = jnp.broadcast_to(sinks.reshape(SWA_HEADS, 1, 1), (SWA_HEADS, WINDOW, 1))
    cos2, sin2 = _rope_tables(s_len)

    (x1, h2), saved1 = _ffn_fwd(
        "ffn1", x, n1, sh1, sc1, g1, gu1_src, 0, down1_src, 0, sh1, next_norm=(n2, sc2, sh2)
    )
    (x2, h3), saved2 = _mix_fwd(
        x1, h2, g2, mix_src, q_norm, kv_norm, bias, sinkb, cos2, sin2, next_norm=(n3, sc3, sh3)
    )
    (dx3, sq, dnf), saved3 = _ffn_fwd(
        "ffn2", x2, n3, sh3, sc3, g3, ffn2_src, 0, None, 2, x2, h_pre=h3, head=(target, nf)
    )
    loss_part = (0.5 / D_MODEL) * jnp.sum(sq)

    dx2, gb2, (dsh3, dsc3, dg3, dn3) = _ffn_bwd("ffn2", dx3, x2, n3, sc3, g3, saved3, hooks2 or _BwdHooks())
    dx1, rest, (dsh2, dsc2, dg2, dn2), dq_norm, dkv_norm, dbias, dsink = _mix_bwd(
        dx2, x1, n2, sc2, g2, q_norm, kv_norm, bias, sinkb, cos2, sin2, saved2, hooks_mix or _BwdHooks()
    )
    rest = rest[:, None]
    deps1 = mix_done(dx1, rest) if mix_done is not None else []
    dx0, gb1, (dsh1, dsc1, dg1, dn1) = _ffn_bwd(
        "ffn1", dx1, x, n1, sc1, g1, saved1, hooks1 or _BwdHooks(), deps=deps1
    )

    dmod = jnp.concatenate([dsh1, dsc1, dg1, dsh2, dsc2, dg2, dsh3, dsc3, dg3], axis=-1)
    drel = _bias_reduce(dbias.reshape(SWA_HEADS, -1), bkt).T
    dsinks = jnp.sum(dsink, axis=(1, 2)).reshape(1, SWA_HEADS)
    small = (dn1, dn2, dn3, dnf, dq_norm, dkv_norm, dsinks, drel)
    return loss_part, dx0, (gb1, gb2, rest), dmod, small


_MESH = pl.DeviceIdType.MESH


def _place():
    return lax.axis_index("x"), lax.axis_index("y"), lax.axis_index("c")


def _other_chips(x, y):
    return [(1 - x, y), (x, 1 - y), (1 - x, 1 - y)]


def _allgather_small(name, blk):
    m_per, n = blk.shape

    def body(x_ref, out_ref, send_sems, recv_sems, local_sem):
        x, y, c = _place()
        me, sibling = (x, y, c), (x, y, 1 - c)
        chips = _other_chips(x, y)

        def rows(px, py, pc):
            return out_ref.at[pl.ds((4 * px + 2 * py + pc) * m_per, m_per), :]

        def copy(k, block, to, src=None):
            return pltpu.make_async_remote_copy(
                src_ref=rows(*block) if src is None else src, dst_ref=rows(*block),
                send_sem=send_sems.at[k], recv_sem=recv_sems.at[k], device_id=to, device_id_type=_MESH,
            )

        mine = pltpu.make_async_copy(x_ref, rows(*me), local_sem)
        mine.start()
        first = [copy(0, me, sibling, src=x_ref)]
        first += [copy(1 + j, me, (*chip, c), src=x_ref) for j, chip in enumerate(chips)]
        for cp in first:
            cp.start()
        passed = [copy(4 + j, (*chip, c), sibling) for j, chip in enumerate(chips)]
        for j, chip in enumerate(chips):
            copy(1 + j, (*chip, c), me).wait_recv()
            passed[j].start()
        copy(0, sibling, me).wait_recv()
        for j, chip in enumerate(chips):
            copy(4 + j, (*chip, 1 - c), me).wait_recv()
        for cp in first + passed:
            cp.wait_send()
        mine.wait()

    return pl.pallas_call(
        body,
        name=name,
        out_shape=_sds((N_DEV * m_per, n), blk.dtype),
        in_specs=[pl.BlockSpec(memory_space=pltpu.VMEM)],
        out_specs=pl.BlockSpec(memory_space=pltpu.VMEM),
        scratch_shapes=[pltpu.SemaphoreType.DMA((7,)), pltpu.SemaphoreType.DMA((7,)), pltpu.SemaphoreType.DMA],
    )(blk)


def _allgather_weights(name, own):
    n = own.shape[0]

    def body(own_ref, g_ref, token, send_sems, recv_sems, local_sem):
        x, y, c = _place()
        sibling = (x, y, 1 - c)
        chips = _other_chips(x, y)
        mine = pltpu.make_async_copy(own_ref, g_ref.at[2 * x + y], local_sem)
        mine.start()
        for j, chip in enumerate(chips):
            for cp in _gather_sends(n, own_ref, g_ref, send_sems.at[j], recv_sems.at[j], x, y, c, chip):
                cp.start()
        for j, chip in enumerate(chips):
            _gather_all(own_ref, g_ref, send_sems.at[j], recv_sems.at[j], chip, c, c).wait_recv()
            for cp in _gather_forwards(n, g_ref, send_sems.at[3 + j], recv_sems.at[3 + j], chip, c, sibling):
                cp.start()
        for j, chip in enumerate(chips):
            _gather_all(own_ref, g_ref, send_sems.at[3 + j], recv_sems.at[3 + j], chip, 1 - c, c).wait_recv()
        for j, chip in enumerate(chips):
            _gather_all(own_ref, g_ref, send_sems.at[j], recv_sems.at[j], chip, c, c).wait_send()
            _gather_all(own_ref, g_ref, send_sems.at[3 + j], recv_sems.at[3 + j], chip, c, c).wait_send()
        mine.wait()
        token[...] = jnp.zeros_like(token)

    return pl.pallas_call(
        body,
        name=name,
        out_shape=[_sds((N_CHIPS,) + own.shape, own.dtype), _sds((8, 128), F32)],
        in_specs=[pl.BlockSpec(memory_space=pl.ANY)],
        out_specs=[pl.BlockSpec(memory_space=pl.ANY), pl.BlockSpec(memory_space=pltpu.VMEM)],
        scratch_shapes=[pltpu.SemaphoreType.DMA((6,)), pltpu.SemaphoreType.DMA((6,)), pltpu.SemaphoreType.DMA],
    )(own)


def _gather_sends(n, own_ref, g_ref, send_sem, recv_sem, x, y, c, chip):
    return [
        pltpu.make_async_remote_copy(
            src_ref=own_ref.at[p, pl.ds(c * HALF, HALF), :], dst_ref=g_ref.at[2 * x + y, p, pl.ds(c * HALF, HALF), :],
            send_sem=send_sem, recv_sem=recv_sem, device_id=(*chip, c), device_id_type=_MESH,
        )
        for p in range(n)
    ]


def _gather_forwards(n, g_ref, send_sem, recv_sem, chip, c, sibling):
    return [
        pltpu.make_async_remote_copy(
            src_ref=g_ref.at[2 * chip[0] + chip[1], p, pl.ds(c * HALF, HALF), :],
            dst_ref=g_ref.at[2 * chip[0] + chip[1], p, pl.ds(c * HALF, HALF), :],
            send_sem=send_sem, recv_sem=recv_sem, device_id=sibling, device_id_type=_MESH,
        )
        for p in range(n)
    ]


def _gather_all(own_ref, g_ref, send_sem, recv_sem, chip, half, c):
    return pltpu.make_async_remote_copy(
        src_ref=own_ref.at[:, pl.ds(c * HALF, HALF), :],
        dst_ref=g_ref.at[2 * chip[0] + chip[1], :, pl.ds(half * HALF, HALF), :],
        send_sem=send_sem, recv_sem=recv_sem, device_id=(*chip, c), device_id_type=_MESH,
    )


_HBM_SPEC = pl.BlockSpec(memory_space=pltpu.HBM)
_SEM_SPEC = pl.BlockSpec(memory_space=pltpu.SEMAPHORE)
_ANY_SPEC = pl.BlockSpec(memory_space=pl.ANY)
_VMEM_SPEC = pl.BlockSpec(memory_space=pltpu.VMEM)
_SPLIT_PARAMS = pltpu.CompilerParams(has_side_effects=pltpu.SideEffectType.DATAFLOW_SIDE_EFFECTING)
_TOKEN = jax.ShapeDtypeStruct((8, 128), F32)


def _in_hbm(a):
    return pltpu.with_memory_space_constraint(a, pltpu.HBM)


class _GatherBehind:
    def __init__(self, tag, own, then=None):
        self.tag, self.n, self.own, self.then = tag, own.shape[0], own, then

    def start(self, after):
        tag, own, n = self.tag, self.own, self.n
        x, y, _ = _place()
        g_init = lax.dynamic_update_slice(
            lax.empty((N_CHIPS,) + own.shape, own.dtype), own[None], (2 * x + y, 0, 0, 0)
        )

        def body(own_ref, g_ref, *rest):
            send_sems, recv_sems, _, _, token = rest[len(after):]
            x, y, c = _place()
            for j, chip in enumerate(_other_chips(x, y)):
                for cp in _gather_sends(n, own_ref, g_ref, send_sems.at[j], recv_sems.at[j], x, y, c, chip):
                    cp.start()
            token[...] = jnp.zeros_like(token)

        self.send1, self.recv1, self.own, self.g, self.token = pl.pallas_call(
            body,
            name=f"{tag}_start",
            out_shape=(
                pltpu.SemaphoreType.DMA((3,)), pltpu.SemaphoreType.DMA((3,)),
                pltpu.HBM(own.shape, own.dtype), pltpu.HBM(g_init.shape, g_init.dtype), _TOKEN,
            ),
            in_specs=(_HBM_SPEC, _HBM_SPEC) + (_ANY_SPEC,) * len(after),
            out_specs=(_SEM_SPEC, _SEM_SPEC, _HBM_SPEC, _HBM_SPEC, _VMEM_SPEC),
            input_output_aliases={0: 2, 1: 3},
            compiler_params=_SPLIT_PARAMS,
        )(_in_hbm(own), _in_hbm(g_init), *after)

    def mid(self, after):
        n = self.n

        def body(own_ref, g_ref, send1, recv1, after_ref, send2, recv2, g_out, token):
            x, y, c = _place()
            sibling = (x, y, 1 - c)
            chips = _other_chips(x, y)
            for j, chip in enumerate(chips):
                _gather_all(own_ref, g_ref, send1.at[j], recv1.at[j], chip, c, c).wait_recv()
                for cp in _gather_forwards(n, g_ref, send2.at[j], recv2.at[j], chip, c, sibling):
                    cp.start()
            for j, chip in enumerate(chips):
                _gather_all(own_ref, g_ref, send1.at[j], recv1.at[j], chip, c, c).wait_send()
            token[...] = jnp.zeros_like(token)

        self.send2, self.recv2, self.g, token = pl.pallas_call(
            body,
            name=f"{self.tag}_mid",
            out_shape=(
                pltpu.SemaphoreType.DMA((3,)), pltpu.SemaphoreType.DMA((3,)),
                pltpu.HBM(self.g.shape, self.g.dtype), _TOKEN,
            ),
            in_specs=(_HBM_SPEC, _HBM_SPEC, _SEM_SPEC, _SEM_SPEC, _ANY_SPEC),
            out_specs=(_SEM_SPEC, _SEM_SPEC, _HBM_SPEC, _VMEM_SPEC),
            input_output_aliases={1: 2},
            compiler_params=_SPLIT_PARAMS,
        )(self.own, self.g, self.send1, self.recv1, after)
        if self.then is None:
            return [token]
        self.then.start([token])
        return [token, self.then.token]

    def get(self, after):
        def body(g_ref, send2, recv2, after_ref, g_out):
            x, y, c = _place()
            for j, chip in enumerate(_other_chips(x, y)):
                slot_in = g_ref.at[2 * chip[0] + chip[1], :, pl.ds((1 - c) * HALF, HALF), :]
                slot_out = g_ref.at[2 * chip[0] + chip[1], :, pl.ds(c * HALF, HALF), :]
                cp = pltpu.make_async_remote_copy(
                    src_ref=slot_out, dst_ref=slot_in, send_sem=send2.at[j], recv_sem=recv2.at[j],
                    device_id=(x, y, 1 - c), device_id_type=_MESH,
                )
                cp.wait_send()
                cp.wait_recv()

        return pl.pallas_call(
            body,
            name=f"{self.tag}_wait",
            out_shape=pltpu.HBM(self.g.shape, self.g.dtype),
            in_specs=(_HBM_SPEC, _SEM_SPEC, _SEM_SPEC, _ANY_SPEC),
            out_specs=_HBM_SPEC,
            input_output_aliases={0: 0},
            compiler_params=_SPLIT_PARAMS,
        )(self.g, self.send2, self.recv2, after)


def _pair_exchange(name, gb):
    def body(gb_ref, land_ref, send_sem, recv_sem):
        x, y, c = _place()
        theirs = gb_ref.at[:, :, pl.ds((1 - c) * HALF, HALF), :]
        cp = pltpu.make_async_remote_copy(
            src_ref=theirs, dst_ref=land_ref, send_sem=send_sem, recv_sem=recv_sem,
            device_id=(x, y, 1 - c), device_id_type=_MESH,
        )
        cp.start()
        cp.wait()

    return pl.pallas_call(
        body,
        name=name,
        out_shape=_sds((N_CHIPS, gb.shape[1], HALF, D_MODEL), gb.dtype),
        in_specs=[pl.BlockSpec(memory_space=pl.ANY)],
        out_specs=pl.BlockSpec(memory_space=pl.ANY),
        scratch_shapes=[pltpu.SemaphoreType.DMA, pltpu.SemaphoreType.DMA],
    )(gb)


def _pair_sum(name, gb, land):
    def body(gb_ref, land_ref, o_ref):
        c = lax.axis_index("c")
        mine = gb_ref[pl.ds(pl.multiple_of(c * HALF, 16), HALF), :]
        o_ref[...] = (mine.astype(F32) + land_ref[...].astype(F32)).astype(o_ref.dtype)

    return pl.pallas_call(
        body,
        name=name,
        grid=(N_CHIPS, gb.shape[1]),
        in_specs=[
            pl.BlockSpec((None, None, F_SHARD, D_MODEL), lambda j, p: (j, p, 0, 0)),
            pl.BlockSpec((None, None, HALF, D_MODEL), lambda j, p: (j, p, 0, 0)),
        ],
        out_specs=pl.BlockSpec((None, None, HALF, D_MODEL), lambda j, p: (j, p, 0, 0)),
        out_shape=_sds(land.shape, BF16),
        compiler_params=_cparams(("parallel", "parallel")),
    )(gb, land)


def _chip_exchange(name, part):
    def body(p_ref, land_ref, send_sems, recv_sems, local_sem):
        x, y, c = _place()
        me = 2 * x + y
        chips = _other_chips(x, y)
        mine = pltpu.make_async_copy(p_ref.at[me], land_ref.at[me], local_sem)
        mine.start()

        def copy(k, chip):
            return pltpu.make_async_remote_copy(
                src_ref=p_ref.at[2 * chip[0] + chip[1]], dst_ref=land_ref.at[me],
                send_sem=send_sems.at[k], recv_sem=recv_sems.at[k], device_id=(*chip, c), device_id_type=_MESH,
            )

        sends = [copy(k, chip) for k, chip in enumerate(chips)]
        for cp in sends:
            cp.start()
        for k, chip in enumerate(chips):
            pltpu.make_async_remote_copy(
                src_ref=p_ref.at[me], dst_ref=land_ref.at[2 * chip[0] + chip[1]],
                send_sem=send_sems.at[k], recv_sem=recv_sems.at[k], device_id=(*chip, c), device_id_type=_MESH,
            ).wait_recv()
        for cp in sends:
            cp.wait_send()
        mine.wait()

    return pl.pallas_call(
        body,
        name=name,
        out_shape=_sds(part.shape, part.dtype),
        in_specs=[pl.BlockSpec(memory_space=pl.ANY)],
        out_specs=pl.BlockSpec(memory_space=pl.ANY),
        scratch_shapes=[pltpu.SemaphoreType.DMA((3,)), pltpu.SemaphoreType.DMA((3,)), pltpu.SemaphoreType.DMA],
    )(part)


def _chip_sum_share(name, land):
    n = land.shape[1]

    def body(l_ref, r_ref, buf, send_sems, recv_sems, local_sems):
        p = pl.program_id(0)
        x, y, c = _place()
        acc = l_ref[0].astype(F32)
        for j in range(1, N_CHIPS):
            acc = acc + l_ref[j].astype(F32)
        buf[p] = acc

        def copies(q):
            mine = r_ref.at[q, pl.ds(c * HALF, HALF), :]
            theirs = r_ref.at[q, pl.ds((1 - c) * HALF, HALF), :]
            local = pltpu.make_async_copy(buf.at[q], mine, local_sems.at[q])
            out = pltpu.make_async_remote_copy(
                src_ref=buf.at[q], dst_ref=mine, send_sem=send_sems.at[q], recv_sem=recv_sems.at[q],
                device_id=(x, y, 1 - c), device_id_type=_MESH,
            )
            arrive = pltpu.make_async_remote_copy(
                src_ref=buf.at[q], dst_ref=theirs, send_sem=send_sems.at[q], recv_sem=recv_sems.at[q],
                device_id=(x, y, 1 - c), device_id_type=_MESH,
            )
            return local, out, arrive

        local, out, _ = copies(p)
        local.start()
        out.start()

        @pl.when(p == n - 1)
        def _():
            for q in range(n):
                local, out, arrive = copies(q)
                arrive.wait_recv()
                out.wait_send()
                local.wait()

    return pl.pallas_call(
        body,
        name=name,
        grid=(n,),
        in_specs=[pl.BlockSpec((N_CHIPS, None, HALF, D_MODEL), lambda p: (0, p, 0, 0))],
        out_specs=pl.BlockSpec(memory_space=pl.ANY),
        out_shape=_sds((n, F_SHARD, D_MODEL), F32),
        scratch_shapes=[
            pltpu.VMEM((n, HALF, D_MODEL), F32),
            pltpu.SemaphoreType.DMA((n,)), pltpu.SemaphoreType.DMA((n,)), pltpu.SemaphoreType.DMA((n,)),
        ],
        compiler_params=_cparams(("arbitrary",)),
    )(land)


class _ReduceBehind:
    def __init__(self, tag, gb, after=()):
        self.tag = tag
        n = gb.shape[1]
        land = lax.empty((N_CHIPS, n, HALF, D_MODEL), gb.dtype)

        def body(gb_ref, land_ref, *rest):
            send_sem, recv_sem, _, _, token = rest[len(after):]
            self._pair_copy(gb_ref, land_ref, send_sem, recv_sem).start()
            token[...] = jnp.zeros_like(token)

        self.send, self.recv, self.gb, self.land, self.token = pl.pallas_call(
            body,
            name=f"grads_pair_start_{tag}",
            out_shape=(
                pltpu.SemaphoreType.DMA((1,)), pltpu.SemaphoreType.DMA((1,)),
                pltpu.HBM(gb.shape, gb.dtype), pltpu.HBM(land.shape, land.dtype), _TOKEN,
            ),
            in_specs=(_HBM_SPEC, _HBM_SPEC) + (_ANY_SPEC,) * len(after),
            out_specs=(_SEM_SPEC, _SEM_SPEC, _HBM_SPEC, _HBM_SPEC, _VMEM_SPEC),
            input_output_aliases={0: 2, 1: 3},
            compiler_params=_SPLIT_PARAMS,
        )(_in_hbm(gb), _in_hbm(land), *after)

    @staticmethod
    def _pair_copy(gb_ref, land_ref, send_sem, recv_sem):
        x, y, c = _place()
        return pltpu.make_async_remote_copy(
            src_ref=gb_ref.at[:, :, pl.ds((1 - c) * HALF, HALF), :], dst_ref=land_ref,
            send_sem=send_sem.at[0], recv_sem=recv_sem.at[0], device_id=(x, y, 1 - c), device_id_type=_MESH,
        )

    @staticmethod
    def _chip_copies(p_ref, land_ref, send_sems, recv_sems):
        x, y, c = _place()
        me = 2 * x + y
        sends, arrivals = [], []
        for k, chip in enumerate(_other_chips(x, y)):
            them = 2 * chip[0] + chip[1]
            sends.append(pltpu.make_async_remote_copy(
                src_ref=p_ref.at[them], dst_ref=land_ref.at[me], send_sem=send_sems.at[k], recv_sem=recv_sems.at[k],
                device_id=(*chip, c), device_id_type=_MESH,
            ))
            arrivals.append(pltpu.make_async_remote_copy(
                src_ref=p_ref.at[me], dst_ref=land_ref.at[them], send_sem=send_sems.at[k], recv_sem=recv_sems.at[k],
                device_id=(*chip, c), device_id_type=_MESH,
            ))
        return sends, arrivals

    def mid(self, after):
        tag = self.tag

        def wait_body(gb_ref, land_ref, send_sem, recv_sem, after_ref, gb_out, land_out):
            cp = self._pair_copy(gb_ref, land_ref, send_sem, recv_sem)
            cp.wait_send()
            cp.wait_recv()

        gb, land = pl.pallas_call(
            wait_body,
            name=f"grads_pair_wait_{tag}",
            out_shape=(pltpu.HBM(self.gb.shape, self.gb.dtype), pltpu.HBM(self.land.shape, self.land.dtype)),
            in_specs=(_HBM_SPEC, _HBM_SPEC, _SEM_SPEC, _SEM_SPEC, _ANY_SPEC),
            out_specs=(_HBM_SPEC, _HBM_SPEC),
            input_output_aliases={0: 0, 1: 1},
            compiler_params=_SPLIT_PARAMS,
        )(self.gb, self.land, self.send, self.recv, after)
        part = _pair_sum(f"grads_pair_sum_{tag}", gb, land)

        x, y, _ = _place()
        start = (2 * x + y, 0, 0, 0)
        own = lax.dynamic_slice(part, start, (1,) + part.shape[1:])
        land2 = lax.dynamic_update_slice(lax.empty(part.shape, part.dtype), own, start)

        def start_body(p_ref, land_ref, send_sems, recv_sems, p_out, land_out, token):
            for cp in self._chip_copies(p_ref, land_ref, send_sems, recv_sems)[0]:
                cp.start()
            token[...] = jnp.zeros_like(token)

        self.send2, self.recv2, self.part, self.land2, token = pl.pallas_call(
            start_body,
            name=f"grads_chip_start_{tag}",
            out_shape=(
                pltpu.SemaphoreType.DMA((3,)), pltpu.SemaphoreType.DMA((3,)),
                pltpu.HBM(part.shape, part.dtype), pltpu.HBM(land2.shape, land2.dtype), _TOKEN,
            ),
            in_specs=(_HBM_SPEC, _HBM_SPEC),
            out_specs=(_SEM_SPEC, _SEM_SPEC, _HBM_SPEC, _HBM_SPEC, _VMEM_SPEC),
            input_output_aliases={0: 2, 1: 3},
            compiler_params=_SPLIT_PARAMS,
        )(_in_hbm(part), _in_hbm(land2))
        return [token]

    def get(self, after):
        n_after = len(after)

        def wait_body(p_ref, land_ref, send_sems, recv_sems, *rest):
            sends, arrivals = self._chip_copies(p_ref, land_ref, send_sems, recv_sems)
            for cp in arrivals:
                cp.wait_recv()
            for cp in sends:
                cp.wait_send()

        _, land2 = pl.pallas_call(
            wait_body,
            name=f"grads_chip_wait_{self.tag}",
            out_shape=(pltpu.HBM(self.part.shape, self.part.dtype), pltpu.HBM(self.land2.shape, self.land2.dtype)),
            in_specs=(_HBM_SPEC, _HBM_SPEC, _SEM_SPEC, _SEM_SPEC) + (_ANY_SPEC,) * n_after,
            out_specs=(_HBM_SPEC, _HBM_SPEC),
            input_output_aliases={0: 0, 1: 1},
            compiler_params=_SPLIT_PARAMS,
        )(self.part, self.land2, self.send2, self.recv2, *after)
        return _chip_sum_share(f"grads_chip_sum_share_{self.tag}", land2)


def _allreduce_small(blk):
    gathered = _allgather_small("allgather_small_grads", blk).reshape(N_DEV, PK_ROWS, 128)

    def body(g_ref, o_ref):
        acc = g_ref[0]
        for k in range(1, N_DEV):
            acc = acc + g_ref[k]
        o_ref[...] = acc

    total = pl.pallas_call(body, name="small_grads_sum", out_shape=_sds((PK_ROWS, 128), F32))(gathered)
    return gathered, total


def _adamw_math(w_, g_, m_, v_):
    m_new = ADAM_B1 * m_ + (1.0 - ADAM_B1) * g_
    v_new = ADAM_B2 * v_ + (1.0 - ADAM_B2) * (g_ * g_)
    m_hat = m_new / (1.0 - ADAM_B1 ** ADAM_STEP)
    v_hat = v_new / (1.0 - ADAM_B2 ** ADAM_STEP)
    delta = -ADAM_LR * (m_hat / (jnp.sqrt(v_hat) + ADAM_EPS) + ADAM_WD * w_)
    return delta, m_new, v_new


def _adamw(name, w, g, m, v):
    rows, cols = w.shape
    tm = rows
    while tm * cols * 4 > (1 << 20) and tm % 16 == 0:
        tm //= 2
    out = _sds(w.shape, F32)
    return _rowwise(name, _adamw_math, [w, g, m, v], [], [out, out, out], [], tm)


def _adamw_small(ws, gs, ms, vs):
    n = len(ws)
    shapes = [w.shape for w in ws]
    as2d = lambda a: a.reshape(1, -1) if a.ndim == 1 else a

    def body(*refs):
        ins, outs = refs[:4 * n], refs[4 * n:]
        for k in range(n):
            res = _adamw_math(*[ins[j * n + k][...] for j in range(4)])
            for j in range(3):
                outs[j * n + k][...] = res[j]

    args = [as2d(a) for group in (ws, gs, ms, vs) for a in group]
    out = pl.pallas_call(
        body, name="adamw_small", out_shape=[_sds(as2d(w).shape, F32) for w in ws] * 3, compiler_params=_cparams()
    )(*args)
    back = [o.reshape(shapes[i % n]) for i, o in enumerate(out)]
    return back[:n], back[n:2 * n], back[2 * n:]


def _pack_small(b_mod_like, n1, n2, n3, nf, qn, kvn, sinks, rel):
    parts = [
        b_mod_like.reshape(PK_MOD, 128),
        n1.reshape(8, 128), n2.reshape(8, 128), n3.reshape(8, 128), nf.reshape(8, 128),
        qn.reshape(2, 128), kvn.reshape(1, 128),
        jnp.pad(sinks.reshape(1, SWA_HEADS), ((0, 0), (0, 128 - SWA_HEADS))),
        rel.reshape(2, 128),
        jnp.zeros((2, 128), F32),
    ]
    return jnp.concatenate(parts, axis=0)


def _unpack_small(p):
    o = PK_MOD
    return (
        p[:o].reshape(1, N_MOD * D_MODEL),
        p[o:o + 8].reshape(1, D_MODEL), p[o + 8:o + 16].reshape(1, D_MODEL),
        p[o + 16:o + 24].reshape(1, D_MODEL), p[o + 24:o + 32].reshape(D_MODEL),
        p[o + 32:o + 34].reshape(1, MLA_Q_RANK), p[o + 34:o + 35].reshape(1, MLA_KV_RANK),
        p[o + 35:o + 36, :SWA_HEADS].reshape(1, SWA_HEADS),
        p[o + 36:o + 38].reshape(NUM_BUCKETS, SWA_HEADS),
    )


def kernel(x, c, w_mod, b_mod, norm_ffn1, ffn1_gate, ffn1_up, ffn1_down, norm_mix, w_in, q_norm, kv_norm, w_uq, w_ukv, sinks, w_o, norm_ffn2, ffn2_gate, ffn2_up, ffn2_down, rel_bias, norm_final, loss_target, m_w_mod, m_b_mod, m_norm_ffn1, m_ffn1_gate, m_ffn1_up, m_ffn1_down, m_norm_mix, m_w_in, m_q_norm, m_kv_norm, m_w_uq, m_w_ukv, m_sinks, m_w_o, m_norm_ffn2, m_ffn2_gate, m_ffn2_up, m_ffn2_down, m_rel_bias, m_norm_final, v_w_mod, v_b_mod, v_norm_ffn1, v_ffn1_gate, v_ffn1_up, v_ffn1_down, v_norm_mix, v_w_in, v_q_norm, v_kv_norm, v_w_uq, v_w_ukv, v_sinks, v_w_o, v_norm_ffn2, v_ffn2_gate, v_ffn2_up, v_ffn2_down, v_rel_bias, v_norm_final):
    ax, ay, ac = _place()
    chip = 2 * ax + ay
    dev = 2 * chip + ac
    n_mod_shard = N_MOD * D_MODEL // N_CHIPS

    def t16(w):
        return w[0].T.astype(BF16)

    rest = jnp.concatenate(
        [
            t16(w_in),
            w_o[0].astype(BF16),
            t16(w_uq).reshape(R_UQ, D_MODEL),
            t16(w_ukv).reshape(R_UKV, D_MODEL),
            jnp.zeros((F_SHARD - O_PAD, D_MODEL), BF16),
        ],
        axis=0,
    )
    own_gu1 = jnp.stack([t16(ffn1_gate), t16(ffn1_up)])
    own_down1 = ffn1_down[0].astype(BF16)[None]
    own_mix = rest[None]
    own_ffn2 = jnp.stack([t16(ffn2_gate), t16(ffn2_up), ffn2_down[0].astype(BF16)])

    (c_act,) = _rowwise(
        "silu_c", lambda v: v * _sigmoid(v), [c.reshape(8, 128)], [], [_sds((8, 128), F32)], [], 8
    )
    c_all = _allgather_small("allgather_c", c_act).reshape(N_DEV, D_MODEL)
    mod_cols = _mm(
        "mod_fwd", "nn", (1, n_mod_shard // 768, 1),
        c_all, (N_DEV, D_MODEL), lambda i, j, k: (0, 0),
        w_mod[0], (D_MODEL, 768), lambda i, j, k: (0, j),
        _sds((N_DEV, n_mod_shard), F32), (N_DEV, 768), lambda i, j, k: (0, j),
        (N_DEV, 768),
    )
    mod_all = _allgather_small("allgather_mod", mod_cols).reshape(N_CHIPS, 2, N_DEV, n_mod_shard)[:, 0]
    mod_all = mod_all.transpose(1, 0, 2).reshape(N_DEV, N_MOD * D_MODEL)
    mod = lax.dynamic_slice_in_dim(mod_all, dev, 1, axis=0) + b_mod

    norms = (norm_ffn1, norm_mix, norm_ffn2, norm_final.reshape(1, D_MODEL))

    ffn2_src = _GatherBehind("gather_ffn2", own_ffn2)
    mix_src = _GatherBehind("gather_mix", own_mix, then=ffn2_src)
    down1_src = _GatherBehind("gather_down1", own_down1, then=mix_src)
    gu1_src = _GatherBehind("gather_gu1", own_gu1, then=down1_src)
    gu1_src.start([mod_all])

    reducing, red = {}, {}

    def begin(tag, gb, after):
        reducing[tag] = _ReduceBehind(tag, gb, after=after)
        return [reducing[tag].token]

    def ffn2_gu_done(gb):
        return begin("ffn2_gu", gb, reducing["ffn2_down"].mid(gb))

    def mix_done(dx1, gb_rest):
        red["ffn2_down"] = reducing["ffn2_down"].get([dx1])
        red["ffn2_gu"] = reducing["ffn2_gu"].get([red["ffn2_down"]])
        return begin("rest", gb_rest, [red["ffn2_gu"]])

    def ffn1_gu_done(gb):
        red["rest"] = reducing["rest"].get([gb])
        begin("ffn1_gu", gb, reducing["ffn1_down"].mid(red["rest"]))
        return reducing["ffn1_gu"].mid(reducing["ffn1_gu"].token)

    loss_part, grad_x, _, dmod, small = _device_step(
        x[0], loss_target[0], mod, gu1_src, down1_src, mix_src, ffn2_src, norms, q_norm, kv_norm, sinks, rel_bias,
        hooks2=_BwdHooks(after_wdown=lambda gb: begin("ffn2_down", gb, ()), after_wgu=ffn2_gu_done),
        hooks_mix=_BwdHooks(after_gate=lambda dz: reducing["ffn2_gu"].mid(dz)),
        mix_done=mix_done,
        hooks1=_BwdHooks(
            after_swiglu=lambda dab3: reducing["rest"].mid(dab3),
            after_wdown=lambda gb: begin("ffn1_down", gb, ()),
            after_wgu=ffn1_gu_done,
        ),
    )
    loss = lax.psum(loss_part, ("x", "y", "c"))

    gathered, total = _allreduce_small(_pack_small(dmod, *small))
    (g_b_mod, g_n1, g_n2, g_n3, g_nf, g_qn, g_kvn, g_sinks, g_rel) = _unpack_small(total)
    dmod_all = gathered[:, :PK_MOD].reshape(N_DEV, N_MOD * D_MODEL)
    dmod_cols = lax.dynamic_slice_in_dim(dmod_all, chip * n_mod_shard, n_mod_shard, axis=1)
    g_w_mod = _mm(
        "mod_bwd", "tn", (1, n_mod_shard // 768, 1),
        c_all, (N_DEV, D_MODEL), lambda i, j, k: (0, 0),
        dmod_cols, (N_DEV, 768), lambda i, j, k: (0, j),
        _sds((D_MODEL, n_mod_shard), F32), (D_MODEL, 768), lambda i, j, k: (0, j),
        (D_MODEL, 768),
    )

    r_rest = red["rest"][0]
    transposed = {"ffn1_gate", "ffn1_up", "ffn2_gate", "ffn2_up", "w_in", "w_uq", "w_ukv"}
    g_big = {
        "ffn2_gate": red["ffn2_gu"][0], "ffn2_up": red["ffn2_gu"][1], "ffn2_down": red["ffn2_down"][0],
        "w_in": r_rest[O_IN:O_IN + R_IN],
        "w_o": r_rest[O_O:O_O + R_O],
        "w_uq": r_rest[O_UQ:O_UQ + R_UQ].reshape(MLA_QK, MLA_Q_RANK),
        "w_ukv": r_rest[O_UKV:O_UKV + R_UKV].reshape(MLA_NOPE + MLA_V, MLA_KV_RANK),
        "w_mod": g_w_mod,
    }
    w_big = {
        "ffn2_gate": (ffn2_gate, m_ffn2_gate, v_ffn2_gate),
        "ffn2_up": (ffn2_up, m_ffn2_up, v_ffn2_up), "ffn2_down": (ffn2_down, m_ffn2_down, v_ffn2_down),
        "w_in": (w_in, m_w_in, v_w_in), "w_o": (w_o, m_w_o, v_w_o), "w_uq": (w_uq, m_w_uq, v_w_uq),
        "w_ukv": (w_ukv, m_w_ukv, v_w_ukv), "w_mod": (w_mod, m_w_mod, v_w_mod),
    }
    grads, deltas, new_m, new_v = {}, {}, {}, {}

    def update(names):
        for nm in names:
            w, m, v = w_big[nm]
            if nm in transposed:
                outs = _adamw(f"adamw_{nm}", w[0].T, g_big[nm], m[0].T, v[0].T)
                g_, d_, m_, v_ = [a.T for a in (g_big[nm],) + tuple(outs)]
            else:
                g_, (d_, m_, v_) = g_big[nm], _adamw(f"adamw_{nm}", w[0], g_big[nm], m[0], v[0])
            grads[nm], deltas[nm], new_m[nm], new_v[nm] = g_[None], d_[None], m_[None], v_[None]

    update(list(w_big))
    red1_down = reducing["ffn1_down"].get([deltas[nm] for nm in w_big])
    red1_gu = reducing["ffn1_gu"].get([red1_down])
    g_big.update({"ffn1_gate": red1_gu[0], "ffn1_up": red1_gu[1], "ffn1_down": red1_down[0]})
    w_big.update({
        "ffn1_gate": (ffn1_gate, m_ffn1_gate, v_ffn1_gate), "ffn1_up": (ffn1_up, m_ffn1_up, v_ffn1_up),
        "ffn1_down": (ffn1_down, m_ffn1_down, v_ffn1_down),
    })
    update(["ffn1_gate", "ffn1_up", "ffn1_down"])

    small_names = ["b_mod", "norm_ffn1", "norm_mix", "norm_ffn2", "norm_final", "q_norm", "kv_norm", "sinks", "rel_bias"]
    w_small = (b_mod, norm_ffn1, norm_mix, norm_ffn2, norm_final, q_norm, kv_norm, sinks, rel_bias)
    m_small = (m_b_mod, m_norm_ffn1, m_norm_mix, m_norm_ffn2, m_norm_final, m_q_norm, m_kv_norm, m_sinks, m_rel_bias)
    v_small = (v_b_mod, v_norm_ffn1, v_norm_mix, v_norm_ffn2, v_norm_final, v_q_norm, v_kv_norm, v_sinks, v_rel_bias)
    g_small = (g_b_mod, g_n1, g_n2, g_n3, g_nf, g_qn, g_kvn, g_sinks, g_rel)
    d_s, m_s, v_s = _adamw_small(w_small, g_small, m_small, v_small)
    for nm, g_, d_, m_, v_ in zip(small_names, g_small, d_s, m_s, v_s):
        grads[nm], deltas[nm], new_m[nm], new_v[nm] = g_, d_, m_, v_

    order = ["w_mod", "b_mod", "norm_ffn1", "ffn1_gate", "ffn1_up", "ffn1_down", "norm_mix", "w_in", "q_norm", "kv_norm",
             "w_uq", "w_ukv", "sinks", "w_o", "norm_ffn2", "ffn2_gate", "ffn2_up", "ffn2_down", "rel_bias", "norm_final"]
    return (loss, grad_x[None], *[grads[n] for n in order], *[deltas[n] for n in order],
            *[new_m[n] for n in order], *[new_v[n] for n in order])
```

```python
import functools
import math

import jax
import jax.numpy as jnp
import numpy as np
from jax import lax
from jax.experimental import pallas as pl
from jax.experimental.pallas import tpu as pltpu

F32, BF16 = jnp.float32, jnp.bfloat16

D_MODEL = 1024
D_FF = 2816
EPS = 1e-6
N_MOD = 9
SWA_HEADS, SWA_KV_HEADS, SWA_HEAD_DIM, WINDOW = 8, 2, 64, 128
SWA_GROUP = SWA_HEADS // SWA_KV_HEADS
MLA_HEADS, MLA_Q_RANK, MLA_KV_RANK, MLA_NOPE, MLA_ROPE, MLA_V = 4, 256, 128, 128, 64, 128
MLA_QK = MLA_NOPE + MLA_ROPE
ROPE_THETA = 10000.0
NUM_BUCKETS, MAX_DISTANCE = 32, 128
D_IN = 1216
N_SWA_COLS = 768
N_MLA_COLS = 512

ADAM_LR, ADAM_B1, ADAM_B2, ADAM_EPS, ADAM_WD, ADAM_STEP = 0.001, 0.9, 0.999, 1e-08, 0.01, 10

N_CHIPS = 4
N_DEV = 8
F_SHARD = D_FF // N_CHIPS
HALF = F_SHARD // 2
R_IN, R_O, R_UQ, R_UKV = 304, 256, 48, 32
O_IN, O_O, O_UQ, O_UKV, O_PAD = 0, 304, 560, 608, 640

PK_MOD = 72
PK_ROWS = 112
VMEM_LIMIT = 56 << 20
ROW_TILE = 2048

_DN = {
    "nn": (((1,), (0,)), ((), ())),
    "nt": (((1,), (1,)), ((), ())),
    "tn": (((0,), (0,)), ((), ())),
}


def _sds(shape, dtype):
    return jax.ShapeDtypeStruct(tuple(shape), dtype)


def _cparams(sem=None):
    kw = {"vmem_limit_bytes": VMEM_LIMIT}
    if sem is not None:
        kw["dimension_semantics"] = sem
    return pltpu.CompilerParams(**kw)


def _dot(a, b, dims):
    return lax.dot_general(a.astype(BF16), b.astype(BF16), _DN[dims], preferred_element_type=F32)


def _mm(name, dims, grid, a, a_blk, a_idx, b, b_blk, b_idx, out, out_blk, out_idx, acc_shape, alias=None, deps=()):
    nk = grid[2]

    def body(*refs):
        a_ref, b_ref = refs[:2]
        o_ref, acc_ref = refs[-2:]
        part = _dot(a_ref[...], b_ref[...], dims)
        if nk == 1:
            o_ref[...] = part.astype(o_ref.dtype)
            return
        k = pl.program_id(2)

        @pl.when(k == 0)
        def _():
            acc_ref[...] = part

        @pl.when((k > 0) & (k < nk - 1))
        def _():
            acc_ref[...] += part

        @pl.when(k == nk - 1)
        def _():
            o_ref[...] = (acc_ref[...] + part).astype(o_ref.dtype)

    in_specs = [pl.BlockSpec(a_blk, a_idx), pl.BlockSpec(b_blk, b_idx)]
    args = [a, b]
    kw = {}
    if alias is not None:
        in_specs.append(pl.BlockSpec(memory_space=pl.ANY))
        args.append(alias)
        kw["input_output_aliases"] = {2: 0}
    in_specs += [pl.BlockSpec(memory_space=pl.ANY)] * len(deps)
    args += list(deps)
    return pl.pallas_call(
        body,
        name=name,
        grid=grid,
        in_specs=in_specs,
        out_specs=pl.BlockSpec(out_blk, out_idx),
        out_shape=out,
        scratch_shapes=[pltpu.VMEM(acc_shape if nk > 1 else (8, 128), F32)],
        compiler_params=_cparams(("parallel", "parallel", "arbitrary")),
        **kw,
    )(*args)


def _rowwise(name, fn, tiled, full, out_tiled, out_red, tm, deps=()):
    rows = tiled[0].shape[-2]
    grid = (rows // tm,)
    n_in = len(tiled) + len(full)
    n_t = len(out_tiled)
    n_dep = len(deps)

    def tspec(shape):
        nd = len(shape)
        return pl.BlockSpec(tuple(shape[:-2]) + (tm, shape[-1]), lambda i, nd=nd: (0,) * (nd - 2) + (i, 0))

    def fspec(shape):
        nd = len(shape)
        return pl.BlockSpec(tuple(shape), lambda i, nd=nd: (0,) * nd)

    def body(*refs):
        outs = fn(*[r[...] for r in refs[:n_in]])
        if not isinstance(outs, (tuple, list)):
            outs = (outs,)
        out_refs = refs[n_in + n_dep:]
        for r, v in zip(out_refs[:n_t], outs[:n_t]):
            r[...] = v.astype(r.dtype)
        red_refs = out_refs[n_t:]
        if red_refs:
            @pl.when(pl.program_id(0) == 0)
            def _():
                for r in red_refs:
                    r[...] = jnp.zeros_like(r)

            for r, v in zip(red_refs, outs[n_t:]):
                r[...] += v.astype(r.dtype)

    res = pl.pallas_call(
        body,
        name=name,
        grid=grid,
        in_specs=[tspec(a.shape) for a in tiled] + [fspec(a.shape) for a in full]
        + [pl.BlockSpec(memory_space=pl.ANY)] * n_dep,
        out_specs=[tspec(o.shape) for o in out_tiled] + [fspec(o.shape) for o in out_red],
        out_shape=list(out_tiled) + list(out_red),
        compiler_params=_cparams(("arbitrary",) if out_red else ("parallel",)),
    )(*tiled, *full, *deps)
    return res


def _sum0(v):
    return jnp.sum(v, axis=0, keepdims=True)


def _sigmoid(v):
    return 1.0 / (1.0 + jnp.exp(-v))


def _rms(x):
    r = lax.rsqrt(jnp.mean(x * x, axis=-1, keepdims=True) + EPS)
    return x * r, r


def _rms_bwd(u, xr, r):
    return r * (u - xr * jnp.mean(u * xr, axis=-1, keepdims=True))


class _Ready:
    def __init__(self, g):
        self.g = g

    def mid(self, after):
        return []

    def get(self, after):
        return self.g


def _normmod(x_, gamma_, sc_, sh_):
    return _rms(x_)[0] * gamma_ * (1.0 + sc_) + sh_


def _row_blocks(tm, size=256):
    size = min(size, tm)
    return [slice(r, r + size) for r in range(0, tm, size)]


def _loss_head(x_, t_, g_):
    xr, r = _rms(x_)
    err = xr * g_ - t_
    dy = err * (1.0 / D_MODEL)
    return _rms_bwd(dy * g_, xr, r), _sum0(err * err), _sum0(dy * xr)


def _ffn_fwd(tag, x, gamma, sh, sc, gate, gu_src, base, down_src, down_idx, mid_after, h_pre=None,
             next_norm=None, head=None):
    s_len = x.shape[0]
    if h_pre is None:
        (h,) = _rowwise(
            f"{tag}_normmod", _normmod, [x], [gamma, sc, sh], [_sds((s_len, D_MODEL), BF16)], [], 256,
            deps=gu_src.mid(mid_after),
        )
        gdeps = []
    else:
        h, gdeps = h_pre, gu_src.mid(mid_after)
    g4 = gu_src.get(h)

    tm = min(ROW_TILE, s_len)
    def gate_up(h_ref, wg_ref, wu_ref, *rest):
        ab_ref, s_ref = rest[-2:]
        wg, wu = wg_ref[...], wu_ref[...]
        for rows in _row_blocks(tm):
            hv = h_ref[rows, :]
            a = _dot(hv, wg, "nt")
            b = _dot(hv, wu, "nt")
            ab_ref[0, rows, :] = a.astype(ab_ref.dtype)
            ab_ref[1, rows, :] = b.astype(ab_ref.dtype)
            s_ref[rows, :] = (a * _sigmoid(a) * b).astype(s_ref.dtype)

    ab4, s3 = pl.pallas_call(
        gate_up,
        name=f"{tag}_gate_up",
        grid=(s_len // tm, N_CHIPS),
        in_specs=[
            pl.BlockSpec((tm, D_MODEL), lambda i, c: (i, 0)),
            pl.BlockSpec((None, None, F_SHARD, D_MODEL), lambda i, c: (c, base, 0, 0)),
            pl.BlockSpec((None, None, F_SHARD, D_MODEL), lambda i, c: (c, base + 1, 0, 0)),
        ] + [pl.BlockSpec(memory_space=pl.ANY)] * len(gdeps),
        out_specs=[
            pl.BlockSpec((None, 2, tm, F_SHARD), lambda i, c: (c, 0, i, 0)),
            pl.BlockSpec((None, tm, F_SHARD), lambda i, c: (c, i, 0)),
        ],
        out_shape=[_sds((N_CHIPS, 2, s_len, F_SHARD), BF16), _sds((N_CHIPS, s_len, F_SHARD), BF16)],
        compiler_params=_cparams(("parallel", "parallel")),
    )(h, g4, g4, *gdeps)
    if down_src is None:
        g_down, ddeps = g4, ()
    else:
        ddeps = down_src.mid(ab4)
        g_down = down_src.get(s3)

    y = _mm(
        f"{tag}_down", "nn", (s_len // tm, 1, N_CHIPS),
        s3, (None, tm, F_SHARD), lambda i, j, k: (k, i, 0),
        g_down, (None, None, F_SHARD, D_MODEL), lambda i, j, k: (k, down_idx, 0, 0),
        _sds((s_len, D_MODEL), F32), (tm, D_MODEL), lambda i, j, k: (i, 0),
        (tm, D_MODEL), deps=ddeps,
    )

    saved = (g4, base, g_down, down_idx, h, ab4, s3, y)
    act = _sds((s_len, D_MODEL), F32)
    if head is not None:
        target, norm_final = head
        vec = _sds((1, D_MODEL), F32)
        out = _rowwise(
            f"{tag}_residual_head", lambda x_, y_, t_, g_, nf_: _loss_head(x_ + 0.5 * g_ * y_, t_, nf_),
            [x, y, target], [gate, norm_final], [act], [vec, vec], 256,
        )
    elif next_norm is not None:
        def residual_norm(x_, y_, g_, gamma_, sc_, sh_):
            x_out = x_ + 0.5 * g_ * y_
            return x_out, _normmod(x_out, gamma_, sc_, sh_)

        out = _rowwise(
            f"{tag}_residual_norm", residual_norm, [x, y], [gate] + list(next_norm),
            [act, _sds((s_len, D_MODEL), BF16)], [], 256,
        )
    else:
        out = _rowwise(f"{tag}_residual", lambda x_, y_, g_: x_ + 0.5 * g_ * y_, [x, y], [gate], [act], [], 256)
    return out, saved


def _normmod_bwd_call(name, dh_parts, x, dxo, gamma, sc, deps=()):
    s_len = x.shape[0]
    n_parts = len(dh_parts)

    def fn(*vals):
        dh = vals[0]
        for extra in vals[1:n_parts]:
            dh = dh + extra
        x_, dxo_, gamma_, sc_ = vals[n_parts:]
        xr, r = _rms(x_)
        n = xr * gamma_
        dn = dh * (1.0 + sc_)
        dx = dxo_ + _rms_bwd(dn * gamma_, xr, r)
        return dx, _sum0(dh * n), _sum0(dh), _sum0(dn * xr)

    vec = _sds((1, D_MODEL), F32)
    return _rowwise(
        name, fn, list(dh_parts) + [x, dxo], [gamma, sc], [_sds((s_len, D_MODEL), F32)], [vec, vec, vec], 256, deps=deps
    )


class _BwdHooks:
    def __init__(self, after_gate=None, after_swiglu=None, after_wdown=None, after_wgu=None, after_dh=None):
        def nothing(_):
            return []

        self.after_gate = after_gate or nothing
        self.after_swiglu = after_swiglu or nothing
        self.after_wdown = after_wdown or nothing
        self.after_wgu = after_wgu or nothing
        self.after_dh = after_dh or nothing


def _ffn_bwd(tag, dxo, x, gamma, sc, gate, saved, hooks, deps=()):
    g4, base, g_down, down_idx, h, ab4, s3, y = saved
    s_len = x.shape[0]
    vec = _sds((1, D_MODEL), F32)

    dy, dgate = _rowwise(
        f"{tag}_bwd_gate", lambda dxo_, y_, g_: (0.5 * g_ * dxo_, _sum0(0.5 * y_ * dxo_)),
        [dxo, y], [gate], [_sds((s_len, D_MODEL), BF16)], [vec], 256, deps=deps,
    )

    tm = min(ROW_TILE, s_len)

    def d_gate_up(dy_ref, wd_ref, ab_ref, o_ref):
        wd = wd_ref[...]
        for rows in _row_blocks(tm):
            ds = _dot(dy_ref[rows, :], wd, "nt")
            a = ab_ref[0, rows, :].astype(F32)
            b = ab_ref[1, rows, :].astype(F32)
            sig = _sigmoid(a)
            o_ref[0, rows, :] = (ds * b * sig * (1.0 + a * (1.0 - sig))).astype(o_ref.dtype)
            o_ref[1, rows, :] = (ds * a * sig).astype(o_ref.dtype)

    ab_spec = pl.BlockSpec((None, 2, tm, F_SHARD), lambda i, c: (c, 0, i, 0))
    dab4 = pl.pallas_call(
        d_gate_up,
        name=f"{tag}_bwd_dgate_up",
        grid=(s_len // tm, N_CHIPS),
        in_specs=[
            pl.BlockSpec((tm, D_MODEL), lambda i, c: (i, 0)),
            pl.BlockSpec((None, None, F_SHARD, D_MODEL), lambda i, c: (c, down_idx, 0, 0)),
            ab_spec,
        ],
        out_specs=ab_spec,
        out_shape=_sds(ab4.shape, BF16),
        compiler_params=_cparams(("parallel", "parallel")),
    )(dy, g_down, ab4)

    tk = tm
    gb_down = _mm(
        f"{tag}_bwd_wdown", "tn", (N_CHIPS, 1, s_len // tk),
        s3, (None, tk, F_SHARD), lambda i, j, k: (i, k, 0),
        dy, (tk, D_MODEL), lambda i, j, k: (k, 0),
        _sds((N_CHIPS, 1, F_SHARD, D_MODEL), BF16), (None, None, F_SHARD, D_MODEL), lambda i, j, k: (i, 0, 0, 0),
        (F_SHARD, D_MODEL), deps=hooks.after_swiglu(dab4),
    )
    gb_gu = _mm(
        f"{tag}_bwd_wgu", "tn", (2 * N_CHIPS, 1, s_len // tk),
        dab4, (None, None, tk, F_SHARD), lambda i, j, k: (i % N_CHIPS, i // N_CHIPS, k, 0),
        h, (tk, D_MODEL), lambda i, j, k: (k, 0),
        _sds((N_CHIPS, 2, F_SHARD, D_MODEL), BF16), (None, None, F_SHARD, D_MODEL),
        lambda i, j, k: (i % N_CHIPS, i // N_CHIPS, 0, 0),
        (F_SHARD, D_MODEL), deps=hooks.after_wdown(gb_down),
    )
    assert base % 2 == 0
    dh_deps = hooks.after_wgu(gb_gu)

    def d_h(dab_ref, w_ref, *rest):
        o_ref, acc_ref = rest[-2:]
        c = pl.program_id(1)
        part = _dot(dab_ref[0], w_ref[0], "nn") + _dot(dab_ref[1], w_ref[1], "nn")

        @pl.when(c == 0)
        def _():
            acc_ref[...] = part

        @pl.when((c > 0) & (c < N_CHIPS - 1))
        def _():
            acc_ref[...] += part

        @pl.when(c == N_CHIPS - 1)
        def _():
            o_ref[...] = acc_ref[...] + part

    dh = pl.pallas_call(
        d_h,
        name=f"{tag}_bwd_dh",
        grid=(s_len // tm, N_CHIPS),
        in_specs=[
            pl.BlockSpec((None, 2, tm, F_SHARD), lambda i, c: (c, 0, i, 0)),
            pl.BlockSpec((None, 2, F_SHARD, D_MODEL), lambda i, c: (c, base // 2, 0, 0)),
        ] + [pl.BlockSpec(memory_space=pl.ANY)] * len(dh_deps),
        out_specs=pl.BlockSpec((tm, D_MODEL), lambda i, c: (i, 0)),
        out_shape=_sds((s_len, D_MODEL), F32),
        scratch_shapes=[pltpu.VMEM((tm, D_MODEL), F32)],
        compiler_params=_cparams(("parallel", "arbitrary")),
    )(dab4, g4, *dh_deps)
    dx, dsc, dsh, dgamma = _normmod_bwd_call(
        f"{tag}_bwd_normmod", [dh], x, dxo, gamma, sc, deps=hooks.after_dh(dh)
    )
    return dx, (gb_down, gb_gu), (dsh, dsc, dgate, dgamma)


def _bucket_map():
    qi = np.arange(WINDOW)[:, None]
    kj = np.arange(2 * WINDOW)[None, :]
    dist = qi + WINDOW - kj
    band = (dist >= 0) & (dist < WINDOW)
    max_exact = NUM_BUCKETS // 2
    n = np.maximum(dist, 0)
    large = []
    for dt in (np.float32, np.float64):
        nf = np.maximum(n, 1).astype(dt)
        val = np.log(nf / dt(max_exact)) / dt(math.log(MAX_DISTANCE / max_exact)) * dt(NUM_BUCKETS - max_exact)
        large.append(np.minimum(max_exact + val.astype(np.int32), NUM_BUCKETS - 1))
    assert np.array_equal(large[0], large[1])
    bucket = np.where(n < max_exact, n, large[0])
    return np.where(band, bucket, -1).astype(np.int32).reshape(1, -1)


def _bias_expand(rel_bias_t, bkt):
    n = bkt.shape[1]

    def body(rb_ref, bkt_ref, o_ref):
        onehot = (lax.broadcasted_iota(jnp.int32, (NUM_BUCKETS, n), 0) == bkt_ref[...]).astype(F32)
        o_ref[...] = lax.dot_general(
            rb_ref[...], onehot, _DN["nn"], precision=lax.Precision.HIGHEST, preferred_element_type=F32
        )

    return pl.pallas_call(
        body, name="bias_expand", out_shape=_sds((SWA_HEADS, n), F32), compiler_params=_cparams()
    )(rel_bias_t, bkt)


def _bias_reduce(dbias_flat, bkt):
    n = bkt.shape[1]

    def body(db_ref, bkt_ref, o_ref):
        onehot = (lax.broadcasted_iota(jnp.int32, (NUM_BUCKETS, n), 0) == bkt_ref[...]).astype(F32)
        o_ref[...] = lax.dot_general(
            db_ref[...], onehot, _DN["nt"], precision=lax.Precision.HIGHEST, preferred_element_type=F32
        )

    return pl.pallas_call(
        body, name="bias_reduce", out_shape=_sds((SWA_HEADS, NUM_BUCKETS), F32), compiler_params=_cparams()
    )(dbias_flat, bkt)


_SWA_SCALE = SWA_HEAD_DIM ** -0.5
_NEG = -1e30


def _swa_in_specs():
    w = WINDOW
    n_q = SWA_HEADS * SWA_HEAD_DIM
    n_kv = 2 * SWA_KV_HEADS * SWA_HEAD_DIM
    prev = lambda n: jnp.maximum(n - 1, 0)
    return [
        pl.BlockSpec((w, n_q), lambda n: (n, 0)),
        pl.BlockSpec((w, n_kv), lambda n: (prev(n), n_q // n_kv)),
        pl.BlockSpec((w, n_kv), lambda n: (n, n_q // n_kv)),
        pl.BlockSpec((SWA_HEADS, w, 2 * w), lambda n: (0, 0, 0)),
        pl.BlockSpec((SWA_HEADS, w, 1), lambda n: (0, 0, 0)),
    ]


def _head_cols(v, first, count):
    hd = SWA_HEAD_DIM
    return jnp.stack([v[:, (first + i) * hd:(first + i + 1) * hd] for i in range(count)])


def _swa_heads(q_ref, kvp_ref, kvc_ref):
    g, w, hd = SWA_GROUP, WINDOW, SWA_HEAD_DIM
    q = q_ref[...].astype(F32)
    kv = jnp.concatenate([kvp_ref[...], kvc_ref[...]], axis=0).astype(F32)
    heads = []
    for j in range(SWA_KV_HEADS):
        qj = _head_cols(q, g * j, g).reshape(g * w, hd)
        heads.append((qj, _head_cols(kv, j, 1)[0], _head_cols(kv, SWA_KV_HEADS + j, 1)[0]))
    return heads


def _swa_scores(q, kk, bias, n):
    g, w = SWA_GROUP, WINDOW
    s = (_dot(q, kk, "nt") * _SWA_SCALE).reshape(g, w, 2 * w) + bias
    qi = lax.broadcasted_iota(jnp.int32, (w, 2 * w), 0)
    kj = lax.broadcasted_iota(jnp.int32, (w, 2 * w), 1)
    dist = qi + w - kj
    valid = ((dist >= 0) & (dist < w) & ((n > 0) | (kj >= w)))[None]
    return jnp.where(valid, s, _NEG), valid


def _swa_fwd(swa2, bias, sinkb):
    s_len = swa2.shape[0]
    g, w, hd = SWA_GROUP, WINDOW, SWA_HEAD_DIM

    def body(q_ref, kvp_ref, kvc_ref, bias_ref, sink_ref, o_ref, lse_ref):
        n = pl.program_id(0)
        outs = []
        for j, (q, kk, vv) in enumerate(_swa_heads(q_ref, kvp_ref, kvc_ref)):
            hs = slice(g * j, g * (j + 1))
            s, valid = _swa_scores(q, kk, bias_ref[hs], n)
            sink = sink_ref[hs]
            m = jnp.maximum(jnp.max(s, axis=-1, keepdims=True), sink)
            p = jnp.where(valid, jnp.exp(s - m), 0.0)
            l = jnp.sum(p, axis=-1, keepdims=True) + jnp.exp(sink - m)
            o = _dot(p.reshape(g * w, 2 * w), vv, "nn").reshape(g, w, hd) / l
            outs += [o[i] for i in range(g)]
            lse_ref[hs] = m + jnp.log(l)
        o_ref[...] = jnp.concatenate(outs, axis=-1).astype(o_ref.dtype)

    n_q = SWA_HEADS * hd
    return pl.pallas_call(
        body,
        name="swa_fwd",
        grid=(s_len // w,),
        in_specs=_swa_in_specs(),
        out_specs=[
            pl.BlockSpec((w, n_q), lambda n: (n, 0)),
            pl.BlockSpec((SWA_HEADS, w, 1), lambda n: (0, n, 0)),
        ],
        out_shape=[_sds((s_len, n_q), BF16), _sds((SWA_HEADS, s_len, 1), F32)],
        compiler_params=_cparams(("parallel",)),
    )(swa2, swa2, swa2, bias, sinkb)


def _swa_bwd(swa2, bias, sinkb, cat, dcat, lse):
    s_len = swa2.shape[0]
    g, w, hd = SWA_GROUP, WINDOW, SWA_HEAD_DIM

    def body(q_ref, kvp_ref, kvc_ref, bias_ref, sink_ref, o_ref, do_ref, lse_ref,
             dq_ref, dkva_ref, dkvb_ref, dbias_ref, dsink_ref):
        n = pl.program_id(0)

        @pl.when(n == 0)
        def _():
            dbias_ref[...] = jnp.zeros_like(dbias_ref)
            dsink_ref[...] = jnp.zeros_like(dsink_ref)

        o_all = o_ref[...].astype(F32)
        do_all = do_ref[...].astype(F32)
        dqs, dks, dvs = [], [], []
        for j, (q, kk, vv) in enumerate(_swa_heads(q_ref, kvp_ref, kvc_ref)):
            hs = slice(g * j, g * (j + 1))
            s, valid = _swa_scores(q, kk, bias_ref[hs], n)
            lse_v = lse_ref[hs]
            p = jnp.where(valid, jnp.exp(s - lse_v), 0.0)
            do = _head_cols(do_all, g * j, g)
            delta = jnp.sum(do * _head_cols(o_all, g * j, g), axis=-1, keepdims=True)
            do2 = do.reshape(g * w, hd)
            dp = _dot(do2, vv, "nt").reshape(g, w, 2 * w)
            ds = p * (dp - delta)
            dbias_ref[hs] += ds
            dsink_ref[hs] -= jnp.exp(sink_ref[hs] - lse_v) * delta
            ds2 = ds.reshape(g * w, 2 * w)
            dq = (_dot(ds2, kk, "nn") * _SWA_SCALE).reshape(g, w, hd)
            dqs += [dq[i] for i in range(g)]
            dks.append(_dot(ds2, q, "tn") * _SWA_SCALE)
            dvs.append(_dot(p.reshape(g * w, 2 * w), do2, "tn"))
        dq_ref[...] = jnp.concatenate(dqs, axis=-1).astype(dq_ref.dtype)
        dkv = jnp.concatenate(dks + dvs, axis=-1)
        dkvb_ref[...] = dkv[:w]
        dkva_ref[...] = dkv[w:]

    n_q = SWA_HEADS * hd
    n_kv = 2 * SWA_KV_HEADS * hd
    q_spec = pl.BlockSpec((w, n_q), lambda n: (n, 0))
    kv_spec = pl.BlockSpec((w, n_kv), lambda n: (n, 0))
    return pl.pallas_call(
        body,
        name="swa_bwd",
        grid=(s_len // w,),
        in_specs=_swa_in_specs() + [q_spec, q_spec, pl.BlockSpec((SWA_HEADS, w, 1), lambda n: (0, n, 0))],
        out_specs=[
            q_spec, kv_spec, kv_spec,
            pl.BlockSpec((SWA_HEADS, w, 2 * w), lambda n: (0, 0, 0)),
            pl.BlockSpec((SWA_HEADS, w, 1), lambda n: (0, 0, 0)),
        ],
        out_shape=[
            _sds((s_len, n_q), BF16), _sds((s_len, n_kv), F32), _sds((s_len, n_kv), F32),
            _sds((SWA_HEADS, w, 2 * w), F32), _sds((SWA_HEADS, w, 1), F32),
        ],
        compiler_params=_cparams(("arbitrary",)),
    )(swa2, swa2, swa2, bias, sinkb, cat, dcat, lse)


_MLA_SCALE = MLA_QK ** -0.5
_MLA_TQ = 256


def _mla_mask(i, tq, n_keys):
    row = i * tq + lax.broadcasted_iota(jnp.int32, (tq, n_keys), 0)
    col = lax.broadcasted_iota(jnp.int32, (tq, n_keys), 1)
    return col <= row


def _per_query_block(i, n_blocks, fn):
    for ii in range(n_blocks):
        pl.when(i == ii)(functools.partial(fn, ii))


def _mla_fwd(q4, k4, v4):
    s_len = q4.shape[1]
    tq = min(_MLA_TQ, s_len)

    def body(q_ref, k_ref, v_ref, o_ref, lse_ref):
        def block(ii):
            n_keys = (ii + 1) * tq
            mask = _mla_mask(ii, tq, n_keys)
            s = jnp.where(mask, _dot(q_ref[...], k_ref[:n_keys, :], "nt") * _MLA_SCALE, _NEG)
            m = jnp.max(s, axis=-1, keepdims=True)
            p = jnp.exp(s - m)
            l = jnp.sum(p, axis=-1, keepdims=True)
            o_ref[...] = (_dot(p, v_ref[:n_keys, :], "nn") / l).astype(o_ref.dtype)
            lse_ref[...] = m + jnp.log(l)

        _per_query_block(pl.program_id(1), s_len // tq, block)

    return pl.pallas_call(
        body,
        name="mla_fwd",
        grid=(MLA_HEADS, s_len // tq),
        in_specs=[
            pl.BlockSpec((None, tq, MLA_QK), lambda h, i: (h, i, 0)),
            pl.BlockSpec((None, s_len, MLA_QK), lambda h, i: (h, 0, 0)),
            pl.BlockSpec((None, s_len, MLA_V), lambda h, i: (h, 0, 0)),
        ],
        out_specs=[
            pl.BlockSpec((tq, MLA_V), lambda h, i: (i, h)),
            pl.BlockSpec((None, tq, 1), lambda h, i: (h, i, 0)),
        ],
        out_shape=[_sds((s_len, MLA_HEADS * MLA_V), BF16), _sds((MLA_HEADS, s_len, 1), F32)],
        compiler_params=_cparams(("parallel", "parallel")),
    )(q4, k4, v4)


def _mla_bwd(q4, k4, v4, cat, dcat, lse):
    s_len = q4.shape[1]
    tq = min(_MLA_TQ, s_len)
    first = SWA_HEADS * SWA_HEAD_DIM // MLA_V

    def body(q_ref, k_ref, v_ref, o_ref, do_ref, lse_ref, dq_ref, dk_ref, dv_ref):
        i = pl.program_id(1)

        @pl.when(i == 0)
        def _():
            dk_ref[...] = jnp.zeros_like(dk_ref)
            dv_ref[...] = jnp.zeros_like(dv_ref)

        def block(ii):
            n_keys = (ii + 1) * tq
            mask = _mla_mask(ii, tq, n_keys)
            q, k, v, do = q_ref[...], k_ref[:n_keys, :], v_ref[:n_keys, :], do_ref[...]
            s = jnp.where(mask, _dot(q, k, "nt") * _MLA_SCALE, _NEG)
            p = jnp.where(mask, jnp.exp(s - lse_ref[...]), 0.0)
            delta = jnp.sum(do.astype(F32) * o_ref[...].astype(F32), axis=-1, keepdims=True)
            ds = (p * (_dot(do, v, "nt") - delta) * _MLA_SCALE).astype(BF16)
            dq_ref[...] = _dot(ds, k, "nn")
            dk_ref[:n_keys, :] += _dot(ds, q, "tn")
            dv_ref[:n_keys, :] += _dot(p, do, "tn")

        _per_query_block(i, s_len // tq, block)

    return pl.pallas_call(
        body,
        name="mla_bwd",
        grid=(MLA_HEADS, s_len // tq),
        in_specs=[
            pl.BlockSpec((None, tq, MLA_QK), lambda h, i: (h, i, 0)),
            pl.BlockSpec((None, s_len, MLA_QK), lambda h, i: (h, 0, 0)),
            pl.BlockSpec((None, s_len, MLA_V), lambda h, i: (h, 0, 0)),
            pl.BlockSpec((tq, MLA_V), lambda h, i: (i, first + h)),
            pl.BlockSpec((tq, MLA_V), lambda h, i: (i, first + h)),
            pl.BlockSpec((None, tq, 1), lambda h, i: (h, i, 0)),
        ],
        out_specs=[
            pl.BlockSpec((None, tq, MLA_QK), lambda h, i: (h, i, 0)),
            pl.BlockSpec((None, s_len, MLA_QK), lambda h, i: (h, 0, 0)),
            pl.BlockSpec((None, s_len, MLA_V), lambda h, i: (h, 0, 0)),
        ],
        out_shape=[
            _sds((MLA_HEADS, s_len, MLA_QK), F32),
            _sds((MLA_HEADS, s_len, MLA_QK), F32),
            _sds((MLA_HEADS, s_len, MLA_V), F32),
        ],
        compiler_params=_cparams(("parallel", "arbitrary")),
    )(q4, k4, v4, cat, dcat, lse)


def _mla_split(pm):
    q_lat = pm[:, :MLA_Q_RANK]
    kv_lat = pm[:, MLA_Q_RANK:MLA_Q_RANK + MLA_KV_RANK]
    kr = pm[:, MLA_Q_RANK + MLA_KV_RANK:MLA_Q_RANK + MLA_KV_RANK + MLA_ROPE]
    kr_sw = pm[:, MLA_Q_RANK + MLA_KV_RANK + MLA_ROPE:]
    return q_lat, kv_lat, kr, kr_sw


def _mla_proj(pm, cos2, sin2, q_norm, kv_norm, wq_ext, wkv):
    s_len = pm.shape[0]

    def fn(pm_, cos_, sin_, qg, kvg, wq, wk):
        q_lat, kv_lat, kr, kr_sw = _mla_split(pm_)
        nq = (_rms(q_lat)[0] * qg).astype(BF16)
        nkv = (_rms(kv_lat)[0] * kvg).astype(BF16)
        k_rot = kr * cos_ + kr_sw * sin_
        qs, ks, vs = [], [], []
        for h in range(MLA_HEADS):
            qe = _dot(nq, wq[h], "nt")
            q_rot = qe[:, MLA_NOPE:MLA_QK] * cos_ + qe[:, MLA_QK:] * sin_
            qs.append(jnp.concatenate([qe[:, :MLA_NOPE], q_rot], axis=-1))
            kve = _dot(nkv, wk[h], "nt")
            ks.append(jnp.concatenate([kve[:, :MLA_NOPE], k_rot], axis=-1))
            vs.append(kve[:, MLA_NOPE:])
        return jnp.stack(qs), jnp.stack(ks), jnp.stack(vs)

    return _rowwise(
        "mla_proj", fn, [pm, cos2, sin2], [q_norm, kv_norm, wq_ext, wkv],
        [_sds((MLA_HEADS, s_len, MLA_QK), BF16), _sds((MLA_HEADS, s_len, MLA_QK), BF16),
         _sds((MLA_HEADS, s_len, MLA_V), BF16)], [], 256,
    )


def _mla_proj_bwd(pm, cos2, sin2, dq4, dk4, dv4, q_norm, kv_norm, wq_ext, wkv):
    s_len = pm.shape[0]

    def fn(pm_, cos_, sin_, dq, dk, dv, qg, kvg, wq, wk):
        q_lat, kv_lat, _, _ = _mla_split(pm_)
        xq, rq = _rms(q_lat)
        xkv, rkv = _rms(kv_lat)
        nq = (xq * qg).astype(BF16)
        nkv = (xkv * kvg).astype(BF16)
        dnq = jnp.zeros_like(q_lat)
        dnkv = jnp.zeros_like(kv_lat)
        dkr = jnp.zeros_like(cos_)
        dwq, dwk = [], []
        for h in range(MLA_HEADS):
            dy = dq[h][:, MLA_NOPE:]
            dqe = jnp.concatenate([dq[h][:, :MLA_NOPE], dy * cos_, dy * sin_], axis=-1).astype(BF16)
            dnq = dnq + _dot(dqe, wq[h], "nn")
            dwq.append(_dot(dqe, nq, "tn"))
            dkve = jnp.concatenate([dk[h][:, :MLA_NOPE], dv[h]], axis=-1).astype(BF16)
            dnkv = dnkv + _dot(dkve, wk[h], "nn")
            dwk.append(_dot(dkve, nkv, "tn"))
            dkr = dkr + dk[h][:, MLA_NOPE:]
        dq_lat = _rms_bwd(dnq * qg, xq, rq)
        dkv_lat = _rms_bwd(dnkv * kvg, xkv, rkv)
        dpm = jnp.concatenate([dq_lat, dkv_lat, dkr * cos_, dkr * sin_], axis=-1)
        return dpm, _sum0(dnq * xq), _sum0(dnkv * xkv), jnp.stack(dwq), jnp.stack(dwk)

    return _rowwise(
        "mla_proj_bwd", fn, [pm, cos2, sin2, dq4, dk4, dv4], [q_norm, kv_norm, wq_ext, wkv],
        [_sds((s_len, N_MLA_COLS), BF16)],
        [_sds((1, MLA_Q_RANK), F32), _sds((1, MLA_KV_RANK), F32),
         _sds(wq_ext.shape, F32), _sds(wkv.shape, F32)], 256,
    )


def _rope_tables(s_len):
    inv = ROPE_THETA ** (-jnp.arange(0, MLA_ROPE, 2, dtype=F32) / MLA_ROPE)
    ang = jnp.arange(s_len, dtype=F32)[:, None] * inv[None, :]
    cos, sin = jnp.cos(ang), jnp.sin(ang)
    return jnp.concatenate([cos, cos], axis=-1), jnp.concatenate([-sin, sin], axis=-1)


def _mix_weights(g_mix):
    rest = g_mix[:, 0]
    w_in_t = rest[:, O_IN:O_IN + R_IN].reshape(D_IN, D_MODEL)
    w_o = rest[:, O_O:O_O + R_O].reshape(D_MODEL, D_MODEL)
    w_uq_t = rest[:, O_UQ:O_UQ + R_UQ].reshape(MLA_HEADS, MLA_QK, MLA_Q_RANK)
    w_ukv_t = rest[:, O_UKV:O_UKV + R_UKV].reshape(MLA_HEADS, MLA_NOPE + MLA_V, MLA_KV_RANK)
    half = MLA_ROPE // 2
    w_swa = w_in_t[:N_SWA_COLS]
    w_mla = jnp.concatenate([w_in_t[N_SWA_COLS:], w_in_t[D_IN - half:], w_in_t[D_IN - MLA_ROPE:D_IN - half]], axis=0)
    wq_ext = jnp.concatenate([w_uq_t, w_uq_t[:, MLA_QK - half:], w_uq_t[:, MLA_NOPE:MLA_QK - half]], axis=1)
    return w_swa, w_mla, w_o, wq_ext, w_ukv_t


def _mix_fwd(x, h, gate, mix_src, q_norm, kv_norm, bias, sinkb, cos2, sin2, next_norm):
    s_len = x.shape[0]
    tm = min(ROW_TILE, s_len)
    deps = mix_src.mid(x)
    weights = _mix_weights(mix_src.get(h))
    w_swa, w_mla, w_o, wq_ext, wkv = weights
    swa2 = _mm(
        "mix_proj_swa", "nt", (s_len // tm, 1, 1),
        h, (tm, D_MODEL), lambda i, j, k: (i, 0),
        w_swa, (N_SWA_COLS, D_MODEL), lambda i, j, k: (0, 0),
        _sds((s_len, N_SWA_COLS), BF16), (tm, N_SWA_COLS), lambda i, j, k: (i, 0),
        (tm, N_SWA_COLS), deps=deps,
    )
    pm = _mm(
        "mix_proj_mla", "nt", (s_len // tm, 1, 1),
        h, (tm, D_MODEL), lambda i, j, k: (i, 0),
        w_mla, (N_MLA_COLS, D_MODEL), lambda i, j, k: (0, 0),
        _sds((s_len, N_MLA_COLS), F32), (tm, N_MLA_COLS), lambda i, j, k: (i, 0),
        (tm, N_MLA_COLS),
    )
    q4, k4, v4 = _mla_proj(pm, cos2, sin2, q_norm, kv_norm, wq_ext, wkv)
    oa, lse_a = _swa_fwd(swa2, bias, sinkb)
    ob, lse_b = _mla_fwd(q4, k4, v4)
    cat = jnp.concatenate([oa, ob], axis=1)
    z = _mm(
        "mix_out", "nn", (s_len // tm, 1, 1),
        cat, (tm, D_MODEL), lambda i, j, k: (i, 0),
        w_o, (D_MODEL, D_MODEL), lambda i, j, k: (0, 0),
        _sds((s_len, D_MODEL), F32), (tm, D_MODEL), lambda i, j, k: (i, 0),
        (tm, D_MODEL),
    )
    def residual_norm(x_, z_, g_, gamma_, sc_, sh_):
        x_out = x_ + g_ * z_
        return x_out, _normmod(x_out, gamma_, sc_, sh_)

    out = _rowwise(
        "mix_residual_norm", residual_norm, [x, z], [gate] + list(next_norm),
        [_sds((s_len, D_MODEL), F32), _sds((s_len, D_MODEL), BF16)], [], 256,
    )
    return out, (weights, h, swa2, pm, q4, k4, v4, cat, lse_a, lse_b, z)


def _mix_bwd(dxo, x, gamma, sc, gate, q_norm, kv_norm, bias, sinkb, cos2, sin2, saved, hooks):
    weights, h, swa2, pm, q4, k4, v4, cat, lse_a, lse_b, z = saved
    w_swa, w_mla, w_o, wq_ext, wkv = weights
    s_len = x.shape[0]
    tm = tk = min(ROW_TILE, s_len)
    vec = _sds((1, D_MODEL), F32)
    n_proj = N_SWA_COLS + N_MLA_COLS
    half_d = D_MODEL // 2

    dz, dgate = _rowwise(
        "mix_bwd_gate", lambda dxo_, z_, g_: (g_ * dxo_, _sum0(z_ * dxo_)),
        [dxo, z], [gate], [_sds((s_len, D_MODEL), BF16)], [vec], 256,
    )
    dw_o = _mm(
        "mix_bwd_wo", "tn", (2, 1, s_len // tk),
        cat, (tk, half_d), lambda i, j, k: (k, i),
        dz, (tk, D_MODEL), lambda i, j, k: (k, 0),
        _sds((D_MODEL, D_MODEL), F32), (half_d, D_MODEL), lambda i, j, k: (i, 0),
        (half_d, D_MODEL),
    )
    dcat = _mm(
        "mix_bwd_dcat", "nt", (s_len // tm, 1, 1),
        dz, (tm, D_MODEL), lambda i, j, k: (i, 0),
        w_o, (D_MODEL, D_MODEL), lambda i, j, k: (0, 0),
        _sds((s_len, D_MODEL), BF16), (tm, D_MODEL), lambda i, j, k: (i, 0),
        (tm, D_MODEL), deps=hooks.after_gate(dz),
    )
    dq_a, dkva, dkvb, dbias, dsink = _swa_bwd(swa2, bias, sinkb, cat, dcat, lse_a)
    dq4, dk4, dv4 = _mla_bwd(q4, k4, v4, cat, dcat, lse_b)
    dpm, dq_norm, dkv_norm, dwq_ext, dwkv = _mla_proj_bwd(pm, cos2, sin2, dq4, dk4, dv4, q_norm, kv_norm, wq_ext, wkv)

    dkv = dkva + jnp.concatenate([dkvb[WINDOW:], jnp.zeros_like(dkvb[:WINDOW])], axis=0)
    dproj = jnp.concatenate([dq_a, dkv.astype(BF16), dpm], axis=1)
    w_proj = jnp.concatenate([w_swa, w_mla], axis=0)

    dw_proj = _mm(
        "mix_bwd_win", "tn", (n_proj // 256, 1, s_len // tk),
        dproj, (tk, 256), lambda i, j, k: (k, i),
        h, (tk, D_MODEL), lambda i, j, k: (k, 0),
        _sds((n_proj, D_MODEL), F32), (256, D_MODEL), lambda i, j, k: (i, 0),
        (256, D_MODEL),
    )
    dw_swa, dw_mla = dw_proj[:N_SWA_COLS], dw_proj[N_SWA_COLS:]
    dh = _mm(
        "mix_bwd_dh", "nn", (s_len // tm, 1, 1),
        dproj, (tm, n_proj), lambda i, j, k: (i, 0),
        w_proj, (n_proj, D_MODEL), lambda i, j, k: (0, 0),
        _sds((s_len, D_MODEL), F32), (tm, D_MODEL), lambda i, j, k: (i, 0),
        (tm, D_MODEL),
    )
    dx, dsc, dsh, dgamma = _normmod_bwd_call("mix_bwd_normmod", [dh], x, dxo, gamma, sc)

    half = MLA_ROPE // 2
    n_mla_in = D_IN - N_SWA_COLS
    dw_mla_base = dw_mla[:n_mla_in]
    dw_mla_base = dw_mla_base.at[n_mla_in - half:].add(dw_mla[n_mla_in:n_mla_in + half])
    dw_mla_base = dw_mla_base.at[n_mla_in - MLA_ROPE:n_mla_in - half].add(dw_mla[n_mla_in + half:])
    dw_in_t = jnp.concatenate([dw_swa, dw_mla_base], axis=0)
    dw_uq_t = dwq_ext[:, :MLA_QK]
    dw_uq_t = dw_uq_t.at[:, MLA_QK - half:].add(dwq_ext[:, MLA_QK:MLA_QK + half])
    dw_uq_t = dw_uq_t.at[:, MLA_NOPE:MLA_QK - half].add(dwq_ext[:, MLA_QK + half:])
    rest = jnp.concatenate(
        [
            dw_in_t.reshape(N_CHIPS, R_IN, D_MODEL),
            dw_o.reshape(N_CHIPS, R_O, D_MODEL),
            dw_uq_t.reshape(N_CHIPS, R_UQ, D_MODEL),
            dwkv.reshape(N_CHIPS, R_UKV, D_MODEL),
            jnp.zeros((N_CHIPS, F_SHARD - O_PAD, D_MODEL), F32),
        ],
        axis=1,
    ).astype(BF16)
    return dx, rest, (dsh, dsc, dgate, dgamma), dq_norm, dkv_norm, dbias, dsink


def _device_step(x, target, mod, gu1_src, down1_src, mix_src, ffn2_src, norms, q_norm, kv_norm, sinks, rel_bias,
                 hooks2=None, hooks_mix=None, mix_done=None, hooks1=None):
    s_len = x.shape[0]
    sh1, sc1, g1, sh2, sc2, g2, sh3, sc3, g3 = [mod[:, i * D_MODEL:(i + 1) * D_MODEL] for i in range(N_MOD)]
    n1, n2, n3, nf = norms
    bkt = jnp.asarray(_bucket_map())
    bias = _bias_expand(rel_bias.T, bkt).reshape(SWA_HEADS, WINDOW, 2 * WINDOW)
    sinkb = jnp.broadcast_to(sinks.reshape(SWA_HEADS, 1, 1), (SWA_HEADS, WINDOW, 1))
    cos2, sin2 = _rope_tables(s_len)

    (x1, h2), saved1 = _ffn_fwd(
        "ffn1", x, n1, sh1, sc1, g1, gu1_src, 0, down1_src, 0, sh1, next_norm=(n2, sc2, sh2)
    )
    (x2, h3), saved2 = _mix_fwd(
        x1, h2, g2, mix_src, q_norm, kv_norm, bias, sinkb, cos2, sin2, next_norm=(n3, sc3, sh3)
    )
    (dx3, sq, dnf), saved3 = _ffn_fwd(
        "ffn2", x2, n3, sh3, sc3, g3, ffn2_src, 0, None, 2, x2, h_pre=h3, head=(target, nf)
    )
    loss_part = (0.5 / D_MODEL) * jnp.sum(sq)

    dx2, gb2, (dsh3, dsc3, dg3, dn3) = _ffn_bwd("ffn2", dx3, x2, n3, sc3, g3, saved3, hooks2 or _BwdHooks())
    dx1, rest, (dsh2, dsc2, dg2, dn2), dq_norm, dkv_norm, dbias, dsink = _mix_bwd(
        dx2, x1, n2, sc2, g2, q_norm, kv_norm, bias, sinkb, cos2, sin2, saved2, hooks_mix or _BwdHooks()
    )
    rest = rest[:, None]
    deps1 = mix_done(dx1, rest) if mix_done is not None else []
    dx0, gb1, (dsh1, dsc1, dg1, dn1) = _ffn_bwd(
        "ffn1", dx1, x, n1, sc1, g1, saved1, hooks1 or _BwdHooks(), deps=deps1
    )

    dmod = jnp.concatenate([dsh1, dsc1, dg1, dsh2, dsc2, dg2, dsh3, dsc3, dg3], axis=-1)
    drel = _bias_reduce(dbias.reshape(SWA_HEADS, -1), bkt).T
    dsinks = jnp.sum(dsink, axis=(1, 2)).reshape(1, SWA_HEADS)
    small = (dn1, dn2, dn3, dnf, dq_norm, dkv_norm, dsinks, drel)
    return loss_part, dx0, (gb1, gb2, rest), dmod, small


_MESH = pl.DeviceIdType.MESH


def _place():
    return lax.axis_index("x"), lax.axis_index("y"), lax.axis_index("c")


def _other_chips(x, y):
    return [(1 - x, y), (x, 1 - y), (1 - x, 1 - y)]


def _allgather_small(name, blk):
    m_per, n = blk.shape

    def body(x_ref, out_ref, send_sems, recv_sems, local_sem):
        x, y, c = _place()
        me, sibling = (x, y, c), (x, y, 1 - c)
        chips = _other_chips(x, y)

        def rows(px, py, pc):
            return out_ref.at[pl.ds((4 * px + 2 * py + pc) * m_per, m_per), :]

        def copy(k, block, to, src=None):
            return pltpu.make_async_remote_copy(
                src_ref=rows(*block) if src is None else src, dst_ref=rows(*block),
                send_sem=send_sems.at[k], recv_sem=recv_sems.at[k], device_id=to, device_id_type=_MESH,
            )

        mine = pltpu.make_async_copy(x_ref, rows(*me), local_sem)
        mine.start()
        first = [copy(0, me, sibling, src=x_ref)]
        first += [copy(1 + j, me, (*chip, c), src=x_ref) for j, chip in enumerate(chips)]
        for cp in first:
            cp.start()
        passed = [copy(4 + j, (*chip, c), sibling) for j, chip in enumerate(chips)]
        for j, chip in enumerate(chips):
            copy(1 + j, (*chip, c), me).wait_recv()
            passed[j].start()
        copy(0, sibling, me).wait_recv()
        for j, chip in enumerate(chips):
            copy(4 + j, (*chip, 1 - c), me).wait_recv()
        for cp in first + passed:
            cp.wait_send()
        mine.wait()

    return pl.pallas_call(
        body,
        name=name,
        out_shape=_sds((N_DEV * m_per, n), blk.dtype),
        in_specs=[pl.BlockSpec(memory_space=pltpu.VMEM)],
        out_specs=pl.BlockSpec(memory_space=pltpu.VMEM),
        scratch_shapes=[pltpu.SemaphoreType.DMA((7,)), pltpu.SemaphoreType.DMA((7,)), pltpu.SemaphoreType.DMA],
    )(blk)


def _gather_sends(n, own_ref, g_ref, send_sem, recv_sem, x, y, c, chip):
    return [
        pltpu.make_async_remote_copy(
            src_ref=own_ref.at[p, pl.ds(c * HALF, HALF), :], dst_ref=g_ref.at[2 * x + y, p, pl.ds(c * HALF, HALF), :],
            send_sem=send_sem, recv_sem=recv_sem, device_id=(*chip, c), device_id_type=_MESH,
        )
        for p in range(n)
    ]


def _gather_forwards(n, g_ref, send_sem, recv_sem, chip, c, sibling):
    return [
        pltpu.make_async_remote_copy(
            src_ref=g_ref.at[2 * chip[0] + chip[1], p, pl.ds(c * HALF, HALF), :],
            dst_ref=g_ref.at[2 * chip[0] + chip[1], p, pl.ds(c * HALF, HALF), :],
            send_sem=send_sem, recv_sem=recv_sem, device_id=sibling, device_id_type=_MESH,
        )
        for p in range(n)
    ]


def _gather_all(own_ref, g_ref, send_sem, recv_sem, chip, half, c):
    return pltpu.make_async_remote_copy(
        src_ref=own_ref.at[:, pl.ds(c * HALF, HALF), :],
        dst_ref=g_ref.at[2 * chip[0] + chip[1], :, pl.ds(half * HALF, HALF), :],
        send_sem=send_sem, recv_sem=recv_sem, device_id=(*chip, c), device_id_type=_MESH,
    )


_HBM_SPEC = pl.BlockSpec(memory_space=pltpu.HBM)
_SEM_SPEC = pl.BlockSpec(memory_space=pltpu.SEMAPHORE)
_ANY_SPEC = pl.BlockSpec(memory_space=pl.ANY)
_VMEM_SPEC = pl.BlockSpec(memory_space=pltpu.VMEM)
_SPLIT_PARAMS = pltpu.CompilerParams(has_side_effects=pltpu.SideEffectType.DATAFLOW_SIDE_EFFECTING)
_TOKEN = jax.ShapeDtypeStruct((8, 128), F32)


def _in_hbm(a):
    return pltpu.with_memory_space_constraint(a, pltpu.HBM)


class _GatherBehind:
    def __init__(self, tag, own, then=None):
        self.tag, self.n, self.own, self.then = tag, own.shape[0], own, then

    def start(self, after):
        tag, own, n = self.tag, self.own, self.n
        x, y, _ = _place()
        g_init = lax.dynamic_update_slice(
            lax.empty((N_CHIPS,) + own.shape, own.dtype), own[None], (2 * x + y, 0, 0, 0)
        )

        def body(own_ref, g_ref, *rest):
            send_sems, recv_sems, _, _, token = rest[len(after):]
            x, y, c = _place()
            for j, chip in enumerate(_other_chips(x, y)):
                for cp in _gather_sends(n, own_ref, g_ref, send_sems.at[j], recv_sems.at[j], x, y, c, chip):
                    cp.start()
            token[...] = jnp.zeros_like(token)

        self.send1, self.recv1, self.own, self.g, self.token = pl.pallas_call(
            body,
            name=f"{tag}_start",
            out_shape=(
                pltpu.SemaphoreType.DMA((3,)), pltpu.SemaphoreType.DMA((3,)),
                pltpu.HBM(own.shape, own.dtype), pltpu.HBM(g_init.shape, g_init.dtype), _TOKEN,
            ),
            in_specs=(_HBM_SPEC, _HBM_SPEC) + (_ANY_SPEC,) * len(after),
            out_specs=(_SEM_SPEC, _SEM_SPEC, _HBM_SPEC, _HBM_SPEC, _VMEM_SPEC),
            input_output_aliases={0: 2, 1: 3},
            compiler_params=_SPLIT_PARAMS,
        )(_in_hbm(own), _in_hbm(g_init), *after)

    def mid(self, after):
        n = self.n

        def body(own_ref, g_ref, send1, recv1, after_ref, send2, recv2, g_out, token):
            x, y, c = _place()
            sibling = (x, y, 1 - c)
            chips = _other_chips(x, y)
            for j, chip in enumerate(chips):
                _gather_all(own_ref, g_ref, send1.at[j], recv1.at[j], chip, c, c).wait_recv()
                for cp in _gather_forwards(n, g_ref, send2.at[j], recv2.at[j], chip, c, sibling):
                    cp.start()
            for j, chip in enumerate(chips):
                _gather_all(own_ref, g_ref, send1.at[j], recv1.at[j], chip, c, c).wait_send()
            token[...] = jnp.zeros_like(token)

        self.send2, self.recv2, self.g, token = pl.pallas_call(
            body,
            name=f"{self.tag}_mid",
            out_shape=(
                pltpu.SemaphoreType.DMA((3,)), pltpu.SemaphoreType.DMA((3,)),
                pltpu.HBM(self.g.shape, self.g.dtype), _TOKEN,
            ),
            in_specs=(_HBM_SPEC, _HBM_SPEC, _SEM_SPEC, _SEM_SPEC, _ANY_SPEC),
            out_specs=(_SEM_SPEC, _SEM_SPEC, _HBM_SPEC, _VMEM_SPEC),
            input_output_aliases={1: 2},
            compiler_params=_SPLIT_PARAMS,
        )(self.own, self.g, self.send1, self.recv1, after)
        if self.then is None:
            return [token]
        self.then.start([token])
        return [token, self.then.token]

    def get(self, after):
        def body(g_ref, send2, recv2, after_ref, g_out):
            x, y, c = _place()
            for j, chip in enumerate(_other_chips(x, y)):
                slot_in = g_ref.at[2 * chip[0] + chip[1], :, pl.ds((1 - c) * HALF, HALF), :]
                slot_out = g_ref.at[2 * chip[0] + chip[1], :, pl.ds(c * HALF, HALF), :]
                cp = pltpu.make_async_remote_copy(
                    src_ref=slot_out, dst_ref=slot_in, send_sem=send2.at[j], recv_sem=recv2.at[j],
                    device_id=(x, y, 1 - c), device_id_type=_MESH,
                )
                cp.wait_send()
                cp.wait_recv()

        return pl.pallas_call(
            body,
            name=f"{self.tag}_wait",
            out_shape=pltpu.HBM(self.g.shape, self.g.dtype),
            in_specs=(_HBM_SPEC, _SEM_SPEC, _SEM_SPEC, _ANY_SPEC),
            out_specs=_HBM_SPEC,
            input_output_aliases={0: 0},
            compiler_params=_SPLIT_PARAMS,
        )(self.g, self.send2, self.recv2, after)


def _pair_sum(name, gb, land):
    def body(c_ref, gb_ref, land_ref, o_ref):
        o_ref[...] = (gb_ref[...].astype(F32) + land_ref[...].astype(F32)).astype(o_ref.dtype)

    core = lax.axis_index("c").astype(jnp.int32).reshape(1)
    return pl.pallas_call(
        body,
        name=name,
        grid_spec=pltpu.PrefetchScalarGridSpec(
            num_scalar_prefetch=1,
            grid=(N_CHIPS, gb.shape[1]),
            in_specs=[
                pl.BlockSpec((None, None, HALF, D_MODEL), lambda j, p, c: (j, p, c[0], 0)),
                pl.BlockSpec((None, None, HALF, D_MODEL), lambda j, p, c: (j, p, 0, 0)),
            ],
            out_specs=pl.BlockSpec((None, None, HALF, D_MODEL), lambda j, p, c: (j, p, 0, 0)),
        ),
        out_shape=_sds(land.shape, BF16),
        compiler_params=_cparams(("parallel", "parallel")),
    )(core, gb, land)


def _chip_sum_share(name, land):
    n = land.shape[1]

    def body(l_ref, r_ref, buf, send_sems, recv_sems, local_sems):
        p = pl.program_id(0)
        x, y, c = _place()
        acc = l_ref[0].astype(F32)
        for j in range(1, N_CHIPS):
            acc = acc + l_ref[j].astype(F32)
        buf[p] = acc

        def copies(q):
            mine = r_ref.at[q, pl.ds(c * HALF, HALF), :]
            theirs = r_ref.at[q, pl.ds((1 - c) * HALF, HALF), :]
            local = pltpu.make_async_copy(buf.at[q], mine, local_sems.at[q])
            out = pltpu.make_async_remote_copy(
                src_ref=buf.at[q], dst_ref=mine, send_sem=send_sems.at[q], recv_sem=recv_sems.at[q],
                device_id=(x, y, 1 - c), device_id_type=_MESH,
            )
            arrive = pltpu.make_async_remote_copy(
                src_ref=buf.at[q], dst_ref=theirs, send_sem=send_sems.at[q], recv_sem=recv_sems.at[q],
                device_id=(x, y, 1 - c), device_id_type=_MESH,
            )
            return local, out, arrive

        local, out, _ = copies(p)
        local.start()
        out.start()

        @pl.when(p == n - 1)
        def _():
            for q in range(n):
                local, out, arrive = copies(q)
                arrive.wait_recv()
                out.wait_send()
                local.wait()

    return pl.pallas_call(
        body,
        name=name,
        grid=(n,),
        in_specs=[pl.BlockSpec((N_CHIPS, None, HALF, D_MODEL), lambda p: (0, p, 0, 0))],
        out_specs=pl.BlockSpec(memory_space=pl.ANY),
        out_shape=_sds((n, F_SHARD, D_MODEL), F32),
        scratch_shapes=[
            pltpu.VMEM((n, HALF, D_MODEL), F32),
            pltpu.SemaphoreType.DMA((n,)), pltpu.SemaphoreType.DMA((n,)), pltpu.SemaphoreType.DMA((n,)),
        ],
        compiler_params=_cparams(("arbitrary",)),
    )(land)


class _ReduceBehind:
    def __init__(self, tag, gb, after=()):
        self.tag = tag
        n = gb.shape[1]
        land = lax.empty((N_CHIPS, n, HALF, D_MODEL), gb.dtype)

        def body(gb_ref, land_ref, *rest):
            send_sem, recv_sem, _, _, token = rest[len(after):]
            self._pair_copy(gb_ref, land_ref, send_sem, recv_sem).start()
            token[...] = jnp.zeros_like(token)

        self.send, self.recv, self.gb, self.land, self.token = pl.pallas_call(
            body,
            name=f"grads_pair_start_{tag}",
            out_shape=(
                pltpu.SemaphoreType.DMA((1,)), pltpu.SemaphoreType.DMA((1,)),
                pltpu.HBM(gb.shape, gb.dtype), pltpu.HBM(land.shape, land.dtype), _TOKEN,
            ),
            in_specs=(_HBM_SPEC, _HBM_SPEC) + (_ANY_SPEC,) * len(after),
            out_specs=(_SEM_SPEC, _SEM_SPEC, _HBM_SPEC, _HBM_SPEC, _VMEM_SPEC),
            input_output_aliases={0: 2, 1: 3},
            compiler_params=_SPLIT_PARAMS,
        )(_in_hbm(gb), _in_hbm(land), *after)

    @staticmethod
    def _pair_copy(gb_ref, land_ref, send_sem, recv_sem):
        x, y, c = _place()
        return pltpu.make_async_remote_copy(
            src_ref=gb_ref.at[:, :, pl.ds((1 - c) * HALF, HALF), :], dst_ref=land_ref,
            send_sem=send_sem.at[0], recv_sem=recv_sem.at[0], device_id=(x, y, 1 - c), device_id_type=_MESH,
        )

    @staticmethod
    def _chip_copies(p_ref, land_ref, send_sems, recv_sems):
        x, y, c = _place()
        me = 2 * x + y
        sends, arrivals = [], []
        for k, chip in enumerate(_other_chips(x, y)):
            them = 2 * chip[0] + chip[1]
            sends.append(pltpu.make_async_remote_copy(
                src_ref=p_ref.at[them], dst_ref=land_ref.at[me], send_sem=send_sems.at[k], recv_sem=recv_sems.at[k],
                device_id=(*chip, c), device_id_type=_MESH,
            ))
            arrivals.append(pltpu.make_async_remote_copy(
                src_ref=p_ref.at[me], dst_ref=land_ref.at[them], send_sem=send_sems.at[k], recv_sem=recv_sems.at[k],
                device_id=(*chip, c), device_id_type=_MESH,
            ))
        return sends, arrivals

    def mid(self, after):
        tag = self.tag

        def wait_body(gb_ref, land_ref, send_sem, recv_sem, after_ref, gb_out, land_out):
            cp = self._pair_copy(gb_ref, land_ref, send_sem, recv_sem)
            cp.wait_send()
            cp.wait_recv()

        gb, land = pl.pallas_call(
            wait_body,
            name=f"grads_pair_wait_{tag}",
            out_shape=(pltpu.HBM(self.gb.shape, self.gb.dtype), pltpu.HBM(self.land.shape, self.land.dtype)),
            in_specs=(_HBM_SPEC, _HBM_SPEC, _SEM_SPEC, _SEM_SPEC, _ANY_SPEC),
            out_specs=(_HBM_SPEC, _HBM_SPEC),
            input_output_aliases={0: 0, 1: 1},
            compiler_params=_SPLIT_PARAMS,
        )(self.gb, self.land, self.send, self.recv, after)
        part = _pair_sum(f"grads_pair_sum_{tag}", gb, land)

        x, y, _ = _place()
        start = (2 * x + y, 0, 0, 0)
        own = lax.dynamic_slice(part, start, (1,) + part.shape[1:])
        land2 = lax.dynamic_update_slice(lax.empty(part.shape, part.dtype), own, start)

        def start_body(p_ref, land_ref, send_sems, recv_sems, p_out, land_out, token):
            for cp in self._chip_copies(p_ref, land_ref, send_sems, recv_sems)[0]:
                cp.start()
            token[...] = jnp.zeros_like(token)

        self.send2, self.recv2, self.part, self.land2, token = pl.pallas_call(
            start_body,
            name=f"grads_chip_start_{tag}",
            out_shape=(
                pltpu.SemaphoreType.DMA((3,)), pltpu.SemaphoreType.DMA((3,)),
                pltpu.HBM(part.shape, part.dtype), pltpu.HBM(land2.shape, land2.dtype), _TOKEN,
            ),
            in_specs=(_HBM_SPEC, _HBM_SPEC),
            out_specs=(_SEM_SPEC, _SEM_SPEC, _HBM_SPEC, _HBM_SPEC, _VMEM_SPEC),
            input_output_aliases={0: 2, 1: 3},
            compiler_params=_SPLIT_PARAMS,
        )(_in_hbm(part), _in_hbm(land2))
        return [token]

    def get(self, after):
        n_after = len(after)

        def wait_body(p_ref, land_ref, send_sems, recv_sems, *rest):
            sends, arrivals = self._chip_copies(p_ref, land_ref, send_sems, recv_sems)
            for cp in arrivals:
                cp.wait_recv()
            for cp in sends:
                cp.wait_send()

        _, land2 = pl.pallas_call(
            wait_body,
            name=f"grads_chip_wait_{self.tag}",
            out_shape=(pltpu.HBM(self.part.shape, self.part.dtype), pltpu.HBM(self.land2.shape, self.land2.dtype)),
            in_specs=(_HBM_SPEC, _HBM_SPEC, _SEM_SPEC, _SEM_SPEC) + (_ANY_SPEC,) * n_after,
            out_specs=(_HBM_SPEC, _HBM_SPEC),
            input_output_aliases={0: 0, 1: 1},
            compiler_params=_SPLIT_PARAMS,
        )(self.part, self.land2, self.send2, self.recv2, *after)
        return _chip_sum_share(f"grads_chip_sum_share_{self.tag}", land2)


def _allreduce_small(blk):
    gathered = _allgather_small("allgather_small_grads", blk).reshape(N_DEV, PK_ROWS, 128)

    def body(g_ref, o_ref):
        acc = g_ref[0]
        for k in range(1, N_DEV):
            acc = acc + g_ref[k]
        o_ref[...] = acc

    total = pl.pallas_call(body, name="small_grads_sum", out_shape=_sds((PK_ROWS, 128), F32))(gathered)
    return gathered, total


def _adamw_math(w_, g_, m_, v_):
    m_new = ADAM_B1 * m_ + (1.0 - ADAM_B1) * g_
    v_new = ADAM_B2 * v_ + (1.0 - ADAM_B2) * (g_ * g_)
    m_hat = m_new / (1.0 - ADAM_B1 ** ADAM_STEP)
    v_hat = v_new / (1.0 - ADAM_B2 ** ADAM_STEP)
    delta = -ADAM_LR * (m_hat / (jnp.sqrt(v_hat) + ADAM_EPS) + ADAM_WD * w_)
    return delta, m_new, v_new


def _adamw(name, w, g, m, v):
    rows, cols = w.shape
    tm = rows
    while tm * cols * 4 > (1 << 20) and tm % 16 == 0:
        tm //= 2
    out = _sds(w.shape, F32)
    return _rowwise(name, _adamw_math, [w, g, m, v], [], [out, out, out], [], tm)


def _adamw_small(ws, gs, ms, vs):
    n = len(ws)
    shapes = [w.shape for w in ws]
    as2d = lambda a: a.reshape(1, -1) if a.ndim == 1 else a

    def body(*refs):
        ins, outs = refs[:4 * n], refs[4 * n:]
        for k in range(n):
            res = _adamw_math(*[ins[j * n + k][...] for j in range(4)])
            for j in range(3):
                outs[j * n + k][...] = res[j]

    args = [as2d(a) for group in (ws, gs, ms, vs) for a in group]
    out = pl.pallas_call(
        body, name="adamw_small", out_shape=[_sds(as2d(w).shape, F32) for w in ws] * 3, compiler_params=_cparams()
    )(*args)
    back = [o.reshape(shapes[i % n]) for i, o in enumerate(out)]
    return back[:n], back[n:2 * n], back[2 * n:]


def _pack_small(b_mod_like, n1, n2, n3, nf, qn, kvn, sinks, rel):
    parts = [
        b_mod_like.reshape(PK_MOD, 128),
        n1.reshape(8, 128), n2.reshape(8, 128), n3.reshape(8, 128), nf.reshape(8, 128),
        qn.reshape(2, 128), kvn.reshape(1, 128),
        jnp.pad(sinks.reshape(1, SWA_HEADS), ((0, 0), (0, 128 - SWA_HEADS))),
        rel.reshape(2, 128),
        jnp.zeros((2, 128), F32),
    ]
    return jnp.concatenate(parts, axis=0)


def _unpack_small(p):
    o = PK_MOD
    return (
        p[:o].reshape(1, N_MOD * D_MODEL),
        p[o:o + 8].reshape(1, D_MODEL), p[o + 8:o + 16].reshape(1, D_MODEL),
        p[o + 16:o + 24].reshape(1, D_MODEL), p[o + 24:o + 32].reshape(D_MODEL),
        p[o + 32:o + 34].reshape(1, MLA_Q_RANK), p[o + 34:o + 35].reshape(1, MLA_KV_RANK),
        p[o + 35:o + 36, :SWA_HEADS].reshape(1, SWA_HEADS),
        p[o + 36:o + 38].reshape(NUM_BUCKETS, SWA_HEADS),
    )


def kernel(x, c, w_mod, b_mod, norm_ffn1, ffn1_gate, ffn1_up, ffn1_down, norm_mix, w_in, q_norm, kv_norm, w_uq, w_ukv, sinks, w_o, norm_ffn2, ffn2_gate, ffn2_up, ffn2_down, rel_bias, norm_final, loss_target, m_w_mod, m_b_mod, m_norm_ffn1, m_ffn1_gate, m_ffn1_up, m_ffn1_down, m_norm_mix, m_w_in, m_q_norm, m_kv_norm, m_w_uq, m_w_ukv, m_sinks, m_w_o, m_norm_ffn2, m_ffn2_gate, m_ffn2_up, m_ffn2_down, m_rel_bias, m_norm_final, v_w_mod, v_b_mod, v_norm_ffn1, v_ffn1_gate, v_ffn1_up, v_ffn1_down, v_norm_mix, v_w_in, v_q_norm, v_kv_norm, v_w_uq, v_w_ukv, v_sinks, v_w_o, v_norm_ffn2, v_ffn2_gate, v_ffn2_up, v_ffn2_down, v_rel_bias, v_norm_final):
    ax, ay, ac = _place()
    chip = 2 * ax + ay
    dev = 2 * chip + ac
    n_mod_shard = N_MOD * D_MODEL // N_CHIPS

    def t16(w):
        return w[0].T.astype(BF16)

    rest = jnp.concatenate(
        [
            t16(w_in),
            w_o[0].astype(BF16),
            t16(w_uq).reshape(R_UQ, D_MODEL),
            t16(w_ukv).reshape(R_UKV, D_MODEL),
            jnp.zeros((F_SHARD - O_PAD, D_MODEL), BF16),
        ],
        axis=0,
    )
    own_gu1 = jnp.stack([t16(ffn1_gate), t16(ffn1_up)])
    own_down1 = ffn1_down[0].astype(BF16)[None]
    own_mix = rest[None]
    own_ffn2 = jnp.stack([t16(ffn2_gate), t16(ffn2_up), ffn2_down[0].astype(BF16)])

    (c_act,) = _rowwise(
        "silu_c", lambda v: v * _sigmoid(v), [c.reshape(8, 128)], [], [_sds((8, 128), F32)], [], 8
    )
    c_all = _allgather_small("allgather_c", c_act).reshape(N_DEV, D_MODEL)
    mod_cols = _mm(
        "mod_fwd", "nn", (1, n_mod_shard // 768, 1),
        c_all, (N_DEV, D_MODEL), lambda i, j, k: (0, 0),
        w_mod[0], (D_MODEL, 768), lambda i, j, k: (0, j),
        _sds((N_DEV, n_mod_shard), F32), (N_DEV, 768), lambda i, j, k: (0, j),
        (N_DEV, 768),
    )
    mod_all = _allgather_small("allgather_mod", mod_cols).reshape(N_CHIPS, 2, N_DEV, n_mod_shard)[:, 0]
    mod_all = mod_all.transpose(1, 0, 2).reshape(N_DEV, N_MOD * D_MODEL)
    mod = lax.dynamic_slice_in_dim(mod_all, dev, 1, axis=0) + b_mod

    norms = (norm_ffn1, norm_mix, norm_ffn2, norm_final.reshape(1, D_MODEL))

    ffn2_src = _GatherBehind("gather_ffn2", own_ffn2)
    mix_src = _GatherBehind("gather_mix", own_mix, then=ffn2_src)
    down1_src = _GatherBehind("gather_down1", own_down1, then=mix_src)
    gu1_src = _GatherBehind("gather_gu1", own_gu1, then=down1_src)
    gu1_src.start([mod_all])

    reducing, red = {}, {}

    def begin(tag, gb, after):
        reducing[tag] = _ReduceBehind(tag, gb, after=after)
        return [reducing[tag].token]

    def ffn2_gu_done(gb):
        return begin("ffn2_gu", gb, reducing["ffn2_down"].mid(gb))

    def mix_done(dx1, gb_rest):
        red["ffn2_down"] = reducing["ffn2_down"].get([dx1])
        red["ffn2_gu"] = reducing["ffn2_gu"].get([red["ffn2_down"]])
        return begin("rest", gb_rest, [red["ffn2_gu"]])

    def ffn1_gu_done(gb):
        red["rest"] = reducing["rest"].get([gb])
        begin("ffn1_gu", gb, reducing["ffn1_down"].mid(red["rest"]))
        return reducing["ffn1_gu"].mid(reducing["ffn1_gu"].token)

    loss_part, grad_x, _, dmod, small = _device_step(
        x[0], loss_target[0], mod, gu1_src, down1_src, mix_src, ffn2_src, norms, q_norm, kv_norm, sinks, rel_bias,
        hooks2=_BwdHooks(after_wdown=lambda gb: begin("ffn2_down", gb, ()), after_wgu=ffn2_gu_done),
        hooks_mix=_BwdHooks(after_gate=lambda dz: reducing["ffn2_gu"].mid(dz)),
        mix_done=mix_done,
        hooks1=_BwdHooks(
            after_swiglu=lambda dab3: reducing["rest"].mid(dab3),
            after_wdown=lambda gb: begin("ffn1_down", gb, ()),
            after_wgu=ffn1_gu_done,
        ),
    )
    loss = lax.psum(loss_part, ("x", "y", "c"))

    gathered, total = _allreduce_small(_pack_small(dmod, *small))
    (g_b_mod, g_n1, g_n2, g_n3, g_nf, g_qn, g_kvn, g_sinks, g_rel) = _unpack_small(total)
    dmod_all = gathered[:, :PK_MOD].reshape(N_DEV, N_MOD * D_MODEL)
    dmod_cols = lax.dynamic_slice_in_dim(dmod_all, chip * n_mod_shard, n_mod_shard, axis=1)
    g_w_mod = _mm(
        "mod_bwd", "tn", (1, n_mod_shard // 768, 1),
        c_all, (N_DEV, D_MODEL), lambda i, j, k: (0, 0),
        dmod_cols, (N_DEV, 768), lambda i, j, k: (0, j),
        _sds((D_MODEL, n_mod_shard), F32), (D_MODEL, 768), lambda i, j, k: (0, j),
        (D_MODEL, 768),
    )

    r_rest = red["rest"][0]
    transposed = {"ffn1_gate", "ffn1_up", "ffn2_gate", "ffn2_up", "w_in", "w_uq", "w_ukv"}
    g_big = {
        "ffn2_gate": red["ffn2_gu"][0], "ffn2_up": red["ffn2_gu"][1], "ffn2_down": red["ffn2_down"][0],
        "w_in": r_rest[O_IN:O_IN + R_IN],
        "w_o": r_rest[O_O:O_O + R_O],
        "w_uq": r_rest[O_UQ:O_UQ + R_UQ].reshape(MLA_QK, MLA_Q_RANK),
        "w_ukv": r_rest[O_UKV:O_UKV + R_UKV].reshape(MLA_NOPE + MLA_V, MLA_KV_RANK),
        "w_mod": g_w_mod,
    }
    w_big = {
        "ffn2_gate": (ffn2_gate, m_ffn2_gate, v_ffn2_gate),
        "ffn2_up": (ffn2_up, m_ffn2_up, v_ffn2_up), "ffn2_down": (ffn2_down, m_ffn2_down, v_ffn2_down),
        "w_in": (w_in, m_w_in, v_w_in), "w_o": (w_o, m_w_o, v_w_o), "w_uq": (w_uq, m_w_uq, v_w_uq),
        "w_ukv": (w_ukv, m_w_ukv, v_w_ukv), "w_mod": (w_mod, m_w_mod, v_w_mod),
    }
    grads, deltas, new_m, new_v = {}, {}, {}, {}

    def update(names):
        for nm in names:
            w, m, v = w_big[nm]
            if nm in transposed:
                outs = _adamw(f"adamw_{nm}", w[0].T, g_big[nm], m[0].T, v[0].T)
                g_, d_, m_, v_ = [a.T for a in (g_big[nm],) + tuple(outs)]
            else:
                g_, (d_, m_, v_) = g_big[nm], _adamw(f"adamw_{nm}", w[0], g_big[nm], m[0], v[0])
            grads[nm], deltas[nm], new_m[nm], new_v[nm] = g_[None], d_[None], m_[None], v_[None]

    update(list(w_big))
    red1_down = reducing["ffn1_down"].get([deltas[nm] for nm in w_big])
    red1_gu = reducing["ffn1_gu"].get([red1_down])
    g_big.update({"ffn1_gate": red1_gu[0], "ffn1_up": red1_gu[1], "ffn1_down": red1_down[0]})
    w_big.update({
        "ffn1_gate": (ffn1_gate, m_ffn1_gate, v_ffn1_gate), "ffn1_up": (ffn1_up, m_ffn1_up, v_ffn1_up),
        "ffn1_down": (ffn1_down, m_ffn1_down, v_ffn1_down),
    })
    update(["ffn1_gate", "ffn1_up", "ffn1_down"])

    small_names = ["b_mod", "norm_ffn1", "norm_mix", "norm_ffn2", "norm_final", "q_norm", "kv_norm", "sinks", "rel_bias"]
    w_small = (b_mod, norm_ffn1, norm_mix, norm_ffn2, norm_final, q_norm, kv_norm, sinks, rel_bias)
    m_small = (m_b_mod, m_norm_ffn1, m_norm_mix, m_norm_ffn2, m_norm_final, m_q_norm, m_kv_norm, m_sinks, m_rel_bias)
    v_small = (v_b_mod, v_norm_ffn1, v_norm_mix, v_norm_ffn2, v_norm_final, v_q_norm, v_kv_norm, v_sinks, v_rel_bias)
    g_small = (g_b_mod, g_n1, g_n2, g_n3, g_nf, g_qn, g_kvn, g_sinks, g_rel)
    d_s, m_s, v_s = _adamw_small(w_small, g_small, m_small, v_small)
    for nm, g_, d_, m_, v_ in zip(small_names, g_small, d_s, m_s, v_s):
        grads[nm], deltas[nm], new_m[nm], new_v[nm] = g_, d_, m_, v_

    order = ["w_mod", "b_mod", "norm_ffn1", "ffn1_gate", "ffn1_up", "ffn1_down", "norm_mix", "w_in", "q_norm", "kv_norm",
             "w_uq", "w_ukv", "sinks", "w_o", "norm_ffn2", "ffn2_gate", "ffn2_up", "ffn2_down", "rel_bias", "norm_final"]
    return (loss, grad_x[None], *[grads[n] for n in order], *[deltas[n] for n in order],
            *[new_m[n] for n in order], *[new_v[n] for n in order])
```

```python
import functools
import math

import jax
import jax.numpy as jnp
import numpy as np
from jax import lax
from jax.experimental import pallas as pl
from jax.experimental.pallas import tpu as pltpu

F32, BF16 = jnp.float32, jnp.bfloat16

D_MODEL = 1024
D_FF = 2816
EPS = 1e-6
N_MOD = 9
SWA_HEADS, SWA_KV_HEADS, SWA_HEAD_DIM, WINDOW = 8, 2, 64, 128
SWA_GROUP = SWA_HEADS // SWA_KV_HEADS
MLA_HEADS, MLA_Q_RANK, MLA_KV_RANK, MLA_NOPE, MLA_ROPE, MLA_V = 4, 256, 128, 128, 64, 128
MLA_QK = MLA_NOPE + MLA_ROPE
ROPE_THETA = 10000.0
NUM_BUCKETS, MAX_DISTANCE = 32, 128
D_IN = 1216
N_SWA_COLS = 768
N_MLA_COLS = 512

ADAM_LR, ADAM_B1, ADAM_B2, ADAM_EPS, ADAM_WD, ADAM_STEP = 0.001, 0.9, 0.999, 1e-08, 0.01, 10

N_CHIPS = 4
N_DEV = 8
F_SHARD = D_FF // N_CHIPS
HALF = F_SHARD // 2
R_IN, R_O, R_UQ, R_UKV = 304, 256, 48, 32
O_IN, O_O, O_UQ, O_UKV, O_PAD = 0, 304, 560, 608, 640

PK_MOD = 72
PK_ROWS = 112
VMEM_LIMIT = 56 << 20
ROW_TILE = 2048

_DN = {
    "nn": (((1,), (0,)), ((), ())),
    "nt": (((1,), (1,)), ((), ())),
    "tn": (((0,), (0,)), ((), ())),
}


def _sds(shape, dtype):
    return jax.ShapeDtypeStruct(tuple(shape), dtype)


def _cparams(sem=None):
    kw = {"vmem_limit_bytes": VMEM_LIMIT}
    if sem is not None:
        kw["dimension_semantics"] = sem
    return pltpu.CompilerParams(**kw)


def _dot(a, b, dims):
    return lax.dot_general(a.astype(BF16), b.astype(BF16), _DN[dims], preferred_element_type=F32)


def _mm(name, dims, grid, a, a_blk, a_idx, b, b_blk, b_idx, out, out_blk, out_idx, acc_shape, alias=None, deps=()):
    nk = grid[2]

    def body(*refs):
        a_ref, b_ref = refs[:2]
        o_ref, acc_ref = refs[-2:]
        part = _dot(a_ref[...], b_ref[...], dims)
        if nk == 1:
            o_ref[...] = part.astype(o_ref.dtype)
            return
        k = pl.program_id(2)

        @pl.when(k == 0)
        def _():
            acc_ref[...] = part

        @pl.when((k > 0) & (k < nk - 1))
        def _():
            acc_ref[...] += part

        @pl.when(k == nk - 1)
        def _():
            o_ref[...] = (acc_ref[...] + part).astype(o_ref.dtype)

    in_specs = [pl.BlockSpec(a_blk, a_idx), pl.BlockSpec(b_blk, b_idx)]
    args = [a, b]
    kw = {}
    if alias is not None:
        in_specs.append(pl.BlockSpec(memory_space=pl.ANY))
        args.append(alias)
        kw["input_output_aliases"] = {2: 0}
    in_specs += [pl.BlockSpec(memory_space=pl.ANY)] * len(deps)
    args += list(deps)
    return pl.pallas_call(
        body,
        name=name,
        grid=grid,
        in_specs=in_specs,
        out_specs=pl.BlockSpec(out_blk, out_idx),
        out_shape=out,
        scratch_shapes=[pltpu.VMEM(acc_shape if nk > 1 else (8, 128), F32)],
        compiler_params=_cparams(("parallel", "parallel", "arbitrary")),
        **kw,
    )(*args)


def _rowwise(name, fn, tiled, full, out_tiled, out_red, tm, deps=()):
    rows = tiled[0].shape[-2]
    grid = (rows // tm,)
    n_in = len(tiled) + len(full)
    n_t = len(out_tiled)
    n_dep = len(deps)

    def tspec(shape):
        nd = len(shape)
        return pl.BlockSpec(tuple(shape[:-2]) + (tm, shape[-1]), lambda i, nd=nd: (0,) * (nd - 2) + (i, 0))

    def fspec(shape):
        nd = len(shape)
        return pl.BlockSpec(tuple(shape), lambda i, nd=nd: (0,) * nd)

    def body(*refs):
        outs = fn(*[r[...] for r in refs[:n_in]])
        if not isinstance(outs, (tuple, list)):
            outs = (outs,)
        out_refs = refs[n_in + n_dep:]
        for r, v in zip(out_refs[:n_t], outs[:n_t]):
            r[...] = v.astype(r.dtype)
        red_refs = out_refs[n_t:]
        if red_refs:
            @pl.when(pl.program_id(0) == 0)
            def _():
                for r in red_refs:
                    r[...] = jnp.zeros_like(r)

            for r, v in zip(red_refs, outs[n_t:]):
                r[...] += v.astype(r.dtype)

    res = pl.pallas_call(
        body,
        name=name,
        grid=grid,
        in_specs=[tspec(a.shape) for a in tiled] + [fspec(a.shape) for a in full]
        + [pl.BlockSpec(memory_space=pl.ANY)] * n_dep,
        out_specs=[tspec(o.shape) for o in out_tiled] + [fspec(o.shape) for o in out_red],
        out_shape=list(out_tiled) + list(out_red),
        compiler_params=_cparams(("arbitrary",) if out_red else ("parallel",)),
    )(*tiled, *full, *deps)
    return res


def _sum0(v):
    return jnp.sum(v, axis=0, keepdims=True)


def _sigmoid(v):
    return 1.0 / (1.0 + jnp.exp(-v))


def _rms(x):
    r = lax.rsqrt(jnp.mean(x * x, axis=-1, keepdims=True) + EPS)
    return x * r, r


def _rms_bwd(u, xr, r):
    return r * (u - xr * jnp.mean(u * xr, axis=-1, keepdims=True))


class _Ready:
    def __init__(self, g):
        self.g = g

    def mid(self, after):
        return []

    def get(self, after):
        return self.g


def _normmod(x_, gamma_, sc_, sh_):
    return _rms(x_)[0] * gamma_ * (1.0 + sc_) + sh_


def _row_blocks(tm, size=256):
    size = min(size, tm)
    return [slice(r, r + size) for r in range(0, tm, size)]


def _loss_head(x_, t_, g_):
    xr, r = _rms(x_)
    err = xr * g_ - t_
    dy = err * (1.0 / D_MODEL)
    return _rms_bwd(dy * g_, xr, r), _sum0(err * err), _sum0(dy * xr)


def _ffn_fwd(tag, x, gamma, sh, sc, gate, gu_src, base, down_src, down_idx, mid_after, h_pre=None,
             next_norm=None, head=None):
    s_len = x.shape[0]
    if h_pre is None:
        (h,) = _rowwise(
            f"{tag}_normmod", _normmod, [x], [gamma, sc, sh], [_sds((s_len, D_MODEL), BF16)], [], 256,
            deps=gu_src.mid(mid_after),
        )
        gdeps = []
    else:
        h, gdeps = h_pre, gu_src.mid(mid_after)
    g4 = gu_src.get(h)

    tm = min(ROW_TILE, s_len)
    def gate_up(h_ref, wg_ref, wu_ref, *rest):
        ab_ref, s_ref = rest[-2:]
        wg, wu = wg_ref[...], wu_ref[...]
        for rows in _row_blocks(tm):
            hv = h_ref[rows, :]
            a = _dot(hv, wg, "nt")
            b = _dot(hv, wu, "nt")
            ab_ref[0, rows, :] = a.astype(ab_ref.dtype)
            ab_ref[1, rows, :] = b.astype(ab_ref.dtype)
            s_ref[rows, :] = (a * _sigmoid(a) * b).astype(s_ref.dtype)

    ab4, s3 = pl.pallas_call(
        gate_up,
        name=f"{tag}_gate_up",
        grid=(s_len // tm, N_CHIPS),
        in_specs=[
            pl.BlockSpec((tm, D_MODEL), lambda i, c: (i, 0)),
            pl.BlockSpec((None, None, F_SHARD, D_MODEL), lambda i, c: (c, base, 0, 0)),
            pl.BlockSpec((None, None, F_SHARD, D_MODEL), lambda i, c: (c, base + 1, 0, 0)),
        ] + [pl.BlockSpec(memory_space=pl.ANY)] * len(gdeps),
        out_specs=[
            pl.BlockSpec((None, 2, tm, F_SHARD), lambda i, c: (c, 0, i, 0)),
            pl.BlockSpec((None, tm, F_SHARD), lambda i, c: (c, i, 0)),
        ],
        out_shape=[_sds((N_CHIPS, 2, s_len, F_SHARD), BF16), _sds((N_CHIPS, s_len, F_SHARD), BF16)],
        compiler_params=_cparams(("parallel", "parallel")),
    )(h, g4, g4, *gdeps)
    if down_src is None:
        g_down, ddeps = g4, ()
    else:
        ddeps = down_src.mid(ab4)
        g_down = down_src.get(s3)

    y = _mm(
        f"{tag}_down", "nn", (s_len // tm, 1, N_CHIPS),
        s3, (None, tm, F_SHARD), lambda i, j, k: (k, i, 0),
        g_down, (None, None, F_SHARD, D_MODEL), lambda i, j, k: (k, down_idx, 0, 0),
        _sds((s_len, D_MODEL), F32), (tm, D_MODEL), lambda i, j, k: (i, 0),
        (tm, D_MODEL), deps=ddeps,
    )

    saved = (g4, base, g_down, down_idx, h, ab4, s3, y)
    act = _sds((s_len, D_MODEL), F32)
    if head is not None:
        target, norm_final = head
        vec = _sds((1, D_MODEL), F32)
        out = _rowwise(
            f"{tag}_residual_head", lambda x_, y_, t_, g_, nf_: _loss_head(x_ + 0.5 * g_ * y_, t_, nf_),
            [x, y, target], [gate, norm_final], [act], [vec, vec], 256,
        )
    elif next_norm is not None:
        def residual_norm(x_, y_, g_, gamma_, sc_, sh_):
            x_out = x_ + 0.5 * g_ * y_
            return x_out, _normmod(x_out, gamma_, sc_, sh_)

        out = _rowwise(
            f"{tag}_residual_norm", residual_norm, [x, y], [gate] + list(next_norm),
            [act, _sds((s_len, D_MODEL), BF16)], [], 256,
        )
    else:
        out = _rowwise(f"{tag}_residual", lambda x_, y_, g_: x_ + 0.5 * g_ * y_, [x, y], [gate], [act], [], 256)
    return out, saved


def _normmod_bwd_call(name, dh_parts, x, dxo, gamma, sc, deps=()):
    s_len = x.shape[0]
    n_parts = len(dh_parts)

    def fn(*vals):
        dh = vals[0]
        for extra in vals[1:n_parts]:
            dh = dh + extra
        x_, dxo_, gamma_, sc_ = vals[n_parts:]
        xr, r = _rms(x_)
        n = xr * gamma_
        dn = dh * (1.0 + sc_)
        dx = dxo_ + _rms_bwd(dn * gamma_, xr, r)
        return dx, _sum0(dh * n), _sum0(dh), _sum0(dn * xr)

    vec = _sds((1, D_MODEL), F32)
    return _rowwise(
        name, fn, list(dh_parts) + [x, dxo], [gamma, sc], [_sds((s_len, D_MODEL), F32)], [vec, vec, vec], 256, deps=deps
    )


class _BwdHooks:
    def __init__(self, after_gate=None, after_swiglu=None, after_wdown=None, after_wgu=None, after_dh=None):
        def nothing(_):
            return []

        self.after_gate = after_gate or nothing
        self.after_swiglu = after_swiglu or nothing
        self.after_wdown = after_wdown or nothing
        self.after_wgu = after_wgu or nothing
        self.after_dh = after_dh or nothing


def _ffn_bwd(tag, dxo, x, gamma, sc, gate, saved, hooks, deps=()):
    g4, base, g_down, down_idx, h, ab4, s3, y = saved
    s_len = x.shape[0]
    vec = _sds((1, D_MODEL), F32)

    dy, dgate = _rowwise(
        f"{tag}_bwd_gate", lambda dxo_, y_, g_: (0.5 * g_ * dxo_, _sum0(0.5 * y_ * dxo_)),
        [dxo, y], [gate], [_sds((s_len, D_MODEL), BF16)], [vec], 256, deps=deps,
    )

    tm = min(ROW_TILE, s_len)

    def d_gate_up(dy_ref, wd_ref, ab_ref, o_ref):
        wd = wd_ref[...]
        for rows in _row_blocks(tm):
            ds = _dot(dy_ref[rows, :], wd, "nt")
            a = ab_ref[0, rows, :].astype(F32)
            b = ab_ref[1, rows, :].astype(F32)
            sig = _sigmoid(a)
            o_ref[0, rows, :] = (ds * b * sig * (1.0 + a * (1.0 - sig))).astype(o_ref.dtype)
            o_ref[1, rows, :] = (ds * a * sig).astype(o_ref.dtype)

    ab_spec = pl.BlockSpec((None, 2, tm, F_SHARD), lambda i, c: (c, 0, i, 0))
    dab4 = pl.pallas_call(
        d_gate_up,
        name=f"{tag}_bwd_dgate_up",
        grid=(s_len // tm, N_CHIPS),
        in_specs=[
            pl.BlockSpec((tm, D_MODEL), lambda i, c: (i, 0)),
            pl.BlockSpec((None, None, F_SHARD, D_MODEL), lambda i, c: (c, down_idx, 0, 0)),
            ab_spec,
        ],
        out_specs=ab_spec,
        out_shape=_sds(ab4.shape, BF16),
        compiler_params=_cparams(("parallel", "parallel")),
    )(dy, g_down, ab4)

    tk = tm
    gb_down = _mm(
        f"{tag}_bwd_wdown", "tn", (N_CHIPS, 1, s_len // tk),
        s3, (None, tk, F_SHARD), lambda i, j, k: (i, k, 0),
        dy, (tk, D_MODEL), lambda i, j, k: (k, 0),
        _sds((N_CHIPS, 1, F_SHARD, D_MODEL), BF16), (None, None, F_SHARD, D_MODEL), lambda i, j, k: (i, 0, 0, 0),
        (F_SHARD, D_MODEL), deps=hooks.after_swiglu(dab4),
    )
    gb_gu = _mm(
        f"{tag}_bwd_wgu", "tn", (2 * N_CHIPS, 1, s_len // tk),
        dab4, (None, None, tk, F_SHARD), lambda i, j, k: (i % N_CHIPS, i // N_CHIPS, k, 0),
        h, (tk, D_MODEL), lambda i, j, k: (k, 0),
        _sds((N_CHIPS, 2, F_SHARD, D_MODEL), BF16), (None, None, F_SHARD, D_MODEL),
        lambda i, j, k: (i % N_CHIPS, i // N_CHIPS, 0, 0),
        (F_SHARD, D_MODEL), deps=hooks.after_wdown(gb_down),
    )
    assert base % 2 == 0
    dh_deps = hooks.after_wgu(gb_gu)

    def d_h(dab_ref, w_ref, *rest):
        o_ref, acc_ref = rest[-2:]
        c = pl.program_id(1)
        part = _dot(dab_ref[0], w_ref[0], "nn") + _dot(dab_ref[1], w_ref[1], "nn")

        @pl.when(c == 0)
        def _():
            acc_ref[...] = part

        @pl.when((c > 0) & (c < N_CHIPS - 1))
        def _():
            acc_ref[...] += part

        @pl.when(c == N_CHIPS - 1)
        def _():
            o_ref[...] = acc_ref[...] + part

    dh = pl.pallas_call(
        d_h,
        name=f"{tag}_bwd_dh",
        grid=(s_len // tm, N_CHIPS),
        in_specs=[
            pl.BlockSpec((None, 2, tm, F_SHARD), lambda i, c: (c, 0, i, 0)),
            pl.BlockSpec((None, 2, F_SHARD, D_MODEL), lambda i, c: (c, base // 2, 0, 0)),
        ] + [pl.BlockSpec(memory_space=pl.ANY)] * len(dh_deps),
        out_specs=pl.BlockSpec((tm, D_MODEL), lambda i, c: (i, 0)),
        out_shape=_sds((s_len, D_MODEL), F32),
        scratch_shapes=[pltpu.VMEM((tm, D_MODEL), F32)],
        compiler_params=_cparams(("parallel", "arbitrary")),
    )(dab4, g4, *dh_deps)
    dx, dsc, dsh, dgamma = _normmod_bwd_call(
        f"{tag}_bwd_normmod", [dh], x, dxo, gamma, sc, deps=hooks.after_dh(dh)
    )
    return dx, (gb_down, gb_gu), (dsh, dsc, dgate, dgamma)


def _bucket_map():
    qi = np.arange(WINDOW)[:, None]
    kj = np.arange(2 * WINDOW)[None, :]
    dist = qi + WINDOW - kj
    band = (dist >= 0) & (dist < WINDOW)
    max_exact = NUM_BUCKETS // 2
    n = np.maximum(dist, 0)
    large = []
    for dt in (np.float32, np.float64):
        nf = np.maximum(n, 1).astype(dt)
        val = np.log(nf / dt(max_exact)) / dt(math.log(MAX_DISTANCE / max_exact)) * dt(NUM_BUCKETS - max_exact)
        large.append(np.minimum(max_exact + val.astype(np.int32), NUM_BUCKETS - 1))
    assert np.array_equal(large[0], large[1])
    bucket = np.where(n < max_exact, n, large[0])
    return np.where(band, bucket, -1).astype(np.int32).reshape(1, -1)


def _bias_expand(rel_bias_t, bkt):
    n = bkt.shape[1]

    def body(rb_ref, bkt_ref, o_ref):
        onehot = (lax.broadcasted_iota(jnp.int32, (NUM_BUCKETS, n), 0) == bkt_ref[...]).astype(F32)
        o_ref[...] = lax.dot_general(
            rb_ref[...], onehot, _DN["nn"], precision=lax.Precision.HIGHEST, preferred_element_type=F32
        )

    return pl.pallas_call(
        body, name="bias_expand", out_shape=_sds((SWA_HEADS, n), F32), compiler_params=_cparams()
    )(rel_bias_t, bkt)


def _bias_reduce(dbias_flat, bkt):
    n = bkt.shape[1]

    def body(db_ref, bkt_ref, o_ref):
        onehot = (lax.broadcasted_iota(jnp.int32, (NUM_BUCKETS, n), 0) == bkt_ref[...]).astype(F32)
        o_ref[...] = lax.dot_general(
            db_ref[...], onehot, _DN["nt"], precision=lax.Precision.HIGHEST, preferred_element_type=F32
        )

    return pl.pallas_call(
        body, name="bias_reduce", out_shape=_sds((SWA_HEADS, NUM_BUCKETS), F32), compiler_params=_cparams()
    )(dbias_flat, bkt)


_SWA_SCALE = SWA_HEAD_DIM ** -0.5
_NEG = -1e30


def _swa_in_specs():
    w = WINDOW
    n_q = SWA_HEADS * SWA_HEAD_DIM
    n_kv = 2 * SWA_KV_HEADS * SWA_HEAD_DIM
    prev = lambda n: jnp.maximum(n - 1, 0)
    return [
        pl.BlockSpec((w, n_q), lambda n: (n, 0)),
        pl.BlockSpec((w, n_kv), lambda n: (prev(n), n_q // n_kv)),
        pl.BlockSpec((w, n_kv), lambda n: (n, n_q // n_kv)),
        pl.BlockSpec((SWA_HEADS, w, 2 * w), lambda n: (0, 0, 0)),
        pl.BlockSpec((SWA_HEADS, w, 1), lambda n: (0, 0, 0)),
    ]


def _head_cols(v, first, count):
    hd = SWA_HEAD_DIM
    return jnp.stack([v[:, (first + i) * hd:(first + i + 1) * hd] for i in range(count)])


def _swa_heads(q_ref, kvp_ref, kvc_ref):
    g, w, hd = SWA_GROUP, WINDOW, SWA_HEAD_DIM
    q = q_ref[...].astype(F32)
    kv = jnp.concatenate([kvp_ref[...], kvc_ref[...]], axis=0).astype(F32)
    heads = []
    for j in range(SWA_KV_HEADS):
        qj = _head_cols(q, g * j, g).reshape(g * w, hd)
        heads.append((qj, _head_cols(kv, j, 1)[0], _head_cols(kv, SWA_KV_HEADS + j, 1)[0]))
    return heads


def _swa_scores(q, kk, bias, n):
    g, w = SWA_GROUP, WINDOW
    s = (_dot(q, kk, "nt") * _SWA_SCALE).reshape(g, w, 2 * w) + bias
    qi = lax.broadcasted_iota(jnp.int32, (w, 2 * w), 0)
    kj = lax.broadcasted_iota(jnp.int32, (w, 2 * w), 1)
    dist = qi + w - kj
    valid = ((dist >= 0) & (dist < w) & ((n > 0) | (kj >= w)))[None]
    return jnp.where(valid, s, _NEG), valid


def _swa_fwd(swa2, bias, sinkb):
    s_len = swa2.shape[0]
    g, w, hd = SWA_GROUP, WINDOW, SWA_HEAD_DIM

    def body(q_ref, kvp_ref, kvc_ref, bias_ref, sink_ref, o_ref, lse_ref):
        n = pl.program_id(0)
        outs = []
        for j, (q, kk, vv) in enumerate(_swa_heads(q_ref, kvp_ref, kvc_ref)):
            hs = slice(g * j, g * (j + 1))
            s, valid = _swa_scores(q, kk, bias_ref[hs], n)
            sink = sink_ref[hs]
            m = jnp.maximum(jnp.max(s, axis=-1, keepdims=True), sink)
            p = jnp.where(valid, jnp.exp(s - m), 0.0)
            l = jnp.sum(p, axis=-1, keepdims=True) + jnp.exp(sink - m)
            o = _dot(p.reshape(g * w, 2 * w), vv, "nn").reshape(g, w, hd) / l
            outs += [o[i] for i in range(g)]
            lse_ref[hs] = m + jnp.log(l)
        o_ref[...] = jnp.concatenate(outs, axis=-1).astype(o_ref.dtype)

    n_q = SWA_HEADS * hd
    return pl.pallas_call(
        body,
        name="swa_fwd",
        grid=(s_len // w,),
        in_specs=_swa_in_specs(),
        out_specs=[
            pl.BlockSpec((w, n_q), lambda n: (n, 0)),
            pl.BlockSpec((SWA_HEADS, w, 1), lambda n: (0, n, 0)),
        ],
        out_shape=[_sds((s_len, n_q), BF16), _sds((SWA_HEADS, s_len, 1), F32)],
        compiler_params=_cparams(("parallel",)),
    )(swa2, swa2, swa2, bias, sinkb)


def _swa_bwd(swa2, bias, sinkb, cat, dcat, lse):
    s_len = swa2.shape[0]
    g, w, hd = SWA_GROUP, WINDOW, SWA_HEAD_DIM

    def body(q_ref, kvp_ref, kvc_ref, bias_ref, sink_ref, o_ref, do_ref, lse_ref,
             dq_ref, dkva_ref, dkvb_ref, dbias_ref, dsink_ref):
        n = pl.program_id(0)

        @pl.when(n == 0)
        def _():
            dbias_ref[...] = jnp.zeros_like(dbias_ref)
            dsink_ref[...] = jnp.zeros_like(dsink_ref)

        o_all = o_ref[...].astype(F32)
        do_all = do_ref[...].astype(F32)
        dqs, dks, dvs = [], [], []
        for j, (q, kk, vv) in enumerate(_swa_heads(q_ref, kvp_ref, kvc_ref)):
            hs = slice(g * j, g * (j + 1))
            s, valid = _swa_scores(q, kk, bias_ref[hs], n)
            lse_v = lse_ref[hs]
            p = jnp.where(valid, jnp.exp(s - lse_v), 0.0)
            do = _head_cols(do_all, g * j, g)
            delta = jnp.sum(do * _head_cols(o_all, g * j, g), axis=-1, keepdims=True)
            do2 = do.reshape(g * w, hd)
            dp = _dot(do2, vv, "nt").reshape(g, w, 2 * w)
            ds = p * (dp - delta)
            dbias_ref[hs] += ds
            dsink_ref[hs] -= jnp.exp(sink_ref[hs] - lse_v) * delta
            ds2 = ds.reshape(g * w, 2 * w)
            dq = (_dot(ds2, kk, "nn") * _SWA_SCALE).reshape(g, w, hd)
            dqs += [dq[i] for i in range(g)]
            dks.append(_dot(ds2, q, "tn") * _SWA_SCALE)
            dvs.append(_dot(p.reshape(g * w, 2 * w), do2, "tn"))
        dq_ref[...] = jnp.concatenate(dqs, axis=-1).astype(dq_ref.dtype)
        dkv = jnp.concatenate(dks + dvs, axis=-1)
        dkvb_ref[...] = dkv[:w]
        dkva_ref[...] = dkv[w:]

    n_q = SWA_HEADS * hd
    n_kv = 2 * SWA_KV_HEADS * hd
    q_spec = pl.BlockSpec((w, n_q), lambda n: (n, 0))
    kv_spec = pl.BlockSpec((w, n_kv), lambda n: (n, 0))
    return pl.pallas_call(
        body,
        name="swa_bwd",
        grid=(s_len // w,),
        in_specs=_swa_in_specs() + [q_spec, q_spec, pl.BlockSpec((SWA_HEADS, w, 1), lambda n: (0, n, 0))],
        out_specs=[
            q_spec, kv_spec, kv_spec,
            pl.BlockSpec((SWA_HEADS, w, 2 * w), lambda n: (0, 0, 0)),
            pl.BlockSpec((SWA_HEADS, w, 1), lambda n: (0, 0, 0)),
        ],
        out_shape=[
            _sds((s_len, n_q), BF16), _sds((s_len, n_kv), F32), _sds((s_len, n_kv), F32),
            _sds((SWA_HEADS, w, 2 * w), F32), _sds((SWA_HEADS, w, 1), F32),
        ],
        compiler_params=_cparams(("arbitrary",)),
    )(swa2, swa2, swa2, bias, sinkb, cat, dcat, lse)


_MLA_SCALE = MLA_QK ** -0.5
_MLA_TQ = 256
_MLA_PAIR = 2


def _mla_mask(i, tq, n_keys):
    row = i * tq + lax.broadcasted_iota(jnp.int32, (tq, n_keys), 0)
    col = lax.broadcasted_iota(jnp.int32, (tq, n_keys), 1)
    return col <= row


def _per_query_block(i, n_blocks, fn):
    for ii in range(n_blocks):
        pl.when(i == ii)(functools.partial(fn, ii))


def _mla_fwd(q4, k4, v4):
    s_len = q4.shape[1]
    tq = min(_MLA_TQ, s_len)

    def body(q_ref, k_ref, v_ref, o_ref, lse_ref):
        def block(ii):
            n_keys = (ii + 1) * tq
            mask = _mla_mask(ii, tq, n_keys)
            for hh in range(_MLA_PAIR):
                s = jnp.where(mask, _dot(q_ref[hh], k_ref[hh, :n_keys, :], "nt") * _MLA_SCALE, _NEG)
                m = jnp.max(s, axis=-1, keepdims=True)
                p = jnp.exp(s - m)
                l = jnp.sum(p, axis=-1, keepdims=True)
                o_ref[:, hh * MLA_V:(hh + 1) * MLA_V] = (_dot(p, v_ref[hh, :n_keys, :], "nn") / l).astype(o_ref.dtype)
                lse_ref[hh] = m + jnp.log(l)

        _per_query_block(pl.program_id(1), s_len // tq, block)

    return pl.pallas_call(
        body,
        name="mla_fwd",
        grid=(MLA_HEADS // _MLA_PAIR, s_len // tq),
        in_specs=[
            pl.BlockSpec((_MLA_PAIR, tq, MLA_QK), lambda h, i: (h, i, 0)),
            pl.BlockSpec((_MLA_PAIR, s_len, MLA_QK), lambda h, i: (h, 0, 0)),
            pl.BlockSpec((_MLA_PAIR, s_len, MLA_V), lambda h, i: (h, 0, 0)),
        ],
        out_specs=[
            pl.BlockSpec((tq, _MLA_PAIR * MLA_V), lambda h, i: (i, h)),
            pl.BlockSpec((_MLA_PAIR, tq, 1), lambda h, i: (h, i, 0)),
        ],
        out_shape=[_sds((s_len, MLA_HEADS * MLA_V), BF16), _sds((MLA_HEADS, s_len, 1), F32)],
        compiler_params=_cparams(("parallel", "parallel")),
    )(q4, k4, v4)


def _mla_bwd(q4, k4, v4, cat, dcat, lse):
    s_len = q4.shape[1]
    tq = min(_MLA_TQ, s_len)
    first = SWA_HEADS * SWA_HEAD_DIM // (_MLA_PAIR * MLA_V)

    def body(q_ref, k_ref, v_ref, o_ref, do_ref, lse_ref, dq_ref, dk_ref, dv_ref):
        i = pl.program_id(1)

        @pl.when(i == 0)
        def _():
            dk_ref[...] = jnp.zeros_like(dk_ref)
            dv_ref[...] = jnp.zeros_like(dv_ref)

        def block(ii):
            n_keys = (ii + 1) * tq
            mask = _mla_mask(ii, tq, n_keys)
            for hh in range(_MLA_PAIR):
                cols = slice(hh * MLA_V, (hh + 1) * MLA_V)
                q, k, v, do = q_ref[hh], k_ref[hh, :n_keys, :], v_ref[hh, :n_keys, :], do_ref[:, cols]
                s = jnp.where(mask, _dot(q, k, "nt") * _MLA_SCALE, _NEG)
                p = jnp.where(mask, jnp.exp(s - lse_ref[hh]), 0.0)
                delta = jnp.sum(do.astype(F32) * o_ref[:, cols].astype(F32), axis=-1, keepdims=True)
                ds = (p * (_dot(do, v, "nt") - delta) * _MLA_SCALE).astype(BF16)
                dq_ref[hh] = _dot(ds, k, "nn")
                dk_ref[hh, :n_keys, :] += _dot(ds, q, "tn")
                dv_ref[hh, :n_keys, :] += _dot(p, do, "tn")

        _per_query_block(i, s_len // tq, block)

    return pl.pallas_call(
        body,
        name="mla_bwd",
        grid=(MLA_HEADS // _MLA_PAIR, s_len // tq),
        in_specs=[
            pl.BlockSpec((_MLA_PAIR, tq, MLA_QK), lambda h, i: (h, i, 0)),
            pl.BlockSpec((_MLA_PAIR, s_len, MLA_QK), lambda h, i: (h, 0, 0)),
            pl.BlockSpec((_MLA_PAIR, s_len, MLA_V), lambda h, i: (h, 0, 0)),
            pl.BlockSpec((tq, _MLA_PAIR * MLA_V), lambda h, i: (i, first + h)),
            pl.BlockSpec((tq, _MLA_PAIR * MLA_V), lambda h, i: (i, first + h)),
            pl.BlockSpec((_MLA_PAIR, tq, 1), lambda h, i: (h, i, 0)),
        ],
        out_specs=[
            pl.BlockSpec((_MLA_PAIR, tq, MLA_QK), lambda h, i: (h, i, 0)),
            pl.BlockSpec((_MLA_PAIR, s_len, MLA_QK), lambda h, i: (h, 0, 0)),
            pl.BlockSpec((_MLA_PAIR, s_len, MLA_V), lambda h, i: (h, 0, 0)),
        ],
        out_shape=[
            _sds((MLA_HEADS, s_len, MLA_QK), F32),
            _sds((MLA_HEADS, s_len, MLA_QK), F32),
            _sds((MLA_HEADS, s_len, MLA_V), F32),
        ],
        compiler_params=_cparams(("parallel", "arbitrary")),
    )(q4, k4, v4, cat, dcat, lse)


def _mla_split(pm):
    q_lat = pm[:, :MLA_Q_RANK]
    kv_lat = pm[:, MLA_Q_RANK:MLA_Q_RANK + MLA_KV_RANK]
    kr = pm[:, MLA_Q_RANK + MLA_KV_RANK:MLA_Q_RANK + MLA_KV_RANK + MLA_ROPE]
    kr_sw = pm[:, MLA_Q_RANK + MLA_KV_RANK + MLA_ROPE:]
    return q_lat, kv_lat, kr, kr_sw


def _mla_proj(pm, cos2, sin2, q_norm, kv_norm, wq_ext, wkv):
    s_len = pm.shape[0]

    def fn(pm_, cos_, sin_, qg, kvg, wq, wk):
        q_lat, kv_lat, kr, kr_sw = _mla_split(pm_)
        nq = (_rms(q_lat)[0] * qg).astype(BF16)
        nkv = (_rms(kv_lat)[0] * kvg).astype(BF16)
        k_rot = kr * cos_ + kr_sw * sin_
        qs, ks, vs = [], [], []
        for h in range(MLA_HEADS):
            qe = _dot(nq, wq[h], "nt")
            q_rot = qe[:, MLA_NOPE:MLA_QK] * cos_ + qe[:, MLA_QK:] * sin_
            qs.append(jnp.concatenate([qe[:, :MLA_NOPE], q_rot], axis=-1))
            kve = _dot(nkv, wk[h], "nt")
            ks.append(jnp.concatenate([kve[:, :MLA_NOPE], k_rot], axis=-1))
            vs.append(kve[:, MLA_NOPE:])
        return jnp.stack(qs), jnp.stack(ks), jnp.stack(vs)

    return _rowwise(
        "mla_proj", fn, [pm, cos2, sin2], [q_norm, kv_norm, wq_ext, wkv],
        [_sds((MLA_HEADS, s_len, MLA_QK), BF16), _sds((MLA_HEADS, s_len, MLA_QK), BF16),
         _sds((MLA_HEADS, s_len, MLA_V), BF16)], [], 256,
    )


def _mla_proj_bwd(pm, cos2, sin2, dq4, dk4, dv4, q_norm, kv_norm, wq_ext, wkv):
    s_len = pm.shape[0]

    def fn(pm_, cos_, sin_, dq, dk, dv, qg, kvg, wq, wk):
        q_lat, kv_lat, _, _ = _mla_split(pm_)
        xq, rq = _rms(q_lat)
        xkv, rkv = _rms(kv_lat)
        nq = (xq * qg).astype(BF16)
        nkv = (xkv * kvg).astype(BF16)
        dnq = jnp.zeros_like(q_lat)
        dnkv = jnp.zeros_like(kv_lat)
        dkr = jnp.zeros_like(cos_)
        dwq, dwk = [], []
        for h in range(MLA_HEADS):
            dy = dq[h][:, MLA_NOPE:]
            dqe = jnp.concatenate([dq[h][:, :MLA_NOPE], dy * cos_, dy * sin_], axis=-1).astype(BF16)
            dnq = dnq + _dot(dqe, wq[h], "nn")
            dwq.append(_dot(dqe, nq, "tn"))
            dkve = jnp.concatenate([dk[h][:, :MLA_NOPE], dv[h]], axis=-1).astype(BF16)
            dnkv = dnkv + _dot(dkve, wk[h], "nn")
            dwk.append(_dot(dkve, nkv, "tn"))
            dkr = dkr + dk[h][:, MLA_NOPE:]
        dq_lat = _rms_bwd(dnq * qg, xq, rq)
        dkv_lat = _rms_bwd(dnkv * kvg, xkv, rkv)
        dpm = jnp.concatenate([dq_lat, dkv_lat, dkr * cos_, dkr * sin_], axis=-1)
        return dpm, _sum0(dnq * xq), _sum0(dnkv * xkv), jnp.stack(dwq), jnp.stack(dwk)

    return _rowwise(
        "mla_proj_bwd", fn, [pm, cos2, sin2, dq4, dk4, dv4], [q_norm, kv_norm, wq_ext, wkv],
        [_sds((s_len, N_MLA_COLS), BF16)],
        [_sds((1, MLA_Q_RANK), F32), _sds((1, MLA_KV_RANK), F32),
         _sds(wq_ext.shape, F32), _sds(wkv.shape, F32)], 256,
    )


def _rope_tables(s_len):
    inv = ROPE_THETA ** (-jnp.arange(0, MLA_ROPE, 2, dtype=F32) / MLA_ROPE)
    ang = jnp.arange(s_len, dtype=F32)[:, None] * inv[None, :]
    cos, sin = jnp.cos(ang), jnp.sin(ang)
    return jnp.concatenate([cos, cos], axis=-1), jnp.concatenate([-sin, sin], axis=-1)


def _mix_weights(g_mix):
    rest = g_mix[:, 0]
    w_in_t = rest[:, O_IN:O_IN + R_IN].reshape(D_IN, D_MODEL)
    w_o = rest[:, O_O:O_O + R_O].reshape(D_MODEL, D_MODEL)
    w_uq_t = rest[:, O_UQ:O_UQ + R_UQ].reshape(MLA_HEADS, MLA_QK, MLA_Q_RANK)
    w_ukv_t = rest[:, O_UKV:O_UKV + R_UKV].reshape(MLA_HEADS, MLA_NOPE + MLA_V, MLA_KV_RANK)
    half = MLA_ROPE // 2
    w_swa = w_in_t[:N_SWA_COLS]
    w_mla = jnp.concatenate([w_in_t[N_SWA_COLS:], w_in_t[D_IN - half:], w_in_t[D_IN - MLA_ROPE:D_IN - half]], axis=0)
    wq_ext = jnp.concatenate([w_uq_t, w_uq_t[:, MLA_QK - half:], w_uq_t[:, MLA_NOPE:MLA_QK - half]], axis=1)
    return w_swa, w_mla, w_o, wq_ext, w_ukv_t


def _mix_fwd(x, h, gate, mix_src, q_norm, kv_norm, bias, sinkb, cos2, sin2, next_norm):
    s_len = x.shape[0]
    tm = min(ROW_TILE, s_len)
    deps = mix_src.mid(x)
    weights = _mix_weights(mix_src.get(h))
    w_swa, w_mla, w_o, wq_ext, wkv = weights
    swa2 = _mm(
        "mix_proj_swa", "nt", (s_len // tm, 1, 1),
        h, (tm, D_MODEL), lambda i, j, k: (i, 0),
        w_swa, (N_SWA_COLS, D_MODEL), lambda i, j, k: (0, 0),
        _sds((s_len, N_SWA_COLS), BF16), (tm, N_SWA_COLS), lambda i, j, k: (i, 0),
        (tm, N_SWA_COLS), deps=deps,
    )
    pm = _mm(
        "mix_proj_mla", "nt", (s_len // tm, 1, 1),
        h, (tm, D_MODEL), lambda i, j, k: (i, 0),
        w_mla, (N_MLA_COLS, D_MODEL), lambda i, j, k: (0, 0),
        _sds((s_len, N_MLA_COLS), F32), (tm, N_MLA_COLS), lambda i, j, k: (i, 0),
        (tm, N_MLA_COLS),
    )
    q4, k4, v4 = _mla_proj(pm, cos2, sin2, q_norm, kv_norm, wq_ext, wkv)
    oa, lse_a = _swa_fwd(swa2, bias, sinkb)
    ob, lse_b = _mla_fwd(q4, k4, v4)
    cat = jnp.concatenate([oa, ob], axis=1)
    z = _mm(
        "mix_out", "nn", (s_len // tm, 1, 1),
        cat, (tm, D_MODEL), lambda i, j, k: (i, 0),
        w_o, (D_MODEL, D_MODEL), lambda i, j, k: (0, 0),
        _sds((s_len, D_MODEL), F32), (tm, D_MODEL), lambda i, j, k: (i, 0),
        (tm, D_MODEL),
    )
    def residual_norm(x_, z_, g_, gamma_, sc_, sh_):
        x_out = x_ + g_ * z_
        return x_out, _normmod(x_out, gamma_, sc_, sh_)

    out = _rowwise(
        "mix_residual_norm", residual_norm, [x, z], [gate] + list(next_norm),
        [_sds((s_len, D_MODEL), F32), _sds((s_len, D_MODEL), BF16)], [], 256,
    )
    return out, (weights, h, swa2, pm, q4, k4, v4, cat, lse_a, lse_b, z)


def _mix_bwd(dxo, x, gamma, sc, gate, q_norm, kv_norm, bias, sinkb, cos2, sin2, saved, hooks):
    weights, h, swa2, pm, q4, k4, v4, cat, lse_a, lse_b, z = saved
    w_swa, w_mla, w_o, wq_ext, wkv = weights
    s_len = x.shape[0]
    tm = tk = min(ROW_TILE, s_len)
    vec = _sds((1, D_MODEL), F32)
    n_proj = N_SWA_COLS + N_MLA_COLS
    half_d = D_MODEL // 2

    dz, dgate = _rowwise(
        "mix_bwd_gate", lambda dxo_, z_, g_: (g_ * dxo_, _sum0(z_ * dxo_)),
        [dxo, z], [gate], [_sds((s_len, D_MODEL), BF16)], [vec], 256,
    )
    dw_o = _mm(
        "mix_bwd_wo", "tn", (2, 1, s_len // tk),
        cat, (tk, half_d), lambda i, j, k: (k, i),
        dz, (tk, D_MODEL), lambda i, j, k: (k, 0),
        _sds((D_MODEL, D_MODEL), F32), (half_d, D_MODEL), lambda i, j, k: (i, 0),
        (half_d, D_MODEL),
    )
    dcat = _mm(
        "mix_bwd_dcat", "nt", (s_len // tm, 1, 1),
        dz, (tm, D_MODEL), lambda i, j, k: (i, 0),
        w_o, (D_MODEL, D_MODEL), lambda i, j, k: (0, 0),
        _sds((s_len, D_MODEL), BF16), (tm, D_MODEL), lambda i, j, k: (i, 0),
        (tm, D_MODEL), deps=hooks.after_gate(dz),
    )
    dq_a, dkva, dkvb, dbias, dsink = _swa_bwd(swa2, bias, sinkb, cat, dcat, lse_a)
    dq4, dk4, dv4 = _mla_bwd(q4, k4, v4, cat, dcat, lse_b)
    dpm, dq_norm, dkv_norm, dwq_ext, dwkv = _mla_proj_bwd(pm, cos2, sin2, dq4, dk4, dv4, q_norm, kv_norm, wq_ext, wkv)

    dkv = dkva + jnp.concatenate([dkvb[WINDOW:], jnp.zeros_like(dkvb[:WINDOW])], axis=0)
    dproj = jnp.concatenate([dq_a, dkv.astype(BF16), dpm], axis=1)
    w_proj = jnp.concatenate([w_swa, w_mla], axis=0)

    dw_proj = _mm(
        "mix_bwd_win", "tn", (n_proj // 256, 1, s_len // tk),
        dproj, (tk, 256), lambda i, j, k: (k, i),
        h, (tk, D_MODEL), lambda i, j, k: (k, 0),
        _sds((n_proj, D_MODEL), F32), (256, D_MODEL), lambda i, j, k: (i, 0),
        (256, D_MODEL),
    )
    dw_swa, dw_mla = dw_proj[:N_SWA_COLS], dw_proj[N_SWA_COLS:]
    dh = _mm(
        "mix_bwd_dh", "nn", (s_len // tm, 1, 1),
        dproj, (tm, n_proj), lambda i, j, k: (i, 0),
        w_proj, (n_proj, D_MODEL), lambda i, j, k: (0, 0),
        _sds((s_len, D_MODEL), F32), (tm, D_MODEL), lambda i, j, k: (i, 0),
        (tm, D_MODEL),
    )
    dx, dsc, dsh, dgamma = _normmod_bwd_call("mix_bwd_normmod", [dh], x, dxo, gamma, sc)

    half = MLA_ROPE // 2
    n_mla_in = D_IN - N_SWA_COLS
    dw_mla_base = dw_mla[:n_mla_in]
    dw_mla_base = dw_mla_base.at[n_mla_in - half:].add(dw_mla[n_mla_in:n_mla_in + half])
    dw_mla_base = dw_mla_base.at[n_mla_in - MLA_ROPE:n_mla_in - half].add(dw_mla[n_mla_in + half:])
    dw_in_t = jnp.concatenate([dw_swa, dw_mla_base], axis=0)
    dw_uq_t = dwq_ext[:, :MLA_QK]
    dw_uq_t = dw_uq_t.at[:, MLA_QK - half:].add(dwq_ext[:, MLA_QK:MLA_QK + half])
    dw_uq_t = dw_uq_t.at[:, MLA_NOPE:MLA_QK - half].add(dwq_ext[:, MLA_QK + half:])
    rest = jnp.concatenate(
        [
            dw_in_t.reshape(N_CHIPS, R_IN, D_MODEL),
            dw_o.reshape(N_CHIPS, R_O, D_MODEL),
            dw_uq_t.reshape(N_CHIPS, R_UQ, D_MODEL),
            dwkv.reshape(N_CHIPS, R_UKV, D_MODEL),
            jnp.zeros((N_CHIPS, F_SHARD - O_PAD, D_MODEL), F32),
        ],
        axis=1,
    ).astype(BF16)
    return dx, rest, (dsh, dsc, dgate, dgamma), dq_norm, dkv_norm, dbias, dsink


def _device_step(x, target, mod, gu1_src, down1_src, mix_src, ffn2_src, norms, q_norm, kv_norm, sinks, rel_bias,
                 hooks2=None, hooks_mix=None, mix_done=None, hooks1=None):
    s_len = x.shape[0]
    sh1, sc1, g1, sh2, sc2, g2, sh3, sc3, g3 = [mod[:, i * D_MODEL:(i + 1) * D_MODEL] for i in range(N_MOD)]
    n1, n2, n3, nf = norms
    bkt = jnp.asarray(_bucket_map())
    bias = _bias_expand(rel_bias.T, bkt).reshape(SWA_HEADS, WINDOW, 2 * WINDOW)
    sinkb = jnp.broadcast_to(sinks.reshape(SWA_HEADS, 1, 1), (SWA_HEADS, WINDOW, 1))
    cos2, sin2 = _rope_tables(s_len)

    (x1, h2), saved1 = _ffn_fwd(
        "ffn1", x, n1, sh1, sc1, g1, gu1_src, 0, down1_src, 0, sh1, next_norm=(n2, sc2, sh2)
    )
    (x2, h3), saved2 = _mix_fwd(
        x1, h2, g2, mix_src, q_norm, kv_norm, bias, sinkb, cos2, sin2, next_norm=(n3, sc3, sh3)
    )
    (dx3, sq, dnf), saved3 = _ffn_fwd(
        "ffn2", x2, n3, sh3, sc3, g3, ffn2_src, 0, None, 2, x2, h_pre=h3, head=(target, nf)
    )
    loss_part = (0.5 / D_MODEL) * jnp.sum(sq)

    dx2, gb2, (dsh3, dsc3, dg3, dn3) = _ffn_bwd("ffn2", dx3, x2, n3, sc3, g3, saved3, hooks2 or _BwdHooks())
    dx1, rest, (dsh2, dsc2, dg2, dn2), dq_norm, dkv_norm, dbias, dsink = _mix_bwd(
        dx2, x1, n2, sc2, g2, q_norm, kv_norm, bias, sinkb, cos2, sin2, saved2, hooks_mix or _BwdHooks()
    )
    rest = rest[:, None]
    deps1 = mix_done(dx1, rest) if mix_done is not None else []
    dx0, gb1, (dsh1, dsc1, dg1, dn1) = _ffn_bwd(
        "ffn1", dx1, x, n1, sc1, g1, saved1, hooks1 or _BwdHooks(), deps=deps1
    )

    dmod = jnp.concatenate([dsh1, dsc1, dg1, dsh2, dsc2, dg2, dsh3, dsc3, dg3], axis=-1)
    drel = _bias_reduce(dbias.reshape(SWA_HEADS, -1), bkt).T
    dsinks = jnp.sum(dsink, axis=(1, 2)).reshape(1, SWA_HEADS)
    small = (dn1, dn2, dn3, dnf, dq_norm, dkv_norm, dsinks, drel)
    return loss_part, dx0, (gb1, gb2, rest), dmod, small


_MESH = pl.DeviceIdType.MESH


def _place():
    return lax.axis_index("x"), lax.axis_index("y"), lax.axis_index("c")


def _other_chips(x, y):
    return [(1 - x, y), (x, 1 - y), (1 - x, 1 - y)]


def _allgather_small(name, blk):
    m_per, n = blk.shape

    def body(x_ref, out_ref, send_sems, recv_sems, local_sem):
        x, y, c = _place()
        me, sibling = (x, y, c), (x, y, 1 - c)
        chips = _other_chips(x, y)

        def rows(px, py, pc):
            return out_ref.at[pl.ds((4 * px + 2 * py + pc) * m_per, m_per), :]

        def copy(k, block, to, src=None):
            return pltpu.make_async_remote_copy(
                src_ref=rows(*block) if src is None else src, dst_ref=rows(*block),
                send_sem=send_sems.at[k], recv_sem=recv_sems.at[k], device_id=to, device_id_type=_MESH,
            )

        mine = pltpu.make_async_copy(x_ref, rows(*me), local_sem)
        mine.start()
        first = [copy(0, me, sibling, src=x_ref)]
        first += [copy(1 + j, me, (*chip, c), src=x_ref) for j, chip in enumerate(chips)]
        for cp in first:
            cp.start()
        passed = [copy(4 + j, (*chip, c), sibling) for j, chip in enumerate(chips)]
        for j, chip in enumerate(chips):
            copy(1 + j, (*chip, c), me).wait_recv()
            passed[j].start()
        copy(0, sibling, me).wait_recv()
        for j, chip in enumerate(chips):
            copy(4 + j, (*chip, 1 - c), me).wait_recv()
        for cp in first + passed:
            cp.wait_send()
        mine.wait()

    return pl.pallas_call(
        body,
        name=name,
        out_shape=_sds((N_DEV * m_per, n), blk.dtype),
        in_specs=[pl.BlockSpec(memory_space=pltpu.VMEM)],
        out_specs=pl.BlockSpec(memory_space=pltpu.VMEM),
        scratch_shapes=[pltpu.SemaphoreType.DMA((7,)), pltpu.SemaphoreType.DMA((7,)), pltpu.SemaphoreType.DMA],
    )(blk)


def _gather_sends(n, own_ref, g_ref, send_sem, recv_sem, x, y, c, chip):
    return [
        pltpu.make_async_remote_copy(
            src_ref=own_ref.at[p, pl.ds(c * HALF, HALF), :], dst_ref=g_ref.at[2 * x + y, p, pl.ds(c * HALF, HALF), :],
            send_sem=send_sem, recv_sem=recv_sem, device_id=(*chip, c), device_id_type=_MESH,
        )
        for p in range(n)
    ]


def _gather_forwards(n, g_ref, send_sem, recv_sem, chip, c, sibling):
    return [
        pltpu.make_async_remote_copy(
            src_ref=g_ref.at[2 * chip[0] + chip[1], p, pl.ds(c * HALF, HALF), :],
            dst_ref=g_ref.at[2 * chip[0] + chip[1], p, pl.ds(c * HALF, HALF), :],
            send_sem=send_sem, recv_sem=recv_sem, device_id=sibling, device_id_type=_MESH,
        )
        for p in range(n)
    ]


def _gather_all(own_ref, g_ref, send_sem, recv_sem, chip, half, c):
    return pltpu.make_async_remote_copy(
        src_ref=own_ref.at[:, pl.ds(c * HALF, HALF), :],
        dst_ref=g_ref.at[2 * chip[0] + chip[1], :, pl.ds(half * HALF, HALF), :],
        send_sem=send_sem, recv_sem=recv_sem, device_id=(*chip, c), device_id_type=_MESH,
    )


_HBM_SPEC = pl.BlockSpec(memory_space=pltpu.HBM)
_SEM_SPEC = pl.BlockSpec(memory_space=pltpu.SEMAPHORE)
_ANY_SPEC = pl.BlockSpec(memory_space=pl.ANY)
_VMEM_SPEC = pl.BlockSpec(memory_space=pltpu.VMEM)
_SPLIT_PARAMS = pltpu.CompilerParams(has_side_effects=pltpu.SideEffectType.DATAFLOW_SIDE_EFFECTING)
_TOKEN = jax.ShapeDtypeStruct((8, 128), F32)


def _in_hbm(a):
    return pltpu.with_memory_space_constraint(a, pltpu.HBM)


class _GatherBehind:
    def __init__(self, tag, own, then=None):
        self.tag, self.n, self.own, self.then = tag, own.shape[0], own, then

    def start(self, after):
        tag, own, n = self.tag, self.own, self.n
        x, y, _ = _place()
        g_init = lax.dynamic_update_slice(
            lax.empty((N_CHIPS,) + own.shape, own.dtype), own[None], (2 * x + y, 0, 0, 0)
        )

        def body(own_ref, g_ref, *rest):
            send_sems, recv_sems, _, _, token = rest[len(after):]
            x, y, c = _place()
            for j, chip in enumerate(_other_chips(x, y)):
                for cp in _gather_sends(n, own_ref, g_ref, send_sems.at[j], recv_sems.at[j], x, y, c, chip):
                    cp.start()
            token[...] = jnp.zeros_like(token)

        self.send1, self.recv1, self.own, self.g, self.token = pl.pallas_call(
            body,
            name=f"{tag}_start",
            out_shape=(
                pltpu.SemaphoreType.DMA((3,)), pltpu.SemaphoreType.DMA((3,)),
                pltpu.HBM(own.shape, own.dtype), pltpu.HBM(g_init.shape, g_init.dtype), _TOKEN,
            ),
            in_specs=(_HBM_SPEC, _HBM_SPEC) + (_ANY_SPEC,) * len(after),
            out_specs=(_SEM_SPEC, _SEM_SPEC, _HBM_SPEC, _HBM_SPEC, _VMEM_SPEC),
            input_output_aliases={0: 2, 1: 3},
            compiler_params=_SPLIT_PARAMS,
        )(_in_hbm(own), _in_hbm(g_init), *after)

    def mid(self, after):
        n = self.n

        def body(own_ref, g_ref, send1, recv1, after_ref, send2, recv2, g_out, token):
            x, y, c = _place()
            sibling = (x, y, 1 - c)
            chips = _other_chips(x, y)
            for j, chip in enumerate(chips):
                _gather_all(own_ref, g_ref, send1.at[j], recv1.at[j], chip, c, c).wait_recv()
                for cp in _gather_forwards(n, g_ref, send2.at[j], recv2.at[j], chip, c, sibling):
                    cp.start()
            for j, chip in enumerate(chips):
                _gather_all(own_ref, g_ref, send1.at[j], recv1.at[j], chip, c, c).wait_send()
            token[...] = jnp.zeros_like(token)

        self.send2, self.recv2, self.g, token = pl.pallas_call(
            body,
            name=f"{self.tag}_mid",
            out_shape=(
                pltpu.SemaphoreType.DMA((3,)), pltpu.SemaphoreType.DMA((3,)),
                pltpu.HBM(self.g.shape, self.g.dtype), _TOKEN,
            ),
            in_specs=(_HBM_SPEC, _HBM_SPEC, _SEM_SPEC, _SEM_SPEC, _ANY_SPEC),
            out_specs=(_SEM_SPEC, _SEM_SPEC, _HBM_SPEC, _VMEM_SPEC),
            input_output_aliases={1: 2},
            compiler_params=_SPLIT_PARAMS,
        )(self.own, self.g, self.send1, self.recv1, after)
        if self.then is None:
            return [token]
        self.then.start([token])
        return [token, self.then.token]

    def get(self, after):
        def body(g_ref, send2, recv2, after_ref, g_out):
            x, y, c = _place()
            for j, chip in enumerate(_other_chips(x, y)):
                slot_in = g_ref.at[2 * chip[0] + chip[1], :, pl.ds((1 - c) * HALF, HALF), :]
                slot_out = g_ref.at[2 * chip[0] + chip[1], :, pl.ds(c * HALF, HALF), :]
                cp = pltpu.make_async_remote_copy(
                    src_ref=slot_out, dst_ref=slot_in, send_sem=send2.at[j], recv_sem=recv2.at[j],
                    device_id=(x, y, 1 - c), device_id_type=_MESH,
                )
                cp.wait_send()
                cp.wait_recv()

        return pl.pallas_call(
            body,
            name=f"{self.tag}_wait",
            out_shape=pltpu.HBM(self.g.shape, self.g.dtype),
            in_specs=(_HBM_SPEC, _SEM_SPEC, _SEM_SPEC, _ANY_SPEC),
            out_specs=_HBM_SPEC,
            input_output_aliases={0: 0},
            compiler_params=_SPLIT_PARAMS,
        )(self.g, self.send2, self.recv2, after)


def _pair_sum(name, gb, land):
    def body(c_ref, gb_ref, land_ref, o_ref):
        o_ref[...] = (gb_ref[...].astype(F32) + land_ref[...].astype(F32)).astype(o_ref.dtype)

    core = lax.axis_index("c").astype(jnp.int32).reshape(1)
    return pl.pallas_call(
        body,
        name=name,
        grid_spec=pltpu.PrefetchScalarGridSpec(
            num_scalar_prefetch=1,
            grid=(N_CHIPS, gb.shape[1]),
            in_specs=[
                pl.BlockSpec((None, None, HALF, D_MODEL), lambda j, p, c: (j, p, c[0], 0)),
                pl.BlockSpec((None, None, HALF, D_MODEL), lambda j, p, c: (j, p, 0, 0)),
            ],
            out_specs=pl.BlockSpec((None, None, HALF, D_MODEL), lambda j, p, c: (j, p, 0, 0)),
        ),
        out_shape=_sds(land.shape, BF16),
        compiler_params=_cparams(("parallel", "parallel")),
    )(core, gb, land)


def _chip_sum_share(name, land):
    n = land.shape[1]

    def body(l_ref, r_ref, buf, send_sems, recv_sems, local_sems):
        p = pl.program_id(0)
        x, y, c = _place()
        acc = l_ref[0].astype(F32)
        for j in range(1, N_CHIPS):
            acc = acc + l_ref[j].astype(F32)
        buf[p] = acc

        def copies(q):
            mine = r_ref.at[q, pl.ds(c * HALF, HALF), :]
            theirs = r_ref.at[q, pl.ds((1 - c) * HALF, HALF), :]
            local = pltpu.make_async_copy(buf.at[q], mine, local_sems.at[q])
            out = pltpu.make_async_remote_copy(
                src_ref=buf.at[q], dst_ref=mine, send_sem=send_sems.at[q], recv_sem=recv_sems.at[q],
                device_id=(x, y, 1 - c), device_id_type=_MESH,
            )
            arrive = pltpu.make_async_remote_copy(
                src_ref=buf.at[q], dst_ref=theirs, send_sem=send_sems.at[q], recv_sem=recv_sems.at[q],
                device_id=(x, y, 1 - c), device_id_type=_MESH,
            )
            return local, out, arrive

        local, out, _ = copies(p)
        local.start()
        out.start()

        @pl.when(p == n - 1)
        def _():
            for q in range(n):
                local, out, arrive = copies(q)
                arrive.wait_recv()
                out.wait_send()
                local.wait()

    return pl.pallas_call(
        body,
        name=name,
        grid=(n,),
        in_specs=[pl.BlockSpec((N_CHIPS, None, HALF, D_MODEL), lambda p: (0, p, 0, 0))],
        out_specs=pl.BlockSpec(memory_space=pl.ANY),
        out_shape=_sds((n, F_SHARD, D_MODEL), F32),
        scratch_shapes=[
            pltpu.VMEM((n, HALF, D_MODEL), F32),
            pltpu.SemaphoreType.DMA((n,)), pltpu.SemaphoreType.DMA((n,)), pltpu.SemaphoreType.DMA((n,)),
        ],
        compiler_params=_cparams(("arbitrary",)),
    )(land)


class _ReduceBehind:
    def __init__(self, tag, gb, after=()):
        self.tag = tag
        n = gb.shape[1]
        land = lax.empty((N_CHIPS, n, HALF, D_MODEL), gb.dtype)

        def body(gb_ref, land_ref, *rest):
            send_sem, recv_sem, _, _, token = rest[len(after):]
            self._pair_copy(gb_ref, land_ref, send_sem, recv_sem).start()
            token[...] = jnp.zeros_like(token)

        self.send, self.recv, self.gb, self.land, self.token = pl.pallas_call(
            body,
            name=f"grads_pair_start_{tag}",
            out_shape=(
                pltpu.SemaphoreType.DMA((1,)), pltpu.SemaphoreType.DMA((1,)),
                pltpu.HBM(gb.shape, gb.dtype), pltpu.HBM(land.shape, land.dtype), _TOKEN,
            ),
            in_specs=(_HBM_SPEC, _HBM_SPEC) + (_ANY_SPEC,) * len(after),
            out_specs=(_SEM_SPEC, _SEM_SPEC, _HBM_SPEC, _HBM_SPEC, _VMEM_SPEC),
            input_output_aliases={0: 2, 1: 3},
            compiler_params=_SPLIT_PARAMS,
        )(_in_hbm(gb), _in_hbm(land), *after)

    @staticmethod
    def _pair_copy(gb_ref, land_ref, send_sem, recv_sem):
        x, y, c = _place()
        return pltpu.make_async_remote_copy(
            src_ref=gb_ref.at[:, :, pl.ds((1 - c) * HALF, HALF), :], dst_ref=land_ref,
            send_sem=send_sem.at[0], recv_sem=recv_sem.at[0], device_id=(x, y, 1 - c), device_id_type=_MESH,
        )

    @staticmethod
    def _chip_copies(p_ref, land_ref, send_sems, recv_sems):
        x, y, c = _place()
        me = 2 * x + y
        sends, arrivals = [], []
        for k, chip in enumerate(_other_chips(x, y)):
            them = 2 * chip[0] + chip[1]
            sends.append(pltpu.make_async_remote_copy(
                src_ref=p_ref.at[them], dst_ref=land_ref.at[me], send_sem=send_sems.at[k], recv_sem=recv_sems.at[k],
                device_id=(*chip, c), device_id_type=_MESH,
            ))
            arrivals.append(pltpu.make_async_remote_copy(
                src_ref=p_ref.at[me], dst_ref=land_ref.at[them], send_sem=send_sems.at[k], recv_sem=recv_sems.at[k],
                device_id=(*chip, c), device_id_type=_MESH,
            ))
        return sends, arrivals

    def mid(self, after):
        tag = self.tag

        def wait_body(gb_ref, land_ref, send_sem, recv_sem, after_ref, gb_out, land_out):
            cp = self._pair_copy(gb_ref, land_ref, send_sem, recv_sem)
            cp.wait_send()
            cp.wait_recv()

        gb, land = pl.pallas_call(
            wait_body,
            name=f"grads_pair_wait_{tag}",
            out_shape=(pltpu.HBM(self.gb.shape, self.gb.dtype), pltpu.HBM(self.land.shape, self.land.dtype)),
            in_specs=(_HBM_SPEC, _HBM_SPEC, _SEM_SPEC, _SEM_SPEC, _ANY_SPEC),
            out_specs=(_HBM_SPEC, _HBM_SPEC),
            input_output_aliases={0: 0, 1: 1},
            compiler_params=_SPLIT_PARAMS,
        )(self.gb, self.land, self.send, self.recv, after)
        part = _pair_sum(f"grads_pair_sum_{tag}", gb, land)

        x, y, _ = _place()
        start = (2 * x + y, 0, 0, 0)
        own = lax.dynamic_slice(part, start, (1,) + part.shape[1:])
        land2 = lax.dynamic_update_slice(lax.empty(part.shape, part.dtype), own, start)

        def start_body(p_ref, land_ref, send_sems, recv_sems, p_out, land_out, token):
            for cp in self._chip_copies(p_ref, land_ref, send_sems, recv_sems)[0]:
                cp.start()
            token[...] = jnp.zeros_like(token)

        self.send2, self.recv2, self.part, self.land2, token = pl.pallas_call(
            start_body,
            name=f"grads_chip_start_{tag}",
            out_shape=(
                pltpu.SemaphoreType.DMA((3,)), pltpu.SemaphoreType.DMA((3,)),
                pltpu.HBM(part.shape, part.dtype), pltpu.HBM(land2.shape, land2.dtype), _TOKEN,
            ),
            in_specs=(_HBM_SPEC, _HBM_SPEC),
            out_specs=(_SEM_SPEC, _SEM_SPEC, _HBM_SPEC, _HBM_SPEC, _VMEM_SPEC),
            input_output_aliases={0: 2, 1: 3},
            compiler_params=_SPLIT_PARAMS,
        )(_in_hbm(part), _in_hbm(land2))
        return [token]

    def get(self, after):
        n_after = len(after)

        def wait_body(p_ref, land_ref, send_sems, recv_sems, *rest):
            sends, arrivals = self._chip_copies(p_ref, land_ref, send_sems, recv_sems)
            for cp in arrivals:
                cp.wait_recv()
            for cp in sends:
                cp.wait_send()

        _, land2 = pl.pallas_call(
            wait_body,
            name=f"grads_chip_wait_{self.tag}",
            out_shape=(pltpu.HBM(self.part.shape, self.part.dtype), pltpu.HBM(self.land2.shape, self.land2.dtype)),
            in_specs=(_HBM_SPEC, _HBM_SPEC, _SEM_SPEC, _SEM_SPEC) + (_ANY_SPEC,) * n_after,
            out_specs=(_HBM_SPEC, _HBM_SPEC),
            input_output_aliases={0: 0, 1: 1},
            compiler_params=_SPLIT_PARAMS,
        )(self.part, self.land2, self.send2, self.recv2, *after)
        return _chip_sum_share(f"grads_chip_sum_share_{self.tag}", land2)


def _allreduce_small(blk):
    gathered = _allgather_small("allgather_small_grads", blk).reshape(N_DEV, PK_ROWS, 128)

    def body(g_ref, o_ref):
        acc = g_ref[0]
        for k in range(1, N_DEV):
            acc = acc + g_ref[k]
        o_ref[...] = acc

    total = pl.pallas_call(body, name="small_grads_sum", out_shape=_sds((PK_ROWS, 128), F32))(gathered)
    return gathered, total


def _adamw_math(w_, g_, m_, v_):
    m_new = ADAM_B1 * m_ + (1.0 - ADAM_B1) * g_
    v_new = ADAM_B2 * v_ + (1.0 - ADAM_B2) * (g_ * g_)
    m_hat = m_new / (1.0 - ADAM_B1 ** ADAM_STEP)
    v_hat = v_new / (1.0 - ADAM_B2 ** ADAM_STEP)
    delta = -ADAM_LR * (m_hat / (jnp.sqrt(v_hat) + ADAM_EPS) + ADAM_WD * w_)
    return delta, m_new, v_new


def _adamw(name, w, g, m, v):
    rows, cols = w.shape
    tm = rows
    while tm * cols * 4 > (1 << 20) and tm % 16 == 0:
        tm //= 2
    out = _sds(w.shape, F32)
    return _rowwise(name, _adamw_math, [w, g, m, v], [], [out, out, out], [], tm)


def _adamw_small(ws, gs, ms, vs):
    n = len(ws)
    shapes = [w.shape for w in ws]
    as2d = lambda a: a.reshape(1, -1) if a.ndim == 1 else a

    def body(*refs):
        ins, outs = refs[:4 * n], refs[4 * n:]
        for k in range(n):
            res = _adamw_math(*[ins[j * n + k][...] for j in range(4)])
            for j in range(3):
                outs[j * n + k][...] = res[j]

    args = [as2d(a) for group in (ws, gs, ms, vs) for a in group]
    out = pl.pallas_call(
        body, name="adamw_small", out_shape=[_sds(as2d(w).shape, F32) for w in ws] * 3, compiler_params=_cparams()
    )(*args)
    back = [o.reshape(shapes[i % n]) for i, o in enumerate(out)]
    return back[:n], back[n:2 * n], back[2 * n:]


def _pack_small(b_mod_like, n1, n2, n3, nf, qn, kvn, sinks, rel):
    parts = [
        b_mod_like.reshape(PK_MOD, 128),
        n1.reshape(8, 128), n2.reshape(8, 128), n3.reshape(8, 128), nf.reshape(8, 128),
        qn.reshape(2, 128), kvn.reshape(1, 128),
        jnp.pad(sinks.reshape(1, SWA_HEADS), ((0, 0), (0, 128 - SWA_HEADS))),
        rel.reshape(2, 128),
        jnp.zeros((2, 128), F32),
    ]
    return jnp.concatenate(parts, axis=0)


def _unpack_small(p):
    o = PK_MOD
    return (
        p[:o].reshape(1, N_MOD * D_MODEL),
        p[o:o + 8].reshape(1, D_MODEL), p[o + 8:o + 16].reshape(1, D_MODEL),
        p[o + 16:o + 24].reshape(1, D_MODEL), p[o + 24:o + 32].reshape(D_MODEL),
        p[o + 32:o + 34].reshape(1, MLA_Q_RANK), p[o + 34:o + 35].reshape(1, MLA_KV_RANK),
        p[o + 35:o + 36, :SWA_HEADS].reshape(1, SWA_HEADS),
        p[o + 36:o + 38].reshape(NUM_BUCKETS, SWA_HEADS),
    )


def kernel(x, c, w_mod, b_mod, norm_ffn1, ffn1_gate, ffn1_up, ffn1_down, norm_mix, w_in, q_norm, kv_norm, w_uq, w_ukv, sinks, w_o, norm_ffn2, ffn2_gate, ffn2_up, ffn2_down, rel_bias, norm_final, loss_target, m_w_mod, m_b_mod, m_norm_ffn1, m_ffn1_gate, m_ffn1_up, m_ffn1_down, m_norm_mix, m_w_in, m_q_norm, m_kv_norm, m_w_uq, m_w_ukv, m_sinks, m_w_o, m_norm_ffn2, m_ffn2_gate, m_ffn2_up, m_ffn2_down, m_rel_bias, m_norm_final, v_w_mod, v_b_mod, v_norm_ffn1, v_ffn1_gate, v_ffn1_up, v_ffn1_down, v_norm_mix, v_w_in, v_q_norm, v_kv_norm, v_w_uq, v_w_ukv, v_sinks, v_w_o, v_norm_ffn2, v_ffn2_gate, v_ffn2_up, v_ffn2_down, v_rel_bias, v_norm_final):
    ax, ay, ac = _place()
    chip = 2 * ax + ay
    dev = 2 * chip + ac
    n_mod_shard = N_MOD * D_MODEL // N_CHIPS

    def t16(w):
        return w[0].T.astype(BF16)

    rest = jnp.concatenate(
        [
            t16(w_in),
            w_o[0].astype(BF16),
            t16(w_uq).reshape(R_UQ, D_MODEL),
            t16(w_ukv).reshape(R_UKV, D_MODEL),
            jnp.zeros((F_SHARD - O_PAD, D_MODEL), BF16),
        ],
        axis=0,
    )
    own_gu1 = jnp.stack([t16(ffn1_gate), t16(ffn1_up)])
    own_down1 = ffn1_down[0].astype(BF16)[None]
    own_mix = rest[None]
    own_ffn2 = jnp.stack([t16(ffn2_gate), t16(ffn2_up), ffn2_down[0].astype(BF16)])

    (c_act,) = _rowwise(
        "silu_c", lambda v: v * _sigmoid(v), [c.reshape(8, 128)], [], [_sds((8, 128), F32)], [], 8
    )
    c_all = _allgather_small("allgather_c", c_act).reshape(N_DEV, D_MODEL)
    mod_cols = _mm(
        "mod_fwd", "nn", (1, n_mod_shard // 768, 1),
        c_all, (N_DEV, D_MODEL), lambda i, j, k: (0, 0),
        w_mod[0], (D_MODEL, 768), lambda i, j, k: (0, j),
        _sds((N_DEV, n_mod_shard), F32), (N_DEV, 768), lambda i, j, k: (0, j),
        (N_DEV, 768),
    )
    mod_all = _allgather_small("allgather_mod", mod_cols).reshape(N_CHIPS, 2, N_DEV, n_mod_shard)[:, 0]
    mod_all = mod_all.transpose(1, 0, 2).reshape(N_DEV, N_MOD * D_MODEL)
    mod = lax.dynamic_slice_in_dim(mod_all, dev, 1, axis=0) + b_mod

    norms = (norm_ffn1, norm_mix, norm_ffn2, norm_final.reshape(1, D_MODEL))

    ffn2_src = _GatherBehind("gather_ffn2", own_ffn2)
    mix_src = _GatherBehind("gather_mix", own_mix, then=ffn2_src)
    down1_src = _GatherBehind("gather_down1", own_down1, then=mix_src)
    gu1_src = _GatherBehind("gather_gu1", own_gu1, then=down1_src)
    gu1_src.start([mod_all])

    reducing, red = {}, {}

    def begin(tag, gb, after):
        reducing[tag] = _ReduceBehind(tag, gb, after=after)
        return [reducing[tag].token]

    def ffn2_gu_done(gb):
        return begin("ffn2_gu", gb, reducing["ffn2_down"].mid(gb))

    def mix_done(dx1, gb_rest):
        red["ffn2_down"] = reducing["ffn2_down"].get([dx1])
        red["ffn2_gu"] = reducing["ffn2_gu"].get([red["ffn2_down"]])
        return begin("rest", gb_rest, [red["ffn2_gu"]])

    def ffn1_gu_done(gb):
        red["rest"] = reducing["rest"].get([gb])
        begin("ffn1_gu", gb, reducing["ffn1_down"].mid(red["rest"]))
        return reducing["ffn1_gu"].mid(reducing["ffn1_gu"].token)

    loss_part, grad_x, _, dmod, small = _device_step(
        x[0], loss_target[0], mod, gu1_src, down1_src, mix_src, ffn2_src, norms, q_norm, kv_norm, sinks, rel_bias,
        hooks2=_BwdHooks(after_wdown=lambda gb: begin("ffn2_down", gb, ()), after_wgu=ffn2_gu_done),
        hooks_mix=_BwdHooks(after_gate=lambda dz: reducing["ffn2_gu"].mid(dz)),
        mix_done=mix_done,
        hooks1=_BwdHooks(
            after_swiglu=lambda dab3: reducing["rest"].mid(dab3),
            after_wdown=lambda gb: begin("ffn1_down", gb, ()),
            after_wgu=ffn1_gu_done,
        ),
    )
    loss = lax.psum(loss_part, ("x", "y", "c"))

    gathered, total = _allreduce_small(_pack_small(dmod, *small))
    (g_b_mod, g_n1, g_n2, g_n3, g_nf, g_qn, g_kvn, g_sinks, g_rel) = _unpack_small(total)
    dmod_all = gathered[:, :PK_MOD].reshape(N_DEV, N_MOD * D_MODEL)
    dmod_cols = lax.dynamic_slice_in_dim(dmod_all, chip * n_mod_shard, n_mod_shard, axis=1)
    g_w_mod = _mm(
        "mod_bwd", "tn", (1, n_mod_shard // 768, 1),
        c_all, (N_DEV, D_MODEL), lambda i, j, k: (0, 0),
        dmod_cols, (N_DEV, 768), lambda i, j, k: (0, j),
        _sds((D_MODEL, n_mod_shard), F32), (D_MODEL, 768), lambda i, j, k: (0, j),
        (D_MODEL, 768),
    )

    r_rest = red["rest"][0]
    transposed = {"ffn1_gate", "ffn1_up", "ffn2_gate", "ffn2_up", "w_in", "w_uq", "w_ukv"}
    g_big = {
        "ffn2_gate": red["ffn2_gu"][0], "ffn2_up": red["ffn2_gu"][1], "ffn2_down": red["ffn2_down"][0],
        "w_in": r_rest[O_IN:O_IN + R_IN],
        "w_o": r_rest[O_O:O_O + R_O],
        "w_uq": r_rest[O_UQ:O_UQ + R_UQ].reshape(MLA_QK, MLA_Q_RANK),
        "w_ukv": r_rest[O_UKV:O_UKV + R_UKV].reshape(MLA_NOPE + MLA_V, MLA_KV_RANK),
        "w_mod": g_w_mod,
    }
    w_big = {
        "ffn2_gate": (ffn2_gate, m_ffn2_gate, v_ffn2_gate),
        "ffn2_up": (ffn2_up, m_ffn2_up, v_ffn2_up), "ffn2_down": (ffn2_down, m_ffn2_down, v_ffn2_down),
        "w_in": (w_in, m_w_in, v_w_in), "w_o": (w_o, m_w_o, v_w_o), "w_uq": (w_uq, m_w_uq, v_w_uq),
        "w_ukv": (w_ukv, m_w_ukv, v_w_ukv), "w_mod": (w_mod, m_w_mod, v_w_mod),
    }
    grads, deltas, new_m, new_v = {}, {}, {}, {}

    def update(names):
        for nm in names:
            w, m, v = w_big[nm]
            if nm in transposed:
                outs = _adamw(f"adamw_{nm}", w[0].T, g_big[nm], m[0].T, v[0].T)
                g_, d_, m_, v_ = [a.T for a in (g_big[nm],) + tuple(outs)]
            else:
                g_, (d_, m_, v_) = g_big[nm], _adamw(f"adamw_{nm}", w[0], g_big[nm], m[0], v[0])
            grads[nm], deltas[nm], new_m[nm], new_v[nm] = g_[None], d_[None], m_[None], v_[None]

    update(list(w_big))
    red1_down = reducing["ffn1_down"].get([deltas[nm] for nm in w_big])
    red1_gu = reducing["ffn1_gu"].get([red1_down])
    g_big.update({"ffn1_gate": red1_gu[0], "ffn1_up": red1_gu[1], "ffn1_down": red1_down[0]})
    w_big.update({
        "ffn1_gate": (ffn1_gate, m_ffn1_gate, v_ffn1_gate), "ffn1_up": (ffn1_up, m_ffn1_up, v_ffn1_up),
        "ffn1_down": (ffn1_down, m_ffn1_down, v_ffn1_down),
    })
    update(["ffn1_gate", "ffn1_up", "ffn1_down"])

    small_names = ["b_mod", "norm_ffn1", "norm_mix", "norm_ffn2", "norm_final", "q_norm", "kv_norm", "sinks", "rel_bias"]
    w_small = (b_mod, norm_ffn1, norm_mix, norm_ffn2, norm_final, q_norm, kv_norm, sinks, rel_bias)
    m_small = (m_b_mod, m_norm_ffn1, m_norm_mix, m_norm_ffn2, m_norm_final, m_q_norm, m_kv_norm, m_sinks, m_rel_bias)
    v_small = (v_b_mod, v_norm_ffn1, v_norm_mix, v_norm_ffn2, v_norm_final, v_q_norm, v_kv_norm, v_sinks, v_rel_bias)
    g_small = (g_b_mod, g_n1, g_n2, g_n3, g_nf, g_qn, g_kvn, g_sinks, g_rel)
    d_s, m_s, v_s = _adamw_small(w_small, g_small, m_small, v_small)
    for nm, g_, d_, m_, v_ in zip(small_names, g_small, d_s, m_s, v_s):
        grads[nm], deltas[nm], new_m[nm], new_v[nm] = g_, d_, m_, v_

    order = ["w_mod", "b_mod", "norm_ffn1", "ffn1_gate", "ffn1_up", "ffn1_down", "norm_mix", "w_in", "q_norm", "kv_norm",
             "w_uq", "w_ukv", "sinks", "w_o", "norm_ffn2", "ffn2_gate", "ffn2_up", "ffn2_down", "rel_bias", "norm_final"]
    return (loss, grad_x[None], *[grads[n] for n in order], *[deltas[n] for n in order],
            *[new_m[n] for n in order], *[new_v[n] for n in order])
```

```python
import functools
import math

import jax
import jax.numpy as jnp
import numpy as np
from jax import lax
from jax.experimental import pallas as pl
from jax.experimental.pallas import tpu as pltpu

F32, BF16 = jnp.float32, jnp.bfloat16

D_MODEL = 1024
D_FF = 2816
EPS = 1e-6
N_MOD = 9
SWA_HEADS, SWA_KV_HEADS, SWA_HEAD_DIM, WINDOW = 8, 2, 64, 128
SWA_GROUP = SWA_HEADS // SWA_KV_HEADS
MLA_HEADS, MLA_Q_RANK, MLA_KV_RANK, MLA_NOPE, MLA_ROPE, MLA_V = 4, 256, 128, 128, 64, 128
MLA_QK = MLA_NOPE + MLA_ROPE
ROPE_THETA = 10000.0
NUM_BUCKETS, MAX_DISTANCE = 32, 128
D_IN = 1216
N_SWA_COLS = 768
N_MLA_COLS = 512

ADAM_LR, ADAM_B1, ADAM_B2, ADAM_EPS, ADAM_WD, ADAM_STEP = 0.001, 0.9, 0.999, 1e-08, 0.01, 10

N_CHIPS = 4
N_DEV = 8
F_SHARD = D_FF // N_CHIPS
HALF = F_SHARD // 2
R_IN, R_O, R_UQ, R_UKV = 304, 256, 48, 32
O_IN, O_O, O_UQ, O_UKV, O_PAD = 0, 304, 560, 608, 640

PK_MOD = 72
PK_ROWS = 112
VMEM_LIMIT = 56 << 20
ROW_TILE = 2048

_DN = {
    "nn": (((1,), (0,)), ((), ())),
    "nt": (((1,), (1,)), ((), ())),
    "tn": (((0,), (0,)), ((), ())),
}


def _sds(shape, dtype):
    return jax.ShapeDtypeStruct(tuple(shape), dtype)


def _cparams(sem=None):
    kw = {"vmem_limit_bytes": VMEM_LIMIT}
    if sem is not None:
        kw["dimension_semantics"] = sem
    return pltpu.CompilerParams(**kw)


def _dot(a, b, dims):
    return lax.dot_general(a.astype(BF16), b.astype(BF16), _DN[dims], preferred_element_type=F32)


def _mm(name, dims, grid, a, a_blk, a_idx, b, b_blk, b_idx, out, out_blk, out_idx, acc_shape, alias=None, deps=()):
    nk = grid[2]

    def body(*refs):
        a_ref, b_ref = refs[:2]
        o_ref, acc_ref = refs[-2:]
        part = _dot(a_ref[...], b_ref[...], dims)
        if nk == 1:
            o_ref[...] = part.astype(o_ref.dtype)
            return
        k = pl.program_id(2)

        @pl.when(k == 0)
        def _():
            acc_ref[...] = part

        @pl.when((k > 0) & (k < nk - 1))
        def _():
            acc_ref[...] += part

        @pl.when(k == nk - 1)
        def _():
            o_ref[...] = (acc_ref[...] + part).astype(o_ref.dtype)

    in_specs = [pl.BlockSpec(a_blk, a_idx), pl.BlockSpec(b_blk, b_idx)]
    args = [a, b]
    kw = {}
    if alias is not None:
        in_specs.append(pl.BlockSpec(memory_space=pl.ANY))
        args.append(alias)
        kw["input_output_aliases"] = {2: 0}
    in_specs += [pl.BlockSpec(memory_space=pl.ANY)] * len(deps)
    args += list(deps)
    return pl.pallas_call(
        body,
        name=name,
        grid=grid,
        in_specs=in_specs,
        out_specs=pl.BlockSpec(out_blk, out_idx),
        out_shape=out,
        scratch_shapes=[pltpu.VMEM(acc_shape if nk > 1 else (8, 128), F32)],
        compiler_params=_cparams(("parallel", "parallel", "arbitrary")),
        **kw,
    )(*args)


def _rowwise(name, fn, tiled, full, out_tiled, out_red, tm, deps=()):
    rows = tiled[0].shape[-2]
    grid = (rows // tm,)
    n_in = len(tiled) + len(full)
    n_t = len(out_tiled)
    n_dep = len(deps)

    def tspec(shape):
        nd = len(shape)
        return pl.BlockSpec(tuple(shape[:-2]) + (tm, shape[-1]), lambda i, nd=nd: (0,) * (nd - 2) + (i, 0))

    def fspec(shape):
        nd = len(shape)
        return pl.BlockSpec(tuple(shape), lambda i, nd=nd: (0,) * nd)

    def body(*refs):
        outs = fn(*[r[...] for r in refs[:n_in]])
        if not isinstance(outs, (tuple, list)):
            outs = (outs,)
        out_refs = refs[n_in + n_dep:]
        for r, v in zip(out_refs[:n_t], outs[:n_t]):
            r[...] = v.astype(r.dtype)
        red_refs = out_refs[n_t:]
        if red_refs:
            @pl.when(pl.program_id(0) == 0)
            def _():
                for r in red_refs:
                    r[...] = jnp.zeros_like(r)

            for r, v in zip(red_refs, outs[n_t:]):
                r[...] += v.astype(r.dtype)

    res = pl.pallas_call(
        body,
        name=name,
        grid=grid,
        in_specs=[tspec(a.shape) for a in tiled] + [fspec(a.shape) for a in full]
        + [pl.BlockSpec(memory_space=pl.ANY)] * n_dep,
        out_specs=[tspec(o.shape) for o in out_tiled] + [fspec(o.shape) for o in out_red],
        out_shape=list(out_tiled) + list(out_red),
        compiler_params=_cparams(("arbitrary",) if out_red else ("parallel",)),
    )(*tiled, *full, *deps)
    return res


def _sum0(v):
    return jnp.sum(v, axis=0, keepdims=True)


def _sigmoid(v):
    return 1.0 / (1.0 + jnp.exp(-v))


def _rms(x):
    r = lax.rsqrt(jnp.mean(x * x, axis=-1, keepdims=True) + EPS)
    return x * r, r


def _rms_bwd(u, xr, r):
    return r * (u - xr * jnp.mean(u * xr, axis=-1, keepdims=True))


class _Ready:
    def __init__(self, g):
        self.g = g

    def mid(self, after):
        return []

    def get(self, after):
        return self.g


def _normmod(x_, gamma_, sc_, sh_):
    return _rms(x_)[0] * gamma_ * (1.0 + sc_) + sh_


def _row_blocks(tm, size=256):
    size = min(size, tm)
    return [slice(r, r + size) for r in range(0, tm, size)]


def _loss_head(x_, t_, g_):
    xr, r = _rms(x_)
    err = xr * g_ - t_
    dy = err * (1.0 / D_MODEL)
    return _rms_bwd(dy * g_, xr, r), _sum0(err * err), _sum0(dy * xr)


def _ffn_fwd(tag, x, gamma, sh, sc, gate, gu_src, base, down_src, down_idx, mid_after, h_pre=None,
             next_norm=None, head=None):
    s_len = x.shape[0]
    if h_pre is None:
        (h,) = _rowwise(
            f"{tag}_normmod", _normmod, [x], [gamma, sc, sh], [_sds((s_len, D_MODEL), BF16)], [], 256,
            deps=gu_src.mid(mid_after),
        )
        gdeps = []
    else:
        h, gdeps = h_pre, gu_src.mid(mid_after)
    g4 = gu_src.get(h)

    tm = min(ROW_TILE, s_len)
    def gate_up(h_ref, wg_ref, wu_ref, *rest):
        ab_ref, s_ref = rest[-2:]
        wg, wu = wg_ref[...], wu_ref[...]
        for rows in _row_blocks(tm):
            hv = h_ref[rows, :]
            a = _dot(hv, wg, "nt")
            b = _dot(hv, wu, "nt")
            ab_ref[0, rows, :] = a.astype(ab_ref.dtype)
            ab_ref[1, rows, :] = b.astype(ab_ref.dtype)
            s_ref[rows, :] = (a * _sigmoid(a) * b).astype(s_ref.dtype)

    ab4, s3 = pl.pallas_call(
        gate_up,
        name=f"{tag}_gate_up",
        grid=(s_len // tm, N_CHIPS),
        in_specs=[
            pl.BlockSpec((tm, D_MODEL), lambda i, c: (i, 0)),
            pl.BlockSpec((None, None, F_SHARD, D_MODEL), lambda i, c: (c, base, 0, 0)),
            pl.BlockSpec((None, None, F_SHARD, D_MODEL), lambda i, c: (c, base + 1, 0, 0)),
        ] + [pl.BlockSpec(memory_space=pl.ANY)] * len(gdeps),
        out_specs=[
            pl.BlockSpec((None, 2, tm, F_SHARD), lambda i, c: (c, 0, i, 0)),
            pl.BlockSpec((None, tm, F_SHARD), lambda i, c: (c, i, 0)),
        ],
        out_shape=[_sds((N_CHIPS, 2, s_len, F_SHARD), BF16), _sds((N_CHIPS, s_len, F_SHARD), BF16)],
        compiler_params=_cparams(("parallel", "parallel")),
    )(h, g4, g4, *gdeps)
    if down_src is None:
        g_down, ddeps = g4, ()
    else:
        ddeps = down_src.mid(ab4)
        g_down = down_src.get(s3)

    y = _mm(
        f"{tag}_down", "nn", (s_len // tm, 1, N_CHIPS),
        s3, (None, tm, F_SHARD), lambda i, j, k: (k, i, 0),
        g_down, (None, None, F_SHARD, D_MODEL), lambda i, j, k: (k, down_idx, 0, 0),
        _sds((s_len, D_MODEL), F32), (tm, D_MODEL), lambda i, j, k: (i, 0),
        (tm, D_MODEL), deps=ddeps,
    )

    saved = (g4, base, g_down, down_idx, h, ab4, s3, y)
    act = _sds((s_len, D_MODEL), F32)
    if head is not None:
        target, norm_final = head
        vec = _sds((1, D_MODEL), F32)
        out = _rowwise(
            f"{tag}_residual_head", lambda x_, y_, t_, g_, nf_: _loss_head(x_ + 0.5 * g_ * y_, t_, nf_),
            [x, y, target], [gate, norm_final], [act], [vec, vec], 256,
        )
    elif next_norm is not None:
        def residual_norm(x_, y_, g_, gamma_, sc_, sh_):
            x_out = x_ + 0.5 * g_ * y_
            return x_out, _normmod(x_out, gamma_, sc_, sh_)

        out = _rowwise(
            f"{tag}_residual_norm", residual_norm, [x, y], [gate] + list(next_norm),
            [act, _sds((s_len, D_MODEL), BF16)], [], 256,
        )
    else:
        out = _rowwise(f"{tag}_residual", lambda x_, y_, g_: x_ + 0.5 * g_ * y_, [x, y], [gate], [act], [], 256)
    return out, saved


def _normmod_bwd_call(name, dh_parts, x, dxo, gamma, sc, deps=()):
    s_len = x.shape[0]
    n_parts = len(dh_parts)

    def fn(*vals):
        dh = vals[0]
        for extra in vals[1:n_parts]:
            dh = dh + extra
        x_, dxo_, gamma_, sc_ = vals[n_parts:]
        xr, r = _rms(x_)
        n = xr * gamma_
        dn = dh * (1.0 + sc_)
        dx = dxo_ + _rms_bwd(dn * gamma_, xr, r)
        return dx, _sum0(dh * n), _sum0(dh), _sum0(dn * xr)

    vec = _sds((1, D_MODEL), F32)
    return _rowwise(
        name, fn, list(dh_parts) + [x, dxo], [gamma, sc], [_sds((s_len, D_MODEL), F32)], [vec, vec, vec], 256, deps=deps
    )


class _BwdHooks:
    def __init__(self, after_gate=None, after_swiglu=None, after_wdown=None, after_wgu=None, after_dh=None):
        def nothing(_):
            return []

        self.after_gate = after_gate or nothing
        self.after_swiglu = after_swiglu or nothing
        self.after_wdown = after_wdown or nothing
        self.after_wgu = after_wgu or nothing
        self.after_dh = after_dh or nothing


def _ffn_bwd(tag, dxo, x, gamma, sc, gate, saved, hooks, deps=()):
    g4, base, g_down, down_idx, h, ab4, s3, y = saved
    s_len = x.shape[0]
    vec = _sds((1, D_MODEL), F32)

    dy, dgate = _rowwise(
        f"{tag}_bwd_gate", lambda dxo_, y_, g_: (0.5 * g_ * dxo_, _sum0(0.5 * y_ * dxo_)),
        [dxo, y], [gate], [_sds((s_len, D_MODEL), BF16)], [vec], 256, deps=deps,
    )

    tm = min(ROW_TILE, s_len)

    def d_gate_up(dy_ref, wd_ref, ab_ref, o_ref):
        wd = wd_ref[...]
        for rows in _row_blocks(tm):
            ds = _dot(dy_ref[rows, :], wd, "nt")
            a = ab_ref[0, rows, :].astype(F32)
            b = ab_ref[1, rows, :].astype(F32)
            sig = _sigmoid(a)
            o_ref[0, rows, :] = (ds * b * sig * (1.0 + a * (1.0 - sig))).astype(o_ref.dtype)
            o_ref[1, rows, :] = (ds * a * sig).astype(o_ref.dtype)

    ab_spec = pl.BlockSpec((None, 2, tm, F_SHARD), lambda i, c: (c, 0, i, 0))
    dab4 = pl.pallas_call(
        d_gate_up,
        name=f"{tag}_bwd_dgate_up",
        grid=(s_len // tm, N_CHIPS),
        in_specs=[
            pl.BlockSpec((tm, D_MODEL), lambda i, c: (i, 0)),
            pl.BlockSpec((None, None, F_SHARD, D_MODEL), lambda i, c: (c, down_idx, 0, 0)),
            ab_spec,
        ],
        out_specs=ab_spec,
        out_shape=_sds(ab4.shape, BF16),
        compiler_params=_cparams(("parallel", "parallel")),
    )(dy, g_down, ab4)

    tk = tm
    gb_down = _mm(
        f"{tag}_bwd_wdown", "tn", (N_CHIPS, 1, s_len // tk),
        s3, (None, tk, F_SHARD), lambda i, j, k: (i, k, 0),
        dy, (tk, D_MODEL), lambda i, j, k: (k, 0),
        _sds((N_CHIPS, 1, F_SHARD, D_MODEL), BF16), (None, None, F_SHARD, D_MODEL), lambda i, j, k: (i, 0, 0, 0),
        (F_SHARD, D_MODEL), deps=hooks.after_swiglu(dab4),
    )
    gb_gu = _mm(
        f"{tag}_bwd_wgu", "tn", (2 * N_CHIPS, 1, s_len // tk),
        dab4, (None, None, tk, F_SHARD), lambda i, j, k: (i % N_CHIPS, i // N_CHIPS, k, 0),
        h, (tk, D_MODEL), lambda i, j, k: (k, 0),
        _sds((N_CHIPS, 2, F_SHARD, D_MODEL), BF16), (None, None, F_SHARD, D_MODEL),
        lambda i, j, k: (i % N_CHIPS, i // N_CHIPS, 0, 0),
        (F_SHARD, D_MODEL), deps=hooks.after_wdown(gb_down),
    )
    assert base % 2 == 0
    dh_deps = hooks.after_wgu(gb_gu)

    def d_h(dab_ref, w_ref, *rest):
        o_ref, acc_ref = rest[-2:]
        c = pl.program_id(1)
        part = _dot(dab_ref[0], w_ref[0], "nn") + _dot(dab_ref[1], w_ref[1], "nn")

        @pl.when(c == 0)
        def _():
            acc_ref[...] = part

        @pl.when((c > 0) & (c < N_CHIPS - 1))
        def _():
            acc_ref[...] += part

        @pl.when(c == N_CHIPS - 1)
        def _():
            o_ref[...] = acc_ref[...] + part

    dh = pl.pallas_call(
        d_h,
        name=f"{tag}_bwd_dh",
        grid=(s_len // tm, N_CHIPS),
        in_specs=[
            pl.BlockSpec((None, 2, tm, F_SHARD), lambda i, c: (c, 0, i, 0)),
            pl.BlockSpec((None, 2, F_SHARD, D_MODEL), lambda i, c: (c, base // 2, 0, 0)),
        ] + [pl.BlockSpec(memory_space=pl.ANY)] * len(dh_deps),
        out_specs=pl.BlockSpec((tm, D_MODEL), lambda i, c: (i, 0)),
        out_shape=_sds((s_len, D_MODEL), F32),
        scratch_shapes=[pltpu.VMEM((tm, D_MODEL), F32)],
        compiler_params=_cparams(("parallel", "arbitrary")),
    )(dab4, g4, *dh_deps)
    dx, dsc, dsh, dgamma = _normmod_bwd_call(
        f"{tag}_bwd_normmod", [dh], x, dxo, gamma, sc, deps=hooks.after_dh(dh)
    )
    return dx, (gb_down, gb_gu), (dsh, dsc, dgate, dgamma)


def _bucket_map():
    qi = np.arange(WINDOW)[:, None]
    kj = np.arange(2 * WINDOW)[None, :]
    dist = qi + WINDOW - kj
    band = (dist >= 0) & (dist < WINDOW)
    max_exact = NUM_BUCKETS // 2
    n = np.maximum(dist, 0)
    large = []
    for dt in (np.float32, np.float64):
        nf = np.maximum(n, 1).astype(dt)
        val = np.log(nf / dt(max_exact)) / dt(math.log(MAX_DISTANCE / max_exact)) * dt(NUM_BUCKETS - max_exact)
        large.append(np.minimum(max_exact + val.astype(np.int32), NUM_BUCKETS - 1))
    assert np.array_equal(large[0], large[1])
    bucket = np.where(n < max_exact, n, large[0])
    return np.where(band, bucket, -1).astype(np.int32).reshape(1, -1)


def _bias_expand(rel_bias_t, bkt):
    n = bkt.shape[1]

    def body(rb_ref, bkt_ref, o_ref):
        onehot = (lax.broadcasted_iota(jnp.int32, (NUM_BUCKETS, n), 0) == bkt_ref[...]).astype(F32)
        o_ref[...] = lax.dot_general(
            rb_ref[...], onehot, _DN["nn"], precision=lax.Precision.HIGHEST, preferred_element_type=F32
        )

    return pl.pallas_call(
        body, name="bias_expand", out_shape=_sds((SWA_HEADS, n), F32), compiler_params=_cparams()
    )(rel_bias_t, bkt)


def _bias_reduce(dbias_flat, bkt):
    n = bkt.shape[1]

    def body(db_ref, bkt_ref, o_ref):
        onehot = (lax.broadcasted_iota(jnp.int32, (NUM_BUCKETS, n), 0) == bkt_ref[...]).astype(F32)
        o_ref[...] = lax.dot_general(
            db_ref[...], onehot, _DN["nt"], precision=lax.Precision.HIGHEST, preferred_element_type=F32
        )

    return pl.pallas_call(
        body, name="bias_reduce", out_shape=_sds((SWA_HEADS, NUM_BUCKETS), F32), compiler_params=_cparams()
    )(dbias_flat, bkt)


_SWA_SCALE = SWA_HEAD_DIM ** -0.5
_NEG = -1e30


_SWA_PAIR = 2


def _swa_in_specs():
    w = WINDOW
    n_q = SWA_HEADS * SWA_HEAD_DIM
    n_kv = 2 * SWA_KV_HEADS * SWA_HEAD_DIM
    prev = lambda m: jnp.maximum(_SWA_PAIR * m - 1, 0)
    return [
        pl.BlockSpec((_SWA_PAIR * w, n_q), lambda m: (m, 0)),
        pl.BlockSpec((w, n_kv), lambda m: (prev(m), n_q // n_kv)),
        pl.BlockSpec((_SWA_PAIR * w, n_kv), lambda m: (m, n_q // n_kv)),
        pl.BlockSpec((SWA_HEADS, w, 2 * w), lambda m: (0, 0, 0)),
        pl.BlockSpec((SWA_HEADS, w, 1), lambda m: (0, 0, 0)),
    ]


def _head_cols(v, first, count):
    hd = SWA_HEAD_DIM
    return jnp.stack([v[:, (first + i) * hd:(first + i + 1) * hd] for i in range(count)])


def _swa_blocks(q_ref, kvp_ref, kvc_ref, m):
    g, w, hd = SWA_GROUP, WINDOW, SWA_HEAD_DIM
    kv_all = jnp.concatenate([kvp_ref[...], kvc_ref[...]], axis=0).astype(F32)
    blocks = []
    for sub in range(_SWA_PAIR):
        rows = slice(sub * w, (sub + 1) * w)
        q = q_ref[rows, :].astype(F32)
        kv = kv_all[sub * w:(sub + 2) * w]
        heads = []
        for j in range(SWA_KV_HEADS):
            qj = _head_cols(q, g * j, g).reshape(g * w, hd)
            heads.append((qj, _head_cols(kv, j, 1)[0], _head_cols(kv, SWA_KV_HEADS + j, 1)[0]))
        blocks.append((_SWA_PAIR * m + sub, rows, heads))
    return blocks


def _swa_scores(q, kk, bias, n):
    g, w = SWA_GROUP, WINDOW
    s = (_dot(q, kk, "nt") * _SWA_SCALE).reshape(g, w, 2 * w) + bias
    qi = lax.broadcasted_iota(jnp.int32, (w, 2 * w), 0)
    kj = lax.broadcasted_iota(jnp.int32, (w, 2 * w), 1)
    dist = qi + w - kj
    valid = ((dist >= 0) & (dist < w) & ((n > 0) | (kj >= w)))[None]
    return jnp.where(valid, s, _NEG), valid


def _swa_fwd(swa2, bias, sinkb):
    s_len = swa2.shape[0]
    g, w, hd = SWA_GROUP, WINDOW, SWA_HEAD_DIM

    def body(q_ref, kvp_ref, kvc_ref, bias_ref, sink_ref, o_ref, lse_ref):
        for n, rows, heads in _swa_blocks(q_ref, kvp_ref, kvc_ref, pl.program_id(0)):
            outs = []
            for j, (q, kk, vv) in enumerate(heads):
                hs = slice(g * j, g * (j + 1))
                s, valid = _swa_scores(q, kk, bias_ref[hs], n)
                sink = sink_ref[hs]
                m = jnp.maximum(jnp.max(s, axis=-1, keepdims=True), sink)
                p = jnp.where(valid, jnp.exp(s - m), 0.0)
                l = jnp.sum(p, axis=-1, keepdims=True) + jnp.exp(sink - m)
                o = _dot(p.reshape(g * w, 2 * w), vv, "nn").reshape(g, w, hd) / l
                outs += [o[i] for i in range(g)]
                lse_ref[hs, rows, :] = m + jnp.log(l)
            o_ref[rows, :] = jnp.concatenate(outs, axis=-1).astype(o_ref.dtype)

    n_q = SWA_HEADS * hd
    return pl.pallas_call(
        body,
        name="swa_fwd",
        grid=(s_len // (_SWA_PAIR * w),),
        in_specs=_swa_in_specs(),
        out_specs=[
            pl.BlockSpec((_SWA_PAIR * w, n_q), lambda m: (m, 0)),
            pl.BlockSpec((SWA_HEADS, _SWA_PAIR * w, 1), lambda m: (0, m, 0)),
        ],
        out_shape=[_sds((s_len, n_q), BF16), _sds((SWA_HEADS, s_len, 1), F32)],
        compiler_params=_cparams(("parallel",)),
    )(swa2, swa2, swa2, bias, sinkb)


def _swa_bwd(swa2, bias, sinkb, cat, dcat, lse):
    s_len = swa2.shape[0]
    g, w, hd = SWA_GROUP, WINDOW, SWA_HEAD_DIM

    def body(q_ref, kvp_ref, kvc_ref, bias_ref, sink_ref, o_ref, do_ref, lse_ref,
             dq_ref, dkva_ref, dkvb_ref, dbias_ref, dsink_ref):
        step = pl.program_id(0)

        @pl.when(step == 0)
        def _():
            dbias_ref[...] = jnp.zeros_like(dbias_ref)
            dsink_ref[...] = jnp.zeros_like(dsink_ref)

        for n, rows, heads in _swa_blocks(q_ref, kvp_ref, kvc_ref, step):
            o_all = o_ref[rows, :].astype(F32)
            do_all = do_ref[rows, :].astype(F32)
            dqs, dks, dvs = [], [], []
            for j, (q, kk, vv) in enumerate(heads):
                hs = slice(g * j, g * (j + 1))
                s, valid = _swa_scores(q, kk, bias_ref[hs], n)
                lse_v = lse_ref[hs, rows, :]
                p = jnp.where(valid, jnp.exp(s - lse_v), 0.0)
                do = _head_cols(do_all, g * j, g)
                delta = jnp.sum(do * _head_cols(o_all, g * j, g), axis=-1, keepdims=True)
                do2 = do.reshape(g * w, hd)
                dp = _dot(do2, vv, "nt").reshape(g, w, 2 * w)
                ds = p * (dp - delta)
                dbias_ref[hs] += ds
                dsink_ref[hs] -= jnp.exp(sink_ref[hs] - lse_v) * delta
                ds2 = ds.reshape(g * w, 2 * w)
                dq = (_dot(ds2, kk, "nn") * _SWA_SCALE).reshape(g, w, hd)
                dqs += [dq[i] for i in range(g)]
                dks.append(_dot(ds2, q, "tn") * _SWA_SCALE)
                dvs.append(_dot(p.reshape(g * w, 2 * w), do2, "tn"))
            dq_ref[rows, :] = jnp.concatenate(dqs, axis=-1).astype(dq_ref.dtype)
            dkv = jnp.concatenate(dks + dvs, axis=-1)
            dkvb_ref[rows, :] = dkv[:w]
            dkva_ref[rows, :] = dkv[w:]

    n_q = SWA_HEADS * hd
    n_kv = 2 * SWA_KV_HEADS * hd
    q_spec = pl.BlockSpec((_SWA_PAIR * w, n_q), lambda m: (m, 0))
    kv_spec = pl.BlockSpec((_SWA_PAIR * w, n_kv), lambda m: (m, 0))
    return pl.pallas_call(
        body,
        name="swa_bwd",
        grid=(s_len // (_SWA_PAIR * w),),
        in_specs=_swa_in_specs() + [q_spec, q_spec, pl.BlockSpec((SWA_HEADS, _SWA_PAIR * w, 1), lambda m: (0, m, 0))],
        out_specs=[
            q_spec, kv_spec, kv_spec,
            pl.BlockSpec((SWA_HEADS, w, 2 * w), lambda n: (0, 0, 0)),
            pl.BlockSpec((SWA_HEADS, w, 1), lambda n: (0, 0, 0)),
        ],
        out_shape=[
            _sds((s_len, n_q), BF16), _sds((s_len, n_kv), F32), _sds((s_len, n_kv), F32),
            _sds((SWA_HEADS, w, 2 * w), F32), _sds((SWA_HEADS, w, 1), F32),
        ],
        compiler_params=_cparams(("arbitrary",)),
    )(swa2, swa2, swa2, bias, sinkb, cat, dcat, lse)


_MLA_SCALE = MLA_QK ** -0.5
_MLA_TQ = 256
_MLA_PAIR = 2


def _mla_mask(i, tq, n_keys):
    row = i * tq + lax.broadcasted_iota(jnp.int32, (tq, n_keys), 0)
    col = lax.broadcasted_iota(jnp.int32, (tq, n_keys), 1)
    return col <= row


def _per_query_block(i, n_blocks, fn):
    for ii in range(n_blocks):
        pl.when(i == ii)(functools.partial(fn, ii))


def _mla_fwd(q4, k4, v4):
    s_len = q4.shape[1]
    tq = min(_MLA_TQ, s_len)

    def body(q_ref, k_ref, v_ref, o_ref, lse_ref):
        def block(ii):
            n_keys = (ii + 1) * tq
            mask = _mla_mask(ii, tq, n_keys)
            for hh in range(_MLA_PAIR):
                s = jnp.where(mask, _dot(q_ref[hh], k_ref[hh, :n_keys, :], "nt") * _MLA_SCALE, _NEG)
                m = jnp.max(s, axis=-1, keepdims=True)
                p = jnp.exp(s - m)
                l = jnp.sum(p, axis=-1, keepdims=True)
                o_ref[:, hh * MLA_V:(hh + 1) * MLA_V] = (_dot(p, v_ref[hh, :n_keys, :], "nn") / l).astype(o_ref.dtype)
                lse_ref[hh] = m + jnp.log(l)

        _per_query_block(pl.program_id(1), s_len // tq, block)

    return pl.pallas_call(
        body,
        name="mla_fwd",
        grid=(MLA_HEADS // _MLA_PAIR, s_len // tq),
        in_specs=[
            pl.BlockSpec((_MLA_PAIR, tq, MLA_QK), lambda h, i: (h, i, 0)),
            pl.BlockSpec((_MLA_PAIR, s_len, MLA_QK), lambda h, i: (h, 0, 0)),
            pl.BlockSpec((_MLA_PAIR, s_len, MLA_V), lambda h, i: (h, 0, 0)),
        ],
        out_specs=[
            pl.BlockSpec((tq, _MLA_PAIR * MLA_V), lambda h, i: (i, h)),
            pl.BlockSpec((_MLA_PAIR, tq, 1), lambda h, i: (h, i, 0)),
        ],
        out_shape=[_sds((s_len, MLA_HEADS * MLA_V), BF16), _sds((MLA_HEADS, s_len, 1), F32)],
        compiler_params=_cparams(("parallel", "parallel")),
    )(q4, k4, v4)


def _mla_bwd(q4, k4, v4, cat, dcat, lse):
    s_len = q4.shape[1]
    tq = min(_MLA_TQ, s_len)
    first = SWA_HEADS * SWA_HEAD_DIM // (_MLA_PAIR * MLA_V)

    def body(q_ref, k_ref, v_ref, o_ref, do_ref, lse_ref, dq_ref, dk_ref, dv_ref):
        i = pl.program_id(1)

        @pl.when(i == 0)
        def _():
            dk_ref[...] = jnp.zeros_like(dk_ref)
            dv_ref[...] = jnp.zeros_like(dv_ref)

        def block(ii):
            n_keys = (ii + 1) * tq
            mask = _mla_mask(ii, tq, n_keys)
            for hh in range(_MLA_PAIR):
                cols = slice(hh * MLA_V, (hh + 1) * MLA_V)
                q, k, v, do = q_ref[hh], k_ref[hh, :n_keys, :], v_ref[hh, :n_keys, :], do_ref[:, cols]
                s = jnp.where(mask, _dot(q, k, "nt") * _MLA_SCALE, _NEG)
                p = jnp.where(mask, jnp.exp(s - lse_ref[hh]), 0.0)
                delta = jnp.sum(do.astype(F32) * o_ref[:, cols].astype(F32), axis=-1, keepdims=True)
                ds = (p * (_dot(do, v, "nt") - delta) * _MLA_SCALE).astype(BF16)
                dq_ref[hh] = _dot(ds, k, "nn")
                dk_ref[hh, :n_keys, :] += _dot(ds, q, "tn")
                dv_ref[hh, :n_keys, :] += _dot(p, do, "tn")

        _per_query_block(i, s_len // tq, block)

    return pl.pallas_call(
        body,
        name="mla_bwd",
        grid=(MLA_HEADS // _MLA_PAIR, s_len // tq),
        in_specs=[
            pl.BlockSpec((_MLA_PAIR, tq, MLA_QK), lambda h, i: (h, i, 0)),
            pl.BlockSpec((_MLA_PAIR, s_len, MLA_QK), lambda h, i: (h, 0, 0)),
            pl.BlockSpec((_MLA_PAIR, s_len, MLA_V), lambda h, i: (h, 0, 0)),
            pl.BlockSpec((tq, _MLA_PAIR * MLA_V), lambda h, i: (i, first + h)),
            pl.BlockSpec((tq, _MLA_PAIR * MLA_V), lambda h, i: (i, first + h)),
            pl.BlockSpec((_MLA_PAIR, tq, 1), lambda h, i: (h, i, 0)),
        ],
        out_specs=[
            pl.BlockSpec((_MLA_PAIR, tq, MLA_QK), lambda h, i: (h, i, 0)),
            pl.BlockSpec((_MLA_PAIR, s_len, MLA_QK), lambda h, i: (h, 0, 0)),
            pl.BlockSpec((_MLA_PAIR, s_len, MLA_V), lambda h, i: (h, 0, 0)),
        ],
        out_shape=[
            _sds((MLA_HEADS, s_len, MLA_QK), F32),
            _sds((MLA_HEADS, s_len, MLA_QK), F32),
            _sds((MLA_HEADS, s_len, MLA_V), F32),
        ],
        compiler_params=_cparams(("parallel", "arbitrary")),
    )(q4, k4, v4, cat, dcat, lse)


def _mla_split(pm):
    q_lat = pm[:, :MLA_Q_RANK]
    kv_lat = pm[:, MLA_Q_RANK:MLA_Q_RANK + MLA_KV_RANK]
    kr = pm[:, MLA_Q_RANK + MLA_KV_RANK:MLA_Q_RANK + MLA_KV_RANK + MLA_ROPE]
    kr_sw = pm[:, MLA_Q_RANK + MLA_KV_RANK + MLA_ROPE:]
    return q_lat, kv_lat, kr, kr_sw


def _mla_proj(pm, cos2, sin2, q_norm, kv_norm, wq_ext, wkv):
    s_len = pm.shape[0]

    def fn(pm_, cos_, sin_, qg, kvg, wq, wk):
        q_lat, kv_lat, kr, kr_sw = _mla_split(pm_)
        nq = (_rms(q_lat)[0] * qg).astype(BF16)
        nkv = (_rms(kv_lat)[0] * kvg).astype(BF16)
        k_rot = kr * cos_ + kr_sw * sin_
        qs, ks, vs = [], [], []
        for h in range(MLA_HEADS):
            qe = _dot(nq, wq[h], "nt")
            q_rot = qe[:, MLA_NOPE:MLA_QK] * cos_ + qe[:, MLA_QK:] * sin_
            qs.append(jnp.concatenate([qe[:, :MLA_NOPE], q_rot], axis=-1))
            kve = _dot(nkv, wk[h], "nt")
            ks.append(jnp.concatenate([kve[:, :MLA_NOPE], k_rot], axis=-1))
            vs.append(kve[:, MLA_NOPE:])
        return jnp.stack(qs), jnp.stack(ks), jnp.stack(vs)

    return _rowwise(
        "mla_proj", fn, [pm, cos2, sin2], [q_norm, kv_norm, wq_ext, wkv],
        [_sds((MLA_HEADS, s_len, MLA_QK), BF16), _sds((MLA_HEADS, s_len, MLA_QK), BF16),
         _sds((MLA_HEADS, s_len, MLA_V), BF16)], [], 256,
    )


def _mla_proj_bwd(pm, cos2, sin2, dq4, dk4, dv4, q_norm, kv_norm, wq_ext, wkv):
    s_len = pm.shape[0]

    def fn(pm_, cos_, sin_, dq, dk, dv, qg, kvg, wq, wk):
        q_lat, kv_lat, _, _ = _mla_split(pm_)
        xq, rq = _rms(q_lat)
        xkv, rkv = _rms(kv_lat)
        nq = (xq * qg).astype(BF16)
        nkv = (xkv * kvg).astype(BF16)
        dnq = jnp.zeros_like(q_lat)
        dnkv = jnp.zeros_like(kv_lat)
        dkr = jnp.zeros_like(cos_)
        dwq, dwk = [], []
        for h in range(MLA_HEADS):
            dy = dq[h][:, MLA_NOPE:]
            dqe = jnp.concatenate([dq[h][:, :MLA_NOPE], dy * cos_, dy * sin_], axis=-1).astype(BF16)
            dnq = dnq + _dot(dqe, wq[h], "nn")
            dwq.append(_dot(dqe, nq, "tn"))
            dkve = jnp.concatenate([dk[h][:, :MLA_NOPE], dv[h]], axis=-1).astype(BF16)
            dnkv = dnkv + _dot(dkve, wk[h], "nn")
            dwk.append(_dot(dkve, nkv, "tn"))
            dkr = dkr + dk[h][:, MLA_NOPE:]
        dq_lat = _rms_bwd(dnq * qg, xq, rq)
        dkv_lat = _rms_bwd(dnkv * kvg, xkv, rkv)
        dpm = jnp.concatenate([dq_lat, dkv_lat, dkr * cos_, dkr * sin_], axis=-1)
        return dpm, _sum0(dnq * xq), _sum0(dnkv * xkv), jnp.stack(dwq), jnp.stack(dwk)

    return _rowwise(
        "mla_proj_bwd", fn, [pm, cos2, sin2, dq4, dk4, dv4], [q_norm, kv_norm, wq_ext, wkv],
        [_sds((s_len, N_MLA_COLS), BF16)],
        [_sds((1, MLA_Q_RANK), F32), _sds((1, MLA_KV_RANK), F32),
         _sds(wq_ext.shape, F32), _sds(wkv.shape, F32)], 256,
    )


def _rope_tables(s_len):
    inv = ROPE_THETA ** (-jnp.arange(0, MLA_ROPE, 2, dtype=F32) / MLA_ROPE)
    ang = jnp.arange(s_len, dtype=F32)[:, None] * inv[None, :]
    cos, sin = jnp.cos(ang), jnp.sin(ang)
    return jnp.concatenate([cos, cos], axis=-1), jnp.concatenate([-sin, sin], axis=-1)


def _mix_weights(g_mix):
    rest = g_mix[:, 0]
    w_in_t = rest[:, O_IN:O_IN + R_IN].reshape(D_IN, D_MODEL)
    w_o = rest[:, O_O:O_O + R_O].reshape(D_MODEL, D_MODEL)
    w_uq_t = rest[:, O_UQ:O_UQ + R_UQ].reshape(MLA_HEADS, MLA_QK, MLA_Q_RANK)
    w_ukv_t = rest[:, O_UKV:O_UKV + R_UKV].reshape(MLA_HEADS, MLA_NOPE + MLA_V, MLA_KV_RANK)
    half = MLA_ROPE // 2
    w_swa = w_in_t[:N_SWA_COLS]
    w_mla = jnp.concatenate([w_in_t[N_SWA_COLS:], w_in_t[D_IN - half:], w_in_t[D_IN - MLA_ROPE:D_IN - half]], axis=0)
    wq_ext = jnp.concatenate([w_uq_t, w_uq_t[:, MLA_QK - half:], w_uq_t[:, MLA_NOPE:MLA_QK - half]], axis=1)
    return w_swa, w_mla, w_o, wq_ext, w_ukv_t


def _mix_fwd(x, h, gate, mix_src, q_norm, kv_norm, bias, sinkb, cos2, sin2, next_norm):
    s_len = x.shape[0]
    tm = min(ROW_TILE, s_len)
    deps = mix_src.mid(x)
    weights = _mix_weights(mix_src.get(h))
    w_swa, w_mla, w_o, wq_ext, wkv = weights
    swa2 = _mm(
        "mix_proj_swa", "nt", (s_len // tm, 1, 1),
        h, (tm, D_MODEL), lambda i, j, k: (i, 0),
        w_swa, (N_SWA_COLS, D_MODEL), lambda i, j, k: (0, 0),
        _sds((s_len, N_SWA_COLS), BF16), (tm, N_SWA_COLS), lambda i, j, k: (i, 0),
        (tm, N_SWA_COLS), deps=deps,
    )
    pm = _mm(
        "mix_proj_mla", "nt", (s_len // tm, 1, 1),
        h, (tm, D_MODEL), lambda i, j, k: (i, 0),
        w_mla, (N_MLA_COLS, D_MODEL), lambda i, j, k: (0, 0),
        _sds((s_len, N_MLA_COLS), F32), (tm, N_MLA_COLS), lambda i, j, k: (i, 0),
        (tm, N_MLA_COLS),
    )
    q4, k4, v4 = _mla_proj(pm, cos2, sin2, q_norm, kv_norm, wq_ext, wkv)
    oa, lse_a = _swa_fwd(swa2, bias, sinkb)
    ob, lse_b = _mla_fwd(q4, k4, v4)
    cat = jnp.concatenate([oa, ob], axis=1)
    z = _mm(
        "mix_out", "nn", (s_len // tm, 1, 1),
        cat, (tm, D_MODEL), lambda i, j, k: (i, 0),
        w_o, (D_MODEL, D_MODEL), lambda i, j, k: (0, 0),
        _sds((s_len, D_MODEL), F32), (tm, D_MODEL), lambda i, j, k: (i, 0),
        (tm, D_MODEL),
    )
    def residual_norm(x_, z_, g_, gamma_, sc_, sh_):
        x_out = x_ + g_ * z_
        return x_out, _normmod(x_out, gamma_, sc_, sh_)

    out = _rowwise(
        "mix_residual_norm", residual_norm, [x, z], [gate] + list(next_norm),
        [_sds((s_len, D_MODEL), F32), _sds((s_len, D_MODEL), BF16)], [], 256,
    )
    return out, (weights, h, swa2, pm, q4, k4, v4, cat, lse_a, lse_b, z)


def _mix_bwd(dxo, x, gamma, sc, gate, q_norm, kv_norm, bias, sinkb, cos2, sin2, saved, hooks):
    weights, h, swa2, pm, q4, k4, v4, cat, lse_a, lse_b, z = saved
    w_swa, w_mla, w_o, wq_ext, wkv = weights
    s_len = x.shape[0]
    tm = tk = min(ROW_TILE, s_len)
    vec = _sds((1, D_MODEL), F32)
    n_proj = N_SWA_COLS + N_MLA_COLS
    half_d = D_MODEL // 2

    dz, dgate = _rowwise(
        "mix_bwd_gate", lambda dxo_, z_, g_: (g_ * dxo_, _sum0(z_ * dxo_)),
        [dxo, z], [gate], [_sds((s_len, D_MODEL), BF16)], [vec], 256,
    )
    dw_o = _mm(
        "mix_bwd_wo", "tn", (2, 1, s_len // tk),
        cat, (tk, half_d), lambda i, j, k: (k, i),
        dz, (tk, D_MODEL), lambda i, j, k: (k, 0),
        _sds((D_MODEL, D_MODEL), F32), (half_d, D_MODEL), lambda i, j, k: (i, 0),
        (half_d, D_MODEL),
    )
    dcat = _mm(
        "mix_bwd_dcat", "nt", (s_len // tm, 1, 1),
        dz, (tm, D_MODEL), lambda i, j, k: (i, 0),
        w_o, (D_MODEL, D_MODEL), lambda i, j, k: (0, 0),
        _sds((s_len, D_MODEL), BF16), (tm, D_MODEL), lambda i, j, k: (i, 0),
        (tm, D_MODEL), deps=hooks.after_gate(dz),
    )
    dq_a, dkva, dkvb, dbias, dsink = _swa_bwd(swa2, bias, sinkb, cat, dcat, lse_a)
    dq4, dk4, dv4 = _mla_bwd(q4, k4, v4, cat, dcat, lse_b)
    dpm, dq_norm, dkv_norm, dwq_ext, dwkv = _mla_proj_bwd(pm, cos2, sin2, dq4, dk4, dv4, q_norm, kv_norm, wq_ext, wkv)

    dkv = dkva + jnp.concatenate([dkvb[WINDOW:], jnp.zeros_like(dkvb[:WINDOW])], axis=0)
    dproj = jnp.concatenate([dq_a, dkv.astype(BF16), dpm], axis=1)
    w_proj = jnp.concatenate([w_swa, w_mla], axis=0)

    dw_proj = _mm(
        "mix_bwd_win", "tn", (n_proj // 256, 1, s_len // tk),
        dproj, (tk, 256), lambda i, j, k: (k, i),
        h, (tk, D_MODEL), lambda i, j, k: (k, 0),
        _sds((n_proj, D_MODEL), F32), (256, D_MODEL), lambda i, j, k: (i, 0),
        (256, D_MODEL),
    )
    dw_swa, dw_mla = dw_proj[:N_SWA_COLS], dw_proj[N_SWA_COLS:]
    dh = _mm(
        "mix_bwd_dh", "nn", (s_len // tm, 1, 1),
        dproj, (tm, n_proj), lambda i, j, k: (i, 0),
        w_proj, (n_proj, D_MODEL), lambda i, j, k: (0, 0),
        _sds((s_len, D_MODEL), F32), (tm, D_MODEL), lambda i, j, k: (i, 0),
        (tm, D_MODEL),
    )
    dx, dsc, dsh, dgamma = _normmod_bwd_call("mix_bwd_normmod", [dh], x, dxo, gamma, sc)

    half = MLA_ROPE // 2
    n_mla_in = D_IN - N_SWA_COLS
    dw_mla_base = dw_mla[:n_mla_in]
    dw_mla_base = dw_mla_base.at[n_mla_in - half:].add(dw_mla[n_mla_in:n_mla_in + half])
    dw_mla_base = dw_mla_base.at[n_mla_in - MLA_ROPE:n_mla_in - half].add(dw_mla[n_mla_in + half:])
    dw_in_t = jnp.concatenate([dw_swa, dw_mla_base], axis=0)
    dw_uq_t = dwq_ext[:, :MLA_QK]
    dw_uq_t = dw_uq_t.at[:, MLA_QK - half:].add(dwq_ext[:, MLA_QK:MLA_QK + half])
    dw_uq_t = dw_uq_t.at[:, MLA_NOPE:MLA_QK - half].add(dwq_ext[:, MLA_QK + half:])
    rest = jnp.concatenate(
        [
            dw_in_t.reshape(N_CHIPS, R_IN, D_MODEL),
            dw_o.reshape(N_CHIPS, R_O, D_MODEL),
            dw_uq_t.reshape(N_CHIPS, R_UQ, D_MODEL),
            dwkv.reshape(N_CHIPS, R_UKV, D_MODEL),
            jnp.zeros((N_CHIPS, F_SHARD - O_PAD, D_MODEL), F32),
        ],
        axis=1,
    ).astype(BF16)
    return dx, rest, (dsh, dsc, dgate, dgamma), dq_norm, dkv_norm, dbias, dsink


def _device_step(x, target, mod, gu1_src, down1_src, mix_src, ffn2_src, norms, q_norm, kv_norm, sinks, rel_bias,
                 hooks2=None, hooks_mix=None, mix_done=None, hooks1=None):
    s_len = x.shape[0]
    sh1, sc1, g1, sh2, sc2, g2, sh3, sc3, g3 = [mod[:, i * D_MODEL:(i + 1) * D_MODEL] for i in range(N_MOD)]
    n1, n2, n3, nf = norms
    bkt = jnp.asarray(_bucket_map())
    bias = _bias_expand(rel_bias.T, bkt).reshape(SWA_HEADS, WINDOW, 2 * WINDOW)
    sinkb = jnp.broadcast_to(sinks.reshape(SWA_HEADS, 1, 1), (SWA_HEADS, WINDOW, 1))
    cos2, sin2 = _rope_tables(s_len)

    (x1, h2), saved1 = _ffn_fwd(
        "ffn1", x, n1, sh1, sc1, g1, gu1_src, 0, down1_src, 0, sh1, next_norm=(n2, sc2, sh2)
    )
    (x2, h3), saved2 = _mix_fwd(
        x1, h2, g2, mix_src, q_norm, kv_norm, bias, sinkb, cos2, sin2, next_norm=(n3, sc3, sh3)
    )
    (dx3, sq, dnf), saved3 = _ffn_fwd(
        "ffn2", x2, n3, sh3, sc3, g3, ffn2_src, 0, None, 2, x2, h_pre=h3, head=(target, nf)
    )
    loss_part = (0.5 / D_MODEL) * jnp.sum(sq)

    dx2, gb2, (dsh3, dsc3, dg3, dn3) = _ffn_bwd("ffn2", dx3, x2, n3, sc3, g3, saved3, hooks2 or _BwdHooks())
    dx1, rest, (dsh2, dsc2, dg2, dn2), dq_norm, dkv_norm, dbias, dsink = _mix_bwd(
        dx2, x1, n2, sc2, g2, q_norm, kv_norm, bias, sinkb, cos2, sin2, saved2, hooks_mix or _BwdHooks()
    )
    rest = rest[:, None]
    deps1 = mix_done(dx1, rest) if mix_done is not None else []
    dx0, gb1, (dsh1, dsc1, dg1, dn1) = _ffn_bwd(
        "ffn1", dx1, x, n1, sc1, g1, saved1, hooks1 or _BwdHooks(), deps=deps1
    )

    dmod = jnp.concatenate([dsh1, dsc1, dg1, dsh2, dsc2, dg2, dsh3, dsc3, dg3], axis=-1)
    drel = _bias_reduce(dbias.reshape(SWA_HEADS, -1), bkt).T
    dsinks = jnp.sum(dsink, axis=(1, 2)).reshape(1, SWA_HEADS)
    small = (dn1, dn2, dn3, dnf, dq_norm, dkv_norm, dsinks, drel)
    return loss_part, dx0, (gb1, gb2, rest), dmod, small


_MESH = pl.DeviceIdType.MESH


def _place():
    return lax.axis_index("x"), lax.axis_index("y"), lax.axis_index("c")


def _other_chips(x, y):
    return [(1 - x, y), (x, 1 - y), (1 - x, 1 - y)]


def _allgather_small(name, blk):
    m_per, n = blk.shape

    def body(x_ref, out_ref, send_sems, recv_sems, local_sem):
        x, y, c = _place()
        me, sibling = (x, y, c), (x, y, 1 - c)
        chips = _other_chips(x, y)

        def rows(px, py, pc):
            return out_ref.at[pl.ds((4 * px + 2 * py + pc) * m_per, m_per), :]

        def copy(k, block, to, src=None):
            return pltpu.make_async_remote_copy(
                src_ref=rows(*block) if src is None else src, dst_ref=rows(*block),
                send_sem=send_sems.at[k], recv_sem=recv_sems.at[k], device_id=to, device_id_type=_MESH,
            )

        mine = pltpu.make_async_copy(x_ref, rows(*me), local_sem)
        mine.start()
        first = [copy(0, me, sibling, src=x_ref)]
        first += [copy(1 + j, me, (*chip, c), src=x_ref) for j, chip in enumerate(chips)]
        for cp in first:
            cp.start()
        passed = [copy(4 + j, (*chip, c), sibling) for j, chip in enumerate(chips)]
        for j, chip in enumerate(chips):
            copy(1 + j, (*chip, c), me).wait_recv()
            passed[j].start()
        copy(0, sibling, me).wait_recv()
        for j, chip in enumerate(chips):
            copy(4 + j, (*chip, 1 - c), me).wait_recv()
        for cp in first + passed:
            cp.wait_send()
        mine.wait()

    return pl.pallas_call(
        body,
        name=name,
        out_shape=_sds((N_DEV * m_per, n), blk.dtype),
        in_specs=[pl.BlockSpec(memory_space=pltpu.VMEM)],
        out_specs=pl.BlockSpec(memory_space=pltpu.VMEM),
        scratch_shapes=[pltpu.SemaphoreType.DMA((7,)), pltpu.SemaphoreType.DMA((7,)), pltpu.SemaphoreType.DMA],
    )(blk)


def _gather_sends(n, own_ref, g_ref, send_sem, recv_sem, x, y, c, chip):
    return [
        pltpu.make_async_remote_copy(
            src_ref=own_ref.at[p, pl.ds(c * HALF, HALF), :], dst_ref=g_ref.at[2 * x + y, p, pl.ds(c * HALF, HALF), :],
            send_sem=send_sem, recv_sem=recv_sem, device_id=(*chip, c), device_id_type=_MESH,
        )
        for p in range(n)
    ]


def _gather_forwards(n, g_ref, send_sem, recv_sem, chip, c, sibling):
    return [
        pltpu.make_async_remote_copy(
            src_ref=g_ref.at[2 * chip[0] + chip[1], p, pl.ds(c * HALF, HALF), :],
            dst_ref=g_ref.at[2 * chip[0] + chip[1], p, pl.ds(c * HALF, HALF), :],
            send_sem=send_sem, recv_sem=recv_sem, device_id=sibling, device_id_type=_MESH,
        )
        for p in range(n)
    ]


def _gather_all(own_ref, g_ref, send_sem, recv_sem, chip, half, c):
    return pltpu.make_async_remote_copy(
        src_ref=own_ref.at[:, pl.ds(c * HALF, HALF), :],
        dst_ref=g_ref.at[2 * chip[0] + chip[1], :, pl.ds(half * HALF, HALF), :],
        send_sem=send_sem, recv_sem=recv_sem, device_id=(*chip, c), device_id_type=_MESH,
    )


_HBM_SPEC = pl.BlockSpec(memory_space=pltpu.HBM)
_SEM_SPEC = pl.BlockSpec(memory_space=pltpu.SEMAPHORE)
_ANY_SPEC = pl.BlockSpec(memory_space=pl.ANY)
_VMEM_SPEC = pl.BlockSpec(memory_space=pltpu.VMEM)
_SPLIT_PARAMS = pltpu.CompilerParams(has_side_effects=pltpu.SideEffectType.DATAFLOW_SIDE_EFFECTING)
_TOKEN = jax.ShapeDtypeStruct((8, 128), F32)


def _in_hbm(a):
    return pltpu.with_memory_space_constraint(a, pltpu.HBM)


class _GatherBehind:
    def __init__(self, tag, own, then=None):
        self.tag, self.n, self.own, self.then = tag, own.shape[0], own, then

    def start(self, after):
        tag, own, n = self.tag, self.own, self.n
        x, y, _ = _place()
        g_init = lax.dynamic_update_slice(
            lax.empty((N_CHIPS,) + own.shape, own.dtype), own[None], (2 * x + y, 0, 0, 0)
        )

        def body(own_ref, g_ref, *rest):
            send_sems, recv_sems, _, _, token = rest[len(after):]
            x, y, c = _place()
            for j, chip in enumerate(_other_chips(x, y)):
                for cp in _gather_sends(n, own_ref, g_ref, send_sems.at[j], recv_sems.at[j], x, y, c, chip):
                    cp.start()
            token[...] = jnp.zeros_like(token)

        self.send1, self.recv1, self.own, self.g, self.token = pl.pallas_call(
            body,
            name=f"{tag}_start",
            out_shape=(
                pltpu.SemaphoreType.DMA((3,)), pltpu.SemaphoreType.DMA((3,)),
                pltpu.HBM(own.shape, own.dtype), pltpu.HBM(g_init.shape, g_init.dtype), _TOKEN,
            ),
            in_specs=(_HBM_SPEC, _HBM_SPEC) + (_ANY_SPEC,) * len(after),
            out_specs=(_SEM_SPEC, _SEM_SPEC, _HBM_SPEC, _HBM_SPEC, _VMEM_SPEC),
            input_output_aliases={0: 2, 1: 3},
            compiler_params=_SPLIT_PARAMS,
        )(_in_hbm(own), _in_hbm(g_init), *after)

    def mid(self, after):
        n = self.n

        def body(own_ref, g_ref, send1, recv1, after_ref, send2, recv2, g_out, token):
            x, y, c = _place()
            sibling = (x, y, 1 - c)
            chips = _other_chips(x, y)
            for j, chip in enumerate(chips):
                _gather_all(own_ref, g_ref, send1.at[j], recv1.at[j], chip, c, c).wait_recv()
                for cp in _gather_forwards(n, g_ref, send2.at[j], recv2.at[j], chip, c, sibling):
                    cp.start()
            for j, chip in enumerate(chips):
                _gather_all(own_ref, g_ref, send1.at[j], recv1.at[j], chip, c, c).wait_send()
            token[...] = jnp.zeros_like(token)

        self.send2, self.recv2, self.g, token = pl.pallas_call(
            body,
            name=f"{self.tag}_mid",
            out_shape=(
                pltpu.SemaphoreType.DMA((3,)), pltpu.SemaphoreType.DMA((3,)),
                pltpu.HBM(self.g.shape, self.g.dtype), _TOKEN,
            ),
            in_specs=(_HBM_SPEC, _HBM_SPEC, _SEM_SPEC, _SEM_SPEC, _ANY_SPEC),
            out_specs=(_SEM_SPEC, _SEM_SPEC, _HBM_SPEC, _VMEM_SPEC),
            input_output_aliases={1: 2},
            compiler_params=_SPLIT_PARAMS,
        )(self.own, self.g, self.send1, self.recv1, after)
        if self.then is None:
            return [token]
        self.then.start([token])
        return [token, self.then.token]

    def get(self, after):
        def body(g_ref, send2, recv2, after_ref, g_out):
            x, y, c = _place()
            for j, chip in enumerate(_other_chips(x, y)):
                slot_in = g_ref.at[2 * chip[0] + chip[1], :, pl.ds((1 - c) * HALF, HALF), :]
                slot_out = g_ref.at[2 * chip[0] + chip[1], :, pl.ds(c * HALF, HALF), :]
                cp = pltpu.make_async_remote_copy(
                    src_ref=slot_out, dst_ref=slot_in, send_sem=send2.at[j], recv_sem=recv2.at[j],
                    device_id=(x, y, 1 - c), device_id_type=_MESH,
                )
                cp.wait_send()
                cp.wait_recv()

        return pl.pallas_call(
            body,
            name=f"{self.tag}_wait",
            out_shape=pltpu.HBM(self.g.shape, self.g.dtype),
            in_specs=(_HBM_SPEC, _SEM_SPEC, _SEM_SPEC, _ANY_SPEC),
            out_specs=_HBM_SPEC,
            input_output_aliases={0: 0},
            compiler_params=_SPLIT_PARAMS,
        )(self.g, self.send2, self.recv2, after)


def _pair_sum(name, gb, land):
    def body(c_ref, gb_ref, land_ref, o_ref):
        o_ref[...] = (gb_ref[...].astype(F32) + land_ref[...].astype(F32)).astype(o_ref.dtype)

    core = lax.axis_index("c").astype(jnp.int32).reshape(1)
    return pl.pallas_call(
        body,
        name=name,
        grid_spec=pltpu.PrefetchScalarGridSpec(
            num_scalar_prefetch=1,
            grid=(N_CHIPS, gb.shape[1]),
            in_specs=[
                pl.BlockSpec((None, None, HALF, D_MODEL), lambda j, p, c: (j, p, c[0], 0)),
                pl.BlockSpec((None, None, HALF, D_MODEL), lambda j, p, c: (j, p, 0, 0)),
            ],
            out_specs=pl.BlockSpec((None, None, HALF, D_MODEL), lambda j, p, c: (j, p, 0, 0)),
        ),
        out_shape=_sds(land.shape, BF16),
        compiler_params=_cparams(("parallel", "parallel")),
    )(core, gb, land)


def _chip_sum_share(name, land):
    n = land.shape[1]

    def body(l_ref, r_ref, buf, send_sems, recv_sems, local_sems):
        p = pl.program_id(0)
        x, y, c = _place()
        acc = l_ref[0].astype(F32)
        for j in range(1, N_CHIPS):
            acc = acc + l_ref[j].astype(F32)
        buf[p] = acc

        def copies(q):
            mine = r_ref.at[q, pl.ds(c * HALF, HALF), :]
            theirs = r_ref.at[q, pl.ds((1 - c) * HALF, HALF), :]
            local = pltpu.make_async_copy(buf.at[q], mine, local_sems.at[q])
            out = pltpu.make_async_remote_copy(
                src_ref=buf.at[q], dst_ref=mine, send_sem=send_sems.at[q], recv_sem=recv_sems.at[q],
                device_id=(x, y, 1 - c), device_id_type=_MESH,
            )
            arrive = pltpu.make_async_remote_copy(
                src_ref=buf.at[q], dst_ref=theirs, send_sem=send_sems.at[q], recv_sem=recv_sems.at[q],
                device_id=(x, y, 1 - c), device_id_type=_MESH,
            )
            return local, out, arrive

        local, out, _ = copies(p)
        local.start()
        out.start()

        @pl.when(p == n - 1)
        def _():
            for q in range(n):
                local, out, arrive = copies(q)
                arrive.wait_recv()
                out.wait_send()
                local.wait()

    return pl.pallas_call(
        body,
        name=name,
        grid=(n,),
        in_specs=[pl.BlockSpec((N_CHIPS, None, HALF, D_MODEL), lambda p: (0, p, 0, 0))],
        out_specs=pl.BlockSpec(memory_space=pl.ANY),
        out_shape=_sds((n, F_SHARD, D_MODEL), F32),
        scratch_shapes=[
            pltpu.VMEM((n, HALF, D_MODEL), F32),
            pltpu.SemaphoreType.DMA((n,)), pltpu.SemaphoreType.DMA((n,)), pltpu.SemaphoreType.DMA((n,)),
        ],
        compiler_params=_cparams(("arbitrary",)),
    )(land)


class _ReduceBehind:
    def __init__(self, tag, gb, after=()):
        self.tag = tag
        n = gb.shape[1]
        land = lax.empty((N_CHIPS, n, HALF, D_MODEL), gb.dtype)

        def body(gb_ref, land_ref, *rest):
            send_sem, recv_sem, _, _, token = rest[len(after):]
            self._pair_copy(gb_ref, land_ref, send_sem, recv_sem).start()
            token[...] = jnp.zeros_like(token)

        self.send, self.recv, self.gb, self.land, self.token = pl.pallas_call(
            body,
            name=f"grads_pair_start_{tag}",
            out_shape=(
                pltpu.SemaphoreType.DMA((1,)), pltpu.SemaphoreType.DMA((1,)),
                pltpu.HBM(gb.shape, gb.dtype), pltpu.HBM(land.shape, land.dtype), _TOKEN,
            ),
            in_specs=(_HBM_SPEC, _HBM_SPEC) + (_ANY_SPEC,) * len(after),
            out_specs=(_SEM_SPEC, _SEM_SPEC, _HBM_SPEC, _HBM_SPEC, _VMEM_SPEC),
            input_output_aliases={0: 2, 1: 3},
            compiler_params=_SPLIT_PARAMS,
        )(_in_hbm(gb), _in_hbm(land), *after)

    @staticmethod
    def _pair_copy(gb_ref, land_ref, send_sem, recv_sem):
        x, y, c = _place()
        return pltpu.make_async_remote_copy(
            src_ref=gb_ref.at[:, :, pl.ds((1 - c) * HALF, HALF), :], dst_ref=land_ref,
            send_sem=send_sem.at[0], recv_sem=recv_sem.at[0], device_id=(x, y, 1 - c), device_id_type=_MESH,
        )

    @staticmethod
    def _chip_copies(p_ref, land_ref, send_sems, recv_sems):
        x, y, c = _place()
        me = 2 * x + y
        sends, arrivals = [], []
        for k, chip in enumerate(_other_chips(x, y)):
            them = 2 * chip[0] + chip[1]
            sends.append(pltpu.make_async_remote_copy(
                src_ref=p_ref.at[them], dst_ref=land_ref.at[me], send_sem=send_sems.at[k], recv_sem=recv_sems.at[k],
                device_id=(*chip, c), device_id_type=_MESH,
            ))
            arrivals.append(pltpu.make_async_remote_copy(
                src_ref=p_ref.at[me], dst_ref=land_ref.at[them], send_sem=send_sems.at[k], recv_sem=recv_sems.at[k],
                device_id=(*chip, c), device_id_type=_MESH,
            ))
        return sends, arrivals

    def mid(self, after):
        tag = self.tag

        def wait_body(gb_ref, land_ref, send_sem, recv_sem, after_ref, gb_out, land_out):
            cp = self._pair_copy(gb_ref, land_ref, send_sem, recv_sem)
            cp.wait_send()
            cp.wait_recv()

        gb, land = pl.pallas_call(
            wait_body,
            name=f"grads_pair_wait_{tag}",
            out_shape=(pltpu.HBM(self.gb.shape, self.gb.dtype), pltpu.HBM(self.land.shape, self.land.dtype)),
            in_specs=(_HBM_SPEC, _HBM_SPEC, _SEM_SPEC, _SEM_SPEC, _ANY_SPEC),
            out_specs=(_HBM_SPEC, _HBM_SPEC),
            input_output_aliases={0: 0, 1: 1},
            compiler_params=_SPLIT_PARAMS,
        )(self.gb, self.land, self.send, self.recv, after)
        part = _pair_sum(f"grads_pair_sum_{tag}", gb, land)

        x, y, _ = _place()
        start = (2 * x + y, 0, 0, 0)
        own = lax.dynamic_slice(part, start, (1,) + part.shape[1:])
        land2 = lax.dynamic_update_slice(lax.empty(part.shape, part.dtype), own, start)

        def start_body(p_ref, land_ref, send_sems, recv_sems, p_out, land_out, token):
            for cp in self._chip_copies(p_ref, land_ref, send_sems, recv_sems)[0]:
                cp.start()
            token[...] = jnp.zeros_like(token)

        self.send2, self.recv2, self.part, self.land2, token = pl.pallas_call(
            start_body,
            name=f"grads_chip_start_{tag}",
            out_shape=(
                pltpu.SemaphoreType.DMA((3,)), pltpu.SemaphoreType.DMA((3,)),
                pltpu.HBM(part.shape, part.dtype), pltpu.HBM(land2.shape, land2.dtype), _TOKEN,
            ),
            in_specs=(_HBM_SPEC, _HBM_SPEC),
            out_specs=(_SEM_SPEC, _SEM_SPEC, _HBM_SPEC, _HBM_SPEC, _VMEM_SPEC),
            input_output_aliases={0: 2, 1: 3},
            compiler_params=_SPLIT_PARAMS,
        )(_in_hbm(part), _in_hbm(land2))
        return [token]

    def get(self, after):
        n_after = len(after)

        def wait_body(p_ref, land_ref, send_sems, recv_sems, *rest):
            sends, arrivals = self._chip_copies(p_ref, land_ref, send_sems, recv_sems)
            for cp in arrivals:
                cp.wait_recv()
            for cp in sends:
                cp.wait_send()

        _, land2 = pl.pallas_call(
            wait_body,
            name=f"grads_chip_wait_{self.tag}",
            out_shape=(pltpu.HBM(self.part.shape, self.part.dtype), pltpu.HBM(self.land2.shape, self.land2.dtype)),
            in_specs=(_HBM_SPEC, _HBM_SPEC, _SEM_SPEC, _SEM_SPEC) + (_ANY_SPEC,) * n_after,
            out_specs=(_HBM_SPEC, _HBM_SPEC),
            input_output_aliases={0: 0, 1: 1},
            compiler_params=_SPLIT_PARAMS,
        )(self.part, self.land2, self.send2, self.recv2, *after)
        return _chip_sum_share(f"grads_chip_sum_share_{self.tag}", land2)


def _allreduce_small(blk):
    gathered = _allgather_small("allgather_small_grads", blk).reshape(N_DEV, PK_ROWS, 128)

    def body(g_ref, o_ref):
        acc = g_ref[0]
        for k in range(1, N_DEV):
            acc = acc + g_ref[k]
        o_ref[...] = acc

    total = pl.pallas_call(body, name="small_grads_sum", out_shape=_sds((PK_ROWS, 128), F32))(gathered)
    return gathered, total


def _adamw_math(w_, g_, m_, v_):
    m_new = ADAM_B1 * m_ + (1.0 - ADAM_B1) * g_
    v_new = ADAM_B2 * v_ + (1.0 - ADAM_B2) * (g_ * g_)
    m_hat = m_new / (1.0 - ADAM_B1 ** ADAM_STEP)
    v_hat = v_new / (1.0 - ADAM_B2 ** ADAM_STEP)
    delta = -ADAM_LR * (m_hat / (jnp.sqrt(v_hat) + ADAM_EPS) + ADAM_WD * w_)
    return delta, m_new, v_new


def _adamw(name, w, g, m, v):
    rows, cols = w.shape
    tm = rows
    while tm * cols * 4 > (1 << 20) and tm % 16 == 0:
        tm //= 2
    out = _sds(w.shape, F32)
    return _rowwise(name, _adamw_math, [w, g, m, v], [], [out, out, out], [], tm)


def _adamw_small(ws, gs, ms, vs):
    n = len(ws)
    shapes = [w.shape for w in ws]
    as2d = lambda a: a.reshape(1, -1) if a.ndim == 1 else a

    def body(*refs):
        ins, outs = refs[:4 * n], refs[4 * n:]
        for k in range(n):
            res = _adamw_math(*[ins[j * n + k][...] for j in range(4)])
            for j in range(3):
                outs[j * n + k][...] = res[j]

    args = [as2d(a) for group in (ws, gs, ms, vs) for a in group]
    out = pl.pallas_call(
        body, name="adamw_small", out_shape=[_sds(as2d(w).shape, F32) for w in ws] * 3, compiler_params=_cparams()
    )(*args)
    back = [o.reshape(shapes[i % n]) for i, o in enumerate(out)]
    return back[:n], back[n:2 * n], back[2 * n:]


def _pack_small(b_mod_like, n1, n2, n3, nf, qn, kvn, sinks, rel):
    parts = [
        b_mod_like.reshape(PK_MOD, 128),
        n1.reshape(8, 128), n2.reshape(8, 128), n3.reshape(8, 128), nf.reshape(8, 128),
        qn.reshape(2, 128), kvn.reshape(1, 128),
        jnp.pad(sinks.reshape(1, SWA_HEADS), ((0, 0), (0, 128 - SWA_HEADS))),
        rel.reshape(2, 128),
        jnp.zeros((2, 128), F32),
    ]
    return jnp.concatenate(parts, axis=0)


def _unpack_small(p):
    o = PK_MOD
    return (
        p[:o].reshape(1, N_MOD * D_MODEL),
        p[o:o + 8].reshape(1, D_MODEL), p[o + 8:o + 16].reshape(1, D_MODEL),
        p[o + 16:o + 24].reshape(1, D_MODEL), p[o + 24:o + 32].reshape(D_MODEL),
        p[o + 32:o + 34].reshape(1, MLA_Q_RANK), p[o + 34:o + 35].reshape(1, MLA_KV_RANK),
        p[o + 35:o + 36, :SWA_HEADS].reshape(1, SWA_HEADS),
        p[o + 36:o + 38].reshape(NUM_BUCKETS, SWA_HEADS),
    )


def kernel(x, c, w_mod, b_mod, norm_ffn1, ffn1_gate, ffn1_up, ffn1_down, norm_mix, w_in, q_norm, kv_norm, w_uq, w_ukv, sinks, w_o, norm_ffn2, ffn2_gate, ffn2_up, ffn2_down, rel_bias, norm_final, loss_target, m_w_mod, m_b_mod, m_norm_ffn1, m_ffn1_gate, m_ffn1_up, m_ffn1_down, m_norm_mix, m_w_in, m_q_norm, m_kv_norm, m_w_uq, m_w_ukv, m_sinks, m_w_o, m_norm_ffn2, m_ffn2_gate, m_ffn2_up, m_ffn2_down, m_rel_bias, m_norm_final, v_w_mod, v_b_mod, v_norm_ffn1, v_ffn1_gate, v_ffn1_up, v_ffn1_down, v_norm_mix, v_w_in, v_q_norm, v_kv_norm, v_w_uq, v_w_ukv, v_sinks, v_w_o, v_norm_ffn2, v_ffn2_gate, v_ffn2_up, v_ffn2_down, v_rel_bias, v_norm_final):
    ax, ay, ac = _place()
    chip = 2 * ax + ay
    dev = 2 * chip + ac
    n_mod_shard = N_MOD * D_MODEL // N_CHIPS

    def t16(w):
        return w[0].T.astype(BF16)

    rest = jnp.concatenate(
        [
            t16(w_in),
            w_o[0].astype(BF16),
            t16(w_uq).reshape(R_UQ, D_MODEL),
            t16(w_ukv).reshape(R_UKV, D_MODEL),
            jnp.zeros((F_SHARD - O_PAD, D_MODEL), BF16),
        ],
        axis=0,
    )
    own_gu1 = jnp.stack([t16(ffn1_gate), t16(ffn1_up)])
    own_down1 = ffn1_down[0].astype(BF16)[None]
    own_mix = rest[None]
    own_ffn2 = jnp.stack([t16(ffn2_gate), t16(ffn2_up), ffn2_down[0].astype(BF16)])

    (c_act,) = _rowwise(
        "silu_c", lambda v: v * _sigmoid(v), [c.reshape(8, 128)], [], [_sds((8, 128), F32)], [], 8
    )
    c_all = _allgather_small("allgather_c", c_act).reshape(N_DEV, D_MODEL)
    mod_cols = _mm(
        "mod_fwd", "nn", (1, n_mod_shard // 768, 1),
        c_all, (N_DEV, D_MODEL), lambda i, j, k: (0, 0),
        w_mod[0], (D_MODEL, 768), lambda i, j, k: (0, j),
        _sds((N_DEV, n_mod_shard), F32), (N_DEV, 768), lambda i, j, k: (0, j),
        (N_DEV, 768),
    )
    mod_all = _allgather_small("allgather_mod", mod_cols).reshape(N_CHIPS, 2, N_DEV, n_mod_shard)[:, 0]
    mod_all = mod_all.transpose(1, 0, 2).reshape(N_DEV, N_MOD * D_MODEL)
    mod = lax.dynamic_slice_in_dim(mod_all, dev, 1, axis=0) + b_mod

    norms = (norm_ffn1, norm_mix, norm_ffn2, norm_final.reshape(1, D_MODEL))

    ffn2_src = _GatherBehind("gather_ffn2", own_ffn2)
    mix_src = _GatherBehind("gather_mix", own_mix, then=ffn2_src)
    down1_src = _GatherBehind("gather_down1", own_down1, then=mix_src)
    gu1_src = _GatherBehind("gather_gu1", own_gu1, then=down1_src)
    gu1_src.start([mod_all])

    reducing, red = {}, {}

    def begin(tag, gb, after):
        reducing[tag] = _ReduceBehind(tag, gb, after=after)
        return [reducing[tag].token]

    def ffn2_gu_done(gb):
        return begin("ffn2_gu", gb, reducing["ffn2_down"].mid(gb))

    def mix_done(dx1, gb_rest):
        red["ffn2_down"] = reducing["ffn2_down"].get([dx1])
        red["ffn2_gu"] = reducing["ffn2_gu"].get([red["ffn2_down"]])
        return begin("rest", gb_rest, [red["ffn2_gu"]])

    def ffn1_gu_done(gb):
        red["rest"] = reducing["rest"].get([gb])
        begin("ffn1_gu", gb, reducing["ffn1_down"].mid(red["rest"]))
        return reducing["ffn1_gu"].mid(reducing["ffn1_gu"].token)

    loss_part, grad_x, _, dmod, small = _device_step(
        x[0], loss_target[0], mod, gu1_src, down1_src, mix_src, ffn2_src, norms, q_norm, kv_norm, sinks, rel_bias,
        hooks2=_BwdHooks(after_wdown=lambda gb: begin("ffn2_down", gb, ()), after_wgu=ffn2_gu_done),
        hooks_mix=_BwdHooks(after_gate=lambda dz: reducing["ffn2_gu"].mid(dz)),
        mix_done=mix_done,
        hooks1=_BwdHooks(
            after_swiglu=lambda dab3: reducing["rest"].mid(dab3),
            after_wdown=lambda gb: begin("ffn1_down", gb, ()),
            after_wgu=ffn1_gu_done,
        ),
    )
    loss = lax.psum(loss_part, ("x", "y", "c"))

    gathered, total = _allreduce_small(_pack_small(dmod, *small))
    (g_b_mod, g_n1, g_n2, g_n3, g_nf, g_qn, g_kvn, g_sinks, g_rel) = _unpack_small(total)
    dmod_all = gathered[:, :PK_MOD].reshape(N_DEV, N_MOD * D_MODEL)
    dmod_cols = lax.dynamic_slice_in_dim(dmod_all, chip * n_mod_shard, n_mod_shard, axis=1)
    g_w_mod = _mm(
        "mod_bwd", "tn", (1, n_mod_shard // 768, 1),
        c_all, (N_DEV, D_MODEL), lambda i, j, k: (0, 0),
        dmod_cols, (N_DEV, 768), lambda i, j, k: (0, j),
        _sds((D_MODEL, n_mod_shard), F32), (D_MODEL, 768), lambda i, j, k: (0, j),
        (D_MODEL, 768),
    )

    r_rest = red["rest"][0]
    transposed = {"ffn1_gate", "ffn1_up", "ffn2_gate", "ffn2_up", "w_in", "w_uq", "w_ukv"}
    g_big = {
        "ffn2_gate": red["ffn2_gu"][0], "ffn2_up": red["ffn2_gu"][1], "ffn2_down": red["ffn2_down"][0],
        "w_in": r_rest[O_IN:O_IN + R_IN],
        "w_o": r_rest[O_O:O_O + R_O],
        "w_uq": r_rest[O_UQ:O_UQ + R_UQ].reshape(MLA_QK, MLA_Q_RANK),
        "w_ukv": r_rest[O_UKV:O_UKV + R_UKV].reshape(MLA_NOPE + MLA_V, MLA_KV_RANK),
        "w_mod": g_w_mod,
    }
    w_big = {
        "ffn2_gate": (ffn2_gate, m_ffn2_gate, v_ffn2_gate),
        "ffn2_up": (ffn2_up, m_ffn2_up, v_ffn2_up), "ffn2_down": (ffn2_down, m_ffn2_down, v_ffn2_down),
        "w_in": (w_in, m_w_in, v_w_in), "w_o": (w_o, m_w_o, v_w_o), "w_uq": (w_uq, m_w_uq, v_w_uq),
        "w_ukv": (w_ukv, m_w_ukv, v_w_ukv), "w_mod": (w_mod, m_w_mod, v_w_mod),
    }
    grads, deltas, new_m, new_v = {}, {}, {}, {}

    def update(names):
        for nm in names:
            w, m, v = w_big[nm]
            if nm in transposed:
                outs = _adamw(f"adamw_{nm}", w[0].T, g_big[nm], m[0].T, v[0].T)
                g_, d_, m_, v_ = [a.T for a in (g_big[nm],) + tuple(outs)]
            else:
                g_, (d_, m_, v_) = g_big[nm], _adamw(f"adamw_{nm}", w[0], g_big[nm], m[0], v[0])
            grads[nm], deltas[nm], new_m[nm], new_v[nm] = g_[None], d_[None], m_[None], v_[None]

    update(list(w_big))
    red1_down = reducing["ffn1_down"].get([deltas[nm] for nm in w_big])
    red1_gu = reducing["ffn1_gu"].get([red1_down])
    g_big.update({"ffn1_gate": red1_gu[0], "ffn1_up": red1_gu[1], "ffn1_down": red1_down[0]})
    w_big.update({
        "ffn1_gate": (ffn1_gate, m_ffn1_gate, v_ffn1_gate), "ffn1_up": (ffn1_up, m_ffn1_up, v_ffn1_up),
        "ffn1_down": (ffn1_down, m_ffn1_down, v_ffn1_down),
    })
    update(["ffn1_gate", "ffn1_up", "ffn1_down"])

    small_names = ["b_mod", "norm_ffn1", "norm_mix", "norm_ffn2", "norm_final", "q_norm", "kv_norm", "sinks", "rel_bias"]
    w_small = (b_mod, norm_ffn1, norm_mix, norm_ffn2, norm_final, q_norm, kv_norm, sinks, rel_bias)
    m_small = (m_b_mod, m_norm_ffn1, m_norm_mix, m_norm_ffn2, m_norm_final, m_q_norm, m_kv_norm, m_sinks, m_rel_bias)
    v_small = (v_b_mod, v_norm_ffn1, v_norm_mix, v_norm_ffn2, v_norm_final, v_q_norm, v_kv_norm, v_sinks, v_rel_bias)
    g_small = (g_b_mod, g_n1, g_n2, g_n3, g_nf, g_qn, g_kvn, g_sinks, g_rel)
    d_s, m_s, v_s = _adamw_small(w_small, g_small, m_small, v_small)
    for nm, g_, d_, m_, v_ in zip(small_names, g_small, d_s, m_s, v_s):
        grads[nm], deltas[nm], new_m[nm], new_v[nm] = g_, d_, m_, v_

    order = ["w_mod", "b_mod", "norm_ffn1", "ffn1_gate", "ffn1_up", "ffn1_down", "norm_mix", "w_in", "q_norm", "kv_norm",
             "w_uq", "w_ukv", "sinks", "w_o", "norm_ffn2", "ffn2_gate", "ffn2_up", "ffn2_down", "rel_bias", "norm_final"]
    return (loss, grad_x[None], *[grads[n] for n in order], *[deltas[n] for n in order],
            *[new_m[n] for n in order], *[new_v[n] for n in order])
```

```python
import functools
import math

import jax
import jax.numpy as jnp
import numpy as np
from jax import lax
from jax.experimental import pallas as pl
from jax.experimental.pallas import tpu as pltpu

F32, BF16 = jnp.float32, jnp.bfloat16

D_MODEL = 1024
D_FF = 2816
EPS = 1e-6
N_MOD = 9
SWA_HEADS, SWA_KV_HEADS, SWA_HEAD_DIM, WINDOW = 8, 2, 64, 128
SWA_GROUP = SWA_HEADS // SWA_KV_HEADS
MLA_HEADS, MLA_Q_RANK, MLA_KV_RANK, MLA_NOPE, MLA_ROPE, MLA_V = 4, 256, 128, 128, 64, 128
MLA_QK = MLA_NOPE + MLA_ROPE
ROPE_THETA = 10000.0
NUM_BUCKETS, MAX_DISTANCE = 32, 128
D_IN = 1216
N_SWA_COLS = 768
N_MLA_COLS = 512

ADAM_LR, ADAM_B1, ADAM_B2, ADAM_EPS, ADAM_WD, ADAM_STEP = 0.001, 0.9, 0.999, 1e-08, 0.01, 10

N_CHIPS = 4
N_DEV = 8
F_SHARD = D_FF // N_CHIPS
HALF = F_SHARD // 2
R_IN, R_O, R_UQ, R_UKV = 304, 256, 48, 32
O_IN, O_O, O_UQ, O_UKV, O_PAD = 0, 304, 560, 608, 640

PK_MOD = 72
PK_ROWS = 112
VMEM_LIMIT = 56 << 20
ROW_TILE = 2048
ELEM_TILE = 512

_DN = {
    "nn": (((1,), (0,)), ((), ())),
    "nt": (((1,), (1,)), ((), ())),
    "tn": (((0,), (0,)), ((), ())),
}


def _sds(shape, dtype):
    return jax.ShapeDtypeStruct(tuple(shape), dtype)


def _cparams(sem=None):
    kw = {"vmem_limit_bytes": VMEM_LIMIT}
    if sem is not None:
        kw["dimension_semantics"] = sem
    return pltpu.CompilerParams(**kw)


def _dot(a, b, dims):
    return lax.dot_general(a.astype(BF16), b.astype(BF16), _DN[dims], preferred_element_type=F32)


def _mm(name, dims, grid, a, a_blk, a_idx, b, b_blk, b_idx, out, out_blk, out_idx, acc_shape, alias=None, deps=()):
    nk = grid[2]

    def body(*refs):
        a_ref, b_ref = refs[:2]
        o_ref, acc_ref = refs[-2:]
        part = _dot(a_ref[...], b_ref[...], dims)
        if nk == 1:
            o_ref[...] = part.astype(o_ref.dtype)
            return
        k = pl.program_id(2)

        @pl.when(k == 0)
        def _():
            acc_ref[...] = part

        @pl.when((k > 0) & (k < nk - 1))
        def _():
            acc_ref[...] += part

        @pl.when(k == nk - 1)
        def _():
            o_ref[...] = (acc_ref[...] + part).astype(o_ref.dtype)

    in_specs = [pl.BlockSpec(a_blk, a_idx), pl.BlockSpec(b_blk, b_idx)]
    args = [a, b]
    kw = {}
    if alias is not None:
        in_specs.append(pl.BlockSpec(memory_space=pl.ANY))
        args.append(alias)
        kw["input_output_aliases"] = {2: 0}
    in_specs += [pl.BlockSpec(memory_space=pl.ANY)] * len(deps)
    args += list(deps)
    return pl.pallas_call(
        body,
        name=name,
        grid=grid,
        in_specs=in_specs,
        out_specs=pl.BlockSpec(out_blk, out_idx),
        out_shape=out,
        scratch_shapes=[pltpu.VMEM(acc_shape if nk > 1 else (8, 128), F32)],
        compiler_params=_cparams(("parallel", "parallel", "arbitrary")),
        **kw,
    )(*args)


def _rowwise(name, fn, tiled, full, out_tiled, out_red, tm, deps=()):
    rows = tiled[0].shape[-2]
    tm = min(tm, rows)
    grid = (rows // tm,)
    n_in = len(tiled) + len(full)
    n_t = len(out_tiled)
    n_dep = len(deps)

    def tspec(shape):
        nd = len(shape)
        return pl.BlockSpec(tuple(shape[:-2]) + (tm, shape[-1]), lambda i, nd=nd: (0,) * (nd - 2) + (i, 0))

    def fspec(shape):
        nd = len(shape)
        return pl.BlockSpec(tuple(shape), lambda i, nd=nd: (0,) * nd)

    def body(*refs):
        outs = fn(*[r[...] for r in refs[:n_in]])
        if not isinstance(outs, (tuple, list)):
            outs = (outs,)
        out_refs = refs[n_in + n_dep:]
        for r, v in zip(out_refs[:n_t], outs[:n_t]):
            r[...] = v.astype(r.dtype)
        red_refs = out_refs[n_t:]
        if red_refs:
            @pl.when(pl.program_id(0) == 0)
            def _():
                for r in red_refs:
                    r[...] = jnp.zeros_like(r)

            for r, v in zip(red_refs, outs[n_t:]):
                r[...] += v.astype(r.dtype)

    res = pl.pallas_call(
        body,
        name=name,
        grid=grid,
        in_specs=[tspec(a.shape) for a in tiled] + [fspec(a.shape) for a in full]
        + [pl.BlockSpec(memory_space=pl.ANY)] * n_dep,
        out_specs=[tspec(o.shape) for o in out_tiled] + [fspec(o.shape) for o in out_red],
        out_shape=list(out_tiled) + list(out_red),
        compiler_params=_cparams(("arbitrary",) if out_red else ("parallel",)),
    )(*tiled, *full, *deps)
    return res


def _sum0(v):
    return jnp.sum(v, axis=0, keepdims=True)


def _sigmoid(v):
    return 1.0 / (1.0 + jnp.exp(-v))


def _rms(x):
    r = lax.rsqrt(jnp.mean(x * x, axis=-1, keepdims=True) + EPS)
    return x * r, r


def _rms_bwd(u, xr, r):
    return r * (u - xr * jnp.mean(u * xr, axis=-1, keepdims=True))


class _Ready:
    def __init__(self, g):
        self.g = g

    def mid(self, after):
        return []

    def get(self, after):
        return self.g


def _normmod(x_, gamma_, sc_, sh_):
    return _rms(x_)[0] * gamma_ * (1.0 + sc_) + sh_


def _row_blocks(tm, size=256):
    size = min(size, tm)
    return [slice(r, r + size) for r in range(0, tm, size)]


def _loss_head(x_, t_, g_):
    xr, r = _rms(x_)
    err = xr * g_ - t_
    dy = err * (1.0 / D_MODEL)
    return _rms_bwd(dy * g_, xr, r), _sum0(err * err), _sum0(dy * xr)


def _ffn_fwd(tag, x, gamma, sh, sc, gate, gu_src, base, down_src, down_idx, mid_after, h_pre=None,
             next_norm=None, head=None):
    s_len = x.shape[0]
    if h_pre is None:
        (h,) = _rowwise(
            f"{tag}_normmod", _normmod, [x], [gamma, sc, sh], [_sds((s_len, D_MODEL), BF16)], [], ELEM_TILE,
            deps=gu_src.mid(mid_after),
        )
        gdeps = []
    else:
        h, gdeps = h_pre, gu_src.mid(mid_after)
    g4 = gu_src.get(h)

    tm = min(ROW_TILE, s_len)
    def gate_up(h_ref, wg_ref, wu_ref, *rest):
        ab_ref, s_ref = rest[-2:]
        wg, wu = wg_ref[...], wu_ref[...]
        for rows in _row_blocks(tm):
            hv = h_ref[rows, :]
            a = _dot(hv, wg, "nt")
            b = _dot(hv, wu, "nt")
            ab_ref[0, rows, :] = a.astype(ab_ref.dtype)
            ab_ref[1, rows, :] = b.astype(ab_ref.dtype)
            s_ref[rows, :] = (a * _sigmoid(a) * b).astype(s_ref.dtype)

    ab4, s3 = pl.pallas_call(
        gate_up,
        name=f"{tag}_gate_up",
        grid=(s_len // tm, N_CHIPS),
        in_specs=[
            pl.BlockSpec((tm, D_MODEL), lambda i, c: (i, 0)),
            pl.BlockSpec((None, None, F_SHARD, D_MODEL), lambda i, c: (c, base, 0, 0)),
            pl.BlockSpec((None, None, F_SHARD, D_MODEL), lambda i, c: (c, base + 1, 0, 0)),
        ] + [pl.BlockSpec(memory_space=pl.ANY)] * len(gdeps),
        out_specs=[
            pl.BlockSpec((None, 2, tm, F_SHARD), lambda i, c: (c, 0, i, 0)),
            pl.BlockSpec((None, tm, F_SHARD), lambda i, c: (c, i, 0)),
        ],
        out_shape=[_sds((N_CHIPS, 2, s_len, F_SHARD), BF16), _sds((N_CHIPS, s_len, F_SHARD), BF16)],
        compiler_params=_cparams(("parallel", "parallel")),
    )(h, g4, g4, *gdeps)
    if down_src is None:
        g_down, ddeps = g4, ()
    else:
        ddeps = down_src.mid(ab4)
        g_down = down_src.get(s3)

    y = _mm(
        f"{tag}_down", "nn", (s_len // tm, 1, N_CHIPS),
        s3, (None, tm, F_SHARD), lambda i, j, k: (k, i, 0),
        g_down, (None, None, F_SHARD, D_MODEL), lambda i, j, k: (k, down_idx, 0, 0),
        _sds((s_len, D_MODEL), F32), (tm, D_MODEL), lambda i, j, k: (i, 0),
        (tm, D_MODEL), deps=ddeps,
    )

    saved = (g4, base, g_down, down_idx, h, ab4, s3, y)
    act = _sds((s_len, D_MODEL), F32)
    if head is not None:
        target, norm_final = head
        vec = _sds((1, D_MODEL), F32)
        out = _rowwise(
            f"{tag}_residual_head", lambda x_, y_, t_, g_, nf_: _loss_head(x_ + 0.5 * g_ * y_, t_, nf_),
            [x, y, target], [gate, norm_final], [act], [vec, vec], ELEM_TILE,
        )
    elif next_norm is not None:
        def residual_norm(x_, y_, g_, gamma_, sc_, sh_):
            x_out = x_ + 0.5 * g_ * y_
            return x_out, _normmod(x_out, gamma_, sc_, sh_)

        out = _rowwise(
            f"{tag}_residual_norm", residual_norm, [x, y], [gate] + list(next_norm),
            [act, _sds((s_len, D_MODEL), BF16)], [], ELEM_TILE,
        )
    else:
        out = _rowwise(f"{tag}_residual", lambda x_, y_, g_: x_ + 0.5 * g_ * y_, [x, y], [gate], [act], [], ELEM_TILE)
    return out, saved


def _normmod_bwd_call(name, dh_parts, x, dxo, gamma, sc, deps=()):
    s_len = x.shape[0]
    n_parts = len(dh_parts)

    def fn(*vals):
        dh = vals[0]
        for extra in vals[1:n_parts]:
            dh = dh + extra
        x_, dxo_, gamma_, sc_ = vals[n_parts:]
        xr, r = _rms(x_)
        n = xr * gamma_
        dn = dh * (1.0 + sc_)
        dx = dxo_ + _rms_bwd(dn * gamma_, xr, r)
        return dx, _sum0(dh * n), _sum0(dh), _sum0(dn * xr)

    vec = _sds((1, D_MODEL), F32)
    return _rowwise(
        name, fn, list(dh_parts) + [x, dxo], [gamma, sc], [_sds((s_len, D_MODEL), F32)], [vec, vec, vec], ELEM_TILE, deps=deps
    )


class _BwdHooks:
    def __init__(self, after_gate=None, after_swiglu=None, after_wdown=None, after_wgu=None, after_dh=None):
        def nothing(_):
            return []

        self.after_gate = after_gate or nothing
        self.after_swiglu = after_swiglu or nothing
        self.after_wdown = after_wdown or nothing
        self.after_wgu = after_wgu or nothing
        self.after_dh = after_dh or nothing


def _ffn_bwd(tag, dxo, x, gamma, sc, gate, saved, hooks, deps=()):
    g4, base, g_down, down_idx, h, ab4, s3, y = saved
    s_len = x.shape[0]
    vec = _sds((1, D_MODEL), F32)

    dy, dgate = _rowwise(
        f"{tag}_bwd_gate", lambda dxo_, y_, g_: (0.5 * g_ * dxo_, _sum0(0.5 * y_ * dxo_)),
        [dxo, y], [gate], [_sds((s_len, D_MODEL), BF16)], [vec], ELEM_TILE, deps=deps,
    )

    tm = min(ROW_TILE, s_len)

    def d_gate_up(dy_ref, wd_ref, ab_ref, o_ref):
        wd = wd_ref[...]
        for rows in _row_blocks(tm):
            ds = _dot(dy_ref[rows, :], wd, "nt")
            a = ab_ref[0, rows, :].astype(F32)
            b = ab_ref[1, rows, :].astype(F32)
            sig = _sigmoid(a)
            o_ref[0, rows, :] = (ds * b * sig * (1.0 + a * (1.0 - sig))).astype(o_ref.dtype)
            o_ref[1, rows, :] = (ds * a * sig).astype(o_ref.dtype)

    ab_spec = pl.BlockSpec((None, 2, tm, F_SHARD), lambda i, c: (c, 0, i, 0))
    dab4 = pl.pallas_call(
        d_gate_up,
        name=f"{tag}_bwd_dgate_up",
        grid=(s_len // tm, N_CHIPS),
        in_specs=[
            pl.BlockSpec((tm, D_MODEL), lambda i, c: (i, 0)),
            pl.BlockSpec((None, None, F_SHARD, D_MODEL), lambda i, c: (c, down_idx, 0, 0)),
            ab_spec,
        ],
        out_specs=ab_spec,
        out_shape=_sds(ab4.shape, BF16),
        compiler_params=_cparams(("parallel", "parallel")),
    )(dy, g_down, ab4)

    tk = tm
    gb_down = _mm(
        f"{tag}_bwd_wdown", "tn", (N_CHIPS, 1, s_len // tk),
        s3, (None, tk, F_SHARD), lambda i, j, k: (i, k, 0),
        dy, (tk, D_MODEL), lambda i, j, k: (k, 0),
        _sds((N_CHIPS, 1, F_SHARD, D_MODEL), BF16), (None, None, F_SHARD, D_MODEL), lambda i, j, k: (i, 0, 0, 0),
        (F_SHARD, D_MODEL), deps=hooks.after_swiglu(dab4),
    )
    gb_gu = _mm(
        f"{tag}_bwd_wgu", "tn", (2 * N_CHIPS, 1, s_len // tk),
        dab4, (None, None, tk, F_SHARD), lambda i, j, k: (i % N_CHIPS, i // N_CHIPS, k, 0),
        h, (tk, D_MODEL), lambda i, j, k: (k, 0),
        _sds((N_CHIPS, 2, F_SHARD, D_MODEL), BF16), (None, None, F_SHARD, D_MODEL),
        lambda i, j, k: (i % N_CHIPS, i // N_CHIPS, 0, 0),
        (F_SHARD, D_MODEL), deps=hooks.after_wdown(gb_down),
    )
    assert base % 2 == 0
    dh_deps = hooks.after_wgu(gb_gu)

    def d_h(dab_ref, w_ref, *rest):
        o_ref, acc_ref = rest[-2:]
        c = pl.program_id(1)
        part = _dot(dab_ref[0], w_ref[0], "nn") + _dot(dab_ref[1], w_ref[1], "nn")

        @pl.when(c == 0)
        def _():
            acc_ref[...] = part

        @pl.when((c > 0) & (c < N_CHIPS - 1))
        def _():
            acc_ref[...] += part

        @pl.when(c == N_CHIPS - 1)
        def _():
            o_ref[...] = acc_ref[...] + part

    dh = pl.pallas_call(
        d_h,
        name=f"{tag}_bwd_dh",
        grid=(s_len // tm, N_CHIPS),
        in_specs=[
            pl.BlockSpec((None, 2, tm, F_SHARD), lambda i, c: (c, 0, i, 0)),
            pl.BlockSpec((None, 2, F_SHARD, D_MODEL), lambda i, c: (c, base // 2, 0, 0)),
        ] + [pl.BlockSpec(memory_space=pl.ANY)] * len(dh_deps),
        out_specs=pl.BlockSpec((tm, D_MODEL), lambda i, c: (i, 0)),
        out_shape=_sds((s_len, D_MODEL), F32),
        scratch_shapes=[pltpu.VMEM((tm, D_MODEL), F32)],
        compiler_params=_cparams(("parallel", "arbitrary")),
    )(dab4, g4, *dh_deps)
    dx, dsc, dsh, dgamma = _normmod_bwd_call(
        f"{tag}_bwd_normmod", [dh], x, dxo, gamma, sc, deps=hooks.after_dh(dh)
    )
    return dx, (gb_down, gb_gu), (dsh, dsc, dgate, dgamma)


def _bucket_map():
    qi = np.arange(WINDOW)[:, None]
    kj = np.arange(2 * WINDOW)[None, :]
    dist = qi + WINDOW - kj
    band = (dist >= 0) & (dist < WINDOW)
    max_exact = NUM_BUCKETS // 2
    n = np.maximum(dist, 0)
    large = []
    for dt in (np.float32, np.float64):
        nf = np.maximum(n, 1).astype(dt)
        val = np.log(nf / dt(max_exact)) / dt(math.log(MAX_DISTANCE / max_exact)) * dt(NUM_BUCKETS - max_exact)
        large.append(np.minimum(max_exact + val.astype(np.int32), NUM_BUCKETS - 1))
    assert np.array_equal(large[0], large[1])
    bucket = np.where(n < max_exact, n, large[0])
    return np.where(band, bucket, -1).astype(np.int32).reshape(1, -1)


def _bias_expand(rel_bias_t, bkt):
    n = bkt.shape[1]

    def body(rb_ref, bkt_ref, o_ref):
        onehot = (lax.broadcasted_iota(jnp.int32, (NUM_BUCKETS, n), 0) == bkt_ref[...]).astype(F32)
        o_ref[...] = lax.dot_general(
            rb_ref[...], onehot, _DN["nn"], precision=lax.Precision.HIGHEST, preferred_element_type=F32
        )

    return pl.pallas_call(
        body, name="bias_expand", out_shape=_sds((SWA_HEADS, n), F32), compiler_params=_cparams()
    )(rel_bias_t, bkt)


def _bias_reduce(dbias_flat, bkt):
    n = bkt.shape[1]

    def body(db_ref, bkt_ref, o_ref):
        onehot = (lax.broadcasted_iota(jnp.int32, (NUM_BUCKETS, n), 0) == bkt_ref[...]).astype(F32)
        o_ref[...] = lax.dot_general(
            db_ref[...], onehot, _DN["nt"], precision=lax.Precision.HIGHEST, preferred_element_type=F32
        )

    return pl.pallas_call(
        body, name="bias_reduce", out_shape=_sds((SWA_HEADS, NUM_BUCKETS), F32), compiler_params=_cparams()
    )(dbias_flat, bkt)


_SWA_SCALE = SWA_HEAD_DIM ** -0.5
_NEG = -1e30


_SWA_PAIR = 2


def _swa_in_specs():
    w = WINDOW
    n_q = SWA_HEADS * SWA_HEAD_DIM
    n_kv = 2 * SWA_KV_HEADS * SWA_HEAD_DIM
    prev = lambda m: jnp.maximum(_SWA_PAIR * m - 1, 0)
    return [
        pl.BlockSpec((_SWA_PAIR * w, n_q), lambda m: (m, 0)),
        pl.BlockSpec((w, n_kv), lambda m: (prev(m), n_q // n_kv)),
        pl.BlockSpec((_SWA_PAIR * w, n_kv), lambda m: (m, n_q // n_kv)),
        pl.BlockSpec((SWA_HEADS, w, 2 * w), lambda m: (0, 0, 0)),
        pl.BlockSpec((SWA_HEADS, w, 1), lambda m: (0, 0, 0)),
    ]


def _head_cols(v, first, count):
    hd = SWA_HEAD_DIM
    return jnp.stack([v[:, (first + i) * hd:(first + i + 1) * hd] for i in range(count)])


def _swa_blocks(q_ref, kvp_ref, kvc_ref, m):
    g, w, hd = SWA_GROUP, WINDOW, SWA_HEAD_DIM
    kv_all = jnp.concatenate([kvp_ref[...], kvc_ref[...]], axis=0).astype(F32)
    blocks = []
    for sub in range(_SWA_PAIR):
        rows = slice(sub * w, (sub + 1) * w)
        q = q_ref[rows, :].astype(F32)
        kv = kv_all[sub * w:(sub + 2) * w]
        heads = []
        for j in range(SWA_KV_HEADS):
            qj = _head_cols(q, g * j, g).reshape(g * w, hd)
            heads.append((qj, _head_cols(kv, j, 1)[0], _head_cols(kv, SWA_KV_HEADS + j, 1)[0]))
        blocks.append((_SWA_PAIR * m + sub, rows, heads))
    return blocks


def _swa_scores(q, kk, bias, n):
    g, w = SWA_GROUP, WINDOW
    s = (_dot(q, kk, "nt") * _SWA_SCALE).reshape(g, w, 2 * w) + bias
    qi = lax.broadcasted_iota(jnp.int32, (w, 2 * w), 0)
    kj = lax.broadcasted_iota(jnp.int32, (w, 2 * w), 1)
    dist = qi + w - kj
    valid = ((dist >= 0) & (dist < w) & ((n > 0) | (kj >= w)))[None]
    return jnp.where(valid, s, _NEG), valid


def _swa_fwd(swa2, bias, sinkb):
    s_len = swa2.shape[0]
    g, w, hd = SWA_GROUP, WINDOW, SWA_HEAD_DIM

    def body(q_ref, kvp_ref, kvc_ref, bias_ref, sink_ref, o_ref, lse_ref):
        for n, rows, heads in _swa_blocks(q_ref, kvp_ref, kvc_ref, pl.program_id(0)):
            outs = []
            for j, (q, kk, vv) in enumerate(heads):
                hs = slice(g * j, g * (j + 1))
                s, valid = _swa_scores(q, kk, bias_ref[hs], n)
                sink = sink_ref[hs]
                m = jnp.maximum(jnp.max(s, axis=-1, keepdims=True), sink)
                p = jnp.where(valid, jnp.exp(s - m), 0.0)
                l = jnp.sum(p, axis=-1, keepdims=True) + jnp.exp(sink - m)
                o = _dot(p.reshape(g * w, 2 * w), vv, "nn").reshape(g, w, hd) / l
                outs += [o[i] for i in range(g)]
                lse_ref[hs, rows, :] = m + jnp.log(l)
            o_ref[rows, :] = jnp.concatenate(outs, axis=-1).astype(o_ref.dtype)

    n_q = SWA_HEADS * hd
    return pl.pallas_call(
        body,
        name="swa_fwd",
        grid=(s_len // (_SWA_PAIR * w),),
        in_specs=_swa_in_specs(),
        out_specs=[
            pl.BlockSpec((_SWA_PAIR * w, n_q), lambda m: (m, 0)),
            pl.BlockSpec((SWA_HEADS, _SWA_PAIR * w, 1), lambda m: (0, m, 0)),
        ],
        out_shape=[_sds((s_len, n_q), BF16), _sds((SWA_HEADS, s_len, 1), F32)],
        compiler_params=_cparams(("parallel",)),
    )(swa2, swa2, swa2, bias, sinkb)


def _swa_bwd(swa2, bias, sinkb, cat, dcat, lse):
    s_len = swa2.shape[0]
    g, w, hd = SWA_GROUP, WINDOW, SWA_HEAD_DIM

    def body(q_ref, kvp_ref, kvc_ref, bias_ref, sink_ref, o_ref, do_ref, lse_ref,
             dq_ref, dkva_ref, dkvb_ref, dbias_ref, dsink_ref):
        step = pl.program_id(0)

        @pl.when(step == 0)
        def _():
            dbias_ref[...] = jnp.zeros_like(dbias_ref)
            dsink_ref[...] = jnp.zeros_like(dsink_ref)

        for n, rows, heads in _swa_blocks(q_ref, kvp_ref, kvc_ref, step):
            o_all = o_ref[rows, :].astype(F32)
            do_all = do_ref[rows, :].astype(F32)
            dqs, dks, dvs = [], [], []
            for j, (q, kk, vv) in enumerate(heads):
                hs = slice(g * j, g * (j + 1))
                s, valid = _swa_scores(q, kk, bias_ref[hs], n)
                lse_v = lse_ref[hs, rows, :]
                p = jnp.where(valid, jnp.exp(s - lse_v), 0.0)
                do = _head_cols(do_all, g * j, g)
                delta = jnp.sum(do * _head_cols(o_all, g * j, g), axis=-1, keepdims=True)
                do2 = do.reshape(g * w, hd)
                dp = _dot(do2, vv, "nt").reshape(g, w, 2 * w)
                ds = p * (dp - delta)
                dbias_ref[hs] += ds
                dsink_ref[hs] -= jnp.exp(sink_ref[hs] - lse_v) * delta
                ds2 = ds.reshape(g * w, 2 * w)
                dq = (_dot(ds2, kk, "nn") * _SWA_SCALE).reshape(g, w, hd)
                dqs += [dq[i] for i in range(g)]
                dks.append(_dot(ds2, q, "tn") * _SWA_SCALE)
                dvs.append(_dot(p.reshape(g * w, 2 * w), do2, "tn"))
            dq_ref[rows, :] = jnp.concatenate(dqs, axis=-1).astype(dq_ref.dtype)
            dkv = jnp.concatenate(dks + dvs, axis=-1)
            dkvb_ref[rows, :] = dkv[:w]
            dkva_ref[rows, :] = dkv[w:]

    n_q = SWA_HEADS * hd
    n_kv = 2 * SWA_KV_HEADS * hd
    q_spec = pl.BlockSpec((_SWA_PAIR * w, n_q), lambda m: (m, 0))
    kv_spec = pl.BlockSpec((_SWA_PAIR * w, n_kv), lambda m: (m, 0))
    return pl.pallas_call(
        body,
        name="swa_bwd",
        grid=(s_len // (_SWA_PAIR * w),),
        in_specs=_swa_in_specs() + [q_spec, q_spec, pl.BlockSpec((SWA_HEADS, _SWA_PAIR * w, 1), lambda m: (0, m, 0))],
        out_specs=[
            q_spec, kv_spec, kv_spec,
            pl.BlockSpec((SWA_HEADS, w, 2 * w), lambda n: (0, 0, 0)),
            pl.BlockSpec((SWA_HEADS, w, 1), lambda n: (0, 0, 0)),
        ],
        out_shape=[
            _sds((s_len, n_q), BF16), _sds((s_len, n_kv), F32), _sds((s_len, n_kv), F32),
            _sds((SWA_HEADS, w, 2 * w), F32), _sds((SWA_HEADS, w, 1), F32),
        ],
        compiler_params=_cparams(("arbitrary",)),
    )(swa2, swa2, swa2, bias, sinkb, cat, dcat, lse)


_MLA_SCALE = MLA_QK ** -0.5
_MLA_TQ = 256
_MLA_PAIR = 4


def _mla_mask(i, tq, n_keys):
    row = i * tq + lax.broadcasted_iota(jnp.int32, (tq, n_keys), 0)
    col = lax.broadcasted_iota(jnp.int32, (tq, n_keys), 1)
    return col <= row


def _per_query_block(i, n_blocks, fn):
    for ii in range(n_blocks):
        pl.when(i == ii)(functools.partial(fn, ii))


def _mla_fwd(q4, k4, v4):
    s_len = q4.shape[1]
    tq = min(_MLA_TQ, s_len)

    def body(q_ref, k_ref, v_ref, o_ref, lse_ref):
        def block(ii):
            n_keys = (ii + 1) * tq
            mask = _mla_mask(ii, tq, n_keys)
            for hh in range(_MLA_PAIR):
                s = jnp.where(mask, _dot(q_ref[hh], k_ref[hh, :n_keys, :], "nt") * _MLA_SCALE, _NEG)
                m = jnp.max(s, axis=-1, keepdims=True)
                p = jnp.exp(s - m)
                l = jnp.sum(p, axis=-1, keepdims=True)
                o_ref[:, hh * MLA_V:(hh + 1) * MLA_V] = (_dot(p, v_ref[hh, :n_keys, :], "nn") / l).astype(o_ref.dtype)
                lse_ref[hh] = m + jnp.log(l)

        _per_query_block(pl.program_id(1), s_len // tq, block)

    return pl.pallas_call(
        body,
        name="mla_fwd",
        grid=(MLA_HEADS // _MLA_PAIR, s_len // tq),
        in_specs=[
            pl.BlockSpec((_MLA_PAIR, tq, MLA_QK), lambda h, i: (h, i, 0)),
            pl.BlockSpec((_MLA_PAIR, s_len, MLA_QK), lambda h, i: (h, 0, 0)),
            pl.BlockSpec((_MLA_PAIR, s_len, MLA_V), lambda h, i: (h, 0, 0)),
        ],
        out_specs=[
            pl.BlockSpec((tq, _MLA_PAIR * MLA_V), lambda h, i: (i, h)),
            pl.BlockSpec((_MLA_PAIR, tq, 1), lambda h, i: (h, i, 0)),
        ],
        out_shape=[_sds((s_len, MLA_HEADS * MLA_V), BF16), _sds((MLA_HEADS, s_len, 1), F32)],
        compiler_params=_cparams(("parallel", "parallel")),
    )(q4, k4, v4)


def _mla_bwd(q4, k4, v4, cat, dcat, lse):
    s_len = q4.shape[1]
    tq = min(_MLA_TQ, s_len)
    first = SWA_HEADS * SWA_HEAD_DIM // (_MLA_PAIR * MLA_V)

    def body(q_ref, k_ref, v_ref, o_ref, do_ref, lse_ref, dq_ref, dk_ref, dv_ref):
        i = pl.program_id(1)

        @pl.when(i == 0)
        def _():
            dk_ref[...] = jnp.zeros_like(dk_ref)
            dv_ref[...] = jnp.zeros_like(dv_ref)

        def block(ii):
            n_keys = (ii + 1) * tq
            mask = _mla_mask(ii, tq, n_keys)
            for hh in range(_MLA_PAIR):
                cols = slice(hh * MLA_V, (hh + 1) * MLA_V)
                q, k, v, do = q_ref[hh], k_ref[hh, :n_keys, :], v_ref[hh, :n_keys, :], do_ref[:, cols]
                s = jnp.where(mask, _dot(q, k, "nt") * _MLA_SCALE, _NEG)
                p = jnp.where(mask, jnp.exp(s - lse_ref[hh]), 0.0)
                delta = jnp.sum(do.astype(F32) * o_ref[:, cols].astype(F32), axis=-1, keepdims=True)
                ds = (p * (_dot(do, v, "nt") - delta) * _MLA_SCALE).astype(BF16)
                dq_ref[hh] = _dot(ds, k, "nn")
                dk_ref[hh, :n_keys, :] += _dot(ds, q, "tn")
                dv_ref[hh, :n_keys, :] += _dot(p, do, "tn")

        _per_query_block(i, s_len // tq, block)

    return pl.pallas_call(
        body,
        name="mla_bwd",
        grid=(MLA_HEADS // _MLA_PAIR, s_len // tq),
        in_specs=[
            pl.BlockSpec((_MLA_PAIR, tq, MLA_QK), lambda h, i: (h, i, 0)),
            pl.BlockSpec((_MLA_PAIR, s_len, MLA_QK), lambda h, i: (h, 0, 0)),
            pl.BlockSpec((_MLA_PAIR, s_len, MLA_V), lambda h, i: (h, 0, 0)),
            pl.BlockSpec((tq, _MLA_PAIR * MLA_V), lambda h, i: (i, first + h)),
            pl.BlockSpec((tq, _MLA_PAIR * MLA_V), lambda h, i: (i, first + h)),
            pl.BlockSpec((_MLA_PAIR, tq, 1), lambda h, i: (h, i, 0)),
        ],
        out_specs=[
            pl.BlockSpec((_MLA_PAIR, tq, MLA_QK), lambda h, i: (h, i, 0)),
            pl.BlockSpec((_MLA_PAIR, s_len, MLA_QK), lambda h, i: (h, 0, 0)),
            pl.BlockSpec((_MLA_PAIR, s_len, MLA_V), lambda h, i: (h, 0, 0)),
        ],
        out_shape=[
            _sds((MLA_HEADS, s_len, MLA_QK), F32),
            _sds((MLA_HEADS, s_len, MLA_QK), F32),
            _sds((MLA_HEADS, s_len, MLA_V), F32),
        ],
        compiler_params=_cparams(("parallel", "arbitrary")),
    )(q4, k4, v4, cat, dcat, lse)


def _mla_split(pm):
    q_lat = pm[:, :MLA_Q_RANK]
    kv_lat = pm[:, MLA_Q_RANK:MLA_Q_RANK + MLA_KV_RANK]
    kr = pm[:, MLA_Q_RANK + MLA_KV_RANK:MLA_Q_RANK + MLA_KV_RANK + MLA_ROPE]
    kr_sw = pm[:, MLA_Q_RANK + MLA_KV_RANK + MLA_ROPE:]
    return q_lat, kv_lat, kr, kr_sw


def _mla_proj(pm, cos2, sin2, q_norm, kv_norm, wq_ext, wkv):
    s_len = pm.shape[0]

    def fn(pm_, cos_, sin_, qg, kvg, wq, wk):
        q_lat, kv_lat, kr, kr_sw = _mla_split(pm_)
        nq = (_rms(q_lat)[0] * qg).astype(BF16)
        nkv = (_rms(kv_lat)[0] * kvg).astype(BF16)
        k_rot = kr * cos_ + kr_sw * sin_
        qs, ks, vs = [], [], []
        for h in range(MLA_HEADS):
            qe = _dot(nq, wq[h], "nt")
            q_rot = qe[:, MLA_NOPE:MLA_QK] * cos_ + qe[:, MLA_QK:] * sin_
            qs.append(jnp.concatenate([qe[:, :MLA_NOPE], q_rot], axis=-1))
            kve = _dot(nkv, wk[h], "nt")
            ks.append(jnp.concatenate([kve[:, :MLA_NOPE], k_rot], axis=-1))
            vs.append(kve[:, MLA_NOPE:])
        return jnp.stack(qs), jnp.stack(ks), jnp.stack(vs)

    return _rowwise(
        "mla_proj", fn, [pm, cos2, sin2], [q_norm, kv_norm, wq_ext, wkv],
        [_sds((MLA_HEADS, s_len, MLA_QK), BF16), _sds((MLA_HEADS, s_len, MLA_QK), BF16),
         _sds((MLA_HEADS, s_len, MLA_V), BF16)], [], 256,
    )


def _mla_proj_bwd(pm, cos2, sin2, dq4, dk4, dv4, q_norm, kv_norm, wq_ext, wkv):
    s_len = pm.shape[0]

    def fn(pm_, cos_, sin_, dq, dk, dv, qg, kvg, wq, wk):
        q_lat, kv_lat, _, _ = _mla_split(pm_)
        xq, rq = _rms(q_lat)
        xkv, rkv = _rms(kv_lat)
        nq = (xq * qg).astype(BF16)
        nkv = (xkv * kvg).astype(BF16)
        dnq = jnp.zeros_like(q_lat)
        dnkv = jnp.zeros_like(kv_lat)
        dkr = jnp.zeros_like(cos_)
        dwq, dwk = [], []
        for h in range(MLA_HEADS):
            dy = dq[h][:, MLA_NOPE:]
            dqe = jnp.concatenate([dq[h][:, :MLA_NOPE], dy * cos_, dy * sin_], axis=-1).astype(BF16)
            dnq = dnq + _dot(dqe, wq[h], "nn")
            dwq.append(_dot(dqe, nq, "tn"))
            dkve = jnp.concatenate([dk[h][:, :MLA_NOPE], dv[h]], axis=-1).astype(BF16)
            dnkv = dnkv + _dot(dkve, wk[h], "nn")
            dwk.append(_dot(dkve, nkv, "tn"))
            dkr = dkr + dk[h][:, MLA_NOPE:]
        dq_lat = _rms_bwd(dnq * qg, xq, rq)
        dkv_lat = _rms_bwd(dnkv * kvg, xkv, rkv)
        dpm = jnp.concatenate([dq_lat, dkv_lat, dkr * cos_, dkr * sin_], axis=-1)
        return dpm, _sum0(dnq * xq), _sum0(dnkv * xkv), jnp.stack(dwq), jnp.stack(dwk)

    return _rowwise(
        "mla_proj_bwd", fn, [pm, cos2, sin2, dq4, dk4, dv4], [q_norm, kv_norm, wq_ext, wkv],
        [_sds((s_len, N_MLA_COLS), BF16)],
        [_sds((1, MLA_Q_RANK), F32), _sds((1, MLA_KV_RANK), F32),
         _sds(wq_ext.shape, F32), _sds(wkv.shape, F32)], 256,
    )


def _rope_tables(s_len):
    inv = ROPE_THETA ** (-jnp.arange(0, MLA_ROPE, 2, dtype=F32) / MLA_ROPE)
    ang = jnp.arange(s_len, dtype=F32)[:, None] * inv[None, :]
    cos, sin = jnp.cos(ang), jnp.sin(ang)
    return jnp.concatenate([cos, cos], axis=-1), jnp.concatenate([-sin, sin], axis=-1)


def _mix_weights(g_mix):
    rest = g_mix[:, 0]
    w_in_t = rest[:, O_IN:O_IN + R_IN].reshape(D_IN, D_MODEL)
    w_o = rest[:, O_O:O_O + R_O].reshape(D_MODEL, D_MODEL)
    w_uq_t = rest[:, O_UQ:O_UQ + R_UQ].reshape(MLA_HEADS, MLA_QK, MLA_Q_RANK)
    w_ukv_t = rest[:, O_UKV:O_UKV + R_UKV].reshape(MLA_HEADS, MLA_NOPE + MLA_V, MLA_KV_RANK)
    half = MLA_ROPE // 2
    w_swa = w_in_t[:N_SWA_COLS]
    w_mla = jnp.concatenate([w_in_t[N_SWA_COLS:], w_in_t[D_IN - half:], w_in_t[D_IN - MLA_ROPE:D_IN - half]], axis=0)
    wq_ext = jnp.concatenate([w_uq_t, w_uq_t[:, MLA_QK - half:], w_uq_t[:, MLA_NOPE:MLA_QK - half]], axis=1)
    return w_swa, w_mla, w_o, wq_ext, w_ukv_t


def _mix_fwd(x, h, gate, mix_src, q_norm, kv_norm, bias, sinkb, cos2, sin2, next_norm):
    s_len = x.shape[0]
    tm = min(ROW_TILE, s_len)
    deps = mix_src.mid(x)
    weights = _mix_weights(mix_src.get(h))
    w_swa, w_mla, w_o, wq_ext, wkv = weights
    swa2 = _mm(
        "mix_proj_swa", "nt", (s_len // tm, 1, 1),
        h, (tm, D_MODEL), lambda i, j, k: (i, 0),
        w_swa, (N_SWA_COLS, D_MODEL), lambda i, j, k: (0, 0),
        _sds((s_len, N_SWA_COLS), BF16), (tm, N_SWA_COLS), lambda i, j, k: (i, 0),
        (tm, N_SWA_COLS), deps=deps,
    )
    pm = _mm(
        "mix_proj_mla", "nt", (s_len // tm, 1, 1),
        h, (tm, D_MODEL), lambda i, j, k: (i, 0),
        w_mla, (N_MLA_COLS, D_MODEL), lambda i, j, k: (0, 0),
        _sds((s_len, N_MLA_COLS), F32), (tm, N_MLA_COLS), lambda i, j, k: (i, 0),
        (tm, N_MLA_COLS),
    )
    q4, k4, v4 = _mla_proj(pm, cos2, sin2, q_norm, kv_norm, wq_ext, wkv)
    oa, lse_a = _swa_fwd(swa2, bias, sinkb)
    ob, lse_b = _mla_fwd(q4, k4, v4)
    cat = jnp.concatenate([oa, ob], axis=1)
    z = _mm(
        "mix_out", "nn", (s_len // tm, 1, 1),
        cat, (tm, D_MODEL), lambda i, j, k: (i, 0),
        w_o, (D_MODEL, D_MODEL), lambda i, j, k: (0, 0),
        _sds((s_len, D_MODEL), F32), (tm, D_MODEL), lambda i, j, k: (i, 0),
        (tm, D_MODEL),
    )
    def residual_norm(x_, z_, g_, gamma_, sc_, sh_):
        x_out = x_ + g_ * z_
        return x_out, _normmod(x_out, gamma_, sc_, sh_)

    out = _rowwise(
        "mix_residual_norm", residual_norm, [x, z], [gate] + list(next_norm),
        [_sds((s_len, D_MODEL), F32), _sds((s_len, D_MODEL), BF16)], [], ELEM_TILE,
    )
    return out, (weights, h, swa2, pm, q4, k4, v4, cat, lse_a, lse_b, z)


def _mix_bwd(dxo, x, gamma, sc, gate, q_norm, kv_norm, bias, sinkb, cos2, sin2, saved, hooks):
    weights, h, swa2, pm, q4, k4, v4, cat, lse_a, lse_b, z = saved
    w_swa, w_mla, w_o, wq_ext, wkv = weights
    s_len = x.shape[0]
    tm = tk = min(ROW_TILE, s_len)
    vec = _sds((1, D_MODEL), F32)
    n_proj = N_SWA_COLS + N_MLA_COLS
    half_d = D_MODEL // 2

    dz, dgate = _rowwise(
        "mix_bwd_gate", lambda dxo_, z_, g_: (g_ * dxo_, _sum0(z_ * dxo_)),
        [dxo, z], [gate], [_sds((s_len, D_MODEL), BF16)], [vec], ELEM_TILE,
    )
    dw_o = _mm(
        "mix_bwd_wo", "tn", (2, 1, s_len // tk),
        cat, (tk, half_d), lambda i, j, k: (k, i),
        dz, (tk, D_MODEL), lambda i, j, k: (k, 0),
        _sds((D_MODEL, D_MODEL), F32), (half_d, D_MODEL), lambda i, j, k: (i, 0),
        (half_d, D_MODEL),
    )
    dcat = _mm(
        "mix_bwd_dcat", "nt", (s_len // tm, 1, 1),
        dz, (tm, D_MODEL), lambda i, j, k: (i, 0),
        w_o, (D_MODEL, D_MODEL), lambda i, j, k: (0, 0),
        _sds((s_len, D_MODEL), BF16), (tm, D_MODEL), lambda i, j, k: (i, 0),
        (tm, D_MODEL), deps=hooks.after_gate(dz),
    )
    dq_a, dkva, dkvb, dbias, dsink = _swa_bwd(swa2, bias, sinkb, cat, dcat, lse_a)
    dq4, dk4, dv4 = _mla_bwd(q4, k4, v4, cat, dcat, lse_b)
    dpm, dq_norm, dkv_norm, dwq_ext, dwkv = _mla_proj_bwd(pm, cos2, sin2, dq4, dk4, dv4, q_norm, kv_norm, wq_ext, wkv)

    dkv = dkva + jnp.concatenate([dkvb[WINDOW:], jnp.zeros_like(dkvb[:WINDOW])], axis=0)
    dproj = jnp.concatenate([dq_a, dkv.astype(BF16), dpm], axis=1)
    w_proj = jnp.concatenate([w_swa, w_mla], axis=0)

    dw_proj = _mm(
        "mix_bwd_win", "tn", (n_proj // 256, 1, s_len // tk),
        dproj, (tk, 256), lambda i, j, k: (k, i),
        h, (tk, D_MODEL), lambda i, j, k: (k, 0),
        _sds((n_proj, D_MODEL), F32), (256, D_MODEL), lambda i, j, k: (i, 0),
        (256, D_MODEL),
    )
    dw_swa, dw_mla = dw_proj[:N_SWA_COLS], dw_proj[N_SWA_COLS:]
    dh = _mm(
        "mix_bwd_dh", "nn", (s_len // tm, 1, 1),
        dproj, (tm, n_proj), lambda i, j, k: (i, 0),
        w_proj, (n_proj, D_MODEL), lambda i, j, k: (0, 0),
        _sds((s_len, D_MODEL), F32), (tm, D_MODEL), lambda i, j, k: (i, 0),
        (tm, D_MODEL),
    )
    dx, dsc, dsh, dgamma = _normmod_bwd_call("mix_bwd_normmod", [dh], x, dxo, gamma, sc)

    half = MLA_ROPE // 2
    n_mla_in = D_IN - N_SWA_COLS
    dw_mla_base = dw_mla[:n_mla_in]
    dw_mla_base = dw_mla_base.at[n_mla_in - half:].add(dw_mla[n_mla_in:n_mla_in + half])
    dw_mla_base = dw_mla_base.at[n_mla_in - MLA_ROPE:n_mla_in - half].add(dw_mla[n_mla_in + half:])
    dw_in_t = jnp.concatenate([dw_swa, dw_mla_base], axis=0)
    dw_uq_t = dwq_ext[:, :MLA_QK]
    dw_uq_t = dw_uq_t.at[:, MLA_QK - half:].add(dwq_ext[:, MLA_QK:MLA_QK + half])
    dw_uq_t = dw_uq_t.at[:, MLA_NOPE:MLA_QK - half].add(dwq_ext[:, MLA_QK + half:])
    rest = jnp.concatenate(
        [
            dw_in_t.reshape(N_CHIPS, R_IN, D_MODEL),
            dw_o.reshape(N_CHIPS, R_O, D_MODEL),
            dw_uq_t.reshape(N_CHIPS, R_UQ, D_MODEL),
            dwkv.reshape(N_CHIPS, R_UKV, D_MODEL),
            jnp.zeros((N_CHIPS, F_SHARD - O_PAD, D_MODEL), F32),
        ],
        axis=1,
    ).astype(BF16)
    return dx, rest, (dsh, dsc, dgate, dgamma), dq_norm, dkv_norm, dbias, dsink


def _device_step(x, target, mod, gu1_src, down1_src, mix_src, ffn2_src, norms, q_norm, kv_norm, sinks, rel_bias,
                 hooks2=None, hooks_mix=None, mix_done=None, hooks1=None):
    s_len = x.shape[0]
    sh1, sc1, g1, sh2, sc2, g2, sh3, sc3, g3 = [mod[:, i * D_MODEL:(i + 1) * D_MODEL] for i in range(N_MOD)]
    n1, n2, n3, nf = norms
    bkt = jnp.asarray(_bucket_map())
    bias = _bias_expand(rel_bias.T, bkt).reshape(SWA_HEADS, WINDOW, 2 * WINDOW)
    sinkb = jnp.broadcast_to(sinks.reshape(SWA_HEADS, 1, 1), (SWA_HEADS, WINDOW, 1))
    cos2, sin2 = _rope_tables(s_len)

    (x1, h2), saved1 = _ffn_fwd(
        "ffn1", x, n1, sh1, sc1, g1, gu1_src, 0, down1_src, 0, sh1, next_norm=(n2, sc2, sh2)
    )
    (x2, h3), saved2 = _mix_fwd(
        x1, h2, g2, mix_src, q_norm, kv_norm, bias, sinkb, cos2, sin2, next_norm=(n3, sc3, sh3)
    )
    (dx3, sq, dnf), saved3 = _ffn_fwd(
        "ffn2", x2, n3, sh3, sc3, g3, ffn2_src, 0, None, 2, x2, h_pre=h3, head=(target, nf)
    )
    loss_part = (0.5 / D_MODEL) * jnp.sum(sq)

    dx2, gb2, (dsh3, dsc3, dg3, dn3) = _ffn_bwd("ffn2", dx3, x2, n3, sc3, g3, saved3, hooks2 or _BwdHooks())
    dx1, rest, (dsh2, dsc2, dg2, dn2), dq_norm, dkv_norm, dbias, dsink = _mix_bwd(
        dx2, x1, n2, sc2, g2, q_norm, kv_norm, bias, sinkb, cos2, sin2, saved2, hooks_mix or _BwdHooks()
    )
    rest = rest[:, None]
    deps1 = mix_done(dx1, rest) if mix_done is not None else []
    dx0, gb1, (dsh1, dsc1, dg1, dn1) = _ffn_bwd(
        "ffn1", dx1, x, n1, sc1, g1, saved1, hooks1 or _BwdHooks(), deps=deps1
    )

    dmod = jnp.concatenate([dsh1, dsc1, dg1, dsh2, dsc2, dg2, dsh3, dsc3, dg3], axis=-1)
    drel = _bias_reduce(dbias.reshape(SWA_HEADS, -1), bkt).T
    dsinks = jnp.sum(dsink, axis=(1, 2)).reshape(1, SWA_HEADS)
    small = (dn1, dn2, dn3, dnf, dq_norm, dkv_norm, dsinks, drel)
    return loss_part, dx0, (gb1, gb2, rest), dmod, small


_MESH = pl.DeviceIdType.MESH


def _place():
    return lax.axis_index("x"), lax.axis_index("y"), lax.axis_index("c")


def _other_chips(x, y):
    return [(1 - x, y), (x, 1 - y), (1 - x, 1 - y)]


def _allgather_small(name, blk):
    m_per, n = blk.shape

    def body(x_ref, out_ref, send_sems, recv_sems, local_sem):
        x, y, c = _place()
        me, sibling = (x, y, c), (x, y, 1 - c)
        chips = _other_chips(x, y)

        def rows(px, py, pc):
            return out_ref.at[pl.ds((4 * px + 2 * py + pc) * m_per, m_per), :]

        def copy(k, block, to, src=None):
            return pltpu.make_async_remote_copy(
                src_ref=rows(*block) if src is None else src, dst_ref=rows(*block),
                send_sem=send_sems.at[k], recv_sem=recv_sems.at[k], device_id=to, device_id_type=_MESH,
            )

        mine = pltpu.make_async_copy(x_ref, rows(*me), local_sem)
        mine.start()
        first = [copy(0, me, sibling, src=x_ref)]
        first += [copy(1 + j, me, (*chip, c), src=x_ref) for j, chip in enumerate(chips)]
        for cp in first:
            cp.start()
        passed = [copy(4 + j, (*chip, c), sibling) for j, chip in enumerate(chips)]
        for j, chip in enumerate(chips):
            copy(1 + j, (*chip, c), me).wait_recv()
            passed[j].start()
        copy(0, sibling, me).wait_recv()
        for j, chip in enumerate(chips):
            copy(4 + j, (*chip, 1 - c), me).wait_recv()
        for cp in first + passed:
            cp.wait_send()
        mine.wait()

    return pl.pallas_call(
        body,
        name=name,
        out_shape=_sds((N_DEV * m_per, n), blk.dtype),
        in_specs=[pl.BlockSpec(memory_space=pltpu.VMEM)],
        out_specs=pl.BlockSpec(memory_space=pltpu.VMEM),
        scratch_shapes=[pltpu.SemaphoreType.DMA((7,)), pltpu.SemaphoreType.DMA((7,)), pltpu.SemaphoreType.DMA],
    )(blk)


def _gather_sends(n, own_ref, g_ref, send_sem, recv_sem, x, y, c, chip):
    return [
        pltpu.make_async_remote_copy(
            src_ref=own_ref.at[p, pl.ds(c * HALF, HALF), :], dst_ref=g_ref.at[2 * x + y, p, pl.ds(c * HALF, HALF), :],
            send_sem=send_sem, recv_sem=recv_sem, device_id=(*chip, c), device_id_type=_MESH,
        )
        for p in range(n)
    ]


def _gather_forwards(n, g_ref, send_sem, recv_sem, chip, c, sibling):
    return [
        pltpu.make_async_remote_copy(
            src_ref=g_ref.at[2 * chip[0] + chip[1], p, pl.ds(c * HALF, HALF), :],
            dst_ref=g_ref.at[2 * chip[0] + chip[1], p, pl.ds(c * HALF, HALF), :],
            send_sem=send_sem, recv_sem=recv_sem, device_id=sibling, device_id_type=_MESH,
        )
        for p in range(n)
    ]


def _gather_all(own_ref, g_ref, send_sem, recv_sem, chip, half, c):
    return pltpu.make_async_remote_copy(
        src_ref=own_ref.at[:, pl.ds(c * HALF, HALF), :],
        dst_ref=g_ref.at[2 * chip[0] + chip[1], :, pl.ds(half * HALF, HALF), :],
        send_sem=send_sem, recv_sem=recv_sem, device_id=(*chip, c), device_id_type=_MESH,
    )


_HBM_SPEC = pl.BlockSpec(memory_space=pltpu.HBM)
_SEM_SPEC = pl.BlockSpec(memory_space=pltpu.SEMAPHORE)
_ANY_SPEC = pl.BlockSpec(memory_space=pl.ANY)
_VMEM_SPEC = pl.BlockSpec(memory_space=pltpu.VMEM)
_SPLIT_PARAMS = pltpu.CompilerParams(has_side_effects=pltpu.SideEffectType.DATAFLOW_SIDE_EFFECTING)
_TOKEN = jax.ShapeDtypeStruct((8, 128), F32)


def _in_hbm(a):
    return pltpu.with_memory_space_constraint(a, pltpu.HBM)


class _GatherBehind:
    def __init__(self, tag, own, then=None):
        self.tag, self.n, self.own, self.then = tag, own.shape[0], own, then

    def start(self, after):
        tag, own, n = self.tag, self.own, self.n
        x, y, _ = _place()
        g_init = lax.dynamic_update_slice(
            lax.empty((N_CHIPS,) + own.shape, own.dtype), own[None], (2 * x + y, 0, 0, 0)
        )

        def body(own_ref, g_ref, *rest):
            send_sems, recv_sems, _, _, token = rest[len(after):]
            x, y, c = _place()
            for j, chip in enumerate(_other_chips(x, y)):
                for cp in _gather_sends(n, own_ref, g_ref, send_sems.at[j], recv_sems.at[j], x, y, c, chip):
                    cp.start()
            token[...] = jnp.zeros_like(token)

        self.send1, self.recv1, self.own, self.g, self.token = pl.pallas_call(
            body,
            name=f"{tag}_start",
            out_shape=(
                pltpu.SemaphoreType.DMA((3,)), pltpu.SemaphoreType.DMA((3,)),
                pltpu.HBM(own.shape, own.dtype), pltpu.HBM(g_init.shape, g_init.dtype), _TOKEN,
            ),
            in_specs=(_HBM_SPEC, _HBM_SPEC) + (_ANY_SPEC,) * len(after),
            out_specs=(_SEM_SPEC, _SEM_SPEC, _HBM_SPEC, _HBM_SPEC, _VMEM_SPEC),
            input_output_aliases={0: 2, 1: 3},
            compiler_params=_SPLIT_PARAMS,
        )(_in_hbm(own), _in_hbm(g_init), *after)

    def mid(self, after):
        n = self.n

        def body(own_ref, g_ref, send1, recv1, after_ref, send2, recv2, g_out, token):
            x, y, c = _place()
            sibling = (x, y, 1 - c)
            chips = _other_chips(x, y)
            for j, chip in enumerate(chips):
                _gather_all(own_ref, g_ref, send1.at[j], recv1.at[j], chip, c, c).wait_recv()
                for cp in _gather_forwards(n, g_ref, send2.at[j], recv2.at[j], chip, c, sibling):
                    cp.start()
            for j, chip in enumerate(chips):
                _gather_all(own_ref, g_ref, send1.at[j], recv1.at[j], chip, c, c).wait_send()
            token[...] = jnp.zeros_like(token)

        self.send2, self.recv2, self.g, token = pl.pallas_call(
            body,
            name=f"{self.tag}_mid",
            out_shape=(
                pltpu.SemaphoreType.DMA((3,)), pltpu.SemaphoreType.DMA((3,)),
                pltpu.HBM(self.g.shape, self.g.dtype), _TOKEN,
            ),
            in_specs=(_HBM_SPEC, _HBM_SPEC, _SEM_SPEC, _SEM_SPEC, _ANY_SPEC),
            out_specs=(_SEM_SPEC, _SEM_SPEC, _HBM_SPEC, _VMEM_SPEC),
            input_output_aliases={1: 2},
            compiler_params=_SPLIT_PARAMS,
        )(self.own, self.g, self.send1, self.recv1, after)
        if self.then is None:
            return [token]
        self.then.start([token])
        return [token, self.then.token]

    def get(self, after):
        def body(g_ref, send2, recv2, after_ref, g_out):
            x, y, c = _place()
            for j, chip in enumerate(_other_chips(x, y)):
                slot_in = g_ref.at[2 * chip[0] + chip[1], :, pl.ds((1 - c) * HALF, HALF), :]
                slot_out = g_ref.at[2 * chip[0] + chip[1], :, pl.ds(c * HALF, HALF), :]
                cp = pltpu.make_async_remote_copy(
                    src_ref=slot_out, dst_ref=slot_in, send_sem=send2.at[j], recv_sem=recv2.at[j],
                    device_id=(x, y, 1 - c), device_id_type=_MESH,
                )
                cp.wait_send()
                cp.wait_recv()

        return pl.pallas_call(
            body,
            name=f"{self.tag}_wait",
            out_shape=pltpu.HBM(self.g.shape, self.g.dtype),
            in_specs=(_HBM_SPEC, _SEM_SPEC, _SEM_SPEC, _ANY_SPEC),
            out_specs=_HBM_SPEC,
            input_output_aliases={0: 0},
            compiler_params=_SPLIT_PARAMS,
        )(self.g, self.send2, self.recv2, after)


def _pair_sum(name, gb, land):
    def body(c_ref, gb_ref, land_ref, o_ref):
        o_ref[...] = (gb_ref[...].astype(F32) + land_ref[...].astype(F32)).astype(o_ref.dtype)

    core = lax.axis_index("c").astype(jnp.int32).reshape(1)
    return pl.pallas_call(
        body,
        name=name,
        grid_spec=pltpu.PrefetchScalarGridSpec(
            num_scalar_prefetch=1,
            grid=(N_CHIPS, gb.shape[1]),
            in_specs=[
                pl.BlockSpec((None, None, HALF, D_MODEL), lambda j, p, c: (j, p, c[0], 0)),
                pl.BlockSpec((None, None, HALF, D_MODEL), lambda j, p, c: (j, p, 0, 0)),
            ],
            out_specs=pl.BlockSpec((None, None, HALF, D_MODEL), lambda j, p, c: (j, p, 0, 0)),
        ),
        out_shape=_sds(land.shape, BF16),
        compiler_params=_cparams(("parallel", "parallel")),
    )(core, gb, land)


def _chip_sum_share(name, land):
    n = land.shape[1]

    def body(l_ref, r_ref, buf, send_sems, recv_sems, local_sems):
        p = pl.program_id(0)
        x, y, c = _place()
        acc = l_ref[0].astype(F32)
        for j in range(1, N_CHIPS):
            acc = acc + l_ref[j].astype(F32)
        buf[p] = acc

        def copies(q):
            mine = r_ref.at[q, pl.ds(c * HALF, HALF), :]
            theirs = r_ref.at[q, pl.ds((1 - c) * HALF, HALF), :]
            local = pltpu.make_async_copy(buf.at[q], mine, local_sems.at[q])
            out = pltpu.make_async_remote_copy(
                src_ref=buf.at[q], dst_ref=mine, send_sem=send_sems.at[q], recv_sem=recv_sems.at[q],
                device_id=(x, y, 1 - c), device_id_type=_MESH,
            )
            arrive = pltpu.make_async_remote_copy(
                src_ref=buf.at[q], dst_ref=theirs, send_sem=send_sems.at[q], recv_sem=recv_sems.at[q],
                device_id=(x, y, 1 - c), device_id_type=_MESH,
            )
            return local, out, arrive

        local, out, _ = copies(p)
        local.start()
        out.start()

        @pl.when(p == n - 1)
        def _():
            for q in range(n):
                local, out, arrive = copies(q)
                arrive.wait_recv()
                out.wait_send()
                local.wait()

    return pl.pallas_call(
        body,
        name=name,
        grid=(n,),
        in_specs=[pl.BlockSpec((N_CHIPS, None, HALF, D_MODEL), lambda p: (0, p, 0, 0))],
        out_specs=pl.BlockSpec(memory_space=pl.ANY),
        out_shape=_sds((n, F_SHARD, D_MODEL), F32),
        scratch_shapes=[
            pltpu.VMEM((n, HALF, D_MODEL), F32),
            pltpu.SemaphoreType.DMA((n,)), pltpu.SemaphoreType.DMA((n,)), pltpu.SemaphoreType.DMA((n,)),
        ],
        compiler_params=_cparams(("arbitrary",)),
    )(land)


class _ReduceBehind:
    def __init__(self, tag, gb, after=()):
        self.tag = tag
        n = gb.shape[1]
        land = lax.empty((N_CHIPS, n, HALF, D_MODEL), gb.dtype)

        def body(gb_ref, land_ref, *rest):
            send_sem, recv_sem, _, _, token = rest[len(after):]
            self._pair_copy(gb_ref, land_ref, send_sem, recv_sem).start()
            token[...] = jnp.zeros_like(token)

        self.send, self.recv, self.gb, self.land, self.token = pl.pallas_call(
            body,
            name=f"grads_pair_start_{tag}",
            out_shape=(
                pltpu.SemaphoreType.DMA((1,)), pltpu.SemaphoreType.DMA((1,)),
                pltpu.HBM(gb.shape, gb.dtype), pltpu.HBM(land.shape, land.dtype), _TOKEN,
            ),
            in_specs=(_HBM_SPEC, _HBM_SPEC) + (_ANY_SPEC,) * len(after),
            out_specs=(_SEM_SPEC, _SEM_SPEC, _HBM_SPEC, _HBM_SPEC, _VMEM_SPEC),
            input_output_aliases={0: 2, 1: 3},
            compiler_params=_SPLIT_PARAMS,
        )(_in_hbm(gb), _in_hbm(land), *after)

    @staticmethod
    def _pair_copy(gb_ref, land_ref, send_sem, recv_sem):
        x, y, c = _place()
        return pltpu.make_async_remote_copy(
            src_ref=gb_ref.at[:, :, pl.ds((1 - c) * HALF, HALF), :], dst_ref=land_ref,
            send_sem=send_sem.at[0], recv_sem=recv_sem.at[0], device_id=(x, y, 1 - c), device_id_type=_MESH,
        )

    @staticmethod
    def _chip_copies(p_ref, land_ref, send_sems, recv_sems):
        x, y, c = _place()
        me = 2 * x + y
        sends, arrivals = [], []
        for k, chip in enumerate(_other_chips(x, y)):
            them = 2 * chip[0] + chip[1]
            sends.append(pltpu.make_async_remote_copy(
                src_ref=p_ref.at[them], dst_ref=land_ref.at[me], send_sem=send_sems.at[k], recv_sem=recv_sems.at[k],
                device_id=(*chip, c), device_id_type=_MESH,
            ))
            arrivals.append(pltpu.make_async_remote_copy(
                src_ref=p_ref.at[me], dst_ref=land_ref.at[them], send_sem=send_sems.at[k], recv_sem=recv_sems.at[k],
                device_id=(*chip, c), device_id_type=_MESH,
            ))
        return sends, arrivals

    def mid(self, after):
        tag = self.tag

        def wait_body(gb_ref, land_ref, send_sem, recv_sem, after_ref, gb_out, land_out):
            cp = self._pair_copy(gb_ref, land_ref, send_sem, recv_sem)
            cp.wait_send()
            cp.wait_recv()

        gb, land = pl.pallas_call(
            wait_body,
            name=f"grads_pair_wait_{tag}",
            out_shape=(pltpu.HBM(self.gb.shape, self.gb.dtype), pltpu.HBM(self.land.shape, self.land.dtype)),
            in_specs=(_HBM_SPEC, _HBM_SPEC, _SEM_SPEC, _SEM_SPEC, _ANY_SPEC),
            out_specs=(_HBM_SPEC, _HBM_SPEC),
            input_output_aliases={0: 0, 1: 1},
            compiler_params=_SPLIT_PARAMS,
        )(self.gb, self.land, self.send, self.recv, after)
        part = _pair_sum(f"grads_pair_sum_{tag}", gb, land)

        x, y, _ = _place()
        start = (2 * x + y, 0, 0, 0)
        own = lax.dynamic_slice(part, start, (1,) + part.shape[1:])
        land2 = lax.dynamic_update_slice(lax.empty(part.shape, part.dtype), own, start)

        def start_body(p_ref, land_ref, send_sems, recv_sems, p_out, land_out, token):
            for cp in self._chip_copies(p_ref, land_ref, send_sems, recv_sems)[0]:
                cp.start()
            token[...] = jnp.zeros_like(token)

        self.send2, self.recv2, self.part, self.land2, token = pl.pallas_call(
            start_body,
            name=f"grads_chip_start_{tag}",
            out_shape=(
                pltpu.SemaphoreType.DMA((3,)), pltpu.SemaphoreType.DMA((3,)),
                pltpu.HBM(part.shape, part.dtype), pltpu.HBM(land2.shape, land2.dtype), _TOKEN,
            ),
            in_specs=(_HBM_SPEC, _HBM_SPEC),
            out_specs=(_SEM_SPEC, _SEM_SPEC, _HBM_SPEC, _HBM_SPEC, _VMEM_SPEC),
            input_output_aliases={0: 2, 1: 3},
            compiler_params=_SPLIT_PARAMS,
        )(_in_hbm(part), _in_hbm(land2))
        return [token]

    def get(self, after):
        n_after = len(after)

        def wait_body(p_ref, land_ref, send_sems, recv_sems, *rest):
            sends, arrivals = self._chip_copies(p_ref, land_ref, send_sems, recv_sems)
            for cp in arrivals:
                cp.wait_recv()
            for cp in sends:
                cp.wait_send()

        _, land2 = pl.pallas_call(
            wait_body,
            name=f"grads_chip_wait_{self.tag}",
            out_shape=(pltpu.HBM(self.part.shape, self.part.dtype), pltpu.HBM(self.land2.shape, self.land2.dtype)),
            in_specs=(_HBM_SPEC, _HBM_SPEC, _SEM_SPEC, _SEM_SPEC) + (_ANY_SPEC,) * n_after,
            out_specs=(_HBM_SPEC, _HBM_SPEC),
            input_output_aliases={0: 0, 1: 1},
            compiler_params=_SPLIT_PARAMS,
        )(self.part, self.land2, self.send2, self.recv2, *after)
        return _chip_sum_share(f"grads_chip_sum_share_{self.tag}", land2)


def _allreduce_small(blk):
    gathered = _allgather_small("allgather_small_grads", blk).reshape(N_DEV, PK_ROWS, 128)

    def body(g_ref, o_ref):
        acc = g_ref[0]
        for k in range(1, N_DEV):
            acc = acc + g_ref[k]
        o_ref[...] = acc

    total = pl.pallas_call(body, name="small_grads_sum", out_shape=_sds((PK_ROWS, 128), F32))(gathered)
    return gathered, total


def _adamw_math(w_, g_, m_, v_):
    m_new = ADAM_B1 * m_ + (1.0 - ADAM_B1) * g_
    v_new = ADAM_B2 * v_ + (1.0 - ADAM_B2) * (g_ * g_)
    m_hat = m_new / (1.0 - ADAM_B1 ** ADAM_STEP)
    v_hat = v_new / (1.0 - ADAM_B2 ** ADAM_STEP)
    delta = -ADAM_LR * (m_hat / (jnp.sqrt(v_hat) + ADAM_EPS) + ADAM_WD * w_)
    return delta, m_new, v_new


def _adamw(name, w, g, m, v):
    rows, cols = w.shape
    tm = rows
    while tm * cols * 4 > (1 << 20) and tm % 16 == 0:
        tm //= 2
    out = _sds(w.shape, F32)
    return _rowwise(name, _adamw_math, [w, g, m, v], [], [out, out, out], [], tm)


def _adamw_small(ws, gs, ms, vs):
    n = len(ws)
    shapes = [w.shape for w in ws]
    as2d = lambda a: a.reshape(1, -1) if a.ndim == 1 else a

    def body(*refs):
        ins, outs = refs[:4 * n], refs[4 * n:]
        for k in range(n):
            res = _adamw_math(*[ins[j * n + k][...] for j in range(4)])
            for j in range(3):
                outs[j * n + k][...] = res[j]

    args = [as2d(a) for group in (ws, gs, ms, vs) for a in group]
    out = pl.pallas_call(
        body, name="adamw_small", out_shape=[_sds(as2d(w).shape, F32) for w in ws] * 3, compiler_params=_cparams()
    )(*args)
    back = [o.reshape(shapes[i % n]) for i, o in enumerate(out)]
    return back[:n], back[n:2 * n], back[2 * n:]


def _pack_small(b_mod_like, n1, n2, n3, nf, qn, kvn, sinks, rel):
    parts = [
        b_mod_like.reshape(PK_MOD, 128),
        n1.reshape(8, 128), n2.reshape(8, 128), n3.reshape(8, 128), nf.reshape(8, 128),
        qn.reshape(2, 128), kvn.reshape(1, 128),
        jnp.pad(sinks.reshape(1, SWA_HEADS), ((0, 0), (0, 128 - SWA_HEADS))),
        rel.reshape(2, 128),
        jnp.zeros((2, 128), F32),
    ]
    return jnp.concatenate(parts, axis=0)


def _unpack_small(p):
    o = PK_MOD
    return (
        p[:o].reshape(1, N_MOD * D_MODEL),
        p[o:o + 8].reshape(1, D_MODEL), p[o + 8:o + 16].reshape(1, D_MODEL),
        p[o + 16:o + 24].reshape(1, D_MODEL), p[o + 24:o + 32].reshape(D_MODEL),
        p[o + 32:o + 34].reshape(1, MLA_Q_RANK), p[o + 34:o + 35].reshape(1, MLA_KV_RANK),
        p[o + 35:o + 36, :SWA_HEADS].reshape(1, SWA_HEADS),
        p[o + 36:o + 38].reshape(NUM_BUCKETS, SWA_HEADS),
    )


def kernel(x, c, w_mod, b_mod, norm_ffn1, ffn1_gate, ffn1_up, ffn1_down, norm_mix, w_in, q_norm, kv_norm, w_uq, w_ukv, sinks, w_o, norm_ffn2, ffn2_gate, ffn2_up, ffn2_down, rel_bias, norm_final, loss_target, m_w_mod, m_b_mod, m_norm_ffn1, m_ffn1_gate, m_ffn1_up, m_ffn1_down, m_norm_mix, m_w_in, m_q_norm, m_kv_norm, m_w_uq, m_w_ukv, m_sinks, m_w_o, m_norm_ffn2, m_ffn2_gate, m_ffn2_up, m_ffn2_down, m_rel_bias, m_norm_final, v_w_mod, v_b_mod, v_norm_ffn1, v_ffn1_gate, v_ffn1_up, v_ffn1_down, v_norm_mix, v_w_in, v_q_norm, v_kv_norm, v_w_uq, v_w_ukv, v_sinks, v_w_o, v_norm_ffn2, v_ffn2_gate, v_ffn2_up, v_ffn2_down, v_rel_bias, v_norm_final):
    ax, ay, ac = _place()
    chip = 2 * ax + ay
    dev = 2 * chip + ac
    n_mod_shard = N_MOD * D_MODEL // N_CHIPS

    def t16(w):
        return w[0].T.astype(BF16)

    rest = jnp.concatenate(
        [
            t16(w_in),
            w_o[0].astype(BF16),
            t16(w_uq).reshape(R_UQ, D_MODEL),
            t16(w_ukv).reshape(R_UKV, D_MODEL),
            jnp.zeros((F_SHARD - O_PAD, D_MODEL), BF16),
        ],
        axis=0,
    )
    own_gu1 = jnp.stack([t16(ffn1_gate), t16(ffn1_up)])
    own_down1 = ffn1_down[0].astype(BF16)[None]
    own_mix = rest[None]
    own_ffn2 = jnp.stack([t16(ffn2_gate), t16(ffn2_up), ffn2_down[0].astype(BF16)])

    (c_act,) = _rowwise(
        "silu_c", lambda v: v * _sigmoid(v), [c.reshape(8, 128)], [], [_sds((8, 128), F32)], [], 8
    )
    c_all = _allgather_small("allgather_c", c_act).reshape(N_DEV, D_MODEL)
    mod_cols = _mm(
        "mod_fwd", "nn", (1, n_mod_shard // 768, 1),
        c_all, (N_DEV, D_MODEL), lambda i, j, k: (0, 0),
        w_mod[0], (D_MODEL, 768), lambda i, j, k: (0, j),
        _sds((N_DEV, n_mod_shard), F32), (N_DEV, 768), lambda i, j, k: (0, j),
        (N_DEV, 768),
    )
    mod_all = _allgather_small("allgather_mod", mod_cols).reshape(N_CHIPS, 2, N_DEV, n_mod_shard)[:, 0]
    mod_all = mod_all.transpose(1, 0, 2).reshape(N_DEV, N_MOD * D_MODEL)
    mod = lax.dynamic_slice_in_dim(mod_all, dev, 1, axis=0) + b_mod

    norms = (norm_ffn1, norm_mix, norm_ffn2, norm_final.reshape(1, D_MODEL))

    ffn2_src = _GatherBehind("gather_ffn2", own_ffn2)
    mix_src = _GatherBehind("gather_mix", own_mix, then=ffn2_src)
    down1_src = _GatherBehind("gather_down1", own_down1, then=mix_src)
    gu1_src = _GatherBehind("gather_gu1", own_gu1, then=down1_src)
    gu1_src.start([mod_all])

    reducing, red = {}, {}

    def begin(tag, gb, after):
        reducing[tag] = _ReduceBehind(tag, gb, after=after)
        return [reducing[tag].token]

    def ffn2_gu_done(gb):
        return begin("ffn2_gu", gb, reducing["ffn2_down"].mid(gb))

    def mix_done(dx1, gb_rest):
        red["ffn2_down"] = reducing["ffn2_down"].get([dx1])
        red["ffn2_gu"] = reducing["ffn2_gu"].get([red["ffn2_down"]])
        return begin("rest", gb_rest, [red["ffn2_gu"]])

    def ffn1_gu_done(gb):
        red["rest"] = reducing["rest"].get([gb])
        begin("ffn1_gu", gb, reducing["ffn1_down"].mid(red["rest"]))
        return reducing["ffn1_gu"].mid(reducing["ffn1_gu"].token)

    loss_part, grad_x, _, dmod, small = _device_step(
        x[0], loss_target[0], mod, gu1_src, down1_src, mix_src, ffn2_src, norms, q_norm, kv_norm, sinks, rel_bias,
        hooks2=_BwdHooks(after_wdown=lambda gb: begin("ffn2_down", gb, ()), after_wgu=ffn2_gu_done),
        hooks_mix=_BwdHooks(after_gate=lambda dz: reducing["ffn2_gu"].mid(dz)),
        mix_done=mix_done,
        hooks1=_BwdHooks(
            after_swiglu=lambda dab3: reducing["rest"].mid(dab3),
            after_wdown=lambda gb: begin("ffn1_down", gb, ()),
            after_wgu=ffn1_gu_done,
        ),
    )
    loss = lax.psum(loss_part, ("x", "y", "c"))

    gathered, total = _allreduce_small(_pack_small(dmod, *small))
    (g_b_mod, g_n1, g_n2, g_n3, g_nf, g_qn, g_kvn, g_sinks, g_rel) = _unpack_small(total)
    dmod_all = gathered[:, :PK_MOD].reshape(N_DEV, N_MOD * D_MODEL)
    dmod_cols = lax.dynamic_slice_in_dim(dmod_all, chip * n_mod_shard, n_mod_shard, axis=1)
    g_w_mod = _mm(
        "mod_bwd", "tn", (1, n_mod_shard // 768, 1),
        c_all, (N_DEV, D_MODEL), lambda i, j, k: (0, 0),
        dmod_cols, (N_DEV, 768), lambda i, j, k: (0, j),
        _sds((D_MODEL, n_mod_shard), F32), (D_MODEL, 768), lambda i, j, k: (0, j),
        (D_MODEL, 768),
    )

    r_rest = red["rest"][0]
    transposed = {"ffn1_gate", "ffn1_up", "ffn2_gate", "ffn2_up", "w_in", "w_uq", "w_ukv"}
    g_big = {
        "ffn2_gate": red["ffn2_gu"][0], "ffn2_up": red["ffn2_gu"][1], "ffn2_down": red["ffn2_down"][0],
        "w_in": r_rest[O_IN:O_IN + R_IN],
        "w_o": r_rest[O_O:O_O + R_O],
        "w_uq": r_rest[O_UQ:O_UQ + R_UQ].reshape(MLA_QK, MLA_Q_RANK),
        "w_ukv": r_rest[O_UKV:O_UKV + R_UKV].reshape(MLA_NOPE + MLA_V, MLA_KV_RANK),
        "w_mod": g_w_mod,
    }
    w_big = {
        "ffn2_gate": (ffn2_gate, m_ffn2_gate, v_ffn2_gate),
        "ffn2_up": (ffn2_up, m_ffn2_up, v_ffn2_up), "ffn2_down": (ffn2_down, m_ffn2_down, v_ffn2_down),
        "w_in": (w_in, m_w_in, v_w_in), "w_o": (w_o, m_w_o, v_w_o), "w_uq": (w_uq, m_w_uq, v_w_uq),
        "w_ukv": (w_ukv, m_w_ukv, v_w_ukv), "w_mod": (w_mod, m_w_mod, v_w_mod),
    }
    grads, deltas, new_m, new_v = {}, {}, {}, {}

    def update(names):
        for nm in names:
            w, m, v = w_big[nm]
            if nm in transposed:
                outs = _adamw(f"adamw_{nm}", w[0].T, g_big[nm], m[0].T, v[0].T)
                g_, d_, m_, v_ = [a.T for a in (g_big[nm],) + tuple(outs)]
            else:
                g_, (d_, m_, v_) = g_big[nm], _adamw(f"adamw_{nm}", w[0], g_big[nm], m[0], v[0])
            grads[nm], deltas[nm], new_m[nm], new_v[nm] = g_[None], d_[None], m_[None], v_[None]

    update(list(w_big))
    red1_down = reducing["ffn1_down"].get([deltas[nm] for nm in w_big])
    red1_gu = reducing["ffn1_gu"].get([red1_down])
    g_big.update({"ffn1_gate": red1_gu[0], "ffn1_up": red1_gu[1], "ffn1_down": red1_down[0]})
    w_big.update({
        "ffn1_gate": (ffn1_gate, m_ffn1_gate, v_ffn1_gate), "ffn1_up": (ffn1_up, m_ffn1_up, v_ffn1_up),
        "ffn1_down": (ffn1_down, m_ffn1_down, v_ffn1_down),
    })
    update(["ffn1_gate", "ffn1_up", "ffn1_down"])

    small_names = ["b_mod", "norm_ffn1", "norm_mix", "norm_ffn2", "norm_final", "q_norm", "kv_norm", "sinks", "rel_bias"]
    w_small = (b_mod, norm_ffn1, norm_mix, norm_ffn2, norm_final, q_norm, kv_norm, sinks, rel_bias)
    m_small = (m_b_mod, m_norm_ffn1, m_norm_mix, m_norm_ffn2, m_norm_final, m_q_norm, m_kv_norm, m_sinks, m_rel_bias)
    v_small = (v_b_mod, v_norm_ffn1, v_norm_mix, v_norm_ffn2, v_norm_final, v_q_norm, v_kv_norm, v_sinks, v_rel_bias)
    g_small = (g_b_mod, g_n1, g_n2, g_n3, g_nf, g_qn, g_kvn, g_sinks, g_rel)
    d_s, m_s, v_s = _adamw_small(w_small, g_small, m_small, v_small)
    for nm, g_, d_, m_, v_ in zip(small_names, g_small, d_s, m_s, v_s):
        grads[nm], deltas[nm], new_m[nm], new_v[nm] = g_, d_, m_, v_

    order = ["w_mod", "b_mod", "norm_ffn1", "ffn1_gate", "ffn1_up", "ffn1_down", "norm_mix", "w_in", "q_norm", "kv_norm",
             "w_uq", "w_ukv", "sinks", "w_o", "norm_ffn2", "ffn2_gate", "ffn2_up", "ffn2_down", "rel_bias", "norm_final"]
    return (loss, grad_x[None], *[grads[n] for n in order], *[deltas[n] for n in order],
            *[new_m[n] for n in order], *[new_v[n] for n in order])
```

```python
import functools
import math

import jax
import jax.numpy as jnp
import numpy as np
from jax import lax
from jax.experimental import pallas as pl
from jax.experimental.pallas import tpu as pltpu

F32, BF16 = jnp.float32, jnp.bfloat16

D_MODEL = 1024
D_FF = 2816
EPS = 1e-6
N_MOD = 9
SWA_HEADS, SWA_KV_HEADS, SWA_HEAD_DIM, WINDOW = 8, 2, 64, 128
SWA_GROUP = SWA_HEADS // SWA_KV_HEADS
MLA_HEADS, MLA_Q_RANK, MLA_KV_RANK, MLA_NOPE, MLA_ROPE, MLA_V = 4, 256, 128, 128, 64, 128
MLA_QK = MLA_NOPE + MLA_ROPE
ROPE_THETA = 10000.0
NUM_BUCKETS, MAX_DISTANCE = 32, 128
D_IN = 1216
N_SWA_COLS = 768
N_MLA_COLS = 512

ADAM_LR, ADAM_B1, ADAM_B2, ADAM_EPS, ADAM_WD, ADAM_STEP = 0.001, 0.9, 0.999, 1e-08, 0.01, 10

N_CHIPS = 4
N_DEV = 8
F_SHARD = D_FF // N_CHIPS
HALF = F_SHARD // 2
R_IN, R_O, R_UQ, R_UKV = 304, 256, 48, 32
O_IN, O_O, O_UQ, O_UKV, O_PAD = 0, 304, 560, 608, 640

PK_MOD = 72
PK_ROWS = 112
VMEM_LIMIT = 56 << 20
ROW_TILE = 2048
ELEM_TILE = 512

_DN = {
    "nn": (((1,), (0,)), ((), ())),
    "nt": (((1,), (1,)), ((), ())),
    "tn": (((0,), (0,)), ((), ())),
}


def _sds(shape, dtype):
    return jax.ShapeDtypeStruct(tuple(shape), dtype)


def _cparams(sem=None):
    kw = {"vmem_limit_bytes": VMEM_LIMIT}
    if sem is not None:
        kw["dimension_semantics"] = sem
    return pltpu.CompilerParams(**kw)


def _dot(a, b, dims):
    return lax.dot_general(a.astype(BF16), b.astype(BF16), _DN[dims], preferred_element_type=F32)


def _mm(name, dims, grid, a, a_blk, a_idx, b, b_blk, b_idx, out, out_blk, out_idx, acc_shape, alias=None, deps=()):
    nk = grid[2]

    def body(*refs):
        a_ref, b_ref = refs[:2]
        o_ref, acc_ref = refs[-2:]
        part = _dot(a_ref[...], b_ref[...], dims)
        if nk == 1:
            o_ref[...] = part.astype(o_ref.dtype)
            return
        k = pl.program_id(2)

        @pl.when(k == 0)
        def _():
            acc_ref[...] = part

        @pl.when((k > 0) & (k < nk - 1))
        def _():
            acc_ref[...] += part

        @pl.when(k == nk - 1)
        def _():
            o_ref[...] = (acc_ref[...] + part).astype(o_ref.dtype)

    in_specs = [pl.BlockSpec(a_blk, a_idx), pl.BlockSpec(b_blk, b_idx)]
    args = [a, b]
    kw = {}
    if alias is not None:
        in_specs.append(pl.BlockSpec(memory_space=pl.ANY))
        args.append(alias)
        kw["input_output_aliases"] = {2: 0}
    in_specs += [pl.BlockSpec(memory_space=pl.ANY)] * len(deps)
    args += list(deps)
    return pl.pallas_call(
        body,
        name=name,
        grid=grid,
        in_specs=in_specs,
        out_specs=pl.BlockSpec(out_blk, out_idx),
        out_shape=out,
        scratch_shapes=[pltpu.VMEM(acc_shape if nk > 1 else (8, 128), F32)],
        compiler_params=_cparams(("parallel", "parallel", "arbitrary")),
        **kw,
    )(*args)


def _rowwise(name, fn, tiled, full, out_tiled, out_red, tm, deps=()):
    rows = tiled[0].shape[-2]
    tm = min(tm, rows)
    grid = (rows // tm,)
    n_in = len(tiled) + len(full)
    n_t = len(out_tiled)
    n_dep = len(deps)

    def tspec(shape):
        nd = len(shape)
        return pl.BlockSpec(tuple(shape[:-2]) + (tm, shape[-1]), lambda i, nd=nd: (0,) * (nd - 2) + (i, 0))

    def fspec(shape):
        nd = len(shape)
        return pl.BlockSpec(tuple(shape), lambda i, nd=nd: (0,) * nd)

    def body(*refs):
        outs = fn(*[r[...] for r in refs[:n_in]])
        if not isinstance(outs, (tuple, list)):
            outs = (outs,)
        out_refs = refs[n_in + n_dep:]
        for r, v in zip(out_refs[:n_t], outs[:n_t]):
            r[...] = v.astype(r.dtype)
        red_refs = out_refs[n_t:]
        if red_refs:
            @pl.when(pl.program_id(0) == 0)
            def _():
                for r in red_refs:
                    r[...] = jnp.zeros_like(r)

            for r, v in zip(red_refs, outs[n_t:]):
                r[...] += v.astype(r.dtype)

    res = pl.pallas_call(
        body,
        name=name,
        grid=grid,
        in_specs=[tspec(a.shape) for a in tiled] + [fspec(a.shape) for a in full]
        + [pl.BlockSpec(memory_space=pl.ANY)] * n_dep,
        out_specs=[tspec(o.shape) for o in out_tiled] + [fspec(o.shape) for o in out_red],
        out_shape=list(out_tiled) + list(out_red),
        compiler_params=_cparams(("arbitrary",) if out_red else ("parallel",)),
    )(*tiled, *full, *deps)
    return res


def _sum0(v):
    return jnp.sum(v, axis=0, keepdims=True)


def _sigmoid(v):
    return 1.0 / (1.0 + jnp.exp(-v))


def _rms(x):
    r = lax.rsqrt(jnp.mean(x * x, axis=-1, keepdims=True) + EPS)
    return x * r, r


def _rms_bwd(u, xr, r):
    return r * (u - xr * jnp.mean(u * xr, axis=-1, keepdims=True))


class _Ready:
    def __init__(self, g):
        self.g = g

    def mid(self, after):
        return []

    def get(self, after):
        return self.g


def _normmod(x_, gamma_, sc_, sh_):
    return _rms(x_)[0] * gamma_ * (1.0 + sc_) + sh_


def _row_blocks(tm, size=256):
    size = min(size, tm)
    return [slice(r, r + size) for r in range(0, tm, size)]


def _loss_head(x_, t_, g_):
    xr, r = _rms(x_)
    err = xr * g_ - t_
    dy = err * (1.0 / D_MODEL)
    return _rms_bwd(dy * g_, xr, r), _sum0(err * err), _sum0(dy * xr)


def _ffn_fwd(tag, x, gamma, sh, sc, gate, gu_src, base, down_src, down_idx, mid_after, h_pre=None,
             next_norm=None, head=None):
    s_len = x.shape[0]
    if h_pre is None:
        (h,) = _rowwise(
            f"{tag}_normmod", _normmod, [x], [gamma, sc, sh], [_sds((s_len, D_MODEL), BF16)], [], ELEM_TILE,
            deps=gu_src.mid(mid_after),
        )
        gdeps = []
    else:
        h, gdeps = h_pre, gu_src.mid(mid_after)
    g4 = gu_src.get(h)

    tm = min(ROW_TILE, s_len)
    def gate_up(h_ref, wg_ref, wu_ref, *rest):
        ab_ref, s_ref = rest[-2:]
        wg, wu = wg_ref[...], wu_ref[...]
        for rows in _row_blocks(tm):
            hv = h_ref[rows, :]
            a = _dot(hv, wg, "nt")
            b = _dot(hv, wu, "nt")
            ab_ref[0, rows, :] = a.astype(ab_ref.dtype)
            ab_ref[1, rows, :] = b.astype(ab_ref.dtype)
            s_ref[rows, :] = (a * _sigmoid(a) * b).astype(s_ref.dtype)

    ab4, s3 = pl.pallas_call(
        gate_up,
        name=f"{tag}_gate_up",
        grid=(s_len // tm, N_CHIPS),
        in_specs=[
            pl.BlockSpec((tm, D_MODEL), lambda i, c: (i, 0)),
            pl.BlockSpec((None, None, F_SHARD, D_MODEL), lambda i, c: (c, base, 0, 0)),
            pl.BlockSpec((None, None, F_SHARD, D_MODEL), lambda i, c: (c, base + 1, 0, 0)),
        ] + [pl.BlockSpec(memory_space=pl.ANY)] * len(gdeps),
        out_specs=[
            pl.BlockSpec((None, 2, tm, F_SHARD), lambda i, c: (c, 0, i, 0)),
            pl.BlockSpec((None, tm, F_SHARD), lambda i, c: (c, i, 0)),
        ],
        out_shape=[_sds((N_CHIPS, 2, s_len, F_SHARD), BF16), _sds((N_CHIPS, s_len, F_SHARD), BF16)],
        compiler_params=_cparams(("parallel", "parallel")),
    )(h, g4, g4, *gdeps)
    if down_src is None:
        g_down, ddeps = g4, ()
    else:
        ddeps = down_src.mid(ab4)
        g_down = down_src.get(s3)

    y = _mm(
        f"{tag}_down", "nn", (s_len // tm, 1, N_CHIPS),
        s3, (None, tm, F_SHARD), lambda i, j, k: (k, i, 0),
        g_down, (None, None, F_SHARD, D_MODEL), lambda i, j, k: (k, down_idx, 0, 0),
        _sds((s_len, D_MODEL), F32), (tm, D_MODEL), lambda i, j, k: (i, 0),
        (tm, D_MODEL), deps=ddeps,
    )

    saved = (g4, base, g_down, down_idx, h, ab4, s3, y)
    act = _sds((s_len, D_MODEL), F32)
    if head is not None:
        target, norm_final = head
        vec = _sds((1, D_MODEL), F32)
        out = _rowwise(
            f"{tag}_residual_head", lambda x_, y_, t_, g_, nf_: _loss_head(x_ + 0.5 * g_ * y_, t_, nf_),
            [x, y, target], [gate, norm_final], [act], [vec, vec], ELEM_TILE,
        )
    elif next_norm is not None:
        def residual_norm(x_, y_, g_, gamma_, sc_, sh_):
            x_out = x_ + 0.5 * g_ * y_
            return x_out, _normmod(x_out, gamma_, sc_, sh_)

        out = _rowwise(
            f"{tag}_residual_norm", residual_norm, [x, y], [gate] + list(next_norm),
            [act, _sds((s_len, D_MODEL), BF16)], [], ELEM_TILE,
        )
    else:
        out = _rowwise(f"{tag}_residual", lambda x_, y_, g_: x_ + 0.5 * g_ * y_, [x, y], [gate], [act], [], ELEM_TILE)
    return out, saved


def _normmod_bwd_call(name, dh_parts, x, dxo, gamma, sc, deps=()):
    s_len = x.shape[0]
    n_parts = len(dh_parts)

    def fn(*vals):
        dh = vals[0]
        for extra in vals[1:n_parts]:
            dh = dh + extra
        x_, dxo_, gamma_, sc_ = vals[n_parts:]
        xr, r = _rms(x_)
        n = xr * gamma_
        dn = dh * (1.0 + sc_)
        dx = dxo_ + _rms_bwd(dn * gamma_, xr, r)
        return dx, _sum0(dh * n), _sum0(dh), _sum0(dn * xr)

    vec = _sds((1, D_MODEL), F32)
    return _rowwise(
        name, fn, list(dh_parts) + [x, dxo], [gamma, sc], [_sds((s_len, D_MODEL), F32)], [vec, vec, vec], ELEM_TILE, deps=deps
    )


class _BwdHooks:
    def __init__(self, after_gate=None, after_swiglu=None, after_wdown=None, after_wgu=None, after_dh=None):
        def nothing(_):
            return []

        self.after_gate = after_gate or nothing
        self.after_swiglu = after_swiglu or nothing
        self.after_wdown = after_wdown or nothing
        self.after_wgu = after_wgu or nothing
        self.after_dh = after_dh or nothing


def _ffn_bwd(tag, dxo, x, gamma, sc, gate, saved, hooks, deps=()):
    g4, base, g_down, down_idx, h, ab4, s3, y = saved
    s_len = x.shape[0]
    vec = _sds((1, D_MODEL), F32)

    dy, dgate = _rowwise(
        f"{tag}_bwd_gate", lambda dxo_, y_, g_: (0.5 * g_ * dxo_, _sum0(0.5 * y_ * dxo_)),
        [dxo, y], [gate], [_sds((s_len, D_MODEL), BF16)], [vec], ELEM_TILE, deps=deps,
    )

    tm = min(ROW_TILE, s_len)

    def d_gate_up(dy_ref, wd_ref, ab_ref, o_ref):
        wd = wd_ref[...]
        for rows in _row_blocks(tm):
            ds = _dot(dy_ref[rows, :], wd, "nt")
            a = ab_ref[0, rows, :].astype(F32)
            b = ab_ref[1, rows, :].astype(F32)
            sig = _sigmoid(a)
            o_ref[0, rows, :] = (ds * b * sig * (1.0 + a * (1.0 - sig))).astype(o_ref.dtype)
            o_ref[1, rows, :] = (ds * a * sig).astype(o_ref.dtype)

    ab_spec = pl.BlockSpec((None, 2, tm, F_SHARD), lambda i, c: (c, 0, i, 0))
    dab4 = pl.pallas_call(
        d_gate_up,
        name=f"{tag}_bwd_dgate_up",
        grid=(s_len // tm, N_CHIPS),
        in_specs=[
            pl.BlockSpec((tm, D_MODEL), lambda i, c: (i, 0)),
            pl.BlockSpec((None, None, F_SHARD, D_MODEL), lambda i, c: (c, down_idx, 0, 0)),
            ab_spec,
        ],
        out_specs=ab_spec,
        out_shape=_sds(ab4.shape, BF16),
        compiler_params=_cparams(("parallel", "parallel")),
    )(dy, g_down, ab4)

    tk = tm
    gb_down = _mm(
        f"{tag}_bwd_wdown", "tn", (N_CHIPS, 1, s_len // tk),
        s3, (None, tk, F_SHARD), lambda i, j, k: (i, k, 0),
        dy, (tk, D_MODEL), lambda i, j, k: (k, 0),
        _sds((N_CHIPS, 1, F_SHARD, D_MODEL), BF16), (None, None, F_SHARD, D_MODEL), lambda i, j, k: (i, 0, 0, 0),
        (F_SHARD, D_MODEL), deps=hooks.after_swiglu(dab4),
    )
    gb_gu = _mm(
        f"{tag}_bwd_wgu", "tn", (2 * N_CHIPS, 1, s_len // tk),
        dab4, (None, None, tk, F_SHARD), lambda i, j, k: (i % N_CHIPS, i // N_CHIPS, k, 0),
        h, (tk, D_MODEL), lambda i, j, k: (k, 0),
        _sds((N_CHIPS, 2, F_SHARD, D_MODEL), BF16), (None, None, F_SHARD, D_MODEL),
        lambda i, j, k: (i % N_CHIPS, i // N_CHIPS, 0, 0),
        (F_SHARD, D_MODEL), deps=hooks.after_wdown(gb_down),
    )
    assert base % 2 == 0
    dh_deps = hooks.after_wgu(gb_gu)

    def d_h(dab_ref, w_ref, *rest):
        o_ref, acc_ref = rest[-2:]
        c = pl.program_id(1)
        part = _dot(dab_ref[0], w_ref[0], "nn") + _dot(dab_ref[1], w_ref[1], "nn")

        @pl.when(c == 0)
        def _():
            acc_ref[...] = part

        @pl.when((c > 0) & (c < N_CHIPS - 1))
        def _():
            acc_ref[...] += part

        @pl.when(c == N_CHIPS - 1)
        def _():
            o_ref[...] = acc_ref[...] + part

    dh = pl.pallas_call(
        d_h,
        name=f"{tag}_bwd_dh",
        grid=(s_len // tm, N_CHIPS),
        in_specs=[
            pl.BlockSpec((None, 2, tm, F_SHARD), lambda i, c: (c, 0, i, 0)),
            pl.BlockSpec((None, 2, F_SHARD, D_MODEL), lambda i, c: (c, base // 2, 0, 0)),
        ] + [pl.BlockSpec(memory_space=pl.ANY)] * len(dh_deps),
        out_specs=pl.BlockSpec((tm, D_MODEL), lambda i, c: (i, 0)),
        out_shape=_sds((s_len, D_MODEL), F32),
        scratch_shapes=[pltpu.VMEM((tm, D_MODEL), F32)],
        compiler_params=_cparams(("parallel", "arbitrary")),
    )(dab4, g4, *dh_deps)
    dx, dsc, dsh, dgamma = _normmod_bwd_call(
        f"{tag}_bwd_normmod", [dh], x, dxo, gamma, sc, deps=hooks.after_dh(dh)
    )
    return dx, (gb_down, gb_gu), (dsh, dsc, dgate, dgamma)


def _bucket_map():
    qi = np.arange(WINDOW)[:, None]
    kj = np.arange(2 * WINDOW)[None, :]
    dist = qi + WINDOW - kj
    band = (dist >= 0) & (dist < WINDOW)
    max_exact = NUM_BUCKETS // 2
    n = np.maximum(dist, 0)
    large = []
    for dt in (np.float32, np.float64):
        nf = np.maximum(n, 1).astype(dt)
        val = np.log(nf / dt(max_exact)) / dt(math.log(MAX_DISTANCE / max_exact)) * dt(NUM_BUCKETS - max_exact)
        large.append(np.minimum(max_exact + val.astype(np.int32), NUM_BUCKETS - 1))
    assert np.array_equal(large[0], large[1])
    bucket = np.where(n < max_exact, n, large[0])
    return np.where(band, bucket, -1).astype(np.int32).reshape(1, -1)


def _bias_expand(rel_bias_t, bkt):
    n = bkt.shape[1]

    def body(rb_ref, bkt_ref, o_ref):
        onehot = (lax.broadcasted_iota(jnp.int32, (NUM_BUCKETS, n), 0) == bkt_ref[...]).astype(F32)
        o_ref[...] = lax.dot_general(
            rb_ref[...], onehot, _DN["nn"], precision=lax.Precision.HIGHEST, preferred_element_type=F32
        )

    return pl.pallas_call(
        body, name="bias_expand", out_shape=_sds((SWA_HEADS, n), F32), compiler_params=_cparams()
    )(rel_bias_t, bkt)


def _bias_reduce(dbias_flat, bkt):
    n = bkt.shape[1]

    def body(db_ref, bkt_ref, o_ref):
        onehot = (lax.broadcasted_iota(jnp.int32, (NUM_BUCKETS, n), 0) == bkt_ref[...]).astype(F32)
        o_ref[...] = lax.dot_general(
            db_ref[...], onehot, _DN["nt"], precision=lax.Precision.HIGHEST, preferred_element_type=F32
        )

    return pl.pallas_call(
        body, name="bias_reduce", out_shape=_sds((SWA_HEADS, NUM_BUCKETS), F32), compiler_params=_cparams()
    )(dbias_flat, bkt)


_SWA_SCALE = SWA_HEAD_DIM ** -0.5
_NEG = -1e30


_SWA_PAIR = 2


def _swa_in_specs():
    w = WINDOW
    n_q = SWA_HEADS * SWA_HEAD_DIM
    n_kv = 2 * SWA_KV_HEADS * SWA_HEAD_DIM
    prev = lambda m: jnp.maximum(_SWA_PAIR * m - 1, 0)
    return [
        pl.BlockSpec((_SWA_PAIR * w, n_q), lambda m: (m, 0)),
        pl.BlockSpec((w, n_kv), lambda m: (prev(m), n_q // n_kv)),
        pl.BlockSpec((_SWA_PAIR * w, n_kv), lambda m: (m, n_q // n_kv)),
        pl.BlockSpec((SWA_HEADS, w, 2 * w), lambda m: (0, 0, 0)),
        pl.BlockSpec((SWA_HEADS, w, 1), lambda m: (0, 0, 0)),
    ]


def _head_cols(v, first, count):
    hd = SWA_HEAD_DIM
    return jnp.stack([v[:, (first + i) * hd:(first + i + 1) * hd] for i in range(count)])


def _swa_blocks(q_ref, kvp_ref, kvc_ref, m):
    g, w, hd = SWA_GROUP, WINDOW, SWA_HEAD_DIM
    kv_all = jnp.concatenate([kvp_ref[...], kvc_ref[...]], axis=0).astype(F32)
    blocks = []
    for sub in range(_SWA_PAIR):
        rows = slice(sub * w, (sub + 1) * w)
        q = q_ref[rows, :].astype(F32)
        kv = kv_all[sub * w:(sub + 2) * w]
        heads = []
        for j in range(SWA_KV_HEADS):
            qj = _head_cols(q, g * j, g).reshape(g * w, hd)
            heads.append((qj, _head_cols(kv, j, 1)[0], _head_cols(kv, SWA_KV_HEADS + j, 1)[0]))
        blocks.append((_SWA_PAIR * m + sub, rows, heads))
    return blocks


def _swa_scores(q, kk, bias, n):
    g, w = SWA_GROUP, WINDOW
    s = (_dot(q, kk, "nt") * _SWA_SCALE).reshape(g, w, 2 * w) + bias
    qi = lax.broadcasted_iota(jnp.int32, (w, 2 * w), 0)
    kj = lax.broadcasted_iota(jnp.int32, (w, 2 * w), 1)
    dist = qi + w - kj
    valid = ((dist >= 0) & (dist < w) & ((n > 0) | (kj >= w)))[None]
    return jnp.where(valid, s, _NEG), valid


def _swa_fwd(swa2, bias, sinkb):
    s_len = swa2.shape[0]
    g, w, hd = SWA_GROUP, WINDOW, SWA_HEAD_DIM

    def body(q_ref, kvp_ref, kvc_ref, bias_ref, sink_ref, o_ref, lse_ref):
        for n, rows, heads in _swa_blocks(q_ref, kvp_ref, kvc_ref, pl.program_id(0)):
            outs = []
            for j, (q, kk, vv) in enumerate(heads):
                hs = slice(g * j, g * (j + 1))
                s, valid = _swa_scores(q, kk, bias_ref[hs], n)
                sink = sink_ref[hs]
                m = jnp.maximum(jnp.max(s, axis=-1, keepdims=True), sink)
                p = jnp.where(valid, jnp.exp(s - m), 0.0)
                l = jnp.sum(p, axis=-1, keepdims=True) + jnp.exp(sink - m)
                o = _dot(p.reshape(g * w, 2 * w), vv, "nn").reshape(g, w, hd) / l
                outs += [o[i] for i in range(g)]
                lse_ref[hs, rows, :] = m + jnp.log(l)
            o_ref[rows, :] = jnp.concatenate(outs, axis=-1).astype(o_ref.dtype)

    n_q = SWA_HEADS * hd
    return pl.pallas_call(
        body,
        name="swa_fwd",
        grid=(s_len // (_SWA_PAIR * w),),
        in_specs=_swa_in_specs(),
        out_specs=[
            pl.BlockSpec((_SWA_PAIR * w, n_q), lambda m: (m, 0)),
            pl.BlockSpec((SWA_HEADS, _SWA_PAIR * w, 1), lambda m: (0, m, 0)),
        ],
        out_shape=[_sds((s_len, n_q), BF16), _sds((SWA_HEADS, s_len, 1), F32)],
        compiler_params=_cparams(("parallel",)),
    )(swa2, swa2, swa2, bias, sinkb)


def _swa_bwd(swa2, bias, sinkb, cat, dcat, lse):
    s_len = swa2.shape[0]
    g, w, hd = SWA_GROUP, WINDOW, SWA_HEAD_DIM

    def body(q_ref, kvp_ref, kvc_ref, bias_ref, sink_ref, o_ref, do_ref, lse_ref,
             dq_ref, dkva_ref, dkvb_ref, dbias_ref, dsink_ref):
        step = pl.program_id(0)

        @pl.when(step == 0)
        def _():
            dbias_ref[...] = jnp.zeros_like(dbias_ref)
            dsink_ref[...] = jnp.zeros_like(dsink_ref)

        for n, rows, heads in _swa_blocks(q_ref, kvp_ref, kvc_ref, step):
            o_all = o_ref[rows, :].astype(F32)
            do_all = do_ref[rows, :].astype(F32)
            dqs, dks, dvs = [], [], []
            for j, (q, kk, vv) in enumerate(heads):
                hs = slice(g * j, g * (j + 1))
                s, valid = _swa_scores(q, kk, bias_ref[hs], n)
                lse_v = lse_ref[hs, rows, :]
                p = jnp.where(valid, jnp.exp(s - lse_v), 0.0)
                do = _head_cols(do_all, g * j, g)
                delta = jnp.sum(do * _head_cols(o_all, g * j, g), axis=-1, keepdims=True)
                do2 = do.reshape(g * w, hd)
                dp = _dot(do2, vv, "nt").reshape(g, w, 2 * w)
                ds = p * (dp - delta)
                dbias_ref[hs] += ds
                dsink_ref[hs] -= jnp.exp(sink_ref[hs] - lse_v) * delta
                ds2 = ds.reshape(g * w, 2 * w)
                dq = (_dot(ds2, kk, "nn") * _SWA_SCALE).reshape(g, w, hd)
                dqs += [dq[i] for i in range(g)]
                dks.append(_dot(ds2, q, "tn") * _SWA_SCALE)
                dvs.append(_dot(p.reshape(g * w, 2 * w), do2, "tn"))
            dq_ref[rows, :] = jnp.concatenate(dqs, axis=-1).astype(dq_ref.dtype)
            dkv = jnp.concatenate(dks + dvs, axis=-1)
            dkvb_ref[rows, :] = dkv[:w]
            dkva_ref[rows, :] = dkv[w:]

    n_q = SWA_HEADS * hd
    n_kv = 2 * SWA_KV_HEADS * hd
    q_spec = pl.BlockSpec((_SWA_PAIR * w, n_q), lambda m: (m, 0))
    kv_spec = pl.BlockSpec((_SWA_PAIR * w, n_kv), lambda m: (m, 0))
    return pl.pallas_call(
        body,
        name="swa_bwd",
        grid=(s_len // (_SWA_PAIR * w),),
        in_specs=_swa_in_specs() + [q_spec, q_spec, pl.BlockSpec((SWA_HEADS, _SWA_PAIR * w, 1), lambda m: (0, m, 0))],
        out_specs=[
            q_spec, kv_spec, kv_spec,
            pl.BlockSpec((SWA_HEADS, w, 2 * w), lambda n: (0, 0, 0)),
            pl.BlockSpec((SWA_HEADS, w, 1), lambda n: (0, 0, 0)),
        ],
        out_shape=[
            _sds((s_len, n_q), BF16), _sds((s_len, n_kv), F32), _sds((s_len, n_kv), F32),
            _sds((SWA_HEADS, w, 2 * w), F32), _sds((SWA_HEADS, w, 1), F32),
        ],
        compiler_params=_cparams(("arbitrary",)),
    )(swa2, swa2, swa2, bias, sinkb, cat, dcat, lse)


_MLA_SCALE = MLA_QK ** -0.5
_MLA_TQ = 256
_MLA_FWD_HEADS = 4
_MLA_BWD_HEADS = 2


def _mla_mask(i, tq, n_keys):
    row = i * tq + lax.broadcasted_iota(jnp.int32, (tq, n_keys), 0)
    col = lax.broadcasted_iota(jnp.int32, (tq, n_keys), 1)
    return col <= row


def _per_query_block(i, n_blocks, fn):
    for ii in range(n_blocks):
        pl.when(i == ii)(functools.partial(fn, ii))


def _mla_fwd(q4, k4, v4):
    s_len = q4.shape[1]
    tq = min(_MLA_TQ, s_len)
    pair = _MLA_FWD_HEADS

    def body(q_ref, k_ref, v_ref, o_ref, lse_ref):
        def block(ii):
            n_keys = (ii + 1) * tq
            mask = _mla_mask(ii, tq, n_keys)
            for hh in range(pair):
                s = jnp.where(mask, _dot(q_ref[hh], k_ref[hh, :n_keys, :], "nt") * _MLA_SCALE, _NEG)
                m = jnp.max(s, axis=-1, keepdims=True)
                p = jnp.exp(s - m)
                l = jnp.sum(p, axis=-1, keepdims=True)
                o_ref[:, hh * MLA_V:(hh + 1) * MLA_V] = (_dot(p, v_ref[hh, :n_keys, :], "nn") / l).astype(o_ref.dtype)
                lse_ref[hh] = m + jnp.log(l)

        _per_query_block(pl.program_id(1), s_len // tq, block)

    return pl.pallas_call(
        body,
        name="mla_fwd",
        grid=(MLA_HEADS // pair, s_len // tq),
        in_specs=[
            pl.BlockSpec((pair, tq, MLA_QK), lambda h, i: (h, i, 0)),
            pl.BlockSpec((pair, s_len, MLA_QK), lambda h, i: (h, 0, 0)),
            pl.BlockSpec((pair, s_len, MLA_V), lambda h, i: (h, 0, 0)),
        ],
        out_specs=[
            pl.BlockSpec((tq, pair * MLA_V), lambda h, i: (i, h)),
            pl.BlockSpec((pair, tq, 1), lambda h, i: (h, i, 0)),
        ],
        out_shape=[_sds((s_len, MLA_HEADS * MLA_V), BF16), _sds((MLA_HEADS, s_len, 1), F32)],
        compiler_params=_cparams(("parallel", "parallel")),
    )(q4, k4, v4)


def _mla_bwd(q4, k4, v4, cat, dcat, lse):
    s_len = q4.shape[1]
    tq = min(_MLA_TQ, s_len)
    pair = _MLA_BWD_HEADS
    first = SWA_HEADS * SWA_HEAD_DIM // (pair * MLA_V)

    def body(q_ref, k_ref, v_ref, o_ref, do_ref, lse_ref, dq_ref, dk_ref, dv_ref):
        i = pl.program_id(1)

        @pl.when(i == 0)
        def _():
            dk_ref[...] = jnp.zeros_like(dk_ref)
            dv_ref[...] = jnp.zeros_like(dv_ref)

        def block(ii):
            n_keys = (ii + 1) * tq
            mask = _mla_mask(ii, tq, n_keys)
            for hh in range(pair):
                cols = slice(hh * MLA_V, (hh + 1) * MLA_V)
                q, k, v, do = q_ref[hh], k_ref[hh, :n_keys, :], v_ref[hh, :n_keys, :], do_ref[:, cols]
                s = jnp.where(mask, _dot(q, k, "nt") * _MLA_SCALE, _NEG)
                p = jnp.where(mask, jnp.exp(s - lse_ref[hh]), 0.0)
                delta = jnp.sum(do.astype(F32) * o_ref[:, cols].astype(F32), axis=-1, keepdims=True)
                ds = (p * (_dot(do, v, "nt") - delta) * _MLA_SCALE).astype(BF16)
                dq_ref[hh] = _dot(ds, k, "nn")
                dk_ref[hh, :n_keys, :] += _dot(ds, q, "tn")
                dv_ref[hh, :n_keys, :] += _dot(p, do, "tn")

        _per_query_block(i, s_len // tq, block)

    return pl.pallas_call(
        body,
        name="mla_bwd",
        grid=(MLA_HEADS // pair, s_len // tq),
        in_specs=[
            pl.BlockSpec((pair, tq, MLA_QK), lambda h, i: (h, i, 0)),
            pl.BlockSpec((pair, s_len, MLA_QK), lambda h, i: (h, 0, 0)),
            pl.BlockSpec((pair, s_len, MLA_V), lambda h, i: (h, 0, 0)),
            pl.BlockSpec((tq, pair * MLA_V), lambda h, i: (i, first + h)),
            pl.BlockSpec((tq, pair * MLA_V), lambda h, i: (i, first + h)),
            pl.BlockSpec((pair, tq, 1), lambda h, i: (h, i, 0)),
        ],
        out_specs=[
            pl.BlockSpec((pair, tq, MLA_QK), lambda h, i: (h, i, 0)),
            pl.BlockSpec((pair, s_len, MLA_QK), lambda h, i: (h, 0, 0)),
            pl.BlockSpec((pair, s_len, MLA_V), lambda h, i: (h, 0, 0)),
        ],
        out_shape=[
            _sds((MLA_HEADS, s_len, MLA_QK), F32),
            _sds((MLA_HEADS, s_len, MLA_QK), F32),
            _sds((MLA_HEADS, s_len, MLA_V), F32),
        ],
        compiler_params=_cparams(("parallel", "arbitrary")),
    )(q4, k4, v4, cat, dcat, lse)


def _mla_split(pm):
    q_lat = pm[:, :MLA_Q_RANK]
    kv_lat = pm[:, MLA_Q_RANK:MLA_Q_RANK + MLA_KV_RANK]
    kr = pm[:, MLA_Q_RANK + MLA_KV_RANK:MLA_Q_RANK + MLA_KV_RANK + MLA_ROPE]
    kr_sw = pm[:, MLA_Q_RANK + MLA_KV_RANK + MLA_ROPE:]
    return q_lat, kv_lat, kr, kr_sw


def _mla_proj(pm, cos2, sin2, q_norm, kv_norm, wq_ext, wkv):
    s_len = pm.shape[0]

    def fn(pm_, cos_, sin_, qg, kvg, wq, wk):
        q_lat, kv_lat, kr, kr_sw = _mla_split(pm_)
        nq = (_rms(q_lat)[0] * qg).astype(BF16)
        nkv = (_rms(kv_lat)[0] * kvg).astype(BF16)
        k_rot = kr * cos_ + kr_sw * sin_
        qs, ks, vs = [], [], []
        for h in range(MLA_HEADS):
            qe = _dot(nq, wq[h], "nt")
            q_rot = qe[:, MLA_NOPE:MLA_QK] * cos_ + qe[:, MLA_QK:] * sin_
            qs.append(jnp.concatenate([qe[:, :MLA_NOPE], q_rot], axis=-1))
            kve = _dot(nkv, wk[h], "nt")
            ks.append(jnp.concatenate([kve[:, :MLA_NOPE], k_rot], axis=-1))
            vs.append(kve[:, MLA_NOPE:])
        return jnp.stack(qs), jnp.stack(ks), jnp.stack(vs)

    return _rowwise(
        "mla_proj", fn, [pm, cos2, sin2], [q_norm, kv_norm, wq_ext, wkv],
        [_sds((MLA_HEADS, s_len, MLA_QK), BF16), _sds((MLA_HEADS, s_len, MLA_QK), BF16),
         _sds((MLA_HEADS, s_len, MLA_V), BF16)], [], 256,
    )


def _mla_proj_bwd(pm, cos2, sin2, dq4, dk4, dv4, q_norm, kv_norm, wq_ext, wkv):
    s_len = pm.shape[0]

    def fn(pm_, cos_, sin_, dq, dk, dv, qg, kvg, wq, wk):
        q_lat, kv_lat, _, _ = _mla_split(pm_)
        xq, rq = _rms(q_lat)
        xkv, rkv = _rms(kv_lat)
        nq = (xq * qg).astype(BF16)
        nkv = (xkv * kvg).astype(BF16)
        dnq = jnp.zeros_like(q_lat)
        dnkv = jnp.zeros_like(kv_lat)
        dkr = jnp.zeros_like(cos_)
        dwq, dwk = [], []
        for h in range(MLA_HEADS):
            dy = dq[h][:, MLA_NOPE:]
            dqe = jnp.concatenate([dq[h][:, :MLA_NOPE], dy * cos_, dy * sin_], axis=-1).astype(BF16)
            dnq = dnq + _dot(dqe, wq[h], "nn")
            dwq.append(_dot(dqe, nq, "tn"))
            dkve = jnp.concatenate([dk[h][:, :MLA_NOPE], dv[h]], axis=-1).astype(BF16)
            dnkv = dnkv + _dot(dkve, wk[h], "nn")
            dwk.append(_dot(dkve, nkv, "tn"))
            dkr = dkr + dk[h][:, MLA_NOPE:]
        dq_lat = _rms_bwd(dnq * qg, xq, rq)
        dkv_lat = _rms_bwd(dnkv * kvg, xkv, rkv)
        dpm = jnp.concatenate([dq_lat, dkv_lat, dkr * cos_, dkr * sin_], axis=-1)
        return dpm, _sum0(dnq * xq), _sum0(dnkv * xkv), jnp.stack(dwq), jnp.stack(dwk)

    return _rowwise(
        "mla_proj_bwd", fn, [pm, cos2, sin2, dq4, dk4, dv4], [q_norm, kv_norm, wq_ext, wkv],
        [_sds((s_len, N_MLA_COLS), BF16)],
        [_sds((1, MLA_Q_RANK), F32), _sds((1, MLA_KV_RANK), F32),
         _sds(wq_ext.shape, F32), _sds(wkv.shape, F32)], 256,
    )


def _rope_tables(s_len):
    inv = ROPE_THETA ** (-jnp.arange(0, MLA_ROPE, 2, dtype=F32) / MLA_ROPE)
    ang = jnp.arange(s_len, dtype=F32)[:, None] * inv[None, :]
    cos, sin = jnp.cos(ang), jnp.sin(ang)
    return jnp.concatenate([cos, cos], axis=-1), jnp.concatenate([-sin, sin], axis=-1)


def _mix_weights(g_mix):
    rest = g_mix[:, 0]
    w_in_t = rest[:, O_IN:O_IN + R_IN].reshape(D_IN, D_MODEL)
    w_o = rest[:, O_O:O_O + R_O].reshape(D_MODEL, D_MODEL)
    w_uq_t = rest[:, O_UQ:O_UQ + R_UQ].reshape(MLA_HEADS, MLA_QK, MLA_Q_RANK)
    w_ukv_t = rest[:, O_UKV:O_UKV + R_UKV].reshape(MLA_HEADS, MLA_NOPE + MLA_V, MLA_KV_RANK)
    half = MLA_ROPE // 2
    w_swa = w_in_t[:N_SWA_COLS]
    w_mla = jnp.concatenate([w_in_t[N_SWA_COLS:], w_in_t[D_IN - half:], w_in_t[D_IN - MLA_ROPE:D_IN - half]], axis=0)
    wq_ext = jnp.concatenate([w_uq_t, w_uq_t[:, MLA_QK - half:], w_uq_t[:, MLA_NOPE:MLA_QK - half]], axis=1)
    return w_swa, w_mla, w_o, wq_ext, w_ukv_t


def _mix_fwd(x, h, gate, mix_src, q_norm, kv_norm, bias, sinkb, cos2, sin2, next_norm):
    s_len = x.shape[0]
    tm = min(ROW_TILE, s_len)
    deps = mix_src.mid(x)
    weights = _mix_weights(mix_src.get(h))
    w_swa, w_mla, w_o, wq_ext, wkv = weights
    swa2 = _mm(
        "mix_proj_swa", "nt", (s_len // tm, 1, 1),
        h, (tm, D_MODEL), lambda i, j, k: (i, 0),
        w_swa, (N_SWA_COLS, D_MODEL), lambda i, j, k: (0, 0),
        _sds((s_len, N_SWA_COLS), BF16), (tm, N_SWA_COLS), lambda i, j, k: (i, 0),
        (tm, N_SWA_COLS), deps=deps,
    )
    pm = _mm(
        "mix_proj_mla", "nt", (s_len // tm, 1, 1),
        h, (tm, D_MODEL), lambda i, j, k: (i, 0),
        w_mla, (N_MLA_COLS, D_MODEL), lambda i, j, k: (0, 0),
        _sds((s_len, N_MLA_COLS), F32), (tm, N_MLA_COLS), lambda i, j, k: (i, 0),
        (tm, N_MLA_COLS),
    )
    q4, k4, v4 = _mla_proj(pm, cos2, sin2, q_norm, kv_norm, wq_ext, wkv)
    oa, lse_a = _swa_fwd(swa2, bias, sinkb)
    ob, lse_b = _mla_fwd(q4, k4, v4)
    cat = jnp.concatenate([oa, ob], axis=1)
    z = _mm(
        "mix_out", "nn", (s_len // tm, 1, 1),
        cat, (tm, D_MODEL), lambda i, j, k: (i, 0),
        w_o, (D_MODEL, D_MODEL), lambda i, j, k: (0, 0),
        _sds((s_len, D_MODEL), F32), (tm, D_MODEL), lambda i, j, k: (i, 0),
        (tm, D_MODEL),
    )
    def residual_norm(x_, z_, g_, gamma_, sc_, sh_):
        x_out = x_ + g_ * z_
        return x_out, _normmod(x_out, gamma_, sc_, sh_)

    out = _rowwise(
        "mix_residual_norm", residual_norm, [x, z], [gate] + list(next_norm),
        [_sds((s_len, D_MODEL), F32), _sds((s_len, D_MODEL), BF16)], [], ELEM_TILE,
    )
    return out, (weights, h, swa2, pm, q4, k4, v4, cat, lse_a, lse_b, z)


def _mix_bwd(dxo, x, gamma, sc, gate, q_norm, kv_norm, bias, sinkb, cos2, sin2, saved, hooks):
    weights, h, swa2, pm, q4, k4, v4, cat, lse_a, lse_b, z = saved
    w_swa, w_mla, w_o, wq_ext, wkv = weights
    s_len = x.shape[0]
    tm = tk = min(ROW_TILE, s_len)
    vec = _sds((1, D_MODEL), F32)
    n_proj = N_SWA_COLS + N_MLA_COLS
    half_d = D_MODEL // 2

    dz, dgate = _rowwise(
        "mix_bwd_gate", lambda dxo_, z_, g_: (g_ * dxo_, _sum0(z_ * dxo_)),
        [dxo, z], [gate], [_sds((s_len, D_MODEL), BF16)], [vec], ELEM_TILE,
    )
    dw_o = _mm(
        "mix_bwd_wo", "tn", (2, 1, s_len // tk),
        cat, (tk, half_d), lambda i, j, k: (k, i),
        dz, (tk, D_MODEL), lambda i, j, k: (k, 0),
        _sds((D_MODEL, D_MODEL), F32), (half_d, D_MODEL), lambda i, j, k: (i, 0),
        (half_d, D_MODEL),
    )
    dcat = _mm(
        "mix_bwd_dcat", "nt", (s_len // tm, 1, 1),
        dz, (tm, D_MODEL), lambda i, j, k: (i, 0),
        w_o, (D_MODEL, D_MODEL), lambda i, j, k: (0, 0),
        _sds((s_len, D_MODEL), BF16), (tm, D_MODEL), lambda i, j, k: (i, 0),
        (tm, D_MODEL), deps=hooks.after_gate(dz),
    )
    dq_a, dkva, dkvb, dbias, dsink = _swa_bwd(swa2, bias, sinkb, cat, dcat, lse_a)
    dq4, dk4, dv4 = _mla_bwd(q4, k4, v4, cat, dcat, lse_b)
    dpm, dq_norm, dkv_norm, dwq_ext, dwkv = _mla_proj_bwd(pm, cos2, sin2, dq4, dk4, dv4, q_norm, kv_norm, wq_ext, wkv)

    dkv = dkva + jnp.concatenate([dkvb[WINDOW:], jnp.zeros_like(dkvb[:WINDOW])], axis=0)
    dproj = jnp.concatenate([dq_a, dkv.astype(BF16), dpm], axis=1)
    w_proj = jnp.concatenate([w_swa, w_mla], axis=0)

    dw_proj = _mm(
        "mix_bwd_win", "tn", (n_proj // 256, 1, s_len // tk),
        dproj, (tk, 256), lambda i, j, k: (k, i),
        h, (tk, D_MODEL), lambda i, j, k: (k, 0),
        _sds((n_proj, D_MODEL), F32), (256, D_MODEL), lambda i, j, k: (i, 0),
        (256, D_MODEL),
    )
    dw_swa, dw_mla = dw_proj[:N_SWA_COLS], dw_proj[N_SWA_COLS:]
    dh = _mm(
        "mix_bwd_dh", "nn", (s_len // tm, 1, 1),
        dproj, (tm, n_proj), lambda i, j, k: (i, 0),
        w_proj, (n_proj, D_MODEL), lambda i, j, k: (0, 0),
        _sds((s_len, D_MODEL), F32), (tm, D_MODEL), lambda i, j, k: (i, 0),
        (tm, D_MODEL),
    )
    dx, dsc, dsh, dgamma = _normmod_bwd_call("mix_bwd_normmod", [dh], x, dxo, gamma, sc)

    half = MLA_ROPE // 2
    n_mla_in = D_IN - N_SWA_COLS
    dw_mla_base = dw_mla[:n_mla_in]
    dw_mla_base = dw_mla_base.at[n_mla_in - half:].add(dw_mla[n_mla_in:n_mla_in + half])
    dw_mla_base = dw_mla_base.at[n_mla_in - MLA_ROPE:n_mla_in - half].add(dw_mla[n_mla_in + half:])
    dw_in_t = jnp.concatenate([dw_swa, dw_mla_base], axis=0)
    dw_uq_t = dwq_ext[:, :MLA_QK]
    dw_uq_t = dw_uq_t.at[:, MLA_QK - half:].add(dwq_ext[:, MLA_QK:MLA_QK + half])
    dw_uq_t = dw_uq_t.at[:, MLA_NOPE:MLA_QK - half].add(dwq_ext[:, MLA_QK + half:])
    rest = jnp.concatenate(
        [
            dw_in_t.reshape(N_CHIPS, R_IN, D_MODEL),
            dw_o.reshape(N_CHIPS, R_O, D_MODEL),
            dw_uq_t.reshape(N_CHIPS, R_UQ, D_MODEL),
            dwkv.reshape(N_CHIPS, R_UKV, D_MODEL),
            jnp.zeros((N_CHIPS, F_SHARD - O_PAD, D_MODEL), F32),
        ],
        axis=1,
    ).astype(BF16)
    return dx, rest, (dsh, dsc, dgate, dgamma), dq_norm, dkv_norm, dbias, dsink


def _device_step(x, target, mod, gu1_src, down1_src, mix_src, ffn2_src, norms, q_norm, kv_norm, sinks, rel_bias,
                 hooks2=None, hooks_mix=None, mix_done=None, hooks1=None):
    s_len = x.shape[0]
    sh1, sc1, g1, sh2, sc2, g2, sh3, sc3, g3 = [mod[:, i * D_MODEL:(i + 1) * D_MODEL] for i in range(N_MOD)]
    n1, n2, n3, nf = norms
    bkt = jnp.asarray(_bucket_map())
    bias = _bias_expand(rel_bias.T, bkt).reshape(SWA_HEADS, WINDOW, 2 * WINDOW)
    sinkb = jnp.broadcast_to(sinks.reshape(SWA_HEADS, 1, 1), (SWA_HEADS, WINDOW, 1))
    cos2, sin2 = _rope_tables(s_len)

    (x1, h2), saved1 = _ffn_fwd(
        "ffn1", x, n1, sh1, sc1, g1, gu1_src, 0, down1_src, 0, sh1, next_norm=(n2, sc2, sh2)
    )
    (x2, h3), saved2 = _mix_fwd(
        x1, h2, g2, mix_src, q_norm, kv_norm, bias, sinkb, cos2, sin2, next_norm=(n3, sc3, sh3)
    )
    (dx3, sq, dnf), saved3 = _ffn_fwd(
        "ffn2", x2, n3, sh3, sc3, g3, ffn2_src, 0, None, 2, x2, h_pre=h3, head=(target, nf)
    )
    loss_part = (0.5 / D_MODEL) * jnp.sum(sq)

    dx2, gb2, (dsh3, dsc3, dg3, dn3) = _ffn_bwd("ffn2", dx3, x2, n3, sc3, g3, saved3, hooks2 or _BwdHooks())
    dx1, rest, (dsh2, dsc2, dg2, dn2), dq_norm, dkv_norm, dbias, dsink = _mix_bwd(
        dx2, x1, n2, sc2, g2, q_norm, kv_norm, bias, sinkb, cos2, sin2, saved2, hooks_mix or _BwdHooks()
    )
    rest = rest[:, None]
    deps1 = mix_done(dx1, rest) if mix_done is not None else []
    dx0, gb1, (dsh1, dsc1, dg1, dn1) = _ffn_bwd(
        "ffn1", dx1, x, n1, sc1, g1, saved1, hooks1 or _BwdHooks(), deps=deps1
    )

    dmod = jnp.concatenate([dsh1, dsc1, dg1, dsh2, dsc2, dg2, dsh3, dsc3, dg3], axis=-1)
    drel = _bias_reduce(dbias.reshape(SWA_HEADS, -1), bkt).T
    dsinks = jnp.sum(dsink, axis=(1, 2)).reshape(1, SWA_HEADS)
    small = (dn1, dn2, dn3, dnf, dq_norm, dkv_norm, dsinks, drel)
    return loss_part, dx0, (gb1, gb2, rest), dmod, small


_MESH = pl.DeviceIdType.MESH


def _place():
    return lax.axis_index("x"), lax.axis_index("y"), lax.axis_index("c")


def _other_chips(x, y):
    return [(1 - x, y), (x, 1 - y), (1 - x, 1 - y)]


def _allgather_small(name, blk):
    m_per, n = blk.shape

    def body(x_ref, out_ref, send_sems, recv_sems, local_sem):
        x, y, c = _place()
        me, sibling = (x, y, c), (x, y, 1 - c)
        chips = _other_chips(x, y)

        def rows(px, py, pc):
            return out_ref.at[pl.ds((4 * px + 2 * py + pc) * m_per, m_per), :]

        def copy(k, block, to, src=None):
            return pltpu.make_async_remote_copy(
                src_ref=rows(*block) if src is None else src, dst_ref=rows(*block),
                send_sem=send_sems.at[k], recv_sem=recv_sems.at[k], device_id=to, device_id_type=_MESH,
            )

        mine = pltpu.make_async_copy(x_ref, rows(*me), local_sem)
        mine.start()
        first = [copy(0, me, sibling, src=x_ref)]
        first += [copy(1 + j, me, (*chip, c), src=x_ref) for j, chip in enumerate(chips)]
        for cp in first:
            cp.start()
        passed = [copy(4 + j, (*chip, c), sibling) for j, chip in enumerate(chips)]
        for j, chip in enumerate(chips):
            copy(1 + j, (*chip, c), me).wait_recv()
            passed[j].start()
        copy(0, sibling, me).wait_recv()
        for j, chip in enumerate(chips):
            copy(4 + j, (*chip, 1 - c), me).wait_recv()
        for cp in first + passed:
            cp.wait_send()
        mine.wait()

    return pl.pallas_call(
        body,
        name=name,
        out_shape=_sds((N_DEV * m_per, n), blk.dtype),
        in_specs=[pl.BlockSpec(memory_space=pltpu.VMEM)],
        out_specs=pl.BlockSpec(memory_space=pltpu.VMEM),
        scratch_shapes=[pltpu.SemaphoreType.DMA((7,)), pltpu.SemaphoreType.DMA((7,)), pltpu.SemaphoreType.DMA],
    )(blk)


def _gather_sends(n, own_ref, g_ref, send_sem, recv_sem, x, y, c, chip):
    return [
        pltpu.make_async_remote_copy(
            src_ref=own_ref.at[p, pl.ds(c * HALF, HALF), :], dst_ref=g_ref.at[2 * x + y, p, pl.ds(c * HALF, HALF), :],
            send_sem=send_sem, recv_sem=recv_sem, device_id=(*chip, c), device_id_type=_MESH,
        )
        for p in range(n)
    ]


def _gather_forwards(n, g_ref, send_sem, recv_sem, chip, c, sibling):
    return [
        pltpu.make_async_remote_copy(
            src_ref=g_ref.at[2 * chip[0] + chip[1], p, pl.ds(c * HALF, HALF), :],
            dst_ref=g_ref.at[2 * chip[0] + chip[1], p, pl.ds(c * HALF, HALF), :],
            send_sem=send_sem, recv_sem=recv_sem, device_id=sibling, device_id_type=_MESH,
        )
        for p in range(n)
    ]


def _gather_all(own_ref, g_ref, send_sem, recv_sem, chip, half, c):
    return pltpu.make_async_remote_copy(
        src_ref=own_ref.at[:, pl.ds(c * HALF, HALF), :],
        dst_ref=g_ref.at[2 * chip[0] + chip[1], :, pl.ds(half * HALF, HALF), :],
        send_sem=send_sem, recv_sem=recv_sem, device_id=(*chip, c), device_id_type=_MESH,
    )


_HBM_SPEC = pl.BlockSpec(memory_space=pltpu.HBM)
_SEM_SPEC = pl.BlockSpec(memory_space=pltpu.SEMAPHORE)
_ANY_SPEC = pl.BlockSpec(memory_space=pl.ANY)
_VMEM_SPEC = pl.BlockSpec(memory_space=pltpu.VMEM)
_SPLIT_PARAMS = pltpu.CompilerParams(has_side_effects=pltpu.SideEffectType.DATAFLOW_SIDE_EFFECTING)
_TOKEN = jax.ShapeDtypeStruct((8, 128), F32)


def _in_hbm(a):
    return pltpu.with_memory_space_constraint(a, pltpu.HBM)


class _GatherBehind:
    def __init__(self, tag, own, then=None):
        self.tag, self.n, self.own, self.then = tag, own.shape[0], own, then

    def start(self, after):
        tag, own, n = self.tag, self.own, self.n
        x, y, _ = _place()
        g_init = lax.dynamic_update_slice(
            lax.empty((N_CHIPS,) + own.shape, own.dtype), own[None], (2 * x + y, 0, 0, 0)
        )

        def body(own_ref, g_ref, *rest):
            send_sems, recv_sems, _, _, token = rest[len(after):]
            x, y, c = _place()
            for j, chip in enumerate(_other_chips(x, y)):
                for cp in _gather_sends(n, own_ref, g_ref, send_sems.at[j], recv_sems.at[j], x, y, c, chip):
                    cp.start()
            token[...] = jnp.zeros_like(token)

        self.send1, self.recv1, self.own, self.g, self.token = pl.pallas_call(
            body,
            name=f"{tag}_start",
            out_shape=(
                pltpu.SemaphoreType.DMA((3,)), pltpu.SemaphoreType.DMA((3,)),
                pltpu.HBM(own.shape, own.dtype), pltpu.HBM(g_init.shape, g_init.dtype), _TOKEN,
            ),
            in_specs=(_HBM_SPEC, _HBM_SPEC) + (_ANY_SPEC,) * len(after),
            out_specs=(_SEM_SPEC, _SEM_SPEC, _HBM_SPEC, _HBM_SPEC, _VMEM_SPEC),
            input_output_aliases={0: 2, 1: 3},
            compiler_params=_SPLIT_PARAMS,
        )(_in_hbm(own), _in_hbm(g_init), *after)

    def mid(self, after):
        n = self.n

        def body(own_ref, g_ref, send1, recv1, after_ref, send2, recv2, g_out, token):
            x, y, c = _place()
            sibling = (x, y, 1 - c)
            chips = _other_chips(x, y)
            for j, chip in enumerate(chips):
                _gather_all(own_ref, g_ref, send1.at[j], recv1.at[j], chip, c, c).wait_recv()
                for cp in _gather_forwards(n, g_ref, send2.at[j], recv2.at[j], chip, c, sibling):
                    cp.start()
            for j, chip in enumerate(chips):
                _gather_all(own_ref, g_ref, send1.at[j], recv1.at[j], chip, c, c).wait_send()
            token[...] = jnp.zeros_like(token)

        self.send2, self.recv2, self.g, token = pl.pallas_call(
            body,
            name=f"{self.tag}_mid",
            out_shape=(
                pltpu.SemaphoreType.DMA((3,)), pltpu.SemaphoreType.DMA((3,)),
                pltpu.HBM(self.g.shape, self.g.dtype), _TOKEN,
            ),
            in_specs=(_HBM_SPEC, _HBM_SPEC, _SEM_SPEC, _SEM_SPEC, _ANY_SPEC),
            out_specs=(_SEM_SPEC, _SEM_SPEC, _HBM_SPEC, _VMEM_SPEC),
            input_output_aliases={1: 2},
            compiler_params=_SPLIT_PARAMS,
        )(self.own, self.g, self.send1, self.recv1, after)
        if self.then is None:
            return [token]
        self.then.start([token])
        return [token, self.then.token]

    def get(self, after):
        def body(g_ref, send2, recv2, after_ref, g_out):
            x, y, c = _place()
            for j, chip in enumerate(_other_chips(x, y)):
                slot_in = g_ref.at[2 * chip[0] + chip[1], :, pl.ds((1 - c) * HALF, HALF), :]
                slot_out = g_ref.at[2 * chip[0] + chip[1], :, pl.ds(c * HALF, HALF), :]
                cp = pltpu.make_async_remote_copy(
                    src_ref=slot_out, dst_ref=slot_in, send_sem=send2.at[j], recv_sem=recv2.at[j],
                    device_id=(x, y, 1 - c), device_id_type=_MESH,
                )
                cp.wait_send()
                cp.wait_recv()

        return pl.pallas_call(
            body,
            name=f"{self.tag}_wait",
            out_shape=pltpu.HBM(self.g.shape, self.g.dtype),
            in_specs=(_HBM_SPEC, _SEM_SPEC, _SEM_SPEC, _ANY_SPEC),
            out_specs=_HBM_SPEC,
            input_output_aliases={0: 0},
            compiler_params=_SPLIT_PARAMS,
        )(self.g, self.send2, self.recv2, after)


def _pair_sum(name, gb, land):
    def body(c_ref, gb_ref, land_ref, o_ref):
        o_ref[...] = (gb_ref[...].astype(F32) + land_ref[...].astype(F32)).astype(o_ref.dtype)

    core = lax.axis_index("c").astype(jnp.int32).reshape(1)
    return pl.pallas_call(
        body,
        name=name,
        grid_spec=pltpu.PrefetchScalarGridSpec(
            num_scalar_prefetch=1,
            grid=(N_CHIPS, gb.shape[1]),
            in_specs=[
                pl.BlockSpec((None, None, HALF, D_MODEL), lambda j, p, c: (j, p, c[0], 0)),
                pl.BlockSpec((None, None, HALF, D_MODEL), lambda j, p, c: (j, p, 0, 0)),
            ],
            out_specs=pl.BlockSpec((None, None, HALF, D_MODEL), lambda j, p, c: (j, p, 0, 0)),
        ),
        out_shape=_sds(land.shape, BF16),
        compiler_params=_cparams(("parallel", "parallel")),
    )(core, gb, land)


def _chip_sum_share(name, land):
    n = land.shape[1]

    def body(l_ref, r_ref, buf, send_sems, recv_sems, local_sems):
        p = pl.program_id(0)
        x, y, c = _place()
        acc = l_ref[0].astype(F32)
        for j in range(1, N_CHIPS):
            acc = acc + l_ref[j].astype(F32)
        buf[p] = acc

        def copies(q):
            mine = r_ref.at[q, pl.ds(c * HALF, HALF), :]
            theirs = r_ref.at[q, pl.ds((1 - c) * HALF, HALF), :]
            local = pltpu.make_async_copy(buf.at[q], mine, local_sems.at[q])
            out = pltpu.make_async_remote_copy(
                src_ref=buf.at[q], dst_ref=mine, send_sem=send_sems.at[q], recv_sem=recv_sems.at[q],
                device_id=(x, y, 1 - c), device_id_type=_MESH,
            )
            arrive = pltpu.make_async_remote_copy(
                src_ref=buf.at[q], dst_ref=theirs, send_sem=send_sems.at[q], recv_sem=recv_sems.at[q],
                device_id=(x, y, 1 - c), device_id_type=_MESH,
            )
            return local, out, arrive

        local, out, _ = copies(p)
        local.start()
        out.start()

        @pl.when(p == n - 1)
        def _():
            for q in range(n):
                local, out, arrive = copies(q)
                arrive.wait_recv()
                out.wait_send()
                local.wait()

    return pl.pallas_call(
        body,
        name=name,
        grid=(n,),
        in_specs=[pl.BlockSpec((N_CHIPS, None, HALF, D_MODEL), lambda p: (0, p, 0, 0))],
        out_specs=pl.BlockSpec(memory_space=pl.ANY),
        out_shape=_sds((n, F_SHARD, D_MODEL), F32),
        scratch_shapes=[
            pltpu.VMEM((n, HALF, D_MODEL), F32),
            pltpu.SemaphoreType.DMA((n,)), pltpu.SemaphoreType.DMA((n,)), pltpu.SemaphoreType.DMA((n,)),
        ],
        compiler_params=_cparams(("arbitrary",)),
    )(land)


class _ReduceBehind:
    def __init__(self, tag, gb, after=()):
        self.tag = tag
        n = gb.shape[1]
        land = lax.empty((N_CHIPS, n, HALF, D_MODEL), gb.dtype)

        def body(gb_ref, land_ref, *rest):
            send_sem, recv_sem, _, _, token = rest[len(after):]
            self._pair_copy(gb_ref, land_ref, send_sem, recv_sem).start()
            token[...] = jnp.zeros_like(token)

        self.send, self.recv, self.gb, self.land, self.token = pl.pallas_call(
            body,
            name=f"grads_pair_start_{tag}",
            out_shape=(
                pltpu.SemaphoreType.DMA((1,)), pltpu.SemaphoreType.DMA((1,)),
                pltpu.HBM(gb.shape, gb.dtype), pltpu.HBM(land.shape, land.dtype), _TOKEN,
            ),
            in_specs=(_HBM_SPEC, _HBM_SPEC) + (_ANY_SPEC,) * len(after),
            out_specs=(_SEM_SPEC, _SEM_SPEC, _HBM_SPEC, _HBM_SPEC, _VMEM_SPEC),
            input_output_aliases={0: 2, 1: 3},
            compiler_params=_SPLIT_PARAMS,
        )(_in_hbm(gb), _in_hbm(land), *after)

    @staticmethod
    def _pair_copy(gb_ref, land_ref, send_sem, recv_sem):
        x, y, c = _place()
        return pltpu.make_async_remote_copy(
            src_ref=gb_ref.at[:, :, pl.ds((1 - c) * HALF, HALF), :], dst_ref=land_ref,
            send_sem=send_sem.at[0], recv_sem=recv_sem.at[0], device_id=(x, y, 1 - c), device_id_type=_MESH,
        )

    @staticmethod
    def _chip_copies(p_ref, land_ref, send_sems, recv_sems):
        x, y, c = _place()
        me = 2 * x + y
        sends, arrivals = [], []
        for k, chip in enumerate(_other_chips(x, y)):
            them = 2 * chip[0] + chip[1]
            sends.append(pltpu.make_async_remote_copy(
                src_ref=p_ref.at[them], dst_ref=land_ref.at[me], send_sem=send_sems.at[k], recv_sem=recv_sems.at[k],
                device_id=(*chip, c), device_id_type=_MESH,
            ))
            arrivals.append(pltpu.make_async_remote_copy(
                src_ref=p_ref.at[me], dst_ref=land_ref.at[them], send_sem=send_sems.at[k], recv_sem=recv_sems.at[k],
                device_id=(*chip, c), device_id_type=_MESH,
            ))
        return sends, arrivals

    def mid(self, after):
        tag = self.tag

        def wait_body(gb_ref, land_ref, send_sem, recv_sem, after_ref, gb_out, land_out):
            cp = self._pair_copy(gb_ref, land_ref, send_sem, recv_sem)
            cp.wait_send()
            cp.wait_recv()

        gb, land = pl.pallas_call(
            wait_body,
            name=f"grads_pair_wait_{tag}",
            out_shape=(pltpu.HBM(self.gb.shape, self.gb.dtype), pltpu.HBM(self.land.shape, self.land.dtype)),
            in_specs=(_HBM_SPEC, _HBM_SPEC, _SEM_SPEC, _SEM_SPEC, _ANY_SPEC),
            out_specs=(_HBM_SPEC, _HBM_SPEC),
            input_output_aliases={0: 0, 1: 1},
            compiler_params=_SPLIT_PARAMS,
        )(self.gb, self.land, self.send, self.recv, after)
        part = _pair_sum(f"grads_pair_sum_{tag}", gb, land)

        x, y, _ = _place()
        start = (2 * x + y, 0, 0, 0)
        own = lax.dynamic_slice(part, start, (1,) + part.shape[1:])
        land2 = lax.dynamic_update_slice(lax.empty(part.shape, part.dtype), own, start)

        def start_body(p_ref, land_ref, send_sems, recv_sems, p_out, land_out, token):
            for cp in self._chip_copies(p_ref, land_ref, send_sems, recv_sems)[0]:
                cp.start()
            token[...] = jnp.zeros_like(token)

        self.send2, self.recv2, self.part, self.land2, token = pl.pallas_call(
            start_body,
            name=f"grads_chip_start_{tag}",
            out_shape=(
                pltpu.SemaphoreType.DMA((3,)), pltpu.SemaphoreType.DMA((3,)),
                pltpu.HBM(part.shape, part.dtype), pltpu.HBM(land2.shape, land2.dtype), _TOKEN,
            ),
            in_specs=(_HBM_SPEC, _HBM_SPEC),
            out_specs=(_SEM_SPEC, _SEM_SPEC, _HBM_SPEC, _HBM_SPEC, _VMEM_SPEC),
            input_output_aliases={0: 2, 1: 3},
            compiler_params=_SPLIT_PARAMS,
        )(_in_hbm(part), _in_hbm(land2))
        return [token]

    def get(self, after):
        n_after = len(after)

        def wait_body(p_ref, land_ref, send_sems, recv_sems, *rest):
            sends, arrivals = self._chip_copies(p_ref, land_ref, send_sems, recv_sems)
            for cp in arrivals:
                cp.wait_recv()
            for cp in sends:
                cp.wait_send()

        _, land2 = pl.pallas_call(
            wait_body,
            name=f"grads_chip_wait_{self.tag}",
            out_shape=(pltpu.HBM(self.part.shape, self.part.dtype), pltpu.HBM(self.land2.shape, self.land2.dtype)),
            in_specs=(_HBM_SPEC, _HBM_SPEC, _SEM_SPEC, _SEM_SPEC) + (_ANY_SPEC,) * n_after,
            out_specs=(_HBM_SPEC, _HBM_SPEC),
            input_output_aliases={0: 0, 1: 1},
            compiler_params=_SPLIT_PARAMS,
        )(self.part, self.land2, self.send2, self.recv2, *after)
        return _chip_sum_share(f"grads_chip_sum_share_{self.tag}", land2)


def _allreduce_small(blk):
    gathered = _allgather_small("allgather_small_grads", blk).reshape(N_DEV, PK_ROWS, 128)

    def body(g_ref, o_ref):
        acc = g_ref[0]
        for k in range(1, N_DEV):
            acc = acc + g_ref[k]
        o_ref[...] = acc

    total = pl.pallas_call(body, name="small_grads_sum", out_shape=_sds((PK_ROWS, 128), F32))(gathered)
    return gathered, total


def _adamw_math(w_, g_, m_, v_):
    m_new = ADAM_B1 * m_ + (1.0 - ADAM_B1) * g_
    v_new = ADAM_B2 * v_ + (1.0 - ADAM_B2) * (g_ * g_)
    m_hat = m_new / (1.0 - ADAM_B1 ** ADAM_STEP)
    v_hat = v_new / (1.0 - ADAM_B2 ** ADAM_STEP)
    delta = -ADAM_LR * (m_hat / (jnp.sqrt(v_hat) + ADAM_EPS) + ADAM_WD * w_)
    return delta, m_new, v_new


def _adamw(name, w, g, m, v):
    rows, cols = w.shape
    tm = rows
    while tm * cols * 4 > (1 << 20) and tm % 16 == 0:
        tm //= 2
    out = _sds(w.shape, F32)
    return _rowwise(name, _adamw_math, [w, g, m, v], [], [out, out, out], [], tm)


def _adamw_small(ws, gs, ms, vs):
    n = len(ws)
    shapes = [w.shape for w in ws]
    as2d = lambda a: a.reshape(1, -1) if a.ndim == 1 else a

    def body(*refs):
        ins, outs = refs[:4 * n], refs[4 * n:]
        for k in range(n):
            res = _adamw_math(*[ins[j * n + k][...] for j in range(4)])
            for j in range(3):
                outs[j * n + k][...] = res[j]

    args = [as2d(a) for group in (ws, gs, ms, vs) for a in group]
    out = pl.pallas_call(
        body, name="adamw_small", out_shape=[_sds(as2d(w).shape, F32) for w in ws] * 3, compiler_params=_cparams()
    )(*args)
    back = [o.reshape(shapes[i % n]) for i, o in enumerate(out)]
    return back[:n], back[n:2 * n], back[2 * n:]


def _pack_small(b_mod_like, n1, n2, n3, nf, qn, kvn, sinks, rel):
    parts = [
        b_mod_like.reshape(PK_MOD, 128),
        n1.reshape(8, 128), n2.reshape(8, 128), n3.reshape(8, 128), nf.reshape(8, 128),
        qn.reshape(2, 128), kvn.reshape(1, 128),
        jnp.pad(sinks.reshape(1, SWA_HEADS), ((0, 0), (0, 128 - SWA_HEADS))),
        rel.reshape(2, 128),
        jnp.zeros((2, 128), F32),
    ]
    return jnp.concatenate(parts, axis=0)


def _unpack_small(p):
    o = PK_MOD
    return (
        p[:o].reshape(1, N_MOD * D_MODEL),
        p[o:o + 8].reshape(1, D_MODEL), p[o + 8:o + 16].reshape(1, D_MODEL),
        p[o + 16:o + 24].reshape(1, D_MODEL), p[o + 24:o + 32].reshape(D_MODEL),
        p[o + 32:o + 34].reshape(1, MLA_Q_RANK), p[o + 34:o + 35].reshape(1, MLA_KV_RANK),
        p[o + 35:o + 36, :SWA_HEADS].reshape(1, SWA_HEADS),
        p[o + 36:o + 38].reshape(NUM_BUCKETS, SWA_HEADS),
    )


def kernel(x, c, w_mod, b_mod, norm_ffn1, ffn1_gate, ffn1_up, ffn1_down, norm_mix, w_in, q_norm, kv_norm, w_uq, w_ukv, sinks, w_o, norm_ffn2, ffn2_gate, ffn2_up, ffn2_down, rel_bias, norm_final, loss_target, m_w_mod, m_b_mod, m_norm_ffn1, m_ffn1_gate, m_ffn1_up, m_ffn1_down, m_norm_mix, m_w_in, m_q_norm, m_kv_norm, m_w_uq, m_w_ukv, m_sinks, m_w_o, m_norm_ffn2, m_ffn2_gate, m_ffn2_up, m_ffn2_down, m_rel_bias, m_norm_final, v_w_mod, v_b_mod, v_norm_ffn1, v_ffn1_gate, v_ffn1_up, v_ffn1_down, v_norm_mix, v_w_in, v_q_norm, v_kv_norm, v_w_uq, v_w_ukv, v_sinks, v_w_o, v_norm_ffn2, v_ffn2_gate, v_ffn2_up, v_ffn2_down, v_rel_bias, v_norm_final):
    ax, ay, ac = _place()
    chip = 2 * ax + ay
    dev = 2 * chip + ac
    n_mod_shard = N_MOD * D_MODEL // N_CHIPS

    def t16(w):
        return w[0].T.astype(BF16)

    rest = jnp.concatenate(
        [
            t16(w_in),
            w_o[0].astype(BF16),
            t16(w_uq).reshape(R_UQ, D_MODEL),
            t16(w_ukv).reshape(R_UKV, D_MODEL),
            jnp.zeros((F_SHARD - O_PAD, D_MODEL), BF16),
        ],
        axis=0,
    )
    own_gu1 = jnp.stack([t16(ffn1_gate), t16(ffn1_up)])
    own_down1 = ffn1_down[0].astype(BF16)[None]
    own_mix = rest[None]
    own_ffn2 = jnp.stack([t16(ffn2_gate), t16(ffn2_up), ffn2_down[0].astype(BF16)])

    (c_act,) = _rowwise(
        "silu_c", lambda v: v * _sigmoid(v), [c.reshape(8, 128)], [], [_sds((8, 128), F32)], [], 8
    )
    c_all = _allgather_small("allgather_c", c_act).reshape(N_DEV, D_MODEL)
    mod_cols = _mm(
        "mod_fwd", "nn", (1, n_mod_shard // 768, 1),
        c_all, (N_DEV, D_MODEL), lambda i, j, k: (0, 0),
        w_mod[0], (D_MODEL, 768), lambda i, j, k: (0, j),
        _sds((N_DEV, n_mod_shard), F32), (N_DEV, 768), lambda i, j, k: (0, j),
        (N_DEV, 768),
    )
    mod_all = _allgather_small("allgather_mod", mod_cols).reshape(N_CHIPS, 2, N_DEV, n_mod_shard)[:, 0]
    mod_all = mod_all.transpose(1, 0, 2).reshape(N_DEV, N_MOD * D_MODEL)
    mod = lax.dynamic_slice_in_dim(mod_all, dev, 1, axis=0) + b_mod

    norms = (norm_ffn1, norm_mix, norm_ffn2, norm_final.reshape(1, D_MODEL))

    ffn2_src = _GatherBehind("gather_ffn2", own_ffn2)
    mix_src = _GatherBehind("gather_mix", own_mix, then=ffn2_src)
    down1_src = _GatherBehind("gather_down1", own_down1, then=mix_src)
    gu1_src = _GatherBehind("gather_gu1", own_gu1, then=down1_src)
    gu1_src.start([mod_all])

    reducing, red = {}, {}

    def begin(tag, gb, after):
        reducing[tag] = _ReduceBehind(tag, gb, after=after)
        return [reducing[tag].token]

    def ffn2_gu_done(gb):
        return begin("ffn2_gu", gb, reducing["ffn2_down"].mid(gb))

    def mix_done(dx1, gb_rest):
        red["ffn2_down"] = reducing["ffn2_down"].get([dx1])
        red["ffn2_gu"] = reducing["ffn2_gu"].get([red["ffn2_down"]])
        return begin("rest", gb_rest, [red["ffn2_gu"]])

    def ffn1_gu_done(gb):
        red["rest"] = reducing["rest"].get([gb])
        begin("ffn1_gu", gb, reducing["ffn1_down"].mid(red["rest"]))
        return reducing["ffn1_gu"].mid(reducing["ffn1_gu"].token)

    loss_part, grad_x, _, dmod, small = _device_step(
        x[0], loss_target[0], mod, gu1_src, down1_src, mix_src, ffn2_src, norms, q_norm, kv_norm, sinks, rel_bias,
        hooks2=_BwdHooks(after_wdown=lambda gb: begin("ffn2_down", gb, ()), after_wgu=ffn2_gu_done),
        hooks_mix=_BwdHooks(after_gate=lambda dz: reducing["ffn2_gu"].mid(dz)),
        mix_done=mix_done,
        hooks1=_BwdHooks(
            after_swiglu=lambda dab3: reducing["rest"].mid(dab3),
            after_wdown=lambda gb: begin("ffn1_down", gb, ()),
            after_wgu=ffn1_gu_done,
        ),
    )
    loss = lax.psum(loss_part, ("x", "y", "c"))

    gathered, total = _allreduce_small(_pack_small(dmod, *small))
    (g_b_mod, g_n1, g_n2, g_n3, g_nf, g_qn, g_kvn, g_sinks, g_rel) = _unpack_small(total)
    dmod_all = gathered[:, :PK_MOD].reshape(N_DEV, N_MOD * D_MODEL)
    dmod_cols = lax.dynamic_slice_in_dim(dmod_all, chip * n_mod_shard, n_mod_shard, axis=1)
    g_w_mod = _mm(
        "mod_bwd", "tn", (1, n_mod_shard // 768, 1),
        c_all, (N_DEV, D_MODEL), lambda i, j, k: (0, 0),
        dmod_cols, (N_DEV, 768), lambda i, j, k: (0, j),
        _sds((D_MODEL, n_mod_shard), F32), (D_MODEL, 768), lambda i, j, k: (0, j),
        (D_MODEL, 768),
    )

    r_rest = red["rest"][0]
    transposed = {"ffn1_gate", "ffn1_up", "ffn2_gate", "ffn2_up", "w_in", "w_uq", "w_ukv"}
    g_big = {
        "ffn2_gate": red["ffn2_gu"][0], "ffn2_up": red["ffn2_gu"][1], "ffn2_down": red["ffn2_down"][0],
        "w_in": r_rest[O_IN:O_IN + R_IN],
        "w_o": r_rest[O_O:O_O + R_O],
        "w_uq": r_rest[O_UQ:O_UQ + R_UQ].reshape(MLA_QK, MLA_Q_RANK),
        "w_ukv": r_rest[O_UKV:O_UKV + R_UKV].reshape(MLA_NOPE + MLA_V, MLA_KV_RANK),
        "w_mod": g_w_mod,
    }
    w_big = {
        "ffn2_gate": (ffn2_gate, m_ffn2_gate, v_ffn2_gate),
        "ffn2_up": (ffn2_up, m_ffn2_up, v_ffn2_up), "ffn2_down": (ffn2_down, m_ffn2_down, v_ffn2_down),
        "w_in": (w_in, m_w_in, v_w_in), "w_o": (w_o, m_w_o, v_w_o), "w_uq": (w_uq, m_w_uq, v_w_uq),
        "w_ukv": (w_ukv, m_w_ukv, v_w_ukv), "w_mod": (w_mod, m_w_mod, v_w_mod),
    }
    grads, deltas, new_m, new_v = {}, {}, {}, {}

    def update(names):
        for nm in names:
            w, m, v = w_big[nm]
            if nm in transposed:
                outs = _adamw(f"adamw_{nm}", w[0].T, g_big[nm], m[0].T, v[0].T)
                g_, d_, m_, v_ = [a.T for a in (g_big[nm],) + tuple(outs)]
            else:
                g_, (d_, m_, v_) = g_big[nm], _adamw(f"adamw_{nm}", w[0], g_big[nm], m[0], v[0])
            grads[nm], deltas[nm], new_m[nm], new_v[nm] = g_[None], d_[None], m_[None], v_[None]

    update(list(w_big))
    red1_down = reducing["ffn1_down"].get([deltas[nm] for nm in w_big])
    red1_gu = reducing["ffn1_gu"].get([red1_down])
    g_big.update({"ffn1_gate": red1_gu[0], "ffn1_up": red1_gu[1], "ffn1_down": red1_down[0]})
    w_big.update({
        "ffn1_gate": (ffn1_gate, m_ffn1_gate, v_ffn1_gate), "ffn1_up": (ffn1_up, m_ffn1_up, v_ffn1_up),
        "ffn1_down": (ffn1_down, m_ffn1_down, v_ffn1_down),
    })
    update(["ffn1_gate", "ffn1_up", "ffn1_down"])

    small_names = ["b_mod", "norm_ffn1", "norm_mix", "norm_ffn2", "norm_final", "q_norm", "kv_norm", "sinks", "rel_bias"]
    w_small = (b_mod, norm_ffn1, norm_mix, norm_ffn2, norm_final, q_norm, kv_norm, sinks, rel_bias)
    m_small = (m_b_mod, m_norm_ffn1, m_norm_mix, m_norm_ffn2, m_norm_final, m_q_norm, m_kv_norm, m_sinks, m_rel_bias)
    v_small = (v_b_mod, v_norm_ffn1, v_norm_mix, v_norm_ffn2, v_norm_final, v_q_norm, v_kv_norm, v_sinks, v_rel_bias)
    g_small = (g_b_mod, g_n1, g_n2, g_n3, g_nf, g_qn, g_kvn, g_sinks, g_rel)
    d_s, m_s, v_s = _adamw_small(w_small, g_small, m_small, v_small)
    for nm, g_, d_, m_, v_ in zip(small_names, g_small, d_s, m_s, v_s):
        grads[nm], deltas[nm], new_m[nm], new_v[nm] = g_, d_, m_, v_

    order = ["w_mod", "b_mod", "norm_ffn1", "ffn1_gate", "ffn1_up", "ffn1_down", "norm_mix", "w_in", "q_norm", "kv_norm",
             "w_uq", "w_ukv", "sinks", "w_o", "norm_ffn2", "ffn2_gate", "ffn2_up", "ffn2_down", "rel_bias", "norm_final"]
    return (loss, grad_x[None], *[grads[n] for n in order], *[deltas[n] for n in order],
            *[new_m[n] for n in order], *[new_v[n] for n in order])
```

```python
import functools
import math

import jax
import jax.numpy as jnp
import numpy as np
from jax import lax
from jax.experimental import pallas as pl
from jax.experimental.pallas import tpu as pltpu

F32, BF16 = jnp.float32, jnp.bfloat16

D_MODEL = 1024
D_FF = 2816
EPS = 1e-6
N_MOD = 9
SWA_HEADS, SWA_KV_HEADS, SWA_HEAD_DIM, WINDOW = 8, 2, 64, 128
SWA_GROUP = SWA_HEADS // SWA_KV_HEADS
MLA_HEADS, MLA_Q_RANK, MLA_KV_RANK, MLA_NOPE, MLA_ROPE, MLA_V = 4, 256, 128, 128, 64, 128
MLA_QK = MLA_NOPE + MLA_ROPE
ROPE_THETA = 10000.0
NUM_BUCKETS, MAX_DISTANCE = 32, 128
D_IN = 1216
N_SWA_COLS = 768
N_MLA_COLS = 512

ADAM_LR, ADAM_B1, ADAM_B2, ADAM_EPS, ADAM_WD, ADAM_STEP = 0.001, 0.9, 0.999, 1e-08, 0.01, 10

N_CHIPS = 4
N_DEV = 8
F_SHARD = D_FF // N_CHIPS
HALF = F_SHARD // 2
R_IN, R_O, R_UQ, R_UKV = 304, 256, 48, 32
O_IN, O_O, O_UQ, O_UKV, O_PAD = 0, 304, 560, 608, 640

PK_MOD = 72
PK_ROWS = 112
VMEM_LIMIT = 56 << 20
ROW_TILE = 2048
ELEM_TILE = 512
MOD_TILE = 768

_DN = {
    "nn": (((1,), (0,)), ((), ())),
    "nt": (((1,), (1,)), ((), ())),
    "tn": (((0,), (0,)), ((), ())),
}


def _sds(shape, dtype):
    return jax.ShapeDtypeStruct(tuple(shape), dtype)


def _cparams(sem=None):
    kw = {"vmem_limit_bytes": VMEM_LIMIT}
    if sem is not None:
        kw["dimension_semantics"] = sem
    return pltpu.CompilerParams(**kw)


def _dot(a, b, dims):
    return lax.dot_general(a.astype(BF16), b.astype(BF16), _DN[dims], preferred_element_type=F32)


def _mm(name, dims, grid, a, a_blk, a_idx, b, b_blk, b_idx, out, out_blk, out_idx, acc_shape, alias=None, deps=()):
    nk = grid[2]

    def body(*refs):
        a_ref, b_ref = refs[:2]
        o_ref, acc_ref = refs[-2:]
        part = _dot(a_ref[...], b_ref[...], dims)
        if nk == 1:
            o_ref[...] = part.astype(o_ref.dtype)
            return
        k = pl.program_id(2)

        @pl.when(k == 0)
        def _():
            acc_ref[...] = part

        @pl.when((k > 0) & (k < nk - 1))
        def _():
            acc_ref[...] += part

        @pl.when(k == nk - 1)
        def _():
            o_ref[...] = (acc_ref[...] + part).astype(o_ref.dtype)

    in_specs = [pl.BlockSpec(a_blk, a_idx), pl.BlockSpec(b_blk, b_idx)]
    args = [a, b]
    kw = {}
    if alias is not None:
        in_specs.append(pl.BlockSpec(memory_space=pl.ANY))
        args.append(alias)
        kw["input_output_aliases"] = {2: 0}
    in_specs += [pl.BlockSpec(memory_space=pl.ANY)] * len(deps)
    args += list(deps)
    return pl.pallas_call(
        body,
        name=name,
        grid=grid,
        in_specs=in_specs,
        out_specs=pl.BlockSpec(out_blk, out_idx),
        out_shape=out,
        scratch_shapes=[pltpu.VMEM(acc_shape if nk > 1 else (8, 128), F32)],
        compiler_params=_cparams(("parallel", "parallel", "arbitrary")),
        **kw,
    )(*args)


def _rowwise(name, fn, tiled, full, out_tiled, out_red, tm, deps=()):
    rows = tiled[0].shape[-2]
    tm = min(tm, rows)
    grid = (rows // tm,)
    n_in = len(tiled) + len(full)
    n_t = len(out_tiled)
    n_dep = len(deps)

    def tspec(shape):
        nd = len(shape)
        return pl.BlockSpec(tuple(shape[:-2]) + (tm, shape[-1]), lambda i, nd=nd: (0,) * (nd - 2) + (i, 0))

    def fspec(shape):
        nd = len(shape)
        return pl.BlockSpec(tuple(shape), lambda i, nd=nd: (0,) * nd)

    def body(*refs):
        outs = fn(*[r[...] for r in refs[:n_in]])
        if not isinstance(outs, (tuple, list)):
            outs = (outs,)
        out_refs = refs[n_in + n_dep:]
        for r, v in zip(out_refs[:n_t], outs[:n_t]):
            r[...] = v.astype(r.dtype)
        red_refs = out_refs[n_t:]
        if red_refs:
            @pl.when(pl.program_id(0) == 0)
            def _():
                for r in red_refs:
                    r[...] = jnp.zeros_like(r)

            for r, v in zip(red_refs, outs[n_t:]):
                r[...] += v.astype(r.dtype)

    res = pl.pallas_call(
        body,
        name=name,
        grid=grid,
        in_specs=[tspec(a.shape) for a in tiled] + [fspec(a.shape) for a in full]
        + [pl.BlockSpec(memory_space=pl.ANY)] * n_dep,
        out_specs=[tspec(o.shape) for o in out_tiled] + [fspec(o.shape) for o in out_red],
        out_shape=list(out_tiled) + list(out_red),
        compiler_params=_cparams(("arbitrary",) if out_red else ("parallel",)),
    )(*tiled, *full, *deps)
    return res


def _sum0(v):
    return jnp.sum(v, axis=0, keepdims=True)


def _sigmoid(v):
    return 1.0 / (1.0 + jnp.exp(-v))


def _rms(x):
    r = lax.rsqrt(jnp.mean(x * x, axis=-1, keepdims=True) + EPS)
    return x * r, r


def _rms_bwd(u, xr, r):
    return r * (u - xr * jnp.mean(u * xr, axis=-1, keepdims=True))


class _Ready:
    def __init__(self, g):
        self.g = g

    def mid(self, after):
        return []

    def get(self, after):
        return self.g


def _normmod(x_, gamma_, sc_, sh_):
    return _rms(x_)[0] * gamma_ * (1.0 + sc_) + sh_


def _row_blocks(tm, size=256):
    size = min(size, tm)
    return [slice(r, r + size) for r in range(0, tm, size)]


def _loss_head(x_, t_, g_):
    xr, r = _rms(x_)
    err = xr * g_ - t_
    dy = err * (1.0 / D_MODEL)
    return _rms_bwd(dy * g_, xr, r), _sum0(err * err), _sum0(dy * xr)


def _ffn_fwd(tag, x, gamma, sh, sc, gate, gu_src, base, down_src, down_idx, mid_after, h_pre=None,
             next_norm=None, head=None):
    s_len = x.shape[0]
    if h_pre is None:
        (h,) = _rowwise(
            f"{tag}_normmod", _normmod, [x], [gamma, sc, sh], [_sds((s_len, D_MODEL), BF16)], [], ELEM_TILE
        )
        gdeps = gu_src.mid(h)
    else:
        h, gdeps = h_pre, gu_src.mid(mid_after)
    g4 = gu_src.get(h)

    tm = min(ROW_TILE, s_len)
    def gate_up(h_ref, wg_ref, wu_ref, *rest):
        ab_ref, s_ref = rest[-2:]
        wg, wu = wg_ref[...], wu_ref[...]
        for rows in _row_blocks(tm):
            hv = h_ref[rows, :]
            a = _dot(hv, wg, "nt")
            b = _dot(hv, wu, "nt")
            ab_ref[0, rows, :] = a.astype(ab_ref.dtype)
            ab_ref[1, rows, :] = b.astype(ab_ref.dtype)
            s_ref[rows, :] = (a * _sigmoid(a) * b).astype(s_ref.dtype)

    ab4, s3 = pl.pallas_call(
        gate_up,
        name=f"{tag}_gate_up",
        grid=(s_len // tm, N_CHIPS),
        in_specs=[
            pl.BlockSpec((tm, D_MODEL), lambda i, c: (i, 0)),
            pl.BlockSpec((None, None, F_SHARD, D_MODEL), lambda i, c: (c, base, 0, 0)),
            pl.BlockSpec((None, None, F_SHARD, D_MODEL), lambda i, c: (c, base + 1, 0, 0)),
        ] + [pl.BlockSpec(memory_space=pl.ANY)] * len(gdeps),
        out_specs=[
            pl.BlockSpec((None, 2, tm, F_SHARD), lambda i, c: (c, 0, i, 0)),
            pl.BlockSpec((None, tm, F_SHARD), lambda i, c: (c, i, 0)),
        ],
        out_shape=[_sds((N_CHIPS, 2, s_len, F_SHARD), BF16), _sds((N_CHIPS, s_len, F_SHARD), BF16)],
        compiler_params=_cparams(("parallel", "parallel")),
    )(h, g4, g4, *gdeps)
    if down_src is None:
        g_down, ddeps = g4, ()
    else:
        ddeps = down_src.mid(ab4)
        g_down = down_src.get(s3)

    y = _mm(
        f"{tag}_down", "nn", (s_len // tm, 1, N_CHIPS),
        s3, (None, tm, F_SHARD), lambda i, j, k: (k, i, 0),
        g_down, (None, None, F_SHARD, D_MODEL), lambda i, j, k: (k, down_idx, 0, 0),
        _sds((s_len, D_MODEL), F32), (tm, D_MODEL), lambda i, j, k: (i, 0),
        (tm, D_MODEL), deps=ddeps,
    )

    saved = (g4, base, g_down, down_idx, h, ab4, s3, y)
    act = _sds((s_len, D_MODEL), F32)
    if head is not None:
        target, norm_final = head
        vec = _sds((1, D_MODEL), F32)
        out = _rowwise(
            f"{tag}_residual_head", lambda x_, y_, t_, g_, nf_: _loss_head(x_ + 0.5 * g_ * y_, t_, nf_),
            [x, y, target], [gate, norm_final], [act], [vec, vec], ELEM_TILE,
        )
    elif next_norm is not None:
        def residual_norm(x_, y_, g_, gamma_, sc_, sh_):
            x_out = x_ + 0.5 * g_ * y_
            return x_out, _normmod(x_out, gamma_, sc_, sh_)

        out = _rowwise(
            f"{tag}_residual_norm", residual_norm, [x, y], [gate] + list(next_norm),
            [act, _sds((s_len, D_MODEL), BF16)], [], ELEM_TILE,
        )
    else:
        out = _rowwise(f"{tag}_residual", lambda x_, y_, g_: x_ + 0.5 * g_ * y_, [x, y], [gate], [act], [], ELEM_TILE)
    return out, saved


def _normmod_bwd_call(name, dh_parts, x, dxo, gamma, sc, deps=()):
    s_len = x.shape[0]
    n_parts = len(dh_parts)

    def fn(*vals):
        dh = vals[0]
        for extra in vals[1:n_parts]:
            dh = dh + extra
        x_, dxo_, gamma_, sc_ = vals[n_parts:]
        xr, r = _rms(x_)
        n = xr * gamma_
        dn = dh * (1.0 + sc_)
        dx = dxo_ + _rms_bwd(dn * gamma_, xr, r)
        return dx, _sum0(dh * n), _sum0(dh), _sum0(dn * xr)

    vec = _sds((1, D_MODEL), F32)
    return _rowwise(
        name, fn, list(dh_parts) + [x, dxo], [gamma, sc], [_sds((s_len, D_MODEL), F32)], [vec, vec, vec], ELEM_TILE, deps=deps
    )


class _BwdHooks:
    def __init__(self, after_gate=None, after_swiglu=None, after_wdown=None, after_wgu=None, after_dh=None):
        def nothing(_):
            return []

        self.after_gate = after_gate or nothing
        self.after_swiglu = after_swiglu or nothing
        self.after_wdown = after_wdown or nothing
        self.after_wgu = after_wgu or nothing
        self.after_dh = after_dh or nothing


def _ffn_bwd(tag, dxo, x, gamma, sc, gate, saved, hooks, deps=()):
    g4, base, g_down, down_idx, h, ab4, s3, y = saved
    s_len = x.shape[0]
    vec = _sds((1, D_MODEL), F32)

    dy, dgate = _rowwise(
        f"{tag}_bwd_gate", lambda dxo_, y_, g_: (0.5 * g_ * dxo_, _sum0(0.5 * y_ * dxo_)),
        [dxo, y], [gate], [_sds((s_len, D_MODEL), BF16)], [vec], ELEM_TILE, deps=deps,
    )

    tm = min(ROW_TILE, s_len)

    def d_gate_up(dy_ref, wd_ref, ab_ref, o_ref):
        wd = wd_ref[...]
        for rows in _row_blocks(tm):
            ds = _dot(dy_ref[rows, :], wd, "nt")
            a = ab_ref[0, rows, :].astype(F32)
            b = ab_ref[1, rows, :].astype(F32)
            sig = _sigmoid(a)
            o_ref[0, rows, :] = (ds * b * sig * (1.0 + a * (1.0 - sig))).astype(o_ref.dtype)
            o_ref[1, rows, :] = (ds * a * sig).astype(o_ref.dtype)

    ab_spec = pl.BlockSpec((None, 2, tm, F_SHARD), lambda i, c: (c, 0, i, 0))
    dab4 = pl.pallas_call(
        d_gate_up,
        name=f"{tag}_bwd_dgate_up",
        grid=(s_len // tm, N_CHIPS),
        in_specs=[
            pl.BlockSpec((tm, D_MODEL), lambda i, c: (i, 0)),
            pl.BlockSpec((None, None, F_SHARD, D_MODEL), lambda i, c: (c, down_idx, 0, 0)),
            ab_spec,
        ],
        out_specs=ab_spec,
        out_shape=_sds(ab4.shape, BF16),
        compiler_params=_cparams(("parallel", "parallel")),
    )(dy, g_down, ab4)

    tk = tm
    gb_down = _mm(
        f"{tag}_bwd_wdown", "tn", (N_CHIPS, 1, s_len // tk),
        s3, (None, tk, F_SHARD), lambda i, j, k: (i, k, 0),
        dy, (tk, D_MODEL), lambda i, j, k: (k, 0),
        _sds((N_CHIPS, 1, F_SHARD, D_MODEL), BF16), (None, None, F_SHARD, D_MODEL), lambda i, j, k: (i, 0, 0, 0),
        (F_SHARD, D_MODEL), deps=hooks.after_swiglu(dab4),
    )
    gb_gu = _mm(
        f"{tag}_bwd_wgu", "tn", (2 * N_CHIPS, 1, s_len // tk),
        dab4, (None, None, tk, F_SHARD), lambda i, j, k: (i % N_CHIPS, i // N_CHIPS, k, 0),
        h, (tk, D_MODEL), lambda i, j, k: (k, 0),
        _sds((N_CHIPS, 2, F_SHARD, D_MODEL), BF16), (None, None, F_SHARD, D_MODEL),
        lambda i, j, k: (i % N_CHIPS, i // N_CHIPS, 0, 0),
        (F_SHARD, D_MODEL), deps=hooks.after_wdown(gb_down),
    )
    assert base % 2 == 0
    dh_deps = hooks.after_wgu(gb_gu)

    def d_h(dab_ref, w_ref, *rest):
        o_ref, acc_ref = rest[-2:]
        c = pl.program_id(1)
        part = _dot(dab_ref[0], w_ref[0], "nn") + _dot(dab_ref[1], w_ref[1], "nn")

        @pl.when(c == 0)
        def _():
            acc_ref[...] = part

        @pl.when((c > 0) & (c < N_CHIPS - 1))
        def _():
            acc_ref[...] += part

        @pl.when(c == N_CHIPS - 1)
        def _():
            o_ref[...] = acc_ref[...] + part

    dh = pl.pallas_call(
        d_h,
        name=f"{tag}_bwd_dh",
        grid=(s_len // tm, N_CHIPS),
        in_specs=[
            pl.BlockSpec((None, 2, tm, F_SHARD), lambda i, c: (c, 0, i, 0)),
            pl.BlockSpec((None, 2, F_SHARD, D_MODEL), lambda i, c: (c, base // 2, 0, 0)),
        ] + [pl.BlockSpec(memory_space=pl.ANY)] * len(dh_deps),
        out_specs=pl.BlockSpec((tm, D_MODEL), lambda i, c: (i, 0)),
        out_shape=_sds((s_len, D_MODEL), F32),
        scratch_shapes=[pltpu.VMEM((tm, D_MODEL), F32)],
        compiler_params=_cparams(("parallel", "arbitrary")),
    )(dab4, g4, *dh_deps)
    dx, dsc, dsh, dgamma = _normmod_bwd_call(
        f"{tag}_bwd_normmod", [dh], x, dxo, gamma, sc, deps=hooks.after_dh(dh)
    )
    return dx, (gb_down, gb_gu), (dsh, dsc, dgate, dgamma)


def _bucket_map():
    qi = np.arange(WINDOW)[:, None]
    kj = np.arange(2 * WINDOW)[None, :]
    dist = qi + WINDOW - kj
    band = (dist >= 0) & (dist < WINDOW)
    max_exact = NUM_BUCKETS // 2
    n = np.maximum(dist, 0)
    large = []
    for dt in (np.float32, np.float64):
        nf = np.maximum(n, 1).astype(dt)
        val = np.log(nf / dt(max_exact)) / dt(math.log(MAX_DISTANCE / max_exact)) * dt(NUM_BUCKETS - max_exact)
        large.append(np.minimum(max_exact + val.astype(np.int32), NUM_BUCKETS - 1))
    assert np.array_equal(large[0], large[1])
    bucket = np.where(n < max_exact, n, large[0])
    return np.where(band, bucket, -1).astype(np.int32).reshape(1, -1)


def _bias_expand(rel_bias_t, bkt):
    n = bkt.shape[1]

    def body(rb_ref, bkt_ref, o_ref):
        onehot = (lax.broadcasted_iota(jnp.int32, (NUM_BUCKETS, n), 0) == bkt_ref[...]).astype(F32)
        o_ref[...] = lax.dot_general(
            rb_ref[...], onehot, _DN["nn"], precision=lax.Precision.HIGHEST, preferred_element_type=F32
        )

    return pl.pallas_call(
        body, name="bias_expand", out_shape=_sds((SWA_HEADS, n), F32), compiler_params=_cparams()
    )(rel_bias_t, bkt)


def _bias_reduce(dbias_flat, bkt):
    n = bkt.shape[1]

    def body(db_ref, bkt_ref, o_ref):
        onehot = (lax.broadcasted_iota(jnp.int32, (NUM_BUCKETS, n), 0) == bkt_ref[...]).astype(F32)
        o_ref[...] = lax.dot_general(
            db_ref[...], onehot, _DN["nt"], precision=lax.Precision.HIGHEST, preferred_element_type=F32
        )

    return pl.pallas_call(
        body, name="bias_reduce", out_shape=_sds((SWA_HEADS, NUM_BUCKETS), F32), compiler_params=_cparams()
    )(dbias_flat, bkt)


_SWA_SCALE = SWA_HEAD_DIM ** -0.5
_NEG = -1e30


_SWA_PAIR = 2


def _swa_in_specs():
    w = WINDOW
    n_q = SWA_HEADS * SWA_HEAD_DIM
    n_kv = 2 * SWA_KV_HEADS * SWA_HEAD_DIM
    prev = lambda m: jnp.maximum(_SWA_PAIR * m - 1, 0)
    return [
        pl.BlockSpec((_SWA_PAIR * w, n_q), lambda m: (m, 0)),
        pl.BlockSpec((w, n_kv), lambda m: (prev(m), n_q // n_kv)),
        pl.BlockSpec((_SWA_PAIR * w, n_kv), lambda m: (m, n_q // n_kv)),
        pl.BlockSpec((SWA_HEADS, w, 2 * w), lambda m: (0, 0, 0)),
        pl.BlockSpec((SWA_HEADS, w, 1), lambda m: (0, 0, 0)),
    ]


def _head_cols(v, first, count):
    hd = SWA_HEAD_DIM
    return jnp.stack([v[:, (first + i) * hd:(first + i + 1) * hd] for i in range(count)])


def _swa_blocks(q_ref, kvp_ref, kvc_ref, m):
    g, w, hd = SWA_GROUP, WINDOW, SWA_HEAD_DIM
    kv_all = jnp.concatenate([kvp_ref[...], kvc_ref[...]], axis=0).astype(F32)
    blocks = []
    for sub in range(_SWA_PAIR):
        rows = slice(sub * w, (sub + 1) * w)
        q = q_ref[rows, :].astype(F32)
        kv = kv_all[sub * w:(sub + 2) * w]
        heads = []
        for j in range(SWA_KV_HEADS):
            qj = _head_cols(q, g * j, g).reshape(g * w, hd)
            heads.append((qj, _head_cols(kv, j, 1)[0], _head_cols(kv, SWA_KV_HEADS + j, 1)[0]))
        blocks.append((_SWA_PAIR * m + sub, rows, heads))
    return blocks


def _swa_scores(q, kk, bias, n):
    g, w = SWA_GROUP, WINDOW
    s = (_dot(q, kk, "nt") * _SWA_SCALE).reshape(g, w, 2 * w) + bias
    qi = lax.broadcasted_iota(jnp.int32, (w, 2 * w), 0)
    kj = lax.broadcasted_iota(jnp.int32, (w, 2 * w), 1)
    dist = qi + w - kj
    valid = ((dist >= 0) & (dist < w) & ((n > 0) | (kj >= w)))[None]
    return jnp.where(valid, s, _NEG), valid


def _swa_fwd(swa2, bias, sinkb):
    s_len = swa2.shape[0]
    g, w, hd = SWA_GROUP, WINDOW, SWA_HEAD_DIM

    def body(q_ref, kvp_ref, kvc_ref, bias_ref, sink_ref, o_ref, lse_ref):
        for n, rows, heads in _swa_blocks(q_ref, kvp_ref, kvc_ref, pl.program_id(0)):
            outs = []
            for j, (q, kk, vv) in enumerate(heads):
                hs = slice(g * j, g * (j + 1))
                s, valid = _swa_scores(q, kk, bias_ref[hs], n)
                sink = sink_ref[hs]
                m = jnp.maximum(jnp.max(s, axis=-1, keepdims=True), sink)
                p = jnp.where(valid, jnp.exp(s - m), 0.0)
                l = jnp.sum(p, axis=-1, keepdims=True) + jnp.exp(sink - m)
                o = _dot(p.reshape(g * w, 2 * w), vv, "nn").reshape(g, w, hd) / l
                outs += [o[i] for i in range(g)]
                lse_ref[hs, rows, :] = m + jnp.log(l)
            o_ref[rows, :] = jnp.concatenate(outs, axis=-1).astype(o_ref.dtype)

    n_q = SWA_HEADS * hd
    return pl.pallas_call(
        body,
        name="swa_fwd",
        grid=(s_len // (_SWA_PAIR * w),),
        in_specs=_swa_in_specs(),
        out_specs=[
            pl.BlockSpec((_SWA_PAIR * w, n_q), lambda m: (m, 0)),
            pl.BlockSpec((SWA_HEADS, _SWA_PAIR * w, 1), lambda m: (0, m, 0)),
        ],
        out_shape=[_sds((s_len, n_q), BF16), _sds((SWA_HEADS, s_len, 1), F32)],
        compiler_params=_cparams(("parallel",)),
    )(swa2, swa2, swa2, bias, sinkb)


def _swa_bwd(swa2, bias, sinkb, cat, dcat, lse):
    s_len = swa2.shape[0]
    g, w, hd = SWA_GROUP, WINDOW, SWA_HEAD_DIM

    def body(q_ref, kvp_ref, kvc_ref, bias_ref, sink_ref, o_ref, do_ref, lse_ref,
             dq_ref, dkva_ref, dkvb_ref, dbias_ref, dsink_ref):
        step = pl.program_id(0)

        @pl.when(step == 0)
        def _():
            dbias_ref[...] = jnp.zeros_like(dbias_ref)
            dsink_ref[...] = jnp.zeros_like(dsink_ref)

        for n, rows, heads in _swa_blocks(q_ref, kvp_ref, kvc_ref, step):
            o_all = o_ref[rows, :].astype(F32)
            do_all = do_ref[rows, :].astype(F32)
            dqs, dks, dvs = [], [], []
            for j, (q, kk, vv) in enumerate(heads):
                hs = slice(g * j, g * (j + 1))
                s, valid = _swa_scores(q, kk, bias_ref[hs], n)
                lse_v = lse_ref[hs, rows, :]
                p = jnp.where(valid, jnp.exp(s - lse_v), 0.0)
                do = _head_cols(do_all, g * j, g)
                delta = jnp.sum(do * _head_cols(o_all, g * j, g), axis=-1, keepdims=True)
                do2 = do.reshape(g * w, hd)
                dp = _dot(do2, vv, "nt").reshape(g, w, 2 * w)
                ds = p * (dp - delta)
                dbias_ref[hs] += ds
                dsink_ref[hs] -= jnp.exp(sink_ref[hs] - lse_v) * delta
                ds2 = ds.reshape(g * w, 2 * w)
                dq = (_dot(ds2, kk, "nn") * _SWA_SCALE).reshape(g, w, hd)
                dqs += [dq[i] for i in range(g)]
                dks.append(_dot(ds2, q, "tn") * _SWA_SCALE)
                dvs.append(_dot(p.reshape(g * w, 2 * w), do2, "tn"))
            dq_ref[rows, :] = jnp.concatenate(dqs, axis=-1).astype(dq_ref.dtype)
            dkv = jnp.concatenate(dks + dvs, axis=-1)
            dkvb_ref[rows, :] = dkv[:w]
            dkva_ref[rows, :] = dkv[w:]

    n_q = SWA_HEADS * hd
    n_kv = 2 * SWA_KV_HEADS * hd
    q_spec = pl.BlockSpec((_SWA_PAIR * w, n_q), lambda m: (m, 0))
    kv_spec = pl.BlockSpec((_SWA_PAIR * w, n_kv), lambda m: (m, 0))
    return pl.pallas_call(
        body,
        name="swa_bwd",
        grid=(s_len // (_SWA_PAIR * w),),
        in_specs=_swa_in_specs() + [q_spec, q_spec, pl.BlockSpec((SWA_HEADS, _SWA_PAIR * w, 1), lambda m: (0, m, 0))],
        out_specs=[
            q_spec, kv_spec, kv_spec,
            pl.BlockSpec((SWA_HEADS, w, 2 * w), lambda n: (0, 0, 0)),
            pl.BlockSpec((SWA_HEADS, w, 1), lambda n: (0, 0, 0)),
        ],
        out_shape=[
            _sds((s_len, n_q), BF16), _sds((s_len, n_kv), F32), _sds((s_len, n_kv), F32),
            _sds((SWA_HEADS, w, 2 * w), F32), _sds((SWA_HEADS, w, 1), F32),
        ],
        compiler_params=_cparams(("arbitrary",)),
    )(swa2, swa2, swa2, bias, sinkb, cat, dcat, lse)


_MLA_SCALE = MLA_QK ** -0.5
_MLA_TQ = 256
_MLA_FWD_HEADS = 4
_MLA_BWD_HEADS = 2


def _mla_mask(i, tq, n_keys):
    row = i * tq + lax.broadcasted_iota(jnp.int32, (tq, n_keys), 0)
    col = lax.broadcasted_iota(jnp.int32, (tq, n_keys), 1)
    return col <= row


def _per_query_block(i, n_blocks, fn):
    for ii in range(n_blocks):
        pl.when(i == ii)(functools.partial(fn, ii))


def _mla_fwd(q4, k4, v4):
    s_len = q4.shape[1]
    tq = min(_MLA_TQ, s_len)
    pair = _MLA_FWD_HEADS

    def body(q_ref, k_ref, v_ref, o_ref, lse_ref):
        def block(ii):
            n_keys = (ii + 1) * tq
            mask = _mla_mask(ii, tq, n_keys)
            for hh in range(pair):
                s = jnp.where(mask, _dot(q_ref[hh], k_ref[hh, :n_keys, :], "nt") * _MLA_SCALE, _NEG)
                m = jnp.max(s, axis=-1, keepdims=True)
                p = jnp.exp(s - m)
                l = jnp.sum(p, axis=-1, keepdims=True)
                o_ref[:, hh * MLA_V:(hh + 1) * MLA_V] = (_dot(p, v_ref[hh, :n_keys, :], "nn") / l).astype(o_ref.dtype)
                lse_ref[hh] = m + jnp.log(l)

        _per_query_block(pl.program_id(1), s_len // tq, block)

    return pl.pallas_call(
        body,
        name="mla_fwd",
        grid=(MLA_HEADS // pair, s_len // tq),
        in_specs=[
            pl.BlockSpec((pair, tq, MLA_QK), lambda h, i: (h, i, 0)),
            pl.BlockSpec((pair, s_len, MLA_QK), lambda h, i: (h, 0, 0)),
            pl.BlockSpec((pair, s_len, MLA_V), lambda h, i: (h, 0, 0)),
        ],
        out_specs=[
            pl.BlockSpec((tq, pair * MLA_V), lambda h, i: (i, h)),
            pl.BlockSpec((pair, tq, 1), lambda h, i: (h, i, 0)),
        ],
        out_shape=[_sds((s_len, MLA_HEADS * MLA_V), BF16), _sds((MLA_HEADS, s_len, 1), F32)],
        compiler_params=_cparams(("parallel", "parallel")),
    )(q4, k4, v4)


def _mla_bwd(q4, k4, v4, cat, dcat, lse):
    s_len = q4.shape[1]
    tq = min(_MLA_TQ, s_len)
    pair = _MLA_BWD_HEADS
    first = SWA_HEADS * SWA_HEAD_DIM // (pair * MLA_V)

    def body(q_ref, k_ref, v_ref, o_ref, do_ref, lse_ref, dq_ref, dk_ref, dv_ref):
        i = pl.program_id(1)

        @pl.when(i == 0)
        def _():
            dk_ref[...] = jnp.zeros_like(dk_ref)
            dv_ref[...] = jnp.zeros_like(dv_ref)

        def block(ii):
            n_keys = (ii + 1) * tq
            mask = _mla_mask(ii, tq, n_keys)
            for hh in range(pair):
                cols = slice(hh * MLA_V, (hh + 1) * MLA_V)
                q, k, v, do = q_ref[hh], k_ref[hh, :n_keys, :], v_ref[hh, :n_keys, :], do_ref[:, cols]
                s = jnp.where(mask, _dot(q, k, "nt") * _MLA_SCALE, _NEG)
                p = jnp.where(mask, jnp.exp(s - lse_ref[hh]), 0.0)
                delta = jnp.sum(do.astype(F32) * o_ref[:, cols].astype(F32), axis=-1, keepdims=True)
                ds = (p * (_dot(do, v, "nt") - delta) * _MLA_SCALE).astype(BF16)
                dq_ref[hh] = _dot(ds, k, "nn")
                dk_ref[hh, :n_keys, :] += _dot(ds, q, "tn")
                dv_ref[hh, :n_keys, :] += _dot(p, do, "tn")

        _per_query_block(i, s_len // tq, block)

    return pl.pallas_call(
        body,
        name="mla_bwd",
        grid=(MLA_HEADS // pair, s_len // tq),
        in_specs=[
            pl.BlockSpec((pair, tq, MLA_QK), lambda h, i: (h, i, 0)),
            pl.BlockSpec((pair, s_len, MLA_QK), lambda h, i: (h, 0, 0)),
            pl.BlockSpec((pair, s_len, MLA_V), lambda h, i: (h, 0, 0)),
            pl.BlockSpec((tq, pair * MLA_V), lambda h, i: (i, first + h)),
            pl.BlockSpec((tq, pair * MLA_V), lambda h, i: (i, first + h)),
            pl.BlockSpec((pair, tq, 1), lambda h, i: (h, i, 0)),
        ],
        out_specs=[
            pl.BlockSpec((pair, tq, MLA_QK), lambda h, i: (h, i, 0)),
            pl.BlockSpec((pair, s_len, MLA_QK), lambda h, i: (h, 0, 0)),
            pl.BlockSpec((pair, s_len, MLA_V), lambda h, i: (h, 0, 0)),
        ],
        out_shape=[
            _sds((MLA_HEADS, s_len, MLA_QK), F32),
            _sds((MLA_HEADS, s_len, MLA_QK), F32),
            _sds((MLA_HEADS, s_len, MLA_V), F32),
        ],
        compiler_params=_cparams(("parallel", "arbitrary")),
    )(q4, k4, v4, cat, dcat, lse)


def _mla_split(pm):
    q_lat = pm[:, :MLA_Q_RANK]
    kv_lat = pm[:, MLA_Q_RANK:MLA_Q_RANK + MLA_KV_RANK]
    kr = pm[:, MLA_Q_RANK + MLA_KV_RANK:MLA_Q_RANK + MLA_KV_RANK + MLA_ROPE]
    kr_sw = pm[:, MLA_Q_RANK + MLA_KV_RANK + MLA_ROPE:]
    return q_lat, kv_lat, kr, kr_sw


def _mla_proj(pm, cos2, sin2, q_norm, kv_norm, wq_ext, wkv):
    s_len = pm.shape[0]

    def fn(pm_, cos_, sin_, qg, kvg, wq, wk):
        q_lat, kv_lat, kr, kr_sw = _mla_split(pm_)
        nq = (_rms(q_lat)[0] * qg).astype(BF16)
        nkv = (_rms(kv_lat)[0] * kvg).astype(BF16)
        k_rot = kr * cos_ + kr_sw * sin_
        qs, ks, vs = [], [], []
        for h in range(MLA_HEADS):
            qe = _dot(nq, wq[h], "nt")
            q_rot = qe[:, MLA_NOPE:MLA_QK] * cos_ + qe[:, MLA_QK:] * sin_
            qs.append(jnp.concatenate([qe[:, :MLA_NOPE], q_rot], axis=-1))
            kve = _dot(nkv, wk[h], "nt")
            ks.append(jnp.concatenate([kve[:, :MLA_NOPE], k_rot], axis=-1))
            vs.append(kve[:, MLA_NOPE:])
        return jnp.stack(qs), jnp.stack(ks), jnp.stack(vs)

    return _rowwise(
        "mla_proj", fn, [pm, cos2, sin2], [q_norm, kv_norm, wq_ext, wkv],
        [_sds((MLA_HEADS, s_len, MLA_QK), BF16), _sds((MLA_HEADS, s_len, MLA_QK), BF16),
         _sds((MLA_HEADS, s_len, MLA_V), BF16)], [], ELEM_TILE,
    )


def _mla_proj_bwd(pm, cos2, sin2, dq4, dk4, dv4, q_norm, kv_norm, wq_ext, wkv):
    s_len = pm.shape[0]

    def fn(pm_, cos_, sin_, dq, dk, dv, qg, kvg, wq, wk):
        q_lat, kv_lat, _, _ = _mla_split(pm_)
        xq, rq = _rms(q_lat)
        xkv, rkv = _rms(kv_lat)
        nq = (xq * qg).astype(BF16)
        nkv = (xkv * kvg).astype(BF16)
        dnq = jnp.zeros_like(q_lat)
        dnkv = jnp.zeros_like(kv_lat)
        dkr = jnp.zeros_like(cos_)
        dwq, dwk = [], []
        for h in range(MLA_HEADS):
            dy = dq[h][:, MLA_NOPE:]
            dqe = jnp.concatenate([dq[h][:, :MLA_NOPE], dy * cos_, dy * sin_], axis=-1).astype(BF16)
            dnq = dnq + _dot(dqe, wq[h], "nn")
            dwq.append(_dot(dqe, nq, "tn"))
            dkve = jnp.concatenate([dk[h][:, :MLA_NOPE], dv[h]], axis=-1).astype(BF16)
            dnkv = dnkv + _dot(dkve, wk[h], "nn")
            dwk.append(_dot(dkve, nkv, "tn"))
            dkr = dkr + dk[h][:, MLA_NOPE:]
        dq_lat = _rms_bwd(dnq * qg, xq, rq)
        dkv_lat = _rms_bwd(dnkv * kvg, xkv, rkv)
        dpm = jnp.concatenate([dq_lat, dkv_lat, dkr * cos_, dkr * sin_], axis=-1)
        return dpm, _sum0(dnq * xq), _sum0(dnkv * xkv), jnp.stack(dwq), jnp.stack(dwk)

    return _rowwise(
        "mla_proj_bwd", fn, [pm, cos2, sin2, dq4, dk4, dv4], [q_norm, kv_norm, wq_ext, wkv],
        [_sds((s_len, N_MLA_COLS), BF16)],
        [_sds((1, MLA_Q_RANK), F32), _sds((1, MLA_KV_RANK), F32),
         _sds(wq_ext.shape, F32), _sds(wkv.shape, F32)], ELEM_TILE,
    )


def _rope_tables(s_len):
    inv = ROPE_THETA ** (-jnp.arange(0, MLA_ROPE, 2, dtype=F32) / MLA_ROPE)
    ang = jnp.arange(s_len, dtype=F32)[:, None] * inv[None, :]
    cos, sin = jnp.cos(ang), jnp.sin(ang)
    return jnp.concatenate([cos, cos], axis=-1), jnp.concatenate([-sin, sin], axis=-1)


def _mix_weights(g_mix):
    rest = g_mix[:, 0]
    w_in_t = rest[:, O_IN:O_IN + R_IN].reshape(D_IN, D_MODEL)
    w_o = rest[:, O_O:O_O + R_O].reshape(D_MODEL, D_MODEL)
    w_uq_t = rest[:, O_UQ:O_UQ + R_UQ].reshape(MLA_HEADS, MLA_QK, MLA_Q_RANK)
    w_ukv_t = rest[:, O_UKV:O_UKV + R_UKV].reshape(MLA_HEADS, MLA_NOPE + MLA_V, MLA_KV_RANK)
    half = MLA_ROPE // 2
    w_swa = w_in_t[:N_SWA_COLS]
    w_mla = jnp.concatenate([w_in_t[N_SWA_COLS:], w_in_t[D_IN - half:], w_in_t[D_IN - MLA_ROPE:D_IN - half]], axis=0)
    wq_ext = jnp.concatenate([w_uq_t, w_uq_t[:, MLA_QK - half:], w_uq_t[:, MLA_NOPE:MLA_QK - half]], axis=1)
    return w_swa, w_mla, w_o, wq_ext, w_ukv_t


def _mix_fwd(x, h, gate, mix_src, q_norm, kv_norm, bias, sinkb, cos2, sin2, next_norm):
    s_len = x.shape[0]
    tm = min(ROW_TILE, s_len)
    deps = mix_src.mid(x)
    weights = _mix_weights(mix_src.get(h))
    w_swa, w_mla, w_o, wq_ext, wkv = weights
    swa2 = _mm(
        "mix_proj_swa", "nt", (s_len // tm, 1, 1),
        h, (tm, D_MODEL), lambda i, j, k: (i, 0),
        w_swa, (N_SWA_COLS, D_MODEL), lambda i, j, k: (0, 0),
        _sds((s_len, N_SWA_COLS), BF16), (tm, N_SWA_COLS), lambda i, j, k: (i, 0),
        (tm, N_SWA_COLS), deps=deps,
    )
    pm = _mm(
        "mix_proj_mla", "nt", (s_len // tm, 1, 1),
        h, (tm, D_MODEL), lambda i, j, k: (i, 0),
        w_mla, (N_MLA_COLS, D_MODEL), lambda i, j, k: (0, 0),
        _sds((s_len, N_MLA_COLS), F32), (tm, N_MLA_COLS), lambda i, j, k: (i, 0),
        (tm, N_MLA_COLS),
    )
    q4, k4, v4 = _mla_proj(pm, cos2, sin2, q_norm, kv_norm, wq_ext, wkv)
    oa, lse_a = _swa_fwd(swa2, bias, sinkb)
    ob, lse_b = _mla_fwd(q4, k4, v4)
    cat = jnp.concatenate([oa, ob], axis=1)
    z = _mm(
        "mix_out", "nn", (s_len // tm, 1, 1),
        cat, (tm, D_MODEL), lambda i, j, k: (i, 0),
        w_o, (D_MODEL, D_MODEL), lambda i, j, k: (0, 0),
        _sds((s_len, D_MODEL), F32), (tm, D_MODEL), lambda i, j, k: (i, 0),
        (tm, D_MODEL),
    )
    def residual_norm(x_, z_, g_, gamma_, sc_, sh_):
        x_out = x_ + g_ * z_
        return x_out, _normmod(x_out, gamma_, sc_, sh_)

    out = _rowwise(
        "mix_residual_norm", residual_norm, [x, z], [gate] + list(next_norm),
        [_sds((s_len, D_MODEL), F32), _sds((s_len, D_MODEL), BF16)], [], ELEM_TILE,
    )
    return out, (weights, h, swa2, pm, q4, k4, v4, cat, lse_a, lse_b, z)


def _mix_bwd(dxo, x, gamma, sc, gate, q_norm, kv_norm, bias, sinkb, cos2, sin2, saved, hooks):
    weights, h, swa2, pm, q4, k4, v4, cat, lse_a, lse_b, z = saved
    w_swa, w_mla, w_o, wq_ext, wkv = weights
    s_len = x.shape[0]
    tm = tk = min(ROW_TILE, s_len)
    vec = _sds((1, D_MODEL), F32)
    n_proj = N_SWA_COLS + N_MLA_COLS
    half_d = D_MODEL // 2

    dz, dgate = _rowwise(
        "mix_bwd_gate", lambda dxo_, z_, g_: (g_ * dxo_, _sum0(z_ * dxo_)),
        [dxo, z], [gate], [_sds((s_len, D_MODEL), BF16)], [vec], ELEM_TILE,
    )
    dw_o = _mm(
        "mix_bwd_wo", "tn", (2, 1, s_len // tk),
        cat, (tk, half_d), lambda i, j, k: (k, i),
        dz, (tk, D_MODEL), lambda i, j, k: (k, 0),
        _sds((D_MODEL, D_MODEL), F32), (half_d, D_MODEL), lambda i, j, k: (i, 0),
        (half_d, D_MODEL),
    )
    dcat = _mm(
        "mix_bwd_dcat", "nt", (s_len // tm, 1, 1),
        dz, (tm, D_MODEL), lambda i, j, k: (i, 0),
        w_o, (D_MODEL, D_MODEL), lambda i, j, k: (0, 0),
        _sds((s_len, D_MODEL), BF16), (tm, D_MODEL), lambda i, j, k: (i, 0),
        (tm, D_MODEL), deps=hooks.after_gate(dz),
    )
    dq_a, dkva, dkvb, dbias, dsink = _swa_bwd(swa2, bias, sinkb, cat, dcat, lse_a)
    dq4, dk4, dv4 = _mla_bwd(q4, k4, v4, cat, dcat, lse_b)
    dpm, dq_norm, dkv_norm, dwq_ext, dwkv = _mla_proj_bwd(pm, cos2, sin2, dq4, dk4, dv4, q_norm, kv_norm, wq_ext, wkv)

    dkv = dkva + jnp.concatenate([dkvb[WINDOW:], jnp.zeros_like(dkvb[:WINDOW])], axis=0)
    dproj = jnp.concatenate([dq_a, dkv.astype(BF16), dpm], axis=1)
    w_proj = jnp.concatenate([w_swa, w_mla], axis=0)

    dw_proj = _mm(
        "mix_bwd_win", "tn", (n_proj // 256, 1, s_len // tk),
        dproj, (tk, 256), lambda i, j, k: (k, i),
        h, (tk, D_MODEL), lambda i, j, k: (k, 0),
        _sds((n_proj, D_MODEL), F32), (256, D_MODEL), lambda i, j, k: (i, 0),
        (256, D_MODEL),
    )
    dw_swa, dw_mla = dw_proj[:N_SWA_COLS], dw_proj[N_SWA_COLS:]
    dh = _mm(
        "mix_bwd_dh", "nn", (s_len // tm, 1, 1),
        dproj, (tm, n_proj), lambda i, j, k: (i, 0),
        w_proj, (n_proj, D_MODEL), lambda i, j, k: (0, 0),
        _sds((s_len, D_MODEL), F32), (tm, D_MODEL), lambda i, j, k: (i, 0),
        (tm, D_MODEL),
    )
    dx, dsc, dsh, dgamma = _normmod_bwd_call("mix_bwd_normmod", [dh], x, dxo, gamma, sc)

    half = MLA_ROPE // 2
    n_mla_in = D_IN - N_SWA_COLS
    dw_mla_base = dw_mla[:n_mla_in]
    dw_mla_base = dw_mla_base.at[n_mla_in - half:].add(dw_mla[n_mla_in:n_mla_in + half])
    dw_mla_base = dw_mla_base.at[n_mla_in - MLA_ROPE:n_mla_in - half].add(dw_mla[n_mla_in + half:])
    dw_in_t = jnp.concatenate([dw_swa, dw_mla_base], axis=0)
    dw_uq_t = dwq_ext[:, :MLA_QK]
    dw_uq_t = dw_uq_t.at[:, MLA_QK - half:].add(dwq_ext[:, MLA_QK:MLA_QK + half])
    dw_uq_t = dw_uq_t.at[:, MLA_NOPE:MLA_QK - half].add(dwq_ext[:, MLA_QK + half:])
    rest = jnp.concatenate(
        [
            dw_in_t.reshape(N_CHIPS, R_IN, D_MODEL),
            dw_o.reshape(N_CHIPS, R_O, D_MODEL),
            dw_uq_t.reshape(N_CHIPS, R_UQ, D_MODEL),
            dwkv.reshape(N_CHIPS, R_UKV, D_MODEL),
            jnp.zeros((N_CHIPS, F_SHARD - O_PAD, D_MODEL), F32),
        ],
        axis=1,
    ).astype(BF16)
    return dx, rest, (dsh, dsc, dgate, dgamma), dq_norm, dkv_norm, dbias, dsink


def _device_step(x, target, mod, gu1_src, down1_src, mix_src, ffn2_src, norms, q_norm, kv_norm, sinks, rel_bias,
                 hooks2=None, hooks_mix=None, mix_done=None, hooks1=None):
    s_len = x.shape[0]
    sh1, sc1, g1, sh2, sc2, g2, sh3, sc3, g3 = [mod[:, i * D_MODEL:(i + 1) * D_MODEL] for i in range(N_MOD)]
    n1, n2, n3, nf = norms
    bkt = jnp.asarray(_bucket_map())
    bias = _bias_expand(rel_bias.T, bkt).reshape(SWA_HEADS, WINDOW, 2 * WINDOW)
    sinkb = jnp.broadcast_to(sinks.reshape(SWA_HEADS, 1, 1), (SWA_HEADS, WINDOW, 1))
    cos2, sin2 = _rope_tables(s_len)

    (x1, h2), saved1 = _ffn_fwd(
        "ffn1", x, n1, sh1, sc1, g1, gu1_src, 0, down1_src, 0, sh1, next_norm=(n2, sc2, sh2)
    )
    (x2, h3), saved2 = _mix_fwd(
        x1, h2, g2, mix_src, q_norm, kv_norm, bias, sinkb, cos2, sin2, next_norm=(n3, sc3, sh3)
    )
    (dx3, sq, dnf), saved3 = _ffn_fwd(
        "ffn2", x2, n3, sh3, sc3, g3, ffn2_src, 0, None, 2, x2, h_pre=h3, head=(target, nf)
    )
    loss_part = (0.5 / D_MODEL) * jnp.sum(sq)

    dx2, gb2, (dsh3, dsc3, dg3, dn3) = _ffn_bwd("ffn2", dx3, x2, n3, sc3, g3, saved3, hooks2 or _BwdHooks())
    dx1, rest, (dsh2, dsc2, dg2, dn2), dq_norm, dkv_norm, dbias, dsink = _mix_bwd(
        dx2, x1, n2, sc2, g2, q_norm, kv_norm, bias, sinkb, cos2, sin2, saved2, hooks_mix or _BwdHooks()
    )
    rest = rest[:, None]
    deps1 = mix_done(dx1, rest) if mix_done is not None else []
    dx0, gb1, (dsh1, dsc1, dg1, dn1) = _ffn_bwd(
        "ffn1", dx1, x, n1, sc1, g1, saved1, hooks1 or _BwdHooks(), deps=deps1
    )

    dmod = jnp.concatenate([dsh1, dsc1, dg1, dsh2, dsc2, dg2, dsh3, dsc3, dg3], axis=-1)
    drel = _bias_reduce(dbias.reshape(SWA_HEADS, -1), bkt).T
    dsinks = jnp.sum(dsink, axis=(1, 2)).reshape(1, SWA_HEADS)
    small = (dn1, dn2, dn3, dnf, dq_norm, dkv_norm, dsinks, drel)
    return loss_part, dx0, (gb1, gb2, rest), dmod, small


_MESH = pl.DeviceIdType.MESH


def _place():
    return lax.axis_index("x"), lax.axis_index("y"), lax.axis_index("c")


def _other_chips(x, y):
    return [(1 - x, y), (x, 1 - y), (1 - x, 1 - y)]


def _allgather_small(name, blk):
    m_per, n = blk.shape

    def body(x_ref, out_ref, send_sems, recv_sems, local_sem):
        x, y, c = _place()
        me, sibling = (x, y, c), (x, y, 1 - c)
        chips = _other_chips(x, y)

        def rows(px, py, pc):
            return out_ref.at[pl.ds((4 * px + 2 * py + pc) * m_per, m_per), :]

        def copy(k, block, to, src=None):
            return pltpu.make_async_remote_copy(
                src_ref=rows(*block) if src is None else src, dst_ref=rows(*block),
                send_sem=send_sems.at[k], recv_sem=recv_sems.at[k], device_id=to, device_id_type=_MESH,
            )

        mine = pltpu.make_async_copy(x_ref, rows(*me), local_sem)
        mine.start()
        first = [copy(0, me, sibling, src=x_ref)]
        first += [copy(1 + j, me, (*chip, c), src=x_ref) for j, chip in enumerate(chips)]
        for cp in first:
            cp.start()
        passed = [copy(4 + j, (*chip, c), sibling) for j, chip in enumerate(chips)]
        for j, chip in enumerate(chips):
            copy(1 + j, (*chip, c), me).wait_recv()
            passed[j].start()
        copy(0, sibling, me).wait_recv()
        for j, chip in enumerate(chips):
            copy(4 + j, (*chip, 1 - c), me).wait_recv()
        for cp in first + passed:
            cp.wait_send()
        mine.wait()

    return pl.pallas_call(
        body,
        name=name,
        out_shape=_sds((N_DEV * m_per, n), blk.dtype),
        in_specs=[pl.BlockSpec(memory_space=pltpu.VMEM)],
        out_specs=pl.BlockSpec(memory_space=pltpu.VMEM),
        scratch_shapes=[pltpu.SemaphoreType.DMA((7,)), pltpu.SemaphoreType.DMA((7,)), pltpu.SemaphoreType.DMA],
    )(blk)


def _gather_sends(n, own_ref, g_ref, send_sem, recv_sem, x, y, c, chip):
    return [
        pltpu.make_async_remote_copy(
            src_ref=own_ref.at[p, pl.ds(c * HALF, HALF), :], dst_ref=g_ref.at[2 * x + y, p, pl.ds(c * HALF, HALF), :],
            send_sem=send_sem, recv_sem=recv_sem, device_id=(*chip, c), device_id_type=_MESH,
        )
        for p in range(n)
    ]


def _gather_forwards(n, g_ref, send_sem, recv_sem, chip, c, sibling):
    return [
        pltpu.make_async_remote_copy(
            src_ref=g_ref.at[2 * chip[0] + chip[1], p, pl.ds(c * HALF, HALF), :],
            dst_ref=g_ref.at[2 * chip[0] + chip[1], p, pl.ds(c * HALF, HALF), :],
            send_sem=send_sem, recv_sem=recv_sem, device_id=sibling, device_id_type=_MESH,
        )
        for p in range(n)
    ]


def _gather_all(own_ref, g_ref, send_sem, recv_sem, chip, half, c):
    return pltpu.make_async_remote_copy(
        src_ref=own_ref.at[:, pl.ds(c * HALF, HALF), :],
        dst_ref=g_ref.at[2 * chip[0] + chip[1], :, pl.ds(half * HALF, HALF), :],
        send_sem=send_sem, recv_sem=recv_sem, device_id=(*chip, c), device_id_type=_MESH,
    )


_HBM_SPEC = pl.BlockSpec(memory_space=pltpu.HBM)
_SEM_SPEC = pl.BlockSpec(memory_space=pltpu.SEMAPHORE)
_ANY_SPEC = pl.BlockSpec(memory_space=pl.ANY)
_VMEM_SPEC = pl.BlockSpec(memory_space=pltpu.VMEM)
_SPLIT_PARAMS = pltpu.CompilerParams(has_side_effects=pltpu.SideEffectType.DATAFLOW_SIDE_EFFECTING)
_TOKEN = jax.ShapeDtypeStruct((8, 128), F32)


def _in_hbm(a):
    return pltpu.with_memory_space_constraint(a, pltpu.HBM)


class _GatherBehind:
    def __init__(self, tag, own, then=None):
        self.tag, self.n, self.own, self.then = tag, own.shape[0], own, then

    def start(self, after):
        tag, own, n = self.tag, self.own, self.n
        x, y, _ = _place()
        g_init = lax.dynamic_update_slice(
            lax.empty((N_CHIPS,) + own.shape, own.dtype), own[None], (2 * x + y, 0, 0, 0)
        )

        def body(own_ref, g_ref, *rest):
            send_sems, recv_sems, _, _, token = rest[len(after):]
            x, y, c = _place()
            for j, chip in enumerate(_other_chips(x, y)):
                for cp in _gather_sends(n, own_ref, g_ref, send_sems.at[j], recv_sems.at[j], x, y, c, chip):
                    cp.start()
            token[...] = jnp.zeros_like(token)

        self.send1, self.recv1, self.own, self.g, self.token = pl.pallas_call(
            body,
            name=f"{tag}_start",
            out_shape=(
                pltpu.SemaphoreType.DMA((3,)), pltpu.SemaphoreType.DMA((3,)),
                pltpu.HBM(own.shape, own.dtype), pltpu.HBM(g_init.shape, g_init.dtype), _TOKEN,
            ),
            in_specs=(_HBM_SPEC, _HBM_SPEC) + (_ANY_SPEC,) * len(after),
            out_specs=(_SEM_SPEC, _SEM_SPEC, _HBM_SPEC, _HBM_SPEC, _VMEM_SPEC),
            input_output_aliases={0: 2, 1: 3},
            compiler_params=_SPLIT_PARAMS,
        )(_in_hbm(own), _in_hbm(g_init), *after)

    def mid(self, after):
        n = self.n

        def body(own_ref, g_ref, send1, recv1, after_ref, send2, recv2, g_out, token):
            x, y, c = _place()
            sibling = (x, y, 1 - c)
            chips = _other_chips(x, y)
            for j, chip in enumerate(chips):
                _gather_all(own_ref, g_ref, send1.at[j], recv1.at[j], chip, c, c).wait_recv()
                for cp in _gather_forwards(n, g_ref, send2.at[j], recv2.at[j], chip, c, sibling):
                    cp.start()
            for j, chip in enumerate(chips):
                _gather_all(own_ref, g_ref, send1.at[j], recv1.at[j], chip, c, c).wait_send()
            token[...] = jnp.zeros_like(token)

        self.send2, self.recv2, self.g, token = pl.pallas_call(
            body,
            name=f"{self.tag}_mid",
            out_shape=(
                pltpu.SemaphoreType.DMA((3,)), pltpu.SemaphoreType.DMA((3,)),
                pltpu.HBM(self.g.shape, self.g.dtype), _TOKEN,
            ),
            in_specs=(_HBM_SPEC, _HBM_SPEC, _SEM_SPEC, _SEM_SPEC, _ANY_SPEC),
            out_specs=(_SEM_SPEC, _SEM_SPEC, _HBM_SPEC, _VMEM_SPEC),
            input_output_aliases={1: 2},
            compiler_params=_SPLIT_PARAMS,
        )(self.own, self.g, self.send1, self.recv1, after)
        if self.then is None:
            return [token]
        self.then.start([token])
        return [token, self.then.token]

    def get(self, after):
        def body(g_ref, send2, recv2, after_ref, g_out):
            x, y, c = _place()
            for j, chip in enumerate(_other_chips(x, y)):
                slot_in = g_ref.at[2 * chip[0] + chip[1], :, pl.ds((1 - c) * HALF, HALF), :]
                slot_out = g_ref.at[2 * chip[0] + chip[1], :, pl.ds(c * HALF, HALF), :]
                cp = pltpu.make_async_remote_copy(
                    src_ref=slot_out, dst_ref=slot_in, send_sem=send2.at[j], recv_sem=recv2.at[j],
                    device_id=(x, y, 1 - c), device_id_type=_MESH,
                )
                cp.wait_send()
                cp.wait_recv()

        return pl.pallas_call(
            body,
            name=f"{self.tag}_wait",
            out_shape=pltpu.HBM(self.g.shape, self.g.dtype),
            in_specs=(_HBM_SPEC, _SEM_SPEC, _SEM_SPEC, _ANY_SPEC),
            out_specs=_HBM_SPEC,
            input_output_aliases={0: 0},
            compiler_params=_SPLIT_PARAMS,
        )(self.g, self.send2, self.recv2, after)


def _pair_sum(name, gb, land):
    def body(c_ref, gb_ref, land_ref, o_ref):
        o_ref[...] = (gb_ref[...].astype(F32) + land_ref[...].astype(F32)).astype(o_ref.dtype)

    core = lax.axis_index("c").astype(jnp.int32).reshape(1)
    return pl.pallas_call(
        body,
        name=name,
        grid_spec=pltpu.PrefetchScalarGridSpec(
            num_scalar_prefetch=1,
            grid=(N_CHIPS, gb.shape[1]),
            in_specs=[
                pl.BlockSpec((None, None, HALF, D_MODEL), lambda j, p, c: (j, p, c[0], 0)),
                pl.BlockSpec((None, None, HALF, D_MODEL), lambda j, p, c: (j, p, 0, 0)),
            ],
            out_specs=pl.BlockSpec((None, None, HALF, D_MODEL), lambda j, p, c: (j, p, 0, 0)),
        ),
        out_shape=_sds(land.shape, BF16),
        compiler_params=_cparams(("parallel", "parallel")),
    )(core, gb, land)


def _chip_sum_share(name, land):
    n = land.shape[1]

    def body(l_ref, r_ref, buf, send_sems, recv_sems, local_sems):
        p = pl.program_id(0)
        x, y, c = _place()
        acc = l_ref[0].astype(F32)
        for j in range(1, N_CHIPS):
            acc = acc + l_ref[j].astype(F32)
        buf[p] = acc

        def copies(q):
            mine = r_ref.at[q, pl.ds(c * HALF, HALF), :]
            theirs = r_ref.at[q, pl.ds((1 - c) * HALF, HALF), :]
            local = pltpu.make_async_copy(buf.at[q], mine, local_sems.at[q])
            out = pltpu.make_async_remote_copy(
                src_ref=buf.at[q], dst_ref=mine, send_sem=send_sems.at[q], recv_sem=recv_sems.at[q],
                device_id=(x, y, 1 - c), device_id_type=_MESH,
            )
            arrive = pltpu.make_async_remote_copy(
                src_ref=buf.at[q], dst_ref=theirs, send_sem=send_sems.at[q], recv_sem=recv_sems.at[q],
                device_id=(x, y, 1 - c), device_id_type=_MESH,
            )
            return local, out, arrive

        local, out, _ = copies(p)
        local.start()
        out.start()

        @pl.when(p == n - 1)
        def _():
            for q in range(n):
                local, out, arrive = copies(q)
                arrive.wait_recv()
                out.wait_send()
                local.wait()

    return pl.pallas_call(
        body,
        name=name,
        grid=(n,),
        in_specs=[pl.BlockSpec((N_CHIPS, None, HALF, D_MODEL), lambda p: (0, p, 0, 0))],
        out_specs=pl.BlockSpec(memory_space=pl.ANY),
        out_shape=_sds((n, F_SHARD, D_MODEL), F32),
        scratch_shapes=[
            pltpu.VMEM((n, HALF, D_MODEL), F32),
            pltpu.SemaphoreType.DMA((n,)), pltpu.SemaphoreType.DMA((n,)), pltpu.SemaphoreType.DMA((n,)),
        ],
        compiler_params=_cparams(("arbitrary",)),
    )(land)


class _ReduceBehind:
    def __init__(self, tag, gb, after=()):
        self.tag = tag
        n = gb.shape[1]
        land = lax.empty((N_CHIPS, n, HALF, D_MODEL), gb.dtype)

        def body(gb_ref, land_ref, *rest):
            send_sem, recv_sem, _, _, token = rest[len(after):]
            self._pair_copy(gb_ref, land_ref, send_sem, recv_sem).start()
            token[...] = jnp.zeros_like(token)

        self.send, self.recv, self.gb, self.land, self.token = pl.pallas_call(
            body,
            name=f"grads_pair_start_{tag}",
            out_shape=(
                pltpu.SemaphoreType.DMA((1,)), pltpu.SemaphoreType.DMA((1,)),
                pltpu.HBM(gb.shape, gb.dtype), pltpu.HBM(land.shape, land.dtype), _TOKEN,
            ),
            in_specs=(_HBM_SPEC, _HBM_SPEC) + (_ANY_SPEC,) * len(after),
            out_specs=(_SEM_SPEC, _SEM_SPEC, _HBM_SPEC, _HBM_SPEC, _VMEM_SPEC),
            input_output_aliases={0: 2, 1: 3},
            compiler_params=_SPLIT_PARAMS,
        )(_in_hbm(gb), _in_hbm(land), *after)

    @staticmethod
    def _pair_copy(gb_ref, land_ref, send_sem, recv_sem):
        x, y, c = _place()
        return pltpu.make_async_remote_copy(
            src_ref=gb_ref.at[:, :, pl.ds((1 - c) * HALF, HALF), :], dst_ref=land_ref,
            send_sem=send_sem.at[0], recv_sem=recv_sem.at[0], device_id=(x, y, 1 - c), device_id_type=_MESH,
        )

    @staticmethod
    def _chip_copies(p_ref, land_ref, send_sems, recv_sems):
        x, y, c = _place()
        me = 2 * x + y
        sends, arrivals = [], []
        for k, chip in enumerate(_other_chips(x, y)):
            them = 2 * chip[0] + chip[1]
            sends.append(pltpu.make_async_remote_copy(
                src_ref=p_ref.at[them], dst_ref=land_ref.at[me], send_sem=send_sems.at[k], recv_sem=recv_sems.at[k],
                device_id=(*chip, c), device_id_type=_MESH,
            ))
            arrivals.append(pltpu.make_async_remote_copy(
                src_ref=p_ref.at[me], dst_ref=land_ref.at[them], send_sem=send_sems.at[k], recv_sem=recv_sems.at[k],
                device_id=(*chip, c), device_id_type=_MESH,
            ))
        return sends, arrivals

    def mid(self, after):
        tag = self.tag

        def wait_body(gb_ref, land_ref, send_sem, recv_sem, after_ref, gb_out, land_out):
            cp = self._pair_copy(gb_ref, land_ref, send_sem, recv_sem)
            cp.wait_send()
            cp.wait_recv()

        gb, land = pl.pallas_call(
            wait_body,
            name=f"grads_pair_wait_{tag}",
            out_shape=(pltpu.HBM(self.gb.shape, self.gb.dtype), pltpu.HBM(self.land.shape, self.land.dtype)),
            in_specs=(_HBM_SPEC, _HBM_SPEC, _SEM_SPEC, _SEM_SPEC, _ANY_SPEC),
            out_specs=(_HBM_SPEC, _HBM_SPEC),
            input_output_aliases={0: 0, 1: 1},
            compiler_params=_SPLIT_PARAMS,
        )(self.gb, self.land, self.send, self.recv, after)
        part = _pair_sum(f"grads_pair_sum_{tag}", gb, land)

        x, y, _ = _place()
        start = (2 * x + y, 0, 0, 0)
        own = lax.dynamic_slice(part, start, (1,) + part.shape[1:])
        land2 = lax.dynamic_update_slice(lax.empty(part.shape, part.dtype), own, start)

        def start_body(p_ref, land_ref, send_sems, recv_sems, p_out, land_out, token):
            for cp in self._chip_copies(p_ref, land_ref, send_sems, recv_sems)[0]:
                cp.start()
            token[...] = jnp.zeros_like(token)

        self.send2, self.recv2, self.part, self.land2, token = pl.pallas_call(
            start_body,
            name=f"grads_chip_start_{tag}",
            out_shape=(
                pltpu.SemaphoreType.DMA((3,)), pltpu.SemaphoreType.DMA((3,)),
                pltpu.HBM(part.shape, part.dtype), pltpu.HBM(land2.shape, land2.dtype), _TOKEN,
            ),
            in_specs=(_HBM_SPEC, _HBM_SPEC),
            out_specs=(_SEM_SPEC, _SEM_SPEC, _HBM_SPEC, _HBM_SPEC, _VMEM_SPEC),
            input_output_aliases={0: 2, 1: 3},
            compiler_params=_SPLIT_PARAMS,
        )(_in_hbm(part), _in_hbm(land2))
        return [token]

    def get(self, after):
        n_after = len(after)

        def wait_body(p_ref, land_ref, send_sems, recv_sems, *rest):
            sends, arrivals = self._chip_copies(p_ref, land_ref, send_sems, recv_sems)
            for cp in arrivals:
                cp.wait_recv()
            for cp in sends:
                cp.wait_send()

        _, land2 = pl.pallas_call(
            wait_body,
            name=f"grads_chip_wait_{self.tag}",
            out_shape=(pltpu.HBM(self.part.shape, self.part.dtype), pltpu.HBM(self.land2.shape, self.land2.dtype)),
            in_specs=(_HBM_SPEC, _HBM_SPEC, _SEM_SPEC, _SEM_SPEC) + (_ANY_SPEC,) * n_after,
            out_specs=(_HBM_SPEC, _HBM_SPEC),
            input_output_aliases={0: 0, 1: 1},
            compiler_params=_SPLIT_PARAMS,
        )(self.part, self.land2, self.send2, self.recv2, *after)
        return _chip_sum_share(f"grads_chip_sum_share_{self.tag}", land2)


def _allreduce_small(blk):
    gathered = _allgather_small("allgather_small_grads", blk).reshape(N_DEV, PK_ROWS, 128)

    def body(g_ref, o_ref):
        acc = g_ref[0]
        for k in range(1, N_DEV):
            acc = acc + g_ref[k]
        o_ref[...] = acc

    total = pl.pallas_call(body, name="small_grads_sum", out_shape=_sds((PK_ROWS, 128), F32))(gathered)
    return gathered, total


def _adamw_math(w_, g_, m_, v_):
    m_new = ADAM_B1 * m_ + (1.0 - ADAM_B1) * g_
    v_new = ADAM_B2 * v_ + (1.0 - ADAM_B2) * (g_ * g_)
    m_hat = m_new / (1.0 - ADAM_B1 ** ADAM_STEP)
    v_hat = v_new / (1.0 - ADAM_B2 ** ADAM_STEP)
    delta = -ADAM_LR * (m_hat / (jnp.sqrt(v_hat) + ADAM_EPS) + ADAM_WD * w_)
    return delta, m_new, v_new


def _adamw(name, w, g, m, v):
    rows, cols = w.shape
    tm = rows
    while tm * cols * 4 > (1 << 20) and tm % 16 == 0:
        tm //= 2
    out = _sds(w.shape, F32)
    return _rowwise(name, _adamw_math, [w, g, m, v], [], [out, out, out], [], tm)


def _adamw_small(ws, gs, ms, vs):
    n = len(ws)
    shapes = [w.shape for w in ws]
    as2d = lambda a: a.reshape(1, -1) if a.ndim == 1 else a

    def body(*refs):
        ins, outs = refs[:4 * n], refs[4 * n:]
        for k in range(n):
            res = _adamw_math(*[ins[j * n + k][...] for j in range(4)])
            for j in range(3):
                outs[j * n + k][...] = res[j]

    args = [as2d(a) for group in (ws, gs, ms, vs) for a in group]
    out = pl.pallas_call(
        body, name="adamw_small", out_shape=[_sds(as2d(w).shape, F32) for w in ws] * 3, compiler_params=_cparams()
    )(*args)
    back = [o.reshape(shapes[i % n]) for i, o in enumerate(out)]
    return back[:n], back[n:2 * n], back[2 * n:]


def _pack_small(b_mod_like, n1, n2, n3, nf, qn, kvn, sinks, rel):
    parts = [
        b_mod_like.reshape(PK_MOD, 128),
        n1.reshape(8, 128), n2.reshape(8, 128), n3.reshape(8, 128), nf.reshape(8, 128),
        qn.reshape(2, 128), kvn.reshape(1, 128),
        jnp.pad(sinks.reshape(1, SWA_HEADS), ((0, 0), (0, 128 - SWA_HEADS))),
        rel.reshape(2, 128),
        jnp.zeros((2, 128), F32),
    ]
    return jnp.concatenate(parts, axis=0)


def _unpack_small(p):
    o = PK_MOD
    return (
        p[:o].reshape(1, N_MOD * D_MODEL),
        p[o:o + 8].reshape(1, D_MODEL), p[o + 8:o + 16].reshape(1, D_MODEL),
        p[o + 16:o + 24].reshape(1, D_MODEL), p[o + 24:o + 32].reshape(D_MODEL),
        p[o + 32:o + 34].reshape(1, MLA_Q_RANK), p[o + 34:o + 35].reshape(1, MLA_KV_RANK),
        p[o + 35:o + 36, :SWA_HEADS].reshape(1, SWA_HEADS),
        p[o + 36:o + 38].reshape(NUM_BUCKETS, SWA_HEADS),
    )


def kernel(x, c, w_mod, b_mod, norm_ffn1, ffn1_gate, ffn1_up, ffn1_down, norm_mix, w_in, q_norm, kv_norm, w_uq, w_ukv, sinks, w_o, norm_ffn2, ffn2_gate, ffn2_up, ffn2_down, rel_bias, norm_final, loss_target, m_w_mod, m_b_mod, m_norm_ffn1, m_ffn1_gate, m_ffn1_up, m_ffn1_down, m_norm_mix, m_w_in, m_q_norm, m_kv_norm, m_w_uq, m_w_ukv, m_sinks, m_w_o, m_norm_ffn2, m_ffn2_gate, m_ffn2_up, m_ffn2_down, m_rel_bias, m_norm_final, v_w_mod, v_b_mod, v_norm_ffn1, v_ffn1_gate, v_ffn1_up, v_ffn1_down, v_norm_mix, v_w_in, v_q_norm, v_kv_norm, v_w_uq, v_w_ukv, v_sinks, v_w_o, v_norm_ffn2, v_ffn2_gate, v_ffn2_up, v_ffn2_down, v_rel_bias, v_norm_final):
    ax, ay, ac = _place()
    chip = 2 * ax + ay
    dev = 2 * chip + ac
    n_mod_shard = N_MOD * D_MODEL // N_CHIPS

    def t16(w):
        return w[0].T.astype(BF16)

    rest = jnp.concatenate(
        [
            t16(w_in),
            w_o[0].astype(BF16),
            t16(w_uq).reshape(R_UQ, D_MODEL),
            t16(w_ukv).reshape(R_UKV, D_MODEL),
            jnp.zeros((F_SHARD - O_PAD, D_MODEL), BF16),
        ],
        axis=0,
    )
    own_gu1 = jnp.stack([t16(ffn1_gate), t16(ffn1_up)])
    own_down1 = ffn1_down[0].astype(BF16)[None]
    own_mix = rest[None]
    own_ffn2 = jnp.stack([t16(ffn2_gate), t16(ffn2_up), ffn2_down[0].astype(BF16)])

    (c_act,) = _rowwise(
        "silu_c", lambda v: v * _sigmoid(v), [c.reshape(8, 128)], [], [_sds((8, 128), F32)], [], 8
    )
    c_all = _allgather_small("allgather_c", c_act).reshape(N_DEV, D_MODEL)
    mod_cols = _mm(
        "mod_fwd", "nn", (1, n_mod_shard // MOD_TILE, 1),
        c_all, (N_DEV, D_MODEL), lambda i, j, k: (0, 0),
        w_mod[0], (D_MODEL, MOD_TILE), lambda i, j, k: (0, j),
        _sds((N_DEV, n_mod_shard), F32), (N_DEV, MOD_TILE), lambda i, j, k: (0, j),
        (N_DEV, MOD_TILE),
    )
    mod_all = _allgather_small("allgather_mod", mod_cols).reshape(N_CHIPS, 2, N_DEV, n_mod_shard)[:, 0]
    mod_all = mod_all.transpose(1, 0, 2).reshape(N_DEV, N_MOD * D_MODEL)
    mod = lax.dynamic_slice_in_dim(mod_all, dev, 1, axis=0) + b_mod

    norms = (norm_ffn1, norm_mix, norm_ffn2, norm_final.reshape(1, D_MODEL))

    ffn2_src = _GatherBehind("gather_ffn2", own_ffn2)
    mix_src = _GatherBehind("gather_mix", own_mix, then=ffn2_src)
    down1_src = _GatherBehind("gather_down1", own_down1, then=mix_src)
    gu1_src = _GatherBehind("gather_gu1", own_gu1, then=down1_src)
    gu1_src.start([mod_all])

    reducing, red = {}, {}

    def begin(tag, gb, after):
        reducing[tag] = _ReduceBehind(tag, gb, after=after)
        return [reducing[tag].token]

    def ffn2_gu_done(gb):
        return begin("ffn2_gu", gb, reducing["ffn2_down"].mid(gb))

    def mix_done(dx1, gb_rest):
        red["ffn2_down"] = reducing["ffn2_down"].get([dx1])
        red["ffn2_gu"] = reducing["ffn2_gu"].get([red["ffn2_down"]])
        return begin("rest", gb_rest, [red["ffn2_gu"]])

    def ffn1_gu_done(gb):
        red["rest"] = reducing["rest"].get([gb])
        begin("ffn1_gu", gb, reducing["ffn1_down"].mid(red["rest"]))
        return reducing["ffn1_gu"].mid(reducing["ffn1_gu"].token)

    loss_part, grad_x, _, dmod, small = _device_step(
        x[0], loss_target[0], mod, gu1_src, down1_src, mix_src, ffn2_src, norms, q_norm, kv_norm, sinks, rel_bias,
        hooks2=_BwdHooks(after_wdown=lambda gb: begin("ffn2_down", gb, ()), after_wgu=ffn2_gu_done),
        hooks_mix=_BwdHooks(after_gate=lambda dz: reducing["ffn2_gu"].mid(dz)),
        mix_done=mix_done,
        hooks1=_BwdHooks(
            after_swiglu=lambda dab3: reducing["rest"].mid(dab3),
            after_wdown=lambda gb: begin("ffn1_down", gb, ()),
            after_wgu=ffn1_gu_done,
        ),
    )
    loss = lax.psum(loss_part, ("x", "y", "c"))

    gathered, total = _allreduce_small(_pack_small(dmod, *small))
    (g_b_mod, g_n1, g_n2, g_n3, g_nf, g_qn, g_kvn, g_sinks, g_rel) = _unpack_small(total)
    dmod_all = gathered[:, :PK_MOD].reshape(N_DEV, N_MOD * D_MODEL)
    dmod_cols = lax.dynamic_slice_in_dim(dmod_all, chip * n_mod_shard, n_mod_shard, axis=1)
    g_w_mod = _mm(
        "mod_bwd", "tn", (1, n_mod_shard // MOD_TILE, 1),
        c_all, (N_DEV, D_MODEL), lambda i, j, k: (0, 0),
        dmod_cols, (N_DEV, MOD_TILE), lambda i, j, k: (0, j),
        _sds((D_MODEL, n_mod_shard), F32), (D_MODEL, MOD_TILE), lambda i, j, k: (0, j),
        (D_MODEL, MOD_TILE),
    )

    r_rest = red["rest"][0]
    transposed = {"ffn1_gate", "ffn1_up", "ffn2_gate", "ffn2_up", "w_in", "w_uq", "w_ukv"}
    g_big = {
        "ffn2_gate": red["ffn2_gu"][0], "ffn2_up": red["ffn2_gu"][1], "ffn2_down": red["ffn2_down"][0],
        "w_in": r_rest[O_IN:O_IN + R_IN],
        "w_o": r_rest[O_O:O_O + R_O],
        "w_uq": r_rest[O_UQ:O_UQ + R_UQ].reshape(MLA_QK, MLA_Q_RANK),
        "w_ukv": r_rest[O_UKV:O_UKV + R_UKV].reshape(MLA_NOPE + MLA_V, MLA_KV_RANK),
        "w_mod": g_w_mod,
    }
    w_big = {
        "ffn2_gate": (ffn2_gate, m_ffn2_gate, v_ffn2_gate),
        "ffn2_up": (ffn2_up, m_ffn2_up, v_ffn2_up), "ffn2_down": (ffn2_down, m_ffn2_down, v_ffn2_down),
        "w_in": (w_in, m_w_in, v_w_in), "w_o": (w_o, m_w_o, v_w_o), "w_uq": (w_uq, m_w_uq, v_w_uq),
        "w_ukv": (w_ukv, m_w_ukv, v_w_ukv), "w_mod": (w_mod, m_w_mod, v_w_mod),
    }
    grads, deltas, new_m, new_v = {}, {}, {}, {}

    def update(names):
        for nm in names:
            w, m, v = w_big[nm]
            if nm in transposed:
                outs = _adamw(f"adamw_{nm}", w[0].T, g_big[nm], m[0].T, v[0].T)
                g_, d_, m_, v_ = [a.T for a in (g_big[nm],) + tuple(outs)]
            else:
                g_, (d_, m_, v_) = g_big[nm], _adamw(f"adamw_{nm}", w[0], g_big[nm], m[0], v[0])
            grads[nm], deltas[nm], new_m[nm], new_v[nm] = g_[None], d_[None], m_[None], v_[None]

    update(list(w_big))
    red1_down = reducing["ffn1_down"].get([deltas[nm] for nm in w_big])
    red1_gu = reducing["ffn1_gu"].get([red1_down])
    g_big.update({"ffn1_gate": red1_gu[0], "ffn1_up": red1_gu[1], "ffn1_down": red1_down[0]})
    w_big.update({
        "ffn1_gate": (ffn1_gate, m_ffn1_gate, v_ffn1_gate), "ffn1_up": (ffn1_up, m_ffn1_up, v_ffn1_up),
        "ffn1_down": (ffn1_down, m_ffn1_down, v_ffn1_down),
    })
    update(["ffn1_gate", "ffn1_up", "ffn1_down"])

    small_names = ["b_mod", "norm_ffn1", "norm_mix", "norm_ffn2", "norm_final", "q_norm", "kv_norm", "sinks", "rel_bias"]
    w_small = (b_mod, norm_ffn1, norm_mix, norm_ffn2, norm_final, q_norm, kv_norm, sinks, rel_bias)
    m_small = (m_b_mod, m_norm_ffn1, m_norm_mix, m_norm_ffn2, m_norm_final, m_q_norm, m_kv_norm, m_sinks, m_rel_bias)
    v_small = (v_b_mod, v_norm_ffn1, v_norm_mix, v_norm_ffn2, v_norm_final, v_q_norm, v_kv_norm, v_sinks, v_rel_bias)
    g_small = (g_b_mod, g_n1, g_n2, g_n3, g_nf, g_qn, g_kvn, g_sinks, g_rel)
    d_s, m_s, v_s = _adamw_small(w_small, g_small, m_small, v_small)
    for nm, g_, d_, m_, v_ in zip(small_names, g_small, d_s, m_s, v_s):
        grads[nm], deltas[nm], new_m[nm], new_v[nm] = g_, d_, m_, v_

    order = ["w_mod", "b_mod", "norm_ffn1", "ffn1_gate", "ffn1_up", "ffn1_down", "norm_mix", "w_in", "q_norm", "kv_norm",
             "w_uq", "w_ukv", "sinks", "w_o", "norm_ffn2", "ffn2_gate", "ffn2_up", "ffn2_down", "rel_bias", "norm_final"]
    return (loss, grad_x[None], *[grads[n] for n in order], *[deltas[n] for n in order],
            *[new_m[n] for n in order], *[new_v[n] for n in order])
```

```python
import functools
import math

import jax
import jax.numpy as jnp
import numpy as np
from jax import lax
from jax.experimental import pallas as pl
from jax.experimental.pallas import tpu as pltpu

F32, BF16 = jnp.float32, jnp.bfloat16

D_MODEL = 1024
D_FF = 2816
EPS = 1e-6
N_MOD = 9
SWA_HEADS, SWA_KV_HEADS, SWA_HEAD_DIM, WINDOW = 8, 2, 64, 128
SWA_GROUP = SWA_HEADS // SWA_KV_HEADS
MLA_HEADS, MLA_Q_RANK, MLA_KV_RANK, MLA_NOPE, MLA_ROPE, MLA_V = 4, 256, 128, 128, 64, 128
MLA_QK = MLA_NOPE + MLA_ROPE
ROPE_THETA = 10000.0
NUM_BUCKETS, MAX_DISTANCE = 32, 128
D_IN = 1216
N_SWA_COLS = 768
N_MLA_COLS = 512

ADAM_LR, ADAM_B1, ADAM_B2, ADAM_EPS, ADAM_WD, ADAM_STEP = 0.001, 0.9, 0.999, 1e-08, 0.01, 10

N_CHIPS = 4
N_DEV = 8
F_SHARD = D_FF // N_CHIPS
HALF = F_SHARD // 2
R_IN, R_O, R_UQ, R_UKV = 304, 256, 48, 32
O_IN, O_O, O_UQ, O_UKV, O_PAD = 0, 304, 560, 608, 640

PK_MOD = 72
PK_ROWS = 112
VMEM_LIMIT = 56 << 20
ROW_TILE = 2048
ELEM_TILE = 512
MOD_TILE = 768

_DN = {
    "nn": (((1,), (0,)), ((), ())),
    "nt": (((1,), (1,)), ((), ())),
    "tn": (((0,), (0,)), ((), ())),
}


def _sds(shape, dtype):
    return jax.ShapeDtypeStruct(tuple(shape), dtype)


def _cparams(sem=None):
    kw = {"vmem_limit_bytes": VMEM_LIMIT}
    if sem is not None:
        kw["dimension_semantics"] = sem
    return pltpu.CompilerParams(**kw)


def _dot(a, b, dims):
    return lax.dot_general(a.astype(BF16), b.astype(BF16), _DN[dims], preferred_element_type=F32)


def _mm(name, dims, grid, a, a_blk, a_idx, b, b_blk, b_idx, out, out_blk, out_idx, acc_shape, alias=None, deps=()):
    nk = grid[2]

    def body(*refs):
        a_ref, b_ref = refs[:2]
        o_ref, acc_ref = refs[-2:]
        part = _dot(a_ref[...], b_ref[...], dims)
        if nk == 1:
            o_ref[...] = part.astype(o_ref.dtype)
            return
        k = pl.program_id(2)

        @pl.when(k == 0)
        def _():
            acc_ref[...] = part

        @pl.when((k > 0) & (k < nk - 1))
        def _():
            acc_ref[...] += part

        @pl.when(k == nk - 1)
        def _():
            o_ref[...] = (acc_ref[...] + part).astype(o_ref.dtype)

    in_specs = [pl.BlockSpec(a_blk, a_idx), pl.BlockSpec(b_blk, b_idx)]
    args = [a, b]
    kw = {}
    if alias is not None:
        in_specs.append(pl.BlockSpec(memory_space=pl.ANY))
        args.append(alias)
        kw["input_output_aliases"] = {2: 0}
    in_specs += [pl.BlockSpec(memory_space=pl.ANY)] * len(deps)
    args += list(deps)
    return pl.pallas_call(
        body,
        name=name,
        grid=grid,
        in_specs=in_specs,
        out_specs=pl.BlockSpec(out_blk, out_idx),
        out_shape=out,
        scratch_shapes=[pltpu.VMEM(acc_shape if nk > 1 else (8, 128), F32)],
        compiler_params=_cparams(("parallel", "parallel", "arbitrary")),
        **kw,
    )(*args)


def _rowwise(name, fn, tiled, full, out_tiled, out_red, tm, deps=()):
    rows = tiled[0].shape[-2]
    tm = min(tm, rows)
    grid = (rows // tm,)
    n_in = len(tiled) + len(full)
    n_t = len(out_tiled)
    n_dep = len(deps)

    def tspec(shape):
        nd = len(shape)
        return pl.BlockSpec(tuple(shape[:-2]) + (tm, shape[-1]), lambda i, nd=nd: (0,) * (nd - 2) + (i, 0))

    def fspec(shape):
        nd = len(shape)
        return pl.BlockSpec(tuple(shape), lambda i, nd=nd: (0,) * nd)

    def body(*refs):
        outs = fn(*[r[...] for r in refs[:n_in]])
        if not isinstance(outs, (tuple, list)):
            outs = (outs,)
        out_refs = refs[n_in + n_dep:]
        for r, v in zip(out_refs[:n_t], outs[:n_t]):
            r[...] = v.astype(r.dtype)
        red_refs = out_refs[n_t:]
        if red_refs:
            @pl.when(pl.program_id(0) == 0)
            def _():
                for r in red_refs:
                    r[...] = jnp.zeros_like(r)

            for r, v in zip(red_refs, outs[n_t:]):
                r[...] += v.astype(r.dtype)

    res = pl.pallas_call(
        body,
        name=name,
        grid=grid,
        in_specs=[tspec(a.shape) for a in tiled] + [fspec(a.shape) for a in full]
        + [pl.BlockSpec(memory_space=pl.ANY)] * n_dep,
        out_specs=[tspec(o.shape) for o in out_tiled] + [fspec(o.shape) for o in out_red],
        out_shape=list(out_tiled) + list(out_red),
        compiler_params=_cparams(("arbitrary",) if out_red else ("parallel",)),
    )(*tiled, *full, *deps)
    return res


def _sum0(v):
    return jnp.sum(v, axis=0, keepdims=True)


def _sigmoid(v):
    return 1.0 / (1.0 + jnp.exp(-v))


def _rms(x):
    r = lax.rsqrt(jnp.mean(x * x, axis=-1, keepdims=True) + EPS)
    return x * r, r


def _rms_bwd(u, xr, r):
    return r * (u - xr * jnp.mean(u * xr, axis=-1, keepdims=True))


class _Ready:
    def __init__(self, g):
        self.g = g

    def mid(self, after):
        return []

    def get(self, after):
        return self.g


def _normmod(x_, gamma_, sc_, sh_):
    return _rms(x_)[0] * gamma_ * (1.0 + sc_) + sh_


def _row_blocks(tm, size=256):
    size = min(size, tm)
    return [slice(r, r + size) for r in range(0, tm, size)]


def _loss_head(x_, t_, g_):
    xr, r = _rms(x_)
    err = xr * g_ - t_
    dy = err * (1.0 / D_MODEL)
    return _rms_bwd(dy * g_, xr, r), _sum0(err * err), _sum0(dy * xr)


def _ffn_fwd(tag, x, gamma, sh, sc, gate, gu_src, base, down_src, down_idx, mid_after, h_pre=None,
             next_norm=None, head=None):
    s_len = x.shape[0]
    if h_pre is None:
        (h,) = _rowwise(
            f"{tag}_normmod", _normmod, [x], [gamma, sc, sh], [_sds((s_len, D_MODEL), BF16)], [], ELEM_TILE
        )
        gdeps = gu_src.mid(h)
    else:
        h, gdeps = h_pre, gu_src.mid(mid_after)
    g4 = gu_src.get(h)

    tm = min(ROW_TILE, s_len)
    def gate_up(h_ref, wg_ref, wu_ref, *rest):
        ab_ref, s_ref = rest[-2:]
        wg, wu = wg_ref[...], wu_ref[...]
        for rows in _row_blocks(tm):
            hv = h_ref[rows, :]
            a = _dot(hv, wg, "nt")
            b = _dot(hv, wu, "nt")
            ab_ref[0, rows, :] = a.astype(ab_ref.dtype)
            ab_ref[1, rows, :] = b.astype(ab_ref.dtype)
            s_ref[rows, :] = (a * _sigmoid(a) * b).astype(s_ref.dtype)

    ab4, s3 = pl.pallas_call(
        gate_up,
        name=f"{tag}_gate_up",
        grid=(s_len // tm, N_CHIPS),
        in_specs=[
            pl.BlockSpec((tm, D_MODEL), lambda i, c: (i, 0)),
            pl.BlockSpec((None, None, F_SHARD, D_MODEL), lambda i, c: (c, base, 0, 0)),
            pl.BlockSpec((None, None, F_SHARD, D_MODEL), lambda i, c: (c, base + 1, 0, 0)),
        ] + [pl.BlockSpec(memory_space=pl.ANY)] * len(gdeps),
        out_specs=[
            pl.BlockSpec((None, 2, tm, F_SHARD), lambda i, c: (c, 0, i, 0)),
            pl.BlockSpec((None, tm, F_SHARD), lambda i, c: (c, i, 0)),
        ],
        out_shape=[_sds((N_CHIPS, 2, s_len, F_SHARD), BF16), _sds((N_CHIPS, s_len, F_SHARD), BF16)],
        compiler_params=_cparams(("parallel", "parallel")),
    )(h, g4, g4, *gdeps)
    if down_src is None:
        g_down, ddeps = g4, ()
    else:
        ddeps = down_src.mid(ab4)
        g_down = down_src.get(s3)

    y = _mm(
        f"{tag}_down", "nn", (s_len // tm, 1, N_CHIPS),
        s3, (None, tm, F_SHARD), lambda i, j, k: (k, i, 0),
        g_down, (None, None, F_SHARD, D_MODEL), lambda i, j, k: (k, down_idx, 0, 0),
        _sds((s_len, D_MODEL), F32), (tm, D_MODEL), lambda i, j, k: (i, 0),
        (tm, D_MODEL), deps=ddeps,
    )

    saved = (g4, base, g_down, down_idx, h, ab4, s3, y)
    act = _sds((s_len, D_MODEL), F32)
    if head is not None:
        target, norm_final = head
        vec = _sds((1, D_MODEL), F32)
        out = _rowwise(
            f"{tag}_residual_head", lambda x_, y_, t_, g_, nf_: _loss_head(x_ + 0.5 * g_ * y_, t_, nf_),
            [x, y, target], [gate, norm_final], [act], [vec, vec], ELEM_TILE,
        )
    elif next_norm is not None:
        def residual_norm(x_, y_, g_, gamma_, sc_, sh_):
            x_out = x_ + 0.5 * g_ * y_
            return x_out, _normmod(x_out, gamma_, sc_, sh_)

        out = _rowwise(
            f"{tag}_residual_norm", residual_norm, [x, y], [gate] + list(next_norm),
            [act, _sds((s_len, D_MODEL), BF16)], [], ELEM_TILE,
        )
    else:
        out = _rowwise(f"{tag}_residual", lambda x_, y_, g_: x_ + 0.5 * g_ * y_, [x, y], [gate], [act], [], ELEM_TILE)
    return out, saved


def _normmod_bwd_call(name, dh_parts, x, dxo, gamma, sc, deps=()):
    s_len = x.shape[0]
    n_parts = len(dh_parts)

    def fn(*vals):
        dh = vals[0]
        for extra in vals[1:n_parts]:
            dh = dh + extra
        x_, dxo_, gamma_, sc_ = vals[n_parts:]
        xr, r = _rms(x_)
        n = xr * gamma_
        dn = dh * (1.0 + sc_)
        dx = dxo_ + _rms_bwd(dn * gamma_, xr, r)
        return dx, _sum0(dh * n), _sum0(dh), _sum0(dn * xr)

    vec = _sds((1, D_MODEL), F32)
    return _rowwise(
        name, fn, list(dh_parts) + [x, dxo], [gamma, sc], [_sds((s_len, D_MODEL), F32)], [vec, vec, vec], ELEM_TILE, deps=deps
    )


class _BwdHooks:
    def __init__(self, after_gate=None, after_swiglu=None, after_wdown=None, after_wgu=None, after_dh=None):
        def nothing(_):
            return []

        self.after_gate = after_gate or nothing
        self.after_swiglu = after_swiglu or nothing
        self.after_wdown = after_wdown or nothing
        self.after_wgu = after_wgu or nothing
        self.after_dh = after_dh or nothing


def _ffn_bwd(tag, dxo, x, gamma, sc, gate, saved, hooks, deps=()):
    g4, base, g_down, down_idx, h, ab4, s3, y = saved
    s_len = x.shape[0]
    vec = _sds((1, D_MODEL), F32)

    dy, dgate = _rowwise(
        f"{tag}_bwd_gate", lambda dxo_, y_, g_: (0.5 * g_ * dxo_, _sum0(0.5 * y_ * dxo_)),
        [dxo, y], [gate], [_sds((s_len, D_MODEL), BF16)], [vec], ELEM_TILE, deps=deps,
    )

    tm = min(ROW_TILE, s_len)

    def d_gate_up(dy_ref, wd_ref, ab_ref, o_ref):
        wd = wd_ref[...]
        for rows in _row_blocks(tm):
            ds = _dot(dy_ref[rows, :], wd, "nt")
            a = ab_ref[0, rows, :].astype(F32)
            b = ab_ref[1, rows, :].astype(F32)
            sig = _sigmoid(a)
            o_ref[0, rows, :] = (ds * b * sig * (1.0 + a * (1.0 - sig))).astype(o_ref.dtype)
            o_ref[1, rows, :] = (ds * a * sig).astype(o_ref.dtype)

    ab_spec = pl.BlockSpec((None, 2, tm, F_SHARD), lambda i, c: (c, 0, i, 0))
    dab4 = pl.pallas_call(
        d_gate_up,
        name=f"{tag}_bwd_dgate_up",
        grid=(s_len // tm, N_CHIPS),
        in_specs=[
            pl.BlockSpec((tm, D_MODEL), lambda i, c: (i, 0)),
            pl.BlockSpec((None, None, F_SHARD, D_MODEL), lambda i, c: (c, down_idx, 0, 0)),
            ab_spec,
        ],
        out_specs=ab_spec,
        out_shape=_sds(ab4.shape, BF16),
        compiler_params=_cparams(("parallel", "parallel")),
    )(dy, g_down, ab4)

    tk = tm
    gb_down = _mm(
        f"{tag}_bwd_wdown", "tn", (N_CHIPS, 1, s_len // tk),
        s3, (None, tk, F_SHARD), lambda i, j, k: (i, k, 0),
        dy, (tk, D_MODEL), lambda i, j, k: (k, 0),
        _sds((N_CHIPS, 1, F_SHARD, D_MODEL), BF16), (None, None, F_SHARD, D_MODEL), lambda i, j, k: (i, 0, 0, 0),
        (F_SHARD, D_MODEL), deps=hooks.after_swiglu(dab4),
    )
    gb_gu = _mm(
        f"{tag}_bwd_wgu", "tn", (2 * N_CHIPS, 1, s_len // tk),
        dab4, (None, None, tk, F_SHARD), lambda i, j, k: (i % N_CHIPS, i // N_CHIPS, k, 0),
        h, (tk, D_MODEL), lambda i, j, k: (k, 0),
        _sds((N_CHIPS, 2, F_SHARD, D_MODEL), BF16), (None, None, F_SHARD, D_MODEL),
        lambda i, j, k: (i % N_CHIPS, i // N_CHIPS, 0, 0),
        (F_SHARD, D_MODEL), deps=hooks.after_wdown(gb_down),
    )
    assert base % 2 == 0
    dh_deps = hooks.after_wgu(gb_gu)

    def d_h(dab_ref, w_ref, *rest):
        o_ref, acc_ref = rest[-2:]
        c = pl.program_id(1)
        part = _dot(dab_ref[0], w_ref[0], "nn") + _dot(dab_ref[1], w_ref[1], "nn")

        @pl.when(c == 0)
        def _():
            acc_ref[...] = part

        @pl.when((c > 0) & (c < N_CHIPS - 1))
        def _():
            acc_ref[...] += part

        @pl.when(c == N_CHIPS - 1)
        def _():
            o_ref[...] = acc_ref[...] + part

    dh = pl.pallas_call(
        d_h,
        name=f"{tag}_bwd_dh",
        grid=(s_len // tm, N_CHIPS),
        in_specs=[
            pl.BlockSpec((None, 2, tm, F_SHARD), lambda i, c: (c, 0, i, 0)),
            pl.BlockSpec((None, 2, F_SHARD, D_MODEL), lambda i, c: (c, base // 2, 0, 0)),
        ] + [pl.BlockSpec(memory_space=pl.ANY)] * len(dh_deps),
        out_specs=pl.BlockSpec((tm, D_MODEL), lambda i, c: (i, 0)),
        out_shape=_sds((s_len, D_MODEL), F32),
        scratch_shapes=[pltpu.VMEM((tm, D_MODEL), F32)],
        compiler_params=_cparams(("parallel", "arbitrary")),
    )(dab4, g4, *dh_deps)
    dx, dsc, dsh, dgamma = _normmod_bwd_call(
        f"{tag}_bwd_normmod", [dh], x, dxo, gamma, sc, deps=hooks.after_dh(dh)
    )
    return dx, (gb_down, gb_gu), (dsh, dsc, dgate, dgamma)


def _bucket_map():
    qi = np.arange(WINDOW)[:, None]
    kj = np.arange(2 * WINDOW)[None, :]
    dist = qi + WINDOW - kj
    band = (dist >= 0) & (dist < WINDOW)
    max_exact = NUM_BUCKETS // 2
    n = np.maximum(dist, 0)
    large = []
    for dt in (np.float32, np.float64):
        nf = np.maximum(n, 1).astype(dt)
        val = np.log(nf / dt(max_exact)) / dt(math.log(MAX_DISTANCE / max_exact)) * dt(NUM_BUCKETS - max_exact)
        large.append(np.minimum(max_exact + val.astype(np.int32), NUM_BUCKETS - 1))
    assert np.array_equal(large[0], large[1])
    bucket = np.where(n < max_exact, n, large[0])
    return np.where(band, bucket, -1).astype(np.int32).reshape(1, -1)


def _bias_expand(rel_bias_t, bkt):
    n = bkt.shape[1]

    def body(rb_ref, bkt_ref, o_ref):
        onehot = (lax.broadcasted_iota(jnp.int32, (NUM_BUCKETS, n), 0) == bkt_ref[...]).astype(F32)
        o_ref[...] = lax.dot_general(
            rb_ref[...], onehot, _DN["nn"], precision=lax.Precision.HIGHEST, preferred_element_type=F32
        )

    return pl.pallas_call(
        body, name="bias_expand", out_shape=_sds((SWA_HEADS, n), F32), compiler_params=_cparams()
    )(rel_bias_t, bkt)


def _bias_reduce(dbias_flat, bkt):
    n = bkt.shape[1]

    def body(db_ref, bkt_ref, o_ref):
        onehot = (lax.broadcasted_iota(jnp.int32, (NUM_BUCKETS, n), 0) == bkt_ref[...]).astype(F32)
        o_ref[...] = lax.dot_general(
            db_ref[...], onehot, _DN["nt"], precision=lax.Precision.HIGHEST, preferred_element_type=F32
        )

    return pl.pallas_call(
        body, name="bias_reduce", out_shape=_sds((SWA_HEADS, NUM_BUCKETS), F32), compiler_params=_cparams()
    )(dbias_flat, bkt)


_SWA_SCALE = SWA_HEAD_DIM ** -0.5
_NEG = -1e30


_SWA_PAIR = 4


def _swa_in_specs():
    w = WINDOW
    n_q = SWA_HEADS * SWA_HEAD_DIM
    n_kv = 2 * SWA_KV_HEADS * SWA_HEAD_DIM
    prev = lambda m: jnp.maximum(_SWA_PAIR * m - 1, 0)
    return [
        pl.BlockSpec((_SWA_PAIR * w, n_q), lambda m: (m, 0)),
        pl.BlockSpec((w, n_kv), lambda m: (prev(m), n_q // n_kv)),
        pl.BlockSpec((_SWA_PAIR * w, n_kv), lambda m: (m, n_q // n_kv)),
        pl.BlockSpec((SWA_HEADS, w, 2 * w), lambda m: (0, 0, 0)),
        pl.BlockSpec((SWA_HEADS, w, 1), lambda m: (0, 0, 0)),
    ]


def _head_cols(v, first, count):
    hd = SWA_HEAD_DIM
    return jnp.stack([v[:, (first + i) * hd:(first + i + 1) * hd] for i in range(count)])


def _swa_blocks(q_ref, kvp_ref, kvc_ref, m):
    g, w, hd = SWA_GROUP, WINDOW, SWA_HEAD_DIM
    kv_all = jnp.concatenate([kvp_ref[...], kvc_ref[...]], axis=0).astype(F32)
    blocks = []
    for sub in range(_SWA_PAIR):
        rows = slice(sub * w, (sub + 1) * w)
        q = q_ref[rows, :].astype(F32)
        kv = kv_all[sub * w:(sub + 2) * w]
        heads = []
        for j in range(SWA_KV_HEADS):
            qj = _head_cols(q, g * j, g).reshape(g * w, hd)
            heads.append((qj, _head_cols(kv, j, 1)[0], _head_cols(kv, SWA_KV_HEADS + j, 1)[0]))
        blocks.append((_SWA_PAIR * m + sub, rows, heads))
    return blocks


def _swa_scores(q, kk, bias, n):
    g, w = SWA_GROUP, WINDOW
    s = (_dot(q, kk, "nt") * _SWA_SCALE).reshape(g, w, 2 * w) + bias
    qi = lax.broadcasted_iota(jnp.int32, (w, 2 * w), 0)
    kj = lax.broadcasted_iota(jnp.int32, (w, 2 * w), 1)
    dist = qi + w - kj
    valid = ((dist >= 0) & (dist < w) & ((n > 0) | (kj >= w)))[None]
    return jnp.where(valid, s, _NEG), valid


def _swa_fwd(swa2, bias, sinkb):
    s_len = swa2.shape[0]
    g, w, hd = SWA_GROUP, WINDOW, SWA_HEAD_DIM

    def body(q_ref, kvp_ref, kvc_ref, bias_ref, sink_ref, o_ref, lse_ref):
        for n, rows, heads in _swa_blocks(q_ref, kvp_ref, kvc_ref, pl.program_id(0)):
            outs = []
            for j, (q, kk, vv) in enumerate(heads):
                hs = slice(g * j, g * (j + 1))
                s, valid = _swa_scores(q, kk, bias_ref[hs], n)
                sink = sink_ref[hs]
                m = jnp.maximum(jnp.max(s, axis=-1, keepdims=True), sink)
                p = jnp.where(valid, jnp.exp(s - m), 0.0)
                l = jnp.sum(p, axis=-1, keepdims=True) + jnp.exp(sink - m)
                o = _dot(p.reshape(g * w, 2 * w), vv, "nn").reshape(g, w, hd) / l
                outs += [o[i] for i in range(g)]
                lse_ref[hs, rows, :] = m + jnp.log(l)
            o_ref[rows, :] = jnp.concatenate(outs, axis=-1).astype(o_ref.dtype)

    n_q = SWA_HEADS * hd
    return pl.pallas_call(
        body,
        name="swa_fwd",
        grid=(s_len // (_SWA_PAIR * w),),
        in_specs=_swa_in_specs(),
        out_specs=[
            pl.BlockSpec((_SWA_PAIR * w, n_q), lambda m: (m, 0)),
            pl.BlockSpec((SWA_HEADS, _SWA_PAIR * w, 1), lambda m: (0, m, 0)),
        ],
        out_shape=[_sds((s_len, n_q), BF16), _sds((SWA_HEADS, s_len, 1), F32)],
        compiler_params=_cparams(("parallel",)),
    )(swa2, swa2, swa2, bias, sinkb)


def _swa_bwd(swa2, bias, sinkb, cat, dcat, lse):
    s_len = swa2.shape[0]
    g, w, hd = SWA_GROUP, WINDOW, SWA_HEAD_DIM

    def body(q_ref, kvp_ref, kvc_ref, bias_ref, sink_ref, o_ref, do_ref, lse_ref,
             dq_ref, dkva_ref, dkvb_ref, dbias_ref, dsink_ref):
        step = pl.program_id(0)

        @pl.when(step == 0)
        def _():
            dbias_ref[...] = jnp.zeros_like(dbias_ref)
            dsink_ref[...] = jnp.zeros_like(dsink_ref)

        for n, rows, heads in _swa_blocks(q_ref, kvp_ref, kvc_ref, step):
            o_all = o_ref[rows, :].astype(F32)
            do_all = do_ref[rows, :].astype(F32)
            dqs, dks, dvs = [], [], []
            for j, (q, kk, vv) in enumerate(heads):
                hs = slice(g * j, g * (j + 1))
                s, valid = _swa_scores(q, kk, bias_ref[hs], n)
                lse_v = lse_ref[hs, rows, :]
                p = jnp.where(valid, jnp.exp(s - lse_v), 0.0)
                do = _head_cols(do_all, g * j, g)
                delta = jnp.sum(do * _head_cols(o_all, g * j, g), axis=-1, keepdims=True)
                do2 = do.reshape(g * w, hd)
                dp = _dot(do2, vv, "nt").reshape(g, w, 2 * w)
                ds = p * (dp - delta)
                dbias_ref[hs] += ds
                dsink_ref[hs] -= jnp.exp(sink_ref[hs] - lse_v) * delta
                ds2 = ds.reshape(g * w, 2 * w)
                dq = (_dot(ds2, kk, "nn") * _SWA_SCALE).reshape(g, w, hd)
                dqs += [dq[i] for i in range(g)]
                dks.append(_dot(ds2, q, "tn") * _SWA_SCALE)
                dvs.append(_dot(p.reshape(g * w, 2 * w), do2, "tn"))
            dq_ref[rows, :] = jnp.concatenate(dqs, axis=-1).astype(dq_ref.dtype)
            dkv = jnp.concatenate(dks + dvs, axis=-1)
            dkvb_ref[rows, :] = dkv[:w]
            dkva_ref[rows, :] = dkv[w:]

    n_q = SWA_HEADS * hd
    n_kv = 2 * SWA_KV_HEADS * hd
    q_spec = pl.BlockSpec((_SWA_PAIR * w, n_q), lambda m: (m, 0))
    kv_spec = pl.BlockSpec((_SWA_PAIR * w, n_kv), lambda m: (m, 0))
    return pl.pallas_call(
        body,
        name="swa_bwd",
        grid=(s_len // (_SWA_PAIR * w),),
        in_specs=_swa_in_specs() + [q_spec, q_spec, pl.BlockSpec((SWA_HEADS, _SWA_PAIR * w, 1), lambda m: (0, m, 0))],
        out_specs=[
            q_spec, kv_spec, kv_spec,
            pl.BlockSpec((SWA_HEADS, w, 2 * w), lambda n: (0, 0, 0)),
            pl.BlockSpec((SWA_HEADS, w, 1), lambda n: (0, 0, 0)),
        ],
        out_shape=[
            _sds((s_len, n_q), BF16), _sds((s_len, n_kv), F32), _sds((s_len, n_kv), F32),
            _sds((SWA_HEADS, w, 2 * w), F32), _sds((SWA_HEADS, w, 1), F32),
        ],
        compiler_params=_cparams(("arbitrary",)),
    )(swa2, swa2, swa2, bias, sinkb, cat, dcat, lse)


_MLA_SCALE = MLA_QK ** -0.5
_MLA_TQ = 256
_MLA_FWD_HEADS = 4
_MLA_BWD_HEADS = 2


def _mla_mask(i, tq, n_keys):
    row = i * tq + lax.broadcasted_iota(jnp.int32, (tq, n_keys), 0)
    col = lax.broadcasted_iota(jnp.int32, (tq, n_keys), 1)
    return col <= row


def _per_query_block(i, n_blocks, fn):
    for ii in range(n_blocks):
        pl.when(i == ii)(functools.partial(fn, ii))


def _mla_fwd(q4, k4, v4):
    s_len = q4.shape[1]
    tq = min(_MLA_TQ, s_len)
    pair = _MLA_FWD_HEADS

    def body(q_ref, k_ref, v_ref, o_ref, lse_ref):
        def block(ii):
            n_keys = (ii + 1) * tq
            mask = _mla_mask(ii, tq, n_keys)
            for hh in range(pair):
                s = jnp.where(mask, _dot(q_ref[hh], k_ref[hh, :n_keys, :], "nt") * _MLA_SCALE, _NEG)
                m = jnp.max(s, axis=-1, keepdims=True)
                p = jnp.exp(s - m)
                l = jnp.sum(p, axis=-1, keepdims=True)
                o_ref[:, hh * MLA_V:(hh + 1) * MLA_V] = (_dot(p, v_ref[hh, :n_keys, :], "nn") / l).astype(o_ref.dtype)
                lse_ref[hh] = m + jnp.log(l)

        _per_query_block(pl.program_id(1), s_len // tq, block)

    return pl.pallas_call(
        body,
        name="mla_fwd",
        grid=(MLA_HEADS // pair, s_len // tq),
        in_specs=[
            pl.BlockSpec((pair, tq, MLA_QK), lambda h, i: (h, i, 0)),
            pl.BlockSpec((pair, s_len, MLA_QK), lambda h, i: (h, 0, 0)),
            pl.BlockSpec((pair, s_len, MLA_V), lambda h, i: (h, 0, 0)),
        ],
        out_specs=[
            pl.BlockSpec((tq, pair * MLA_V), lambda h, i: (i, h)),
            pl.BlockSpec((pair, tq, 1), lambda h, i: (h, i, 0)),
        ],
        out_shape=[_sds((s_len, MLA_HEADS * MLA_V), BF16), _sds((MLA_HEADS, s_len, 1), F32)],
        compiler_params=_cparams(("parallel", "parallel")),
    )(q4, k4, v4)


def _mla_bwd(q4, k4, v4, cat, dcat, lse):
    s_len = q4.shape[1]
    tq = min(_MLA_TQ, s_len)
    pair = _MLA_BWD_HEADS
    first = SWA_HEADS * SWA_HEAD_DIM // (pair * MLA_V)

    def body(q_ref, k_ref, v_ref, o_ref, do_ref, lse_ref, dq_ref, dk_ref, dv_ref):
        i = pl.program_id(1)

        @pl.when(i == 0)
        def _():
            dk_ref[...] = jnp.zeros_like(dk_ref)
            dv_ref[...] = jnp.zeros_like(dv_ref)

        def block(ii):
            n_keys = (ii + 1) * tq
            mask = _mla_mask(ii, tq, n_keys)
            for hh in range(pair):
                cols = slice(hh * MLA_V, (hh + 1) * MLA_V)
                q, k, v, do = q_ref[hh], k_ref[hh, :n_keys, :], v_ref[hh, :n_keys, :], do_ref[:, cols]
                s = jnp.where(mask, _dot(q, k, "nt") * _MLA_SCALE, _NEG)
                p = jnp.where(mask, jnp.exp(s - lse_ref[hh]), 0.0)
                delta = jnp.sum(do.astype(F32) * o_ref[:, cols].astype(F32), axis=-1, keepdims=True)
                ds = (p * (_dot(do, v, "nt") - delta) * _MLA_SCALE).astype(BF16)
                dq_ref[hh] = _dot(ds, k, "nn")
                dk_ref[hh, :n_keys, :] += _dot(ds, q, "tn")
                dv_ref[hh, :n_keys, :] += _dot(p, do, "tn")

        _per_query_block(i, s_len // tq, block)

    return pl.pallas_call(
        body,
        name="mla_bwd",
        grid=(MLA_HEADS // pair, s_len // tq),
        in_specs=[
            pl.BlockSpec((pair, tq, MLA_QK), lambda h, i: (h, i, 0)),
            pl.BlockSpec((pair, s_len, MLA_QK), lambda h, i: (h, 0, 0)),
            pl.BlockSpec((pair, s_len, MLA_V), lambda h, i: (h, 0, 0)),
            pl.BlockSpec((tq, pair * MLA_V), lambda h, i: (i, first + h)),
            pl.BlockSpec((tq, pair * MLA_V), lambda h, i: (i, first + h)),
            pl.BlockSpec((pair, tq, 1), lambda h, i: (h, i, 0)),
        ],
        out_specs=[
            pl.BlockSpec((pair, tq, MLA_QK), lambda h, i: (h, i, 0)),
            pl.BlockSpec((pair, s_len, MLA_QK), lambda h, i: (h, 0, 0)),
            pl.BlockSpec((pair, s_len, MLA_V), lambda h, i: (h, 0, 0)),
        ],
        out_shape=[
            _sds((MLA_HEADS, s_len, MLA_QK), F32),
            _sds((MLA_HEADS, s_len, MLA_QK), F32),
            _sds((MLA_HEADS, s_len, MLA_V), F32),
        ],
        compiler_params=_cparams(("parallel", "arbitrary")),
    )(q4, k4, v4, cat, dcat, lse)


def _mla_split(pm):
    q_lat = pm[:, :MLA_Q_RANK]
    kv_lat = pm[:, MLA_Q_RANK:MLA_Q_RANK + MLA_KV_RANK]
    kr = pm[:, MLA_Q_RANK + MLA_KV_RANK:MLA_Q_RANK + MLA_KV_RANK + MLA_ROPE]
    kr_sw = pm[:, MLA_Q_RANK + MLA_KV_RANK + MLA_ROPE:]
    return q_lat, kv_lat, kr, kr_sw


def _mla_proj(pm, cos2, sin2, q_norm, kv_norm, wq_ext, wkv):
    s_len = pm.shape[0]

    def fn(pm_, cos_, sin_, qg, kvg, wq, wk):
        q_lat, kv_lat, kr, kr_sw = _mla_split(pm_)
        nq = (_rms(q_lat)[0] * qg).astype(BF16)
        nkv = (_rms(kv_lat)[0] * kvg).astype(BF16)
        k_rot = kr * cos_ + kr_sw * sin_
        qs, ks, vs = [], [], []
        for h in range(MLA_HEADS):
            qe = _dot(nq, wq[h], "nt")
            q_rot = qe[:, MLA_NOPE:MLA_QK] * cos_ + qe[:, MLA_QK:] * sin_
            qs.append(jnp.concatenate([qe[:, :MLA_NOPE], q_rot], axis=-1))
            kve = _dot(nkv, wk[h], "nt")
            ks.append(jnp.concatenate([kve[:, :MLA_NOPE], k_rot], axis=-1))
            vs.append(kve[:, MLA_NOPE:])
        return jnp.stack(qs), jnp.stack(ks), jnp.stack(vs)

    return _rowwise(
        "mla_proj", fn, [pm, cos2, sin2], [q_norm, kv_norm, wq_ext, wkv],
        [_sds((MLA_HEADS, s_len, MLA_QK), BF16), _sds((MLA_HEADS, s_len, MLA_QK), BF16),
         _sds((MLA_HEADS, s_len, MLA_V), BF16)], [], ELEM_TILE,
    )


def _mla_proj_bwd(pm, cos2, sin2, dq4, dk4, dv4, q_norm, kv_norm, wq_ext, wkv):
    s_len = pm.shape[0]

    def fn(pm_, cos_, sin_, dq, dk, dv, qg, kvg, wq, wk):
        q_lat, kv_lat, _, _ = _mla_split(pm_)
        xq, rq = _rms(q_lat)
        xkv, rkv = _rms(kv_lat)
        nq = (xq * qg).astype(BF16)
        nkv = (xkv * kvg).astype(BF16)
        dnq = jnp.zeros_like(q_lat)
        dnkv = jnp.zeros_like(kv_lat)
        dkr = jnp.zeros_like(cos_)
        dwq, dwk = [], []
        for h in range(MLA_HEADS):
            dy = dq[h][:, MLA_NOPE:]
            dqe = jnp.concatenate([dq[h][:, :MLA_NOPE], dy * cos_, dy * sin_], axis=-1).astype(BF16)
            dnq = dnq + _dot(dqe, wq[h], "nn")
            dwq.append(_dot(dqe, nq, "tn"))
            dkve = jnp.concatenate([dk[h][:, :MLA_NOPE], dv[h]], axis=-1).astype(BF16)
            dnkv = dnkv + _dot(dkve, wk[h], "nn")
            dwk.append(_dot(dkve, nkv, "tn"))
            dkr = dkr + dk[h][:, MLA_NOPE:]
        dq_lat = _rms_bwd(dnq * qg, xq, rq)
        dkv_lat = _rms_bwd(dnkv * kvg, xkv, rkv)
        dpm = jnp.concatenate([dq_lat, dkv_lat, dkr * cos_, dkr * sin_], axis=-1)
        return dpm, _sum0(dnq * xq), _sum0(dnkv * xkv), jnp.stack(dwq), jnp.stack(dwk)

    return _rowwise(
        "mla_proj_bwd", fn, [pm, cos2, sin2, dq4, dk4, dv4], [q_norm, kv_norm, wq_ext, wkv],
        [_sds((s_len, N_MLA_COLS), BF16)],
        [_sds((1, MLA_Q_RANK), F32), _sds((1, MLA_KV_RANK), F32),
         _sds(wq_ext.shape, F32), _sds(wkv.shape, F32)], ELEM_TILE,
    )


def _rope_tables(s_len):
    inv = ROPE_THETA ** (-jnp.arange(0, MLA_ROPE, 2, dtype=F32) / MLA_ROPE)
    ang = jnp.arange(s_len, dtype=F32)[:, None] * inv[None, :]
    cos, sin = jnp.cos(ang), jnp.sin(ang)
    return jnp.concatenate([cos, cos], axis=-1), jnp.concatenate([-sin, sin], axis=-1)


def _mix_weights(g_mix):
    rest = g_mix[:, 0]
    w_in_t = rest[:, O_IN:O_IN + R_IN].reshape(D_IN, D_MODEL)
    w_o = rest[:, O_O:O_O + R_O].reshape(D_MODEL, D_MODEL)
    w_uq_t = rest[:, O_UQ:O_UQ + R_UQ].reshape(MLA_HEADS, MLA_QK, MLA_Q_RANK)
    w_ukv_t = rest[:, O_UKV:O_UKV + R_UKV].reshape(MLA_HEADS, MLA_NOPE + MLA_V, MLA_KV_RANK)
    half = MLA_ROPE // 2
    w_swa = w_in_t[:N_SWA_COLS]
    w_mla = jnp.concatenate([w_in_t[N_SWA_COLS:], w_in_t[D_IN - half:], w_in_t[D_IN - MLA_ROPE:D_IN - half]], axis=0)
    wq_ext = jnp.concatenate([w_uq_t, w_uq_t[:, MLA_QK - half:], w_uq_t[:, MLA_NOPE:MLA_QK - half]], axis=1)
    return w_swa, w_mla, w_o, wq_ext, w_ukv_t


def _mix_fwd(x, h, gate, mix_src, q_norm, kv_norm, bias, sinkb, cos2, sin2, next_norm):
    s_len = x.shape[0]
    tm = min(ROW_TILE, s_len)
    deps = mix_src.mid(x)
    weights = _mix_weights(mix_src.get(h))
    w_swa, w_mla, w_o, wq_ext, wkv = weights
    swa2 = _mm(
        "mix_proj_swa", "nt", (s_len // tm, 1, 1),
        h, (tm, D_MODEL), lambda i, j, k: (i, 0),
        w_swa, (N_SWA_COLS, D_MODEL), lambda i, j, k: (0, 0),
        _sds((s_len, N_SWA_COLS), BF16), (tm, N_SWA_COLS), lambda i, j, k: (i, 0),
        (tm, N_SWA_COLS), deps=deps,
    )
    pm = _mm(
        "mix_proj_mla", "nt", (s_len // tm, 1, 1),
        h, (tm, D_MODEL), lambda i, j, k: (i, 0),
        w_mla, (N_MLA_COLS, D_MODEL), lambda i, j, k: (0, 0),
        _sds((s_len, N_MLA_COLS), F32), (tm, N_MLA_COLS), lambda i, j, k: (i, 0),
        (tm, N_MLA_COLS),
    )
    q4, k4, v4 = _mla_proj(pm, cos2, sin2, q_norm, kv_norm, wq_ext, wkv)
    oa, lse_a = _swa_fwd(swa2, bias, sinkb)
    ob, lse_b = _mla_fwd(q4, k4, v4)
    cat = jnp.concatenate([oa, ob], axis=1)
    z = _mm(
        "mix_out", "nn", (s_len // tm, 1, 1),
        cat, (tm, D_MODEL), lambda i, j, k: (i, 0),
        w_o, (D_MODEL, D_MODEL), lambda i, j, k: (0, 0),
        _sds((s_len, D_MODEL), F32), (tm, D_MODEL), lambda i, j, k: (i, 0),
        (tm, D_MODEL),
    )
    def residual_norm(x_, z_, g_, gamma_, sc_, sh_):
        x_out = x_ + g_ * z_
        return x_out, _normmod(x_out, gamma_, sc_, sh_)

    out = _rowwise(
        "mix_residual_norm", residual_norm, [x, z], [gate] + list(next_norm),
        [_sds((s_len, D_MODEL), F32), _sds((s_len, D_MODEL), BF16)], [], ELEM_TILE,
    )
    return out, (weights, h, swa2, pm, q4, k4, v4, cat, lse_a, lse_b, z)


def _mix_bwd(dxo, x, gamma, sc, gate, q_norm, kv_norm, bias, sinkb, cos2, sin2, saved, hooks):
    weights, h, swa2, pm, q4, k4, v4, cat, lse_a, lse_b, z = saved
    w_swa, w_mla, w_o, wq_ext, wkv = weights
    s_len = x.shape[0]
    tm = tk = min(ROW_TILE, s_len)
    vec = _sds((1, D_MODEL), F32)
    n_proj = N_SWA_COLS + N_MLA_COLS
    half_d = D_MODEL // 2

    dz, dgate = _rowwise(
        "mix_bwd_gate", lambda dxo_, z_, g_: (g_ * dxo_, _sum0(z_ * dxo_)),
        [dxo, z], [gate], [_sds((s_len, D_MODEL), BF16)], [vec], ELEM_TILE,
    )
    dw_o = _mm(
        "mix_bwd_wo", "tn", (2, 1, s_len // tk),
        cat, (tk, half_d), lambda i, j, k: (k, i),
        dz, (tk, D_MODEL), lambda i, j, k: (k, 0),
        _sds((D_MODEL, D_MODEL), F32), (half_d, D_MODEL), lambda i, j, k: (i, 0),
        (half_d, D_MODEL),
    )
    dcat = _mm(
        "mix_bwd_dcat", "nt", (s_len // tm, 1, 1),
        dz, (tm, D_MODEL), lambda i, j, k: (i, 0),
        w_o, (D_MODEL, D_MODEL), lambda i, j, k: (0, 0),
        _sds((s_len, D_MODEL), BF16), (tm, D_MODEL), lambda i, j, k: (i, 0),
        (tm, D_MODEL), deps=hooks.after_gate(dz),
    )
    dq_a, dkva, dkvb, dbias, dsink = _swa_bwd(swa2, bias, sinkb, cat, dcat, lse_a)
    dq4, dk4, dv4 = _mla_bwd(q4, k4, v4, cat, dcat, lse_b)
    dpm, dq_norm, dkv_norm, dwq_ext, dwkv = _mla_proj_bwd(pm, cos2, sin2, dq4, dk4, dv4, q_norm, kv_norm, wq_ext, wkv)

    dkv = dkva + jnp.concatenate([dkvb[WINDOW:], jnp.zeros_like(dkvb[:WINDOW])], axis=0)
    dproj = jnp.concatenate([dq_a, dkv.astype(BF16), dpm], axis=1)
    w_proj = jnp.concatenate([w_swa, w_mla], axis=0)

    dw_proj = _mm(
        "mix_bwd_win", "tn", (n_proj // 256, 1, s_len // tk),
        dproj, (tk, 256), lambda i, j, k: (k, i),
        h, (tk, D_MODEL), lambda i, j, k: (k, 0),
        _sds((n_proj, D_MODEL), F32), (256, D_MODEL), lambda i, j, k: (i, 0),
        (256, D_MODEL),
    )
    dw_swa, dw_mla = dw_proj[:N_SWA_COLS], dw_proj[N_SWA_COLS:]
    dh = _mm(
        "mix_bwd_dh", "nn", (s_len // tm, 1, 1),
        dproj, (tm, n_proj), lambda i, j, k: (i, 0),
        w_proj, (n_proj, D_MODEL), lambda i, j, k: (0, 0),
        _sds((s_len, D_MODEL), F32), (tm, D_MODEL), lambda i, j, k: (i, 0),
        (tm, D_MODEL),
    )
    dx, dsc, dsh, dgamma = _normmod_bwd_call("mix_bwd_normmod", [dh], x, dxo, gamma, sc)

    half = MLA_ROPE // 2
    n_mla_in = D_IN - N_SWA_COLS
    dw_mla_base = dw_mla[:n_mla_in]
    dw_mla_base = dw_mla_base.at[n_mla_in - half:].add(dw_mla[n_mla_in:n_mla_in + half])
    dw_mla_base = dw_mla_base.at[n_mla_in - MLA_ROPE:n_mla_in - half].add(dw_mla[n_mla_in + half:])
    dw_in_t = jnp.concatenate([dw_swa, dw_mla_base], axis=0)
    dw_uq_t = dwq_ext[:, :MLA_QK]
    dw_uq_t = dw_uq_t.at[:, MLA_QK - half:].add(dwq_ext[:, MLA_QK:MLA_QK + half])
    dw_uq_t = dw_uq_t.at[:, MLA_NOPE:MLA_QK - half].add(dwq_ext[:, MLA_QK + half:])
    rest = jnp.concatenate(
        [
            dw_in_t.reshape(N_CHIPS, R_IN, D_MODEL),
            dw_o.reshape(N_CHIPS, R_O, D_MODEL),
            dw_uq_t.reshape(N_CHIPS, R_UQ, D_MODEL),
            dwkv.reshape(N_CHIPS, R_UKV, D_MODEL),
            jnp.zeros((N_CHIPS, F_SHARD - O_PAD, D_MODEL), F32),
        ],
        axis=1,
    ).astype(BF16)
    return dx, rest, (dsh, dsc, dgate, dgamma), dq_norm, dkv_norm, dbias, dsink


def _device_step(x, target, mod, gu1_src, down1_src, mix_src, ffn2_src, norms, q_norm, kv_norm, sinks, rel_bias,
                 hooks2=None, hooks_mix=None, mix_done=None, hooks1=None):
    s_len = x.shape[0]
    sh1, sc1, g1, sh2, sc2, g2, sh3, sc3, g3 = [mod[:, i * D_MODEL:(i + 1) * D_MODEL] for i in range(N_MOD)]
    n1, n2, n3, nf = norms
    bkt = jnp.asarray(_bucket_map())
    bias = _bias_expand(rel_bias.T, bkt).reshape(SWA_HEADS, WINDOW, 2 * WINDOW)
    sinkb = jnp.broadcast_to(sinks.reshape(SWA_HEADS, 1, 1), (SWA_HEADS, WINDOW, 1))
    cos2, sin2 = _rope_tables(s_len)

    (x1, h2), saved1 = _ffn_fwd(
        "ffn1", x, n1, sh1, sc1, g1, gu1_src, 0, down1_src, 0, sh1, next_norm=(n2, sc2, sh2)
    )
    (x2, h3), saved2 = _mix_fwd(
        x1, h2, g2, mix_src, q_norm, kv_norm, bias, sinkb, cos2, sin2, next_norm=(n3, sc3, sh3)
    )
    (dx3, sq, dnf), saved3 = _ffn_fwd(
        "ffn2", x2, n3, sh3, sc3, g3, ffn2_src, 0, None, 2, x2, h_pre=h3, head=(target, nf)
    )
    loss_part = (0.5 / D_MODEL) * jnp.sum(sq)

    dx2, gb2, (dsh3, dsc3, dg3, dn3) = _ffn_bwd("ffn2", dx3, x2, n3, sc3, g3, saved3, hooks2 or _BwdHooks())
    dx1, rest, (dsh2, dsc2, dg2, dn2), dq_norm, dkv_norm, dbias, dsink = _mix_bwd(
        dx2, x1, n2, sc2, g2, q_norm, kv_norm, bias, sinkb, cos2, sin2, saved2, hooks_mix or _BwdHooks()
    )
    rest = rest[:, None]
    deps1 = mix_done(dx1, rest) if mix_done is not None else []
    dx0, gb1, (dsh1, dsc1, dg1, dn1) = _ffn_bwd(
        "ffn1", dx1, x, n1, sc1, g1, saved1, hooks1 or _BwdHooks(), deps=deps1
    )

    dmod = jnp.concatenate([dsh1, dsc1, dg1, dsh2, dsc2, dg2, dsh3, dsc3, dg3], axis=-1)
    drel = _bias_reduce(dbias.reshape(SWA_HEADS, -1), bkt).T
    dsinks = jnp.sum(dsink, axis=(1, 2)).reshape(1, SWA_HEADS)
    small = (dn1, dn2, dn3, dnf, dq_norm, dkv_norm, dsinks, drel)
    return loss_part, dx0, (gb1, gb2, rest), dmod, small


_MESH = pl.DeviceIdType.MESH


def _place():
    return lax.axis_index("x"), lax.axis_index("y"), lax.axis_index("c")


def _other_chips(x, y):
    return [(1 - x, y), (x, 1 - y), (1 - x, 1 - y)]


def _allgather_small(name, blk):
    m_per, n = blk.shape

    def body(x_ref, out_ref, send_sems, recv_sems, local_sem):
        x, y, c = _place()
        me, sibling = (x, y, c), (x, y, 1 - c)
        chips = _other_chips(x, y)

        def rows(px, py, pc):
            return out_ref.at[pl.ds((4 * px + 2 * py + pc) * m_per, m_per), :]

        def copy(k, block, to, src=None):
            return pltpu.make_async_remote_copy(
                src_ref=rows(*block) if src is None else src, dst_ref=rows(*block),
                send_sem=send_sems.at[k], recv_sem=recv_sems.at[k], device_id=to, device_id_type=_MESH,
            )

        mine = pltpu.make_async_copy(x_ref, rows(*me), local_sem)
        mine.start()
        first = [copy(0, me, sibling, src=x_ref)]
        first += [copy(1 + j, me, (*chip, c), src=x_ref) for j, chip in enumerate(chips)]
        for cp in first:
            cp.start()
        passed = [copy(4 + j, (*chip, c), sibling) for j, chip in enumerate(chips)]
        for j, chip in enumerate(chips):
            copy(1 + j, (*chip, c), me).wait_recv()
            passed[j].start()
        copy(0, sibling, me).wait_recv()
        for j, chip in enumerate(chips):
            copy(4 + j, (*chip, 1 - c), me).wait_recv()
        for cp in first + passed:
            cp.wait_send()
        mine.wait()

    return pl.pallas_call(
        body,
        name=name,
        out_shape=_sds((N_DEV * m_per, n), blk.dtype),
        in_specs=[pl.BlockSpec(memory_space=pltpu.VMEM)],
        out_specs=pl.BlockSpec(memory_space=pltpu.VMEM),
        scratch_shapes=[pltpu.SemaphoreType.DMA((7,)), pltpu.SemaphoreType.DMA((7,)), pltpu.SemaphoreType.DMA],
    )(blk)


def _gather_sends(n, own_ref, g_ref, send_sem, recv_sem, x, y, c, chip):
    return [
        pltpu.make_async_remote_copy(
            src_ref=own_ref.at[p, pl.ds(c * HALF, HALF), :], dst_ref=g_ref.at[2 * x + y, p, pl.ds(c * HALF, HALF), :],
            send_sem=send_sem, recv_sem=recv_sem, device_id=(*chip, c), device_id_type=_MESH,
        )
        for p in range(n)
    ]


def _gather_forwards(n, g_ref, send_sem, recv_sem, chip, c, sibling):
    return [
        pltpu.make_async_remote_copy(
            src_ref=g_ref.at[2 * chip[0] + chip[1], p, pl.ds(c * HALF, HALF), :],
            dst_ref=g_ref.at[2 * chip[0] + chip[1], p, pl.ds(c * HALF, HALF), :],
            send_sem=send_sem, recv_sem=recv_sem, device_id=sibling, device_id_type=_MESH,
        )
        for p in range(n)
    ]


def _gather_all(own_ref, g_ref, send_sem, recv_sem, chip, half, c):
    return pltpu.make_async_remote_copy(
        src_ref=own_ref.at[:, pl.ds(c * HALF, HALF), :],
        dst_ref=g_ref.at[2 * chip[0] + chip[1], :, pl.ds(half * HALF, HALF), :],
        send_sem=send_sem, recv_sem=recv_sem, device_id=(*chip, c), device_id_type=_MESH,
    )


_HBM_SPEC = pl.BlockSpec(memory_space=pltpu.HBM)
_SEM_SPEC = pl.BlockSpec(memory_space=pltpu.SEMAPHORE)
_ANY_SPEC = pl.BlockSpec(memory_space=pl.ANY)
_VMEM_SPEC = pl.BlockSpec(memory_space=pltpu.VMEM)
_SPLIT_PARAMS = pltpu.CompilerParams(has_side_effects=pltpu.SideEffectType.DATAFLOW_SIDE_EFFECTING)
_TOKEN = jax.ShapeDtypeStruct((8, 128), F32)


def _in_hbm(a):
    return pltpu.with_memory_space_constraint(a, pltpu.HBM)


class _GatherBehind:
    def __init__(self, tag, own, then=None):
        self.tag, self.n, self.own, self.then = tag, own.shape[0], own, then

    def start(self, after):
        tag, own, n = self.tag, self.own, self.n
        x, y, _ = _place()
        g_init = lax.dynamic_update_slice(
            lax.empty((N_CHIPS,) + own.shape, own.dtype), own[None], (2 * x + y, 0, 0, 0)
        )

        def body(own_ref, g_ref, *rest):
            send_sems, recv_sems, _, _, token = rest[len(after):]
            x, y, c = _place()
            for j, chip in enumerate(_other_chips(x, y)):
                for cp in _gather_sends(n, own_ref, g_ref, send_sems.at[j], recv_sems.at[j], x, y, c, chip):
                    cp.start()
            token[...] = jnp.zeros_like(token)

        self.send1, self.recv1, self.own, self.g, self.token = pl.pallas_call(
            body,
            name=f"{tag}_start",
            out_shape=(
                pltpu.SemaphoreType.DMA((3,)), pltpu.SemaphoreType.DMA((3,)),
                pltpu.HBM(own.shape, own.dtype), pltpu.HBM(g_init.shape, g_init.dtype), _TOKEN,
            ),
            in_specs=(_HBM_SPEC, _HBM_SPEC) + (_ANY_SPEC,) * len(after),
            out_specs=(_SEM_SPEC, _SEM_SPEC, _HBM_SPEC, _HBM_SPEC, _VMEM_SPEC),
            input_output_aliases={0: 2, 1: 3},
            compiler_params=_SPLIT_PARAMS,
        )(_in_hbm(own), _in_hbm(g_init), *after)

    def mid(self, after):
        n = self.n

        def body(own_ref, g_ref, send1, recv1, after_ref, send2, recv2, g_out, token):
            x, y, c = _place()
            sibling = (x, y, 1 - c)
            chips = _other_chips(x, y)
            for j, chip in enumerate(chips):
                _gather_all(own_ref, g_ref, send1.at[j], recv1.at[j], chip, c, c).wait_recv()
                for cp in _gather_forwards(n, g_ref, send2.at[j], recv2.at[j], chip, c, sibling):
                    cp.start()
            for j, chip in enumerate(chips):
                _gather_all(own_ref, g_ref, send1.at[j], recv1.at[j], chip, c, c).wait_send()
            token[...] = jnp.zeros_like(token)

        self.send2, self.recv2, self.g, token = pl.pallas_call(
            body,
            name=f"{self.tag}_mid",
            out_shape=(
                pltpu.SemaphoreType.DMA((3,)), pltpu.SemaphoreType.DMA((3,)),
                pltpu.HBM(self.g.shape, self.g.dtype), _TOKEN,
            ),
            in_specs=(_HBM_SPEC, _HBM_SPEC, _SEM_SPEC, _SEM_SPEC, _ANY_SPEC),
            out_specs=(_SEM_SPEC, _SEM_SPEC, _HBM_SPEC, _VMEM_SPEC),
            input_output_aliases={1: 2},
            compiler_params=_SPLIT_PARAMS,
        )(self.own, self.g, self.send1, self.recv1, after)
        if self.then is None:
            return [token]
        self.then.start([token])
        return [token, self.then.token]

    def get(self, after):
        def body(g_ref, send2, recv2, after_ref, g_out):
            x, y, c = _place()
            for j, chip in enumerate(_other_chips(x, y)):
                slot_in = g_ref.at[2 * chip[0] + chip[1], :, pl.ds((1 - c) * HALF, HALF), :]
                slot_out = g_ref.at[2 * chip[0] + chip[1], :, pl.ds(c * HALF, HALF), :]
                cp = pltpu.make_async_remote_copy(
                    src_ref=slot_out, dst_ref=slot_in, send_sem=send2.at[j], recv_sem=recv2.at[j],
                    device_id=(x, y, 1 - c), device_id_type=_MESH,
                )
                cp.wait_send()
                cp.wait_recv()

        return pl.pallas_call(
            body,
            name=f"{self.tag}_wait",
            out_shape=pltpu.HBM(self.g.shape, self.g.dtype),
            in_specs=(_HBM_SPEC, _SEM_SPEC, _SEM_SPEC, _ANY_SPEC),
            out_specs=_HBM_SPEC,
            input_output_aliases={0: 0},
            compiler_params=_SPLIT_PARAMS,
        )(self.g, self.send2, self.recv2, after)


def _pair_sum(name, gb, land):
    def body(c_ref, gb_ref, land_ref, o_ref):
        o_ref[...] = (gb_ref[...].astype(F32) + land_ref[...].astype(F32)).astype(o_ref.dtype)

    core = lax.axis_index("c").astype(jnp.int32).reshape(1)
    return pl.pallas_call(
        body,
        name=name,
        grid_spec=pltpu.PrefetchScalarGridSpec(
            num_scalar_prefetch=1,
            grid=(N_CHIPS, gb.shape[1]),
            in_specs=[
                pl.BlockSpec((None, None, HALF, D_MODEL), lambda j, p, c: (j, p, c[0], 0)),
                pl.BlockSpec((None, None, HALF, D_MODEL), lambda j, p, c: (j, p, 0, 0)),
            ],
            out_specs=pl.BlockSpec((None, None, HALF, D_MODEL), lambda j, p, c: (j, p, 0, 0)),
        ),
        out_shape=_sds(land.shape, BF16),
        compiler_params=_cparams(("parallel", "parallel")),
    )(core, gb, land)


def _chip_sum_share(name, land):
    n = land.shape[1]

    def body(l_ref, r_ref, buf, send_sems, recv_sems, local_sems):
        p = pl.program_id(0)
        x, y, c = _place()
        acc = l_ref[0].astype(F32)
        for j in range(1, N_CHIPS):
            acc = acc + l_ref[j].astype(F32)
        buf[p] = acc

        def copies(q):
            mine = r_ref.at[q, pl.ds(c * HALF, HALF), :]
            theirs = r_ref.at[q, pl.ds((1 - c) * HALF, HALF), :]
            local = pltpu.make_async_copy(buf.at[q], mine, local_sems.at[q])
            out = pltpu.make_async_remote_copy(
                src_ref=buf.at[q], dst_ref=mine, send_sem=send_sems.at[q], recv_sem=recv_sems.at[q],
                device_id=(x, y, 1 - c), device_id_type=_MESH,
            )
            arrive = pltpu.make_async_remote_copy(
                src_ref=buf.at[q], dst_ref=theirs, send_sem=send_sems.at[q], recv_sem=recv_sems.at[q],
                device_id=(x, y, 1 - c), device_id_type=_MESH,
            )
            return local, out, arrive

        local, out, _ = copies(p)
        local.start()
        out.start()

        @pl.when(p == n - 1)
        def _():
            for q in range(n):
                local, out, arrive = copies(q)
                arrive.wait_recv()
                out.wait_send()
                local.wait()

    return pl.pallas_call(
        body,
        name=name,
        grid=(n,),
        in_specs=[pl.BlockSpec((N_CHIPS, None, HALF, D_MODEL), lambda p: (0, p, 0, 0))],
        out_specs=pl.BlockSpec(memory_space=pl.ANY),
        out_shape=_sds((n, F_SHARD, D_MODEL), F32),
        scratch_shapes=[
            pltpu.VMEM((n, HALF, D_MODEL), F32),
            pltpu.SemaphoreType.DMA((n,)), pltpu.SemaphoreType.DMA((n,)), pltpu.SemaphoreType.DMA((n,)),
        ],
        compiler_params=_cparams(("arbitrary",)),
    )(land)


class _ReduceBehind:
    def __init__(self, tag, gb, after=()):
        self.tag = tag
        n = gb.shape[1]
        land = lax.empty((N_CHIPS, n, HALF, D_MODEL), gb.dtype)

        def body(gb_ref, land_ref, *rest):
            send_sem, recv_sem, _, _, token = rest[len(after):]
            self._pair_copy(gb_ref, land_ref, send_sem, recv_sem).start()
            token[...] = jnp.zeros_like(token)

        self.send, self.recv, self.gb, self.land, self.token = pl.pallas_call(
            body,
            name=f"grads_pair_start_{tag}",
            out_shape=(
                pltpu.SemaphoreType.DMA((1,)), pltpu.SemaphoreType.DMA((1,)),
                pltpu.HBM(gb.shape, gb.dtype), pltpu.HBM(land.shape, land.dtype), _TOKEN,
            ),
            in_specs=(_HBM_SPEC, _HBM_SPEC) + (_ANY_SPEC,) * len(after),
            out_specs=(_SEM_SPEC, _SEM_SPEC, _HBM_SPEC, _HBM_SPEC, _VMEM_SPEC),
            input_output_aliases={0: 2, 1: 3},
            compiler_params=_SPLIT_PARAMS,
        )(_in_hbm(gb), _in_hbm(land), *after)

    @staticmethod
    def _pair_copy(gb_ref, land_ref, send_sem, recv_sem):
        x, y, c = _place()
        return pltpu.make_async_remote_copy(
            src_ref=gb_ref.at[:, :, pl.ds((1 - c) * HALF, HALF), :], dst_ref=land_ref,
            send_sem=send_sem.at[0], recv_sem=recv_sem.at[0], device_id=(x, y, 1 - c), device_id_type=_MESH,
        )

    @staticmethod
    def _chip_copies(p_ref, land_ref, send_sems, recv_sems):
        x, y, c = _place()
        me = 2 * x + y
        sends, arrivals = [], []
        for k, chip in enumerate(_other_chips(x, y)):
            them = 2 * chip[0] + chip[1]
            sends.append(pltpu.make_async_remote_copy(
                src_ref=p_ref.at[them], dst_ref=land_ref.at[me], send_sem=send_sems.at[k], recv_sem=recv_sems.at[k],
                device_id=(*chip, c), device_id_type=_MESH,
            ))
            arrivals.append(pltpu.make_async_remote_copy(
                src_ref=p_ref.at[me], dst_ref=land_ref.at[them], send_sem=send_sems.at[k], recv_sem=recv_sems.at[k],
                device_id=(*chip, c), device_id_type=_MESH,
            ))
        return sends, arrivals

    def mid(self, after):
        tag = self.tag

        def wait_body(gb_ref, land_ref, send_sem, recv_sem, after_ref, gb_out, land_out):
            cp = self._pair_copy(gb_ref, land_ref, send_sem, recv_sem)
            cp.wait_send()
            cp.wait_recv()

        gb, land = pl.pallas_call(
            wait_body,
            name=f"grads_pair_wait_{tag}",
            out_shape=(pltpu.HBM(self.gb.shape, self.gb.dtype), pltpu.HBM(self.land.shape, self.land.dtype)),
            in_specs=(_HBM_SPEC, _HBM_SPEC, _SEM_SPEC, _SEM_SPEC, _ANY_SPEC),
            out_specs=(_HBM_SPEC, _HBM_SPEC),
            input_output_aliases={0: 0, 1: 1},
            compiler_params=_SPLIT_PARAMS,
        )(self.gb, self.land, self.send, self.recv, after)
        part = _pair_sum(f"grads_pair_sum_{tag}", gb, land)

        x, y, _ = _place()
        start = (2 * x + y, 0, 0, 0)
        own = lax.dynamic_slice(part, start, (1,) + part.shape[1:])
        land2 = lax.dynamic_update_slice(lax.empty(part.shape, part.dtype), own, start)

        def start_body(p_ref, land_ref, send_sems, recv_sems, p_out, land_out, token):
            for cp in self._chip_copies(p_ref, land_ref, send_sems, recv_sems)[0]:
                cp.start()
            token[...] = jnp.zeros_like(token)

        self.send2, self.recv2, self.part, self.land2, token = pl.pallas_call(
            start_body,
            name=f"grads_chip_start_{tag}",
            out_shape=(
                pltpu.SemaphoreType.DMA((3,)), pltpu.SemaphoreType.DMA((3,)),
                pltpu.HBM(part.shape, part.dtype), pltpu.HBM(land2.shape, land2.dtype), _TOKEN,
            ),
            in_specs=(_HBM_SPEC, _HBM_SPEC),
            out_specs=(_SEM_SPEC, _SEM_SPEC, _HBM_SPEC, _HBM_SPEC, _VMEM_SPEC),
            input_output_aliases={0: 2, 1: 3},
            compiler_params=_SPLIT_PARAMS,
        )(_in_hbm(part), _in_hbm(land2))
        return [token]

    def get(self, after):
        n_after = len(after)

        def wait_body(p_ref, land_ref, send_sems, recv_sems, *rest):
            sends, arrivals = self._chip_copies(p_ref, land_ref, send_sems, recv_sems)
            for cp in arrivals:
                cp.wait_recv()
            for cp in sends:
                cp.wait_send()

        _, land2 = pl.pallas_call(
            wait_body,
            name=f"grads_chip_wait_{self.tag}",
            out_shape=(pltpu.HBM(self.part.shape, self.part.dtype), pltpu.HBM(self.land2.shape, self.land2.dtype)),
            in_specs=(_HBM_SPEC, _HBM_SPEC, _SEM_SPEC, _SEM_SPEC) + (_ANY_SPEC,) * n_after,
            out_specs=(_HBM_SPEC, _HBM_SPEC),
            input_output_aliases={0: 0, 1: 1},
            compiler_params=_SPLIT_PARAMS,
        )(self.part, self.land2, self.send2, self.recv2, *after)
        return _chip_sum_share(f"grads_chip_sum_share_{self.tag}", land2)


def _allreduce_small(blk):
    gathered = _allgather_small("allgather_small_grads", blk).reshape(N_DEV, PK_ROWS, 128)

    def body(g_ref, o_ref):
        acc = g_ref[0]
        for k in range(1, N_DEV):
            acc = acc + g_ref[k]
        o_ref[...] = acc

    total = pl.pallas_call(body, name="small_grads_sum", out_shape=_sds((PK_ROWS, 128), F32))(gathered)
    return gathered, total


def _adamw_math(w_, g_, m_, v_):
    m_new = ADAM_B1 * m_ + (1.0 - ADAM_B1) * g_
    v_new = ADAM_B2 * v_ + (1.0 - ADAM_B2) * (g_ * g_)
    m_hat = m_new / (1.0 - ADAM_B1 ** ADAM_STEP)
    v_hat = v_new / (1.0 - ADAM_B2 ** ADAM_STEP)
    delta = -ADAM_LR * (m_hat / (jnp.sqrt(v_hat) + ADAM_EPS) + ADAM_WD * w_)
    return delta, m_new, v_new


def _adamw(name, w, g, m, v):
    rows, cols = w.shape
    tm = rows
    while tm * cols * 4 > (1 << 20) and tm % 16 == 0:
        tm //= 2
    out = _sds(w.shape, F32)
    return _rowwise(name, _adamw_math, [w, g, m, v], [], [out, out, out], [], tm)


def _adamw_small(ws, gs, ms, vs):
    n = len(ws)
    shapes = [w.shape for w in ws]
    as2d = lambda a: a.reshape(1, -1) if a.ndim == 1 else a

    def body(*refs):
        ins, outs = refs[:4 * n], refs[4 * n:]
        for k in range(n):
            res = _adamw_math(*[ins[j * n + k][...] for j in range(4)])
            for j in range(3):
                outs[j * n + k][...] = res[j]

    args = [as2d(a) for group in (ws, gs, ms, vs) for a in group]
    out = pl.pallas_call(
        body, name="adamw_small", out_shape=[_sds(as2d(w).shape, F32) for w in ws] * 3, compiler_params=_cparams()
    )(*args)
    back = [o.reshape(shapes[i % n]) for i, o in enumerate(out)]
    return back[:n], back[n:2 * n], back[2 * n:]


def _pack_small(b_mod_like, n1, n2, n3, nf, qn, kvn, sinks, rel):
    parts = [
        b_mod_like.reshape(PK_MOD, 128),
        n1.reshape(8, 128), n2.reshape(8, 128), n3.reshape(8, 128), nf.reshape(8, 128),
        qn.reshape(2, 128), kvn.reshape(1, 128),
        jnp.pad(sinks.reshape(1, SWA_HEADS), ((0, 0), (0, 128 - SWA_HEADS))),
        rel.reshape(2, 128),
        jnp.zeros((2, 128), F32),
    ]
    return jnp.concatenate(parts, axis=0)


def _unpack_small(p):
    o = PK_MOD
    return (
        p[:o].reshape(1, N_MOD * D_MODEL),
        p[o:o + 8].reshape(1, D_MODEL), p[o + 8:o + 16].reshape(1, D_MODEL),
        p[o + 16:o + 24].reshape(1, D_MODEL), p[o + 24:o + 32].reshape(D_MODEL),
        p[o + 32:o + 34].reshape(1, MLA_Q_RANK), p[o + 34:o + 35].reshape(1, MLA_KV_RANK),
        p[o + 35:o + 36, :SWA_HEADS].reshape(1, SWA_HEADS),
        p[o + 36:o + 38].reshape(NUM_BUCKETS, SWA_HEADS),
    )


def kernel(x, c, w_mod, b_mod, norm_ffn1, ffn1_gate, ffn1_up, ffn1_down, norm_mix, w_in, q_norm, kv_norm, w_uq, w_ukv, sinks, w_o, norm_ffn2, ffn2_gate, ffn2_up, ffn2_down, rel_bias, norm_final, loss_target, m_w_mod, m_b_mod, m_norm_ffn1, m_ffn1_gate, m_ffn1_up, m_ffn1_down, m_norm_mix, m_w_in, m_q_norm, m_kv_norm, m_w_uq, m_w_ukv, m_sinks, m_w_o, m_norm_ffn2, m_ffn2_gate, m_ffn2_up, m_ffn2_down, m_rel_bias, m_norm_final, v_w_mod, v_b_mod, v_norm_ffn1, v_ffn1_gate, v_ffn1_up, v_ffn1_down, v_norm_mix, v_w_in, v_q_norm, v_kv_norm, v_w_uq, v_w_ukv, v_sinks, v_w_o, v_norm_ffn2, v_ffn2_gate, v_ffn2_up, v_ffn2_down, v_rel_bias, v_norm_final):
    ax, ay, ac = _place()
    chip = 2 * ax + ay
    dev = 2 * chip + ac
    n_mod_shard = N_MOD * D_MODEL // N_CHIPS

    def t16(w):
        return w[0].T.astype(BF16)

    rest = jnp.concatenate(
        [
            t16(w_in),
            w_o[0].astype(BF16),
            t16(w_uq).reshape(R_UQ, D_MODEL),
            t16(w_ukv).reshape(R_UKV, D_MODEL),
            jnp.zeros((F_SHARD - O_PAD, D_MODEL), BF16),
        ],
        axis=0,
    )
    own_gu1 = jnp.stack([t16(ffn1_gate), t16(ffn1_up)])
    own_down1 = ffn1_down[0].astype(BF16)[None]
    own_mix = rest[None]
    own_ffn2 = jnp.stack([t16(ffn2_gate), t16(ffn2_up), ffn2_down[0].astype(BF16)])

    (c_act,) = _rowwise(
        "silu_c", lambda v: v * _sigmoid(v), [c.reshape(8, 128)], [], [_sds((8, 128), F32)], [], 8
    )
    c_all = _allgather_small("allgather_c", c_act).reshape(N_DEV, D_MODEL)
    mod_cols = _mm(
        "mod_fwd", "nn", (1, n_mod_shard // MOD_TILE, 1),
        c_all, (N_DEV, D_MODEL), lambda i, j, k: (0, 0),
        w_mod[0], (D_MODEL, MOD_TILE), lambda i, j, k: (0, j),
        _sds((N_DEV, n_mod_shard), F32), (N_DEV, MOD_TILE), lambda i, j, k: (0, j),
        (N_DEV, MOD_TILE),
    )
    mod_all = _allgather_small("allgather_mod", mod_cols).reshape(N_CHIPS, 2, N_DEV, n_mod_shard)[:, 0]
    mod_all = mod_all.transpose(1, 0, 2).reshape(N_DEV, N_MOD * D_MODEL)
    mod = lax.dynamic_slice_in_dim(mod_all, dev, 1, axis=0) + b_mod

    norms = (norm_ffn1, norm_mix, norm_ffn2, norm_final.reshape(1, D_MODEL))

    ffn2_src = _GatherBehind("gather_ffn2", own_ffn2)
    mix_src = _GatherBehind("gather_mix", own_mix, then=ffn2_src)
    down1_src = _GatherBehind("gather_down1", own_down1, then=mix_src)
    gu1_src = _GatherBehind("gather_gu1", own_gu1, then=down1_src)
    gu1_src.start([mod_all])

    reducing, red = {}, {}

    def begin(tag, gb, after):
        reducing[tag] = _ReduceBehind(tag, gb, after=after)
        return [reducing[tag].token]

    def ffn2_gu_done(gb):
        return begin("ffn2_gu", gb, reducing["ffn2_down"].mid(gb))

    def mix_done(dx1, gb_rest):
        red["ffn2_down"] = reducing["ffn2_down"].get([dx1])
        red["ffn2_gu"] = reducing["ffn2_gu"].get([red["ffn2_down"]])
        return begin("rest", gb_rest, [red["ffn2_gu"]])

    def ffn1_gu_done(gb):
        red["rest"] = reducing["rest"].get([gb])
        begin("ffn1_gu", gb, reducing["ffn1_down"].mid(red["rest"]))
        return reducing["ffn1_gu"].mid(reducing["ffn1_gu"].token)

    loss_part, grad_x, _, dmod, small = _device_step(
        x[0], loss_target[0], mod, gu1_src, down1_src, mix_src, ffn2_src, norms, q_norm, kv_norm, sinks, rel_bias,
        hooks2=_BwdHooks(after_wdown=lambda gb: begin("ffn2_down", gb, ()), after_wgu=ffn2_gu_done),
        hooks_mix=_BwdHooks(after_gate=lambda dz: reducing["ffn2_gu"].mid(dz)),
        mix_done=mix_done,
        hooks1=_BwdHooks(
            after_swiglu=lambda dab3: reducing["rest"].mid(dab3),
            after_wdown=lambda gb: begin("ffn1_down", gb, ()),
            after_wgu=ffn1_gu_done,
        ),
    )
    loss = lax.psum(loss_part, ("x", "y", "c"))

    gathered, total = _allreduce_small(_pack_small(dmod, *small))
    (g_b_mod, g_n1, g_n2, g_n3, g_nf, g_qn, g_kvn, g_sinks, g_rel) = _unpack_small(total)
    dmod_all = gathered[:, :PK_MOD].reshape(N_DEV, N_MOD * D_MODEL)
    dmod_cols = lax.dynamic_slice_in_dim(dmod_all, chip * n_mod_shard, n_mod_shard, axis=1)
    g_w_mod = _mm(
        "mod_bwd", "tn", (1, n_mod_shard // MOD_TILE, 1),
        c_all, (N_DEV, D_MODEL), lambda i, j, k: (0, 0),
        dmod_cols, (N_DEV, MOD_TILE), lambda i, j, k: (0, j),
        _sds((D_MODEL, n_mod_shard), F32), (D_MODEL, MOD_TILE), lambda i, j, k: (0, j),
        (D_MODEL, MOD_TILE),
    )

    r_rest = red["rest"][0]
    transposed = {"ffn1_gate", "ffn1_up", "ffn2_gate", "ffn2_up", "w_in", "w_uq", "w_ukv"}
    g_big = {
        "ffn2_gate": red["ffn2_gu"][0], "ffn2_up": red["ffn2_gu"][1], "ffn2_down": red["ffn2_down"][0],
        "w_in": r_rest[O_IN:O_IN + R_IN],
        "w_o": r_rest[O_O:O_O + R_O],
        "w_uq": r_rest[O_UQ:O_UQ + R_UQ].reshape(MLA_QK, MLA_Q_RANK),
        "w_ukv": r_rest[O_UKV:O_UKV + R_UKV].reshape(MLA_NOPE + MLA_V, MLA_KV_RANK),
        "w_mod": g_w_mod,
    }
    w_big = {
        "ffn2_gate": (ffn2_gate, m_ffn2_gate, v_ffn2_gate),
        "ffn2_up": (ffn2_up, m_ffn2_up, v_ffn2_up), "ffn2_down": (ffn2_down, m_ffn2_down, v_ffn2_down),
        "w_in": (w_in, m_w_in, v_w_in), "w_o": (w_o, m_w_o, v_w_o), "w_uq": (w_uq, m_w_uq, v_w_uq),
        "w_ukv": (w_ukv, m_w_ukv, v_w_ukv), "w_mod": (w_mod, m_w_mod, v_w_mod),
    }
    grads, deltas, new_m, new_v = {}, {}, {}, {}

    def update(names):
        for nm in names:
            w, m, v = w_big[nm]
            if nm in transposed:
                outs = _adamw(f"adamw_{nm}", w[0].T, g_big[nm], m[0].T, v[0].T)
                g_, d_, m_, v_ = [a.T for a in (g_big[nm],) + tuple(outs)]
            else:
                g_, (d_, m_, v_) = g_big[nm], _adamw(f"adamw_{nm}", w[0], g_big[nm], m[0], v[0])
            grads[nm], deltas[nm], new_m[nm], new_v[nm] = g_[None], d_[None], m_[None], v_[None]

    update(list(w_big))
    red1_down = reducing["ffn1_down"].get([deltas[nm] for nm in w_big])
    red1_gu = reducing["ffn1_gu"].get([red1_down])
    g_big.update({"ffn1_gate": red1_gu[0], "ffn1_up": red1_gu[1], "ffn1_down": red1_down[0]})
    w_big.update({
        "ffn1_gate": (ffn1_gate, m_ffn1_gate, v_ffn1_gate), "ffn1_up": (ffn1_up, m_ffn1_up, v_ffn1_up),
        "ffn1_down": (ffn1_down, m_ffn1_down, v_ffn1_down),
    })
    update(["ffn1_gate", "ffn1_up", "ffn1_down"])

    small_names = ["b_mod", "norm_ffn1", "norm_mix", "norm_ffn2", "norm_final", "q_norm", "kv_norm", "sinks", "rel_bias"]
    w_small = (b_mod, norm_ffn1, norm_mix, norm_ffn2, norm_final, q_norm, kv_norm, sinks, rel_bias)
    m_small = (m_b_mod, m_norm_ffn1, m_norm_mix, m_norm_ffn2, m_norm_final, m_q_norm, m_kv_norm, m_sinks, m_rel_bias)
    v_small = (v_b_mod, v_norm_ffn1, v_norm_mix, v_norm_ffn2, v_norm_final, v_q_norm, v_kv_norm, v_sinks, v_rel_bias)
    g_small = (g_b_mod, g_n1, g_n2, g_n3, g_nf, g_qn, g_kvn, g_sinks, g_rel)
    d_s, m_s, v_s = _adamw_small(w_small, g_small, m_small, v_small)
    for nm, g_, d_, m_, v_ in zip(small_names, g_small, d_s, m_s, v_s):
        grads[nm], deltas[nm], new_m[nm], new_v[nm] = g_, d_, m_, v_

    order = ["w_mod", "b_mod", "norm_ffn1", "ffn1_gate", "ffn1_up", "ffn1_down", "norm_mix", "w_in", "q_norm", "kv_norm",
             "w_uq", "w_ukv", "sinks", "w_o", "norm_ffn2", "ffn2_gate", "ffn2_up", "ffn2_down", "rel_bias", "norm_final"]
    return (loss, grad_x[None], *[grads[n] for n in order], *[deltas[n] for n in order],
            *[new_m[n] for n in order], *[new_v[n] for n in order])
```

```python
import functools
import math

import jax
import jax.numpy as jnp
import numpy as np
from jax import lax
from jax.experimental import pallas as pl
from jax.experimental.pallas import tpu as pltpu

F32, BF16 = jnp.float32, jnp.bfloat16

D_MODEL = 1024
D_FF = 2816
EPS = 1e-6
N_MOD = 9
SWA_HEADS, SWA_KV_HEADS, SWA_HEAD_DIM, WINDOW = 8, 2, 64, 128
SWA_GROUP = SWA_HEADS // SWA_KV_HEADS
MLA_HEADS, MLA_Q_RANK, MLA_KV_RANK, MLA_NOPE, MLA_ROPE, MLA_V = 4, 256, 128, 128, 64, 128
MLA_QK = MLA_NOPE + MLA_ROPE
ROPE_THETA = 10000.0
NUM_BUCKETS, MAX_DISTANCE = 32, 128
D_IN = 1216
N_SWA_COLS = 768
N_MLA_COLS = 512

ADAM_LR, ADAM_B1, ADAM_B2, ADAM_EPS, ADAM_WD, ADAM_STEP = 0.001, 0.9, 0.999, 1e-08, 0.01, 10

N_CHIPS = 4
N_DEV = 8
F_SHARD = D_FF // N_CHIPS
HALF = F_SHARD // 2
R_IN, R_O, R_UQ, R_UKV = 304, 256, 48, 32
O_IN, O_O, O_UQ, O_UKV, O_PAD = 0, 304, 560, 608, 640

PK_MOD = 72
PK_ROWS = 112
VMEM_LIMIT = 56 << 20
ROW_TILE = 2048
ELEM_TILE = 512
MOD_TILE = 768

_DN = {
    "nn": (((1,), (0,)), ((), ())),
    "nt": (((1,), (1,)), ((), ())),
    "tn": (((0,), (0,)), ((), ())),
}


def _sds(shape, dtype):
    return jax.ShapeDtypeStruct(tuple(shape), dtype)


def _cparams(sem=None):
    kw = {"vmem_limit_bytes": VMEM_LIMIT}
    if sem is not None:
        kw["dimension_semantics"] = sem
    return pltpu.CompilerParams(**kw)


def _dot(a, b, dims):
    return lax.dot_general(a.astype(BF16), b.astype(BF16), _DN[dims], preferred_element_type=F32)


def _mm(name, dims, grid, a, a_blk, a_idx, b, b_blk, b_idx, out, out_blk, out_idx, acc_shape, alias=None, deps=()):
    nk = grid[2]

    def body(*refs):
        a_ref, b_ref = refs[:2]
        o_ref, acc_ref = refs[-2:]
        part = _dot(a_ref[...], b_ref[...], dims)
        if nk == 1:
            o_ref[...] = part.astype(o_ref.dtype)
            return
        k = pl.program_id(2)

        @pl.when(k == 0)
        def _():
            acc_ref[...] = part

        @pl.when((k > 0) & (k < nk - 1))
        def _():
            acc_ref[...] += part

        @pl.when(k == nk - 1)
        def _():
            o_ref[...] = (acc_ref[...] + part).astype(o_ref.dtype)

    in_specs = [pl.BlockSpec(a_blk, a_idx), pl.BlockSpec(b_blk, b_idx)]
    args = [a, b]
    kw = {}
    if alias is not None:
        in_specs.append(pl.BlockSpec(memory_space=pl.ANY))
        args.append(alias)
        kw["input_output_aliases"] = {2: 0}
    in_specs += [pl.BlockSpec(memory_space=pl.ANY)] * len(deps)
    args += list(deps)
    return pl.pallas_call(
        body,
        name=name,
        grid=grid,
        in_specs=in_specs,
        out_specs=pl.BlockSpec(out_blk, out_idx),
        out_shape=out,
        scratch_shapes=[pltpu.VMEM(acc_shape if nk > 1 else (8, 128), F32)],
        compiler_params=_cparams(("parallel", "parallel", "arbitrary")),
        **kw,
    )(*args)


def _rowwise(name, fn, tiled, full, out_tiled, out_red, tm, deps=()):
    rows = tiled[0].shape[-2]
    tm = min(tm, rows)
    grid = (rows // tm,)
    n_in = len(tiled) + len(full)
    n_t = len(out_tiled)
    n_dep = len(deps)

    def tspec(shape):
        nd = len(shape)
        return pl.BlockSpec(tuple(shape[:-2]) + (tm, shape[-1]), lambda i, nd=nd: (0,) * (nd - 2) + (i, 0))

    def fspec(shape):
        nd = len(shape)
        return pl.BlockSpec(tuple(shape), lambda i, nd=nd: (0,) * nd)

    def body(*refs):
        outs = fn(*[r[...] for r in refs[:n_in]])
        if not isinstance(outs, (tuple, list)):
            outs = (outs,)
        out_refs = refs[n_in + n_dep:]
        for r, v in zip(out_refs[:n_t], outs[:n_t]):
            r[...] = v.astype(r.dtype)
        red_refs = out_refs[n_t:]
        if red_refs:
            @pl.when(pl.program_id(0) == 0)
            def _():
                for r in red_refs:
                    r[...] = jnp.zeros_like(r)

            for r, v in zip(red_refs, outs[n_t:]):
                r[...] += v.astype(r.dtype)

    res = pl.pallas_call(
        body,
        name=name,
        grid=grid,
        in_specs=[tspec(a.shape) for a in tiled] + [fspec(a.shape) for a in full]
        + [pl.BlockSpec(memory_space=pl.ANY)] * n_dep,
        out_specs=[tspec(o.shape) for o in out_tiled] + [fspec(o.shape) for o in out_red],
        out_shape=list(out_tiled) + list(out_red),
        compiler_params=_cparams(("arbitrary",) if out_red else ("parallel",)),
    )(*tiled, *full, *deps)
    return res


def _sum0(v):
    return jnp.sum(v, axis=0, keepdims=True)


def _sigmoid(v):
    return 1.0 / (1.0 + jnp.exp(-v))


def _rms(x):
    r = lax.rsqrt(jnp.mean(x * x, axis=-1, keepdims=True) + EPS)
    return x * r, r


def _rms_bwd(u, xr, r):
    return r * (u - xr * jnp.mean(u * xr, axis=-1, keepdims=True))


class _Ready:
    def __init__(self, g):
        self.g = g

    def mid(self, after):
        return []

    def get(self, after):
        return self.g


def _normmod(x_, gamma_, sc_, sh_):
    return _rms(x_)[0] * gamma_ * (1.0 + sc_) + sh_


def _row_blocks(tm, size=512):
    size = min(size, tm)
    return [slice(r, r + size) for r in range(0, tm, size)]


def _loss_head(x_, t_, g_):
    xr, r = _rms(x_)
    err = xr * g_ - t_
    dy = err * (1.0 / D_MODEL)
    return _rms_bwd(dy * g_, xr, r), _sum0(err * err), _sum0(dy * xr)


def _ffn_fwd(tag, x, gamma, sh, sc, gate, gu_src, base, down_src, down_idx, mid_after, h_pre=None,
             next_norm=None, head=None):
    s_len = x.shape[0]
    if h_pre is None:
        (h,) = _rowwise(
            f"{tag}_normmod", _normmod, [x], [gamma, sc, sh], [_sds((s_len, D_MODEL), BF16)], [], ELEM_TILE
        )
        gdeps = gu_src.mid(h)
    else:
        h, gdeps = h_pre, gu_src.mid(mid_after)
    g4 = gu_src.get(h)

    tm = min(ROW_TILE, s_len)
    def gate_up(h_ref, wg_ref, wu_ref, *rest):
        ab_ref, s_ref = rest[-2:]
        wg, wu = wg_ref[...], wu_ref[...]
        for rows in _row_blocks(tm):
            hv = h_ref[rows, :]
            a = _dot(hv, wg, "nt")
            b = _dot(hv, wu, "nt")
            ab_ref[0, rows, :] = a.astype(ab_ref.dtype)
            ab_ref[1, rows, :] = b.astype(ab_ref.dtype)
            s_ref[rows, :] = (a * _sigmoid(a) * b).astype(s_ref.dtype)

    ab4, s3 = pl.pallas_call(
        gate_up,
        name=f"{tag}_gate_up",
        grid=(s_len // tm, N_CHIPS),
        in_specs=[
            pl.BlockSpec((tm, D_MODEL), lambda i, c: (i, 0)),
            pl.BlockSpec((None, None, F_SHARD, D_MODEL), lambda i, c: (c, base, 0, 0)),
            pl.BlockSpec((None, None, F_SHARD, D_MODEL), lambda i, c: (c, base + 1, 0, 0)),
        ] + [pl.BlockSpec(memory_space=pl.ANY)] * len(gdeps),
        out_specs=[
            pl.BlockSpec((None, 2, tm, F_SHARD), lambda i, c: (c, 0, i, 0)),
            pl.BlockSpec((None, tm, F_SHARD), lambda i, c: (c, i, 0)),
        ],
        out_shape=[_sds((N_CHIPS, 2, s_len, F_SHARD), BF16), _sds((N_CHIPS, s_len, F_SHARD), BF16)],
        compiler_params=_cparams(("parallel", "parallel")),
    )(h, g4, g4, *gdeps)
    if down_src is None:
        g_down, ddeps = g4, ()
    else:
        ddeps = down_src.mid(ab4)
        g_down = down_src.get(s3)

    y = _mm(
        f"{tag}_down", "nn", (s_len // tm, 1, N_CHIPS),
        s3, (None, tm, F_SHARD), lambda i, j, k: (k, i, 0),
        g_down, (None, None, F_SHARD, D_MODEL), lambda i, j, k: (k, down_idx, 0, 0),
        _sds((s_len, D_MODEL), F32), (tm, D_MODEL), lambda i, j, k: (i, 0),
        (tm, D_MODEL), deps=ddeps,
    )

    saved = (g4, base, g_down, down_idx, h, ab4, s3, y)
    act = _sds((s_len, D_MODEL), F32)
    if head is not None:
        target, norm_final = head
        vec = _sds((1, D_MODEL), F32)
        out = _rowwise(
            f"{tag}_residual_head", lambda x_, y_, t_, g_, nf_: _loss_head(x_ + 0.5 * g_ * y_, t_, nf_),
            [x, y, target], [gate, norm_final], [act], [vec, vec], ELEM_TILE,
        )
    elif next_norm is not None:
        def residual_norm(x_, y_, g_, gamma_, sc_, sh_):
            x_out = x_ + 0.5 * g_ * y_
            return x_out, _normmod(x_out, gamma_, sc_, sh_)

        out = _rowwise(
            f"{tag}_residual_norm", residual_norm, [x, y], [gate] + list(next_norm),
            [act, _sds((s_len, D_MODEL), BF16)], [], ELEM_TILE,
        )
    else:
        out = _rowwise(f"{tag}_residual", lambda x_, y_, g_: x_ + 0.5 * g_ * y_, [x, y], [gate], [act], [], ELEM_TILE)
    return out, saved


def _normmod_bwd_call(name, dh_parts, x, dxo, gamma, sc, deps=()):
    s_len = x.shape[0]
    n_parts = len(dh_parts)

    def fn(*vals):
        dh = vals[0]
        for extra in vals[1:n_parts]:
            dh = dh + extra
        x_, dxo_, gamma_, sc_ = vals[n_parts:]
        xr, r = _rms(x_)
        n = xr * gamma_
        dn = dh * (1.0 + sc_)
        dx = dxo_ + _rms_bwd(dn * gamma_, xr, r)
        return dx, _sum0(dh * n), _sum0(dh), _sum0(dn * xr)

    vec = _sds((1, D_MODEL), F32)
    return _rowwise(
        name, fn, list(dh_parts) + [x, dxo], [gamma, sc], [_sds((s_len, D_MODEL), F32)], [vec, vec, vec], ELEM_TILE, deps=deps
    )


class _BwdHooks:
    def __init__(self, after_gate=None, after_swiglu=None, after_wdown=None, after_wgu=None, after_dh=None):
        def nothing(_):
            return []

        self.after_gate = after_gate or nothing
        self.after_swiglu = after_swiglu or nothing
        self.after_wdown = after_wdown or nothing
        self.after_wgu = after_wgu or nothing
        self.after_dh = after_dh or nothing


def _ffn_bwd(tag, dxo, x, gamma, sc, gate, saved, hooks, deps=()):
    g4, base, g_down, down_idx, h, ab4, s3, y = saved
    s_len = x.shape[0]
    vec = _sds((1, D_MODEL), F32)

    dy, dgate = _rowwise(
        f"{tag}_bwd_gate", lambda dxo_, y_, g_: (0.5 * g_ * dxo_, _sum0(0.5 * y_ * dxo_)),
        [dxo, y], [gate], [_sds((s_len, D_MODEL), BF16)], [vec], ELEM_TILE, deps=deps,
    )

    tm = min(ROW_TILE, s_len)

    def d_gate_up(dy_ref, wd_ref, ab_ref, o_ref):
        wd = wd_ref[...]
        for rows in _row_blocks(tm):
            ds = _dot(dy_ref[rows, :], wd, "nt")
            a = ab_ref[0, rows, :].astype(F32)
            b = ab_ref[1, rows, :].astype(F32)
            sig = _sigmoid(a)
            o_ref[0, rows, :] = (ds * b * sig * (1.0 + a * (1.0 - sig))).astype(o_ref.dtype)
            o_ref[1, rows, :] = (ds * a * sig).astype(o_ref.dtype)

    ab_spec = pl.BlockSpec((None, 2, tm, F_SHARD), lambda i, c: (c, 0, i, 0))
    dab4 = pl.pallas_call(
        d_gate_up,
        name=f"{tag}_bwd_dgate_up",
        grid=(s_len // tm, N_CHIPS),
        in_specs=[
            pl.BlockSpec((tm, D_MODEL), lambda i, c: (i, 0)),
            pl.BlockSpec((None, None, F_SHARD, D_MODEL), lambda i, c: (c, down_idx, 0, 0)),
            ab_spec,
        ],
        out_specs=ab_spec,
        out_shape=_sds(ab4.shape, BF16),
        compiler_params=_cparams(("parallel", "parallel")),
    )(dy, g_down, ab4)

    tk = tm
    gb_down = _mm(
        f"{tag}_bwd_wdown", "tn", (N_CHIPS, 1, s_len // tk),
        s3, (None, tk, F_SHARD), lambda i, j, k: (i, k, 0),
        dy, (tk, D_MODEL), lambda i, j, k: (k, 0),
        _sds((N_CHIPS, 1, F_SHARD, D_MODEL), BF16), (None, None, F_SHARD, D_MODEL), lambda i, j, k: (i, 0, 0, 0),
        (F_SHARD, D_MODEL), deps=hooks.after_swiglu(dab4),
    )
    gb_gu = _mm(
        f"{tag}_bwd_wgu", "tn", (2 * N_CHIPS, 1, s_len // tk),
        dab4, (None, None, tk, F_SHARD), lambda i, j, k: (i % N_CHIPS, i // N_CHIPS, k, 0),
        h, (tk, D_MODEL), lambda i, j, k: (k, 0),
        _sds((N_CHIPS, 2, F_SHARD, D_MODEL), BF16), (None, None, F_SHARD, D_MODEL),
        lambda i, j, k: (i % N_CHIPS, i // N_CHIPS, 0, 0),
        (F_SHARD, D_MODEL), deps=hooks.after_wdown(gb_down),
    )
    assert base % 2 == 0
    dh_deps = hooks.after_wgu(gb_gu)

    def d_h(dab_ref, w_ref, *rest):
        o_ref, acc_ref = rest[-2:]
        c = pl.program_id(1)
        part = _dot(dab_ref[0], w_ref[0], "nn") + _dot(dab_ref[1], w_ref[1], "nn")

        @pl.when(c == 0)
        def _():
            acc_ref[...] = part

        @pl.when((c > 0) & (c < N_CHIPS - 1))
        def _():
            acc_ref[...] += part

        @pl.when(c == N_CHIPS - 1)
        def _():
            o_ref[...] = acc_ref[...] + part

    dh = pl.pallas_call(
        d_h,
        name=f"{tag}_bwd_dh",
        grid=(s_len // tm, N_CHIPS),
        in_specs=[
            pl.BlockSpec((None, 2, tm, F_SHARD), lambda i, c: (c, 0, i, 0)),
            pl.BlockSpec((None, 2, F_SHARD, D_MODEL), lambda i, c: (c, base // 2, 0, 0)),
        ] + [pl.BlockSpec(memory_space=pl.ANY)] * len(dh_deps),
        out_specs=pl.BlockSpec((tm, D_MODEL), lambda i, c: (i, 0)),
        out_shape=_sds((s_len, D_MODEL), F32),
        scratch_shapes=[pltpu.VMEM((tm, D_MODEL), F32)],
        compiler_params=_cparams(("parallel", "arbitrary")),
    )(dab4, g4, *dh_deps)
    dx, dsc, dsh, dgamma = _normmod_bwd_call(
        f"{tag}_bwd_normmod", [dh], x, dxo, gamma, sc, deps=hooks.after_dh(dh)
    )
    return dx, (gb_down, gb_gu), (dsh, dsc, dgate, dgamma)


def _bucket_map():
    qi = np.arange(WINDOW)[:, None]
    kj = np.arange(2 * WINDOW)[None, :]
    dist = qi + WINDOW - kj
    band = (dist >= 0) & (dist < WINDOW)
    max_exact = NUM_BUCKETS // 2
    n = np.maximum(dist, 0)
    large = []
    for dt in (np.float32, np.float64):
        nf = np.maximum(n, 1).astype(dt)
        val = np.log(nf / dt(max_exact)) / dt(math.log(MAX_DISTANCE / max_exact)) * dt(NUM_BUCKETS - max_exact)
        large.append(np.minimum(max_exact + val.astype(np.int32), NUM_BUCKETS - 1))
    assert np.array_equal(large[0], large[1])
    bucket = np.where(n < max_exact, n, large[0])
    return np.where(band, bucket, -1).astype(np.int32).reshape(1, -1)


def _bias_expand(rel_bias_t, bkt):
    n = bkt.shape[1]

    def body(rb_ref, bkt_ref, o_ref):
        onehot = (lax.broadcasted_iota(jnp.int32, (NUM_BUCKETS, n), 0) == bkt_ref[...]).astype(F32)
        o_ref[...] = lax.dot_general(
            rb_ref[...], onehot, _DN["nn"], precision=lax.Precision.HIGHEST, preferred_element_type=F32
        )

    return pl.pallas_call(
        body, name="bias_expand", out_shape=_sds((SWA_HEADS, n), F32), compiler_params=_cparams()
    )(rel_bias_t, bkt)


def _bias_reduce(dbias_flat, bkt):
    n = bkt.shape[1]

    def body(db_ref, bkt_ref, o_ref):
        onehot = (lax.broadcasted_iota(jnp.int32, (NUM_BUCKETS, n), 0) == bkt_ref[...]).astype(F32)
        o_ref[...] = lax.dot_general(
            db_ref[...], onehot, _DN["nt"], precision=lax.Precision.HIGHEST, preferred_element_type=F32
        )

    return pl.pallas_call(
        body, name="bias_reduce", out_shape=_sds((SWA_HEADS, NUM_BUCKETS), F32), compiler_params=_cparams()
    )(dbias_flat, bkt)


_SWA_SCALE = SWA_HEAD_DIM ** -0.5
_NEG = -1e30


_SWA_PAIR = 2


def _swa_in_specs():
    w = WINDOW
    n_q = SWA_HEADS * SWA_HEAD_DIM
    n_kv = 2 * SWA_KV_HEADS * SWA_HEAD_DIM
    prev = lambda m: jnp.maximum(_SWA_PAIR * m - 1, 0)
    return [
        pl.BlockSpec((_SWA_PAIR * w, n_q), lambda m: (m, 0)),
        pl.BlockSpec((w, n_kv), lambda m: (prev(m), n_q // n_kv)),
        pl.BlockSpec((_SWA_PAIR * w, n_kv), lambda m: (m, n_q // n_kv)),
        pl.BlockSpec((SWA_HEADS, w, 2 * w), lambda m: (0, 0, 0)),
        pl.BlockSpec((SWA_HEADS, w, 1), lambda m: (0, 0, 0)),
    ]


def _head_cols(v, first, count):
    hd = SWA_HEAD_DIM
    return jnp.stack([v[:, (first + i) * hd:(first + i + 1) * hd] for i in range(count)])


def _swa_blocks(q_ref, kvp_ref, kvc_ref, m):
    g, w, hd = SWA_GROUP, WINDOW, SWA_HEAD_DIM
    kv_all = jnp.concatenate([kvp_ref[...], kvc_ref[...]], axis=0).astype(F32)
    blocks = []
    for sub in range(_SWA_PAIR):
        rows = slice(sub * w, (sub + 1) * w)
        q = q_ref[rows, :].astype(F32)
        kv = kv_all[sub * w:(sub + 2) * w]
        heads = []
        for j in range(SWA_KV_HEADS):
            qj = _head_cols(q, g * j, g).reshape(g * w, hd)
            heads.append((qj, _head_cols(kv, j, 1)[0], _head_cols(kv, SWA_KV_HEADS + j, 1)[0]))
        blocks.append((_SWA_PAIR * m + sub, rows, heads))
    return blocks


def _swa_scores(q, kk, bias, n):
    g, w = SWA_GROUP, WINDOW
    s = (_dot(q, kk, "nt") * _SWA_SCALE).reshape(g, w, 2 * w) + bias
    qi = lax.broadcasted_iota(jnp.int32, (w, 2 * w), 0)
    kj = lax.broadcasted_iota(jnp.int32, (w, 2 * w), 1)
    dist = qi + w - kj
    valid = ((dist >= 0) & (dist < w) & ((n > 0) | (kj >= w)))[None]
    return jnp.where(valid, s, _NEG), valid


def _swa_fwd(swa2, bias, sinkb):
    s_len = swa2.shape[0]
    g, w, hd = SWA_GROUP, WINDOW, SWA_HEAD_DIM

    def body(q_ref, kvp_ref, kvc_ref, bias_ref, sink_ref, o_ref, lse_ref):
        for n, rows, heads in _swa_blocks(q_ref, kvp_ref, kvc_ref, pl.program_id(0)):
            outs = []
            for j, (q, kk, vv) in enumerate(heads):
                hs = slice(g * j, g * (j + 1))
                s, valid = _swa_scores(q, kk, bias_ref[hs], n)
                sink = sink_ref[hs]
                m = jnp.maximum(jnp.max(s, axis=-1, keepdims=True), sink)
                p = jnp.where(valid, jnp.exp(s - m), 0.0)
                l = jnp.sum(p, axis=-1, keepdims=True) + jnp.exp(sink - m)
                o = _dot(p.reshape(g * w, 2 * w), vv, "nn").reshape(g, w, hd) / l
                outs += [o[i] for i in range(g)]
                lse_ref[hs, rows, :] = m + jnp.log(l)
            o_ref[rows, :] = jnp.concatenate(outs, axis=-1).astype(o_ref.dtype)

    n_q = SWA_HEADS * hd
    return pl.pallas_call(
        body,
        name="swa_fwd",
        grid=(s_len // (_SWA_PAIR * w),),
        in_specs=_swa_in_specs(),
        out_specs=[
            pl.BlockSpec((_SWA_PAIR * w, n_q), lambda m: (m, 0)),
            pl.BlockSpec((SWA_HEADS, _SWA_PAIR * w, 1), lambda m: (0, m, 0)),
        ],
        out_shape=[_sds((s_len, n_q), BF16), _sds((SWA_HEADS, s_len, 1), F32)],
        compiler_params=_cparams(("parallel",)),
    )(swa2, swa2, swa2, bias, sinkb)


def _swa_bwd(swa2, bias, sinkb, cat, dcat, lse):
    s_len = swa2.shape[0]
    g, w, hd = SWA_GROUP, WINDOW, SWA_HEAD_DIM

    def body(q_ref, kvp_ref, kvc_ref, bias_ref, sink_ref, o_ref, do_ref, lse_ref,
             dq_ref, dkva_ref, dkvb_ref, dbias_ref, dsink_ref):
        step = pl.program_id(0)

        @pl.when(step == 0)
        def _():
            dbias_ref[...] = jnp.zeros_like(dbias_ref)
            dsink_ref[...] = jnp.zeros_like(dsink_ref)

        for n, rows, heads in _swa_blocks(q_ref, kvp_ref, kvc_ref, step):
            o_all = o_ref[rows, :].astype(F32)
            do_all = do_ref[rows, :].astype(F32)
            dqs, dks, dvs = [], [], []
            for j, (q, kk, vv) in enumerate(heads):
                hs = slice(g * j, g * (j + 1))
                s, valid = _swa_scores(q, kk, bias_ref[hs], n)
                lse_v = lse_ref[hs, rows, :]
                p = jnp.where(valid, jnp.exp(s - lse_v), 0.0)
                do = _head_cols(do_all, g * j, g)
                delta = jnp.sum(do * _head_cols(o_all, g * j, g), axis=-1, keepdims=True)
                do2 = do.reshape(g * w, hd)
                dp = _dot(do2, vv, "nt").reshape(g, w, 2 * w)
                ds = p * (dp - delta)
                dbias_ref[hs] += ds
                dsink_ref[hs] -= jnp.exp(sink_ref[hs] - lse_v) * delta
                ds2 = ds.reshape(g * w, 2 * w)
                dq = (_dot(ds2, kk, "nn") * _SWA_SCALE).reshape(g, w, hd)
                dqs += [dq[i] for i in range(g)]
                dks.append(_dot(ds2, q, "tn") * _SWA_SCALE)
                dvs.append(_dot(p.reshape(g * w, 2 * w), do2, "tn"))
            dq_ref[rows, :] = jnp.concatenate(dqs, axis=-1).astype(dq_ref.dtype)
            dkv = jnp.concatenate(dks + dvs, axis=-1)
            dkvb_ref[rows, :] = dkv[:w]
            dkva_ref[rows, :] = dkv[w:]

    n_q = SWA_HEADS * hd
    n_kv = 2 * SWA_KV_HEADS * hd
    q_spec = pl.BlockSpec((_SWA_PAIR * w, n_q), lambda m: (m, 0))
    kv_spec = pl.BlockSpec((_SWA_PAIR * w, n_kv), lambda m: (m, 0))
    return pl.pallas_call(
        body,
        name="swa_bwd",
        grid=(s_len // (_SWA_PAIR * w),),
        in_specs=_swa_in_specs() + [q_spec, q_spec, pl.BlockSpec((SWA_HEADS, _SWA_PAIR * w, 1), lambda m: (0, m, 0))],
        out_specs=[
            q_spec, kv_spec, kv_spec,
            pl.BlockSpec((SWA_HEADS, w, 2 * w), lambda n: (0, 0, 0)),
            pl.BlockSpec((SWA_HEADS, w, 1), lambda n: (0, 0, 0)),
        ],
        out_shape=[
            _sds((s_len, n_q), BF16), _sds((s_len, n_kv), F32), _sds((s_len, n_kv), F32),
            _sds((SWA_HEADS, w, 2 * w), F32), _sds((SWA_HEADS, w, 1), F32),
        ],
        compiler_params=_cparams(("arbitrary",)),
    )(swa2, swa2, swa2, bias, sinkb, cat, dcat, lse)


_MLA_SCALE = MLA_QK ** -0.5
_MLA_TQ = 256
_MLA_FWD_HEADS = 4
_MLA_BWD_HEADS = 2


def _mla_mask(i, tq, n_keys):
    row = i * tq + lax.broadcasted_iota(jnp.int32, (tq, n_keys), 0)
    col = lax.broadcasted_iota(jnp.int32, (tq, n_keys), 1)
    return col <= row


def _per_query_block(i, n_blocks, fn):
    for ii in range(n_blocks):
        pl.when(i == ii)(functools.partial(fn, ii))


def _mla_fwd(q4, k4, v4):
    s_len = q4.shape[1]
    tq = min(_MLA_TQ, s_len)
    pair = _MLA_FWD_HEADS

    def body(q_ref, k_ref, v_ref, o_ref, lse_ref):
        def block(ii):
            n_keys = (ii + 1) * tq
            mask = _mla_mask(ii, tq, n_keys)
            for hh in range(pair):
                s = jnp.where(mask, _dot(q_ref[hh], k_ref[hh, :n_keys, :], "nt") * _MLA_SCALE, _NEG)
                m = jnp.max(s, axis=-1, keepdims=True)
                p = jnp.exp(s - m)
                l = jnp.sum(p, axis=-1, keepdims=True)
                o_ref[:, hh * MLA_V:(hh + 1) * MLA_V] = (_dot(p, v_ref[hh, :n_keys, :], "nn") / l).astype(o_ref.dtype)
                lse_ref[hh] = m + jnp.log(l)

        _per_query_block(pl.program_id(1), s_len // tq, block)

    return pl.pallas_call(
        body,
        name="mla_fwd",
        grid=(MLA_HEADS // pair, s_len // tq),
        in_specs=[
            pl.BlockSpec((pair, tq, MLA_QK), lambda h, i: (h, i, 0)),
            pl.BlockSpec((pair, s_len, MLA_QK), lambda h, i: (h, 0, 0)),
            pl.BlockSpec((pair, s_len, MLA_V), lambda h, i: (h, 0, 0)),
        ],
        out_specs=[
            pl.BlockSpec((tq, pair * MLA_V), lambda h, i: (i, h)),
            pl.BlockSpec((pair, tq, 1), lambda h, i: (h, i, 0)),
        ],
        out_shape=[_sds((s_len, MLA_HEADS * MLA_V), BF16), _sds((MLA_HEADS, s_len, 1), F32)],
        compiler_params=_cparams(("parallel", "parallel")),
    )(q4, k4, v4)


def _mla_bwd(q4, k4, v4, cat, dcat, lse):
    s_len = q4.shape[1]
    tq = min(_MLA_TQ, s_len)
    pair = _MLA_BWD_HEADS
    first = SWA_HEADS * SWA_HEAD_DIM // (pair * MLA_V)

    def body(q_ref, k_ref, v_ref, o_ref, do_ref, lse_ref, dq_ref, dk_ref, dv_ref):
        i = pl.program_id(1)

        @pl.when(i == 0)
        def _():
            dk_ref[...] = jnp.zeros_like(dk_ref)
            dv_ref[...] = jnp.zeros_like(dv_ref)

        def block(ii):
            n_keys = (ii + 1) * tq
            mask = _mla_mask(ii, tq, n_keys)
            for hh in range(pair):
                cols = slice(hh * MLA_V, (hh + 1) * MLA_V)
                q, k, v, do = q_ref[hh], k_ref[hh, :n_keys, :], v_ref[hh, :n_keys, :], do_ref[:, cols]
                s = jnp.where(mask, _dot(q, k, "nt") * _MLA_SCALE, _NEG)
                p = jnp.where(mask, jnp.exp(s - lse_ref[hh]), 0.0)
                delta = jnp.sum(do.astype(F32) * o_ref[:, cols].astype(F32), axis=-1, keepdims=True)
                ds = (p * (_dot(do, v, "nt") - delta) * _MLA_SCALE).astype(BF16)
                dq_ref[hh] = _dot(ds, k, "nn")
                dk_ref[hh, :n_keys, :] += _dot(ds, q, "tn")
                dv_ref[hh, :n_keys, :] += _dot(p, do, "tn")

        _per_query_block(i, s_len // tq, block)

    return pl.pallas_call(
        body,
        name="mla_bwd",
        grid=(MLA_HEADS // pair, s_len // tq),
        in_specs=[
            pl.BlockSpec((pair, tq, MLA_QK), lambda h, i: (h, i, 0)),
            pl.BlockSpec((pair, s_len, MLA_QK), lambda h, i: (h, 0, 0)),
            pl.BlockSpec((pair, s_len, MLA_V), lambda h, i: (h, 0, 0)),
            pl.BlockSpec((tq, pair * MLA_V), lambda h, i: (i, first + h)),
            pl.BlockSpec((tq, pair * MLA_V), lambda h, i: (i, first + h)),
            pl.BlockSpec((pair, tq, 1), lambda h, i: (h, i, 0)),
        ],
        out_specs=[
            pl.BlockSpec((pair, tq, MLA_QK), lambda h, i: (h, i, 0)),
            pl.BlockSpec((pair, s_len, MLA_QK), lambda h, i: (h, 0, 0)),
            pl.BlockSpec((pair, s_len, MLA_V), lambda h, i: (h, 0, 0)),
        ],
        out_shape=[
            _sds((MLA_HEADS, s_len, MLA_QK), F32),
            _sds((MLA_HEADS, s_len, MLA_QK), F32),
            _sds((MLA_HEADS, s_len, MLA_V), F32),
        ],
        compiler_params=_cparams(("parallel", "arbitrary")),
    )(q4, k4, v4, cat, dcat, lse)


def _mla_split(pm):
    q_lat = pm[:, :MLA_Q_RANK]
    kv_lat = pm[:, MLA_Q_RANK:MLA_Q_RANK + MLA_KV_RANK]
    kr = pm[:, MLA_Q_RANK + MLA_KV_RANK:MLA_Q_RANK + MLA_KV_RANK + MLA_ROPE]
    kr_sw = pm[:, MLA_Q_RANK + MLA_KV_RANK + MLA_ROPE:]
    return q_lat, kv_lat, kr, kr_sw


def _mla_proj(pm, cos2, sin2, q_norm, kv_norm, wq_ext, wkv):
    s_len = pm.shape[0]

    def fn(pm_, cos_, sin_, qg, kvg, wq, wk):
        q_lat, kv_lat, kr, kr_sw = _mla_split(pm_)
        nq = (_rms(q_lat)[0] * qg).astype(BF16)
        nkv = (_rms(kv_lat)[0] * kvg).astype(BF16)
        k_rot = kr * cos_ + kr_sw * sin_
        qs, ks, vs = [], [], []
        for h in range(MLA_HEADS):
            qe = _dot(nq, wq[h], "nt")
            q_rot = qe[:, MLA_NOPE:MLA_QK] * cos_ + qe[:, MLA_QK:] * sin_
            qs.append(jnp.concatenate([qe[:, :MLA_NOPE], q_rot], axis=-1))
            kve = _dot(nkv, wk[h], "nt")
            ks.append(jnp.concatenate([kve[:, :MLA_NOPE], k_rot], axis=-1))
            vs.append(kve[:, MLA_NOPE:])
        return jnp.stack(qs), jnp.stack(ks), jnp.stack(vs)

    return _rowwise(
        "mla_proj", fn, [pm, cos2, sin2], [q_norm, kv_norm, wq_ext, wkv],
        [_sds((MLA_HEADS, s_len, MLA_QK), BF16), _sds((MLA_HEADS, s_len, MLA_QK), BF16),
         _sds((MLA_HEADS, s_len, MLA_V), BF16)], [], ELEM_TILE,
    )


def _mla_proj_bwd(pm, cos2, sin2, dq4, dk4, dv4, q_norm, kv_norm, wq_ext, wkv):
    s_len = pm.shape[0]

    def fn(pm_, cos_, sin_, dq, dk, dv, qg, kvg, wq, wk):
        q_lat, kv_lat, _, _ = _mla_split(pm_)
        xq, rq = _rms(q_lat)
        xkv, rkv = _rms(kv_lat)
        nq = (xq * qg).astype(BF16)
        nkv = (xkv * kvg).astype(BF16)
        dnq = jnp.zeros_like(q_lat)
        dnkv = jnp.zeros_like(kv_lat)
        dkr = jnp.zeros_like(cos_)
        dwq, dwk = [], []
        for h in range(MLA_HEADS):
            dy = dq[h][:, MLA_NOPE:]
            dqe = jnp.concatenate([dq[h][:, :MLA_NOPE], dy * cos_, dy * sin_], axis=-1).astype(BF16)
            dnq = dnq + _dot(dqe, wq[h], "nn")
            dwq.append(_dot(dqe, nq, "tn"))
            dkve = jnp.concatenate([dk[h][:, :MLA_NOPE], dv[h]], axis=-1).astype(BF16)
            dnkv = dnkv + _dot(dkve, wk[h], "nn")
            dwk.append(_dot(dkve, nkv, "tn"))
            dkr = dkr + dk[h][:, MLA_NOPE:]
        dq_lat = _rms_bwd(dnq * qg, xq, rq)
        dkv_lat = _rms_bwd(dnkv * kvg, xkv, rkv)
        dpm = jnp.concatenate([dq_lat, dkv_lat, dkr * cos_, dkr * sin_], axis=-1)
        return dpm, _sum0(dnq * xq), _sum0(dnkv * xkv), jnp.stack(dwq), jnp.stack(dwk)

    return _rowwise(
        "mla_proj_bwd", fn, [pm, cos2, sin2, dq4, dk4, dv4], [q_norm, kv_norm, wq_ext, wkv],
        [_sds((s_len, N_MLA_COLS), BF16)],
        [_sds((1, MLA_Q_RANK), F32), _sds((1, MLA_KV_RANK), F32),
         _sds(wq_ext.shape, F32), _sds(wkv.shape, F32)], ELEM_TILE,
    )


def _rope_tables(s_len):
    inv = ROPE_THETA ** (-jnp.arange(0, MLA_ROPE, 2, dtype=F32) / MLA_ROPE)
    ang = jnp.arange(s_len, dtype=F32)[:, None] * inv[None, :]
    cos, sin = jnp.cos(ang), jnp.sin(ang)
    return jnp.concatenate([cos, cos], axis=-1), jnp.concatenate([-sin, sin], axis=-1)


def _mix_weights(g_mix):
    rest = g_mix[:, 0]
    w_in_t = rest[:, O_IN:O_IN + R_IN].reshape(D_IN, D_MODEL)
    w_o = rest[:, O_O:O_O + R_O].reshape(D_MODEL, D_MODEL)
    w_uq_t = rest[:, O_UQ:O_UQ + R_UQ].reshape(MLA_HEADS, MLA_QK, MLA_Q_RANK)
    w_ukv_t = rest[:, O_UKV:O_UKV + R_UKV].reshape(MLA_HEADS, MLA_NOPE + MLA_V, MLA_KV_RANK)
    half = MLA_ROPE // 2
    w_swa = w_in_t[:N_SWA_COLS]
    w_mla = jnp.concatenate([w_in_t[N_SWA_COLS:], w_in_t[D_IN - half:], w_in_t[D_IN - MLA_ROPE:D_IN - half]], axis=0)
    wq_ext = jnp.concatenate([w_uq_t, w_uq_t[:, MLA_QK - half:], w_uq_t[:, MLA_NOPE:MLA_QK - half]], axis=1)
    return w_swa, w_mla, w_o, wq_ext, w_ukv_t


def _mix_fwd(x, h, gate, mix_src, q_norm, kv_norm, bias, sinkb, cos2, sin2, next_norm):
    s_len = x.shape[0]
    tm = min(ROW_TILE, s_len)
    deps = mix_src.mid(x)
    weights = _mix_weights(mix_src.get(h))
    w_swa, w_mla, w_o, wq_ext, wkv = weights
    swa2 = _mm(
        "mix_proj_swa", "nt", (s_len // tm, 1, 1),
        h, (tm, D_MODEL), lambda i, j, k: (i, 0),
        w_swa, (N_SWA_COLS, D_MODEL), lambda i, j, k: (0, 0),
        _sds((s_len, N_SWA_COLS), BF16), (tm, N_SWA_COLS), lambda i, j, k: (i, 0),
        (tm, N_SWA_COLS), deps=deps,
    )
    pm = _mm(
        "mix_proj_mla", "nt", (s_len // tm, 1, 1),
        h, (tm, D_MODEL), lambda i, j, k: (i, 0),
        w_mla, (N_MLA_COLS, D_MODEL), lambda i, j, k: (0, 0),
        _sds((s_len, N_MLA_COLS), F32), (tm, N_MLA_COLS), lambda i, j, k: (i, 0),
        (tm, N_MLA_COLS),
    )
    q4, k4, v4 = _mla_proj(pm, cos2, sin2, q_norm, kv_norm, wq_ext, wkv)
    oa, lse_a = _swa_fwd(swa2, bias, sinkb)
    ob, lse_b = _mla_fwd(q4, k4, v4)
    cat = jnp.concatenate([oa, ob], axis=1)
    z = _mm(
        "mix_out", "nn", (s_len // tm, 1, 1),
        cat, (tm, D_MODEL), lambda i, j, k: (i, 0),
        w_o, (D_MODEL, D_MODEL), lambda i, j, k: (0, 0),
        _sds((s_len, D_MODEL), F32), (tm, D_MODEL), lambda i, j, k: (i, 0),
        (tm, D_MODEL),
    )
    def residual_norm(x_, z_, g_, gamma_, sc_, sh_):
        x_out = x_ + g_ * z_
        return x_out, _normmod(x_out, gamma_, sc_, sh_)

    out = _rowwise(
        "mix_residual_norm", residual_norm, [x, z], [gate] + list(next_norm),
        [_sds((s_len, D_MODEL), F32), _sds((s_len, D_MODEL), BF16)], [], ELEM_TILE,
    )
    return out, (weights, h, swa2, pm, q4, k4, v4, cat, lse_a, lse_b, z)


def _mix_bwd(dxo, x, gamma, sc, gate, q_norm, kv_norm, bias, sinkb, cos2, sin2, saved, hooks):
    weights, h, swa2, pm, q4, k4, v4, cat, lse_a, lse_b, z = saved
    w_swa, w_mla, w_o, wq_ext, wkv = weights
    s_len = x.shape[0]
    tm = tk = min(ROW_TILE, s_len)
    vec = _sds((1, D_MODEL), F32)
    n_proj = N_SWA_COLS + N_MLA_COLS
    half_d = D_MODEL // 2

    dz, dgate = _rowwise(
        "mix_bwd_gate", lambda dxo_, z_, g_: (g_ * dxo_, _sum0(z_ * dxo_)),
        [dxo, z], [gate], [_sds((s_len, D_MODEL), BF16)], [vec], ELEM_TILE,
    )
    dw_o = _mm(
        "mix_bwd_wo", "tn", (2, 1, s_len // tk),
        cat, (tk, half_d), lambda i, j, k: (k, i),
        dz, (tk, D_MODEL), lambda i, j, k: (k, 0),
        _sds((D_MODEL, D_MODEL), F32), (half_d, D_MODEL), lambda i, j, k: (i, 0),
        (half_d, D_MODEL),
    )
    dcat = _mm(
        "mix_bwd_dcat", "nt", (s_len // tm, 1, 1),
        dz, (tm, D_MODEL), lambda i, j, k: (i, 0),
        w_o, (D_MODEL, D_MODEL), lambda i, j, k: (0, 0),
        _sds((s_len, D_MODEL), BF16), (tm, D_MODEL), lambda i, j, k: (i, 0),
        (tm, D_MODEL), deps=hooks.after_gate(dz),
    )
    dq_a, dkva, dkvb, dbias, dsink = _swa_bwd(swa2, bias, sinkb, cat, dcat, lse_a)
    dq4, dk4, dv4 = _mla_bwd(q4, k4, v4, cat, dcat, lse_b)
    dpm, dq_norm, dkv_norm, dwq_ext, dwkv = _mla_proj_bwd(pm, cos2, sin2, dq4, dk4, dv4, q_norm, kv_norm, wq_ext, wkv)

    dkv = dkva + jnp.concatenate([dkvb[WINDOW:], jnp.zeros_like(dkvb[:WINDOW])], axis=0)
    dproj = jnp.concatenate([dq_a, dkv.astype(BF16), dpm], axis=1)
    w_proj = jnp.concatenate([w_swa, w_mla], axis=0)

    dw_proj = _mm(
        "mix_bwd_win", "tn", (n_proj // 256, 1, s_len // tk),
        dproj, (tk, 256), lambda i, j, k: (k, i),
        h, (tk, D_MODEL), lambda i, j, k: (k, 0),
        _sds((n_proj, D_MODEL), F32), (256, D_MODEL), lambda i, j, k: (i, 0),
        (256, D_MODEL),
    )
    dw_swa, dw_mla = dw_proj[:N_SWA_COLS], dw_proj[N_SWA_COLS:]
    dh = _mm(
        "mix_bwd_dh", "nn", (s_len // tm, 1, 1),
        dproj, (tm, n_proj), lambda i, j, k: (i, 0),
        w_proj, (n_proj, D_MODEL), lambda i, j, k: (0, 0),
        _sds((s_len, D_MODEL), F32), (tm, D_MODEL), lambda i, j, k: (i, 0),
        (tm, D_MODEL),
    )
    dx, dsc, dsh, dgamma = _normmod_bwd_call("mix_bwd_normmod", [dh], x, dxo, gamma, sc)

    half = MLA_ROPE // 2
    n_mla_in = D_IN - N_SWA_COLS
    dw_mla_base = dw_mla[:n_mla_in]
    dw_mla_base = dw_mla_base.at[n_mla_in - half:].add(dw_mla[n_mla_in:n_mla_in + half])
    dw_mla_base = dw_mla_base.at[n_mla_in - MLA_ROPE:n_mla_in - half].add(dw_mla[n_mla_in + half:])
    dw_in_t = jnp.concatenate([dw_swa, dw_mla_base], axis=0)
    dw_uq_t = dwq_ext[:, :MLA_QK]
    dw_uq_t = dw_uq_t.at[:, MLA_QK - half:].add(dwq_ext[:, MLA_QK:MLA_QK + half])
    dw_uq_t = dw_uq_t.at[:, MLA_NOPE:MLA_QK - half].add(dwq_ext[:, MLA_QK + half:])
    rest = jnp.concatenate(
        [
            dw_in_t.reshape(N_CHIPS, R_IN, D_MODEL),
            dw_o.reshape(N_CHIPS, R_O, D_MODEL),
            dw_uq_t.reshape(N_CHIPS, R_UQ, D_MODEL),
            dwkv.reshape(N_CHIPS, R_UKV, D_MODEL),
            jnp.zeros((N_CHIPS, F_SHARD - O_PAD, D_MODEL), F32),
        ],
        axis=1,
    ).astype(BF16)
    return dx, rest, (dsh, dsc, dgate, dgamma), dq_norm, dkv_norm, dbias, dsink


def _device_step(x, target, mod, gu1_src, down1_src, mix_src, ffn2_src, norms, q_norm, kv_norm, sinks, rel_bias,
                 hooks2=None, hooks_mix=None, mix_done=None, hooks1=None):
    s_len = x.shape[0]
    sh1, sc1, g1, sh2, sc2, g2, sh3, sc3, g3 = [mod[:, i * D_MODEL:(i + 1) * D_MODEL] for i in range(N_MOD)]
    n1, n2, n3, nf = norms
    bkt = jnp.asarray(_bucket_map())
    bias = _bias_expand(rel_bias.T, bkt).reshape(SWA_HEADS, WINDOW, 2 * WINDOW)
    sinkb = jnp.broadcast_to(sinks.reshape(SWA_HEADS, 1, 1), (SWA_HEADS, WINDOW, 1))
    cos2, sin2 = _rope_tables(s_len)

    (x1, h2), saved1 = _ffn_fwd(
        "ffn1", x, n1, sh1, sc1, g1, gu1_src, 0, down1_src, 0, sh1, next_norm=(n2, sc2, sh2)
    )
    (x2, h3), saved2 = _mix_fwd(
        x1, h2, g2, mix_src, q_norm, kv_norm, bias, sinkb, cos2, sin2, next_norm=(n3, sc3, sh3)
    )
    (dx3, sq, dnf), saved3 = _ffn_fwd(
        "ffn2", x2, n3, sh3, sc3, g3, ffn2_src, 0, None, 2, x2, h_pre=h3, head=(target, nf)
    )
    loss_part = (0.5 / D_MODEL) * jnp.sum(sq)

    dx2, gb2, (dsh3, dsc3, dg3, dn3) = _ffn_bwd("ffn2", dx3, x2, n3, sc3, g3, saved3, hooks2 or _BwdHooks())
    dx1, rest, (dsh2, dsc2, dg2, dn2), dq_norm, dkv_norm, dbias, dsink = _mix_bwd(
        dx2, x1, n2, sc2, g2, q_norm, kv_norm, bias, sinkb, cos2, sin2, saved2, hooks_mix or _BwdHooks()
    )
    rest = rest[:, None]
    deps1 = mix_done(dx1, rest) if mix_done is not None else []
    dx0, gb1, (dsh1, dsc1, dg1, dn1) = _ffn_bwd(
        "ffn1", dx1, x, n1, sc1, g1, saved1, hooks1 or _BwdHooks(), deps=deps1
    )

    dmod = jnp.concatenate([dsh1, dsc1, dg1, dsh2, dsc2, dg2, dsh3, dsc3, dg3], axis=-1)
    drel = _bias_reduce(dbias.reshape(SWA_HEADS, -1), bkt).T
    dsinks = jnp.sum(dsink, axis=(1, 2)).reshape(1, SWA_HEADS)
    small = (dn1, dn2, dn3, dnf, dq_norm, dkv_norm, dsinks, drel)
    return loss_part, dx0, (gb1, gb2, rest), dmod, small


_MESH = pl.DeviceIdType.MESH


def _place():
    return lax.axis_index("x"), lax.axis_index("y"), lax.axis_index("c")


def _other_chips(x, y):
    return [(1 - x, y), (x, 1 - y), (1 - x, 1 - y)]


def _allgather_small(name, blk):
    m_per, n = blk.shape

    def body(x_ref, out_ref, send_sems, recv_sems, local_sem):
        x, y, c = _place()
        me, sibling = (x, y, c), (x, y, 1 - c)
        chips = _other_chips(x, y)

        def rows(px, py, pc):
            return out_ref.at[pl.ds((4 * px + 2 * py + pc) * m_per, m_per), :]

        def copy(k, block, to, src=None):
            return pltpu.make_async_remote_copy(
                src_ref=rows(*block) if src is None else src, dst_ref=rows(*block),
                send_sem=send_sems.at[k], recv_sem=recv_sems.at[k], device_id=to, device_id_type=_MESH,
            )

        mine = pltpu.make_async_copy(x_ref, rows(*me), local_sem)
        mine.start()
        first = [copy(0, me, sibling, src=x_ref)]
        first += [copy(1 + j, me, (*chip, c), src=x_ref) for j, chip in enumerate(chips)]
        for cp in first:
            cp.start()
        passed = [copy(4 + j, (*chip, c), sibling) for j, chip in enumerate(chips)]
        for j, chip in enumerate(chips):
            copy(1 + j, (*chip, c), me).wait_recv()
            passed[j].start()
        copy(0, sibling, me).wait_recv()
        for j, chip in enumerate(chips):
            copy(4 + j, (*chip, 1 - c), me).wait_recv()
        for cp in first + passed:
            cp.wait_send()
        mine.wait()

    return pl.pallas_call(
        body,
        name=name,
        out_shape=_sds((N_DEV * m_per, n), blk.dtype),
        in_specs=[pl.BlockSpec(memory_space=pltpu.VMEM)],
        out_specs=pl.BlockSpec(memory_space=pltpu.VMEM),
        scratch_shapes=[pltpu.SemaphoreType.DMA((7,)), pltpu.SemaphoreType.DMA((7,)), pltpu.SemaphoreType.DMA],
    )(blk)


def _gather_sends(n, own_ref, g_ref, send_sem, recv_sem, x, y, c, chip):
    return [
        pltpu.make_async_remote_copy(
            src_ref=own_ref.at[p, pl.ds(c * HALF, HALF), :], dst_ref=g_ref.at[2 * x + y, p, pl.ds(c * HALF, HALF), :],
            send_sem=send_sem, recv_sem=recv_sem, device_id=(*chip, c), device_id_type=_MESH,
        )
        for p in range(n)
    ]


def _gather_forwards(n, g_ref, send_sem, recv_sem, chip, c, sibling):
    return [
        pltpu.make_async_remote_copy(
            src_ref=g_ref.at[2 * chip[0] + chip[1], p, pl.ds(c * HALF, HALF), :],
            dst_ref=g_ref.at[2 * chip[0] + chip[1], p, pl.ds(c * HALF, HALF), :],
            send_sem=send_sem, recv_sem=recv_sem, device_id=sibling, device_id_type=_MESH,
        )
        for p in range(n)
    ]


def _gather_all(own_ref, g_ref, send_sem, recv_sem, chip, half, c):
    return pltpu.make_async_remote_copy(
        src_ref=own_ref.at[:, pl.ds(c * HALF, HALF), :],
        dst_ref=g_ref.at[2 * chip[0] + chip[1], :, pl.ds(half * HALF, HALF), :],
        send_sem=send_sem, recv_sem=recv_sem, device_id=(*chip, c), device_id_type=_MESH,
    )


_HBM_SPEC = pl.BlockSpec(memory_space=pltpu.HBM)
_SEM_SPEC = pl.BlockSpec(memory_space=pltpu.SEMAPHORE)
_ANY_SPEC = pl.BlockSpec(memory_space=pl.ANY)
_VMEM_SPEC = pl.BlockSpec(memory_space=pltpu.VMEM)
_SPLIT_PARAMS = pltpu.CompilerParams(has_side_effects=pltpu.SideEffectType.DATAFLOW_SIDE_EFFECTING)
_TOKEN = jax.ShapeDtypeStruct((8, 128), F32)


def _in_hbm(a):
    return pltpu.with_memory_space_constraint(a, pltpu.HBM)


class _GatherBehind:
    def __init__(self, tag, own, then=None):
        self.tag, self.n, self.own, self.then = tag, own.shape[0], own, then

    def start(self, after):
        tag, own, n = self.tag, self.own, self.n
        x, y, _ = _place()
        g_init = lax.dynamic_update_slice(
            lax.empty((N_CHIPS,) + own.shape, own.dtype), own[None], (2 * x + y, 0, 0, 0)
        )

        def body(own_ref, g_ref, *rest):
            send_sems, recv_sems, _, _, token = rest[len(after):]
            x, y, c = _place()
            for j, chip in enumerate(_other_chips(x, y)):
                for cp in _gather_sends(n, own_ref, g_ref, send_sems.at[j], recv_sems.at[j], x, y, c, chip):
                    cp.start()
            token[...] = jnp.zeros_like(token)

        self.send1, self.recv1, self.own, self.g, self.token = pl.pallas_call(
            body,
            name=f"{tag}_start",
            out_shape=(
                pltpu.SemaphoreType.DMA((3,)), pltpu.SemaphoreType.DMA((3,)),
                pltpu.HBM(own.shape, own.dtype), pltpu.HBM(g_init.shape, g_init.dtype), _TOKEN,
            ),
            in_specs=(_HBM_SPEC, _HBM_SPEC) + (_ANY_SPEC,) * len(after),
            out_specs=(_SEM_SPEC, _SEM_SPEC, _HBM_SPEC, _HBM_SPEC, _VMEM_SPEC),
            input_output_aliases={0: 2, 1: 3},
            compiler_params=_SPLIT_PARAMS,
        )(_in_hbm(own), _in_hbm(g_init), *after)

    def mid(self, after):
        n = self.n

        def body(own_ref, g_ref, send1, recv1, after_ref, send2, recv2, g_out, token):
            x, y, c = _place()
            sibling = (x, y, 1 - c)
            chips = _other_chips(x, y)
            for j, chip in enumerate(chips):
                _gather_all(own_ref, g_ref, send1.at[j], recv1.at[j], chip, c, c).wait_recv()
                for cp in _gather_forwards(n, g_ref, send2.at[j], recv2.at[j], chip, c, sibling):
                    cp.start()
            for j, chip in enumerate(chips):
                _gather_all(own_ref, g_ref, send1.at[j], recv1.at[j], chip, c, c).wait_send()
            token[...] = jnp.zeros_like(token)

        self.send2, self.recv2, self.g, token = pl.pallas_call(
            body,
            name=f"{self.tag}_mid",
            out_shape=(
                pltpu.SemaphoreType.DMA((3,)), pltpu.SemaphoreType.DMA((3,)),
                pltpu.HBM(self.g.shape, self.g.dtype), _TOKEN,
            ),
            in_specs=(_HBM_SPEC, _HBM_SPEC, _SEM_SPEC, _SEM_SPEC, _ANY_SPEC),
            out_specs=(_SEM_SPEC, _SEM_SPEC, _HBM_SPEC, _VMEM_SPEC),
            input_output_aliases={1: 2},
            compiler_params=_SPLIT_PARAMS,
        )(self.own, self.g, self.send1, self.recv1, after)
        if self.then is None:
            return [token]
        self.then.start([token])
        return [token, self.then.token]

    def get(self, after):
        def body(g_ref, send2, recv2, after_ref, g_out):
            x, y, c = _place()
            for j, chip in enumerate(_other_chips(x, y)):
                slot_in = g_ref.at[2 * chip[0] + chip[1], :, pl.ds((1 - c) * HALF, HALF), :]
                slot_out = g_ref.at[2 * chip[0] + chip[1], :, pl.ds(c * HALF, HALF), :]
                cp = pltpu.make_async_remote_copy(
                    src_ref=slot_out, dst_ref=slot_in, send_sem=send2.at[j], recv_sem=recv2.at[j],
                    device_id=(x, y, 1 - c), device_id_type=_MESH,
                )
                cp.wait_send()
                cp.wait_recv()

        return pl.pallas_call(
            body,
            name=f"{self.tag}_wait",
            out_shape=pltpu.HBM(self.g.shape, self.g.dtype),
            in_specs=(_HBM_SPEC, _SEM_SPEC, _SEM_SPEC, _ANY_SPEC),
            out_specs=_HBM_SPEC,
            input_output_aliases={0: 0},
            compiler_params=_SPLIT_PARAMS,
        )(self.g, self.send2, self.recv2, after)


def _pair_sum(name, gb, land):
    def body(c_ref, gb_ref, land_ref, o_ref):
        o_ref[...] = (gb_ref[...].astype(F32) + land_ref[...].astype(F32)).astype(o_ref.dtype)

    core = lax.axis_index("c").astype(jnp.int32).reshape(1)
    return pl.pallas_call(
        body,
        name=name,
        grid_spec=pltpu.PrefetchScalarGridSpec(
            num_scalar_prefetch=1,
            grid=(N_CHIPS, gb.shape[1]),
            in_specs=[
                pl.BlockSpec((None, None, HALF, D_MODEL), lambda j, p, c: (j, p, c[0], 0)),
                pl.BlockSpec((None, None, HALF, D_MODEL), lambda j, p, c: (j, p, 0, 0)),
            ],
            out_specs=pl.BlockSpec((None, None, HALF, D_MODEL), lambda j, p, c: (j, p, 0, 0)),
        ),
        out_shape=_sds(land.shape, BF16),
        compiler_params=_cparams(("parallel", "parallel")),
    )(core, gb, land)


def _chip_sum_share(name, land):
    n = land.shape[1]

    def body(l_ref, r_ref, buf, send_sems, recv_sems, local_sems):
        p = pl.program_id(0)
        x, y, c = _place()
        acc = l_ref[0].astype(F32)
        for j in range(1, N_CHIPS):
            acc = acc + l_ref[j].astype(F32)
        buf[p] = acc

        def copies(q):
            mine = r_ref.at[q, pl.ds(c * HALF, HALF), :]
            theirs = r_ref.at[q, pl.ds((1 - c) * HALF, HALF), :]
            local = pltpu.make_async_copy(buf.at[q], mine, local_sems.at[q])
            out = pltpu.make_async_remote_copy(
                src_ref=buf.at[q], dst_ref=mine, send_sem=send_sems.at[q], recv_sem=recv_sems.at[q],
                device_id=(x, y, 1 - c), device_id_type=_MESH,
            )
            arrive = pltpu.make_async_remote_copy(
                src_ref=buf.at[q], dst_ref=theirs, send_sem=send_sems.at[q], recv_sem=recv_sems.at[q],
                device_id=(x, y, 1 - c), device_id_type=_MESH,
            )
            return local, out, arrive

        local, out, _ = copies(p)
        local.start()
        out.start()

        @pl.when(p == n - 1)
        def _():
            for q in range(n):
                local, out, arrive = copies(q)
                arrive.wait_recv()
                out.wait_send()
                local.wait()

    return pl.pallas_call(
        body,
        name=name,
        grid=(n,),
        in_specs=[pl.BlockSpec((N_CHIPS, None, HALF, D_MODEL), lambda p: (0, p, 0, 0))],
        out_specs=pl.BlockSpec(memory_space=pl.ANY),
        out_shape=_sds((n, F_SHARD, D_MODEL), F32),
        scratch_shapes=[
            pltpu.VMEM((n, HALF, D_MODEL), F32),
            pltpu.SemaphoreType.DMA((n,)), pltpu.SemaphoreType.DMA((n,)), pltpu.SemaphoreType.DMA((n,)),
        ],
        compiler_params=_cparams(("arbitrary",)),
    )(land)


class _ReduceBehind:
    def __init__(self, tag, gb, after=()):
        self.tag = tag
        n = gb.shape[1]
        land = lax.empty((N_CHIPS, n, HALF, D_MODEL), gb.dtype)

        def body(gb_ref, land_ref, *rest):
            send_sem, recv_sem, _, _, token = rest[len(after):]
            self._pair_copy(gb_ref, land_ref, send_sem, recv_sem).start()
            token[...] = jnp.zeros_like(token)

        self.send, self.recv, self.gb, self.land, self.token = pl.pallas_call(
            body,
            name=f"grads_pair_start_{tag}",
            out_shape=(
                pltpu.SemaphoreType.DMA((1,)), pltpu.SemaphoreType.DMA((1,)),
                pltpu.HBM(gb.shape, gb.dtype), pltpu.HBM(land.shape, land.dtype), _TOKEN,
            ),
            in_specs=(_HBM_SPEC, _HBM_SPEC) + (_ANY_SPEC,) * len(after),
            out_specs=(_SEM_SPEC, _SEM_SPEC, _HBM_SPEC, _HBM_SPEC, _VMEM_SPEC),
            input_output_aliases={0: 2, 1: 3},
            compiler_params=_SPLIT_PARAMS,
        )(_in_hbm(gb), _in_hbm(land), *after)

    @staticmethod
    def _pair_copy(gb_ref, land_ref, send_sem, recv_sem):
        x, y, c = _place()
        return pltpu.make_async_remote_copy(
            src_ref=gb_ref.at[:, :, pl.ds((1 - c) * HALF, HALF), :], dst_ref=land_ref,
            send_sem=send_sem.at[0], recv_sem=recv_sem.at[0], device_id=(x, y, 1 - c), device_id_type=_MESH,
        )

    @staticmethod
    def _chip_copies(p_ref, land_ref, send_sems, recv_sems):
        x, y, c = _place()
        me = 2 * x + y
        sends, arrivals = [], []
        for k, chip in enumerate(_other_chips(x, y)):
            them = 2 * chip[0] + chip[1]
            sends.append(pltpu.make_async_remote_copy(
                src_ref=p_ref.at[them], dst_ref=land_ref.at[me], send_sem=send_sems.at[k], recv_sem=recv_sems.at[k],
                device_id=(*chip, c), device_id_type=_MESH,
            ))
            arrivals.append(pltpu.make_async_remote_copy(
                src_ref=p_ref.at[me], dst_ref=land_ref.at[them], send_sem=send_sems.at[k], recv_sem=recv_sems.at[k],
                device_id=(*chip, c), device_id_type=_MESH,
            ))
        return sends, arrivals

    def mid(self, after):
        tag = self.tag

        def wait_body(gb_ref, land_ref, send_sem, recv_sem, after_ref, gb_out, land_out):
            cp = self._pair_copy(gb_ref, land_ref, send_sem, recv_sem)
            cp.wait_send()
            cp.wait_recv()

        gb, land = pl.pallas_call(
            wait_body,
            name=f"grads_pair_wait_{tag}",
            out_shape=(pltpu.HBM(self.gb.shape, self.gb.dtype), pltpu.HBM(self.land.shape, self.land.dtype)),
            in_specs=(_HBM_SPEC, _HBM_SPEC, _SEM_SPEC, _SEM_SPEC, _ANY_SPEC),
            out_specs=(_HBM_SPEC, _HBM_SPEC),
            input_output_aliases={0: 0, 1: 1},
            compiler_params=_SPLIT_PARAMS,
        )(self.gb, self.land, self.send, self.recv, after)
        part = _pair_sum(f"grads_pair_sum_{tag}", gb, land)

        x, y, _ = _place()
        start = (2 * x + y, 0, 0, 0)
        own = lax.dynamic_slice(part, start, (1,) + part.shape[1:])
        land2 = lax.dynamic_update_slice(lax.empty(part.shape, part.dtype), own, start)

        def start_body(p_ref, land_ref, send_sems, recv_sems, p_out, land_out, token):
            for cp in self._chip_copies(p_ref, land_ref, send_sems, recv_sems)[0]:
                cp.start()
            token[...] = jnp.zeros_like(token)

        self.send2, self.recv2, self.part, self.land2, token = pl.pallas_call(
            start_body,
            name=f"grads_chip_start_{tag}",
            out_shape=(
                pltpu.SemaphoreType.DMA((3,)), pltpu.SemaphoreType.DMA((3,)),
                pltpu.HBM(part.shape, part.dtype), pltpu.HBM(land2.shape, land2.dtype), _TOKEN,
            ),
            in_specs=(_HBM_SPEC, _HBM_SPEC),
            out_specs=(_SEM_SPEC, _SEM_SPEC, _HBM_SPEC, _HBM_SPEC, _VMEM_SPEC),
            input_output_aliases={0: 2, 1: 3},
            compiler_params=_SPLIT_PARAMS,
        )(_in_hbm(part), _in_hbm(land2))
        return [token]

    def get(self, after):
        n_after = len(after)

        def wait_body(p_ref, land_ref, send_sems, recv_sems, *rest):
            sends, arrivals = self._chip_copies(p_ref, land_ref, send_sems, recv_sems)
            for cp in arrivals:
                cp.wait_recv()
            for cp in sends:
                cp.wait_send()

        _, land2 = pl.pallas_call(
            wait_body,
            name=f"grads_chip_wait_{self.tag}",
            out_shape=(pltpu.HBM(self.part.shape, self.part.dtype), pltpu.HBM(self.land2.shape, self.land2.dtype)),
            in_specs=(_HBM_SPEC, _HBM_SPEC, _SEM_SPEC, _SEM_SPEC) + (_ANY_SPEC,) * n_after,
            out_specs=(_HBM_SPEC, _HBM_SPEC),
            input_output_aliases={0: 0, 1: 1},
            compiler_params=_SPLIT_PARAMS,
        )(self.part, self.land2, self.send2, self.recv2, *after)
        return _chip_sum_share(f"grads_chip_sum_share_{self.tag}", land2)


def _allreduce_small(blk):
    gathered = _allgather_small("allgather_small_grads", blk).reshape(N_DEV, PK_ROWS, 128)

    def body(g_ref, o_ref):
        acc = g_ref[0]
        for k in range(1, N_DEV):
            acc = acc + g_ref[k]
        o_ref[...] = acc

    total = pl.pallas_call(body, name="small_grads_sum", out_shape=_sds((PK_ROWS, 128), F32))(gathered)
    return gathered, total


def _adamw_math(w_, g_, m_, v_):
    m_new = ADAM_B1 * m_ + (1.0 - ADAM_B1) * g_
    v_new = ADAM_B2 * v_ + (1.0 - ADAM_B2) * (g_ * g_)
    m_hat = m_new / (1.0 - ADAM_B1 ** ADAM_STEP)
    v_hat = v_new / (1.0 - ADAM_B2 ** ADAM_STEP)
    delta = -ADAM_LR * (m_hat / (jnp.sqrt(v_hat) + ADAM_EPS) + ADAM_WD * w_)
    return delta, m_new, v_new


def _adamw(name, w, g, m, v):
    rows, cols = w.shape
    tm = rows
    while tm * cols * 4 > (1 << 21) and tm % 16 == 0:
        tm //= 2
    out = _sds(w.shape, F32)
    return _rowwise(name, _adamw_math, [w, g, m, v], [], [out, out, out], [], tm)


def _adamw_small(ws, gs, ms, vs):
    n = len(ws)
    shapes = [w.shape for w in ws]
    as2d = lambda a: a.reshape(1, -1) if a.ndim == 1 else a

    def body(*refs):
        ins, outs = refs[:4 * n], refs[4 * n:]
        for k in range(n):
            res = _adamw_math(*[ins[j * n + k][...] for j in range(4)])
            for j in range(3):
                outs[j * n + k][...] = res[j]

    args = [as2d(a) for group in (ws, gs, ms, vs) for a in group]
    out = pl.pallas_call(
        body, name="adamw_small", out_shape=[_sds(as2d(w).shape, F32) for w in ws] * 3, compiler_params=_cparams()
    )(*args)
    back = [o.reshape(shapes[i % n]) for i, o in enumerate(out)]
    return back[:n], back[n:2 * n], back[2 * n:]


def _pack_small(b_mod_like, n1, n2, n3, nf, qn, kvn, sinks, rel):
    parts = [
        b_mod_like.reshape(PK_MOD, 128),
        n1.reshape(8, 128), n2.reshape(8, 128), n3.reshape(8, 128), nf.reshape(8, 128),
        qn.reshape(2, 128), kvn.reshape(1, 128),
        jnp.pad(sinks.reshape(1, SWA_HEADS), ((0, 0), (0, 128 - SWA_HEADS))),
        rel.reshape(2, 128),
        jnp.zeros((2, 128), F32),
    ]
    return jnp.concatenate(parts, axis=0)


def _unpack_small(p):
    o = PK_MOD
    return (
        p[:o].reshape(1, N_MOD * D_MODEL),
        p[o:o + 8].reshape(1, D_MODEL), p[o + 8:o + 16].reshape(1, D_MODEL),
        p[o + 16:o + 24].reshape(1, D_MODEL), p[o + 24:o + 32].reshape(D_MODEL),
        p[o + 32:o + 34].reshape(1, MLA_Q_RANK), p[o + 34:o + 35].reshape(1, MLA_KV_RANK),
        p[o + 35:o + 36, :SWA_HEADS].reshape(1, SWA_HEADS),
        p[o + 36:o + 38].reshape(NUM_BUCKETS, SWA_HEADS),
    )


def kernel(x, c, w_mod, b_mod, norm_ffn1, ffn1_gate, ffn1_up, ffn1_down, norm_mix, w_in, q_norm, kv_norm, w_uq, w_ukv, sinks, w_o, norm_ffn2, ffn2_gate, ffn2_up, ffn2_down, rel_bias, norm_final, loss_target, m_w_mod, m_b_mod, m_norm_ffn1, m_ffn1_gate, m_ffn1_up, m_ffn1_down, m_norm_mix, m_w_in, m_q_norm, m_kv_norm, m_w_uq, m_w_ukv, m_sinks, m_w_o, m_norm_ffn2, m_ffn2_gate, m_ffn2_up, m_ffn2_down, m_rel_bias, m_norm_final, v_w_mod, v_b_mod, v_norm_ffn1, v_ffn1_gate, v_ffn1_up, v_ffn1_down, v_norm_mix, v_w_in, v_q_norm, v_kv_norm, v_w_uq, v_w_ukv, v_sinks, v_w_o, v_norm_ffn2, v_ffn2_gate, v_ffn2_up, v_ffn2_down, v_rel_bias, v_norm_final):
    ax, ay, ac = _place()
    chip = 2 * ax + ay
    dev = 2 * chip + ac
    n_mod_shard = N_MOD * D_MODEL // N_CHIPS

    def t16(w):
        return w[0].T.astype(BF16)

    rest = jnp.concatenate(
        [
            t16(w_in),
            w_o[0].astype(BF16),
            t16(w_uq).reshape(R_UQ, D_MODEL),
            t16(w_ukv).reshape(R_UKV, D_MODEL),
            jnp.zeros((F_SHARD - O_PAD, D_MODEL), BF16),
        ],
        axis=0,
    )
    own_gu1 = jnp.stack([t16(ffn1_gate), t16(ffn1_up)])
    own_down1 = ffn1_down[0].astype(BF16)[None]
    own_mix = rest[None]
    own_ffn2 = jnp.stack([t16(ffn2_gate), t16(ffn2_up), ffn2_down[0].astype(BF16)])

    (c_act,) = _rowwise(
        "silu_c", lambda v: v * _sigmoid(v), [c.reshape(8, 128)], [], [_sds((8, 128), F32)], [], 8
    )
    c_all = _allgather_small("allgather_c", c_act).reshape(N_DEV, D_MODEL)
    mod_cols = _mm(
        "mod_fwd", "nn", (1, n_mod_shard // MOD_TILE, 1),
        c_all, (N_DEV, D_MODEL), lambda i, j, k: (0, 0),
        w_mod[0], (D_MODEL, MOD_TILE), lambda i, j, k: (0, j),
        _sds((N_DEV, n_mod_shard), F32), (N_DEV, MOD_TILE), lambda i, j, k: (0, j),
        (N_DEV, MOD_TILE),
    )
    mod_all = _allgather_small("allgather_mod", mod_cols).reshape(N_CHIPS, 2, N_DEV, n_mod_shard)[:, 0]
    mod_all = mod_all.transpose(1, 0, 2).reshape(N_DEV, N_MOD * D_MODEL)
    mod = lax.dynamic_slice_in_dim(mod_all, dev, 1, axis=0) + b_mod

    norms = (norm_ffn1, norm_mix, norm_ffn2, norm_final.reshape(1, D_MODEL))

    ffn2_src = _GatherBehind("gather_ffn2", own_ffn2)
    mix_src = _GatherBehind("gather_mix", own_mix, then=ffn2_src)
    down1_src = _GatherBehind("gather_down1", own_down1, then=mix_src)
    gu1_src = _GatherBehind("gather_gu1", own_gu1, then=down1_src)
    gu1_src.start([mod_all])

    reducing, red = {}, {}

    def begin(tag, gb, after):
        reducing[tag] = _ReduceBehind(tag, gb, after=after)
        return [reducing[tag].token]

    def ffn2_gu_done(gb):
        return begin("ffn2_gu", gb, reducing["ffn2_down"].mid(gb))

    def mix_done(dx1, gb_rest):
        red["ffn2_down"] = reducing["ffn2_down"].get([dx1])
        red["ffn2_gu"] = reducing["ffn2_gu"].get([red["ffn2_down"]])
        return begin("rest", gb_rest, [red["ffn2_gu"]])

    def ffn1_gu_done(gb):
        red["rest"] = reducing["rest"].get([gb])
        begin("ffn1_gu", gb, reducing["ffn1_down"].mid(red["rest"]))
        return reducing["ffn1_gu"].mid(reducing["ffn1_gu"].token)

    loss_part, grad_x, _, dmod, small = _device_step(
        x[0], loss_target[0], mod, gu1_src, down1_src, mix_src, ffn2_src, norms, q_norm, kv_norm, sinks, rel_bias,
        hooks2=_BwdHooks(after_wdown=lambda gb: begin("ffn2_down", gb, ()), after_wgu=ffn2_gu_done),
        hooks_mix=_BwdHooks(after_gate=lambda dz: reducing["ffn2_gu"].mid(dz)),
        mix_done=mix_done,
        hooks1=_BwdHooks(
            after_swiglu=lambda dab3: reducing["rest"].mid(dab3),
            after_wdown=lambda gb: begin("ffn1_down", gb, ()),
            after_wgu=ffn1_gu_done,
        ),
    )
    loss = lax.psum(loss_part, ("x", "y", "c"))

    gathered, total = _allreduce_small(_pack_small(dmod, *small))
    (g_b_mod, g_n1, g_n2, g_n3, g_nf, g_qn, g_kvn, g_sinks, g_rel) = _unpack_small(total)
    dmod_all = gathered[:, :PK_MOD].reshape(N_DEV, N_MOD * D_MODEL)
    dmod_cols = lax.dynamic_slice_in_dim(dmod_all, chip * n_mod_shard, n_mod_shard, axis=1)
    g_w_mod = _mm(
        "mod_bwd", "tn", (1, n_mod_shard // MOD_TILE, 1),
        c_all, (N_DEV, D_MODEL), lambda i, j, k: (0, 0),
        dmod_cols, (N_DEV, MOD_TILE), lambda i, j, k: (0, j),
        _sds((D_MODEL, n_mod_shard), F32), (D_MODEL, MOD_TILE), lambda i, j, k: (0, j),
        (D_MODEL, MOD_TILE),
    )

    r_rest = red["rest"][0]
    transposed = {"ffn1_gate", "ffn1_up", "ffn2_gate", "ffn2_up", "w_in", "w_uq", "w_ukv"}
    g_big = {
        "ffn2_gate": red["ffn2_gu"][0], "ffn2_up": red["ffn2_gu"][1], "ffn2_down": red["ffn2_down"][0],
        "w_in": r_rest[O_IN:O_IN + R_IN],
        "w_o": r_rest[O_O:O_O + R_O],
        "w_uq": r_rest[O_UQ:O_UQ + R_UQ].reshape(MLA_QK, MLA_Q_RANK),
        "w_ukv": r_rest[O_UKV:O_UKV + R_UKV].reshape(MLA_NOPE + MLA_V, MLA_KV_RANK),
        "w_mod": g_w_mod,
    }
    w_big = {
        "ffn2_gate": (ffn2_gate, m_ffn2_gate, v_ffn2_gate),
        "ffn2_up": (ffn2_up, m_ffn2_up, v_ffn2_up), "ffn2_down": (ffn2_down, m_ffn2_down, v_ffn2_down),
        "w_in": (w_in, m_w_in, v_w_in), "w_o": (w_o, m_w_o, v_w_o), "w_uq": (w_uq, m_w_uq, v_w_uq),
        "w_ukv": (w_ukv, m_w_ukv, v_w_ukv), "w_mod": (w_mod, m_w_mod, v_w_mod),
    }
    grads, deltas, new_m, new_v = {}, {}, {}, {}

    def update(names):
        for nm in names:
            w, m, v = w_big[nm]
            if nm in transposed:
                outs = _adamw(f"adamw_{nm}", w[0].T, g_big[nm], m[0].T, v[0].T)
                g_, d_, m_, v_ = [a.T for a in (g_big[nm],) + tuple(outs)]
            else:
                g_, (d_, m_, v_) = g_big[nm], _adamw(f"adamw_{nm}", w[0], g_big[nm], m[0], v[0])
            grads[nm], deltas[nm], new_m[nm], new_v[nm] = g_[None], d_[None], m_[None], v_[None]

    update(list(w_big))
    red1_down = reducing["ffn1_down"].get([deltas[nm] for nm in w_big])
    red1_gu = reducing["ffn1_gu"].get([red1_down])
    g_big.update({"ffn1_gate": red1_gu[0], "ffn1_up": red1_gu[1], "ffn1_down": red1_down[0]})
    w_big.update({
        "ffn1_gate": (ffn1_gate, m_ffn1_gate, v_ffn1_gate), "ffn1_up": (ffn1_up, m_ffn1_up, v_ffn1_up),
        "ffn1_down": (ffn1_down, m_ffn1_down, v_ffn1_down),
    })
    update(["ffn1_gate", "ffn1_up", "ffn1_down"])

    small_names = ["b_mod", "norm_ffn1", "norm_mix", "norm_ffn2", "norm_final", "q_norm", "kv_norm", "sinks", "rel_bias"]
    w_small = (b_mod, norm_ffn1, norm_mix, norm_ffn2, norm_final, q_norm, kv_norm, sinks, rel_bias)
    m_small = (m_b_mod, m_norm_ffn1, m_norm_mix, m_norm_ffn2, m_norm_final, m_q_norm, m_kv_norm, m_sinks, m_rel_bias)
    v_small = (v_b_mod, v_norm_ffn1, v_norm_mix, v_norm_ffn2, v_norm_final, v_q_norm, v_kv_norm, v_sinks, v_rel_bias)
    g_small = (g_b_mod, g_n1, g_n2, g_n3, g_nf, g_qn, g_kvn, g_sinks, g_rel)
    d_s, m_s, v_s = _adamw_small(w_small, g_small, m_small, v_small)
    for nm, g_, d_, m_, v_ in zip(small_names, g_small, d_s, m_s, v_s):
        grads[nm], deltas[nm], new_m[nm], new_v[nm] = g_, d_, m_, v_

    order = ["w_mod", "b_mod", "norm_ffn1", "ffn1_gate", "ffn1_up", "ffn1_down", "norm_mix", "w_in", "q_norm", "kv_norm",
             "w_uq", "w_ukv", "sinks", "w_o", "norm_ffn2", "ffn2_gate", "ffn2_up", "ffn2_down", "rel_bias", "norm_final"]
    return (loss, grad_x[None], *[grads[n] for n in order], *[deltas[n] for n in order],
            *[new_m[n] for n in order], *[new_v[n] for n in order])
```

```python
import functools
import math

import jax
import jax.numpy as jnp
import numpy as np
from jax import lax
from jax.experimental import pallas as pl
from jax.experimental.pallas import tpu as pltpu

F32, BF16 = jnp.float32, jnp.bfloat16

D_MODEL = 1024
D_FF = 2816
EPS = 1e-6
N_MOD = 9
SWA_HEADS, SWA_KV_HEADS, SWA_HEAD_DIM, WINDOW = 8, 2, 64, 128
SWA_GROUP = SWA_HEADS // SWA_KV_HEADS
MLA_HEADS, MLA_Q_RANK, MLA_KV_RANK, MLA_NOPE, MLA_ROPE, MLA_V = 4, 256, 128, 128, 64, 128
MLA_QK = MLA_NOPE + MLA_ROPE
ROPE_THETA = 10000.0
NUM_BUCKETS, MAX_DISTANCE = 32, 128
D_IN = 1216
N_SWA_COLS = 768
N_MLA_COLS = 512

ADAM_LR, ADAM_B1, ADAM_B2, ADAM_EPS, ADAM_WD, ADAM_STEP = 0.001, 0.9, 0.999, 1e-08, 0.01, 10

N_CHIPS = 4
N_DEV = 8
F_SHARD = D_FF // N_CHIPS
HALF = F_SHARD // 2
R_IN, R_O, R_UQ, R_UKV = 304, 256, 48, 32
O_IN, O_O, O_UQ, O_UKV, O_PAD = 0, 304, 560, 608, 640

PK_MOD = 72
PK_ROWS = 112
VMEM_LIMIT = 56 << 20
ROW_TILE = 2048
ELEM_TILE = 1024
MOD_TILE = 768

_DN = {
    "nn": (((1,), (0,)), ((), ())),
    "nt": (((1,), (1,)), ((), ())),
    "tn": (((0,), (0,)), ((), ())),
}


def _sds(shape, dtype):
    return jax.ShapeDtypeStruct(tuple(shape), dtype)


def _cparams(sem=None):
    kw = {"vmem_limit_bytes": VMEM_LIMIT}
    if sem is not None:
        kw["dimension_semantics"] = sem
    return pltpu.CompilerParams(**kw)


def _dot(a, b, dims):
    return lax.dot_general(a.astype(BF16), b.astype(BF16), _DN[dims], preferred_element_type=F32)


def _mm(name, dims, grid, a, a_blk, a_idx, b, b_blk, b_idx, out, out_blk, out_idx, acc_shape, alias=None, deps=()):
    nk = grid[2]

    def body(*refs):
        a_ref, b_ref = refs[:2]
        o_ref, acc_ref = refs[-2:]
        part = _dot(a_ref[...], b_ref[...], dims)
        if nk == 1:
            o_ref[...] = part.astype(o_ref.dtype)
            return
        k = pl.program_id(2)

        @pl.when(k == 0)
        def _():
            acc_ref[...] = part

        @pl.when((k > 0) & (k < nk - 1))
        def _():
            acc_ref[...] += part

        @pl.when(k == nk - 1)
        def _():
            o_ref[...] = (acc_ref[...] + part).astype(o_ref.dtype)

    in_specs = [pl.BlockSpec(a_blk, a_idx), pl.BlockSpec(b_blk, b_idx)]
    args = [a, b]
    kw = {}
    if alias is not None:
        in_specs.append(pl.BlockSpec(memory_space=pl.ANY))
        args.append(alias)
        kw["input_output_aliases"] = {2: 0}
    in_specs += [pl.BlockSpec(memory_space=pl.ANY)] * len(deps)
    args += list(deps)
    return pl.pallas_call(
        body,
        name=name,
        grid=grid,
        in_specs=in_specs,
        out_specs=pl.BlockSpec(out_blk, out_idx),
        out_shape=out,
        scratch_shapes=[pltpu.VMEM(acc_shape if nk > 1 else (8, 128), F32)],
        compiler_params=_cparams(("parallel", "parallel", "arbitrary")),
        **kw,
    )(*args)


def _rowwise(name, fn, tiled, full, out_tiled, out_red, tm, deps=()):
    rows = tiled[0].shape[-2]
    tm = min(tm, rows)
    grid = (rows // tm,)
    n_in = len(tiled) + len(full)
    n_t = len(out_tiled)
    n_dep = len(deps)

    def tspec(shape):
        nd = len(shape)
        return pl.BlockSpec(tuple(shape[:-2]) + (tm, shape[-1]), lambda i, nd=nd: (0,) * (nd - 2) + (i, 0))

    def fspec(shape):
        nd = len(shape)
        return pl.BlockSpec(tuple(shape), lambda i, nd=nd: (0,) * nd)

    def body(*refs):
        outs = fn(*[r[...] for r in refs[:n_in]])
        if not isinstance(outs, (tuple, list)):
            outs = (outs,)
        out_refs = refs[n_in + n_dep:]
        for r, v in zip(out_refs[:n_t], outs[:n_t]):
            r[...] = v.astype(r.dtype)
        red_refs = out_refs[n_t:]
        if red_refs:
            @pl.when(pl.program_id(0) == 0)
            def _():
                for r in red_refs:
                    r[...] = jnp.zeros_like(r)

            for r, v in zip(red_refs, outs[n_t:]):
                r[...] += v.astype(r.dtype)

    res = pl.pallas_call(
        body,
        name=name,
        grid=grid,
        in_specs=[tspec(a.shape) for a in tiled] + [fspec(a.shape) for a in full]
        + [pl.BlockSpec(memory_space=pl.ANY)] * n_dep,
        out_specs=[tspec(o.shape) for o in out_tiled] + [fspec(o.shape) for o in out_red],
        out_shape=list(out_tiled) + list(out_red),
        compiler_params=_cparams(("arbitrary",) if out_red else ("parallel",)),
    )(*tiled, *full, *deps)
    return res


def _sum0(v):
    return jnp.sum(v, axis=0, keepdims=True)


def _sigmoid(v):
    return 1.0 / (1.0 + jnp.exp(-v))


def _rms(x):
    r = lax.rsqrt(jnp.mean(x * x, axis=-1, keepdims=True) + EPS)
    return x * r, r


def _rms_bwd(u, xr, r):
    return r * (u - xr * jnp.mean(u * xr, axis=-1, keepdims=True))


def _normmod(x_, gamma_, sc_, sh_):
    return _rms(x_)[0] * gamma_ * (1.0 + sc_) + sh_


def _row_blocks(tm, size=512):
    size = min(size, tm)
    return [slice(r, r + size) for r in range(0, tm, size)]


def _loss_head(x_, t_, g_):
    xr, r = _rms(x_)
    err = xr * g_ - t_
    dy = err * (1.0 / D_MODEL)
    return _rms_bwd(dy * g_, xr, r), _sum0(err * err), _sum0(dy * xr)


def _ffn_fwd(tag, x, gamma, sh, sc, gate, gu_src, base, down_src, down_idx, mid_after, h_pre=None,
             next_norm=None, head=None):
    s_len = x.shape[0]
    if h_pre is None:
        (h,) = _rowwise(
            f"{tag}_normmod", _normmod, [x], [gamma, sc, sh], [_sds((s_len, D_MODEL), BF16)], [], ELEM_TILE
        )
        gdeps = gu_src.mid(h)
    else:
        h, gdeps = h_pre, gu_src.mid(mid_after)
    g4 = gu_src.get(h)

    tm = min(ROW_TILE, s_len)
    def gate_up(h_ref, wg_ref, wu_ref, *rest):
        ab_ref, s_ref = rest[-2:]
        wg, wu = wg_ref[...], wu_ref[...]
        for rows in _row_blocks(tm):
            hv = h_ref[rows, :]
            a = _dot(hv, wg, "nt")
            b = _dot(hv, wu, "nt")
            ab_ref[0, rows, :] = a.astype(ab_ref.dtype)
            ab_ref[1, rows, :] = b.astype(ab_ref.dtype)
            s_ref[rows, :] = (a * _sigmoid(a) * b).astype(s_ref.dtype)

    ab4, s3 = pl.pallas_call(
        gate_up,
        name=f"{tag}_gate_up",
        grid=(s_len // tm, N_CHIPS),
        in_specs=[
            pl.BlockSpec((tm, D_MODEL), lambda i, c: (i, 0)),
            pl.BlockSpec((None, None, F_SHARD, D_MODEL), lambda i, c: (c, base, 0, 0)),
            pl.BlockSpec((None, None, F_SHARD, D_MODEL), lambda i, c: (c, base + 1, 0, 0)),
        ] + [pl.BlockSpec(memory_space=pl.ANY)] * len(gdeps),
        out_specs=[
            pl.BlockSpec((None, 2, tm, F_SHARD), lambda i, c: (c, 0, i, 0)),
            pl.BlockSpec((None, tm, F_SHARD), lambda i, c: (c, i, 0)),
        ],
        out_shape=[_sds((N_CHIPS, 2, s_len, F_SHARD), BF16), _sds((N_CHIPS, s_len, F_SHARD), BF16)],
        compiler_params=_cparams(("parallel", "parallel")),
    )(h, g4, g4, *gdeps)
    if down_src is None:
        g_down, ddeps = g4, ()
    else:
        ddeps = down_src.mid(ab4)
        g_down = down_src.get(s3)

    y = _mm(
        f"{tag}_down", "nn", (s_len // tm, 1, N_CHIPS),
        s3, (None, tm, F_SHARD), lambda i, j, k: (k, i, 0),
        g_down, (None, None, F_SHARD, D_MODEL), lambda i, j, k: (k, down_idx, 0, 0),
        _sds((s_len, D_MODEL), F32), (tm, D_MODEL), lambda i, j, k: (i, 0),
        (tm, D_MODEL), deps=ddeps,
    )

    saved = (g4, base, g_down, down_idx, h, ab4, s3, y)
    act = _sds((s_len, D_MODEL), F32)
    if head is not None:
        target, norm_final = head
        vec = _sds((1, D_MODEL), F32)
        out = _rowwise(
            f"{tag}_residual_head", lambda x_, y_, t_, g_, nf_: _loss_head(x_ + 0.5 * g_ * y_, t_, nf_),
            [x, y, target], [gate, norm_final], [act], [vec, vec], ELEM_TILE,
        )
    elif next_norm is not None:
        def residual_norm(x_, y_, g_, gamma_, sc_, sh_):
            x_out = x_ + 0.5 * g_ * y_
            return x_out, _normmod(x_out, gamma_, sc_, sh_)

        out = _rowwise(
            f"{tag}_residual_norm", residual_norm, [x, y], [gate] + list(next_norm),
            [act, _sds((s_len, D_MODEL), BF16)], [], ELEM_TILE,
        )
    else:
        out = _rowwise(f"{tag}_residual", lambda x_, y_, g_: x_ + 0.5 * g_ * y_, [x, y], [gate], [act], [], ELEM_TILE)
    return out, saved


def _normmod_bwd_call(name, dh_parts, x, dxo, gamma, sc, deps=()):
    s_len = x.shape[0]
    n_parts = len(dh_parts)

    def fn(*vals):
        dh = vals[0]
        for extra in vals[1:n_parts]:
            dh = dh + extra
        x_, dxo_, gamma_, sc_ = vals[n_parts:]
        xr, r = _rms(x_)
        n = xr * gamma_
        dn = dh * (1.0 + sc_)
        dx = dxo_ + _rms_bwd(dn * gamma_, xr, r)
        return dx, _sum0(dh * n), _sum0(dh), _sum0(dn * xr)

    vec = _sds((1, D_MODEL), F32)
    return _rowwise(
        name, fn, list(dh_parts) + [x, dxo], [gamma, sc], [_sds((s_len, D_MODEL), F32)], [vec, vec, vec], ELEM_TILE, deps=deps
    )


class _BwdHooks:
    def __init__(self, after_gate=None, after_swiglu=None, after_wdown=None, after_wgu=None, after_dh=None):
        def nothing(_):
            return []

        self.after_gate = after_gate or nothing
        self.after_swiglu = after_swiglu or nothing
        self.after_wdown = after_wdown or nothing
        self.after_wgu = after_wgu or nothing
        self.after_dh = after_dh or nothing


def _ffn_bwd(tag, dxo, x, gamma, sc, gate, saved, hooks, deps=()):
    g4, base, g_down, down_idx, h, ab4, s3, y = saved
    s_len = x.shape[0]
    vec = _sds((1, D_MODEL), F32)

    dy, dgate = _rowwise(
        f"{tag}_bwd_gate", lambda dxo_, y_, g_: (0.5 * g_ * dxo_, _sum0(0.5 * y_ * dxo_)),
        [dxo, y], [gate], [_sds((s_len, D_MODEL), BF16)], [vec], ELEM_TILE, deps=deps,
    )

    tm = min(ROW_TILE, s_len)

    def d_gate_up(dy_ref, wd_ref, ab_ref, o_ref):
        wd = wd_ref[...]
        for rows in _row_blocks(tm):
            ds = _dot(dy_ref[rows, :], wd, "nt")
            a = ab_ref[0, rows, :].astype(F32)
            b = ab_ref[1, rows, :].astype(F32)
            sig = _sigmoid(a)
            o_ref[0, rows, :] = (ds * b * sig * (1.0 + a * (1.0 - sig))).astype(o_ref.dtype)
            o_ref[1, rows, :] = (ds * a * sig).astype(o_ref.dtype)

    ab_spec = pl.BlockSpec((None, 2, tm, F_SHARD), lambda i, c: (c, 0, i, 0))
    dab4 = pl.pallas_call(
        d_gate_up,
        name=f"{tag}_bwd_dgate_up",
        grid=(s_len // tm, N_CHIPS),
        in_specs=[
            pl.BlockSpec((tm, D_MODEL), lambda i, c: (i, 0)),
            pl.BlockSpec((None, None, F_SHARD, D_MODEL), lambda i, c: (c, down_idx, 0, 0)),
            ab_spec,
        ],
        out_specs=ab_spec,
        out_shape=_sds(ab4.shape, BF16),
        compiler_params=_cparams(("parallel", "parallel")),
    )(dy, g_down, ab4)

    tk = tm
    gb_down = _mm(
        f"{tag}_bwd_wdown", "tn", (N_CHIPS, 1, s_len // tk),
        s3, (None, tk, F_SHARD), lambda i, j, k: (i, k, 0),
        dy, (tk, D_MODEL), lambda i, j, k: (k, 0),
        _sds((N_CHIPS, 1, F_SHARD, D_MODEL), BF16), (None, None, F_SHARD, D_MODEL), lambda i, j, k: (i, 0, 0, 0),
        (F_SHARD, D_MODEL), deps=hooks.after_swiglu(dab4),
    )
    gb_gu = _mm(
        f"{tag}_bwd_wgu", "tn", (2 * N_CHIPS, 1, s_len // tk),
        dab4, (None, None, tk, F_SHARD), lambda i, j, k: (i % N_CHIPS, i // N_CHIPS, k, 0),
        h, (tk, D_MODEL), lambda i, j, k: (k, 0),
        _sds((N_CHIPS, 2, F_SHARD, D_MODEL), BF16), (None, None, F_SHARD, D_MODEL),
        lambda i, j, k: (i % N_CHIPS, i // N_CHIPS, 0, 0),
        (F_SHARD, D_MODEL), deps=hooks.after_wdown(gb_down),
    )
    assert base % 2 == 0
    dh_deps = hooks.after_wgu(gb_gu)

    def d_h(dab_ref, w_ref, *rest):
        o_ref, acc_ref = rest[-2:]
        c = pl.program_id(1)
        part = _dot(dab_ref[0], w_ref[0], "nn") + _dot(dab_ref[1], w_ref[1], "nn")

        @pl.when(c == 0)
        def _():
            acc_ref[...] = part

        @pl.when((c > 0) & (c < N_CHIPS - 1))
        def _():
            acc_ref[...] += part

        @pl.when(c == N_CHIPS - 1)
        def _():
            o_ref[...] = acc_ref[...] + part

    dh = pl.pallas_call(
        d_h,
        name=f"{tag}_bwd_dh",
        grid=(s_len // tm, N_CHIPS),
        in_specs=[
            pl.BlockSpec((None, 2, tm, F_SHARD), lambda i, c: (c, 0, i, 0)),
            pl.BlockSpec((None, 2, F_SHARD, D_MODEL), lambda i, c: (c, base // 2, 0, 0)),
        ] + [pl.BlockSpec(memory_space=pl.ANY)] * len(dh_deps),
        out_specs=pl.BlockSpec((tm, D_MODEL), lambda i, c: (i, 0)),
        out_shape=_sds((s_len, D_MODEL), F32),
        scratch_shapes=[pltpu.VMEM((tm, D_MODEL), F32)],
        compiler_params=_cparams(("parallel", "arbitrary")),
    )(dab4, g4, *dh_deps)
    dx, dsc, dsh, dgamma = _normmod_bwd_call(
        f"{tag}_bwd_normmod", [dh], x, dxo, gamma, sc, deps=hooks.after_dh(dh)
    )
    return dx, (gb_down, gb_gu), (dsh, dsc, dgate, dgamma)


def _bucket_map():
    qi = np.arange(WINDOW)[:, None]
    kj = np.arange(2 * WINDOW)[None, :]
    dist = qi + WINDOW - kj
    band = (dist >= 0) & (dist < WINDOW)
    max_exact = NUM_BUCKETS // 2
    n = np.maximum(dist, 0)
    large = []
    for dt in (np.float32, np.float64):
        nf = np.maximum(n, 1).astype(dt)
        val = np.log(nf / dt(max_exact)) / dt(math.log(MAX_DISTANCE / max_exact)) * dt(NUM_BUCKETS - max_exact)
        large.append(np.minimum(max_exact + val.astype(np.int32), NUM_BUCKETS - 1))
    assert np.array_equal(large[0], large[1])
    bucket = np.where(n < max_exact, n, large[0])
    return np.where(band, bucket, -1).astype(np.int32).reshape(1, -1)


def _bias_expand(rel_bias_t, bkt):
    n = bkt.shape[1]

    def body(rb_ref, bkt_ref, o_ref):
        onehot = (lax.broadcasted_iota(jnp.int32, (NUM_BUCKETS, n), 0) == bkt_ref[...]).astype(F32)
        o_ref[...] = lax.dot_general(
            rb_ref[...], onehot, _DN["nn"], precision=lax.Precision.HIGHEST, preferred_element_type=F32
        )

    return pl.pallas_call(
        body, name="bias_expand", out_shape=_sds((SWA_HEADS, n), F32), compiler_params=_cparams()
    )(rel_bias_t, bkt)


def _bias_reduce(dbias_flat, bkt):
    n = bkt.shape[1]

    def body(db_ref, bkt_ref, o_ref):
        onehot = (lax.broadcasted_iota(jnp.int32, (NUM_BUCKETS, n), 0) == bkt_ref[...]).astype(F32)
        o_ref[...] = lax.dot_general(
            db_ref[...], onehot, _DN["nt"], precision=lax.Precision.HIGHEST, preferred_element_type=F32
        )

    return pl.pallas_call(
        body, name="bias_reduce", out_shape=_sds((SWA_HEADS, NUM_BUCKETS), F32), compiler_params=_cparams()
    )(dbias_flat, bkt)


_SWA_SCALE = SWA_HEAD_DIM ** -0.5
_NEG = -1e30


_SWA_PAIR = 2


def _swa_in_specs():
    w = WINDOW
    n_q = SWA_HEADS * SWA_HEAD_DIM
    n_kv = 2 * SWA_KV_HEADS * SWA_HEAD_DIM
    prev = lambda m: jnp.maximum(_SWA_PAIR * m - 1, 0)
    return [
        pl.BlockSpec((_SWA_PAIR * w, n_q), lambda m: (m, 0)),
        pl.BlockSpec((w, n_kv), lambda m: (prev(m), n_q // n_kv)),
        pl.BlockSpec((_SWA_PAIR * w, n_kv), lambda m: (m, n_q // n_kv)),
        pl.BlockSpec((SWA_HEADS, w, 2 * w), lambda m: (0, 0, 0)),
        pl.BlockSpec((SWA_HEADS, w, 1), lambda m: (0, 0, 0)),
    ]


def _head_cols(v, first, count):
    hd = SWA_HEAD_DIM
    return jnp.stack([v[:, (first + i) * hd:(first + i + 1) * hd] for i in range(count)])


def _swa_blocks(q_ref, kvp_ref, kvc_ref, m):
    g, w, hd = SWA_GROUP, WINDOW, SWA_HEAD_DIM
    kv_all = jnp.concatenate([kvp_ref[...], kvc_ref[...]], axis=0).astype(F32)
    blocks = []
    for sub in range(_SWA_PAIR):
        rows = slice(sub * w, (sub + 1) * w)
        q = q_ref[rows, :].astype(F32)
        kv = kv_all[sub * w:(sub + 2) * w]
        heads = []
        for j in range(SWA_KV_HEADS):
            qj = _head_cols(q, g * j, g).reshape(g * w, hd)
            heads.append((qj, _head_cols(kv, j, 1)[0], _head_cols(kv, SWA_KV_HEADS + j, 1)[0]))
        blocks.append((_SWA_PAIR * m + sub, rows, heads))
    return blocks


def _swa_scores(q, kk, bias, n):
    g, w = SWA_GROUP, WINDOW
    s = (_dot(q, kk, "nt") * _SWA_SCALE).reshape(g, w, 2 * w) + bias
    qi = lax.broadcasted_iota(jnp.int32, (w, 2 * w), 0)
    kj = lax.broadcasted_iota(jnp.int32, (w, 2 * w), 1)
    dist = qi + w - kj
    valid = ((dist >= 0) & (dist < w) & ((n > 0) | (kj >= w)))[None]
    return jnp.where(valid, s, _NEG), valid


def _swa_fwd(swa2, bias, sinkb):
    s_len = swa2.shape[0]
    g, w, hd = SWA_GROUP, WINDOW, SWA_HEAD_DIM

    def body(q_ref, kvp_ref, kvc_ref, bias_ref, sink_ref, o_ref, lse_ref):
        for n, rows, heads in _swa_blocks(q_ref, kvp_ref, kvc_ref, pl.program_id(0)):
            outs = []
            for j, (q, kk, vv) in enumerate(heads):
                hs = slice(g * j, g * (j + 1))
                s, valid = _swa_scores(q, kk, bias_ref[hs], n)
                sink = sink_ref[hs]
                m = jnp.maximum(jnp.max(s, axis=-1, keepdims=True), sink)
                p = jnp.where(valid, jnp.exp(s - m), 0.0)
                l = jnp.sum(p, axis=-1, keepdims=True) + jnp.exp(sink - m)
                o = _dot(p.reshape(g * w, 2 * w), vv, "nn").reshape(g, w, hd) / l
                outs += [o[i] for i in range(g)]
                lse_ref[hs, rows, :] = m + jnp.log(l)
            o_ref[rows, :] = jnp.concatenate(outs, axis=-1).astype(o_ref.dtype)

    n_q = SWA_HEADS * hd
    return pl.pallas_call(
        body,
        name="swa_fwd",
        grid=(s_len // (_SWA_PAIR * w),),
        in_specs=_swa_in_specs(),
        out_specs=[
            pl.BlockSpec((_SWA_PAIR * w, n_q), lambda m: (m, 0)),
            pl.BlockSpec((SWA_HEADS, _SWA_PAIR * w, 1), lambda m: (0, m, 0)),
        ],
        out_shape=[_sds((s_len, n_q), BF16), _sds((SWA_HEADS, s_len, 1), F32)],
        compiler_params=_cparams(("parallel",)),
    )(swa2, swa2, swa2, bias, sinkb)


def _swa_bwd(swa2, bias, sinkb, cat, dcat, lse):
    s_len = swa2.shape[0]
    g, w, hd = SWA_GROUP, WINDOW, SWA_HEAD_DIM

    def body(q_ref, kvp_ref, kvc_ref, bias_ref, sink_ref, o_ref, do_ref, lse_ref,
             dq_ref, dkva_ref, dkvb_ref, dbias_ref, dsink_ref):
        step = pl.program_id(0)

        @pl.when(step == 0)
        def _():
            dbias_ref[...] = jnp.zeros_like(dbias_ref)
            dsink_ref[...] = jnp.zeros_like(dsink_ref)

        for n, rows, heads in _swa_blocks(q_ref, kvp_ref, kvc_ref, step):
            o_all = o_ref[rows, :].astype(F32)
            do_all = do_ref[rows, :].astype(F32)
            dqs, dks, dvs = [], [], []
            for j, (q, kk, vv) in enumerate(heads):
                hs = slice(g * j, g * (j + 1))
                s, valid = _swa_scores(q, kk, bias_ref[hs], n)
                lse_v = lse_ref[hs, rows, :]
                p = jnp.where(valid, jnp.exp(s - lse_v), 0.0)
                do = _head_cols(do_all, g * j, g)
                delta = jnp.sum(do * _head_cols(o_all, g * j, g), axis=-1, keepdims=True)
                do2 = do.reshape(g * w, hd)
                dp = _dot(do2, vv, "nt").reshape(g, w, 2 * w)
                ds = p * (dp - delta)
                dbias_ref[hs] += ds
                dsink_ref[hs] -= jnp.exp(sink_ref[hs] - lse_v) * delta
                ds2 = ds.reshape(g * w, 2 * w)
                dq = (_dot(ds2, kk, "nn") * _SWA_SCALE).reshape(g, w, hd)
                dqs += [dq[i] for i in range(g)]
                dks.append(_dot(ds2, q, "tn") * _SWA_SCALE)
                dvs.append(_dot(p.reshape(g * w, 2 * w), do2, "tn"))
            dq_ref[rows, :] = jnp.concatenate(dqs, axis=-1).astype(dq_ref.dtype)
            dkv = jnp.concatenate(dks + dvs, axis=-1)
            dkvb_ref[rows, :] = dkv[:w]
            dkva_ref[rows, :] = dkv[w:]

    n_q = SWA_HEADS * hd
    n_kv = 2 * SWA_KV_HEADS * hd
    q_spec = pl.BlockSpec((_SWA_PAIR * w, n_q), lambda m: (m, 0))
    kv_spec = pl.BlockSpec((_SWA_PAIR * w, n_kv), lambda m: (m, 0))
    return pl.pallas_call(
        body,
        name="swa_bwd",
        grid=(s_len // (_SWA_PAIR * w),),
        in_specs=_swa_in_specs() + [q_spec, q_spec, pl.BlockSpec((SWA_HEADS, _SWA_PAIR * w, 1), lambda m: (0, m, 0))],
        out_specs=[
            q_spec, kv_spec, kv_spec,
            pl.BlockSpec((SWA_HEADS, w, 2 * w), lambda n: (0, 0, 0)),
            pl.BlockSpec((SWA_HEADS, w, 1), lambda n: (0, 0, 0)),
        ],
        out_shape=[
            _sds((s_len, n_q), BF16), _sds((s_len, n_kv), F32), _sds((s_len, n_kv), F32),
            _sds((SWA_HEADS, w, 2 * w), F32), _sds((SWA_HEADS, w, 1), F32),
        ],
        compiler_params=_cparams(("arbitrary",)),
    )(swa2, swa2, swa2, bias, sinkb, cat, dcat, lse)


_MLA_SCALE = MLA_QK ** -0.5
_MLA_TQ = 256
_MLA_FWD_HEADS = 4
_MLA_BWD_HEADS = 2


def _mla_mask(i, tq, n_keys):
    row = i * tq + lax.broadcasted_iota(jnp.int32, (tq, n_keys), 0)
    col = lax.broadcasted_iota(jnp.int32, (tq, n_keys), 1)
    return col <= row


def _per_query_block(i, n_blocks, fn):
    for ii in range(n_blocks):
        pl.when(i == ii)(functools.partial(fn, ii))


def _mla_fwd(q4, k4, v4):
    s_len = q4.shape[1]
    tq = min(_MLA_TQ, s_len)
    pair = _MLA_FWD_HEADS

    def body(q_ref, k_ref, v_ref, o_ref, lse_ref):
        def block(ii):
            n_keys = (ii + 1) * tq
            mask = _mla_mask(ii, tq, n_keys)
            for hh in range(pair):
                s = jnp.where(mask, _dot(q_ref[hh], k_ref[hh, :n_keys, :], "nt") * _MLA_SCALE, _NEG)
                m = jnp.max(s, axis=-1, keepdims=True)
                p = jnp.exp(s - m)
                l = jnp.sum(p, axis=-1, keepdims=True)
                o_ref[:, hh * MLA_V:(hh + 1) * MLA_V] = (_dot(p, v_ref[hh, :n_keys, :], "nn") / l).astype(o_ref.dtype)
                lse_ref[hh] = m + jnp.log(l)

        _per_query_block(pl.program_id(1), s_len // tq, block)

    return pl.pallas_call(
        body,
        name="mla_fwd",
        grid=(MLA_HEADS // pair, s_len // tq),
        in_specs=[
            pl.BlockSpec((pair, tq, MLA_QK), lambda h, i: (h, i, 0)),
            pl.BlockSpec((pair, s_len, MLA_QK), lambda h, i: (h, 0, 0)),
            pl.BlockSpec((pair, s_len, MLA_V), lambda h, i: (h, 0, 0)),
        ],
        out_specs=[
            pl.BlockSpec((tq, pair * MLA_V), lambda h, i: (i, h)),
            pl.BlockSpec((pair, tq, 1), lambda h, i: (h, i, 0)),
        ],
        out_shape=[_sds((s_len, MLA_HEADS * MLA_V), BF16), _sds((MLA_HEADS, s_len, 1), F32)],
        compiler_params=_cparams(("parallel", "parallel")),
    )(q4, k4, v4)


def _mla_bwd(q4, k4, v4, cat, dcat, lse):
    s_len = q4.shape[1]
    tq = min(_MLA_TQ, s_len)
    pair = _MLA_BWD_HEADS
    first = SWA_HEADS * SWA_HEAD_DIM // (pair * MLA_V)

    def body(q_ref, k_ref, v_ref, o_ref, do_ref, lse_ref, dq_ref, dk_ref, dv_ref):
        i = pl.program_id(1)

        @pl.when(i == 0)
        def _():
            dk_ref[...] = jnp.zeros_like(dk_ref)
            dv_ref[...] = jnp.zeros_like(dv_ref)

        def block(ii):
            n_keys = (ii + 1) * tq
            mask = _mla_mask(ii, tq, n_keys)
            for hh in range(pair):
                cols = slice(hh * MLA_V, (hh + 1) * MLA_V)
                q, k, v, do = q_ref[hh], k_ref[hh, :n_keys, :], v_ref[hh, :n_keys, :], do_ref[:, cols]
                s = jnp.where(mask, _dot(q, k, "nt") * _MLA_SCALE, _NEG)
                p = jnp.where(mask, jnp.exp(s - lse_ref[hh]), 0.0)
                delta = jnp.sum(do.astype(F32) * o_ref[:, cols].astype(F32), axis=-1, keepdims=True)
                ds = (p * (_dot(do, v, "nt") - delta) * _MLA_SCALE).astype(BF16)
                dq_ref[hh] = _dot(ds, k, "nn")
                dk_ref[hh, :n_keys, :] += _dot(ds, q, "tn")
                dv_ref[hh, :n_keys, :] += _dot(p, do, "tn")

        _per_query_block(i, s_len // tq, block)

    return pl.pallas_call(
        body,
        name="mla_bwd",
        grid=(MLA_HEADS // pair, s_len // tq),
        in_specs=[
            pl.BlockSpec((pair, tq, MLA_QK), lambda h, i: (h, i, 0)),
            pl.BlockSpec((pair, s_len, MLA_QK), lambda h, i: (h, 0, 0)),
            pl.BlockSpec((pair, s_len, MLA_V), lambda h, i: (h, 0, 0)),
            pl.BlockSpec((tq, pair * MLA_V), lambda h, i: (i, first + h)),
            pl.BlockSpec((tq, pair * MLA_V), lambda h, i: (i, first + h)),
            pl.BlockSpec((pair, tq, 1), lambda h, i: (h, i, 0)),
        ],
        out_specs=[
            pl.BlockSpec((pair, tq, MLA_QK), lambda h, i: (h, i, 0)),
            pl.BlockSpec((pair, s_len, MLA_QK), lambda h, i: (h, 0, 0)),
            pl.BlockSpec((pair, s_len, MLA_V), lambda h, i: (h, 0, 0)),
        ],
        out_shape=[
            _sds((MLA_HEADS, s_len, MLA_QK), F32),
            _sds((MLA_HEADS, s_len, MLA_QK), F32),
            _sds((MLA_HEADS, s_len, MLA_V), F32),
        ],
        compiler_params=_cparams(("parallel", "arbitrary")),
    )(q4, k4, v4, cat, dcat, lse)


def _mla_split(pm):
    q_lat = pm[:, :MLA_Q_RANK]
    kv_lat = pm[:, MLA_Q_RANK:MLA_Q_RANK + MLA_KV_RANK]
    kr = pm[:, MLA_Q_RANK + MLA_KV_RANK:MLA_Q_RANK + MLA_KV_RANK + MLA_ROPE]
    kr_sw = pm[:, MLA_Q_RANK + MLA_KV_RANK + MLA_ROPE:]
    return q_lat, kv_lat, kr, kr_sw


def _mla_proj(pm, cos2, sin2, q_norm, kv_norm, wq_ext, wkv):
    s_len = pm.shape[0]

    def fn(pm_, cos_, sin_, qg, kvg, wq, wk):
        q_lat, kv_lat, kr, kr_sw = _mla_split(pm_)
        nq = (_rms(q_lat)[0] * qg).astype(BF16)
        nkv = (_rms(kv_lat)[0] * kvg).astype(BF16)
        k_rot = kr * cos_ + kr_sw * sin_
        qs, ks, vs = [], [], []
        for h in range(MLA_HEADS):
            qe = _dot(nq, wq[h], "nt")
            q_rot = qe[:, MLA_NOPE:MLA_QK] * cos_ + qe[:, MLA_QK:] * sin_
            qs.append(jnp.concatenate([qe[:, :MLA_NOPE], q_rot], axis=-1))
            kve = _dot(nkv, wk[h], "nt")
            ks.append(jnp.concatenate([kve[:, :MLA_NOPE], k_rot], axis=-1))
            vs.append(kve[:, MLA_NOPE:])
        return jnp.stack(qs), jnp.stack(ks), jnp.stack(vs)

    return _rowwise(
        "mla_proj", fn, [pm, cos2, sin2], [q_norm, kv_norm, wq_ext, wkv],
        [_sds((MLA_HEADS, s_len, MLA_QK), BF16), _sds((MLA_HEADS, s_len, MLA_QK), BF16),
         _sds((MLA_HEADS, s_len, MLA_V), BF16)], [], ELEM_TILE,
    )


def _mla_proj_bwd(pm, cos2, sin2, dq4, dk4, dv4, q_norm, kv_norm, wq_ext, wkv):
    s_len = pm.shape[0]

    def fn(pm_, cos_, sin_, dq, dk, dv, qg, kvg, wq, wk):
        q_lat, kv_lat, _, _ = _mla_split(pm_)
        xq, rq = _rms(q_lat)
        xkv, rkv = _rms(kv_lat)
        nq = (xq * qg).astype(BF16)
        nkv = (xkv * kvg).astype(BF16)
        dnq = jnp.zeros_like(q_lat)
        dnkv = jnp.zeros_like(kv_lat)
        dkr = jnp.zeros_like(cos_)
        dwq, dwk = [], []
        for h in range(MLA_HEADS):
            dy = dq[h][:, MLA_NOPE:]
            dqe = jnp.concatenate([dq[h][:, :MLA_NOPE], dy * cos_, dy * sin_], axis=-1).astype(BF16)
            dnq = dnq + _dot(dqe, wq[h], "nn")
            dwq.append(_dot(dqe, nq, "tn"))
            dkve = jnp.concatenate([dk[h][:, :MLA_NOPE], dv[h]], axis=-1).astype(BF16)
            dnkv = dnkv + _dot(dkve, wk[h], "nn")
            dwk.append(_dot(dkve, nkv, "tn"))
            dkr = dkr + dk[h][:, MLA_NOPE:]
        dq_lat = _rms_bwd(dnq * qg, xq, rq)
        dkv_lat = _rms_bwd(dnkv * kvg, xkv, rkv)
        dpm = jnp.concatenate([dq_lat, dkv_lat, dkr * cos_, dkr * sin_], axis=-1)
        return dpm, _sum0(dnq * xq), _sum0(dnkv * xkv), jnp.stack(dwq), jnp.stack(dwk)

    return _rowwise(
        "mla_proj_bwd", fn, [pm, cos2, sin2, dq4, dk4, dv4], [q_norm, kv_norm, wq_ext, wkv],
        [_sds((s_len, N_MLA_COLS), BF16)],
        [_sds((1, MLA_Q_RANK), F32), _sds((1, MLA_KV_RANK), F32),
         _sds(wq_ext.shape, F32), _sds(wkv.shape, F32)], ELEM_TILE,
    )


def _rope_tables(s_len):
    inv = ROPE_THETA ** (-jnp.arange(0, MLA_ROPE, 2, dtype=F32) / MLA_ROPE)
    ang = jnp.arange(s_len, dtype=F32)[:, None] * inv[None, :]
    cos, sin = jnp.cos(ang), jnp.sin(ang)
    return jnp.concatenate([cos, cos], axis=-1), jnp.concatenate([-sin, sin], axis=-1)


def _mix_weights(g_mix):
    rest = g_mix[:, 0]
    w_in_t = rest[:, O_IN:O_IN + R_IN].reshape(D_IN, D_MODEL)
    w_o = rest[:, O_O:O_O + R_O].reshape(D_MODEL, D_MODEL)
    w_uq_t = rest[:, O_UQ:O_UQ + R_UQ].reshape(MLA_HEADS, MLA_QK, MLA_Q_RANK)
    w_ukv_t = rest[:, O_UKV:O_UKV + R_UKV].reshape(MLA_HEADS, MLA_NOPE + MLA_V, MLA_KV_RANK)
    half = MLA_ROPE // 2
    w_swa = w_in_t[:N_SWA_COLS]
    w_mla = jnp.concatenate([w_in_t[N_SWA_COLS:], w_in_t[D_IN - half:], w_in_t[D_IN - MLA_ROPE:D_IN - half]], axis=0)
    wq_ext = jnp.concatenate([w_uq_t, w_uq_t[:, MLA_QK - half:], w_uq_t[:, MLA_NOPE:MLA_QK - half]], axis=1)
    return w_swa, w_mla, w_o, wq_ext, w_ukv_t


def _mix_fwd(x, h, gate, mix_src, q_norm, kv_norm, bias, sinkb, cos2, sin2, next_norm):
    s_len = x.shape[0]
    tm = min(ROW_TILE, s_len)
    deps = mix_src.mid(x)
    weights = _mix_weights(mix_src.get(h))
    w_swa, w_mla, w_o, wq_ext, wkv = weights
    swa2 = _mm(
        "mix_proj_swa", "nt", (s_len // tm, 1, 1),
        h, (tm, D_MODEL), lambda i, j, k: (i, 0),
        w_swa, (N_SWA_COLS, D_MODEL), lambda i, j, k: (0, 0),
        _sds((s_len, N_SWA_COLS), BF16), (tm, N_SWA_COLS), lambda i, j, k: (i, 0),
        (tm, N_SWA_COLS), deps=deps,
    )
    pm = _mm(
        "mix_proj_mla", "nt", (s_len // tm, 1, 1),
        h, (tm, D_MODEL), lambda i, j, k: (i, 0),
        w_mla, (N_MLA_COLS, D_MODEL), lambda i, j, k: (0, 0),
        _sds((s_len, N_MLA_COLS), F32), (tm, N_MLA_COLS), lambda i, j, k: (i, 0),
        (tm, N_MLA_COLS),
    )
    q4, k4, v4 = _mla_proj(pm, cos2, sin2, q_norm, kv_norm, wq_ext, wkv)
    oa, lse_a = _swa_fwd(swa2, bias, sinkb)
    ob, lse_b = _mla_fwd(q4, k4, v4)
    cat = jnp.concatenate([oa, ob], axis=1)
    z = _mm(
        "mix_out", "nn", (s_len // tm, 1, 1),
        cat, (tm, D_MODEL), lambda i, j, k: (i, 0),
        w_o, (D_MODEL, D_MODEL), lambda i, j, k: (0, 0),
        _sds((s_len, D_MODEL), F32), (tm, D_MODEL), lambda i, j, k: (i, 0),
        (tm, D_MODEL),
    )
    def residual_norm(x_, z_, g_, gamma_, sc_, sh_):
        x_out = x_ + g_ * z_
        return x_out, _normmod(x_out, gamma_, sc_, sh_)

    out = _rowwise(
        "mix_residual_norm", residual_norm, [x, z], [gate] + list(next_norm),
        [_sds((s_len, D_MODEL), F32), _sds((s_len, D_MODEL), BF16)], [], ELEM_TILE,
    )
    return out, (weights, h, swa2, pm, q4, k4, v4, cat, lse_a, lse_b, z)


def _mix_bwd(dxo, x, gamma, sc, gate, q_norm, kv_norm, bias, sinkb, cos2, sin2, saved, hooks):
    weights, h, swa2, pm, q4, k4, v4, cat, lse_a, lse_b, z = saved
    w_swa, w_mla, w_o, wq_ext, wkv = weights
    s_len = x.shape[0]
    tm = tk = min(ROW_TILE, s_len)
    vec = _sds((1, D_MODEL), F32)
    n_proj = N_SWA_COLS + N_MLA_COLS
    half_d = D_MODEL // 2

    dz, dgate = _rowwise(
        "mix_bwd_gate", lambda dxo_, z_, g_: (g_ * dxo_, _sum0(z_ * dxo_)),
        [dxo, z], [gate], [_sds((s_len, D_MODEL), BF16)], [vec], ELEM_TILE,
    )
    dw_o = _mm(
        "mix_bwd_wo", "tn", (2, 1, s_len // tk),
        cat, (tk, half_d), lambda i, j, k: (k, i),
        dz, (tk, D_MODEL), lambda i, j, k: (k, 0),
        _sds((D_MODEL, D_MODEL), F32), (half_d, D_MODEL), lambda i, j, k: (i, 0),
        (half_d, D_MODEL),
    )
    dcat = _mm(
        "mix_bwd_dcat", "nt", (s_len // tm, 1, 1),
        dz, (tm, D_MODEL), lambda i, j, k: (i, 0),
        w_o, (D_MODEL, D_MODEL), lambda i, j, k: (0, 0),
        _sds((s_len, D_MODEL), BF16), (tm, D_MODEL), lambda i, j, k: (i, 0),
        (tm, D_MODEL), deps=hooks.after_gate(dz),
    )
    dq_a, dkva, dkvb, dbias, dsink = _swa_bwd(swa2, bias, sinkb, cat, dcat, lse_a)
    dq4, dk4, dv4 = _mla_bwd(q4, k4, v4, cat, dcat, lse_b)
    dpm, dq_norm, dkv_norm, dwq_ext, dwkv = _mla_proj_bwd(pm, cos2, sin2, dq4, dk4, dv4, q_norm, kv_norm, wq_ext, wkv)

    dkv = dkva + jnp.concatenate([dkvb[WINDOW:], jnp.zeros_like(dkvb[:WINDOW])], axis=0)
    dproj = jnp.concatenate([dq_a, dkv.astype(BF16), dpm], axis=1)
    w_proj = jnp.concatenate([w_swa, w_mla], axis=0)

    dw_proj = _mm(
        "mix_bwd_win", "tn", (n_proj // 256, 1, s_len // tk),
        dproj, (tk, 256), lambda i, j, k: (k, i),
        h, (tk, D_MODEL), lambda i, j, k: (k, 0),
        _sds((n_proj, D_MODEL), F32), (256, D_MODEL), lambda i, j, k: (i, 0),
        (256, D_MODEL),
    )
    dw_swa, dw_mla = dw_proj[:N_SWA_COLS], dw_proj[N_SWA_COLS:]
    dh = _mm(
        "mix_bwd_dh", "nn", (s_len // tm, 1, 1),
        dproj, (tm, n_proj), lambda i, j, k: (i, 0),
        w_proj, (n_proj, D_MODEL), lambda i, j, k: (0, 0),
        _sds((s_len, D_MODEL), F32), (tm, D_MODEL), lambda i, j, k: (i, 0),
        (tm, D_MODEL),
    )
    dx, dsc, dsh, dgamma = _normmod_bwd_call("mix_bwd_normmod", [dh], x, dxo, gamma, sc)

    half = MLA_ROPE // 2
    n_mla_in = D_IN - N_SWA_COLS
    dw_mla_base = dw_mla[:n_mla_in]
    dw_mla_base = dw_mla_base.at[n_mla_in - half:].add(dw_mla[n_mla_in:n_mla_in + half])
    dw_mla_base = dw_mla_base.at[n_mla_in - MLA_ROPE:n_mla_in - half].add(dw_mla[n_mla_in + half:])
    dw_in_t = jnp.concatenate([dw_swa, dw_mla_base], axis=0)
    dw_uq_t = dwq_ext[:, :MLA_QK]
    dw_uq_t = dw_uq_t.at[:, MLA_QK - half:].add(dwq_ext[:, MLA_QK:MLA_QK + half])
    dw_uq_t = dw_uq_t.at[:, MLA_NOPE:MLA_QK - half].add(dwq_ext[:, MLA_QK + half:])
    rest = jnp.concatenate(
        [
            dw_in_t.reshape(N_CHIPS, R_IN, D_MODEL),
            dw_o.reshape(N_CHIPS, R_O, D_MODEL),
            dw_uq_t.reshape(N_CHIPS, R_UQ, D_MODEL),
            dwkv.reshape(N_CHIPS, R_UKV, D_MODEL),
            jnp.zeros((N_CHIPS, F_SHARD - O_PAD, D_MODEL), F32),
        ],
        axis=1,
    ).astype(BF16)
    return dx, rest, (dsh, dsc, dgate, dgamma), dq_norm, dkv_norm, dbias, dsink


def _device_step(x, target, mod, gu1_src, down1_src, mix_src, ffn2_src, norms, q_norm, kv_norm, sinks, rel_bias,
                 hooks2=None, hooks_mix=None, mix_done=None, hooks1=None):
    s_len = x.shape[0]
    sh1, sc1, g1, sh2, sc2, g2, sh3, sc3, g3 = [mod[:, i * D_MODEL:(i + 1) * D_MODEL] for i in range(N_MOD)]
    n1, n2, n3, nf = norms
    bkt = jnp.asarray(_bucket_map())
    bias = _bias_expand(rel_bias.T, bkt).reshape(SWA_HEADS, WINDOW, 2 * WINDOW)
    sinkb = jnp.broadcast_to(sinks.reshape(SWA_HEADS, 1, 1), (SWA_HEADS, WINDOW, 1))
    cos2, sin2 = _rope_tables(s_len)

    (x1, h2), saved1 = _ffn_fwd(
        "ffn1", x, n1, sh1, sc1, g1, gu1_src, 0, down1_src, 0, sh1, next_norm=(n2, sc2, sh2)
    )
    (x2, h3), saved2 = _mix_fwd(
        x1, h2, g2, mix_src, q_norm, kv_norm, bias, sinkb, cos2, sin2, next_norm=(n3, sc3, sh3)
    )
    (dx3, sq, dnf), saved3 = _ffn_fwd(
        "ffn2", x2, n3, sh3, sc3, g3, ffn2_src, 0, None, 2, x2, h_pre=h3, head=(target, nf)
    )
    loss_part = (0.5 / D_MODEL) * jnp.sum(sq)

    dx2, gb2, (dsh3, dsc3, dg3, dn3) = _ffn_bwd("ffn2", dx3, x2, n3, sc3, g3, saved3, hooks2 or _BwdHooks())
    dx1, rest, (dsh2, dsc2, dg2, dn2), dq_norm, dkv_norm, dbias, dsink = _mix_bwd(
        dx2, x1, n2, sc2, g2, q_norm, kv_norm, bias, sinkb, cos2, sin2, saved2, hooks_mix or _BwdHooks()
    )
    rest = rest[:, None]
    deps1 = mix_done(dx1, rest) if mix_done is not None else []
    dx0, gb1, (dsh1, dsc1, dg1, dn1) = _ffn_bwd(
        "ffn1", dx1, x, n1, sc1, g1, saved1, hooks1 or _BwdHooks(), deps=deps1
    )

    dmod = jnp.concatenate([dsh1, dsc1, dg1, dsh2, dsc2, dg2, dsh3, dsc3, dg3], axis=-1)
    drel = _bias_reduce(dbias.reshape(SWA_HEADS, -1), bkt).T
    dsinks = jnp.sum(dsink, axis=(1, 2)).reshape(1, SWA_HEADS)
    small = (dn1, dn2, dn3, dnf, dq_norm, dkv_norm, dsinks, drel)
    return loss_part, dx0, (gb1, gb2, rest), dmod, small


_MESH = pl.DeviceIdType.MESH


def _place():
    return lax.axis_index("x"), lax.axis_index("y"), lax.axis_index("c")


def _other_chips(x, y):
    return [(1 - x, y), (x, 1 - y), (1 - x, 1 - y)]


def _allgather_small(name, blk):
    m_per, n = blk.shape

    def body(x_ref, out_ref, send_sems, recv_sems, local_sem):
        x, y, c = _place()
        me, sibling = (x, y, c), (x, y, 1 - c)
        chips = _other_chips(x, y)

        def rows(px, py, pc):
            return out_ref.at[pl.ds((4 * px + 2 * py + pc) * m_per, m_per), :]

        def copy(k, block, to, src=None):
            return pltpu.make_async_remote_copy(
                src_ref=rows(*block) if src is None else src, dst_ref=rows(*block),
                send_sem=send_sems.at[k], recv_sem=recv_sems.at[k], device_id=to, device_id_type=_MESH,
            )

        mine = pltpu.make_async_copy(x_ref, rows(*me), local_sem)
        mine.start()
        first = [copy(0, me, sibling, src=x_ref)]
        first += [copy(1 + j, me, (*chip, c), src=x_ref) for j, chip in enumerate(chips)]
        for cp in first:
            cp.start()
        passed = [copy(4 + j, (*chip, c), sibling) for j, chip in enumerate(chips)]
        for j, chip in enumerate(chips):
            copy(1 + j, (*chip, c), me).wait_recv()
            passed[j].start()
        copy(0, sibling, me).wait_recv()
        for j, chip in enumerate(chips):
            copy(4 + j, (*chip, 1 - c), me).wait_recv()
        for cp in first + passed:
            cp.wait_send()
        mine.wait()

    return pl.pallas_call(
        body,
        name=name,
        out_shape=_sds((N_DEV * m_per, n), blk.dtype),
        in_specs=[pl.BlockSpec(memory_space=pltpu.VMEM)],
        out_specs=pl.BlockSpec(memory_space=pltpu.VMEM),
        scratch_shapes=[pltpu.SemaphoreType.DMA((7,)), pltpu.SemaphoreType.DMA((7,)), pltpu.SemaphoreType.DMA],
    )(blk)


def _gather_sends(n, own_ref, g_ref, send_sem, recv_sem, x, y, c, chip):
    return [
        pltpu.make_async_remote_copy(
            src_ref=own_ref.at[p, pl.ds(c * HALF, HALF), :], dst_ref=g_ref.at[2 * x + y, p, pl.ds(c * HALF, HALF), :],
            send_sem=send_sem, recv_sem=recv_sem, device_id=(*chip, c), device_id_type=_MESH,
        )
        for p in range(n)
    ]


def _gather_forwards(n, g_ref, send_sem, recv_sem, chip, c, sibling):
    return [
        pltpu.make_async_remote_copy(
            src_ref=g_ref.at[2 * chip[0] + chip[1], p, pl.ds(c * HALF, HALF), :],
            dst_ref=g_ref.at[2 * chip[0] + chip[1], p, pl.ds(c * HALF, HALF), :],
            send_sem=send_sem, recv_sem=recv_sem, device_id=sibling, device_id_type=_MESH,
        )
        for p in range(n)
    ]


def _gather_all(own_ref, g_ref, send_sem, recv_sem, chip, half, c):
    return pltpu.make_async_remote_copy(
        src_ref=own_ref.at[:, pl.ds(c * HALF, HALF), :],
        dst_ref=g_ref.at[2 * chip[0] + chip[1], :, pl.ds(half * HALF, HALF), :],
        send_sem=send_sem, recv_sem=recv_sem, device_id=(*chip, c), device_id_type=_MESH,
    )


_HBM_SPEC = pl.BlockSpec(memory_space=pltpu.HBM)
_SEM_SPEC = pl.BlockSpec(memory_space=pltpu.SEMAPHORE)
_ANY_SPEC = pl.BlockSpec(memory_space=pl.ANY)
_VMEM_SPEC = pl.BlockSpec(memory_space=pltpu.VMEM)
_SPLIT_PARAMS = pltpu.CompilerParams(has_side_effects=pltpu.SideEffectType.DATAFLOW_SIDE_EFFECTING)
_TOKEN = jax.ShapeDtypeStruct((8, 128), F32)


def _in_hbm(a):
    return pltpu.with_memory_space_constraint(a, pltpu.HBM)


class _GatherBehind:
    def __init__(self, tag, own, then=None):
        self.tag, self.n, self.own, self.then = tag, own.shape[0], own, then

    def start(self, after):
        tag, own, n = self.tag, self.own, self.n
        x, y, _ = _place()
        g_init = lax.dynamic_update_slice(
            lax.empty((N_CHIPS,) + own.shape, own.dtype), own[None], (2 * x + y, 0, 0, 0)
        )

        def body(own_ref, g_ref, *rest):
            send_sems, recv_sems, _, _, token = rest[len(after):]
            x, y, c = _place()
            for j, chip in enumerate(_other_chips(x, y)):
                for cp in _gather_sends(n, own_ref, g_ref, send_sems.at[j], recv_sems.at[j], x, y, c, chip):
                    cp.start()
            token[...] = jnp.zeros_like(token)

        self.send1, self.recv1, self.own, self.g, self.token = pl.pallas_call(
            body,
            name=f"{tag}_start",
            out_shape=(
                pltpu.SemaphoreType.DMA((3,)), pltpu.SemaphoreType.DMA((3,)),
                pltpu.HBM(own.shape, own.dtype), pltpu.HBM(g_init.shape, g_init.dtype), _TOKEN,
            ),
            in_specs=(_HBM_SPEC, _HBM_SPEC) + (_ANY_SPEC,) * len(after),
            out_specs=(_SEM_SPEC, _SEM_SPEC, _HBM_SPEC, _HBM_SPEC, _VMEM_SPEC),
            input_output_aliases={0: 2, 1: 3},
            compiler_params=_SPLIT_PARAMS,
        )(_in_hbm(own), _in_hbm(g_init), *after)

    def mid(self, after):
        n = self.n

        def body(own_ref, g_ref, send1, recv1, after_ref, send2, recv2, g_out, token):
            x, y, c = _place()
            sibling = (x, y, 1 - c)
            chips = _other_chips(x, y)
            for j, chip in enumerate(chips):
                _gather_all(own_ref, g_ref, send1.at[j], recv1.at[j], chip, c, c).wait_recv()
                for cp in _gather_forwards(n, g_ref, send2.at[j], recv2.at[j], chip, c, sibling):
                    cp.start()
            for j, chip in enumerate(chips):
                _gather_all(own_ref, g_ref, send1.at[j], recv1.at[j], chip, c, c).wait_send()
            token[...] = jnp.zeros_like(token)

        self.send2, self.recv2, self.g, token = pl.pallas_call(
            body,
            name=f"{self.tag}_mid",
            out_shape=(
                pltpu.SemaphoreType.DMA((3,)), pltpu.SemaphoreType.DMA((3,)),
                pltpu.HBM(self.g.shape, self.g.dtype), _TOKEN,
            ),
            in_specs=(_HBM_SPEC, _HBM_SPEC, _SEM_SPEC, _SEM_SPEC, _ANY_SPEC),
            out_specs=(_SEM_SPEC, _SEM_SPEC, _HBM_SPEC, _VMEM_SPEC),
            input_output_aliases={1: 2},
            compiler_params=_SPLIT_PARAMS,
        )(self.own, self.g, self.send1, self.recv1, after)
        if self.then is None:
            return [token]
        self.then.start([token])
        return [token, self.then.token]

    def get(self, after):
        def body(g_ref, send2, recv2, after_ref, g_out):
            x, y, c = _place()
            for j, chip in enumerate(_other_chips(x, y)):
                slot_in = g_ref.at[2 * chip[0] + chip[1], :, pl.ds((1 - c) * HALF, HALF), :]
                slot_out = g_ref.at[2 * chip[0] + chip[1], :, pl.ds(c * HALF, HALF), :]
                cp = pltpu.make_async_remote_copy(
                    src_ref=slot_out, dst_ref=slot_in, send_sem=send2.at[j], recv_sem=recv2.at[j],
                    device_id=(x, y, 1 - c), device_id_type=_MESH,
                )
                cp.wait_send()
                cp.wait_recv()

        return pl.pallas_call(
            body,
            name=f"{self.tag}_wait",
            out_shape=pltpu.HBM(self.g.shape, self.g.dtype),
            in_specs=(_HBM_SPEC, _SEM_SPEC, _SEM_SPEC, _ANY_SPEC),
            out_specs=_HBM_SPEC,
            input_output_aliases={0: 0},
            compiler_params=_SPLIT_PARAMS,
        )(self.g, self.send2, self.recv2, after)


def _pair_sum(name, gb, land):
    def body(c_ref, gb_ref, land_ref, o_ref):
        o_ref[...] = (gb_ref[...].astype(F32) + land_ref[...].astype(F32)).astype(o_ref.dtype)

    core = lax.axis_index("c").astype(jnp.int32).reshape(1)
    return pl.pallas_call(
        body,
        name=name,
        grid_spec=pltpu.PrefetchScalarGridSpec(
            num_scalar_prefetch=1,
            grid=(N_CHIPS, gb.shape[1]),
            in_specs=[
                pl.BlockSpec((None, None, HALF, D_MODEL), lambda j, p, c: (j, p, c[0], 0)),
                pl.BlockSpec((None, None, HALF, D_MODEL), lambda j, p, c: (j, p, 0, 0)),
            ],
            out_specs=pl.BlockSpec((None, None, HALF, D_MODEL), lambda j, p, c: (j, p, 0, 0)),
        ),
        out_shape=_sds(land.shape, BF16),
        compiler_params=_cparams(("parallel", "parallel")),
    )(core, gb, land)


def _chip_sum_share(name, land):
    n = land.shape[1]

    def body(l_ref, r_ref, buf, send_sems, recv_sems, local_sems):
        p = pl.program_id(0)
        x, y, c = _place()
        acc = l_ref[0].astype(F32)
        for j in range(1, N_CHIPS):
            acc = acc + l_ref[j].astype(F32)
        buf[p] = acc

        def copies(q):
            mine = r_ref.at[q, pl.ds(c * HALF, HALF), :]
            theirs = r_ref.at[q, pl.ds((1 - c) * HALF, HALF), :]
            local = pltpu.make_async_copy(buf.at[q], mine, local_sems.at[q])
            out = pltpu.make_async_remote_copy(
                src_ref=buf.at[q], dst_ref=mine, send_sem=send_sems.at[q], recv_sem=recv_sems.at[q],
                device_id=(x, y, 1 - c), device_id_type=_MESH,
            )
            arrive = pltpu.make_async_remote_copy(
                src_ref=buf.at[q], dst_ref=theirs, send_sem=send_sems.at[q], recv_sem=recv_sems.at[q],
                device_id=(x, y, 1 - c), device_id_type=_MESH,
            )
            return local, out, arrive

        local, out, _ = copies(p)
        local.start()
        out.start()

        @pl.when(p == n - 1)
        def _():
            for q in range(n):
                local, out, arrive = copies(q)
                arrive.wait_recv()
                out.wait_send()
                local.wait()

    return pl.pallas_call(
        body,
        name=name,
        grid=(n,),
        in_specs=[pl.BlockSpec((N_CHIPS, None, HALF, D_MODEL), lambda p: (0, p, 0, 0))],
        out_specs=pl.BlockSpec(memory_space=pl.ANY),
        out_shape=_sds((n, F_SHARD, D_MODEL), F32),
        scratch_shapes=[
            pltpu.VMEM((n, HALF, D_MODEL), F32),
            pltpu.SemaphoreType.DMA((n,)), pltpu.SemaphoreType.DMA((n,)), pltpu.SemaphoreType.DMA((n,)),
        ],
        compiler_params=_cparams(("arbitrary",)),
    )(land)


class _ReduceBehind:
    def __init__(self, tag, gb, after=()):
        self.tag = tag
        n = gb.shape[1]
        land = lax.empty((N_CHIPS, n, HALF, D_MODEL), gb.dtype)

        def body(gb_ref, land_ref, *rest):
            send_sem, recv_sem, _, _, token = rest[len(after):]
            self._pair_copy(gb_ref, land_ref, send_sem, recv_sem).start()
            token[...] = jnp.zeros_like(token)

        self.send, self.recv, self.gb, self.land, self.token = pl.pallas_call(
            body,
            name=f"grads_pair_start_{tag}",
            out_shape=(
                pltpu.SemaphoreType.DMA((1,)), pltpu.SemaphoreType.DMA((1,)),
                pltpu.HBM(gb.shape, gb.dtype), pltpu.HBM(land.shape, land.dtype), _TOKEN,
            ),
            in_specs=(_HBM_SPEC, _HBM_SPEC) + (_ANY_SPEC,) * len(after),
            out_specs=(_SEM_SPEC, _SEM_SPEC, _HBM_SPEC, _HBM_SPEC, _VMEM_SPEC),
            input_output_aliases={0: 2, 1: 3},
            compiler_params=_SPLIT_PARAMS,
        )(_in_hbm(gb), _in_hbm(land), *after)

    @staticmethod
    def _pair_copy(gb_ref, land_ref, send_sem, recv_sem):
        x, y, c = _place()
        return pltpu.make_async_remote_copy(
            src_ref=gb_ref.at[:, :, pl.ds((1 - c) * HALF, HALF), :], dst_ref=land_ref,
            send_sem=send_sem.at[0], recv_sem=recv_sem.at[0], device_id=(x, y, 1 - c), device_id_type=_MESH,
        )

    @staticmethod
    def _chip_copies(p_ref, land_ref, send_sems, recv_sems):
        x, y, c = _place()
        me = 2 * x + y
        sends, arrivals = [], []
        for k, chip in enumerate(_other_chips(x, y)):
            them = 2 * chip[0] + chip[1]
            sends.append(pltpu.make_async_remote_copy(
                src_ref=p_ref.at[them], dst_ref=land_ref.at[me], send_sem=send_sems.at[k], recv_sem=recv_sems.at[k],
                device_id=(*chip, c), device_id_type=_MESH,
            ))
            arrivals.append(pltpu.make_async_remote_copy(
                src_ref=p_ref.at[me], dst_ref=land_ref.at[them], send_sem=send_sems.at[k], recv_sem=recv_sems.at[k],
                device_id=(*chip, c), device_id_type=_MESH,
            ))
        return sends, arrivals

    def mid(self, after):
        tag = self.tag

        def wait_body(gb_ref, land_ref, send_sem, recv_sem, after_ref, gb_out, land_out):
            cp = self._pair_copy(gb_ref, land_ref, send_sem, recv_sem)
            cp.wait_send()
            cp.wait_recv()

        gb, land = pl.pallas_call(
            wait_body,
            name=f"grads_pair_wait_{tag}",
            out_shape=(pltpu.HBM(self.gb.shape, self.gb.dtype), pltpu.HBM(self.land.shape, self.land.dtype)),
            in_specs=(_HBM_SPEC, _HBM_SPEC, _SEM_SPEC, _SEM_SPEC, _ANY_SPEC),
            out_specs=(_HBM_SPEC, _HBM_SPEC),
            input_output_aliases={0: 0, 1: 1},
            compiler_params=_SPLIT_PARAMS,
        )(self.gb, self.land, self.send, self.recv, after)
        part = _pair_sum(f"grads_pair_sum_{tag}", gb, land)

        x, y, _ = _place()
        start = (2 * x + y, 0, 0, 0)
        own = lax.dynamic_slice(part, start, (1,) + part.shape[1:])
        land2 = lax.dynamic_update_slice(lax.empty(part.shape, part.dtype), own, start)

        def start_body(p_ref, land_ref, send_sems, recv_sems, p_out, land_out, token):
            for cp in self._chip_copies(p_ref, land_ref, send_sems, recv_sems)[0]:
                cp.start()
            token[...] = jnp.zeros_like(token)

        self.send2, self.recv2, self.part, self.land2, token = pl.pallas_call(
            start_body,
            name=f"grads_chip_start_{tag}",
            out_shape=(
                pltpu.SemaphoreType.DMA((3,)), pltpu.SemaphoreType.DMA((3,)),
                pltpu.HBM(part.shape, part.dtype), pltpu.HBM(land2.shape, land2.dtype), _TOKEN,
            ),
            in_specs=(_HBM_SPEC, _HBM_SPEC),
            out_specs=(_SEM_SPEC, _SEM_SPEC, _HBM_SPEC, _HBM_SPEC, _VMEM_SPEC),
            input_output_aliases={0: 2, 1: 3},
            compiler_params=_SPLIT_PARAMS,
        )(_in_hbm(part), _in_hbm(land2))
        return [token]

    def get(self, after):
        n_after = len(after)

        def wait_body(p_ref, land_ref, send_sems, recv_sems, *rest):
            sends, arrivals = self._chip_copies(p_ref, land_ref, send_sems, recv_sems)
            for cp in arrivals:
                cp.wait_recv()
            for cp in sends:
                cp.wait_send()

        _, land2 = pl.pallas_call(
            wait_body,
            name=f"grads_chip_wait_{self.tag}",
            out_shape=(pltpu.HBM(self.part.shape, self.part.dtype), pltpu.HBM(self.land2.shape, self.land2.dtype)),
            in_specs=(_HBM_SPEC, _HBM_SPEC, _SEM_SPEC, _SEM_SPEC) + (_ANY_SPEC,) * n_after,
            out_specs=(_HBM_SPEC, _HBM_SPEC),
            input_output_aliases={0: 0, 1: 1},
            compiler_params=_SPLIT_PARAMS,
        )(self.part, self.land2, self.send2, self.recv2, *after)
        return _chip_sum_share(f"grads_chip_sum_share_{self.tag}", land2)


def _allreduce_small(blk):
    gathered = _allgather_small("allgather_small_grads", blk).reshape(N_DEV, PK_ROWS, 128)

    def body(g_ref, o_ref):
        acc = g_ref[0]
        for k in range(1, N_DEV):
            acc = acc + g_ref[k]
        o_ref[...] = acc

    total = pl.pallas_call(body, name="small_grads_sum", out_shape=_sds((PK_ROWS, 128), F32))(gathered)
    return gathered, total


def _adamw_math(w_, g_, m_, v_):
    m_new = ADAM_B1 * m_ + (1.0 - ADAM_B1) * g_
    v_new = ADAM_B2 * v_ + (1.0 - ADAM_B2) * (g_ * g_)
    m_hat = m_new / (1.0 - ADAM_B1 ** ADAM_STEP)
    v_hat = v_new / (1.0 - ADAM_B2 ** ADAM_STEP)
    delta = -ADAM_LR * (m_hat / (jnp.sqrt(v_hat) + ADAM_EPS) + ADAM_WD * w_)
    return delta, m_new, v_new


def _adamw(name, w, g, m, v):
    rows, cols = w.shape
    tm = rows
    while tm * cols * 4 > (1 << 21) and tm % 16 == 0:
        tm //= 2
    out = _sds(w.shape, F32)
    return _rowwise(name, _adamw_math, [w, g, m, v], [], [out, out, out], [], tm)


def _adamw_small(ws, gs, ms, vs):
    n = len(ws)
    shapes = [w.shape for w in ws]
    as2d = lambda a: a.reshape(1, -1) if a.ndim == 1 else a

    def body(*refs):
        ins, outs = refs[:4 * n], refs[4 * n:]
        for k in range(n):
            res = _adamw_math(*[ins[j * n + k][...] for j in range(4)])
            for j in range(3):
                outs[j * n + k][...] = res[j]

    args = [as2d(a) for group in (ws, gs, ms, vs) for a in group]
    out = pl.pallas_call(
        body, name="adamw_small", out_shape=[_sds(as2d(w).shape, F32) for w in ws] * 3, compiler_params=_cparams()
    )(*args)
    back = [o.reshape(shapes[i % n]) for i, o in enumerate(out)]
    return back[:n], back[n:2 * n], back[2 * n:]


def _pack_small(b_mod_like, n1, n2, n3, nf, qn, kvn, sinks, rel):
    parts = [
        b_mod_like.reshape(PK_MOD, 128),
        n1.reshape(8, 128), n2.reshape(8, 128), n3.reshape(8, 128), nf.reshape(8, 128),
        qn.reshape(2, 128), kvn.reshape(1, 128),
        jnp.pad(sinks.reshape(1, SWA_HEADS), ((0, 0), (0, 128 - SWA_HEADS))),
        rel.reshape(2, 128),
        jnp.zeros((2, 128), F32),
    ]
    return jnp.concatenate(parts, axis=0)


def _unpack_small(p):
    o = PK_MOD
    return (
        p[:o].reshape(1, N_MOD * D_MODEL),
        p[o:o + 8].reshape(1, D_MODEL), p[o + 8:o + 16].reshape(1, D_MODEL),
        p[o + 16:o + 24].reshape(1, D_MODEL), p[o + 24:o + 32].reshape(D_MODEL),
        p[o + 32:o + 34].reshape(1, MLA_Q_RANK), p[o + 34:o + 35].reshape(1, MLA_KV_RANK),
        p[o + 35:o + 36, :SWA_HEADS].reshape(1, SWA_HEADS),
        p[o + 36:o + 38].reshape(NUM_BUCKETS, SWA_HEADS),
    )


def kernel(x, c, w_mod, b_mod, norm_ffn1, ffn1_gate, ffn1_up, ffn1_down, norm_mix, w_in, q_norm, kv_norm, w_uq, w_ukv, sinks, w_o, norm_ffn2, ffn2_gate, ffn2_up, ffn2_down, rel_bias, norm_final, loss_target, m_w_mod, m_b_mod, m_norm_ffn1, m_ffn1_gate, m_ffn1_up, m_ffn1_down, m_norm_mix, m_w_in, m_q_norm, m_kv_norm, m_w_uq, m_w_ukv, m_sinks, m_w_o, m_norm_ffn2, m_ffn2_gate, m_ffn2_up, m_ffn2_down, m_rel_bias, m_norm_final, v_w_mod, v_b_mod, v_norm_ffn1, v_ffn1_gate, v_ffn1_up, v_ffn1_down, v_norm_mix, v_w_in, v_q_norm, v_kv_norm, v_w_uq, v_w_ukv, v_sinks, v_w_o, v_norm_ffn2, v_ffn2_gate, v_ffn2_up, v_ffn2_down, v_rel_bias, v_norm_final):
    ax, ay, ac = _place()
    chip = 2 * ax + ay
    dev = 2 * chip + ac
    n_mod_shard = N_MOD * D_MODEL // N_CHIPS

    def t16(w):
        return w[0].T.astype(BF16)

    rest = jnp.concatenate(
        [
            t16(w_in),
            w_o[0].astype(BF16),
            t16(w_uq).reshape(R_UQ, D_MODEL),
            t16(w_ukv).reshape(R_UKV, D_MODEL),
            jnp.zeros((F_SHARD - O_PAD, D_MODEL), BF16),
        ],
        axis=0,
    )
    own_gu1 = jnp.stack([t16(ffn1_gate), t16(ffn1_up)])
    own_down1 = ffn1_down[0].astype(BF16)[None]
    own_mix = rest[None]
    own_ffn2 = jnp.stack([t16(ffn2_gate), t16(ffn2_up), ffn2_down[0].astype(BF16)])

    (c_act,) = _rowwise(
        "silu_c", lambda v: v * _sigmoid(v), [c.reshape(8, 128)], [], [_sds((8, 128), F32)], [], 8
    )
    c_all = _allgather_small("allgather_c", c_act).reshape(N_DEV, D_MODEL)
    mod_cols = _mm(
        "mod_fwd", "nn", (1, n_mod_shard // MOD_TILE, 1),
        c_all, (N_DEV, D_MODEL), lambda i, j, k: (0, 0),
        w_mod[0], (D_MODEL, MOD_TILE), lambda i, j, k: (0, j),
        _sds((N_DEV, n_mod_shard), F32), (N_DEV, MOD_TILE), lambda i, j, k: (0, j),
        (N_DEV, MOD_TILE),
    )
    mod_all = _allgather_small("allgather_mod", mod_cols).reshape(N_CHIPS, 2, N_DEV, n_mod_shard)[:, 0]
    mod_all = mod_all.transpose(1, 0, 2).reshape(N_DEV, N_MOD * D_MODEL)
    mod = lax.dynamic_slice_in_dim(mod_all, dev, 1, axis=0) + b_mod

    norms = (norm_ffn1, norm_mix, norm_ffn2, norm_final.reshape(1, D_MODEL))

    ffn2_src = _GatherBehind("gather_ffn2", own_ffn2)
    mix_src = _GatherBehind("gather_mix", own_mix, then=ffn2_src)
    down1_src = _GatherBehind("gather_down1", own_down1, then=mix_src)
    gu1_src = _GatherBehind("gather_gu1", own_gu1, then=down1_src)
    gu1_src.start([mod_all])

    reducing, red = {}, {}

    def begin(tag, gb, after):
        reducing[tag] = _ReduceBehind(tag, gb, after=after)
        return [reducing[tag].token]

    def ffn2_gu_done(gb):
        return begin("ffn2_gu", gb, reducing["ffn2_down"].mid(gb))

    def mix_done(dx1, gb_rest):
        red["ffn2_down"] = reducing["ffn2_down"].get([dx1])
        red["ffn2_gu"] = reducing["ffn2_gu"].get([red["ffn2_down"]])
        return begin("rest", gb_rest, [red["ffn2_gu"]])

    def ffn1_gu_done(gb):
        red["rest"] = reducing["rest"].get([gb])
        begin("ffn1_gu", gb, reducing["ffn1_down"].mid(red["rest"]))
        return reducing["ffn1_gu"].mid(reducing["ffn1_gu"].token)

    loss_part, grad_x, _, dmod, small = _device_step(
        x[0], loss_target[0], mod, gu1_src, down1_src, mix_src, ffn2_src, norms, q_norm, kv_norm, sinks, rel_bias,
        hooks2=_BwdHooks(after_wdown=lambda gb: begin("ffn2_down", gb, ()), after_wgu=ffn2_gu_done),
        hooks_mix=_BwdHooks(after_gate=lambda dz: reducing["ffn2_gu"].mid(dz)),
        mix_done=mix_done,
        hooks1=_BwdHooks(
            after_swiglu=lambda dab3: reducing["rest"].mid(dab3),
            after_wdown=lambda gb: begin("ffn1_down", gb, ()),
            after_wgu=ffn1_gu_done,
        ),
    )
    loss = lax.psum(loss_part, ("x", "y", "c"))

    gathered, total = _allreduce_small(_pack_small(dmod, *small))
    (g_b_mod, g_n1, g_n2, g_n3, g_nf, g_qn, g_kvn, g_sinks, g_rel) = _unpack_small(total)
    dmod_all = gathered[:, :PK_MOD].reshape(N_DEV, N_MOD * D_MODEL)
    dmod_cols = lax.dynamic_slice_in_dim(dmod_all, chip * n_mod_shard, n_mod_shard, axis=1)
    g_w_mod = _mm(
        "mod_bwd", "tn", (1, n_mod_shard // MOD_TILE, 1),
        c_all, (N_DEV, D_MODEL), lambda i, j, k: (0, 0),
        dmod_cols, (N_DEV, MOD_TILE), lambda i, j, k: (0, j),
        _sds((D_MODEL, n_mod_shard), F32), (D_MODEL, MOD_TILE), lambda i, j, k: (0, j),
        (D_MODEL, MOD_TILE),
    )

    r_rest = red["rest"][0]
    transposed = {"ffn1_gate", "ffn1_up", "ffn2_gate", "ffn2_up", "w_in", "w_uq", "w_ukv"}
    g_big = {
        "ffn2_gate": red["ffn2_gu"][0], "ffn2_up": red["ffn2_gu"][1], "ffn2_down": red["ffn2_down"][0],
        "w_in": r_rest[O_IN:O_IN + R_IN],
        "w_o": r_rest[O_O:O_O + R_O],
        "w_uq": r_rest[O_UQ:O_UQ + R_UQ].reshape(MLA_QK, MLA_Q_RANK),
        "w_ukv": r_rest[O_UKV:O_UKV + R_UKV].reshape(MLA_NOPE + MLA_V, MLA_KV_RANK),
        "w_mod": g_w_mod,
    }
    w_big = {
        "ffn2_gate": (ffn2_gate, m_ffn2_gate, v_ffn2_gate),
        "ffn2_up": (ffn2_up, m_ffn2_up, v_ffn2_up), "ffn2_down": (ffn2_down, m_ffn2_down, v_ffn2_down),
        "w_in": (w_in, m_w_in, v_w_in), "w_o": (w_o, m_w_o, v_w_o), "w_uq": (w_uq, m_w_uq, v_w_uq),
        "w_ukv": (w_ukv, m_w_ukv, v_w_ukv), "w_mod": (w_mod, m_w_mod, v_w_mod),
    }
    grads, deltas, new_m, new_v = {}, {}, {}, {}

    def update(names):
        for nm in names:
            w, m, v = w_big[nm]
            if nm in transposed:
                outs = _adamw(f"adamw_{nm}", w[0].T, g_big[nm], m[0].T, v[0].T)
                g_, d_, m_, v_ = [a.T for a in (g_big[nm],) + tuple(outs)]
            else:
                g_, (d_, m_, v_) = g_big[nm], _adamw(f"adamw_{nm}", w[0], g_big[nm], m[0], v[0])
            grads[nm], deltas[nm], new_m[nm], new_v[nm] = g_[None], d_[None], m_[None], v_[None]

    update(list(w_big))
    red1_down = reducing["ffn1_down"].get([deltas[nm] for nm in w_big])
    red1_gu = reducing["ffn1_gu"].get([red1_down])
    g_big.update({"ffn1_gate": red1_gu[0], "ffn1_up": red1_gu[1], "ffn1_down": red1_down[0]})
    w_big.update({
        "ffn1_gate": (ffn1_gate, m_ffn1_gate, v_ffn1_gate), "ffn1_up": (ffn1_up, m_ffn1_up, v_ffn1_up),
        "ffn1_down": (ffn1_down, m_ffn1_down, v_ffn1_down),
    })
    update(["ffn1_gate", "ffn1_up", "ffn1_down"])

    small_names = ["b_mod", "norm_ffn1", "norm_mix", "norm_ffn2", "norm_final", "q_norm", "kv_norm", "sinks", "rel_bias"]
    w_small = (b_mod, norm_ffn1, norm_mix, norm_ffn2, norm_final, q_norm, kv_norm, sinks, rel_bias)
    m_small = (m_b_mod, m_norm_ffn1, m_norm_mix, m_norm_ffn2, m_norm_final, m_q_norm, m_kv_norm, m_sinks, m_rel_bias)
    v_small = (v_b_mod, v_norm_ffn1, v_norm_mix, v_norm_ffn2, v_norm_final, v_q_norm, v_kv_norm, v_sinks, v_rel_bias)
    g_small = (g_b_mod, g_n1, g_n2, g_n3, g_nf, g_qn, g_kvn, g_sinks, g_rel)
    d_s, m_s, v_s = _adamw_small(w_small, g_small, m_small, v_small)
    for nm, g_, d_, m_, v_ in zip(small_names, g_small, d_s, m_s, v_s):
        grads[nm], deltas[nm], new_m[nm], new_v[nm] = g_, d_, m_, v_

    order = ["w_mod", "b_mod", "norm_ffn1", "ffn1_gate", "ffn1_up", "ffn1_down", "norm_mix", "w_in", "q_norm", "kv_norm",
             "w_uq", "w_ukv", "sinks", "w_o", "norm_ffn2", "ffn2_gate", "ffn2_up", "ffn2_down", "rel_bias", "norm_final"]
    return (loss, grad_x[None], *[grads[n] for n in order], *[deltas[n] for n in order],
            *[new_m[n] for n in order], *[new_v[n] for n in order])
```

```python
import functools
import math

import jax
import jax.numpy as jnp
import numpy as np
from jax import lax
from jax.experimental import pallas as pl
from jax.experimental.pallas import tpu as pltpu

F32, BF16 = jnp.float32, jnp.bfloat16

D_MODEL = 1024
D_FF = 2816
EPS = 1e-6
N_MOD = 9
SWA_HEADS, SWA_KV_HEADS, SWA_HEAD_DIM, WINDOW = 8, 2, 64, 128
SWA_GROUP = SWA_HEADS // SWA_KV_HEADS
MLA_HEADS, MLA_Q_RANK, MLA_KV_RANK, MLA_NOPE, MLA_ROPE, MLA_V = 4, 256, 128, 128, 64, 128
MLA_QK = MLA_NOPE + MLA_ROPE
ROPE_THETA = 10000.0
NUM_BUCKETS, MAX_DISTANCE = 32, 128
D_IN = 1216
N_SWA_COLS = 768
N_MLA_COLS = 512

ADAM_LR, ADAM_B1, ADAM_B2, ADAM_EPS, ADAM_WD, ADAM_STEP = 0.001, 0.9, 0.999, 1e-08, 0.01, 10

N_CHIPS = 4
N_DEV = 8
F_SHARD = D_FF // N_CHIPS
HALF = F_SHARD // 2
R_IN, R_O, R_UQ, R_UKV = 304, 256, 48, 32
O_IN, O_O, O_UQ, O_UKV, O_PAD = 0, 304, 560, 608, 640

PK_MOD = 72
PK_ROWS = 112
VMEM_LIMIT = 56 << 20
ROW_TILE = 2048
ELEM_TILE = 512
MOD_TILE = 768

_DN = {
    "nn": (((1,), (0,)), ((), ())),
    "nt": (((1,), (1,)), ((), ())),
    "tn": (((0,), (0,)), ((), ())),
}


def _sds(shape, dtype):
    return jax.ShapeDtypeStruct(tuple(shape), dtype)


def _cparams(sem=None):
    kw = {"vmem_limit_bytes": VMEM_LIMIT}
    if sem is not None:
        kw["dimension_semantics"] = sem
    return pltpu.CompilerParams(**kw)


def _dot(a, b, dims):
    return lax.dot_general(a.astype(BF16), b.astype(BF16), _DN[dims], preferred_element_type=F32)


def _mm(name, dims, grid, a, a_blk, a_idx, b, b_blk, b_idx, out, out_blk, out_idx, acc_shape, alias=None, deps=()):
    nk = grid[2]

    def body(*refs):
        a_ref, b_ref = refs[:2]
        o_ref, acc_ref = refs[-2:]
        part = _dot(a_ref[...], b_ref[...], dims)
        if nk == 1:
            o_ref[...] = part.astype(o_ref.dtype)
            return
        k = pl.program_id(2)

        @pl.when(k == 0)
        def _():
            acc_ref[...] = part

        @pl.when((k > 0) & (k < nk - 1))
        def _():
            acc_ref[...] += part

        @pl.when(k == nk - 1)
        def _():
            o_ref[...] = (acc_ref[...] + part).astype(o_ref.dtype)

    in_specs = [pl.BlockSpec(a_blk, a_idx), pl.BlockSpec(b_blk, b_idx)]
    args = [a, b]
    kw = {}
    if alias is not None:
        in_specs.append(pl.BlockSpec(memory_space=pl.ANY))
        args.append(alias)
        kw["input_output_aliases"] = {2: 0}
    in_specs += [pl.BlockSpec(memory_space=pl.ANY)] * len(deps)
    args += list(deps)
    return pl.pallas_call(
        body,
        name=name,
        grid=grid,
        in_specs=in_specs,
        out_specs=pl.BlockSpec(out_blk, out_idx),
        out_shape=out,
        scratch_shapes=[pltpu.VMEM(acc_shape if nk > 1 else (8, 128), F32)],
        compiler_params=_cparams(("parallel", "parallel", "arbitrary")),
        **kw,
    )(*args)


def _rowwise(name, fn, tiled, full, out_tiled, out_red, tm, deps=()):
    rows = tiled[0].shape[-2]
    tm = min(tm, rows)
    grid = (rows // tm,)
    n_in = len(tiled) + len(full)
    n_t = len(out_tiled)
    n_dep = len(deps)

    def tspec(shape):
        nd = len(shape)
        return pl.BlockSpec(tuple(shape[:-2]) + (tm, shape[-1]), lambda i, nd=nd: (0,) * (nd - 2) + (i, 0))

    def fspec(shape):
        nd = len(shape)
        return pl.BlockSpec(tuple(shape), lambda i, nd=nd: (0,) * nd)

    def body(*refs):
        outs = fn(*[r[...] for r in refs[:n_in]])
        if not isinstance(outs, (tuple, list)):
            outs = (outs,)
        out_refs = refs[n_in + n_dep:]
        for r, v in zip(out_refs[:n_t], outs[:n_t]):
            r[...] = v.astype(r.dtype)
        red_refs = out_refs[n_t:]
        if red_refs:
            @pl.when(pl.program_id(0) == 0)
            def _():
                for r in red_refs:
                    r[...] = jnp.zeros_like(r)

            for r, v in zip(red_refs, outs[n_t:]):
                r[...] += v.astype(r.dtype)

    res = pl.pallas_call(
        body,
        name=name,
        grid=grid,
        in_specs=[tspec(a.shape) for a in tiled] + [fspec(a.shape) for a in full]
        + [pl.BlockSpec(memory_space=pl.ANY)] * n_dep,
        out_specs=[tspec(o.shape) for o in out_tiled] + [fspec(o.shape) for o in out_red],
        out_shape=list(out_tiled) + list(out_red),
        compiler_params=_cparams(("arbitrary",) if out_red else ("parallel",)),
    )(*tiled, *full, *deps)
    return res


def _sum0(v):
    return jnp.sum(v, axis=0, keepdims=True)


def _sigmoid(v):
    return 1.0 / (1.0 + jnp.exp(-v))


def _rms(x):
    r = lax.rsqrt(jnp.mean(x * x, axis=-1, keepdims=True) + EPS)
    return x * r, r


def _rms_bwd(u, xr, r):
    return r * (u - xr * jnp.mean(u * xr, axis=-1, keepdims=True))


def _normmod(x_, gamma_, sc_, sh_):
    return _rms(x_)[0] * gamma_ * (1.0 + sc_) + sh_


def _row_blocks(tm, size=512):
    size = min(size, tm)
    return [slice(r, r + size) for r in range(0, tm, size)]


def _loss_head(x_, t_, g_):
    xr, r = _rms(x_)
    err = xr * g_ - t_
    dy = err * (1.0 / D_MODEL)
    return _rms_bwd(dy * g_, xr, r), _sum0(err * err), _sum0(dy * xr)


def _ffn_fwd(tag, x, gamma, sh, sc, gate, gu_src, base, down_src, down_idx, mid_after, h_pre=None,
             next_norm=None, head=None):
    s_len = x.shape[0]
    if h_pre is None:
        (h,) = _rowwise(
            f"{tag}_normmod", _normmod, [x], [gamma, sc, sh], [_sds((s_len, D_MODEL), BF16)], [], ELEM_TILE
        )
        gdeps = gu_src.mid(h)
    else:
        h, gdeps = h_pre, gu_src.mid(mid_after)
    g4 = gu_src.get(h)

    tm = min(ROW_TILE, s_len)
    def gate_up(h_ref, wg_ref, wu_ref, *rest):
        ab_ref, s_ref = rest[-2:]
        wg, wu = wg_ref[...], wu_ref[...]
        for rows in _row_blocks(tm):
            hv = h_ref[rows, :]
            a = _dot(hv, wg, "nt")
            b = _dot(hv, wu, "nt")
            ab_ref[0, rows, :] = a.astype(ab_ref.dtype)
            ab_ref[1, rows, :] = b.astype(ab_ref.dtype)
            s_ref[rows, :] = (a * _sigmoid(a) * b).astype(s_ref.dtype)

    ab4, s3 = pl.pallas_call(
        gate_up,
        name=f"{tag}_gate_up",
        grid=(s_len // tm, N_CHIPS),
        in_specs=[
            pl.BlockSpec((tm, D_MODEL), lambda i, c: (i, 0)),
            pl.BlockSpec((None, None, F_SHARD, D_MODEL), lambda i, c: (c, base, 0, 0)),
            pl.BlockSpec((None, None, F_SHARD, D_MODEL), lambda i, c: (c, base + 1, 0, 0)),
        ] + [pl.BlockSpec(memory_space=pl.ANY)] * len(gdeps),
        out_specs=[
            pl.BlockSpec((None, 2, tm, F_SHARD), lambda i, c: (c, 0, i, 0)),
            pl.BlockSpec((None, tm, F_SHARD), lambda i, c: (c, i, 0)),
        ],
        out_shape=[_sds((N_CHIPS, 2, s_len, F_SHARD), BF16), _sds((N_CHIPS, s_len, F_SHARD), BF16)],
        compiler_params=_cparams(("parallel", "parallel")),
    )(h, g4, g4, *gdeps)
    if down_src is None:
        g_down, ddeps = g4, ()
    else:
        ddeps = down_src.mid(ab4)
        g_down = down_src.get(s3)

    y = _mm(
        f"{tag}_down", "nn", (s_len // tm, 1, N_CHIPS),
        s3, (None, tm, F_SHARD), lambda i, j, k: (k, i, 0),
        g_down, (None, None, F_SHARD, D_MODEL), lambda i, j, k: (k, down_idx, 0, 0),
        _sds((s_len, D_MODEL), F32), (tm, D_MODEL), lambda i, j, k: (i, 0),
        (tm, D_MODEL), deps=ddeps,
    )

    saved = (g4, base, g_down, down_idx, h, ab4, s3, y)
    act = _sds((s_len, D_MODEL), F32)
    if head is not None:
        target, norm_final = head
        vec = _sds((1, D_MODEL), F32)
        out = _rowwise(
            f"{tag}_residual_head", lambda x_, y_, t_, g_, nf_: _loss_head(x_ + 0.5 * g_ * y_, t_, nf_),
            [x, y, target], [gate, norm_final], [act], [vec, vec], ELEM_TILE,
        )
    elif next_norm is not None:
        def residual_norm(x_, y_, g_, gamma_, sc_, sh_):
            x_out = x_ + 0.5 * g_ * y_
            return x_out, _normmod(x_out, gamma_, sc_, sh_)

        out = _rowwise(
            f"{tag}_residual_norm", residual_norm, [x, y], [gate] + list(next_norm),
            [act, _sds((s_len, D_MODEL), BF16)], [], ELEM_TILE,
        )
    else:
        out = _rowwise(f"{tag}_residual", lambda x_, y_, g_: x_ + 0.5 * g_ * y_, [x, y], [gate], [act], [], ELEM_TILE)
    return out, saved


def _normmod_bwd_call(name, dh_parts, x, dxo, gamma, sc, deps=()):
    s_len = x.shape[0]
    n_parts = len(dh_parts)

    def fn(*vals):
        dh = vals[0]
        for extra in vals[1:n_parts]:
            dh = dh + extra
        x_, dxo_, gamma_, sc_ = vals[n_parts:]
        xr, r = _rms(x_)
        n = xr * gamma_
        dn = dh * (1.0 + sc_)
        dx = dxo_ + _rms_bwd(dn * gamma_, xr, r)
        return dx, _sum0(dh * n), _sum0(dh), _sum0(dn * xr)

    vec = _sds((1, D_MODEL), F32)
    return _rowwise(
        name, fn, list(dh_parts) + [x, dxo], [gamma, sc], [_sds((s_len, D_MODEL), F32)], [vec, vec, vec], ELEM_TILE, deps=deps
    )


class _BwdHooks:
    def __init__(self, after_gate=None, after_swiglu=None, after_wdown=None, after_wgu=None, after_dh=None):
        def nothing(_):
            return []

        self.after_gate = after_gate or nothing
        self.after_swiglu = after_swiglu or nothing
        self.after_wdown = after_wdown or nothing
        self.after_wgu = after_wgu or nothing
        self.after_dh = after_dh or nothing


def _ffn_bwd(tag, dxo, x, gamma, sc, gate, saved, hooks, deps=()):
    g4, base, g_down, down_idx, h, ab4, s3, y = saved
    s_len = x.shape[0]
    vec = _sds((1, D_MODEL), F32)

    dy, dgate = _rowwise(
        f"{tag}_bwd_gate", lambda dxo_, y_, g_: (0.5 * g_ * dxo_, _sum0(0.5 * y_ * dxo_)),
        [dxo, y], [gate], [_sds((s_len, D_MODEL), BF16)], [vec], ELEM_TILE, deps=deps,
    )

    tm = min(ROW_TILE, s_len)

    def d_gate_up(dy_ref, wd_ref, ab_ref, o_ref):
        wd = wd_ref[...]
        for rows in _row_blocks(tm):
            ds = _dot(dy_ref[rows, :], wd, "nt")
            a = ab_ref[0, rows, :].astype(F32)
            b = ab_ref[1, rows, :].astype(F32)
            sig = _sigmoid(a)
            o_ref[0, rows, :] = (ds * b * sig * (1.0 + a * (1.0 - sig))).astype(o_ref.dtype)
            o_ref[1, rows, :] = (ds * a * sig).astype(o_ref.dtype)

    ab_spec = pl.BlockSpec((None, 2, tm, F_SHARD), lambda i, c: (c, 0, i, 0))
    dab4 = pl.pallas_call(
        d_gate_up,
        name=f"{tag}_bwd_dgate_up",
        grid=(s_len // tm, N_CHIPS),
        in_specs=[
            pl.BlockSpec((tm, D_MODEL), lambda i, c: (i, 0)),
            pl.BlockSpec((None, None, F_SHARD, D_MODEL), lambda i, c: (c, down_idx, 0, 0)),
            ab_spec,
        ],
        out_specs=ab_spec,
        out_shape=_sds(ab4.shape, BF16),
        compiler_params=_cparams(("parallel", "parallel")),
    )(dy, g_down, ab4)

    tk = tm
    gb_down = _mm(
        f"{tag}_bwd_wdown", "tn", (N_CHIPS, 1, s_len // tk),
        s3, (None, tk, F_SHARD), lambda i, j, k: (i, k, 0),
        dy, (tk, D_MODEL), lambda i, j, k: (k, 0),
        _sds((N_CHIPS, 1, F_SHARD, D_MODEL), BF16), (None, None, F_SHARD, D_MODEL), lambda i, j, k: (i, 0, 0, 0),
        (F_SHARD, D_MODEL), deps=hooks.after_swiglu(dab4),
    )
    gb_gu = _mm(
        f"{tag}_bwd_wgu", "tn", (2 * N_CHIPS, 1, s_len // tk),
        dab4, (None, None, tk, F_SHARD), lambda i, j, k: (i % N_CHIPS, i // N_CHIPS, k, 0),
        h, (tk, D_MODEL), lambda i, j, k: (k, 0),
        _sds((N_CHIPS, 2, F_SHARD, D_MODEL), BF16), (None, None, F_SHARD, D_MODEL),
        lambda i, j, k: (i % N_CHIPS, i // N_CHIPS, 0, 0),
        (F_SHARD, D_MODEL), deps=hooks.after_wdown(gb_down),
    )
    assert base % 2 == 0
    dh_deps = hooks.after_wgu(gb_gu)

    def d_h(dab_ref, w_ref, *rest):
        o_ref, acc_ref = rest[-2:]
        c = pl.program_id(1)
        part = _dot(dab_ref[0], w_ref[0], "nn") + _dot(dab_ref[1], w_ref[1], "nn")

        @pl.when(c == 0)
        def _():
            acc_ref[...] = part

        @pl.when((c > 0) & (c < N_CHIPS - 1))
        def _():
            acc_ref[...] += part

        @pl.when(c == N_CHIPS - 1)
        def _():
            o_ref[...] = acc_ref[...] + part

    dh = pl.pallas_call(
        d_h,
        name=f"{tag}_bwd_dh",
        grid=(s_len // tm, N_CHIPS),
        in_specs=[
            pl.BlockSpec((None, 2, tm, F_SHARD), lambda i, c: (c, 0, i, 0)),
            pl.BlockSpec((None, 2, F_SHARD, D_MODEL), lambda i, c: (c, base // 2, 0, 0)),
        ] + [pl.BlockSpec(memory_space=pl.ANY)] * len(dh_deps),
        out_specs=pl.BlockSpec((tm, D_MODEL), lambda i, c: (i, 0)),
        out_shape=_sds((s_len, D_MODEL), F32),
        scratch_shapes=[pltpu.VMEM((tm, D_MODEL), F32)],
        compiler_params=_cparams(("parallel", "arbitrary")),
    )(dab4, g4, *dh_deps)
    dx, dsc, dsh, dgamma = _normmod_bwd_call(
        f"{tag}_bwd_normmod", [dh], x, dxo, gamma, sc, deps=hooks.after_dh(dh)
    )
    return dx, (gb_down, gb_gu), (dsh, dsc, dgate, dgamma)


def _bucket_map():
    qi = np.arange(WINDOW)[:, None]
    kj = np.arange(2 * WINDOW)[None, :]
    dist = qi + WINDOW - kj
    band = (dist >= 0) & (dist < WINDOW)
    max_exact = NUM_BUCKETS // 2
    n = np.maximum(dist, 0)
    large = []
    for dt in (np.float32, np.float64):
        nf = np.maximum(n, 1).astype(dt)
        val = np.log(nf / dt(max_exact)) / dt(math.log(MAX_DISTANCE / max_exact)) * dt(NUM_BUCKETS - max_exact)
        large.append(np.minimum(max_exact + val.astype(np.int32), NUM_BUCKETS - 1))
    assert np.array_equal(large[0], large[1])
    bucket = np.where(n < max_exact, n, large[0])
    return np.where(band, bucket, -1).astype(np.int32).reshape(1, -1)


def _bias_expand(rel_bias_t, bkt):
    n = bkt.shape[1]

    def body(rb_ref, bkt_ref, o_ref):
        onehot = (lax.broadcasted_iota(jnp.int32, (NUM_BUCKETS, n), 0) == bkt_ref[...]).astype(F32)
        o_ref[...] = lax.dot_general(
            rb_ref[...], onehot, _DN["nn"], precision=lax.Precision.HIGHEST, preferred_element_type=F32
        )

    return pl.pallas_call(
        body, name="bias_expand", out_shape=_sds((SWA_HEADS, n), F32), compiler_params=_cparams()
    )(rel_bias_t, bkt)


def _bias_reduce(dbias_flat, bkt):
    n = bkt.shape[1]

    def body(db_ref, bkt_ref, o_ref):
        onehot = (lax.broadcasted_iota(jnp.int32, (NUM_BUCKETS, n), 0) == bkt_ref[...]).astype(F32)
        o_ref[...] = lax.dot_general(
            db_ref[...], onehot, _DN["nt"], precision=lax.Precision.HIGHEST, preferred_element_type=F32
        )

    return pl.pallas_call(
        body, name="bias_reduce", out_shape=_sds((SWA_HEADS, NUM_BUCKETS), F32), compiler_params=_cparams()
    )(dbias_flat, bkt)


_SWA_SCALE = SWA_HEAD_DIM ** -0.5
_NEG = -1e30


_SWA_PAIR = 2


def _swa_in_specs():
    w = WINDOW
    n_q = SWA_HEADS * SWA_HEAD_DIM
    n_kv = 2 * SWA_KV_HEADS * SWA_HEAD_DIM
    prev = lambda m: jnp.maximum(_SWA_PAIR * m - 1, 0)
    return [
        pl.BlockSpec((_SWA_PAIR * w, n_q), lambda m: (m, 0)),
        pl.BlockSpec((w, n_kv), lambda m: (prev(m), n_q // n_kv)),
        pl.BlockSpec((_SWA_PAIR * w, n_kv), lambda m: (m, n_q // n_kv)),
        pl.BlockSpec((SWA_HEADS, w, 2 * w), lambda m: (0, 0, 0)),
        pl.BlockSpec((SWA_HEADS, w, 1), lambda m: (0, 0, 0)),
    ]


def _head_cols(v, first, count):
    hd = SWA_HEAD_DIM
    return jnp.stack([v[:, (first + i) * hd:(first + i + 1) * hd] for i in range(count)])


def _swa_blocks(q_ref, kvp_ref, kvc_ref, m):
    g, w, hd = SWA_GROUP, WINDOW, SWA_HEAD_DIM
    kv_all = jnp.concatenate([kvp_ref[...], kvc_ref[...]], axis=0).astype(F32)
    blocks = []
    for sub in range(_SWA_PAIR):
        rows = slice(sub * w, (sub + 1) * w)
        q = q_ref[rows, :].astype(F32)
        kv = kv_all[sub * w:(sub + 2) * w]
        heads = []
        for j in range(SWA_KV_HEADS):
            qj = _head_cols(q, g * j, g).reshape(g * w, hd)
            heads.append((qj, _head_cols(kv, j, 1)[0], _head_cols(kv, SWA_KV_HEADS + j, 1)[0]))
        blocks.append((_SWA_PAIR * m + sub, rows, heads))
    return blocks


def _swa_scores(q, kk, bias, n):
    g, w = SWA_GROUP, WINDOW
    s = (_dot(q, kk, "nt") * _SWA_SCALE).reshape(g, w, 2 * w) + bias
    qi = lax.broadcasted_iota(jnp.int32, (w, 2 * w), 0)
    kj = lax.broadcasted_iota(jnp.int32, (w, 2 * w), 1)
    dist = qi + w - kj
    valid = ((dist >= 0) & (dist < w) & ((n > 0) | (kj >= w)))[None]
    return jnp.where(valid, s, _NEG), valid


def _swa_fwd(swa2, bias, sinkb):
    s_len = swa2.shape[0]
    g, w, hd = SWA_GROUP, WINDOW, SWA_HEAD_DIM

    def body(q_ref, kvp_ref, kvc_ref, bias_ref, sink_ref, o_ref, lse_ref):
        for n, rows, heads in _swa_blocks(q_ref, kvp_ref, kvc_ref, pl.program_id(0)):
            outs = []
            for j, (q, kk, vv) in enumerate(heads):
                hs = slice(g * j, g * (j + 1))
                s, valid = _swa_scores(q, kk, bias_ref[hs], n)
                sink = sink_ref[hs]
                m = jnp.maximum(jnp.max(s, axis=-1, keepdims=True), sink)
                p = jnp.where(valid, jnp.exp(s - m), 0.0)
                l = jnp.sum(p, axis=-1, keepdims=True) + jnp.exp(sink - m)
                o = _dot(p.reshape(g * w, 2 * w), vv, "nn").reshape(g, w, hd) / l
                outs += [o[i] for i in range(g)]
                lse_ref[hs, rows, :] = m + jnp.log(l)
            o_ref[rows, :] = jnp.concatenate(outs, axis=-1).astype(o_ref.dtype)

    n_q = SWA_HEADS * hd
    return pl.pallas_call(
        body,
        name="swa_fwd",
        grid=(s_len // (_SWA_PAIR * w),),
        in_specs=_swa_in_specs(),
        out_specs=[
            pl.BlockSpec((_SWA_PAIR * w, n_q), lambda m: (m, 0)),
            pl.BlockSpec((SWA_HEADS, _SWA_PAIR * w, 1), lambda m: (0, m, 0)),
        ],
        out_shape=[_sds((s_len, n_q), BF16), _sds((SWA_HEADS, s_len, 1), F32)],
        compiler_params=_cparams(("parallel",)),
    )(swa2, swa2, swa2, bias, sinkb)


def _swa_bwd(swa2, bias, sinkb, cat, dcat, lse):
    s_len = swa2.shape[0]
    g, w, hd = SWA_GROUP, WINDOW, SWA_HEAD_DIM

    def body(q_ref, kvp_ref, kvc_ref, bias_ref, sink_ref, o_ref, do_ref, lse_ref,
             dq_ref, dkva_ref, dkvb_ref, dbias_ref, dsink_ref):
        step = pl.program_id(0)

        @pl.when(step == 0)
        def _():
            dbias_ref[...] = jnp.zeros_like(dbias_ref)
            dsink_ref[...] = jnp.zeros_like(dsink_ref)

        for n, rows, heads in _swa_blocks(q_ref, kvp_ref, kvc_ref, step):
            o_all = o_ref[rows, :].astype(F32)
            do_all = do_ref[rows, :].astype(F32)
            dqs, dks, dvs = [], [], []
            for j, (q, kk, vv) in enumerate(heads):
                hs = slice(g * j, g * (j + 1))
                s, valid = _swa_scores(q, kk, bias_ref[hs], n)
                lse_v = lse_ref[hs, rows, :]
                p = jnp.where(valid, jnp.exp(s - lse_v), 0.0)
                do = _head_cols(do_all, g * j, g)
                delta = jnp.sum(do * _head_cols(o_all, g * j, g), axis=-1, keepdims=True)
                do2 = do.reshape(g * w, hd)
                dp = _dot(do2, vv, "nt").reshape(g, w, 2 * w)
                ds = p * (dp - delta)
                dbias_ref[hs] += ds
                dsink_ref[hs] -= jnp.exp(sink_ref[hs] - lse_v) * delta
                ds2 = ds.reshape(g * w, 2 * w)
                dq = (_dot(ds2, kk, "nn") * _SWA_SCALE).reshape(g, w, hd)
                dqs += [dq[i] for i in range(g)]
                dks.append(_dot(ds2, q, "tn") * _SWA_SCALE)
                dvs.append(_dot(p.reshape(g * w, 2 * w), do2, "tn"))
            dq_ref[rows, :] = jnp.concatenate(dqs, axis=-1).astype(dq_ref.dtype)
            dkv = jnp.concatenate(dks + dvs, axis=-1)
            dkvb_ref[rows, :] = dkv[:w]
            dkva_ref[rows, :] = dkv[w:]

    n_q = SWA_HEADS * hd
    n_kv = 2 * SWA_KV_HEADS * hd
    q_spec = pl.BlockSpec((_SWA_PAIR * w, n_q), lambda m: (m, 0))
    kv_spec = pl.BlockSpec((_SWA_PAIR * w, n_kv), lambda m: (m, 0))
    return pl.pallas_call(
        body,
        name="swa_bwd",
        grid=(s_len // (_SWA_PAIR * w),),
        in_specs=_swa_in_specs() + [q_spec, q_spec, pl.BlockSpec((SWA_HEADS, _SWA_PAIR * w, 1), lambda m: (0, m, 0))],
        out_specs=[
            q_spec, kv_spec, kv_spec,
            pl.BlockSpec((SWA_HEADS, w, 2 * w), lambda n: (0, 0, 0)),
            pl.BlockSpec((SWA_HEADS, w, 1), lambda n: (0, 0, 0)),
        ],
        out_shape=[
            _sds((s_len, n_q), BF16), _sds((s_len, n_kv), F32), _sds((s_len, n_kv), F32),
            _sds((SWA_HEADS, w, 2 * w), F32), _sds((SWA_HEADS, w, 1), F32),
        ],
        compiler_params=_cparams(("arbitrary",)),
    )(swa2, swa2, swa2, bias, sinkb, cat, dcat, lse)


_MLA_SCALE = MLA_QK ** -0.5
_MLA_TQ = 256
_MLA_FWD_HEADS = 4
_MLA_BWD_HEADS = 2


def _mla_mask(i, tq, n_keys):
    row = i * tq + lax.broadcasted_iota(jnp.int32, (tq, n_keys), 0)
    col = lax.broadcasted_iota(jnp.int32, (tq, n_keys), 1)
    return col <= row


def _per_query_block(i, n_blocks, fn):
    for ii in range(n_blocks):
        pl.when(i == ii)(functools.partial(fn, ii))


def _mla_fwd(q4, k4, v4):
    s_len = q4.shape[1]
    tq = min(_MLA_TQ, s_len)
    pair = _MLA_FWD_HEADS

    def body(q_ref, k_ref, v_ref, o_ref, lse_ref):
        def block(ii):
            n_keys = (ii + 1) * tq
            mask = _mla_mask(ii, tq, n_keys)
            for hh in range(pair):
                s = jnp.where(mask, _dot(q_ref[hh], k_ref[hh, :n_keys, :], "nt") * _MLA_SCALE, _NEG)
                m = jnp.max(s, axis=-1, keepdims=True)
                p = jnp.exp(s - m)
                l = jnp.sum(p, axis=-1, keepdims=True)
                o_ref[:, hh * MLA_V:(hh + 1) * MLA_V] = (_dot(p, v_ref[hh, :n_keys, :], "nn") / l).astype(o_ref.dtype)
                lse_ref[hh] = m + jnp.log(l)

        _per_query_block(pl.program_id(1), s_len // tq, block)

    return pl.pallas_call(
        body,
        name="mla_fwd",
        grid=(MLA_HEADS // pair, s_len // tq),
        in_specs=[
            pl.BlockSpec((pair, tq, MLA_QK), lambda h, i: (h, i, 0)),
            pl.BlockSpec((pair, s_len, MLA_QK), lambda h, i: (h, 0, 0)),
            pl.BlockSpec((pair, s_len, MLA_V), lambda h, i: (h, 0, 0)),
        ],
        out_specs=[
            pl.BlockSpec((tq, pair * MLA_V), lambda h, i: (i, h)),
            pl.BlockSpec((pair, tq, 1), lambda h, i: (h, i, 0)),
        ],
        out_shape=[_sds((s_len, MLA_HEADS * MLA_V), BF16), _sds((MLA_HEADS, s_len, 1), F32)],
        compiler_params=_cparams(("parallel", "parallel")),
    )(q4, k4, v4)


def _mla_bwd(q4, k4, v4, cat, dcat, lse):
    s_len = q4.shape[1]
    tq = min(_MLA_TQ, s_len)
    pair = _MLA_BWD_HEADS
    first = SWA_HEADS * SWA_HEAD_DIM // (pair * MLA_V)

    def body(q_ref, k_ref, v_ref, o_ref, do_ref, lse_ref, dq_ref, dk_ref, dv_ref):
        i = pl.program_id(1)

        @pl.when(i == 0)
        def _():
            dk_ref[...] = jnp.zeros_like(dk_ref)
            dv_ref[...] = jnp.zeros_like(dv_ref)

        def block(ii):
            n_keys = (ii + 1) * tq
            mask = _mla_mask(ii, tq, n_keys)
            for hh in range(pair):
                cols = slice(hh * MLA_V, (hh + 1) * MLA_V)
                q, k, v, do = q_ref[hh], k_ref[hh, :n_keys, :], v_ref[hh, :n_keys, :], do_ref[:, cols]
                s = jnp.where(mask, _dot(q, k, "nt") * _MLA_SCALE, _NEG)
                p = jnp.where(mask, jnp.exp(s - lse_ref[hh]), 0.0)
                delta = jnp.sum(do.astype(F32) * o_ref[:, cols].astype(F32), axis=-1, keepdims=True)
                ds = (p * (_dot(do, v, "nt") - delta) * _MLA_SCALE).astype(BF16)
                dq_ref[hh] = _dot(ds, k, "nn")
                dk_ref[hh, :n_keys, :] += _dot(ds, q, "tn")
                dv_ref[hh, :n_keys, :] += _dot(p, do, "tn")

        _per_query_block(i, s_len // tq, block)

    return pl.pallas_call(
        body,
        name="mla_bwd",
        grid=(MLA_HEADS // pair, s_len // tq),
        in_specs=[
            pl.BlockSpec((pair, tq, MLA_QK), lambda h, i: (h, i, 0)),
            pl.BlockSpec((pair, s_len, MLA_QK), lambda h, i: (h, 0, 0)),
            pl.BlockSpec((pair, s_len, MLA_V), lambda h, i: (h, 0, 0)),
            pl.BlockSpec((tq, pair * MLA_V), lambda h, i: (i, first + h)),
            pl.BlockSpec((tq, pair * MLA_V), lambda h, i: (i, first + h)),
            pl.BlockSpec((pair, tq, 1), lambda h, i: (h, i, 0)),
        ],
        out_specs=[
            pl.BlockSpec((pair, tq, MLA_QK), lambda h, i: (h, i, 0)),
            pl.BlockSpec((pair, s_len, MLA_QK), lambda h, i: (h, 0, 0)),
            pl.BlockSpec((pair, s_len, MLA_V), lambda h, i: (h, 0, 0)),
        ],
        out_shape=[
            _sds((MLA_HEADS, s_len, MLA_QK), F32),
            _sds((MLA_HEADS, s_len, MLA_QK), F32),
            _sds((MLA_HEADS, s_len, MLA_V), F32),
        ],
        compiler_params=_cparams(("parallel", "arbitrary")),
    )(q4, k4, v4, cat, dcat, lse)


def _mla_split(pm):
    q_lat = pm[:, :MLA_Q_RANK]
    kv_lat = pm[:, MLA_Q_RANK:MLA_Q_RANK + MLA_KV_RANK]
    kr = pm[:, MLA_Q_RANK + MLA_KV_RANK:MLA_Q_RANK + MLA_KV_RANK + MLA_ROPE]
    kr_sw = pm[:, MLA_Q_RANK + MLA_KV_RANK + MLA_ROPE:]
    return q_lat, kv_lat, kr, kr_sw


def _mla_proj(pm, cos2, sin2, q_norm, kv_norm, wq_ext, wkv):
    s_len = pm.shape[0]

    def fn(pm_, cos_, sin_, qg, kvg, wq, wk):
        q_lat, kv_lat, kr, kr_sw = _mla_split(pm_)
        nq = (_rms(q_lat)[0] * qg).astype(BF16)
        nkv = (_rms(kv_lat)[0] * kvg).astype(BF16)
        k_rot = kr * cos_ + kr_sw * sin_
        qs, ks, vs = [], [], []
        for h in range(MLA_HEADS):
            qe = _dot(nq, wq[h], "nt")
            q_rot = qe[:, MLA_NOPE:MLA_QK] * cos_ + qe[:, MLA_QK:] * sin_
            qs.append(jnp.concatenate([qe[:, :MLA_NOPE], q_rot], axis=-1))
            kve = _dot(nkv, wk[h], "nt")
            ks.append(jnp.concatenate([kve[:, :MLA_NOPE], k_rot], axis=-1))
            vs.append(kve[:, MLA_NOPE:])
        return jnp.stack(qs), jnp.stack(ks), jnp.stack(vs)

    return _rowwise(
        "mla_proj", fn, [pm, cos2, sin2], [q_norm, kv_norm, wq_ext, wkv],
        [_sds((MLA_HEADS, s_len, MLA_QK), BF16), _sds((MLA_HEADS, s_len, MLA_QK), BF16),
         _sds((MLA_HEADS, s_len, MLA_V), BF16)], [], ELEM_TILE,
    )


def _mla_proj_bwd(pm, cos2, sin2, dq4, dk4, dv4, q_norm, kv_norm, wq_ext, wkv):
    s_len = pm.shape[0]

    def fn(pm_, cos_, sin_, dq, dk, dv, qg, kvg, wq, wk):
        q_lat, kv_lat, _, _ = _mla_split(pm_)
        xq, rq = _rms(q_lat)
        xkv, rkv = _rms(kv_lat)
        nq = (xq * qg).astype(BF16)
        nkv = (xkv * kvg).astype(BF16)
        dnq = jnp.zeros_like(q_lat)
        dnkv = jnp.zeros_like(kv_lat)
        dkr = jnp.zeros_like(cos_)
        dwq, dwk = [], []
        for h in range(MLA_HEADS):
            dy = dq[h][:, MLA_NOPE:]
            dqe = jnp.concatenate([dq[h][:, :MLA_NOPE], dy * cos_, dy * sin_], axis=-1).astype(BF16)
            dnq = dnq + _dot(dqe, wq[h], "nn")
            dwq.append(_dot(dqe, nq, "tn"))
            dkve = jnp.concatenate([dk[h][:, :MLA_NOPE], dv[h]], axis=-1).astype(BF16)
            dnkv = dnkv + _dot(dkve, wk[h], "nn")
            dwk.append(_dot(dkve, nkv, "tn"))
            dkr = dkr + dk[h][:, MLA_NOPE:]
        dq_lat = _rms_bwd(dnq * qg, xq, rq)
        dkv_lat = _rms_bwd(dnkv * kvg, xkv, rkv)
        dpm = jnp.concatenate([dq_lat, dkv_lat, dkr * cos_, dkr * sin_], axis=-1)
        return dpm, _sum0(dnq * xq), _sum0(dnkv * xkv), jnp.stack(dwq), jnp.stack(dwk)

    return _rowwise(
        "mla_proj_bwd", fn, [pm, cos2, sin2, dq4, dk4, dv4], [q_norm, kv_norm, wq_ext, wkv],
        [_sds((s_len, N_MLA_COLS), BF16)],
        [_sds((1, MLA_Q_RANK), F32), _sds((1, MLA_KV_RANK), F32),
         _sds(wq_ext.shape, F32), _sds(wkv.shape, F32)], ELEM_TILE,
    )


def _rope_tables(s_len):
    inv = ROPE_THETA ** (-jnp.arange(0, MLA_ROPE, 2, dtype=F32) / MLA_ROPE)
    ang = jnp.arange(s_len, dtype=F32)[:, None] * inv[None, :]
    cos, sin = jnp.cos(ang), jnp.sin(ang)
    return jnp.concatenate([cos, cos], axis=-1), jnp.concatenate([-sin, sin], axis=-1)


def _mix_weights(g_mix):
    rest = g_mix[:, 0]
    w_in_t = rest[:, O_IN:O_IN + R_IN].reshape(D_IN, D_MODEL)
    w_o = rest[:, O_O:O_O + R_O].reshape(D_MODEL, D_MODEL)
    w_uq_t = rest[:, O_UQ:O_UQ + R_UQ].reshape(MLA_HEADS, MLA_QK, MLA_Q_RANK)
    w_ukv_t = rest[:, O_UKV:O_UKV + R_UKV].reshape(MLA_HEADS, MLA_NOPE + MLA_V, MLA_KV_RANK)
    half = MLA_ROPE // 2
    w_swa = w_in_t[:N_SWA_COLS]
    w_mla = jnp.concatenate([w_in_t[N_SWA_COLS:], w_in_t[D_IN - half:], w_in_t[D_IN - MLA_ROPE:D_IN - half]], axis=0)
    wq_ext = jnp.concatenate([w_uq_t, w_uq_t[:, MLA_QK - half:], w_uq_t[:, MLA_NOPE:MLA_QK - half]], axis=1)
    return w_swa, w_mla, w_o, wq_ext, w_ukv_t


def _mix_fwd(x, h, gate, mix_src, q_norm, kv_norm, bias, sinkb, cos2, sin2, next_norm):
    s_len = x.shape[0]
    tm = min(ROW_TILE, s_len)
    deps = mix_src.mid(x)
    weights = _mix_weights(mix_src.get(h))
    w_swa, w_mla, w_o, wq_ext, wkv = weights
    swa2 = _mm(
        "mix_proj_swa", "nt", (s_len // tm, 1, 1),
        h, (tm, D_MODEL), lambda i, j, k: (i, 0),
        w_swa, (N_SWA_COLS, D_MODEL), lambda i, j, k: (0, 0),
        _sds((s_len, N_SWA_COLS), BF16), (tm, N_SWA_COLS), lambda i, j, k: (i, 0),
        (tm, N_SWA_COLS), deps=deps,
    )
    pm = _mm(
        "mix_proj_mla", "nt", (s_len // tm, 1, 1),
        h, (tm, D_MODEL), lambda i, j, k: (i, 0),
        w_mla, (N_MLA_COLS, D_MODEL), lambda i, j, k: (0, 0),
        _sds((s_len, N_MLA_COLS), F32), (tm, N_MLA_COLS), lambda i, j, k: (i, 0),
        (tm, N_MLA_COLS),
    )
    q4, k4, v4 = _mla_proj(pm, cos2, sin2, q_norm, kv_norm, wq_ext, wkv)
    oa, lse_a = _swa_fwd(swa2, bias, sinkb)
    ob, lse_b = _mla_fwd(q4, k4, v4)
    cat = jnp.concatenate([oa, ob], axis=1)
    z = _mm(
        "mix_out", "nn", (s_len // tm, 1, 1),
        cat, (tm, D_MODEL), lambda i, j, k: (i, 0),
        w_o, (D_MODEL, D_MODEL), lambda i, j, k: (0, 0),
        _sds((s_len, D_MODEL), F32), (tm, D_MODEL), lambda i, j, k: (i, 0),
        (tm, D_MODEL),
    )
    def residual_norm(x_, z_, g_, gamma_, sc_, sh_):
        x_out = x_ + g_ * z_
        return x_out, _normmod(x_out, gamma_, sc_, sh_)

    out = _rowwise(
        "mix_residual_norm", residual_norm, [x, z], [gate] + list(next_norm),
        [_sds((s_len, D_MODEL), F32), _sds((s_len, D_MODEL), BF16)], [], ELEM_TILE,
    )
    return out, (weights, h, swa2, pm, q4, k4, v4, cat, lse_a, lse_b, z)


def _mix_bwd(dxo, x, gamma, sc, gate, q_norm, kv_norm, bias, sinkb, cos2, sin2, saved, hooks):
    weights, h, swa2, pm, q4, k4, v4, cat, lse_a, lse_b, z = saved
    w_swa, w_mla, w_o, wq_ext, wkv = weights
    s_len = x.shape[0]
    tm = tk = min(ROW_TILE, s_len)
    vec = _sds((1, D_MODEL), F32)
    n_proj = N_SWA_COLS + N_MLA_COLS
    half_d = D_MODEL // 2

    dz, dgate = _rowwise(
        "mix_bwd_gate", lambda dxo_, z_, g_: (g_ * dxo_, _sum0(z_ * dxo_)),
        [dxo, z], [gate], [_sds((s_len, D_MODEL), BF16)], [vec], ELEM_TILE,
    )
    dw_o = _mm(
        "mix_bwd_wo", "tn", (2, 1, s_len // tk),
        cat, (tk, half_d), lambda i, j, k: (k, i),
        dz, (tk, D_MODEL), lambda i, j, k: (k, 0),
        _sds((D_MODEL, D_MODEL), F32), (half_d, D_MODEL), lambda i, j, k: (i, 0),
        (half_d, D_MODEL),
    )
    dcat = _mm(
        "mix_bwd_dcat", "nt", (s_len // tm, 1, 1),
        dz, (tm, D_MODEL), lambda i, j, k: (i, 0),
        w_o, (D_MODEL, D_MODEL), lambda i, j, k: (0, 0),
        _sds((s_len, D_MODEL), BF16), (tm, D_MODEL), lambda i, j, k: (i, 0),
        (tm, D_MODEL), deps=hooks.after_gate(dz),
    )
    dq_a, dkva, dkvb, dbias, dsink = _swa_bwd(swa2, bias, sinkb, cat, dcat, lse_a)
    dq4, dk4, dv4 = _mla_bwd(q4, k4, v4, cat, dcat, lse_b)
    dpm, dq_norm, dkv_norm, dwq_ext, dwkv = _mla_proj_bwd(pm, cos2, sin2, dq4, dk4, dv4, q_norm, kv_norm, wq_ext, wkv)

    dkv = dkva + jnp.concatenate([dkvb[WINDOW:], jnp.zeros_like(dkvb[:WINDOW])], axis=0)
    dproj = jnp.concatenate([dq_a, dkv.astype(BF16), dpm], axis=1)
    w_proj = jnp.concatenate([w_swa, w_mla], axis=0)

    dw_proj = _mm(
        "mix_bwd_win", "tn", (n_proj // 256, 1, s_len // tk),
        dproj, (tk, 256), lambda i, j, k: (k, i),
        h, (tk, D_MODEL), lambda i, j, k: (k, 0),
        _sds((n_proj, D_MODEL), F32), (256, D_MODEL), lambda i, j, k: (i, 0),
        (256, D_MODEL),
    )
    dw_swa, dw_mla = dw_proj[:N_SWA_COLS], dw_proj[N_SWA_COLS:]
    dh = _mm(
        "mix_bwd_dh", "nn", (s_len // tm, 1, 1),
        dproj, (tm, n_proj), lambda i, j, k: (i, 0),
        w_proj, (n_proj, D_MODEL), lambda i, j, k: (0, 0),
        _sds((s_len, D_MODEL), F32), (tm, D_MODEL), lambda i, j, k: (i, 0),
        (tm, D_MODEL),
    )
    dx, dsc, dsh, dgamma = _normmod_bwd_call("mix_bwd_normmod", [dh], x, dxo, gamma, sc)

    half = MLA_ROPE // 2
    n_mla_in = D_IN - N_SWA_COLS
    dw_mla_base = dw_mla[:n_mla_in]
    dw_mla_base = dw_mla_base.at[n_mla_in - half:].add(dw_mla[n_mla_in:n_mla_in + half])
    dw_mla_base = dw_mla_base.at[n_mla_in - MLA_ROPE:n_mla_in - half].add(dw_mla[n_mla_in + half:])
    dw_in_t = jnp.concatenate([dw_swa, dw_mla_base], axis=0)
    dw_uq_t = dwq_ext[:, :MLA_QK]
    dw_uq_t = dw_uq_t.at[:, MLA_QK - half:].add(dwq_ext[:, MLA_QK:MLA_QK + half])
    dw_uq_t = dw_uq_t.at[:, MLA_NOPE:MLA_QK - half].add(dwq_ext[:, MLA_QK + half:])
    rest = jnp.concatenate(
        [
            dw_in_t.reshape(N_CHIPS, R_IN, D_MODEL),
            dw_o.reshape(N_CHIPS, R_O, D_MODEL),
            dw_uq_t.reshape(N_CHIPS, R_UQ, D_MODEL),
            dwkv.reshape(N_CHIPS, R_UKV, D_MODEL),
            jnp.zeros((N_CHIPS, F_SHARD - O_PAD, D_MODEL), F32),
        ],
        axis=1,
    ).astype(BF16)
    return dx, rest, (dsh, dsc, dgate, dgamma), dq_norm, dkv_norm, dbias, dsink


def _device_step(x, target, mod, gu1_src, down1_src, mix_src, ffn2_src, norms, q_norm, kv_norm, sinks, rel_bias,
                 hooks2=None, hooks_mix=None, mix_done=None, hooks1=None):
    s_len = x.shape[0]
    sh1, sc1, g1, sh2, sc2, g2, sh3, sc3, g3 = [mod[:, i * D_MODEL:(i + 1) * D_MODEL] for i in range(N_MOD)]
    n1, n2, n3, nf = norms
    bkt = jnp.asarray(_bucket_map())
    bias = _bias_expand(rel_bias.T, bkt).reshape(SWA_HEADS, WINDOW, 2 * WINDOW)
    sinkb = jnp.broadcast_to(sinks.reshape(SWA_HEADS, 1, 1), (SWA_HEADS, WINDOW, 1))
    cos2, sin2 = _rope_tables(s_len)

    (x1, h2), saved1 = _ffn_fwd(
        "ffn1", x, n1, sh1, sc1, g1, gu1_src, 0, down1_src, 0, sh1, next_norm=(n2, sc2, sh2)
    )
    (x2, h3), saved2 = _mix_fwd(
        x1, h2, g2, mix_src, q_norm, kv_norm, bias, sinkb, cos2, sin2, next_norm=(n3, sc3, sh3)
    )
    (dx3, sq, dnf), saved3 = _ffn_fwd(
        "ffn2", x2, n3, sh3, sc3, g3, ffn2_src, 0, None, 2, x2, h_pre=h3, head=(target, nf)
    )
    loss_part = (0.5 / D_MODEL) * jnp.sum(sq)

    dx2, gb2, (dsh3, dsc3, dg3, dn3) = _ffn_bwd("ffn2", dx3, x2, n3, sc3, g3, saved3, hooks2 or _BwdHooks())
    dx1, rest, (dsh2, dsc2, dg2, dn2), dq_norm, dkv_norm, dbias, dsink = _mix_bwd(
        dx2, x1, n2, sc2, g2, q_norm, kv_norm, bias, sinkb, cos2, sin2, saved2, hooks_mix or _BwdHooks()
    )
    rest = rest[:, None]
    deps1 = mix_done(dx1, rest) if mix_done is not None else []
    dx0, gb1, (dsh1, dsc1, dg1, dn1) = _ffn_bwd(
        "ffn1", dx1, x, n1, sc1, g1, saved1, hooks1 or _BwdHooks(), deps=deps1
    )

    dmod = jnp.concatenate([dsh1, dsc1, dg1, dsh2, dsc2, dg2, dsh3, dsc3, dg3], axis=-1)
    drel = _bias_reduce(dbias.reshape(SWA_HEADS, -1), bkt).T
    dsinks = jnp.sum(dsink, axis=(1, 2)).reshape(1, SWA_HEADS)
    small = (dn1, dn2, dn3, dnf, dq_norm, dkv_norm, dsinks, drel)
    return loss_part, dx0, (gb1, gb2, rest), dmod, small


_MESH = pl.DeviceIdType.MESH


def _place():
    return lax.axis_index("x"), lax.axis_index("y"), lax.axis_index("c")


def _other_chips(x, y):
    return [(1 - x, y), (x, 1 - y), (1 - x, 1 - y)]


def _allgather_small(name, blk):
    m_per, n = blk.shape

    def body(x_ref, out_ref, send_sems, recv_sems, local_sem):
        x, y, c = _place()
        me, sibling = (x, y, c), (x, y, 1 - c)
        chips = _other_chips(x, y)

        def rows(px, py, pc):
            return out_ref.at[pl.ds((4 * px + 2 * py + pc) * m_per, m_per), :]

        def copy(k, block, to, src=None):
            return pltpu.make_async_remote_copy(
                src_ref=rows(*block) if src is None else src, dst_ref=rows(*block),
                send_sem=send_sems.at[k], recv_sem=recv_sems.at[k], device_id=to, device_id_type=_MESH,
            )

        mine = pltpu.make_async_copy(x_ref, rows(*me), local_sem)
        mine.start()
        first = [copy(0, me, sibling, src=x_ref)]
        first += [copy(1 + j, me, (*chip, c), src=x_ref) for j, chip in enumerate(chips)]
        for cp in first:
            cp.start()
        passed = [copy(4 + j, (*chip, c), sibling) for j, chip in enumerate(chips)]
        for j, chip in enumerate(chips):
            copy(1 + j, (*chip, c), me).wait_recv()
            passed[j].start()
        copy(0, sibling, me).wait_recv()
        for j, chip in enumerate(chips):
            copy(4 + j, (*chip, 1 - c), me).wait_recv()
        for cp in first + passed:
            cp.wait_send()
        mine.wait()

    return pl.pallas_call(
        body,
        name=name,
        out_shape=_sds((N_DEV * m_per, n), blk.dtype),
        in_specs=[pl.BlockSpec(memory_space=pltpu.VMEM)],
        out_specs=pl.BlockSpec(memory_space=pltpu.VMEM),
        scratch_shapes=[pltpu.SemaphoreType.DMA((7,)), pltpu.SemaphoreType.DMA((7,)), pltpu.SemaphoreType.DMA],
    )(blk)


def _gather_sends(n, own_ref, g_ref, send_sem, recv_sem, x, y, c, chip):
    return [
        pltpu.make_async_remote_copy(
            src_ref=own_ref.at[p, pl.ds(c * HALF, HALF), :], dst_ref=g_ref.at[2 * x + y, p, pl.ds(c * HALF, HALF), :],
            send_sem=send_sem, recv_sem=recv_sem, device_id=(*chip, c), device_id_type=_MESH,
        )
        for p in range(n)
    ]


def _gather_forwards(n, g_ref, send_sem, recv_sem, chip, c, sibling):
    return [
        pltpu.make_async_remote_copy(
            src_ref=g_ref.at[2 * chip[0] + chip[1], p, pl.ds(c * HALF, HALF), :],
            dst_ref=g_ref.at[2 * chip[0] + chip[1], p, pl.ds(c * HALF, HALF), :],
            send_sem=send_sem, recv_sem=recv_sem, device_id=sibling, device_id_type=_MESH,
        )
        for p in range(n)
    ]


def _gather_all(own_ref, g_ref, send_sem, recv_sem, chip, half, c):
    return pltpu.make_async_remote_copy(
        src_ref=own_ref.at[:, pl.ds(c * HALF, HALF), :],
        dst_ref=g_ref.at[2 * chip[0] + chip[1], :, pl.ds(half * HALF, HALF), :],
        send_sem=send_sem, recv_sem=recv_sem, device_id=(*chip, c), device_id_type=_MESH,
    )


_HBM_SPEC = pl.BlockSpec(memory_space=pltpu.HBM)
_SEM_SPEC = pl.BlockSpec(memory_space=pltpu.SEMAPHORE)
_ANY_SPEC = pl.BlockSpec(memory_space=pl.ANY)
_VMEM_SPEC = pl.BlockSpec(memory_space=pltpu.VMEM)
_SPLIT_PARAMS = pltpu.CompilerParams(has_side_effects=pltpu.SideEffectType.DATAFLOW_SIDE_EFFECTING)
_TOKEN = jax.ShapeDtypeStruct((8, 128), F32)


def _in_hbm(a):
    return pltpu.with_memory_space_constraint(a, pltpu.HBM)


class _GatherBehind:
    def __init__(self, tag, own, then=()):
        self.tag, self.n, self.own, self.then = tag, own.shape[0], own, tuple(then)

    def start(self, after):
        tag, own, n = self.tag, self.own, self.n
        x, y, _ = _place()
        g_init = lax.dynamic_update_slice(
            lax.empty((N_CHIPS,) + own.shape, own.dtype), own[None], (2 * x + y, 0, 0, 0)
        )

        def body(own_ref, g_ref, *rest):
            send_sems, recv_sems, _, _, token = rest[len(after):]
            x, y, c = _place()
            for j, chip in enumerate(_other_chips(x, y)):
                for cp in _gather_sends(n, own_ref, g_ref, send_sems.at[j], recv_sems.at[j], x, y, c, chip):
                    cp.start()
            token[...] = jnp.zeros_like(token)

        self.send1, self.recv1, self.own, self.g, self.token = pl.pallas_call(
            body,
            name=f"{tag}_start",
            out_shape=(
                pltpu.SemaphoreType.DMA((3,)), pltpu.SemaphoreType.DMA((3,)),
                pltpu.HBM(own.shape, own.dtype), pltpu.HBM(g_init.shape, g_init.dtype), _TOKEN,
            ),
            in_specs=(_HBM_SPEC, _HBM_SPEC) + (_ANY_SPEC,) * len(after),
            out_specs=(_SEM_SPEC, _SEM_SPEC, _HBM_SPEC, _HBM_SPEC, _VMEM_SPEC),
            input_output_aliases={0: 2, 1: 3},
            compiler_params=_SPLIT_PARAMS,
        )(_in_hbm(own), _in_hbm(g_init), *after)

    def mid(self, after):
        n = self.n

        def body(own_ref, g_ref, send1, recv1, after_ref, send2, recv2, g_out, token):
            x, y, c = _place()
            sibling = (x, y, 1 - c)
            chips = _other_chips(x, y)
            for j, chip in enumerate(chips):
                _gather_all(own_ref, g_ref, send1.at[j], recv1.at[j], chip, c, c).wait_recv()
                for cp in _gather_forwards(n, g_ref, send2.at[j], recv2.at[j], chip, c, sibling):
                    cp.start()
            for j, chip in enumerate(chips):
                _gather_all(own_ref, g_ref, send1.at[j], recv1.at[j], chip, c, c).wait_send()
            token[...] = jnp.zeros_like(token)

        self.send2, self.recv2, self.g, token = pl.pallas_call(
            body,
            name=f"{self.tag}_mid",
            out_shape=(
                pltpu.SemaphoreType.DMA((3,)), pltpu.SemaphoreType.DMA((3,)),
                pltpu.HBM(self.g.shape, self.g.dtype), _TOKEN,
            ),
            in_specs=(_HBM_SPEC, _HBM_SPEC, _SEM_SPEC, _SEM_SPEC, _ANY_SPEC),
            out_specs=(_SEM_SPEC, _SEM_SPEC, _HBM_SPEC, _VMEM_SPEC),
            input_output_aliases={1: 2},
            compiler_params=_SPLIT_PARAMS,
        )(self.own, self.g, self.send1, self.recv1, after)
        deps = [token]
        for nxt in self.then:
            nxt.start(deps[-1:])
            deps.append(nxt.token)
        return deps

    def get(self, after):
        def body(g_ref, send2, recv2, after_ref, g_out):
            x, y, c = _place()
            for j, chip in enumerate(_other_chips(x, y)):
                slot_in = g_ref.at[2 * chip[0] + chip[1], :, pl.ds((1 - c) * HALF, HALF), :]
                slot_out = g_ref.at[2 * chip[0] + chip[1], :, pl.ds(c * HALF, HALF), :]
                cp = pltpu.make_async_remote_copy(
                    src_ref=slot_out, dst_ref=slot_in, send_sem=send2.at[j], recv_sem=recv2.at[j],
                    device_id=(x, y, 1 - c), device_id_type=_MESH,
                )
                cp.wait_send()
                cp.wait_recv()

        return pl.pallas_call(
            body,
            name=f"{self.tag}_wait",
            out_shape=pltpu.HBM(self.g.shape, self.g.dtype),
            in_specs=(_HBM_SPEC, _SEM_SPEC, _SEM_SPEC, _ANY_SPEC),
            out_specs=_HBM_SPEC,
            input_output_aliases={0: 0},
            compiler_params=_SPLIT_PARAMS,
        )(self.g, self.send2, self.recv2, after)


def _pair_sum(name, gb, land):
    def body(c_ref, gb_ref, land_ref, o_ref):
        o_ref[...] = (gb_ref[...].astype(F32) + land_ref[...].astype(F32)).astype(o_ref.dtype)

    core = lax.axis_index("c").astype(jnp.int32).reshape(1)
    return pl.pallas_call(
        body,
        name=name,
        grid_spec=pltpu.PrefetchScalarGridSpec(
            num_scalar_prefetch=1,
            grid=(N_CHIPS, gb.shape[1]),
            in_specs=[
                pl.BlockSpec((None, None, HALF, D_MODEL), lambda j, p, c: (j, p, c[0], 0)),
                pl.BlockSpec((None, None, HALF, D_MODEL), lambda j, p, c: (j, p, 0, 0)),
            ],
            out_specs=pl.BlockSpec((None, None, HALF, D_MODEL), lambda j, p, c: (j, p, 0, 0)),
        ),
        out_shape=_sds(land.shape, BF16),
        compiler_params=_cparams(("parallel", "parallel")),
    )(core, gb, land)


def _chip_sum_share(name, land):
    n = land.shape[1]

    def body(l_ref, r_ref, buf, send_sems, recv_sems, local_sems):
        p = pl.program_id(0)
        x, y, c = _place()
        acc = l_ref[0].astype(F32)
        for j in range(1, N_CHIPS):
            acc = acc + l_ref[j].astype(F32)
        buf[p] = acc

        def copies(q):
            mine = r_ref.at[q, pl.ds(c * HALF, HALF), :]
            theirs = r_ref.at[q, pl.ds((1 - c) * HALF, HALF), :]
            local = pltpu.make_async_copy(buf.at[q], mine, local_sems.at[q])
            out = pltpu.make_async_remote_copy(
                src_ref=buf.at[q], dst_ref=mine, send_sem=send_sems.at[q], recv_sem=recv_sems.at[q],
                device_id=(x, y, 1 - c), device_id_type=_MESH,
            )
            arrive = pltpu.make_async_remote_copy(
                src_ref=buf.at[q], dst_ref=theirs, send_sem=send_sems.at[q], recv_sem=recv_sems.at[q],
                device_id=(x, y, 1 - c), device_id_type=_MESH,
            )
            return local, out, arrive

        local, out, _ = copies(p)
        local.start()
        out.start()

        @pl.when(p == n - 1)
        def _():
            for q in range(n):
                local, out, arrive = copies(q)
                arrive.wait_recv()
                out.wait_send()
                local.wait()

    return pl.pallas_call(
        body,
        name=name,
        grid=(n,),
        in_specs=[pl.BlockSpec((N_CHIPS, None, HALF, D_MODEL), lambda p: (0, p, 0, 0))],
        out_specs=pl.BlockSpec(memory_space=pl.ANY),
        out_shape=_sds((n, F_SHARD, D_MODEL), F32),
        scratch_shapes=[
            pltpu.VMEM((n, HALF, D_MODEL), F32),
            pltpu.SemaphoreType.DMA((n,)), pltpu.SemaphoreType.DMA((n,)), pltpu.SemaphoreType.DMA((n,)),
        ],
        compiler_params=_cparams(("arbitrary",)),
    )(land)


class _ReduceBehind:
    def __init__(self, tag, gb, after=()):
        self.tag = tag
        n = gb.shape[1]
        land = lax.empty((N_CHIPS, n, HALF, D_MODEL), gb.dtype)

        def body(gb_ref, land_ref, *rest):
            send_sem, recv_sem, _, _, token = rest[len(after):]
            self._pair_copy(gb_ref, land_ref, send_sem, recv_sem).start()
            token[...] = jnp.zeros_like(token)

        self.send, self.recv, self.gb, self.land, self.token = pl.pallas_call(
            body,
            name=f"grads_pair_start_{tag}",
            out_shape=(
                pltpu.SemaphoreType.DMA((1,)), pltpu.SemaphoreType.DMA((1,)),
                pltpu.HBM(gb.shape, gb.dtype), pltpu.HBM(land.shape, land.dtype), _TOKEN,
            ),
            in_specs=(_HBM_SPEC, _HBM_SPEC) + (_ANY_SPEC,) * len(after),
            out_specs=(_SEM_SPEC, _SEM_SPEC, _HBM_SPEC, _HBM_SPEC, _VMEM_SPEC),
            input_output_aliases={0: 2, 1: 3},
            compiler_params=_SPLIT_PARAMS,
        )(_in_hbm(gb), _in_hbm(land), *after)

    @staticmethod
    def _pair_copy(gb_ref, land_ref, send_sem, recv_sem):
        x, y, c = _place()
        return pltpu.make_async_remote_copy(
            src_ref=gb_ref.at[:, :, pl.ds((1 - c) * HALF, HALF), :], dst_ref=land_ref,
            send_sem=send_sem.at[0], recv_sem=recv_sem.at[0], device_id=(x, y, 1 - c), device_id_type=_MESH,
        )

    @staticmethod
    def _chip_copies(p_ref, land_ref, send_sems, recv_sems):
        x, y, c = _place()
        me = 2 * x + y
        sends, arrivals = [], []
        for k, chip in enumerate(_other_chips(x, y)):
            them = 2 * chip[0] + chip[1]
            sends.append(pltpu.make_async_remote_copy(
                src_ref=p_ref.at[them], dst_ref=land_ref.at[me], send_sem=send_sems.at[k], recv_sem=recv_sems.at[k],
                device_id=(*chip, c), device_id_type=_MESH,
            ))
            arrivals.append(pltpu.make_async_remote_copy(
                src_ref=p_ref.at[me], dst_ref=land_ref.at[them], send_sem=send_sems.at[k], recv_sem=recv_sems.at[k],
                device_id=(*chip, c), device_id_type=_MESH,
            ))
        return sends, arrivals

    def mid(self, after):
        tag = self.tag

        def wait_body(gb_ref, land_ref, send_sem, recv_sem, after_ref, gb_out, land_out):
            cp = self._pair_copy(gb_ref, land_ref, send_sem, recv_sem)
            cp.wait_send()
            cp.wait_recv()

        gb, land = pl.pallas_call(
            wait_body,
            name=f"grads_pair_wait_{tag}",
            out_shape=(pltpu.HBM(self.gb.shape, self.gb.dtype), pltpu.HBM(self.land.shape, self.land.dtype)),
            in_specs=(_HBM_SPEC, _HBM_SPEC, _SEM_SPEC, _SEM_SPEC, _ANY_SPEC),
            out_specs=(_HBM_SPEC, _HBM_SPEC),
            input_output_aliases={0: 0, 1: 1},
            compiler_params=_SPLIT_PARAMS,
        )(self.gb, self.land, self.send, self.recv, after)
        part = _pair_sum(f"grads_pair_sum_{tag}", gb, land)

        x, y, _ = _place()
        start = (2 * x + y, 0, 0, 0)
        own = lax.dynamic_slice(part, start, (1,) + part.shape[1:])
        land2 = lax.dynamic_update_slice(lax.empty(part.shape, part.dtype), own, start)

        def start_body(p_ref, land_ref, send_sems, recv_sems, p_out, land_out, token):
            for cp in self._chip_copies(p_ref, land_ref, send_sems, recv_sems)[0]:
                cp.start()
            token[...] = jnp.zeros_like(token)

        self.send2, self.recv2, self.part, self.land2, token = pl.pallas_call(
            start_body,
            name=f"grads_chip_start_{tag}",
            out_shape=(
                pltpu.SemaphoreType.DMA((3,)), pltpu.SemaphoreType.DMA((3,)),
                pltpu.HBM(part.shape, part.dtype), pltpu.HBM(land2.shape, land2.dtype), _TOKEN,
            ),
            in_specs=(_HBM_SPEC, _HBM_SPEC),
            out_specs=(_SEM_SPEC, _SEM_SPEC, _HBM_SPEC, _HBM_SPEC, _VMEM_SPEC),
            input_output_aliases={0: 2, 1: 3},
            compiler_params=_SPLIT_PARAMS,
        )(_in_hbm(part), _in_hbm(land2))
        return [token]

    def get(self, after):
        n_after = len(after)

        def wait_body(p_ref, land_ref, send_sems, recv_sems, *rest):
            sends, arrivals = self._chip_copies(p_ref, land_ref, send_sems, recv_sems)
            for cp in arrivals:
                cp.wait_recv()
            for cp in sends:
                cp.wait_send()

        _, land2 = pl.pallas_call(
            wait_body,
            name=f"grads_chip_wait_{self.tag}",
            out_shape=(pltpu.HBM(self.part.shape, self.part.dtype), pltpu.HBM(self.land2.shape, self.land2.dtype)),
            in_specs=(_HBM_SPEC, _HBM_SPEC, _SEM_SPEC, _SEM_SPEC) + (_ANY_SPEC,) * n_after,
            out_specs=(_HBM_SPEC, _HBM_SPEC),
            input_output_aliases={0: 0, 1: 1},
            compiler_params=_SPLIT_PARAMS,
        )(self.part, self.land2, self.send2, self.recv2, *after)
        return _chip_sum_share(f"grads_chip_sum_share_{self.tag}", land2)


def _allreduce_small(blk):
    gathered = _allgather_small("allgather_small_grads", blk).reshape(N_DEV, PK_ROWS, 128)

    def body(g_ref, o_ref):
        acc = g_ref[0]
        for k in range(1, N_DEV):
            acc = acc + g_ref[k]
        o_ref[...] = acc

    total = pl.pallas_call(body, name="small_grads_sum", out_shape=_sds((PK_ROWS, 128), F32))(gathered)
    return gathered, total


def _adamw_math(w_, g_, m_, v_):
    m_new = ADAM_B1 * m_ + (1.0 - ADAM_B1) * g_
    v_new = ADAM_B2 * v_ + (1.0 - ADAM_B2) * (g_ * g_)
    m_hat = m_new / (1.0 - ADAM_B1 ** ADAM_STEP)
    v_hat = v_new / (1.0 - ADAM_B2 ** ADAM_STEP)
    delta = -ADAM_LR * (m_hat / (jnp.sqrt(v_hat) + ADAM_EPS) + ADAM_WD * w_)
    return delta, m_new, v_new


def _adamw(name, w, g, m, v):
    rows, cols = w.shape
    tm = rows
    while tm * cols * 4 > (1 << 21) and tm % 16 == 0:
        tm //= 2
    out = _sds(w.shape, F32)
    return _rowwise(name, _adamw_math, [w, g, m, v], [], [out, out, out], [], tm)


def _adamw_small(ws, gs, ms, vs):
    n = len(ws)
    shapes = [w.shape for w in ws]
    as2d = lambda a: a.reshape(1, -1) if a.ndim == 1 else a

    def body(*refs):
        ins, outs = refs[:4 * n], refs[4 * n:]
        for k in range(n):
            res = _adamw_math(*[ins[j * n + k][...] for j in range(4)])
            for j in range(3):
                outs[j * n + k][...] = res[j]

    args = [as2d(a) for group in (ws, gs, ms, vs) for a in group]
    out = pl.pallas_call(
        body, name="adamw_small", out_shape=[_sds(as2d(w).shape, F32) for w in ws] * 3, compiler_params=_cparams()
    )(*args)
    back = [o.reshape(shapes[i % n]) for i, o in enumerate(out)]
    return back[:n], back[n:2 * n], back[2 * n:]


def _pack_small(b_mod_like, n1, n2, n3, nf, qn, kvn, sinks, rel):
    parts = [
        b_mod_like.reshape(PK_MOD, 128),
        n1.reshape(8, 128), n2.reshape(8, 128), n3.reshape(8, 128), nf.reshape(8, 128),
        qn.reshape(2, 128), kvn.reshape(1, 128),
        jnp.pad(sinks.reshape(1, SWA_HEADS), ((0, 0), (0, 128 - SWA_HEADS))),
        rel.reshape(2, 128),
        jnp.zeros((2, 128), F32),
    ]
    return jnp.concatenate(parts, axis=0)


def _unpack_small(p):
    o = PK_MOD
    return (
        p[:o].reshape(1, N_MOD * D_MODEL),
        p[o:o + 8].reshape(1, D_MODEL), p[o + 8:o + 16].reshape(1, D_MODEL),
        p[o + 16:o + 24].reshape(1, D_MODEL), p[o + 24:o + 32].reshape(D_MODEL),
        p[o + 32:o + 34].reshape(1, MLA_Q_RANK), p[o + 34:o + 35].reshape(1, MLA_KV_RANK),
        p[o + 35:o + 36, :SWA_HEADS].reshape(1, SWA_HEADS),
        p[o + 36:o + 38].reshape(NUM_BUCKETS, SWA_HEADS),
    )


def kernel(x, c, w_mod, b_mod, norm_ffn1, ffn1_gate, ffn1_up, ffn1_down, norm_mix, w_in, q_norm, kv_norm, w_uq, w_ukv, sinks, w_o, norm_ffn2, ffn2_gate, ffn2_up, ffn2_down, rel_bias, norm_final, loss_target, m_w_mod, m_b_mod, m_norm_ffn1, m_ffn1_gate, m_ffn1_up, m_ffn1_down, m_norm_mix, m_w_in, m_q_norm, m_kv_norm, m_w_uq, m_w_ukv, m_sinks, m_w_o, m_norm_ffn2, m_ffn2_gate, m_ffn2_up, m_ffn2_down, m_rel_bias, m_norm_final, v_w_mod, v_b_mod, v_norm_ffn1, v_ffn1_gate, v_ffn1_up, v_ffn1_down, v_norm_mix, v_w_in, v_q_norm, v_kv_norm, v_w_uq, v_w_ukv, v_sinks, v_w_o, v_norm_ffn2, v_ffn2_gate, v_ffn2_up, v_ffn2_down, v_rel_bias, v_norm_final):
    ax, ay, ac = _place()
    chip = 2 * ax + ay
    dev = 2 * chip + ac
    n_mod_shard = N_MOD * D_MODEL // N_CHIPS

    def t16(w):
        return w[0].T.astype(BF16)

    rest = jnp.concatenate(
        [
            t16(w_in),
            w_o[0].astype(BF16),
            t16(w_uq).reshape(R_UQ, D_MODEL),
            t16(w_ukv).reshape(R_UKV, D_MODEL),
            jnp.zeros((F_SHARD - O_PAD, D_MODEL), BF16),
        ],
        axis=0,
    )
    own_gu1 = jnp.stack([t16(ffn1_gate), t16(ffn1_up)])
    own_down1 = ffn1_down[0].astype(BF16)[None]
    own_mix = rest[None]
    own_ffn2 = jnp.stack([t16(ffn2_gate), t16(ffn2_up), ffn2_down[0].astype(BF16)])

    (c_act,) = _rowwise(
        "silu_c", lambda v: v * _sigmoid(v), [c.reshape(8, 128)], [], [_sds((8, 128), F32)], [], 8
    )
    c_all = _allgather_small("allgather_c", c_act).reshape(N_DEV, D_MODEL)
    mod_cols = _mm(
        "mod_fwd", "nn", (1, n_mod_shard // MOD_TILE, 1),
        c_all, (N_DEV, D_MODEL), lambda i, j, k: (0, 0),
        w_mod[0], (D_MODEL, MOD_TILE), lambda i, j, k: (0, j),
        _sds((N_DEV, n_mod_shard), F32), (N_DEV, MOD_TILE), lambda i, j, k: (0, j),
        (N_DEV, MOD_TILE),
    )
    mod_all = _allgather_small("allgather_mod", mod_cols).reshape(N_CHIPS, 2, N_DEV, n_mod_shard)[:, 0]
    mod_all = mod_all.transpose(1, 0, 2).reshape(N_DEV, N_MOD * D_MODEL)
    mod = lax.dynamic_slice_in_dim(mod_all, dev, 1, axis=0) + b_mod

    norms = (norm_ffn1, norm_mix, norm_ffn2, norm_final.reshape(1, D_MODEL))

    ffn2_src = _GatherBehind("gather_ffn2", own_ffn2)
    mix_src = _GatherBehind("gather_mix", own_mix)
    down1_src = _GatherBehind("gather_down1", own_down1, then=[mix_src, ffn2_src])
    gu1_src = _GatherBehind("gather_gu1", own_gu1, then=[down1_src])
    gu1_src.start([mod_all])

    reducing, red = {}, {}

    def begin(tag, gb, after):
        reducing[tag] = _ReduceBehind(tag, gb, after=after)
        return [reducing[tag].token]

    def ffn2_gu_done(gb):
        return begin("ffn2_gu", gb, reducing["ffn2_down"].mid(gb))

    def mix_done(dx1, gb_rest):
        red["ffn2_down"] = reducing["ffn2_down"].get([dx1])
        red["ffn2_gu"] = reducing["ffn2_gu"].get([red["ffn2_down"]])
        return begin("rest", gb_rest, [red["ffn2_gu"]])

    def ffn1_gu_done(gb):
        red["rest"] = reducing["rest"].get([gb])
        begin("ffn1_gu", gb, reducing["ffn1_down"].mid(red["rest"]))
        return reducing["ffn1_gu"].mid(reducing["ffn1_gu"].token)

    loss_part, grad_x, _, dmod, small = _device_step(
        x[0], loss_target[0], mod, gu1_src, down1_src, mix_src, ffn2_src, norms, q_norm, kv_norm, sinks, rel_bias,
        hooks2=_BwdHooks(after_wdown=lambda gb: begin("ffn2_down", gb, ()), after_wgu=ffn2_gu_done),
        hooks_mix=_BwdHooks(after_gate=lambda dz: reducing["ffn2_gu"].mid(dz)),
        mix_done=mix_done,
        hooks1=_BwdHooks(
            after_swiglu=lambda dab3: reducing["rest"].mid(dab3),
            after_wdown=lambda gb: begin("ffn1_down", gb, ()),
            after_wgu=ffn1_gu_done,
        ),
    )
    loss = lax.psum(loss_part, ("x", "y", "c"))

    gathered, total = _allreduce_small(_pack_small(dmod, *small))
    (g_b_mod, g_n1, g_n2, g_n3, g_nf, g_qn, g_kvn, g_sinks, g_rel) = _unpack_small(total)
    dmod_all = gathered[:, :PK_MOD].reshape(N_DEV, N_MOD * D_MODEL)
    dmod_cols = lax.dynamic_slice_in_dim(dmod_all, chip * n_mod_shard, n_mod_shard, axis=1)
    g_w_mod = _mm(
        "mod_bwd", "tn", (1, n_mod_shard // MOD_TILE, 1),
        c_all, (N_DEV, D_MODEL), lambda i, j, k: (0, 0),
        dmod_cols, (N_DEV, MOD_TILE), lambda i, j, k: (0, j),
        _sds((D_MODEL, n_mod_shard), F32), (D_MODEL, MOD_TILE), lambda i, j, k: (0, j),
        (D_MODEL, MOD_TILE),
    )

    r_rest = red["rest"][0]
    transposed = {"ffn1_gate", "ffn1_up", "ffn2_gate", "ffn2_up", "w_in", "w_uq", "w_ukv"}
    g_big = {
        "ffn2_gate": red["ffn2_gu"][0], "ffn2_up": red["ffn2_gu"][1], "ffn2_down": red["ffn2_down"][0],
        "w_in": r_rest[O_IN:O_IN + R_IN],
        "w_o": r_rest[O_O:O_O + R_O],
        "w_uq": r_rest[O_UQ:O_UQ + R_UQ].reshape(MLA_QK, MLA_Q_RANK),
        "w_ukv": r_rest[O_UKV:O_UKV + R_UKV].reshape(MLA_NOPE + MLA_V, MLA_KV_RANK),
        "w_mod": g_w_mod,
    }
    w_big = {
        "ffn2_gate": (ffn2_gate, m_ffn2_gate, v_ffn2_gate),
        "ffn2_up": (ffn2_up, m_ffn2_up, v_ffn2_up), "ffn2_down": (ffn2_down, m_ffn2_down, v_ffn2_down),
        "w_in": (w_in, m_w_in, v_w_in), "w_o": (w_o, m_w_o, v_w_o), "w_uq": (w_uq, m_w_uq, v_w_uq),
        "w_ukv": (w_ukv, m_w_ukv, v_w_ukv), "w_mod": (w_mod, m_w_mod, v_w_mod),
    }
    grads, deltas, new_m, new_v = {}, {}, {}, {}

    def update(names):
        for nm in names:
            w, m, v = w_big[nm]
            if nm in transposed:
                outs = _adamw(f"adamw_{nm}", w[0].T, g_big[nm], m[0].T, v[0].T)
                g_, d_, m_, v_ = [a.T for a in (g_big[nm],) + tuple(outs)]
            else:
                g_, (d_, m_, v_) = g_big[nm], _adamw(f"adamw_{nm}", w[0], g_big[nm], m[0], v[0])
            grads[nm], deltas[nm], new_m[nm], new_v[nm] = g_[None], d_[None], m_[None], v_[None]

    update(list(w_big))
    red1_down = reducing["ffn1_down"].get([deltas[nm] for nm in w_big])
    red1_gu = reducing["ffn1_gu"].get([red1_down])
    g_big.update({"ffn1_gate": red1_gu[0], "ffn1_up": red1_gu[1], "ffn1_down": red1_down[0]})
    w_big.update({
        "ffn1_gate": (ffn1_gate, m_ffn1_gate, v_ffn1_gate), "ffn1_up": (ffn1_up, m_ffn1_up, v_ffn1_up),
        "ffn1_down": (ffn1_down, m_ffn1_down, v_ffn1_down),
    })
    update(["ffn1_gate", "ffn1_up", "ffn1_down"])

    small_names = ["b_mod", "norm_ffn1", "norm_mix", "norm_ffn2", "norm_final", "q_norm", "kv_norm", "sinks", "rel_bias"]
    w_small = (b_mod, norm_ffn1, norm_mix, norm_ffn2, norm_final, q_norm, kv_norm, sinks, rel_bias)
    m_small = (m_b_mod, m_norm_ffn1, m_norm_mix, m_norm_ffn2, m_norm_final, m_q_norm, m_kv_norm, m_sinks, m_rel_bias)
    v_small = (v_b_mod, v_norm_ffn1, v_norm_mix, v_norm_ffn2, v_norm_final, v_q_norm, v_kv_norm, v_sinks, v_rel_bias)
    g_small = (g_b_mod, g_n1, g_n2, g_n3, g_nf, g_qn, g_kvn, g_sinks, g_rel)
    d_s, m_s, v_s = _adamw_small(w_small, g_small, m_small, v_small)
    for nm, g_, d_, m_, v_ in zip(small_names, g_small, d_s, m_s, v_s):
        grads[nm], deltas[nm], new_m[nm], new_v[nm] = g_, d_, m_, v_

    order = ["w_mod", "b_mod", "norm_ffn1", "ffn1_gate", "ffn1_up", "ffn1_down", "norm_mix", "w_in", "q_norm", "kv_norm",
             "w_uq", "w_ukv", "sinks", "w_o", "norm_ffn2", "ffn2_gate", "ffn2_up", "ffn2_down", "rel_bias", "norm_final"]
    return (loss, grad_x[None], *[grads[n] for n in order], *[deltas[n] for n in order],
            *[new_m[n] for n in order], *[new_v[n] for n in order])
```

```python
import functools
import math

import jax
import jax.numpy as jnp
import numpy as np
from jax import lax
from jax.experimental import pallas as pl
from jax.experimental.pallas import tpu as pltpu

F32, BF16 = jnp.float32, jnp.bfloat16

D_MODEL = 1024
D_FF = 2816
EPS = 1e-6
N_MOD = 9
SWA_HEADS, SWA_KV_HEADS, SWA_HEAD_DIM, WINDOW = 8, 2, 64, 128
SWA_GROUP = SWA_HEADS // SWA_KV_HEADS
MLA_HEADS, MLA_Q_RANK, MLA_KV_RANK, MLA_NOPE, MLA_ROPE, MLA_V = 4, 256, 128, 128, 64, 128
MLA_QK = MLA_NOPE + MLA_ROPE
ROPE_THETA = 10000.0
NUM_BUCKETS, MAX_DISTANCE = 32, 128
D_IN = 1216
N_SWA_COLS = 768
N_MLA_COLS = 512

ADAM_LR, ADAM_B1, ADAM_B2, ADAM_EPS, ADAM_WD, ADAM_STEP = 0.001, 0.9, 0.999, 1e-08, 0.01, 10

N_CHIPS = 4
N_DEV = 8
F_SHARD = D_FF // N_CHIPS
HALF = F_SHARD // 2
R_IN, R_O, R_UQ, R_UKV = 304, 256, 48, 32
O_IN, O_O, O_UQ, O_UKV, O_PAD = 0, 304, 560, 608, 640

PK_MOD = 72
PK_ROWS = 112
VMEM_LIMIT = 56 << 20
ROW_TILE = 2048
ELEM_TILE = 512
MOD_TILE = 768

_DN = {
    "nn": (((1,), (0,)), ((), ())),
    "nt": (((1,), (1,)), ((), ())),
    "tn": (((0,), (0,)), ((), ())),
}


def _sds(shape, dtype):
    return jax.ShapeDtypeStruct(tuple(shape), dtype)


def _cparams(sem=None):
    kw = {"vmem_limit_bytes": VMEM_LIMIT}
    if sem is not None:
        kw["dimension_semantics"] = sem
    return pltpu.CompilerParams(**kw)


def _dot(a, b, dims):
    return lax.dot_general(a.astype(BF16), b.astype(BF16), _DN[dims], preferred_element_type=F32)


def _mm(name, dims, grid, a, a_blk, a_idx, b, b_blk, b_idx, out, out_blk, out_idx, acc_shape, alias=None, deps=()):
    nk = grid[2]

    def body(*refs):
        a_ref, b_ref = refs[:2]
        o_ref, acc_ref = refs[-2:]
        part = _dot(a_ref[...], b_ref[...], dims)
        if nk == 1:
            o_ref[...] = part.astype(o_ref.dtype)
            return
        k = pl.program_id(2)

        @pl.when(k == 0)
        def _():
            acc_ref[...] = part

        @pl.when((k > 0) & (k < nk - 1))
        def _():
            acc_ref[...] += part

        @pl.when(k == nk - 1)
        def _():
            o_ref[...] = (acc_ref[...] + part).astype(o_ref.dtype)

    in_specs = [pl.BlockSpec(a_blk, a_idx), pl.BlockSpec(b_blk, b_idx)]
    args = [a, b]
    kw = {}
    if alias is not None:
        in_specs.append(pl.BlockSpec(memory_space=pl.ANY))
        args.append(alias)
        kw["input_output_aliases"] = {2: 0}
    in_specs += [pl.BlockSpec(memory_space=pl.ANY)] * len(deps)
    args += list(deps)
    return pl.pallas_call(
        body,
        name=name,
        grid=grid,
        in_specs=in_specs,
        out_specs=pl.BlockSpec(out_blk, out_idx),
        out_shape=out,
        scratch_shapes=[pltpu.VMEM(acc_shape if nk > 1 else (8, 128), F32)],
        compiler_params=_cparams(("parallel", "parallel", "arbitrary")),
        **kw,
    )(*args)


def _rowwise(name, fn, tiled, full, out_tiled, out_red, tm, deps=()):
    rows = tiled[0].shape[-2]
    tm = min(tm, rows)
    grid = (rows // tm,)
    n_in = len(tiled) + len(full)
    n_t = len(out_tiled)
    n_dep = len(deps)

    def tspec(shape):
        nd = len(shape)
        return pl.BlockSpec(tuple(shape[:-2]) + (tm, shape[-1]), lambda i, nd=nd: (0,) * (nd - 2) + (i, 0))

    def fspec(shape):
        nd = len(shape)
        return pl.BlockSpec(tuple(shape), lambda i, nd=nd: (0,) * nd)

    def body(*refs):
        outs = fn(*[r[...] for r in refs[:n_in]])
        if not isinstance(outs, (tuple, list)):
            outs = (outs,)
        out_refs = refs[n_in + n_dep:]
        for r, v in zip(out_refs[:n_t], outs[:n_t]):
            r[...] = v.astype(r.dtype)
        red_refs = out_refs[n_t:]
        if red_refs:
            @pl.when(pl.program_id(0) == 0)
            def _():
                for r in red_refs:
                    r[...] = jnp.zeros_like(r)

            for r, v in zip(red_refs, outs[n_t:]):
                r[...] += v.astype(r.dtype)

    res = pl.pallas_call(
        body,
        name=name,
        grid=grid,
        in_specs=[tspec(a.shape) for a in tiled] + [fspec(a.shape) for a in full]
        + [pl.BlockSpec(memory_space=pl.ANY)] * n_dep,
        out_specs=[tspec(o.shape) for o in out_tiled] + [fspec(o.shape) for o in out_red],
        out_shape=list(out_tiled) + list(out_red),
        compiler_params=_cparams(("arbitrary",) if out_red else ("parallel",)),
    )(*tiled, *full, *deps)
    return res


def _sum0(v):
    return jnp.sum(v, axis=0, keepdims=True)


def _sigmoid(v):
    return 1.0 / (1.0 + jnp.exp(-v))


def _rms(x):
    r = lax.rsqrt(jnp.mean(x * x, axis=-1, keepdims=True) + EPS)
    return x * r, r


def _rms_bwd(u, xr, r):
    return r * (u - xr * jnp.mean(u * xr, axis=-1, keepdims=True))


def _normmod(x_, gamma_, sc_, sh_):
    return _rms(x_)[0] * gamma_ * (1.0 + sc_) + sh_


def _row_blocks(tm, size=512):
    size = min(size, tm)
    return [slice(r, r + size) for r in range(0, tm, size)]


def _loss_head(x_, t_, g_):
    xr, r = _rms(x_)
    err = xr * g_ - t_
    dy = err * (1.0 / D_MODEL)
    return _rms_bwd(dy * g_, xr, r), _sum0(err * err), _sum0(dy * xr)


def _ffn_fwd(tag, x, gamma, sh, sc, gate, gu_src, base, down_src, down_idx, mid_after, h_pre=None,
             next_norm=None, head=None):
    s_len = x.shape[0]
    if h_pre is None:
        (h,) = _rowwise(
            f"{tag}_normmod", _normmod, [x], [gamma, sc, sh], [_sds((s_len, D_MODEL), BF16)], [], ELEM_TILE
        )
        gdeps = gu_src.mid(h)
    else:
        h, gdeps = h_pre, gu_src.mid(mid_after)
    g4 = gu_src.get(h)

    tm = min(ROW_TILE, s_len)
    def gate_up(h_ref, wg_ref, wu_ref, *rest):
        ab_ref, s_ref = rest[-2:]
        wg, wu = wg_ref[...], wu_ref[...]
        for rows in _row_blocks(tm):
            hv = h_ref[rows, :]
            a = _dot(hv, wg, "nt")
            b = _dot(hv, wu, "nt")
            ab_ref[0, rows, :] = a.astype(ab_ref.dtype)
            ab_ref[1, rows, :] = b.astype(ab_ref.dtype)
            s_ref[rows, :] = (a * _sigmoid(a) * b).astype(s_ref.dtype)

    ab4, s3 = pl.pallas_call(
        gate_up,
        name=f"{tag}_gate_up",
        grid=(s_len // tm, N_CHIPS),
        in_specs=[
            pl.BlockSpec((tm, D_MODEL), lambda i, c: (i, 0)),
            pl.BlockSpec((None, None, F_SHARD, D_MODEL), lambda i, c: (c, base, 0, 0)),
            pl.BlockSpec((None, None, F_SHARD, D_MODEL), lambda i, c: (c, base + 1, 0, 0)),
        ] + [pl.BlockSpec(memory_space=pl.ANY)] * len(gdeps),
        out_specs=[
            pl.BlockSpec((None, 2, tm, F_SHARD), lambda i, c: (c, 0, i, 0)),
            pl.BlockSpec((None, tm, F_SHARD), lambda i, c: (c, i, 0)),
        ],
        out_shape=[_sds((N_CHIPS, 2, s_len, F_SHARD), BF16), _sds((N_CHIPS, s_len, F_SHARD), BF16)],
        compiler_params=_cparams(("parallel", "parallel")),
    )(h, g4, g4, *gdeps)
    if down_src is None:
        g_down, ddeps = g4, ()
    else:
        ddeps = down_src.mid(ab4)
        g_down = down_src.get(s3)

    y = _mm(
        f"{tag}_down", "nn", (s_len // tm, 1, N_CHIPS),
        s3, (None, tm, F_SHARD), lambda i, j, k: (k, i, 0),
        g_down, (None, None, F_SHARD, D_MODEL), lambda i, j, k: (k, down_idx, 0, 0),
        _sds((s_len, D_MODEL), F32), (tm, D_MODEL), lambda i, j, k: (i, 0),
        (tm, D_MODEL), deps=ddeps,
    )

    saved = (g4, base, g_down, down_idx, h, ab4, s3, y)
    act = _sds((s_len, D_MODEL), F32)
    if head is not None:
        target, norm_final = head
        vec = _sds((1, D_MODEL), F32)
        out = _rowwise(
            f"{tag}_residual_head", lambda x_, y_, t_, g_, nf_: _loss_head(x_ + 0.5 * g_ * y_, t_, nf_),
            [x, y, target], [gate, norm_final], [act], [vec, vec], ELEM_TILE,
        )
    elif next_norm is not None:
        def residual_norm(x_, y_, g_, gamma_, sc_, sh_):
            x_out = x_ + 0.5 * g_ * y_
            return x_out, _normmod(x_out, gamma_, sc_, sh_)

        out = _rowwise(
            f"{tag}_residual_norm", residual_norm, [x, y], [gate] + list(next_norm),
            [act, _sds((s_len, D_MODEL), BF16)], [], ELEM_TILE,
        )
    else:
        out = _rowwise(f"{tag}_residual", lambda x_, y_, g_: x_ + 0.5 * g_ * y_, [x, y], [gate], [act], [], ELEM_TILE)
    return out, saved


def _normmod_bwd_call(name, dh_parts, x, dxo, gamma, sc, deps=()):
    s_len = x.shape[0]
    n_parts = len(dh_parts)

    def fn(*vals):
        dh = vals[0]
        for extra in vals[1:n_parts]:
            dh = dh + extra
        x_, dxo_, gamma_, sc_ = vals[n_parts:]
        xr, r = _rms(x_)
        n = xr * gamma_
        dn = dh * (1.0 + sc_)
        dx = dxo_ + _rms_bwd(dn * gamma_, xr, r)
        return dx, _sum0(dh * n), _sum0(dh), _sum0(dn * xr)

    vec = _sds((1, D_MODEL), F32)
    return _rowwise(
        name, fn, list(dh_parts) + [x, dxo], [gamma, sc], [_sds((s_len, D_MODEL), F32)], [vec, vec, vec], ELEM_TILE, deps=deps
    )


class _BwdHooks:
    def __init__(self, after_gate=None, after_swiglu=None, after_wdown=None, after_wgu=None, after_dh=None):
        def nothing(_):
            return []

        self.after_gate = after_gate or nothing
        self.after_swiglu = after_swiglu or nothing
        self.after_wdown = after_wdown or nothing
        self.after_wgu = after_wgu or nothing
        self.after_dh = after_dh or nothing


def _ffn_bwd(tag, dxo, x, gamma, sc, gate, saved, hooks, deps=()):
    g4, base, g_down, down_idx, h, ab4, s3, y = saved
    s_len = x.shape[0]
    vec = _sds((1, D_MODEL), F32)

    dy, dgate = _rowwise(
        f"{tag}_bwd_gate", lambda dxo_, y_, g_: (0.5 * g_ * dxo_, _sum0(0.5 * y_ * dxo_)),
        [dxo, y], [gate], [_sds((s_len, D_MODEL), BF16)], [vec], ELEM_TILE, deps=deps,
    )

    tm = min(ROW_TILE, s_len)

    def d_gate_up(dy_ref, wd_ref, ab_ref, o_ref):
        wd = wd_ref[...]
        for rows in _row_blocks(tm):
            ds = _dot(dy_ref[rows, :], wd, "nt")
            a = ab_ref[0, rows, :].astype(F32)
            b = ab_ref[1, rows, :].astype(F32)
            sig = _sigmoid(a)
            o_ref[0, rows, :] = (ds * b * sig * (1.0 + a * (1.0 - sig))).astype(o_ref.dtype)
            o_ref[1, rows, :] = (ds * a * sig).astype(o_ref.dtype)

    ab_spec = pl.BlockSpec((None, 2, tm, F_SHARD), lambda i, c: (c, 0, i, 0))
    dab4 = pl.pallas_call(
        d_gate_up,
        name=f"{tag}_bwd_dgate_up",
        grid=(s_len // tm, N_CHIPS),
        in_specs=[
            pl.BlockSpec((tm, D_MODEL), lambda i, c: (i, 0)),
            pl.BlockSpec((None, None, F_SHARD, D_MODEL), lambda i, c: (c, down_idx, 0, 0)),
            ab_spec,
        ],
        out_specs=ab_spec,
        out_shape=_sds(ab4.shape, BF16),
        compiler_params=_cparams(("parallel", "parallel")),
    )(dy, g_down, ab4)

    tk = tm
    gb_down = _mm(
        f"{tag}_bwd_wdown", "tn", (N_CHIPS, 1, s_len // tk),
        s3, (None, tk, F_SHARD), lambda i, j, k: (i, k, 0),
        dy, (tk, D_MODEL), lambda i, j, k: (k, 0),
        _sds((N_CHIPS, 1, F_SHARD, D_MODEL), BF16), (None, None, F_SHARD, D_MODEL), lambda i, j, k: (i, 0, 0, 0),
        (F_SHARD, D_MODEL), deps=hooks.after_swiglu(dab4),
    )
    gb_gu = _mm(
        f"{tag}_bwd_wgu", "tn", (2 * N_CHIPS, 1, s_len // tk),
        dab4, (None, None, tk, F_SHARD), lambda i, j, k: (i % N_CHIPS, i // N_CHIPS, k, 0),
        h, (tk, D_MODEL), lambda i, j, k: (k, 0),
        _sds((N_CHIPS, 2, F_SHARD, D_MODEL), BF16), (None, None, F_SHARD, D_MODEL),
        lambda i, j, k: (i % N_CHIPS, i // N_CHIPS, 0, 0),
        (F_SHARD, D_MODEL), deps=hooks.after_wdown(gb_down),
    )
    assert base % 2 == 0
    dh_deps = hooks.after_wgu(gb_gu)

    def d_h(dab_ref, w_ref, *rest):
        o_ref, acc_ref = rest[-2:]
        c = pl.program_id(1)
        part = _dot(dab_ref[0], w_ref[0], "nn") + _dot(dab_ref[1], w_ref[1], "nn")

        @pl.when(c == 0)
        def _():
            acc_ref[...] = part

        @pl.when((c > 0) & (c < N_CHIPS - 1))
        def _():
            acc_ref[...] += part

        @pl.when(c == N_CHIPS - 1)
        def _():
            o_ref[...] = acc_ref[...] + part

    dh = pl.pallas_call(
        d_h,
        name=f"{tag}_bwd_dh",
        grid=(s_len // tm, N_CHIPS),
        in_specs=[
            pl.BlockSpec((None, 2, tm, F_SHARD), lambda i, c: (c, 0, i, 0)),
            pl.BlockSpec((None, 2, F_SHARD, D_MODEL), lambda i, c: (c, base // 2, 0, 0)),
        ] + [pl.BlockSpec(memory_space=pl.ANY)] * len(dh_deps),
        out_specs=pl.BlockSpec((tm, D_MODEL), lambda i, c: (i, 0)),
        out_shape=_sds((s_len, D_MODEL), F32),
        scratch_shapes=[pltpu.VMEM((tm, D_MODEL), F32)],
        compiler_params=_cparams(("parallel", "arbitrary")),
    )(dab4, g4, *dh_deps)
    dx, dsc, dsh, dgamma = _normmod_bwd_call(
        f"{tag}_bwd_normmod", [dh], x, dxo, gamma, sc, deps=hooks.after_dh(dh)
    )
    return dx, (gb_down, gb_gu), (dsh, dsc, dgate, dgamma)


def _bucket_map():
    qi = np.arange(WINDOW)[:, None]
    kj = np.arange(2 * WINDOW)[None, :]
    dist = qi + WINDOW - kj
    band = (dist >= 0) & (dist < WINDOW)
    max_exact = NUM_BUCKETS // 2
    n = np.maximum(dist, 0)
    large = []
    for dt in (np.float32, np.float64):
        nf = np.maximum(n, 1).astype(dt)
        val = np.log(nf / dt(max_exact)) / dt(math.log(MAX_DISTANCE / max_exact)) * dt(NUM_BUCKETS - max_exact)
        large.append(np.minimum(max_exact + val.astype(np.int32), NUM_BUCKETS - 1))
    assert np.array_equal(large[0], large[1])
    bucket = np.where(n < max_exact, n, large[0])
    return np.where(band, bucket, -1).astype(np.int32).reshape(1, -1)


def _bias_expand(rel_bias_t, bkt):
    n = bkt.shape[1]

    def body(rb_ref, bkt_ref, o_ref):
        onehot = (lax.broadcasted_iota(jnp.int32, (NUM_BUCKETS, n), 0) == bkt_ref[...]).astype(F32)
        o_ref[...] = lax.dot_general(
            rb_ref[...], onehot, _DN["nn"], precision=lax.Precision.HIGHEST, preferred_element_type=F32
        )

    return pl.pallas_call(
        body, name="bias_expand", out_shape=_sds((SWA_HEADS, n), F32), compiler_params=_cparams()
    )(rel_bias_t, bkt)


def _bias_reduce(dbias_flat, bkt):
    n = bkt.shape[1]

    def body(db_ref, bkt_ref, o_ref):
        onehot = (lax.broadcasted_iota(jnp.int32, (NUM_BUCKETS, n), 0) == bkt_ref[...]).astype(F32)
        o_ref[...] = lax.dot_general(
            db_ref[...], onehot, _DN["nt"], precision=lax.Precision.HIGHEST, preferred_element_type=F32
        )

    return pl.pallas_call(
        body, name="bias_reduce", out_shape=_sds((SWA_HEADS, NUM_BUCKETS), F32), compiler_params=_cparams()
    )(dbias_flat, bkt)


_SWA_SCALE = SWA_HEAD_DIM ** -0.5
_NEG = -1e30


_SWA_PAIR = 2


def _swa_in_specs():
    w = WINDOW
    n_q = SWA_HEADS * SWA_HEAD_DIM
    n_kv = 2 * SWA_KV_HEADS * SWA_HEAD_DIM
    prev = lambda m: jnp.maximum(_SWA_PAIR * m - 1, 0)
    return [
        pl.BlockSpec((_SWA_PAIR * w, n_q), lambda m: (m, 0)),
        pl.BlockSpec((w, n_kv), lambda m: (prev(m), n_q // n_kv)),
        pl.BlockSpec((_SWA_PAIR * w, n_kv), lambda m: (m, n_q // n_kv)),
        pl.BlockSpec((SWA_HEADS, w, 2 * w), lambda m: (0, 0, 0)),
        pl.BlockSpec((SWA_HEADS, w, 1), lambda m: (0, 0, 0)),
    ]


def _head_cols(v, first, count):
    hd = SWA_HEAD_DIM
    return jnp.stack([v[:, (first + i) * hd:(first + i + 1) * hd] for i in range(count)])


def _swa_blocks(q_ref, kvp_ref, kvc_ref, m):
    g, w, hd = SWA_GROUP, WINDOW, SWA_HEAD_DIM
    kv_all = jnp.concatenate([kvp_ref[...], kvc_ref[...]], axis=0).astype(F32)
    blocks = []
    for sub in range(_SWA_PAIR):
        rows = slice(sub * w, (sub + 1) * w)
        q = q_ref[rows, :].astype(F32)
        kv = kv_all[sub * w:(sub + 2) * w]
        heads = []
        for j in range(SWA_KV_HEADS):
            qj = _head_cols(q, g * j, g).reshape(g * w, hd)
            heads.append((qj, _head_cols(kv, j, 1)[0], _head_cols(kv, SWA_KV_HEADS + j, 1)[0]))
        blocks.append((_SWA_PAIR * m + sub, rows, heads))
    return blocks


def _swa_scores(q, kk, bias, n):
    g, w = SWA_GROUP, WINDOW
    s = (_dot(q, kk, "nt") * _SWA_SCALE).reshape(g, w, 2 * w) + bias
    qi = lax.broadcasted_iota(jnp.int32, (w, 2 * w), 0)
    kj = lax.broadcasted_iota(jnp.int32, (w, 2 * w), 1)
    dist = qi + w - kj
    valid = ((dist >= 0) & (dist < w) & ((n > 0) | (kj >= w)))[None]
    return jnp.where(valid, s, _NEG), valid


def _swa_fwd(swa2, bias, sinkb):
    s_len = swa2.shape[0]
    g, w, hd = SWA_GROUP, WINDOW, SWA_HEAD_DIM

    def body(q_ref, kvp_ref, kvc_ref, bias_ref, sink_ref, o_ref, lse_ref):
        for n, rows, heads in _swa_blocks(q_ref, kvp_ref, kvc_ref, pl.program_id(0)):
            outs = []
            for j, (q, kk, vv) in enumerate(heads):
                hs = slice(g * j, g * (j + 1))
                s, valid = _swa_scores(q, kk, bias_ref[hs], n)
                sink = sink_ref[hs]
                m = jnp.maximum(jnp.max(s, axis=-1, keepdims=True), sink)
                p = jnp.where(valid, jnp.exp(s - m), 0.0)
                l = jnp.sum(p, axis=-1, keepdims=True) + jnp.exp(sink - m)
                o = _dot(p.reshape(g * w, 2 * w), vv, "nn").reshape(g, w, hd) / l
                outs += [o[i] for i in range(g)]
                lse_ref[hs, rows, :] = m + jnp.log(l)
            o_ref[rows, :] = jnp.concatenate(outs, axis=-1).astype(o_ref.dtype)

    n_q = SWA_HEADS * hd
    return pl.pallas_call(
        body,
        name="swa_fwd",
        grid=(s_len // (_SWA_PAIR * w),),
        in_specs=_swa_in_specs(),
        out_specs=[
            pl.BlockSpec((_SWA_PAIR * w, n_q), lambda m: (m, 0)),
            pl.BlockSpec((SWA_HEADS, _SWA_PAIR * w, 1), lambda m: (0, m, 0)),
        ],
        out_shape=[_sds((s_len, n_q), BF16), _sds((SWA_HEADS, s_len, 1), F32)],
        compiler_params=_cparams(("parallel",)),
    )(swa2, swa2, swa2, bias, sinkb)


def _swa_bwd(swa2, bias, sinkb, cat, dcat, lse):
    s_len = swa2.shape[0]
    g, w, hd = SWA_GROUP, WINDOW, SWA_HEAD_DIM

    def body(q_ref, kvp_ref, kvc_ref, bias_ref, sink_ref, o_ref, do_ref, lse_ref,
             dq_ref, dkva_ref, dkvb_ref, dbias_ref, dsink_ref):
        step = pl.program_id(0)

        @pl.when(step == 0)
        def _():
            dbias_ref[...] = jnp.zeros_like(dbias_ref)
            dsink_ref[...] = jnp.zeros_like(dsink_ref)

        for n, rows, heads in _swa_blocks(q_ref, kvp_ref, kvc_ref, step):
            o_all = o_ref[rows, :].astype(F32)
            do_all = do_ref[rows, :].astype(F32)
            dqs, dks, dvs = [], [], []
            for j, (q, kk, vv) in enumerate(heads):
                hs = slice(g * j, g * (j + 1))
                s, valid = _swa_scores(q, kk, bias_ref[hs], n)
                lse_v = lse_ref[hs, rows, :]
                p = jnp.where(valid, jnp.exp(s - lse_v), 0.0)
                do = _head_cols(do_all, g * j, g)
                delta = jnp.sum(do * _head_cols(o_all, g * j, g), axis=-1, keepdims=True)
                do2 = do.reshape(g * w, hd)
                dp = _dot(do2, vv, "nt").reshape(g, w, 2 * w)
                ds = p * (dp - delta)
                dbias_ref[hs] += ds
                dsink_ref[hs] -= jnp.exp(sink_ref[hs] - lse_v) * delta
                ds2 = ds.reshape(g * w, 2 * w)
                dq = (_dot(ds2, kk, "nn") * _SWA_SCALE).reshape(g, w, hd)
                dqs += [dq[i] for i in range(g)]
                dks.append(_dot(ds2, q, "tn") * _SWA_SCALE)
                dvs.append(_dot(p.reshape(g * w, 2 * w), do2, "tn"))
            dq_ref[rows, :] = jnp.concatenate(dqs, axis=-1).astype(dq_ref.dtype)
            dkv = jnp.concatenate(dks + dvs, axis=-1)
            dkvb_ref[rows, :] = dkv[:w]
            dkva_ref[rows, :] = dkv[w:]

    n_q = SWA_HEADS * hd
    n_kv = 2 * SWA_KV_HEADS * hd
    q_spec = pl.BlockSpec((_SWA_PAIR * w, n_q), lambda m: (m, 0))
    kv_spec = pl.BlockSpec((_SWA_PAIR * w, n_kv), lambda m: (m, 0))
    return pl.pallas_call(
        body,
        name="swa_bwd",
        grid=(s_len // (_SWA_PAIR * w),),
        in_specs=_swa_in_specs() + [q_spec, q_spec, pl.BlockSpec((SWA_HEADS, _SWA_PAIR * w, 1), lambda m: (0, m, 0))],
        out_specs=[
            q_spec, kv_spec, kv_spec,
            pl.BlockSpec((SWA_HEADS, w, 2 * w), lambda n: (0, 0, 0)),
            pl.BlockSpec((SWA_HEADS, w, 1), lambda n: (0, 0, 0)),
        ],
        out_shape=[
            _sds((s_len, n_q), BF16), _sds((s_len, n_kv), F32), _sds((s_len, n_kv), F32),
            _sds((SWA_HEADS, w, 2 * w), F32), _sds((SWA_HEADS, w, 1), F32),
        ],
        compiler_params=_cparams(("arbitrary",)),
    )(swa2, swa2, swa2, bias, sinkb, cat, dcat, lse)


_MLA_SCALE = MLA_QK ** -0.5
_MLA_TQ = 256
_MLA_FWD_HEADS = 4
_MLA_BWD_HEADS = 2


def _mla_mask(i, tq, n_keys):
    row = i * tq + lax.broadcasted_iota(jnp.int32, (tq, n_keys), 0)
    col = lax.broadcasted_iota(jnp.int32, (tq, n_keys), 1)
    return col <= row


def _per_query_block(i, n_blocks, fn):
    for ii in range(n_blocks):
        pl.when(i == ii)(functools.partial(fn, ii))


def _mla_fwd(q4, k4, v4):
    s_len = q4.shape[1]
    tq = min(_MLA_TQ, s_len)
    pair = _MLA_FWD_HEADS

    def body(q_ref, k_ref, v_ref, o_ref, lse_ref):
        def block(ii):
            n_keys = (ii + 1) * tq
            mask = _mla_mask(ii, tq, n_keys)
            for hh in range(pair):
                s = jnp.where(mask, _dot(q_ref[hh], k_ref[hh, :n_keys, :], "nt") * _MLA_SCALE, _NEG)
                m = jnp.max(s, axis=-1, keepdims=True)
                p = jnp.exp(s - m)
                l = jnp.sum(p, axis=-1, keepdims=True)
                o_ref[:, hh * MLA_V:(hh + 1) * MLA_V] = (_dot(p, v_ref[hh, :n_keys, :], "nn") / l).astype(o_ref.dtype)
                lse_ref[hh] = m + jnp.log(l)

        _per_query_block(pl.program_id(1), s_len // tq, block)

    return pl.pallas_call(
        body,
        name="mla_fwd",
        grid=(MLA_HEADS // pair, s_len // tq),
        in_specs=[
            pl.BlockSpec((pair, tq, MLA_QK), lambda h, i: (h, i, 0)),
            pl.BlockSpec((pair, s_len, MLA_QK), lambda h, i: (h, 0, 0)),
            pl.BlockSpec((pair, s_len, MLA_V), lambda h, i: (h, 0, 0)),
        ],
        out_specs=[
            pl.BlockSpec((tq, pair * MLA_V), lambda h, i: (i, h)),
            pl.BlockSpec((pair, tq, 1), lambda h, i: (h, i, 0)),
        ],
        out_shape=[_sds((s_len, MLA_HEADS * MLA_V), BF16), _sds((MLA_HEADS, s_len, 1), F32)],
        compiler_params=_cparams(("parallel", "parallel")),
    )(q4, k4, v4)


def _mla_bwd(q4, k4, v4, cat, dcat, lse):
    s_len = q4.shape[1]
    tq = min(_MLA_TQ, s_len)
    pair = _MLA_BWD_HEADS
    first = SWA_HEADS * SWA_HEAD_DIM // (pair * MLA_V)

    def body(q_ref, k_ref, v_ref, o_ref, do_ref, lse_ref, dq_ref, dk_ref, dv_ref):
        i = pl.program_id(1)

        @pl.when(i == 0)
        def _():
            dk_ref[...] = jnp.zeros_like(dk_ref)
            dv_ref[...] = jnp.zeros_like(dv_ref)

        def block(ii):
            n_keys = (ii + 1) * tq
            mask = _mla_mask(ii, tq, n_keys)
            for hh in range(pair):
                cols = slice(hh * MLA_V, (hh + 1) * MLA_V)
                q, k, v, do = q_ref[hh], k_ref[hh, :n_keys, :], v_ref[hh, :n_keys, :], do_ref[:, cols]
                s = jnp.where(mask, _dot(q, k, "nt") * _MLA_SCALE, _NEG)
                p = jnp.where(mask, jnp.exp(s - lse_ref[hh]), 0.0)
                delta = jnp.sum(do.astype(F32) * o_ref[:, cols].astype(F32), axis=-1, keepdims=True)
                ds = (p * (_dot(do, v, "nt") - delta) * _MLA_SCALE).astype(BF16)
                dq_ref[hh] = _dot(ds, k, "nn")
                dk_ref[hh, :n_keys, :] += _dot(ds, q, "tn")
                dv_ref[hh, :n_keys, :] += _dot(p, do, "tn")

        _per_query_block(i, s_len // tq, block)

    return pl.pallas_call(
        body,
        name="mla_bwd",
        grid=(MLA_HEADS // pair, s_len // tq),
        in_specs=[
            pl.BlockSpec((pair, tq, MLA_QK), lambda h, i: (h, i, 0)),
            pl.BlockSpec((pair, s_len, MLA_QK), lambda h, i: (h, 0, 0)),
            pl.BlockSpec((pair, s_len, MLA_V), lambda h, i: (h, 0, 0)),
            pl.BlockSpec((tq, pair * MLA_V), lambda h, i: (i, first + h)),
            pl.BlockSpec((tq, pair * MLA_V), lambda h, i: (i, first + h)),
            pl.BlockSpec((pair, tq, 1), lambda h, i: (h, i, 0)),
        ],
        out_specs=[
            pl.BlockSpec((pair, tq, MLA_QK), lambda h, i: (h, i, 0)),
            pl.BlockSpec((pair, s_len, MLA_QK), lambda h, i: (h, 0, 0)),
            pl.BlockSpec((pair, s_len, MLA_V), lambda h, i: (h, 0, 0)),
        ],
        out_shape=[
            _sds((MLA_HEADS, s_len, MLA_QK), F32),
            _sds((MLA_HEADS, s_len, MLA_QK), F32),
            _sds((MLA_HEADS, s_len, MLA_V), F32),
        ],
        compiler_params=_cparams(("parallel", "arbitrary")),
    )(q4, k4, v4, cat, dcat, lse)


def _mla_split(pm):
    q_lat = pm[:, :MLA_Q_RANK]
    kv_lat = pm[:, MLA_Q_RANK:MLA_Q_RANK + MLA_KV_RANK]
    kr = pm[:, MLA_Q_RANK + MLA_KV_RANK:MLA_Q_RANK + MLA_KV_RANK + MLA_ROPE]
    kr_sw = pm[:, MLA_Q_RANK + MLA_KV_RANK + MLA_ROPE:]
    return q_lat, kv_lat, kr, kr_sw


def _mla_proj(pm, cos2, sin2, q_norm, kv_norm, wq_ext, wkv):
    s_len = pm.shape[0]

    def fn(pm_, cos_, sin_, qg, kvg, wq, wk):
        q_lat, kv_lat, kr, kr_sw = _mla_split(pm_)
        nq = (_rms(q_lat)[0] * qg).astype(BF16)
        nkv = (_rms(kv_lat)[0] * kvg).astype(BF16)
        k_rot = kr * cos_ + kr_sw * sin_
        qs, ks, vs = [], [], []
        for h in range(MLA_HEADS):
            qe = _dot(nq, wq[h], "nt")
            q_rot = qe[:, MLA_NOPE:MLA_QK] * cos_ + qe[:, MLA_QK:] * sin_
            qs.append(jnp.concatenate([qe[:, :MLA_NOPE], q_rot], axis=-1))
            kve = _dot(nkv, wk[h], "nt")
            ks.append(jnp.concatenate([kve[:, :MLA_NOPE], k_rot], axis=-1))
            vs.append(kve[:, MLA_NOPE:])
        return jnp.stack(qs), jnp.stack(ks), jnp.stack(vs)

    return _rowwise(
        "mla_proj", fn, [pm, cos2, sin2], [q_norm, kv_norm, wq_ext, wkv],
        [_sds((MLA_HEADS, s_len, MLA_QK), BF16), _sds((MLA_HEADS, s_len, MLA_QK), BF16),
         _sds((MLA_HEADS, s_len, MLA_V), BF16)], [], ELEM_TILE,
    )


def _mla_proj_bwd(pm, cos2, sin2, dq4, dk4, dv4, q_norm, kv_norm, wq_ext, wkv):
    s_len = pm.shape[0]

    def fn(pm_, cos_, sin_, dq, dk, dv, qg, kvg, wq, wk):
        q_lat, kv_lat, _, _ = _mla_split(pm_)
        xq, rq = _rms(q_lat)
        xkv, rkv = _rms(kv_lat)
        nq = (xq * qg).astype(BF16)
        nkv = (xkv * kvg).astype(BF16)
        dnq = jnp.zeros_like(q_lat)
        dnkv = jnp.zeros_like(kv_lat)
        dkr = jnp.zeros_like(cos_)
        dwq, dwk = [], []
        for h in range(MLA_HEADS):
            dy = dq[h][:, MLA_NOPE:]
            dqe = jnp.concatenate([dq[h][:, :MLA_NOPE], dy * cos_, dy * sin_], axis=-1).astype(BF16)
            dnq = dnq + _dot(dqe, wq[h], "nn")
            dwq.append(_dot(dqe, nq, "tn"))
            dkve = jnp.concatenate([dk[h][:, :MLA_NOPE], dv[h]], axis=-1).astype(BF16)
            dnkv = dnkv + _dot(dkve, wk[h], "nn")
            dwk.append(_dot(dkve, nkv, "tn"))
            dkr = dkr + dk[h][:, MLA_NOPE:]
        dq_lat = _rms_bwd(dnq * qg, xq, rq)
        dkv_lat = _rms_bwd(dnkv * kvg, xkv, rkv)
        dpm = jnp.concatenate([dq_lat, dkv_lat, dkr * cos_, dkr * sin_], axis=-1)
        return dpm, _sum0(dnq * xq), _sum0(dnkv * xkv), jnp.stack(dwq), jnp.stack(dwk)

    return _rowwise(
        "mla_proj_bwd", fn, [pm, cos2, sin2, dq4, dk4, dv4], [q_norm, kv_norm, wq_ext, wkv],
        [_sds((s_len, N_MLA_COLS), BF16)],
        [_sds((1, MLA_Q_RANK), F32), _sds((1, MLA_KV_RANK), F32),
         _sds(wq_ext.shape, F32), _sds(wkv.shape, F32)], ELEM_TILE,
    )


def _rope_tables(s_len):
    inv = ROPE_THETA ** (-jnp.arange(0, MLA_ROPE, 2, dtype=F32) / MLA_ROPE)
    ang = jnp.arange(s_len, dtype=F32)[:, None] * inv[None, :]
    cos, sin = jnp.cos(ang), jnp.sin(ang)
    return jnp.concatenate([cos, cos], axis=-1), jnp.concatenate([-sin, sin], axis=-1)


def _mix_weights(g_mix):
    rest = g_mix[:, 0]
    w_in_t = rest[:, O_IN:O_IN + R_IN].reshape(D_IN, D_MODEL)
    w_o = rest[:, O_O:O_O + R_O].reshape(D_MODEL, D_MODEL)
    w_uq_t = rest[:, O_UQ:O_UQ + R_UQ].reshape(MLA_HEADS, MLA_QK, MLA_Q_RANK)
    w_ukv_t = rest[:, O_UKV:O_UKV + R_UKV].reshape(MLA_HEADS, MLA_NOPE + MLA_V, MLA_KV_RANK)
    half = MLA_ROPE // 2
    w_swa = w_in_t[:N_SWA_COLS]
    w_mla = jnp.concatenate([w_in_t[N_SWA_COLS:], w_in_t[D_IN - half:], w_in_t[D_IN - MLA_ROPE:D_IN - half]], axis=0)
    wq_ext = jnp.concatenate([w_uq_t, w_uq_t[:, MLA_QK - half:], w_uq_t[:, MLA_NOPE:MLA_QK - half]], axis=1)
    return w_swa, w_mla, w_o, wq_ext, w_ukv_t


def _mix_fwd(x, h, gate, mix_src, q_norm, kv_norm, bias, sinkb, cos2, sin2, next_norm, next_src):
    s_len = x.shape[0]
    tm = min(ROW_TILE, s_len)
    deps = mix_src.mid(x)
    weights = _mix_weights(mix_src.get(h))
    w_swa, w_mla, w_o, wq_ext, wkv = weights
    swa2 = _mm(
        "mix_proj_swa", "nt", (s_len // tm, 1, 1),
        h, (tm, D_MODEL), lambda i, j, k: (i, 0),
        w_swa, (N_SWA_COLS, D_MODEL), lambda i, j, k: (0, 0),
        _sds((s_len, N_SWA_COLS), BF16), (tm, N_SWA_COLS), lambda i, j, k: (i, 0),
        (tm, N_SWA_COLS), deps=deps,
    )
    pm = _mm(
        "mix_proj_mla", "nt", (s_len // tm, 1, 1),
        h, (tm, D_MODEL), lambda i, j, k: (i, 0),
        w_mla, (N_MLA_COLS, D_MODEL), lambda i, j, k: (0, 0),
        _sds((s_len, N_MLA_COLS), F32), (tm, N_MLA_COLS), lambda i, j, k: (i, 0),
        (tm, N_MLA_COLS),
    )
    q4, k4, v4 = _mla_proj(pm, cos2, sin2, q_norm, kv_norm, wq_ext, wkv)
    oa, lse_a = _swa_fwd(swa2, bias, sinkb)
    ob, lse_b = _mla_fwd(q4, k4, v4)
    cat = jnp.concatenate([oa, ob], axis=1)
    z = _mm(
        "mix_out", "nn", (s_len // tm, 1, 1),
        cat, (tm, D_MODEL), lambda i, j, k: (i, 0),
        w_o, (D_MODEL, D_MODEL), lambda i, j, k: (0, 0),
        _sds((s_len, D_MODEL), F32), (tm, D_MODEL), lambda i, j, k: (i, 0),
        (tm, D_MODEL), deps=next_src.mid(cat),
    )

    def residual_norm(x_, z_, g_, gamma_, sc_, sh_):
        x_out = x_ + g_ * z_
        return x_out, _normmod(x_out, gamma_, sc_, sh_)

    out = _rowwise(
        "mix_residual_norm", residual_norm, [x, z], [gate] + list(next_norm),
        [_sds((s_len, D_MODEL), F32), _sds((s_len, D_MODEL), BF16)], [], ELEM_TILE,
    )
    return out, (weights, h, swa2, pm, q4, k4, v4, cat, lse_a, lse_b, z)


def _mix_bwd(dxo, x, gamma, sc, gate, q_norm, kv_norm, bias, sinkb, cos2, sin2, saved, hooks):
    weights, h, swa2, pm, q4, k4, v4, cat, lse_a, lse_b, z = saved
    w_swa, w_mla, w_o, wq_ext, wkv = weights
    s_len = x.shape[0]
    tm = tk = min(ROW_TILE, s_len)
    vec = _sds((1, D_MODEL), F32)
    n_proj = N_SWA_COLS + N_MLA_COLS
    half_d = D_MODEL // 2

    dz, dgate = _rowwise(
        "mix_bwd_gate", lambda dxo_, z_, g_: (g_ * dxo_, _sum0(z_ * dxo_)),
        [dxo, z], [gate], [_sds((s_len, D_MODEL), BF16)], [vec], ELEM_TILE,
    )
    dw_o = _mm(
        "mix_bwd_wo", "tn", (2, 1, s_len // tk),
        cat, (tk, half_d), lambda i, j, k: (k, i),
        dz, (tk, D_MODEL), lambda i, j, k: (k, 0),
        _sds((D_MODEL, D_MODEL), F32), (half_d, D_MODEL), lambda i, j, k: (i, 0),
        (half_d, D_MODEL),
    )
    dcat = _mm(
        "mix_bwd_dcat", "nt", (s_len // tm, 1, 1),
        dz, (tm, D_MODEL), lambda i, j, k: (i, 0),
        w_o, (D_MODEL, D_MODEL), lambda i, j, k: (0, 0),
        _sds((s_len, D_MODEL), BF16), (tm, D_MODEL), lambda i, j, k: (i, 0),
        (tm, D_MODEL), deps=hooks.after_gate(dz),
    )
    dq_a, dkva, dkvb, dbias, dsink = _swa_bwd(swa2, bias, sinkb, cat, dcat, lse_a)
    dq4, dk4, dv4 = _mla_bwd(q4, k4, v4, cat, dcat, lse_b)
    dpm, dq_norm, dkv_norm, dwq_ext, dwkv = _mla_proj_bwd(pm, cos2, sin2, dq4, dk4, dv4, q_norm, kv_norm, wq_ext, wkv)

    dkv = dkva + jnp.concatenate([dkvb[WINDOW:], jnp.zeros_like(dkvb[:WINDOW])], axis=0)
    dproj = jnp.concatenate([dq_a, dkv.astype(BF16), dpm], axis=1)
    w_proj = jnp.concatenate([w_swa, w_mla], axis=0)

    dw_proj = _mm(
        "mix_bwd_win", "tn", (n_proj // 256, 1, s_len // tk),
        dproj, (tk, 256), lambda i, j, k: (k, i),
        h, (tk, D_MODEL), lambda i, j, k: (k, 0),
        _sds((n_proj, D_MODEL), F32), (256, D_MODEL), lambda i, j, k: (i, 0),
        (256, D_MODEL),
    )
    dw_swa, dw_mla = dw_proj[:N_SWA_COLS], dw_proj[N_SWA_COLS:]
    dh = _mm(
        "mix_bwd_dh", "nn", (s_len // tm, 1, 1),
        dproj, (tm, n_proj), lambda i, j, k: (i, 0),
        w_proj, (n_proj, D_MODEL), lambda i, j, k: (0, 0),
        _sds((s_len, D_MODEL), F32), (tm, D_MODEL), lambda i, j, k: (i, 0),
        (tm, D_MODEL),
    )
    dx, dsc, dsh, dgamma = _normmod_bwd_call("mix_bwd_normmod", [dh], x, dxo, gamma, sc)

    half = MLA_ROPE // 2
    n_mla_in = D_IN - N_SWA_COLS
    dw_mla_base = dw_mla[:n_mla_in]
    dw_mla_base = dw_mla_base.at[n_mla_in - half:].add(dw_mla[n_mla_in:n_mla_in + half])
    dw_mla_base = dw_mla_base.at[n_mla_in - MLA_ROPE:n_mla_in - half].add(dw_mla[n_mla_in + half:])
    dw_in_t = jnp.concatenate([dw_swa, dw_mla_base], axis=0)
    dw_uq_t = dwq_ext[:, :MLA_QK]
    dw_uq_t = dw_uq_t.at[:, MLA_QK - half:].add(dwq_ext[:, MLA_QK:MLA_QK + half])
    dw_uq_t = dw_uq_t.at[:, MLA_NOPE:MLA_QK - half].add(dwq_ext[:, MLA_QK + half:])
    rest = jnp.concatenate(
        [
            dw_in_t.reshape(N_CHIPS, R_IN, D_MODEL),
            dw_o.reshape(N_CHIPS, R_O, D_MODEL),
            dw_uq_t.reshape(N_CHIPS, R_UQ, D_MODEL),
            dwkv.reshape(N_CHIPS, R_UKV, D_MODEL),
            jnp.zeros((N_CHIPS, F_SHARD - O_PAD, D_MODEL), F32),
        ],
        axis=1,
    ).astype(BF16)
    return dx, rest, (dsh, dsc, dgate, dgamma), dq_norm, dkv_norm, dbias, dsink


def _device_step(x, target, mod, gu1_src, down1_src, mix_src, ffn2_src, norms, q_norm, kv_norm, sinks, rel_bias,
                 hooks2=None, hooks_mix=None, mix_done=None, hooks1=None):
    s_len = x.shape[0]
    sh1, sc1, g1, sh2, sc2, g2, sh3, sc3, g3 = [mod[:, i * D_MODEL:(i + 1) * D_MODEL] for i in range(N_MOD)]
    n1, n2, n3, nf = norms
    bkt = jnp.asarray(_bucket_map())
    bias = _bias_expand(rel_bias.T, bkt).reshape(SWA_HEADS, WINDOW, 2 * WINDOW)
    sinkb = jnp.broadcast_to(sinks.reshape(SWA_HEADS, 1, 1), (SWA_HEADS, WINDOW, 1))
    cos2, sin2 = _rope_tables(s_len)

    (x1, h2), saved1 = _ffn_fwd(
        "ffn1", x, n1, sh1, sc1, g1, gu1_src, 0, down1_src, 0, sh1, next_norm=(n2, sc2, sh2)
    )
    (x2, h3), saved2 = _mix_fwd(
        x1, h2, g2, mix_src, q_norm, kv_norm, bias, sinkb, cos2, sin2, next_norm=(n3, sc3, sh3), next_src=ffn2_src
    )
    (dx3, sq, dnf), saved3 = _ffn_fwd(
        "ffn2", x2, n3, sh3, sc3, g3, ffn2_src, 0, None, 2, x2, h_pre=h3, head=(target, nf)
    )
    loss_part = (0.5 / D_MODEL) * jnp.sum(sq)

    dx2, gb2, (dsh3, dsc3, dg3, dn3) = _ffn_bwd("ffn2", dx3, x2, n3, sc3, g3, saved3, hooks2 or _BwdHooks())
    dx1, rest, (dsh2, dsc2, dg2, dn2), dq_norm, dkv_norm, dbias, dsink = _mix_bwd(
        dx2, x1, n2, sc2, g2, q_norm, kv_norm, bias, sinkb, cos2, sin2, saved2, hooks_mix or _BwdHooks()
    )
    rest = rest[:, None]
    deps1 = mix_done(dx1, rest) if mix_done is not None else []
    dx0, gb1, (dsh1, dsc1, dg1, dn1) = _ffn_bwd(
        "ffn1", dx1, x, n1, sc1, g1, saved1, hooks1 or _BwdHooks(), deps=deps1
    )

    dmod = jnp.concatenate([dsh1, dsc1, dg1, dsh2, dsc2, dg2, dsh3, dsc3, dg3], axis=-1)
    drel = _bias_reduce(dbias.reshape(SWA_HEADS, -1), bkt).T
    dsinks = jnp.sum(dsink, axis=(1, 2)).reshape(1, SWA_HEADS)
    small = (dn1, dn2, dn3, dnf, dq_norm, dkv_norm, dsinks, drel)
    return loss_part, dx0, (gb1, gb2, rest), dmod, small


_MESH = pl.DeviceIdType.MESH


def _place():
    return lax.axis_index("x"), lax.axis_index("y"), lax.axis_index("c")


def _other_chips(x, y):
    return [(1 - x, y), (x, 1 - y), (1 - x, 1 - y)]


def _allgather_small(name, blk):
    m_per, n = blk.shape

    def body(x_ref, out_ref, send_sems, recv_sems, local_sem):
        x, y, c = _place()
        me, sibling = (x, y, c), (x, y, 1 - c)
        chips = _other_chips(x, y)

        def rows(px, py, pc):
            return out_ref.at[pl.ds((4 * px + 2 * py + pc) * m_per, m_per), :]

        def copy(k, block, to, src=None):
            return pltpu.make_async_remote_copy(
                src_ref=rows(*block) if src is None else src, dst_ref=rows(*block),
                send_sem=send_sems.at[k], recv_sem=recv_sems.at[k], device_id=to, device_id_type=_MESH,
            )

        mine = pltpu.make_async_copy(x_ref, rows(*me), local_sem)
        mine.start()
        first = [copy(0, me, sibling, src=x_ref)]
        first += [copy(1 + j, me, (*chip, c), src=x_ref) for j, chip in enumerate(chips)]
        for cp in first:
            cp.start()
        passed = [copy(4 + j, (*chip, c), sibling) for j, chip in enumerate(chips)]
        for j, chip in enumerate(chips):
            copy(1 + j, (*chip, c), me).wait_recv()
            passed[j].start()
        copy(0, sibling, me).wait_recv()
        for j, chip in enumerate(chips):
            copy(4 + j, (*chip, 1 - c), me).wait_recv()
        for cp in first + passed:
            cp.wait_send()
        mine.wait()

    return pl.pallas_call(
        body,
        name=name,
        out_shape=_sds((N_DEV * m_per, n), blk.dtype),
        in_specs=[pl.BlockSpec(memory_space=pltpu.VMEM)],
        out_specs=pl.BlockSpec(memory_space=pltpu.VMEM),
        scratch_shapes=[pltpu.SemaphoreType.DMA((7,)), pltpu.SemaphoreType.DMA((7,)), pltpu.SemaphoreType.DMA],
    )(blk)


def _gather_sends(n, own_ref, g_ref, send_sem, recv_sem, x, y, c, chip):
    return [
        pltpu.make_async_remote_copy(
            src_ref=own_ref.at[p, pl.ds(c * HALF, HALF), :], dst_ref=g_ref.at[2 * x + y, p, pl.ds(c * HALF, HALF), :],
            send_sem=send_sem, recv_sem=recv_sem, device_id=(*chip, c), device_id_type=_MESH,
        )
        for p in range(n)
    ]


def _gather_forwards(n, g_ref, send_sem, recv_sem, chip, c, sibling):
    return [
        pltpu.make_async_remote_copy(
            src_ref=g_ref.at[2 * chip[0] + chip[1], p, pl.ds(c * HALF, HALF), :],
            dst_ref=g_ref.at[2 * chip[0] + chip[1], p, pl.ds(c * HALF, HALF), :],
            send_sem=send_sem, recv_sem=recv_sem, device_id=sibling, device_id_type=_MESH,
        )
        for p in range(n)
    ]


def _gather_all(own_ref, g_ref, send_sem, recv_sem, chip, half, c):
    return pltpu.make_async_remote_copy(
        src_ref=own_ref.at[:, pl.ds(c * HALF, HALF), :],
        dst_ref=g_ref.at[2 * chip[0] + chip[1], :, pl.ds(half * HALF, HALF), :],
        send_sem=send_sem, recv_sem=recv_sem, device_id=(*chip, c), device_id_type=_MESH,
    )


_HBM_SPEC = pl.BlockSpec(memory_space=pltpu.HBM)
_SEM_SPEC = pl.BlockSpec(memory_space=pltpu.SEMAPHORE)
_ANY_SPEC = pl.BlockSpec(memory_space=pl.ANY)
_VMEM_SPEC = pl.BlockSpec(memory_space=pltpu.VMEM)
_SPLIT_PARAMS = pltpu.CompilerParams(has_side_effects=pltpu.SideEffectType.DATAFLOW_SIDE_EFFECTING)
_TOKEN = jax.ShapeDtypeStruct((8, 128), F32)


def _in_hbm(a):
    return pltpu.with_memory_space_constraint(a, pltpu.HBM)


class _GatherBehind:
    def __init__(self, tag, own, then=()):
        self.tag, self.n, self.own, self.then = tag, own.shape[0], own, tuple(then)
        self.mid_deps = None

    def start(self, after):
        tag, own, n = self.tag, self.own, self.n
        x, y, _ = _place()
        g_init = lax.dynamic_update_slice(
            lax.empty((N_CHIPS,) + own.shape, own.dtype), own[None], (2 * x + y, 0, 0, 0)
        )

        def body(own_ref, g_ref, *rest):
            send_sems, recv_sems, _, _, token = rest[len(after):]
            x, y, c = _place()
            for j, chip in enumerate(_other_chips(x, y)):
                for cp in _gather_sends(n, own_ref, g_ref, send_sems.at[j], recv_sems.at[j], x, y, c, chip):
                    cp.start()
            token[...] = jnp.zeros_like(token)

        self.send1, self.recv1, self.own, self.g, self.token = pl.pallas_call(
            body,
            name=f"{tag}_start",
            out_shape=(
                pltpu.SemaphoreType.DMA((3,)), pltpu.SemaphoreType.DMA((3,)),
                pltpu.HBM(own.shape, own.dtype), pltpu.HBM(g_init.shape, g_init.dtype), _TOKEN,
            ),
            in_specs=(_HBM_SPEC, _HBM_SPEC) + (_ANY_SPEC,) * len(after),
            out_specs=(_SEM_SPEC, _SEM_SPEC, _HBM_SPEC, _HBM_SPEC, _VMEM_SPEC),
            input_output_aliases={0: 2, 1: 3},
            compiler_params=_SPLIT_PARAMS,
        )(_in_hbm(own), _in_hbm(g_init), *after)

    def mid(self, after):
        if self.mid_deps is not None:
            return self.mid_deps
        n = self.n

        def body(own_ref, g_ref, send1, recv1, after_ref, send2, recv2, g_out, token):
            x, y, c = _place()
            sibling = (x, y, 1 - c)
            chips = _other_chips(x, y)
            for j, chip in enumerate(chips):
                _gather_all(own_ref, g_ref, send1.at[j], recv1.at[j], chip, c, c).wait_recv()
                for cp in _gather_forwards(n, g_ref, send2.at[j], recv2.at[j], chip, c, sibling):
                    cp.start()
            for j, chip in enumerate(chips):
                _gather_all(own_ref, g_ref, send1.at[j], recv1.at[j], chip, c, c).wait_send()
            token[...] = jnp.zeros_like(token)

        self.send2, self.recv2, self.g, token = pl.pallas_call(
            body,
            name=f"{self.tag}_mid",
            out_shape=(
                pltpu.SemaphoreType.DMA((3,)), pltpu.SemaphoreType.DMA((3,)),
                pltpu.HBM(self.g.shape, self.g.dtype), _TOKEN,
            ),
            in_specs=(_HBM_SPEC, _HBM_SPEC, _SEM_SPEC, _SEM_SPEC, _ANY_SPEC),
            out_specs=(_SEM_SPEC, _SEM_SPEC, _HBM_SPEC, _VMEM_SPEC),
            input_output_aliases={1: 2},
            compiler_params=_SPLIT_PARAMS,
        )(self.own, self.g, self.send1, self.recv1, after)
        deps = [token]
        for nxt in self.then:
            nxt.start(deps[-1:])
            deps.append(nxt.token)
        self.mid_deps = deps
        return deps

    def get(self, after):
        def body(g_ref, send2, recv2, after_ref, g_out):
            x, y, c = _place()
            for j, chip in enumerate(_other_chips(x, y)):
                slot_in = g_ref.at[2 * chip[0] + chip[1], :, pl.ds((1 - c) * HALF, HALF), :]
                slot_out = g_ref.at[2 * chip[0] + chip[1], :, pl.ds(c * HALF, HALF), :]
                cp = pltpu.make_async_remote_copy(
                    src_ref=slot_out, dst_ref=slot_in, send_sem=send2.at[j], recv_sem=recv2.at[j],
                    device_id=(x, y, 1 - c), device_id_type=_MESH,
                )
                cp.wait_send()
                cp.wait_recv()

        return pl.pallas_call(
            body,
            name=f"{self.tag}_wait",
            out_shape=pltpu.HBM(self.g.shape, self.g.dtype),
            in_specs=(_HBM_SPEC, _SEM_SPEC, _SEM_SPEC, _ANY_SPEC),
            out_specs=_HBM_SPEC,
            input_output_aliases={0: 0},
            compiler_params=_SPLIT_PARAMS,
        )(self.g, self.send2, self.recv2, after)


def _pair_sum(name, gb, land):
    def body(c_ref, gb_ref, land_ref, o_ref):
        o_ref[...] = (gb_ref[...].astype(F32) + land_ref[...].astype(F32)).astype(o_ref.dtype)

    core = lax.axis_index("c").astype(jnp.int32).reshape(1)
    return pl.pallas_call(
        body,
        name=name,
        grid_spec=pltpu.PrefetchScalarGridSpec(
            num_scalar_prefetch=1,
            grid=(N_CHIPS, gb.shape[1]),
            in_specs=[
                pl.BlockSpec((None, None, HALF, D_MODEL), lambda j, p, c: (j, p, c[0], 0)),
                pl.BlockSpec((None, None, HALF, D_MODEL), lambda j, p, c: (j, p, 0, 0)),
            ],
            out_specs=pl.BlockSpec((None, None, HALF, D_MODEL), lambda j, p, c: (j, p, 0, 0)),
        ),
        out_shape=_sds(land.shape, BF16),
        compiler_params=_cparams(("parallel", "parallel")),
    )(core, gb, land)


def _chip_sum_share(name, land):
    n = land.shape[1]

    def body(l_ref, r_ref, buf, send_sems, recv_sems, local_sems):
        p = pl.program_id(0)
        x, y, c = _place()
        acc = l_ref[0].astype(F32)
        for j in range(1, N_CHIPS):
            acc = acc + l_ref[j].astype(F32)
        buf[p] = acc

        def copies(q):
            mine = r_ref.at[q, pl.ds(c * HALF, HALF), :]
            theirs = r_ref.at[q, pl.ds((1 - c) * HALF, HALF), :]
            local = pltpu.make_async_copy(buf.at[q], mine, local_sems.at[q])
            out = pltpu.make_async_remote_copy(
                src_ref=buf.at[q], dst_ref=mine, send_sem=send_sems.at[q], recv_sem=recv_sems.at[q],
                device_id=(x, y, 1 - c), device_id_type=_MESH,
            )
            arrive = pltpu.make_async_remote_copy(
                src_ref=buf.at[q], dst_ref=theirs, send_sem=send_sems.at[q], recv_sem=recv_sems.at[q],
                device_id=(x, y, 1 - c), device_id_type=_MESH,
            )
            return local, out, arrive

        local, out, _ = copies(p)
        local.start()
        out.start()

        @pl.when(p == n - 1)
        def _():
            for q in range(n):
                local, out, arrive = copies(q)
                arrive.wait_recv()
                out.wait_send()
                local.wait()

    return pl.pallas_call(
        body,
        name=name,
        grid=(n,),
        in_specs=[pl.BlockSpec((N_CHIPS, None, HALF, D_MODEL), lambda p: (0, p, 0, 0))],
        out_specs=pl.BlockSpec(memory_space=pl.ANY),
        out_shape=_sds((n, F_SHARD, D_MODEL), F32),
        scratch_shapes=[
            pltpu.VMEM((n, HALF, D_MODEL), F32),
            pltpu.SemaphoreType.DMA((n,)), pltpu.SemaphoreType.DMA((n,)), pltpu.SemaphoreType.DMA((n,)),
        ],
        compiler_params=_cparams(("arbitrary",)),
    )(land)


class _ReduceBehind:
    def __init__(self, tag, gb, after=()):
        self.tag = tag
        n = gb.shape[1]
        land = lax.empty((N_CHIPS, n, HALF, D_MODEL), gb.dtype)

        def body(gb_ref, land_ref, *rest):
            send_sem, recv_sem, _, _, token = rest[len(after):]
            self._pair_copy(gb_ref, land_ref, send_sem, recv_sem).start()
            token[...] = jnp.zeros_like(token)

        self.send, self.recv, self.gb, self.land, self.token = pl.pallas_call(
            body,
            name=f"grads_pair_start_{tag}",
            out_shape=(
                pltpu.SemaphoreType.DMA((1,)), pltpu.SemaphoreType.DMA((1,)),
                pltpu.HBM(gb.shape, gb.dtype), pltpu.HBM(land.shape, land.dtype), _TOKEN,
            ),
            in_specs=(_HBM_SPEC, _HBM_SPEC) + (_ANY_SPEC,) * len(after),
            out_specs=(_SEM_SPEC, _SEM_SPEC, _HBM_SPEC, _HBM_SPEC, _VMEM_SPEC),
            input_output_aliases={0: 2, 1: 3},
            compiler_params=_SPLIT_PARAMS,
        )(_in_hbm(gb), _in_hbm(land), *after)

    @staticmethod
    def _pair_copy(gb_ref, land_ref, send_sem, recv_sem):
        x, y, c = _place()
        return pltpu.make_async_remote_copy(
            src_ref=gb_ref.at[:, :, pl.ds((1 - c) * HALF, HALF), :], dst_ref=land_ref,
            send_sem=send_sem.at[0], recv_sem=recv_sem.at[0], device_id=(x, y, 1 - c), device_id_type=_MESH,
        )

    @staticmethod
    def _chip_copies(p_ref, land_ref, send_sems, recv_sems):
        x, y, c = _place()
        me = 2 * x + y
        sends, arrivals = [], []
        for k, chip in enumerate(_other_chips(x, y)):
            them = 2 * chip[0] + chip[1]
            sends.append(pltpu.make_async_remote_copy(
                src_ref=p_ref.at[them], dst_ref=land_ref.at[me], send_sem=send_sems.at[k], recv_sem=recv_sems.at[k],
                device_id=(*chip, c), device_id_type=_MESH,
            ))
            arrivals.append(pltpu.make_async_remote_copy(
                src_ref=p_ref.at[me], dst_ref=land_ref.at[them], send_sem=send_sems.at[k], recv_sem=recv_sems.at[k],
                device_id=(*chip, c), device_id_type=_MESH,
            ))
        return sends, arrivals

    def mid(self, after):
        tag = self.tag

        def wait_body(gb_ref, land_ref, send_sem, recv_sem, after_ref, gb_out, land_out):
            cp = self._pair_copy(gb_ref, land_ref, send_sem, recv_sem)
            cp.wait_send()
            cp.wait_recv()

        gb, land = pl.pallas_call(
            wait_body,
            name=f"grads_pair_wait_{tag}",
            out_shape=(pltpu.HBM(self.gb.shape, self.gb.dtype), pltpu.HBM(self.land.shape, self.land.dtype)),
            in_specs=(_HBM_SPEC, _HBM_SPEC, _SEM_SPEC, _SEM_SPEC, _ANY_SPEC),
            out_specs=(_HBM_SPEC, _HBM_SPEC),
            input_output_aliases={0: 0, 1: 1},
            compiler_params=_SPLIT_PARAMS,
        )(self.gb, self.land, self.send, self.recv, after)
        part = _pair_sum(f"grads_pair_sum_{tag}", gb, land)

        x, y, _ = _place()
        start = (2 * x + y, 0, 0, 0)
        own = lax.dynamic_slice(part, start, (1,) + part.shape[1:])
        land2 = lax.dynamic_update_slice(lax.empty(part.shape, part.dtype), own, start)

        def start_body(p_ref, land_ref, send_sems, recv_sems, p_out, land_out, token):
            for cp in self._chip_copies(p_ref, land_ref, send_sems, recv_sems)[0]:
                cp.start()
            token[...] = jnp.zeros_like(token)

        self.send2, self.recv2, self.part, self.land2, token = pl.pallas_call(
            start_body,
            name=f"grads_chip_start_{tag}",
            out_shape=(
                pltpu.SemaphoreType.DMA((3,)), pltpu.SemaphoreType.DMA((3,)),
                pltpu.HBM(part.shape, part.dtype), pltpu.HBM(land2.shape, land2.dtype), _TOKEN,
            ),
            in_specs=(_HBM_SPEC, _HBM_SPEC),
            out_specs=(_SEM_SPEC, _SEM_SPEC, _HBM_SPEC, _HBM_SPEC, _VMEM_SPEC),
            input_output_aliases={0: 2, 1: 3},
            compiler_params=_SPLIT_PARAMS,
        )(_in_hbm(part), _in_hbm(land2))
        return [token]

    def get(self, after):
        n_after = len(after)

        def wait_body(p_ref, land_ref, send_sems, recv_sems, *rest):
            sends, arrivals = self._chip_copies(p_ref, land_ref, send_sems, recv_sems)
            for cp in arrivals:
                cp.wait_recv()
            for cp in sends:
                cp.wait_send()

        _, land2 = pl.pallas_call(
            wait_body,
            name=f"grads_chip_wait_{self.tag}",
            out_shape=(pltpu.HBM(self.part.shape, self.part.dtype), pltpu.HBM(self.land2.shape, self.land2.dtype)),
            in_specs=(_HBM_SPEC, _HBM_SPEC, _SEM_SPEC, _SEM_SPEC) + (_ANY_SPEC,) * n_after,
            out_specs=(_HBM_SPEC, _HBM_SPEC),
            input_output_aliases={0: 0, 1: 1},
            compiler_params=_SPLIT_PARAMS,
        )(self.part, self.land2, self.send2, self.recv2, *after)
        return _chip_sum_share(f"grads_chip_sum_share_{self.tag}", land2)


def _allreduce_small(blk):
    gathered = _allgather_small("allgather_small_grads", blk).reshape(N_DEV, PK_ROWS, 128)

    def body(g_ref, o_ref):
        acc = g_ref[0]
        for k in range(1, N_DEV):
            acc = acc + g_ref[k]
        o_ref[...] = acc

    total = pl.pallas_call(body, name="small_grads_sum", out_shape=_sds((PK_ROWS, 128), F32))(gathered)
    return gathered, total


def _adamw_math(w_, g_, m_, v_):
    m_new = ADAM_B1 * m_ + (1.0 - ADAM_B1) * g_
    v_new = ADAM_B2 * v_ + (1.0 - ADAM_B2) * (g_ * g_)
    m_hat = m_new / (1.0 - ADAM_B1 ** ADAM_STEP)
    v_hat = v_new / (1.0 - ADAM_B2 ** ADAM_STEP)
    delta = -ADAM_LR * (m_hat / (jnp.sqrt(v_hat) + ADAM_EPS) + ADAM_WD * w_)
    return delta, m_new, v_new


def _adamw(name, w, g, m, v):
    rows, cols = w.shape
    tm = rows
    while tm * cols * 4 > (1 << 21) and tm % 16 == 0:
        tm //= 2
    out = _sds(w.shape, F32)
    return _rowwise(name, _adamw_math, [w, g, m, v], [], [out, out, out], [], tm)


def _adamw_small(ws, gs, ms, vs):
    n = len(ws)
    shapes = [w.shape for w in ws]
    as2d = lambda a: a.reshape(1, -1) if a.ndim == 1 else a

    def body(*refs):
        ins, outs = refs[:4 * n], refs[4 * n:]
        for k in range(n):
            res = _adamw_math(*[ins[j * n + k][...] for j in range(4)])
            for j in range(3):
                outs[j * n + k][...] = res[j]

    args = [as2d(a) for group in (ws, gs, ms, vs) for a in group]
    out = pl.pallas_call(
        body, name="adamw_small", out_shape=[_sds(as2d(w).shape, F32) for w in ws] * 3, compiler_params=_cparams()
    )(*args)
    back = [o.reshape(shapes[i % n]) for i, o in enumerate(out)]
    return back[:n], back[n:2 * n], back[2 * n:]


def _pack_small(b_mod_like, n1, n2, n3, nf, qn, kvn, sinks, rel):
    parts = [
        b_mod_like.reshape(PK_MOD, 128),
        n1.reshape(8, 128), n2.reshape(8, 128), n3.reshape(8, 128), nf.reshape(8, 128),
        qn.reshape(2, 128), kvn.reshape(1, 128),
        jnp.pad(sinks.reshape(1, SWA_HEADS), ((0, 0), (0, 128 - SWA_HEADS))),
        rel.reshape(2, 128),
        jnp.zeros((2, 128), F32),
    ]
    return jnp.concatenate(parts, axis=0)


def _unpack_small(p):
    o = PK_MOD
    return (
        p[:o].reshape(1, N_MOD * D_MODEL),
        p[o:o + 8].reshape(1, D_MODEL), p[o + 8:o + 16].reshape(1, D_MODEL),
        p[o + 16:o + 24].reshape(1, D_MODEL), p[o + 24:o + 32].reshape(D_MODEL),
        p[o + 32:o + 34].reshape(1, MLA_Q_RANK), p[o + 34:o + 35].reshape(1, MLA_KV_RANK),
        p[o + 35:o + 36, :SWA_HEADS].reshape(1, SWA_HEADS),
        p[o + 36:o + 38].reshape(NUM_BUCKETS, SWA_HEADS),
    )


def kernel(x, c, w_mod, b_mod, norm_ffn1, ffn1_gate, ffn1_up, ffn1_down, norm_mix, w_in, q_norm, kv_norm, w_uq, w_ukv, sinks, w_o, norm_ffn2, ffn2_gate, ffn2_up, ffn2_down, rel_bias, norm_final, loss_target, m_w_mod, m_b_mod, m_norm_ffn1, m_ffn1_gate, m_ffn1_up, m_ffn1_down, m_norm_mix, m_w_in, m_q_norm, m_kv_norm, m_w_uq, m_w_ukv, m_sinks, m_w_o, m_norm_ffn2, m_ffn2_gate, m_ffn2_up, m_ffn2_down, m_rel_bias, m_norm_final, v_w_mod, v_b_mod, v_norm_ffn1, v_ffn1_gate, v_ffn1_up, v_ffn1_down, v_norm_mix, v_w_in, v_q_norm, v_kv_norm, v_w_uq, v_w_ukv, v_sinks, v_w_o, v_norm_ffn2, v_ffn2_gate, v_ffn2_up, v_ffn2_down, v_rel_bias, v_norm_final):
    ax, ay, ac = _place()
    chip = 2 * ax + ay
    dev = 2 * chip + ac
    n_mod_shard = N_MOD * D_MODEL // N_CHIPS

    def t16(w):
        return w[0].T.astype(BF16)

    rest = jnp.concatenate(
        [
            t16(w_in),
            w_o[0].astype(BF16),
            t16(w_uq).reshape(R_UQ, D_MODEL),
            t16(w_ukv).reshape(R_UKV, D_MODEL),
            jnp.zeros((F_SHARD - O_PAD, D_MODEL), BF16),
        ],
        axis=0,
    )
    own_gu1 = jnp.stack([t16(ffn1_gate), t16(ffn1_up)])
    own_down1 = ffn1_down[0].astype(BF16)[None]
    own_mix = rest[None]
    own_ffn2 = jnp.stack([t16(ffn2_gate), t16(ffn2_up), ffn2_down[0].astype(BF16)])

    (c_act,) = _rowwise(
        "silu_c", lambda v: v * _sigmoid(v), [c.reshape(8, 128)], [], [_sds((8, 128), F32)], [], 8
    )
    c_all = _allgather_small("allgather_c", c_act).reshape(N_DEV, D_MODEL)
    mod_cols = _mm(
        "mod_fwd", "nn", (1, n_mod_shard // MOD_TILE, 1),
        c_all, (N_DEV, D_MODEL), lambda i, j, k: (0, 0),
        w_mod[0], (D_MODEL, MOD_TILE), lambda i, j, k: (0, j),
        _sds((N_DEV, n_mod_shard), F32), (N_DEV, MOD_TILE), lambda i, j, k: (0, j),
        (N_DEV, MOD_TILE),
    )
    mod_all = _allgather_small("allgather_mod", mod_cols).reshape(N_CHIPS, 2, N_DEV, n_mod_shard)[:, 0]
    mod_all = mod_all.transpose(1, 0, 2).reshape(N_DEV, N_MOD * D_MODEL)
    mod = lax.dynamic_slice_in_dim(mod_all, dev, 1, axis=0) + b_mod

    norms = (norm_ffn1, norm_mix, norm_ffn2, norm_final.reshape(1, D_MODEL))

    ffn2_src = _GatherBehind("gather_ffn2", own_ffn2)
    mix_src = _GatherBehind("gather_mix", own_mix)
    down1_src = _GatherBehind("gather_down1", own_down1, then=[mix_src, ffn2_src])
    gu1_src = _GatherBehind("gather_gu1", own_gu1, then=[down1_src])
    gu1_src.start([mod_all])

    reducing, red = {}, {}

    def begin(tag, gb, after):
        reducing[tag] = _ReduceBehind(tag, gb, after=after)
        return [reducing[tag].token]

    def ffn2_gu_done(gb):
        return begin("ffn2_gu", gb, reducing["ffn2_down"].mid(gb))

    def mix_done(dx1, gb_rest):
        red["ffn2_down"] = reducing["ffn2_down"].get([dx1])
        red["ffn2_gu"] = reducing["ffn2_gu"].get([red["ffn2_down"]])
        return begin("rest", gb_rest, [red["ffn2_gu"]])

    def ffn1_gu_done(gb):
        red["rest"] = reducing["rest"].get([gb])
        begin("ffn1_gu", gb, reducing["ffn1_down"].mid(red["rest"]))
        return reducing["ffn1_gu"].mid(reducing["ffn1_gu"].token)

    loss_part, grad_x, _, dmod, small = _device_step(
        x[0], loss_target[0], mod, gu1_src, down1_src, mix_src, ffn2_src, norms, q_norm, kv_norm, sinks, rel_bias,
        hooks2=_BwdHooks(after_wdown=lambda gb: begin("ffn2_down", gb, ()), after_wgu=ffn2_gu_done),
        hooks_mix=_BwdHooks(after_gate=lambda dz: reducing["ffn2_gu"].mid(dz)),
        mix_done=mix_done,
        hooks1=_BwdHooks(
            after_swiglu=lambda dab3: reducing["rest"].mid(dab3),
            after_wdown=lambda gb: begin("ffn1_down", gb, ()),
            after_wgu=ffn1_gu_done,
        ),
    )
    loss = lax.psum(loss_part, ("x", "y", "c"))

    gathered, total = _allreduce_small(_pack_small(dmod, *small))
    (g_b_mod, g_n1, g_n2, g_n3, g_nf, g_qn, g_kvn, g_sinks, g_rel) = _unpack_small(total)
    dmod_all = gathered[:, :PK_MOD].reshape(N_DEV, N_MOD * D_MODEL)
    dmod_cols = lax.dynamic_slice_in_dim(dmod_all, chip * n_mod_shard, n_mod_shard, axis=1)
    g_w_mod = _mm(
        "mod_bwd", "tn", (1, n_mod_shard // MOD_TILE, 1),
        c_all, (N_DEV, D_MODEL), lambda i, j, k: (0, 0),
        dmod_cols, (N_DEV, MOD_TILE), lambda i, j, k: (0, j),
        _sds((D_MODEL, n_mod_shard), F32), (D_MODEL, MOD_TILE), lambda i, j, k: (0, j),
        (D_MODEL, MOD_TILE),
    )

    r_rest = red["rest"][0]
    transposed = {"ffn1_gate", "ffn1_up", "ffn2_gate", "ffn2_up", "w_in", "w_uq", "w_ukv"}
    g_big = {
        "ffn2_gate": red["ffn2_gu"][0], "ffn2_up": red["ffn2_gu"][1], "ffn2_down": red["ffn2_down"][0],
        "w_in": r_rest[O_IN:O_IN + R_IN],
        "w_o": r_rest[O_O:O_O + R_O],
        "w_uq": r_rest[O_UQ:O_UQ + R_UQ].reshape(MLA_QK, MLA_Q_RANK),
        "w_ukv": r_rest[O_UKV:O_UKV + R_UKV].reshape(MLA_NOPE + MLA_V, MLA_KV_RANK),
        "w_mod": g_w_mod,
    }
    w_big = {
        "ffn2_gate": (ffn2_gate, m_ffn2_gate, v_ffn2_gate),
        "ffn2_up": (ffn2_up, m_ffn2_up, v_ffn2_up), "ffn2_down": (ffn2_down, m_ffn2_down, v_ffn2_down),
        "w_in": (w_in, m_w_in, v_w_in), "w_o": (w_o, m_w_o, v_w_o), "w_uq": (w_uq, m_w_uq, v_w_uq),
        "w_ukv": (w_ukv, m_w_ukv, v_w_ukv), "w_mod": (w_mod, m_w_mod, v_w_mod),
    }
    grads, deltas, new_m, new_v = {}, {}, {}, {}

    def update(names):
        for nm in names:
            w, m, v = w_big[nm]
            if nm in transposed:
                outs = _adamw(f"adamw_{nm}", w[0].T, g_big[nm], m[0].T, v[0].T)
                g_, d_, m_, v_ = [a.T for a in (g_big[nm],) + tuple(outs)]
            else:
                g_, (d_, m_, v_) = g_big[nm], _adamw(f"adamw_{nm}", w[0], g_big[nm], m[0], v[0])
            grads[nm], deltas[nm], new_m[nm], new_v[nm] = g_[None], d_[None], m_[None], v_[None]

    update(list(w_big))
    red1_down = reducing["ffn1_down"].get([deltas[nm] for nm in w_big])
    red1_gu = reducing["ffn1_gu"].get([red1_down])
    g_big.update({"ffn1_gate": red1_gu[0], "ffn1_up": red1_gu[1], "ffn1_down": red1_down[0]})
    w_big.update({
        "ffn1_gate": (ffn1_gate, m_ffn1_gate, v_ffn1_gate), "ffn1_up": (ffn1_up, m_ffn1_up, v_ffn1_up),
        "ffn1_down": (ffn1_down, m_ffn1_down, v_ffn1_down),
    })
    update(["ffn1_gate", "ffn1_up", "ffn1_down"])

    small_names = ["b_mod", "norm_ffn1", "norm_mix", "norm_ffn2", "norm_final", "q_norm", "kv_norm", "sinks", "rel_bias"]
    w_small = (b_mod, norm_ffn1, norm_mix, norm_ffn2, norm_final, q_norm, kv_norm, sinks, rel_bias)
    m_small = (m_b_mod, m_norm_ffn1, m_norm_mix, m_norm_ffn2, m_norm_final, m_q_norm, m_kv_norm, m_sinks, m_rel_bias)
    v_small = (v_b_mod, v_norm_ffn1, v_norm_mix, v_norm_ffn2, v_norm_final, v_q_norm, v_kv_norm, v_sinks, v_rel_bias)
    g_small = (g_b_mod, g_n1, g_n2, g_n3, g_nf, g_qn, g_kvn, g_sinks, g_rel)
    d_s, m_s, v_s = _adamw_small(w_small, g_small, m_small, v_small)
    for nm, g_, d_, m_, v_ in zip(small_names, g_small, d_s, m_s, v_s):
        grads[nm], deltas[nm], new_m[nm], new_v[nm] = g_, d_, m_, v_

    order = ["w_mod", "b_mod", "norm_ffn1", "ffn1_gate", "ffn1_up", "ffn1_down", "norm_mix", "w_in", "q_norm", "kv_norm",
             "w_uq", "w_ukv", "sinks", "w_o", "norm_ffn2", "ffn2_gate", "ffn2_up", "ffn2_down", "rel_bias", "norm_final"]
    return (loss, grad_x[None], *[grads[n] for n in order], *[deltas[n] for n in order],
            *[new_m[n] for n in order], *[new_v[n] for n in order])
```

```python
import functools
import math

import jax
import jax.numpy as jnp
import numpy as np
from jax import lax
from jax.experimental import pallas as pl
from jax.experimental.pallas import tpu as pltpu

F32, BF16 = jnp.float32, jnp.bfloat16

D_MODEL = 1024
D_FF = 2816
EPS = 1e-6
N_MOD = 9
SWA_HEADS, SWA_KV_HEADS, SWA_HEAD_DIM, WINDOW = 8, 2, 64, 128
SWA_GROUP = SWA_HEADS // SWA_KV_HEADS
MLA_HEADS, MLA_Q_RANK, MLA_KV_RANK, MLA_NOPE, MLA_ROPE, MLA_V = 4, 256, 128, 128, 64, 128
MLA_QK = MLA_NOPE + MLA_ROPE
ROPE_THETA = 10000.0
NUM_BUCKETS, MAX_DISTANCE = 32, 128
D_IN = 1216
N_SWA_COLS = 768
N_MLA_COLS = 512

ADAM_LR, ADAM_B1, ADAM_B2, ADAM_EPS, ADAM_WD, ADAM_STEP = 0.001, 0.9, 0.999, 1e-08, 0.01, 10

N_CHIPS = 4
N_DEV = 8
F_SHARD = D_FF // N_CHIPS
HALF = F_SHARD // 2
R_IN, R_O, R_UQ, R_UKV = 304, 256, 48, 32
O_IN, O_O, O_UQ, O_UKV, O_PAD = 0, 304, 560, 608, 640

PK_MOD = 72
PK_ROWS = 112
VMEM_LIMIT = 56 << 20
ROW_TILE = 2048
ELEM_TILE = 512
MOD_TILE = 768

_DN = {
    "nn": (((1,), (0,)), ((), ())),
    "nt": (((1,), (1,)), ((), ())),
    "tn": (((0,), (0,)), ((), ())),
}


def _sds(shape, dtype):
    return jax.ShapeDtypeStruct(tuple(shape), dtype)


def _cparams(sem=None):
    kw = {"vmem_limit_bytes": VMEM_LIMIT}
    if sem is not None:
        kw["dimension_semantics"] = sem
    return pltpu.CompilerParams(**kw)


def _dot(a, b, dims):
    return lax.dot_general(a.astype(BF16), b.astype(BF16), _DN[dims], preferred_element_type=F32)


def _mm(name, dims, grid, a, a_blk, a_idx, b, b_blk, b_idx, out, out_blk, out_idx, acc_shape, alias=None, deps=()):
    nk = grid[2]

    def body(*refs):
        a_ref, b_ref = refs[:2]
        o_ref, acc_ref = refs[-2:]
        part = _dot(a_ref[...], b_ref[...], dims)
        if nk == 1:
            o_ref[...] = part.astype(o_ref.dtype)
            return
        k = pl.program_id(2)

        @pl.when(k == 0)
        def _():
            acc_ref[...] = part

        @pl.when((k > 0) & (k < nk - 1))
        def _():
            acc_ref[...] += part

        @pl.when(k == nk - 1)
        def _():
            o_ref[...] = (acc_ref[...] + part).astype(o_ref.dtype)

    in_specs = [pl.BlockSpec(a_blk, a_idx), pl.BlockSpec(b_blk, b_idx)]
    args = [a, b]
    kw = {}
    if alias is not None:
        in_specs.append(pl.BlockSpec(memory_space=pl.ANY))
        args.append(alias)
        kw["input_output_aliases"] = {2: 0}
    in_specs += [pl.BlockSpec(memory_space=pl.ANY)] * len(deps)
    args += list(deps)
    return pl.pallas_call(
        body,
        name=name,
        grid=grid,
        in_specs=in_specs,
        out_specs=pl.BlockSpec(out_blk, out_idx),
        out_shape=out,
        scratch_shapes=[pltpu.VMEM(acc_shape if nk > 1 else (8, 128), F32)],
        compiler_params=_cparams(("parallel", "parallel", "arbitrary")),
        **kw,
    )(*args)


def _rowwise(name, fn, tiled, full, out_tiled, out_red, tm, deps=()):
    rows = tiled[0].shape[-2]
    tm = min(tm, rows)
    grid = (rows // tm,)
    n_in = len(tiled) + len(full)
    n_t = len(out_tiled)
    n_dep = len(deps)

    def tspec(shape):
        nd = len(shape)
        return pl.BlockSpec(tuple(shape[:-2]) + (tm, shape[-1]), lambda i, nd=nd: (0,) * (nd - 2) + (i, 0))

    def fspec(shape):
        nd = len(shape)
        return pl.BlockSpec(tuple(shape), lambda i, nd=nd: (0,) * nd)

    def body(*refs):
        outs = fn(*[r[...] for r in refs[:n_in]])
        if not isinstance(outs, (tuple, list)):
            outs = (outs,)
        out_refs = refs[n_in + n_dep:]
        for r, v in zip(out_refs[:n_t], outs[:n_t]):
            r[...] = v.astype(r.dtype)
        red_refs = out_refs[n_t:]
        if red_refs:
            @pl.when(pl.program_id(0) == 0)
            def _():
                for r in red_refs:
                    r[...] = jnp.zeros_like(r)

            for r, v in zip(red_refs, outs[n_t:]):
                r[...] += v.astype(r.dtype)

    res = pl.pallas_call(
        body,
        name=name,
        grid=grid,
        in_specs=[tspec(a.shape) for a in tiled] + [fspec(a.shape) for a in full]
        + [pl.BlockSpec(memory_space=pl.ANY)] * n_dep,
        out_specs=[tspec(o.shape) for o in out_tiled] + [fspec(o.shape) for o in out_red],
        out_shape=list(out_tiled) + list(out_red),
        compiler_params=_cparams(("arbitrary",) if out_red else ("parallel",)),
    )(*tiled, *full, *deps)
    return res


def _sum0(v):
    return jnp.sum(v, axis=0, keepdims=True)


def _sigmoid(v):
    return 1.0 / (1.0 + jnp.exp(-v))


def _rms(x):
    r = lax.rsqrt(jnp.mean(x * x, axis=-1, keepdims=True) + EPS)
    return x * r, r


def _rms_bwd(u, xr, r):
    return r * (u - xr * jnp.mean(u * xr, axis=-1, keepdims=True))


def _normmod(x_, gamma_, sc_, sh_):
    return _rms(x_)[0] * gamma_ * (1.0 + sc_) + sh_


def _row_blocks(tm, size=512):
    size = min(size, tm)
    return [slice(r, r + size) for r in range(0, tm, size)]


def _loss_head(x_, t_, g_):
    xr, r = _rms(x_)
    err = xr * g_ - t_
    dy = err * (1.0 / D_MODEL)
    return _rms_bwd(dy * g_, xr, r), _sum0(err * err), _sum0(dy * xr)


def _ffn_fwd(tag, x, gamma, sh, sc, gate, gu_src, base, down_src, down_idx, mid_after, h_pre=None,
             next_norm=None, head=None, next_src=None):
    s_len = x.shape[0]
    if h_pre is None:
        (h,) = _rowwise(
            f"{tag}_normmod", _normmod, [x], [gamma, sc, sh], [_sds((s_len, D_MODEL), BF16)], [], ELEM_TILE
        )
        gdeps = gu_src.mid(h)
    else:
        h, gdeps = h_pre, gu_src.mid(mid_after)
    g4 = gu_src.get(h)

    tm = min(ROW_TILE, s_len)
    def gate_up(h_ref, wg_ref, wu_ref, *rest):
        ab_ref, s_ref = rest[-2:]
        wg, wu = wg_ref[...], wu_ref[...]
        for rows in _row_blocks(tm):
            hv = h_ref[rows, :]
            a = _dot(hv, wg, "nt")
            b = _dot(hv, wu, "nt")
            ab_ref[0, rows, :] = a.astype(ab_ref.dtype)
            ab_ref[1, rows, :] = b.astype(ab_ref.dtype)
            s_ref[rows, :] = (a * _sigmoid(a) * b).astype(s_ref.dtype)

    ab4, s3 = pl.pallas_call(
        gate_up,
        name=f"{tag}_gate_up",
        grid=(s_len // tm, N_CHIPS),
        in_specs=[
            pl.BlockSpec((tm, D_MODEL), lambda i, c: (i, 0)),
            pl.BlockSpec((None, None, F_SHARD, D_MODEL), lambda i, c: (c, base, 0, 0)),
            pl.BlockSpec((None, None, F_SHARD, D_MODEL), lambda i, c: (c, base + 1, 0, 0)),
        ] + [pl.BlockSpec(memory_space=pl.ANY)] * len(gdeps),
        out_specs=[
            pl.BlockSpec((None, 2, tm, F_SHARD), lambda i, c: (c, 0, i, 0)),
            pl.BlockSpec((None, tm, F_SHARD), lambda i, c: (c, i, 0)),
        ],
        out_shape=[_sds((N_CHIPS, 2, s_len, F_SHARD), BF16), _sds((N_CHIPS, s_len, F_SHARD), BF16)],
        compiler_params=_cparams(("parallel", "parallel")),
    )(h, g4, g4, *gdeps)
    if down_src is None:
        g_down, ddeps = g4, ()
    else:
        ddeps = down_src.mid(ab4)
        g_down = down_src.get(s3)

    y = _mm(
        f"{tag}_down", "nn", (s_len // tm, 1, N_CHIPS),
        s3, (None, tm, F_SHARD), lambda i, j, k: (k, i, 0),
        g_down, (None, None, F_SHARD, D_MODEL), lambda i, j, k: (k, down_idx, 0, 0),
        _sds((s_len, D_MODEL), F32), (tm, D_MODEL), lambda i, j, k: (i, 0),
        (tm, D_MODEL), deps=ddeps,
    )

    saved = (g4, base, g_down, down_idx, h, ab4, s3, y)
    act = _sds((s_len, D_MODEL), F32)
    if head is not None:
        target, norm_final = head
        vec = _sds((1, D_MODEL), F32)
        out = _rowwise(
            f"{tag}_residual_head", lambda x_, y_, t_, g_, nf_: _loss_head(x_ + 0.5 * g_ * y_, t_, nf_),
            [x, y, target], [gate, norm_final], [act], [vec, vec], ELEM_TILE,
        )
    elif next_norm is not None:
        def residual_norm(x_, y_, g_, gamma_, sc_, sh_):
            x_out = x_ + 0.5 * g_ * y_
            return x_out, _normmod(x_out, gamma_, sc_, sh_)

        out = _rowwise(
            f"{tag}_residual_norm", residual_norm, [x, y], [gate] + list(next_norm),
            [act, _sds((s_len, D_MODEL), BF16)], [], ELEM_TILE,
            deps=next_src.mid(y) if next_src is not None else (),
        )
    else:
        out = _rowwise(f"{tag}_residual", lambda x_, y_, g_: x_ + 0.5 * g_ * y_, [x, y], [gate], [act], [], ELEM_TILE)
    return out, saved


def _normmod_bwd_call(name, dh_parts, x, dxo, gamma, sc, deps=()):
    s_len = x.shape[0]
    n_parts = len(dh_parts)

    def fn(*vals):
        dh = vals[0]
        for extra in vals[1:n_parts]:
            dh = dh + extra
        x_, dxo_, gamma_, sc_ = vals[n_parts:]
        xr, r = _rms(x_)
        n = xr * gamma_
        dn = dh * (1.0 + sc_)
        dx = dxo_ + _rms_bwd(dn * gamma_, xr, r)
        return dx, _sum0(dh * n), _sum0(dh), _sum0(dn * xr)

    vec = _sds((1, D_MODEL), F32)
    return _rowwise(
        name, fn, list(dh_parts) + [x, dxo], [gamma, sc], [_sds((s_len, D_MODEL), F32)], [vec, vec, vec], ELEM_TILE, deps=deps
    )


class _BwdHooks:
    def __init__(self, after_gate=None, after_swiglu=None, after_wdown=None, after_wgu=None, after_dh=None):
        def nothing(_):
            return []

        self.after_gate = after_gate or nothing
        self.after_swiglu = after_swiglu or nothing
        self.after_wdown = after_wdown or nothing
        self.after_wgu = after_wgu or nothing
        self.after_dh = after_dh or nothing


def _ffn_bwd(tag, dxo, x, gamma, sc, gate, saved, hooks, deps=()):
    g4, base, g_down, down_idx, h, ab4, s3, y = saved
    s_len = x.shape[0]
    vec = _sds((1, D_MODEL), F32)

    dy, dgate = _rowwise(
        f"{tag}_bwd_gate", lambda dxo_, y_, g_: (0.5 * g_ * dxo_, _sum0(0.5 * y_ * dxo_)),
        [dxo, y], [gate], [_sds((s_len, D_MODEL), BF16)], [vec], ELEM_TILE, deps=deps,
    )

    tm = min(ROW_TILE, s_len)

    def d_gate_up(dy_ref, wd_ref, ab_ref, o_ref):
        wd = wd_ref[...]
        for rows in _row_blocks(tm):
            ds = _dot(dy_ref[rows, :], wd, "nt")
            a = ab_ref[0, rows, :].astype(F32)
            b = ab_ref[1, rows, :].astype(F32)
            sig = _sigmoid(a)
            o_ref[0, rows, :] = (ds * b * sig * (1.0 + a * (1.0 - sig))).astype(o_ref.dtype)
            o_ref[1, rows, :] = (ds * a * sig).astype(o_ref.dtype)

    ab_spec = pl.BlockSpec((None, 2, tm, F_SHARD), lambda i, c: (c, 0, i, 0))
    dab4 = pl.pallas_call(
        d_gate_up,
        name=f"{tag}_bwd_dgate_up",
        grid=(s_len // tm, N_CHIPS),
        in_specs=[
            pl.BlockSpec((tm, D_MODEL), lambda i, c: (i, 0)),
            pl.BlockSpec((None, None, F_SHARD, D_MODEL), lambda i, c: (c, down_idx, 0, 0)),
            ab_spec,
        ],
        out_specs=ab_spec,
        out_shape=_sds(ab4.shape, BF16),
        compiler_params=_cparams(("parallel", "parallel")),
    )(dy, g_down, ab4)

    tk = tm
    gb_down = _mm(
        f"{tag}_bwd_wdown", "tn", (N_CHIPS, 1, s_len // tk),
        s3, (None, tk, F_SHARD), lambda i, j, k: (i, k, 0),
        dy, (tk, D_MODEL), lambda i, j, k: (k, 0),
        _sds((N_CHIPS, 1, F_SHARD, D_MODEL), BF16), (None, None, F_SHARD, D_MODEL), lambda i, j, k: (i, 0, 0, 0),
        (F_SHARD, D_MODEL), deps=hooks.after_swiglu(dab4),
    )
    gb_gu = _mm(
        f"{tag}_bwd_wgu", "tn", (2 * N_CHIPS, 1, s_len // tk),
        dab4, (None, None, tk, F_SHARD), lambda i, j, k: (i % N_CHIPS, i // N_CHIPS, k, 0),
        h, (tk, D_MODEL), lambda i, j, k: (k, 0),
        _sds((N_CHIPS, 2, F_SHARD, D_MODEL), BF16), (None, None, F_SHARD, D_MODEL),
        lambda i, j, k: (i % N_CHIPS, i // N_CHIPS, 0, 0),
        (F_SHARD, D_MODEL), deps=hooks.after_wdown(gb_down),
    )
    assert base % 2 == 0
    dh_deps = hooks.after_wgu(gb_gu)

    def d_h(dab_ref, w_ref, *rest):
        o_ref, acc_ref = rest[-2:]
        c = pl.program_id(1)
        part = _dot(dab_ref[0], w_ref[0], "nn") + _dot(dab_ref[1], w_ref[1], "nn")

        @pl.when(c == 0)
        def _():
            acc_ref[...] = part

        @pl.when((c > 0) & (c < N_CHIPS - 1))
        def _():
            acc_ref[...] += part

        @pl.when(c == N_CHIPS - 1)
        def _():
            o_ref[...] = acc_ref[...] + part

    dh = pl.pallas_call(
        d_h,
        name=f"{tag}_bwd_dh",
        grid=(s_len // tm, N_CHIPS),
        in_specs=[
            pl.BlockSpec((None, 2, tm, F_SHARD), lambda i, c: (c, 0, i, 0)),
            pl.BlockSpec((None, 2, F_SHARD, D_MODEL), lambda i, c: (c, base // 2, 0, 0)),
        ] + [pl.BlockSpec(memory_space=pl.ANY)] * len(dh_deps),
        out_specs=pl.BlockSpec((tm, D_MODEL), lambda i, c: (i, 0)),
        out_shape=_sds((s_len, D_MODEL), F32),
        scratch_shapes=[pltpu.VMEM((tm, D_MODEL), F32)],
        compiler_params=_cparams(("parallel", "arbitrary")),
    )(dab4, g4, *dh_deps)
    dx, dsc, dsh, dgamma = _normmod_bwd_call(
        f"{tag}_bwd_normmod", [dh], x, dxo, gamma, sc, deps=hooks.after_dh(dh)
    )
    return dx, (gb_down, gb_gu), (dsh, dsc, dgate, dgamma)


def _bucket_map():
    qi = np.arange(WINDOW)[:, None]
    kj = np.arange(2 * WINDOW)[None, :]
    dist = qi + WINDOW - kj
    band = (dist >= 0) & (dist < WINDOW)
    max_exact = NUM_BUCKETS // 2
    n = np.maximum(dist, 0)
    large = []
    for dt in (np.float32, np.float64):
        nf = np.maximum(n, 1).astype(dt)
        val = np.log(nf / dt(max_exact)) / dt(math.log(MAX_DISTANCE / max_exact)) * dt(NUM_BUCKETS - max_exact)
        large.append(np.minimum(max_exact + val.astype(np.int32), NUM_BUCKETS - 1))
    assert np.array_equal(large[0], large[1])
    bucket = np.where(n < max_exact, n, large[0])
    return np.where(band, bucket, -1).astype(np.int32).reshape(1, -1)


def _bias_expand(rel_bias_t, bkt):
    n = bkt.shape[1]

    def body(rb_ref, bkt_ref, o_ref):
        onehot = (lax.broadcasted_iota(jnp.int32, (NUM_BUCKETS, n), 0) == bkt_ref[...]).astype(F32)
        o_ref[...] = lax.dot_general(
            rb_ref[...], onehot, _DN["nn"], precision=lax.Precision.HIGHEST, preferred_element_type=F32
        )

    return pl.pallas_call(
        body, name="bias_expand", out_shape=_sds((SWA_HEADS, n), F32), compiler_params=_cparams()
    )(rel_bias_t, bkt)


def _bias_reduce(dbias_flat, bkt):
    n = bkt.shape[1]

    def body(db_ref, bkt_ref, o_ref):
        onehot = (lax.broadcasted_iota(jnp.int32, (NUM_BUCKETS, n), 0) == bkt_ref[...]).astype(F32)
        o_ref[...] = lax.dot_general(
            db_ref[...], onehot, _DN["nt"], precision=lax.Precision.HIGHEST, preferred_element_type=F32
        )

    return pl.pallas_call(
        body, name="bias_reduce", out_shape=_sds((SWA_HEADS, NUM_BUCKETS), F32), compiler_params=_cparams()
    )(dbias_flat, bkt)


_SWA_SCALE = SWA_HEAD_DIM ** -0.5
_NEG = -1e30


_SWA_PAIR = 2


def _swa_in_specs():
    w = WINDOW
    n_q = SWA_HEADS * SWA_HEAD_DIM
    n_kv = 2 * SWA_KV_HEADS * SWA_HEAD_DIM
    prev = lambda m: jnp.maximum(_SWA_PAIR * m - 1, 0)
    return [
        pl.BlockSpec((_SWA_PAIR * w, n_q), lambda m: (m, 0)),
        pl.BlockSpec((w, n_kv), lambda m: (prev(m), n_q // n_kv)),
        pl.BlockSpec((_SWA_PAIR * w, n_kv), lambda m: (m, n_q // n_kv)),
        pl.BlockSpec((SWA_HEADS, w, 2 * w), lambda m: (0, 0, 0)),
        pl.BlockSpec((SWA_HEADS, w, 1), lambda m: (0, 0, 0)),
    ]


def _head_cols(v, first, count):
    hd = SWA_HEAD_DIM
    return jnp.stack([v[:, (first + i) * hd:(first + i + 1) * hd] for i in range(count)])


def _swa_blocks(q_ref, kvp_ref, kvc_ref, m):
    g, w, hd = SWA_GROUP, WINDOW, SWA_HEAD_DIM
    kv_all = jnp.concatenate([kvp_ref[...], kvc_ref[...]], axis=0).astype(F32)
    blocks = []
    for sub in range(_SWA_PAIR):
        rows = slice(sub * w, (sub + 1) * w)
        q = q_ref[rows, :].astype(F32)
        kv = kv_all[sub * w:(sub + 2) * w]
        heads = []
        for j in range(SWA_KV_HEADS):
            qj = _head_cols(q, g * j, g).reshape(g * w, hd)
            heads.append((qj, _head_cols(kv, j, 1)[0], _head_cols(kv, SWA_KV_HEADS + j, 1)[0]))
        blocks.append((_SWA_PAIR * m + sub, rows, heads))
    return blocks


def _swa_scores(q, kk, bias, n):
    g, w = SWA_GROUP, WINDOW
    s = (_dot(q, kk, "nt") * _SWA_SCALE).reshape(g, w, 2 * w) + bias
    qi = lax.broadcasted_iota(jnp.int32, (w, 2 * w), 0)
    kj = lax.broadcasted_iota(jnp.int32, (w, 2 * w), 1)
    dist = qi + w - kj
    valid = ((dist >= 0) & (dist < w) & ((n > 0) | (kj >= w)))[None]
    return jnp.where(valid, s, _NEG), valid


def _swa_fwd(swa2, bias, sinkb):
    s_len = swa2.shape[0]
    g, w, hd = SWA_GROUP, WINDOW, SWA_HEAD_DIM

    def body(q_ref, kvp_ref, kvc_ref, bias_ref, sink_ref, o_ref, lse_ref):
        for n, rows, heads in _swa_blocks(q_ref, kvp_ref, kvc_ref, pl.program_id(0)):
            outs = []
            for j, (q, kk, vv) in enumerate(heads):
                hs = slice(g * j, g * (j + 1))
                s, valid = _swa_scores(q, kk, bias_ref[hs], n)
                sink = sink_ref[hs]
                m = jnp.maximum(jnp.max(s, axis=-1, keepdims=True), sink)
                p = jnp.where(valid, jnp.exp(s - m), 0.0)
                l = jnp.sum(p, axis=-1, keepdims=True) + jnp.exp(sink - m)
                o = _dot(p.reshape(g * w, 2 * w), vv, "nn").reshape(g, w, hd) / l
                outs += [o[i] for i in range(g)]
                lse_ref[hs, rows, :] = m + jnp.log(l)
            o_ref[rows, :] = jnp.concatenate(outs, axis=-1).astype(o_ref.dtype)

    n_q = SWA_HEADS * hd
    return pl.pallas_call(
        body,
        name="swa_fwd",
        grid=(s_len // (_SWA_PAIR * w),),
        in_specs=_swa_in_specs(),
        out_specs=[
            pl.BlockSpec((_SWA_PAIR * w, n_q), lambda m: (m, 0)),
            pl.BlockSpec((SWA_HEADS, _SWA_PAIR * w, 1), lambda m: (0, m, 0)),
        ],
        out_shape=[_sds((s_len, n_q), BF16), _sds((SWA_HEADS, s_len, 1), F32)],
        compiler_params=_cparams(("parallel",)),
    )(swa2, swa2, swa2, bias, sinkb)


def _swa_bwd(swa2, bias, sinkb, cat, dcat, lse):
    s_len = swa2.shape[0]
    g, w, hd = SWA_GROUP, WINDOW, SWA_HEAD_DIM

    def body(q_ref, kvp_ref, kvc_ref, bias_ref, sink_ref, o_ref, do_ref, lse_ref,
             dq_ref, dkva_ref, dkvb_ref, dbias_ref, dsink_ref):
        step = pl.program_id(0)

        @pl.when(step == 0)
        def _():
            dbias_ref[...] = jnp.zeros_like(dbias_ref)
            dsink_ref[...] = jnp.zeros_like(dsink_ref)

        for n, rows, heads in _swa_blocks(q_ref, kvp_ref, kvc_ref, step):
            o_all = o_ref[rows, :].astype(F32)
            do_all = do_ref[rows, :].astype(F32)
            dqs, dks, dvs = [], [], []
            for j, (q, kk, vv) in enumerate(heads):
                hs = slice(g * j, g * (j + 1))
                s, valid = _swa_scores(q, kk, bias_ref[hs], n)
                lse_v = lse_ref[hs, rows, :]
                p = jnp.where(valid, jnp.exp(s - lse_v), 0.0)
                do = _head_cols(do_all, g * j, g)
                delta = jnp.sum(do * _head_cols(o_all, g * j, g), axis=-1, keepdims=True)
                do2 = do.reshape(g * w, hd)
                dp = _dot(do2, vv, "nt").reshape(g, w, 2 * w)
                ds = p * (dp - delta)
                dbias_ref[hs] += ds
                dsink_ref[hs] -= jnp.exp(sink_ref[hs] - lse_v) * delta
                ds2 = ds.reshape(g * w, 2 * w)
                dq = (_dot(ds2, kk, "nn") * _SWA_SCALE).reshape(g, w, hd)
                dqs += [dq[i] for i in range(g)]
                dks.append(_dot(ds2, q, "tn") * _SWA_SCALE)
                dvs.append(_dot(p.reshape(g * w, 2 * w), do2, "tn"))
            dq_ref[rows, :] = jnp.concatenate(dqs, axis=-1).astype(dq_ref.dtype)
            dkv = jnp.concatenate(dks + dvs, axis=-1)
            dkvb_ref[rows, :] = dkv[:w]
            dkva_ref[rows, :] = dkv[w:]

    n_q = SWA_HEADS * hd
    n_kv = 2 * SWA_KV_HEADS * hd
    q_spec = pl.BlockSpec((_SWA_PAIR * w, n_q), lambda m: (m, 0))
    kv_spec = pl.BlockSpec((_SWA_PAIR * w, n_kv), lambda m: (m, 0))
    return pl.pallas_call(
        body,
        name="swa_bwd",
        grid=(s_len // (_SWA_PAIR * w),),
        in_specs=_swa_in_specs() + [q_spec, q_spec, pl.BlockSpec((SWA_HEADS, _SWA_PAIR * w, 1), lambda m: (0, m, 0))],
        out_specs=[
            q_spec, kv_spec, kv_spec,
            pl.BlockSpec((SWA_HEADS, w, 2 * w), lambda n: (0, 0, 0)),
            pl.BlockSpec((SWA_HEADS, w, 1), lambda n: (0, 0, 0)),
        ],
        out_shape=[
            _sds((s_len, n_q), BF16), _sds((s_len, n_kv), F32), _sds((s_len, n_kv), F32),
            _sds((SWA_HEADS, w, 2 * w), F32), _sds((SWA_HEADS, w, 1), F32),
        ],
        compiler_params=_cparams(("arbitrary",)),
    )(swa2, swa2, swa2, bias, sinkb, cat, dcat, lse)


_MLA_SCALE = MLA_QK ** -0.5
_MLA_TQ = 256
_MLA_FWD_HEADS = 4
_MLA_BWD_HEADS = 2


def _mla_mask(i, tq, n_keys):
    row = i * tq + lax.broadcasted_iota(jnp.int32, (tq, n_keys), 0)
    col = lax.broadcasted_iota(jnp.int32, (tq, n_keys), 1)
    return col <= row


def _per_query_block(i, n_blocks, fn):
    for ii in range(n_blocks):
        pl.when(i == ii)(functools.partial(fn, ii))


def _mla_fwd(q4, k4, v4):
    s_len = q4.shape[1]
    tq = min(_MLA_TQ, s_len)
    pair = _MLA_FWD_HEADS

    def body(q_ref, k_ref, v_ref, o_ref, lse_ref):
        def block(ii):
            n_keys = (ii + 1) * tq
            mask = _mla_mask(ii, tq, n_keys)
            for hh in range(pair):
                s = jnp.where(mask, _dot(q_ref[hh], k_ref[hh, :n_keys, :], "nt") * _MLA_SCALE, _NEG)
                m = jnp.max(s, axis=-1, keepdims=True)
                p = jnp.exp(s - m)
                l = jnp.sum(p, axis=-1, keepdims=True)
                o_ref[:, hh * MLA_V:(hh + 1) * MLA_V] = (_dot(p, v_ref[hh, :n_keys, :], "nn") / l).astype(o_ref.dtype)
                lse_ref[hh] = m + jnp.log(l)

        _per_query_block(pl.program_id(1), s_len // tq, block)

    return pl.pallas_call(
        body,
        name="mla_fwd",
        grid=(MLA_HEADS // pair, s_len // tq),
        in_specs=[
            pl.BlockSpec((pair, tq, MLA_QK), lambda h, i: (h, i, 0)),
            pl.BlockSpec((pair, s_len, MLA_QK), lambda h, i: (h, 0, 0)),
            pl.BlockSpec((pair, s_len, MLA_V), lambda h, i: (h, 0, 0)),
        ],
        out_specs=[
            pl.BlockSpec((tq, pair * MLA_V), lambda h, i: (i, h)),
            pl.BlockSpec((pair, tq, 1), lambda h, i: (h, i, 0)),
        ],
        out_shape=[_sds((s_len, MLA_HEADS * MLA_V), BF16), _sds((MLA_HEADS, s_len, 1), F32)],
        compiler_params=_cparams(("parallel", "parallel")),
    )(q4, k4, v4)


def _mla_bwd(q4, k4, v4, cat, dcat, lse):
    s_len = q4.shape[1]
    tq = min(_MLA_TQ, s_len)
    pair = _MLA_BWD_HEADS
    first = SWA_HEADS * SWA_HEAD_DIM // (pair * MLA_V)

    def body(q_ref, k_ref, v_ref, o_ref, do_ref, lse_ref, dq_ref, dk_ref, dv_ref):
        i = pl.program_id(1)

        @pl.when(i == 0)
        def _():
            dk_ref[...] = jnp.zeros_like(dk_ref)
            dv_ref[...] = jnp.zeros_like(dv_ref)

        def block(ii):
            n_keys = (ii + 1) * tq
            mask = _mla_mask(ii, tq, n_keys)
            for hh in range(pair):
                cols = slice(hh * MLA_V, (hh + 1) * MLA_V)
                q, k, v, do = q_ref[hh], k_ref[hh, :n_keys, :], v_ref[hh, :n_keys, :], do_ref[:, cols]
                s = jnp.where(mask, _dot(q, k, "nt") * _MLA_SCALE, _NEG)
                p = jnp.where(mask, jnp.exp(s - lse_ref[hh]), 0.0)
                delta = jnp.sum(do.astype(F32) * o_ref[:, cols].astype(F32), axis=-1, keepdims=True)
                ds = (p * (_dot(do, v, "nt") - delta) * _MLA_SCALE).astype(BF16)
                dq_ref[hh] = _dot(ds, k, "nn")
                dk_ref[hh, :n_keys, :] += _dot(ds, q, "tn")
                dv_ref[hh, :n_keys, :] += _dot(p, do, "tn")

        _per_query_block(i, s_len // tq, block)

    return pl.pallas_call(
        body,
        name="mla_bwd",
        grid=(MLA_HEADS // pair, s_len // tq),
        in_specs=[
            pl.BlockSpec((pair, tq, MLA_QK), lambda h, i: (h, i, 0)),
            pl.BlockSpec((pair, s_len, MLA_QK), lambda h, i: (h, 0, 0)),
            pl.BlockSpec((pair, s_len, MLA_V), lambda h, i: (h, 0, 0)),
            pl.BlockSpec((tq, pair * MLA_V), lambda h, i: (i, first + h)),
            pl.BlockSpec((tq, pair * MLA_V), lambda h, i: (i, first + h)),
            pl.BlockSpec((pair, tq, 1), lambda h, i: (h, i, 0)),
        ],
        out_specs=[
            pl.BlockSpec((pair, tq, MLA_QK), lambda h, i: (h, i, 0)),
            pl.BlockSpec((pair, s_len, MLA_QK), lambda h, i: (h, 0, 0)),
            pl.BlockSpec((pair, s_len, MLA_V), lambda h, i: (h, 0, 0)),
        ],
        out_shape=[
            _sds((MLA_HEADS, s_len, MLA_QK), F32),
            _sds((MLA_HEADS, s_len, MLA_QK), F32),
            _sds((MLA_HEADS, s_len, MLA_V), F32),
        ],
        compiler_params=_cparams(("parallel", "arbitrary")),
    )(q4, k4, v4, cat, dcat, lse)


def _mla_split(pm):
    q_lat = pm[:, :MLA_Q_RANK]
    kv_lat = pm[:, MLA_Q_RANK:MLA_Q_RANK + MLA_KV_RANK]
    kr = pm[:, MLA_Q_RANK + MLA_KV_RANK:MLA_Q_RANK + MLA_KV_RANK + MLA_ROPE]
    kr_sw = pm[:, MLA_Q_RANK + MLA_KV_RANK + MLA_ROPE:]
    return q_lat, kv_lat, kr, kr_sw


def _mla_proj(pm, cos2, sin2, q_norm, kv_norm, wq_ext, wkv):
    s_len = pm.shape[0]

    def fn(pm_, cos_, sin_, qg, kvg, wq, wk):
        q_lat, kv_lat, kr, kr_sw = _mla_split(pm_)
        nq = (_rms(q_lat)[0] * qg).astype(BF16)
        nkv = (_rms(kv_lat)[0] * kvg).astype(BF16)
        k_rot = kr * cos_ + kr_sw * sin_
        qs, ks, vs = [], [], []
        for h in range(MLA_HEADS):
            qe = _dot(nq, wq[h], "nt")
            q_rot = qe[:, MLA_NOPE:MLA_QK] * cos_ + qe[:, MLA_QK:] * sin_
            qs.append(jnp.concatenate([qe[:, :MLA_NOPE], q_rot], axis=-1))
            kve = _dot(nkv, wk[h], "nt")
            ks.append(jnp.concatenate([kve[:, :MLA_NOPE], k_rot], axis=-1))
            vs.append(kve[:, MLA_NOPE:])
        return jnp.stack(qs), jnp.stack(ks), jnp.stack(vs)

    return _rowwise(
        "mla_proj", fn, [pm, cos2, sin2], [q_norm, kv_norm, wq_ext, wkv],
        [_sds((MLA_HEADS, s_len, MLA_QK), BF16), _sds((MLA_HEADS, s_len, MLA_QK), BF16),
         _sds((MLA_HEADS, s_len, MLA_V), BF16)], [], ELEM_TILE,
    )


def _mla_proj_bwd(pm, cos2, sin2, dq4, dk4, dv4, q_norm, kv_norm, wq_ext, wkv):
    s_len = pm.shape[0]

    def fn(pm_, cos_, sin_, dq, dk, dv, qg, kvg, wq, wk):
        q_lat, kv_lat, _, _ = _mla_split(pm_)
        xq, rq = _rms(q_lat)
        xkv, rkv = _rms(kv_lat)
        nq = (xq * qg).astype(BF16)
        nkv = (xkv * kvg).astype(BF16)
        dnq = jnp.zeros_like(q_lat)
        dnkv = jnp.zeros_like(kv_lat)
        dkr = jnp.zeros_like(cos_)
        dwq, dwk = [], []
        for h in range(MLA_HEADS):
            dy = dq[h][:, MLA_NOPE:]
            dqe = jnp.concatenate([dq[h][:, :MLA_NOPE], dy * cos_, dy * sin_], axis=-1).astype(BF16)
            dnq = dnq + _dot(dqe, wq[h], "nn")
            dwq.append(_dot(dqe, nq, "tn"))
            dkve = jnp.concatenate([dk[h][:, :MLA_NOPE], dv[h]], axis=-1).astype(BF16)
            dnkv = dnkv + _dot(dkve, wk[h], "nn")
            dwk.append(_dot(dkve, nkv, "tn"))
            dkr = dkr + dk[h][:, MLA_NOPE:]
        dq_lat = _rms_bwd(dnq * qg, xq, rq)
        dkv_lat = _rms_bwd(dnkv * kvg, xkv, rkv)
        dpm = jnp.concatenate([dq_lat, dkv_lat, dkr * cos_, dkr * sin_], axis=-1)
        return dpm, _sum0(dnq * xq), _sum0(dnkv * xkv), jnp.stack(dwq), jnp.stack(dwk)

    return _rowwise(
        "mla_proj_bwd", fn, [pm, cos2, sin2, dq4, dk4, dv4], [q_norm, kv_norm, wq_ext, wkv],
        [_sds((s_len, N_MLA_COLS), BF16)],
        [_sds((1, MLA_Q_RANK), F32), _sds((1, MLA_KV_RANK), F32),
         _sds(wq_ext.shape, F32), _sds(wkv.shape, F32)], ELEM_TILE,
    )


def _rope_tables(s_len):
    inv = ROPE_THETA ** (-jnp.arange(0, MLA_ROPE, 2, dtype=F32) / MLA_ROPE)
    ang = jnp.arange(s_len, dtype=F32)[:, None] * inv[None, :]
    cos, sin = jnp.cos(ang), jnp.sin(ang)
    return jnp.concatenate([cos, cos], axis=-1), jnp.concatenate([-sin, sin], axis=-1)


def _mix_weights(g_mix):
    rest = g_mix[:, 0]
    w_in_t = rest[:, O_IN:O_IN + R_IN].reshape(D_IN, D_MODEL)
    w_o = rest[:, O_O:O_O + R_O].reshape(D_MODEL, D_MODEL)
    w_uq_t = rest[:, O_UQ:O_UQ + R_UQ].reshape(MLA_HEADS, MLA_QK, MLA_Q_RANK)
    w_ukv_t = rest[:, O_UKV:O_UKV + R_UKV].reshape(MLA_HEADS, MLA_NOPE + MLA_V, MLA_KV_RANK)
    half = MLA_ROPE // 2
    w_swa = w_in_t[:N_SWA_COLS]
    w_mla = jnp.concatenate([w_in_t[N_SWA_COLS:], w_in_t[D_IN - half:], w_in_t[D_IN - MLA_ROPE:D_IN - half]], axis=0)
    wq_ext = jnp.concatenate([w_uq_t, w_uq_t[:, MLA_QK - half:], w_uq_t[:, MLA_NOPE:MLA_QK - half]], axis=1)
    return w_swa, w_mla, w_o, wq_ext, w_ukv_t


def _mix_fwd(x, h, gate, mix_src, q_norm, kv_norm, bias, sinkb, cos2, sin2, next_norm, next_src):
    s_len = x.shape[0]
    tm = min(ROW_TILE, s_len)
    deps = mix_src.mid(x)
    weights = _mix_weights(mix_src.get(h))
    w_swa, w_mla, w_o, wq_ext, wkv = weights
    swa2 = _mm(
        "mix_proj_swa", "nt", (s_len // tm, 1, 1),
        h, (tm, D_MODEL), lambda i, j, k: (i, 0),
        w_swa, (N_SWA_COLS, D_MODEL), lambda i, j, k: (0, 0),
        _sds((s_len, N_SWA_COLS), BF16), (tm, N_SWA_COLS), lambda i, j, k: (i, 0),
        (tm, N_SWA_COLS), deps=deps,
    )
    pm = _mm(
        "mix_proj_mla", "nt", (s_len // tm, 1, 1),
        h, (tm, D_MODEL), lambda i, j, k: (i, 0),
        w_mla, (N_MLA_COLS, D_MODEL), lambda i, j, k: (0, 0),
        _sds((s_len, N_MLA_COLS), F32), (tm, N_MLA_COLS), lambda i, j, k: (i, 0),
        (tm, N_MLA_COLS),
    )
    q4, k4, v4 = _mla_proj(pm, cos2, sin2, q_norm, kv_norm, wq_ext, wkv)
    oa, lse_a = _swa_fwd(swa2, bias, sinkb)
    ob, lse_b = _mla_fwd(q4, k4, v4)
    cat = jnp.concatenate([oa, ob], axis=1)
    z = _mm(
        "mix_out", "nn", (s_len // tm, 1, 1),
        cat, (tm, D_MODEL), lambda i, j, k: (i, 0),
        w_o, (D_MODEL, D_MODEL), lambda i, j, k: (0, 0),
        _sds((s_len, D_MODEL), F32), (tm, D_MODEL), lambda i, j, k: (i, 0),
        (tm, D_MODEL), deps=next_src.mid(cat),
    )

    def residual_norm(x_, z_, g_, gamma_, sc_, sh_):
        x_out = x_ + g_ * z_
        return x_out, _normmod(x_out, gamma_, sc_, sh_)

    out = _rowwise(
        "mix_residual_norm", residual_norm, [x, z], [gate] + list(next_norm),
        [_sds((s_len, D_MODEL), F32), _sds((s_len, D_MODEL), BF16)], [], ELEM_TILE,
    )
    return out, (weights, h, swa2, pm, q4, k4, v4, cat, lse_a, lse_b, z)


def _mix_bwd(dxo, x, gamma, sc, gate, q_norm, kv_norm, bias, sinkb, cos2, sin2, saved, hooks):
    weights, h, swa2, pm, q4, k4, v4, cat, lse_a, lse_b, z = saved
    w_swa, w_mla, w_o, wq_ext, wkv = weights
    s_len = x.shape[0]
    tm = tk = min(ROW_TILE, s_len)
    vec = _sds((1, D_MODEL), F32)
    n_proj = N_SWA_COLS + N_MLA_COLS
    half_d = D_MODEL // 2

    dz, dgate = _rowwise(
        "mix_bwd_gate", lambda dxo_, z_, g_: (g_ * dxo_, _sum0(z_ * dxo_)),
        [dxo, z], [gate], [_sds((s_len, D_MODEL), BF16)], [vec], ELEM_TILE,
    )
    dw_o = _mm(
        "mix_bwd_wo", "tn", (2, 1, s_len // tk),
        cat, (tk, half_d), lambda i, j, k: (k, i),
        dz, (tk, D_MODEL), lambda i, j, k: (k, 0),
        _sds((D_MODEL, D_MODEL), F32), (half_d, D_MODEL), lambda i, j, k: (i, 0),
        (half_d, D_MODEL),
    )
    dcat = _mm(
        "mix_bwd_dcat", "nt", (s_len // tm, 1, 1),
        dz, (tm, D_MODEL), lambda i, j, k: (i, 0),
        w_o, (D_MODEL, D_MODEL), lambda i, j, k: (0, 0),
        _sds((s_len, D_MODEL), BF16), (tm, D_MODEL), lambda i, j, k: (i, 0),
        (tm, D_MODEL), deps=hooks.after_gate(dz),
    )
    dq_a, dkva, dkvb, dbias, dsink = _swa_bwd(swa2, bias, sinkb, cat, dcat, lse_a)
    dq4, dk4, dv4 = _mla_bwd(q4, k4, v4, cat, dcat, lse_b)
    dpm, dq_norm, dkv_norm, dwq_ext, dwkv = _mla_proj_bwd(pm, cos2, sin2, dq4, dk4, dv4, q_norm, kv_norm, wq_ext, wkv)

    dkv = dkva + jnp.concatenate([dkvb[WINDOW:], jnp.zeros_like(dkvb[:WINDOW])], axis=0)
    dproj = jnp.concatenate([dq_a, dkv.astype(BF16), dpm], axis=1)
    w_proj = jnp.concatenate([w_swa, w_mla], axis=0)

    dw_proj = _mm(
        "mix_bwd_win", "tn", (n_proj // 256, 1, s_len // tk),
        dproj, (tk, 256), lambda i, j, k: (k, i),
        h, (tk, D_MODEL), lambda i, j, k: (k, 0),
        _sds((n_proj, D_MODEL), F32), (256, D_MODEL), lambda i, j, k: (i, 0),
        (256, D_MODEL),
    )
    dw_swa, dw_mla = dw_proj[:N_SWA_COLS], dw_proj[N_SWA_COLS:]
    dh = _mm(
        "mix_bwd_dh", "nn", (s_len // tm, 1, 1),
        dproj, (tm, n_proj), lambda i, j, k: (i, 0),
        w_proj, (n_proj, D_MODEL), lambda i, j, k: (0, 0),
        _sds((s_len, D_MODEL), F32), (tm, D_MODEL), lambda i, j, k: (i, 0),
        (tm, D_MODEL),
    )
    dx, dsc, dsh, dgamma = _normmod_bwd_call("mix_bwd_normmod", [dh], x, dxo, gamma, sc)

    half = MLA_ROPE // 2
    n_mla_in = D_IN - N_SWA_COLS
    dw_mla_base = dw_mla[:n_mla_in]
    dw_mla_base = dw_mla_base.at[n_mla_in - half:].add(dw_mla[n_mla_in:n_mla_in + half])
    dw_mla_base = dw_mla_base.at[n_mla_in - MLA_ROPE:n_mla_in - half].add(dw_mla[n_mla_in + half:])
    dw_in_t = jnp.concatenate([dw_swa, dw_mla_base], axis=0)
    dw_uq_t = dwq_ext[:, :MLA_QK]
    dw_uq_t = dw_uq_t.at[:, MLA_QK - half:].add(dwq_ext[:, MLA_QK:MLA_QK + half])
    dw_uq_t = dw_uq_t.at[:, MLA_NOPE:MLA_QK - half].add(dwq_ext[:, MLA_QK + half:])
    rest = jnp.concatenate(
        [
            dw_in_t.reshape(N_CHIPS, R_IN, D_MODEL),
            dw_o.reshape(N_CHIPS, R_O, D_MODEL),
            dw_uq_t.reshape(N_CHIPS, R_UQ, D_MODEL),
            dwkv.reshape(N_CHIPS, R_UKV, D_MODEL),
            jnp.zeros((N_CHIPS, F_SHARD - O_PAD, D_MODEL), F32),
        ],
        axis=1,
    ).astype(BF16)
    return dx, rest, (dsh, dsc, dgate, dgamma), dq_norm, dkv_norm, dbias, dsink


def _device_step(x, target, mod, gu1_src, down1_src, mix_src, ffn2_src, norms, q_norm, kv_norm, sinks, rel_bias,
                 hooks2=None, hooks_mix=None, mix_done=None, hooks1=None):
    s_len = x.shape[0]
    sh1, sc1, g1, sh2, sc2, g2, sh3, sc3, g3 = [mod[:, i * D_MODEL:(i + 1) * D_MODEL] for i in range(N_MOD)]
    n1, n2, n3, nf = norms
    bkt = jnp.asarray(_bucket_map())
    bias = _bias_expand(rel_bias.T, bkt).reshape(SWA_HEADS, WINDOW, 2 * WINDOW)
    sinkb = jnp.broadcast_to(sinks.reshape(SWA_HEADS, 1, 1), (SWA_HEADS, WINDOW, 1))
    cos2, sin2 = _rope_tables(s_len)

    (x1, h2), saved1 = _ffn_fwd(
        "ffn1", x, n1, sh1, sc1, g1, gu1_src, 0, down1_src, 0, sh1, next_norm=(n2, sc2, sh2), next_src=mix_src
    )
    (x2, h3), saved2 = _mix_fwd(
        x1, h2, g2, mix_src, q_norm, kv_norm, bias, sinkb, cos2, sin2, next_norm=(n3, sc3, sh3), next_src=ffn2_src
    )
    (dx3, sq, dnf), saved3 = _ffn_fwd(
        "ffn2", x2, n3, sh3, sc3, g3, ffn2_src, 0, None, 2, x2, h_pre=h3, head=(target, nf)
    )
    loss_part = (0.5 / D_MODEL) * jnp.sum(sq)

    dx2, gb2, (dsh3, dsc3, dg3, dn3) = _ffn_bwd("ffn2", dx3, x2, n3, sc3, g3, saved3, hooks2 or _BwdHooks())
    dx1, rest, (dsh2, dsc2, dg2, dn2), dq_norm, dkv_norm, dbias, dsink = _mix_bwd(
        dx2, x1, n2, sc2, g2, q_norm, kv_norm, bias, sinkb, cos2, sin2, saved2, hooks_mix or _BwdHooks()
    )
    rest = rest[:, None]
    deps1 = mix_done(dx1, rest) if mix_done is not None else []
    dx0, gb1, (dsh1, dsc1, dg1, dn1) = _ffn_bwd(
        "ffn1", dx1, x, n1, sc1, g1, saved1, hooks1 or _BwdHooks(), deps=deps1
    )

    dmod = jnp.concatenate([dsh1, dsc1, dg1, dsh2, dsc2, dg2, dsh3, dsc3, dg3], axis=-1)
    drel = _bias_reduce(dbias.reshape(SWA_HEADS, -1), bkt).T
    dsinks = jnp.sum(dsink, axis=(1, 2)).reshape(1, SWA_HEADS)
    small = (dn1, dn2, dn3, dnf, dq_norm, dkv_norm, dsinks, drel)
    return loss_part, dx0, (gb1, gb2, rest), dmod, small


_MESH = pl.DeviceIdType.MESH


def _place():
    return lax.axis_index("x"), lax.axis_index("y"), lax.axis_index("c")


def _other_chips(x, y):
    return [(1 - x, y), (x, 1 - y), (1 - x, 1 - y)]


def _allgather_small(name, blk):
    m_per, n = blk.shape

    def body(x_ref, out_ref, send_sems, recv_sems, local_sem):
        x, y, c = _place()
        me, sibling = (x, y, c), (x, y, 1 - c)
        chips = _other_chips(x, y)

        def rows(px, py, pc):
            return out_ref.at[pl.ds((4 * px + 2 * py + pc) * m_per, m_per), :]

        def copy(k, block, to, src=None):
            return pltpu.make_async_remote_copy(
                src_ref=rows(*block) if src is None else src, dst_ref=rows(*block),
                send_sem=send_sems.at[k], recv_sem=recv_sems.at[k], device_id=to, device_id_type=_MESH,
            )

        mine = pltpu.make_async_copy(x_ref, rows(*me), local_sem)
        mine.start()
        first = [copy(0, me, sibling, src=x_ref)]
        first += [copy(1 + j, me, (*chip, c), src=x_ref) for j, chip in enumerate(chips)]
        for cp in first:
            cp.start()
        passed = [copy(4 + j, (*chip, c), sibling) for j, chip in enumerate(chips)]
        for j, chip in enumerate(chips):
            copy(1 + j, (*chip, c), me).wait_recv()
            passed[j].start()
        copy(0, sibling, me).wait_recv()
        for j, chip in enumerate(chips):
            copy(4 + j, (*chip, 1 - c), me).wait_recv()
        for cp in first + passed:
            cp.wait_send()
        mine.wait()

    return pl.pallas_call(
        body,
        name=name,
        out_shape=_sds((N_DEV * m_per, n), blk.dtype),
        in_specs=[pl.BlockSpec(memory_space=pltpu.VMEM)],
        out_specs=pl.BlockSpec(memory_space=pltpu.VMEM),
        scratch_shapes=[pltpu.SemaphoreType.DMA((7,)), pltpu.SemaphoreType.DMA((7,)), pltpu.SemaphoreType.DMA],
    )(blk)


def _gather_sends(n, own_ref, g_ref, send_sem, recv_sem, x, y, c, chip):
    return [
        pltpu.make_async_remote_copy(
            src_ref=own_ref.at[p, pl.ds(c * HALF, HALF), :], dst_ref=g_ref.at[2 * x + y, p, pl.ds(c * HALF, HALF), :],
            send_sem=send_sem, recv_sem=recv_sem, device_id=(*chip, c), device_id_type=_MESH,
        )
        for p in range(n)
    ]


def _gather_forwards(n, g_ref, send_sem, recv_sem, chip, c, sibling):
    return [
        pltpu.make_async_remote_copy(
            src_ref=g_ref.at[2 * chip[0] + chip[1], p, pl.ds(c * HALF, HALF), :],
            dst_ref=g_ref.at[2 * chip[0] + chip[1], p, pl.ds(c * HALF, HALF), :],
            send_sem=send_sem, recv_sem=recv_sem, device_id=sibling, device_id_type=_MESH,
        )
        for p in range(n)
    ]


def _gather_all(own_ref, g_ref, send_sem, recv_sem, chip, half, c):
    return pltpu.make_async_remote_copy(
        src_ref=own_ref.at[:, pl.ds(c * HALF, HALF), :],
        dst_ref=g_ref.at[2 * chip[0] + chip[1], :, pl.ds(half * HALF, HALF), :],
        send_sem=send_sem, recv_sem=recv_sem, device_id=(*chip, c), device_id_type=_MESH,
    )


_HBM_SPEC = pl.BlockSpec(memory_space=pltpu.HBM)
_SEM_SPEC = pl.BlockSpec(memory_space=pltpu.SEMAPHORE)
_ANY_SPEC = pl.BlockSpec(memory_space=pl.ANY)
_VMEM_SPEC = pl.BlockSpec(memory_space=pltpu.VMEM)
_SPLIT_PARAMS = pltpu.CompilerParams(has_side_effects=pltpu.SideEffectType.DATAFLOW_SIDE_EFFECTING)
_TOKEN = jax.ShapeDtypeStruct((8, 128), F32)


def _in_hbm(a):
    return pltpu.with_memory_space_constraint(a, pltpu.HBM)


class _GatherBehind:
    def __init__(self, tag, own, then=()):
        self.tag, self.n, self.own, self.then = tag, own.shape[0], own, tuple(then)
        self.mid_deps = None

    def start(self, after):
        tag, own, n = self.tag, self.own, self.n
        x, y, _ = _place()
        g_init = lax.dynamic_update_slice(
            lax.empty((N_CHIPS,) + own.shape, own.dtype), own[None], (2 * x + y, 0, 0, 0)
        )

        def body(own_ref, g_ref, *rest):
            send_sems, recv_sems, _, _, token = rest[len(after):]
            x, y, c = _place()
            for j, chip in enumerate(_other_chips(x, y)):
                for cp in _gather_sends(n, own_ref, g_ref, send_sems.at[j], recv_sems.at[j], x, y, c, chip):
                    cp.start()
            token[...] = jnp.zeros_like(token)

        self.send1, self.recv1, self.own, self.g, self.token = pl.pallas_call(
            body,
            name=f"{tag}_start",
            out_shape=(
                pltpu.SemaphoreType.DMA((3,)), pltpu.SemaphoreType.DMA((3,)),
                pltpu.HBM(own.shape, own.dtype), pltpu.HBM(g_init.shape, g_init.dtype), _TOKEN,
            ),
            in_specs=(_HBM_SPEC, _HBM_SPEC) + (_ANY_SPEC,) * len(after),
            out_specs=(_SEM_SPEC, _SEM_SPEC, _HBM_SPEC, _HBM_SPEC, _VMEM_SPEC),
            input_output_aliases={0: 2, 1: 3},
            compiler_params=_SPLIT_PARAMS,
        )(_in_hbm(own), _in_hbm(g_init), *after)

    def mid(self, after):
        if self.mid_deps is not None:
            return self.mid_deps
        n = self.n

        def body(own_ref, g_ref, send1, recv1, after_ref, send2, recv2, g_out, token):
            x, y, c = _place()
            sibling = (x, y, 1 - c)
            chips = _other_chips(x, y)
            for j, chip in enumerate(chips):
                _gather_all(own_ref, g_ref, send1.at[j], recv1.at[j], chip, c, c).wait_recv()
                for cp in _gather_forwards(n, g_ref, send2.at[j], recv2.at[j], chip, c, sibling):
                    cp.start()
            for j, chip in enumerate(chips):
                _gather_all(own_ref, g_ref, send1.at[j], recv1.at[j], chip, c, c).wait_send()
            token[...] = jnp.zeros_like(token)

        self.send2, self.recv2, self.g, token = pl.pallas_call(
            body,
            name=f"{self.tag}_mid",
            out_shape=(
                pltpu.SemaphoreType.DMA((3,)), pltpu.SemaphoreType.DMA((3,)),
                pltpu.HBM(self.g.shape, self.g.dtype), _TOKEN,
            ),
            in_specs=(_HBM_SPEC, _HBM_SPEC, _SEM_SPEC, _SEM_SPEC, _ANY_SPEC),
            out_specs=(_SEM_SPEC, _SEM_SPEC, _HBM_SPEC, _VMEM_SPEC),
            input_output_aliases={1: 2},
            compiler_params=_SPLIT_PARAMS,
        )(self.own, self.g, self.send1, self.recv1, after)
        deps = [token]
        for nxt in self.then:
            nxt.start(deps[-1:])
            deps.append(nxt.token)
        self.mid_deps = deps
        return deps

    def get(self, after):
        def body(g_ref, send2, recv2, after_ref, g_out):
            x, y, c = _place()
            for j, chip in enumerate(_other_chips(x, y)):
                slot_in = g_ref.at[2 * chip[0] + chip[1], :, pl.ds((1 - c) * HALF, HALF), :]
                slot_out = g_ref.at[2 * chip[0] + chip[1], :, pl.ds(c * HALF, HALF), :]
                cp = pltpu.make_async_remote_copy(
                    src_ref=slot_out, dst_ref=slot_in, send_sem=send2.at[j], recv_sem=recv2.at[j],
                    device_id=(x, y, 1 - c), device_id_type=_MESH,
                )
                cp.wait_send()
                cp.wait_recv()

        return pl.pallas_call(
            body,
            name=f"{self.tag}_wait",
            out_shape=pltpu.HBM(self.g.shape, self.g.dtype),
            in_specs=(_HBM_SPEC, _SEM_SPEC, _SEM_SPEC, _ANY_SPEC),
            out_specs=_HBM_SPEC,
            input_output_aliases={0: 0},
            compiler_params=_SPLIT_PARAMS,
        )(self.g, self.send2, self.recv2, after)


def _pair_sum(name, gb, land):
    def body(c_ref, gb_ref, land_ref, o_ref):
        o_ref[...] = (gb_ref[...].astype(F32) + land_ref[...].astype(F32)).astype(o_ref.dtype)

    core = lax.axis_index("c").astype(jnp.int32).reshape(1)
    return pl.pallas_call(
        body,
        name=name,
        grid_spec=pltpu.PrefetchScalarGridSpec(
            num_scalar_prefetch=1,
            grid=(N_CHIPS, gb.shape[1]),
            in_specs=[
                pl.BlockSpec((None, None, HALF, D_MODEL), lambda j, p, c: (j, p, c[0], 0)),
                pl.BlockSpec((None, None, HALF, D_MODEL), lambda j, p, c: (j, p, 0, 0)),
            ],
            out_specs=pl.BlockSpec((None, None, HALF, D_MODEL), lambda j, p, c: (j, p, 0, 0)),
        ),
        out_shape=_sds(land.shape, BF16),
        compiler_params=_cparams(("parallel", "parallel")),
    )(core, gb, land)


def _chip_sum_share(name, land):
    n = land.shape[1]

    def body(l_ref, r_ref, buf, send_sems, recv_sems, local_sems):
        p = pl.program_id(0)
        x, y, c = _place()
        acc = l_ref[0].astype(F32)
        for j in range(1, N_CHIPS):
            acc = acc + l_ref[j].astype(F32)
        buf[p] = acc

        def copies(q):
            mine = r_ref.at[q, pl.ds(c * HALF, HALF), :]
            theirs = r_ref.at[q, pl.ds((1 - c) * HALF, HALF), :]
            local = pltpu.make_async_copy(buf.at[q], mine, local_sems.at[q])
            out = pltpu.make_async_remote_copy(
                src_ref=buf.at[q], dst_ref=mine, send_sem=send_sems.at[q], recv_sem=recv_sems.at[q],
                device_id=(x, y, 1 - c), device_id_type=_MESH,
            )
            arrive = pltpu.make_async_remote_copy(
                src_ref=buf.at[q], dst_ref=theirs, send_sem=send_sems.at[q], recv_sem=recv_sems.at[q],
                device_id=(x, y, 1 - c), device_id_type=_MESH,
            )
            return local, out, arrive

        local, out, _ = copies(p)
        local.start()
        out.start()

        @pl.when(p == n - 1)
        def _():
            for q in range(n):
                local, out, arrive = copies(q)
                arrive.wait_recv()
                out.wait_send()
                local.wait()

    return pl.pallas_call(
        body,
        name=name,
        grid=(n,),
        in_specs=[pl.BlockSpec((N_CHIPS, None, HALF, D_MODEL), lambda p: (0, p, 0, 0))],
        out_specs=pl.BlockSpec(memory_space=pl.ANY),
        out_shape=_sds((n, F_SHARD, D_MODEL), F32),
        scratch_shapes=[
            pltpu.VMEM((n, HALF, D_MODEL), F32),
            pltpu.SemaphoreType.DMA((n,)), pltpu.SemaphoreType.DMA((n,)), pltpu.SemaphoreType.DMA((n,)),
        ],
        compiler_params=_cparams(("arbitrary",)),
    )(land)


class _ReduceBehind:
    def __init__(self, tag, gb, after=()):
        self.tag = tag
        n = gb.shape[1]
        land = lax.empty((N_CHIPS, n, HALF, D_MODEL), gb.dtype)

        def body(gb_ref, land_ref, *rest):
            send_sem, recv_sem, _, _, token = rest[len(after):]
            self._pair_copy(gb_ref, land_ref, send_sem, recv_sem).start()
            token[...] = jnp.zeros_like(token)

        self.send, self.recv, self.gb, self.land, self.token = pl.pallas_call(
            body,
            name=f"grads_pair_start_{tag}",
            out_shape=(
                pltpu.SemaphoreType.DMA((1,)), pltpu.SemaphoreType.DMA((1,)),
                pltpu.HBM(gb.shape, gb.dtype), pltpu.HBM(land.shape, land.dtype), _TOKEN,
            ),
            in_specs=(_HBM_SPEC, _HBM_SPEC) + (_ANY_SPEC,) * len(after),
            out_specs=(_SEM_SPEC, _SEM_SPEC, _HBM_SPEC, _HBM_SPEC, _VMEM_SPEC),
            input_output_aliases={0: 2, 1: 3},
            compiler_params=_SPLIT_PARAMS,
        )(_in_hbm(gb), _in_hbm(land), *after)

    @staticmethod
    def _pair_copy(gb_ref, land_ref, send_sem, recv_sem):
        x, y, c = _place()
        return pltpu.make_async_remote_copy(
            src_ref=gb_ref.at[:, :, pl.ds((1 - c) * HALF, HALF), :], dst_ref=land_ref,
            send_sem=send_sem.at[0], recv_sem=recv_sem.at[0], device_id=(x, y, 1 - c), device_id_type=_MESH,
        )

    @staticmethod
    def _chip_copies(p_ref, land_ref, send_sems, recv_sems):
        x, y, c = _place()
        me = 2 * x + y
        sends, arrivals = [], []
        for k, chip in enumerate(_other_chips(x, y)):
            them = 2 * chip[0] + chip[1]
            sends.append(pltpu.make_async_remote_copy(
                src_ref=p_ref.at[them], dst_ref=land_ref.at[me], send_sem=send_sems.at[k], recv_sem=recv_sems.at[k],
                device_id=(*chip, c), device_id_type=_MESH,
            ))
            arrivals.append(pltpu.make_async_remote_copy(
                src_ref=p_ref.at[me], dst_ref=land_ref.at[them], send_sem=send_sems.at[k], recv_sem=recv_sems.at[k],
                device_id=(*chip, c), device_id_type=_MESH,
            ))
        return sends, arrivals

    def mid(self, after):
        tag = self.tag

        def wait_body(gb_ref, land_ref, send_sem, recv_sem, after_ref, gb_out, land_out):
            cp = self._pair_copy(gb_ref, land_ref, send_sem, recv_sem)
            cp.wait_send()
            cp.wait_recv()

        gb, land = pl.pallas_call(
            wait_body,
            name=f"grads_pair_wait_{tag}",
            out_shape=(pltpu.HBM(self.gb.shape, self.gb.dtype), pltpu.HBM(self.land.shape, self.land.dtype)),
            in_specs=(_HBM_SPEC, _HBM_SPEC, _SEM_SPEC, _SEM_SPEC, _ANY_SPEC),
            out_specs=(_HBM_SPEC, _HBM_SPEC),
            input_output_aliases={0: 0, 1: 1},
            compiler_params=_SPLIT_PARAMS,
        )(self.gb, self.land, self.send, self.recv, after)
        part = _pair_sum(f"grads_pair_sum_{tag}", gb, land)

        x, y, _ = _place()
        start = (2 * x + y, 0, 0, 0)
        own = lax.dynamic_slice(part, start, (1,) + part.shape[1:])
        land2 = lax.dynamic_update_slice(lax.empty(part.shape, part.dtype), own, start)

        def start_body(p_ref, land_ref, send_sems, recv_sems, p_out, land_out, token):
            for cp in self._chip_copies(p_ref, land_ref, send_sems, recv_sems)[0]:
                cp.start()
            token[...] = jnp.zeros_like(token)

        self.send2, self.recv2, self.part, self.land2, token = pl.pallas_call(
            start_body,
            name=f"grads_chip_start_{tag}",
            out_shape=(
                pltpu.SemaphoreType.DMA((3,)), pltpu.SemaphoreType.DMA((3,)),
                pltpu.HBM(part.shape, part.dtype), pltpu.HBM(land2.shape, land2.dtype), _TOKEN,
            ),
            in_specs=(_HBM_SPEC, _HBM_SPEC),
            out_specs=(_SEM_SPEC, _SEM_SPEC, _HBM_SPEC, _HBM_SPEC, _VMEM_SPEC),
            input_output_aliases={0: 2, 1: 3},
            compiler_params=_SPLIT_PARAMS,
        )(_in_hbm(part), _in_hbm(land2))
        return [token]

    def get(self, after):
        n_after = len(after)

        def wait_body(p_ref, land_ref, send_sems, recv_sems, *rest):
            sends, arrivals = self._chip_copies(p_ref, land_ref, send_sems, recv_sems)
            for cp in arrivals:
                cp.wait_recv()
            for cp in sends:
                cp.wait_send()

        _, land2 = pl.pallas_call(
            wait_body,
            name=f"grads_chip_wait_{self.tag}",
            out_shape=(pltpu.HBM(self.part.shape, self.part.dtype), pltpu.HBM(self.land2.shape, self.land2.dtype)),
            in_specs=(_HBM_SPEC, _HBM_SPEC, _SEM_SPEC, _SEM_SPEC) + (_ANY_SPEC,) * n_after,
            out_specs=(_HBM_SPEC, _HBM_SPEC),
            input_output_aliases={0: 0, 1: 1},
            compiler_params=_SPLIT_PARAMS,
        )(self.part, self.land2, self.send2, self.recv2, *after)
        return _chip_sum_share(f"grads_chip_sum_share_{self.tag}", land2)


def _allreduce_small(blk):
    gathered = _allgather_small("allgather_small_grads", blk).reshape(N_DEV, PK_ROWS, 128)

    def body(g_ref, o_ref):
        acc = g_ref[0]
        for k in range(1, N_DEV):
            acc = acc + g_ref[k]
        o_ref[...] = acc

    total = pl.pallas_call(body, name="small_grads_sum", out_shape=_sds((PK_ROWS, 128), F32))(gathered)
    return gathered, total


def _adamw_math(w_, g_, m_, v_):
    m_new = ADAM_B1 * m_ + (1.0 - ADAM_B1) * g_
    v_new = ADAM_B2 * v_ + (1.0 - ADAM_B2) * (g_ * g_)
    m_hat = m_new / (1.0 - ADAM_B1 ** ADAM_STEP)
    v_hat = v_new / (1.0 - ADAM_B2 ** ADAM_STEP)
    delta = -ADAM_LR * (m_hat / (jnp.sqrt(v_hat) + ADAM_EPS) + ADAM_WD * w_)
    return delta, m_new, v_new


def _adamw(name, w, g, m, v):
    rows, cols = w.shape
    tm = rows
    while tm * cols * 4 > (1 << 21) and tm % 16 == 0:
        tm //= 2
    out = _sds(w.shape, F32)
    return _rowwise(name, _adamw_math, [w, g, m, v], [], [out, out, out], [], tm)


def _adamw_small(ws, gs, ms, vs):
    n = len(ws)
    shapes = [w.shape for w in ws]
    as2d = lambda a: a.reshape(1, -1) if a.ndim == 1 else a

    def body(*refs):
        ins, outs = refs[:4 * n], refs[4 * n:]
        for k in range(n):
            res = _adamw_math(*[ins[j * n + k][...] for j in range(4)])
            for j in range(3):
                outs[j * n + k][...] = res[j]

    args = [as2d(a) for group in (ws, gs, ms, vs) for a in group]
    out = pl.pallas_call(
        body, name="adamw_small", out_shape=[_sds(as2d(w).shape, F32) for w in ws] * 3, compiler_params=_cparams()
    )(*args)
    back = [o.reshape(shapes[i % n]) for i, o in enumerate(out)]
    return back[:n], back[n:2 * n], back[2 * n:]


def _pack_small(b_mod_like, n1, n2, n3, nf, qn, kvn, sinks, rel):
    parts = [
        b_mod_like.reshape(PK_MOD, 128),
        n1.reshape(8, 128), n2.reshape(8, 128), n3.reshape(8, 128), nf.reshape(8, 128),
        qn.reshape(2, 128), kvn.reshape(1, 128),
        jnp.pad(sinks.reshape(1, SWA_HEADS), ((0, 0), (0, 128 - SWA_HEADS))),
        rel.reshape(2, 128),
        jnp.zeros((2, 128), F32),
    ]
    return jnp.concatenate(parts, axis=0)


def _unpack_small(p):
    o = PK_MOD
    return (
        p[:o].reshape(1, N_MOD * D_MODEL),
        p[o:o + 8].reshape(1, D_MODEL), p[o + 8:o + 16].reshape(1, D_MODEL),
        p[o + 16:o + 24].reshape(1, D_MODEL), p[o + 24:o + 32].reshape(D_MODEL),
        p[o + 32:o + 34].reshape(1, MLA_Q_RANK), p[o + 34:o + 35].reshape(1, MLA_KV_RANK),
        p[o + 35:o + 36, :SWA_HEADS].reshape(1, SWA_HEADS),
        p[o + 36:o + 38].reshape(NUM_BUCKETS, SWA_HEADS),
    )


def kernel(x, c, w_mod, b_mod, norm_ffn1, ffn1_gate, ffn1_up, ffn1_down, norm_mix, w_in, q_norm, kv_norm, w_uq, w_ukv, sinks, w_o, norm_ffn2, ffn2_gate, ffn2_up, ffn2_down, rel_bias, norm_final, loss_target, m_w_mod, m_b_mod, m_norm_ffn1, m_ffn1_gate, m_ffn1_up, m_ffn1_down, m_norm_mix, m_w_in, m_q_norm, m_kv_norm, m_w_uq, m_w_ukv, m_sinks, m_w_o, m_norm_ffn2, m_ffn2_gate, m_ffn2_up, m_ffn2_down, m_rel_bias, m_norm_final, v_w_mod, v_b_mod, v_norm_ffn1, v_ffn1_gate, v_ffn1_up, v_ffn1_down, v_norm_mix, v_w_in, v_q_norm, v_kv_norm, v_w_uq, v_w_ukv, v_sinks, v_w_o, v_norm_ffn2, v_ffn2_gate, v_ffn2_up, v_ffn2_down, v_rel_bias, v_norm_final):
    ax, ay, ac = _place()
    chip = 2 * ax + ay
    dev = 2 * chip + ac
    n_mod_shard = N_MOD * D_MODEL // N_CHIPS

    def t16(w):
        return w[0].T.astype(BF16)

    rest = jnp.concatenate(
        [
            t16(w_in),
            w_o[0].astype(BF16),
            t16(w_uq).reshape(R_UQ, D_MODEL),
            t16(w_ukv).reshape(R_UKV, D_MODEL),
            jnp.zeros((F_SHARD - O_PAD, D_MODEL), BF16),
        ],
        axis=0,
    )
    own_gu1 = jnp.stack([t16(ffn1_gate), t16(ffn1_up)])
    own_down1 = ffn1_down[0].astype(BF16)[None]
    own_mix = rest[None]
    own_ffn2 = jnp.stack([t16(ffn2_gate), t16(ffn2_up), ffn2_down[0].astype(BF16)])

    (c_act,) = _rowwise(
        "silu_c", lambda v: v * _sigmoid(v), [c.reshape(8, 128)], [], [_sds((8, 128), F32)], [], 8
    )
    c_all = _allgather_small("allgather_c", c_act).reshape(N_DEV, D_MODEL)
    mod_cols = _mm(
        "mod_fwd", "nn", (1, n_mod_shard // MOD_TILE, 1),
        c_all, (N_DEV, D_MODEL), lambda i, j, k: (0, 0),
        w_mod[0], (D_MODEL, MOD_TILE), lambda i, j, k: (0, j),
        _sds((N_DEV, n_mod_shard), F32), (N_DEV, MOD_TILE), lambda i, j, k: (0, j),
        (N_DEV, MOD_TILE),
    )
    mod_all = _allgather_small("allgather_mod", mod_cols).reshape(N_CHIPS, 2, N_DEV, n_mod_shard)[:, 0]
    mod_all = mod_all.transpose(1, 0, 2).reshape(N_DEV, N_MOD * D_MODEL)
    mod = lax.dynamic_slice_in_dim(mod_all, dev, 1, axis=0) + b_mod

    norms = (norm_ffn1, norm_mix, norm_ffn2, norm_final.reshape(1, D_MODEL))

    ffn2_src = _GatherBehind("gather_ffn2", own_ffn2)
    mix_src = _GatherBehind("gather_mix", own_mix)
    down1_src = _GatherBehind("gather_down1", own_down1, then=[mix_src, ffn2_src])
    gu1_src = _GatherBehind("gather_gu1", own_gu1, then=[down1_src])
    gu1_src.start([mod_all])

    reducing, red = {}, {}

    def begin(tag, gb, after):
        reducing[tag] = _ReduceBehind(tag, gb, after=after)
        return [reducing[tag].token]

    def ffn2_gu_done(gb):
        return begin("ffn2_gu", gb, reducing["ffn2_down"].mid(gb))

    def mix_done(dx1, gb_rest):
        red["ffn2_down"] = reducing["ffn2_down"].get([dx1])
        red["ffn2_gu"] = reducing["ffn2_gu"].get([red["ffn2_down"]])
        return begin("rest", gb_rest, [red["ffn2_gu"]])

    def ffn1_gu_done(gb):
        red["rest"] = reducing["rest"].get([gb])
        begin("ffn1_gu", gb, reducing["ffn1_down"].mid(red["rest"]))
        return reducing["ffn1_gu"].mid(reducing["ffn1_gu"].token)

    loss_part, grad_x, _, dmod, small = _device_step(
        x[0], loss_target[0], mod, gu1_src, down1_src, mix_src, ffn2_src, norms, q_norm, kv_norm, sinks, rel_bias,
        hooks2=_BwdHooks(after_wdown=lambda gb: begin("ffn2_down", gb, ()), after_wgu=ffn2_gu_done),
        hooks_mix=_BwdHooks(after_gate=lambda dz: reducing["ffn2_gu"].mid(dz)),
        mix_done=mix_done,
        hooks1=_BwdHooks(
            after_swiglu=lambda dab3: reducing["rest"].mid(dab3),
            after_wdown=lambda gb: begin("ffn1_down", gb, ()),
            after_wgu=ffn1_gu_done,
        ),
    )
    loss = lax.psum(loss_part, ("x", "y", "c"))

    gathered, total = _allreduce_small(_pack_small(dmod, *small))
    (g_b_mod, g_n1, g_n2, g_n3, g_nf, g_qn, g_kvn, g_sinks, g_rel) = _unpack_small(total)
    dmod_all = gathered[:, :PK_MOD].reshape(N_DEV, N_MOD * D_MODEL)
    dmod_cols = lax.dynamic_slice_in_dim(dmod_all, chip * n_mod_shard, n_mod_shard, axis=1)
    g_w_mod = _mm(
        "mod_bwd", "tn", (1, n_mod_shard // MOD_TILE, 1),
        c_all, (N_DEV, D_MODEL), lambda i, j, k: (0, 0),
        dmod_cols, (N_DEV, MOD_TILE), lambda i, j, k: (0, j),
        _sds((D_MODEL, n_mod_shard), F32), (D_MODEL, MOD_TILE), lambda i, j, k: (0, j),
        (D_MODEL, MOD_TILE),
    )

    r_rest = red["rest"][0]
    transposed = {"ffn1_gate", "ffn1_up", "ffn2_gate", "ffn2_up", "w_in", "w_uq", "w_ukv"}
    g_big = {
        "ffn2_gate": red["ffn2_gu"][0], "ffn2_up": red["ffn2_gu"][1], "ffn2_down": red["ffn2_down"][0],
        "w_in": r_rest[O_IN:O_IN + R_IN],
        "w_o": r_rest[O_O:O_O + R_O],
        "w_uq": r_rest[O_UQ:O_UQ + R_UQ].reshape(MLA_QK, MLA_Q_RANK),
        "w_ukv": r_rest[O_UKV:O_UKV + R_UKV].reshape(MLA_NOPE + MLA_V, MLA_KV_RANK),
        "w_mod": g_w_mod,
    }
    w_big = {
        "ffn2_gate": (ffn2_gate, m_ffn2_gate, v_ffn2_gate),
        "ffn2_up": (ffn2_up, m_ffn2_up, v_ffn2_up), "ffn2_down": (ffn2_down, m_ffn2_down, v_ffn2_down),
        "w_in": (w_in, m_w_in, v_w_in), "w_o": (w_o, m_w_o, v_w_o), "w_uq": (w_uq, m_w_uq, v_w_uq),
        "w_ukv": (w_ukv, m_w_ukv, v_w_ukv), "w_mod": (w_mod, m_w_mod, v_w_mod),
    }
    grads, deltas, new_m, new_v = {}, {}, {}, {}

    def update(names):
        for nm in names:
            w, m, v = w_big[nm]
            if nm in transposed:
                outs = _adamw(f"adamw_{nm}", w[0].T, g_big[nm], m[0].T, v[0].T)
                g_, d_, m_, v_ = [a.T for a in (g_big[nm],) + tuple(outs)]
            else:
                g_, (d_, m_, v_) = g_big[nm], _adamw(f"adamw_{nm}", w[0], g_big[nm], m[0], v[0])
            grads[nm], deltas[nm], new_m[nm], new_v[nm] = g_[None], d_[None], m_[None], v_[None]

    update(list(w_big))
    red1_down = reducing["ffn1_down"].get([deltas[nm] for nm in w_big])
    red1_gu = reducing["ffn1_gu"].get([red1_down])
    g_big.update({"ffn1_gate": red1_gu[0], "ffn1_up": red1_gu[1], "ffn1_down": red1_down[0]})
    w_big.update({
        "ffn1_gate": (ffn1_gate, m_ffn1_gate, v_ffn1_gate), "ffn1_up": (ffn1_up, m_ffn1_up, v_ffn1_up),
        "ffn1_down": (ffn1_down, m_ffn1_down, v_ffn1_down),
    })
    update(["ffn1_gate", "ffn1_up", "ffn1_down"])

    small_names = ["b_mod", "norm_ffn1", "norm_mix", "norm_ffn2", "norm_final", "q_norm", "kv_norm", "sinks", "rel_bias"]
    w_small = (b_mod, norm_ffn1, norm_mix, norm_ffn2, norm_final, q_norm, kv_norm, sinks, rel_bias)
    m_small = (m_b_mod, m_norm_ffn1, m_norm_mix, m_norm_ffn2, m_norm_final, m_q_norm, m_kv_norm, m_sinks, m_rel_bias)
    v_small = (v_b_mod, v_norm_ffn1, v_norm_mix, v_norm_ffn2, v_norm_final, v_q_norm, v_kv_norm, v_sinks, v_rel_bias)
    g_small = (g_b_mod, g_n1, g_n2, g_n3, g_nf, g_qn, g_kvn, g_sinks, g_rel)
    d_s, m_s, v_s = _adamw_small(w_small, g_small, m_small, v_small)
    for nm, g_, d_, m_, v_ in zip(small_names, g_small, d_s, m_s, v_s):
        grads[nm], deltas[nm], new_m[nm], new_v[nm] = g_, d_, m_, v_

    order = ["w_mod", "b_mod", "norm_ffn1", "ffn1_gate", "ffn1_up", "ffn1_down", "norm_mix", "w_in", "q_norm", "kv_norm",
             "w_uq", "w_ukv", "sinks", "w_o", "norm_ffn2", "ffn2_gate", "ffn2_up", "ffn2_down", "rel_bias", "norm_final"]
    return (loss, grad_x[None], *[grads[n] for n in order], *[deltas[n] for n in order],
            *[new_m[n] for n in order], *[new_v[n] for n in order])
```

```python
import functools
import math

import jax
import jax.numpy as jnp
import numpy as np
from jax import lax
from jax.experimental import pallas as pl
from jax.experimental.pallas import tpu as pltpu

F32, BF16 = jnp.float32, jnp.bfloat16

D_MODEL = 1024
D_FF = 2816
EPS = 1e-6
N_MOD = 9
SWA_HEADS, SWA_KV_HEADS, SWA_HEAD_DIM, WINDOW = 8, 2, 64, 128
SWA_GROUP = SWA_HEADS // SWA_KV_HEADS
MLA_HEADS, MLA_Q_RANK, MLA_KV_RANK, MLA_NOPE, MLA_ROPE, MLA_V = 4, 256, 128, 128, 64, 128
MLA_QK = MLA_NOPE + MLA_ROPE
ROPE_THETA = 10000.0
NUM_BUCKETS, MAX_DISTANCE = 32, 128
D_IN = 1216
N_SWA_COLS = 768
N_MLA_COLS = 512

ADAM_LR, ADAM_B1, ADAM_B2, ADAM_EPS, ADAM_WD, ADAM_STEP = 0.001, 0.9, 0.999, 1e-08, 0.01, 10

N_CHIPS = 4
N_DEV = 8
F_SHARD = D_FF // N_CHIPS
HALF = F_SHARD // 2
R_IN, R_O, R_UQ, R_UKV = 304, 256, 48, 32
O_IN, O_O, O_UQ, O_UKV, O_PAD = 0, 304, 560, 608, 640

PK_MOD = 72
PK_ROWS = 112
VMEM_LIMIT = 56 << 20
ROW_TILE = 2048
ELEM_TILE = 512
MOD_TILE = 768

_DN = {
    "nn": (((1,), (0,)), ((), ())),
    "nt": (((1,), (1,)), ((), ())),
    "tn": (((0,), (0,)), ((), ())),
}


def _sds(shape, dtype):
    return jax.ShapeDtypeStruct(tuple(shape), dtype)


def _cparams(sem=None):
    kw = {"vmem_limit_bytes": VMEM_LIMIT}
    if sem is not None:
        kw["dimension_semantics"] = sem
    return pltpu.CompilerParams(**kw)


def _dot(a, b, dims):
    return lax.dot_general(a.astype(BF16), b.astype(BF16), _DN[dims], preferred_element_type=F32)


def _mm(name, dims, grid, a, a_blk, a_idx, b, b_blk, b_idx, out, out_blk, out_idx, acc_shape, alias=None, deps=()):
    nk = grid[2]

    def body(*refs):
        a_ref, b_ref = refs[:2]
        o_ref, acc_ref = refs[-2:]
        part = _dot(a_ref[...], b_ref[...], dims)
        if nk == 1:
            o_ref[...] = part.astype(o_ref.dtype)
            return
        k = pl.program_id(2)

        @pl.when(k == 0)
        def _():
            acc_ref[...] = part

        @pl.when((k > 0) & (k < nk - 1))
        def _():
            acc_ref[...] += part

        @pl.when(k == nk - 1)
        def _():
            o_ref[...] = (acc_ref[...] + part).astype(o_ref.dtype)

    in_specs = [pl.BlockSpec(a_blk, a_idx), pl.BlockSpec(b_blk, b_idx)]
    args = [a, b]
    kw = {}
    if alias is not None:
        in_specs.append(pl.BlockSpec(memory_space=pl.ANY))
        args.append(alias)
        kw["input_output_aliases"] = {2: 0}
    in_specs += [pl.BlockSpec(memory_space=pl.ANY)] * len(deps)
    args += list(deps)
    return pl.pallas_call(
        body,
        name=name,
        grid=grid,
        in_specs=in_specs,
        out_specs=pl.BlockSpec(out_blk, out_idx),
        out_shape=out,
        scratch_shapes=[pltpu.VMEM(acc_shape if nk > 1 else (8, 128), F32)],
        compiler_params=_cparams(("parallel", "parallel", "arbitrary")),
        **kw,
    )(*args)


def _rowwise(name, fn, tiled, full, out_tiled, out_red, tm, deps=()):
    rows = tiled[0].shape[-2]
    tm = min(tm, rows)
    grid = (rows // tm,)
    n_in = len(tiled) + len(full)
    n_t = len(out_tiled)
    n_dep = len(deps)

    def tspec(shape):
        nd = len(shape)
        return pl.BlockSpec(tuple(shape[:-2]) + (tm, shape[-1]), lambda i, nd=nd: (0,) * (nd - 2) + (i, 0))

    def fspec(shape):
        nd = len(shape)
        return pl.BlockSpec(tuple(shape), lambda i, nd=nd: (0,) * nd)

    def body(*refs):
        outs = fn(*[r[...] for r in refs[:n_in]])
        if not isinstance(outs, (tuple, list)):
            outs = (outs,)
        out_refs = refs[n_in + n_dep:]
        for r, v in zip(out_refs[:n_t], outs[:n_t]):
            r[...] = v.astype(r.dtype)
        red_refs = out_refs[n_t:]
        if red_refs:
            @pl.when(pl.program_id(0) == 0)
            def _():
                for r in red_refs:
                    r[...] = jnp.zeros_like(r)

            for r, v in zip(red_refs, outs[n_t:]):
                r[...] += v.astype(r.dtype)

    res = pl.pallas_call(
        body,
        name=name,
        grid=grid,
        in_specs=[tspec(a.shape) for a in tiled] + [fspec(a.shape) for a in full]
        + [pl.BlockSpec(memory_space=pl.ANY)] * n_dep,
        out_specs=[tspec(o.shape) for o in out_tiled] + [fspec(o.shape) for o in out_red],
        out_shape=list(out_tiled) + list(out_red),
        compiler_params=_cparams(("arbitrary",) if out_red else ("parallel",)),
    )(*tiled, *full, *deps)
    return res


def _sum0(v):
    return jnp.sum(v, axis=0, keepdims=True)


def _sigmoid(v):
    return 1.0 / (1.0 + jnp.exp(-v))


def _rms(x):
    r = lax.rsqrt(jnp.mean(x * x, axis=-1, keepdims=True) + EPS)
    return x * r, r


def _rms_bwd(u, xr, r):
    return r * (u - xr * jnp.mean(u * xr, axis=-1, keepdims=True))


def _normmod(x_, gamma_, sc_, sh_):
    return _rms(x_)[0] * gamma_ * (1.0 + sc_) + sh_


def _row_blocks(tm, size=512):
    size = min(size, tm)
    return [slice(r, r + size) for r in range(0, tm, size)]


def _loss_head(x_, t_, g_):
    xr, r = _rms(x_)
    err = xr * g_ - t_
    dy = err * (1.0 / D_MODEL)
    return _rms_bwd(dy * g_, xr, r), _sum0(err * err), _sum0(dy * xr)


def _ffn_fwd(tag, x, gamma, sh, sc, gate, gu_src, base, down_src, down_idx, mid_after, h_pre=None,
             next_norm=None, head=None, next_src=None):
    s_len = x.shape[0]
    if h_pre is None:
        (h,) = _rowwise(
            f"{tag}_normmod", _normmod, [x], [gamma, sc, sh], [_sds((s_len, D_MODEL), BF16)], [], ELEM_TILE
        )
        gdeps = gu_src.mid(h)
    else:
        h, gdeps = h_pre, gu_src.mid(mid_after)
    g4 = gu_src.get(h)

    tm = min(ROW_TILE, s_len)
    def gate_up(h_ref, wg_ref, wu_ref, *rest):
        ab_ref, s_ref = rest[-2:]
        wg, wu = wg_ref[...], wu_ref[...]
        for rows in _row_blocks(tm):
            hv = h_ref[rows, :]
            a = _dot(hv, wg, "nt")
            b = _dot(hv, wu, "nt")
            ab_ref[0, rows, :] = a.astype(ab_ref.dtype)
            ab_ref[1, rows, :] = b.astype(ab_ref.dtype)
            s_ref[rows, :] = (a * _sigmoid(a) * b).astype(s_ref.dtype)

    ab4, s3 = pl.pallas_call(
        gate_up,
        name=f"{tag}_gate_up",
        grid=(s_len // tm, N_CHIPS),
        in_specs=[
            pl.BlockSpec((tm, D_MODEL), lambda i, c: (i, 0)),
            pl.BlockSpec((None, None, F_SHARD, D_MODEL), lambda i, c: (c, base, 0, 0)),
            pl.BlockSpec((None, None, F_SHARD, D_MODEL), lambda i, c: (c, base + 1, 0, 0)),
        ] + [pl.BlockSpec(memory_space=pl.ANY)] * len(gdeps),
        out_specs=[
            pl.BlockSpec((None, 2, tm, F_SHARD), lambda i, c: (c, 0, i, 0)),
            pl.BlockSpec((None, tm, F_SHARD), lambda i, c: (c, i, 0)),
        ],
        out_shape=[_sds((N_CHIPS, 2, s_len, F_SHARD), BF16), _sds((N_CHIPS, s_len, F_SHARD), BF16)],
        compiler_params=_cparams(("parallel", "parallel")),
    )(h, g4, g4, *gdeps)
    if down_src is None:
        g_down, ddeps = g4, ()
    else:
        ddeps = down_src.mid(ab4)
        g_down = down_src.get(s3)

    y = _mm(
        f"{tag}_down", "nn", (s_len // tm, 1, N_CHIPS),
        s3, (None, tm, F_SHARD), lambda i, j, k: (k, i, 0),
        g_down, (None, None, F_SHARD, D_MODEL), lambda i, j, k: (k, down_idx, 0, 0),
        _sds((s_len, D_MODEL), F32), (tm, D_MODEL), lambda i, j, k: (i, 0),
        (tm, D_MODEL), deps=ddeps,
    )

    saved = (g4, base, g_down, down_idx, h, ab4, s3, y)
    act = _sds((s_len, D_MODEL), F32)
    if head is not None:
        target, norm_final = head
        vec = _sds((1, D_MODEL), F32)
        out = _rowwise(
            f"{tag}_residual_head", lambda x_, y_, t_, g_, nf_: _loss_head(x_ + 0.5 * g_ * y_, t_, nf_),
            [x, y, target], [gate, norm_final], [act], [vec, vec], ELEM_TILE,
        )
    elif next_norm is not None:
        def residual_norm(x_, y_, g_, gamma_, sc_, sh_):
            x_out = x_ + 0.5 * g_ * y_
            return x_out, _normmod(x_out, gamma_, sc_, sh_)

        out = _rowwise(
            f"{tag}_residual_norm", residual_norm, [x, y], [gate] + list(next_norm),
            [act, _sds((s_len, D_MODEL), BF16)], [], ELEM_TILE,
            deps=next_src.mid(y) if next_src is not None else (),
        )
    else:
        out = _rowwise(f"{tag}_residual", lambda x_, y_, g_: x_ + 0.5 * g_ * y_, [x, y], [gate], [act], [], ELEM_TILE)
    return out, saved


def _normmod_bwd_call(name, dh_parts, x, dxo, gamma, sc, deps=()):
    s_len = x.shape[0]
    n_parts = len(dh_parts)

    def fn(*vals):
        dh = vals[0]
        for extra in vals[1:n_parts]:
            dh = dh + extra
        x_, dxo_, gamma_, sc_ = vals[n_parts:]
        xr, r = _rms(x_)
        n = xr * gamma_
        dn = dh * (1.0 + sc_)
        dx = dxo_ + _rms_bwd(dn * gamma_, xr, r)
        return dx, _sum0(dh * n), _sum0(dh), _sum0(dn * xr)

    vec = _sds((1, D_MODEL), F32)
    return _rowwise(
        name, fn, list(dh_parts) + [x, dxo], [gamma, sc], [_sds((s_len, D_MODEL), F32)], [vec, vec, vec], ELEM_TILE, deps=deps
    )


class _BwdHooks:
    def __init__(self, after_gate=None, after_swiglu=None, after_wdown=None, after_wgu=None, after_dh=None):
        def nothing(_):
            return []

        self.after_gate = after_gate or nothing
        self.after_swiglu = after_swiglu or nothing
        self.after_wdown = after_wdown or nothing
        self.after_wgu = after_wgu or nothing
        self.after_dh = after_dh or nothing


def _ffn_bwd(tag, dxo, x, gamma, sc, gate, saved, hooks, deps=()):
    g4, base, g_down, down_idx, h, ab4, s3, y = saved
    s_len = x.shape[0]
    vec = _sds((1, D_MODEL), F32)

    dy, dgate = _rowwise(
        f"{tag}_bwd_gate", lambda dxo_, y_, g_: (0.5 * g_ * dxo_, _sum0(0.5 * y_ * dxo_)),
        [dxo, y], [gate], [_sds((s_len, D_MODEL), BF16)], [vec], ELEM_TILE, deps=deps,
    )

    tm = min(ROW_TILE, s_len)

    def d_gate_up(dy_ref, wd_ref, ab_ref, o_ref):
        wd = wd_ref[...]
        for rows in _row_blocks(tm):
            ds = _dot(dy_ref[rows, :], wd, "nt")
            a = ab_ref[0, rows, :].astype(F32)
            b = ab_ref[1, rows, :].astype(F32)
            sig = _sigmoid(a)
            o_ref[0, rows, :] = (ds * b * sig * (1.0 + a * (1.0 - sig))).astype(o_ref.dtype)
            o_ref[1, rows, :] = (ds * a * sig).astype(o_ref.dtype)

    ab_spec = pl.BlockSpec((None, 2, tm, F_SHARD), lambda i, c: (c, 0, i, 0))
    dab4 = pl.pallas_call(
        d_gate_up,
        name=f"{tag}_bwd_dgate_up",
        grid=(s_len // tm, N_CHIPS),
        in_specs=[
            pl.BlockSpec((tm, D_MODEL), lambda i, c: (i, 0)),
            pl.BlockSpec((None, None, F_SHARD, D_MODEL), lambda i, c: (c, down_idx, 0, 0)),
            ab_spec,
        ],
        out_specs=ab_spec,
        out_shape=_sds(ab4.shape, BF16),
        compiler_params=_cparams(("parallel", "parallel")),
    )(dy, g_down, ab4)

    tk = tm
    gb_down = _mm(
        f"{tag}_bwd_wdown", "tn", (N_CHIPS, 1, s_len // tk),
        s3, (None, tk, F_SHARD), lambda i, j, k: (i, k, 0),
        dy, (tk, D_MODEL), lambda i, j, k: (k, 0),
        _sds((N_CHIPS, 1, F_SHARD, D_MODEL), BF16), (None, None, F_SHARD, D_MODEL), lambda i, j, k: (i, 0, 0, 0),
        (F_SHARD, D_MODEL), deps=hooks.after_swiglu(dab4),
    )
    gb_gu = _mm(
        f"{tag}_bwd_wgu", "tn", (2 * N_CHIPS, 1, s_len // tk),
        dab4, (None, None, tk, F_SHARD), lambda i, j, k: (i % N_CHIPS, i // N_CHIPS, k, 0),
        h, (tk, D_MODEL), lambda i, j, k: (k, 0),
        _sds((N_CHIPS, 2, F_SHARD, D_MODEL), BF16), (None, None, F_SHARD, D_MODEL),
        lambda i, j, k: (i % N_CHIPS, i // N_CHIPS, 0, 0),
        (F_SHARD, D_MODEL), deps=hooks.after_wdown(gb_down),
    )
    assert base % 2 == 0
    dh_deps = hooks.after_wgu(gb_gu)

    def d_h(dab_ref, w_ref, *rest):
        o_ref, acc_ref = rest[-2:]
        c = pl.program_id(1)
        part = _dot(dab_ref[0], w_ref[0], "nn") + _dot(dab_ref[1], w_ref[1], "nn")

        @pl.when(c == 0)
        def _():
            acc_ref[...] = part

        @pl.when((c > 0) & (c < N_CHIPS - 1))
        def _():
            acc_ref[...] += part

        @pl.when(c == N_CHIPS - 1)
        def _():
            o_ref[...] = acc_ref[...] + part

    dh = pl.pallas_call(
        d_h,
        name=f"{tag}_bwd_dh",
        grid=(s_len // tm, N_CHIPS),
        in_specs=[
            pl.BlockSpec((None, 2, tm, F_SHARD), lambda i, c: (c, 0, i, 0)),
            pl.BlockSpec((None, 2, F_SHARD, D_MODEL), lambda i, c: (c, base // 2, 0, 0)),
        ] + [pl.BlockSpec(memory_space=pl.ANY)] * len(dh_deps),
        out_specs=pl.BlockSpec((tm, D_MODEL), lambda i, c: (i, 0)),
        out_shape=_sds((s_len, D_MODEL), F32),
        scratch_shapes=[pltpu.VMEM((tm, D_MODEL), F32)],
        compiler_params=_cparams(("parallel", "arbitrary")),
    )(dab4, g4, *dh_deps)
    dx, dsc, dsh, dgamma = _normmod_bwd_call(
        f"{tag}_bwd_normmod", [dh], x, dxo, gamma, sc, deps=hooks.after_dh(dh)
    )
    return dx, (gb_down, gb_gu), (dsh, dsc, dgate, dgamma)


def _bucket_map():
    qi = np.arange(WINDOW)[:, None]
    kj = np.arange(2 * WINDOW)[None, :]
    dist = qi + WINDOW - kj
    band = (dist >= 0) & (dist < WINDOW)
    max_exact = NUM_BUCKETS // 2
    n = np.maximum(dist, 0)
    large = []
    for dt in (np.float32, np.float64):
        nf = np.maximum(n, 1).astype(dt)
        val = np.log(nf / dt(max_exact)) / dt(math.log(MAX_DISTANCE / max_exact)) * dt(NUM_BUCKETS - max_exact)
        large.append(np.minimum(max_exact + val.astype(np.int32), NUM_BUCKETS - 1))
    assert np.array_equal(large[0], large[1])
    bucket = np.where(n < max_exact, n, large[0])
    return np.where(band, bucket, -1).astype(np.int32).reshape(1, -1)


def _bias_expand(rel_bias_t, bkt):
    n = bkt.shape[1]

    def body(rb_ref, bkt_ref, o_ref):
        onehot = (lax.broadcasted_iota(jnp.int32, (NUM_BUCKETS, n), 0) == bkt_ref[...]).astype(F32)
        o_ref[...] = lax.dot_general(
            rb_ref[...], onehot, _DN["nn"], precision=lax.Precision.HIGHEST, preferred_element_type=F32
        )

    return pl.pallas_call(
        body, name="bias_expand", out_shape=_sds((SWA_HEADS, n), F32), compiler_params=_cparams()
    )(rel_bias_t, bkt)


def _bias_reduce(dbias_flat, bkt):
    n = bkt.shape[1]

    def body(db_ref, bkt_ref, o_ref):
        onehot = (lax.broadcasted_iota(jnp.int32, (NUM_BUCKETS, n), 0) == bkt_ref[...]).astype(F32)
        o_ref[...] = lax.dot_general(
            db_ref[...], onehot, _DN["nt"], precision=lax.Precision.HIGHEST, preferred_element_type=F32
        )

    return pl.pallas_call(
        body, name="bias_reduce", out_shape=_sds((SWA_HEADS, NUM_BUCKETS), F32), compiler_params=_cparams()
    )(dbias_flat, bkt)


_SWA_SCALE = SWA_HEAD_DIM ** -0.5
_NEG = -1e30


_SWA_PAIR = 2


def _swa_in_specs():
    w = WINDOW
    n_q = SWA_HEADS * SWA_HEAD_DIM
    n_kv = 2 * SWA_KV_HEADS * SWA_HEAD_DIM
    prev = lambda m: jnp.maximum(_SWA_PAIR * m - 1, 0)
    return [
        pl.BlockSpec((_SWA_PAIR * w, n_q), lambda m: (m, 0)),
        pl.BlockSpec((w, n_kv), lambda m: (prev(m), n_q // n_kv)),
        pl.BlockSpec((_SWA_PAIR * w, n_kv), lambda m: (m, n_q // n_kv)),
        pl.BlockSpec((SWA_HEADS, w, 2 * w), lambda m: (0, 0, 0)),
        pl.BlockSpec((SWA_HEADS, w, 1), lambda m: (0, 0, 0)),
    ]


def _head_cols(v, first, count):
    hd = SWA_HEAD_DIM
    return jnp.stack([v[:, (first + i) * hd:(first + i + 1) * hd] for i in range(count)])


def _swa_blocks(q_ref, kvp_ref, kvc_ref, m):
    g, w, hd = SWA_GROUP, WINDOW, SWA_HEAD_DIM
    kv_all = jnp.concatenate([kvp_ref[...], kvc_ref[...]], axis=0).astype(F32)
    blocks = []
    for sub in range(_SWA_PAIR):
        rows = slice(sub * w, (sub + 1) * w)
        q = q_ref[rows, :].astype(F32)
        kv = kv_all[sub * w:(sub + 2) * w]
        heads = []
        for j in range(SWA_KV_HEADS):
            qj = _head_cols(q, g * j, g).reshape(g * w, hd)
            heads.append((qj, _head_cols(kv, j, 1)[0], _head_cols(kv, SWA_KV_HEADS + j, 1)[0]))
        blocks.append((_SWA_PAIR * m + sub, rows, heads))
    return blocks


def _swa_scores(q, kk, bias, n):
    g, w = SWA_GROUP, WINDOW
    s = (_dot(q, kk, "nt") * _SWA_SCALE).reshape(g, w, 2 * w) + bias
    qi = lax.broadcasted_iota(jnp.int32, (w, 2 * w), 0)
    kj = lax.broadcasted_iota(jnp.int32, (w, 2 * w), 1)
    dist = qi + w - kj
    valid = ((dist >= 0) & (dist < w) & ((n > 0) | (kj >= w)))[None]
    return jnp.where(valid, s, _NEG), valid


def _swa_fwd(swa2, bias, sinkb):
    s_len = swa2.shape[0]
    g, w, hd = SWA_GROUP, WINDOW, SWA_HEAD_DIM

    def body(q_ref, kvp_ref, kvc_ref, bias_ref, sink_ref, o_ref, lse_ref):
        for n, rows, heads in _swa_blocks(q_ref, kvp_ref, kvc_ref, pl.program_id(0)):
            outs = []
            for j, (q, kk, vv) in enumerate(heads):
                hs = slice(g * j, g * (j + 1))
                s, valid = _swa_scores(q, kk, bias_ref[hs], n)
                sink = sink_ref[hs]
                m = jnp.maximum(jnp.max(s, axis=-1, keepdims=True), sink)
                p = jnp.where(valid, jnp.exp(s - m), 0.0)
                l = jnp.sum(p, axis=-1, keepdims=True) + jnp.exp(sink - m)
                o = _dot(p.reshape(g * w, 2 * w), vv, "nn").reshape(g, w, hd) / l
                outs += [o[i] for i in range(g)]
                lse_ref[hs, rows, :] = m + jnp.log(l)
            o_ref[rows, :] = jnp.concatenate(outs, axis=-1).astype(o_ref.dtype)

    n_q = SWA_HEADS * hd
    return pl.pallas_call(
        body,
        name="swa_fwd",
        grid=(s_len // (_SWA_PAIR * w),),
        in_specs=_swa_in_specs(),
        out_specs=[
            pl.BlockSpec((_SWA_PAIR * w, n_q), lambda m: (m, 0)),
            pl.BlockSpec((SWA_HEADS, _SWA_PAIR * w, 1), lambda m: (0, m, 0)),
        ],
        out_shape=[_sds((s_len, n_q), BF16), _sds((SWA_HEADS, s_len, 1), F32)],
        compiler_params=_cparams(("parallel",)),
    )(swa2, swa2, swa2, bias, sinkb)


def _swa_bwd(swa2, bias, sinkb, cat, dcat, lse):
    s_len = swa2.shape[0]
    g, w, hd = SWA_GROUP, WINDOW, SWA_HEAD_DIM

    def body(q_ref, kvp_ref, kvc_ref, bias_ref, sink_ref, o_ref, do_ref, lse_ref,
             dq_ref, dkva_ref, dkvb_ref, dbias_ref, dsink_ref):
        step = pl.program_id(0)

        @pl.when(step == 0)
        def _():
            dbias_ref[...] = jnp.zeros_like(dbias_ref)
            dsink_ref[...] = jnp.zeros_like(dsink_ref)

        for n, rows, heads in _swa_blocks(q_ref, kvp_ref, kvc_ref, step):
            o_all = o_ref[rows, :].astype(F32)
            do_all = do_ref[rows, :].astype(F32)
            dqs, dks, dvs = [], [], []
            for j, (q, kk, vv) in enumerate(heads):
                hs = slice(g * j, g * (j + 1))
                s, valid = _swa_scores(q, kk, bias_ref[hs], n)
                lse_v = lse_ref[hs, rows, :]
                p = jnp.where(valid, jnp.exp(s - lse_v), 0.0)
                do = _head_cols(do_all, g * j, g)
                delta = jnp.sum(do * _head_cols(o_all, g * j, g), axis=-1, keepdims=True)
                do2 = do.reshape(g * w, hd)
                dp = _dot(do2, vv, "nt").reshape(g, w, 2 * w)
                ds = p * (dp - delta)
                dbias_ref[hs] += ds
                dsink_ref[hs] -= jnp.exp(sink_ref[hs] - lse_v) * delta
                ds2 = ds.reshape(g * w, 2 * w)
                dq = (_dot(ds2, kk, "nn") * _SWA_SCALE).reshape(g, w, hd)
                dqs += [dq[i] for i in range(g)]
                dks.append(_dot(ds2, q, "tn") * _SWA_SCALE)
                dvs.append(_dot(p.reshape(g * w, 2 * w), do2, "tn"))
            dq_ref[rows, :] = jnp.concatenate(dqs, axis=-1).astype(dq_ref.dtype)
            dkv = jnp.concatenate(dks + dvs, axis=-1)
            dkvb_ref[rows, :] = dkv[:w]
            dkva_ref[rows, :] = dkv[w:]

    n_q = SWA_HEADS * hd
    n_kv = 2 * SWA_KV_HEADS * hd
    q_spec = pl.BlockSpec((_SWA_PAIR * w, n_q), lambda m: (m, 0))
    kv_spec = pl.BlockSpec((_SWA_PAIR * w, n_kv), lambda m: (m, 0))
    return pl.pallas_call(
        body,
        name="swa_bwd",
        grid=(s_len // (_SWA_PAIR * w),),
        in_specs=_swa_in_specs() + [q_spec, q_spec, pl.BlockSpec((SWA_HEADS, _SWA_PAIR * w, 1), lambda m: (0, m, 0))],
        out_specs=[
            q_spec, kv_spec, kv_spec,
            pl.BlockSpec((SWA_HEADS, w, 2 * w), lambda n: (0, 0, 0)),
            pl.BlockSpec((SWA_HEADS, w, 1), lambda n: (0, 0, 0)),
        ],
        out_shape=[
            _sds((s_len, n_q), BF16), _sds((s_len, n_kv), F32), _sds((s_len, n_kv), F32),
            _sds((SWA_HEADS, w, 2 * w), F32), _sds((SWA_HEADS, w, 1), F32),
        ],
        compiler_params=_cparams(("arbitrary",)),
    )(swa2, swa2, swa2, bias, sinkb, cat, dcat, lse)


_MLA_SCALE = MLA_QK ** -0.5
_MLA_TQ = 256
_MLA_FWD_HEADS = 4
_MLA_BWD_HEADS = 2


def _mla_mask(i, tq, n_keys):
    row = i * tq + lax.broadcasted_iota(jnp.int32, (tq, n_keys), 0)
    col = lax.broadcasted_iota(jnp.int32, (tq, n_keys), 1)
    return col <= row


def _per_query_block(i, n_blocks, fn):
    for ii in range(n_blocks):
        pl.when(i == ii)(functools.partial(fn, ii))


def _mla_fwd(q4, k4, v4):
    s_len = q4.shape[1]
    tq = min(_MLA_TQ, s_len)
    pair = _MLA_FWD_HEADS

    def body(q_ref, k_ref, v_ref, o_ref, lse_ref):
        def block(ii):
            n_keys = (ii + 1) * tq
            mask = _mla_mask(ii, tq, n_keys)
            for hh in range(pair):
                s = jnp.where(mask, _dot(q_ref[hh], k_ref[hh, :n_keys, :], "nt") * _MLA_SCALE, _NEG)
                m = jnp.max(s, axis=-1, keepdims=True)
                p = jnp.exp(s - m)
                l = jnp.sum(p, axis=-1, keepdims=True)
                o_ref[:, hh * MLA_V:(hh + 1) * MLA_V] = (_dot(p, v_ref[hh, :n_keys, :], "nn") / l).astype(o_ref.dtype)
                lse_ref[hh] = m + jnp.log(l)

        _per_query_block(pl.program_id(1), s_len // tq, block)

    return pl.pallas_call(
        body,
        name="mla_fwd",
        grid=(MLA_HEADS // pair, s_len // tq),
        in_specs=[
            pl.BlockSpec((pair, tq, MLA_QK), lambda h, i: (h, i, 0)),
            pl.BlockSpec((pair, s_len, MLA_QK), lambda h, i: (h, 0, 0)),
            pl.BlockSpec((pair, s_len, MLA_V), lambda h, i: (h, 0, 0)),
        ],
        out_specs=[
            pl.BlockSpec((tq, pair * MLA_V), lambda h, i: (i, h)),
            pl.BlockSpec((pair, tq, 1), lambda h, i: (h, i, 0)),
        ],
        out_shape=[_sds((s_len, MLA_HEADS * MLA_V), BF16), _sds((MLA_HEADS, s_len, 1), F32)],
        compiler_params=_cparams(("parallel", "parallel")),
    )(q4, k4, v4)


def _mla_bwd(q4, k4, v4, cat, dcat, lse):
    s_len = q4.shape[1]
    tq = min(_MLA_TQ, s_len)
    pair = _MLA_BWD_HEADS
    first = SWA_HEADS * SWA_HEAD_DIM // (pair * MLA_V)

    def body(q_ref, k_ref, v_ref, o_ref, do_ref, lse_ref, dq_ref, dk_ref, dv_ref):
        i = pl.program_id(1)

        @pl.when(i == 0)
        def _():
            dk_ref[...] = jnp.zeros_like(dk_ref)
            dv_ref[...] = jnp.zeros_like(dv_ref)

        def block(ii):
            n_keys = (ii + 1) * tq
            mask = _mla_mask(ii, tq, n_keys)
            for hh in range(pair):
                cols = slice(hh * MLA_V, (hh + 1) * MLA_V)
                q, k, v, do = q_ref[hh], k_ref[hh, :n_keys, :], v_ref[hh, :n_keys, :], do_ref[:, cols]
                s = jnp.where(mask, _dot(q, k, "nt") * _MLA_SCALE, _NEG)
                p = jnp.where(mask, jnp.exp(s - lse_ref[hh]), 0.0)
                delta = jnp.sum(do.astype(F32) * o_ref[:, cols].astype(F32), axis=-1, keepdims=True)
                ds = (p * (_dot(do, v, "nt") - delta) * _MLA_SCALE).astype(BF16)
                dq_ref[hh] = _dot(ds, k, "nn")
                dk_ref[hh, :n_keys, :] += _dot(ds, q, "tn")
                dv_ref[hh, :n_keys, :] += _dot(p, do, "tn")

        _per_query_block(i, s_len // tq, block)

    return pl.pallas_call(
        body,
        name="mla_bwd",
        grid=(MLA_HEADS // pair, s_len // tq),
        in_specs=[
            pl.BlockSpec((pair, tq, MLA_QK), lambda h, i: (h, i, 0)),
            pl.BlockSpec((pair, s_len, MLA_QK), lambda h, i: (h, 0, 0)),
            pl.BlockSpec((pair, s_len, MLA_V), lambda h, i: (h, 0, 0)),
            pl.BlockSpec((tq, pair * MLA_V), lambda h, i: (i, first + h)),
            pl.BlockSpec((tq, pair * MLA_V), lambda h, i: (i, first + h)),
            pl.BlockSpec((pair, tq, 1), lambda h, i: (h, i, 0)),
        ],
        out_specs=[
            pl.BlockSpec((pair, tq, MLA_QK), lambda h, i: (h, i, 0)),
            pl.BlockSpec((pair, s_len, MLA_QK), lambda h, i: (h, 0, 0)),
            pl.BlockSpec((pair, s_len, MLA_V), lambda h, i: (h, 0, 0)),
        ],
        out_shape=[
            _sds((MLA_HEADS, s_len, MLA_QK), F32),
            _sds((MLA_HEADS, s_len, MLA_QK), F32),
            _sds((MLA_HEADS, s_len, MLA_V), F32),
        ],
        compiler_params=_cparams(("parallel", "arbitrary")),
    )(q4, k4, v4, cat, dcat, lse)


def _mla_split(pm):
    q_lat = pm[:, :MLA_Q_RANK]
    kv_lat = pm[:, MLA_Q_RANK:MLA_Q_RANK + MLA_KV_RANK]
    kr = pm[:, MLA_Q_RANK + MLA_KV_RANK:MLA_Q_RANK + MLA_KV_RANK + MLA_ROPE]
    kr_sw = pm[:, MLA_Q_RANK + MLA_KV_RANK + MLA_ROPE:]
    return q_lat, kv_lat, kr, kr_sw


def _mla_proj(pm, cos2, sin2, q_norm, kv_norm, wq_ext, wkv):
    s_len = pm.shape[0]

    def fn(pm_, cos_, sin_, qg, kvg, wq, wk):
        q_lat, kv_lat, kr, kr_sw = _mla_split(pm_)
        nq = (_rms(q_lat)[0] * qg).astype(BF16)
        nkv = (_rms(kv_lat)[0] * kvg).astype(BF16)
        k_rot = kr * cos_ + kr_sw * sin_
        qs, ks, vs = [], [], []
        for h in range(MLA_HEADS):
            qe = _dot(nq, wq[h], "nt")
            q_rot = qe[:, MLA_NOPE:MLA_QK] * cos_ + qe[:, MLA_QK:] * sin_
            qs.append(jnp.concatenate([qe[:, :MLA_NOPE], q_rot], axis=-1))
            kve = _dot(nkv, wk[h], "nt")
            ks.append(jnp.concatenate([kve[:, :MLA_NOPE], k_rot], axis=-1))
            vs.append(kve[:, MLA_NOPE:])
        return jnp.stack(qs), jnp.stack(ks), jnp.stack(vs)

    return _rowwise(
        "mla_proj", fn, [pm, cos2, sin2], [q_norm, kv_norm, wq_ext, wkv],
        [_sds((MLA_HEADS, s_len, MLA_QK), BF16), _sds((MLA_HEADS, s_len, MLA_QK), BF16),
         _sds((MLA_HEADS, s_len, MLA_V), BF16)], [], ELEM_TILE,
    )


def _mla_proj_bwd(pm, cos2, sin2, dq4, dk4, dv4, q_norm, kv_norm, wq_ext, wkv):
    s_len = pm.shape[0]

    def fn(pm_, cos_, sin_, dq, dk, dv, qg, kvg, wq, wk):
        q_lat, kv_lat, _, _ = _mla_split(pm_)
        xq, rq = _rms(q_lat)
        xkv, rkv = _rms(kv_lat)
        nq = (xq * qg).astype(BF16)
        nkv = (xkv * kvg).astype(BF16)
        dnq = jnp.zeros_like(q_lat)
        dnkv = jnp.zeros_like(kv_lat)
        dkr = jnp.zeros_like(cos_)
        dwq, dwk = [], []
        for h in range(MLA_HEADS):
            dy = dq[h][:, MLA_NOPE:]
            dqe = jnp.concatenate([dq[h][:, :MLA_NOPE], dy * cos_, dy * sin_], axis=-1).astype(BF16)
            dnq = dnq + _dot(dqe, wq[h], "nn")
            dwq.append(_dot(dqe, nq, "tn"))
            dkve = jnp.concatenate([dk[h][:, :MLA_NOPE], dv[h]], axis=-1).astype(BF16)
            dnkv = dnkv + _dot(dkve, wk[h], "nn")
            dwk.append(_dot(dkve, nkv, "tn"))
            dkr = dkr + dk[h][:, MLA_NOPE:]
        dq_lat = _rms_bwd(dnq * qg, xq, rq)
        dkv_lat = _rms_bwd(dnkv * kvg, xkv, rkv)
        dpm = jnp.concatenate([dq_lat, dkv_lat, dkr * cos_, dkr * sin_], axis=-1)
        return dpm, _sum0(dnq * xq), _sum0(dnkv * xkv), jnp.stack(dwq), jnp.stack(dwk)

    return _rowwise(
        "mla_proj_bwd", fn, [pm, cos2, sin2, dq4, dk4, dv4], [q_norm, kv_norm, wq_ext, wkv],
        [_sds((s_len, N_MLA_COLS), BF16)],
        [_sds((1, MLA_Q_RANK), F32), _sds((1, MLA_KV_RANK), F32),
         _sds(wq_ext.shape, F32), _sds(wkv.shape, F32)], ELEM_TILE,
    )


def _rope_tables(s_len):
    inv = ROPE_THETA ** (-jnp.arange(0, MLA_ROPE, 2, dtype=F32) / MLA_ROPE)
    ang = jnp.arange(s_len, dtype=F32)[:, None] * inv[None, :]
    cos, sin = jnp.cos(ang), jnp.sin(ang)
    return jnp.concatenate([cos, cos], axis=-1), jnp.concatenate([-sin, sin], axis=-1)


def _mix_weights(g_mix):
    rest = g_mix[:, 0]
    w_in_t = rest[:, O_IN:O_IN + R_IN].reshape(D_IN, D_MODEL)
    w_o = rest[:, O_O:O_O + R_O].reshape(D_MODEL, D_MODEL)
    w_uq_t = rest[:, O_UQ:O_UQ + R_UQ].reshape(MLA_HEADS, MLA_QK, MLA_Q_RANK)
    w_ukv_t = rest[:, O_UKV:O_UKV + R_UKV].reshape(MLA_HEADS, MLA_NOPE + MLA_V, MLA_KV_RANK)
    half = MLA_ROPE // 2
    w_swa = w_in_t[:N_SWA_COLS]
    w_mla = jnp.concatenate([w_in_t[N_SWA_COLS:], w_in_t[D_IN - half:], w_in_t[D_IN - MLA_ROPE:D_IN - half]], axis=0)
    wq_ext = jnp.concatenate([w_uq_t, w_uq_t[:, MLA_QK - half:], w_uq_t[:, MLA_NOPE:MLA_QK - half]], axis=1)
    return w_swa, w_mla, w_o, wq_ext, w_ukv_t


def _mix_fwd(x, h, gate, mix_src, q_norm, kv_norm, bias, sinkb, cos2, sin2, next_norm, next_src):
    s_len = x.shape[0]
    tm = min(ROW_TILE, s_len)
    deps = mix_src.mid(x)
    weights = _mix_weights(mix_src.get(h))
    w_swa, w_mla, w_o, wq_ext, wkv = weights
    swa2 = _mm(
        "mix_proj_swa", "nt", (s_len // tm, 1, 1),
        h, (tm, D_MODEL), lambda i, j, k: (i, 0),
        w_swa, (N_SWA_COLS, D_MODEL), lambda i, j, k: (0, 0),
        _sds((s_len, N_SWA_COLS), BF16), (tm, N_SWA_COLS), lambda i, j, k: (i, 0),
        (tm, N_SWA_COLS), deps=deps,
    )
    pm = _mm(
        "mix_proj_mla", "nt", (s_len // tm, 1, 1),
        h, (tm, D_MODEL), lambda i, j, k: (i, 0),
        w_mla, (N_MLA_COLS, D_MODEL), lambda i, j, k: (0, 0),
        _sds((s_len, N_MLA_COLS), F32), (tm, N_MLA_COLS), lambda i, j, k: (i, 0),
        (tm, N_MLA_COLS),
    )
    q4, k4, v4 = _mla_proj(pm, cos2, sin2, q_norm, kv_norm, wq_ext, wkv)
    oa, lse_a = _swa_fwd(swa2, bias, sinkb)
    ob, lse_b = _mla_fwd(q4, k4, v4)
    cat = jnp.concatenate([oa, ob], axis=1)
    z = _mm(
        "mix_out", "nn", (s_len // tm, 1, 1),
        cat, (tm, D_MODEL), lambda i, j, k: (i, 0),
        w_o, (D_MODEL, D_MODEL), lambda i, j, k: (0, 0),
        _sds((s_len, D_MODEL), F32), (tm, D_MODEL), lambda i, j, k: (i, 0),
        (tm, D_MODEL), deps=next_src.mid(cat),
    )

    def residual_norm(x_, z_, g_, gamma_, sc_, sh_):
        x_out = x_ + g_ * z_
        return x_out, _normmod(x_out, gamma_, sc_, sh_)

    out = _rowwise(
        "mix_residual_norm", residual_norm, [x, z], [gate] + list(next_norm),
        [_sds((s_len, D_MODEL), F32), _sds((s_len, D_MODEL), BF16)], [], ELEM_TILE,
    )
    return out, (weights, h, swa2, pm, q4, k4, v4, cat, lse_a, lse_b, z)


def _mix_bwd(dxo, x, gamma, sc, gate, q_norm, kv_norm, bias, sinkb, cos2, sin2, saved, hooks):
    weights, h, swa2, pm, q4, k4, v4, cat, lse_a, lse_b, z = saved
    w_swa, w_mla, w_o, wq_ext, wkv = weights
    s_len = x.shape[0]
    tm = tk = min(ROW_TILE, s_len)
    vec = _sds((1, D_MODEL), F32)
    n_proj = N_SWA_COLS + N_MLA_COLS
    half_d = D_MODEL // 2

    dz, dgate = _rowwise(
        "mix_bwd_gate", lambda dxo_, z_, g_: (g_ * dxo_, _sum0(z_ * dxo_)),
        [dxo, z], [gate], [_sds((s_len, D_MODEL), BF16)], [vec], ELEM_TILE,
    )
    dw_o = _mm(
        "mix_bwd_wo", "tn", (2, 1, s_len // tk),
        cat, (tk, half_d), lambda i, j, k: (k, i),
        dz, (tk, D_MODEL), lambda i, j, k: (k, 0),
        _sds((D_MODEL, D_MODEL), F32), (half_d, D_MODEL), lambda i, j, k: (i, 0),
        (half_d, D_MODEL),
    )
    dcat = _mm(
        "mix_bwd_dcat", "nt", (s_len // tm, 1, 1),
        dz, (tm, D_MODEL), lambda i, j, k: (i, 0),
        w_o, (D_MODEL, D_MODEL), lambda i, j, k: (0, 0),
        _sds((s_len, D_MODEL), BF16), (tm, D_MODEL), lambda i, j, k: (i, 0),
        (tm, D_MODEL), deps=hooks.after_gate(dz),
    )
    dq_a, dkva, dkvb, dbias, dsink = _swa_bwd(swa2, bias, sinkb, cat, dcat, lse_a)
    dq4, dk4, dv4 = _mla_bwd(q4, k4, v4, cat, dcat, lse_b)
    dpm, dq_norm, dkv_norm, dwq_ext, dwkv = _mla_proj_bwd(pm, cos2, sin2, dq4, dk4, dv4, q_norm, kv_norm, wq_ext, wkv)

    dkv = dkva + jnp.concatenate([dkvb[WINDOW:], jnp.zeros_like(dkvb[:WINDOW])], axis=0)
    dproj = jnp.concatenate([dq_a, dkv.astype(BF16), dpm], axis=1)
    w_proj = jnp.concatenate([w_swa, w_mla], axis=0)

    dw_proj = _mm(
        "mix_bwd_win", "tn", (n_proj // 256, 1, s_len // tk),
        dproj, (tk, 256), lambda i, j, k: (k, i),
        h, (tk, D_MODEL), lambda i, j, k: (k, 0),
        _sds((n_proj, D_MODEL), F32), (256, D_MODEL), lambda i, j, k: (i, 0),
        (256, D_MODEL),
    )
    dw_swa, dw_mla = dw_proj[:N_SWA_COLS], dw_proj[N_SWA_COLS:]
    dh = _mm(
        "mix_bwd_dh", "nn", (s_len // tm, 1, 1),
        dproj, (tm, n_proj), lambda i, j, k: (i, 0),
        w_proj, (n_proj, D_MODEL), lambda i, j, k: (0, 0),
        _sds((s_len, D_MODEL), F32), (tm, D_MODEL), lambda i, j, k: (i, 0),
        (tm, D_MODEL),
    )
    dx, dsc, dsh, dgamma = _normmod_bwd_call("mix_bwd_normmod", [dh], x, dxo, gamma, sc)

    half = MLA_ROPE // 2
    n_mla_in = D_IN - N_SWA_COLS
    dw_mla_base = dw_mla[:n_mla_in]
    dw_mla_base = dw_mla_base.at[n_mla_in - half:].add(dw_mla[n_mla_in:n_mla_in + half])
    dw_mla_base = dw_mla_base.at[n_mla_in - MLA_ROPE:n_mla_in - half].add(dw_mla[n_mla_in + half:])
    dw_in_t = jnp.concatenate([dw_swa, dw_mla_base], axis=0)
    dw_uq_t = dwq_ext[:, :MLA_QK]
    dw_uq_t = dw_uq_t.at[:, MLA_QK - half:].add(dwq_ext[:, MLA_QK:MLA_QK + half])
    dw_uq_t = dw_uq_t.at[:, MLA_NOPE:MLA_QK - half].add(dwq_ext[:, MLA_QK + half:])
    rest = jnp.concatenate(
        [
            dw_in_t.reshape(N_CHIPS, R_IN, D_MODEL),
            dw_o.reshape(N_CHIPS, R_O, D_MODEL),
            dw_uq_t.reshape(N_CHIPS, R_UQ, D_MODEL),
            dwkv.reshape(N_CHIPS, R_UKV, D_MODEL),
            jnp.zeros((N_CHIPS, F_SHARD - O_PAD, D_MODEL), F32),
        ],
        axis=1,
    ).astype(BF16)
    return dx, rest, (dsh, dsc, dgate, dgamma), dq_norm, dkv_norm, dbias, dsink


def _device_step(x, target, mod, gu1_src, down1_src, mix_src, ffn2_src, norms, q_norm, kv_norm, sinks, rel_bias,
                 hooks2=None, hooks_mix=None, mix_done=None, hooks1=None):
    s_len = x.shape[0]
    sh1, sc1, g1, sh2, sc2, g2, sh3, sc3, g3 = [mod[:, i * D_MODEL:(i + 1) * D_MODEL] for i in range(N_MOD)]
    n1, n2, n3, nf = norms
    bkt = jnp.asarray(_bucket_map())
    bias = _bias_expand(rel_bias.T, bkt).reshape(SWA_HEADS, WINDOW, 2 * WINDOW)
    sinkb = jnp.broadcast_to(sinks.reshape(SWA_HEADS, 1, 1), (SWA_HEADS, WINDOW, 1))
    cos2, sin2 = _rope_tables(s_len)

    (x1, h2), saved1 = _ffn_fwd(
        "ffn1", x, n1, sh1, sc1, g1, gu1_src, 0, down1_src, 0, sh1, next_norm=(n2, sc2, sh2), next_src=mix_src
    )
    (x2, h3), saved2 = _mix_fwd(
        x1, h2, g2, mix_src, q_norm, kv_norm, bias, sinkb, cos2, sin2, next_norm=(n3, sc3, sh3), next_src=ffn2_src
    )
    (dx3, sq, dnf), saved3 = _ffn_fwd(
        "ffn2", x2, n3, sh3, sc3, g3, ffn2_src, 0, None, 2, x2, h_pre=h3, head=(target, nf)
    )
    loss_part = (0.5 / D_MODEL) * jnp.sum(sq)

    dx2, gb2, (dsh3, dsc3, dg3, dn3) = _ffn_bwd("ffn2", dx3, x2, n3, sc3, g3, saved3, hooks2 or _BwdHooks())
    dx1, rest, (dsh2, dsc2, dg2, dn2), dq_norm, dkv_norm, dbias, dsink = _mix_bwd(
        dx2, x1, n2, sc2, g2, q_norm, kv_norm, bias, sinkb, cos2, sin2, saved2, hooks_mix or _BwdHooks()
    )
    rest = rest[:, None]
    deps1 = mix_done(dx1, rest) if mix_done is not None else []
    dx0, gb1, (dsh1, dsc1, dg1, dn1) = _ffn_bwd(
        "ffn1", dx1, x, n1, sc1, g1, saved1, hooks1 or _BwdHooks(), deps=deps1
    )

    dmod = jnp.concatenate([dsh1, dsc1, dg1, dsh2, dsc2, dg2, dsh3, dsc3, dg3], axis=-1)
    drel = _bias_reduce(dbias.reshape(SWA_HEADS, -1), bkt).T
    dsinks = jnp.sum(dsink, axis=(1, 2)).reshape(1, SWA_HEADS)
    small = (dn1, dn2, dn3, dnf, dq_norm, dkv_norm, dsinks, drel)
    return loss_part, dx0, (gb1, gb2, rest), dmod, small


_MESH = pl.DeviceIdType.MESH


def _place():
    return lax.axis_index("x"), lax.axis_index("y"), lax.axis_index("c")


def _other_chips(x, y):
    return [(1 - x, y), (x, 1 - y), (1 - x, 1 - y)]


def _allgather_small(name, blk):
    m_per, n = blk.shape

    def body(x_ref, out_ref, send_sems, recv_sems, local_sem):
        x, y, c = _place()
        me, sibling = (x, y, c), (x, y, 1 - c)
        chips = _other_chips(x, y)

        def rows(px, py, pc):
            return out_ref.at[pl.ds((4 * px + 2 * py + pc) * m_per, m_per), :]

        def copy(k, block, to, src=None):
            return pltpu.make_async_remote_copy(
                src_ref=rows(*block) if src is None else src, dst_ref=rows(*block),
                send_sem=send_sems.at[k], recv_sem=recv_sems.at[k], device_id=to, device_id_type=_MESH,
            )

        mine = pltpu.make_async_copy(x_ref, rows(*me), local_sem)
        mine.start()
        first = [copy(0, me, sibling, src=x_ref)]
        first += [copy(1 + j, me, (*chip, c), src=x_ref) for j, chip in enumerate(chips)]
        for cp in first:
            cp.start()
        passed = [copy(4 + j, (*chip, c), sibling) for j, chip in enumerate(chips)]
        for j, chip in enumerate(chips):
            copy(1 + j, (*chip, c), me).wait_recv()
            passed[j].start()
        copy(0, sibling, me).wait_recv()
        for j, chip in enumerate(chips):
            copy(4 + j, (*chip, 1 - c), me).wait_recv()
        for cp in first + passed:
            cp.wait_send()
        mine.wait()

    return pl.pallas_call(
        body,
        name=name,
        out_shape=_sds((N_DEV * m_per, n), blk.dtype),
        in_specs=[pl.BlockSpec(memory_space=pltpu.VMEM)],
        out_specs=pl.BlockSpec(memory_space=pltpu.VMEM),
        scratch_shapes=[pltpu.SemaphoreType.DMA((7,)), pltpu.SemaphoreType.DMA((7,)), pltpu.SemaphoreType.DMA],
    )(blk)


def _gather_sends(n, own_ref, g_ref, send_sem, recv_sem, x, y, c, chip):
    return [
        pltpu.make_async_remote_copy(
            src_ref=own_ref.at[p, pl.ds(c * HALF, HALF), :], dst_ref=g_ref.at[2 * x + y, p, pl.ds(c * HALF, HALF), :],
            send_sem=send_sem, recv_sem=recv_sem, device_id=(*chip, c), device_id_type=_MESH,
        )
        for p in range(n)
    ]


def _gather_forwards(n, g_ref, send_sem, recv_sem, chip, c, sibling):
    return [
        pltpu.make_async_remote_copy(
            src_ref=g_ref.at[2 * chip[0] + chip[1], p, pl.ds(c * HALF, HALF), :],
            dst_ref=g_ref.at[2 * chip[0] + chip[1], p, pl.ds(c * HALF, HALF), :],
            send_sem=send_sem, recv_sem=recv_sem, device_id=sibling, device_id_type=_MESH,
        )
        for p in range(n)
    ]


def _gather_all(own_ref, g_ref, send_sem, recv_sem, chip, half, c):
    return pltpu.make_async_remote_copy(
        src_ref=own_ref.at[:, pl.ds(c * HALF, HALF), :],
        dst_ref=g_ref.at[2 * chip[0] + chip[1], :, pl.ds(half * HALF, HALF), :],
        send_sem=send_sem, recv_sem=recv_sem, device_id=(*chip, c), device_id_type=_MESH,
    )


_HBM_SPEC = pl.BlockSpec(memory_space=pltpu.HBM)
_SEM_SPEC = pl.BlockSpec(memory_space=pltpu.SEMAPHORE)
_ANY_SPEC = pl.BlockSpec(memory_space=pl.ANY)
_VMEM_SPEC = pl.BlockSpec(memory_space=pltpu.VMEM)
_SPLIT_PARAMS = pltpu.CompilerParams(has_side_effects=pltpu.SideEffectType.DATAFLOW_SIDE_EFFECTING)
_TOKEN = jax.ShapeDtypeStruct((8, 128), F32)


def _in_hbm(a):
    return pltpu.with_memory_space_constraint(a, pltpu.HBM)


class _GatherBehind:
    def __init__(self, tag, own, then=()):
        self.tag, self.n, self.own, self.then = tag, own.shape[0], own, tuple(then)
        self.mid_deps = None

    def start(self, after):
        tag, own, n = self.tag, self.own, self.n
        x, y, _ = _place()
        g_init = lax.dynamic_update_slice(
            lax.empty((N_CHIPS,) + own.shape, own.dtype), own[None], (2 * x + y, 0, 0, 0)
        )

        def body(own_ref, g_ref, *rest):
            send_sems, recv_sems, _, _, token = rest[len(after):]
            x, y, c = _place()
            for j, chip in enumerate(_other_chips(x, y)):
                for cp in _gather_sends(n, own_ref, g_ref, send_sems.at[j], recv_sems.at[j], x, y, c, chip):
                    cp.start()
            token[...] = jnp.zeros_like(token)

        self.send1, self.recv1, self.own, self.g, self.token = pl.pallas_call(
            body,
            name=f"{tag}_start",
            out_shape=(
                pltpu.SemaphoreType.DMA((3,)), pltpu.SemaphoreType.DMA((3,)),
                pltpu.HBM(own.shape, own.dtype), pltpu.HBM(g_init.shape, g_init.dtype), _TOKEN,
            ),
            in_specs=(_HBM_SPEC, _HBM_SPEC) + (_ANY_SPEC,) * len(after),
            out_specs=(_SEM_SPEC, _SEM_SPEC, _HBM_SPEC, _HBM_SPEC, _VMEM_SPEC),
            input_output_aliases={0: 2, 1: 3},
            compiler_params=_SPLIT_PARAMS,
        )(_in_hbm(own), _in_hbm(g_init), *after)

    def mid(self, after):
        if self.mid_deps is not None:
            return self.mid_deps
        n = self.n

        def body(own_ref, g_ref, send1, recv1, after_ref, send2, recv2, g_out, token):
            x, y, c = _place()
            sibling = (x, y, 1 - c)
            chips = _other_chips(x, y)
            for j, chip in enumerate(chips):
                _gather_all(own_ref, g_ref, send1.at[j], recv1.at[j], chip, c, c).wait_recv()
                for cp in _gather_forwards(n, g_ref, send2.at[j], recv2.at[j], chip, c, sibling):
                    cp.start()
            for j, chip in enumerate(chips):
                _gather_all(own_ref, g_ref, send1.at[j], recv1.at[j], chip, c, c).wait_send()
            token[...] = jnp.zeros_like(token)

        self.send2, self.recv2, self.g, token = pl.pallas_call(
            body,
            name=f"{self.tag}_mid",
            out_shape=(
                pltpu.SemaphoreType.DMA((3,)), pltpu.SemaphoreType.DMA((3,)),
                pltpu.HBM(self.g.shape, self.g.dtype), _TOKEN,
            ),
            in_specs=(_HBM_SPEC, _HBM_SPEC, _SEM_SPEC, _SEM_SPEC, _ANY_SPEC),
            out_specs=(_SEM_SPEC, _SEM_SPEC, _HBM_SPEC, _VMEM_SPEC),
            input_output_aliases={1: 2},
            compiler_params=_SPLIT_PARAMS,
        )(self.own, self.g, self.send1, self.recv1, after)
        deps = [token]
        for nxt in self.then:
            nxt.start(deps[-1:])
            deps.append(nxt.token)
        self.mid_deps = deps
        return deps

    def get(self, after):
        def body(g_ref, send2, recv2, after_ref, g_out):
            x, y, c = _place()
            for j, chip in enumerate(_other_chips(x, y)):
                slot_in = g_ref.at[2 * chip[0] + chip[1], :, pl.ds((1 - c) * HALF, HALF), :]
                slot_out = g_ref.at[2 * chip[0] + chip[1], :, pl.ds(c * HALF, HALF), :]
                cp = pltpu.make_async_remote_copy(
                    src_ref=slot_out, dst_ref=slot_in, send_sem=send2.at[j], recv_sem=recv2.at[j],
                    device_id=(x, y, 1 - c), device_id_type=_MESH,
                )
                cp.wait_send()
                cp.wait_recv()

        return pl.pallas_call(
            body,
            name=f"{self.tag}_wait",
            out_shape=pltpu.HBM(self.g.shape, self.g.dtype),
            in_specs=(_HBM_SPEC, _SEM_SPEC, _SEM_SPEC, _ANY_SPEC),
            out_specs=_HBM_SPEC,
            input_output_aliases={0: 0},
            compiler_params=_SPLIT_PARAMS,
        )(self.g, self.send2, self.recv2, after)


def _pair_sum(name, gb, land):
    def body(c_ref, gb_ref, land_ref, o_ref):
        o_ref[...] = (gb_ref[...].astype(F32) + land_ref[...].astype(F32)).astype(o_ref.dtype)

    core = lax.axis_index("c").astype(jnp.int32).reshape(1)
    return pl.pallas_call(
        body,
        name=name,
        grid_spec=pltpu.PrefetchScalarGridSpec(
            num_scalar_prefetch=1,
            grid=(N_CHIPS, gb.shape[1]),
            in_specs=[
                pl.BlockSpec((None, None, HALF, D_MODEL), lambda j, p, c: (j, p, c[0], 0)),
                pl.BlockSpec((None, None, HALF, D_MODEL), lambda j, p, c: (j, p, 0, 0)),
            ],
            out_specs=pl.BlockSpec((None, None, HALF, D_MODEL), lambda j, p, c: (j, p, 0, 0)),
        ),
        out_shape=_sds(land.shape, BF16),
        compiler_params=_cparams(("parallel", "parallel")),
    )(core, gb, land)


def _chip_sum_share(name, land):
    n = land.shape[1]

    def body(l_ref, r_ref, buf, send_sems, recv_sems, local_sems):
        p = pl.program_id(0)
        x, y, c = _place()
        acc = l_ref[0].astype(F32)
        for j in range(1, N_CHIPS):
            acc = acc + l_ref[j].astype(F32)
        buf[p] = acc

        def copies(q):
            mine = r_ref.at[q, pl.ds(c * HALF, HALF), :]
            theirs = r_ref.at[q, pl.ds((1 - c) * HALF, HALF), :]
            local = pltpu.make_async_copy(buf.at[q], mine, local_sems.at[q])
            out = pltpu.make_async_remote_copy(
                src_ref=buf.at[q], dst_ref=mine, send_sem=send_sems.at[q], recv_sem=recv_sems.at[q],
                device_id=(x, y, 1 - c), device_id_type=_MESH,
            )
            arrive = pltpu.make_async_remote_copy(
                src_ref=buf.at[q], dst_ref=theirs, send_sem=send_sems.at[q], recv_sem=recv_sems.at[q],
                device_id=(x, y, 1 - c), device_id_type=_MESH,
            )
            return local, out, arrive

        local, out, _ = copies(p)
        local.start()
        out.start()

        @pl.when(p == n - 1)
        def _():
            for q in range(n):
                local, out, arrive = copies(q)
                arrive.wait_recv()
                out.wait_send()
                local.wait()

    return pl.pallas_call(
        body,
        name=name,
        grid=(n,),
        in_specs=[pl.BlockSpec((N_CHIPS, None, HALF, D_MODEL), lambda p: (0, p, 0, 0))],
        out_specs=pl.BlockSpec(memory_space=pl.ANY),
        out_shape=_sds((n, F_SHARD, D_MODEL), F32),
        scratch_shapes=[
            pltpu.VMEM((n, HALF, D_MODEL), F32),
            pltpu.SemaphoreType.DMA((n,)), pltpu.SemaphoreType.DMA((n,)), pltpu.SemaphoreType.DMA((n,)),
        ],
        compiler_params=_cparams(("arbitrary",)),
    )(land)


class _ReduceBehind:
    def __init__(self, tag, gb, after=()):
        self.tag = tag
        n = gb.shape[1]
        land = lax.empty((N_CHIPS, n, HALF, D_MODEL), gb.dtype)

        def body(gb_ref, land_ref, *rest):
            send_sem, recv_sem, _, _, token = rest[len(after):]
            self._pair_copy(gb_ref, land_ref, send_sem, recv_sem).start()
            token[...] = jnp.zeros_like(token)

        self.send, self.recv, self.gb, self.land, self.token = pl.pallas_call(
            body,
            name=f"grads_pair_start_{tag}",
            out_shape=(
                pltpu.SemaphoreType.DMA((1,)), pltpu.SemaphoreType.DMA((1,)),
                pltpu.HBM(gb.shape, gb.dtype), pltpu.HBM(land.shape, land.dtype), _TOKEN,
            ),
            in_specs=(_HBM_SPEC, _HBM_SPEC) + (_ANY_SPEC,) * len(after),
            out_specs=(_SEM_SPEC, _SEM_SPEC, _HBM_SPEC, _HBM_SPEC, _VMEM_SPEC),
            input_output_aliases={0: 2, 1: 3},
            compiler_params=_SPLIT_PARAMS,
        )(_in_hbm(gb), _in_hbm(land), *after)

    @staticmethod
    def _pair_copy(gb_ref, land_ref, send_sem, recv_sem):
        x, y, c = _place()
        return pltpu.make_async_remote_copy(
            src_ref=gb_ref.at[:, :, pl.ds((1 - c) * HALF, HALF), :], dst_ref=land_ref,
            send_sem=send_sem.at[0], recv_sem=recv_sem.at[0], device_id=(x, y, 1 - c), device_id_type=_MESH,
        )

    @staticmethod
    def _chip_copies(p_ref, land_ref, send_sems, recv_sems):
        x, y, c = _place()
        me = 2 * x + y
        sends, arrivals = [], []
        for k, chip in enumerate(_other_chips(x, y)):
            them = 2 * chip[0] + chip[1]
            sends.append(pltpu.make_async_remote_copy(
                src_ref=p_ref.at[them], dst_ref=land_ref.at[me], send_sem=send_sems.at[k], recv_sem=recv_sems.at[k],
                device_id=(*chip, c), device_id_type=_MESH,
            ))
            arrivals.append(pltpu.make_async_remote_copy(
                src_ref=p_ref.at[me], dst_ref=land_ref.at[them], send_sem=send_sems.at[k], recv_sem=recv_sems.at[k],
                device_id=(*chip, c), device_id_type=_MESH,
            ))
        return sends, arrivals

    def mid(self, after):
        tag = self.tag

        def wait_body(gb_ref, land_ref, send_sem, recv_sem, after_ref, gb_out, land_out):
            cp = self._pair_copy(gb_ref, land_ref, send_sem, recv_sem)
            cp.wait_send()
            cp.wait_recv()

        gb, land = pl.pallas_call(
            wait_body,
            name=f"grads_pair_wait_{tag}",
            out_shape=(pltpu.HBM(self.gb.shape, self.gb.dtype), pltpu.HBM(self.land.shape, self.land.dtype)),
            in_specs=(_HBM_SPEC, _HBM_SPEC, _SEM_SPEC, _SEM_SPEC, _ANY_SPEC),
            out_specs=(_HBM_SPEC, _HBM_SPEC),
            input_output_aliases={0: 0, 1: 1},
            compiler_params=_SPLIT_PARAMS,
        )(self.gb, self.land, self.send, self.recv, after)
        part = _pair_sum(f"grads_pair_sum_{tag}", gb, land)

        x, y, _ = _place()
        start = (2 * x + y, 0, 0, 0)
        own = lax.dynamic_slice(part, start, (1,) + part.shape[1:])
        land2 = lax.dynamic_update_slice(lax.empty(part.shape, part.dtype), own, start)

        def start_body(p_ref, land_ref, send_sems, recv_sems, p_out, land_out, token):
            for cp in self._chip_copies(p_ref, land_ref, send_sems, recv_sems)[0]:
                cp.start()
            token[...] = jnp.zeros_like(token)

        self.send2, self.recv2, self.part, self.land2, token = pl.pallas_call(
            start_body,
            name=f"grads_chip_start_{tag}",
            out_shape=(
                pltpu.SemaphoreType.DMA((3,)), pltpu.SemaphoreType.DMA((3,)),
                pltpu.HBM(part.shape, part.dtype), pltpu.HBM(land2.shape, land2.dtype), _TOKEN,
            ),
            in_specs=(_HBM_SPEC, _HBM_SPEC),
            out_specs=(_SEM_SPEC, _SEM_SPEC, _HBM_SPEC, _HBM_SPEC, _VMEM_SPEC),
            input_output_aliases={0: 2, 1: 3},
            compiler_params=_SPLIT_PARAMS,
        )(_in_hbm(part), _in_hbm(land2))
        return [token]

    def get(self, after):
        n_after = len(after)

        def wait_body(p_ref, land_ref, send_sems, recv_sems, *rest):
            sends, arrivals = self._chip_copies(p_ref, land_ref, send_sems, recv_sems)
            for cp in arrivals:
                cp.wait_recv()
            for cp in sends:
                cp.wait_send()

        _, land2 = pl.pallas_call(
            wait_body,
            name=f"grads_chip_wait_{self.tag}",
            out_shape=(pltpu.HBM(self.part.shape, self.part.dtype), pltpu.HBM(self.land2.shape, self.land2.dtype)),
            in_specs=(_HBM_SPEC, _HBM_SPEC, _SEM_SPEC, _SEM_SPEC) + (_ANY_SPEC,) * n_after,
            out_specs=(_HBM_SPEC, _HBM_SPEC),
            input_output_aliases={0: 0, 1: 1},
            compiler_params=_SPLIT_PARAMS,
        )(self.part, self.land2, self.send2, self.recv2, *after)
        return _chip_sum_share(f"grads_chip_sum_share_{self.tag}", land2)


def _allreduce_small(blk):
    gathered = _allgather_small("allgather_small_grads", blk).reshape(N_DEV, PK_ROWS, 128)

    def body(g_ref, o_ref):
        acc = g_ref[0]
        for k in range(1, N_DEV):
            acc = acc + g_ref[k]
        o_ref[...] = acc

    total = pl.pallas_call(body, name="small_grads_sum", out_shape=_sds((PK_ROWS, 128), F32))(gathered)
    return gathered, total


def _adamw_math(w_, g_, m_, v_):
    m_new = ADAM_B1 * m_ + (1.0 - ADAM_B1) * g_
    v_new = ADAM_B2 * v_ + (1.0 - ADAM_B2) * (g_ * g_)
    m_hat = m_new / (1.0 - ADAM_B1 ** ADAM_STEP)
    v_hat = v_new / (1.0 - ADAM_B2 ** ADAM_STEP)
    delta = -ADAM_LR * (m_hat / (jnp.sqrt(v_hat) + ADAM_EPS) + ADAM_WD * w_)
    return delta, m_new, v_new


def _adamw(name, w, g, m, v):
    rows, cols = w.shape
    tm = rows
    while (tm * cols * 4 > (1 << 21) or tm == rows) and tm % 16 == 0:
        tm //= 2
    out = _sds(w.shape, F32)
    return _rowwise(name, _adamw_math, [w, g, m, v], [], [out, out, out], [], tm)


def _adamw_small(ws, gs, ms, vs):
    n = len(ws)
    shapes = [w.shape for w in ws]
    as2d = lambda a: a.reshape(1, -1) if a.ndim == 1 else a

    def body(*refs):
        ins, outs = refs[:4 * n], refs[4 * n:]
        for k in range(n):
            res = _adamw_math(*[ins[j * n + k][...] for j in range(4)])
            for j in range(3):
                outs[j * n + k][...] = res[j]

    args = [as2d(a) for group in (ws, gs, ms, vs) for a in group]
    out = pl.pallas_call(
        body, name="adamw_small", out_shape=[_sds(as2d(w).shape, F32) for w in ws] * 3, compiler_params=_cparams()
    )(*args)
    back = [o.reshape(shapes[i % n]) for i, o in enumerate(out)]
    return back[:n], back[n:2 * n], back[2 * n:]


def _pack_small(b_mod_like, n1, n2, n3, nf, qn, kvn, sinks, rel):
    parts = [
        b_mod_like.reshape(PK_MOD, 128),
        n1.reshape(8, 128), n2.reshape(8, 128), n3.reshape(8, 128), nf.reshape(8, 128),
        qn.reshape(2, 128), kvn.reshape(1, 128),
        jnp.pad(sinks.reshape(1, SWA_HEADS), ((0, 0), (0, 128 - SWA_HEADS))),
        rel.reshape(2, 128),
        jnp.zeros((2, 128), F32),
    ]
    return jnp.concatenate(parts, axis=0)


def _unpack_small(p):
    o = PK_MOD
    return (
        p[:o].reshape(1, N_MOD * D_MODEL),
        p[o:o + 8].reshape(1, D_MODEL), p[o + 8:o + 16].reshape(1, D_MODEL),
        p[o + 16:o + 24].reshape(1, D_MODEL), p[o + 24:o + 32].reshape(D_MODEL),
        p[o + 32:o + 34].reshape(1, MLA_Q_RANK), p[o + 34:o + 35].reshape(1, MLA_KV_RANK),
        p[o + 35:o + 36, :SWA_HEADS].reshape(1, SWA_HEADS),
        p[o + 36:o + 38].reshape(NUM_BUCKETS, SWA_HEADS),
    )


def kernel(x, c, w_mod, b_mod, norm_ffn1, ffn1_gate, ffn1_up, ffn1_down, norm_mix, w_in, q_norm, kv_norm, w_uq, w_ukv, sinks, w_o, norm_ffn2, ffn2_gate, ffn2_up, ffn2_down, rel_bias, norm_final, loss_target, m_w_mod, m_b_mod, m_norm_ffn1, m_ffn1_gate, m_ffn1_up, m_ffn1_down, m_norm_mix, m_w_in, m_q_norm, m_kv_norm, m_w_uq, m_w_ukv, m_sinks, m_w_o, m_norm_ffn2, m_ffn2_gate, m_ffn2_up, m_ffn2_down, m_rel_bias, m_norm_final, v_w_mod, v_b_mod, v_norm_ffn1, v_ffn1_gate, v_ffn1_up, v_ffn1_down, v_norm_mix, v_w_in, v_q_norm, v_kv_norm, v_w_uq, v_w_ukv, v_sinks, v_w_o, v_norm_ffn2, v_ffn2_gate, v_ffn2_up, v_ffn2_down, v_rel_bias, v_norm_final):
    ax, ay, ac = _place()
    chip = 2 * ax + ay
    dev = 2 * chip + ac
    n_mod_shard = N_MOD * D_MODEL // N_CHIPS

    def t16(w):
        return w[0].T.astype(BF16)

    rest = jnp.concatenate(
        [
            t16(w_in),
            w_o[0].astype(BF16),
            t16(w_uq).reshape(R_UQ, D_MODEL),
            t16(w_ukv).reshape(R_UKV, D_MODEL),
            jnp.zeros((F_SHARD - O_PAD, D_MODEL), BF16),
        ],
        axis=0,
    )
    own_gu1 = jnp.stack([t16(ffn1_gate), t16(ffn1_up)])
    own_down1 = ffn1_down[0].astype(BF16)[None]
    own_mix = rest[None]
    own_ffn2 = jnp.stack([t16(ffn2_gate), t16(ffn2_up), ffn2_down[0].astype(BF16)])

    (c_act,) = _rowwise(
        "silu_c", lambda v: v * _sigmoid(v), [c.reshape(8, 128)], [], [_sds((8, 128), F32)], [], 8
    )
    c_all = _allgather_small("allgather_c", c_act).reshape(N_DEV, D_MODEL)
    mod_cols = _mm(
        "mod_fwd", "nn", (1, n_mod_shard // MOD_TILE, 1),
        c_all, (N_DEV, D_MODEL), lambda i, j, k: (0, 0),
        w_mod[0], (D_MODEL, MOD_TILE), lambda i, j, k: (0, j),
        _sds((N_DEV, n_mod_shard), F32), (N_DEV, MOD_TILE), lambda i, j, k: (0, j),
        (N_DEV, MOD_TILE),
    )
    mod_all = _allgather_small("allgather_mod", mod_cols).reshape(N_CHIPS, 2, N_DEV, n_mod_shard)[:, 0]
    mod_all = mod_all.transpose(1, 0, 2).reshape(N_DEV, N_MOD * D_MODEL)
    mod = lax.dynamic_slice_in_dim(mod_all, dev, 1, axis=0) + b_mod

    norms = (norm_ffn1, norm_mix, norm_ffn2, norm_final.reshape(1, D_MODEL))

    ffn2_src = _GatherBehind("gather_ffn2", own_ffn2)
    mix_src = _GatherBehind("gather_mix", own_mix)
    down1_src = _GatherBehind("gather_down1", own_down1, then=[mix_src, ffn2_src])
    gu1_src = _GatherBehind("gather_gu1", own_gu1, then=[down1_src])
    gu1_src.start([mod_all])

    reducing, red = {}, {}

    def begin(tag, gb, after):
        reducing[tag] = _ReduceBehind(tag, gb, after=after)
        return [reducing[tag].token]

    def ffn2_gu_done(gb):
        return begin("ffn2_gu", gb, reducing["ffn2_down"].mid(gb))

    def mix_done(dx1, gb_rest):
        red["ffn2_down"] = reducing["ffn2_down"].get([dx1])
        red["ffn2_gu"] = reducing["ffn2_gu"].get([red["ffn2_down"]])
        return begin("rest", gb_rest, [red["ffn2_gu"]])

    def ffn1_gu_done(gb):
        red["rest"] = reducing["rest"].get([gb])
        begin("ffn1_gu", gb, reducing["ffn1_down"].mid(red["rest"]))
        return reducing["ffn1_gu"].mid(reducing["ffn1_gu"].token)

    loss_part, grad_x, _, dmod, small = _device_step(
        x[0], loss_target[0], mod, gu1_src, down1_src, mix_src, ffn2_src, norms, q_norm, kv_norm, sinks, rel_bias,
        hooks2=_BwdHooks(after_wdown=lambda gb: begin("ffn2_down", gb, ()), after_wgu=ffn2_gu_done),
        hooks_mix=_BwdHooks(after_gate=lambda dz: reducing["ffn2_gu"].mid(dz)),
        mix_done=mix_done,
        hooks1=_BwdHooks(
            after_swiglu=lambda dab3: reducing["rest"].mid(dab3),
            after_wdown=lambda gb: begin("ffn1_down", gb, ()),
            after_wgu=ffn1_gu_done,
        ),
    )
    loss = lax.psum(loss_part, ("x", "y", "c"))

    gathered, total = _allreduce_small(_pack_small(dmod, *small))
    (g_b_mod, g_n1, g_n2, g_n3, g_nf, g_qn, g_kvn, g_sinks, g_rel) = _unpack_small(total)
    dmod_all = gathered[:, :PK_MOD].reshape(N_DEV, N_MOD * D_MODEL)
    dmod_cols = lax.dynamic_slice_in_dim(dmod_all, chip * n_mod_shard, n_mod_shard, axis=1)
    g_w_mod = _mm(
        "mod_bwd", "tn", (1, n_mod_shard // MOD_TILE, 1),
        c_all, (N_DEV, D_MODEL), lambda i, j, k: (0, 0),
        dmod_cols, (N_DEV, MOD_TILE), lambda i, j, k: (0, j),
        _sds((D_MODEL, n_mod_shard), F32), (D_MODEL, MOD_TILE), lambda i, j, k: (0, j),
        (D_MODEL, MOD_TILE),
    )

    r_rest = red["rest"][0]
    transposed = {"ffn1_gate", "ffn1_up", "ffn2_gate", "ffn2_up", "w_in", "w_uq", "w_ukv"}
    g_big = {
        "ffn2_gate": red["ffn2_gu"][0], "ffn2_up": red["ffn2_gu"][1], "ffn2_down": red["ffn2_down"][0],
        "w_in": r_rest[O_IN:O_IN + R_IN],
        "w_o": r_rest[O_O:O_O + R_O],
        "w_uq": r_rest[O_UQ:O_UQ + R_UQ].reshape(MLA_QK, MLA_Q_RANK),
        "w_ukv": r_rest[O_UKV:O_UKV + R_UKV].reshape(MLA_NOPE + MLA_V, MLA_KV_RANK),
        "w_mod": g_w_mod,
    }
    w_big = {
        "ffn2_gate": (ffn2_gate, m_ffn2_gate, v_ffn2_gate),
        "ffn2_up": (ffn2_up, m_ffn2_up, v_ffn2_up), "ffn2_down": (ffn2_down, m_ffn2_down, v_ffn2_down),
        "w_in": (w_in, m_w_in, v_w_in), "w_o": (w_o, m_w_o, v_w_o), "w_uq": (w_uq, m_w_uq, v_w_uq),
        "w_ukv": (w_ukv, m_w_ukv, v_w_ukv), "w_mod": (w_mod, m_w_mod, v_w_mod),
    }
    grads, deltas, new_m, new_v = {}, {}, {}, {}

    def update(names):
        for nm in names:
            w, m, v = w_big[nm]
            if nm in transposed:
                outs = _adamw(f"adamw_{nm}", w[0].T, g_big[nm], m[0].T, v[0].T)
                g_, d_, m_, v_ = [a.T for a in (g_big[nm],) + tuple(outs)]
            else:
                g_, (d_, m_, v_) = g_big[nm], _adamw(f"adamw_{nm}", w[0], g_big[nm], m[0], v[0])
            grads[nm], deltas[nm], new_m[nm], new_v[nm] = g_[None], d_[None], m_[None], v_[None]

    update(list(w_big))
    red1_down = reducing["ffn1_down"].get([deltas[nm] for nm in w_big])
    red1_gu = reducing["ffn1_gu"].get([red1_down])
    g_big.update({"ffn1_gate": red1_gu[0], "ffn1_up": red1_gu[1], "ffn1_down": red1_down[0]})
    w_big.update({
        "ffn1_gate": (ffn1_gate, m_ffn1_gate, v_ffn1_gate), "ffn1_up": (ffn1_up, m_ffn1_up, v_ffn1_up),
        "ffn1_down": (ffn1_down, m_ffn1_down, v_ffn1_down),
    })
    update(["ffn1_gate", "ffn1_up", "ffn1_down"])

    small_names = ["b_mod", "norm_ffn1", "norm_mix", "norm_ffn2", "norm_final", "q_norm", "kv_norm", "sinks", "rel_bias"]
    w_small = (b_mod, norm_ffn1, norm_mix, norm_ffn2, norm_final, q_norm, kv_norm, sinks, rel_bias)
    m_small = (m_b_mod, m_norm_ffn1, m_norm_mix, m_norm_ffn2, m_norm_final, m_q_norm, m_kv_norm, m_sinks, m_rel_bias)
    v_small = (v_b_mod, v_norm_ffn1, v_norm_mix, v_norm_ffn2, v_norm_final, v_q_norm, v_kv_norm, v_sinks, v_rel_bias)
    g_small = (g_b_mod, g_n1, g_n2, g_n3, g_nf, g_qn, g_kvn, g_sinks, g_rel)
    d_s, m_s, v_s = _adamw_small(w_small, g_small, m_small, v_small)
    for nm, g_, d_, m_, v_ in zip(small_names, g_small, d_s, m_s, v_s):
        grads[nm], deltas[nm], new_m[nm], new_v[nm] = g_, d_, m_, v_

    order = ["w_mod", "b_mod", "norm_ffn1", "ffn1_gate", "ffn1_up", "ffn1_down", "norm_mix", "w_in", "q_norm", "kv_norm",
             "w_uq", "w_ukv", "sinks", "w_o", "norm_ffn2", "ffn2_gate", "ffn2_up", "ffn2_down", "rel_bias", "norm_final"]
    return (loss, grad_x[None], *[grads[n] for n in order], *[deltas[n] for n in order],
            *[new_m[n] for n in order], *[new_v[n] for n in order])
```
